```python
import math
import jax, jax.numpy as jnp
from jax import lax
import numpy as np

D_MODEL = 1024
BATCH = 8
SEQ = 4096
DEPTH = 1

N_Q_HEADS = 8
N_KV_HEADS = 2
HEAD_DIM = 64
Q_GROUP = N_Q_HEADS // N_KV_HEADS
ATTN_WIDTH = N_Q_HEADS * HEAD_DIM
KV_WIDTH = N_KV_HEADS * HEAD_DIM
WINDOW = 128
BLOCK = 128
N_BUCKETS = 32
MAX_DISTANCE = 128
NEG_INF = -1e30
SSM_WIDTH = D_MODEL // 2
SSM_GROUP = 16
SSM_GROUPS = SSM_WIDTH // SSM_GROUP
SSM_STATE = 64
DT_MIN = 1e-3
DT_MAX = 1e-1
N_BRANCHES = 2
D_FF = 4 * D_MODEL
IN_WIDTH = ATTN_WIDTH + 2 * KV_WIDTH + SSM_WIDTH + N_BRANCHES * D_MODEL
SPLITS = (ATTN_WIDTH, ATTN_WIDTH + KV_WIDTH, ATTN_WIDTH + 2 * KV_WIDTH,
          ATTN_WIDTH + 2 * KV_WIDTH + SSM_WIDTH, ATTN_WIDTH + 2 * KV_WIDTH + SSM_WIDTH + D_MODEL)
RMS_EPS = 1e-6

kernel_name = "hybrid_swa_sink_s5_gated_block"


def rmsnorm(x, g):
    xf = x.astype(jnp.float32)
    y = xf * lax.rsqrt(jnp.mean(xf * xf, axis=-1, keepdims=True) + RMS_EPS)
    return (y * g.astype(jnp.float32)).astype(x.dtype)


def t5_causal_bucket(dist):
    max_exact = N_BUCKETS // 2
    d = jnp.maximum(dist, 0)
    df = jnp.maximum(d, 1).astype(jnp.float32)
    large = max_exact + (jnp.log(df / max_exact) / math.log(MAX_DISTANCE / max_exact)
                         * (N_BUCKETS - max_exact)).astype(jnp.int32)
    large = jnp.minimum(large, N_BUCKETS - 1)
    return jnp.where(d < max_exact, d, large)


def sliding_window_attention(q, k, v, sinks, rel_bias):
    B, L = q.shape[0], q.shape[1]
    nb = L // BLOCK
    qb = q.reshape(B, nb, BLOCK, N_KV_HEADS, Q_GROUP, HEAD_DIM)

    def band(t):
        tb = t.reshape(B, nb, BLOCK, N_KV_HEADS, HEAD_DIM)
        prev = jnp.pad(tb, ((0, 0), (1, 0), (0, 0), (0, 0), (0, 0)))[:, :-1]
        return jnp.concatenate([prev, tb], axis=2)

    kb, vb = band(k), band(v)
    logits = jnp.einsum('bnqkgd,bnskd->bnkgqs', qb, kb,
                        preferred_element_type=jnp.float32) * (HEAD_DIM ** -0.5)
    qi = jnp.arange(BLOCK)[:, None]
    kj = jnp.arange(2 * BLOCK)[None, :]
    dist = qi + BLOCK - kj
    band_ok = (dist >= 0) & (dist < WINDOW)
    key_pos = jnp.arange(nb)[:, None] * BLOCK - BLOCK + jnp.arange(2 * BLOCK)[None, :]
    mask = band_ok[None] & (key_pos >= 0)[:, None, :]
    bias = rel_bias.astype(jnp.float32)[t5_causal_bucket(dist)]
    bias = jnp.transpose(bias, (2, 0, 1)).reshape(N_KV_HEADS, Q_GROUP, BLOCK, 2 * BLOCK)
    logits = jnp.where(mask[None, :, None, None], logits + bias[None, None], NEG_INF)
    s = sinks.astype(jnp.float32).reshape(1, 1, N_KV_HEADS, Q_GROUP, 1, 1)
    m = jnp.maximum(jnp.max(logits, axis=-1, keepdims=True), s)
    p = jnp.exp(logits - m)
    denom = jnp.sum(p, axis=-1, keepdims=True) + jnp.exp(s - m)
    probs = (p / denom).astype(v.dtype)
    out = jnp.einsum('bnkgqs,bnskd->bnqkgd', probs, vb)
    return out.reshape(B, L, ATTN_WIDTH)


def s5_ssm(u, lam_re, lam_im, log_dt, b_re, b_im, c_re, c_im, d_skip):
    B, L = u.shape[0], u.shape[1]
    uf = u.astype(jnp.float32).reshape(B, L, SSM_GROUPS, SSM_GROUP)
    dt = jnp.exp(log_dt.astype(jnp.float32))[:, None]
    lr = lam_re.astype(jnp.float32)
    li = lam_im.astype(jnp.float32)
    mag = jnp.exp(lr * dt)
    ab_re = mag * jnp.cos(li * dt)
    ab_im = mag * jnp.sin(li * dt)
    nr = ab_re - 1.0
    den = lr * lr + li * li
    f_re = (nr * lr + ab_im * li) / den
    f_im = (ab_im * lr - nr * li) / den
    br = b_re.astype(jnp.float32)
    bi = b_im.astype(jnp.float32)
    bb_re = f_re[..., None] * br - f_im[..., None] * bi
    bb_im = f_re[..., None] * bi + f_im[..., None] * br
    bu_re = jnp.einsum('blgc,gpc->blgp', uf, bb_re)
    bu_im = jnp.einsum('blgc,gpc->blgp', uf, bb_im)
    a_re = jnp.broadcast_to(ab_re, bu_re.shape)
    a_im = jnp.broadcast_to(ab_im, bu_im.shape)

    def combine(e1, e2):
        a1r, a1i, b1r, b1i = e1
        a2r, a2i, b2r, b2i = e2
        return (a2r * a1r - a2i * a1i,
                a2r * a1i + a2i * a1r,
                a2r * b1r - a2i * b1i + b2r,
                a2r * b1i + a2i * b1r + b2i)

    _, _, h_re, h_im = lax.associative_scan(combine, (a_re, a_im, bu_re, bu_im), axis=1)
    y = (jnp.einsum('blgp,gcp->blgc', h_re, c_re.astype(jnp.float32))
         - jnp.einsum('blgp,gcp->blgc', h_im, c_im.astype(jnp.float32)))
    y = y + d_skip.astype(jnp.float32).reshape(SSM_GROUPS, SSM_GROUP) * uf
    return y.reshape(B, L, SSM_WIDTH).astype(u.dtype)


def _fwd_setup_inputs(seed: int = 0) -> dict:
    key = jax.random.key(seed)
    ks = jax.random.split(key, 24)
    f32 = jnp.float32
    nrm = lambda k, shape, scale: jax.random.normal(k, shape, f32) * scale
    n_idx = jnp.arange(SSM_STATE, dtype=f32)
    return {
        "x": jax.random.normal(ks[0], (BATCH, SEQ, D_MODEL), f32),
        "norm_mix_pre": 1.0 + nrm(ks[1], (DEPTH, D_MODEL), 0.05),
        "norm_mix_post": 1.0 + nrm(ks[2], (DEPTH, D_MODEL), 0.05),
        "norm_mlp_pre": 1.0 + nrm(ks[3], (DEPTH, D_MODEL), 0.05),
        "norm_mlp_post": 1.0 + nrm(ks[4], (DEPTH, D_MODEL), 0.05),
        "w_in": nrm(ks[5], (DEPTH, D_MODEL, IN_WIDTH), D_MODEL ** -0.5),
        "rel_bias": nrm(ks[6], (N_BUCKETS, N_Q_HEADS), 0.5),
        "sinks": nrm(ks[7], (DEPTH, N_Q_HEADS), 0.5),
        "lam_re": -0.5 + nrm(ks[8], (DEPTH, SSM_GROUPS, SSM_STATE), 0.02),
        "lam_im": math.pi * n_idx + nrm(ks[9], (DEPTH, SSM_GROUPS, SSM_STATE), 0.02),
        "log_dt": jax.random.uniform(ks[10], (DEPTH, SSM_GROUPS), f32,
                                     math.log(DT_MIN), math.log(DT_MAX)),
        "b_re": nrm(ks[11], (DEPTH, SSM_GROUPS, SSM_STATE, SSM_GROUP), SSM_GROUP ** -0.5),
        "b_im": nrm(ks[12], (DEPTH, SSM_GROUPS, SSM_STATE, SSM_GROUP), SSM_GROUP ** -0.5),
        "c_re": nrm(ks[13], (DEPTH, SSM_GROUPS, SSM_GROUP, SSM_STATE), SSM_STATE ** -0.5),
        "c_im": nrm(ks[14], (DEPTH, SSM_GROUPS, SSM_GROUP, SSM_STATE), SSM_STATE ** -0.5),
        "d_skip": nrm(ks[15], (DEPTH, SSM_WIDTH), 1.0),
        "w_glu": nrm(ks[16], (DEPTH, SSM_WIDTH, SSM_WIDTH), SSM_WIDTH ** -0.5),
        "w_attn_branch": nrm(ks[17], (DEPTH, ATTN_WIDTH, D_MODEL), ATTN_WIDTH ** -0.5),
        "w_ssm_branch": nrm(ks[18], (DEPTH, SSM_WIDTH, D_MODEL), SSM_WIDTH ** -0.5),
        "w_out": nrm(ks[19], (DEPTH, D_MODEL, D_MODEL), D_MODEL ** -0.5),
        "w_ff_in": nrm(ks[20], (DEPTH, D_MODEL, D_FF), D_MODEL ** -0.5),
        "w_ff_out": nrm(ks[21], (DEPTH, D_FF, D_MODEL), D_FF ** -0.5),
    }


def _fwd_reference(x, norm_mix_pre, norm_mix_post, norm_mlp_pre, norm_mlp_post, w_in, rel_bias,
              sinks, lam_re, lam_im, log_dt, b_re, b_im, c_re, c_im, d_skip, w_glu,
              w_attn_branch, w_ssm_branch, w_out, w_ff_in, w_ff_out):
    B, L = x.shape[0], x.shape[1]
    for l in range(DEPTH):
        h = rmsnorm(x, norm_mix_pre[l])
        proj = h @ w_in[l]
        q, k, v, u, g_attn, g_ssm = jnp.split(proj, SPLITS, axis=-1)
        q = q.reshape(B, L, N_Q_HEADS, HEAD_DIM)
        k = k.reshape(B, L, N_KV_HEADS, HEAD_DIM)
        v = v.reshape(B, L, N_KV_HEADS, HEAD_DIM)
        y_attn = sliding_window_attention(q, k, v, sinks[l], rel_bias) @ w_attn_branch[l]
        z = jax.nn.gelu(s5_ssm(u, lam_re[l], lam_im[l], log_dt[l], b_re[l], b_im[l],
                               c_re[l], c_im[l], d_skip[l]))
        z = z * jax.nn.sigmoid(z @ w_glu[l])
        y_ssm = z @ w_ssm_branch[l]
        merged = jax.nn.sigmoid(g_attn) * y_attn + jax.nn.sigmoid(g_ssm) * y_ssm
        x = x + rmsnorm(merged @ w_out[l], norm_mix_post[l])
        h = rmsnorm(x, norm_mlp_pre[l])
        f = jnp.square(jax.nn.relu(h @ w_ff_in[l])) @ w_ff_out[l]
        x = x + rmsnorm(f, norm_mlp_post[l])
    return x


import jax as _jax
import jax.numpy as _jnp

TWIN_FORMAT = 'train_step'
FWD_PARAMS = ['x', 'norm_mix_pre', 'norm_mix_post', 'norm_mlp_pre', 'norm_mlp_post', 'w_in', 'rel_bias', 'sinks', 'lam_re', 'lam_im', 'log_dt', 'b_re', 'b_im', 'c_re', 'c_im', 'd_skip', 'w_glu', 'w_attn_branch', 'w_ssm_branch', 'w_out', 'w_ff_in', 'w_ff_out']
TWIN_WEIGHTS = ['norm_mix_pre', 'norm_mix_post', 'norm_mlp_pre', 'norm_mlp_post', 'w_in', 'rel_bias', 'sinks', 'lam_re', 'lam_im', 'log_dt', 'b_re', 'b_im', 'c_re', 'c_im', 'd_skip', 'w_glu', 'w_attn_branch', 'w_ssm_branch', 'w_out', 'w_ff_in', 'w_ff_out']
TWIN_DIFF_INPUT = 'x'
TWIN_INPUTS = ['x', 'norm_mix_pre', 'norm_mix_post', 'norm_mlp_pre', 'norm_mlp_post', 'w_in', 'rel_bias', 'sinks', 'lam_re', 'lam_im', 'log_dt', 'b_re', 'b_im', 'c_re', 'c_im', 'd_skip', 'w_glu', 'w_attn_branch', 'w_ssm_branch', 'w_out', 'w_ff_in', 'w_ff_out', 'loss_target', 'm_norm_mix_pre', 'm_norm_mix_post', 'm_norm_mlp_pre', 'm_norm_mlp_post', 'm_w_in', 'm_rel_bias', 'm_sinks', 'm_lam_re', 'm_lam_im', 'm_log_dt', 'm_b_re', 'm_b_im', 'm_c_re', 'm_c_im', 'm_d_skip', 'm_w_glu', 'm_w_attn_branch', 'm_w_ssm_branch', 'm_w_out', 'm_w_ff_in', 'm_w_ff_out', 'v_norm_mix_pre', 'v_norm_mix_post', 'v_norm_mlp_pre', 'v_norm_mlp_post', 'v_w_in', 'v_rel_bias', 'v_sinks', 'v_lam_re', 'v_lam_im', 'v_log_dt', 'v_b_re', 'v_b_im', 'v_c_re', 'v_c_im', 'v_d_skip', 'v_w_glu', 'v_w_attn_branch', 'v_w_ssm_branch', 'v_w_out', 'v_w_ff_in', 'v_w_ff_out']
TWIN_OUTPUTS = ['loss', 'grad_x', 'grad_norm_mix_pre', 'grad_norm_mix_post', 'grad_norm_mlp_pre', 'grad_norm_mlp_post', 'grad_w_in', 'grad_rel_bias', 'grad_sinks', 'grad_lam_re', 'grad_lam_im', 'grad_log_dt', 'grad_b_re', 'grad_b_im', 'grad_c_re', 'grad_c_im', 'grad_d_skip', 'grad_w_glu', 'grad_w_attn_branch', 'grad_w_ssm_branch', 'grad_w_out', 'grad_w_ff_in', 'grad_w_ff_out', 'delta_norm_mix_pre', 'delta_norm_mix_post', 'delta_norm_mlp_pre', 'delta_norm_mlp_post', 'delta_w_in', 'delta_rel_bias', 'delta_sinks', 'delta_lam_re', 'delta_lam_im', 'delta_log_dt', 'delta_b_re', 'delta_b_im', 'delta_c_re', 'delta_c_im', 'delta_d_skip', 'delta_w_glu', 'delta_w_attn_branch', 'delta_w_ssm_branch', 'delta_w_out', 'delta_w_ff_in', 'delta_w_ff_out', 'new_m_norm_mix_pre', 'new_m_norm_mix_post', 'new_m_norm_mlp_pre', 'new_m_norm_mlp_post', 'new_m_w_in', 'new_m_rel_bias', 'new_m_sinks', 'new_m_lam_re', 'new_m_lam_im', 'new_m_log_dt', 'new_m_b_re', 'new_m_b_im', 'new_m_c_re', 'new_m_c_im', 'new_m_d_skip', 'new_m_w_glu', 'new_m_w_attn_branch', 'new_m_w_ssm_branch', 'new_m_w_out', 'new_m_w_ff_in', 'new_m_w_ff_out', 'new_v_norm_mix_pre', 'new_v_norm_mix_post', 'new_v_norm_mlp_pre', 'new_v_norm_mlp_post', 'new_v_w_in', 'new_v_rel_bias', 'new_v_sinks', 'new_v_lam_re', 'new_v_lam_im', 'new_v_log_dt', 'new_v_b_re', 'new_v_b_im', 'new_v_c_re', 'new_v_c_im', 'new_v_d_skip', 'new_v_w_glu', 'new_v_w_attn_branch', 'new_v_w_ssm_branch', 'new_v_w_out', 'new_v_w_ff_in', 'new_v_w_ff_out']
TWIN_LEAF_KINDS = {'loss': 'loss', 'grad_x': 'grad_x', 'grad_norm_mix_pre': 'grad_w', 'grad_norm_mix_post': 'grad_w', 'grad_norm_mlp_pre': 'grad_w', 'grad_norm_mlp_post': 'grad_w', 'grad_w_in': 'grad_w', 'grad_rel_bias': 'grad_w', 'grad_sinks': 'grad_w', 'grad_lam_re': 'grad_w', 'grad_lam_im': 'grad_w', 'grad_log_dt': 'grad_w', 'grad_b_re': 'grad_w', 'grad_b_im': 'grad_w', 'grad_c_re': 'grad_w', 'grad_c_im': 'grad_w', 'grad_d_skip': 'grad_w', 'grad_w_glu': 'grad_w', 'grad_w_attn_branch': 'grad_w', 'grad_w_ssm_branch': 'grad_w', 'grad_w_out': 'grad_w', 'grad_w_ff_in': 'grad_w', 'grad_w_ff_out': 'grad_w', 'delta_norm_mix_pre': 'delta_w', 'delta_norm_mix_post': 'delta_w', 'delta_norm_mlp_pre': 'delta_w', 'delta_norm_mlp_post': 'delta_w', 'delta_w_in': 'delta_w', 'delta_rel_bias': 'delta_w', 'delta_sinks': 'delta_w', 'delta_lam_re': 'delta_w', 'delta_lam_im': 'delta_w', 'delta_log_dt': 'delta_w', 'delta_b_re': 'delta_w', 'delta_b_im': 'delta_w', 'delta_c_re': 'delta_w', 'delta_c_im': 'delta_w', 'delta_d_skip': 'delta_w', 'delta_w_glu': 'delta_w', 'delta_w_attn_branch': 'delta_w', 'delta_w_ssm_branch': 'delta_w', 'delta_w_out': 'delta_w', 'delta_w_ff_in': 'delta_w', 'delta_w_ff_out': 'delta_w', 'new_m_norm_mix_pre': 'new_m', 'new_m_norm_mix_post': 'new_m', 'new_m_norm_mlp_pre': 'new_m', 'new_m_norm_mlp_post': 'new_m', 'new_m_w_in': 'new_m', 'new_m_rel_bias': 'new_m', 'new_m_sinks': 'new_m', 'new_m_lam_re': 'new_m', 'new_m_lam_im': 'new_m', 'new_m_log_dt': 'new_m', 'new_m_b_re': 'new_m', 'new_m_b_im': 'new_m', 'new_m_c_re': 'new_m', 'new_m_c_im': 'new_m', 'new_m_d_skip': 'new_m', 'new_m_w_glu': 'new_m', 'new_m_w_attn_branch': 'new_m', 'new_m_w_ssm_branch': 'new_m', 'new_m_w_out': 'new_m', 'new_m_w_ff_in': 'new_m', 'new_m_w_ff_out': 'new_m', 'new_v_norm_mix_pre': 'new_v', 'new_v_norm_mix_post': 'new_v', 'new_v_norm_mlp_pre': 'new_v', 'new_v_norm_mlp_post': 'new_v', 'new_v_w_in': 'new_v', 'new_v_rel_bias': 'new_v', 'new_v_sinks': 'new_v', 'new_v_lam_re': 'new_v', 'new_v_lam_im': 'new_v', 'new_v_log_dt': 'new_v', 'new_v_b_re': 'new_v', 'new_v_b_im': 'new_v', 'new_v_c_re': 'new_v', 'new_v_c_im': 'new_v', 'new_v_d_skip': 'new_v', 'new_v_w_glu': 'new_v', 'new_v_w_attn_branch': 'new_v', 'new_v_w_ssm_branch': 'new_v', 'new_v_w_out': 'new_v', 'new_v_w_ff_in': 'new_v', 'new_v_w_ff_out': 'new_v'}


def _forward(args):
    return _fwd_reference(*[args[k] for k in FWD_PARAMS])


def _output_shape():
    def fwd():
        inp = _fwd_setup_inputs(0)
        return _fwd_reference(*[inp[k] for k in FWD_PARAMS])
    out = _jax.eval_shape(fwd)
    return out.shape, out.dtype

N_MICROBATCH = 1
ADAM_LR = 0.001
ADAM_B1 = 0.9
ADAM_B2 = 0.999
ADAM_EPS = 1e-08
ADAM_WD = 0.01
ADAM_STEP = 10
PER_EXAMPLE_BATCH_AXIS = {'x': 0, 'loss_target': 0}
SHARED_INPUTS = []
_WEIGHT_DTYPES = {'norm_mix_pre': _jnp.float32, 'norm_mix_post': _jnp.float32, 'norm_mlp_pre': _jnp.float32, 'norm_mlp_post': _jnp.float32, 'w_in': _jnp.float32, 'rel_bias': _jnp.float32, 'sinks': _jnp.float32, 'lam_re': _jnp.float32, 'lam_im': _jnp.float32, 'log_dt': _jnp.float32, 'b_re': _jnp.float32, 'b_im': _jnp.float32, 'c_re': _jnp.float32, 'c_im': _jnp.float32, 'd_skip': _jnp.float32, 'w_glu': _jnp.float32, 'w_attn_branch': _jnp.float32, 'w_ssm_branch': _jnp.float32, 'w_out': _jnp.float32, 'w_ff_in': _jnp.float32, 'w_ff_out': _jnp.float32}
MOMENT_SCALE = {'norm_mix_pre': 8.839586e-01, 'norm_mix_post': 3.333066e+01, 'norm_mlp_pre': 2.461069e+00, 'norm_mlp_post': 3.345840e+01, 'w_in': 4.601333e-01, 'rel_bias': 3.278974e-01, 'sinks': 1.031311e-01, 'lam_re': 5.776083e-02, 'lam_im': 6.596473e-02, 'log_dt': 3.723467e+01, 'b_re': 3.175326e-02, 'b_im': 3.229393e-02, 'c_re': 7.440953e-02, 'c_im': 6.514676e-02, 'd_skip': 9.134133e+00, 'w_glu': 1.344013e+00, 'w_attn_branch': 3.299959e-01, 'w_ssm_branch': 6.155624e+00, 'w_out': 6.209109e+00, 'w_ff_in': 1.284637e+00, 'w_ff_out': 5.464358e+00}


def _to_microbatches(a, axis):
    t = _jnp.moveaxis(a, axis, 0)
    t = t.reshape((N_MICROBATCH, t.shape[0] // N_MICROBATCH) + t.shape[1:])
    return _jnp.moveaxis(t, 1, axis + 1)


def setup_inputs(seed: int = 0) -> dict:
    inp = _fwd_setup_inputs(seed)
    key = _jax.random.fold_in(_jax.random.key(seed), 7919)
    shape, _ = _output_shape()
    out = dict(inp)
    out["loss_target"] = _jax.random.normal(_jax.random.fold_in(key, 0), shape, _jnp.float32)
    for i, name in enumerate(TWIN_WEIGHTS):
        w = inp[name].astype(_jnp.float32)
        if MOMENT_SCALE is None:
            s = _jnp.sqrt(_jnp.mean(_jnp.square(w)) + 1e-30)
        else:
            s = MOMENT_SCALE[name]
        km, kv = _jax.random.split(_jax.random.fold_in(key, i + 1))
        out[name] = w
        out["m_" + name] = s * _jax.random.normal(km, w.shape, _jnp.float32)
        out["v_" + name] = (s * s) * _jax.random.uniform(kv, w.shape, _jnp.float32, 0.5, 1.5)
    if N_MICROBATCH > 1:
        for name, axis in PER_EXAMPLE_BATCH_AXIS.items():
            out[name] = _to_microbatches(out[name], axis)
    return {'x': out['x'], 'norm_mix_pre': out['norm_mix_pre'], 'norm_mix_post': out['norm_mix_post'], 'norm_mlp_pre': out['norm_mlp_pre'], 'norm_mlp_post': out['norm_mlp_post'], 'w_in': out['w_in'], 'rel_bias': out['rel_bias'], 'sinks': out['sinks'], 'lam_re': out['lam_re'], 'lam_im': out['lam_im'], 'log_dt': out['log_dt'], 'b_re': out['b_re'], 'b_im': out['b_im'], 'c_re': out['c_re'], 'c_im': out['c_im'], 'd_skip': out['d_skip'], 'w_glu': out['w_glu'], 'w_attn_branch': out['w_attn_branch'], 'w_ssm_branch': out['w_ssm_branch'], 'w_out': out['w_out'], 'w_ff_in': out['w_ff_in'], 'w_ff_out': out['w_ff_out'], 'loss_target': out['loss_target'], 'm_norm_mix_pre': out['m_norm_mix_pre'], 'm_norm_mix_post': out['m_norm_mix_post'], 'm_norm_mlp_pre': out['m_norm_mlp_pre'], 'm_norm_mlp_post': out['m_norm_mlp_post'], 'm_w_in': out['m_w_in'], 'm_rel_bias': out['m_rel_bias'], 'm_sinks': out['m_sinks'], 'm_lam_re': out['m_lam_re'], 'm_lam_im': out['m_lam_im'], 'm_log_dt': out['m_log_dt'], 'm_b_re': out['m_b_re'], 'm_b_im': out['m_b_im'], 'm_c_re': out['m_c_re'], 'm_c_im': out['m_c_im'], 'm_d_skip': out['m_d_skip'], 'm_w_glu': out['m_w_glu'], 'm_w_attn_branch': out['m_w_attn_branch'], 'm_w_ssm_branch': out['m_w_ssm_branch'], 'm_w_out': out['m_w_out'], 'm_w_ff_in': out['m_w_ff_in'], 'm_w_ff_out': out['m_w_ff_out'], 'v_norm_mix_pre': out['v_norm_mix_pre'], 'v_norm_mix_post': out['v_norm_mix_post'], 'v_norm_mlp_pre': out['v_norm_mlp_pre'], 'v_norm_mlp_post': out['v_norm_mlp_post'], 'v_w_in': out['v_w_in'], 'v_rel_bias': out['v_rel_bias'], 'v_sinks': out['v_sinks'], 'v_lam_re': out['v_lam_re'], 'v_lam_im': out['v_lam_im'], 'v_log_dt': out['v_log_dt'], 'v_b_re': out['v_b_re'], 'v_b_im': out['v_b_im'], 'v_c_re': out['v_c_re'], 'v_c_im': out['v_c_im'], 'v_d_skip': out['v_d_skip'], 'v_w_glu': out['v_w_glu'], 'v_w_attn_branch': out['v_w_attn_branch'], 'v_w_ssm_branch': out['v_w_ssm_branch'], 'v_w_out': out['v_w_out'], 'v_w_ff_in': out['v_w_ff_in'], 'v_w_ff_out': out['v_w_ff_out']}


def _loss(weights, diff, rest, loss_target):
    with _jax.named_scope("forward"):
        args = {**rest, TWIN_DIFF_INPUT: diff, **{k: w.astype(_WEIGHT_DTYPES[k]) for k, w in weights.items()}}
        y = _forward(args)
    with _jax.named_scope("loss_head"):
        err = _jnp.square(y.astype(_jnp.float32) - loss_target)
        return 0.5 * _jnp.sum(_jnp.mean(err, axis=-1)) if err.ndim else 0.5 * err


def _adamw(w, g, m, v):
    m = ADAM_B1 * m + (1.0 - ADAM_B1) * g
    v = ADAM_B2 * v + (1.0 - ADAM_B2) * _jnp.square(g)
    m_hat = m / (1.0 - ADAM_B1 ** ADAM_STEP)
    v_hat = v / (1.0 - ADAM_B2 ** ADAM_STEP)
    delta = -ADAM_LR * (m_hat / (_jnp.sqrt(v_hat) + ADAM_EPS) + ADAM_WD * w)
    return delta, m, v


def reference(x, norm_mix_pre, norm_mix_post, norm_mlp_pre, norm_mlp_post, w_in, rel_bias, sinks, lam_re, lam_im, log_dt, b_re, b_im, c_re, c_im, d_skip, w_glu, w_attn_branch, w_ssm_branch, w_out, w_ff_in, w_ff_out, loss_target, m_norm_mix_pre, m_norm_mix_post, m_norm_mlp_pre, m_norm_mlp_post, m_w_in, m_rel_bias, m_sinks, m_lam_re, m_lam_im, m_log_dt, m_b_re, m_b_im, m_c_re, m_c_im, m_d_skip, m_w_glu, m_w_attn_branch, m_w_ssm_branch, m_w_out, m_w_ff_in, m_w_ff_out, v_norm_mix_pre, v_norm_mix_post, v_norm_mlp_pre, v_norm_mlp_post, v_w_in, v_rel_bias, v_sinks, v_lam_re, v_lam_im, v_log_dt, v_b_re, v_b_im, v_c_re, v_c_im, v_d_skip, v_w_glu, v_w_attn_branch, v_w_ssm_branch, v_w_out, v_w_ff_in, v_w_ff_out):
    given = dict(x=x, norm_mix_pre=norm_mix_pre, norm_mix_post=norm_mix_post, norm_mlp_pre=norm_mlp_pre, norm_mlp_post=norm_mlp_post, w_in=w_in, rel_bias=rel_bias, sinks=sinks, lam_re=lam_re, lam_im=lam_im, log_dt=log_dt, b_re=b_re, b_im=b_im, c_re=c_re, c_im=c_im, d_skip=d_skip, w_glu=w_glu, w_attn_branch=w_attn_branch, w_ssm_branch=w_ssm_branch, w_out=w_out, w_ff_in=w_ff_in, w_ff_out=w_ff_out, loss_target=loss_target, m_norm_mix_pre=m_norm_mix_pre, m_norm_mix_post=m_norm_mix_post, m_norm_mlp_pre=m_norm_mlp_pre, m_norm_mlp_post=m_norm_mlp_post, m_w_in=m_w_in, m_rel_bias=m_rel_bias, m_sinks=m_sinks, m_lam_re=m_lam_re, m_lam_im=m_lam_im, m_log_dt=m_log_dt, m_b_re=m_b_re, m_b_im=m_b_im, m_c_re=m_c_re, m_c_im=m_c_im, m_d_skip=m_d_skip, m_w_glu=m_w_glu, m_w_attn_branch=m_w_attn_branch, m_w_ssm_branch=m_w_ssm_branch, m_w_out=m_w_out, m_w_ff_in=m_w_ff_in, m_w_ff_out=m_w_ff_out, v_norm_mix_pre=v_norm_mix_pre, v_norm_mix_post=v_norm_mix_post, v_norm_mlp_pre=v_norm_mlp_pre, v_norm_mlp_post=v_norm_mlp_post, v_w_in=v_w_in, v_rel_bias=v_rel_bias, v_sinks=v_sinks, v_lam_re=v_lam_re, v_lam_im=v_lam_im, v_log_dt=v_log_dt, v_b_re=v_b_re, v_b_im=v_b_im, v_c_re=v_c_re, v_c_im=v_c_im, v_d_skip=v_d_skip, v_w_glu=v_w_glu, v_w_attn_branch=v_w_attn_branch, v_w_ssm_branch=v_w_ssm_branch, v_w_out=v_w_out, v_w_ff_in=v_w_ff_in, v_w_ff_out=v_w_ff_out)
    weights = {n: given[n] for n in TWIN_WEIGHTS}
    shared = {n: given[n] for n in SHARED_INPUTS}
    per_example = {n: given[n] for n in ['x']}
    grad_fn = _jax.value_and_grad(_loss, argnums=(0, 1))

    def one_microbatch(ex, loss_target):
        ex = dict(ex)
        diff = ex.pop(TWIN_DIFF_INPUT)
        return grad_fn(weights, diff, {**shared, **ex}, loss_target)

    if N_MICROBATCH == 1:
        loss, (grad_w, grad_x) = one_microbatch(per_example, given["loss_target"])
    else:
        def body(carry, xs):
            loss_sum, grad_sum = carry
            l_k, (gw_k, gx_k) = one_microbatch(xs[0], xs[1])
            with _jax.named_scope("update"):
                return (loss_sum + l_k, _jax.tree.map(_jnp.add, grad_sum, gw_k)), gx_k

        init = (_jnp.zeros((), _jnp.float32), _jax.tree.map(_jnp.zeros_like, weights))
        (loss, grad_w), grad_x = _jax.lax.scan(body, init, (per_example, given["loss_target"]))
    with _jax.named_scope("update"):
        delta_w, new_m, new_v = {}, {}, {}
        for n in TWIN_WEIGHTS:
            delta_w[n], new_m[n], new_v[n] = _adamw(weights[n], grad_w[n], given["m_" + n], given["v_" + n])
    return (loss, grad_x, *[grad_w[n] for n in TWIN_WEIGHTS], *[delta_w[n] for n in TWIN_WEIGHTS],
            *[new_m[n] for n in TWIN_WEIGHTS], *[new_v[n] for n in TWIN_WEIGHTS])
```

```python
import functools
import math

import numpy as np
import jax
import jax.numpy as jnp
from jax import lax
from jax.experimental import pallas as pl
from jax.experimental.pallas import tpu as pltpu

F32 = jnp.float32
BF16 = jnp.bfloat16

D_MODEL = 1024
N_HEADS = 8
N_KV = 2
Q_GROUP = 4
HEAD_DIM = 64
ATTN_W = 512
KV_W = 128
BLOCK = 128
N_BUCKETS = 32
MAX_DISTANCE = 128
NEG_INF = -1e30
SSM_W = 512
SSM_GROUP = 16
SSM_GROUPS = 32
SSM_STATE = 64
N_SUPER = 4
GROUPS_PER_SUPER = SSM_GROUPS // N_SUPER
SUPER_IN = GROUPS_PER_SUPER * SSM_GROUP
SUPER_HALF = GROUPS_PER_SUPER * SSM_STATE
SUPER_W = 2 * SUPER_HALF
STATE_COLS = N_SUPER * SUPER_W
D_FF = 4096
FF_CHUNKS = 4
IN_W = 3328
SPLITS = (0, 512, 640, 768, 1280, 2304, 3328)
RMS_EPS = 1e-6
N_CHIPS = 4
SUBLANES = 8

ADAM_LR = 0.001
ADAM_B1 = 0.9
ADAM_B2 = 0.999
ADAM_EPS = 1e-08
ADAM_WD = 0.01
ADAM_STEP = 10

VMEM_BIG = 56 * 1024 * 1024
SDS = jax.ShapeDtypeStruct
MESH_ID = pl.DeviceIdType.MESH
ANY = pl.BlockSpec(memory_space=pl.ANY)


def _bf(x):
    return x.astype(BF16)


def _mm(a, b):
    return jnp.dot(a, b, preferred_element_type=F32)


def _mm_nt(a, b):
    return lax.dot_general(a, b, (((1,), (1,)), ((), ())), preferred_element_type=F32)


def _mm_tn(a, b):
    return lax.dot_general(a, b, (((0,), (0,)), ((), ())), preferred_element_type=F32)


def _sig(x):
    return 1.0 / (1.0 + jnp.exp(-x))


def _rms(x, g):
    r = lax.rsqrt(jnp.mean(x * x, axis=-1, keepdims=True) + RMS_EPS)
    xh = x * r
    return xh * g, xh, r


def _rms_bwd(dout, xh, r, g):
    dg = jnp.sum(dout * xh, axis=0, keepdims=True)
    dxh = dout * g
    dx = r * (dxh - xh * jnp.mean(dxh * xh, axis=-1, keepdims=True))
    return dx, dg


_GELU_C = math.sqrt(2.0 / math.pi)


def _gelu_and_grad(x):
    x2 = x * x
    inner = _GELU_C * (x + 0.044715 * (x2 * x))
    t = jnp.tanh(inner)
    y = 0.5 * x * (1.0 + t)
    dy = 0.5 * (1.0 + t) + 0.5 * x * (1.0 - t * t) * (_GELU_C * (1.0 + 3.0 * 0.044715 * x2))
    return y, dy


def _zero_map(nd, *_):
    return (0,) * nd


def _params(n_axes, vmem=None):
    return pltpu.CompilerParams(dimension_semantics=("arbitrary",) * n_axes, vmem_limit_bytes=vmem)


def _rowcall(name, body, seq, tb, rows, consts, row_outs, acc_outs, scratch=(), reverse=False, vmem=None):
    nb = seq // tb
    rmap = (lambda i: (nb - 1 - i, 0)) if reverse else (lambda i: (i, 0))
    in_specs = [pl.BlockSpec((tb, a.shape[1]), rmap) for a in rows]
    in_specs += [pl.BlockSpec(a.shape, functools.partial(_zero_map, a.ndim)) for a in consts]
    out_specs = [pl.BlockSpec((tb, c), rmap) for c, _ in row_outs]
    out_specs += [pl.BlockSpec(s, functools.partial(_zero_map, len(s))) for s, _ in acc_outs]
    out_shape = [SDS((seq, c), dt) for c, dt in row_outs] + [SDS(s, dt) for s, dt in acc_outs]
    return pl.pallas_call(
        body, grid=(nb,), in_specs=in_specs, out_specs=out_specs, out_shape=out_shape,
        scratch_shapes=list(scratch), name=name, compiler_params=_params(1, vmem),
    )(*rows, *consts)


def _inproj_fwd(x, g1, w_in, tb):
    seq = x.shape[0]

    def body(x_ref, g_ref, w_ref, h_ref, q_ref, k_ref, v_ref, u_ref, ga_ref, gs_ref):
        h, _, _ = _rms(x_ref[...], g_ref[...])
        hb = _bf(h)
        h_ref[...] = hb
        pj = _mm(hb, w_ref[...])
        q_ref[...] = _bf(pj[:, SPLITS[0]:SPLITS[1]])
        k_ref[...] = _bf(pj[:, SPLITS[1]:SPLITS[2]])
        v_ref[...] = _bf(pj[:, SPLITS[2]:SPLITS[3]])
        u_ref[...] = pj[:, SPLITS[3]:SPLITS[4]]
        ga_ref[...] = pj[:, SPLITS[4]:SPLITS[5]]
        gs_ref[...] = pj[:, SPLITS[5]:SPLITS[6]]

    return _rowcall("inproj_fwd", body, seq, tb, [x], [g1, w_in],
                    [(D_MODEL, BF16), (ATTN_W, BF16), (KV_W, BF16), (KV_W, BF16), (SSM_W, F32),
                     (D_MODEL, F32), (D_MODEL, F32)], [], vmem=VMEM_BIG)


def _inproj_bwd(x, dx2, dq, dk, dv, du, dga, dgs, g1, w_in, tb):
    seq = x.shape[0]

    def body(x_ref, dx2_ref, dq_ref, dk_ref, dv_ref, du_ref, dga_ref, dgs_ref, g_ref, w_ref,
             dx_ref, dpj_ref, dg_ref):
        @pl.when(pl.program_id(0) == 0)
        def _():
            dg_ref[...] = jnp.zeros_like(dg_ref)

        dpj = jnp.concatenate([dq_ref[...], dk_ref[...], dv_ref[...], _bf(du_ref[...]),
                               dga_ref[...], dgs_ref[...]], axis=1)
        dpj_ref[...] = dpj
        dh = _mm_nt(dpj, w_ref[...])
        g = g_ref[...]
        _, xh, r = _rms(x_ref[...], g)
        dxn, dg = _rms_bwd(dh, xh, r, g)
        dx_ref[...] = dx2_ref[...] + dxn
        dg_ref[...] += dg

    return _rowcall("inproj_bwd", body, seq, tb, [x, dx2, dq, dk, dv, du, dga, dgs], [g1, w_in],
                    [(D_MODEL, F32), (IN_W, BF16)], [((1, D_MODEL), F32)], vmem=VMEM_BIG)


def _bucket_table():
    qi = np.arange(BLOCK)[:, None]
    kj = np.arange(2 * BLOCK)[None, :]
    dist = qi + BLOCK - kj
    max_exact = N_BUCKETS // 2
    d = np.maximum(dist, 0)
    df = np.maximum(d, 1).astype(np.float32)
    large = max_exact + (np.log(df / np.float32(max_exact)) / np.float32(math.log(MAX_DISTANCE / max_exact))
                         * np.float32(N_BUCKETS - max_exact)).astype(np.int32)
    large = np.minimum(large, N_BUCKETS - 1)
    bucket = np.where(d < max_exact, d, large)
    valid = (dist >= 0) & (dist < BLOCK)
    return np.where(valid, bucket, -1).astype(np.int32)


def _bias_table(rel_bias, bucket):
    def body(rb_ref, bk_ref, o_ref):
        bk = bk_ref[...]
        for h in range(N_HEADS):
            acc = jnp.zeros((BLOCK, 2 * BLOCK), F32)
            for b in range(N_BUCKETS):
                acc = jnp.where(bk == b, rb_ref[b, h], acc)
            o_ref[h] = acc

    return pl.pallas_call(
        body, out_shape=SDS((N_HEADS, BLOCK, 2 * BLOCK), F32),
        in_specs=[pl.BlockSpec(memory_space=pltpu.SMEM), pl.BlockSpec(memory_space=pltpu.VMEM)],
        out_specs=pl.BlockSpec(memory_space=pltpu.VMEM), name="bias_table",
    )(rel_bias, bucket)


def _bias_grad(dbias, bucket):
    def body(db_ref, bk_ref, o_ref):
        bk = bk_ref[...]
        for h in range(N_HEADS):
            db = db_ref[h]
            for b in range(N_BUCKETS):
                o_ref[b, h] = jnp.sum(jnp.where(bk == b, db, 0.0))

    return pl.pallas_call(
        body, out_shape=SDS((N_BUCKETS, N_HEADS), F32),
        in_specs=[pl.BlockSpec(memory_space=pltpu.VMEM), pl.BlockSpec(memory_space=pltpu.VMEM)],
        out_specs=pl.BlockSpec(memory_space=pltpu.SMEM), name="bias_grad",
    )(dbias, bucket)


def _band_mask(n):
    qi = lax.broadcasted_iota(jnp.int32, (Q_GROUP * BLOCK, 2 * BLOCK), 0) % BLOCK
    kj = lax.broadcasted_iota(jnp.int32, (Q_GROUP * BLOCK, 2 * BLOCK), 1)
    dist = qi + BLOCK - kj
    return (dist >= 0) & (dist < BLOCK) & ((kj >= BLOCK) | (n > 0))


def _attn_probs(qs, kk, bias, sink, ok):
    lg = _mm_nt(qs, kk) * (HEAD_DIM ** -0.5) + bias
    lg = jnp.where(ok, lg, NEG_INF)
    m = jnp.maximum(jnp.max(lg, axis=-1, keepdims=True), sink)
    p = jnp.exp(lg - m)
    es = jnp.exp(sink - m)
    den = jnp.sum(p, axis=-1, keepdims=True) + es
    return p / den, es / den


def _attn_fwd(q, k, v, bias, sink_rows):
    seq = q.shape[1]
    nblk = seq // BLOCK

    def body(q_ref, kp_ref, kc_ref, vp_ref, vc_ref, b_ref, s_ref, o_ref):
        ok = _band_mask(pl.program_id(0))
        for kh in range(N_KV):
            qs = q_ref[kh * Q_GROUP:(kh + 1) * Q_GROUP].reshape(Q_GROUP * BLOCK, HEAD_DIM)
            kk = jnp.concatenate([kp_ref[kh], kc_ref[kh]], axis=0)
            vv = jnp.concatenate([vp_ref[kh], vc_ref[kh]], axis=0)
            pr, _ = _attn_probs(qs, kk, b_ref[kh], s_ref[kh], ok)
            o = _mm(_bf(pr), vv)
            o_ref[kh * Q_GROUP:(kh + 1) * Q_GROUP] = _bf(o).reshape(Q_GROUP, BLOCK, HEAD_DIM)

    cur = lambda n: (0, n, 0)
    prev = lambda n: (0, jnp.maximum(n - 1, 0), 0)
    return pl.pallas_call(
        body, grid=(nblk,),
        in_specs=[pl.BlockSpec((N_HEADS, BLOCK, HEAD_DIM), cur),
                  pl.BlockSpec((N_KV, BLOCK, HEAD_DIM), prev), pl.BlockSpec((N_KV, BLOCK, HEAD_DIM), cur),
                  pl.BlockSpec((N_KV, BLOCK, HEAD_DIM), prev), pl.BlockSpec((N_KV, BLOCK, HEAD_DIM), cur),
                  pl.BlockSpec(bias.shape, functools.partial(_zero_map, 3)),
                  pl.BlockSpec(sink_rows.shape, functools.partial(_zero_map, 3))],
        out_specs=pl.BlockSpec((N_HEADS, BLOCK, HEAD_DIM), cur),
        out_shape=SDS((N_HEADS, seq, HEAD_DIM), BF16), name="attn_fwd", compiler_params=_params(1),
    )(q, k, k, v, v, bias, sink_rows)


def _attn_bwd(q, k, v, d_out, bias, sink_rows):
    seq = q.shape[1]
    nblk = seq // BLOCK

    def body(q_ref, kp_ref, kc_ref, vp_ref, vc_ref, do_ref, b_ref, s_ref,
             dq_ref, dk_ref, dv_ref, db_ref, ds_ref, ck_ref, cv_ref):
        n = pl.program_id(0)

        @pl.when(n == 0)
        def _():
            db_ref[...] = jnp.zeros_like(db_ref)
            ds_ref[...] = jnp.zeros_like(ds_ref)
            ck_ref[...] = jnp.zeros_like(ck_ref)
            cv_ref[...] = jnp.zeros_like(cv_ref)

        @pl.when(n < nblk)
        def _():
            ok = _band_mask(n)
            scale = HEAD_DIM ** -0.5
            for kh in range(N_KV):
                qs = q_ref[kh * Q_GROUP:(kh + 1) * Q_GROUP].reshape(Q_GROUP * BLOCK, HEAD_DIM)
                dos = do_ref[kh * Q_GROUP:(kh + 1) * Q_GROUP].reshape(Q_GROUP * BLOCK, HEAD_DIM)
                kk = jnp.concatenate([kp_ref[kh], kc_ref[kh]], axis=0)
                vv = jnp.concatenate([vp_ref[kh], vc_ref[kh]], axis=0)
                pr, ps = _attn_probs(qs, kk, b_ref[kh], s_ref[kh], ok)
                dp = _mm_nt(dos, vv)
                rs = jnp.sum(pr * dp, axis=-1, keepdims=True)
                dlg = pr * (dp - rs)
                ds_ref[kh] += -ps * rs
                db_ref[kh] += dlg
                dlb = _bf(dlg)
                dq = _mm(dlb, kk) * scale
                dq_ref[kh * Q_GROUP:(kh + 1) * Q_GROUP] = _bf(dq).reshape(Q_GROUP, BLOCK, HEAD_DIM)
                dkk = _mm_tn(dlb, qs) * scale
                dvv = _mm_tn(_bf(pr), dos)
                dk_ref[kh] = _bf(ck_ref[kh] + dkk[:BLOCK])
                ck_ref[kh] = dkk[BLOCK:]
                dv_ref[kh] = _bf(cv_ref[kh] + dvv[:BLOCK])
                cv_ref[kh] = dvv[BLOCK:]

        @pl.when(n == nblk)
        def _():
            dk_ref[...] = _bf(ck_ref[...])
            dv_ref[...] = _bf(cv_ref[...])

    cur = lambda n: (0, jnp.minimum(n, nblk - 1), 0)
    prev = lambda n: (0, jnp.maximum(jnp.minimum(n, nblk - 1) - 1, 0), 0)
    late = lambda n: (0, jnp.maximum(n - 1, 0), 0)
    kv_spec = lambda m: pl.BlockSpec((N_KV, BLOCK, HEAD_DIM), m)
    return pl.pallas_call(
        body, grid=(nblk + 1,),
        in_specs=[pl.BlockSpec((N_HEADS, BLOCK, HEAD_DIM), cur), kv_spec(prev), kv_spec(cur), kv_spec(prev),
                  kv_spec(cur), pl.BlockSpec((N_HEADS, BLOCK, HEAD_DIM), cur),
                  pl.BlockSpec(bias.shape, functools.partial(_zero_map, 3)),
                  pl.BlockSpec(sink_rows.shape, functools.partial(_zero_map, 3))],
        out_specs=[pl.BlockSpec((N_HEADS, BLOCK, HEAD_DIM), cur), kv_spec(late), kv_spec(late),
                   pl.BlockSpec(bias.shape, functools.partial(_zero_map, 3)),
                   pl.BlockSpec(sink_rows.shape, functools.partial(_zero_map, 3))],
        out_shape=[SDS((N_HEADS, seq, HEAD_DIM), BF16), SDS((N_KV, seq, HEAD_DIM), BF16),
                   SDS((N_KV, seq, HEAD_DIM), BF16), SDS(bias.shape, F32), SDS(sink_rows.shape, F32)],
        scratch_shapes=[pltpu.VMEM((N_KV, BLOCK, HEAD_DIM), F32), pltpu.VMEM((N_KV, BLOCK, HEAD_DIM), F32)],
        name="attn_bwd", compiler_params=_params(1),
    )(q, k, k, v, v, d_out, bias, sink_rows)


def _ssm_discretize(lam_re, lam_im, log_dt, b_re, b_im):
    dt = jnp.exp(log_dt)[:, None]
    mag = jnp.exp(lam_re * dt)
    ab_re = mag * jnp.cos(lam_im * dt)
    ab_im = mag * jnp.sin(lam_im * dt)
    nr = ab_re - 1.0
    den = lam_re * lam_re + lam_im * lam_im
    f_re = (nr * lam_re + ab_im * lam_im) / den
    f_im = (ab_im * lam_re - nr * lam_im) / den
    bb_re = f_re[..., None] * b_re - f_im[..., None] * b_im
    bb_im = f_re[..., None] * b_im + f_im[..., None] * b_re
    return ab_re, ab_im, bb_re, bb_im


def _state_layout(re, im):
    z = jnp.stack([re, im]).reshape(2, N_SUPER, GROUPS_PER_SUPER, SSM_STATE)
    return z.transpose(1, 0, 2, 3).reshape(STATE_COLS)


def _state_unlayout(vec):
    z = vec.reshape(N_SUPER, 2, GROUPS_PER_SUPER, SSM_STATE).transpose(1, 0, 2, 3)
    z = z.reshape(2, SSM_GROUPS, SSM_STATE)
    return z[0], z[1]


def _scan_tables(ab_re, ab_im):
    pw = [None, (ab_re, ab_im)]
    for _ in range(2, SUBLANES + 1):
        pr, pi_ = pw[-1]
        pw.append((pr * ab_re - pi_ * ab_im, pr * ab_im + pi_ * ab_re))
    rows = np.arange(SUBLANES)[:, None]
    fwd, bwd = [], []
    for shift in (1, 2, 4):
        fwd.append(_state_layout(*pw[shift])[None, :] * (rows >= shift).astype(np.float32))
        bwd.append(_state_layout(pw[shift][0], -pw[shift][1])[None, :] * (rows < SUBLANES - shift).astype(np.float32))
    fwd.append(jnp.stack([_state_layout(*pw[r + 1]) for r in range(SUBLANES)]))
    bwd.append(jnp.stack([_state_layout(pw[SUBLANES - r][0], -pw[SUBLANES - r][1]) for r in range(SUBLANES)]))
    return jnp.stack(fwd), jnp.stack(bwd)


_EYE = np.eye(GROUPS_PER_SUPER, dtype=np.float32)


def _b_matrix(bb_re, bb_im):
    bb = jnp.stack([bb_re, bb_im]).reshape(2, N_SUPER, GROUPS_PER_SUPER, SSM_STATE, SSM_GROUP)
    m = jnp.einsum('rsgpc,gh->sgcrhp', bb, _EYE)
    return m.reshape(N_SUPER, SUPER_IN, SUPER_W)


def _b_matrix_grad(dm):
    d = dm.reshape(N_SUPER, GROUPS_PER_SUPER, SSM_GROUP, 2, GROUPS_PER_SUPER, SSM_STATE)
    d = jnp.sum(d * _EYE[None, :, None, None, :, None], axis=4)
    d = d.transpose(3, 0, 1, 4, 2).reshape(2, SSM_GROUPS, SSM_STATE, SSM_GROUP)
    return d[0], d[1]


def _c_matrix(c_re, c_im):
    cc = jnp.stack([c_re, -c_im]).reshape(2, N_SUPER, GROUPS_PER_SUPER, SSM_GROUP, SSM_STATE)
    m = jnp.einsum('rsgcp,gh->srgphc', cc, _EYE)
    return m.reshape(N_SUPER, SUPER_W, SUPER_IN)


def _c_matrix_grad(dm):
    d = dm.reshape(N_SUPER, 2, GROUPS_PER_SUPER, SSM_STATE, GROUPS_PER_SUPER, SSM_GROUP)
    d = jnp.sum(d * _EYE[None, None, :, None, :, None], axis=4)
    d = d.transpose(1, 0, 2, 4, 3).reshape(2, SSM_GROUPS, SSM_GROUP, SSM_STATE)
    return d[0], -d[1]


def _scan_rows(buf_ref, tab_ref, carry_ref, n_groups, reverse, h_ref=None, da_ref=None):
    edge = 0 if reverse else SUBLANES - 1
    for sb in range(N_SUPER):
        cr = pl.ds(sb * SUPER_W, SUPER_HALF)
        ci = pl.ds(sb * SUPER_W + SUPER_HALF, SUPER_HALF)

        def step(gi, carry, cr=cr, ci=ci):
            g = (n_groups - 1 - gi) if reverse else gi
            rows = pl.ds(pl.multiple_of(g * SUBLANES, SUBLANES), SUBLANES)
            c_re, c_im = carry[0], carry[1]
            xr = buf_ref[rows, cr]
            xi = buf_ref[rows, ci]
            for k, shift in enumerate((1, 2, 4)):
                s = (SUBLANES - shift) if reverse else shift
                sr = pltpu.roll(xr, s, 0)
                si = pltpu.roll(xi, s, 0)
                ar = tab_ref[k, :, cr]
                ai = tab_ref[k, :, ci]
                xr, xi = xr + ar * sr - ai * si, xi + ar * si + ai * sr
            pr = tab_ref[3, :, cr]
            pi_ = tab_ref[3, :, ci]
            xr, xi = xr + pr * c_re - pi_ * c_im, xi + pr * c_im + pi_ * c_re
            buf_ref[rows, cr] = xr
            buf_ref[rows, ci] = xi
            out = [jnp.broadcast_to(xr[edge:edge + 1], xr.shape), jnp.broadcast_to(xi[edge:edge + 1], xi.shape)]
            if h_ref is not None:
                last = lax.broadcasted_iota(jnp.int32, xr.shape, 0) == SUBLANES - 1
                gr = jnp.where(last, c_re, pltpu.roll(xr, SUBLANES - 1, 0))
                gim = jnp.where(last, c_im, pltpu.roll(xi, SUBLANES - 1, 0))
                hr = h_ref[rows, cr]
                hi = h_ref[rows, ci]
                out += [carry[2] + gr * hr + gim * hi, carry[3] + gim * hr - gr * hi]
            return tuple(out)

        init = [carry_ref[:, cr], carry_ref[:, ci]]
        if h_ref is not None:
            init += [da_ref[:, cr], da_ref[:, ci]]
        fin = lax.fori_loop(0, n_groups, step, tuple(init))
        carry_ref[:, cr] = fin[0]
        carry_ref[:, ci] = fin[1]
        if h_ref is not None:
            da_ref[:, cr] = fin[2]
            da_ref[:, ci] = fin[3]


def _ssm_fwd(u, bmat, cmat, tab, d_skip, tb):
    seq = u.shape[0]

    def body(u_ref, b_ref, c_ref, t_ref, d_ref, s_ref, h_ref, carry_ref):
        @pl.when(pl.program_id(0) == 0)
        def _():
            carry_ref[...] = jnp.zeros_like(carry_ref)

        u_blk = u_ref[...]
        ub = _bf(u_blk)
        for sb in range(N_SUPER):
            h_ref[:, sb * SUPER_W:(sb + 1) * SUPER_W] = _mm(ub[:, sb * SUPER_IN:(sb + 1) * SUPER_IN], b_ref[sb])
        _scan_rows(h_ref, t_ref, carry_ref, tb // SUBLANES, False)
        ys = [_mm(_bf(h_ref[:, sb * SUPER_W:(sb + 1) * SUPER_W]), c_ref[sb]) for sb in range(N_SUPER)]
        s_ref[...] = jnp.concatenate(ys, axis=1) + d_ref[...] * u_blk

    return _rowcall("ssm_fwd", body, seq, tb, [u], [bmat, cmat, tab, d_skip],
                    [(SSM_W, F32), (STATE_COLS, F32)], [],
                    scratch=[pltpu.VMEM((SUBLANES, STATE_COLS), F32)], vmem=VMEM_BIG)


def _ssm_bwd(ds, u, h, bmat_t, cmat_t, tab, d_skip, tb):
    seq = u.shape[0]

    def body(ds_ref, u_ref, h_ref, bt_ref, ct_ref, t_ref, d_ref,
             du_ref, db_ref, dc_ref, da_ref, dd_ref, g_ref, carry_ref):
        @pl.when(pl.program_id(0) == 0)
        def _():
            carry_ref[...] = jnp.zeros_like(carry_ref)
            db_ref[...] = jnp.zeros_like(db_ref)
            dc_ref[...] = jnp.zeros_like(dc_ref)
            da_ref[...] = jnp.zeros_like(da_ref)
            dd_ref[...] = jnp.zeros_like(dd_ref)

        ds_blk = ds_ref[...]
        dsb = _bf(ds_blk)
        u_blk = u_ref[...]
        ub = _bf(u_blk)
        for sb in range(N_SUPER):
            g_ref[:, sb * SUPER_W:(sb + 1) * SUPER_W] = _mm(dsb[:, sb * SUPER_IN:(sb + 1) * SUPER_IN], ct_ref[sb])
        _scan_rows(g_ref, t_ref, carry_ref, tb // SUBLANES, True, h_ref=h_ref, da_ref=da_ref)
        dus = []
        for sb in range(N_SUPER):
            gb = _bf(g_ref[:, sb * SUPER_W:(sb + 1) * SUPER_W])
            dus.append(_mm(gb, bt_ref[sb]))
            db_ref[sb] += _mm_tn(ub[:, sb * SUPER_IN:(sb + 1) * SUPER_IN], gb)
            dc_ref[sb] += _mm_tn(_bf(h_ref[:, sb * SUPER_W:(sb + 1) * SUPER_W]),
                                 dsb[:, sb * SUPER_IN:(sb + 1) * SUPER_IN])
        du_ref[...] = jnp.concatenate(dus, axis=1) + d_ref[...] * ds_blk
        dd_ref[...] += jnp.sum(ds_blk * u_blk, axis=0, keepdims=True)

    return _rowcall("ssm_bwd", body, seq, tb, [ds, u, h], [bmat_t, cmat_t, tab, d_skip],
                    [(SSM_W, F32)],
                    [((N_SUPER, SUPER_IN, SUPER_W), F32), ((N_SUPER, SUPER_W, SUPER_IN), F32),
                     ((SUBLANES, STATE_COLS), F32), ((1, SSM_W), F32)],
                    scratch=[pltpu.VMEM((tb, STATE_COLS), F32), pltpu.VMEM((SUBLANES, STATE_COLS), F32)],
                    reverse=True, vmem=VMEM_BIG)


def _merge_core(s, attb, ga, gs, wg_ref, wab_ref, wsb_ref, wout_ref):
    zg, dgelu = _gelu_and_grad(s)
    zgb = _bf(zg)
    sg = _sig(_mm(zgb, wg_ref[...]))
    z = zg * sg
    zb = _bf(z)
    ys = jnp.concatenate([_mm(zb, wsb_ref[j]) for j in range(N_CHIPS)], axis=1)
    ya = jnp.concatenate([_mm(attb, wab_ref[j]) for j in range(N_CHIPS)], axis=1)
    sa = _sig(ga)
    ss = _sig(gs)
    mgb = _bf(sa * ya + ss * ys)
    o = _mm(mgb, wout_ref[...])
    return dict(zg=zg, dgelu=dgelu, zgb=zgb, sg=sg, zb=zb, ys=ys, ya=ya, sa=sa, ss=ss, mgb=mgb, o=o)


def _merge_fwd(x, s, att, ga, gs, g2, w_glu, w_ab, w_sb, w_out, tb):
    seq = x.shape[0]

    def body(x_ref, s_ref, att_ref, ga_ref, gs_ref, g_ref, wg_ref, wab_ref, wsb_ref, wout_ref, x2_ref):
        f = _merge_core(s_ref[...], att_ref[...], ga_ref[...], gs_ref[...], wg_ref, wab_ref, wsb_ref, wout_ref)
        n, _, _ = _rms(f["o"], g_ref[...])
        x2_ref[...] = x_ref[...] + n

    return _rowcall("merge_fwd", body, seq, tb, [x, s, att, ga, gs], [g2, w_glu, w_ab, w_sb, w_out],
                    [(D_MODEL, F32)], [], vmem=VMEM_BIG)[0]


def _merge_bwd(dx2, s, att, ga, gs, g2, w_glu, w_ab, w_sb, w_out, tb):
    seq = s.shape[0]
    cw = D_MODEL // N_CHIPS

    def body(dx2_ref, s_ref, att_ref, ga_ref, gs_ref, g_ref, wg_ref, wab_ref, wsb_ref, wout_ref,
             ds_ref, datt_ref, dga_ref, dgs_ref, dg_ref, dwg_ref, dwab_ref, dwsb_ref, dwout_ref):
        @pl.when(pl.program_id(0) == 0)
        def _():
            for r in (dg_ref, dwg_ref, dwab_ref, dwsb_ref, dwout_ref):
                r[...] = jnp.zeros_like(r)

        attb = att_ref[...]
        f = _merge_core(s_ref[...], attb, ga_ref[...], gs_ref[...], wg_ref, wab_ref, wsb_ref, wout_ref)
        g = g_ref[...]
        _, oh, r2 = _rms(f["o"], g)
        do, dg = _rms_bwd(dx2_ref[...], oh, r2, g)
        dg_ref[...] += dg
        dob = _bf(do)
        dwout_ref[...] += _mm_tn(f["mgb"], dob)
        dmg = _mm_nt(dob, wout_ref[...])
        sa, ss = f["sa"], f["ss"]
        dyab = _bf(dmg * sa)
        dysb = _bf(dmg * ss)
        dga_ref[...] = _bf(dmg * f["ya"] * sa * (1.0 - sa))
        dgs_ref[...] = _bf(dmg * f["ys"] * ss * (1.0 - ss))
        dwab = _mm_tn(attb, dyab)
        dwsb = _mm_tn(f["zb"], dysb)
        datt = jnp.zeros((tb, ATTN_W), F32)
        dz = jnp.zeros((tb, SSM_W), F32)
        for j in range(N_CHIPS):
            dwab_ref[j] += dwab[:, j * cw:(j + 1) * cw]
            dwsb_ref[j] += dwsb[:, j * cw:(j + 1) * cw]
            datt = datt + _mm_nt(dyab[:, j * cw:(j + 1) * cw], wab_ref[j])
            dz = dz + _mm_nt(dysb[:, j * cw:(j + 1) * cw], wsb_ref[j])
        datt_ref[...] = _bf(datt)
        sg, zg = f["sg"], f["zg"]
        dglb = _bf(dz * zg * sg * (1.0 - sg))
        dwg_ref[...] += _mm_tn(f["zgb"], dglb)
        dzg = dz * sg + _mm_nt(dglb, wg_ref[...])
        ds_ref[...] = dzg * f["dgelu"]

    return _rowcall("merge_bwd", body, seq, tb, [dx2, s, att, ga, gs], [g2, w_glu, w_ab, w_sb, w_out],
                    [(SSM_W, F32), (ATTN_W, BF16), (D_MODEL, BF16), (D_MODEL, BF16)],
                    [((1, D_MODEL), F32), (w_glu.shape, F32), (w_ab.shape, F32), (w_sb.shape, F32),
                     (w_out.shape, F32)], vmem=VMEM_BIG)


def _mlp_fwd_loss(x2, target, g3, g4, w_ffi, w_ffo, tb):
    seq = x2.shape[0]

    def body(x2_ref, t_ref, g3_ref, g4_ref, wi_ref, wo_ref, dy_ref, df_ref, h_ref, loss_ref, dg_ref):
        @pl.when(pl.program_id(0) == 0)
        def _():
            loss_ref[...] = jnp.zeros_like(loss_ref)
            dg_ref[...] = jnp.zeros_like(dg_ref)

        x2_blk = x2_ref[...]
        h3, _, _ = _rms(x2_blk, g3_ref[...])
        hb = _bf(h3)
        h_ref[...] = hb
        f = jnp.zeros((tb, D_MODEL), F32)
        for j in range(FF_CHUNKS):
            a = _mm(hb, wi_ref[j])
            f = f + _mm(_bf(jnp.square(jnp.maximum(a, 0.0))), wo_ref[j])
        g4 = g4_ref[...]
        n4, fh, r4 = _rms(f, g4)
        e = (x2_blk + n4) - t_ref[...]
        loss_ref[...] += 0.5 * jnp.sum(jnp.mean(e * e, axis=-1, keepdims=True))
        dy = e * (1.0 / D_MODEL)
        dy_ref[...] = dy
        df, dg = _rms_bwd(dy, fh, r4, g4)
        df_ref[...] = _bf(df)
        dg_ref[...] += dg

    return _rowcall("mlp_fwd_loss", body, seq, tb, [x2, target], [g3, g4, w_ffi, w_ffo],
                    [(D_MODEL, F32), (D_MODEL, BF16), (D_MODEL, BF16)],
                    [((SUBLANES, 128), F32), ((1, D_MODEL), F32)], vmem=VMEM_BIG)


def _mlp_bwd(x2, dy, df, h3, g3, w_ffi, w_ffo, tb):
    seq = x2.shape[0]
    cw = D_FF // FF_CHUNKS

    def body(x2_ref, dy_ref, df_ref, h_ref, g3_ref, wi_ref, wo_ref, dx_ref, act_ref, da_ref, dg_ref):
        @pl.when(pl.program_id(0) == 0)
        def _():
            dg_ref[...] = jnp.zeros_like(dg_ref)

        hb = h_ref[...]
        dfb = df_ref[...]
        dh = jnp.zeros((tb, D_MODEL), F32)
        for j in range(FF_CHUNKS):
            ra = jnp.maximum(_mm(hb, wi_ref[j]), 0.0)
            act_ref[:, j * cw:(j + 1) * cw] = _bf(ra * ra)
            dab = _bf(_mm_nt(dfb, wo_ref[j]) * (2.0 * ra))
            da_ref[:, j * cw:(j + 1) * cw] = dab
            dh = dh + _mm_nt(dab, wi_ref[j])
        g3 = g3_ref[...]
        _, xh, r3 = _rms(x2_ref[...], g3)
        dxn, dg = _rms_bwd(dh, xh, r3, g3)
        dx_ref[...] = dy_ref[...] + dxn
        dg_ref[...] += dg

    return _rowcall("mlp_bwd", body, seq, tb, [x2, dy, df, h3], [g3, w_ffi, w_ffo],
                    [(D_MODEL, F32), (D_FF, BF16), (D_FF, BF16)], [((1, D_MODEL), F32)], vmem=VMEM_BIG)


def _matmul_tn(name, a, b, tk, tn, tl, chunk_major):
    seq, kdim = a.shape
    ndim = b.shape[1]

    def body(a_ref, b_ref, o_ref):
        @pl.when(pl.program_id(2) == 0)
        def _():
            o_ref[...] = jnp.zeros_like(o_ref)

        o_ref[...] += _mm_tn(a_ref[...], b_ref[...])

    if chunk_major:
        out_shape = SDS((ndim // tn, kdim, tn), F32)
        out_spec = pl.BlockSpec((None, tk, tn), lambda k, n, l: (n, k, 0))
    else:
        out_shape = SDS((kdim, ndim), F32)
        out_spec = pl.BlockSpec((tk, tn), lambda k, n, l: (k, n))
    return pl.pallas_call(
        body, grid=(kdim // tk, ndim // tn, seq // tl),
        in_specs=[pl.BlockSpec((tl, tk), lambda k, n, l: (l, k)), pl.BlockSpec((tl, tn), lambda k, n, l: (l, n))],
        out_specs=out_spec, out_shape=out_shape, name=name, compiler_params=_params(3, VMEM_BIG),
    )(a, b)


def _ew_call(name, fn, ins, n_out):
    rows, cols = ins[0].shape
    tr = rows
    while tr * cols * 4 > (1 << 20) and tr % 16 == 0:
        tr //= 2
    spec = pl.BlockSpec((tr, cols), lambda i: (i, 0))

    def body(*refs):
        outs = fn(*[r[...] for r in refs[:len(ins)]])
        for r, o in zip(refs[len(ins):], outs):
            r[...] = o

    return pl.pallas_call(
        body, grid=(rows // tr,), in_specs=[spec] * len(ins), out_specs=[spec] * n_out,
        out_shape=[SDS((rows, cols), F32)] * n_out, name=name, compiler_params=_params(1),
    )(*ins)


def _adam_math(w, g, m, v):
    m2 = ADAM_B1 * m + (1.0 - ADAM_B1) * g
    v2 = ADAM_B2 * v + (1.0 - ADAM_B2) * (g * g)
    m_hat = m2 / (1.0 - ADAM_B1 ** ADAM_STEP)
    v_hat = v2 / (1.0 - ADAM_B2 ** ADAM_STEP)
    delta = -ADAM_LR * (m_hat / (jnp.sqrt(v_hat) + ADAM_EPS) + ADAM_WD * w)
    return delta, m2, v2


def _sum4(name, own, r0, r1, r2):
    return _ew_call(name, lambda a, b, c, d: (((a + b.astype(F32)) + c.astype(F32)) + d.astype(F32),),
                    [own, r0, r1, r2], 1)[0]


def _adam_pair(name, w, p_own, p_sib, m, v):
    def fn(w_, a, b, m_, v_):
        g = a + b
        return (g,) + _adam_math(w_, g, m_, v_)

    return _ew_call(name, fn, [w, p_own, p_sib, m, v], 4)


def _adam_single(name, w, g, m, v):
    return _ew_call(name, lambda w_, g_, m_, v_: _adam_math(w_, g_, m_, v_), [w, g, m, v], 3)


def _place():
    return lax.axis_index("x"), lax.axis_index("y"), lax.axis_index("c")


def _other_chips(x, y):
    return [(1 - x, y), (x, 1 - y), (1 - x, 1 - y)]


def _gather_chips(shards):
    n = len(shards)

    def body(*refs):
        ins, outs = refs[:n], refs[n:2 * n]
        send, recv, loc = refs[2 * n:]
        x, y, c = _place()
        me = 2 * x + y
        peers = _other_chips(x, y)
        local = [pltpu.make_async_copy(ins[a], outs[a].at[me], loc.at[a]) for a in range(n)]
        for cp in local:
            cp.start()

        def copy(a, j, slot):
            px, py = peers[j]
            return pltpu.make_async_remote_copy(
                src_ref=ins[a], dst_ref=outs[a].at[slot], send_sem=send.at[a, j], recv_sem=recv.at[a, j],
                device_id=(px, py, c), device_id_type=MESH_ID)

        sends = [copy(a, j, me) for a in range(n) for j in range(3)]
        for cp in sends:
            cp.start()
        for a in range(n):
            for j, (px, py) in enumerate(peers):
                copy(a, j, 2 * px + py).wait_recv()
        for cp in sends:
            cp.wait_send()
        for cp in local:
            cp.wait()

    return pl.pallas_call(
        body, in_specs=[ANY] * n, out_specs=[ANY] * n,
        out_shape=[SDS((N_CHIPS,) + s.shape, s.dtype) for s in shards],
        scratch_shapes=[pltpu.SemaphoreType.DMA((n, 3)), pltpu.SemaphoreType.DMA((n, 3)),
                        pltpu.SemaphoreType.DMA((n,))],
        name="gather_weights",
    )(*shards)


def _scatter_chips(chunks):
    n = len(chunks)

    def body(*refs):
        ins, outs = refs[:n], refs[n:2 * n]
        send, recv = refs[2 * n:]
        x, y, c = _place()
        peers = _other_chips(x, y)

        def copy(a, j):
            px, py = peers[j]
            return pltpu.make_async_remote_copy(
                src_ref=ins[a].at[2 * px + py], dst_ref=outs[a].at[j], send_sem=send.at[a, j],
                recv_sem=recv.at[a, j], device_id=(px, py, c), device_id_type=MESH_ID)

        cps = [copy(a, j) for a in range(n) for j in range(3)]
        for cp in cps:
            cp.start()
        for cp in cps:
            cp.wait_recv()
        for cp in cps:
            cp.wait_send()

    return pl.pallas_call(
        body, in_specs=[ANY] * n, out_specs=[ANY] * n,
        out_shape=[SDS((3,) + s.shape[1:], s.dtype) for s in chunks],
        scratch_shapes=[pltpu.SemaphoreType.DMA((n, 3)), pltpu.SemaphoreType.DMA((n, 3))],
        name="scatter_grads",
    )(*chunks)


def _swap_sibling(arrs):
    n = len(arrs)

    def body(*refs):
        ins, outs = refs[:n], refs[n:2 * n]
        send, recv = refs[2 * n:]
        x, y, c = _place()
        cps = [pltpu.make_async_remote_copy(
            src_ref=ins[a], dst_ref=outs[a], send_sem=send.at[a], recv_sem=recv.at[a],
            device_id=(x, y, 1 - c), device_id_type=MESH_ID) for a in range(n)]
        for cp in cps:
            cp.start()
        for cp in cps:
            cp.wait_recv()
        for cp in cps:
            cp.wait_send()

    return pl.pallas_call(
        body, in_specs=[ANY] * n, out_specs=[ANY] * n, out_shape=[SDS(s.shape, s.dtype) for s in arrs],
        scratch_shapes=[pltpu.SemaphoreType.DMA((n,)), pltpu.SemaphoreType.DMA((n,))],
        name="swap_sibling",
    )(*arrs)


def _allreduce_small(packed):
    rows, cols = packed.shape
    n_dev = 8

    def body(in_ref, out_ref, slots, send, recv):
        x, y, c = _place()
        me = 4 * x + 2 * y + c
        slots[me] = in_ref[...]
        cps = []
        for k in range(1, n_dev):
            peer = (x ^ (k >> 2), y ^ ((k >> 1) & 1), c ^ (k & 1))
            cps.append(pltpu.make_async_remote_copy(
                src_ref=in_ref, dst_ref=slots.at[me], send_sem=send.at[k - 1], recv_sem=recv.at[k - 1],
                device_id=peer, device_id_type=MESH_ID))
        for cp in cps:
            cp.start()
        for k in range(1, n_dev):
            pltpu.make_async_remote_copy(
                src_ref=in_ref, dst_ref=slots.at[me ^ k], send_sem=send.at[k - 1], recv_sem=recv.at[k - 1],
                device_id=(x, y, c), device_id_type=MESH_ID).wait_recv()
        acc = slots[0]
        for d in range(1, n_dev):
            acc = acc + slots[d]
        out_ref[...] = acc
        for cp in cps:
            cp.wait_send()

    return pl.pallas_call(
        body, in_specs=[pl.BlockSpec(memory_space=pltpu.VMEM)], out_specs=pl.BlockSpec(memory_space=pltpu.VMEM),
        out_shape=SDS((rows, cols), F32),
        scratch_shapes=[pltpu.VMEM((n_dev, rows, cols), F32), pltpu.SemaphoreType.DMA((n_dev - 1,)),
                        pltpu.SemaphoreType.DMA((n_dev - 1,))],
        name="allreduce_small", compiler_params=pltpu.CompilerParams(vmem_limit_bytes=32 * 1024 * 1024),
    )(packed)


def _heads(t, n):
    return t.reshape(t.shape[0], n, HEAD_DIM).transpose(1, 0, 2)


def _unheads(t):
    return t.transpose(1, 0, 2).reshape(t.shape[1], t.shape[0] * HEAD_DIM)


def _local_step(x, target, small, w_in, w_glu, w_ab, w_sb, w_out, w_ffi, w_ffo, tb):
    g1, g2, g3, g4 = small["norm_mix_pre"], small["norm_mix_post"], small["norm_mlp_pre"], small["norm_mlp_post"]
    bucket = jnp.asarray(_bucket_table())

    bias = _bias_table(small["rel_bias"], bucket).reshape(N_KV, Q_GROUP * BLOCK, 2 * BLOCK)
    sink_rows = jnp.repeat(small["sinks"].reshape(N_KV, Q_GROUP), BLOCK, axis=1)[..., None]
    disc_args = (small["lam_re"], small["lam_im"], small["log_dt"], small["b_re"], small["b_im"])
    (ab_re, ab_im, bb_re, bb_im), disc_vjp = jax.vjp(_ssm_discretize, *disc_args)
    tab_f, tab_b = _scan_tables(ab_re, ab_im)
    bmat = _bf(_b_matrix(bb_re, bb_im))
    cmat = _bf(_c_matrix(small["c_re"], small["c_im"]))
    d_skip = small["d_skip"]

    h1, q, k, v, u, ga, gs = _inproj_fwd(x, g1, w_in, tb)
    qh, kh, vh = _heads(q, N_HEADS), _heads(k, N_KV), _heads(v, N_KV)
    att = _unheads(_attn_fwd(qh, kh, vh, bias, sink_rows))
    s, h = _ssm_fwd(u, bmat, cmat, tab_f, d_skip, tb)
    x2 = _merge_fwd(x, s, att, ga, gs, g2, w_glu, w_ab, w_sb, w_out, tb)
    dy, df, h3, loss_acc, dg4 = _mlp_fwd_loss(x2, target, g3, g4, w_ffi, w_ffo, tb)

    dx2, act, da, dg3 = _mlp_bwd(x2, dy, df, h3, g3, w_ffi, w_ffo, tb)
    tl = min(512, x.shape[0])
    d_ffi = _matmul_tn("grad_w_ff_in", h3, da, D_MODEL, D_FF // FF_CHUNKS, tl, True)
    d_ffo = _matmul_tn("grad_w_ff_out", act, df, D_FF // FF_CHUNKS, D_MODEL, tl, False)
    ds, datt, dga, dgs, dg2, d_glu, d_ab, d_sb, d_out = _merge_bwd(dx2, s, att, ga, gs, g2, w_glu, w_ab, w_sb,
                                                                  w_out, tb)
    du, d_bmat, d_cmat, da_acc, dd_skip = _ssm_bwd(ds, u, h, bmat.transpose(0, 2, 1), cmat.transpose(0, 2, 1),
                                                   tab_b, d_skip, tb)
    dqh, dkh, dvh, dbias, dsink_rows = _attn_bwd(qh, kh, vh, _heads(datt, N_HEADS), bias, sink_rows)
    dx, dpj, dg1 = _inproj_bwd(x, dx2, _unheads(dqh), _unheads(dkh), _unheads(dvh), du, dga, dgs, g1, w_in, tb)
    d_in = _matmul_tn("grad_w_in", h1, dpj, D_MODEL, IN_W // 2, tl, True)

    dab_re, dab_im = _state_unlayout(jnp.sum(da_acc, axis=0))
    dbb_re, dbb_im = _b_matrix_grad(d_bmat)
    d_lam_re, d_lam_im, d_log_dt, d_b_re, d_b_im = disc_vjp((dab_re, dab_im, dbb_re, dbb_im))
    d_c_re, d_c_im = _c_matrix_grad(d_cmat)
    d_rel = _bias_grad(dbias.reshape(N_HEADS, BLOCK, 2 * BLOCK), bucket)
    d_sinks = jnp.sum(dsink_rows.reshape(N_HEADS, BLOCK), axis=1)
    small_grads = dict(
        norm_mix_pre=dg1, norm_mix_post=dg2, norm_mlp_pre=dg3, norm_mlp_post=dg4, rel_bias=d_rel, sinks=d_sinks,
        lam_re=d_lam_re, lam_im=d_lam_im, log_dt=d_log_dt, b_re=d_b_re, b_im=d_b_im, c_re=d_c_re, c_im=d_c_im,
        d_skip=dd_skip)
    half = IN_W // 2
    quarter = IN_W // N_CHIPS
    d_in4 = jnp.stack([d_in[0][:, :quarter], d_in[0][:, quarter:], d_in[1][:, :quarter], d_in[1][:, quarter:]])
    assert half == 2 * quarter
    big_grads = dict(
        w_in=d_in4, w_glu=d_glu.reshape(N_CHIPS, SSM_W // N_CHIPS, SSM_W), w_attn_branch=d_ab, w_ssm_branch=d_sb,
        w_out=d_out.reshape(N_CHIPS, D_MODEL // N_CHIPS, D_MODEL), w_ff_in=d_ffi,
        w_ff_out=d_ffo.reshape(N_CHIPS, D_FF // N_CHIPS, D_MODEL))
    return loss_acc, dx, small_grads, big_grads


SMALL = ['norm_mix_pre', 'norm_mix_post', 'norm_mlp_pre', 'norm_mlp_post', 'rel_bias', 'sinks', 'lam_re', 'lam_im',
         'log_dt', 'b_re', 'b_im', 'c_re', 'c_im', 'd_skip']
BIG = ['w_in', 'w_glu', 'w_attn_branch', 'w_ssm_branch', 'w_out', 'w_ff_in', 'w_ff_out']
WEIGHTS = ['norm_mix_pre', 'norm_mix_post', 'norm_mlp_pre', 'norm_mlp_post', 'w_in', 'rel_bias', 'sinks', 'lam_re',
           'lam_im', 'log_dt', 'b_re', 'b_im', 'c_re', 'c_im', 'd_skip', 'w_glu', 'w_attn_branch', 'w_ssm_branch',
           'w_out', 'w_ff_in', 'w_ff_out']
PACK_COLS = 1024


def _pack(arrs):
    flat = jnp.concatenate([a.reshape(-1) for a in arrs])
    rows = -(-flat.shape[0] // (PACK_COLS * SUBLANES)) * SUBLANES
    return jnp.pad(flat, (0, rows * PACK_COLS - flat.shape[0])).reshape(rows, PACK_COLS)


def _unpack(packed, shapes):
    flat = packed.reshape(-1)
    out, at = [], 0
    for s in shapes:
        n = int(np.prod(s))
        out.append(flat[at:at + n].reshape(s))
        at += n
    return out


def kernel(x, norm_mix_pre, norm_mix_post, norm_mlp_pre, norm_mlp_post, w_in, rel_bias, sinks, lam_re, lam_im, log_dt, b_re, b_im, c_re, c_im, d_skip, w_glu, w_attn_branch, w_ssm_branch, w_out, w_ff_in, w_ff_out, loss_target, m_norm_mix_pre, m_norm_mix_post, m_norm_mlp_pre, m_norm_mlp_post, m_w_in, m_rel_bias, m_sinks, m_lam_re, m_lam_im, m_log_dt, m_b_re, m_b_im, m_c_re, m_c_im, m_d_skip, m_w_glu, m_w_attn_branch, m_w_ssm_branch, m_w_out, m_w_ff_in, m_w_ff_out, v_norm_mix_pre, v_norm_mix_post, v_norm_mlp_pre, v_norm_mlp_post, v_w_in, v_rel_bias, v_sinks, v_lam_re, v_lam_im, v_log_dt, v_b_re, v_b_im, v_c_re, v_c_im, v_d_skip, v_w_glu, v_w_attn_branch, v_w_ssm_branch, v_w_out, v_w_ff_in, v_w_ff_out):
    env = dict(locals())
    w = {n: env[n] for n in WEIGHTS}
    m = {n: env["m_" + n] for n in WEIGHTS}
    v = {n: env["v_" + n] for n in WEIGHTS}
    seq = x.shape[1]
    tb = min(256, seq)

    shards = [_bf(w[n][0]) for n in BIG]
    g_in, g_glu, g_ab, g_sb, g_out, g_ffi, g_ffo = _gather_chips(shards)
    full_in = g_in.transpose(1, 0, 2).reshape(D_MODEL, IN_W)
    full_glu = g_glu.reshape(SSM_W, SSM_W)
    full_out = g_out.reshape(D_MODEL, D_MODEL)

    small = {n: w[n] for n in ('norm_mix_pre', 'norm_mix_post', 'norm_mlp_pre', 'norm_mlp_post', 'rel_bias')}
    small.update({n: w[n][0] for n in ('sinks', 'lam_re', 'lam_im', 'log_dt', 'b_re', 'b_im', 'c_re', 'c_im')})
    small['d_skip'] = w['d_skip']
    loss_acc, dx, small_g, big_g = _local_step(
        x[0], loss_target[0], small, full_in, full_glu, g_ab, g_sb, full_out, g_ffi, g_ffo, tb)

    loss = lax.psum(loss_acc[0, 0], ("x", "y", "c"))

    me = 2 * lax.axis_index("x") + lax.axis_index("y")
    chunks = [big_g[n] for n in BIG]
    received = _scatter_chips([_bf(cks) for cks in chunks])
    partial = []
    for n, cks, rcv in zip(BIG, chunks, received):
        own = lax.dynamic_index_in_dim(cks, me, 0, keepdims=False)
        partial.append(_sum4("sum_" + n, own, rcv[0], rcv[1], rcv[2]))
    sibling = _swap_sibling(partial)
    grads, deltas, new_m, new_v = {}, {}, {}, {}
    for n, p_own, p_sib in zip(BIG, partial, sibling):
        g, d, m2, v2 = _adam_pair("adam_" + n, w[n][0], p_own, p_sib, m[n][0], v[n][0])
        grads[n], deltas[n], new_m[n], new_v[n] = g[None], d[None], m2[None], v2[None]

    shapes = [w[n].shape for n in SMALL]
    g_small = _allreduce_small(_pack([small_g[n] for n in SMALL]))
    d_small, m_small, v_small = _adam_single(
        "adam_small", _pack([w[n] for n in SMALL]), g_small, _pack([m[n] for n in SMALL]),
        _pack([v[n] for n in SMALL]))
    for dst, src in ((grads, g_small), (deltas, d_small), (new_m, m_small), (new_v, v_small)):
        dst.update(dict(zip(SMALL, _unpack(src, shapes))))

    return (loss, dx[None], *[grads[n] for n in WEIGHTS], *[deltas[n] for n in WEIGHTS],
            *[new_m[n] for n in WEIGHTS], *[new_v[n] for n in WEIGHTS])
```

```python
import functools
import math

import numpy as np
import jax
import jax.numpy as jnp
from jax import lax
from jax.experimental import pallas as pl
from jax.experimental.pallas import tpu as pltpu

F32 = jnp.float32
BF16 = jnp.bfloat16

D_MODEL = 1024
N_HEADS = 8
N_KV = 2
Q_GROUP = 4
HEAD_DIM = 64
ATTN_W = 512
KV_W = 128
BLOCK = 128
N_BUCKETS = 32
MAX_DISTANCE = 128
NEG_INF = -1e30
SSM_W = 512
SSM_GROUP = 16
SSM_GROUPS = 32
SSM_STATE = 64
N_SUPER = 4
GROUPS_PER_SUPER = SSM_GROUPS // N_SUPER
SUPER_IN = GROUPS_PER_SUPER * SSM_GROUP
SUPER_HALF = GROUPS_PER_SUPER * SSM_STATE
SUPER_W = 2 * SUPER_HALF
STATE_COLS = N_SUPER * SUPER_W
D_FF = 4096
FF_CHUNKS = 4
IN_W = 3328
SPLITS = (0, 512, 640, 768, 1280, 2304, 3328)
RMS_EPS = 1e-6
N_CHIPS = 4
SUBLANES = 8

ADAM_LR = 0.001
ADAM_B1 = 0.9
ADAM_B2 = 0.999
ADAM_EPS = 1e-08
ADAM_WD = 0.01
ADAM_STEP = 10

VMEM_BIG = 56 * 1024 * 1024
SDS = jax.ShapeDtypeStruct
MESH_ID = pl.DeviceIdType.MESH
ANY = pl.BlockSpec(memory_space=pl.ANY)


def _bf(x):
    return x.astype(BF16)


def _mm(a, b):
    return jnp.dot(a, b, preferred_element_type=F32)


def _mm_nt(a, b):
    return lax.dot_general(a, b, (((1,), (1,)), ((), ())), preferred_element_type=F32)


def _mm_tn(a, b):
    return lax.dot_general(a, b, (((0,), (0,)), ((), ())), preferred_element_type=F32)


def _sig(x):
    return 1.0 / (1.0 + jnp.exp(-x))


def _rms(x, g):
    r = lax.rsqrt(jnp.mean(x * x, axis=-1, keepdims=True) + RMS_EPS)
    xh = x * r
    return xh * g, xh, r


def _rms_bwd(dout, xh, r, g):
    dg = jnp.sum(dout * xh, axis=0, keepdims=True)
    dxh = dout * g
    dx = r * (dxh - xh * jnp.mean(dxh * xh, axis=-1, keepdims=True))
    return dx, dg


_GELU_C = math.sqrt(2.0 / math.pi)


def _gelu_and_grad(x):
    x2 = x * x
    inner = _GELU_C * (x + 0.044715 * (x2 * x))
    t = jnp.tanh(inner)
    y = 0.5 * x * (1.0 + t)
    dy = 0.5 * (1.0 + t) + 0.5 * x * (1.0 - t * t) * (_GELU_C * (1.0 + 3.0 * 0.044715 * x2))
    return y, dy


def _zero_map(nd, *_):
    return (0,) * nd


def _params(n_axes, vmem=None):
    return pltpu.CompilerParams(dimension_semantics=("arbitrary",) * n_axes, vmem_limit_bytes=vmem)


class _Exchange:
    def __init__(self, ins, outs, sems, start, wait):
        self.ins, self.outs, self.sems, self.start, self.wait = list(ins), list(outs), list(sems), start, wait


def _fused_call(name, body, grid, in_specs, out_specs, out_shape, scratch, args, exchange, params):
    n_in, n_out, n_scr = len(in_specs), len(out_specs), len(scratch)
    if exchange is None:
        fn = body
    else:
        ex = exchange
        n_xi, n_xo = len(ex.ins), len(ex.outs)
        last = grid[0] - 1

        def fn(*refs):
            at = 0
            parts = []
            for n in (n_in, n_xi, n_out, n_xo, n_scr, len(ex.sems)):
                parts.append(refs[at:at + n])
                at += n
            ins, x_in, outs, x_out, scr, x_sem = parts

            @pl.when(pl.program_id(0) == 0)
            def _():
                ex.start(x_in, x_out, x_sem)

            body(*ins, *outs, *scr)

            @pl.when(pl.program_id(0) == last)
            def _():
                ex.wait(x_in, x_out, x_sem)

        in_specs = list(in_specs) + [ANY] * n_xi
        out_specs = list(out_specs) + [ANY] * n_xo
        out_shape = list(out_shape) + ex.outs
        scratch = list(scratch) + ex.sems
        args = list(args) + ex.ins
    return pl.pallas_call(fn, grid=grid, in_specs=in_specs, out_specs=out_specs, out_shape=out_shape,
                          scratch_shapes=list(scratch), name=name, compiler_params=params)(*args)


def _exchange_alone(name, ex):
    def body(*refs):
        n_xi, n_xo = len(ex.ins), len(ex.outs)
        x_in, x_out, x_sem = refs[:n_xi], refs[n_xi:n_xi + n_xo], refs[n_xi + n_xo:]
        ex.start(x_in, x_out, x_sem)
        ex.wait(x_in, x_out, x_sem)

    return pl.pallas_call(body, in_specs=[ANY] * len(ex.ins), out_specs=[ANY] * len(ex.outs), out_shape=ex.outs,
                          scratch_shapes=ex.sems, name=name)(*ex.ins)


def _rowcall(name, body, seq, tb, rows, consts, row_outs, acc_outs, scratch=(), reverse=False, vmem=None,
             exchange=None):
    nb = seq // tb
    rmap = (lambda i: (nb - 1 - i, 0)) if reverse else (lambda i: (i, 0))
    in_specs = [pl.BlockSpec((tb, a.shape[1]), rmap) for a in rows]
    in_specs += [pl.BlockSpec(a.shape, functools.partial(_zero_map, a.ndim)) for a in consts]
    out_specs = [pl.BlockSpec((tb, c), rmap) for c, _ in row_outs]
    out_specs += [pl.BlockSpec(s, functools.partial(_zero_map, len(s))) for s, _ in acc_outs]
    out_shape = [SDS((seq, c), dt) for c, dt in row_outs] + [SDS(s, dt) for s, dt in acc_outs]
    return _fused_call(name, body, (nb,), in_specs, out_specs, out_shape, list(scratch), [*rows, *consts],
                       exchange, _params(1, vmem))


def _inproj_fwd(x, g1, w_in, tb, exchange=None):
    seq = x.shape[0]

    def body(x_ref, g_ref, w_ref, h_ref, q_ref, k_ref, v_ref, u_ref, ga_ref, gs_ref):
        h, _, _ = _rms(x_ref[...], g_ref[...])
        hb = _bf(h)
        h_ref[...] = hb
        pj = _mm(hb, w_ref[...])
        q_ref[...] = _bf(pj[:, SPLITS[0]:SPLITS[1]])
        k_ref[...] = _bf(pj[:, SPLITS[1]:SPLITS[2]])
        v_ref[...] = _bf(pj[:, SPLITS[2]:SPLITS[3]])
        u_ref[...] = pj[:, SPLITS[3]:SPLITS[4]]
        ga_ref[...] = pj[:, SPLITS[4]:SPLITS[5]]
        gs_ref[...] = pj[:, SPLITS[5]:SPLITS[6]]

    return _rowcall("inproj_fwd", body, seq, tb, [x], [g1, w_in],
                    [(D_MODEL, BF16), (ATTN_W, BF16), (KV_W, BF16), (KV_W, BF16), (SSM_W, F32),
                     (D_MODEL, F32), (D_MODEL, F32)], [], vmem=VMEM_BIG, exchange=exchange)


def _inproj_bwd(x, dx2, dq, dk, dv, du, dga, dgs, g1, w_in, tb, exchange=None):
    seq = x.shape[0]

    def body(x_ref, dx2_ref, dq_ref, dk_ref, dv_ref, du_ref, dga_ref, dgs_ref, g_ref, w_ref,
             dx_ref, dpj_ref, dg_ref):
        @pl.when(pl.program_id(0) == 0)
        def _():
            dg_ref[...] = jnp.zeros_like(dg_ref)

        dpj = jnp.concatenate([dq_ref[...], dk_ref[...], dv_ref[...], _bf(du_ref[...]),
                               dga_ref[...], dgs_ref[...]], axis=1)
        dpj_ref[...] = dpj
        dh = _mm_nt(dpj, w_ref[...])
        g = g_ref[...]
        _, xh, r = _rms(x_ref[...], g)
        dxn, dg = _rms_bwd(dh, xh, r, g)
        dx_ref[...] = dx2_ref[...] + dxn
        dg_ref[...] += dg

    return _rowcall("inproj_bwd", body, seq, tb, [x, dx2, dq, dk, dv, du, dga, dgs], [g1, w_in],
                    [(D_MODEL, F32), (IN_W, BF16)], [((1, D_MODEL), F32)], vmem=VMEM_BIG, exchange=exchange)


def _bucket_table():
    qi = np.arange(BLOCK)[:, None]
    kj = np.arange(2 * BLOCK)[None, :]
    dist = qi + BLOCK - kj
    max_exact = N_BUCKETS // 2
    d = np.maximum(dist, 0)
    df = np.maximum(d, 1).astype(np.float32)
    large = max_exact + (np.log(df / np.float32(max_exact)) / np.float32(math.log(MAX_DISTANCE / max_exact))
                         * np.float32(N_BUCKETS - max_exact)).astype(np.int32)
    large = np.minimum(large, N_BUCKETS - 1)
    bucket = np.where(d < max_exact, d, large)
    valid = (dist >= 0) & (dist < BLOCK)
    return np.where(valid, bucket, -1).astype(np.int32)


def _bias_table(rel_bias, bucket):
    def body(rb_ref, bk_ref, o_ref):
        bk = bk_ref[...]
        for h in range(N_HEADS):
            acc = jnp.zeros((BLOCK, 2 * BLOCK), F32)
            for b in range(N_BUCKETS):
                acc = jnp.where(bk == b, rb_ref[b, h], acc)
            o_ref[h] = acc

    return pl.pallas_call(
        body, out_shape=SDS((N_HEADS, BLOCK, 2 * BLOCK), F32),
        in_specs=[pl.BlockSpec(memory_space=pltpu.SMEM), pl.BlockSpec(memory_space=pltpu.VMEM)],
        out_specs=pl.BlockSpec(memory_space=pltpu.VMEM), name="bias_table",
    )(rel_bias, bucket)


def _bias_grad(dbias, bucket):
    def body(db_ref, bk_ref, o_ref):
        bk = bk_ref[...]
        for h in range(N_HEADS):
            db = db_ref[h]
            for b in range(N_BUCKETS):
                o_ref[b, h] = jnp.sum(jnp.where(bk == b, db, 0.0))

    return pl.pallas_call(
        body, out_shape=SDS((N_BUCKETS, N_HEADS), F32),
        in_specs=[pl.BlockSpec(memory_space=pltpu.VMEM), pl.BlockSpec(memory_space=pltpu.VMEM)],
        out_specs=pl.BlockSpec(memory_space=pltpu.SMEM), name="bias_grad",
    )(dbias, bucket)


def _band_mask(n):
    qi = lax.broadcasted_iota(jnp.int32, (Q_GROUP * BLOCK, 2 * BLOCK), 0) % BLOCK
    kj = lax.broadcasted_iota(jnp.int32, (Q_GROUP * BLOCK, 2 * BLOCK), 1)
    dist = qi + BLOCK - kj
    return (dist >= 0) & (dist < BLOCK) & ((kj >= BLOCK) | (n > 0))


def _attn_probs(qs, kk, bias, sink, ok):
    lg = _mm_nt(qs, kk) * (HEAD_DIM ** -0.5) + bias
    lg = jnp.where(ok, lg, NEG_INF)
    m = jnp.maximum(jnp.max(lg, axis=-1, keepdims=True), sink)
    p = jnp.exp(lg - m)
    es = jnp.exp(sink - m)
    den = jnp.sum(p, axis=-1, keepdims=True) + es
    return p / den, es / den


def _attn_fwd(q, k, v, bias, sink_rows, exchange=None):
    seq = q.shape[1]
    nblk = seq // BLOCK

    def body(q_ref, kp_ref, kc_ref, vp_ref, vc_ref, b_ref, s_ref, o_ref):
        ok = _band_mask(pl.program_id(0))
        for kh in range(N_KV):
            qs = q_ref[kh * Q_GROUP:(kh + 1) * Q_GROUP].reshape(Q_GROUP * BLOCK, HEAD_DIM)
            kk = jnp.concatenate([kp_ref[kh], kc_ref[kh]], axis=0)
            vv = jnp.concatenate([vp_ref[kh], vc_ref[kh]], axis=0)
            pr, _ = _attn_probs(qs, kk, b_ref[kh], s_ref[kh], ok)
            o = _mm(_bf(pr), vv)
            o_ref[kh * Q_GROUP:(kh + 1) * Q_GROUP] = _bf(o).reshape(Q_GROUP, BLOCK, HEAD_DIM)

    cur = lambda n: (0, n, 0)
    prev = lambda n: (0, jnp.maximum(n - 1, 0), 0)
    return _fused_call(
        "attn_fwd", body, (nblk,),
        [pl.BlockSpec((N_HEADS, BLOCK, HEAD_DIM), cur),
         pl.BlockSpec((N_KV, BLOCK, HEAD_DIM), prev), pl.BlockSpec((N_KV, BLOCK, HEAD_DIM), cur),
         pl.BlockSpec((N_KV, BLOCK, HEAD_DIM), prev), pl.BlockSpec((N_KV, BLOCK, HEAD_DIM), cur),
         pl.BlockSpec(bias.shape, functools.partial(_zero_map, 3)),
         pl.BlockSpec(sink_rows.shape, functools.partial(_zero_map, 3))],
        [pl.BlockSpec((N_HEADS, BLOCK, HEAD_DIM), cur)], [SDS((N_HEADS, seq, HEAD_DIM), BF16)], [],
        [q, k, k, v, v, bias, sink_rows], exchange, _params(1))


def _attn_bwd(q, k, v, d_out, bias, sink_rows, exchange=None):
    seq = q.shape[1]
    nblk = seq // BLOCK

    def body(q_ref, kp_ref, kc_ref, vp_ref, vc_ref, do_ref, b_ref, s_ref,
             dq_ref, dk_ref, dv_ref, db_ref, ds_ref, ck_ref, cv_ref):
        n = pl.program_id(0)

        @pl.when(n == 0)
        def _():
            db_ref[...] = jnp.zeros_like(db_ref)
            ds_ref[...] = jnp.zeros_like(ds_ref)
            ck_ref[...] = jnp.zeros_like(ck_ref)
            cv_ref[...] = jnp.zeros_like(cv_ref)

        @pl.when(n < nblk)
        def _():
            ok = _band_mask(n)
            scale = HEAD_DIM ** -0.5
            for kh in range(N_KV):
                qs = q_ref[kh * Q_GROUP:(kh + 1) * Q_GROUP].reshape(Q_GROUP * BLOCK, HEAD_DIM)
                dos = do_ref[kh * Q_GROUP:(kh + 1) * Q_GROUP].reshape(Q_GROUP * BLOCK, HEAD_DIM)
                kk = jnp.concatenate([kp_ref[kh], kc_ref[kh]], axis=0)
                vv = jnp.concatenate([vp_ref[kh], vc_ref[kh]], axis=0)
                pr, ps = _attn_probs(qs, kk, b_ref[kh], s_ref[kh], ok)
                dp = _mm_nt(dos, vv)
                rs = jnp.sum(pr * dp, axis=-1, keepdims=True)
                dlg = pr * (dp - rs)
                ds_ref[kh] += -ps * rs
                db_ref[kh] += dlg
                dlb = _bf(dlg)
                dq = _mm(dlb, kk) * scale
                dq_ref[kh * Q_GROUP:(kh + 1) * Q_GROUP] = _bf(dq).reshape(Q_GROUP, BLOCK, HEAD_DIM)
                dkk = _mm_tn(dlb, qs) * scale
                dvv = _mm_tn(_bf(pr), dos)
                dk_ref[kh] = _bf(ck_ref[kh] + dkk[:BLOCK])
                ck_ref[kh] = dkk[BLOCK:]
                dv_ref[kh] = _bf(cv_ref[kh] + dvv[:BLOCK])
                cv_ref[kh] = dvv[BLOCK:]

        @pl.when(n == nblk)
        def _():
            dk_ref[...] = _bf(ck_ref[...])
            dv_ref[...] = _bf(cv_ref[...])

    cur = lambda n: (0, jnp.minimum(n, nblk - 1), 0)
    prev = lambda n: (0, jnp.maximum(jnp.minimum(n, nblk - 1) - 1, 0), 0)
    late = lambda n: (0, jnp.maximum(n - 1, 0), 0)
    kv_spec = lambda m: pl.BlockSpec((N_KV, BLOCK, HEAD_DIM), m)
    return _fused_call(
        "attn_bwd", body, (nblk + 1,),
        [pl.BlockSpec((N_HEADS, BLOCK, HEAD_DIM), cur), kv_spec(prev), kv_spec(cur), kv_spec(prev),
         kv_spec(cur), pl.BlockSpec((N_HEADS, BLOCK, HEAD_DIM), cur),
         pl.BlockSpec(bias.shape, functools.partial(_zero_map, 3)),
         pl.BlockSpec(sink_rows.shape, functools.partial(_zero_map, 3))],
        [pl.BlockSpec((N_HEADS, BLOCK, HEAD_DIM), cur), kv_spec(late), kv_spec(late),
         pl.BlockSpec(bias.shape, functools.partial(_zero_map, 3)),
         pl.BlockSpec(sink_rows.shape, functools.partial(_zero_map, 3))],
        [SDS((N_HEADS, seq, HEAD_DIM), BF16), SDS((N_KV, seq, HEAD_DIM), BF16),
         SDS((N_KV, seq, HEAD_DIM), BF16), SDS(bias.shape, F32), SDS(sink_rows.shape, F32)],
        [pltpu.VMEM((N_KV, BLOCK, HEAD_DIM), F32), pltpu.VMEM((N_KV, BLOCK, HEAD_DIM), F32)],
        [q, k, k, v, v, d_out, bias, sink_rows], exchange, _params(1))


def _ssm_discretize(lam_re, lam_im, log_dt, b_re, b_im):
    dt = jnp.exp(log_dt)[:, None]
    mag = jnp.exp(lam_re * dt)
    ab_re = mag * jnp.cos(lam_im * dt)
    ab_im = mag * jnp.sin(lam_im * dt)
    nr = ab_re - 1.0
    den = lam_re * lam_re + lam_im * lam_im
    f_re = (nr * lam_re + ab_im * lam_im) / den
    f_im = (ab_im * lam_re - nr * lam_im) / den
    bb_re = f_re[..., None] * b_re - f_im[..., None] * b_im
    bb_im = f_re[..., None] * b_im + f_im[..., None] * b_re
    return ab_re, ab_im, bb_re, bb_im


def _state_layout(re, im):
    z = jnp.stack([re, im]).reshape(2, N_SUPER, GROUPS_PER_SUPER, SSM_STATE)
    return z.transpose(1, 0, 2, 3).reshape(STATE_COLS)


def _state_unlayout(vec):
    z = vec.reshape(N_SUPER, 2, GROUPS_PER_SUPER, SSM_STATE).transpose(1, 0, 2, 3)
    z = z.reshape(2, SSM_GROUPS, SSM_STATE)
    return z[0], z[1]


def _scan_tables(ab_re, ab_im):
    pw = [None, (ab_re, ab_im)]
    for _ in range(2, SUBLANES + 1):
        pr, pi_ = pw[-1]
        pw.append((pr * ab_re - pi_ * ab_im, pr * ab_im + pi_ * ab_re))
    rows = np.arange(SUBLANES)[:, None]
    fwd, bwd = [], []
    for shift in (1, 2, 4):
        fwd.append(_state_layout(*pw[shift])[None, :] * (rows >= shift).astype(np.float32))
        bwd.append(_state_layout(pw[shift][0], -pw[shift][1])[None, :] * (rows < SUBLANES - shift).astype(np.float32))
    fwd.append(jnp.stack([_state_layout(*pw[r + 1]) for r in range(SUBLANES)]))
    bwd.append(jnp.stack([_state_layout(pw[SUBLANES - r][0], -pw[SUBLANES - r][1]) for r in range(SUBLANES)]))
    return jnp.stack(fwd), jnp.stack(bwd)


_EYE = np.eye(GROUPS_PER_SUPER, dtype=np.float32)


def _b_matrix(bb_re, bb_im):
    bb = jnp.stack([bb_re, bb_im]).reshape(2, N_SUPER, GROUPS_PER_SUPER, SSM_STATE, SSM_GROUP)
    m = jnp.einsum('rsgpc,gh->sgcrhp', bb, _EYE)
    return m.reshape(N_SUPER, SUPER_IN, SUPER_W)


def _b_matrix_grad(dm):
    d = dm.reshape(N_SUPER, GROUPS_PER_SUPER, SSM_GROUP, 2, GROUPS_PER_SUPER, SSM_STATE)
    d = jnp.sum(d * _EYE[None, :, None, None, :, None], axis=4)
    d = d.transpose(3, 0, 1, 4, 2).reshape(2, SSM_GROUPS, SSM_STATE, SSM_GROUP)
    return d[0], d[1]


def _c_matrix(c_re, c_im):
    cc = jnp.stack([c_re, -c_im]).reshape(2, N_SUPER, GROUPS_PER_SUPER, SSM_GROUP, SSM_STATE)
    m = jnp.einsum('rsgcp,gh->srgphc', cc, _EYE)
    return m.reshape(N_SUPER, SUPER_W, SUPER_IN)


def _c_matrix_grad(dm):
    d = dm.reshape(N_SUPER, 2, GROUPS_PER_SUPER, SSM_STATE, GROUPS_PER_SUPER, SSM_GROUP)
    d = jnp.sum(d * _EYE[None, None, :, None, :, None], axis=4)
    d = d.transpose(1, 0, 2, 4, 3).reshape(2, SSM_GROUPS, SSM_GROUP, SSM_STATE)
    return d[0], -d[1]


def _scan_rows(buf_ref, tab_ref, carry_ref, n_groups, reverse, h_ref=None, da_ref=None):
    edge = 0 if reverse else SUBLANES - 1
    for sb in range(N_SUPER):
        cr = pl.ds(sb * SUPER_W, SUPER_HALF)
        ci = pl.ds(sb * SUPER_W + SUPER_HALF, SUPER_HALF)

        def step(gi, carry, cr=cr, ci=ci):
            g = (n_groups - 1 - gi) if reverse else gi
            rows = pl.ds(pl.multiple_of(g * SUBLANES, SUBLANES), SUBLANES)
            c_re, c_im = carry[0], carry[1]
            xr = buf_ref[rows, cr]
            xi = buf_ref[rows, ci]
            for k, shift in enumerate((1, 2, 4)):
                s = (SUBLANES - shift) if reverse else shift
                sr = pltpu.roll(xr, s, 0)
                si = pltpu.roll(xi, s, 0)
                ar = tab_ref[k, :, cr]
                ai = tab_ref[k, :, ci]
                xr, xi = xr + ar * sr - ai * si, xi + ar * si + ai * sr
            pr = tab_ref[3, :, cr]
            pi_ = tab_ref[3, :, ci]
            xr, xi = xr + pr * c_re - pi_ * c_im, xi + pr * c_im + pi_ * c_re
            buf_ref[rows, cr] = xr
            buf_ref[rows, ci] = xi
            out = [jnp.broadcast_to(xr[edge:edge + 1], xr.shape), jnp.broadcast_to(xi[edge:edge + 1], xi.shape)]
            if h_ref is not None:
                last = lax.broadcasted_iota(jnp.int32, xr.shape, 0) == SUBLANES - 1
                gr = jnp.where(last, c_re, pltpu.roll(xr, SUBLANES - 1, 0))
                gim = jnp.where(last, c_im, pltpu.roll(xi, SUBLANES - 1, 0))
                hr = h_ref[rows, cr]
                hi = h_ref[rows, ci]
                out += [carry[2] + gr * hr + gim * hi, carry[3] + gim * hr - gr * hi]
            return tuple(out)

        init = [carry_ref[:, cr], carry_ref[:, ci]]
        if h_ref is not None:
            init += [da_ref[:, cr], da_ref[:, ci]]
        fin = lax.fori_loop(0, n_groups, step, tuple(init))
        carry_ref[:, cr] = fin[0]
        carry_ref[:, ci] = fin[1]
        if h_ref is not None:
            da_ref[:, cr] = fin[2]
            da_ref[:, ci] = fin[3]


def _ssm_fwd(u, bmat, cmat, tab, d_skip, tb, exchange=None):
    seq = u.shape[0]

    def body(u_ref, b_ref, c_ref, t_ref, d_ref, s_ref, h_ref, carry_ref):
        @pl.when(pl.program_id(0) == 0)
        def _():
            carry_ref[...] = jnp.zeros_like(carry_ref)

        u_blk = u_ref[...]
        ub = _bf(u_blk)
        for sb in range(N_SUPER):
            h_ref[:, sb * SUPER_W:(sb + 1) * SUPER_W] = _mm(ub[:, sb * SUPER_IN:(sb + 1) * SUPER_IN], b_ref[sb])
        _scan_rows(h_ref, t_ref, carry_ref, tb // SUBLANES, False)
        ys = [_mm(_bf(h_ref[:, sb * SUPER_W:(sb + 1) * SUPER_W]), c_ref[sb]) for sb in range(N_SUPER)]
        s_ref[...] = jnp.concatenate(ys, axis=1) + d_ref[...] * u_blk

    return _rowcall("ssm_fwd", body, seq, tb, [u], [bmat, cmat, tab, d_skip],
                    [(SSM_W, F32), (STATE_COLS, F32)], [],
                    scratch=[pltpu.VMEM((SUBLANES, STATE_COLS), F32)], vmem=VMEM_BIG, exchange=exchange)


def _ssm_bwd(ds, u, h, bmat_t, cmat_t, tab, d_skip, tb, exchange=None):
    seq = u.shape[0]

    def body(ds_ref, u_ref, h_ref, bt_ref, ct_ref, t_ref, d_ref,
             du_ref, db_ref, dc_ref, da_ref, dd_ref, g_ref, carry_ref):
        @pl.when(pl.program_id(0) == 0)
        def _():
            carry_ref[...] = jnp.zeros_like(carry_ref)
            db_ref[...] = jnp.zeros_like(db_ref)
            dc_ref[...] = jnp.zeros_like(dc_ref)
            da_ref[...] = jnp.zeros_like(da_ref)
            dd_ref[...] = jnp.zeros_like(dd_ref)

        ds_blk = ds_ref[...]
        dsb = _bf(ds_blk)
        u_blk = u_ref[...]
        ub = _bf(u_blk)
        for sb in range(N_SUPER):
            g_ref[:, sb * SUPER_W:(sb + 1) * SUPER_W] = _mm(dsb[:, sb * SUPER_IN:(sb + 1) * SUPER_IN], ct_ref[sb])
        _scan_rows(g_ref, t_ref, carry_ref, tb // SUBLANES, True, h_ref=h_ref, da_ref=da_ref)
        dus = []
        for sb in range(N_SUPER):
            gb = _bf(g_ref[:, sb * SUPER_W:(sb + 1) * SUPER_W])
            dus.append(_mm(gb, bt_ref[sb]))
            db_ref[sb] += _mm_tn(ub[:, sb * SUPER_IN:(sb + 1) * SUPER_IN], gb)
            dc_ref[sb] += _mm_tn(_bf(h_ref[:, sb * SUPER_W:(sb + 1) * SUPER_W]),
                                 dsb[:, sb * SUPER_IN:(sb + 1) * SUPER_IN])
        du_ref[...] = jnp.concatenate(dus, axis=1) + d_ref[...] * ds_blk
        dd_ref[...] += jnp.sum(ds_blk * u_blk, axis=0, keepdims=True)

    return _rowcall("ssm_bwd", body, seq, tb, [ds, u, h], [bmat_t, cmat_t, tab, d_skip],
                    [(SSM_W, F32)],
                    [((N_SUPER, SUPER_IN, SUPER_W), F32), ((N_SUPER, SUPER_W, SUPER_IN), F32),
                     ((SUBLANES, STATE_COLS), F32), ((1, SSM_W), F32)],
                    scratch=[pltpu.VMEM((tb, STATE_COLS), F32), pltpu.VMEM((SUBLANES, STATE_COLS), F32)],
                    reverse=True, vmem=VMEM_BIG, exchange=exchange)


def _merge_core(s, attb, ga, gs, wg_ref, wab_ref, wsb_ref, wout_ref):
    zg, dgelu = _gelu_and_grad(s)
    zgb = _bf(zg)
    sg = _sig(_mm(zgb, wg_ref[...]))
    z = zg * sg
    zb = _bf(z)
    ys = jnp.concatenate([_mm(zb, wsb_ref[j]) for j in range(N_CHIPS)], axis=1)
    ya = jnp.concatenate([_mm(attb, wab_ref[j]) for j in range(N_CHIPS)], axis=1)
    sa = _sig(ga)
    ss = _sig(gs)
    mgb = _bf(sa * ya + ss * ys)
    o = _mm(mgb, wout_ref[...])
    return dict(zg=zg, dgelu=dgelu, zgb=zgb, sg=sg, zb=zb, ys=ys, ya=ya, sa=sa, ss=ss, mgb=mgb, o=o)


def _merge_fwd(x, s, att, ga, gs, g2, w_glu, w_ab, w_sb, w_out, tb):
    seq = x.shape[0]

    def body(x_ref, s_ref, att_ref, ga_ref, gs_ref, g_ref, wg_ref, wab_ref, wsb_ref, wout_ref, x2_ref):
        f = _merge_core(s_ref[...], att_ref[...], ga_ref[...], gs_ref[...], wg_ref, wab_ref, wsb_ref, wout_ref)
        n, _, _ = _rms(f["o"], g_ref[...])
        x2_ref[...] = x_ref[...] + n

    return _rowcall("merge_fwd", body, seq, tb, [x, s, att, ga, gs], [g2, w_glu, w_ab, w_sb, w_out],
                    [(D_MODEL, F32)], [], vmem=VMEM_BIG)[0]


def _merge_bwd(dx2, s, att, ga, gs, g2, w_glu, w_ab, w_sb, w_out, tb, exchange=None):
    seq = s.shape[0]
    cw = D_MODEL // N_CHIPS
    last = seq // tb - 1

    def body(dx2_ref, s_ref, att_ref, ga_ref, gs_ref, g_ref, wg_ref, wab_ref, wsb_ref, wout_ref,
             ds_ref, datt_ref, dga_ref, dgs_ref, dg_ref, dwg_ref, dwab_ref, dwsb_ref, dwout_ref,
             bwg_ref, bwab_ref, bwsb_ref, bwout_ref):
        @pl.when(pl.program_id(0) == 0)
        def _():
            for r in (dg_ref, dwg_ref, dwab_ref, dwsb_ref, dwout_ref):
                r[...] = jnp.zeros_like(r)

        attb = att_ref[...]
        f = _merge_core(s_ref[...], attb, ga_ref[...], gs_ref[...], wg_ref, wab_ref, wsb_ref, wout_ref)
        g = g_ref[...]
        _, oh, r2 = _rms(f["o"], g)
        do, dg = _rms_bwd(dx2_ref[...], oh, r2, g)
        dg_ref[...] += dg
        dob = _bf(do)
        dwout_ref[...] += _mm_tn(f["mgb"], dob)
        dmg = _mm_nt(dob, wout_ref[...])
        sa, ss = f["sa"], f["ss"]
        dyab = _bf(dmg * sa)
        dysb = _bf(dmg * ss)
        dga_ref[...] = _bf(dmg * f["ya"] * sa * (1.0 - sa))
        dgs_ref[...] = _bf(dmg * f["ys"] * ss * (1.0 - ss))
        dwab = _mm_tn(attb, dyab)
        dwsb = _mm_tn(f["zb"], dysb)
        datt = jnp.zeros((tb, ATTN_W), F32)
        dz = jnp.zeros((tb, SSM_W), F32)
        for j in range(N_CHIPS):
            dwab_ref[j] += dwab[:, j * cw:(j + 1) * cw]
            dwsb_ref[j] += dwsb[:, j * cw:(j + 1) * cw]
            datt = datt + _mm_nt(dyab[:, j * cw:(j + 1) * cw], wab_ref[j])
            dz = dz + _mm_nt(dysb[:, j * cw:(j + 1) * cw], wsb_ref[j])
        datt_ref[...] = _bf(datt)
        sg, zg = f["sg"], f["zg"]
        dglb = _bf(dz * zg * sg * (1.0 - sg))
        dwg_ref[...] += _mm_tn(f["zgb"], dglb)
        dzg = dz * sg + _mm_nt(dglb, wg_ref[...])
        ds_ref[...] = dzg * f["dgelu"]

        @pl.when(pl.program_id(0) == last)
        def _():
            for dst, src in ((bwg_ref, dwg_ref), (bwab_ref, dwab_ref), (bwsb_ref, dwsb_ref), (bwout_ref, dwout_ref)):
                dst[...] = _bf(src[...])

    shapes = [w_glu.shape, w_ab.shape, w_sb.shape, w_out.shape]
    return _rowcall("merge_bwd", body, seq, tb, [dx2, s, att, ga, gs], [g2, w_glu, w_ab, w_sb, w_out],
                    [(SSM_W, F32), (ATTN_W, BF16), (D_MODEL, BF16), (D_MODEL, BF16)],
                    [((1, D_MODEL), F32)] + [(sh, F32) for sh in shapes] + [(sh, BF16) for sh in shapes],
                    vmem=VMEM_BIG, exchange=exchange)


def _mlp_fwd_loss(x2, target, g3, g4, w_ffi, w_ffo, tb):
    seq = x2.shape[0]

    def body(x2_ref, t_ref, g3_ref, g4_ref, wi_ref, wo_ref, dy_ref, df_ref, h_ref, loss_ref, dg_ref):
        @pl.when(pl.program_id(0) == 0)
        def _():
            loss_ref[...] = jnp.zeros_like(loss_ref)
            dg_ref[...] = jnp.zeros_like(dg_ref)

        x2_blk = x2_ref[...]
        h3, _, _ = _rms(x2_blk, g3_ref[...])
        hb = _bf(h3)
        h_ref[...] = hb
        f = jnp.zeros((tb, D_MODEL), F32)
        for j in range(FF_CHUNKS):
            a = _mm(hb, wi_ref[j])
            f = f + _mm(_bf(jnp.square(jnp.maximum(a, 0.0))), wo_ref[j])
        g4 = g4_ref[...]
        n4, fh, r4 = _rms(f, g4)
        e = (x2_blk + n4) - t_ref[...]
        loss_ref[...] += 0.5 * jnp.sum(jnp.mean(e * e, axis=-1, keepdims=True))
        dy = e * (1.0 / D_MODEL)
        dy_ref[...] = dy
        df, dg = _rms_bwd(dy, fh, r4, g4)
        df_ref[...] = _bf(df)
        dg_ref[...] += dg

    return _rowcall("mlp_fwd_loss", body, seq, tb, [x2, target], [g3, g4, w_ffi, w_ffo],
                    [(D_MODEL, F32), (D_MODEL, BF16), (D_MODEL, BF16)],
                    [((SUBLANES, 128), F32), ((1, D_MODEL), F32)], vmem=VMEM_BIG)


def _mlp_bwd(x2, dy, df, h3, g3, w_ffi, w_ffo, tb):
    seq = x2.shape[0]
    cw = D_FF // FF_CHUNKS

    def body(x2_ref, dy_ref, df_ref, h_ref, g3_ref, wi_ref, wo_ref, dx_ref, act_ref, da_ref, dg_ref):
        @pl.when(pl.program_id(0) == 0)
        def _():
            dg_ref[...] = jnp.zeros_like(dg_ref)

        hb = h_ref[...]
        dfb = df_ref[...]
        dh = jnp.zeros((tb, D_MODEL), F32)
        for j in range(FF_CHUNKS):
            ra = jnp.maximum(_mm(hb, wi_ref[j]), 0.0)
            act_ref[:, j * cw:(j + 1) * cw] = _bf(ra * ra)
            dab = _bf(_mm_nt(dfb, wo_ref[j]) * (2.0 * ra))
            da_ref[:, j * cw:(j + 1) * cw] = dab
            dh = dh + _mm_nt(dab, wi_ref[j])
        g3 = g3_ref[...]
        _, xh, r3 = _rms(x2_ref[...], g3)
        dxn, dg = _rms_bwd(dh, xh, r3, g3)
        dx_ref[...] = dy_ref[...] + dxn
        dg_ref[...] += dg

    return _rowcall("mlp_bwd", body, seq, tb, [x2, dy, df, h3], [g3, w_ffi, w_ffo],
                    [(D_MODEL, F32), (D_FF, BF16), (D_FF, BF16)], [((1, D_MODEL), F32)], vmem=VMEM_BIG)


def _matmul_tn(name, a, b, tk, tn, tl, chunk_major):
    seq, kdim = a.shape
    ndim = b.shape[1]
    last = seq // tl - 1

    def body(a_ref, b_ref, o_ref, ob_ref):
        @pl.when(pl.program_id(2) == 0)
        def _():
            o_ref[...] = jnp.zeros_like(o_ref)

        o_ref[...] += _mm_tn(a_ref[...], b_ref[...])

        @pl.when(pl.program_id(2) == last)
        def _():
            ob_ref[...] = _bf(o_ref[...])

    if chunk_major:
        shape = (ndim // tn, kdim, tn)
        out_spec = pl.BlockSpec((None, tk, tn), lambda k, n, l: (n, k, 0))
    else:
        shape = (kdim, ndim)
        out_spec = pl.BlockSpec((tk, tn), lambda k, n, l: (k, n))
    return pl.pallas_call(
        body, grid=(kdim // tk, ndim // tn, seq // tl),
        in_specs=[pl.BlockSpec((tl, tk), lambda k, n, l: (l, k)), pl.BlockSpec((tl, tn), lambda k, n, l: (l, n))],
        out_specs=[out_spec, out_spec], out_shape=[SDS(shape, F32), SDS(shape, BF16)], name=name,
        compiler_params=_params(3, VMEM_BIG),
    )(a, b)


def _ew_call(name, fn, ins, n_out):
    rows, cols = ins[0].shape
    tr = rows
    while tr * cols * 4 > (1 << 20) and tr % 16 == 0:
        tr //= 2
    spec = pl.BlockSpec((tr, cols), lambda i: (i, 0))

    def body(*refs):
        outs = fn(*[r[...] for r in refs[:len(ins)]])
        for r, o in zip(refs[len(ins):], outs):
            r[...] = o

    return pl.pallas_call(
        body, grid=(rows // tr,), in_specs=[spec] * len(ins), out_specs=[spec] * n_out,
        out_shape=[SDS((rows, cols), F32)] * n_out, name=name, compiler_params=_params(1),
    )(*ins)


def _adam_math(w, g, m, v):
    m2 = ADAM_B1 * m + (1.0 - ADAM_B1) * g
    v2 = ADAM_B2 * v + (1.0 - ADAM_B2) * (g * g)
    m_hat = m2 / (1.0 - ADAM_B1 ** ADAM_STEP)
    v_hat = v2 / (1.0 - ADAM_B2 ** ADAM_STEP)
    delta = -ADAM_LR * (m_hat / (jnp.sqrt(v_hat) + ADAM_EPS) + ADAM_WD * w)
    return delta, m2, v2


def _sum4(name, own, recv, idx):
    _, rows, cols = own.shape
    tr = rows
    while tr * cols * 4 > (1 << 20) and tr % 16 == 0:
        tr //= 2

    def body(idx_ref, o_ref, r0_ref, r1_ref, r2_ref, out_ref):
        out_ref[...] = ((o_ref[...] + r0_ref[...].astype(F32)) + r1_ref[...].astype(F32)) + r2_ref[...].astype(F32)

    blk = (None, tr, cols)
    grid_spec = pltpu.PrefetchScalarGridSpec(
        num_scalar_prefetch=1, grid=(rows // tr,),
        in_specs=[pl.BlockSpec(blk, lambda i, s: (s[0], i, 0)), pl.BlockSpec(blk, lambda i, s: (0, i, 0)),
                  pl.BlockSpec(blk, lambda i, s: (1, i, 0)), pl.BlockSpec(blk, lambda i, s: (2, i, 0))],
        out_specs=pl.BlockSpec((tr, cols), lambda i, s: (i, 0)))
    return pl.pallas_call(body, grid_spec=grid_spec, out_shape=SDS((rows, cols), F32), name=name,
                          compiler_params=_params(1))(jnp.reshape(idx, (1,)).astype(jnp.int32), own, recv, recv, recv)


def _adam_pair(name, w, p_own, p_sib, m, v):
    def fn(w_, a, b, m_, v_):
        g = a + b
        return (g,) + _adam_math(w_, g, m_, v_)

    return _ew_call(name, fn, [w, p_own, p_sib, m, v], 4)


def _adam_single(name, w, g, m, v):
    return _ew_call(name, lambda w_, g_, m_, v_: _adam_math(w_, g_, m_, v_), [w, g, m, v], 3)


def _place():
    return lax.axis_index("x"), lax.axis_index("y"), lax.axis_index("c")


def _other_chips(x, y):
    return [(1 - x, y), (x, 1 - y), (1 - x, 1 - y)]


def _gather_chips(shards):
    n = len(shards)

    def copies(ins, outs, sems):
        send, recv, loc = sems
        x, y, c = _place()
        me = 2 * x + y
        peers = _other_chips(x, y)
        local = [pltpu.make_async_copy(ins[a], outs[a].at[me], loc.at[a]) for a in range(n)]

        def copy(a, j, slot):
            px, py = peers[j]
            return pltpu.make_async_remote_copy(
                src_ref=ins[a], dst_ref=outs[a].at[slot], send_sem=send.at[a, j], recv_sem=recv.at[a, j],
                device_id=(px, py, c), device_id_type=MESH_ID)

        sends = [copy(a, j, me) for a in range(n) for j in range(3)]
        recvs = [copy(a, j, 2 * px + py) for a in range(n) for j, (px, py) in enumerate(peers)]
        return local, sends, recvs

    def start(ins, outs, sems):
        local, sends, _ = copies(ins, outs, sems)
        for cp in local + sends:
            cp.start()

    def wait(ins, outs, sems):
        local, sends, recvs = copies(ins, outs, sems)
        for cp in recvs:
            cp.wait_recv()
        for cp in sends:
            cp.wait_send()
        for cp in local:
            cp.wait()

    return _Exchange(shards, [SDS((N_CHIPS,) + s.shape, s.dtype) for s in shards],
                     [pltpu.SemaphoreType.DMA((n, 3)), pltpu.SemaphoreType.DMA((n, 3)), pltpu.SemaphoreType.DMA((n,))],
                     start, wait)


def _scatter_chips(chunks):
    n = len(chunks)

    def copies(ins, outs, sems):
        send, recv = sems
        x, y, c = _place()
        return [pltpu.make_async_remote_copy(
            src_ref=ins[a].at[2 * px + py], dst_ref=outs[a].at[j], send_sem=send.at[a, j],
            recv_sem=recv.at[a, j], device_id=(px, py, c), device_id_type=MESH_ID)
            for a in range(n) for j, (px, py) in enumerate(_other_chips(x, y))]

    def start(ins, outs, sems):
        for cp in copies(ins, outs, sems):
            cp.start()

    def wait(ins, outs, sems):
        cps = copies(ins, outs, sems)
        for cp in cps:
            cp.wait_recv()
        for cp in cps:
            cp.wait_send()

    return _Exchange(chunks, [SDS((3,) + s.shape[1:], s.dtype) for s in chunks],
                     [pltpu.SemaphoreType.DMA((n, 3)), pltpu.SemaphoreType.DMA((n, 3))], start, wait)


def _swap_sibling(arrs):
    n = len(arrs)

    def copies(ins, outs, sems):
        send, recv = sems
        x, y, c = _place()
        return [pltpu.make_async_remote_copy(
            src_ref=ins[a], dst_ref=outs[a], send_sem=send.at[a], recv_sem=recv.at[a],
            device_id=(x, y, 1 - c), device_id_type=MESH_ID) for a in range(n)]

    def start(ins, outs, sems):
        for cp in copies(ins, outs, sems):
            cp.start()

    def wait(ins, outs, sems):
        cps = copies(ins, outs, sems)
        for cp in cps:
            cp.wait_recv()
        for cp in cps:
            cp.wait_send()

    return _Exchange(arrs, [SDS(s.shape, s.dtype) for s in arrs],
                     [pltpu.SemaphoreType.DMA((n,)), pltpu.SemaphoreType.DMA((n,))], start, wait)


def _allreduce_small(packed):
    rows, cols = packed.shape
    n_dev = 8

    def body(in_ref, out_ref, slots, send, recv):
        x, y, c = _place()
        me = 4 * x + 2 * y + c
        slots[me] = in_ref[...]
        cps = []
        for k in range(1, n_dev):
            peer = (x ^ (k >> 2), y ^ ((k >> 1) & 1), c ^ (k & 1))
            cps.append(pltpu.make_async_remote_copy(
                src_ref=in_ref, dst_ref=slots.at[me], send_sem=send.at[k - 1], recv_sem=recv.at[k - 1],
                device_id=peer, device_id_type=MESH_ID))
        for cp in cps:
            cp.start()
        for k in range(1, n_dev):
            pltpu.make_async_remote_copy(
                src_ref=in_ref, dst_ref=slots.at[me ^ k], send_sem=send.at[k - 1], recv_sem=recv.at[k - 1],
                device_id=(x, y, c), device_id_type=MESH_ID).wait_recv()
        acc = slots[0]
        for d in range(1, n_dev):
            acc = acc + slots[d]
        out_ref[...] = acc
        for cp in cps:
            cp.wait_send()

    return pl.pallas_call(
        body, in_specs=[pl.BlockSpec(memory_space=pltpu.VMEM)], out_specs=pl.BlockSpec(memory_space=pltpu.VMEM),
        out_shape=SDS((rows, cols), F32),
        scratch_shapes=[pltpu.VMEM((n_dev, rows, cols), F32), pltpu.SemaphoreType.DMA((n_dev - 1,)),
                        pltpu.SemaphoreType.DMA((n_dev - 1,))],
        name="allreduce_small", compiler_params=pltpu.CompilerParams(vmem_limit_bytes=32 * 1024 * 1024),
    )(packed)


def _heads(t, n):
    return t.reshape(t.shape[0], n, HEAD_DIM).transpose(1, 0, 2)


def _unheads(t):
    return t.transpose(1, 0, 2).reshape(t.shape[1], t.shape[0] * HEAD_DIM)


def _local_step(x, target, small, big, tb, distributed):
    g1, g2, g3, g4 = small["norm_mix_pre"], small["norm_mix_post"], small["norm_mlp_pre"], small["norm_mlp_post"]
    dist = distributed
    me = (2 * lax.axis_index("x") + lax.axis_index("y")) if dist else 0
    bucket = jnp.asarray(_bucket_table())

    bias = _bias_table(small["rel_bias"], bucket).reshape(N_KV, Q_GROUP * BLOCK, 2 * BLOCK)
    sink_rows = jnp.repeat(small["sinks"].reshape(N_KV, Q_GROUP), BLOCK, axis=1)[..., None]
    disc_args = (small["lam_re"], small["lam_im"], small["log_dt"], small["b_re"], small["b_im"])
    (ab_re, ab_im, bb_re, bb_im), disc_vjp = jax.vjp(_ssm_discretize, *disc_args)
    tab_f, tab_b = _scan_tables(ab_re, ab_im)
    bmat = _bf(_b_matrix(bb_re, bb_im))
    cmat = _bf(_c_matrix(small["c_re"], small["c_im"]))
    d_skip = small["d_skip"]

    if dist:
        (g_in,) = _exchange_alone("gather_w_in", _gather_chips([big["w_in"]]))
        w_in = g_in.transpose(1, 0, 2).reshape(D_MODEL, IN_W)
    else:
        w_in = big["w_in"]
    mix = ("w_glu", "w_attn_branch", "w_ssm_branch", "w_out")
    outs = _inproj_fwd(x, g1, w_in, tb, _gather_chips([big[n] for n in mix]) if dist else None)
    h1, q, k, v, u, ga, gs = outs[:7]
    w_glu, w_ab, w_sb, w_out = outs[7:] if dist else [big[n] for n in mix]
    w_glu = w_glu.reshape(SSM_W, SSM_W)
    w_out = w_out.reshape(D_MODEL, D_MODEL)
    qh, kh, vh = _heads(q, N_HEADS), _heads(k, N_KV), _heads(v, N_KV)
    outs = _attn_fwd(qh, kh, vh, bias, sink_rows, _gather_chips([big["w_ff_in"]]) if dist else None)
    att = _unheads(outs[0])
    w_ffi = outs[1] if dist else big["w_ff_in"]
    outs = _ssm_fwd(u, bmat, cmat, tab_f, d_skip, tb, _gather_chips([big["w_ff_out"]]) if dist else None)
    s, h = outs[:2]
    w_ffo = outs[2] if dist else big["w_ff_out"]
    x2 = _merge_fwd(x, s, att, ga, gs, g2, w_glu, w_ab, w_sb, w_out, tb)
    dy, df, h3, loss_acc, dg4 = _mlp_fwd_loss(x2, target, g3, g4, w_ffi, w_ffo, tb)

    dx2, act, da, dg3 = _mlp_bwd(x2, dy, df, h3, g3, w_ffi, w_ffo, tb)
    tl = min(512, x.shape[0])
    chunked = (N_CHIPS, D_FF // N_CHIPS, D_MODEL)
    d_ffi, b_ffi = _matmul_tn("grad_w_ff_in", h3, da, D_MODEL, D_FF // FF_CHUNKS, tl, True)
    d_ffo, b_ffo = _matmul_tn("grad_w_ff_out", act, df, D_FF // FF_CHUNKS, D_MODEL, tl, False)
    d_ffo, b_ffo = d_ffo.reshape(chunked), b_ffo.reshape(chunked)
    outs = _merge_bwd(dx2, s, att, ga, gs, g2, w_glu, w_ab, w_sb, w_out, tb, _scatter_chips([b_ffi]) if dist else None)
    ds, datt, dga, dgs, dg2, d_glu, d_ab, d_sb, d_out, b_glu, b_ab, b_sb, b_out = outs[:13]
    r_ffi = outs[13:]
    glu4, out4 = (N_CHIPS, SSM_W // N_CHIPS, SSM_W), (N_CHIPS, D_MODEL // N_CHIPS, D_MODEL)
    d_mix = [d_glu.reshape(glu4), d_ab, d_sb, d_out.reshape(out4)]
    b_mix = [b_glu.reshape(glu4), b_ab, b_sb, b_out.reshape(out4)]
    outs = _ssm_bwd(ds, u, h, bmat.transpose(0, 2, 1), cmat.transpose(0, 2, 1), tab_b, d_skip, tb,
                    _scatter_chips([b_ffo]) if dist else None)
    du, d_bmat, d_cmat, da_acc, dd_skip = outs[:5]
    r_ffo = outs[5:]
    outs = _attn_bwd(qh, kh, vh, _heads(datt, N_HEADS), bias, sink_rows, _scatter_chips(b_mix) if dist else None)
    dqh, dkh, dvh, dbias, dsink_rows = outs[:5]
    r_mix = outs[5:]
    if dist:
        p_ffi = _sum4("sum_w_ff_in", d_ffi, r_ffi[0], me)
        p_ffo = _sum4("sum_w_ff_out", d_ffo, r_ffo[0], me)
    outs = _inproj_bwd(x, dx2, _unheads(dqh), _unheads(dkh), _unheads(dvh), du, dga, dgs, g1, w_in, tb,
                       _swap_sibling([p_ffi, p_ffo]) if dist else None)
    dx, dpj, dg1 = outs[:3]
    d_in, b_in = _matmul_tn("grad_w_in", h1, dpj, D_MODEL, IN_W // 2, tl, True)

    dab_re, dab_im = _state_unlayout(jnp.sum(da_acc, axis=0))
    dbb_re, dbb_im = _b_matrix_grad(d_bmat)
    d_lam_re, d_lam_im, d_log_dt, d_b_re, d_b_im = disc_vjp((dab_re, dab_im, dbb_re, dbb_im))
    d_c_re, d_c_im = _c_matrix_grad(d_cmat)
    d_rel = _bias_grad(dbias.reshape(N_HEADS, BLOCK, 2 * BLOCK), bucket)
    d_sinks = jnp.sum(dsink_rows.reshape(N_HEADS, BLOCK), axis=1)
    small_grads = dict(
        norm_mix_pre=dg1, norm_mix_post=dg2, norm_mlp_pre=dg3, norm_mlp_post=dg4, rel_bias=d_rel, sinks=d_sinks,
        lam_re=d_lam_re, lam_im=d_lam_im, log_dt=d_log_dt, b_re=d_b_re, b_im=d_b_im, c_re=d_c_re, c_im=d_c_im,
        d_skip=dd_skip)
    quarter = IN_W // N_CHIPS

    def in_chunks(t):
        return jnp.stack([t[0][:, :quarter], t[0][:, quarter:], t[1][:, :quarter], t[1][:, quarter:]])

    if not dist:
        return loss_acc, dx, small_grads, dict(zip(BIG, [in_chunks(d_in)] + d_mix + [d_ffi, d_ffo]))
    (r_in,) = _exchange_alone("scatter_w_in", _scatter_chips([in_chunks(b_in)]))
    own_in = lax.dynamic_slice(d_in, (me // 2, 0, (me % 2) * quarter), (1, D_MODEL, quarter))
    parts = [_sum4("sum_w_in", own_in, r_in, 0)]
    parts += [_sum4("sum_" + n, d, r, me) for n, d, r in zip(mix, d_mix, r_mix)]
    sibs = _exchange_alone("swap_rest", _swap_sibling(parts))
    parts += [p_ffi, p_ffo]
    sibs = list(sibs) + list(outs[3:])
    return loss_acc, dx, small_grads, dict(zip(BIG, zip(parts, sibs)))


SMALL = ['norm_mix_pre', 'norm_mix_post', 'norm_mlp_pre', 'norm_mlp_post', 'rel_bias', 'sinks', 'lam_re', 'lam_im',
         'log_dt', 'b_re', 'b_im', 'c_re', 'c_im', 'd_skip']
BIG = ['w_in', 'w_glu', 'w_attn_branch', 'w_ssm_branch', 'w_out', 'w_ff_in', 'w_ff_out']
WEIGHTS = ['norm_mix_pre', 'norm_mix_post', 'norm_mlp_pre', 'norm_mlp_post', 'w_in', 'rel_bias', 'sinks', 'lam_re',
           'lam_im', 'log_dt', 'b_re', 'b_im', 'c_re', 'c_im', 'd_skip', 'w_glu', 'w_attn_branch', 'w_ssm_branch',
           'w_out', 'w_ff_in', 'w_ff_out']
PACK_COLS = 1024


def _pack(arrs):
    flat = jnp.concatenate([a.reshape(-1) for a in arrs])
    rows = -(-flat.shape[0] // (PACK_COLS * SUBLANES)) * SUBLANES
    return jnp.pad(flat, (0, rows * PACK_COLS - flat.shape[0])).reshape(rows, PACK_COLS)


def _unpack(packed, shapes):
    flat = packed.reshape(-1)
    out, at = [], 0
    for s in shapes:
        n = int(np.prod(s))
        out.append(flat[at:at + n].reshape(s))
        at += n
    return out


def kernel(x, norm_mix_pre, norm_mix_post, norm_mlp_pre, norm_mlp_post, w_in, rel_bias, sinks, lam_re, lam_im, log_dt, b_re, b_im, c_re, c_im, d_skip, w_glu, w_attn_branch, w_ssm_branch, w_out, w_ff_in, w_ff_out, loss_target, m_norm_mix_pre, m_norm_mix_post, m_norm_mlp_pre, m_norm_mlp_post, m_w_in, m_rel_bias, m_sinks, m_lam_re, m_lam_im, m_log_dt, m_b_re, m_b_im, m_c_re, m_c_im, m_d_skip, m_w_glu, m_w_attn_branch, m_w_ssm_branch, m_w_out, m_w_ff_in, m_w_ff_out, v_norm_mix_pre, v_norm_mix_post, v_norm_mlp_pre, v_norm_mlp_post, v_w_in, v_rel_bias, v_sinks, v_lam_re, v_lam_im, v_log_dt, v_b_re, v_b_im, v_c_re, v_c_im, v_d_skip, v_w_glu, v_w_attn_branch, v_w_ssm_branch, v_w_out, v_w_ff_in, v_w_ff_out):
    env = dict(locals())
    w = {n: env[n] for n in WEIGHTS}
    m = {n: env["m_" + n] for n in WEIGHTS}
    v = {n: env["v_" + n] for n in WEIGHTS}
    seq = x.shape[1]
    tb = min(256, seq)

    small = {n: w[n] for n in ('norm_mix_pre', 'norm_mix_post', 'norm_mlp_pre', 'norm_mlp_post', 'rel_bias')}
    small.update({n: w[n][0] for n in ('sinks', 'lam_re', 'lam_im', 'log_dt', 'b_re', 'b_im', 'c_re', 'c_im')})
    small['d_skip'] = w['d_skip']
    loss_acc, dx, small_g, big_g = _local_step(
        x[0], loss_target[0], small, {n: _bf(w[n][0]) for n in BIG}, tb, True)

    loss = lax.psum(loss_acc[0, 0], ("x", "y", "c"))

    grads, deltas, new_m, new_v = {}, {}, {}, {}
    for n in BIG:
        p_own, p_sib = big_g[n]
        g, d, m2, v2 = _adam_pair("adam_" + n, w[n][0], p_own, p_sib, m[n][0], v[n][0])
        grads[n], deltas[n], new_m[n], new_v[n] = g[None], d[None], m2[None], v2[None]

    shapes = [w[n].shape for n in SMALL]
    g_small = _allreduce_small(_pack([small_g[n] for n in SMALL]))
    d_small, m_small, v_small = _adam_single(
        "adam_small", _pack([w[n] for n in SMALL]), g_small, _pack([m[n] for n in SMALL]),
        _pack([v[n] for n in SMALL]))
    for dst, src in ((grads, g_small), (deltas, d_small), (new_m, m_small), (new_v, v_small)):
        dst.update(dict(zip(SMALL, _unpack(src, shapes))))

    return (loss, dx[None], *[grads[n] for n in WEIGHTS], *[deltas[n] for n in WEIGHTS],
            *[new_m[n] for n in WEIGHTS], *[new_v[n] for n in WEIGHTS])
```

```python
import functools
import math

import numpy as np
import jax
import jax.numpy as jnp
from jax import lax
from jax.experimental import pallas as pl
from jax.experimental.pallas import tpu as pltpu

F32 = jnp.float32
BF16 = jnp.bfloat16

D_MODEL = 1024
N_HEADS = 8
N_KV = 2
Q_GROUP = 4
HEAD_DIM = 64
ATTN_W = 512
KV_W = 128
BLOCK = 128
N_BUCKETS = 32
MAX_DISTANCE = 128
NEG_INF = -1e30
SSM_W = 512
SSM_GROUP = 16
SSM_GROUPS = 32
SSM_STATE = 64
N_SUPER = 4
GROUPS_PER_SUPER = SSM_GROUPS // N_SUPER
SUPER_IN = GROUPS_PER_SUPER * SSM_GROUP
SUPER_HALF = GROUPS_PER_SUPER * SSM_STATE
SUPER_W = 2 * SUPER_HALF
STATE_COLS = N_SUPER * SUPER_W
D_FF = 4096
FF_CHUNKS = 4
IN_W = 3328
SPLITS = (0, 512, 640, 768, 1280, 2304, 3328)
RMS_EPS = 1e-6
N_CHIPS = 4
SUBLANES = 8

ADAM_LR = 0.001
ADAM_B1 = 0.9
ADAM_B2 = 0.999
ADAM_EPS = 1e-08
ADAM_WD = 0.01
ADAM_STEP = 10

VMEM_BIG = 56 * 1024 * 1024
SDS = jax.ShapeDtypeStruct
MESH_ID = pl.DeviceIdType.MESH
ANY = pl.BlockSpec(memory_space=pl.ANY)


def _bf(x):
    return x.astype(BF16)


def _mm(a, b):
    return jnp.dot(a, b, preferred_element_type=F32)


def _mm_nt(a, b):
    return lax.dot_general(a, b, (((1,), (1,)), ((), ())), preferred_element_type=F32)


def _mm_tn(a, b):
    return lax.dot_general(a, b, (((0,), (0,)), ((), ())), preferred_element_type=F32)


def _sig(x):
    return 1.0 / (1.0 + jnp.exp(-x))


def _rms(x, g):
    r = lax.rsqrt(jnp.mean(x * x, axis=-1, keepdims=True) + RMS_EPS)
    xh = x * r
    return xh * g, xh, r


def _rms_bwd(dout, xh, r, g):
    dg = jnp.sum(dout * xh, axis=0, keepdims=True)
    dxh = dout * g
    dx = r * (dxh - xh * jnp.mean(dxh * xh, axis=-1, keepdims=True))
    return dx, dg


_GELU_C = math.sqrt(2.0 / math.pi)


def _gelu_and_grad(x):
    x2 = x * x
    inner = _GELU_C * (x + 0.044715 * (x2 * x))
    t = jnp.tanh(inner)
    y = 0.5 * x * (1.0 + t)
    dy = 0.5 * (1.0 + t) + 0.5 * x * (1.0 - t * t) * (_GELU_C * (1.0 + 3.0 * 0.044715 * x2))
    return y, dy


def _zero_map(nd, *_):
    return (0,) * nd


def _params(n_axes, vmem=None):
    return pltpu.CompilerParams(dimension_semantics=("arbitrary",) * n_axes, vmem_limit_bytes=vmem)


class _Exchange:
    def __init__(self, ins, outs, sems, start, wait):
        self.ins, self.outs, self.sems, self.start, self.wait = list(ins), list(outs), list(sems), start, wait


def _fused_call(name, body, grid, in_specs, out_specs, out_shape, scratch, args, exchange, params):
    n_in, n_out, n_scr = len(in_specs), len(out_specs), len(scratch)
    if exchange is None:
        fn = body
    else:
        ex = exchange
        n_xi, n_xo = len(ex.ins), len(ex.outs)

        def fn(*refs):
            at = 0
            parts = []
            for n in (n_in, n_xi, n_out, n_xo, n_scr, len(ex.sems)):
                parts.append(refs[at:at + n])
                at += n
            ins, x_in, outs, x_out, scr, x_sem = parts
            ids = [pl.program_id(a) for a in range(len(grid))]
            first = functools.reduce(jnp.logical_and, [i == 0 for i in ids])
            last = functools.reduce(jnp.logical_and, [i == g - 1 for i, g in zip(ids, grid)])

            @pl.when(first)
            def _():
                ex.start(x_in, x_out, x_sem)

            body(*ins, *outs, *scr)

            @pl.when(last)
            def _():
                ex.wait(x_in, x_out, x_sem)

        in_specs = list(in_specs) + [ANY] * n_xi
        out_specs = list(out_specs) + [ANY] * n_xo
        out_shape = list(out_shape) + ex.outs
        scratch = list(scratch) + ex.sems
        args = list(args) + ex.ins
    return pl.pallas_call(fn, grid=grid, in_specs=in_specs, out_specs=out_specs, out_shape=out_shape,
                          scratch_shapes=list(scratch), name=name, compiler_params=params)(*args)


def _exchange_alone(name, ex):
    def body(*refs):
        n_xi, n_xo = len(ex.ins), len(ex.outs)
        x_in, x_out, x_sem = refs[:n_xi], refs[n_xi:n_xi + n_xo], refs[n_xi + n_xo:]
        ex.start(x_in, x_out, x_sem)
        ex.wait(x_in, x_out, x_sem)

    return pl.pallas_call(body, in_specs=[ANY] * len(ex.ins), out_specs=[ANY] * len(ex.outs), out_shape=ex.outs,
                          scratch_shapes=ex.sems, name=name)(*ex.ins)


def _rowcall(name, body, seq, tb, rows, consts, row_outs, acc_outs, scratch=(), reverse=False, vmem=None,
             exchange=None):
    nb = seq // tb
    rmap = (lambda i: (nb - 1 - i, 0)) if reverse else (lambda i: (i, 0))
    in_specs = [pl.BlockSpec((tb, a.shape[1]), rmap) for a in rows]
    in_specs += [pl.BlockSpec(a.shape, functools.partial(_zero_map, a.ndim)) for a in consts]
    out_specs = [pl.BlockSpec((tb, c), rmap) for c, _ in row_outs]
    out_specs += [pl.BlockSpec(s, functools.partial(_zero_map, len(s))) for s, _ in acc_outs]
    out_shape = [SDS((seq, c), dt) for c, dt in row_outs] + [SDS(s, dt) for s, dt in acc_outs]
    return _fused_call(name, body, (nb,), in_specs, out_specs, out_shape, list(scratch), [*rows, *consts],
                       exchange, _params(1, vmem))


def _inproj_fwd(x, g1, w_in, tb, exchange=None):
    seq = x.shape[0]

    def body(x_ref, g_ref, w_ref, h_ref, q_ref, k_ref, v_ref, u_ref, ga_ref, gs_ref):
        h, _, _ = _rms(x_ref[...], g_ref[...])
        hb = _bf(h)
        h_ref[...] = hb
        pj = _mm(hb, w_ref[...])
        q_ref[...] = _bf(pj[:, SPLITS[0]:SPLITS[1]])
        k_ref[...] = _bf(pj[:, SPLITS[1]:SPLITS[2]])
        v_ref[...] = _bf(pj[:, SPLITS[2]:SPLITS[3]])
        u_ref[...] = pj[:, SPLITS[3]:SPLITS[4]]
        ga_ref[...] = pj[:, SPLITS[4]:SPLITS[5]]
        gs_ref[...] = pj[:, SPLITS[5]:SPLITS[6]]

    return _rowcall("inproj_fwd", body, seq, tb, [x], [g1, w_in],
                    [(D_MODEL, BF16), (ATTN_W, BF16), (KV_W, BF16), (KV_W, BF16), (SSM_W, F32),
                     (D_MODEL, F32), (D_MODEL, F32)], [], vmem=VMEM_BIG, exchange=exchange)


def _inproj_bwd(x, dx2, dq, dk, dv, du, dga, dgs, g1, w_in, tb, exchange=None):
    seq = x.shape[0]

    def body(x_ref, dx2_ref, dq_ref, dk_ref, dv_ref, du_ref, dga_ref, dgs_ref, g_ref, w_ref,
             dx_ref, dpj_ref, dg_ref):
        @pl.when(pl.program_id(0) == 0)
        def _():
            dg_ref[...] = jnp.zeros_like(dg_ref)

        dpj = jnp.concatenate([dq_ref[...], dk_ref[...], dv_ref[...], _bf(du_ref[...]),
                               dga_ref[...], dgs_ref[...]], axis=1)
        dpj_ref[...] = dpj
        dh = _mm_nt(dpj, w_ref[...])
        g = g_ref[...]
        _, xh, r = _rms(x_ref[...], g)
        dxn, dg = _rms_bwd(dh, xh, r, g)
        dx_ref[...] = dx2_ref[...] + dxn
        dg_ref[...] += dg

    return _rowcall("inproj_bwd", body, seq, tb, [x, dx2, dq, dk, dv, du, dga, dgs], [g1, w_in],
                    [(D_MODEL, F32), (IN_W, BF16)], [((1, D_MODEL), F32)], vmem=VMEM_BIG, exchange=exchange)


def _bucket_table():
    qi = np.arange(BLOCK)[:, None]
    kj = np.arange(2 * BLOCK)[None, :]
    dist = qi + BLOCK - kj
    max_exact = N_BUCKETS // 2
    d = np.maximum(dist, 0)
    df = np.maximum(d, 1).astype(np.float32)
    large = max_exact + (np.log(df / np.float32(max_exact)) / np.float32(math.log(MAX_DISTANCE / max_exact))
                         * np.float32(N_BUCKETS - max_exact)).astype(np.int32)
    large = np.minimum(large, N_BUCKETS - 1)
    bucket = np.where(d < max_exact, d, large)
    valid = (dist >= 0) & (dist < BLOCK)
    return np.where(valid, bucket, -1).astype(np.int32)


def _bias_table(rel_bias, bucket):
    def body(rb_ref, bk_ref, o_ref):
        bk = bk_ref[...]
        for h in range(N_HEADS):
            acc = jnp.zeros((BLOCK, 2 * BLOCK), F32)
            for b in range(N_BUCKETS):
                acc = jnp.where(bk == b, rb_ref[b, h], acc)
            o_ref[h] = acc

    return pl.pallas_call(
        body, out_shape=SDS((N_HEADS, BLOCK, 2 * BLOCK), F32),
        in_specs=[pl.BlockSpec(memory_space=pltpu.SMEM), pl.BlockSpec(memory_space=pltpu.VMEM)],
        out_specs=pl.BlockSpec(memory_space=pltpu.VMEM), name="bias_table",
    )(rel_bias, bucket)


def _bias_grad(dbias, bucket):
    def body(db_ref, bk_ref, o_ref):
        bk = bk_ref[...]
        for h in range(N_HEADS):
            db = db_ref[h]
            for b in range(N_BUCKETS):
                o_ref[b, h] = jnp.sum(jnp.where(bk == b, db, 0.0))

    return pl.pallas_call(
        body, out_shape=SDS((N_BUCKETS, N_HEADS), F32),
        in_specs=[pl.BlockSpec(memory_space=pltpu.VMEM), pl.BlockSpec(memory_space=pltpu.VMEM)],
        out_specs=pl.BlockSpec(memory_space=pltpu.SMEM), name="bias_grad",
    )(dbias, bucket)


def _band_mask(n):
    qi = lax.broadcasted_iota(jnp.int32, (Q_GROUP * BLOCK, 2 * BLOCK), 0) % BLOCK
    kj = lax.broadcasted_iota(jnp.int32, (Q_GROUP * BLOCK, 2 * BLOCK), 1)
    dist = qi + BLOCK - kj
    return (dist >= 0) & (dist < BLOCK) & ((kj >= BLOCK) | (n > 0))


def _attn_probs(qs, kk, bias, sink, ok):
    lg = _mm_nt(qs, kk) * (HEAD_DIM ** -0.5) + bias
    lg = jnp.where(ok, lg, NEG_INF)
    m = jnp.maximum(jnp.max(lg, axis=-1, keepdims=True), sink)
    p = jnp.exp(lg - m)
    es = jnp.exp(sink - m)
    den = jnp.sum(p, axis=-1, keepdims=True) + es
    return p / den, es / den


def _attn_fwd(q, k, v, bias, sink_rows, exchange=None):
    seq = q.shape[1]
    nblk = seq // BLOCK

    def body(q_ref, kp_ref, kc_ref, vp_ref, vc_ref, b_ref, s_ref, o_ref):
        ok = _band_mask(pl.program_id(0))
        for kh in range(N_KV):
            qs = q_ref[kh * Q_GROUP:(kh + 1) * Q_GROUP].reshape(Q_GROUP * BLOCK, HEAD_DIM)
            kk = jnp.concatenate([kp_ref[kh], kc_ref[kh]], axis=0)
            vv = jnp.concatenate([vp_ref[kh], vc_ref[kh]], axis=0)
            pr, _ = _attn_probs(qs, kk, b_ref[kh], s_ref[kh], ok)
            o = _mm(_bf(pr), vv)
            o_ref[kh * Q_GROUP:(kh + 1) * Q_GROUP] = _bf(o).reshape(Q_GROUP, BLOCK, HEAD_DIM)

    cur = lambda n: (0, n, 0)
    prev = lambda n: (0, jnp.maximum(n - 1, 0), 0)
    return _fused_call(
        "attn_fwd", body, (nblk,),
        [pl.BlockSpec((N_HEADS, BLOCK, HEAD_DIM), cur),
         pl.BlockSpec((N_KV, BLOCK, HEAD_DIM), prev), pl.BlockSpec((N_KV, BLOCK, HEAD_DIM), cur),
         pl.BlockSpec((N_KV, BLOCK, HEAD_DIM), prev), pl.BlockSpec((N_KV, BLOCK, HEAD_DIM), cur),
         pl.BlockSpec(bias.shape, functools.partial(_zero_map, 3)),
         pl.BlockSpec(sink_rows.shape, functools.partial(_zero_map, 3))],
        [pl.BlockSpec((N_HEADS, BLOCK, HEAD_DIM), cur)], [SDS((N_HEADS, seq, HEAD_DIM), BF16)], [],
        [q, k, k, v, v, bias, sink_rows], exchange, _params(1))


def _attn_bwd(q, k, v, d_out, bias, sink_rows, exchange=None):
    seq = q.shape[1]
    nblk = seq // BLOCK

    def body(q_ref, kp_ref, kc_ref, vp_ref, vc_ref, do_ref, b_ref, s_ref,
             dq_ref, dk_ref, dv_ref, db_ref, ds_ref, ck_ref, cv_ref):
        n = pl.program_id(0)

        @pl.when(n == 0)
        def _():
            db_ref[...] = jnp.zeros_like(db_ref)
            ds_ref[...] = jnp.zeros_like(ds_ref)
            ck_ref[...] = jnp.zeros_like(ck_ref)
            cv_ref[...] = jnp.zeros_like(cv_ref)

        @pl.when(n < nblk)
        def _():
            ok = _band_mask(n)
            scale = HEAD_DIM ** -0.5
            for kh in range(N_KV):
                qs = q_ref[kh * Q_GROUP:(kh + 1) * Q_GROUP].reshape(Q_GROUP * BLOCK, HEAD_DIM)
                dos = do_ref[kh * Q_GROUP:(kh + 1) * Q_GROUP].reshape(Q_GROUP * BLOCK, HEAD_DIM)
                kk = jnp.concatenate([kp_ref[kh], kc_ref[kh]], axis=0)
                vv = jnp.concatenate([vp_ref[kh], vc_ref[kh]], axis=0)
                pr, ps = _attn_probs(qs, kk, b_ref[kh], s_ref[kh], ok)
                dp = _mm_nt(dos, vv)
                rs = jnp.sum(pr * dp, axis=-1, keepdims=True)
                dlg = pr * (dp - rs)
                ds_ref[kh] += -ps * rs
                db_ref[kh] += dlg
                dlb = _bf(dlg)
                dq = _mm(dlb, kk) * scale
                dq_ref[kh * Q_GROUP:(kh + 1) * Q_GROUP] = _bf(dq).reshape(Q_GROUP, BLOCK, HEAD_DIM)
                dkk = _mm_tn(dlb, qs) * scale
                dvv = _mm_tn(_bf(pr), dos)
                dk_ref[kh] = _bf(ck_ref[kh] + dkk[:BLOCK])
                ck_ref[kh] = dkk[BLOCK:]
                dv_ref[kh] = _bf(cv_ref[kh] + dvv[:BLOCK])
                cv_ref[kh] = dvv[BLOCK:]

        @pl.when(n == nblk)
        def _():
            dk_ref[...] = _bf(ck_ref[...])
            dv_ref[...] = _bf(cv_ref[...])

    cur = lambda n: (0, jnp.minimum(n, nblk - 1), 0)
    prev = lambda n: (0, jnp.maximum(jnp.minimum(n, nblk - 1) - 1, 0), 0)
    late = lambda n: (0, jnp.maximum(n - 1, 0), 0)
    kv_spec = lambda m: pl.BlockSpec((N_KV, BLOCK, HEAD_DIM), m)
    return _fused_call(
        "attn_bwd", body, (nblk + 1,),
        [pl.BlockSpec((N_HEADS, BLOCK, HEAD_DIM), cur), kv_spec(prev), kv_spec(cur), kv_spec(prev),
         kv_spec(cur), pl.BlockSpec((N_HEADS, BLOCK, HEAD_DIM), cur),
         pl.BlockSpec(bias.shape, functools.partial(_zero_map, 3)),
         pl.BlockSpec(sink_rows.shape, functools.partial(_zero_map, 3))],
        [pl.BlockSpec((N_HEADS, BLOCK, HEAD_DIM), cur), kv_spec(late), kv_spec(late),
         pl.BlockSpec(bias.shape, functools.partial(_zero_map, 3)),
         pl.BlockSpec(sink_rows.shape, functools.partial(_zero_map, 3))],
        [SDS((N_HEADS, seq, HEAD_DIM), BF16), SDS((N_KV, seq, HEAD_DIM), BF16),
         SDS((N_KV, seq, HEAD_DIM), BF16), SDS(bias.shape, F32), SDS(sink_rows.shape, F32)],
        [pltpu.VMEM((N_KV, BLOCK, HEAD_DIM), F32), pltpu.VMEM((N_KV, BLOCK, HEAD_DIM), F32)],
        [q, k, k, v, v, d_out, bias, sink_rows], exchange, _params(1))


def _ssm_discretize(lam_re, lam_im, log_dt, b_re, b_im):
    dt = jnp.exp(log_dt)[:, None]
    mag = jnp.exp(lam_re * dt)
    ab_re = mag * jnp.cos(lam_im * dt)
    ab_im = mag * jnp.sin(lam_im * dt)
    nr = ab_re - 1.0
    den = lam_re * lam_re + lam_im * lam_im
    f_re = (nr * lam_re + ab_im * lam_im) / den
    f_im = (ab_im * lam_re - nr * lam_im) / den
    bb_re = f_re[..., None] * b_re - f_im[..., None] * b_im
    bb_im = f_re[..., None] * b_im + f_im[..., None] * b_re
    return ab_re, ab_im, bb_re, bb_im


def _state_layout(re, im):
    z = jnp.stack([re, im]).reshape(2, N_SUPER, GROUPS_PER_SUPER, SSM_STATE)
    return z.transpose(1, 0, 2, 3).reshape(STATE_COLS)


def _state_unlayout(vec):
    z = vec.reshape(N_SUPER, 2, GROUPS_PER_SUPER, SSM_STATE).transpose(1, 0, 2, 3)
    z = z.reshape(2, SSM_GROUPS, SSM_STATE)
    return z[0], z[1]


def _scan_tables(ab_re, ab_im):
    pw = [None, (ab_re, ab_im)]
    for _ in range(2, SUBLANES + 1):
        pr, pi_ = pw[-1]
        pw.append((pr * ab_re - pi_ * ab_im, pr * ab_im + pi_ * ab_re))
    rows = np.arange(SUBLANES)[:, None]
    fwd, bwd = [], []
    for shift in (1, 2, 4):
        fwd.append(_state_layout(*pw[shift])[None, :] * (rows >= shift).astype(np.float32))
        bwd.append(_state_layout(pw[shift][0], -pw[shift][1])[None, :] * (rows < SUBLANES - shift).astype(np.float32))
    fwd.append(jnp.stack([_state_layout(*pw[r + 1]) for r in range(SUBLANES)]))
    bwd.append(jnp.stack([_state_layout(pw[SUBLANES - r][0], -pw[SUBLANES - r][1]) for r in range(SUBLANES)]))
    return jnp.stack(fwd), jnp.stack(bwd)


_EYE = np.eye(GROUPS_PER_SUPER, dtype=np.float32)


def _b_matrix(bb_re, bb_im):
    bb = jnp.stack([bb_re, bb_im]).reshape(2, N_SUPER, GROUPS_PER_SUPER, SSM_STATE, SSM_GROUP)
    m = jnp.einsum('rsgpc,gh->sgcrhp', bb, _EYE)
    return m.reshape(N_SUPER, SUPER_IN, SUPER_W)


def _b_matrix_grad(dm):
    d = dm.reshape(N_SUPER, GROUPS_PER_SUPER, SSM_GROUP, 2, GROUPS_PER_SUPER, SSM_STATE)
    d = jnp.sum(d * _EYE[None, :, None, None, :, None], axis=4)
    d = d.transpose(3, 0, 1, 4, 2).reshape(2, SSM_GROUPS, SSM_STATE, SSM_GROUP)
    return d[0], d[1]


def _c_matrix(c_re, c_im):
    cc = jnp.stack([c_re, -c_im]).reshape(2, N_SUPER, GROUPS_PER_SUPER, SSM_GROUP, SSM_STATE)
    m = jnp.einsum('rsgcp,gh->srgphc', cc, _EYE)
    return m.reshape(N_SUPER, SUPER_W, SUPER_IN)


def _c_matrix_grad(dm):
    d = dm.reshape(N_SUPER, 2, GROUPS_PER_SUPER, SSM_STATE, GROUPS_PER_SUPER, SSM_GROUP)
    d = jnp.sum(d * _EYE[None, None, :, None, :, None], axis=4)
    d = d.transpose(1, 0, 2, 4, 3).reshape(2, SSM_GROUPS, SSM_GROUP, SSM_STATE)
    return d[0], -d[1]


def _scan_rows(buf_ref, tab_ref, carry_ref, n_groups, reverse, h_ref=None, da_ref=None):
    edge = 0 if reverse else SUBLANES - 1
    for sb in range(N_SUPER):
        cr = pl.ds(sb * SUPER_W, SUPER_HALF)
        ci = pl.ds(sb * SUPER_W + SUPER_HALF, SUPER_HALF)

        def step(gi, carry, cr=cr, ci=ci):
            g = (n_groups - 1 - gi) if reverse else gi
            rows = pl.ds(pl.multiple_of(g * SUBLANES, SUBLANES), SUBLANES)
            c_re, c_im = carry[0], carry[1]
            xr = buf_ref[rows, cr]
            xi = buf_ref[rows, ci]
            for k, shift in enumerate((1, 2, 4)):
                s = (SUBLANES - shift) if reverse else shift
                sr = pltpu.roll(xr, s, 0)
                si = pltpu.roll(xi, s, 0)
                ar = tab_ref[k, :, cr]
                ai = tab_ref[k, :, ci]
                xr, xi = xr + ar * sr - ai * si, xi + ar * si + ai * sr
            pr = tab_ref[3, :, cr]
            pi_ = tab_ref[3, :, ci]
            xr, xi = xr + pr * c_re - pi_ * c_im, xi + pr * c_im + pi_ * c_re
            buf_ref[rows, cr] = xr
            buf_ref[rows, ci] = xi
            out = [jnp.broadcast_to(xr[edge:edge + 1], xr.shape), jnp.broadcast_to(xi[edge:edge + 1], xi.shape)]
            if h_ref is not None:
                last = lax.broadcasted_iota(jnp.int32, xr.shape, 0) == SUBLANES - 1
                gr = jnp.where(last, c_re, pltpu.roll(xr, SUBLANES - 1, 0))
                gim = jnp.where(last, c_im, pltpu.roll(xi, SUBLANES - 1, 0))
                hr = h_ref[rows, cr]
                hi = h_ref[rows, ci]
                out += [carry[2] + gr * hr + gim * hi, carry[3] + gim * hr - gr * hi]
            return tuple(out)

        init = [carry_ref[:, cr], carry_ref[:, ci]]
        if h_ref is not None:
            init += [da_ref[:, cr], da_ref[:, ci]]
        fin = lax.fori_loop(0, n_groups, step, tuple(init))
        carry_ref[:, cr] = fin[0]
        carry_ref[:, ci] = fin[1]
        if h_ref is not None:
            da_ref[:, cr] = fin[2]
            da_ref[:, ci] = fin[3]


def _ssm_fwd(u, bmat, cmat, tab, d_skip, tb, exchange=None):
    seq = u.shape[0]

    def body(u_ref, b_ref, c_ref, t_ref, d_ref, s_ref, h_ref, carry_ref):
        @pl.when(pl.program_id(0) == 0)
        def _():
            carry_ref[...] = jnp.zeros_like(carry_ref)

        u_blk = u_ref[...]
        ub = _bf(u_blk)
        for sb in range(N_SUPER):
            h_ref[:, sb * SUPER_W:(sb + 1) * SUPER_W] = _mm(ub[:, sb * SUPER_IN:(sb + 1) * SUPER_IN], b_ref[sb])
        _scan_rows(h_ref, t_ref, carry_ref, tb // SUBLANES, False)
        ys = [_mm(_bf(h_ref[:, sb * SUPER_W:(sb + 1) * SUPER_W]), c_ref[sb]) for sb in range(N_SUPER)]
        s_ref[...] = jnp.concatenate(ys, axis=1) + d_ref[...] * u_blk

    return _rowcall("ssm_fwd", body, seq, tb, [u], [bmat, cmat, tab, d_skip],
                    [(SSM_W, F32), (STATE_COLS, F32)], [],
                    scratch=[pltpu.VMEM((SUBLANES, STATE_COLS), F32)], vmem=VMEM_BIG, exchange=exchange)


def _ssm_bwd(ds, u, h, bmat_t, cmat_t, tab, d_skip, tb, exchange=None):
    seq = u.shape[0]

    def body(ds_ref, u_ref, h_ref, bt_ref, ct_ref, t_ref, d_ref,
             du_ref, db_ref, dc_ref, da_ref, dd_ref, g_ref, carry_ref):
        @pl.when(pl.program_id(0) == 0)
        def _():
            carry_ref[...] = jnp.zeros_like(carry_ref)
            db_ref[...] = jnp.zeros_like(db_ref)
            dc_ref[...] = jnp.zeros_like(dc_ref)
            da_ref[...] = jnp.zeros_like(da_ref)
            dd_ref[...] = jnp.zeros_like(dd_ref)

        ds_blk = ds_ref[...]
        dsb = _bf(ds_blk)
        u_blk = u_ref[...]
        ub = _bf(u_blk)
        for sb in range(N_SUPER):
            g_ref[:, sb * SUPER_W:(sb + 1) * SUPER_W] = _mm(dsb[:, sb * SUPER_IN:(sb + 1) * SUPER_IN], ct_ref[sb])
        _scan_rows(g_ref, t_ref, carry_ref, tb // SUBLANES, True, h_ref=h_ref, da_ref=da_ref)
        dus = []
        for sb in range(N_SUPER):
            gb = _bf(g_ref[:, sb * SUPER_W:(sb + 1) * SUPER_W])
            dus.append(_mm(gb, bt_ref[sb]))
            db_ref[sb] += _mm_tn(ub[:, sb * SUPER_IN:(sb + 1) * SUPER_IN], gb)
            dc_ref[sb] += _mm_tn(_bf(h_ref[:, sb * SUPER_W:(sb + 1) * SUPER_W]),
                                 dsb[:, sb * SUPER_IN:(sb + 1) * SUPER_IN])
        du_ref[...] = jnp.concatenate(dus, axis=1) + d_ref[...] * ds_blk
        dd_ref[...] += jnp.sum(ds_blk * u_blk, axis=0, keepdims=True)

    return _rowcall("ssm_bwd", body, seq, tb, [ds, u, h], [bmat_t, cmat_t, tab, d_skip],
                    [(SSM_W, F32)],
                    [((N_SUPER, SUPER_IN, SUPER_W), F32), ((N_SUPER, SUPER_W, SUPER_IN), F32),
                     ((SUBLANES, STATE_COLS), F32), ((1, SSM_W), F32)],
                    scratch=[pltpu.VMEM((tb, STATE_COLS), F32), pltpu.VMEM((SUBLANES, STATE_COLS), F32)],
                    reverse=True, vmem=VMEM_BIG, exchange=exchange)


def _merge_core(s, attb, ga, gs, wg_ref, wab_ref, wsb_ref, wout_ref):
    zg, dgelu = _gelu_and_grad(s)
    zgb = _bf(zg)
    sg = _sig(_mm(zgb, wg_ref[...]))
    z = zg * sg
    zb = _bf(z)
    ys = jnp.concatenate([_mm(zb, wsb_ref[j]) for j in range(N_CHIPS)], axis=1)
    ya = jnp.concatenate([_mm(attb, wab_ref[j]) for j in range(N_CHIPS)], axis=1)
    sa = _sig(ga)
    ss = _sig(gs)
    mgb = _bf(sa * ya + ss * ys)
    o = _mm(mgb, wout_ref[...])
    return dict(zg=zg, dgelu=dgelu, zgb=zgb, sg=sg, zb=zb, ys=ys, ya=ya, sa=sa, ss=ss, mgb=mgb, o=o)


def _merge_fwd(x, s, att, ga, gs, g2, w_glu, w_ab, w_sb, w_out, tb):
    seq = x.shape[0]

    def body(x_ref, s_ref, att_ref, ga_ref, gs_ref, g_ref, wg_ref, wab_ref, wsb_ref, wout_ref, x2_ref):
        f = _merge_core(s_ref[...], att_ref[...], ga_ref[...], gs_ref[...], wg_ref, wab_ref, wsb_ref, wout_ref)
        n, _, _ = _rms(f["o"], g_ref[...])
        x2_ref[...] = x_ref[...] + n

    return _rowcall("merge_fwd", body, seq, tb, [x, s, att, ga, gs], [g2, w_glu, w_ab, w_sb, w_out],
                    [(D_MODEL, F32)], [], vmem=VMEM_BIG)[0]


def _merge_bwd(dx2, s, att, ga, gs, g2, w_glu, w_ab, w_sb, w_out, tb, exchange=None):
    seq = s.shape[0]
    cw = D_MODEL // N_CHIPS
    last = seq // tb - 1

    def body(dx2_ref, s_ref, att_ref, ga_ref, gs_ref, g_ref, wg_ref, wab_ref, wsb_ref, wout_ref,
             ds_ref, datt_ref, dga_ref, dgs_ref, dg_ref, dwg_ref, dwab_ref, dwsb_ref, dwout_ref,
             bwg_ref, bwab_ref, bwsb_ref, bwout_ref):
        @pl.when(pl.program_id(0) == 0)
        def _():
            for r in (dg_ref, dwg_ref, dwab_ref, dwsb_ref, dwout_ref):
                r[...] = jnp.zeros_like(r)

        attb = att_ref[...]
        f = _merge_core(s_ref[...], attb, ga_ref[...], gs_ref[...], wg_ref, wab_ref, wsb_ref, wout_ref)
        g = g_ref[...]
        _, oh, r2 = _rms(f["o"], g)
        do, dg = _rms_bwd(dx2_ref[...], oh, r2, g)
        dg_ref[...] += dg
        dob = _bf(do)
        dwout_ref[...] += _mm_tn(f["mgb"], dob)
        dmg = _mm_nt(dob, wout_ref[...])
        sa, ss = f["sa"], f["ss"]
        dyab = _bf(dmg * sa)
        dysb = _bf(dmg * ss)
        dga_ref[...] = _bf(dmg * f["ya"] * sa * (1.0 - sa))
        dgs_ref[...] = _bf(dmg * f["ys"] * ss * (1.0 - ss))
        dwab = _mm_tn(attb, dyab)
        dwsb = _mm_tn(f["zb"], dysb)
        datt = jnp.zeros((tb, ATTN_W), F32)
        dz = jnp.zeros((tb, SSM_W), F32)
        for j in range(N_CHIPS):
            dwab_ref[j] += dwab[:, j * cw:(j + 1) * cw]
            dwsb_ref[j] += dwsb[:, j * cw:(j + 1) * cw]
            datt = datt + _mm_nt(dyab[:, j * cw:(j + 1) * cw], wab_ref[j])
            dz = dz + _mm_nt(dysb[:, j * cw:(j + 1) * cw], wsb_ref[j])
        datt_ref[...] = _bf(datt)
        sg, zg = f["sg"], f["zg"]
        dglb = _bf(dz * zg * sg * (1.0 - sg))
        dwg_ref[...] += _mm_tn(f["zgb"], dglb)
        dzg = dz * sg + _mm_nt(dglb, wg_ref[...])
        ds_ref[...] = dzg * f["dgelu"]

        @pl.when(pl.program_id(0) == last)
        def _():
            for dst, src in ((bwg_ref, dwg_ref), (bwab_ref, dwab_ref), (bwsb_ref, dwsb_ref), (bwout_ref, dwout_ref)):
                dst[...] = _bf(src[...])

    shapes = [w_glu.shape, w_ab.shape, w_sb.shape, w_out.shape]
    return _rowcall("merge_bwd", body, seq, tb, [dx2, s, att, ga, gs], [g2, w_glu, w_ab, w_sb, w_out],
                    [(SSM_W, F32), (ATTN_W, BF16), (D_MODEL, BF16), (D_MODEL, BF16)],
                    [((1, D_MODEL), F32)] + [(sh, F32) for sh in shapes] + [(sh, BF16) for sh in shapes],
                    vmem=VMEM_BIG, exchange=exchange)


def _mlp_fwd_loss(x2, target, g3, g4, w_ffi, w_ffo, tb):
    seq = x2.shape[0]

    def body(x2_ref, t_ref, g3_ref, g4_ref, wi_ref, wo_ref, dy_ref, df_ref, h_ref, loss_ref, dg_ref):
        @pl.when(pl.program_id(0) == 0)
        def _():
            loss_ref[...] = jnp.zeros_like(loss_ref)
            dg_ref[...] = jnp.zeros_like(dg_ref)

        x2_blk = x2_ref[...]
        h3, _, _ = _rms(x2_blk, g3_ref[...])
        hb = _bf(h3)
        h_ref[...] = hb
        f = jnp.zeros((tb, D_MODEL), F32)
        for j in range(FF_CHUNKS):
            a = _mm(hb, wi_ref[j])
            f = f + _mm(_bf(jnp.square(jnp.maximum(a, 0.0))), wo_ref[j])
        g4 = g4_ref[...]
        n4, fh, r4 = _rms(f, g4)
        e = (x2_blk + n4) - t_ref[...]
        loss_ref[...] += 0.5 * jnp.sum(jnp.mean(e * e, axis=-1, keepdims=True))
        dy = e * (1.0 / D_MODEL)
        dy_ref[...] = dy
        df, dg = _rms_bwd(dy, fh, r4, g4)
        df_ref[...] = _bf(df)
        dg_ref[...] += dg

    return _rowcall("mlp_fwd_loss", body, seq, tb, [x2, target], [g3, g4, w_ffi, w_ffo],
                    [(D_MODEL, F32), (D_MODEL, BF16), (D_MODEL, BF16)],
                    [((SUBLANES, 128), F32), ((1, D_MODEL), F32)], vmem=VMEM_BIG)


def _mlp_bwd(x2, dy, df, h3, g3, w_ffi, w_ffo, tb):
    seq = x2.shape[0]
    cw = D_FF // FF_CHUNKS

    def body(x2_ref, dy_ref, df_ref, h_ref, g3_ref, wi_ref, wo_ref, dx_ref, act_ref, da_ref, dg_ref):
        @pl.when(pl.program_id(0) == 0)
        def _():
            dg_ref[...] = jnp.zeros_like(dg_ref)

        hb = h_ref[...]
        dfb = df_ref[...]
        dh = jnp.zeros((tb, D_MODEL), F32)
        for j in range(FF_CHUNKS):
            ra = jnp.maximum(_mm(hb, wi_ref[j]), 0.0)
            act_ref[:, j * cw:(j + 1) * cw] = _bf(ra * ra)
            dab = _bf(_mm_nt(dfb, wo_ref[j]) * (2.0 * ra))
            da_ref[:, j * cw:(j + 1) * cw] = dab
            dh = dh + _mm_nt(dab, wi_ref[j])
        g3 = g3_ref[...]
        _, xh, r3 = _rms(x2_ref[...], g3)
        dxn, dg = _rms_bwd(dh, xh, r3, g3)
        dx_ref[...] = dy_ref[...] + dxn
        dg_ref[...] += dg

    return _rowcall("mlp_bwd", body, seq, tb, [x2, dy, df, h3], [g3, w_ffi, w_ffo],
                    [(D_MODEL, F32), (D_FF, BF16), (D_FF, BF16)], [((1, D_MODEL), F32)], vmem=VMEM_BIG)


def _matmul_tn(name, a, b, tk, tn, tl, chunk_major, exchange=None):
    seq, kdim = a.shape
    ndim = b.shape[1]
    last = seq // tl - 1

    def body(a_ref, b_ref, o_ref, ob_ref):
        @pl.when(pl.program_id(2) == 0)
        def _():
            o_ref[...] = jnp.zeros_like(o_ref)

        o_ref[...] += _mm_tn(a_ref[...], b_ref[...])

        @pl.when(pl.program_id(2) == last)
        def _():
            ob_ref[...] = _bf(o_ref[...])

    if chunk_major:
        shape = (ndim // tn, kdim, tn)
        out_spec = pl.BlockSpec((None, tk, tn), lambda k, n, l: (n, k, 0))
    else:
        shape = (kdim, ndim)
        out_spec = pl.BlockSpec((tk, tn), lambda k, n, l: (k, n))
    return _fused_call(
        name, body, (kdim // tk, ndim // tn, seq // tl),
        [pl.BlockSpec((tl, tk), lambda k, n, l: (l, k)), pl.BlockSpec((tl, tn), lambda k, n, l: (l, n))],
        [out_spec, out_spec], [SDS(shape, F32), SDS(shape, BF16)], [], [a, b], exchange, _params(3, VMEM_BIG))


def _ew_call(name, fn, ins, n_out):
    rows, cols = ins[0].shape
    tr = rows
    while tr * cols * 4 > (1 << 20) and tr % 16 == 0:
        tr //= 2
    spec = pl.BlockSpec((tr, cols), lambda i: (i, 0))

    def body(*refs):
        outs = fn(*[r[...] for r in refs[:len(ins)]])
        for r, o in zip(refs[len(ins):], outs):
            r[...] = o

    return pl.pallas_call(
        body, grid=(rows // tr,), in_specs=[spec] * len(ins), out_specs=[spec] * n_out,
        out_shape=[SDS((rows, cols), F32)] * n_out, name=name, compiler_params=_params(1),
    )(*ins)


def _adam_math(w, g, m, v):
    m2 = ADAM_B1 * m + (1.0 - ADAM_B1) * g
    v2 = ADAM_B2 * v + (1.0 - ADAM_B2) * (g * g)
    m_hat = m2 / (1.0 - ADAM_B1 ** ADAM_STEP)
    v_hat = v2 / (1.0 - ADAM_B2 ** ADAM_STEP)
    delta = -ADAM_LR * (m_hat / (jnp.sqrt(v_hat) + ADAM_EPS) + ADAM_WD * w)
    return delta, m2, v2


def _sum4(name, own, recv, idx):
    _, rows, cols = own.shape
    tr = rows
    while tr * cols * 4 > (1 << 20) and tr % 16 == 0:
        tr //= 2

    def body(idx_ref, o_ref, r0_ref, r1_ref, r2_ref, out_ref):
        out_ref[...] = ((o_ref[...] + r0_ref[...].astype(F32)) + r1_ref[...].astype(F32)) + r2_ref[...].astype(F32)

    blk = (None, tr, cols)
    grid_spec = pltpu.PrefetchScalarGridSpec(
        num_scalar_prefetch=1, grid=(rows // tr,),
        in_specs=[pl.BlockSpec(blk, lambda i, s: (s[0], i, 0)), pl.BlockSpec(blk, lambda i, s: (0, i, 0)),
                  pl.BlockSpec(blk, lambda i, s: (1, i, 0)), pl.BlockSpec(blk, lambda i, s: (2, i, 0))],
        out_specs=pl.BlockSpec((tr, cols), lambda i, s: (i, 0)))
    return pl.pallas_call(body, grid_spec=grid_spec, out_shape=SDS((rows, cols), F32), name=name,
                          compiler_params=_params(1))(jnp.reshape(idx, (1,)).astype(jnp.int32), own, recv, recv, recv)


def _adam_pair(name, w, p_own, p_sib, m, v):
    def fn(w_, a, b, m_, v_):
        g = a + b
        return (g,) + _adam_math(w_, g, m_, v_)

    return _ew_call(name, fn, [w, p_own, p_sib, m, v], 4)


def _place():
    return lax.axis_index("x"), lax.axis_index("y"), lax.axis_index("c")


def _other_chips(x, y):
    return [(1 - x, y), (x, 1 - y), (1 - x, 1 - y)]


def _gather_chips(shards):
    n = len(shards)

    def copies(ins, outs, sems):
        send, recv, loc = sems
        x, y, c = _place()
        me = 2 * x + y
        peers = _other_chips(x, y)
        local = [pltpu.make_async_copy(ins[a], outs[a].at[me], loc.at[a]) for a in range(n)]

        def copy(a, j, slot):
            px, py = peers[j]
            return pltpu.make_async_remote_copy(
                src_ref=ins[a], dst_ref=outs[a].at[slot], send_sem=send.at[a, j], recv_sem=recv.at[a, j],
                device_id=(px, py, c), device_id_type=MESH_ID)

        sends = [copy(a, j, me) for a in range(n) for j in range(3)]
        recvs = [copy(a, j, 2 * px + py) for a in range(n) for j, (px, py) in enumerate(peers)]
        return local, sends, recvs

    def start(ins, outs, sems):
        local, sends, _ = copies(ins, outs, sems)
        for cp in local + sends:
            cp.start()

    def wait(ins, outs, sems):
        local, sends, recvs = copies(ins, outs, sems)
        for cp in recvs:
            cp.wait_recv()
        for cp in sends:
            cp.wait_send()
        for cp in local:
            cp.wait()

    return _Exchange(shards, [SDS((N_CHIPS,) + s.shape, s.dtype) for s in shards],
                     [pltpu.SemaphoreType.DMA((n, 3)), pltpu.SemaphoreType.DMA((n, 3)), pltpu.SemaphoreType.DMA((n,))],
                     start, wait)


def _scatter_chips(chunks):
    n = len(chunks)

    def copies(ins, outs, sems):
        send, recv = sems
        x, y, c = _place()
        return [pltpu.make_async_remote_copy(
            src_ref=ins[a].at[2 * px + py], dst_ref=outs[a].at[j], send_sem=send.at[a, j],
            recv_sem=recv.at[a, j], device_id=(px, py, c), device_id_type=MESH_ID)
            for a in range(n) for j, (px, py) in enumerate(_other_chips(x, y))]

    def start(ins, outs, sems):
        for cp in copies(ins, outs, sems):
            cp.start()

    def wait(ins, outs, sems):
        cps = copies(ins, outs, sems)
        for cp in cps:
            cp.wait_recv()
        for cp in cps:
            cp.wait_send()

    return _Exchange(chunks, [SDS((3,) + s.shape[1:], s.dtype) for s in chunks],
                     [pltpu.SemaphoreType.DMA((n, 3)), pltpu.SemaphoreType.DMA((n, 3))], start, wait)


def _swap_sibling(arrs):
    n = len(arrs)

    def copies(ins, outs, sems):
        send, recv = sems
        x, y, c = _place()
        return [pltpu.make_async_remote_copy(
            src_ref=ins[a], dst_ref=outs[a], send_sem=send.at[a], recv_sem=recv.at[a],
            device_id=(x, y, 1 - c), device_id_type=MESH_ID) for a in range(n)]

    def start(ins, outs, sems):
        for cp in copies(ins, outs, sems):
            cp.start()

    def wait(ins, outs, sems):
        cps = copies(ins, outs, sems)
        for cp in cps:
            cp.wait_recv()
        for cp in cps:
            cp.wait_send()

    return _Exchange(arrs, [SDS(s.shape, s.dtype) for s in arrs],
                     [pltpu.SemaphoreType.DMA((n,)), pltpu.SemaphoreType.DMA((n,))], start, wait)


N_DEV = 8


def _gather_devices(block):
    def copies(ins, outs, sems):
        send, recv, loc = sems
        x, y, c = _place()
        me = 4 * x + 2 * y + c
        local = pltpu.make_async_copy(ins[0], outs[0].at[me], loc.at[0])
        sends, recvs = [], []
        for k in range(1, N_DEV):
            peer = (x ^ (k >> 2), y ^ ((k >> 1) & 1), c ^ (k & 1))
            for group, slot in ((sends, me), (recvs, me ^ k)):
                group.append(pltpu.make_async_remote_copy(
                    src_ref=ins[0], dst_ref=outs[0].at[slot], send_sem=send.at[k - 1], recv_sem=recv.at[k - 1],
                    device_id=peer, device_id_type=MESH_ID))
        return local, sends, recvs

    def start(ins, outs, sems):
        local, sends, _ = copies(ins, outs, sems)
        for cp in [local] + sends:
            cp.start()

    def wait(ins, outs, sems):
        local, sends, recvs = copies(ins, outs, sems)
        for cp in recvs:
            cp.wait_recv()
        for cp in sends:
            cp.wait_send()
        local.wait()

    return _Exchange([block], [SDS((N_DEV,) + block.shape, block.dtype)],
                     [pltpu.SemaphoreType.DMA((N_DEV - 1,)), pltpu.SemaphoreType.DMA((N_DEV - 1,)),
                      pltpu.SemaphoreType.DMA((1,))], start, wait)


def _both(ex_a, ex_b):
    na_i, na_o, na_s = len(ex_a.ins), len(ex_a.outs), len(ex_a.sems)

    def start(ins, outs, sems):
        ex_a.start(ins[:na_i], outs[:na_o], sems[:na_s])
        ex_b.start(ins[na_i:], outs[na_o:], sems[na_s:])

    def wait(ins, outs, sems):
        ex_a.wait(ins[:na_i], outs[:na_o], sems[:na_s])
        ex_b.wait(ins[na_i:], outs[na_o:], sems[na_s:])

    return _Exchange(ex_a.ins + ex_b.ins, ex_a.outs + ex_b.outs, ex_a.sems + ex_b.sems, start, wait)


def _sum_devices(slots):
    def body(s_ref, o_ref):
        acc = s_ref[0]
        for d in range(1, N_DEV):
            acc = acc + s_ref[d]
        o_ref[...] = acc

    return pl.pallas_call(
        body, in_specs=[pl.BlockSpec(memory_space=pltpu.VMEM)], out_specs=pl.BlockSpec(memory_space=pltpu.VMEM),
        out_shape=SDS(slots.shape[1:], F32), name="sum_small",
        compiler_params=pltpu.CompilerParams(vmem_limit_bytes=32 * 1024 * 1024))(slots)


def _adam_small(ws, gs, ms, vs):
    n = len(ws)

    def body(*refs):
        for i in range(n):
            w_ref, g_ref, m_ref, v_ref = (refs[k * n + i] for k in range(4))
            outs = _adam_math(w_ref[...], g_ref[...], m_ref[...], v_ref[...])
            for k in range(3):
                refs[(4 + k) * n + i][...] = outs[k]

    vmem = pl.BlockSpec(memory_space=pltpu.VMEM)
    return pl.pallas_call(
        body, in_specs=[vmem] * (4 * n), out_specs=[vmem] * (3 * n),
        out_shape=[SDS(w.shape, F32) for w in ws] * 3, name="adam_small",
        compiler_params=pltpu.CompilerParams(vmem_limit_bytes=32 * 1024 * 1024))(*ws, *gs, *ms, *vs)


def _heads(t, n):
    return t.reshape(t.shape[0], n, HEAD_DIM).transpose(1, 0, 2)


def _unheads(t):
    return t.transpose(1, 0, 2).reshape(t.shape[1], t.shape[0] * HEAD_DIM)


def _local_step(x, target, small, big, tb, distributed):
    g1, g2, g3, g4 = small["norm_mix_pre"], small["norm_mix_post"], small["norm_mlp_pre"], small["norm_mlp_post"]
    dist = distributed
    me = (2 * lax.axis_index("x") + lax.axis_index("y")) if dist else 0
    bucket = jnp.asarray(_bucket_table())

    bias = _bias_table(small["rel_bias"], bucket).reshape(N_KV, Q_GROUP * BLOCK, 2 * BLOCK)
    sink_rows = jnp.repeat(small["sinks"].reshape(N_KV, Q_GROUP), BLOCK, axis=1)[..., None]
    disc_args = (small["lam_re"], small["lam_im"], small["log_dt"], small["b_re"], small["b_im"])
    (ab_re, ab_im, bb_re, bb_im), disc_vjp = jax.vjp(_ssm_discretize, *disc_args)
    tab_f, tab_b = _scan_tables(ab_re, ab_im)
    bmat = _bf(_b_matrix(bb_re, bb_im))
    cmat = _bf(_c_matrix(small["c_re"], small["c_im"]))
    d_skip = small["d_skip"]

    if dist:
        (g_in,) = _exchange_alone("gather_w_in", _gather_chips([big["w_in"]]))
        w_in = g_in.transpose(1, 0, 2).reshape(D_MODEL, IN_W)
    else:
        w_in = big["w_in"]
    mix = ("w_glu", "w_attn_branch", "w_ssm_branch", "w_out")
    outs = _inproj_fwd(x, g1, w_in, tb, _gather_chips([big[n] for n in mix]) if dist else None)
    h1, q, k, v, u, ga, gs = outs[:7]
    w_glu, w_ab, w_sb, w_out = outs[7:] if dist else [big[n] for n in mix]
    w_glu = w_glu.reshape(SSM_W, SSM_W)
    w_out = w_out.reshape(D_MODEL, D_MODEL)
    qh, kh, vh = _heads(q, N_HEADS), _heads(k, N_KV), _heads(v, N_KV)
    outs = _attn_fwd(qh, kh, vh, bias, sink_rows, _gather_chips([big["w_ff_in"]]) if dist else None)
    att = _unheads(outs[0])
    w_ffi = outs[1] if dist else big["w_ff_in"]
    outs = _ssm_fwd(u, bmat, cmat, tab_f, d_skip, tb, _gather_chips([big["w_ff_out"]]) if dist else None)
    s, h = outs[:2]
    w_ffo = outs[2] if dist else big["w_ff_out"]
    x2 = _merge_fwd(x, s, att, ga, gs, g2, w_glu, w_ab, w_sb, w_out, tb)
    dy, df, h3, loss_acc, dg4 = _mlp_fwd_loss(x2, target, g3, g4, w_ffi, w_ffo, tb)

    dx2, act, da, dg3 = _mlp_bwd(x2, dy, df, h3, g3, w_ffi, w_ffo, tb)
    tl = min(512, x.shape[0])
    chunked = (N_CHIPS, D_FF // N_CHIPS, D_MODEL)
    d_ffi, b_ffi = _matmul_tn("grad_w_ff_in", h3, da, D_MODEL, D_FF // FF_CHUNKS, tl, True)
    d_ffo, b_ffo = _matmul_tn("grad_w_ff_out", act, df, D_FF // FF_CHUNKS, D_MODEL, tl, False)
    d_ffo, b_ffo = d_ffo.reshape(chunked), b_ffo.reshape(chunked)
    outs = _merge_bwd(dx2, s, att, ga, gs, g2, w_glu, w_ab, w_sb, w_out, tb, _scatter_chips([b_ffi]) if dist else None)
    ds, datt, dga, dgs, dg2, d_glu, d_ab, d_sb, d_out, b_glu, b_ab, b_sb, b_out = outs[:13]
    r_ffi = outs[13:]
    glu4, out4 = (N_CHIPS, SSM_W // N_CHIPS, SSM_W), (N_CHIPS, D_MODEL // N_CHIPS, D_MODEL)
    d_mix = [d_glu.reshape(glu4), d_ab, d_sb, d_out.reshape(out4)]
    b_mix = [b_glu.reshape(glu4), b_ab, b_sb, b_out.reshape(out4)]
    outs = _ssm_bwd(ds, u, h, bmat.transpose(0, 2, 1), cmat.transpose(0, 2, 1), tab_b, d_skip, tb,
                    _scatter_chips([b_ffo]) if dist else None)
    du, d_bmat, d_cmat, da_acc, dd_skip = outs[:5]
    r_ffo = outs[5:]
    outs = _attn_bwd(qh, kh, vh, _heads(datt, N_HEADS), bias, sink_rows, _scatter_chips(b_mix) if dist else None)
    dqh, dkh, dvh, dbias, dsink_rows = outs[:5]
    r_mix = outs[5:]
    if dist:
        p_ffi = _sum4("sum_w_ff_in", d_ffi, r_ffi[0], me)
        p_ffo = _sum4("sum_w_ff_out", d_ffo, r_ffo[0], me)
    dx, dpj, dg1 = _inproj_bwd(x, dx2, _unheads(dqh), _unheads(dkh), _unheads(dvh), du, dga, dgs, g1, w_in, tb)

    dab_re, dab_im = _state_unlayout(jnp.sum(da_acc, axis=0))
    dbb_re, dbb_im = _b_matrix_grad(d_bmat)
    d_lam_re, d_lam_im, d_log_dt, d_b_re, d_b_im = disc_vjp((dab_re, dab_im, dbb_re, dbb_im))
    d_c_re, d_c_im = _c_matrix_grad(d_cmat)
    d_rel = _bias_grad(dbias.reshape(N_HEADS, BLOCK, 2 * BLOCK), bucket)
    d_sinks = jnp.sum(dsink_rows.reshape(N_HEADS, BLOCK), axis=1)
    small_grads = dict(
        norm_mix_pre=dg1, norm_mix_post=dg2, norm_mlp_pre=dg3, norm_mlp_post=dg4, rel_bias=d_rel, sinks=d_sinks,
        lam_re=d_lam_re, lam_im=d_lam_im, log_dt=d_log_dt, b_re=d_b_re, b_im=d_b_im, c_re=d_c_re, c_im=d_c_im,
        d_skip=dd_skip)
    ride = _both(_swap_sibling([p_ffi, p_ffo]), _gather_devices(_pack(small_grads))) if dist else None
    outs = _matmul_tn("grad_w_in", h1, dpj, D_MODEL, IN_W // 2, tl, True, ride)
    d_in, b_in = outs[:2]
    quarter = IN_W // N_CHIPS

    def in_chunks(t):
        return jnp.stack([t[0][:, :quarter], t[0][:, quarter:], t[1][:, :quarter], t[1][:, quarter:]])

    if not dist:
        return loss_acc, dx, small_grads, dict(zip(BIG, [in_chunks(d_in)] + d_mix + [d_ffi, d_ffo]))
    s_ffi, s_ffo, slots = outs[2:]
    (r_in,) = _exchange_alone("scatter_w_in", _scatter_chips([in_chunks(b_in)]))
    own_in = lax.dynamic_slice(d_in, (me // 2, 0, (me % 2) * quarter), (1, D_MODEL, quarter))
    parts = [_sum4("sum_w_in", own_in, r_in, 0)]
    parts += [_sum4("sum_" + n, d, r, me) for n, d, r in zip(mix, d_mix, r_mix)]
    sibs = _exchange_alone("swap_rest", _swap_sibling(parts))
    parts += [p_ffi, p_ffo]
    sibs = list(sibs) + [s_ffi, s_ffo]
    return loss_acc, dx, _sum_devices(slots), dict(zip(BIG, zip(parts, sibs)))


SMALL = ['norm_mix_pre', 'norm_mix_post', 'norm_mlp_pre', 'norm_mlp_post', 'rel_bias', 'sinks', 'lam_re', 'lam_im',
         'log_dt', 'b_re', 'b_im', 'c_re', 'c_im', 'd_skip']
BIG = ['w_in', 'w_glu', 'w_attn_branch', 'w_ssm_branch', 'w_out', 'w_ff_in', 'w_ff_out']
WEIGHTS = ['norm_mix_pre', 'norm_mix_post', 'norm_mlp_pre', 'norm_mlp_post', 'w_in', 'rel_bias', 'sinks', 'lam_re',
           'lam_im', 'log_dt', 'b_re', 'b_im', 'c_re', 'c_im', 'd_skip', 'w_glu', 'w_attn_branch', 'w_ssm_branch',
           'w_out', 'w_ff_in', 'w_ff_out']
PACK_COLS = 1024
PACK_ORDER = ['b_re', 'b_im', 'c_re', 'c_im', 'lam_re', 'lam_im', 'norm_mix_pre', 'norm_mix_post', 'norm_mlp_pre',
              'norm_mlp_post', 'rel_bias', 'sinks', 'log_dt', 'd_skip']


def _pack(named):
    parts = []
    for n in PACK_ORDER:
        flat = named[n].reshape(-1)
        rows = -(-flat.shape[0] // PACK_COLS)
        parts.append(jnp.pad(flat, (0, rows * PACK_COLS - flat.shape[0])).reshape(rows, PACK_COLS))
    total = sum(p.shape[0] for p in parts)
    parts.append(jnp.zeros((-total % SUBLANES, PACK_COLS), F32))
    return jnp.concatenate(parts, axis=0)


def _unpack(packed, shapes):
    out, at = {}, 0
    for n in PACK_ORDER:
        size = int(np.prod(shapes[n]))
        rows = -(-size // PACK_COLS)
        blk = packed[at:at + rows]
        out[n] = (blk.reshape(-1)[:size] if size % PACK_COLS else blk).reshape(shapes[n])
        at += rows
    return out


def kernel(x, norm_mix_pre, norm_mix_post, norm_mlp_pre, norm_mlp_post, w_in, rel_bias, sinks, lam_re, lam_im, log_dt, b_re, b_im, c_re, c_im, d_skip, w_glu, w_attn_branch, w_ssm_branch, w_out, w_ff_in, w_ff_out, loss_target, m_norm_mix_pre, m_norm_mix_post, m_norm_mlp_pre, m_norm_mlp_post, m_w_in, m_rel_bias, m_sinks, m_lam_re, m_lam_im, m_log_dt, m_b_re, m_b_im, m_c_re, m_c_im, m_d_skip, m_w_glu, m_w_attn_branch, m_w_ssm_branch, m_w_out, m_w_ff_in, m_w_ff_out, v_norm_mix_pre, v_norm_mix_post, v_norm_mlp_pre, v_norm_mlp_post, v_w_in, v_rel_bias, v_sinks, v_lam_re, v_lam_im, v_log_dt, v_b_re, v_b_im, v_c_re, v_c_im, v_d_skip, v_w_glu, v_w_attn_branch, v_w_ssm_branch, v_w_out, v_w_ff_in, v_w_ff_out):
    env = dict(locals())
    w = {n: env[n] for n in WEIGHTS}
    m = {n: env["m_" + n] for n in WEIGHTS}
    v = {n: env["v_" + n] for n in WEIGHTS}
    seq = x.shape[1]
    tb = min(256, seq)

    small = {n: w[n] for n in ('norm_mix_pre', 'norm_mix_post', 'norm_mlp_pre', 'norm_mlp_post', 'rel_bias')}
    small.update({n: w[n][0] for n in ('sinks', 'lam_re', 'lam_im', 'log_dt', 'b_re', 'b_im', 'c_re', 'c_im')})
    small['d_skip'] = w['d_skip']
    loss_acc, dx, small_g, big_g = _local_step(
        x[0], loss_target[0], small, {n: _bf(w[n][0]) for n in BIG}, tb, True)

    loss = lax.psum(loss_acc[0, 0], ("x", "y", "c"))

    grads, deltas, new_m, new_v = {}, {}, {}, {}
    for n in BIG:
        p_own, p_sib = big_g[n]
        g, d, m2, v2 = _adam_pair("adam_" + n, w[n][0], p_own, p_sib, m[n][0], v[n][0])
        grads[n], deltas[n], new_m[n], new_v[n] = g[None], d[None], m2[None], v2[None]

    grads.update(_unpack(small_g, {n: w[n].shape for n in SMALL}))
    outs = _adam_small([w[n] for n in SMALL], [grads[n] for n in SMALL], [m[n] for n in SMALL],
                       [v[n] for n in SMALL])
    for k, dst in enumerate((deltas, new_m, new_v)):
        dst.update(dict(zip(SMALL, outs[k * len(SMALL):(k + 1) * len(SMALL)])))

    return (loss, dx[None], *[grads[n] for n in WEIGHTS], *[deltas[n] for n in WEIGHTS],
            *[new_m[n] for n in WEIGHTS], *[new_v[n] for n in WEIGHTS])
```

```python
import functools
import math

import numpy as np
import jax
import jax.numpy as jnp
from jax import lax
from jax.experimental import pallas as pl
from jax.experimental.pallas import tpu as pltpu

F32 = jnp.float32
BF16 = jnp.bfloat16

D_MODEL = 1024
N_HEADS = 8
N_KV = 2
Q_GROUP = 4
HEAD_DIM = 64
ATTN_W = 512
KV_W = 128
BLOCK = 128
N_BUCKETS = 32
MAX_DISTANCE = 128
NEG_INF = -1e30
SSM_W = 512
SSM_GROUP = 16
SSM_GROUPS = 32
SSM_STATE = 64
N_SUPER = 4
GROUPS_PER_SUPER = SSM_GROUPS // N_SUPER
SUPER_IN = GROUPS_PER_SUPER * SSM_GROUP
SUPER_HALF = GROUPS_PER_SUPER * SSM_STATE
SUPER_W = 2 * SUPER_HALF
STATE_COLS = N_SUPER * SUPER_W
D_FF = 4096
FF_CHUNKS = 4
IN_W = 3328
SPLITS = (0, 512, 640, 768, 1280, 2304, 3328)
RMS_EPS = 1e-6
N_CHIPS = 4
SUBLANES = 8

ADAM_LR = 0.001
ADAM_B1 = 0.9
ADAM_B2 = 0.999
ADAM_EPS = 1e-08
ADAM_WD = 0.01
ADAM_STEP = 10

VMEM_BIG = 56 * 1024 * 1024
SDS = jax.ShapeDtypeStruct
MESH_ID = pl.DeviceIdType.MESH
ANY = pl.BlockSpec(memory_space=pl.ANY)


def _bf(x):
    return x.astype(BF16)


def _mm(a, b):
    return jnp.dot(a, b, preferred_element_type=F32)


def _mm_nt(a, b):
    return lax.dot_general(a, b, (((1,), (1,)), ((), ())), preferred_element_type=F32)


def _mm_tn(a, b):
    return lax.dot_general(a, b, (((0,), (0,)), ((), ())), preferred_element_type=F32)


def _sig(x):
    return 1.0 / (1.0 + jnp.exp(-x))


def _rms(x, g):
    r = lax.rsqrt(jnp.mean(x * x, axis=-1, keepdims=True) + RMS_EPS)
    xh = x * r
    return xh * g, xh, r


def _rms_bwd(dout, xh, r, g):
    dg = jnp.sum(dout * xh, axis=0, keepdims=True)
    dxh = dout * g
    dx = r * (dxh - xh * jnp.mean(dxh * xh, axis=-1, keepdims=True))
    return dx, dg


_GELU_C = math.sqrt(2.0 / math.pi)


def _gelu_and_grad(x):
    x2 = x * x
    inner = _GELU_C * (x + 0.044715 * (x2 * x))
    t = jnp.tanh(inner)
    y = 0.5 * x * (1.0 + t)
    dy = 0.5 * (1.0 + t) + 0.5 * x * (1.0 - t * t) * (_GELU_C * (1.0 + 3.0 * 0.044715 * x2))
    return y, dy


def _zero_map(nd, *_):
    return (0,) * nd


def _params(n_axes, vmem=None):
    return pltpu.CompilerParams(dimension_semantics=("arbitrary",) * n_axes, vmem_limit_bytes=vmem)


class _Exchange:
    def __init__(self, ins, outs, sems, start, wait):
        self.ins, self.outs, self.sems, self.start, self.wait = list(ins), list(outs), list(sems), start, wait


def _fused_call(name, body, grid, in_specs, out_specs, out_shape, scratch, args, exchange, params):
    n_in, n_out, n_scr = len(in_specs), len(out_specs), len(scratch)
    if exchange is None:
        fn = body
    else:
        ex = exchange
        n_xi, n_xo = len(ex.ins), len(ex.outs)

        def fn(*refs):
            at = 0
            parts = []
            for n in (n_in, n_xi, n_out, n_xo, n_scr, len(ex.sems)):
                parts.append(refs[at:at + n])
                at += n
            ins, x_in, outs, x_out, scr, x_sem = parts
            ids = [pl.program_id(a) for a in range(len(grid))]
            first = functools.reduce(jnp.logical_and, [i == 0 for i in ids])
            last = functools.reduce(jnp.logical_and, [i == g - 1 for i, g in zip(ids, grid)])

            @pl.when(first)
            def _():
                ex.start(x_in, x_out, x_sem)

            body(*ins, *outs, *scr)

            @pl.when(last)
            def _():
                ex.wait(x_in, x_out, x_sem)

        in_specs = list(in_specs) + [ANY] * n_xi
        out_specs = list(out_specs) + [ANY] * n_xo
        out_shape = list(out_shape) + ex.outs
        scratch = list(scratch) + ex.sems
        args = list(args) + ex.ins
    return pl.pallas_call(fn, grid=grid, in_specs=in_specs, out_specs=out_specs, out_shape=out_shape,
                          scratch_shapes=list(scratch), name=name, compiler_params=params)(*args)


def _exchange_alone(name, ex):
    def body(*refs):
        n_xi, n_xo = len(ex.ins), len(ex.outs)
        x_in, x_out, x_sem = refs[:n_xi], refs[n_xi:n_xi + n_xo], refs[n_xi + n_xo:]
        ex.start(x_in, x_out, x_sem)
        ex.wait(x_in, x_out, x_sem)

    return pl.pallas_call(body, in_specs=[ANY] * len(ex.ins), out_specs=[ANY] * len(ex.outs), out_shape=ex.outs,
                          scratch_shapes=ex.sems, name=name)(*ex.ins)


def _rowcall(name, body, seq, tb, rows, consts, row_outs, acc_outs, scratch=(), reverse=False, vmem=None,
             exchange=None):
    nb = seq // tb
    rmap = (lambda i: (nb - 1 - i, 0)) if reverse else (lambda i: (i, 0))
    in_specs = [pl.BlockSpec((tb, a.shape[1]), rmap) for a in rows]
    in_specs += [pl.BlockSpec(a.shape, functools.partial(_zero_map, a.ndim)) for a in consts]
    out_specs = [pl.BlockSpec((tb, c), rmap) for c, _ in row_outs]
    out_specs += [pl.BlockSpec(s, functools.partial(_zero_map, len(s))) for s, _ in acc_outs]
    out_shape = [SDS((seq, c), dt) for c, dt in row_outs] + [SDS(s, dt) for s, dt in acc_outs]
    return _fused_call(name, body, (nb,), in_specs, out_specs, out_shape, list(scratch), [*rows, *consts],
                       exchange, _params(1, vmem))


def _inproj_fwd(x, g1, w_in, tb, exchange=None):
    seq = x.shape[0]

    def body(x_ref, g_ref, w_ref, h_ref, q_ref, k_ref, v_ref, u_ref, ga_ref, gs_ref):
        h, _, _ = _rms(x_ref[...], g_ref[...])
        hb = _bf(h)
        h_ref[...] = hb
        pj = _mm(hb, w_ref[...])
        q_ref[...] = _bf(pj[:, SPLITS[0]:SPLITS[1]])
        k_ref[...] = _bf(pj[:, SPLITS[1]:SPLITS[2]])
        v_ref[...] = _bf(pj[:, SPLITS[2]:SPLITS[3]])
        u_ref[...] = pj[:, SPLITS[3]:SPLITS[4]]
        ga_ref[...] = pj[:, SPLITS[4]:SPLITS[5]]
        gs_ref[...] = pj[:, SPLITS[5]:SPLITS[6]]

    return _rowcall("inproj_fwd", body, seq, tb, [x], [g1, w_in],
                    [(D_MODEL, BF16), (ATTN_W, BF16), (KV_W, BF16), (KV_W, BF16), (SSM_W, F32),
                     (D_MODEL, F32), (D_MODEL, F32)], [], vmem=VMEM_BIG, exchange=exchange)


def _inproj_bwd(x, dx2, dq, dk, dv, du, dga, dgs, g1, w_in, tb, exchange=None):
    seq = x.shape[0]

    def body(x_ref, dx2_ref, dq_ref, dk_ref, dv_ref, du_ref, dga_ref, dgs_ref, g_ref, w_ref,
             dx_ref, dpj_ref, dg_ref):
        @pl.when(pl.program_id(0) == 0)
        def _():
            dg_ref[...] = jnp.zeros_like(dg_ref)

        dpj = jnp.concatenate([dq_ref[...], dk_ref[...], dv_ref[...], _bf(du_ref[...]),
                               dga_ref[...], dgs_ref[...]], axis=1)
        dpj_ref[...] = dpj
        dh = _mm_nt(dpj, w_ref[...])
        g = g_ref[...]
        _, xh, r = _rms(x_ref[...], g)
        dxn, dg = _rms_bwd(dh, xh, r, g)
        dx_ref[...] = dx2_ref[...] + dxn
        dg_ref[...] += dg

    return _rowcall("inproj_bwd", body, seq, tb, [x, dx2, dq, dk, dv, du, dga, dgs], [g1, w_in],
                    [(D_MODEL, F32), (IN_W, BF16)], [((1, D_MODEL), F32)], vmem=VMEM_BIG, exchange=exchange)


def _bucket_table():
    qi = np.arange(BLOCK)[:, None]
    kj = np.arange(2 * BLOCK)[None, :]
    dist = qi + BLOCK - kj
    max_exact = N_BUCKETS // 2
    d = np.maximum(dist, 0)
    df = np.maximum(d, 1).astype(np.float32)
    large = max_exact + (np.log(df / np.float32(max_exact)) / np.float32(math.log(MAX_DISTANCE / max_exact))
                         * np.float32(N_BUCKETS - max_exact)).astype(np.int32)
    large = np.minimum(large, N_BUCKETS - 1)
    bucket = np.where(d < max_exact, d, large)
    valid = (dist >= 0) & (dist < BLOCK)
    return np.where(valid, bucket, -1).astype(np.int32)


def _bias_table(rel_bias, bucket):
    def body(rb_ref, bk_ref, o_ref):
        bk = bk_ref[...]
        has_prev = lax.broadcasted_iota(jnp.int32, bk.shape, 1) >= BLOCK
        for h in range(N_HEADS):
            acc = jnp.full((BLOCK, 2 * BLOCK), NEG_INF, F32)
            for b in range(N_BUCKETS):
                acc = jnp.where(bk == b, rb_ref[b, h], acc)
            o_ref[0, h] = jnp.where(has_prev, acc, NEG_INF)
            o_ref[1, h] = acc

    return pl.pallas_call(
        body, out_shape=SDS((2, N_HEADS, BLOCK, 2 * BLOCK), F32),
        in_specs=[pl.BlockSpec(memory_space=pltpu.SMEM), pl.BlockSpec(memory_space=pltpu.VMEM)],
        out_specs=pl.BlockSpec(memory_space=pltpu.VMEM), name="bias_table",
    )(rel_bias, bucket)


def _bias_grad(dbias, bucket):
    def body(db_ref, bk_ref, o_ref):
        bk = bk_ref[...]
        for h in range(N_HEADS):
            db = db_ref[h]
            for b in range(N_BUCKETS):
                o_ref[b, h] = jnp.sum(jnp.where(bk == b, db, 0.0))

    return pl.pallas_call(
        body, out_shape=SDS((N_BUCKETS, N_HEADS), F32),
        in_specs=[pl.BlockSpec(memory_space=pltpu.VMEM), pl.BlockSpec(memory_space=pltpu.VMEM)],
        out_specs=pl.BlockSpec(memory_space=pltpu.SMEM), name="bias_grad",
    )(dbias, bucket)


TILE = 2 * HEAD_DIM


def _pair_layout(t):
    lead = t.shape[:-3]
    t = t.reshape(lead + (N_KV, 2, 2) + t.shape[-2:])
    nl = len(lead)
    t = jnp.transpose(t, tuple(range(nl)) + (nl, nl + 2, nl + 1, nl + 3, nl + 4))
    return t.reshape(lead + (N_KV, 2, 2 * BLOCK, t.shape[-1]))


def _pair_unlayout(t):
    t = t.reshape(N_KV, 2, 2, BLOCK, t.shape[-1]).transpose(0, 2, 1, 3, 4)
    return t.reshape(N_HEADS, BLOCK, t.shape[-1])


def _halves(t):
    tf = t.astype(F32)
    low = lax.broadcasted_iota(jnp.int32, tf.shape, 1) < HEAD_DIM
    swapped = pltpu.roll(tf, HEAD_DIM, 1)
    zero = jnp.zeros_like(tf)
    return ((_bf(jnp.where(low, tf, zero)), _bf(jnp.where(low, zero, swapped))),
            (_bf(jnp.where(low, swapped, zero)), _bf(jnp.where(low, zero, tf))))


def _fold_halves(even, odd):
    low = lax.broadcasted_iota(jnp.int32, even.shape, 1) < HEAD_DIM
    comb = jnp.where(low, even, odd)
    return comb + pltpu.roll(comb, HEAD_DIM, 1)


def _tile_rows(ref, kh):
    return jnp.concatenate([ref[:, (2 * kh) * TILE:(2 * kh + 1) * TILE],
                            ref[:, (2 * kh + 1) * TILE:(2 * kh + 2) * TILE]], axis=0)


def _attn_probs(qk, km, bias, sink):
    lg = _mm_nt(qk, km) * (HEAD_DIM ** -0.5) + bias
    m = jnp.maximum(jnp.max(lg, axis=-1, keepdims=True), sink)
    p = jnp.exp(lg - m)
    es = jnp.exp(sink - m)
    inv = 1.0 / (jnp.sum(p, axis=-1, keepdims=True) + es)
    return p * inv, es * inv


def _attn_fwd(q, k, v, bias, sink_rows, exchange=None):
    seq = q.shape[0]
    nblk = seq // BLOCK

    def body(q_ref, kp_ref, kc_ref, vp_ref, vc_ref, b_ref, s_ref, o_ref):
        which = jnp.minimum(pl.program_id(0), 1)
        kms = _halves(jnp.concatenate([kp_ref[...], kc_ref[...]], axis=0))
        vms = _halves(jnp.concatenate([vp_ref[...], vc_ref[...]], axis=0))
        for kh in range(N_KV):
            qk = _tile_rows(q_ref, kh)
            acc = jnp.zeros((2 * BLOCK, TILE), F32)
            for par in range(2):
                pr, _ = _attn_probs(qk, kms[kh][par], b_ref[which, kh, par], s_ref[kh, par])
                acc = acc + _mm(_bf(pr), vms[kh][par])
            o_ref[:, (2 * kh) * TILE:(2 * kh + 1) * TILE] = _bf(acc[:BLOCK])
            o_ref[:, (2 * kh + 1) * TILE:(2 * kh + 2) * TILE] = _bf(acc[BLOCK:])

    cur = lambda n: (n, 0)
    prev = lambda n: (jnp.maximum(n - 1, 0), 0)
    return _fused_call(
        "attn_fwd", body, (nblk,),
        [pl.BlockSpec((BLOCK, ATTN_W), cur),
         pl.BlockSpec((BLOCK, KV_W), prev), pl.BlockSpec((BLOCK, KV_W), cur),
         pl.BlockSpec((BLOCK, KV_W), prev), pl.BlockSpec((BLOCK, KV_W), cur),
         pl.BlockSpec(bias.shape, functools.partial(_zero_map, bias.ndim)),
         pl.BlockSpec(sink_rows.shape, functools.partial(_zero_map, sink_rows.ndim))],
        [pl.BlockSpec((BLOCK, ATTN_W), cur)], [SDS((seq, ATTN_W), BF16)], [],
        [q, k, k, v, v, bias, sink_rows], exchange, _params(1))


def _attn_bwd(q, k, v, d_out, bias, sink_rows, exchange=None):
    seq = q.shape[0]
    nblk = seq // BLOCK

    def body(q_ref, kp_ref, kc_ref, vp_ref, vc_ref, do_ref, b_ref, s_ref,
             dq_ref, dk_ref, dv_ref, db_ref, ds_ref, ck_ref, cv_ref):
        n = pl.program_id(0)

        @pl.when(n == 0)
        def _():
            db_ref[...] = jnp.zeros_like(db_ref)
            ds_ref[...] = jnp.zeros_like(ds_ref)
            ck_ref[...] = jnp.zeros_like(ck_ref)
            cv_ref[...] = jnp.zeros_like(cv_ref)

        @pl.when(n < nblk)
        def _():
            which = jnp.minimum(n, 1)
            scale = HEAD_DIM ** -0.5
            kms = _halves(jnp.concatenate([kp_ref[...], kc_ref[...]], axis=0))
            vms = _halves(jnp.concatenate([vp_ref[...], vc_ref[...]], axis=0))
            dks, dvs = [], []
            for kh in range(N_KV):
                qk = _tile_rows(q_ref, kh)
                dok = _tile_rows(do_ref, kh)
                dq = jnp.zeros((2 * BLOCK, TILE), F32)
                dkp, dvp = [], []
                for par in range(2):
                    pr, ps = _attn_probs(qk, kms[kh][par], b_ref[which, kh, par], s_ref[kh, par])
                    dp = _mm_nt(dok, vms[kh][par])
                    rs = jnp.sum(pr * dp, axis=-1, keepdims=True)
                    dlg = pr * (dp - rs)
                    ds_ref[kh, par] += -ps * rs
                    db_ref[kh, par] += dlg
                    dlb = _bf(dlg)
                    dq = dq + _mm(dlb, kms[kh][par])
                    dkp.append(_mm_tn(dlb, qk))
                    dvp.append(_mm_tn(_bf(pr), dok))
                dq = _bf(dq * scale)
                dq_ref[:, (2 * kh) * TILE:(2 * kh + 1) * TILE] = dq[:BLOCK]
                dq_ref[:, (2 * kh + 1) * TILE:(2 * kh + 2) * TILE] = dq[BLOCK:]
                dks.append(_fold_halves(*dkp))
                dvs.append(_fold_halves(*dvp))
            low = lax.broadcasted_iota(jnp.int32, (2 * BLOCK, TILE), 1) < HEAD_DIM
            dkk = jnp.where(low, dks[0], dks[1]) * scale
            dvv = jnp.where(low, dvs[0], dvs[1])
            dk_ref[...] = _bf(ck_ref[...] + dkk[:BLOCK])
            ck_ref[...] = dkk[BLOCK:]
            dv_ref[...] = _bf(cv_ref[...] + dvv[:BLOCK])
            cv_ref[...] = dvv[BLOCK:]

        @pl.when(n == nblk)
        def _():
            dk_ref[...] = _bf(ck_ref[...])
            dv_ref[...] = _bf(cv_ref[...])

    cur = lambda n: (jnp.minimum(n, nblk - 1), 0)
    prev = lambda n: (jnp.maximum(jnp.minimum(n, nblk - 1) - 1, 0), 0)
    late = lambda n: (jnp.maximum(n - 1, 0), 0)
    kv_spec = lambda m: pl.BlockSpec((BLOCK, KV_W), m)
    acc_b = pl.BlockSpec(bias.shape[1:], functools.partial(_zero_map, bias.ndim - 1))
    acc_s = pl.BlockSpec(sink_rows.shape, functools.partial(_zero_map, sink_rows.ndim))
    return _fused_call(
        "attn_bwd", body, (nblk + 1,),
        [pl.BlockSpec((BLOCK, ATTN_W), cur), kv_spec(prev), kv_spec(cur), kv_spec(prev), kv_spec(cur),
         pl.BlockSpec((BLOCK, ATTN_W), cur),
         pl.BlockSpec(bias.shape, functools.partial(_zero_map, bias.ndim)), acc_s],
        [pl.BlockSpec((BLOCK, ATTN_W), cur), kv_spec(late), kv_spec(late), acc_b, acc_s],
        [SDS((seq, ATTN_W), BF16), SDS((seq, KV_W), BF16), SDS((seq, KV_W), BF16),
         SDS(bias.shape[1:], F32), SDS(sink_rows.shape, F32)],
        [pltpu.VMEM((BLOCK, KV_W), F32), pltpu.VMEM((BLOCK, KV_W), F32)],
        [q, k, k, v, v, d_out, bias, sink_rows], exchange, _params(1))


def _ssm_discretize(lam_re, lam_im, log_dt, b_re, b_im):
    dt = jnp.exp(log_dt)[:, None]
    mag = jnp.exp(lam_re * dt)
    ab_re = mag * jnp.cos(lam_im * dt)
    ab_im = mag * jnp.sin(lam_im * dt)
    nr = ab_re - 1.0
    den = lam_re * lam_re + lam_im * lam_im
    f_re = (nr * lam_re + ab_im * lam_im) / den
    f_im = (ab_im * lam_re - nr * lam_im) / den
    bb_re = f_re[..., None] * b_re - f_im[..., None] * b_im
    bb_im = f_re[..., None] * b_im + f_im[..., None] * b_re
    return ab_re, ab_im, bb_re, bb_im


def _state_layout(re, im):
    z = jnp.stack([re, im]).reshape(2, N_SUPER, GROUPS_PER_SUPER, SSM_STATE)
    return z.transpose(1, 0, 2, 3).reshape(STATE_COLS)


def _state_unlayout(vec):
    z = vec.reshape(N_SUPER, 2, GROUPS_PER_SUPER, SSM_STATE).transpose(1, 0, 2, 3)
    z = z.reshape(2, SSM_GROUPS, SSM_STATE)
    return z[0], z[1]


def _scan_tables(ab_re, ab_im):
    pw = [None, (ab_re, ab_im)]
    for _ in range(2, SUBLANES + 1):
        pr, pi_ = pw[-1]
        pw.append((pr * ab_re - pi_ * ab_im, pr * ab_im + pi_ * ab_re))
    rows = np.arange(SUBLANES)[:, None]
    fwd, bwd = [], []
    for shift in (1, 2, 4):
        fwd.append(_state_layout(*pw[shift])[None, :] * (rows >= shift).astype(np.float32))
        bwd.append(_state_layout(pw[shift][0], -pw[shift][1])[None, :] * (rows < SUBLANES - shift).astype(np.float32))
    fwd.append(jnp.stack([_state_layout(*pw[r + 1]) for r in range(SUBLANES)]))
    bwd.append(jnp.stack([_state_layout(pw[SUBLANES - r][0], -pw[SUBLANES - r][1]) for r in range(SUBLANES)]))
    return jnp.stack(fwd), jnp.stack(bwd)


_EYE = np.eye(GROUPS_PER_SUPER, dtype=np.float32)


def _b_matrix(bb_re, bb_im):
    bb = jnp.stack([bb_re, bb_im]).reshape(2, N_SUPER, GROUPS_PER_SUPER, SSM_STATE, SSM_GROUP)
    m = jnp.einsum('rsgpc,gh->sgcrhp', bb, _EYE)
    return m.reshape(N_SUPER, SUPER_IN, SUPER_W)


def _b_matrix_grad(dm):
    d = dm.reshape(N_SUPER, GROUPS_PER_SUPER, SSM_GROUP, 2, GROUPS_PER_SUPER, SSM_STATE)
    d = jnp.sum(d * _EYE[None, :, None, None, :, None], axis=4)
    d = d.transpose(3, 0, 1, 4, 2).reshape(2, SSM_GROUPS, SSM_STATE, SSM_GROUP)
    return d[0], d[1]


def _c_matrix(c_re, c_im):
    cc = jnp.stack([c_re, -c_im]).reshape(2, N_SUPER, GROUPS_PER_SUPER, SSM_GROUP, SSM_STATE)
    m = jnp.einsum('rsgcp,gh->srgphc', cc, _EYE)
    return m.reshape(N_SUPER, SUPER_W, SUPER_IN)


def _c_matrix_grad(dm):
    d = dm.reshape(N_SUPER, 2, GROUPS_PER_SUPER, SSM_STATE, GROUPS_PER_SUPER, SSM_GROUP)
    d = jnp.sum(d * _EYE[None, None, :, None, :, None], axis=4)
    d = d.transpose(1, 0, 2, 4, 3).reshape(2, SSM_GROUPS, SSM_GROUP, SSM_STATE)
    return d[0], -d[1]


def _scan_rows(buf_ref, tab_ref, carry_ref, n_groups, reverse, h_ref=None, da_ref=None):
    edge = 0 if reverse else SUBLANES - 1
    for sb in range(N_SUPER):
        cr = pl.ds(sb * SUPER_W, SUPER_HALF)
        ci = pl.ds(sb * SUPER_W + SUPER_HALF, SUPER_HALF)

        def step(gi, carry, cr=cr, ci=ci):
            g = (n_groups - 1 - gi) if reverse else gi
            rows = pl.ds(pl.multiple_of(g * SUBLANES, SUBLANES), SUBLANES)
            c_re, c_im = carry[0], carry[1]
            xr = buf_ref[rows, cr]
            xi = buf_ref[rows, ci]
            for k, shift in enumerate((1, 2, 4)):
                s = (SUBLANES - shift) if reverse else shift
                sr = pltpu.roll(xr, s, 0)
                si = pltpu.roll(xi, s, 0)
                ar = tab_ref[k, :, cr]
                ai = tab_ref[k, :, ci]
                xr, xi = xr + ar * sr - ai * si, xi + ar * si + ai * sr
            pr = tab_ref[3, :, cr]
            pi_ = tab_ref[3, :, ci]
            xr, xi = xr + pr * c_re - pi_ * c_im, xi + pr * c_im + pi_ * c_re
            buf_ref[rows, cr] = xr
            buf_ref[rows, ci] = xi
            out = [jnp.broadcast_to(xr[edge:edge + 1], xr.shape), jnp.broadcast_to(xi[edge:edge + 1], xi.shape)]
            if h_ref is not None:
                last = lax.broadcasted_iota(jnp.int32, xr.shape, 0) == SUBLANES - 1
                gr = jnp.where(last, c_re, pltpu.roll(xr, SUBLANES - 1, 0))
                gim = jnp.where(last, c_im, pltpu.roll(xi, SUBLANES - 1, 0))
                hr = h_ref[rows, cr]
                hi = h_ref[rows, ci]
                out += [carry[2] + gr * hr + gim * hi, carry[3] + gim * hr - gr * hi]
            return tuple(out)

        init = [carry_ref[:, cr], carry_ref[:, ci]]
        if h_ref is not None:
            init += [da_ref[:, cr], da_ref[:, ci]]
        fin = lax.fori_loop(0, n_groups, step, tuple(init))
        carry_ref[:, cr] = fin[0]
        carry_ref[:, ci] = fin[1]
        if h_ref is not None:
            da_ref[:, cr] = fin[2]
            da_ref[:, ci] = fin[3]


def _ssm_fwd(u, bmat, cmat, tab, d_skip, tb, exchange=None):
    seq = u.shape[0]

    def body(u_ref, b_ref, c_ref, t_ref, d_ref, s_ref, h_ref, carry_ref):
        @pl.when(pl.program_id(0) == 0)
        def _():
            carry_ref[...] = jnp.zeros_like(carry_ref)

        u_blk = u_ref[...]
        ub = _bf(u_blk)
        for sb in range(N_SUPER):
            h_ref[:, sb * SUPER_W:(sb + 1) * SUPER_W] = _mm(ub[:, sb * SUPER_IN:(sb + 1) * SUPER_IN], b_ref[sb])
        _scan_rows(h_ref, t_ref, carry_ref, tb // SUBLANES, False)
        ys = [_mm(_bf(h_ref[:, sb * SUPER_W:(sb + 1) * SUPER_W]), c_ref[sb]) for sb in range(N_SUPER)]
        s_ref[...] = jnp.concatenate(ys, axis=1) + d_ref[...] * u_blk

    return _rowcall("ssm_fwd", body, seq, tb, [u], [bmat, cmat, tab, d_skip],
                    [(SSM_W, F32), (STATE_COLS, F32)], [],
                    scratch=[pltpu.VMEM((SUBLANES, STATE_COLS), F32)], vmem=VMEM_BIG, exchange=exchange)


def _ssm_bwd(ds, u, h, bmat_t, cmat_t, tab, d_skip, tb, exchange=None):
    seq = u.shape[0]

    def body(ds_ref, u_ref, h_ref, bt_ref, ct_ref, t_ref, d_ref,
             du_ref, db_ref, dc_ref, da_ref, dd_ref, g_ref, carry_ref):
        @pl.when(pl.program_id(0) == 0)
        def _():
            carry_ref[...] = jnp.zeros_like(carry_ref)
            db_ref[...] = jnp.zeros_like(db_ref)
            dc_ref[...] = jnp.zeros_like(dc_ref)
            da_ref[...] = jnp.zeros_like(da_ref)
            dd_ref[...] = jnp.zeros_like(dd_ref)

        ds_blk = ds_ref[...]
        dsb = _bf(ds_blk)
        u_blk = u_ref[...]
        ub = _bf(u_blk)
        for sb in range(N_SUPER):
            g_ref[:, sb * SUPER_W:(sb + 1) * SUPER_W] = _mm(dsb[:, sb * SUPER_IN:(sb + 1) * SUPER_IN], ct_ref[sb])
        _scan_rows(g_ref, t_ref, carry_ref, tb // SUBLANES, True, h_ref=h_ref, da_ref=da_ref)
        dus = []
        for sb in range(N_SUPER):
            gb = _bf(g_ref[:, sb * SUPER_W:(sb + 1) * SUPER_W])
            dus.append(_mm(gb, bt_ref[sb]))
            db_ref[sb] += _mm_tn(ub[:, sb * SUPER_IN:(sb + 1) * SUPER_IN], gb)
            dc_ref[sb] += _mm_tn(_bf(h_ref[:, sb * SUPER_W:(sb + 1) * SUPER_W]),
                                 dsb[:, sb * SUPER_IN:(sb + 1) * SUPER_IN])
        du_ref[...] = jnp.concatenate(dus, axis=1) + d_ref[...] * ds_blk
        dd_ref[...] += jnp.sum(ds_blk * u_blk, axis=0, keepdims=True)

    return _rowcall("ssm_bwd", body, seq, tb, [ds, u, h], [bmat_t, cmat_t, tab, d_skip],
                    [(SSM_W, F32)],
                    [((N_SUPER, SUPER_IN, SUPER_W), F32), ((N_SUPER, SUPER_W, SUPER_IN), F32),
                     ((SUBLANES, STATE_COLS), F32), ((1, SSM_W), F32)],
                    scratch=[pltpu.VMEM((tb, STATE_COLS), F32), pltpu.VMEM((SUBLANES, STATE_COLS), F32)],
                    reverse=True, vmem=VMEM_BIG, exchange=exchange)


def _merge_core(s, attb, ga, gs, wg_ref, wab_ref, wsb_ref, wout_ref):
    zg, dgelu = _gelu_and_grad(s)
    zgb = _bf(zg)
    sg = _sig(_mm(zgb, wg_ref[...]))
    z = zg * sg
    zb = _bf(z)
    ys = jnp.concatenate([_mm(zb, wsb_ref[j]) for j in range(N_CHIPS)], axis=1)
    ya = jnp.concatenate([_mm(attb, wab_ref[j]) for j in range(N_CHIPS)], axis=1)
    sa = _sig(ga)
    ss = _sig(gs)
    mgb = _bf(sa * ya + ss * ys)
    o = _mm(mgb, wout_ref[...])
    return dict(zg=zg, dgelu=dgelu, zgb=zgb, sg=sg, zb=zb, ys=ys, ya=ya, sa=sa, ss=ss, mgb=mgb, o=o)


def _merge_fwd(x, s, att, ga, gs, g2, w_glu, w_ab, w_sb, w_out, tb):
    seq = x.shape[0]

    def body(x_ref, s_ref, att_ref, ga_ref, gs_ref, g_ref, wg_ref, wab_ref, wsb_ref, wout_ref, x2_ref):
        f = _merge_core(s_ref[...], att_ref[...], ga_ref[...], gs_ref[...], wg_ref, wab_ref, wsb_ref, wout_ref)
        n, _, _ = _rms(f["o"], g_ref[...])
        x2_ref[...] = x_ref[...] + n

    return _rowcall("merge_fwd", body, seq, tb, [x, s, att, ga, gs], [g2, w_glu, w_ab, w_sb, w_out],
                    [(D_MODEL, F32)], [], vmem=VMEM_BIG)[0]


def _merge_bwd(dx2, s, att, ga, gs, g2, w_glu, w_ab, w_sb, w_out, tb, exchange=None):
    seq = s.shape[0]
    cw = D_MODEL // N_CHIPS
    last = seq // tb - 1

    def body(dx2_ref, s_ref, att_ref, ga_ref, gs_ref, g_ref, wg_ref, wab_ref, wsb_ref, wout_ref,
             ds_ref, datt_ref, dga_ref, dgs_ref, dg_ref, dwg_ref, dwab_ref, dwsb_ref, dwout_ref,
             bwg_ref, bwab_ref, bwsb_ref, bwout_ref):
        @pl.when(pl.program_id(0) == 0)
        def _():
            for r in (dg_ref, dwg_ref, dwab_ref, dwsb_ref, dwout_ref):
                r[...] = jnp.zeros_like(r)

        attb = att_ref[...]
        f = _merge_core(s_ref[...], attb, ga_ref[...], gs_ref[...], wg_ref, wab_ref, wsb_ref, wout_ref)
        g = g_ref[...]
        _, oh, r2 = _rms(f["o"], g)
        do, dg = _rms_bwd(dx2_ref[...], oh, r2, g)
        dg_ref[...] += dg
        dob = _bf(do)
        dwout_ref[...] += _mm_tn(f["mgb"], dob)
        dmg = _mm_nt(dob, wout_ref[...])
        sa, ss = f["sa"], f["ss"]
        dyab = _bf(dmg * sa)
        dysb = _bf(dmg * ss)
        dga_ref[...] = _bf(dmg * f["ya"] * sa * (1.0 - sa))
        dgs_ref[...] = _bf(dmg * f["ys"] * ss * (1.0 - ss))
        dwab = _mm_tn(attb, dyab)
        dwsb = _mm_tn(f["zb"], dysb)
        datt = jnp.zeros((tb, ATTN_W), F32)
        dz = jnp.zeros((tb, SSM_W), F32)
        for j in range(N_CHIPS):
            dwab_ref[j] += dwab[:, j * cw:(j + 1) * cw]
            dwsb_ref[j] += dwsb[:, j * cw:(j + 1) * cw]
            datt = datt + _mm_nt(dyab[:, j * cw:(j + 1) * cw], wab_ref[j])
            dz = dz + _mm_nt(dysb[:, j * cw:(j + 1) * cw], wsb_ref[j])
        datt_ref[...] = _bf(datt)
        sg, zg = f["sg"], f["zg"]
        dglb = _bf(dz * zg * sg * (1.0 - sg))
        dwg_ref[...] += _mm_tn(f["zgb"], dglb)
        dzg = dz * sg + _mm_nt(dglb, wg_ref[...])
        ds_ref[...] = dzg * f["dgelu"]

        @pl.when(pl.program_id(0) == last)
        def _():
            for dst, src in ((bwg_ref, dwg_ref), (bwab_ref, dwab_ref), (bwsb_ref, dwsb_ref), (bwout_ref, dwout_ref)):
                dst[...] = _bf(src[...])

    shapes = [w_glu.shape, w_ab.shape, w_sb.shape, w_out.shape]
    return _rowcall("merge_bwd", body, seq, tb, [dx2, s, att, ga, gs], [g2, w_glu, w_ab, w_sb, w_out],
                    [(SSM_W, F32), (ATTN_W, BF16), (D_MODEL, BF16), (D_MODEL, BF16)],
                    [((1, D_MODEL), F32)] + [(sh, F32) for sh in shapes] + [(sh, BF16) for sh in shapes],
                    vmem=VMEM_BIG, exchange=exchange)


def _mlp_fwd_loss(x2, target, g3, g4, w_ffi, w_ffo, tb):
    seq = x2.shape[0]

    def body(x2_ref, t_ref, g3_ref, g4_ref, wi_ref, wo_ref, dy_ref, df_ref, h_ref, loss_ref, dg_ref):
        @pl.when(pl.program_id(0) == 0)
        def _():
            loss_ref[...] = jnp.zeros_like(loss_ref)
            dg_ref[...] = jnp.zeros_like(dg_ref)

        x2_blk = x2_ref[...]
        h3, _, _ = _rms(x2_blk, g3_ref[...])
        hb = _bf(h3)
        h_ref[...] = hb
        f = jnp.zeros((tb, D_MODEL), F32)
        for j in range(FF_CHUNKS):
            a = _mm(hb, wi_ref[j])
            f = f + _mm(_bf(jnp.square(jnp.maximum(a, 0.0))), wo_ref[j])
        g4 = g4_ref[...]
        n4, fh, r4 = _rms(f, g4)
        e = (x2_blk + n4) - t_ref[...]
        loss_ref[...] += 0.5 * jnp.sum(jnp.mean(e * e, axis=-1, keepdims=True))
        dy = e * (1.0 / D_MODEL)
        dy_ref[...] = dy
        df, dg = _rms_bwd(dy, fh, r4, g4)
        df_ref[...] = _bf(df)
        dg_ref[...] += dg

    return _rowcall("mlp_fwd_loss", body, seq, tb, [x2, target], [g3, g4, w_ffi, w_ffo],
                    [(D_MODEL, F32), (D_MODEL, BF16), (D_MODEL, BF16)],
                    [((SUBLANES, 128), F32), ((1, D_MODEL), F32)], vmem=VMEM_BIG)


def _mlp_bwd(x2, dy, df, h3, g3, w_ffi, w_ffo, tb):
    seq = x2.shape[0]
    cw = D_FF // FF_CHUNKS

    def body(x2_ref, dy_ref, df_ref, h_ref, g3_ref, wi_ref, wo_ref, dx_ref, act_ref, da_ref, dg_ref):
        @pl.when(pl.program_id(0) == 0)
        def _():
            dg_ref[...] = jnp.zeros_like(dg_ref)

        hb = h_ref[...]
        dfb = df_ref[...]
        dh = jnp.zeros((tb, D_MODEL), F32)
        for j in range(FF_CHUNKS):
            ra = jnp.maximum(_mm(hb, wi_ref[j]), 0.0)
            act_ref[:, j * cw:(j + 1) * cw] = _bf(ra * ra)
            dab = _bf(_mm_nt(dfb, wo_ref[j]) * (2.0 * ra))
            da_ref[:, j * cw:(j + 1) * cw] = dab
            dh = dh + _mm_nt(dab, wi_ref[j])
        g3 = g3_ref[...]
        _, xh, r3 = _rms(x2_ref[...], g3)
        dxn, dg = _rms_bwd(dh, xh, r3, g3)
        dx_ref[...] = dy_ref[...] + dxn
        dg_ref[...] += dg

    return _rowcall("mlp_bwd", body, seq, tb, [x2, dy, df, h3], [g3, w_ffi, w_ffo],
                    [(D_MODEL, F32), (D_FF, BF16), (D_FF, BF16)], [((1, D_MODEL), F32)], vmem=VMEM_BIG)


def _matmul_tn(name, a, b, tk, tn, tl, chunk_major, exchange=None):
    seq, kdim = a.shape
    ndim = b.shape[1]
    last = seq // tl - 1

    def body(a_ref, b_ref, o_ref, ob_ref):
        @pl.when(pl.program_id(2) == 0)
        def _():
            o_ref[...] = jnp.zeros_like(o_ref)

        o_ref[...] += _mm_tn(a_ref[...], b_ref[...])

        @pl.when(pl.program_id(2) == last)
        def _():
            ob_ref[...] = _bf(o_ref[...])

    if chunk_major:
        shape = (ndim // tn, kdim, tn)
        out_spec = pl.BlockSpec((None, tk, tn), lambda k, n, l: (n, k, 0))
    else:
        shape = (kdim, ndim)
        out_spec = pl.BlockSpec((tk, tn), lambda k, n, l: (k, n))
    return _fused_call(
        name, body, (kdim // tk, ndim // tn, seq // tl),
        [pl.BlockSpec((tl, tk), lambda k, n, l: (l, k)), pl.BlockSpec((tl, tn), lambda k, n, l: (l, n))],
        [out_spec, out_spec], [SDS(shape, F32), SDS(shape, BF16)], [], [a, b], exchange, _params(3, VMEM_BIG))


def _ew_call(name, fn, ins, n_out):
    rows, cols = ins[0].shape
    tr = rows
    while tr * cols * 4 > (1 << 20) and tr % 16 == 0:
        tr //= 2
    spec = pl.BlockSpec((tr, cols), lambda i: (i, 0))

    def body(*refs):
        outs = fn(*[r[...] for r in refs[:len(ins)]])
        for r, o in zip(refs[len(ins):], outs):
            r[...] = o

    return pl.pallas_call(
        body, grid=(rows // tr,), in_specs=[spec] * len(ins), out_specs=[spec] * n_out,
        out_shape=[SDS((rows, cols), F32)] * n_out, name=name, compiler_params=_params(1),
    )(*ins)


def _adam_math(w, g, m, v):
    m2 = ADAM_B1 * m + (1.0 - ADAM_B1) * g
    v2 = ADAM_B2 * v + (1.0 - ADAM_B2) * (g * g)
    m_hat = m2 / (1.0 - ADAM_B1 ** ADAM_STEP)
    v_hat = v2 / (1.0 - ADAM_B2 ** ADAM_STEP)
    delta = -ADAM_LR * (m_hat / (jnp.sqrt(v_hat) + ADAM_EPS) + ADAM_WD * w)
    return delta, m2, v2


def _sum4(name, own, recv, idx):
    _, rows, cols = own.shape
    tr = rows
    while tr * cols * 4 > (1 << 20) and tr % 16 == 0:
        tr //= 2

    def body(idx_ref, o_ref, r0_ref, r1_ref, r2_ref, out_ref):
        out_ref[...] = ((o_ref[...] + r0_ref[...].astype(F32)) + r1_ref[...].astype(F32)) + r2_ref[...].astype(F32)

    blk = (None, tr, cols)
    grid_spec = pltpu.PrefetchScalarGridSpec(
        num_scalar_prefetch=1, grid=(rows // tr,),
        in_specs=[pl.BlockSpec(blk, lambda i, s: (s[0], i, 0)), pl.BlockSpec(blk, lambda i, s: (0, i, 0)),
                  pl.BlockSpec(blk, lambda i, s: (1, i, 0)), pl.BlockSpec(blk, lambda i, s: (2, i, 0))],
        out_specs=pl.BlockSpec((tr, cols), lambda i, s: (i, 0)))
    return pl.pallas_call(body, grid_spec=grid_spec, out_shape=SDS((rows, cols), F32), name=name,
                          compiler_params=_params(1))(jnp.reshape(idx, (1,)).astype(jnp.int32), own, recv, recv, recv)


def _adam_pair(name, w, p_own, p_sib, m, v):
    def fn(w_, a, b, m_, v_):
        g = a + b
        return (g,) + _adam_math(w_, g, m_, v_)

    return _ew_call(name, fn, [w, p_own, p_sib, m, v], 4)


def _place():
    return lax.axis_index("x"), lax.axis_index("y"), lax.axis_index("c")


def _other_chips(x, y):
    return [(1 - x, y), (x, 1 - y), (1 - x, 1 - y)]


def _gather_chips(shards):
    n = len(shards)

    def copies(ins, outs, sems):
        send, recv, loc = sems
        x, y, c = _place()
        me = 2 * x + y
        peers = _other_chips(x, y)
        local = [pltpu.make_async_copy(ins[a], outs[a].at[me], loc.at[a]) for a in range(n)]

        def copy(a, j, slot):
            px, py = peers[j]
            return pltpu.make_async_remote_copy(
                src_ref=ins[a], dst_ref=outs[a].at[slot], send_sem=send.at[a, j], recv_sem=recv.at[a, j],
                device_id=(px, py, c), device_id_type=MESH_ID)

        sends = [copy(a, j, me) for a in range(n) for j in range(3)]
        recvs = [copy(a, j, 2 * px + py) for a in range(n) for j, (px, py) in enumerate(peers)]
        return local, sends, recvs

    def start(ins, outs, sems):
        local, sends, _ = copies(ins, outs, sems)
        for cp in local + sends:
            cp.start()

    def wait(ins, outs, sems):
        local, sends, recvs = copies(ins, outs, sems)
        for cp in recvs:
            cp.wait_recv()
        for cp in sends:
            cp.wait_send()
        for cp in local:
            cp.wait()

    return _Exchange(shards, [SDS((N_CHIPS,) + s.shape, s.dtype) for s in shards],
                     [pltpu.SemaphoreType.DMA((n, 3)), pltpu.SemaphoreType.DMA((n, 3)), pltpu.SemaphoreType.DMA((n,))],
                     start, wait)


def _scatter_chips(chunks):
    n = len(chunks)

    def copies(ins, outs, sems):
        send, recv = sems
        x, y, c = _place()
        return [pltpu.make_async_remote_copy(
            src_ref=ins[a].at[2 * px + py], dst_ref=outs[a].at[j], send_sem=send.at[a, j],
            recv_sem=recv.at[a, j], device_id=(px, py, c), device_id_type=MESH_ID)
            for a in range(n) for j, (px, py) in enumerate(_other_chips(x, y))]

    def start(ins, outs, sems):
        for cp in copies(ins, outs, sems):
            cp.start()

    def wait(ins, outs, sems):
        cps = copies(ins, outs, sems)
        for cp in cps:
            cp.wait_recv()
        for cp in cps:
            cp.wait_send()

    return _Exchange(chunks, [SDS((3,) + s.shape[1:], s.dtype) for s in chunks],
                     [pltpu.SemaphoreType.DMA((n, 3)), pltpu.SemaphoreType.DMA((n, 3))], start, wait)


def _swap_sibling(arrs):
    n = len(arrs)

    def copies(ins, outs, sems):
        send, recv = sems
        x, y, c = _place()
        return [pltpu.make_async_remote_copy(
            src_ref=ins[a], dst_ref=outs[a], send_sem=send.at[a], recv_sem=recv.at[a],
            device_id=(x, y, 1 - c), device_id_type=MESH_ID) for a in range(n)]

    def start(ins, outs, sems):
        for cp in copies(ins, outs, sems):
            cp.start()

    def wait(ins, outs, sems):
        cps = copies(ins, outs, sems)
        for cp in cps:
            cp.wait_recv()
        for cp in cps:
            cp.wait_send()

    return _Exchange(arrs, [SDS(s.shape, s.dtype) for s in arrs],
                     [pltpu.SemaphoreType.DMA((n,)), pltpu.SemaphoreType.DMA((n,))], start, wait)


N_DEV = 8


def _gather_devices(block):
    def copies(ins, outs, sems):
        send, recv, loc = sems
        x, y, c = _place()
        me = 4 * x + 2 * y + c
        local = pltpu.make_async_copy(ins[0], outs[0].at[me], loc.at[0])
        sends, recvs = [], []
        for k in range(1, N_DEV):
            peer = (x ^ (k >> 2), y ^ ((k >> 1) & 1), c ^ (k & 1))
            for group, slot in ((sends, me), (recvs, me ^ k)):
                group.append(pltpu.make_async_remote_copy(
                    src_ref=ins[0], dst_ref=outs[0].at[slot], send_sem=send.at[k - 1], recv_sem=recv.at[k - 1],
                    device_id=peer, device_id_type=MESH_ID))
        return local, sends, recvs

    def start(ins, outs, sems):
        local, sends, _ = copies(ins, outs, sems)
        for cp in [local] + sends:
            cp.start()

    def wait(ins, outs, sems):
        local, sends, recvs = copies(ins, outs, sems)
        for cp in recvs:
            cp.wait_recv()
        for cp in sends:
            cp.wait_send()
        local.wait()

    return _Exchange([block], [SDS((N_DEV,) + block.shape, block.dtype)],
                     [pltpu.SemaphoreType.DMA((N_DEV - 1,)), pltpu.SemaphoreType.DMA((N_DEV - 1,)),
                      pltpu.SemaphoreType.DMA((1,))], start, wait)


def _both(ex_a, ex_b):
    na_i, na_o, na_s = len(ex_a.ins), len(ex_a.outs), len(ex_a.sems)

    def start(ins, outs, sems):
        ex_a.start(ins[:na_i], outs[:na_o], sems[:na_s])
        ex_b.start(ins[na_i:], outs[na_o:], sems[na_s:])

    def wait(ins, outs, sems):
        ex_a.wait(ins[:na_i], outs[:na_o], sems[:na_s])
        ex_b.wait(ins[na_i:], outs[na_o:], sems[na_s:])

    return _Exchange(ex_a.ins + ex_b.ins, ex_a.outs + ex_b.outs, ex_a.sems + ex_b.sems, start, wait)


def _sum_devices(slots):
    def body(s_ref, o_ref):
        acc = s_ref[0]
        for d in range(1, N_DEV):
            acc = acc + s_ref[d]
        o_ref[...] = acc

    return pl.pallas_call(
        body, in_specs=[pl.BlockSpec(memory_space=pltpu.VMEM)], out_specs=pl.BlockSpec(memory_space=pltpu.VMEM),
        out_shape=SDS(slots.shape[1:], F32), name="sum_small",
        compiler_params=pltpu.CompilerParams(vmem_limit_bytes=32 * 1024 * 1024))(slots)


def _adam_small(ws, gs, ms, vs):
    n = len(ws)

    def body(*refs):
        for i in range(n):
            w_ref, g_ref, m_ref, v_ref = (refs[k * n + i] for k in range(4))
            outs = _adam_math(w_ref[...], g_ref[...], m_ref[...], v_ref[...])
            for k in range(3):
                refs[(4 + k) * n + i][...] = outs[k]

    vmem = pl.BlockSpec(memory_space=pltpu.VMEM)
    return pl.pallas_call(
        body, in_specs=[vmem] * (4 * n), out_specs=[vmem] * (3 * n),
        out_shape=[SDS(w.shape, F32) for w in ws] * 3, name="adam_small",
        compiler_params=pltpu.CompilerParams(vmem_limit_bytes=32 * 1024 * 1024))(*ws, *gs, *ms, *vs)


def _local_step(x, target, small, big, tb, distributed):
    g1, g2, g3, g4 = small["norm_mix_pre"], small["norm_mix_post"], small["norm_mlp_pre"], small["norm_mlp_post"]
    dist = distributed
    me = (2 * lax.axis_index("x") + lax.axis_index("y")) if dist else 0
    bucket = jnp.asarray(_bucket_table())

    bias = _pair_layout(_bias_table(small["rel_bias"], bucket))
    sink_rows = _pair_layout(jnp.broadcast_to(small["sinks"].reshape(N_HEADS, 1, 1), (N_HEADS, BLOCK, 1)))
    disc_args = (small["lam_re"], small["lam_im"], small["log_dt"], small["b_re"], small["b_im"])
    (ab_re, ab_im, bb_re, bb_im), disc_vjp = jax.vjp(_ssm_discretize, *disc_args)
    tab_f, tab_b = _scan_tables(ab_re, ab_im)
    bmat = _bf(_b_matrix(bb_re, bb_im))
    cmat = _bf(_c_matrix(small["c_re"], small["c_im"]))
    d_skip = small["d_skip"]

    if dist:
        (g_in,) = _exchange_alone("gather_w_in", _gather_chips([big["w_in"]]))
        w_in = g_in.transpose(1, 0, 2).reshape(D_MODEL, IN_W)
    else:
        w_in = big["w_in"]
    mix = ("w_glu", "w_attn_branch", "w_ssm_branch", "w_out")
    outs = _inproj_fwd(x, g1, w_in, tb, _gather_chips([big[n] for n in mix]) if dist else None)
    h1, q, k, v, u, ga, gs = outs[:7]
    w_glu, w_ab, w_sb, w_out = outs[7:] if dist else [big[n] for n in mix]
    w_glu = w_glu.reshape(SSM_W, SSM_W)
    w_out = w_out.reshape(D_MODEL, D_MODEL)
    outs = _attn_fwd(q, k, v, bias, sink_rows, _gather_chips([big["w_ff_in"]]) if dist else None)
    att = outs[0]
    w_ffi = outs[1] if dist else big["w_ff_in"]
    outs = _ssm_fwd(u, bmat, cmat, tab_f, d_skip, tb, _gather_chips([big["w_ff_out"]]) if dist else None)
    s, h = outs[:2]
    w_ffo = outs[2] if dist else big["w_ff_out"]
    x2 = _merge_fwd(x, s, att, ga, gs, g2, w_glu, w_ab, w_sb, w_out, tb)
    dy, df, h3, loss_acc, dg4 = _mlp_fwd_loss(x2, target, g3, g4, w_ffi, w_ffo, tb)

    dx2, act, da, dg3 = _mlp_bwd(x2, dy, df, h3, g3, w_ffi, w_ffo, tb)
    tl = min(512, x.shape[0])
    chunked = (N_CHIPS, D_FF // N_CHIPS, D_MODEL)
    d_ffi, b_ffi = _matmul_tn("grad_w_ff_in", h3, da, D_MODEL, D_FF // FF_CHUNKS, tl, True)
    d_ffo, b_ffo = _matmul_tn("grad_w_ff_out", act, df, D_FF // FF_CHUNKS, D_MODEL, tl, False)
    d_ffo, b_ffo = d_ffo.reshape(chunked), b_ffo.reshape(chunked)
    outs = _merge_bwd(dx2, s, att, ga, gs, g2, w_glu, w_ab, w_sb, w_out, tb, _scatter_chips([b_ffi]) if dist else None)
    ds, datt, dga, dgs, dg2, d_glu, d_ab, d_sb, d_out, b_glu, b_ab, b_sb, b_out = outs[:13]
    r_ffi = outs[13:]
    glu4, out4 = (N_CHIPS, SSM_W // N_CHIPS, SSM_W), (N_CHIPS, D_MODEL // N_CHIPS, D_MODEL)
    d_mix = [d_glu.reshape(glu4), d_ab, d_sb, d_out.reshape(out4)]
    b_mix = [b_glu.reshape(glu4), b_ab, b_sb, b_out.reshape(out4)]
    outs = _ssm_bwd(ds, u, h, bmat.transpose(0, 2, 1), cmat.transpose(0, 2, 1), tab_b, d_skip, tb,
                    _scatter_chips([b_ffo]) if dist else None)
    du, d_bmat, d_cmat, da_acc, dd_skip = outs[:5]
    r_ffo = outs[5:]
    outs = _attn_bwd(q, k, v, datt, bias, sink_rows, _scatter_chips(b_mix) if dist else None)
    dq, dk, dv, dbias, dsink_rows = outs[:5]
    r_mix = outs[5:]
    if dist:
        p_ffi = _sum4("sum_w_ff_in", d_ffi, r_ffi[0], me)
        p_ffo = _sum4("sum_w_ff_out", d_ffo, r_ffo[0], me)
    dx, dpj, dg1 = _inproj_bwd(x, dx2, dq, dk, dv, du, dga, dgs, g1, w_in, tb)

    dab_re, dab_im = _state_unlayout(jnp.sum(da_acc, axis=0))
    dbb_re, dbb_im = _b_matrix_grad(d_bmat)
    d_lam_re, d_lam_im, d_log_dt, d_b_re, d_b_im = disc_vjp((dab_re, dab_im, dbb_re, dbb_im))
    d_c_re, d_c_im = _c_matrix_grad(d_cmat)
    d_rel = _bias_grad(_pair_unlayout(dbias), bucket)
    d_sinks = jnp.sum(_pair_unlayout(dsink_rows), axis=(1, 2))
    small_grads = dict(
        norm_mix_pre=dg1, norm_mix_post=dg2, norm_mlp_pre=dg3, norm_mlp_post=dg4, rel_bias=d_rel, sinks=d_sinks,
        lam_re=d_lam_re, lam_im=d_lam_im, log_dt=d_log_dt, b_re=d_b_re, b_im=d_b_im, c_re=d_c_re, c_im=d_c_im,
        d_skip=dd_skip)
    ride = _both(_swap_sibling([p_ffi, p_ffo]), _gather_devices(_pack(small_grads))) if dist else None
    outs = _matmul_tn("grad_w_in", h1, dpj, D_MODEL, IN_W // 2, tl, True, ride)
    d_in, b_in = outs[:2]
    quarter = IN_W // N_CHIPS

    def in_chunks(t):
        return jnp.stack([t[0][:, :quarter], t[0][:, quarter:], t[1][:, :quarter], t[1][:, quarter:]])

    if not dist:
        return loss_acc, dx, small_grads, dict(zip(BIG, [in_chunks(d_in)] + d_mix + [d_ffi, d_ffo]))
    s_ffi, s_ffo, slots = outs[2:]
    (r_in,) = _exchange_alone("scatter_w_in", _scatter_chips([in_chunks(b_in)]))
    own_in = lax.dynamic_slice(d_in, (me // 2, 0, (me % 2) * quarter), (1, D_MODEL, quarter))
    parts = [_sum4("sum_w_in", own_in, r_in, 0)]
    parts += [_sum4("sum_" + n, d, r, me) for n, d, r in zip(mix, d_mix, r_mix)]
    sibs = _exchange_alone("swap_rest", _swap_sibling(parts))
    parts += [p_ffi, p_ffo]
    sibs = list(sibs) + [s_ffi, s_ffo]
    return loss_acc, dx, _sum_devices(slots), dict(zip(BIG, zip(parts, sibs)))


SMALL = ['norm_mix_pre', 'norm_mix_post', 'norm_mlp_pre', 'norm_mlp_post', 'rel_bias', 'sinks', 'lam_re', 'lam_im',
         'log_dt', 'b_re', 'b_im', 'c_re', 'c_im', 'd_skip']
BIG = ['w_in', 'w_glu', 'w_attn_branch', 'w_ssm_branch', 'w_out', 'w_ff_in', 'w_ff_out']
WEIGHTS = ['norm_mix_pre', 'norm_mix_post', 'norm_mlp_pre', 'norm_mlp_post', 'w_in', 'rel_bias', 'sinks', 'lam_re',
           'lam_im', 'log_dt', 'b_re', 'b_im', 'c_re', 'c_im', 'd_skip', 'w_glu', 'w_attn_branch', 'w_ssm_branch',
           'w_out', 'w_ff_in', 'w_ff_out']
PACK_COLS = 1024
PACK_ORDER = ['b_re', 'b_im', 'c_re', 'c_im', 'lam_re', 'lam_im', 'norm_mix_pre', 'norm_mix_post', 'norm_mlp_pre',
              'norm_mlp_post', 'rel_bias', 'sinks', 'log_dt', 'd_skip']


def _pack(named):
    parts = []
    for n in PACK_ORDER:
        flat = named[n].reshape(-1)
        rows = -(-flat.shape[0] // PACK_COLS)
        parts.append(jnp.pad(flat, (0, rows * PACK_COLS - flat.shape[0])).reshape(rows, PACK_COLS))
    total = sum(p.shape[0] for p in parts)
    parts.append(jnp.zeros((-total % SUBLANES, PACK_COLS), F32))
    return jnp.concatenate(parts, axis=0)


def _unpack(packed, shapes):
    out, at = {}, 0
    for n in PACK_ORDER:
        size = int(np.prod(shapes[n]))
        rows = -(-size // PACK_COLS)
        blk = packed[at:at + rows]
        out[n] = (blk.reshape(-1)[:size] if size % PACK_COLS else blk).reshape(shapes[n])
        at += rows
    return out


def kernel(x, norm_mix_pre, norm_mix_post, norm_mlp_pre, norm_mlp_post, w_in, rel_bias, sinks, lam_re, lam_im, log_dt, b_re, b_im, c_re, c_im, d_skip, w_glu, w_attn_branch, w_ssm_branch, w_out, w_ff_in, w_ff_out, loss_target, m_norm_mix_pre, m_norm_mix_post, m_norm_mlp_pre, m_norm_mlp_post, m_w_in, m_rel_bias, m_sinks, m_lam_re, m_lam_im, m_log_dt, m_b_re, m_b_im, m_c_re, m_c_im, m_d_skip, m_w_glu, m_w_attn_branch, m_w_ssm_branch, m_w_out, m_w_ff_in, m_w_ff_out, v_norm_mix_pre, v_norm_mix_post, v_norm_mlp_pre, v_norm_mlp_post, v_w_in, v_rel_bias, v_sinks, v_lam_re, v_lam_im, v_log_dt, v_b_re, v_b_im, v_c_re, v_c_im, v_d_skip, v_w_glu, v_w_attn_branch, v_w_ssm_branch, v_w_out, v_w_ff_in, v_w_ff_out):
    env = dict(locals())
    w = {n: env[n] for n in WEIGHTS}
    m = {n: env["m_" + n] for n in WEIGHTS}
    v = {n: env["v_" + n] for n in WEIGHTS}
    seq = x.shape[1]
    tb = min(256, seq)

    small = {n: w[n] for n in ('norm_mix_pre', 'norm_mix_post', 'norm_mlp_pre', 'norm_mlp_post', 'rel_bias')}
    small.update({n: w[n][0] for n in ('sinks', 'lam_re', 'lam_im', 'log_dt', 'b_re', 'b_im', 'c_re', 'c_im')})
    small['d_skip'] = w['d_skip']
    loss_acc, dx, small_g, big_g = _local_step(
        x[0], loss_target[0], small, {n: _bf(w[n][0]) for n in BIG}, tb, True)

    loss = lax.psum(loss_acc[0, 0], ("x", "y", "c"))

    grads, deltas, new_m, new_v = {}, {}, {}, {}
    for n in BIG:
        p_own, p_sib = big_g[n]
        g, d, m2, v2 = _adam_pair("adam_" + n, w[n][0], p_own, p_sib, m[n][0], v[n][0])
        grads[n], deltas[n], new_m[n], new_v[n] = g[None], d[None], m2[None], v2[None]

    grads.update(_unpack(small_g, {n: w[n].shape for n in SMALL}))
    outs = _adam_small([w[n] for n in SMALL], [grads[n] for n in SMALL], [m[n] for n in SMALL],
                       [v[n] for n in SMALL])
    for k, dst in enumerate((deltas, new_m, new_v)):
        dst.update(dict(zip(SMALL, outs[k * len(SMALL):(k + 1) * len(SMALL)])))

    return (loss, dx[None], *[grads[n] for n in WEIGHTS], *[deltas[n] for n in WEIGHTS],
            *[new_m[n] for n in WEIGHTS], *[new_v[n] for n in WEIGHTS])
```

```python
import functools
import math

import numpy as np
import jax
import jax.numpy as jnp
from jax import lax
from jax.experimental import pallas as pl
from jax.experimental.pallas import tpu as pltpu

F32 = jnp.float32
BF16 = jnp.bfloat16

D_MODEL = 1024
N_HEADS = 8
N_KV = 2
Q_GROUP = 4
HEAD_DIM = 64
ATTN_W = 512
KV_W = 128
BLOCK = 128
N_BUCKETS = 32
MAX_DISTANCE = 128
NEG_INF = -1e30
SSM_W = 512
SSM_GROUP = 16
SSM_GROUPS = 32
SSM_STATE = 64
N_SUPER = 4
GROUPS_PER_SUPER = SSM_GROUPS // N_SUPER
SUPER_IN = GROUPS_PER_SUPER * SSM_GROUP
SUPER_HALF = GROUPS_PER_SUPER * SSM_STATE
SUPER_W = 2 * SUPER_HALF
STATE_COLS = N_SUPER * SUPER_W
D_FF = 4096
FF_CHUNKS = 4
IN_W = 3328
SPLITS = (0, 512, 640, 768, 1280, 2304, 3328)
RMS_EPS = 1e-6
N_CHIPS = 4
SUBLANES = 8

ADAM_LR = 0.001
ADAM_B1 = 0.9
ADAM_B2 = 0.999
ADAM_EPS = 1e-08
ADAM_WD = 0.01
ADAM_STEP = 10

VMEM_BIG = 56 * 1024 * 1024
SDS = jax.ShapeDtypeStruct
MESH_ID = pl.DeviceIdType.MESH
ANY = pl.BlockSpec(memory_space=pl.ANY)


def _bf(x):
    return x.astype(BF16)


def _mm(a, b):
    return jnp.dot(a, b, preferred_element_type=F32)


def _mm_nt(a, b):
    return lax.dot_general(a, b, (((1,), (1,)), ((), ())), preferred_element_type=F32)


def _mm_tn(a, b):
    return lax.dot_general(a, b, (((0,), (0,)), ((), ())), preferred_element_type=F32)


def _sig(x):
    return 1.0 / (1.0 + jnp.exp(-x))


def _rms(x, g):
    r = lax.rsqrt(jnp.mean(x * x, axis=-1, keepdims=True) + RMS_EPS)
    xh = x * r
    return xh * g, xh, r


def _rms_bwd(dout, xh, r, g):
    dg = jnp.sum(dout * xh, axis=0, keepdims=True)
    dxh = dout * g
    dx = r * (dxh - xh * jnp.mean(dxh * xh, axis=-1, keepdims=True))
    return dx, dg


_GELU_C = math.sqrt(2.0 / math.pi)


def _gelu_and_grad(x):
    x2 = x * x
    inner = _GELU_C * (x + 0.044715 * (x2 * x))
    t = jnp.tanh(inner)
    y = 0.5 * x * (1.0 + t)
    dy = 0.5 * (1.0 + t) + 0.5 * x * (1.0 - t * t) * (_GELU_C * (1.0 + 3.0 * 0.044715 * x2))
    return y, dy


def _zero_map(nd, *_):
    return (0,) * nd


def _params(n_axes, vmem=None):
    return pltpu.CompilerParams(dimension_semantics=("arbitrary",) * n_axes, vmem_limit_bytes=vmem)


class _Exchange:
    def __init__(self, ins, outs, sems, start, wait):
        self.ins, self.outs, self.sems, self.start, self.wait = list(ins), list(outs), list(sems), start, wait


def _fused_call(name, body, grid, in_specs, out_specs, out_shape, scratch, args, exchange, params):
    n_in, n_out, n_scr = len(in_specs), len(out_specs), len(scratch)
    if exchange is None:
        fn = body
    else:
        ex = exchange
        n_xi, n_xo = len(ex.ins), len(ex.outs)

        def fn(*refs):
            at = 0
            parts = []
            for n in (n_in, n_xi, n_out, n_xo, n_scr, len(ex.sems)):
                parts.append(refs[at:at + n])
                at += n
            ins, x_in, outs, x_out, scr, x_sem = parts
            ids = [pl.program_id(a) for a in range(len(grid))]
            first = functools.reduce(jnp.logical_and, [i == 0 for i in ids])
            last = functools.reduce(jnp.logical_and, [i == g - 1 for i, g in zip(ids, grid)])

            @pl.when(first)
            def _():
                ex.start(x_in, x_out, x_sem)

            body(*ins, *outs, *scr)

            @pl.when(last)
            def _():
                ex.wait(x_in, x_out, x_sem)

        in_specs = list(in_specs) + [ANY] * n_xi
        out_specs = list(out_specs) + [ANY] * n_xo
        out_shape = list(out_shape) + ex.outs
        scratch = list(scratch) + ex.sems
        args = list(args) + ex.ins
    return pl.pallas_call(fn, grid=grid, in_specs=in_specs, out_specs=out_specs, out_shape=out_shape,
                          scratch_shapes=list(scratch), name=name, compiler_params=params)(*args)


def _exchange_alone(name, ex):
    def body(*refs):
        n_xi, n_xo = len(ex.ins), len(ex.outs)
        x_in, x_out, x_sem = refs[:n_xi], refs[n_xi:n_xi + n_xo], refs[n_xi + n_xo:]
        ex.start(x_in, x_out, x_sem)
        ex.wait(x_in, x_out, x_sem)

    return pl.pallas_call(body, in_specs=[ANY] * len(ex.ins), out_specs=[ANY] * len(ex.outs), out_shape=ex.outs,
                          scratch_shapes=ex.sems, name=name)(*ex.ins)


def _rowcall(name, body, seq, tb, rows, consts, row_outs, acc_outs, scratch=(), reverse=False, vmem=None,
             exchange=None):
    nb = seq // tb
    rmap = (lambda i: (nb - 1 - i, 0)) if reverse else (lambda i: (i, 0))
    in_specs = [pl.BlockSpec((tb, a.shape[1]), rmap) for a in rows]
    in_specs += [pl.BlockSpec(a.shape, functools.partial(_zero_map, a.ndim), pipeline_mode=pl.Buffered(1))
                 for a in consts]
    out_specs = [pl.BlockSpec((tb, c), rmap) for c, _ in row_outs]
    out_specs += [pl.BlockSpec(s, functools.partial(_zero_map, len(s))) for s, _ in acc_outs]
    out_shape = [SDS((seq, c), dt) for c, dt in row_outs] + [SDS(s, dt) for s, dt in acc_outs]
    return _fused_call(name, body, (nb,), in_specs, out_specs, out_shape, list(scratch), [*rows, *consts],
                       exchange, _params(1, vmem))


def _inproj_fwd(x, g1, w_in, tb, exchange=None):
    seq = x.shape[0]

    def body(x_ref, g_ref, w_ref, h_ref, q_ref, k_ref, v_ref, u_ref, ga_ref, gs_ref):
        h, _, _ = _rms(x_ref[...], g_ref[...])
        hb = _bf(h)
        h_ref[...] = hb
        pj = _mm_nt(hb, w_ref[...])
        q_ref[...] = _bf(pj[:, SPLITS[0]:SPLITS[1]])
        k_ref[...] = _bf(pj[:, SPLITS[1]:SPLITS[2]])
        v_ref[...] = _bf(pj[:, SPLITS[2]:SPLITS[3]])
        u_ref[...] = pj[:, SPLITS[3]:SPLITS[4]]
        ga_ref[...] = pj[:, SPLITS[4]:SPLITS[5]]
        gs_ref[...] = pj[:, SPLITS[5]:SPLITS[6]]

    return _rowcall("inproj_fwd", body, seq, tb, [x], [g1, w_in],
                    [(D_MODEL, BF16), (ATTN_W, BF16), (KV_W, BF16), (KV_W, BF16), (SSM_W, F32),
                     (D_MODEL, F32), (D_MODEL, F32)], [], vmem=VMEM_BIG, exchange=exchange)


def _inproj_bwd(x, dx2, dq, dk, dv, du, dga, dgs, g1, w_in, tb, exchange=None):
    seq = x.shape[0]

    def body(x_ref, dx2_ref, dq_ref, dk_ref, dv_ref, du_ref, dga_ref, dgs_ref, g_ref, w_ref,
             dx_ref, dpj_ref, dg_ref):
        @pl.when(pl.program_id(0) == 0)
        def _():
            dg_ref[...] = jnp.zeros_like(dg_ref)

        dpj = jnp.concatenate([dq_ref[...], dk_ref[...], dv_ref[...], _bf(du_ref[...]),
                               dga_ref[...], dgs_ref[...]], axis=1)
        dpj_ref[...] = dpj
        dh = _mm(dpj, w_ref[...])
        g = g_ref[...]
        _, xh, r = _rms(x_ref[...], g)
        dxn, dg = _rms_bwd(dh, xh, r, g)
        dx_ref[...] = dx2_ref[...] + dxn
        dg_ref[...] += dg

    return _rowcall("inproj_bwd", body, seq, tb, [x, dx2, dq, dk, dv, du, dga, dgs], [g1, w_in],
                    [(D_MODEL, F32), (IN_W, BF16)], [((1, D_MODEL), F32)], vmem=VMEM_BIG, exchange=exchange)


def _bucket_table():
    qi = np.arange(BLOCK)[:, None]
    kj = np.arange(2 * BLOCK)[None, :]
    dist = qi + BLOCK - kj
    max_exact = N_BUCKETS // 2
    d = np.maximum(dist, 0)
    df = np.maximum(d, 1).astype(np.float32)
    large = max_exact + (np.log(df / np.float32(max_exact)) / np.float32(math.log(MAX_DISTANCE / max_exact))
                         * np.float32(N_BUCKETS - max_exact)).astype(np.int32)
    large = np.minimum(large, N_BUCKETS - 1)
    bucket = np.where(d < max_exact, d, large)
    valid = (dist >= 0) & (dist < BLOCK)
    return np.where(valid, bucket, -1).astype(np.int32)


def _bias_table(rel_bias, bucket):
    def body(rb_ref, bk_ref, o_ref):
        bk = bk_ref[...]
        has_prev = lax.broadcasted_iota(jnp.int32, bk.shape, 1) >= BLOCK
        for h in range(N_HEADS):
            acc = jnp.full((BLOCK, 2 * BLOCK), NEG_INF, F32)
            for b in range(N_BUCKETS):
                acc = jnp.where(bk == b, rb_ref[b, h], acc)
            o_ref[0, h] = jnp.where(has_prev, acc, NEG_INF)
            o_ref[1, h] = acc

    return pl.pallas_call(
        body, out_shape=SDS((2, N_HEADS, BLOCK, 2 * BLOCK), F32),
        in_specs=[pl.BlockSpec(memory_space=pltpu.SMEM), pl.BlockSpec(memory_space=pltpu.VMEM)],
        out_specs=pl.BlockSpec(memory_space=pltpu.VMEM), name="bias_table",
    )(rel_bias, bucket)


def _bias_grad(dbias, bucket):
    def body(db_ref, bk_ref, o_ref):
        bk = bk_ref[...]
        for h in range(N_HEADS):
            db = db_ref[h]
            for b in range(N_BUCKETS):
                o_ref[b, h] = jnp.sum(jnp.where(bk == b, db, 0.0))

    return pl.pallas_call(
        body, out_shape=SDS((N_BUCKETS, N_HEADS), F32),
        in_specs=[pl.BlockSpec(memory_space=pltpu.VMEM), pl.BlockSpec(memory_space=pltpu.VMEM)],
        out_specs=pl.BlockSpec(memory_space=pltpu.SMEM), name="bias_grad",
    )(dbias, bucket)


TILE = 2 * HEAD_DIM


def _pair_layout(t):
    lead = t.shape[:-3]
    t = t.reshape(lead + (N_KV, 2, 2) + t.shape[-2:])
    nl = len(lead)
    t = jnp.transpose(t, tuple(range(nl)) + (nl, nl + 2, nl + 1, nl + 3, nl + 4))
    return t.reshape(lead + (N_KV, 2, 2 * BLOCK, t.shape[-1]))


def _pair_unlayout(t):
    t = t.reshape(N_KV, 2, 2, BLOCK, t.shape[-1]).transpose(0, 2, 1, 3, 4)
    return t.reshape(N_HEADS, BLOCK, t.shape[-1])


def _halves(t):
    tf = t.astype(F32)
    low = lax.broadcasted_iota(jnp.int32, tf.shape, 1) < HEAD_DIM
    swapped = pltpu.roll(tf, HEAD_DIM, 1)
    zero = jnp.zeros_like(tf)
    return ((_bf(jnp.where(low, tf, zero)), _bf(jnp.where(low, zero, swapped))),
            (_bf(jnp.where(low, swapped, zero)), _bf(jnp.where(low, zero, tf))))


def _fold_halves(even, odd):
    low = lax.broadcasted_iota(jnp.int32, even.shape, 1) < HEAD_DIM
    comb = jnp.where(low, even, odd)
    return comb + pltpu.roll(comb, HEAD_DIM, 1)


def _tile_rows(ref, kh):
    return jnp.concatenate([ref[:, (2 * kh) * TILE:(2 * kh + 1) * TILE],
                            ref[:, (2 * kh + 1) * TILE:(2 * kh + 2) * TILE]], axis=0)


def _attn_probs(qk, km, bias, sink):
    lg = _mm_nt(qk, km) * (HEAD_DIM ** -0.5) + bias
    m = jnp.maximum(jnp.max(lg, axis=-1, keepdims=True), sink)
    p = jnp.exp(lg - m)
    es = jnp.exp(sink - m)
    inv = 1.0 / (jnp.sum(p, axis=-1, keepdims=True) + es)
    return p * inv, es * inv


def _attn_fwd(q, k, v, bias, sink_rows, exchange=None):
    seq = q.shape[0]
    nblk = seq // BLOCK

    def body(q_ref, kp_ref, kc_ref, vp_ref, vc_ref, b_ref, s_ref, o_ref):
        which = jnp.minimum(pl.program_id(0), 1)
        kms = _halves(jnp.concatenate([kp_ref[...], kc_ref[...]], axis=0))
        vms = _halves(jnp.concatenate([vp_ref[...], vc_ref[...]], axis=0))
        for kh in range(N_KV):
            qk = _tile_rows(q_ref, kh)
            acc = jnp.zeros((2 * BLOCK, TILE), F32)
            for par in range(2):
                pr, _ = _attn_probs(qk, kms[kh][par], b_ref[which, kh, par], s_ref[kh, par])
                acc = acc + _mm(_bf(pr), vms[kh][par])
            o_ref[:, (2 * kh) * TILE:(2 * kh + 1) * TILE] = _bf(acc[:BLOCK])
            o_ref[:, (2 * kh + 1) * TILE:(2 * kh + 2) * TILE] = _bf(acc[BLOCK:])

    cur = lambda n: (n, 0)
    prev = lambda n: (jnp.maximum(n - 1, 0), 0)
    return _fused_call(
        "attn_fwd", body, (nblk,),
        [pl.BlockSpec((BLOCK, ATTN_W), cur),
         pl.BlockSpec((BLOCK, KV_W), prev), pl.BlockSpec((BLOCK, KV_W), cur),
         pl.BlockSpec((BLOCK, KV_W), prev), pl.BlockSpec((BLOCK, KV_W), cur),
         pl.BlockSpec(bias.shape, functools.partial(_zero_map, bias.ndim)),
         pl.BlockSpec(sink_rows.shape, functools.partial(_zero_map, sink_rows.ndim))],
        [pl.BlockSpec((BLOCK, ATTN_W), cur)], [SDS((seq, ATTN_W), BF16)], [],
        [q, k, k, v, v, bias, sink_rows], exchange, _params(1))


def _attn_bwd(q, k, v, d_out, bias, sink_rows, exchange=None):
    seq = q.shape[0]
    nblk = seq // BLOCK

    def body(q_ref, kp_ref, kc_ref, vp_ref, vc_ref, do_ref, b_ref, s_ref,
             dq_ref, dk_ref, dv_ref, db_ref, ds_ref, ck_ref, cv_ref):
        n = pl.program_id(0)

        @pl.when(n == 0)
        def _():
            db_ref[...] = jnp.zeros_like(db_ref)
            ds_ref[...] = jnp.zeros_like(ds_ref)
            ck_ref[...] = jnp.zeros_like(ck_ref)
            cv_ref[...] = jnp.zeros_like(cv_ref)

        @pl.when(n < nblk)
        def _():
            which = jnp.minimum(n, 1)
            scale = HEAD_DIM ** -0.5
            kms = _halves(jnp.concatenate([kp_ref[...], kc_ref[...]], axis=0))
            vms = _halves(jnp.concatenate([vp_ref[...], vc_ref[...]], axis=0))
            dks, dvs = [], []
            for kh in range(N_KV):
                qk = _tile_rows(q_ref, kh)
                dok = _tile_rows(do_ref, kh)
                dq = jnp.zeros((2 * BLOCK, TILE), F32)
                dkp, dvp = [], []
                for par in range(2):
                    pr, ps = _attn_probs(qk, kms[kh][par], b_ref[which, kh, par], s_ref[kh, par])
                    dp = _mm_nt(dok, vms[kh][par])
                    rs = jnp.sum(pr * dp, axis=-1, keepdims=True)
                    dlg = pr * (dp - rs)
                    ds_ref[kh, par] += -ps * rs
                    db_ref[kh, par] += dlg
                    dlb = _bf(dlg)
                    dq = dq + _mm(dlb, kms[kh][par])
                    dkp.append(_mm_tn(dlb, qk))
                    dvp.append(_mm_tn(_bf(pr), dok))
                dq = _bf(dq * scale)
                dq_ref[:, (2 * kh) * TILE:(2 * kh + 1) * TILE] = dq[:BLOCK]
                dq_ref[:, (2 * kh + 1) * TILE:(2 * kh + 2) * TILE] = dq[BLOCK:]
                dks.append(_fold_halves(*dkp))
                dvs.append(_fold_halves(*dvp))
            low = lax.broadcasted_iota(jnp.int32, (2 * BLOCK, TILE), 1) < HEAD_DIM
            dkk = jnp.where(low, dks[0], dks[1]) * scale
            dvv = jnp.where(low, dvs[0], dvs[1])
            dk_ref[...] = _bf(ck_ref[...] + dkk[:BLOCK])
            ck_ref[...] = dkk[BLOCK:]
            dv_ref[...] = _bf(cv_ref[...] + dvv[:BLOCK])
            cv_ref[...] = dvv[BLOCK:]

        @pl.when(n == nblk)
        def _():
            dk_ref[...] = _bf(ck_ref[...])
            dv_ref[...] = _bf(cv_ref[...])

    cur = lambda n: (jnp.minimum(n, nblk - 1), 0)
    prev = lambda n: (jnp.maximum(jnp.minimum(n, nblk - 1) - 1, 0), 0)
    late = lambda n: (jnp.maximum(n - 1, 0), 0)
    kv_spec = lambda m: pl.BlockSpec((BLOCK, KV_W), m)
    acc_b = pl.BlockSpec(bias.shape[1:], functools.partial(_zero_map, bias.ndim - 1))
    acc_s = pl.BlockSpec(sink_rows.shape, functools.partial(_zero_map, sink_rows.ndim))
    return _fused_call(
        "attn_bwd", body, (nblk + 1,),
        [pl.BlockSpec((BLOCK, ATTN_W), cur), kv_spec(prev), kv_spec(cur), kv_spec(prev), kv_spec(cur),
         pl.BlockSpec((BLOCK, ATTN_W), cur),
         pl.BlockSpec(bias.shape, functools.partial(_zero_map, bias.ndim)), acc_s],
        [pl.BlockSpec((BLOCK, ATTN_W), cur), kv_spec(late), kv_spec(late), acc_b, acc_s],
        [SDS((seq, ATTN_W), BF16), SDS((seq, KV_W), BF16), SDS((seq, KV_W), BF16),
         SDS(bias.shape[1:], F32), SDS(sink_rows.shape, F32)],
        [pltpu.VMEM((BLOCK, KV_W), F32), pltpu.VMEM((BLOCK, KV_W), F32)],
        [q, k, k, v, v, d_out, bias, sink_rows], exchange, _params(1))


def _ssm_discretize(lam_re, lam_im, log_dt, b_re, b_im):
    dt = jnp.exp(log_dt)[:, None]
    mag = jnp.exp(lam_re * dt)
    ab_re = mag * jnp.cos(lam_im * dt)
    ab_im = mag * jnp.sin(lam_im * dt)
    nr = ab_re - 1.0
    den = lam_re * lam_re + lam_im * lam_im
    f_re = (nr * lam_re + ab_im * lam_im) / den
    f_im = (ab_im * lam_re - nr * lam_im) / den
    bb_re = f_re[..., None] * b_re - f_im[..., None] * b_im
    bb_im = f_re[..., None] * b_im + f_im[..., None] * b_re
    return ab_re, ab_im, bb_re, bb_im


def _state_layout(re, im):
    z = jnp.stack([re, im]).reshape(2, N_SUPER, GROUPS_PER_SUPER, SSM_STATE)
    return z.transpose(1, 0, 2, 3).reshape(STATE_COLS)


def _state_unlayout(vec):
    z = vec.reshape(N_SUPER, 2, GROUPS_PER_SUPER, SSM_STATE).transpose(1, 0, 2, 3)
    z = z.reshape(2, SSM_GROUPS, SSM_STATE)
    return z[0], z[1]


def _scan_tables(ab_re, ab_im):
    pw = [None, (ab_re, ab_im)]
    for _ in range(2, SUBLANES + 1):
        pr, pi_ = pw[-1]
        pw.append((pr * ab_re - pi_ * ab_im, pr * ab_im + pi_ * ab_re))
    rows = np.arange(SUBLANES)[:, None]
    fwd, bwd = [], []
    for shift in (1, 2, 4):
        fwd.append(_state_layout(*pw[shift])[None, :] * (rows >= shift).astype(np.float32))
        bwd.append(_state_layout(pw[shift][0], -pw[shift][1])[None, :] * (rows < SUBLANES - shift).astype(np.float32))
    fwd.append(jnp.stack([_state_layout(*pw[r + 1]) for r in range(SUBLANES)]))
    bwd.append(jnp.stack([_state_layout(pw[SUBLANES - r][0], -pw[SUBLANES - r][1]) for r in range(SUBLANES)]))
    return jnp.stack(fwd), jnp.stack(bwd)


_EYE = np.eye(GROUPS_PER_SUPER, dtype=np.float32)


def _b_matrix(bb_re, bb_im):
    bb = jnp.stack([bb_re, bb_im]).reshape(2, N_SUPER, GROUPS_PER_SUPER, SSM_STATE, SSM_GROUP)
    m = jnp.einsum('rsgpc,gh->sgcrhp', bb, _EYE)
    return m.reshape(N_SUPER, SUPER_IN, SUPER_W)


def _b_matrix_grad(dm):
    d = dm.reshape(N_SUPER, GROUPS_PER_SUPER, SSM_GROUP, 2, GROUPS_PER_SUPER, SSM_STATE)
    d = jnp.sum(d * _EYE[None, :, None, None, :, None], axis=4)
    d = d.transpose(3, 0, 1, 4, 2).reshape(2, SSM_GROUPS, SSM_STATE, SSM_GROUP)
    return d[0], d[1]


def _c_matrix(c_re, c_im):
    cc = jnp.stack([c_re, -c_im]).reshape(2, N_SUPER, GROUPS_PER_SUPER, SSM_GROUP, SSM_STATE)
    m = jnp.einsum('rsgcp,gh->srgphc', cc, _EYE)
    return m.reshape(N_SUPER, SUPER_W, SUPER_IN)


def _c_matrix_grad(dm):
    d = dm.reshape(N_SUPER, 2, GROUPS_PER_SUPER, SSM_STATE, GROUPS_PER_SUPER, SSM_GROUP)
    d = jnp.sum(d * _EYE[None, None, :, None, :, None], axis=4)
    d = d.transpose(1, 0, 2, 4, 3).reshape(2, SSM_GROUPS, SSM_GROUP, SSM_STATE)
    return d[0], -d[1]


def _scan_rows(buf_ref, tab_ref, carry_ref, n_groups, reverse, h_ref=None, da_ref=None):
    edge = 0 if reverse else SUBLANES - 1
    for sb in range(N_SUPER):
        cr = pl.ds(sb * SUPER_W, SUPER_HALF)
        ci = pl.ds(sb * SUPER_W + SUPER_HALF, SUPER_HALF)

        def step(gi, carry, cr=cr, ci=ci):
            g = (n_groups - 1 - gi) if reverse else gi
            rows = pl.ds(pl.multiple_of(g * SUBLANES, SUBLANES), SUBLANES)
            c_re, c_im = carry[0], carry[1]
            xr = buf_ref[rows, cr]
            xi = buf_ref[rows, ci]
            for k, shift in enumerate((1, 2, 4)):
                s = (SUBLANES - shift) if reverse else shift
                sr = pltpu.roll(xr, s, 0)
                si = pltpu.roll(xi, s, 0)
                ar = tab_ref[k, :, cr]
                ai = tab_ref[k, :, ci]
                xr, xi = xr + ar * sr - ai * si, xi + ar * si + ai * sr
            pr = tab_ref[3, :, cr]
            pi_ = tab_ref[3, :, ci]
            xr, xi = xr + pr * c_re - pi_ * c_im, xi + pr * c_im + pi_ * c_re
            buf_ref[rows, cr] = xr
            buf_ref[rows, ci] = xi
            out = [jnp.broadcast_to(xr[edge:edge + 1], xr.shape), jnp.broadcast_to(xi[edge:edge + 1], xi.shape)]
            if h_ref is not None:
                last = lax.broadcasted_iota(jnp.int32, xr.shape, 0) == SUBLANES - 1
                gr = jnp.where(last, c_re, pltpu.roll(xr, SUBLANES - 1, 0))
                gim = jnp.where(last, c_im, pltpu.roll(xi, SUBLANES - 1, 0))
                hr = h_ref[rows, cr]
                hi = h_ref[rows, ci]
                out += [carry[2] + gr * hr + gim * hi, carry[3] + gim * hr - gr * hi]
            return tuple(out)

        init = [carry_ref[:, cr], carry_ref[:, ci]]
        if h_ref is not None:
            init += [da_ref[:, cr], da_ref[:, ci]]
        fin = lax.fori_loop(0, n_groups, step, tuple(init))
        carry_ref[:, cr] = fin[0]
        carry_ref[:, ci] = fin[1]
        if h_ref is not None:
            da_ref[:, cr] = fin[2]
            da_ref[:, ci] = fin[3]


def _ssm_fwd(u, bmat, cmat, tab, d_skip, tb, exchange=None):
    seq = u.shape[0]

    def body(u_ref, b_ref, c_ref, t_ref, d_ref, s_ref, h_ref, carry_ref):
        @pl.when(pl.program_id(0) == 0)
        def _():
            carry_ref[...] = jnp.zeros_like(carry_ref)

        u_blk = u_ref[...]
        ub = _bf(u_blk)
        for sb in range(N_SUPER):
            h_ref[:, sb * SUPER_W:(sb + 1) * SUPER_W] = _mm(ub[:, sb * SUPER_IN:(sb + 1) * SUPER_IN], b_ref[sb])
        _scan_rows(h_ref, t_ref, carry_ref, tb // SUBLANES, False)
        ys = [_mm(_bf(h_ref[:, sb * SUPER_W:(sb + 1) * SUPER_W]), c_ref[sb]) for sb in range(N_SUPER)]
        s_ref[...] = jnp.concatenate(ys, axis=1) + d_ref[...] * u_blk

    return _rowcall("ssm_fwd", body, seq, tb, [u], [bmat, cmat, tab, d_skip],
                    [(SSM_W, F32), (STATE_COLS, F32)], [],
                    scratch=[pltpu.VMEM((SUBLANES, STATE_COLS), F32)], vmem=VMEM_BIG, exchange=exchange)


def _ssm_bwd(ds, u, h, bmat_t, cmat_t, tab, d_skip, tb, exchange=None):
    seq = u.shape[0]

    def body(ds_ref, u_ref, h_ref, bt_ref, ct_ref, t_ref, d_ref,
             du_ref, db_ref, dc_ref, da_ref, dd_ref, g_ref, carry_ref):
        @pl.when(pl.program_id(0) == 0)
        def _():
            carry_ref[...] = jnp.zeros_like(carry_ref)
            db_ref[...] = jnp.zeros_like(db_ref)
            dc_ref[...] = jnp.zeros_like(dc_ref)
            da_ref[...] = jnp.zeros_like(da_ref)
            dd_ref[...] = jnp.zeros_like(dd_ref)

        ds_blk = ds_ref[...]
        dsb = _bf(ds_blk)
        u_blk = u_ref[...]
        ub = _bf(u_blk)
        for sb in range(N_SUPER):
            g_ref[:, sb * SUPER_W:(sb + 1) * SUPER_W] = _mm(dsb[:, sb * SUPER_IN:(sb + 1) * SUPER_IN], ct_ref[sb])
        _scan_rows(g_ref, t_ref, carry_ref, tb // SUBLANES, True, h_ref=h_ref, da_ref=da_ref)
        dus = []
        for sb in range(N_SUPER):
            gb = _bf(g_ref[:, sb * SUPER_W:(sb + 1) * SUPER_W])
            dus.append(_mm(gb, bt_ref[sb]))
            db_ref[sb] += _mm_tn(ub[:, sb * SUPER_IN:(sb + 1) * SUPER_IN], gb)
            dc_ref[sb] += _mm_tn(_bf(h_ref[:, sb * SUPER_W:(sb + 1) * SUPER_W]),
                                 dsb[:, sb * SUPER_IN:(sb + 1) * SUPER_IN])
        du_ref[...] = jnp.concatenate(dus, axis=1) + d_ref[...] * ds_blk
        dd_ref[...] += jnp.sum(ds_blk * u_blk, axis=0, keepdims=True)

    return _rowcall("ssm_bwd", body, seq, tb, [ds, u, h], [bmat_t, cmat_t, tab, d_skip],
                    [(SSM_W, F32)],
                    [((N_SUPER, SUPER_IN, SUPER_W), F32), ((N_SUPER, SUPER_W, SUPER_IN), F32),
                     ((SUBLANES, STATE_COLS), F32), ((1, SSM_W), F32)],
                    scratch=[pltpu.VMEM((tb, STATE_COLS), F32), pltpu.VMEM((SUBLANES, STATE_COLS), F32)],
                    reverse=True, vmem=VMEM_BIG, exchange=exchange)


def _merge_core(s, attb, ga, gs, wg_ref, wab_ref, wsb_ref, wout_ref):
    zg, dgelu = _gelu_and_grad(s)
    zgb = _bf(zg)
    sg = _sig(_mm(zgb, wg_ref[...]))
    z = zg * sg
    zb = _bf(z)
    ys = jnp.concatenate([_mm(zb, wsb_ref[j]) for j in range(N_CHIPS)], axis=1)
    ya = jnp.concatenate([_mm(attb, wab_ref[j]) for j in range(N_CHIPS)], axis=1)
    sa = _sig(ga)
    ss = _sig(gs)
    mgb = _bf(sa * ya + ss * ys)
    o = _mm(mgb, wout_ref[...])
    return dict(zg=zg, dgelu=dgelu, zgb=zgb, sg=sg, zb=zb, ys=ys, ya=ya, sa=sa, ss=ss, mgb=mgb, o=o)


def _merge_fwd(x, s, att, ga, gs, g2, w_glu, w_ab, w_sb, w_out, tb):
    seq = x.shape[0]

    def body(x_ref, s_ref, att_ref, ga_ref, gs_ref, g_ref, wg_ref, wab_ref, wsb_ref, wout_ref, x2_ref):
        f = _merge_core(s_ref[...], att_ref[...], ga_ref[...], gs_ref[...], wg_ref, wab_ref, wsb_ref, wout_ref)
        n, _, _ = _rms(f["o"], g_ref[...])
        x2_ref[...] = x_ref[...] + n

    return _rowcall("merge_fwd", body, seq, tb, [x, s, att, ga, gs], [g2, w_glu, w_ab, w_sb, w_out],
                    [(D_MODEL, F32)], [], vmem=VMEM_BIG)[0]


def _merge_bwd(dx2, s, att, ga, gs, g2, w_glu, w_ab, w_sb, w_out, tb, exchange=None):
    seq = s.shape[0]
    cw = D_MODEL // N_CHIPS
    last = seq // tb - 1

    def body(dx2_ref, s_ref, att_ref, ga_ref, gs_ref, g_ref, wg_ref, wab_ref, wsb_ref, wout_ref,
             ds_ref, datt_ref, dga_ref, dgs_ref, dg_ref, dwg_ref, dwab_ref, dwsb_ref, dwout_ref,
             bwg_ref, bwab_ref, bwsb_ref, bwout_ref):
        @pl.when(pl.program_id(0) == 0)
        def _():
            for r in (dg_ref, dwg_ref, dwab_ref, dwsb_ref, dwout_ref):
                r[...] = jnp.zeros_like(r)

        attb = att_ref[...]
        f = _merge_core(s_ref[...], attb, ga_ref[...], gs_ref[...], wg_ref, wab_ref, wsb_ref, wout_ref)
        g = g_ref[...]
        _, oh, r2 = _rms(f["o"], g)
        do, dg = _rms_bwd(dx2_ref[...], oh, r2, g)
        dg_ref[...] += dg
        dob = _bf(do)
        dwout_ref[...] += _mm_tn(f["mgb"], dob)
        dmg = _mm_nt(dob, wout_ref[...])
        sa, ss = f["sa"], f["ss"]
        dyab = _bf(dmg * sa)
        dysb = _bf(dmg * ss)
        dga_ref[...] = _bf(dmg * f["ya"] * sa * (1.0 - sa))
        dgs_ref[...] = _bf(dmg * f["ys"] * ss * (1.0 - ss))
        dwab = _mm_tn(attb, dyab)
        dwsb = _mm_tn(f["zb"], dysb)
        datt = jnp.zeros((tb, ATTN_W), F32)
        dz = jnp.zeros((tb, SSM_W), F32)
        for j in range(N_CHIPS):
            dwab_ref[j] += dwab[:, j * cw:(j + 1) * cw]
            dwsb_ref[j] += dwsb[:, j * cw:(j + 1) * cw]
            datt = datt + _mm_nt(dyab[:, j * cw:(j + 1) * cw], wab_ref[j])
            dz = dz + _mm_nt(dysb[:, j * cw:(j + 1) * cw], wsb_ref[j])
        datt_ref[...] = _bf(datt)
        sg, zg = f["sg"], f["zg"]
        dglb = _bf(dz * zg * sg * (1.0 - sg))
        dwg_ref[...] += _mm_tn(f["zgb"], dglb)
        dzg = dz * sg + _mm_nt(dglb, wg_ref[...])
        ds_ref[...] = dzg * f["dgelu"]

        @pl.when(pl.program_id(0) == last)
        def _():
            for dst, src in ((bwg_ref, dwg_ref), (bwab_ref, dwab_ref), (bwsb_ref, dwsb_ref), (bwout_ref, dwout_ref)):
                dst[...] = _bf(src[...])

    shapes = [w_glu.shape, w_ab.shape, w_sb.shape, w_out.shape]
    return _rowcall("merge_bwd", body, seq, tb, [dx2, s, att, ga, gs], [g2, w_glu, w_ab, w_sb, w_out],
                    [(SSM_W, F32), (ATTN_W, BF16), (D_MODEL, BF16), (D_MODEL, BF16)],
                    [((1, D_MODEL), F32)] + [(sh, F32) for sh in shapes] + [(sh, BF16) for sh in shapes],
                    vmem=VMEM_BIG, exchange=exchange)


def _mlp_fwd_loss(x2, target, g3, g4, w_ffi, w_ffo, tb):
    seq = x2.shape[0]

    def body(x2_ref, t_ref, g3_ref, g4_ref, wi_ref, wo_ref, dy_ref, df_ref, h_ref, loss_ref, dg_ref):
        @pl.when(pl.program_id(0) == 0)
        def _():
            loss_ref[...] = jnp.zeros_like(loss_ref)
            dg_ref[...] = jnp.zeros_like(dg_ref)

        x2_blk = x2_ref[...]
        h3, _, _ = _rms(x2_blk, g3_ref[...])
        hb = _bf(h3)
        h_ref[...] = hb
        f = jnp.zeros((tb, D_MODEL), F32)
        for j in range(FF_CHUNKS):
            a = _mm(hb, wi_ref[j])
            f = f + _mm(_bf(jnp.square(jnp.maximum(a, 0.0))), wo_ref[j])
        g4 = g4_ref[...]
        n4, fh, r4 = _rms(f, g4)
        e = (x2_blk + n4) - t_ref[...]
        loss_ref[...] += 0.5 * jnp.sum(jnp.mean(e * e, axis=-1, keepdims=True))
        dy = e * (1.0 / D_MODEL)
        dy_ref[...] = dy
        df, dg = _rms_bwd(dy, fh, r4, g4)
        df_ref[...] = _bf(df)
        dg_ref[...] += dg

    return _rowcall("mlp_fwd_loss", body, seq, tb, [x2, target], [g3, g4, w_ffi, w_ffo],
                    [(D_MODEL, F32), (D_MODEL, BF16), (D_MODEL, BF16)],
                    [((SUBLANES, 128), F32), ((1, D_MODEL), F32)], vmem=VMEM_BIG)


def _mlp_bwd(x2, dy, df, h3, g3, w_ffi, w_ffo, tb):
    seq = x2.shape[0]
    cw = D_FF // FF_CHUNKS

    def body(x2_ref, dy_ref, df_ref, h_ref, g3_ref, wi_ref, wo_ref, dx_ref, act_ref, da_ref, dg_ref):
        @pl.when(pl.program_id(0) == 0)
        def _():
            dg_ref[...] = jnp.zeros_like(dg_ref)

        hb = h_ref[...]
        dfb = df_ref[...]
        dh = jnp.zeros((tb, D_MODEL), F32)
        for j in range(FF_CHUNKS):
            ra = jnp.maximum(_mm(hb, wi_ref[j]), 0.0)
            act_ref[:, j * cw:(j + 1) * cw] = _bf(ra * ra)
            dab = _bf(_mm_nt(dfb, wo_ref[j]) * (2.0 * ra))
            da_ref[:, j * cw:(j + 1) * cw] = dab
            dh = dh + _mm_nt(dab, wi_ref[j])
        g3 = g3_ref[...]
        _, xh, r3 = _rms(x2_ref[...], g3)
        dxn, dg = _rms_bwd(dh, xh, r3, g3)
        dx_ref[...] = dy_ref[...] + dxn
        dg_ref[...] += dg

    return _rowcall("mlp_bwd", body, seq, tb, [x2, dy, df, h3], [g3, w_ffi, w_ffo],
                    [(D_MODEL, F32), (D_FF, BF16), (D_FF, BF16)], [((1, D_MODEL), F32)], vmem=VMEM_BIG)


def _matmul_tn(name, a, b, tk, tn, tl, chunk_major, exchange=None):
    seq, kdim = a.shape
    ndim = b.shape[1]
    last = seq // tl - 1

    def body(a_ref, b_ref, o_ref, ob_ref):
        @pl.when(pl.program_id(2) == 0)
        def _():
            o_ref[...] = jnp.zeros_like(o_ref)

        o_ref[...] += _mm_tn(a_ref[...], b_ref[...])

        @pl.when(pl.program_id(2) == last)
        def _():
            ob_ref[...] = _bf(o_ref[...])

    if chunk_major:
        shape = (ndim // tn, kdim, tn)
        out_spec = pl.BlockSpec((None, tk, tn), lambda k, n, l: (n, k, 0))
    else:
        shape = (kdim, ndim)
        out_spec = pl.BlockSpec((tk, tn), lambda k, n, l: (k, n))
    return _fused_call(
        name, body, (kdim // tk, ndim // tn, seq // tl),
        [pl.BlockSpec((tl, tk), lambda k, n, l: (l, k)), pl.BlockSpec((tl, tn), lambda k, n, l: (l, n))],
        [out_spec, out_spec], [SDS(shape, F32), SDS(shape, BF16)], [], [a, b], exchange, _params(3, VMEM_BIG))


def _ew_call(name, fn, ins, n_out):
    rows, cols = ins[0].shape
    tr = rows
    while tr * cols * 4 > (1 << 20) and tr % 16 == 0:
        tr //= 2
    spec = pl.BlockSpec((tr, cols), lambda i: (i, 0))

    def body(*refs):
        outs = fn(*[r[...] for r in refs[:len(ins)]])
        for r, o in zip(refs[len(ins):], outs):
            r[...] = o

    return pl.pallas_call(
        body, grid=(rows // tr,), in_specs=[spec] * len(ins), out_specs=[spec] * n_out,
        out_shape=[SDS((rows, cols), F32)] * n_out, name=name, compiler_params=_params(1),
    )(*ins)


def _adam_math(w, g, m, v):
    m2 = ADAM_B1 * m + (1.0 - ADAM_B1) * g
    v2 = ADAM_B2 * v + (1.0 - ADAM_B2) * (g * g)
    m_hat = m2 / (1.0 - ADAM_B1 ** ADAM_STEP)
    v_hat = v2 / (1.0 - ADAM_B2 ** ADAM_STEP)
    delta = -ADAM_LR * (m_hat / (jnp.sqrt(v_hat) + ADAM_EPS) + ADAM_WD * w)
    return delta, m2, v2


def _sum4(name, own, recv, idx):
    _, rows, cols = own.shape
    tr = rows
    while tr * cols * 4 > (1 << 20) and tr % 16 == 0:
        tr //= 2

    def body(idx_ref, o_ref, r0_ref, r1_ref, r2_ref, out_ref):
        out_ref[...] = ((o_ref[...] + r0_ref[...].astype(F32)) + r1_ref[...].astype(F32)) + r2_ref[...].astype(F32)

    blk = (None, tr, cols)
    grid_spec = pltpu.PrefetchScalarGridSpec(
        num_scalar_prefetch=1, grid=(rows // tr,),
        in_specs=[pl.BlockSpec(blk, lambda i, s: (s[0], i, 0)), pl.BlockSpec(blk, lambda i, s: (0, i, 0)),
                  pl.BlockSpec(blk, lambda i, s: (1, i, 0)), pl.BlockSpec(blk, lambda i, s: (2, i, 0))],
        out_specs=pl.BlockSpec((tr, cols), lambda i, s: (i, 0)))
    return pl.pallas_call(body, grid_spec=grid_spec, out_shape=SDS((rows, cols), F32), name=name,
                          compiler_params=_params(1))(jnp.reshape(idx, (1,)).astype(jnp.int32), own, recv, recv, recv)


def _adam_pair(name, w, p_own, p_sib, m, v):
    def fn(w_, a, b, m_, v_):
        g = a + b
        return (g,) + _adam_math(w_, g, m_, v_)

    return _ew_call(name, fn, [w, p_own, p_sib, m, v], 4)


def _place():
    return lax.axis_index("x"), lax.axis_index("y"), lax.axis_index("c")


def _other_chips(x, y):
    return [(1 - x, y), (x, 1 - y), (1 - x, 1 - y)]


def _gather_chips(shards):
    n = len(shards)

    def copies(ins, outs, sems):
        send, recv, loc = sems
        x, y, c = _place()
        me = 2 * x + y
        peers = _other_chips(x, y)
        local = [pltpu.make_async_copy(ins[a], outs[a].at[me], loc.at[a]) for a in range(n)]

        def copy(a, j, slot):
            px, py = peers[j]
            return pltpu.make_async_remote_copy(
                src_ref=ins[a], dst_ref=outs[a].at[slot], send_sem=send.at[a, j], recv_sem=recv.at[a, j],
                device_id=(px, py, c), device_id_type=MESH_ID)

        sends = [copy(a, j, me) for a in range(n) for j in range(3)]
        recvs = [copy(a, j, 2 * px + py) for a in range(n) for j, (px, py) in enumerate(peers)]
        return local, sends, recvs

    def start(ins, outs, sems):
        local, sends, _ = copies(ins, outs, sems)
        for cp in local + sends:
            cp.start()

    def wait(ins, outs, sems):
        local, sends, recvs = copies(ins, outs, sems)
        for cp in recvs:
            cp.wait_recv()
        for cp in sends:
            cp.wait_send()
        for cp in local:
            cp.wait()

    return _Exchange(shards, [SDS((N_CHIPS,) + s.shape, s.dtype) for s in shards],
                     [pltpu.SemaphoreType.DMA((n, 3)), pltpu.SemaphoreType.DMA((n, 3)), pltpu.SemaphoreType.DMA((n,))],
                     start, wait)


def _scatter_chips(chunks):
    n = len(chunks)

    def copies(ins, outs, sems):
        send, recv = sems
        x, y, c = _place()
        return [pltpu.make_async_remote_copy(
            src_ref=ins[a].at[2 * px + py], dst_ref=outs[a].at[j], send_sem=send.at[a, j],
            recv_sem=recv.at[a, j], device_id=(px, py, c), device_id_type=MESH_ID)
            for a in range(n) for j, (px, py) in enumerate(_other_chips(x, y))]

    def start(ins, outs, sems):
        for cp in copies(ins, outs, sems):
            cp.start()

    def wait(ins, outs, sems):
        cps = copies(ins, outs, sems)
        for cp in cps:
            cp.wait_recv()
        for cp in cps:
            cp.wait_send()

    return _Exchange(chunks, [SDS((3,) + s.shape[1:], s.dtype) for s in chunks],
                     [pltpu.SemaphoreType.DMA((n, 3)), pltpu.SemaphoreType.DMA((n, 3))], start, wait)


def _swap_sibling(arrs):
    n = len(arrs)

    def copies(ins, outs, sems):
        send, recv = sems
        x, y, c = _place()
        return [pltpu.make_async_remote_copy(
            src_ref=ins[a], dst_ref=outs[a], send_sem=send.at[a], recv_sem=recv.at[a],
            device_id=(x, y, 1 - c), device_id_type=MESH_ID) for a in range(n)]

    def start(ins, outs, sems):
        for cp in copies(ins, outs, sems):
            cp.start()

    def wait(ins, outs, sems):
        cps = copies(ins, outs, sems)
        for cp in cps:
            cp.wait_recv()
        for cp in cps:
            cp.wait_send()

    return _Exchange(arrs, [SDS(s.shape, s.dtype) for s in arrs],
                     [pltpu.SemaphoreType.DMA((n,)), pltpu.SemaphoreType.DMA((n,))], start, wait)


N_DEV = 8


def _gather_devices(block):
    def copies(ins, outs, sems):
        send, recv, loc = sems
        x, y, c = _place()
        me = 4 * x + 2 * y + c
        local = pltpu.make_async_copy(ins[0], outs[0].at[me], loc.at[0])
        sends, recvs = [], []
        for k in range(1, N_DEV):
            peer = (x ^ (k >> 2), y ^ ((k >> 1) & 1), c ^ (k & 1))
            for group, slot in ((sends, me), (recvs, me ^ k)):
                group.append(pltpu.make_async_remote_copy(
                    src_ref=ins[0], dst_ref=outs[0].at[slot], send_sem=send.at[k - 1], recv_sem=recv.at[k - 1],
                    device_id=peer, device_id_type=MESH_ID))
        return local, sends, recvs

    def start(ins, outs, sems):
        local, sends, _ = copies(ins, outs, sems)
        for cp in [local] + sends:
            cp.start()

    def wait(ins, outs, sems):
        local, sends, recvs = copies(ins, outs, sems)
        for cp in recvs:
            cp.wait_recv()
        for cp in sends:
            cp.wait_send()
        local.wait()

    return _Exchange([block], [SDS((N_DEV,) + block.shape, block.dtype)],
                     [pltpu.SemaphoreType.DMA((N_DEV - 1,)), pltpu.SemaphoreType.DMA((N_DEV - 1,)),
                      pltpu.SemaphoreType.DMA((1,))], start, wait)


def _both(ex_a, ex_b):
    na_i, na_o, na_s = len(ex_a.ins), len(ex_a.outs), len(ex_a.sems)

    def start(ins, outs, sems):
        ex_a.start(ins[:na_i], outs[:na_o], sems[:na_s])
        ex_b.start(ins[na_i:], outs[na_o:], sems[na_s:])

    def wait(ins, outs, sems):
        ex_a.wait(ins[:na_i], outs[:na_o], sems[:na_s])
        ex_b.wait(ins[na_i:], outs[na_o:], sems[na_s:])

    return _Exchange(ex_a.ins + ex_b.ins, ex_a.outs + ex_b.outs, ex_a.sems + ex_b.sems, start, wait)


def _sum_devices(slots):
    def body(s_ref, o_ref):
        acc = s_ref[0]
        for d in range(1, N_DEV):
            acc = acc + s_ref[d]
        o_ref[...] = acc

    return pl.pallas_call(
        body, in_specs=[pl.BlockSpec(memory_space=pltpu.VMEM)], out_specs=pl.BlockSpec(memory_space=pltpu.VMEM),
        out_shape=SDS(slots.shape[1:], F32), name="sum_small",
        compiler_params=pltpu.CompilerParams(vmem_limit_bytes=32 * 1024 * 1024))(slots)


def _adam_small(ws, gs, ms, vs):
    n = len(ws)

    def body(*refs):
        for i in range(n):
            w_ref, g_ref, m_ref, v_ref = (refs[k * n + i] for k in range(4))
            outs = _adam_math(w_ref[...], g_ref[...], m_ref[...], v_ref[...])
            for k in range(3):
                refs[(4 + k) * n + i][...] = outs[k]

    vmem = pl.BlockSpec(memory_space=pltpu.VMEM)
    return pl.pallas_call(
        body, in_specs=[vmem] * (4 * n), out_specs=[vmem] * (3 * n),
        out_shape=[SDS(w.shape, F32) for w in ws] * 3, name="adam_small",
        compiler_params=pltpu.CompilerParams(vmem_limit_bytes=32 * 1024 * 1024))(*ws, *gs, *ms, *vs)


def _local_step(x, target, small, big, tb, distributed):
    g1, g2, g3, g4 = small["norm_mix_pre"], small["norm_mix_post"], small["norm_mlp_pre"], small["norm_mlp_post"]
    dist = distributed
    me = (2 * lax.axis_index("x") + lax.axis_index("y")) if dist else 0
    tb_ssm = min(tb, 256)
    bucket = jnp.asarray(_bucket_table())

    bias = _pair_layout(_bias_table(small["rel_bias"], bucket))
    sink_rows = _pair_layout(jnp.broadcast_to(small["sinks"].reshape(N_HEADS, 1, 1), (N_HEADS, BLOCK, 1)))
    disc_args = (small["lam_re"], small["lam_im"], small["log_dt"], small["b_re"], small["b_im"])
    (ab_re, ab_im, bb_re, bb_im), disc_vjp = jax.vjp(_ssm_discretize, *disc_args)
    tab_f, tab_b = _scan_tables(ab_re, ab_im)
    bmat = _bf(_b_matrix(bb_re, bb_im))
    cmat = _bf(_c_matrix(small["c_re"], small["c_im"]))
    d_skip = small["d_skip"]

    if dist:
        (g_in,) = _exchange_alone("gather_w_in", _gather_chips([big["w_in"]]))
        w_in = g_in.reshape(IN_W, D_MODEL)
    else:
        w_in = big["w_in"]
    mix = ("w_glu", "w_attn_branch", "w_ssm_branch", "w_out")
    outs = _inproj_fwd(x, g1, w_in, tb, _gather_chips([big[n] for n in mix]) if dist else None)
    h1, q, k, v, u, ga, gs = outs[:7]
    w_glu, w_ab, w_sb, w_out = outs[7:] if dist else [big[n] for n in mix]
    w_glu = w_glu.reshape(SSM_W, SSM_W)
    w_out = w_out.reshape(D_MODEL, D_MODEL)
    outs = _attn_fwd(q, k, v, bias, sink_rows, _gather_chips([big["w_ff_in"]]) if dist else None)
    att = outs[0]
    w_ffi = outs[1] if dist else big["w_ff_in"]
    outs = _ssm_fwd(u, bmat, cmat, tab_f, d_skip, tb_ssm, _gather_chips([big["w_ff_out"]]) if dist else None)
    s, h = outs[:2]
    w_ffo = outs[2] if dist else big["w_ff_out"]
    x2 = _merge_fwd(x, s, att, ga, gs, g2, w_glu, w_ab, w_sb, w_out, tb)
    dy, df, h3, loss_acc, dg4 = _mlp_fwd_loss(x2, target, g3, g4, w_ffi, w_ffo, tb)

    dx2, act, da, dg3 = _mlp_bwd(x2, dy, df, h3, g3, w_ffi, w_ffo, tb)
    tl = min(512, x.shape[0])
    chunked = (N_CHIPS, D_FF // N_CHIPS, D_MODEL)
    d_ffi, b_ffi = _matmul_tn("grad_w_ff_in", h3, da, D_MODEL, D_FF // FF_CHUNKS, tl, True)
    d_ffo, b_ffo = _matmul_tn("grad_w_ff_out", act, df, D_FF // FF_CHUNKS, D_MODEL, tl, False)
    d_ffo, b_ffo = d_ffo.reshape(chunked), b_ffo.reshape(chunked)
    outs = _merge_bwd(dx2, s, att, ga, gs, g2, w_glu, w_ab, w_sb, w_out, tb_ssm,
                      _scatter_chips([b_ffi]) if dist else None)
    ds, datt, dga, dgs, dg2, d_glu, d_ab, d_sb, d_out, b_glu, b_ab, b_sb, b_out = outs[:13]
    r_ffi = outs[13:]
    glu4, out4 = (N_CHIPS, SSM_W // N_CHIPS, SSM_W), (N_CHIPS, D_MODEL // N_CHIPS, D_MODEL)
    d_mix = [d_glu.reshape(glu4), d_ab, d_sb, d_out.reshape(out4)]
    b_mix = [b_glu.reshape(glu4), b_ab, b_sb, b_out.reshape(out4)]
    outs = _ssm_bwd(ds, u, h, bmat.transpose(0, 2, 1), cmat.transpose(0, 2, 1), tab_b, d_skip, tb_ssm,
                    _scatter_chips([b_ffo]) if dist else None)
    du, d_bmat, d_cmat, da_acc, dd_skip = outs[:5]
    r_ffo = outs[5:]
    outs = _attn_bwd(q, k, v, datt, bias, sink_rows, _scatter_chips(b_mix) if dist else None)
    dq, dk, dv, dbias, dsink_rows = outs[:5]
    r_mix = outs[5:]
    if dist:
        p_ffi = _sum4("sum_w_ff_in", d_ffi, r_ffi[0], me)
        p_ffo = _sum4("sum_w_ff_out", d_ffo, r_ffo[0], me)
    dx, dpj, dg1 = _inproj_bwd(x, dx2, dq, dk, dv, du, dga, dgs, g1, w_in, tb)

    dab_re, dab_im = _state_unlayout(jnp.sum(da_acc, axis=0))
    dbb_re, dbb_im = _b_matrix_grad(d_bmat)
    d_lam_re, d_lam_im, d_log_dt, d_b_re, d_b_im = disc_vjp((dab_re, dab_im, dbb_re, dbb_im))
    d_c_re, d_c_im = _c_matrix_grad(d_cmat)
    d_rel = _bias_grad(_pair_unlayout(dbias), bucket)
    d_sinks = jnp.sum(_pair_unlayout(dsink_rows), axis=(1, 2))
    small_grads = dict(
        norm_mix_pre=dg1, norm_mix_post=dg2, norm_mlp_pre=dg3, norm_mlp_post=dg4, rel_bias=d_rel, sinks=d_sinks,
        lam_re=d_lam_re, lam_im=d_lam_im, log_dt=d_log_dt, b_re=d_b_re, b_im=d_b_im, c_re=d_c_re, c_im=d_c_im,
        d_skip=dd_skip)
    ride = _both(_swap_sibling([p_ffi, p_ffo]), _gather_devices(_pack(small_grads, loss_acc))) if dist else None
    outs = _matmul_tn("grad_w_in", dpj, h1, IN_W // 2, D_MODEL, tl, False, ride)
    in4 = (N_CHIPS, IN_W // N_CHIPS, D_MODEL)
    d_in, b_in = outs[0].reshape(in4), outs[1].reshape(in4)
    if not dist:
        return loss_acc, dx, small_grads, dict(zip(BIG, [d_in] + d_mix + [d_ffi, d_ffo]))
    s_ffi, s_ffo, slots = outs[2:]
    (r_in,) = _exchange_alone("scatter_w_in", _scatter_chips([b_in]))
    parts = [_sum4("sum_" + n, d, r, me) for n, d, r in zip(("w_in",) + mix, [d_in] + d_mix, [r_in] + list(r_mix))]
    sibs = _exchange_alone("swap_rest", _swap_sibling(parts))
    parts += [p_ffi, p_ffo]
    sibs = list(sibs) + [s_ffi, s_ffo]
    return loss_acc, dx, _sum_devices(slots), dict(zip(BIG, zip(parts, sibs)))


SMALL = ['norm_mix_pre', 'norm_mix_post', 'norm_mlp_pre', 'norm_mlp_post', 'rel_bias', 'sinks', 'lam_re', 'lam_im',
         'log_dt', 'b_re', 'b_im', 'c_re', 'c_im', 'd_skip']
BIG = ['w_in', 'w_glu', 'w_attn_branch', 'w_ssm_branch', 'w_out', 'w_ff_in', 'w_ff_out']
WEIGHTS = ['norm_mix_pre', 'norm_mix_post', 'norm_mlp_pre', 'norm_mlp_post', 'w_in', 'rel_bias', 'sinks', 'lam_re',
           'lam_im', 'log_dt', 'b_re', 'b_im', 'c_re', 'c_im', 'd_skip', 'w_glu', 'w_attn_branch', 'w_ssm_branch',
           'w_out', 'w_ff_in', 'w_ff_out']
PACK_COLS = 1024
PACK_ORDER = ['b_re', 'b_im', 'c_re', 'c_im', 'lam_re', 'lam_im', 'norm_mix_pre', 'norm_mix_post', 'norm_mlp_pre',
              'norm_mlp_post', 'rel_bias', 'sinks', 'log_dt', 'd_skip']


STATE_MINOR = ('b_re', 'b_im')
PACK_ROWS = 144
LOSS_ROW = 140


def _pack(named, loss_acc):
    parts = []
    for n in PACK_ORDER:
        a = jnp.swapaxes(named[n], -1, -2) if n in STATE_MINOR else named[n]
        flat = a.reshape(-1)
        rows = -(-flat.shape[0] // PACK_COLS)
        parts.append(jnp.pad(flat, (0, rows * PACK_COLS - flat.shape[0])).reshape(rows, PACK_COLS))
    assert sum(p.shape[0] for p in parts) == LOSS_ROW
    parts.append(jnp.pad(loss_acc[0:1], ((0, PACK_ROWS - LOSS_ROW - 1), (0, PACK_COLS - loss_acc.shape[1]))))
    return jnp.concatenate(parts, axis=0)


def _unpack(packed, shapes):
    out, at = {}, 0
    for n in PACK_ORDER:
        shape = shapes[n][:-2] + (shapes[n][-1], shapes[n][-2]) if n in STATE_MINOR else shapes[n]
        size = int(np.prod(shape))
        rows = -(-size // PACK_COLS)
        blk = packed[at:at + rows]
        out[n] = (blk.reshape(-1)[:size] if size % PACK_COLS else blk).reshape(shape)
        at += rows
    return out


def kernel(x, norm_mix_pre, norm_mix_post, norm_mlp_pre, norm_mlp_post, w_in, rel_bias, sinks, lam_re, lam_im, log_dt, b_re, b_im, c_re, c_im, d_skip, w_glu, w_attn_branch, w_ssm_branch, w_out, w_ff_in, w_ff_out, loss_target, m_norm_mix_pre, m_norm_mix_post, m_norm_mlp_pre, m_norm_mlp_post, m_w_in, m_rel_bias, m_sinks, m_lam_re, m_lam_im, m_log_dt, m_b_re, m_b_im, m_c_re, m_c_im, m_d_skip, m_w_glu, m_w_attn_branch, m_w_ssm_branch, m_w_out, m_w_ff_in, m_w_ff_out, v_norm_mix_pre, v_norm_mix_post, v_norm_mlp_pre, v_norm_mlp_post, v_w_in, v_rel_bias, v_sinks, v_lam_re, v_lam_im, v_log_dt, v_b_re, v_b_im, v_c_re, v_c_im, v_d_skip, v_w_glu, v_w_attn_branch, v_w_ssm_branch, v_w_out, v_w_ff_in, v_w_ff_out):
    env = dict(locals())
    w = {n: env[n] for n in WEIGHTS}
    m = {n: env["m_" + n] for n in WEIGHTS}
    v = {n: env["v_" + n] for n in WEIGHTS}
    seq = x.shape[1]
    tb = min(512, seq)

    small = {n: w[n] for n in ('norm_mix_pre', 'norm_mix_post', 'norm_mlp_pre', 'norm_mlp_post', 'rel_bias')}
    small.update({n: w[n][0] for n in ('sinks', 'lam_re', 'lam_im', 'log_dt', 'b_re', 'b_im', 'c_re', 'c_im')})
    small['d_skip'] = w['d_skip']
    shard = lambda t, n: t[n][0].T if n == 'w_in' else t[n][0]
    unshard = lambda a, n: (a.T if n == 'w_in' else a)[None]
    _, dx, small_g, big_g = _local_step(
        x[0], loss_target[0], small, {n: _bf(shard(w, n)) for n in BIG}, tb, True)

    loss = small_g[LOSS_ROW, 0]

    grads, deltas, new_m, new_v = {}, {}, {}, {}
    for n in BIG:
        p_own, p_sib = big_g[n]
        outs = _adam_pair("adam_" + n, shard(w, n), p_own, p_sib, shard(m, n), shard(v, n))
        grads[n], deltas[n], new_m[n], new_v[n] = [unshard(a, n) for a in outs]

    minor = lambda t, n: jnp.swapaxes(t, -1, -2) if n in STATE_MINOR else t
    g_small = _unpack(small_g, {n: w[n].shape for n in SMALL})
    outs = _adam_small([minor(w[n], n) for n in SMALL], [g_small[n] for n in SMALL],
                       [minor(m[n], n) for n in SMALL], [minor(v[n], n) for n in SMALL])
    grads.update({n: minor(g_small[n], n) for n in SMALL})
    for k, dst in enumerate((deltas, new_m, new_v)):
        dst.update({n: minor(a, n) for n, a in zip(SMALL, outs[k * len(SMALL):(k + 1) * len(SMALL)])})

    return (loss, dx[None], *[grads[n] for n in WEIGHTS], *[deltas[n] for n in WEIGHTS],
            *[new_m[n] for n in WEIGHTS], *[new_v[n] for n in WEIGHTS])
```

```python
import functools
import math

import numpy as np
import jax
import jax.numpy as jnp
from jax import lax
from jax.experimental import pallas as pl
from jax.experimental.pallas import tpu as pltpu

F32 = jnp.float32
BF16 = jnp.bfloat16

D_MODEL = 1024
N_HEADS = 8
N_KV = 2
Q_GROUP = 4
HEAD_DIM = 64
ATTN_W = 512
KV_W = 128
BLOCK = 128
N_BUCKETS = 32
MAX_DISTANCE = 128
NEG_INF = -1e30
SSM_W = 512
SSM_GROUP = 16
SSM_GROUPS = 32
SSM_STATE = 64
N_SUPER = 4
GROUPS_PER_SUPER = SSM_GROUPS // N_SUPER
SUPER_IN = GROUPS_PER_SUPER * SSM_GROUP
SUPER_HALF = GROUPS_PER_SUPER * SSM_STATE
SUPER_W = 2 * SUPER_HALF
STATE_COLS = N_SUPER * SUPER_W
D_FF = 4096
FF_CHUNKS = 4
IN_W = 3328
SPLITS = (0, 512, 640, 768, 1280, 2304, 3328)
RMS_EPS = 1e-6
N_CHIPS = 4
SUBLANES = 8

ADAM_LR = 0.001
ADAM_B1 = 0.9
ADAM_B2 = 0.999
ADAM_EPS = 1e-08
ADAM_WD = 0.01
ADAM_STEP = 10

VMEM_BIG = 56 * 1024 * 1024
SDS = jax.ShapeDtypeStruct
MESH_ID = pl.DeviceIdType.MESH
ANY = pl.BlockSpec(memory_space=pl.ANY)


def _bf(x):
    return x.astype(BF16)


def _mm(a, b):
    return jnp.dot(a, b, preferred_element_type=F32)


def _mm_nt(a, b):
    return lax.dot_general(a, b, (((1,), (1,)), ((), ())), preferred_element_type=F32)


def _mm_tn(a, b):
    return lax.dot_general(a, b, (((0,), (0,)), ((), ())), preferred_element_type=F32)


def _sig(x):
    return 1.0 / (1.0 + jnp.exp(-x))


def _rms(x, g):
    r = lax.rsqrt(jnp.mean(x * x, axis=-1, keepdims=True) + RMS_EPS)
    xh = x * r
    return xh * g, xh, r


def _rms_bwd(dout, xh, r, g):
    dg = jnp.sum(dout * xh, axis=0, keepdims=True)
    dxh = dout * g
    dx = r * (dxh - xh * jnp.mean(dxh * xh, axis=-1, keepdims=True))
    return dx, dg


_GELU_C = math.sqrt(2.0 / math.pi)


def _gelu_and_grad(x):
    x2 = x * x
    inner = _GELU_C * (x + 0.044715 * (x2 * x))
    t = jnp.tanh(inner)
    y = 0.5 * x * (1.0 + t)
    dy = 0.5 * (1.0 + t) + 0.5 * x * (1.0 - t * t) * (_GELU_C * (1.0 + 3.0 * 0.044715 * x2))
    return y, dy


def _zero_map(nd, *_):
    return (0,) * nd


def _params(n_axes, vmem=None):
    return pltpu.CompilerParams(dimension_semantics=("arbitrary",) * n_axes, vmem_limit_bytes=vmem)


class _Exchange:
    def __init__(self, ins, outs, sems, start, wait):
        self.ins, self.outs, self.sems, self.start, self.wait = list(ins), list(outs), list(sems), start, wait


def _fused_call(name, body, grid, in_specs, out_specs, out_shape, scratch, args, exchange, params):
    n_in, n_out, n_scr = len(in_specs), len(out_specs), len(scratch)
    if exchange is None:
        fn = body
    else:
        ex = exchange
        n_xi, n_xo = len(ex.ins), len(ex.outs)

        def fn(*refs):
            at = 0
            parts = []
            for n in (n_in, n_xi, n_out, n_xo, n_scr, len(ex.sems)):
                parts.append(refs[at:at + n])
                at += n
            ins, x_in, outs, x_out, scr, x_sem = parts
            ids = [pl.program_id(a) for a in range(len(grid))]
            first = functools.reduce(jnp.logical_and, [i == 0 for i in ids])
            last = functools.reduce(jnp.logical_and, [i == g - 1 for i, g in zip(ids, grid)])

            @pl.when(first)
            def _():
                ex.start(x_in, x_out, x_sem)

            body(*ins, *outs, *scr)

            @pl.when(last)
            def _():
                ex.wait(x_in, x_out, x_sem)

        in_specs = list(in_specs) + [ANY] * n_xi
        out_specs = list(out_specs) + [ANY] * n_xo
        out_shape = list(out_shape) + ex.outs
        scratch = list(scratch) + ex.sems
        args = list(args) + ex.ins
    return pl.pallas_call(fn, grid=grid, in_specs=in_specs, out_specs=out_specs, out_shape=out_shape,
                          scratch_shapes=list(scratch), name=name, compiler_params=params)(*args)


def _exchange_alone(name, ex):
    def body(*refs):
        n_xi, n_xo = len(ex.ins), len(ex.outs)
        x_in, x_out, x_sem = refs[:n_xi], refs[n_xi:n_xi + n_xo], refs[n_xi + n_xo:]
        ex.start(x_in, x_out, x_sem)
        ex.wait(x_in, x_out, x_sem)

    return pl.pallas_call(body, in_specs=[ANY] * len(ex.ins), out_specs=[ANY] * len(ex.outs), out_shape=ex.outs,
                          scratch_shapes=ex.sems, name=name)(*ex.ins)


def _rowcall(name, body, seq, tb, rows, consts, row_outs, acc_outs, scratch=(), reverse=False, vmem=None,
             exchange=None):
    nb = seq // tb
    rmap = (lambda i: (nb - 1 - i, 0)) if reverse else (lambda i: (i, 0))
    in_specs = [pl.BlockSpec((tb, a.shape[1]), rmap) for a in rows]
    in_specs += [pl.BlockSpec(a.shape, functools.partial(_zero_map, a.ndim), pipeline_mode=pl.Buffered(1))
                 for a in consts]
    out_specs = [pl.BlockSpec((tb, c), rmap) for c, _ in row_outs]
    out_specs += [pl.BlockSpec(s, functools.partial(_zero_map, len(s))) for s, _ in acc_outs]
    out_shape = [SDS((seq, c), dt) for c, dt in row_outs] + [SDS(s, dt) for s, dt in acc_outs]
    return _fused_call(name, body, (nb,), in_specs, out_specs, out_shape, list(scratch), [*rows, *consts],
                       exchange, _params(1, vmem))


def _inproj_fwd(x, g1, w_in, tb, exchange=None):
    seq = x.shape[0]

    def body(x_ref, g_ref, w_ref, h_ref, q_ref, k_ref, v_ref, u_ref, ga_ref, gs_ref):
        h, _, _ = _rms(x_ref[...], g_ref[...])
        hb = _bf(h)
        h_ref[...] = hb
        pj = _mm_nt(hb, w_ref[...])
        q_ref[...] = _bf(pj[:, SPLITS[0]:SPLITS[1]])
        k_ref[...] = _bf(pj[:, SPLITS[1]:SPLITS[2]])
        v_ref[...] = _bf(pj[:, SPLITS[2]:SPLITS[3]])
        u_ref[...] = pj[:, SPLITS[3]:SPLITS[4]]
        ga_ref[...] = pj[:, SPLITS[4]:SPLITS[5]]
        gs_ref[...] = pj[:, SPLITS[5]:SPLITS[6]]

    return _rowcall("inproj_fwd", body, seq, tb, [x], [g1, w_in],
                    [(D_MODEL, BF16), (ATTN_W, BF16), (KV_W, BF16), (KV_W, BF16), (SSM_W, F32),
                     (D_MODEL, F32), (D_MODEL, F32)], [], vmem=VMEM_BIG, exchange=exchange)


def _inproj_bwd(x, dx2, dq, dk, dv, du, dga, dgs, g1, w_in, tb, exchange=None):
    seq = x.shape[0]

    def body(x_ref, dx2_ref, dq_ref, dk_ref, dv_ref, du_ref, dga_ref, dgs_ref, g_ref, w_ref,
             dx_ref, dpj_ref, dg_ref):
        @pl.when(pl.program_id(0) == 0)
        def _():
            dg_ref[...] = jnp.zeros_like(dg_ref)

        dpj = jnp.concatenate([dq_ref[...], dk_ref[...], dv_ref[...], _bf(du_ref[...]),
                               dga_ref[...], dgs_ref[...]], axis=1)
        dpj_ref[...] = dpj
        dh = _mm(dpj, w_ref[...])
        g = g_ref[...]
        _, xh, r = _rms(x_ref[...], g)
        dxn, dg = _rms_bwd(dh, xh, r, g)
        dx_ref[...] = dx2_ref[...] + dxn
        dg_ref[...] += dg

    return _rowcall("inproj_bwd", body, seq, tb, [x, dx2, dq, dk, dv, du, dga, dgs], [g1, w_in],
                    [(D_MODEL, F32), (IN_W, BF16)], [((1, D_MODEL), F32)], vmem=VMEM_BIG, exchange=exchange)


def _bucket_table():
    qi = np.arange(BLOCK)[:, None]
    kj = np.arange(2 * BLOCK)[None, :]
    dist = qi + BLOCK - kj
    max_exact = N_BUCKETS // 2
    d = np.maximum(dist, 0)
    df = np.maximum(d, 1).astype(np.float32)
    large = max_exact + (np.log(df / np.float32(max_exact)) / np.float32(math.log(MAX_DISTANCE / max_exact))
                         * np.float32(N_BUCKETS - max_exact)).astype(np.int32)
    large = np.minimum(large, N_BUCKETS - 1)
    bucket = np.where(d < max_exact, d, large)
    valid = (dist >= 0) & (dist < BLOCK)
    return np.where(valid, bucket, -1).astype(np.int32)


def _bias_table(rel_bias, bucket):
    def body(rb_ref, bk_ref, o_ref):
        bk = bk_ref[...]
        has_prev = lax.broadcasted_iota(jnp.int32, bk.shape, 1) >= BLOCK
        for h in range(N_HEADS):
            acc = jnp.full((BLOCK, 2 * BLOCK), NEG_INF, F32)
            for b in range(N_BUCKETS):
                acc = jnp.where(bk == b, rb_ref[b, h], acc)
            o_ref[0, h] = jnp.where(has_prev, acc, NEG_INF)
            o_ref[1, h] = acc

    return pl.pallas_call(
        body, out_shape=SDS((2, N_HEADS, BLOCK, 2 * BLOCK), F32),
        in_specs=[pl.BlockSpec(memory_space=pltpu.SMEM), pl.BlockSpec(memory_space=pltpu.VMEM)],
        out_specs=pl.BlockSpec(memory_space=pltpu.VMEM), name="bias_table",
    )(rel_bias, bucket)


def _bias_grad(dbias, bucket):
    def body(db_ref, bk_ref, o_ref):
        bk = bk_ref[...]
        for h in range(N_HEADS):
            db = db_ref[h]
            for b in range(N_BUCKETS):
                o_ref[b, h] = jnp.sum(jnp.where(bk == b, db, 0.0))

    return pl.pallas_call(
        body, out_shape=SDS((N_BUCKETS, N_HEADS), F32),
        in_specs=[pl.BlockSpec(memory_space=pltpu.VMEM), pl.BlockSpec(memory_space=pltpu.VMEM)],
        out_specs=pl.BlockSpec(memory_space=pltpu.SMEM), name="bias_grad",
    )(dbias, bucket)


TILE = 2 * HEAD_DIM


def _pair_layout(t):
    lead = t.shape[:-3]
    t = t.reshape(lead + (N_KV, 2, 2) + t.shape[-2:])
    nl = len(lead)
    t = jnp.transpose(t, tuple(range(nl)) + (nl, nl + 2, nl + 1, nl + 3, nl + 4))
    return t.reshape(lead + (N_KV, 2, 2 * BLOCK, t.shape[-1]))


def _pair_unlayout(t):
    t = t.reshape(N_KV, 2, 2, BLOCK, t.shape[-1]).transpose(0, 2, 1, 3, 4)
    return t.reshape(N_HEADS, BLOCK, t.shape[-1])


def _halves(t):
    tf = t.astype(F32)
    low = lax.broadcasted_iota(jnp.int32, tf.shape, 1) < HEAD_DIM
    swapped = pltpu.roll(tf, HEAD_DIM, 1)
    zero = jnp.zeros_like(tf)
    return ((_bf(jnp.where(low, tf, zero)), _bf(jnp.where(low, zero, swapped))),
            (_bf(jnp.where(low, swapped, zero)), _bf(jnp.where(low, zero, tf))))


def _fold_halves(even, odd):
    low = lax.broadcasted_iota(jnp.int32, even.shape, 1) < HEAD_DIM
    comb = jnp.where(low, even, odd)
    return comb + pltpu.roll(comb, HEAD_DIM, 1)


def _tile_rows(ref, kh):
    return jnp.concatenate([ref[:, (2 * kh) * TILE:(2 * kh + 1) * TILE],
                            ref[:, (2 * kh + 1) * TILE:(2 * kh + 2) * TILE]], axis=0)


def _halves_t(t):
    tt = t.astype(F32).T
    top = lax.broadcasted_iota(jnp.int32, tt.shape, 0) < HEAD_DIM
    swapped = jnp.concatenate([tt[HEAD_DIM:], tt[:HEAD_DIM]], axis=0)
    zero = jnp.zeros_like(tt)
    return ((_bf(jnp.where(top, tt, zero)), _bf(jnp.where(top, zero, swapped))),
            (_bf(jnp.where(top, swapped, zero)), _bf(jnp.where(top, zero, tt))))


def _attn_probs(km, qk, bias, sink):
    lg = _mm_nt(km, qk) * (HEAD_DIM ** -0.5) + bias
    m = jnp.maximum(jnp.max(lg, axis=0, keepdims=True), sink)
    p = jnp.exp(lg - m)
    es = jnp.exp(sink - m)
    inv = 1.0 / (jnp.sum(p, axis=0, keepdims=True) + es)
    return p * inv, es * inv


def _attn_fwd(q, k, v, bias, sink_rows, exchange=None):
    seq = q.shape[0]
    nblk = seq // BLOCK

    def body(q_ref, kp_ref, kc_ref, vp_ref, vc_ref, b_ref, s_ref, o_ref):
        which = jnp.minimum(pl.program_id(0), 1)
        kms = _halves(jnp.concatenate([kp_ref[...], kc_ref[...]], axis=0))
        vts = _halves_t(jnp.concatenate([vp_ref[...], vc_ref[...]], axis=0))
        for kh in range(N_KV):
            qk = _tile_rows(q_ref, kh)
            acc = jnp.zeros((TILE, 2 * BLOCK), F32)
            for par in range(2):
                pr, _ = _attn_probs(kms[kh][par], qk, b_ref[which, kh, par], s_ref[kh, par])
                acc = acc + _mm(vts[kh][par], _bf(pr))
            acc = acc.T
            o_ref[:, (2 * kh) * TILE:(2 * kh + 1) * TILE] = _bf(acc[:BLOCK])
            o_ref[:, (2 * kh + 1) * TILE:(2 * kh + 2) * TILE] = _bf(acc[BLOCK:])

    cur = lambda n: (n, 0)
    prev = lambda n: (jnp.maximum(n - 1, 0), 0)
    return _fused_call(
        "attn_fwd", body, (nblk,),
        [pl.BlockSpec((BLOCK, ATTN_W), cur),
         pl.BlockSpec((BLOCK, KV_W), prev), pl.BlockSpec((BLOCK, KV_W), cur),
         pl.BlockSpec((BLOCK, KV_W), prev), pl.BlockSpec((BLOCK, KV_W), cur),
         pl.BlockSpec(bias.shape, functools.partial(_zero_map, bias.ndim)),
         pl.BlockSpec(sink_rows.shape, functools.partial(_zero_map, sink_rows.ndim))],
        [pl.BlockSpec((BLOCK, ATTN_W), cur)], [SDS((seq, ATTN_W), BF16)], [],
        [q, k, k, v, v, bias, sink_rows], exchange, _params(1))


def _attn_bwd(q, k, v, d_out, bias, sink_rows, exchange=None):
    seq = q.shape[0]
    nblk = seq // BLOCK

    def body(q_ref, kp_ref, kc_ref, vp_ref, vc_ref, do_ref, b_ref, s_ref,
             dq_ref, dk_ref, dv_ref, db_ref, ds_ref, ck_ref, cv_ref):
        n = pl.program_id(0)

        @pl.when(n == 0)
        def _():
            db_ref[...] = jnp.zeros_like(db_ref)
            ds_ref[...] = jnp.zeros_like(ds_ref)
            ck_ref[...] = jnp.zeros_like(ck_ref)
            cv_ref[...] = jnp.zeros_like(cv_ref)

        @pl.when(n < nblk)
        def _():
            which = jnp.minimum(n, 1)
            scale = HEAD_DIM ** -0.5
            kcat = jnp.concatenate([kp_ref[...], kc_ref[...]], axis=0)
            kms = _halves(kcat)
            kts = _halves_t(kcat)
            vms = _halves(jnp.concatenate([vp_ref[...], vc_ref[...]], axis=0))
            dks, dvs = [], []
            for kh in range(N_KV):
                qk = _tile_rows(q_ref, kh)
                dok = _tile_rows(do_ref, kh)
                dq = jnp.zeros((TILE, 2 * BLOCK), F32)
                dkp, dvp = [], []
                for par in range(2):
                    pr, ps = _attn_probs(kms[kh][par], qk, b_ref[which, kh, par], s_ref[kh, par])
                    dp = _mm_nt(vms[kh][par], dok)
                    rs = jnp.sum(pr * dp, axis=0, keepdims=True)
                    dlg = pr * (dp - rs)
                    ds_ref[kh, par] += -ps * rs
                    db_ref[kh, par] += dlg
                    dlb = _bf(dlg)
                    dq = dq + _mm(kts[kh][par], dlb)
                    dkp.append(_mm(dlb, qk))
                    dvp.append(_mm(_bf(pr), dok))
                dq = _bf((dq * scale).T)
                dq_ref[:, (2 * kh) * TILE:(2 * kh + 1) * TILE] = dq[:BLOCK]
                dq_ref[:, (2 * kh + 1) * TILE:(2 * kh + 2) * TILE] = dq[BLOCK:]
                dks.append(_fold_halves(*dkp))
                dvs.append(_fold_halves(*dvp))
            low = lax.broadcasted_iota(jnp.int32, (2 * BLOCK, TILE), 1) < HEAD_DIM
            dkk = jnp.where(low, dks[0], dks[1]) * scale
            dvv = jnp.where(low, dvs[0], dvs[1])
            dk_ref[...] = _bf(ck_ref[...] + dkk[:BLOCK])
            ck_ref[...] = dkk[BLOCK:]
            dv_ref[...] = _bf(cv_ref[...] + dvv[:BLOCK])
            cv_ref[...] = dvv[BLOCK:]

        @pl.when(n == nblk)
        def _():
            dk_ref[...] = _bf(ck_ref[...])
            dv_ref[...] = _bf(cv_ref[...])

    cur = lambda n: (jnp.minimum(n, nblk - 1), 0)
    prev = lambda n: (jnp.maximum(jnp.minimum(n, nblk - 1) - 1, 0), 0)
    late = lambda n: (jnp.maximum(n - 1, 0), 0)
    kv_spec = lambda m: pl.BlockSpec((BLOCK, KV_W), m)
    acc_b = pl.BlockSpec(bias.shape[1:], functools.partial(_zero_map, bias.ndim - 1))
    acc_s = pl.BlockSpec(sink_rows.shape, functools.partial(_zero_map, sink_rows.ndim))
    return _fused_call(
        "attn_bwd", body, (nblk + 1,),
        [pl.BlockSpec((BLOCK, ATTN_W), cur), kv_spec(prev), kv_spec(cur), kv_spec(prev), kv_spec(cur),
         pl.BlockSpec((BLOCK, ATTN_W), cur),
         pl.BlockSpec(bias.shape, functools.partial(_zero_map, bias.ndim)), acc_s],
        [pl.BlockSpec((BLOCK, ATTN_W), cur), kv_spec(late), kv_spec(late), acc_b, acc_s],
        [SDS((seq, ATTN_W), BF16), SDS((seq, KV_W), BF16), SDS((seq, KV_W), BF16),
         SDS(bias.shape[1:], F32), SDS(sink_rows.shape, F32)],
        [pltpu.VMEM((BLOCK, KV_W), F32), pltpu.VMEM((BLOCK, KV_W), F32)],
        [q, k, k, v, v, d_out, bias, sink_rows], exchange, _params(1))


def _ssm_discretize(lam_re, lam_im, log_dt, b_re, b_im):
    dt = jnp.exp(log_dt)[:, None]
    mag = jnp.exp(lam_re * dt)
    ab_re = mag * jnp.cos(lam_im * dt)
    ab_im = mag * jnp.sin(lam_im * dt)
    nr = ab_re - 1.0
    den = lam_re * lam_re + lam_im * lam_im
    f_re = (nr * lam_re + ab_im * lam_im) / den
    f_im = (ab_im * lam_re - nr * lam_im) / den
    bb_re = f_re[..., None] * b_re - f_im[..., None] * b_im
    bb_im = f_re[..., None] * b_im + f_im[..., None] * b_re
    return ab_re, ab_im, bb_re, bb_im


def _state_layout(re, im):
    z = jnp.stack([re, im]).reshape(2, N_SUPER, GROUPS_PER_SUPER, SSM_STATE)
    return z.transpose(1, 0, 2, 3).reshape(STATE_COLS)


def _state_unlayout(vec):
    z = vec.reshape(N_SUPER, 2, GROUPS_PER_SUPER, SSM_STATE).transpose(1, 0, 2, 3)
    z = z.reshape(2, SSM_GROUPS, SSM_STATE)
    return z[0], z[1]


def _scan_tables(ab_re, ab_im):
    pw = [None, (ab_re, ab_im)]
    for _ in range(2, SUBLANES + 1):
        pr, pi_ = pw[-1]
        pw.append((pr * ab_re - pi_ * ab_im, pr * ab_im + pi_ * ab_re))
    rows = np.arange(SUBLANES)[:, None]
    fwd, bwd = [], []
    for shift in (1, 2, 4):
        fwd.append(_state_layout(*pw[shift])[None, :] * (rows >= shift).astype(np.float32))
        bwd.append(_state_layout(pw[shift][0], -pw[shift][1])[None, :] * (rows < SUBLANES - shift).astype(np.float32))
    fwd.append(jnp.stack([_state_layout(*pw[r + 1]) for r in range(SUBLANES)]))
    bwd.append(jnp.stack([_state_layout(pw[SUBLANES - r][0], -pw[SUBLANES - r][1]) for r in range(SUBLANES)]))
    return jnp.stack(fwd), jnp.stack(bwd)


_EYE = np.eye(GROUPS_PER_SUPER, dtype=np.float32)


def _b_matrix(bb_re, bb_im):
    bb = jnp.stack([bb_re, bb_im]).reshape(2, N_SUPER, GROUPS_PER_SUPER, SSM_STATE, SSM_GROUP)
    m = jnp.einsum('rsgpc,gh->sgcrhp', bb, _EYE)
    return m.reshape(N_SUPER, SUPER_IN, SUPER_W)


def _b_matrix_grad(dm):
    d = dm.reshape(N_SUPER, GROUPS_PER_SUPER, SSM_GROUP, 2, GROUPS_PER_SUPER, SSM_STATE)
    d = jnp.sum(d * _EYE[None, :, None, None, :, None], axis=4)
    d = d.transpose(3, 0, 1, 4, 2).reshape(2, SSM_GROUPS, SSM_STATE, SSM_GROUP)
    return d[0], d[1]


def _c_matrix(c_re, c_im):
    cc = jnp.stack([c_re, -c_im]).reshape(2, N_SUPER, GROUPS_PER_SUPER, SSM_GROUP, SSM_STATE)
    m = jnp.einsum('rsgcp,gh->srgphc', cc, _EYE)
    return m.reshape(N_SUPER, SUPER_W, SUPER_IN)


def _c_matrix_grad(dm):
    d = dm.reshape(N_SUPER, 2, GROUPS_PER_SUPER, SSM_STATE, GROUPS_PER_SUPER, SSM_GROUP)
    d = jnp.sum(d * _EYE[None, None, :, None, :, None], axis=4)
    d = d.transpose(1, 0, 2, 4, 3).reshape(2, SSM_GROUPS, SSM_GROUP, SSM_STATE)
    return d[0], -d[1]


def _scan_rows(buf_ref, tab_ref, carry_ref, n_groups, reverse, h_ref=None, da_ref=None):
    edge = 0 if reverse else SUBLANES - 1
    for sb in range(N_SUPER):
        cr = pl.ds(sb * SUPER_W, SUPER_HALF)
        ci = pl.ds(sb * SUPER_W + SUPER_HALF, SUPER_HALF)

        def step(gi, carry, cr=cr, ci=ci):
            g = (n_groups - 1 - gi) if reverse else gi
            rows = pl.ds(pl.multiple_of(g * SUBLANES, SUBLANES), SUBLANES)
            c_re, c_im = carry[0], carry[1]
            xr = buf_ref[rows, cr]
            xi = buf_ref[rows, ci]
            for k, shift in enumerate((1, 2, 4)):
                s = (SUBLANES - shift) if reverse else shift
                sr = pltpu.roll(xr, s, 0)
                si = pltpu.roll(xi, s, 0)
                ar = tab_ref[k, :, cr]
                ai = tab_ref[k, :, ci]
                xr, xi = xr + ar * sr - ai * si, xi + ar * si + ai * sr
            pr = tab_ref[3, :, cr]
            pi_ = tab_ref[3, :, ci]
            xr, xi = xr + pr * c_re - pi_ * c_im, xi + pr * c_im + pi_ * c_re
            buf_ref[rows, cr] = xr
            buf_ref[rows, ci] = xi
            out = [jnp.broadcast_to(xr[edge:edge + 1], xr.shape), jnp.broadcast_to(xi[edge:edge + 1], xi.shape)]
            if h_ref is not None:
                last = lax.broadcasted_iota(jnp.int32, xr.shape, 0) == SUBLANES - 1
                gr = jnp.where(last, c_re, pltpu.roll(xr, SUBLANES - 1, 0))
                gim = jnp.where(last, c_im, pltpu.roll(xi, SUBLANES - 1, 0))
                hr = h_ref[rows, cr]
                hi = h_ref[rows, ci]
                out += [carry[2] + gr * hr + gim * hi, carry[3] + gim * hr - gr * hi]
            return tuple(out)

        init = [carry_ref[:, cr], carry_ref[:, ci]]
        if h_ref is not None:
            init += [da_ref[:, cr], da_ref[:, ci]]
        fin = lax.fori_loop(0, n_groups, step, tuple(init))
        carry_ref[:, cr] = fin[0]
        carry_ref[:, ci] = fin[1]
        if h_ref is not None:
            da_ref[:, cr] = fin[2]
            da_ref[:, ci] = fin[3]


def _ssm_fwd(u, bmat, cmat, tab, d_skip, tb, exchange=None):
    seq = u.shape[0]

    def body(u_ref, b_ref, c_ref, t_ref, d_ref, s_ref, h_ref, carry_ref):
        @pl.when(pl.program_id(0) == 0)
        def _():
            carry_ref[...] = jnp.zeros_like(carry_ref)

        u_blk = u_ref[...]
        ub = _bf(u_blk)
        for sb in range(N_SUPER):
            h_ref[:, sb * SUPER_W:(sb + 1) * SUPER_W] = _mm(ub[:, sb * SUPER_IN:(sb + 1) * SUPER_IN], b_ref[sb])
        _scan_rows(h_ref, t_ref, carry_ref, tb // SUBLANES, False)
        ys = [_mm(_bf(h_ref[:, sb * SUPER_W:(sb + 1) * SUPER_W]), c_ref[sb]) for sb in range(N_SUPER)]
        s_ref[...] = jnp.concatenate(ys, axis=1) + d_ref[...] * u_blk

    return _rowcall("ssm_fwd", body, seq, tb, [u], [bmat, cmat, tab, d_skip],
                    [(SSM_W, F32), (STATE_COLS, F32)], [],
                    scratch=[pltpu.VMEM((SUBLANES, STATE_COLS), F32)], vmem=VMEM_BIG, exchange=exchange)


def _ssm_bwd(ds, u, h, bmat_t, cmat_t, tab, d_skip, tb, exchange=None):
    seq = u.shape[0]

    def body(ds_ref, u_ref, h_ref, bt_ref, ct_ref, t_ref, d_ref,
             du_ref, db_ref, dc_ref, da_ref, dd_ref, g_ref, carry_ref):
        @pl.when(pl.program_id(0) == 0)
        def _():
            carry_ref[...] = jnp.zeros_like(carry_ref)
            db_ref[...] = jnp.zeros_like(db_ref)
            dc_ref[...] = jnp.zeros_like(dc_ref)
            da_ref[...] = jnp.zeros_like(da_ref)
            dd_ref[...] = jnp.zeros_like(dd_ref)

        ds_blk = ds_ref[...]
        dsb = _bf(ds_blk)
        u_blk = u_ref[...]
        ub = _bf(u_blk)
        for sb in range(N_SUPER):
            g_ref[:, sb * SUPER_W:(sb + 1) * SUPER_W] = _mm(dsb[:, sb * SUPER_IN:(sb + 1) * SUPER_IN], ct_ref[sb])
        _scan_rows(g_ref, t_ref, carry_ref, tb // SUBLANES, True, h_ref=h_ref, da_ref=da_ref)
        dus = []
        for sb in range(N_SUPER):
            gb = _bf(g_ref[:, sb * SUPER_W:(sb + 1) * SUPER_W])
            dus.append(_mm(gb, bt_ref[sb]))
            db_ref[sb] += _mm_tn(ub[:, sb * SUPER_IN:(sb + 1) * SUPER_IN], gb)
            dc_ref[sb] += _mm_tn(_bf(h_ref[:, sb * SUPER_W:(sb + 1) * SUPER_W]),
                                 dsb[:, sb * SUPER_IN:(sb + 1) * SUPER_IN])
        du_ref[...] = jnp.concatenate(dus, axis=1) + d_ref[...] * ds_blk
        dd_ref[...] += jnp.sum(ds_blk * u_blk, axis=0, keepdims=True)

    return _rowcall("ssm_bwd", body, seq, tb, [ds, u, h], [bmat_t, cmat_t, tab, d_skip],
                    [(SSM_W, F32)],
                    [((N_SUPER, SUPER_IN, SUPER_W), F32), ((N_SUPER, SUPER_W, SUPER_IN), F32),
                     ((SUBLANES, STATE_COLS), F32), ((1, SSM_W), F32)],
                    scratch=[pltpu.VMEM((tb, STATE_COLS), F32), pltpu.VMEM((SUBLANES, STATE_COLS), F32)],
                    reverse=True, vmem=VMEM_BIG, exchange=exchange)


def _merge_core(s, attb, ga, gs, wg_ref, wab_ref, wsb_ref, wout_ref):
    zg, dgelu = _gelu_and_grad(s)
    zgb = _bf(zg)
    sg = _sig(_mm(zgb, wg_ref[...]))
    z = zg * sg
    zb = _bf(z)
    ys = jnp.concatenate([_mm(zb, wsb_ref[j]) for j in range(N_CHIPS)], axis=1)
    ya = jnp.concatenate([_mm(attb, wab_ref[j]) for j in range(N_CHIPS)], axis=1)
    sa = _sig(ga)
    ss = _sig(gs)
    mgb = _bf(sa * ya + ss * ys)
    o = _mm(mgb, wout_ref[...])
    return dict(zg=zg, dgelu=dgelu, zgb=zgb, sg=sg, zb=zb, ys=ys, ya=ya, sa=sa, ss=ss, mgb=mgb, o=o)


def _merge_fwd(x, s, att, ga, gs, g2, w_glu, w_ab, w_sb, w_out, tb):
    seq = x.shape[0]

    def body(x_ref, s_ref, att_ref, ga_ref, gs_ref, g_ref, wg_ref, wab_ref, wsb_ref, wout_ref, x2_ref):
        f = _merge_core(s_ref[...], att_ref[...], ga_ref[...], gs_ref[...], wg_ref, wab_ref, wsb_ref, wout_ref)
        n, _, _ = _rms(f["o"], g_ref[...])
        x2_ref[...] = x_ref[...] + n

    return _rowcall("merge_fwd", body, seq, tb, [x, s, att, ga, gs], [g2, w_glu, w_ab, w_sb, w_out],
                    [(D_MODEL, F32)], [], vmem=VMEM_BIG)[0]


def _merge_bwd(dx2, s, att, ga, gs, g2, w_glu, w_ab, w_sb, w_out, tb, exchange=None):
    seq = s.shape[0]
    cw = D_MODEL // N_CHIPS
    last = seq // tb - 1

    def body(dx2_ref, s_ref, att_ref, ga_ref, gs_ref, g_ref, wg_ref, wab_ref, wsb_ref, wout_ref,
             ds_ref, datt_ref, dga_ref, dgs_ref, dg_ref, dwg_ref, dwab_ref, dwsb_ref, dwout_ref,
             bwg_ref, bwab_ref, bwsb_ref, bwout_ref):
        @pl.when(pl.program_id(0) == 0)
        def _():
            for r in (dg_ref, dwg_ref, dwab_ref, dwsb_ref, dwout_ref):
                r[...] = jnp.zeros_like(r)

        attb = att_ref[...]
        f = _merge_core(s_ref[...], attb, ga_ref[...], gs_ref[...], wg_ref, wab_ref, wsb_ref, wout_ref)
        g = g_ref[...]
        _, oh, r2 = _rms(f["o"], g)
        do, dg = _rms_bwd(dx2_ref[...], oh, r2, g)
        dg_ref[...] += dg
        dob = _bf(do)
        dwout_ref[...] += _mm_tn(f["mgb"], dob)
        dmg = _mm_nt(dob, wout_ref[...])
        sa, ss = f["sa"], f["ss"]
        dyab = _bf(dmg * sa)
        dysb = _bf(dmg * ss)
        dga_ref[...] = _bf(dmg * f["ya"] * sa * (1.0 - sa))
        dgs_ref[...] = _bf(dmg * f["ys"] * ss * (1.0 - ss))
        dwab = _mm_tn(attb, dyab)
        dwsb = _mm_tn(f["zb"], dysb)
        datt = jnp.zeros((tb, ATTN_W), F32)
        dz = jnp.zeros((tb, SSM_W), F32)
        for j in range(N_CHIPS):
            dwab_ref[j] += dwab[:, j * cw:(j + 1) * cw]
            dwsb_ref[j] += dwsb[:, j * cw:(j + 1) * cw]
            datt = datt + _mm_nt(dyab[:, j * cw:(j + 1) * cw], wab_ref[j])
            dz = dz + _mm_nt(dysb[:, j * cw:(j + 1) * cw], wsb_ref[j])
        datt_ref[...] = _bf(datt)
        sg, zg = f["sg"], f["zg"]
        dglb = _bf(dz * zg * sg * (1.0 - sg))
        dwg_ref[...] += _mm_tn(f["zgb"], dglb)
        dzg = dz * sg + _mm_nt(dglb, wg_ref[...])
        ds_ref[...] = dzg * f["dgelu"]

        @pl.when(pl.program_id(0) == last)
        def _():
            for dst, src in ((bwg_ref, dwg_ref), (bwab_ref, dwab_ref), (bwsb_ref, dwsb_ref), (bwout_ref, dwout_ref)):
                dst[...] = _bf(src[...])

    shapes = [w_glu.shape, w_ab.shape, w_sb.shape, w_out.shape]
    return _rowcall("merge_bwd", body, seq, tb, [dx2, s, att, ga, gs], [g2, w_glu, w_ab, w_sb, w_out],
                    [(SSM_W, F32), (ATTN_W, BF16), (D_MODEL, BF16), (D_MODEL, BF16)],
                    [((1, D_MODEL), F32)] + [(sh, F32) for sh in shapes] + [(sh, BF16) for sh in shapes],
                    vmem=VMEM_BIG, exchange=exchange)


def _mlp_fwd_loss(x2, target, g3, g4, w_ffi, w_ffo, tb):
    seq = x2.shape[0]

    def body(x2_ref, t_ref, g3_ref, g4_ref, wi_ref, wo_ref, dy_ref, df_ref, h_ref, loss_ref, dg_ref):
        @pl.when(pl.program_id(0) == 0)
        def _():
            loss_ref[...] = jnp.zeros_like(loss_ref)
            dg_ref[...] = jnp.zeros_like(dg_ref)

        x2_blk = x2_ref[...]
        h3, _, _ = _rms(x2_blk, g3_ref[...])
        hb = _bf(h3)
        h_ref[...] = hb
        f = jnp.zeros((tb, D_MODEL), F32)
        for j in range(FF_CHUNKS):
            a = _mm(hb, wi_ref[j])
            f = f + _mm(_bf(jnp.square(jnp.maximum(a, 0.0))), wo_ref[j])
        g4 = g4_ref[...]
        n4, fh, r4 = _rms(f, g4)
        e = (x2_blk + n4) - t_ref[...]
        loss_ref[...] += 0.5 * jnp.sum(jnp.mean(e * e, axis=-1, keepdims=True))
        dy = e * (1.0 / D_MODEL)
        dy_ref[...] = dy
        df, dg = _rms_bwd(dy, fh, r4, g4)
        df_ref[...] = _bf(df)
        dg_ref[...] += dg

    return _rowcall("mlp_fwd_loss", body, seq, tb, [x2, target], [g3, g4, w_ffi, w_ffo],
                    [(D_MODEL, F32), (D_MODEL, BF16), (D_MODEL, BF16)],
                    [((SUBLANES, 128), F32), ((1, D_MODEL), F32)], vmem=VMEM_BIG)


def _mlp_bwd(x2, dy, df, h3, g3, w_ffi, w_ffo, tb):
    seq = x2.shape[0]
    cw = D_FF // FF_CHUNKS

    def body(x2_ref, dy_ref, df_ref, h_ref, g3_ref, wi_ref, wo_ref, dx_ref, act_ref, da_ref, dg_ref):
        @pl.when(pl.program_id(0) == 0)
        def _():
            dg_ref[...] = jnp.zeros_like(dg_ref)

        hb = h_ref[...]
        dfb = df_ref[...]
        dh = jnp.zeros((tb, D_MODEL), F32)
        for j in range(FF_CHUNKS):
            ra = jnp.maximum(_mm(hb, wi_ref[j]), 0.0)
            act_ref[:, j * cw:(j + 1) * cw] = _bf(ra * ra)
            dab = _bf(_mm_nt(dfb, wo_ref[j]) * (2.0 * ra))
            da_ref[:, j * cw:(j + 1) * cw] = dab
            dh = dh + _mm_nt(dab, wi_ref[j])
        g3 = g3_ref[...]
        _, xh, r3 = _rms(x2_ref[...], g3)
        dxn, dg = _rms_bwd(dh, xh, r3, g3)
        dx_ref[...] = dy_ref[...] + dxn
        dg_ref[...] += dg

    return _rowcall("mlp_bwd", body, seq, tb, [x2, dy, df, h3], [g3, w_ffi, w_ffo],
                    [(D_MODEL, F32), (D_FF, BF16), (D_FF, BF16)], [((1, D_MODEL), F32)], vmem=VMEM_BIG)


def _matmul_tn(name, a, b, tk, tn, tl, chunk_major, exchange=None):
    seq, kdim = a.shape
    ndim = b.shape[1]
    last = seq // tl - 1

    def body(a_ref, b_ref, o_ref, ob_ref):
        @pl.when(pl.program_id(2) == 0)
        def _():
            o_ref[...] = jnp.zeros_like(o_ref)

        o_ref[...] += _mm_tn(a_ref[...], b_ref[...])

        @pl.when(pl.program_id(2) == last)
        def _():
            ob_ref[...] = _bf(o_ref[...])

    if chunk_major:
        shape = (ndim // tn, kdim, tn)
        out_spec = pl.BlockSpec((None, tk, tn), lambda k, n, l: (n, k, 0))
    else:
        shape = (kdim, ndim)
        out_spec = pl.BlockSpec((tk, tn), lambda k, n, l: (k, n))
    return _fused_call(
        name, body, (kdim // tk, ndim // tn, seq // tl),
        [pl.BlockSpec((tl, tk), lambda k, n, l: (l, k)), pl.BlockSpec((tl, tn), lambda k, n, l: (l, n))],
        [out_spec, out_spec], [SDS(shape, F32), SDS(shape, BF16)], [], [a, b], exchange, _params(3, VMEM_BIG))


def _ew_call(name, fn, ins, n_out):
    rows, cols = ins[0].shape
    tr = rows
    while tr * cols * 4 > (1 << 20) and tr % 16 == 0:
        tr //= 2
    spec = pl.BlockSpec((tr, cols), lambda i: (i, 0))

    def body(*refs):
        outs = fn(*[r[...] for r in refs[:len(ins)]])
        for r, o in zip(refs[len(ins):], outs):
            r[...] = o

    return pl.pallas_call(
        body, grid=(rows // tr,), in_specs=[spec] * len(ins), out_specs=[spec] * n_out,
        out_shape=[SDS((rows, cols), F32)] * n_out, name=name, compiler_params=_params(1),
    )(*ins)


def _adam_math(w, g, m, v):
    m2 = ADAM_B1 * m + (1.0 - ADAM_B1) * g
    v2 = ADAM_B2 * v + (1.0 - ADAM_B2) * (g * g)
    m_hat = m2 / (1.0 - ADAM_B1 ** ADAM_STEP)
    v_hat = v2 / (1.0 - ADAM_B2 ** ADAM_STEP)
    delta = -ADAM_LR * (m_hat / (jnp.sqrt(v_hat) + ADAM_EPS) + ADAM_WD * w)
    return delta, m2, v2


def _sum4(name, own, recv, idx):
    _, rows, cols = own.shape
    tr = rows
    while tr * cols * 4 > (1 << 20) and tr % 16 == 0:
        tr //= 2

    def body(idx_ref, o_ref, r0_ref, r1_ref, r2_ref, out_ref):
        out_ref[...] = ((o_ref[...] + r0_ref[...].astype(F32)) + r1_ref[...].astype(F32)) + r2_ref[...].astype(F32)

    blk = (None, tr, cols)
    grid_spec = pltpu.PrefetchScalarGridSpec(
        num_scalar_prefetch=1, grid=(rows // tr,),
        in_specs=[pl.BlockSpec(blk, lambda i, s: (s[0], i, 0)), pl.BlockSpec(blk, lambda i, s: (0, i, 0)),
                  pl.BlockSpec(blk, lambda i, s: (1, i, 0)), pl.BlockSpec(blk, lambda i, s: (2, i, 0))],
        out_specs=pl.BlockSpec((tr, cols), lambda i, s: (i, 0)))
    return pl.pallas_call(body, grid_spec=grid_spec, out_shape=SDS((rows, cols), F32), name=name,
                          compiler_params=_params(1))(jnp.reshape(idx, (1,)).astype(jnp.int32), own, recv, recv, recv)


def _adam_pair(name, w, p_own, p_sib, m, v):
    def fn(w_, a, b, m_, v_):
        g = a + b
        return (g,) + _adam_math(w_, g, m_, v_)

    return _ew_call(name, fn, [w, p_own, p_sib, m, v], 4)


def _place():
    return lax.axis_index("x"), lax.axis_index("y"), lax.axis_index("c")


def _other_chips(x, y):
    return [(1 - x, y), (x, 1 - y), (1 - x, 1 - y)]


def _gather_chips(shards):
    n = len(shards)

    def copies(ins, outs, sems):
        send, recv, loc = sems
        x, y, c = _place()
        me = 2 * x + y
        peers = _other_chips(x, y)
        local = [pltpu.make_async_copy(ins[a], outs[a].at[me], loc.at[a]) for a in range(n)]

        def copy(a, j, slot):
            px, py = peers[j]
            return pltpu.make_async_remote_copy(
                src_ref=ins[a], dst_ref=outs[a].at[slot], send_sem=send.at[a, j], recv_sem=recv.at[a, j],
                device_id=(px, py, c), device_id_type=MESH_ID)

        sends = [copy(a, j, me) for a in range(n) for j in range(3)]
        recvs = [copy(a, j, 2 * px + py) for a in range(n) for j, (px, py) in enumerate(peers)]
        return local, sends, recvs

    def start(ins, outs, sems):
        local, sends, _ = copies(ins, outs, sems)
        for cp in local + sends:
            cp.start()

    def wait(ins, outs, sems):
        local, sends, recvs = copies(ins, outs, sems)
        for cp in recvs:
            cp.wait_recv()
        for cp in sends:
            cp.wait_send()
        for cp in local:
            cp.wait()

    return _Exchange(shards, [SDS((N_CHIPS,) + s.shape, s.dtype) for s in shards],
                     [pltpu.SemaphoreType.DMA((n, 3)), pltpu.SemaphoreType.DMA((n, 3)), pltpu.SemaphoreType.DMA((n,))],
                     start, wait)


def _scatter_chips(chunks):
    n = len(chunks)

    def copies(ins, outs, sems):
        send, recv = sems
        x, y, c = _place()
        return [pltpu.make_async_remote_copy(
            src_ref=ins[a].at[2 * px + py], dst_ref=outs[a].at[j], send_sem=send.at[a, j],
            recv_sem=recv.at[a, j], device_id=(px, py, c), device_id_type=MESH_ID)
            for a in range(n) for j, (px, py) in enumerate(_other_chips(x, y))]

    def start(ins, outs, sems):
        for cp in copies(ins, outs, sems):
            cp.start()

    def wait(ins, outs, sems):
        cps = copies(ins, outs, sems)
        for cp in cps:
            cp.wait_recv()
        for cp in cps:
            cp.wait_send()

    return _Exchange(chunks, [SDS((3,) + s.shape[1:], s.dtype) for s in chunks],
                     [pltpu.SemaphoreType.DMA((n, 3)), pltpu.SemaphoreType.DMA((n, 3))], start, wait)


def _swap_sibling(arrs):
    n = len(arrs)

    def copies(ins, outs, sems):
        send, recv = sems
        x, y, c = _place()
        return [pltpu.make_async_remote_copy(
            src_ref=ins[a], dst_ref=outs[a], send_sem=send.at[a], recv_sem=recv.at[a],
            device_id=(x, y, 1 - c), device_id_type=MESH_ID) for a in range(n)]

    def start(ins, outs, sems):
        for cp in copies(ins, outs, sems):
            cp.start()

    def wait(ins, outs, sems):
        cps = copies(ins, outs, sems)
        for cp in cps:
            cp.wait_recv()
        for cp in cps:
            cp.wait_send()

    return _Exchange(arrs, [SDS(s.shape, s.dtype) for s in arrs],
                     [pltpu.SemaphoreType.DMA((n,)), pltpu.SemaphoreType.DMA((n,))], start, wait)


N_DEV = 8


def _gather_devices(block):
    def copies(ins, outs, sems):
        send, recv, loc = sems
        x, y, c = _place()
        me = 4 * x + 2 * y + c
        local = pltpu.make_async_copy(ins[0], outs[0].at[me], loc.at[0])
        sends, recvs = [], []
        for k in range(1, N_DEV):
            peer = (x ^ (k >> 2), y ^ ((k >> 1) & 1), c ^ (k & 1))
            for group, slot in ((sends, me), (recvs, me ^ k)):
                group.append(pltpu.make_async_remote_copy(
                    src_ref=ins[0], dst_ref=outs[0].at[slot], send_sem=send.at[k - 1], recv_sem=recv.at[k - 1],
                    device_id=peer, device_id_type=MESH_ID))
        return local, sends, recvs

    def start(ins, outs, sems):
        local, sends, _ = copies(ins, outs, sems)
        for cp in [local] + sends:
            cp.start()

    def wait(ins, outs, sems):
        local, sends, recvs = copies(ins, outs, sems)
        for cp in recvs:
            cp.wait_recv()
        for cp in sends:
            cp.wait_send()
        local.wait()

    return _Exchange([block], [SDS((N_DEV,) + block.shape, block.dtype)],
                     [pltpu.SemaphoreType.DMA((N_DEV - 1,)), pltpu.SemaphoreType.DMA((N_DEV - 1,)),
                      pltpu.SemaphoreType.DMA((1,))], start, wait)


def _both(ex_a, ex_b):
    na_i, na_o, na_s = len(ex_a.ins), len(ex_a.outs), len(ex_a.sems)

    def start(ins, outs, sems):
        ex_a.start(ins[:na_i], outs[:na_o], sems[:na_s])
        ex_b.start(ins[na_i:], outs[na_o:], sems[na_s:])

    def wait(ins, outs, sems):
        ex_a.wait(ins[:na_i], outs[:na_o], sems[:na_s])
        ex_b.wait(ins[na_i:], outs[na_o:], sems[na_s:])

    return _Exchange(ex_a.ins + ex_b.ins, ex_a.outs + ex_b.outs, ex_a.sems + ex_b.sems, start, wait)


def _sum_devices(slots):
    def body(s_ref, o_ref):
        acc = s_ref[0]
        for d in range(1, N_DEV):
            acc = acc + s_ref[d]
        o_ref[...] = acc

    return pl.pallas_call(
        body, in_specs=[pl.BlockSpec(memory_space=pltpu.VMEM)], out_specs=pl.BlockSpec(memory_space=pltpu.VMEM),
        out_shape=SDS(slots.shape[1:], F32), name="sum_small",
        compiler_params=pltpu.CompilerParams(vmem_limit_bytes=32 * 1024 * 1024))(slots)


def _adam_small(ws, gs, ms, vs):
    n = len(ws)

    def body(*refs):
        for i in range(n):
            w_ref, g_ref, m_ref, v_ref = (refs[k * n + i] for k in range(4))
            outs = _adam_math(w_ref[...], g_ref[...], m_ref[...], v_ref[...])
            for k in range(3):
                refs[(4 + k) * n + i][...] = outs[k]

    vmem = pl.BlockSpec(memory_space=pltpu.VMEM)
    return pl.pallas_call(
        body, in_specs=[vmem] * (4 * n), out_specs=[vmem] * (3 * n),
        out_shape=[SDS(w.shape, F32) for w in ws] * 3, name="adam_small",
        compiler_params=pltpu.CompilerParams(vmem_limit_bytes=32 * 1024 * 1024))(*ws, *gs, *ms, *vs)


def _local_step(x, target, small, big, tb, distributed):
    g1, g2, g3, g4 = small["norm_mix_pre"], small["norm_mix_post"], small["norm_mlp_pre"], small["norm_mlp_post"]
    dist = distributed
    me = (2 * lax.axis_index("x") + lax.axis_index("y")) if dist else 0
    tb_ssm = min(tb, 256)
    bucket = jnp.asarray(_bucket_table())

    keys_first = lambda t: jnp.swapaxes(t, -1, -2)
    bias = keys_first(_pair_layout(_bias_table(small["rel_bias"], bucket)))
    sink_rows = keys_first(_pair_layout(jnp.broadcast_to(small["sinks"].reshape(N_HEADS, 1, 1), (N_HEADS, BLOCK, 1))))
    disc_args = (small["lam_re"], small["lam_im"], small["log_dt"], small["b_re"], small["b_im"])
    (ab_re, ab_im, bb_re, bb_im), disc_vjp = jax.vjp(_ssm_discretize, *disc_args)
    tab_f, tab_b = _scan_tables(ab_re, ab_im)
    bmat = _bf(_b_matrix(bb_re, bb_im))
    cmat = _bf(_c_matrix(small["c_re"], small["c_im"]))
    d_skip = small["d_skip"]

    if dist:
        (g_in,) = _exchange_alone("gather_w_in", _gather_chips([big["w_in"]]))
        w_in = g_in.reshape(IN_W, D_MODEL)
    else:
        w_in = big["w_in"]
    mix = ("w_glu", "w_attn_branch", "w_ssm_branch", "w_out")
    outs = _inproj_fwd(x, g1, w_in, tb, _gather_chips([big[n] for n in mix]) if dist else None)
    h1, q, k, v, u, ga, gs = outs[:7]
    w_glu, w_ab, w_sb, w_out = outs[7:] if dist else [big[n] for n in mix]
    w_glu = w_glu.reshape(SSM_W, SSM_W)
    w_out = w_out.reshape(D_MODEL, D_MODEL)
    outs = _attn_fwd(q, k, v, bias, sink_rows, _gather_chips([big["w_ff_in"]]) if dist else None)
    att = outs[0]
    w_ffi = outs[1] if dist else big["w_ff_in"]
    outs = _ssm_fwd(u, bmat, cmat, tab_f, d_skip, tb_ssm, _gather_chips([big["w_ff_out"]]) if dist else None)
    s, h = outs[:2]
    w_ffo = outs[2] if dist else big["w_ff_out"]
    x2 = _merge_fwd(x, s, att, ga, gs, g2, w_glu, w_ab, w_sb, w_out, tb)
    dy, df, h3, loss_acc, dg4 = _mlp_fwd_loss(x2, target, g3, g4, w_ffi, w_ffo, tb)

    dx2, act, da, dg3 = _mlp_bwd(x2, dy, df, h3, g3, w_ffi, w_ffo, tb)
    tl = min(512, x.shape[0])
    chunked = (N_CHIPS, D_FF // N_CHIPS, D_MODEL)
    d_ffi, b_ffi = _matmul_tn("grad_w_ff_in", h3, da, D_MODEL, D_FF // FF_CHUNKS, tl, True)
    d_ffo, b_ffo = _matmul_tn("grad_w_ff_out", act, df, D_FF // FF_CHUNKS, D_MODEL, tl, False)
    d_ffo, b_ffo = d_ffo.reshape(chunked), b_ffo.reshape(chunked)
    outs = _merge_bwd(dx2, s, att, ga, gs, g2, w_glu, w_ab, w_sb, w_out, tb_ssm,
                      _scatter_chips([b_ffi]) if dist else None)
    ds, datt, dga, dgs, dg2, d_glu, d_ab, d_sb, d_out, b_glu, b_ab, b_sb, b_out = outs[:13]
    r_ffi = outs[13:]
    glu4, out4 = (N_CHIPS, SSM_W // N_CHIPS, SSM_W), (N_CHIPS, D_MODEL // N_CHIPS, D_MODEL)
    d_mix = [d_glu.reshape(glu4), d_ab, d_sb, d_out.reshape(out4)]
    b_mix = [b_glu.reshape(glu4), b_ab, b_sb, b_out.reshape(out4)]
    outs = _ssm_bwd(ds, u, h, bmat.transpose(0, 2, 1), cmat.transpose(0, 2, 1), tab_b, d_skip, tb_ssm,
                    _scatter_chips([b_ffo]) if dist else None)
    du, d_bmat, d_cmat, da_acc, dd_skip = outs[:5]
    r_ffo = outs[5:]
    outs = _attn_bwd(q, k, v, datt, bias, sink_rows, _scatter_chips(b_mix) if dist else None)
    dq, dk, dv, dbias, dsink_rows = outs[:5]
    r_mix = outs[5:]
    if dist:
        p_ffi = _sum4("sum_w_ff_in", d_ffi, r_ffi[0], me)
        p_ffo = _sum4("sum_w_ff_out", d_ffo, r_ffo[0], me)
    dx, dpj, dg1 = _inproj_bwd(x, dx2, dq, dk, dv, du, dga, dgs, g1, w_in, tb)

    dab_re, dab_im = _state_unlayout(jnp.sum(da_acc, axis=0))
    dbb_re, dbb_im = _b_matrix_grad(d_bmat)
    d_lam_re, d_lam_im, d_log_dt, d_b_re, d_b_im = disc_vjp((dab_re, dab_im, dbb_re, dbb_im))
    d_c_re, d_c_im = _c_matrix_grad(d_cmat)
    d_rel = _bias_grad(_pair_unlayout(keys_first(dbias)), bucket)
    d_sinks = jnp.sum(_pair_unlayout(keys_first(dsink_rows)), axis=(1, 2))
    small_grads = dict(
        norm_mix_pre=dg1, norm_mix_post=dg2, norm_mlp_pre=dg3, norm_mlp_post=dg4, rel_bias=d_rel, sinks=d_sinks,
        lam_re=d_lam_re, lam_im=d_lam_im, log_dt=d_log_dt, b_re=d_b_re, b_im=d_b_im, c_re=d_c_re, c_im=d_c_im,
        d_skip=dd_skip)
    ride = _both(_swap_sibling([p_ffi, p_ffo]), _gather_devices(_pack(small_grads, loss_acc))) if dist else None
    outs = _matmul_tn("grad_w_in", dpj, h1, IN_W // 2, D_MODEL, tl, False, ride)
    in4 = (N_CHIPS, IN_W // N_CHIPS, D_MODEL)
    d_in, b_in = outs[0].reshape(in4), outs[1].reshape(in4)
    if not dist:
        return loss_acc, dx, small_grads, dict(zip(BIG, [d_in] + d_mix + [d_ffi, d_ffo]))
    s_ffi, s_ffo, slots = outs[2:]
    (r_in,) = _exchange_alone("scatter_w_in", _scatter_chips([b_in]))
    parts = [_sum4("sum_" + n, d, r, me) for n, d, r in zip(("w_in",) + mix, [d_in] + d_mix, [r_in] + list(r_mix))]
    sibs = _exchange_alone("swap_rest", _swap_sibling(parts))
    parts += [p_ffi, p_ffo]
    sibs = list(sibs) + [s_ffi, s_ffo]
    return loss_acc, dx, _sum_devices(slots), dict(zip(BIG, zip(parts, sibs)))


SMALL = ['norm_mix_pre', 'norm_mix_post', 'norm_mlp_pre', 'norm_mlp_post', 'rel_bias', 'sinks', 'lam_re', 'lam_im',
         'log_dt', 'b_re', 'b_im', 'c_re', 'c_im', 'd_skip']
BIG = ['w_in', 'w_glu', 'w_attn_branch', 'w_ssm_branch', 'w_out', 'w_ff_in', 'w_ff_out']
WEIGHTS = ['norm_mix_pre', 'norm_mix_post', 'norm_mlp_pre', 'norm_mlp_post', 'w_in', 'rel_bias', 'sinks', 'lam_re',
           'lam_im', 'log_dt', 'b_re', 'b_im', 'c_re', 'c_im', 'd_skip', 'w_glu', 'w_attn_branch', 'w_ssm_branch',
           'w_out', 'w_ff_in', 'w_ff_out']
PACK_COLS = 1024
PACK_ORDER = ['b_re', 'b_im', 'c_re', 'c_im', 'lam_re', 'lam_im', 'norm_mix_pre', 'norm_mix_post', 'norm_mlp_pre',
              'norm_mlp_post', 'rel_bias', 'sinks', 'log_dt', 'd_skip']


STATE_MINOR = ('b_re', 'b_im')
PACK_ROWS = 144
LOSS_ROW = 140


def _pack(named, loss_acc):
    parts = []
    for n in PACK_ORDER:
        a = jnp.swapaxes(named[n], -1, -2) if n in STATE_MINOR else named[n]
        flat = a.reshape(-1)
        rows = -(-flat.shape[0] // PACK_COLS)
        parts.append(jnp.pad(flat, (0, rows * PACK_COLS - flat.shape[0])).reshape(rows, PACK_COLS))
    assert sum(p.shape[0] for p in parts) == LOSS_ROW
    parts.append(jnp.pad(loss_acc[0:1], ((0, PACK_ROWS - LOSS_ROW - 1), (0, PACK_COLS - loss_acc.shape[1]))))
    return jnp.concatenate(parts, axis=0)


def _unpack(packed, shapes):
    out, at = {}, 0
    for n in PACK_ORDER:
        shape = shapes[n][:-2] + (shapes[n][-1], shapes[n][-2]) if n in STATE_MINOR else shapes[n]
        size = int(np.prod(shape))
        rows = -(-size // PACK_COLS)
        blk = packed[at:at + rows]
        out[n] = (blk.reshape(-1)[:size] if size % PACK_COLS else blk).reshape(shape)
        at += rows
    return out


def kernel(x, norm_mix_pre, norm_mix_post, norm_mlp_pre, norm_mlp_post, w_in, rel_bias, sinks, lam_re, lam_im, log_dt, b_re, b_im, c_re, c_im, d_skip, w_glu, w_attn_branch, w_ssm_branch, w_out, w_ff_in, w_ff_out, loss_target, m_norm_mix_pre, m_norm_mix_post, m_norm_mlp_pre, m_norm_mlp_post, m_w_in, m_rel_bias, m_sinks, m_lam_re, m_lam_im, m_log_dt, m_b_re, m_b_im, m_c_re, m_c_im, m_d_skip, m_w_glu, m_w_attn_branch, m_w_ssm_branch, m_w_out, m_w_ff_in, m_w_ff_out, v_norm_mix_pre, v_norm_mix_post, v_norm_mlp_pre, v_norm_mlp_post, v_w_in, v_rel_bias, v_sinks, v_lam_re, v_lam_im, v_log_dt, v_b_re, v_b_im, v_c_re, v_c_im, v_d_skip, v_w_glu, v_w_attn_branch, v_w_ssm_branch, v_w_out, v_w_ff_in, v_w_ff_out):
    env = dict(locals())
    w = {n: env[n] for n in WEIGHTS}
    m = {n: env["m_" + n] for n in WEIGHTS}
    v = {n: env["v_" + n] for n in WEIGHTS}
    seq = x.shape[1]
    tb = min(512, seq)

    small = {n: w[n] for n in ('norm_mix_pre', 'norm_mix_post', 'norm_mlp_pre', 'norm_mlp_post', 'rel_bias')}
    small.update({n: w[n][0] for n in ('sinks', 'lam_re', 'lam_im', 'log_dt', 'b_re', 'b_im', 'c_re', 'c_im')})
    small['d_skip'] = w['d_skip']
    shard = lambda t, n: t[n][0].T if n == 'w_in' else t[n][0]
    unshard = lambda a, n: (a.T if n == 'w_in' else a)[None]
    _, dx, small_g, big_g = _local_step(
        x[0], loss_target[0], small, {n: _bf(shard(w, n)) for n in BIG}, tb, True)

    loss = small_g[LOSS_ROW, 0]

    grads, deltas, new_m, new_v = {}, {}, {}, {}
    for n in BIG:
        p_own, p_sib = big_g[n]
        outs = _adam_pair("adam_" + n, shard(w, n), p_own, p_sib, shard(m, n), shard(v, n))
        grads[n], deltas[n], new_m[n], new_v[n] = [unshard(a, n) for a in outs]

    minor = lambda t, n: jnp.swapaxes(t, -1, -2) if n in STATE_MINOR else t
    g_small = _unpack(small_g, {n: w[n].shape for n in SMALL})
    outs = _adam_small([minor(w[n], n) for n in SMALL], [g_small[n] for n in SMALL],
                       [minor(m[n], n) for n in SMALL], [minor(v[n], n) for n in SMALL])
    grads.update({n: minor(g_small[n], n) for n in SMALL})
    for k, dst in enumerate((deltas, new_m, new_v)):
        dst.update({n: minor(a, n) for n, a in zip(SMALL, outs[k * len(SMALL):(k + 1) * len(SMALL)])})

    return (loss, dx[None], *[grads[n] for n in WEIGHTS], *[deltas[n] for n in WEIGHTS],
            *[new_m[n] for n in WEIGHTS], *[new_v[n] for n in WEIGHTS])
```

```python
import functools
import math

import numpy as np
import jax
import jax.numpy as jnp
from jax import lax
from jax.experimental import pallas as pl
from jax.experimental.pallas import tpu as pltpu

F32 = jnp.float32
BF16 = jnp.bfloat16

D_MODEL = 1024
N_HEADS = 8
N_KV = 2
Q_GROUP = 4
HEAD_DIM = 64
ATTN_W = 512
KV_W = 128
BLOCK = 128
N_BUCKETS = 32
MAX_DISTANCE = 128
NEG_INF = -1e30
SSM_W = 512
SSM_GROUP = 16
SSM_GROUPS = 32
SSM_STATE = 64
N_SUPER = 4
GROUPS_PER_SUPER = SSM_GROUPS // N_SUPER
SUPER_IN = GROUPS_PER_SUPER * SSM_GROUP
SUPER_HALF = GROUPS_PER_SUPER * SSM_STATE
SUPER_W = 2 * SUPER_HALF
STATE_COLS = N_SUPER * SUPER_W
D_FF = 4096
FF_CHUNKS = 4
IN_W = 3328
SPLITS = (0, 512, 640, 768, 1280, 2304, 3328)
RMS_EPS = 1e-6
N_CHIPS = 4
SUBLANES = 8

ADAM_LR = 0.001
ADAM_B1 = 0.9
ADAM_B2 = 0.999
ADAM_EPS = 1e-08
ADAM_WD = 0.01
ADAM_STEP = 10

VMEM_BIG = 56 * 1024 * 1024
SDS = jax.ShapeDtypeStruct
MESH_ID = pl.DeviceIdType.MESH
ANY = pl.BlockSpec(memory_space=pl.ANY)


def _bf(x):
    return x.astype(BF16)


def _mm(a, b):
    return jnp.dot(a, b, preferred_element_type=F32)


def _mm_nt(a, b):
    return lax.dot_general(a, b, (((1,), (1,)), ((), ())), preferred_element_type=F32)


def _mm_tn(a, b):
    return lax.dot_general(a, b, (((0,), (0,)), ((), ())), preferred_element_type=F32)


def _sig(x):
    return 1.0 / (1.0 + jnp.exp(-x))


def _rms(x, g):
    r = lax.rsqrt(jnp.mean(x * x, axis=-1, keepdims=True) + RMS_EPS)
    xh = x * r
    return xh * g, xh, r


def _rms_bwd(dout, xh, r, g):
    dg = jnp.sum(dout * xh, axis=0, keepdims=True)
    dxh = dout * g
    dx = r * (dxh - xh * jnp.mean(dxh * xh, axis=-1, keepdims=True))
    return dx, dg


_GELU_C = math.sqrt(2.0 / math.pi)


def _gelu_and_grad(x):
    x2 = x * x
    inner = _GELU_C * (x + 0.044715 * (x2 * x))
    t = jnp.tanh(inner)
    y = 0.5 * x * (1.0 + t)
    dy = 0.5 * (1.0 + t) + 0.5 * x * (1.0 - t * t) * (_GELU_C * (1.0 + 3.0 * 0.044715 * x2))
    return y, dy


def _zero_map(nd, *_):
    return (0,) * nd


def _params(n_axes, vmem=None):
    return pltpu.CompilerParams(dimension_semantics=("arbitrary",) * n_axes, vmem_limit_bytes=vmem)


class _Exchange:
    def __init__(self, ins, outs, sems, start, wait):
        self.ins, self.outs, self.sems, self.start, self.wait = list(ins), list(outs), list(sems), start, wait


def _fused_call(name, body, grid, in_specs, out_specs, out_shape, scratch, args, exchange, params):
    n_in, n_out, n_scr = len(in_specs), len(out_specs), len(scratch)
    if exchange is None:
        fn = body
    else:
        ex = exchange
        n_xi, n_xo = len(ex.ins), len(ex.outs)

        def fn(*refs):
            at = 0
            parts = []
            for n in (n_in, n_xi, n_out, n_xo, n_scr, len(ex.sems)):
                parts.append(refs[at:at + n])
                at += n
            ins, x_in, outs, x_out, scr, x_sem = parts
            ids = [pl.program_id(a) for a in range(len(grid))]
            first = functools.reduce(jnp.logical_and, [i == 0 for i in ids])
            last = functools.reduce(jnp.logical_and, [i == g - 1 for i, g in zip(ids, grid)])

            @pl.when(first)
            def _():
                ex.start(x_in, x_out, x_sem)

            body(*ins, *outs, *scr)

            @pl.when(last)
            def _():
                ex.wait(x_in, x_out, x_sem)

        in_specs = list(in_specs) + [ANY] * n_xi
        out_specs = list(out_specs) + [ANY] * n_xo
        out_shape = list(out_shape) + ex.outs
        scratch = list(scratch) + ex.sems
        args = list(args) + ex.ins
    return pl.pallas_call(fn, grid=grid, in_specs=in_specs, out_specs=out_specs, out_shape=out_shape,
                          scratch_shapes=list(scratch), name=name, compiler_params=params)(*args)


def _exchange_alone(name, ex):
    def body(*refs):
        n_xi, n_xo = len(ex.ins), len(ex.outs)
        x_in, x_out, x_sem = refs[:n_xi], refs[n_xi:n_xi + n_xo], refs[n_xi + n_xo:]
        ex.start(x_in, x_out, x_sem)
        ex.wait(x_in, x_out, x_sem)

    return pl.pallas_call(body, in_specs=[ANY] * len(ex.ins), out_specs=[ANY] * len(ex.outs), out_shape=ex.outs,
                          scratch_shapes=ex.sems, name=name)(*ex.ins)


def _rowcall(name, body, seq, tb, rows, consts, row_outs, acc_outs, scratch=(), reverse=False, vmem=None,
             exchange=None):
    nb = seq // tb
    rmap = (lambda i: (nb - 1 - i, 0)) if reverse else (lambda i: (i, 0))
    in_specs = [pl.BlockSpec((tb, a.shape[1]), rmap) for a in rows]
    in_specs += [pl.BlockSpec(a.shape, functools.partial(_zero_map, a.ndim), pipeline_mode=pl.Buffered(1))
                 for a in consts]
    out_specs = [pl.BlockSpec((tb, c), rmap) for c, _ in row_outs]
    out_specs += [pl.BlockSpec(s, functools.partial(_zero_map, len(s))) for s, _ in acc_outs]
    out_shape = [SDS((seq, c), dt) for c, dt in row_outs] + [SDS(s, dt) for s, dt in acc_outs]
    return _fused_call(name, body, (nb,), in_specs, out_specs, out_shape, list(scratch), [*rows, *consts],
                       exchange, _params(1, vmem))


def _inproj_fwd(x, g1, w_in, tb, exchange=None):
    seq = x.shape[0]

    def body(x_ref, g_ref, w_ref, h_ref, q_ref, k_ref, v_ref, u_ref, ga_ref, gs_ref):
        h, _, _ = _rms(x_ref[...], g_ref[...])
        hb = _bf(h)
        h_ref[...] = hb
        pj = _mm_nt(hb, w_ref[...])
        q_ref[...] = _bf(pj[:, SPLITS[0]:SPLITS[1]])
        k_ref[...] = _bf(pj[:, SPLITS[1]:SPLITS[2]])
        v_ref[...] = _bf(pj[:, SPLITS[2]:SPLITS[3]])
        u_ref[...] = pj[:, SPLITS[3]:SPLITS[4]]
        ga_ref[...] = pj[:, SPLITS[4]:SPLITS[5]]
        gs_ref[...] = pj[:, SPLITS[5]:SPLITS[6]]

    return _rowcall("inproj_fwd", body, seq, tb, [x], [g1, w_in],
                    [(D_MODEL, BF16), (ATTN_W, BF16), (KV_W, BF16), (KV_W, BF16), (SSM_W, F32),
                     (D_MODEL, F32), (D_MODEL, F32)], [], vmem=VMEM_BIG, exchange=exchange)


def _inproj_bwd(x, dx2, dq, dk, dv, du, dga, dgs, g1, w_in, tb, exchange=None):
    seq = x.shape[0]

    def body(x_ref, dx2_ref, dq_ref, dk_ref, dv_ref, du_ref, dga_ref, dgs_ref, g_ref, w_ref,
             dx_ref, dpj_ref, dg_ref):
        @pl.when(pl.program_id(0) == 0)
        def _():
            dg_ref[...] = jnp.zeros_like(dg_ref)

        dpj = jnp.concatenate([dq_ref[...], dk_ref[...], dv_ref[...], _bf(du_ref[...]),
                               dga_ref[...], dgs_ref[...]], axis=1)
        dpj_ref[...] = dpj
        dh = _mm(dpj, w_ref[...])
        g = g_ref[...]
        _, xh, r = _rms(x_ref[...], g)
        dxn, dg = _rms_bwd(dh, xh, r, g)
        dx_ref[...] = dx2_ref[...] + dxn
        dg_ref[...] += dg

    return _rowcall("inproj_bwd", body, seq, tb, [x, dx2, dq, dk, dv, du, dga, dgs], [g1, w_in],
                    [(D_MODEL, F32), (IN_W, BF16)], [((1, D_MODEL), F32)], vmem=VMEM_BIG, exchange=exchange)


def _bucket_table():
    qi = np.arange(BLOCK)[:, None]
    kj = np.arange(2 * BLOCK)[None, :]
    dist = qi + BLOCK - kj
    max_exact = N_BUCKETS // 2
    d = np.maximum(dist, 0)
    df = np.maximum(d, 1).astype(np.float32)
    large = max_exact + (np.log(df / np.float32(max_exact)) / np.float32(math.log(MAX_DISTANCE / max_exact))
                         * np.float32(N_BUCKETS - max_exact)).astype(np.int32)
    large = np.minimum(large, N_BUCKETS - 1)
    bucket = np.where(d < max_exact, d, large)
    valid = (dist >= 0) & (dist < BLOCK)
    return np.where(valid, bucket, -1).astype(np.int32)


def _bias_table(rel_bias, bucket):
    def body(rb_ref, bk_ref, o_ref):
        bk = bk_ref[...]
        has_prev = lax.broadcasted_iota(jnp.int32, bk.shape, 1) >= BLOCK
        for h in range(N_HEADS):
            acc = jnp.full((BLOCK, 2 * BLOCK), NEG_INF, F32)
            for b in range(N_BUCKETS):
                acc = jnp.where(bk == b, rb_ref[b, h], acc)
            o_ref[0, h] = jnp.where(has_prev, acc, NEG_INF)
            o_ref[1, h] = acc

    return pl.pallas_call(
        body, out_shape=SDS((2, N_HEADS, BLOCK, 2 * BLOCK), F32),
        in_specs=[pl.BlockSpec(memory_space=pltpu.SMEM), pl.BlockSpec(memory_space=pltpu.VMEM)],
        out_specs=pl.BlockSpec(memory_space=pltpu.VMEM), name="bias_table",
    )(rel_bias, bucket)


def _bias_grad(dbias, bucket):
    def body(db_ref, bk_ref, o_ref):
        bk = bk_ref[...]
        for h in range(N_HEADS):
            db = db_ref[h]
            for b in range(N_BUCKETS):
                o_ref[b, h] = jnp.sum(jnp.where(bk == b, db, 0.0))

    return pl.pallas_call(
        body, out_shape=SDS((N_BUCKETS, N_HEADS), F32),
        in_specs=[pl.BlockSpec(memory_space=pltpu.VMEM), pl.BlockSpec(memory_space=pltpu.VMEM)],
        out_specs=pl.BlockSpec(memory_space=pltpu.SMEM), name="bias_grad",
    )(dbias, bucket)


TILE = 2 * HEAD_DIM


def _pair_layout(t):
    lead = t.shape[:-3]
    t = t.reshape(lead + (N_KV, 2, 2) + t.shape[-2:])
    nl = len(lead)
    t = jnp.transpose(t, tuple(range(nl)) + (nl, nl + 2, nl + 1, nl + 3, nl + 4))
    return t.reshape(lead + (N_KV, 2, 2 * BLOCK, t.shape[-1]))


def _pair_unlayout(t):
    t = t.reshape(N_KV, 2, 2, BLOCK, t.shape[-1]).transpose(0, 2, 1, 3, 4)
    return t.reshape(N_HEADS, BLOCK, t.shape[-1])


def _halves(t):
    tf = t.astype(F32)
    low = lax.broadcasted_iota(jnp.int32, tf.shape, 1) < HEAD_DIM
    swapped = pltpu.roll(tf, HEAD_DIM, 1)
    zero = jnp.zeros_like(tf)
    return ((_bf(jnp.where(low, tf, zero)), _bf(jnp.where(low, zero, swapped))),
            (_bf(jnp.where(low, swapped, zero)), _bf(jnp.where(low, zero, tf))))


def _fold_halves(even, odd):
    low = lax.broadcasted_iota(jnp.int32, even.shape, 1) < HEAD_DIM
    comb = jnp.where(low, even, odd)
    return comb + pltpu.roll(comb, HEAD_DIM, 1)


def _tile_rows(ref, kh):
    return jnp.concatenate([ref[:, (2 * kh) * TILE:(2 * kh + 1) * TILE],
                            ref[:, (2 * kh + 1) * TILE:(2 * kh + 2) * TILE]], axis=0)


def _halves_t(t):
    tt = t.astype(F32).T
    top = lax.broadcasted_iota(jnp.int32, tt.shape, 0) < HEAD_DIM
    swapped = jnp.concatenate([tt[HEAD_DIM:], tt[:HEAD_DIM]], axis=0)
    zero = jnp.zeros_like(tt)
    return ((_bf(jnp.where(top, tt, zero)), _bf(jnp.where(top, zero, swapped))),
            (_bf(jnp.where(top, swapped, zero)), _bf(jnp.where(top, zero, tt))))


def _attn_probs(km, qk, bias, sink):
    lg = _mm_nt(km, qk) * (HEAD_DIM ** -0.5) + bias
    m = jnp.maximum(jnp.max(lg, axis=0, keepdims=True), sink)
    p = jnp.exp(lg - m)
    es = jnp.exp(sink - m)
    inv = 1.0 / (jnp.sum(p, axis=0, keepdims=True) + es)
    return p * inv, es * inv


def _attn_fwd(q, k, v, bias, sink_rows, exchange=None):
    seq = q.shape[0]
    nblk = seq // BLOCK

    def body(q_ref, kp_ref, kc_ref, vp_ref, vc_ref, b_ref, s_ref, o_ref):
        which = jnp.minimum(pl.program_id(0), 1)
        kms = _halves(jnp.concatenate([kp_ref[...], kc_ref[...]], axis=0))
        vts = _halves_t(jnp.concatenate([vp_ref[...], vc_ref[...]], axis=0))
        for kh in range(N_KV):
            qk = _tile_rows(q_ref, kh)
            acc = jnp.zeros((TILE, 2 * BLOCK), F32)
            for par in range(2):
                pr, _ = _attn_probs(kms[kh][par], qk, b_ref[which, kh, par], s_ref[kh, par])
                acc = acc + _mm(vts[kh][par], _bf(pr))
            acc = acc.T
            o_ref[:, (2 * kh) * TILE:(2 * kh + 1) * TILE] = _bf(acc[:BLOCK])
            o_ref[:, (2 * kh + 1) * TILE:(2 * kh + 2) * TILE] = _bf(acc[BLOCK:])

    cur = lambda n: (n, 0)
    prev = lambda n: (jnp.maximum(n - 1, 0), 0)
    return _fused_call(
        "attn_fwd", body, (nblk,),
        [pl.BlockSpec((BLOCK, ATTN_W), cur),
         pl.BlockSpec((BLOCK, KV_W), prev), pl.BlockSpec((BLOCK, KV_W), cur),
         pl.BlockSpec((BLOCK, KV_W), prev), pl.BlockSpec((BLOCK, KV_W), cur),
         pl.BlockSpec(bias.shape, functools.partial(_zero_map, bias.ndim)),
         pl.BlockSpec(sink_rows.shape, functools.partial(_zero_map, sink_rows.ndim))],
        [pl.BlockSpec((BLOCK, ATTN_W), cur)], [SDS((seq, ATTN_W), BF16)], [],
        [q, k, k, v, v, bias, sink_rows], exchange, _params(1))


def _attn_bwd(q, k, v, d_out, bias, sink_rows, exchange=None):
    seq = q.shape[0]
    nblk = seq // BLOCK

    def body(q_ref, kp_ref, kc_ref, vp_ref, vc_ref, do_ref, b_ref, s_ref,
             dq_ref, dk_ref, dv_ref, db_ref, ds_ref, ck_ref, cv_ref):
        n = pl.program_id(0)

        @pl.when(n == 0)
        def _():
            db_ref[...] = jnp.zeros_like(db_ref)
            ds_ref[...] = jnp.zeros_like(ds_ref)
            ck_ref[...] = jnp.zeros_like(ck_ref)
            cv_ref[...] = jnp.zeros_like(cv_ref)

        @pl.when(n < nblk)
        def _():
            which = jnp.minimum(n, 1)
            scale = HEAD_DIM ** -0.5
            kcat = jnp.concatenate([kp_ref[...], kc_ref[...]], axis=0)
            kms = _halves(kcat)
            kts = _halves_t(kcat)
            vms = _halves(jnp.concatenate([vp_ref[...], vc_ref[...]], axis=0))
            dks, dvs = [], []
            for kh in range(N_KV):
                qk = _tile_rows(q_ref, kh)
                dok = _tile_rows(do_ref, kh)
                dq = jnp.zeros((TILE, 2 * BLOCK), F32)
                dkp, dvp = [], []
                for par in range(2):
                    pr, ps = _attn_probs(kms[kh][par], qk, b_ref[which, kh, par], s_ref[kh, par])
                    dp = _mm_nt(vms[kh][par], dok)
                    rs = jnp.sum(pr * dp, axis=0, keepdims=True)
                    dlg = pr * (dp - rs)
                    ds_ref[kh, par] += -ps * rs
                    db_ref[kh, par] += dlg
                    dlb = _bf(dlg)
                    dq = dq + _mm(kts[kh][par], dlb)
                    dkp.append(_mm(dlb, qk))
                    dvp.append(_mm(_bf(pr), dok))
                dq = _bf((dq * scale).T)
                dq_ref[:, (2 * kh) * TILE:(2 * kh + 1) * TILE] = dq[:BLOCK]
                dq_ref[:, (2 * kh + 1) * TILE:(2 * kh + 2) * TILE] = dq[BLOCK:]
                dks.append(_fold_halves(*dkp))
                dvs.append(_fold_halves(*dvp))
            low = lax.broadcasted_iota(jnp.int32, (2 * BLOCK, TILE), 1) < HEAD_DIM
            dkk = jnp.where(low, dks[0], dks[1]) * scale
            dvv = jnp.where(low, dvs[0], dvs[1])
            dk_ref[...] = _bf(ck_ref[...] + dkk[:BLOCK])
            ck_ref[...] = dkk[BLOCK:]
            dv_ref[...] = _bf(cv_ref[...] + dvv[:BLOCK])
            cv_ref[...] = dvv[BLOCK:]

        @pl.when(n == nblk)
        def _():
            dk_ref[...] = _bf(ck_ref[...])
            dv_ref[...] = _bf(cv_ref[...])

    cur = lambda n: (jnp.minimum(n, nblk - 1), 0)
    prev = lambda n: (jnp.maximum(jnp.minimum(n, nblk - 1) - 1, 0), 0)
    late = lambda n: (jnp.maximum(n - 1, 0), 0)
    kv_spec = lambda m: pl.BlockSpec((BLOCK, KV_W), m)
    acc_b = pl.BlockSpec(bias.shape[1:], functools.partial(_zero_map, bias.ndim - 1))
    acc_s = pl.BlockSpec(sink_rows.shape, functools.partial(_zero_map, sink_rows.ndim))
    return _fused_call(
        "attn_bwd", body, (nblk + 1,),
        [pl.BlockSpec((BLOCK, ATTN_W), cur), kv_spec(prev), kv_spec(cur), kv_spec(prev), kv_spec(cur),
         pl.BlockSpec((BLOCK, ATTN_W), cur),
         pl.BlockSpec(bias.shape, functools.partial(_zero_map, bias.ndim)), acc_s],
        [pl.BlockSpec((BLOCK, ATTN_W), cur), kv_spec(late), kv_spec(late), acc_b, acc_s],
        [SDS((seq, ATTN_W), BF16), SDS((seq, KV_W), BF16), SDS((seq, KV_W), BF16),
         SDS(bias.shape[1:], F32), SDS(sink_rows.shape, F32)],
        [pltpu.VMEM((BLOCK, KV_W), F32), pltpu.VMEM((BLOCK, KV_W), F32)],
        [q, k, k, v, v, d_out, bias, sink_rows], exchange, _params(1))


def _ssm_discretize(lam_re, lam_im, log_dt, b_re, b_im):
    dt = jnp.exp(log_dt)[:, None]
    mag = jnp.exp(lam_re * dt)
    ab_re = mag * jnp.cos(lam_im * dt)
    ab_im = mag * jnp.sin(lam_im * dt)
    nr = ab_re - 1.0
    den = lam_re * lam_re + lam_im * lam_im
    f_re = (nr * lam_re + ab_im * lam_im) / den
    f_im = (ab_im * lam_re - nr * lam_im) / den
    bb_re = f_re[..., None] * b_re - f_im[..., None] * b_im
    bb_im = f_re[..., None] * b_im + f_im[..., None] * b_re
    return ab_re, ab_im, bb_re, bb_im


def _state_layout(re, im):
    z = jnp.stack([re, im]).reshape(2, N_SUPER, GROUPS_PER_SUPER, SSM_STATE)
    return z.transpose(1, 0, 2, 3).reshape(STATE_COLS)


def _state_unlayout(vec):
    z = vec.reshape(N_SUPER, 2, GROUPS_PER_SUPER, SSM_STATE).transpose(1, 0, 2, 3)
    z = z.reshape(2, SSM_GROUPS, SSM_STATE)
    return z[0], z[1]


def _scan_tables(ab_re, ab_im):
    pw = [None, (ab_re, ab_im)]
    for _ in range(2, SUBLANES + 1):
        pr, pi_ = pw[-1]
        pw.append((pr * ab_re - pi_ * ab_im, pr * ab_im + pi_ * ab_re))
    rows = np.arange(SUBLANES)[:, None]
    fwd, bwd = [], []
    for shift in (1, 2, 4):
        fwd.append(_state_layout(*pw[shift])[None, :] * (rows >= shift).astype(np.float32))
        bwd.append(_state_layout(pw[shift][0], -pw[shift][1])[None, :] * (rows < SUBLANES - shift).astype(np.float32))
    fwd.append(jnp.stack([_state_layout(*pw[r + 1]) for r in range(SUBLANES)]))
    bwd.append(jnp.stack([_state_layout(pw[SUBLANES - r][0], -pw[SUBLANES - r][1]) for r in range(SUBLANES)]))
    return jnp.stack(fwd), jnp.stack(bwd)


_EYE = np.eye(GROUPS_PER_SUPER, dtype=np.float32)


def _b_matrix(bb_re, bb_im):
    bb = jnp.stack([bb_re, bb_im]).reshape(2, N_SUPER, GROUPS_PER_SUPER, SSM_STATE, SSM_GROUP)
    m = jnp.einsum('rsgpc,gh->sgcrhp', bb, _EYE)
    return m.reshape(N_SUPER, SUPER_IN, SUPER_W)


def _b_matrix_grad(dm):
    d = dm.reshape(N_SUPER, GROUPS_PER_SUPER, SSM_GROUP, 2, GROUPS_PER_SUPER, SSM_STATE)
    d = jnp.sum(d * _EYE[None, :, None, None, :, None], axis=4)
    d = d.transpose(3, 0, 1, 4, 2).reshape(2, SSM_GROUPS, SSM_STATE, SSM_GROUP)
    return d[0], d[1]


def _c_matrix(c_re, c_im):
    cc = jnp.stack([c_re, -c_im]).reshape(2, N_SUPER, GROUPS_PER_SUPER, SSM_GROUP, SSM_STATE)
    m = jnp.einsum('rsgcp,gh->srgphc', cc, _EYE)
    return m.reshape(N_SUPER, SUPER_W, SUPER_IN)


def _c_matrix_grad(dm):
    d = dm.reshape(N_SUPER, 2, GROUPS_PER_SUPER, SSM_STATE, GROUPS_PER_SUPER, SSM_GROUP)
    d = jnp.sum(d * _EYE[None, None, :, None, :, None], axis=4)
    d = d.transpose(1, 0, 2, 4, 3).reshape(2, SSM_GROUPS, SSM_GROUP, SSM_STATE)
    return d[0], -d[1]


def _scan_rows(buf_ref, tab_ref, carry_ref, n_groups, reverse, h_ref=None, da_ref=None):
    edge = 0 if reverse else SUBLANES - 1
    for sb in range(N_SUPER):
        cr = pl.ds(sb * SUPER_W, SUPER_HALF)
        ci = pl.ds(sb * SUPER_W + SUPER_HALF, SUPER_HALF)

        def step(gi, carry, cr=cr, ci=ci):
            g = (n_groups - 1 - gi) if reverse else gi
            rows = pl.ds(pl.multiple_of(g * SUBLANES, SUBLANES), SUBLANES)
            c_re, c_im = carry[0], carry[1]
            xr = buf_ref[rows, cr]
            xi = buf_ref[rows, ci]
            for k, shift in enumerate((1, 2, 4)):
                s = (SUBLANES - shift) if reverse else shift
                sr = pltpu.roll(xr, s, 0)
                si = pltpu.roll(xi, s, 0)
                ar = tab_ref[k, :, cr]
                ai = tab_ref[k, :, ci]
                xr, xi = xr + ar * sr - ai * si, xi + ar * si + ai * sr
            pr = tab_ref[3, :, cr]
            pi_ = tab_ref[3, :, ci]
            xr, xi = xr + pr * c_re - pi_ * c_im, xi + pr * c_im + pi_ * c_re
            buf_ref[rows, cr] = xr
            buf_ref[rows, ci] = xi
            out = [jnp.broadcast_to(xr[edge:edge + 1], xr.shape), jnp.broadcast_to(xi[edge:edge + 1], xi.shape)]
            if h_ref is not None:
                last = lax.broadcasted_iota(jnp.int32, xr.shape, 0) == SUBLANES - 1
                gr = jnp.where(last, c_re, pltpu.roll(xr, SUBLANES - 1, 0))
                gim = jnp.where(last, c_im, pltpu.roll(xi, SUBLANES - 1, 0))
                hr = h_ref[rows, cr]
                hi = h_ref[rows, ci]
                out += [carry[2] + gr * hr + gim * hi, carry[3] + gim * hr - gr * hi]
            return tuple(out)

        init = [carry_ref[:, cr], carry_ref[:, ci]]
        if h_ref is not None:
            init += [da_ref[:, cr], da_ref[:, ci]]
        fin = lax.fori_loop(0, n_groups, step, tuple(init))
        carry_ref[:, cr] = fin[0]
        carry_ref[:, ci] = fin[1]
        if h_ref is not None:
            da_ref[:, cr] = fin[2]
            da_ref[:, ci] = fin[3]


def _ssm_fwd(u, bmat, cmat, tab, d_skip, tb, exchange=None):
    seq = u.shape[0]

    def body(u_ref, b_ref, c_ref, t_ref, d_ref, s_ref, h_ref, carry_ref):
        @pl.when(pl.program_id(0) == 0)
        def _():
            carry_ref[...] = jnp.zeros_like(carry_ref)

        u_blk = u_ref[...]
        ub = _bf(u_blk)
        for sb in range(N_SUPER):
            h_ref[:, sb * SUPER_W:(sb + 1) * SUPER_W] = _mm(ub[:, sb * SUPER_IN:(sb + 1) * SUPER_IN], b_ref[sb])
        _scan_rows(h_ref, t_ref, carry_ref, tb // SUBLANES, False)
        ys = [_mm(_bf(h_ref[:, sb * SUPER_W:(sb + 1) * SUPER_W]), c_ref[sb]) for sb in range(N_SUPER)]
        s_ref[...] = jnp.concatenate(ys, axis=1) + d_ref[...] * u_blk

    return _rowcall("ssm_fwd", body, seq, tb, [u], [bmat, cmat, tab, d_skip],
                    [(SSM_W, F32), (STATE_COLS, F32)], [],
                    scratch=[pltpu.VMEM((SUBLANES, STATE_COLS), F32)], vmem=VMEM_BIG, exchange=exchange)


def _ssm_bwd(ds, u, h, bmat_t, cmat_t, tab, d_skip, tb, exchange=None):
    seq = u.shape[0]

    def body(ds_ref, u_ref, h_ref, bt_ref, ct_ref, t_ref, d_ref,
             du_ref, db_ref, dc_ref, da_ref, dd_ref, g_ref, carry_ref):
        @pl.when(pl.program_id(0) == 0)
        def _():
            carry_ref[...] = jnp.zeros_like(carry_ref)
            db_ref[...] = jnp.zeros_like(db_ref)
            dc_ref[...] = jnp.zeros_like(dc_ref)
            da_ref[...] = jnp.zeros_like(da_ref)
            dd_ref[...] = jnp.zeros_like(dd_ref)

        ds_blk = ds_ref[...]
        dsb = _bf(ds_blk)
        u_blk = u_ref[...]
        ub = _bf(u_blk)
        for sb in range(N_SUPER):
            g_ref[:, sb * SUPER_W:(sb + 1) * SUPER_W] = _mm(dsb[:, sb * SUPER_IN:(sb + 1) * SUPER_IN], ct_ref[sb])
        _scan_rows(g_ref, t_ref, carry_ref, tb // SUBLANES, True, h_ref=h_ref, da_ref=da_ref)
        dus = []
        for sb in range(N_SUPER):
            gb = _bf(g_ref[:, sb * SUPER_W:(sb + 1) * SUPER_W])
            dus.append(_mm(gb, bt_ref[sb]))
            db_ref[sb] += _mm_tn(ub[:, sb * SUPER_IN:(sb + 1) * SUPER_IN], gb)
            dc_ref[sb] += _mm_tn(_bf(h_ref[:, sb * SUPER_W:(sb + 1) * SUPER_W]),
                                 dsb[:, sb * SUPER_IN:(sb + 1) * SUPER_IN])
        du_ref[...] = jnp.concatenate(dus, axis=1) + d_ref[...] * ds_blk
        dd_ref[...] += jnp.sum(ds_blk * u_blk, axis=0, keepdims=True)

    return _rowcall("ssm_bwd", body, seq, tb, [ds, u, h], [bmat_t, cmat_t, tab, d_skip],
                    [(SSM_W, F32)],
                    [((N_SUPER, SUPER_IN, SUPER_W), F32), ((N_SUPER, SUPER_W, SUPER_IN), F32),
                     ((SUBLANES, STATE_COLS), F32), ((1, SSM_W), F32)],
                    scratch=[pltpu.VMEM((tb, STATE_COLS), F32), pltpu.VMEM((SUBLANES, STATE_COLS), F32)],
                    reverse=True, vmem=VMEM_BIG, exchange=exchange)


def _merge_core(s, attb, ga, gs, wg_ref, wab_ref, wsb_ref, wout_ref):
    zg, dgelu = _gelu_and_grad(s)
    zgb = _bf(zg)
    sg = _sig(_mm(zgb, wg_ref[...]))
    z = zg * sg
    zb = _bf(z)
    ys = jnp.concatenate([_mm(zb, wsb_ref[j]) for j in range(N_CHIPS)], axis=1)
    ya = jnp.concatenate([_mm(attb, wab_ref[j]) for j in range(N_CHIPS)], axis=1)
    sa = _sig(ga)
    ss = _sig(gs)
    mgb = _bf(sa * ya + ss * ys)
    o = _mm(mgb, wout_ref[...])
    return dict(zg=zg, dgelu=dgelu, zgb=zgb, sg=sg, zb=zb, ys=ys, ya=ya, sa=sa, ss=ss, mgb=mgb, o=o)


def _merge_fwd(x, s, att, ga, gs, g2, w_glu, w_ab, w_sb, w_out, tb):
    seq = x.shape[0]

    def body(x_ref, s_ref, att_ref, ga_ref, gs_ref, g_ref, wg_ref, wab_ref, wsb_ref, wout_ref, x2_ref):
        f = _merge_core(s_ref[...], att_ref[...], ga_ref[...], gs_ref[...], wg_ref, wab_ref, wsb_ref, wout_ref)
        n, _, _ = _rms(f["o"], g_ref[...])
        x2_ref[...] = x_ref[...] + n

    return _rowcall("merge_fwd", body, seq, tb, [x, s, att, ga, gs], [g2, w_glu, w_ab, w_sb, w_out],
                    [(D_MODEL, F32)], [], vmem=VMEM_BIG)[0]


def _merge_bwd(dx2, s, att, ga, gs, g2, w_glu, w_ab, w_sb, w_out, tb, exchange=None):
    seq = s.shape[0]
    cw = D_MODEL // N_CHIPS
    last = seq // tb - 1

    def body(dx2_ref, s_ref, att_ref, ga_ref, gs_ref, g_ref, wg_ref, wab_ref, wsb_ref, wout_ref,
             ds_ref, datt_ref, dga_ref, dgs_ref, dg_ref, dwg_ref, dwab_ref, dwsb_ref, dwout_ref,
             bwg_ref, bwab_ref, bwsb_ref, bwout_ref):
        @pl.when(pl.program_id(0) == 0)
        def _():
            for r in (dg_ref, dwg_ref, dwab_ref, dwsb_ref, dwout_ref):
                r[...] = jnp.zeros_like(r)

        attb = att_ref[...]
        f = _merge_core(s_ref[...], attb, ga_ref[...], gs_ref[...], wg_ref, wab_ref, wsb_ref, wout_ref)
        g = g_ref[...]
        _, oh, r2 = _rms(f["o"], g)
        do, dg = _rms_bwd(dx2_ref[...], oh, r2, g)
        dg_ref[...] += dg
        dob = _bf(do)
        dwout_ref[...] += _mm_tn(f["mgb"], dob)
        dmg = _mm_nt(dob, wout_ref[...])
        sa, ss = f["sa"], f["ss"]
        dyab = _bf(dmg * sa)
        dysb = _bf(dmg * ss)
        dga_ref[...] = _bf(dmg * f["ya"] * sa * (1.0 - sa))
        dgs_ref[...] = _bf(dmg * f["ys"] * ss * (1.0 - ss))
        dwab = _mm_tn(attb, dyab)
        dwsb = _mm_tn(f["zb"], dysb)
        datt = jnp.zeros((tb, ATTN_W), F32)
        dz = jnp.zeros((tb, SSM_W), F32)
        for j in range(N_CHIPS):
            dwab_ref[j] += dwab[:, j * cw:(j + 1) * cw]
            dwsb_ref[j] += dwsb[:, j * cw:(j + 1) * cw]
            datt = datt + _mm_nt(dyab[:, j * cw:(j + 1) * cw], wab_ref[j])
            dz = dz + _mm_nt(dysb[:, j * cw:(j + 1) * cw], wsb_ref[j])
        datt_ref[...] = _bf(datt)
        sg, zg = f["sg"], f["zg"]
        dglb = _bf(dz * zg * sg * (1.0 - sg))
        dwg_ref[...] += _mm_tn(f["zgb"], dglb)
        dzg = dz * sg + _mm_nt(dglb, wg_ref[...])
        ds_ref[...] = dzg * f["dgelu"]

        @pl.when(pl.program_id(0) == last)
        def _():
            for dst, src in ((bwg_ref, dwg_ref), (bwab_ref, dwab_ref), (bwsb_ref, dwsb_ref), (bwout_ref, dwout_ref)):
                dst[...] = _bf(src[...])

    shapes = [w_glu.shape, w_ab.shape, w_sb.shape, w_out.shape]
    return _rowcall("merge_bwd", body, seq, tb, [dx2, s, att, ga, gs], [g2, w_glu, w_ab, w_sb, w_out],
                    [(SSM_W, F32), (ATTN_W, BF16), (D_MODEL, BF16), (D_MODEL, BF16)],
                    [((1, D_MODEL), F32)] + [(sh, F32) for sh in shapes] + [(sh, BF16) for sh in shapes],
                    vmem=VMEM_BIG, exchange=exchange)


def _mlp_fwd_loss(x2, target, g3, g4, w_ffi, w_ffo, tb):
    seq = x2.shape[0]

    def body(x2_ref, t_ref, g3_ref, g4_ref, wi_ref, wo_ref, dy_ref, df_ref, h_ref, loss_ref, dg_ref):
        @pl.when(pl.program_id(0) == 0)
        def _():
            loss_ref[...] = jnp.zeros_like(loss_ref)
            dg_ref[...] = jnp.zeros_like(dg_ref)

        x2_blk = x2_ref[...]
        h3, _, _ = _rms(x2_blk, g3_ref[...])
        hb = _bf(h3)
        h_ref[...] = hb
        f = jnp.zeros((tb, D_MODEL), F32)
        for j in range(FF_CHUNKS):
            a = _mm(hb, wi_ref[j])
            f = f + _mm(_bf(jnp.square(jnp.maximum(a, 0.0))), wo_ref[j])
        g4 = g4_ref[...]
        n4, fh, r4 = _rms(f, g4)
        e = (x2_blk + n4) - t_ref[...]
        loss_ref[...] += 0.5 * jnp.sum(jnp.mean(e * e, axis=-1, keepdims=True))
        dy = e * (1.0 / D_MODEL)
        dy_ref[...] = dy
        df, dg = _rms_bwd(dy, fh, r4, g4)
        df_ref[...] = _bf(df)
        dg_ref[...] += dg

    return _rowcall("mlp_fwd_loss", body, seq, tb, [x2, target], [g3, g4, w_ffi, w_ffo],
                    [(D_MODEL, F32), (D_MODEL, BF16), (D_MODEL, BF16)],
                    [((SUBLANES, 128), F32), ((1, D_MODEL), F32)], vmem=VMEM_BIG)


def _mlp_bwd(x2, dy, df, h3, g3, w_ffi, w_ffo, tb):
    seq = x2.shape[0]
    cw = D_FF // FF_CHUNKS

    def body(x2_ref, dy_ref, df_ref, h_ref, g3_ref, wi_ref, wo_ref, dx_ref, act_ref, da_ref, dg_ref):
        @pl.when(pl.program_id(0) == 0)
        def _():
            dg_ref[...] = jnp.zeros_like(dg_ref)

        hb = h_ref[...]
        dfb = df_ref[...]
        dh = jnp.zeros((tb, D_MODEL), F32)
        for j in range(FF_CHUNKS):
            ra = jnp.maximum(_mm(hb, wi_ref[j]), 0.0)
            act_ref[:, j * cw:(j + 1) * cw] = _bf(ra * ra)
            dab = _bf(_mm_nt(dfb, wo_ref[j]) * (2.0 * ra))
            da_ref[:, j * cw:(j + 1) * cw] = dab
            dh = dh + _mm_nt(dab, wi_ref[j])
        g3 = g3_ref[...]
        _, xh, r3 = _rms(x2_ref[...], g3)
        dxn, dg = _rms_bwd(dh, xh, r3, g3)
        dx_ref[...] = dy_ref[...] + dxn
        dg_ref[...] += dg

    return _rowcall("mlp_bwd", body, seq, tb, [x2, dy, df, h3], [g3, w_ffi, w_ffo],
                    [(D_MODEL, F32), (D_FF, BF16), (D_FF, BF16)], [((1, D_MODEL), F32)], vmem=VMEM_BIG)


def _matmul_tn(name, a, b, tk, tn, tl, chunk_major, exchange=None):
    seq, kdim = a.shape
    ndim = b.shape[1]
    last = seq // tl - 1

    def body(a_ref, b_ref, o_ref, ob_ref):
        @pl.when(pl.program_id(2) == 0)
        def _():
            o_ref[...] = jnp.zeros_like(o_ref)

        o_ref[...] += _mm_tn(a_ref[...], b_ref[...])

        @pl.when(pl.program_id(2) == last)
        def _():
            ob_ref[...] = _bf(o_ref[...])

    if chunk_major:
        shape = (ndim // tn, kdim, tn)
        out_spec = pl.BlockSpec((None, tk, tn), lambda k, n, l: (n, k, 0))
    else:
        shape = (kdim, ndim)
        out_spec = pl.BlockSpec((tk, tn), lambda k, n, l: (k, n))
    return _fused_call(
        name, body, (kdim // tk, ndim // tn, seq // tl),
        [pl.BlockSpec((tl, tk), lambda k, n, l: (l, k)), pl.BlockSpec((tl, tn), lambda k, n, l: (l, n))],
        [out_spec, out_spec], [SDS(shape, F32), SDS(shape, BF16)], [], [a, b], exchange, _params(3, VMEM_BIG))


def _ew_call(name, fn, ins, n_out):
    rows, cols = ins[0].shape
    tr = rows
    while tr * cols * 4 > (1 << 20) and tr % 16 == 0:
        tr //= 2
    spec = pl.BlockSpec((tr, cols), lambda i: (i, 0))

    def body(*refs):
        outs = fn(*[r[...] for r in refs[:len(ins)]])
        for r, o in zip(refs[len(ins):], outs):
            r[...] = o

    return pl.pallas_call(
        body, grid=(rows // tr,), in_specs=[spec] * len(ins), out_specs=[spec] * n_out,
        out_shape=[SDS((rows, cols), F32)] * n_out, name=name, compiler_params=_params(1),
    )(*ins)


def _adam_math(w, g, m, v):
    m2 = ADAM_B1 * m + (1.0 - ADAM_B1) * g
    v2 = ADAM_B2 * v + (1.0 - ADAM_B2) * (g * g)
    m_hat = m2 / (1.0 - ADAM_B1 ** ADAM_STEP)
    v_hat = v2 / (1.0 - ADAM_B2 ** ADAM_STEP)
    delta = -ADAM_LR * (m_hat / (jnp.sqrt(v_hat) + ADAM_EPS) + ADAM_WD * w)
    return delta, m2, v2


def _sum4(name, own, recv, idx):
    _, rows, cols = own.shape
    tr = rows
    while tr * cols * 4 > (1 << 20) and tr % 16 == 0:
        tr //= 2

    def body(idx_ref, o_ref, r0_ref, r1_ref, r2_ref, out_ref):
        out_ref[...] = ((o_ref[...] + r0_ref[...].astype(F32)) + r1_ref[...].astype(F32)) + r2_ref[...].astype(F32)

    blk = (None, tr, cols)
    grid_spec = pltpu.PrefetchScalarGridSpec(
        num_scalar_prefetch=1, grid=(rows // tr,),
        in_specs=[pl.BlockSpec(blk, lambda i, s: (s[0], i, 0)), pl.BlockSpec(blk, lambda i, s: (0, i, 0)),
                  pl.BlockSpec(blk, lambda i, s: (1, i, 0)), pl.BlockSpec(blk, lambda i, s: (2, i, 0))],
        out_specs=pl.BlockSpec((tr, cols), lambda i, s: (i, 0)))
    return pl.pallas_call(body, grid_spec=grid_spec, out_shape=SDS((rows, cols), F32), name=name,
                          compiler_params=_params(1))(jnp.reshape(idx, (1,)).astype(jnp.int32), own, recv, recv, recv)


def _adam_pair(name, w, p_own, p_sib, m, v):
    def fn(w_, a, b, m_, v_):
        g = a + b
        return (g,) + _adam_math(w_, g, m_, v_)

    return _ew_call(name, fn, [w, p_own, p_sib, m, v], 4)


def _place():
    return lax.axis_index("x"), lax.axis_index("y"), lax.axis_index("c")


def _other_chips(x, y):
    return [(1 - x, y), (x, 1 - y), (1 - x, 1 - y)]


def _gather_chips(shards):
    n = len(shards)

    def copies(ins, outs, sems):
        send, recv, fwd_send, fwd_recv, loc = sems
        x, y, c = _place()
        me = 2 * x + y
        peers = _other_chips(x, y)
        local = [pltpu.make_async_copy(ins[a], outs[a].at[me], loc.at[a]) for a in range(n)]
        sends, recvs, passes, passed = [], [], [], []
        for a in range(n):
            half = shards[a].shape[0] // 2
            mine = pl.ds(c * half, half)
            theirs = pl.ds((1 - c) * half, half)
            for j, (px, py) in enumerate(peers):
                far = 2 * px + py
                sends.append(pltpu.make_async_remote_copy(
                    src_ref=ins[a].at[mine], dst_ref=outs[a].at[me, mine], send_sem=send.at[a, j],
                    recv_sem=recv.at[a, j], device_id=(px, py, c), device_id_type=MESH_ID))
                recvs.append(pltpu.make_async_remote_copy(
                    src_ref=ins[a].at[mine], dst_ref=outs[a].at[far, mine], send_sem=send.at[a, j],
                    recv_sem=recv.at[a, j], device_id=(px, py, c), device_id_type=MESH_ID))
                passes.append(pltpu.make_async_remote_copy(
                    src_ref=outs[a].at[far, mine], dst_ref=outs[a].at[far, mine], send_sem=fwd_send.at[a, j],
                    recv_sem=fwd_recv.at[a, j], device_id=(x, y, 1 - c), device_id_type=MESH_ID))
                passed.append(pltpu.make_async_remote_copy(
                    src_ref=outs[a].at[far, theirs], dst_ref=outs[a].at[far, theirs], send_sem=fwd_send.at[a, j],
                    recv_sem=fwd_recv.at[a, j], device_id=(x, y, 1 - c), device_id_type=MESH_ID))
        return local, sends, recvs, passes, passed

    def start(ins, outs, sems):
        local, sends, _, _, _ = copies(ins, outs, sems)
        for cp in local + sends:
            cp.start()

    def wait(ins, outs, sems):
        local, sends, recvs, passes, passed = copies(ins, outs, sems)
        for got, on in zip(recvs, passes):
            got.wait_recv()
            on.start()
        for cp in passed:
            cp.wait_recv()
        for cp in passes + sends:
            cp.wait_send()
        for cp in local:
            cp.wait()

    assert all(s.shape[0] % 32 == 0 for s in shards)
    pair = pltpu.SemaphoreType.DMA((n, 3))
    return _Exchange(shards, [SDS((N_CHIPS,) + s.shape, s.dtype) for s in shards],
                     [pair, pair, pair, pair, pltpu.SemaphoreType.DMA((n,))], start, wait)


def _scatter_chips(chunks):
    n = len(chunks)

    def copies(ins, outs, sems):
        send, recv = sems
        x, y, c = _place()
        return [pltpu.make_async_remote_copy(
            src_ref=ins[a].at[2 * px + py], dst_ref=outs[a].at[j], send_sem=send.at[a, j],
            recv_sem=recv.at[a, j], device_id=(px, py, c), device_id_type=MESH_ID)
            for a in range(n) for j, (px, py) in enumerate(_other_chips(x, y))]

    def start(ins, outs, sems):
        for cp in copies(ins, outs, sems):
            cp.start()

    def wait(ins, outs, sems):
        cps = copies(ins, outs, sems)
        for cp in cps:
            cp.wait_recv()
        for cp in cps:
            cp.wait_send()

    return _Exchange(chunks, [SDS((3,) + s.shape[1:], s.dtype) for s in chunks],
                     [pltpu.SemaphoreType.DMA((n, 3)), pltpu.SemaphoreType.DMA((n, 3))], start, wait)


def _swap_sibling(arrs):
    n = len(arrs)

    def copies(ins, outs, sems):
        send, recv = sems
        x, y, c = _place()
        return [pltpu.make_async_remote_copy(
            src_ref=ins[a], dst_ref=outs[a], send_sem=send.at[a], recv_sem=recv.at[a],
            device_id=(x, y, 1 - c), device_id_type=MESH_ID) for a in range(n)]

    def start(ins, outs, sems):
        for cp in copies(ins, outs, sems):
            cp.start()

    def wait(ins, outs, sems):
        cps = copies(ins, outs, sems)
        for cp in cps:
            cp.wait_recv()
        for cp in cps:
            cp.wait_send()

    return _Exchange(arrs, [SDS(s.shape, s.dtype) for s in arrs],
                     [pltpu.SemaphoreType.DMA((n,)), pltpu.SemaphoreType.DMA((n,))], start, wait)


N_DEV = 8


def _gather_devices(block):
    def copies(ins, outs, sems):
        send, recv, loc = sems
        x, y, c = _place()
        me = 4 * x + 2 * y + c
        local = pltpu.make_async_copy(ins[0], outs[0].at[me], loc.at[0])
        sends, recvs = [], []
        for k in range(1, N_DEV):
            peer = (x ^ (k >> 2), y ^ ((k >> 1) & 1), c ^ (k & 1))
            for group, slot in ((sends, me), (recvs, me ^ k)):
                group.append(pltpu.make_async_remote_copy(
                    src_ref=ins[0], dst_ref=outs[0].at[slot], send_sem=send.at[k - 1], recv_sem=recv.at[k - 1],
                    device_id=peer, device_id_type=MESH_ID))
        return local, sends, recvs

    def start(ins, outs, sems):
        local, sends, _ = copies(ins, outs, sems)
        for cp in [local] + sends:
            cp.start()

    def wait(ins, outs, sems):
        local, sends, recvs = copies(ins, outs, sems)
        for cp in recvs:
            cp.wait_recv()
        for cp in sends:
            cp.wait_send()
        local.wait()

    return _Exchange([block], [SDS((N_DEV,) + block.shape, block.dtype)],
                     [pltpu.SemaphoreType.DMA((N_DEV - 1,)), pltpu.SemaphoreType.DMA((N_DEV - 1,)),
                      pltpu.SemaphoreType.DMA((1,))], start, wait)


def _both(ex_a, ex_b):
    na_i, na_o, na_s = len(ex_a.ins), len(ex_a.outs), len(ex_a.sems)

    def start(ins, outs, sems):
        ex_a.start(ins[:na_i], outs[:na_o], sems[:na_s])
        ex_b.start(ins[na_i:], outs[na_o:], sems[na_s:])

    def wait(ins, outs, sems):
        ex_a.wait(ins[:na_i], outs[:na_o], sems[:na_s])
        ex_b.wait(ins[na_i:], outs[na_o:], sems[na_s:])

    return _Exchange(ex_a.ins + ex_b.ins, ex_a.outs + ex_b.outs, ex_a.sems + ex_b.sems, start, wait)


def _sum_devices(slots):
    def body(s_ref, o_ref):
        acc = s_ref[0]
        for d in range(1, N_DEV):
            acc = acc + s_ref[d]
        o_ref[...] = acc

    return pl.pallas_call(
        body, in_specs=[pl.BlockSpec(memory_space=pltpu.VMEM)], out_specs=pl.BlockSpec(memory_space=pltpu.VMEM),
        out_shape=SDS(slots.shape[1:], F32), name="sum_small",
        compiler_params=pltpu.CompilerParams(vmem_limit_bytes=32 * 1024 * 1024))(slots)


def _adam_small(ws, gs, ms, vs):
    n = len(ws)

    def body(*refs):
        for i in range(n):
            w_ref, g_ref, m_ref, v_ref = (refs[k * n + i] for k in range(4))
            outs = _adam_math(w_ref[...], g_ref[...], m_ref[...], v_ref[...])
            for k in range(3):
                refs[(4 + k) * n + i][...] = outs[k]

    vmem = pl.BlockSpec(memory_space=pltpu.VMEM)
    return pl.pallas_call(
        body, in_specs=[vmem] * (4 * n), out_specs=[vmem] * (3 * n),
        out_shape=[SDS(w.shape, F32) for w in ws] * 3, name="adam_small",
        compiler_params=pltpu.CompilerParams(vmem_limit_bytes=32 * 1024 * 1024))(*ws, *gs, *ms, *vs)


def _local_step(x, target, small, big, tb, distributed):
    g1, g2, g3, g4 = small["norm_mix_pre"], small["norm_mix_post"], small["norm_mlp_pre"], small["norm_mlp_post"]
    dist = distributed
    me = (2 * lax.axis_index("x") + lax.axis_index("y")) if dist else 0
    tb_ssm = min(tb, 256)
    bucket = jnp.asarray(_bucket_table())

    keys_first = lambda t: jnp.swapaxes(t, -1, -2)
    bias = keys_first(_pair_layout(_bias_table(small["rel_bias"], bucket)))
    sink_rows = keys_first(_pair_layout(jnp.broadcast_to(small["sinks"].reshape(N_HEADS, 1, 1), (N_HEADS, BLOCK, 1))))
    disc_args = (small["lam_re"], small["lam_im"], small["log_dt"], small["b_re"], small["b_im"])
    (ab_re, ab_im, bb_re, bb_im), disc_vjp = jax.vjp(_ssm_discretize, *disc_args)
    tab_f, tab_b = _scan_tables(ab_re, ab_im)
    bmat = _bf(_b_matrix(bb_re, bb_im))
    cmat = _bf(_c_matrix(small["c_re"], small["c_im"]))
    d_skip = small["d_skip"]

    if dist:
        (g_in,) = _exchange_alone("gather_w_in", _gather_chips([big["w_in"]]))
        w_in = g_in.reshape(IN_W, D_MODEL)
    else:
        w_in = big["w_in"]
    mix = ("w_glu", "w_attn_branch", "w_ssm_branch", "w_out")
    outs = _inproj_fwd(x, g1, w_in, tb, _gather_chips([big[n] for n in mix]) if dist else None)
    h1, q, k, v, u, ga, gs = outs[:7]
    w_glu, w_ab, w_sb, w_out = outs[7:] if dist else [big[n] for n in mix]
    w_glu = w_glu.reshape(SSM_W, SSM_W)
    w_out = w_out.reshape(D_MODEL, D_MODEL)
    outs = _attn_fwd(q, k, v, bias, sink_rows, _gather_chips([big["w_ff_in"]]) if dist else None)
    att = outs[0]
    w_ffi = outs[1] if dist else big["w_ff_in"]
    outs = _ssm_fwd(u, bmat, cmat, tab_f, d_skip, tb_ssm, _gather_chips([big["w_ff_out"]]) if dist else None)
    s, h = outs[:2]
    w_ffo = outs[2] if dist else big["w_ff_out"]
    x2 = _merge_fwd(x, s, att, ga, gs, g2, w_glu, w_ab, w_sb, w_out, tb)
    dy, df, h3, loss_acc, dg4 = _mlp_fwd_loss(x2, target, g3, g4, w_ffi, w_ffo, tb)

    dx2, act, da, dg3 = _mlp_bwd(x2, dy, df, h3, g3, w_ffi, w_ffo, tb)
    tl = min(512, x.shape[0])
    chunked = (N_CHIPS, D_FF // N_CHIPS, D_MODEL)
    d_ffi, b_ffi = _matmul_tn("grad_w_ff_in", h3, da, D_MODEL, D_FF // FF_CHUNKS, tl, True)
    d_ffo, b_ffo = _matmul_tn("grad_w_ff_out", act, df, D_FF // FF_CHUNKS, D_MODEL, tl, False)
    d_ffo, b_ffo = d_ffo.reshape(chunked), b_ffo.reshape(chunked)
    outs = _merge_bwd(dx2, s, att, ga, gs, g2, w_glu, w_ab, w_sb, w_out, tb_ssm,
                      _scatter_chips([b_ffi]) if dist else None)
    ds, datt, dga, dgs, dg2, d_glu, d_ab, d_sb, d_out, b_glu, b_ab, b_sb, b_out = outs[:13]
    r_ffi = outs[13:]
    glu4, out4 = (N_CHIPS, SSM_W // N_CHIPS, SSM_W), (N_CHIPS, D_MODEL // N_CHIPS, D_MODEL)
    d_mix = [d_glu.reshape(glu4), d_ab, d_sb, d_out.reshape(out4)]
    b_mix = [b_glu.reshape(glu4), b_ab, b_sb, b_out.reshape(out4)]
    outs = _ssm_bwd(ds, u, h, bmat.transpose(0, 2, 1), cmat.transpose(0, 2, 1), tab_b, d_skip, tb_ssm,
                    _scatter_chips([b_ffo]) if dist else None)
    du, d_bmat, d_cmat, da_acc, dd_skip = outs[:5]
    r_ffo = outs[5:]
    outs = _attn_bwd(q, k, v, datt, bias, sink_rows, _scatter_chips(b_mix) if dist else None)
    dq, dk, dv, dbias, dsink_rows = outs[:5]
    r_mix = outs[5:]
    if dist:
        p_ffi = _sum4("sum_w_ff_in", d_ffi, r_ffi[0], me)
        p_ffo = _sum4("sum_w_ff_out", d_ffo, r_ffo[0], me)
    dx, dpj, dg1 = _inproj_bwd(x, dx2, dq, dk, dv, du, dga, dgs, g1, w_in, tb)

    dab_re, dab_im = _state_unlayout(jnp.sum(da_acc, axis=0))
    dbb_re, dbb_im = _b_matrix_grad(d_bmat)
    d_lam_re, d_lam_im, d_log_dt, d_b_re, d_b_im = disc_vjp((dab_re, dab_im, dbb_re, dbb_im))
    d_c_re, d_c_im = _c_matrix_grad(d_cmat)
    d_rel = _bias_grad(_pair_unlayout(keys_first(dbias)), bucket)
    d_sinks = jnp.sum(_pair_unlayout(keys_first(dsink_rows)), axis=(1, 2))
    small_grads = dict(
        norm_mix_pre=dg1, norm_mix_post=dg2, norm_mlp_pre=dg3, norm_mlp_post=dg4, rel_bias=d_rel, sinks=d_sinks,
        lam_re=d_lam_re, lam_im=d_lam_im, log_dt=d_log_dt, b_re=d_b_re, b_im=d_b_im, c_re=d_c_re, c_im=d_c_im,
        d_skip=dd_skip)
    ride = _both(_swap_sibling([p_ffi, p_ffo]), _gather_devices(_pack(small_grads, loss_acc))) if dist else None
    outs = _matmul_tn("grad_w_in", dpj, h1, IN_W // 2, D_MODEL, tl, False, ride)
    in4 = (N_CHIPS, IN_W // N_CHIPS, D_MODEL)
    d_in, b_in = outs[0].reshape(in4), outs[1].reshape(in4)
    if not dist:
        return loss_acc, dx, small_grads, dict(zip(BIG, [d_in] + d_mix + [d_ffi, d_ffo]))
    s_ffi, s_ffo, slots = outs[2:]
    (r_in,) = _exchange_alone("scatter_w_in", _scatter_chips([b_in]))
    parts = [_sum4("sum_" + n, d, r, me) for n, d, r in zip(("w_in",) + mix, [d_in] + d_mix, [r_in] + list(r_mix))]
    sibs = _exchange_alone("swap_rest", _swap_sibling(parts))
    parts += [p_ffi, p_ffo]
    sibs = list(sibs) + [s_ffi, s_ffo]
    return loss_acc, dx, _sum_devices(slots), dict(zip(BIG, zip(parts, sibs)))


SMALL = ['norm_mix_pre', 'norm_mix_post', 'norm_mlp_pre', 'norm_mlp_post', 'rel_bias', 'sinks', 'lam_re', 'lam_im',
         'log_dt', 'b_re', 'b_im', 'c_re', 'c_im', 'd_skip']
BIG = ['w_in', 'w_glu', 'w_attn_branch', 'w_ssm_branch', 'w_out', 'w_ff_in', 'w_ff_out']
WEIGHTS = ['norm_mix_pre', 'norm_mix_post', 'norm_mlp_pre', 'norm_mlp_post', 'w_in', 'rel_bias', 'sinks', 'lam_re',
           'lam_im', 'log_dt', 'b_re', 'b_im', 'c_re', 'c_im', 'd_skip', 'w_glu', 'w_attn_branch', 'w_ssm_branch',
           'w_out', 'w_ff_in', 'w_ff_out']
PACK_COLS = 1024
PACK_ORDER = ['b_re', 'b_im', 'c_re', 'c_im', 'lam_re', 'lam_im', 'norm_mix_pre', 'norm_mix_post', 'norm_mlp_pre',
              'norm_mlp_post', 'rel_bias', 'sinks', 'log_dt', 'd_skip']


STATE_MINOR = ('b_re', 'b_im')
PACK_ROWS = 144
LOSS_ROW = 140


def _pack(named, loss_acc):
    parts = []
    for n in PACK_ORDER:
        a = jnp.swapaxes(named[n], -1, -2) if n in STATE_MINOR else named[n]
        flat = a.reshape(-1)
        rows = -(-flat.shape[0] // PACK_COLS)
        parts.append(jnp.pad(flat, (0, rows * PACK_COLS - flat.shape[0])).reshape(rows, PACK_COLS))
    assert sum(p.shape[0] for p in parts) == LOSS_ROW
    parts.append(jnp.pad(loss_acc[0:1], ((0, PACK_ROWS - LOSS_ROW - 1), (0, PACK_COLS - loss_acc.shape[1]))))
    return jnp.concatenate(parts, axis=0)


def _unpack(packed, shapes):
    out, at = {}, 0
    for n in PACK_ORDER:
        shape = shapes[n][:-2] + (shapes[n][-1], shapes[n][-2]) if n in STATE_MINOR else shapes[n]
        size = int(np.prod(shape))
        rows = -(-size // PACK_COLS)
        blk = packed[at:at + rows]
        out[n] = (blk.reshape(-1)[:size] if size % PACK_COLS else blk).reshape(shape)
        at += rows
    return out


def kernel(x, norm_mix_pre, norm_mix_post, norm_mlp_pre, norm_mlp_post, w_in, rel_bias, sinks, lam_re, lam_im, log_dt, b_re, b_im, c_re, c_im, d_skip, w_glu, w_attn_branch, w_ssm_branch, w_out, w_ff_in, w_ff_out, loss_target, m_norm_mix_pre, m_norm_mix_post, m_norm_mlp_pre, m_norm_mlp_post, m_w_in, m_rel_bias, m_sinks, m_lam_re, m_lam_im, m_log_dt, m_b_re, m_b_im, m_c_re, m_c_im, m_d_skip, m_w_glu, m_w_attn_branch, m_w_ssm_branch, m_w_out, m_w_ff_in, m_w_ff_out, v_norm_mix_pre, v_norm_mix_post, v_norm_mlp_pre, v_norm_mlp_post, v_w_in, v_rel_bias, v_sinks, v_lam_re, v_lam_im, v_log_dt, v_b_re, v_b_im, v_c_re, v_c_im, v_d_skip, v_w_glu, v_w_attn_branch, v_w_ssm_branch, v_w_out, v_w_ff_in, v_w_ff_out):
    env = dict(locals())
    w = {n: env[n] for n in WEIGHTS}
    m = {n: env["m_" + n] for n in WEIGHTS}
    v = {n: env["v_" + n] for n in WEIGHTS}
    seq = x.shape[1]
    tb = min(512, seq)

    small = {n: w[n] for n in ('norm_mix_pre', 'norm_mix_post', 'norm_mlp_pre', 'norm_mlp_post', 'rel_bias')}
    small.update({n: w[n][0] for n in ('sinks', 'lam_re', 'lam_im', 'log_dt', 'b_re', 'b_im', 'c_re', 'c_im')})
    small['d_skip'] = w['d_skip']
    shard = lambda t, n: t[n][0].T if n == 'w_in' else t[n][0]
    unshard = lambda a, n: (a.T if n == 'w_in' else a)[None]
    _, dx, small_g, big_g = _local_step(
        x[0], loss_target[0], small, {n: _bf(shard(w, n)) for n in BIG}, tb, True)

    loss = small_g[LOSS_ROW, 0]

    grads, deltas, new_m, new_v = {}, {}, {}, {}
    for n in BIG:
        p_own, p_sib = big_g[n]
        outs = _adam_pair("adam_" + n, shard(w, n), p_own, p_sib, shard(m, n), shard(v, n))
        grads[n], deltas[n], new_m[n], new_v[n] = [unshard(a, n) for a in outs]

    minor = lambda t, n: jnp.swapaxes(t, -1, -2) if n in STATE_MINOR else t
    g_small = _unpack(small_g, {n: w[n].shape for n in SMALL})
    outs = _adam_small([minor(w[n], n) for n in SMALL], [g_small[n] for n in SMALL],
                       [minor(m[n], n) for n in SMALL], [minor(v[n], n) for n in SMALL])
    grads.update({n: minor(g_small[n], n) for n in SMALL})
    for k, dst in enumerate((deltas, new_m, new_v)):
        dst.update({n: minor(a, n) for n, a in zip(SMALL, outs[k * len(SMALL):(k + 1) * len(SMALL)])})

    return (loss, dx[None], *[grads[n] for n in WEIGHTS], *[deltas[n] for n in WEIGHTS],
            *[new_m[n] for n in WEIGHTS], *[new_v[n] for n in WEIGHTS])
```

```python
import functools
import math

import numpy as np
import jax
import jax.numpy as jnp
from jax import lax
from jax.experimental import pallas as pl
from jax.experimental.pallas import tpu as pltpu

F32 = jnp.float32
BF16 = jnp.bfloat16

D_MODEL = 1024
N_HEADS = 8
N_KV = 2
Q_GROUP = 4
HEAD_DIM = 64
ATTN_W = 512
KV_W = 128
BLOCK = 128
N_BUCKETS = 32
MAX_DISTANCE = 128
NEG_INF = -1e30
SSM_W = 512
SSM_GROUP = 16
SSM_GROUPS = 32
SSM_STATE = 64
N_SUPER = 4
GROUPS_PER_SUPER = SSM_GROUPS // N_SUPER
SUPER_IN = GROUPS_PER_SUPER * SSM_GROUP
SUPER_HALF = GROUPS_PER_SUPER * SSM_STATE
SUPER_W = 2 * SUPER_HALF
STATE_COLS = N_SUPER * SUPER_W
D_FF = 4096
FF_CHUNKS = 4
IN_W = 3328
SPLITS = (0, 512, 640, 768, 1280, 2304, 3328)
RMS_EPS = 1e-6
N_CHIPS = 4
SUBLANES = 8

ADAM_LR = 0.001
ADAM_B1 = 0.9
ADAM_B2 = 0.999
ADAM_EPS = 1e-08
ADAM_WD = 0.01
ADAM_STEP = 10

VMEM_BIG = 56 * 1024 * 1024
SDS = jax.ShapeDtypeStruct
MESH_ID = pl.DeviceIdType.MESH
ANY = pl.BlockSpec(memory_space=pl.ANY)


def _bf(x):
    return x.astype(BF16)


def _mm(a, b):
    return jnp.dot(a, b, preferred_element_type=F32)


def _mm_nt(a, b):
    return lax.dot_general(a, b, (((1,), (1,)), ((), ())), preferred_element_type=F32)


def _mm_tn(a, b):
    return lax.dot_general(a, b, (((0,), (0,)), ((), ())), preferred_element_type=F32)


def _sig(x):
    return 1.0 / (1.0 + jnp.exp(-x))


def _rms(x, g):
    r = lax.rsqrt(jnp.mean(x * x, axis=-1, keepdims=True) + RMS_EPS)
    xh = x * r
    return xh * g, xh, r


def _rms_bwd(dout, xh, r, g):
    dg = jnp.sum(dout * xh, axis=0, keepdims=True)
    dxh = dout * g
    dx = r * (dxh - xh * jnp.mean(dxh * xh, axis=-1, keepdims=True))
    return dx, dg


_GELU_C = math.sqrt(2.0 / math.pi)


def _gelu_and_grad(x):
    x2 = x * x
    inner = _GELU_C * (x + 0.044715 * (x2 * x))
    t = jnp.tanh(inner)
    y = 0.5 * x * (1.0 + t)
    dy = 0.5 * (1.0 + t) + 0.5 * x * (1.0 - t * t) * (_GELU_C * (1.0 + 3.0 * 0.044715 * x2))
    return y, dy


def _zero_map(nd, *_):
    return (0,) * nd


def _params(n_axes, vmem=None):
    return pltpu.CompilerParams(dimension_semantics=("arbitrary",) * n_axes, vmem_limit_bytes=vmem)


class _Exchange:
    def __init__(self, ins, outs, sems, start, wait):
        self.ins, self.outs, self.sems, self.start, self.wait = list(ins), list(outs), list(sems), start, wait


def _fused_call(name, body, grid, in_specs, out_specs, out_shape, scratch, args, exchange, params):
    n_in, n_out, n_scr = len(in_specs), len(out_specs), len(scratch)
    if exchange is None:
        fn = body
    else:
        ex = exchange
        n_xi, n_xo = len(ex.ins), len(ex.outs)

        def fn(*refs):
            at = 0
            parts = []
            for n in (n_in, n_xi, n_out, n_xo, n_scr, len(ex.sems)):
                parts.append(refs[at:at + n])
                at += n
            ins, x_in, outs, x_out, scr, x_sem = parts
            ids = [pl.program_id(a) for a in range(len(grid))]
            first = functools.reduce(jnp.logical_and, [i == 0 for i in ids])
            last = functools.reduce(jnp.logical_and, [i == g - 1 for i, g in zip(ids, grid)])

            @pl.when(first)
            def _():
                ex.start(x_in, x_out, x_sem)

            body(*ins, *outs, *scr)

            @pl.when(last)
            def _():
                ex.wait(x_in, x_out, x_sem)

        in_specs = list(in_specs) + [ANY] * n_xi
        out_specs = list(out_specs) + [ANY] * n_xo
        out_shape = list(out_shape) + ex.outs
        scratch = list(scratch) + ex.sems
        args = list(args) + ex.ins
    return pl.pallas_call(fn, grid=grid, in_specs=in_specs, out_specs=out_specs, out_shape=out_shape,
                          scratch_shapes=list(scratch), name=name, compiler_params=params)(*args)


def _exchange_alone(name, ex):
    def body(*refs):
        n_xi, n_xo = len(ex.ins), len(ex.outs)
        x_in, x_out, x_sem = refs[:n_xi], refs[n_xi:n_xi + n_xo], refs[n_xi + n_xo:]
        ex.start(x_in, x_out, x_sem)
        ex.wait(x_in, x_out, x_sem)

    return pl.pallas_call(body, in_specs=[ANY] * len(ex.ins), out_specs=[ANY] * len(ex.outs), out_shape=ex.outs,
                          scratch_shapes=ex.sems, name=name)(*ex.ins)


def _rowcall(name, body, seq, tb, rows, consts, row_outs, acc_outs, scratch=(), reverse=False, vmem=None,
             exchange=None):
    nb = seq // tb
    rmap = (lambda i: (nb - 1 - i, 0)) if reverse else (lambda i: (i, 0))
    in_specs = [pl.BlockSpec((tb, a.shape[1]), rmap) for a in rows]
    in_specs += [pl.BlockSpec(a.shape, functools.partial(_zero_map, a.ndim), pipeline_mode=pl.Buffered(1))
                 for a in consts]
    out_specs = [pl.BlockSpec((tb, c), rmap) for c, _ in row_outs] + [ANY] * len(acc_outs)
    out_shape = [SDS((seq, c), dt) for c, dt in row_outs] + [SDS(s, dt) for s, dt in acc_outs]
    n_main = len(rows) + len(consts) + len(row_outs)
    n_acc = len(acc_outs)

    def fn(*refs):
        main, acc_hbm, rest = refs[:n_main], refs[n_main:n_main + n_acc], refs[n_main + n_acc:]
        acc_vmem, own = rest[:n_acc], rest[n_acc:]
        body(*main, *acc_vmem, *own)

        @pl.when(pl.program_id(0) == nb - 1)
        def _():
            for src, dst in zip(acc_vmem, acc_hbm):
                pltpu.sync_copy(src, dst)

    buffers = [pltpu.VMEM(s, dt) for s, dt in acc_outs] + list(scratch)
    return _fused_call(name, fn if acc_outs else body, (nb,), in_specs, out_specs, out_shape, buffers,
                       [*rows, *consts], exchange, _params(1, vmem))


def _inproj_fwd(x, g1, w_in, tb, exchange=None):
    seq = x.shape[0]

    def body(x_ref, g_ref, w_ref, h_ref, q_ref, k_ref, v_ref, u_ref, ga_ref, gs_ref):
        h, _, _ = _rms(x_ref[...], g_ref[...])
        hb = _bf(h)
        h_ref[...] = hb
        pj = _mm_nt(hb, w_ref[...])
        q_ref[...] = _bf(pj[:, SPLITS[0]:SPLITS[1]])
        k_ref[...] = _bf(pj[:, SPLITS[1]:SPLITS[2]])
        v_ref[...] = _bf(pj[:, SPLITS[2]:SPLITS[3]])
        u_ref[...] = pj[:, SPLITS[3]:SPLITS[4]]
        ga_ref[...] = pj[:, SPLITS[4]:SPLITS[5]]
        gs_ref[...] = pj[:, SPLITS[5]:SPLITS[6]]

    return _rowcall("inproj_fwd", body, seq, tb, [x], [g1, w_in],
                    [(D_MODEL, BF16), (ATTN_W, BF16), (KV_W, BF16), (KV_W, BF16), (SSM_W, F32),
                     (D_MODEL, F32), (D_MODEL, F32)], [], vmem=VMEM_BIG, exchange=exchange)


def _inproj_bwd(x, dx2, dq, dk, dv, du, dga, dgs, g1, w_in, tb, exchange=None):
    seq = x.shape[0]

    def body(x_ref, dx2_ref, dq_ref, dk_ref, dv_ref, du_ref, dga_ref, dgs_ref, g_ref, w_ref,
             dx_ref, dpj_ref, dg_ref):
        @pl.when(pl.program_id(0) == 0)
        def _():
            dg_ref[...] = jnp.zeros_like(dg_ref)

        dpj = jnp.concatenate([dq_ref[...], dk_ref[...], dv_ref[...], _bf(du_ref[...]),
                               dga_ref[...], dgs_ref[...]], axis=1)
        dpj_ref[...] = dpj
        dh = _mm(dpj, w_ref[...])
        g = g_ref[...]
        _, xh, r = _rms(x_ref[...], g)
        dxn, dg = _rms_bwd(dh, xh, r, g)
        dx_ref[...] = dx2_ref[...] + dxn
        dg_ref[...] += dg

    return _rowcall("inproj_bwd", body, seq, tb, [x, dx2, dq, dk, dv, du, dga, dgs], [g1, w_in],
                    [(D_MODEL, F32), (IN_W, BF16)], [((1, D_MODEL), F32)], vmem=VMEM_BIG, exchange=exchange)


def _bucket_table():
    qi = np.arange(BLOCK)[:, None]
    kj = np.arange(2 * BLOCK)[None, :]
    dist = qi + BLOCK - kj
    max_exact = N_BUCKETS // 2
    d = np.maximum(dist, 0)
    df = np.maximum(d, 1).astype(np.float32)
    large = max_exact + (np.log(df / np.float32(max_exact)) / np.float32(math.log(MAX_DISTANCE / max_exact))
                         * np.float32(N_BUCKETS - max_exact)).astype(np.int32)
    large = np.minimum(large, N_BUCKETS - 1)
    bucket = np.where(d < max_exact, d, large)
    valid = (dist >= 0) & (dist < BLOCK)
    return np.where(valid, bucket, -1).astype(np.int32)


def _bias_table(rel_bias, bucket):
    def body(rb_ref, bk_ref, o_ref):
        bk = bk_ref[...]
        has_prev = lax.broadcasted_iota(jnp.int32, bk.shape, 1) >= BLOCK
        for h in range(N_HEADS):
            acc = jnp.full((BLOCK, 2 * BLOCK), NEG_INF, F32)
            for b in range(N_BUCKETS):
                acc = jnp.where(bk == b, rb_ref[b, h], acc)
            o_ref[0, h] = jnp.where(has_prev, acc, NEG_INF)
            o_ref[1, h] = acc

    return pl.pallas_call(
        body, out_shape=SDS((2, N_HEADS, BLOCK, 2 * BLOCK), F32),
        in_specs=[pl.BlockSpec(memory_space=pltpu.SMEM), pl.BlockSpec(memory_space=pltpu.VMEM)],
        out_specs=pl.BlockSpec(memory_space=pltpu.VMEM), name="bias_table",
    )(rel_bias, bucket)


def _bias_grad(dbias, bucket):
    def body(db_ref, bk_ref, o_ref):
        bk = bk_ref[...]
        for h in range(N_HEADS):
            db = db_ref[h]
            for b in range(N_BUCKETS):
                o_ref[b, h] = jnp.sum(jnp.where(bk == b, db, 0.0))

    return pl.pallas_call(
        body, out_shape=SDS((N_BUCKETS, N_HEADS), F32),
        in_specs=[pl.BlockSpec(memory_space=pltpu.VMEM), pl.BlockSpec(memory_space=pltpu.VMEM)],
        out_specs=pl.BlockSpec(memory_space=pltpu.SMEM), name="bias_grad",
    )(dbias, bucket)


TILE = 2 * HEAD_DIM


def _pair_layout(t):
    lead = t.shape[:-3]
    t = t.reshape(lead + (N_KV, 2, 2) + t.shape[-2:])
    nl = len(lead)
    t = jnp.transpose(t, tuple(range(nl)) + (nl, nl + 2, nl + 1, nl + 3, nl + 4))
    return t.reshape(lead + (N_KV, 2, 2 * BLOCK, t.shape[-1]))


def _pair_unlayout(t):
    t = t.reshape(N_KV, 2, 2, BLOCK, t.shape[-1]).transpose(0, 2, 1, 3, 4)
    return t.reshape(N_HEADS, BLOCK, t.shape[-1])


def _halves(t):
    tf = t.astype(F32)
    low = lax.broadcasted_iota(jnp.int32, tf.shape, 1) < HEAD_DIM
    swapped = pltpu.roll(tf, HEAD_DIM, 1)
    zero = jnp.zeros_like(tf)
    return ((_bf(jnp.where(low, tf, zero)), _bf(jnp.where(low, zero, swapped))),
            (_bf(jnp.where(low, swapped, zero)), _bf(jnp.where(low, zero, tf))))


def _fold_halves(even, odd):
    low = lax.broadcasted_iota(jnp.int32, even.shape, 1) < HEAD_DIM
    comb = jnp.where(low, even, odd)
    return comb + pltpu.roll(comb, HEAD_DIM, 1)


def _tile_rows(ref, kh):
    return jnp.concatenate([ref[:, (2 * kh) * TILE:(2 * kh + 1) * TILE],
                            ref[:, (2 * kh + 1) * TILE:(2 * kh + 2) * TILE]], axis=0)


def _halves_t(t):
    tt = t.astype(F32).T
    top = lax.broadcasted_iota(jnp.int32, tt.shape, 0) < HEAD_DIM
    swapped = jnp.concatenate([tt[HEAD_DIM:], tt[:HEAD_DIM]], axis=0)
    zero = jnp.zeros_like(tt)
    return ((_bf(jnp.where(top, tt, zero)), _bf(jnp.where(top, zero, swapped))),
            (_bf(jnp.where(top, swapped, zero)), _bf(jnp.where(top, zero, tt))))


def _attn_probs(km, qk, bias, sink):
    lg = _mm_nt(km, qk) * (HEAD_DIM ** -0.5) + bias
    m = jnp.maximum(jnp.max(lg, axis=0, keepdims=True), sink)
    p = jnp.exp(lg - m)
    es = jnp.exp(sink - m)
    inv = 1.0 / (jnp.sum(p, axis=0, keepdims=True) + es)
    return p * inv, es * inv


def _attn_fwd(q, k, v, bias, sink_rows, exchange=None):
    seq = q.shape[0]
    nblk = seq // BLOCK

    def body(q_ref, kp_ref, kc_ref, vp_ref, vc_ref, b_ref, s_ref, o_ref):
        which = jnp.minimum(pl.program_id(0), 1)
        kms = _halves(jnp.concatenate([kp_ref[...], kc_ref[...]], axis=0))
        vts = _halves_t(jnp.concatenate([vp_ref[...], vc_ref[...]], axis=0))
        for kh in range(N_KV):
            qk = _tile_rows(q_ref, kh)
            acc = jnp.zeros((TILE, 2 * BLOCK), F32)
            for par in range(2):
                pr, _ = _attn_probs(kms[kh][par], qk, b_ref[which, kh, par], s_ref[kh, par])
                acc = acc + _mm(vts[kh][par], _bf(pr))
            acc = acc.T
            o_ref[:, (2 * kh) * TILE:(2 * kh + 1) * TILE] = _bf(acc[:BLOCK])
            o_ref[:, (2 * kh + 1) * TILE:(2 * kh + 2) * TILE] = _bf(acc[BLOCK:])

    cur = lambda n: (n, 0)
    prev = lambda n: (jnp.maximum(n - 1, 0), 0)
    return _fused_call(
        "attn_fwd", body, (nblk,),
        [pl.BlockSpec((BLOCK, ATTN_W), cur),
         pl.BlockSpec((BLOCK, KV_W), prev), pl.BlockSpec((BLOCK, KV_W), cur),
         pl.BlockSpec((BLOCK, KV_W), prev), pl.BlockSpec((BLOCK, KV_W), cur),
         pl.BlockSpec(bias.shape, functools.partial(_zero_map, bias.ndim)),
         pl.BlockSpec(sink_rows.shape, functools.partial(_zero_map, sink_rows.ndim))],
        [pl.BlockSpec((BLOCK, ATTN_W), cur)], [SDS((seq, ATTN_W), BF16)], [],
        [q, k, k, v, v, bias, sink_rows], exchange, _params(1))


def _attn_bwd(q, k, v, d_out, bias, sink_rows, exchange=None):
    seq = q.shape[0]
    nblk = seq // BLOCK

    def body(q_ref, kp_ref, kc_ref, vp_ref, vc_ref, do_ref, b_ref, s_ref,
             dq_ref, dk_ref, dv_ref, db_ref, ds_ref, ck_ref, cv_ref):
        n = pl.program_id(0)

        @pl.when(n == 0)
        def _():
            db_ref[...] = jnp.zeros_like(db_ref)
            ds_ref[...] = jnp.zeros_like(ds_ref)
            ck_ref[...] = jnp.zeros_like(ck_ref)
            cv_ref[...] = jnp.zeros_like(cv_ref)

        @pl.when(n < nblk)
        def _():
            which = jnp.minimum(n, 1)
            scale = HEAD_DIM ** -0.5
            kcat = jnp.concatenate([kp_ref[...], kc_ref[...]], axis=0)
            kms = _halves(kcat)
            kts = _halves_t(kcat)
            vms = _halves(jnp.concatenate([vp_ref[...], vc_ref[...]], axis=0))
            dks, dvs = [], []
            for kh in range(N_KV):
                qk = _tile_rows(q_ref, kh)
                dok = _tile_rows(do_ref, kh)
                dq = jnp.zeros((TILE, 2 * BLOCK), F32)
                dkp, dvp = [], []
                for par in range(2):
                    pr, ps = _attn_probs(kms[kh][par], qk, b_ref[which, kh, par], s_ref[kh, par])
                    dp = _mm_nt(vms[kh][par], dok)
                    rs = jnp.sum(pr * dp, axis=0, keepdims=True)
                    dlg = pr * (dp - rs)
                    ds_ref[kh, par] += -ps * rs
                    db_ref[kh, par] += dlg
                    dlb = _bf(dlg)
                    dq = dq + _mm(kts[kh][par], dlb)
                    dkp.append(_mm(dlb, qk))
                    dvp.append(_mm(_bf(pr), dok))
                dq = _bf((dq * scale).T)
                dq_ref[:, (2 * kh) * TILE:(2 * kh + 1) * TILE] = dq[:BLOCK]
                dq_ref[:, (2 * kh + 1) * TILE:(2 * kh + 2) * TILE] = dq[BLOCK:]
                dks.append(_fold_halves(*dkp))
                dvs.append(_fold_halves(*dvp))
            low = lax.broadcasted_iota(jnp.int32, (2 * BLOCK, TILE), 1) < HEAD_DIM
            dkk = jnp.where(low, dks[0], dks[1]) * scale
            dvv = jnp.where(low, dvs[0], dvs[1])
            dk_ref[...] = _bf(ck_ref[...] + dkk[:BLOCK])
            ck_ref[...] = dkk[BLOCK:]
            dv_ref[...] = _bf(cv_ref[...] + dvv[:BLOCK])
            cv_ref[...] = dvv[BLOCK:]

        @pl.when(n == nblk)
        def _():
            dk_ref[...] = _bf(ck_ref[...])
            dv_ref[...] = _bf(cv_ref[...])

    cur = lambda n: (jnp.minimum(n, nblk - 1), 0)
    prev = lambda n: (jnp.maximum(jnp.minimum(n, nblk - 1) - 1, 0), 0)
    late = lambda n: (jnp.maximum(n - 1, 0), 0)
    kv_spec = lambda m: pl.BlockSpec((BLOCK, KV_W), m)
    acc_b = pl.BlockSpec(bias.shape[1:], functools.partial(_zero_map, bias.ndim - 1))
    acc_s = pl.BlockSpec(sink_rows.shape, functools.partial(_zero_map, sink_rows.ndim))
    return _fused_call(
        "attn_bwd", body, (nblk + 1,),
        [pl.BlockSpec((BLOCK, ATTN_W), cur), kv_spec(prev), kv_spec(cur), kv_spec(prev), kv_spec(cur),
         pl.BlockSpec((BLOCK, ATTN_W), cur),
         pl.BlockSpec(bias.shape, functools.partial(_zero_map, bias.ndim)), acc_s],
        [pl.BlockSpec((BLOCK, ATTN_W), cur), kv_spec(late), kv_spec(late), acc_b, acc_s],
        [SDS((seq, ATTN_W), BF16), SDS((seq, KV_W), BF16), SDS((seq, KV_W), BF16),
         SDS(bias.shape[1:], F32), SDS(sink_rows.shape, F32)],
        [pltpu.VMEM((BLOCK, KV_W), F32), pltpu.VMEM((BLOCK, KV_W), F32)],
        [q, k, k, v, v, d_out, bias, sink_rows], exchange, _params(1))


def _ssm_discretize(lam_re, lam_im, log_dt, b_re, b_im):
    dt = jnp.exp(log_dt)[:, None]
    mag = jnp.exp(lam_re * dt)
    ab_re = mag * jnp.cos(lam_im * dt)
    ab_im = mag * jnp.sin(lam_im * dt)
    nr = ab_re - 1.0
    den = lam_re * lam_re + lam_im * lam_im
    f_re = (nr * lam_re + ab_im * lam_im) / den
    f_im = (ab_im * lam_re - nr * lam_im) / den
    bb_re = f_re[..., None] * b_re - f_im[..., None] * b_im
    bb_im = f_re[..., None] * b_im + f_im[..., None] * b_re
    return ab_re, ab_im, bb_re, bb_im


def _state_layout(re, im):
    z = jnp.stack([re, im]).reshape(2, N_SUPER, GROUPS_PER_SUPER, SSM_STATE)
    return z.transpose(1, 0, 2, 3).reshape(STATE_COLS)


def _state_unlayout(vec):
    z = vec.reshape(N_SUPER, 2, GROUPS_PER_SUPER, SSM_STATE).transpose(1, 0, 2, 3)
    z = z.reshape(2, SSM_GROUPS, SSM_STATE)
    return z[0], z[1]


def _scan_tables(ab_re, ab_im):
    pw = [None, (ab_re, ab_im)]
    for _ in range(2, SUBLANES + 1):
        pr, pi_ = pw[-1]
        pw.append((pr * ab_re - pi_ * ab_im, pr * ab_im + pi_ * ab_re))
    rows = np.arange(SUBLANES)[:, None]
    fwd, bwd = [], []
    for shift in (1, 2, 4):
        fwd.append(_state_layout(*pw[shift])[None, :] * (rows >= shift).astype(np.float32))
        bwd.append(_state_layout(pw[shift][0], -pw[shift][1])[None, :] * (rows < SUBLANES - shift).astype(np.float32))
    fwd.append(jnp.stack([_state_layout(*pw[r + 1]) for r in range(SUBLANES)]))
    bwd.append(jnp.stack([_state_layout(pw[SUBLANES - r][0], -pw[SUBLANES - r][1]) for r in range(SUBLANES)]))
    return jnp.stack(fwd), jnp.stack(bwd)


_EYE = np.eye(GROUPS_PER_SUPER, dtype=np.float32)


def _b_matrix(bb_re, bb_im):
    bb = jnp.stack([bb_re, bb_im]).reshape(2, N_SUPER, GROUPS_PER_SUPER, SSM_STATE, SSM_GROUP)
    m = jnp.einsum('rsgpc,gh->sgcrhp', bb, _EYE)
    return m.reshape(N_SUPER, SUPER_IN, SUPER_W)


def _b_matrix_grad(dm):
    d = dm.reshape(N_SUPER, GROUPS_PER_SUPER, SSM_GROUP, 2, GROUPS_PER_SUPER, SSM_STATE)
    d = jnp.sum(d * _EYE[None, :, None, None, :, None], axis=4)
    d = d.transpose(3, 0, 1, 4, 2).reshape(2, SSM_GROUPS, SSM_STATE, SSM_GROUP)
    return d[0], d[1]


def _c_matrix(c_re, c_im):
    cc = jnp.stack([c_re, -c_im]).reshape(2, N_SUPER, GROUPS_PER_SUPER, SSM_GROUP, SSM_STATE)
    m = jnp.einsum('rsgcp,gh->srgphc', cc, _EYE)
    return m.reshape(N_SUPER, SUPER_W, SUPER_IN)


def _c_matrix_grad(dm):
    d = dm.reshape(N_SUPER, 2, GROUPS_PER_SUPER, SSM_STATE, GROUPS_PER_SUPER, SSM_GROUP)
    d = jnp.sum(d * _EYE[None, None, :, None, :, None], axis=4)
    d = d.transpose(1, 0, 2, 4, 3).reshape(2, SSM_GROUPS, SSM_GROUP, SSM_STATE)
    return d[0], -d[1]


def _scan_rows(buf_ref, tab_ref, carry_ref, n_groups, reverse, h_ref=None, da_ref=None):
    edge = 0 if reverse else SUBLANES - 1
    for sb in range(N_SUPER):
        cr = pl.ds(sb * SUPER_W, SUPER_HALF)
        ci = pl.ds(sb * SUPER_W + SUPER_HALF, SUPER_HALF)

        def step(gi, carry, cr=cr, ci=ci):
            g = (n_groups - 1 - gi) if reverse else gi
            rows = pl.ds(pl.multiple_of(g * SUBLANES, SUBLANES), SUBLANES)
            c_re, c_im = carry[0], carry[1]
            xr = buf_ref[rows, cr]
            xi = buf_ref[rows, ci]
            for k, shift in enumerate((1, 2, 4)):
                s = (SUBLANES - shift) if reverse else shift
                sr = pltpu.roll(xr, s, 0)
                si = pltpu.roll(xi, s, 0)
                ar = tab_ref[k, :, cr]
                ai = tab_ref[k, :, ci]
                xr, xi = xr + ar * sr - ai * si, xi + ar * si + ai * sr
            pr = tab_ref[3, :, cr]
            pi_ = tab_ref[3, :, ci]
            xr, xi = xr + pr * c_re - pi_ * c_im, xi + pr * c_im + pi_ * c_re
            buf_ref[rows, cr] = xr
            buf_ref[rows, ci] = xi
            out = [jnp.broadcast_to(xr[edge:edge + 1], xr.shape), jnp.broadcast_to(xi[edge:edge + 1], xi.shape)]
            if h_ref is not None:
                last = lax.broadcasted_iota(jnp.int32, xr.shape, 0) == SUBLANES - 1
                gr = jnp.where(last, c_re, pltpu.roll(xr, SUBLANES - 1, 0))
                gim = jnp.where(last, c_im, pltpu.roll(xi, SUBLANES - 1, 0))
                hr = h_ref[rows, cr]
                hi = h_ref[rows, ci]
                out += [carry[2] + gr * hr + gim * hi, carry[3] + gim * hr - gr * hi]
            return tuple(out)

        init = [carry_ref[:, cr], carry_ref[:, ci]]
        if h_ref is not None:
            init += [da_ref[:, cr], da_ref[:, ci]]
        fin = lax.fori_loop(0, n_groups, step, tuple(init))
        carry_ref[:, cr] = fin[0]
        carry_ref[:, ci] = fin[1]
        if h_ref is not None:
            da_ref[:, cr] = fin[2]
            da_ref[:, ci] = fin[3]


def _ssm_fwd(u, bmat, cmat, tab, d_skip, tb, exchange=None):
    seq = u.shape[0]

    def body(u_ref, b_ref, c_ref, t_ref, d_ref, s_ref, h_ref, carry_ref):
        @pl.when(pl.program_id(0) == 0)
        def _():
            carry_ref[...] = jnp.zeros_like(carry_ref)

        u_blk = u_ref[...]
        ub = _bf(u_blk)
        for sb in range(N_SUPER):
            h_ref[:, sb * SUPER_W:(sb + 1) * SUPER_W] = _mm(ub[:, sb * SUPER_IN:(sb + 1) * SUPER_IN], b_ref[sb])
        _scan_rows(h_ref, t_ref, carry_ref, tb // SUBLANES, False)
        ys = [_mm(_bf(h_ref[:, sb * SUPER_W:(sb + 1) * SUPER_W]), c_ref[sb]) for sb in range(N_SUPER)]
        s_ref[...] = jnp.concatenate(ys, axis=1) + d_ref[...] * u_blk

    return _rowcall("ssm_fwd", body, seq, tb, [u], [bmat, cmat, tab, d_skip],
                    [(SSM_W, F32), (STATE_COLS, F32)], [],
                    scratch=[pltpu.VMEM((SUBLANES, STATE_COLS), F32)], vmem=VMEM_BIG, exchange=exchange)


def _ssm_bwd(ds, u, h, bmat_t, cmat_t, tab, d_skip, tb, exchange=None):
    seq = u.shape[0]

    def body(ds_ref, u_ref, h_ref, bt_ref, ct_ref, t_ref, d_ref,
             du_ref, db_ref, dc_ref, da_ref, dd_ref, g_ref, carry_ref):
        @pl.when(pl.program_id(0) == 0)
        def _():
            carry_ref[...] = jnp.zeros_like(carry_ref)
            db_ref[...] = jnp.zeros_like(db_ref)
            dc_ref[...] = jnp.zeros_like(dc_ref)
            da_ref[...] = jnp.zeros_like(da_ref)
            dd_ref[...] = jnp.zeros_like(dd_ref)

        ds_blk = ds_ref[...]
        dsb = _bf(ds_blk)
        u_blk = u_ref[...]
        ub = _bf(u_blk)
        for sb in range(N_SUPER):
            g_ref[:, sb * SUPER_W:(sb + 1) * SUPER_W] = _mm(dsb[:, sb * SUPER_IN:(sb + 1) * SUPER_IN], ct_ref[sb])
        _scan_rows(g_ref, t_ref, carry_ref, tb // SUBLANES, True, h_ref=h_ref, da_ref=da_ref)
        dus = []
        for sb in range(N_SUPER):
            gb = _bf(g_ref[:, sb * SUPER_W:(sb + 1) * SUPER_W])
            dus.append(_mm(gb, bt_ref[sb]))
            db_ref[sb] += _mm_tn(ub[:, sb * SUPER_IN:(sb + 1) * SUPER_IN], gb)
            dc_ref[sb] += _mm_tn(_bf(h_ref[:, sb * SUPER_W:(sb + 1) * SUPER_W]),
                                 dsb[:, sb * SUPER_IN:(sb + 1) * SUPER_IN])
        du_ref[...] = jnp.concatenate(dus, axis=1) + d_ref[...] * ds_blk
        dd_ref[...] += jnp.sum(ds_blk * u_blk, axis=0, keepdims=True)

    return _rowcall("ssm_bwd", body, seq, tb, [ds, u, h], [bmat_t, cmat_t, tab, d_skip],
                    [(SSM_W, F32)],
                    [((N_SUPER, SUPER_IN, SUPER_W), F32), ((N_SUPER, SUPER_W, SUPER_IN), F32),
                     ((SUBLANES, STATE_COLS), F32), ((1, SSM_W), F32)],
                    scratch=[pltpu.VMEM((tb, STATE_COLS), F32), pltpu.VMEM((SUBLANES, STATE_COLS), F32)],
                    reverse=True, vmem=VMEM_BIG, exchange=exchange)


def _merge_core(s, attb, ga, gs, wg_ref, wab_ref, wsb_ref, wout_ref):
    zg, dgelu = _gelu_and_grad(s)
    zgb = _bf(zg)
    sg = _sig(_mm(zgb, wg_ref[...]))
    z = zg * sg
    zb = _bf(z)
    ys = jnp.concatenate([_mm(zb, wsb_ref[j]) for j in range(N_CHIPS)], axis=1)
    ya = jnp.concatenate([_mm(attb, wab_ref[j]) for j in range(N_CHIPS)], axis=1)
    sa = _sig(ga)
    ss = _sig(gs)
    mgb = _bf(sa * ya + ss * ys)
    o = _mm(mgb, wout_ref[...])
    return dict(zg=zg, dgelu=dgelu, zgb=zgb, sg=sg, zb=zb, ys=ys, ya=ya, sa=sa, ss=ss, mgb=mgb, o=o)


def _merge_fwd(x, s, att, ga, gs, g2, w_glu, w_ab, w_sb, w_out, tb):
    seq = x.shape[0]

    def body(x_ref, s_ref, att_ref, ga_ref, gs_ref, g_ref, wg_ref, wab_ref, wsb_ref, wout_ref, x2_ref):
        f = _merge_core(s_ref[...], att_ref[...], ga_ref[...], gs_ref[...], wg_ref, wab_ref, wsb_ref, wout_ref)
        n, _, _ = _rms(f["o"], g_ref[...])
        x2_ref[...] = x_ref[...] + n

    return _rowcall("merge_fwd", body, seq, tb, [x, s, att, ga, gs], [g2, w_glu, w_ab, w_sb, w_out],
                    [(D_MODEL, F32)], [], vmem=VMEM_BIG)[0]


def _merge_bwd(dx2, s, att, ga, gs, g2, w_glu, w_ab, w_sb, w_out, tb, exchange=None):
    seq = s.shape[0]
    cw = D_MODEL // N_CHIPS
    last = seq // tb - 1

    def body(dx2_ref, s_ref, att_ref, ga_ref, gs_ref, g_ref, wg_ref, wab_ref, wsb_ref, wout_ref,
             ds_ref, datt_ref, dga_ref, dgs_ref, dg_ref, dwg_ref, dwab_ref, dwsb_ref, dwout_ref,
             bwg_ref, bwab_ref, bwsb_ref, bwout_ref):
        @pl.when(pl.program_id(0) == 0)
        def _():
            for r in (dg_ref, dwg_ref, dwab_ref, dwsb_ref, dwout_ref):
                r[...] = jnp.zeros_like(r)

        attb = att_ref[...]
        f = _merge_core(s_ref[...], attb, ga_ref[...], gs_ref[...], wg_ref, wab_ref, wsb_ref, wout_ref)
        g = g_ref[...]
        _, oh, r2 = _rms(f["o"], g)
        do, dg = _rms_bwd(dx2_ref[...], oh, r2, g)
        dg_ref[...] += dg
        dob = _bf(do)
        dwout_ref[...] += _mm_tn(f["mgb"], dob)
        dmg = _mm_nt(dob, wout_ref[...])
        sa, ss = f["sa"], f["ss"]
        dyab = _bf(dmg * sa)
        dysb = _bf(dmg * ss)
        dga_ref[...] = _bf(dmg * f["ya"] * sa * (1.0 - sa))
        dgs_ref[...] = _bf(dmg * f["ys"] * ss * (1.0 - ss))
        dwab = _mm_tn(attb, dyab)
        dwsb = _mm_tn(f["zb"], dysb)
        datt = jnp.zeros((tb, ATTN_W), F32)
        dz = jnp.zeros((tb, SSM_W), F32)
        for j in range(N_CHIPS):
            dwab_ref[j] += dwab[:, j * cw:(j + 1) * cw]
            dwsb_ref[j] += dwsb[:, j * cw:(j + 1) * cw]
            datt = datt + _mm_nt(dyab[:, j * cw:(j + 1) * cw], wab_ref[j])
            dz = dz + _mm_nt(dysb[:, j * cw:(j + 1) * cw], wsb_ref[j])
        datt_ref[...] = _bf(datt)
        sg, zg = f["sg"], f["zg"]
        dglb = _bf(dz * zg * sg * (1.0 - sg))
        dwg_ref[...] += _mm_tn(f["zgb"], dglb)
        dzg = dz * sg + _mm_nt(dglb, wg_ref[...])
        ds_ref[...] = dzg * f["dgelu"]

        @pl.when(pl.program_id(0) == last)
        def _():
            for dst, src in ((bwg_ref, dwg_ref), (bwab_ref, dwab_ref), (bwsb_ref, dwsb_ref), (bwout_ref, dwout_ref)):
                dst[...] = _bf(src[...])

    shapes = [w_glu.shape, w_ab.shape, w_sb.shape, w_out.shape]
    return _rowcall("merge_bwd", body, seq, tb, [dx2, s, att, ga, gs], [g2, w_glu, w_ab, w_sb, w_out],
                    [(SSM_W, F32), (ATTN_W, BF16), (D_MODEL, BF16), (D_MODEL, BF16)],
                    [((1, D_MODEL), F32)] + [(sh, F32) for sh in shapes] + [(sh, BF16) for sh in shapes],
                    vmem=VMEM_BIG, exchange=exchange)


def _mlp_fwd_loss(x2, target, g3, g4, w_ffi, w_ffo, tb):
    seq = x2.shape[0]

    def body(x2_ref, t_ref, g3_ref, g4_ref, wi_ref, wo_ref, dy_ref, df_ref, h_ref, loss_ref, dg_ref):
        @pl.when(pl.program_id(0) == 0)
        def _():
            loss_ref[...] = jnp.zeros_like(loss_ref)
            dg_ref[...] = jnp.zeros_like(dg_ref)

        x2_blk = x2_ref[...]
        h3, _, _ = _rms(x2_blk, g3_ref[...])
        hb = _bf(h3)
        h_ref[...] = hb
        f = jnp.zeros((tb, D_MODEL), F32)
        for j in range(FF_CHUNKS):
            a = _mm(hb, wi_ref[j])
            f = f + _mm(_bf(jnp.square(jnp.maximum(a, 0.0))), wo_ref[j])
        g4 = g4_ref[...]
        n4, fh, r4 = _rms(f, g4)
        e = (x2_blk + n4) - t_ref[...]
        loss_ref[...] += 0.5 * jnp.sum(jnp.mean(e * e, axis=-1, keepdims=True))
        dy = e * (1.0 / D_MODEL)
        dy_ref[...] = dy
        df, dg = _rms_bwd(dy, fh, r4, g4)
        df_ref[...] = _bf(df)
        dg_ref[...] += dg

    return _rowcall("mlp_fwd_loss", body, seq, tb, [x2, target], [g3, g4, w_ffi, w_ffo],
                    [(D_MODEL, F32), (D_MODEL, BF16), (D_MODEL, BF16)],
                    [((SUBLANES, 128), F32), ((1, D_MODEL), F32)], vmem=VMEM_BIG)


def _mlp_bwd(x2, dy, df, h3, g3, w_ffi, w_ffo, tb):
    seq = x2.shape[0]
    cw = D_FF // FF_CHUNKS

    def body(x2_ref, dy_ref, df_ref, h_ref, g3_ref, wi_ref, wo_ref, dx_ref, act_ref, da_ref, dg_ref):
        @pl.when(pl.program_id(0) == 0)
        def _():
            dg_ref[...] = jnp.zeros_like(dg_ref)

        hb = h_ref[...]
        dfb = df_ref[...]
        dh = jnp.zeros((tb, D_MODEL), F32)
        for j in range(FF_CHUNKS):
            ra = jnp.maximum(_mm(hb, wi_ref[j]), 0.0)
            act_ref[:, j * cw:(j + 1) * cw] = _bf(ra * ra)
            dab = _bf(_mm_nt(dfb, wo_ref[j]) * (2.0 * ra))
            da_ref[:, j * cw:(j + 1) * cw] = dab
            dh = dh + _mm_nt(dab, wi_ref[j])
        g3 = g3_ref[...]
        _, xh, r3 = _rms(x2_ref[...], g3)
        dxn, dg = _rms_bwd(dh, xh, r3, g3)
        dx_ref[...] = dy_ref[...] + dxn
        dg_ref[...] += dg

    return _rowcall("mlp_bwd", body, seq, tb, [x2, dy, df, h3], [g3, w_ffi, w_ffo],
                    [(D_MODEL, F32), (D_FF, BF16), (D_FF, BF16)], [((1, D_MODEL), F32)], vmem=VMEM_BIG)


def _matmul_tn(name, a, b, tk, tn, tl, chunk_major, exchange=None):
    seq, kdim = a.shape
    ndim = b.shape[1]
    last = seq // tl - 1

    def body(a_ref, b_ref, o_ref, ob_ref):
        @pl.when(pl.program_id(2) == 0)
        def _():
            o_ref[...] = jnp.zeros_like(o_ref)

        o_ref[...] += _mm_tn(a_ref[...], b_ref[...])

        @pl.when(pl.program_id(2) == last)
        def _():
            ob_ref[...] = _bf(o_ref[...])

    if chunk_major:
        shape = (ndim // tn, kdim, tn)
        out_spec = pl.BlockSpec((None, tk, tn), lambda k, n, l: (n, k, 0))
    else:
        shape = (kdim, ndim)
        out_spec = pl.BlockSpec((tk, tn), lambda k, n, l: (k, n))
    return _fused_call(
        name, body, (kdim // tk, ndim // tn, seq // tl),
        [pl.BlockSpec((tl, tk), lambda k, n, l: (l, k)), pl.BlockSpec((tl, tn), lambda k, n, l: (l, n))],
        [out_spec, out_spec], [SDS(shape, F32), SDS(shape, BF16)], [], [a, b], exchange, _params(3, VMEM_BIG))


def _ew_call(name, fn, ins, n_out, exchange=None):
    rows, cols = ins[0].shape
    tr = rows
    while tr * cols * 4 > min(1 << 20, (9 << 20) // (len(ins) + n_out)) and tr % 16 == 0:
        tr //= 2
    spec = pl.BlockSpec((tr, cols), lambda i: (i, 0))

    def body(*refs):
        outs = fn(*[r[...] for r in refs[:len(ins)]])
        for r, o in zip(refs[len(ins):], outs):
            r[...] = o

    return _fused_call(name, body, (rows // tr,), [spec] * len(ins), [spec] * n_out,
                       [SDS((rows, cols), F32)] * n_out, [], list(ins), exchange, _params(1))


def _adam_math(w, g, m, v):
    m2 = ADAM_B1 * m + (1.0 - ADAM_B1) * g
    v2 = ADAM_B2 * v + (1.0 - ADAM_B2) * (g * g)
    m_hat = m2 / (1.0 - ADAM_B1 ** ADAM_STEP)
    v_hat = v2 / (1.0 - ADAM_B2 ** ADAM_STEP)
    delta = -ADAM_LR * (m_hat / (jnp.sqrt(v_hat) + ADAM_EPS) + ADAM_WD * w)
    return delta, m2, v2


def _sum4(name, own, recv, idx):
    _, rows, cols = own.shape
    tr = rows
    while tr * cols * 4 > (1 << 20) and tr % 16 == 0:
        tr //= 2

    def body(idx_ref, o_ref, r0_ref, r1_ref, r2_ref, out_ref):
        out_ref[...] = ((o_ref[...] + r0_ref[...].astype(F32)) + r1_ref[...].astype(F32)) + r2_ref[...].astype(F32)

    blk = (None, tr, cols)
    grid_spec = pltpu.PrefetchScalarGridSpec(
        num_scalar_prefetch=1, grid=(rows // tr,),
        in_specs=[pl.BlockSpec(blk, lambda i, s: (s[0], i, 0)), pl.BlockSpec(blk, lambda i, s: (0, i, 0)),
                  pl.BlockSpec(blk, lambda i, s: (1, i, 0)), pl.BlockSpec(blk, lambda i, s: (2, i, 0))],
        out_specs=pl.BlockSpec((tr, cols), lambda i, s: (i, 0)))
    return pl.pallas_call(body, grid_spec=grid_spec, out_shape=SDS((rows, cols), F32), name=name,
                          compiler_params=_params(1))(jnp.reshape(idx, (1,)).astype(jnp.int32), own, recv, recv, recv)


def _adam_pair(name, items, exchange=None):
    def fn(*vals):
        outs = ()
        for i in range(len(items)):
            w_, a, b, m_, v_ = vals[5 * i:5 * i + 5]
            g = a + b
            outs += (g,) + _adam_math(w_, g, m_, v_)
        return outs

    flat = _ew_call(name, fn, [a for item in items for a in item], 4 * len(items), exchange)
    return [flat[4 * i:4 * i + 4] for i in range(len(items))] + [list(flat[4 * len(items):])]


def _place():
    return lax.axis_index("x"), lax.axis_index("y"), lax.axis_index("c")


def _other_chips(x, y):
    return [(1 - x, y), (x, 1 - y), (1 - x, 1 - y)]


def _gather_chips(shards):
    n = len(shards)

    def copies(ins, outs, sems):
        send, recv, fwd_send, fwd_recv, loc = sems
        x, y, c = _place()
        me = 2 * x + y
        peers = _other_chips(x, y)
        local = [pltpu.make_async_copy(ins[a], outs[a].at[me], loc.at[a]) for a in range(n)]
        sends, recvs, passes, passed = [], [], [], []
        for a in range(n):
            half = shards[a].shape[0] // 2
            mine = pl.ds(c * half, half)
            theirs = pl.ds((1 - c) * half, half)
            for j, (px, py) in enumerate(peers):
                far = 2 * px + py
                sends.append(pltpu.make_async_remote_copy(
                    src_ref=ins[a].at[mine], dst_ref=outs[a].at[me, mine], send_sem=send.at[a, j],
                    recv_sem=recv.at[a, j], device_id=(px, py, c), device_id_type=MESH_ID))
                recvs.append(pltpu.make_async_remote_copy(
                    src_ref=ins[a].at[mine], dst_ref=outs[a].at[far, mine], send_sem=send.at[a, j],
                    recv_sem=recv.at[a, j], device_id=(px, py, c), device_id_type=MESH_ID))
                passes.append(pltpu.make_async_remote_copy(
                    src_ref=outs[a].at[far, mine], dst_ref=outs[a].at[far, mine], send_sem=fwd_send.at[a, j],
                    recv_sem=fwd_recv.at[a, j], device_id=(x, y, 1 - c), device_id_type=MESH_ID))
                passed.append(pltpu.make_async_remote_copy(
                    src_ref=outs[a].at[far, theirs], dst_ref=outs[a].at[far, theirs], send_sem=fwd_send.at[a, j],
                    recv_sem=fwd_recv.at[a, j], device_id=(x, y, 1 - c), device_id_type=MESH_ID))
        return local, sends, recvs, passes, passed

    def start(ins, outs, sems):
        local, sends, _, _, _ = copies(ins, outs, sems)
        for cp in local + sends:
            cp.start()

    def wait(ins, outs, sems):
        local, sends, recvs, passes, passed = copies(ins, outs, sems)
        for got, on in zip(recvs, passes):
            got.wait_recv()
            on.start()
        for cp in passed:
            cp.wait_recv()
        for cp in passes + sends:
            cp.wait_send()
        for cp in local:
            cp.wait()

    assert all(s.shape[0] % 32 == 0 for s in shards)
    pair = pltpu.SemaphoreType.DMA((n, 3))
    return _Exchange(shards, [SDS((N_CHIPS,) + s.shape, s.dtype) for s in shards],
                     [pair, pair, pair, pair, pltpu.SemaphoreType.DMA((n,))], start, wait)


def _scatter_chips(chunks):
    n = len(chunks)

    def copies(ins, outs, sems):
        send, recv = sems
        x, y, c = _place()
        return [pltpu.make_async_remote_copy(
            src_ref=ins[a].at[2 * px + py], dst_ref=outs[a].at[j], send_sem=send.at[a, j],
            recv_sem=recv.at[a, j], device_id=(px, py, c), device_id_type=MESH_ID)
            for a in range(n) for j, (px, py) in enumerate(_other_chips(x, y))]

    def start(ins, outs, sems):
        for cp in copies(ins, outs, sems):
            cp.start()

    def wait(ins, outs, sems):
        cps = copies(ins, outs, sems)
        for cp in cps:
            cp.wait_recv()
        for cp in cps:
            cp.wait_send()

    return _Exchange(chunks, [SDS((3,) + s.shape[1:], s.dtype) for s in chunks],
                     [pltpu.SemaphoreType.DMA((n, 3)), pltpu.SemaphoreType.DMA((n, 3))], start, wait)


def _swap_sibling(arrs):
    n = len(arrs)

    def copies(ins, outs, sems):
        send, recv = sems
        x, y, c = _place()
        return [pltpu.make_async_remote_copy(
            src_ref=ins[a], dst_ref=outs[a], send_sem=send.at[a], recv_sem=recv.at[a],
            device_id=(x, y, 1 - c), device_id_type=MESH_ID) for a in range(n)]

    def start(ins, outs, sems):
        for cp in copies(ins, outs, sems):
            cp.start()

    def wait(ins, outs, sems):
        cps = copies(ins, outs, sems)
        for cp in cps:
            cp.wait_recv()
        for cp in cps:
            cp.wait_send()

    return _Exchange(arrs, [SDS(s.shape, s.dtype) for s in arrs],
                     [pltpu.SemaphoreType.DMA((n,)), pltpu.SemaphoreType.DMA((n,))], start, wait)


N_DEV = 8


def _gather_devices(block):
    def copies(ins, outs, sems):
        send, recv, loc = sems
        x, y, c = _place()
        me = 4 * x + 2 * y + c
        local = pltpu.make_async_copy(ins[0], outs[0].at[me], loc.at[0])
        sends, recvs = [], []
        for k in range(1, N_DEV):
            peer = (x ^ (k >> 2), y ^ ((k >> 1) & 1), c ^ (k & 1))
            for group, slot in ((sends, me), (recvs, me ^ k)):
                group.append(pltpu.make_async_remote_copy(
                    src_ref=ins[0], dst_ref=outs[0].at[slot], send_sem=send.at[k - 1], recv_sem=recv.at[k - 1],
                    device_id=peer, device_id_type=MESH_ID))
        return local, sends, recvs

    def start(ins, outs, sems):
        local, sends, _ = copies(ins, outs, sems)
        for cp in [local] + sends:
            cp.start()

    def wait(ins, outs, sems):
        local, sends, recvs = copies(ins, outs, sems)
        for cp in recvs:
            cp.wait_recv()
        for cp in sends:
            cp.wait_send()
        local.wait()

    return _Exchange([block], [SDS((N_DEV,) + block.shape, block.dtype)],
                     [pltpu.SemaphoreType.DMA((N_DEV - 1,)), pltpu.SemaphoreType.DMA((N_DEV - 1,)),
                      pltpu.SemaphoreType.DMA((1,))], start, wait)


def _both(ex_a, ex_b):
    na_i, na_o, na_s = len(ex_a.ins), len(ex_a.outs), len(ex_a.sems)

    def start(ins, outs, sems):
        ex_a.start(ins[:na_i], outs[:na_o], sems[:na_s])
        ex_b.start(ins[na_i:], outs[na_o:], sems[na_s:])

    def wait(ins, outs, sems):
        ex_a.wait(ins[:na_i], outs[:na_o], sems[:na_s])
        ex_b.wait(ins[na_i:], outs[na_o:], sems[na_s:])

    return _Exchange(ex_a.ins + ex_b.ins, ex_a.outs + ex_b.outs, ex_a.sems + ex_b.sems, start, wait)


def _sum_devices(slots):
    def body(s_ref, o_ref):
        acc = s_ref[0]
        for d in range(1, N_DEV):
            acc = acc + s_ref[d]
        o_ref[...] = acc

    return pl.pallas_call(
        body, in_specs=[pl.BlockSpec(memory_space=pltpu.VMEM)], out_specs=pl.BlockSpec(memory_space=pltpu.VMEM),
        out_shape=SDS(slots.shape[1:], F32), name="sum_small",
        compiler_params=pltpu.CompilerParams(vmem_limit_bytes=32 * 1024 * 1024))(slots)


def _adam_small(ws, gs, ms, vs):
    n = len(ws)

    def body(*refs):
        for i in range(n):
            w_ref, g_ref, m_ref, v_ref = (refs[k * n + i] for k in range(4))
            outs = _adam_math(w_ref[...], g_ref[...], m_ref[...], v_ref[...])
            for k in range(3):
                refs[(4 + k) * n + i][...] = outs[k]

    vmem = pl.BlockSpec(memory_space=pltpu.VMEM)
    return pl.pallas_call(
        body, in_specs=[vmem] * (4 * n), out_specs=[vmem] * (3 * n),
        out_shape=[SDS(w.shape, F32) for w in ws] * 3, name="adam_small",
        compiler_params=pltpu.CompilerParams(vmem_limit_bytes=32 * 1024 * 1024))(*ws, *gs, *ms, *vs)


def _local_step(x, target, small, big, tb, distributed):
    g1, g2, g3, g4 = small["norm_mix_pre"], small["norm_mix_post"], small["norm_mlp_pre"], small["norm_mlp_post"]
    dist = distributed
    me = (2 * lax.axis_index("x") + lax.axis_index("y")) if dist else 0
    tb_ssm = min(tb, 256)
    bucket = jnp.asarray(_bucket_table())

    keys_first = lambda t: jnp.swapaxes(t, -1, -2)
    bias = keys_first(_pair_layout(_bias_table(small["rel_bias"], bucket)))
    sink_rows = keys_first(_pair_layout(jnp.broadcast_to(small["sinks"].reshape(N_HEADS, 1, 1), (N_HEADS, BLOCK, 1))))
    disc_args = (small["lam_re"], small["lam_im"], small["log_dt"], small["b_re"], small["b_im"])
    (ab_re, ab_im, bb_re, bb_im), disc_vjp = jax.vjp(_ssm_discretize, *disc_args)
    tab_f, tab_b = _scan_tables(ab_re, ab_im)
    bmat = _bf(_b_matrix(bb_re, bb_im))
    cmat = _bf(_c_matrix(small["c_re"], small["c_im"]))
    d_skip = small["d_skip"]

    if dist:
        (g_in,) = _exchange_alone("gather_w_in", _gather_chips([big["w_in"]]))
        w_in = g_in.reshape(IN_W, D_MODEL)
    else:
        w_in = big["w_in"]
    mix = ("w_glu", "w_attn_branch", "w_ssm_branch", "w_out")
    outs = _inproj_fwd(x, g1, w_in, tb, _gather_chips([big[n] for n in mix]) if dist else None)
    h1, q, k, v, u, ga, gs = outs[:7]
    w_glu, w_ab, w_sb, w_out = outs[7:] if dist else [big[n] for n in mix]
    w_glu = w_glu.reshape(SSM_W, SSM_W)
    w_out = w_out.reshape(D_MODEL, D_MODEL)
    outs = _attn_fwd(q, k, v, bias, sink_rows, _gather_chips([big["w_ff_in"]]) if dist else None)
    att = outs[0]
    w_ffi = outs[1] if dist else big["w_ff_in"]
    outs = _ssm_fwd(u, bmat, cmat, tab_f, d_skip, tb_ssm, _gather_chips([big["w_ff_out"]]) if dist else None)
    s, h = outs[:2]
    w_ffo = outs[2] if dist else big["w_ff_out"]
    x2 = _merge_fwd(x, s, att, ga, gs, g2, w_glu, w_ab, w_sb, w_out, tb)
    dy, df, h3, loss_acc, dg4 = _mlp_fwd_loss(x2, target, g3, g4, w_ffi, w_ffo, tb)

    dx2, act, da, dg3 = _mlp_bwd(x2, dy, df, h3, g3, w_ffi, w_ffo, tb)
    tl = min(2048, x.shape[0])
    chunked = (N_CHIPS, D_FF // N_CHIPS, D_MODEL)
    d_ffi, b_ffi = _matmul_tn("grad_w_ff_in", h3, da, D_MODEL, D_FF // FF_CHUNKS, tl, True)
    d_ffo, b_ffo = _matmul_tn("grad_w_ff_out", act, df, D_FF // FF_CHUNKS, D_MODEL, tl, False)
    d_ffo, b_ffo = d_ffo.reshape(chunked), b_ffo.reshape(chunked)
    outs = _merge_bwd(dx2, s, att, ga, gs, g2, w_glu, w_ab, w_sb, w_out, tb_ssm,
                      _scatter_chips([b_ffi]) if dist else None)
    ds, datt, dga, dgs, dg2, d_glu, d_ab, d_sb, d_out, b_glu, b_ab, b_sb, b_out = outs[:13]
    r_ffi = outs[13:]
    glu4, out4 = (N_CHIPS, SSM_W // N_CHIPS, SSM_W), (N_CHIPS, D_MODEL // N_CHIPS, D_MODEL)
    d_mix = [d_glu.reshape(glu4), d_ab, d_sb, d_out.reshape(out4)]
    b_mix = [b_glu.reshape(glu4), b_ab, b_sb, b_out.reshape(out4)]
    outs = _ssm_bwd(ds, u, h, bmat.transpose(0, 2, 1), cmat.transpose(0, 2, 1), tab_b, d_skip, tb_ssm,
                    _scatter_chips([b_ffo]) if dist else None)
    du, d_bmat, d_cmat, da_acc, dd_skip = outs[:5]
    r_ffo = outs[5:]
    outs = _attn_bwd(q, k, v, datt, bias, sink_rows, _scatter_chips(b_mix) if dist else None)
    dq, dk, dv, dbias, dsink_rows = outs[:5]
    r_mix = outs[5:]
    if dist:
        p_ffi = _sum4("sum_w_ff_in", d_ffi, r_ffi[0], me)
        p_ffo = _sum4("sum_w_ff_out", d_ffo, r_ffo[0], me)
    dx, dpj, dg1 = _inproj_bwd(x, dx2, dq, dk, dv, du, dga, dgs, g1, w_in, tb)

    dab_re, dab_im = _state_unlayout(jnp.sum(da_acc, axis=0))
    dbb_re, dbb_im = _b_matrix_grad(d_bmat)
    d_lam_re, d_lam_im, d_log_dt, d_b_re, d_b_im = disc_vjp((dab_re, dab_im, dbb_re, dbb_im))
    d_c_re, d_c_im = _c_matrix_grad(d_cmat)
    d_rel = _bias_grad(_pair_unlayout(keys_first(dbias)), bucket)
    d_sinks = jnp.sum(_pair_unlayout(keys_first(dsink_rows)), axis=(1, 2))
    small_grads = dict(
        norm_mix_pre=dg1, norm_mix_post=dg2, norm_mlp_pre=dg3, norm_mlp_post=dg4, rel_bias=d_rel, sinks=d_sinks,
        lam_re=d_lam_re, lam_im=d_lam_im, log_dt=d_log_dt, b_re=d_b_re, b_im=d_b_im, c_re=d_c_re, c_im=d_c_im,
        d_skip=dd_skip)
    ride = _both(_swap_sibling([p_ffi, p_ffo]), _gather_devices(_pack(small_grads, loss_acc))) if dist else None
    outs = _matmul_tn("grad_w_in", dpj, h1, IN_W // 2, D_MODEL, tl, False, ride)
    in4 = (N_CHIPS, IN_W // N_CHIPS, D_MODEL)
    d_in, b_in = outs[0].reshape(in4), outs[1].reshape(in4)
    if not dist:
        return loss_acc, dx, small_grads, dict(zip(BIG, [d_in] + d_mix + [d_ffi, d_ffo]))
    s_ffi, s_ffo, slots = outs[2:]
    p_mix = [_sum4("sum_" + n, d, r, me) for n, d, r in zip(mix, d_mix, r_mix)]
    pending = dict(d_in=d_in, b_in=b_in, p_mix=p_mix, w_ff_in=(p_ffi, s_ffi), w_ff_out=(p_ffo, s_ffo), me=me)
    return loss_acc, dx, _sum_devices(slots), pending


SMALL = ['norm_mix_pre', 'norm_mix_post', 'norm_mlp_pre', 'norm_mlp_post', 'rel_bias', 'sinks', 'lam_re', 'lam_im',
         'log_dt', 'b_re', 'b_im', 'c_re', 'c_im', 'd_skip']
BIG = ['w_in', 'w_glu', 'w_attn_branch', 'w_ssm_branch', 'w_out', 'w_ff_in', 'w_ff_out']
WEIGHTS = ['norm_mix_pre', 'norm_mix_post', 'norm_mlp_pre', 'norm_mlp_post', 'w_in', 'rel_bias', 'sinks', 'lam_re',
           'lam_im', 'log_dt', 'b_re', 'b_im', 'c_re', 'c_im', 'd_skip', 'w_glu', 'w_attn_branch', 'w_ssm_branch',
           'w_out', 'w_ff_in', 'w_ff_out']
PACK_COLS = 1024
PACK_ORDER = ['b_re', 'b_im', 'c_re', 'c_im', 'lam_re', 'lam_im', 'norm_mix_pre', 'norm_mix_post', 'norm_mlp_pre',
              'norm_mlp_post', 'rel_bias', 'sinks', 'log_dt', 'd_skip']


STATE_MINOR = ('b_re', 'b_im')
PACK_ROWS = 144
LOSS_ROW = 140


def _pack(named, loss_acc):
    parts = []
    for n in PACK_ORDER:
        a = jnp.swapaxes(named[n], -1, -2) if n in STATE_MINOR else named[n]
        flat = a.reshape(-1)
        rows = -(-flat.shape[0] // PACK_COLS)
        parts.append(jnp.pad(flat, (0, rows * PACK_COLS - flat.shape[0])).reshape(rows, PACK_COLS))
    assert sum(p.shape[0] for p in parts) == LOSS_ROW
    parts.append(jnp.pad(loss_acc[0:1], ((0, PACK_ROWS - LOSS_ROW - 1), (0, PACK_COLS - loss_acc.shape[1]))))
    return jnp.concatenate(parts, axis=0)


def _unpack(packed, shapes):
    out, at = {}, 0
    for n in PACK_ORDER:
        shape = shapes[n][:-2] + (shapes[n][-1], shapes[n][-2]) if n in STATE_MINOR else shapes[n]
        size = int(np.prod(shape))
        rows = -(-size // PACK_COLS)
        blk = packed[at:at + rows]
        out[n] = (blk.reshape(-1)[:size] if size % PACK_COLS else blk).reshape(shape)
        at += rows
    return out


def kernel(x, norm_mix_pre, norm_mix_post, norm_mlp_pre, norm_mlp_post, w_in, rel_bias, sinks, lam_re, lam_im, log_dt, b_re, b_im, c_re, c_im, d_skip, w_glu, w_attn_branch, w_ssm_branch, w_out, w_ff_in, w_ff_out, loss_target, m_norm_mix_pre, m_norm_mix_post, m_norm_mlp_pre, m_norm_mlp_post, m_w_in, m_rel_bias, m_sinks, m_lam_re, m_lam_im, m_log_dt, m_b_re, m_b_im, m_c_re, m_c_im, m_d_skip, m_w_glu, m_w_attn_branch, m_w_ssm_branch, m_w_out, m_w_ff_in, m_w_ff_out, v_norm_mix_pre, v_norm_mix_post, v_norm_mlp_pre, v_norm_mlp_post, v_w_in, v_rel_bias, v_sinks, v_lam_re, v_lam_im, v_log_dt, v_b_re, v_b_im, v_c_re, v_c_im, v_d_skip, v_w_glu, v_w_attn_branch, v_w_ssm_branch, v_w_out, v_w_ff_in, v_w_ff_out):
    env = dict(locals())
    w = {n: env[n] for n in WEIGHTS}
    m = {n: env["m_" + n] for n in WEIGHTS}
    v = {n: env["v_" + n] for n in WEIGHTS}
    seq = x.shape[1]
    tb = min(512, seq)

    small = {n: w[n] for n in ('norm_mix_pre', 'norm_mix_post', 'norm_mlp_pre', 'norm_mlp_post', 'rel_bias')}
    small.update({n: w[n][0] for n in ('sinks', 'lam_re', 'lam_im', 'log_dt', 'b_re', 'b_im', 'c_re', 'c_im')})
    small['d_skip'] = w['d_skip']
    shard = lambda t, n: t[n][0].T if n == 'w_in' else t[n][0]
    unshard = lambda a, n: (a.T if n == 'w_in' else a)[None]
    _, dx, small_g, pending = _local_step(
        x[0], loss_target[0], small, {n: _bf(shard(w, n)) for n in BIG}, tb, True)

    loss = small_g[LOSS_ROW, 0]

    grads, deltas, new_m, new_v = {}, {}, {}, {}

    def adam(name, names, partials, exchange=None):
        items = [(shard(w, n), *partials[n], shard(m, n), shard(v, n)) for n in names]
        outs = _adam_pair(name, items, exchange)
        for n, res in zip(names, outs):
            grads[n], deltas[n], new_m[n], new_v[n] = [unshard(a, n) for a in res[:4]]
        return outs[len(names):]

    (r_in,), = adam("adam_w_ff", ("w_ff_in", "w_ff_out"), pending, _scatter_chips([pending["b_in"]]))
    mix = ("w_glu", "w_attn_branch", "w_ssm_branch", "w_out")
    parts = [_sum4("sum_w_in", pending["d_in"], r_in, pending["me"])] + pending["p_mix"]
    sibs = _exchange_alone("swap_rest", _swap_sibling(parts))
    partials = dict(zip(("w_in",) + mix, zip(parts, sibs)))
    for n in ("w_in",) + mix:
        adam("adam_" + n, (n,), partials)

    minor = lambda t, n: jnp.swapaxes(t, -1, -2) if n in STATE_MINOR else t
    g_small = _unpack(small_g, {n: w[n].shape for n in SMALL})
    outs = _adam_small([minor(w[n], n) for n in SMALL], [g_small[n] for n in SMALL],
                       [minor(m[n], n) for n in SMALL], [minor(v[n], n) for n in SMALL])
    grads.update({n: minor(g_small[n], n) for n in SMALL})
    for k, dst in enumerate((deltas, new_m, new_v)):
        dst.update({n: minor(a, n) for n, a in zip(SMALL, outs[k * len(SMALL):(k + 1) * len(SMALL)])})

    return (loss, dx[None], *[grads[n] for n in WEIGHTS], *[deltas[n] for n in WEIGHTS],
            *[new_m[n] for n in WEIGHTS], *[new_v[n] for n in WEIGHTS])
```

```python
import functools
import math

import numpy as np
import jax
import jax.numpy as jnp
from jax import lax
from jax.experimental import pallas as pl
from jax.experimental.pallas import tpu as pltpu

F32 = jnp.float32
BF16 = jnp.bfloat16

D_MODEL = 1024
N_HEADS = 8
N_KV = 2
Q_GROUP = 4
HEAD_DIM = 64
ATTN_W = 512
KV_W = 128
BLOCK = 128
N_BUCKETS = 32
MAX_DISTANCE = 128
NEG_INF = -1e30
SSM_W = 512
SSM_GROUP = 16
SSM_GROUPS = 32
SSM_STATE = 64
N_SUPER = 4
GROUPS_PER_SUPER = SSM_GROUPS // N_SUPER
SUPER_IN = GROUPS_PER_SUPER * SSM_GROUP
SUPER_HALF = GROUPS_PER_SUPER * SSM_STATE
SUPER_W = 2 * SUPER_HALF
STATE_COLS = N_SUPER * SUPER_W
D_FF = 4096
FF_CHUNKS = 4
IN_W = 3328
SPLITS = (0, 512, 640, 768, 1280, 2304, 3328)
RMS_EPS = 1e-6
N_CHIPS = 4
SUBLANES = 8
LANES = 128
STATE_TILES = STATE_COLS // LANES
SUPER_TILES = SUPER_W // LANES

ADAM_LR = 0.001
ADAM_B1 = 0.9
ADAM_B2 = 0.999
ADAM_EPS = 1e-08
ADAM_WD = 0.01
ADAM_STEP = 10

VMEM_BIG = 56 * 1024 * 1024
SDS = jax.ShapeDtypeStruct
MESH_ID = pl.DeviceIdType.MESH
ANY = pl.BlockSpec(memory_space=pl.ANY)


def _bf(x):
    return x.astype(BF16)


def _mm(a, b):
    return jnp.dot(a, b, preferred_element_type=F32)


def _mm_nt(a, b):
    return lax.dot_general(a, b, (((1,), (1,)), ((), ())), preferred_element_type=F32)


def _mm_tn(a, b):
    return lax.dot_general(a, b, (((0,), (0,)), ((), ())), preferred_element_type=F32)


def _sig(x):
    return 1.0 / (1.0 + jnp.exp(-x))


def _rms(x, g):
    r = lax.rsqrt(jnp.mean(x * x, axis=-1, keepdims=True) + RMS_EPS)
    xh = x * r
    return xh * g, xh, r


def _rms_bwd(dout, xh, r, g):
    dg = jnp.sum(dout * xh, axis=0, keepdims=True)
    dxh = dout * g
    dx = r * (dxh - xh * jnp.mean(dxh * xh, axis=-1, keepdims=True))
    return dx, dg


_GELU_C = math.sqrt(2.0 / math.pi)


def _gelu_and_grad(x):
    x2 = x * x
    inner = _GELU_C * (x + 0.044715 * (x2 * x))
    t = jnp.tanh(inner)
    y = 0.5 * x * (1.0 + t)
    dy = 0.5 * (1.0 + t) + 0.5 * x * (1.0 - t * t) * (_GELU_C * (1.0 + 3.0 * 0.044715 * x2))
    return y, dy


def _zero_map(nd, *_):
    return (0,) * nd


def _params(n_axes, vmem=None):
    return pltpu.CompilerParams(dimension_semantics=("arbitrary",) * n_axes, vmem_limit_bytes=vmem)


class _Exchange:
    def __init__(self, ins, outs, sems, start, wait):
        self.ins, self.outs, self.sems, self.start, self.wait = list(ins), list(outs), list(sems), start, wait


def _fused_call(name, body, grid, in_specs, out_specs, out_shape, scratch, args, exchange, params):
    n_in, n_out, n_scr = len(in_specs), len(out_specs), len(scratch)
    if exchange is None:
        fn = body
    else:
        ex = exchange
        n_xi, n_xo = len(ex.ins), len(ex.outs)

        def fn(*refs):
            at = 0
            parts = []
            for n in (n_in, n_xi, n_out, n_xo, n_scr, len(ex.sems)):
                parts.append(refs[at:at + n])
                at += n
            ins, x_in, outs, x_out, scr, x_sem = parts
            ids = [pl.program_id(a) for a in range(len(grid))]
            first = functools.reduce(jnp.logical_and, [i == 0 for i in ids])
            last = functools.reduce(jnp.logical_and, [i == g - 1 for i, g in zip(ids, grid)])

            @pl.when(first)
            def _():
                ex.start(x_in, x_out, x_sem)

            body(*ins, *outs, *scr)

            @pl.when(last)
            def _():
                ex.wait(x_in, x_out, x_sem)

        in_specs = list(in_specs) + [ANY] * n_xi
        out_specs = list(out_specs) + [ANY] * n_xo
        out_shape = list(out_shape) + ex.outs
        scratch = list(scratch) + ex.sems
        args = list(args) + ex.ins
    return pl.pallas_call(fn, grid=grid, in_specs=in_specs, out_specs=out_specs, out_shape=out_shape,
                          scratch_shapes=list(scratch), name=name, compiler_params=params)(*args)


def _exchange_alone(name, ex):
    def body(*refs):
        n_xi, n_xo = len(ex.ins), len(ex.outs)
        x_in, x_out, x_sem = refs[:n_xi], refs[n_xi:n_xi + n_xo], refs[n_xi + n_xo:]
        ex.start(x_in, x_out, x_sem)
        ex.wait(x_in, x_out, x_sem)

    return pl.pallas_call(body, in_specs=[ANY] * len(ex.ins), out_specs=[ANY] * len(ex.outs), out_shape=ex.outs,
                          scratch_shapes=ex.sems, name=name)(*ex.ins)


def _rowcall(name, body, seq, tb, rows, consts, row_outs, acc_outs, scratch=(), reverse=False, vmem=None,
             exchange=None):
    nb = seq // tb
    rmap = (lambda i: (nb - 1 - i, 0)) if reverse else (lambda i: (i, 0))
    tmap = lambda i: (0,) + rmap(i)

    def row_spec(width):
        if isinstance(width, tuple):
            return pl.BlockSpec((width[0], tb, width[1]), tmap)
        return pl.BlockSpec((tb, width), rmap)

    def row_shape(width):
        return (width[0], seq, width[1]) if isinstance(width, tuple) else (seq, width)

    in_specs = [row_spec(a.shape[1] if a.ndim == 2 else (a.shape[0], a.shape[2])) for a in rows]
    in_specs += [pl.BlockSpec(a.shape, functools.partial(_zero_map, a.ndim), pipeline_mode=pl.Buffered(1))
                 for a in consts]
    out_specs = [row_spec(c) for c, _ in row_outs] + [ANY] * len(acc_outs)
    out_shape = [SDS(row_shape(c), dt) for c, dt in row_outs] + [SDS(s, dt) for s, dt in acc_outs]
    n_main = len(rows) + len(consts) + len(row_outs)
    n_acc = len(acc_outs)

    def fn(*refs):
        main, acc_hbm, rest = refs[:n_main], refs[n_main:n_main + n_acc], refs[n_main + n_acc:]
        acc_vmem, own = rest[:n_acc], rest[n_acc:]
        body(*main, *acc_vmem, *own)

        @pl.when(pl.program_id(0) == nb - 1)
        def _():
            for src, dst in zip(acc_vmem, acc_hbm):
                pltpu.sync_copy(src, dst)

    buffers = [pltpu.VMEM(s, dt) for s, dt in acc_outs] + list(scratch)
    return _fused_call(name, fn if acc_outs else body, (nb,), in_specs, out_specs, out_shape, buffers,
                       [*rows, *consts], exchange, _params(1, vmem))


def _inproj_fwd(x, g1, w_in, tb, exchange=None):
    seq = x.shape[0]

    def body(x_ref, g_ref, w_ref, h_ref, q_ref, k_ref, v_ref, u_ref, ga_ref, gs_ref):
        h, _, _ = _rms(x_ref[...], g_ref[...])
        hb = _bf(h)
        h_ref[...] = hb
        pj = _mm_nt(hb, w_ref[...])
        q_ref[...] = _bf(pj[:, SPLITS[0]:SPLITS[1]])
        k_ref[...] = _bf(pj[:, SPLITS[1]:SPLITS[2]])
        v_ref[...] = _bf(pj[:, SPLITS[2]:SPLITS[3]])
        u_ref[...] = pj[:, SPLITS[3]:SPLITS[4]]
        ga_ref[...] = pj[:, SPLITS[4]:SPLITS[5]]
        gs_ref[...] = pj[:, SPLITS[5]:SPLITS[6]]

    return _rowcall("inproj_fwd", body, seq, tb, [x], [g1, w_in],
                    [(D_MODEL, BF16), (ATTN_W, BF16), (KV_W, BF16), (KV_W, BF16), (SSM_W, F32),
                     (D_MODEL, F32), (D_MODEL, F32)], [], vmem=VMEM_BIG, exchange=exchange)


def _inproj_bwd(x, dx2, dq, dk, dv, du, dga, dgs, g1, w_in, tb, exchange=None):
    seq = x.shape[0]

    def body(x_ref, dx2_ref, dq_ref, dk_ref, dv_ref, du_ref, dga_ref, dgs_ref, g_ref, w_ref,
             dx_ref, dpj_ref, dg_ref):
        @pl.when(pl.program_id(0) == 0)
        def _():
            dg_ref[...] = jnp.zeros_like(dg_ref)

        dpj = jnp.concatenate([dq_ref[...], dk_ref[...], dv_ref[...], _bf(du_ref[...]),
                               dga_ref[...], dgs_ref[...]], axis=1)
        dpj_ref[...] = dpj
        dh = _mm(dpj, w_ref[...])
        g = g_ref[...]
        _, xh, r = _rms(x_ref[...], g)
        dxn, dg = _rms_bwd(dh, xh, r, g)
        dx_ref[...] = dx2_ref[...] + dxn
        dg_ref[...] += dg

    return _rowcall("inproj_bwd", body, seq, tb, [x, dx2, dq, dk, dv, du, dga, dgs], [g1, w_in],
                    [(D_MODEL, F32), (IN_W, BF16)], [((1, D_MODEL), F32)], vmem=VMEM_BIG, exchange=exchange)


def _bucket_table():
    qi = np.arange(BLOCK)[:, None]
    kj = np.arange(2 * BLOCK)[None, :]
    dist = qi + BLOCK - kj
    max_exact = N_BUCKETS // 2
    d = np.maximum(dist, 0)
    df = np.maximum(d, 1).astype(np.float32)
    large = max_exact + (np.log(df / np.float32(max_exact)) / np.float32(math.log(MAX_DISTANCE / max_exact))
                         * np.float32(N_BUCKETS - max_exact)).astype(np.int32)
    large = np.minimum(large, N_BUCKETS - 1)
    bucket = np.where(d < max_exact, d, large)
    valid = (dist >= 0) & (dist < BLOCK)
    return np.where(valid, bucket, -1).astype(np.int32)


def _bias_table(rel_bias, bucket):
    def body(rb_ref, bk_ref, o_ref):
        bk = bk_ref[...]
        has_prev = lax.broadcasted_iota(jnp.int32, bk.shape, 1) >= BLOCK
        for h in range(N_HEADS):
            acc = jnp.full((BLOCK, 2 * BLOCK), NEG_INF, F32)
            for b in range(N_BUCKETS):
                acc = jnp.where(bk == b, rb_ref[b, h], acc)
            o_ref[0, h] = jnp.where(has_prev, acc, NEG_INF)
            o_ref[1, h] = acc

    return pl.pallas_call(
        body, out_shape=SDS((2, N_HEADS, BLOCK, 2 * BLOCK), F32),
        in_specs=[pl.BlockSpec(memory_space=pltpu.SMEM), pl.BlockSpec(memory_space=pltpu.VMEM)],
        out_specs=pl.BlockSpec(memory_space=pltpu.VMEM), name="bias_table",
    )(rel_bias, bucket)


def _bias_grad(dbias, bucket):
    def body(db_ref, bk_ref, o_ref):
        bk = bk_ref[...]
        for h in range(N_HEADS):
            db = db_ref[h]
            for b in range(N_BUCKETS):
                o_ref[b, h] = jnp.sum(jnp.where(bk == b, db, 0.0))

    return pl.pallas_call(
        body, out_shape=SDS((N_BUCKETS, N_HEADS), F32),
        in_specs=[pl.BlockSpec(memory_space=pltpu.VMEM), pl.BlockSpec(memory_space=pltpu.VMEM)],
        out_specs=pl.BlockSpec(memory_space=pltpu.SMEM), name="bias_grad",
    )(dbias, bucket)


TILE = 2 * HEAD_DIM


def _pair_layout(t):
    lead = t.shape[:-3]
    t = t.reshape(lead + (N_KV, 2, 2) + t.shape[-2:])
    nl = len(lead)
    t = jnp.transpose(t, tuple(range(nl)) + (nl, nl + 2, nl + 1, nl + 3, nl + 4))
    return t.reshape(lead + (N_KV, 2, 2 * BLOCK, t.shape[-1]))


def _pair_unlayout(t):
    t = t.reshape(N_KV, 2, 2, BLOCK, t.shape[-1]).transpose(0, 2, 1, 3, 4)
    return t.reshape(N_HEADS, BLOCK, t.shape[-1])


def _halves(t):
    tf = t.astype(F32)
    low = lax.broadcasted_iota(jnp.int32, tf.shape, 1) < HEAD_DIM
    swapped = pltpu.roll(tf, HEAD_DIM, 1)
    zero = jnp.zeros_like(tf)
    return ((_bf(jnp.where(low, tf, zero)), _bf(jnp.where(low, zero, swapped))),
            (_bf(jnp.where(low, swapped, zero)), _bf(jnp.where(low, zero, tf))))


def _fold_halves(even, odd):
    low = lax.broadcasted_iota(jnp.int32, even.shape, 1) < HEAD_DIM
    comb = jnp.where(low, even, odd)
    return comb + pltpu.roll(comb, HEAD_DIM, 1)


def _tile_rows(ref, kh):
    return jnp.concatenate([ref[:, (2 * kh) * TILE:(2 * kh + 1) * TILE],
                            ref[:, (2 * kh + 1) * TILE:(2 * kh + 2) * TILE]], axis=0)


def _halves_t(t):
    tt = t.astype(F32).T
    top = lax.broadcasted_iota(jnp.int32, tt.shape, 0) < HEAD_DIM
    swapped = jnp.concatenate([tt[HEAD_DIM:], tt[:HEAD_DIM]], axis=0)
    zero = jnp.zeros_like(tt)
    return ((_bf(jnp.where(top, tt, zero)), _bf(jnp.where(top, zero, swapped))),
            (_bf(jnp.where(top, swapped, zero)), _bf(jnp.where(top, zero, tt))))


def _attn_probs(km, qk, bias, sink):
    lg = _mm_nt(km, qk) * (HEAD_DIM ** -0.5) + bias
    m = jnp.maximum(jnp.max(lg, axis=0, keepdims=True), sink)
    p = jnp.exp(lg - m)
    es = jnp.exp(sink - m)
    inv = 1.0 / (jnp.sum(p, axis=0, keepdims=True) + es)
    return p * inv, es * inv


def _attn_fwd(q, k, v, bias, sink_rows, exchange=None):
    seq = q.shape[0]
    nblk = seq // BLOCK

    def body(q_ref, kp_ref, kc_ref, vp_ref, vc_ref, b_ref, s_ref, o_ref):
        which = jnp.minimum(pl.program_id(0), 1)
        kms = _halves(jnp.concatenate([kp_ref[...], kc_ref[...]], axis=0))
        vts = _halves_t(jnp.concatenate([vp_ref[...], vc_ref[...]], axis=0))
        for kh in range(N_KV):
            qk = _tile_rows(q_ref, kh)
            acc = jnp.zeros((TILE, 2 * BLOCK), F32)
            for par in range(2):
                pr, _ = _attn_probs(kms[kh][par], qk, b_ref[which, kh, par], s_ref[kh, par])
                acc = acc + _mm(vts[kh][par], _bf(pr))
            acc = acc.T
            o_ref[:, (2 * kh) * TILE:(2 * kh + 1) * TILE] = _bf(acc[:BLOCK])
            o_ref[:, (2 * kh + 1) * TILE:(2 * kh + 2) * TILE] = _bf(acc[BLOCK:])

    cur = lambda n: (n, 0)
    prev = lambda n: (jnp.maximum(n - 1, 0), 0)
    return _fused_call(
        "attn_fwd", body, (nblk,),
        [pl.BlockSpec((BLOCK, ATTN_W), cur),
         pl.BlockSpec((BLOCK, KV_W), prev), pl.BlockSpec((BLOCK, KV_W), cur),
         pl.BlockSpec((BLOCK, KV_W), prev), pl.BlockSpec((BLOCK, KV_W), cur),
         pl.BlockSpec(bias.shape, functools.partial(_zero_map, bias.ndim)),
         pl.BlockSpec(sink_rows.shape, functools.partial(_zero_map, sink_rows.ndim))],
        [pl.BlockSpec((BLOCK, ATTN_W), cur)], [SDS((seq, ATTN_W), BF16)], [],
        [q, k, k, v, v, bias, sink_rows], exchange, _params(1))


def _attn_bwd(q, k, v, d_out, bias, sink_rows, exchange=None):
    seq = q.shape[0]
    nblk = seq // BLOCK

    def body(q_ref, kp_ref, kc_ref, vp_ref, vc_ref, do_ref, b_ref, s_ref,
             dq_ref, dk_ref, dv_ref, db_ref, ds_ref, ck_ref, cv_ref):
        n = pl.program_id(0)

        @pl.when(n == 0)
        def _():
            db_ref[...] = jnp.zeros_like(db_ref)
            ds_ref[...] = jnp.zeros_like(ds_ref)
            ck_ref[...] = jnp.zeros_like(ck_ref)
            cv_ref[...] = jnp.zeros_like(cv_ref)

        @pl.when(n < nblk)
        def _():
            which = jnp.minimum(n, 1)
            scale = HEAD_DIM ** -0.5
            kcat = jnp.concatenate([kp_ref[...], kc_ref[...]], axis=0)
            kms = _halves(kcat)
            kts = _halves_t(kcat)
            vms = _halves(jnp.concatenate([vp_ref[...], vc_ref[...]], axis=0))
            dks, dvs = [], []
            for kh in range(N_KV):
                qk = _tile_rows(q_ref, kh)
                dok = _tile_rows(do_ref, kh)
                dq = jnp.zeros((TILE, 2 * BLOCK), F32)
                dkp, dvp = [], []
                for par in range(2):
                    pr, ps = _attn_probs(kms[kh][par], qk, b_ref[which, kh, par], s_ref[kh, par])
                    dp = _mm_nt(vms[kh][par], dok)
                    rs = jnp.sum(pr * dp, axis=0, keepdims=True)
                    dlg = pr * (dp - rs)
                    ds_ref[kh, par] += -ps * rs
                    db_ref[kh, par] += dlg
                    dlb = _bf(dlg)
                    dq = dq + _mm(kts[kh][par], dlb)
                    dkp.append(_mm(dlb, qk))
                    dvp.append(_mm(_bf(pr), dok))
                dq = _bf((dq * scale).T)
                dq_ref[:, (2 * kh) * TILE:(2 * kh + 1) * TILE] = dq[:BLOCK]
                dq_ref[:, (2 * kh + 1) * TILE:(2 * kh + 2) * TILE] = dq[BLOCK:]
                dks.append(_fold_halves(*dkp))
                dvs.append(_fold_halves(*dvp))
            low = lax.broadcasted_iota(jnp.int32, (2 * BLOCK, TILE), 1) < HEAD_DIM
            dkk = jnp.where(low, dks[0], dks[1]) * scale
            dvv = jnp.where(low, dvs[0], dvs[1])
            dk_ref[...] = _bf(ck_ref[...] + dkk[:BLOCK])
            ck_ref[...] = dkk[BLOCK:]
            dv_ref[...] = _bf(cv_ref[...] + dvv[:BLOCK])
            cv_ref[...] = dvv[BLOCK:]

        @pl.when(n == nblk)
        def _():
            dk_ref[...] = _bf(ck_ref[...])
            dv_ref[...] = _bf(cv_ref[...])

    cur = lambda n: (jnp.minimum(n, nblk - 1), 0)
    prev = lambda n: (jnp.maximum(jnp.minimum(n, nblk - 1) - 1, 0), 0)
    late = lambda n: (jnp.maximum(n - 1, 0), 0)
    kv_spec = lambda m: pl.BlockSpec((BLOCK, KV_W), m)
    acc_b = pl.BlockSpec(bias.shape[1:], functools.partial(_zero_map, bias.ndim - 1))
    acc_s = pl.BlockSpec(sink_rows.shape, functools.partial(_zero_map, sink_rows.ndim))
    return _fused_call(
        "attn_bwd", body, (nblk + 1,),
        [pl.BlockSpec((BLOCK, ATTN_W), cur), kv_spec(prev), kv_spec(cur), kv_spec(prev), kv_spec(cur),
         pl.BlockSpec((BLOCK, ATTN_W), cur),
         pl.BlockSpec(bias.shape, functools.partial(_zero_map, bias.ndim)), acc_s],
        [pl.BlockSpec((BLOCK, ATTN_W), cur), kv_spec(late), kv_spec(late), acc_b, acc_s],
        [SDS((seq, ATTN_W), BF16), SDS((seq, KV_W), BF16), SDS((seq, KV_W), BF16),
         SDS(bias.shape[1:], F32), SDS(sink_rows.shape, F32)],
        [pltpu.VMEM((BLOCK, KV_W), F32), pltpu.VMEM((BLOCK, KV_W), F32)],
        [q, k, k, v, v, d_out, bias, sink_rows], exchange, _params(1))


def _ssm_discretize(lam_re, lam_im, log_dt, b_re, b_im):
    dt = jnp.exp(log_dt)[:, None]
    mag = jnp.exp(lam_re * dt)
    ab_re = mag * jnp.cos(lam_im * dt)
    ab_im = mag * jnp.sin(lam_im * dt)
    nr = ab_re - 1.0
    den = lam_re * lam_re + lam_im * lam_im
    f_re = (nr * lam_re + ab_im * lam_im) / den
    f_im = (ab_im * lam_re - nr * lam_im) / den
    bb_re = f_re[..., None] * b_re - f_im[..., None] * b_im
    bb_im = f_re[..., None] * b_im + f_im[..., None] * b_re
    return ab_re, ab_im, bb_re, bb_im


def _state_layout(re, im):
    z = jnp.stack([re, im]).reshape(2, N_SUPER, GROUPS_PER_SUPER, SSM_STATE)
    return z.transpose(1, 0, 2, 3).reshape(STATE_COLS)


def _state_unlayout(vec):
    z = vec.reshape(N_SUPER, 2, GROUPS_PER_SUPER, SSM_STATE).transpose(1, 0, 2, 3)
    z = z.reshape(2, SSM_GROUPS, SSM_STATE)
    return z[0], z[1]


SEG = 4
WINDOW = SEG * SUBLANES


def _scan_tables(ab_re, ab_im):
    pw = [None, (ab_re, ab_im)]
    for _ in range(2, WINDOW + 1):
        pr, pi_ = pw[-1]
        pw.append((pr * ab_re - pi_ * ab_im, pr * ab_im + pi_ * ab_re))
    rows = np.arange(SUBLANES)[:, None]
    ones = np.ones((SUBLANES, 1), np.float32)
    conj = lambda p: (p[0], -p[1])
    fwd, bwd = [], []
    for shift in (1, 2, 4):
        fwd.append(_state_layout(*pw[SEG * shift])[None, :] * (rows >= shift).astype(np.float32))
        bwd.append(_state_layout(*conj(pw[SEG * shift]))[None, :] * (rows < SUBLANES - shift).astype(np.float32))
    fwd.append(jnp.stack([_state_layout(*pw[SEG * (r + 1)]) for r in range(SUBLANES)]))
    bwd.append(jnp.stack([_state_layout(*conj(pw[SEG * (SUBLANES - r)])) for r in range(SUBLANES)]))
    for k in range(1, SEG):
        fwd.append(_state_layout(*pw[k])[None, :] * ones)
        bwd.append(_state_layout(*conj(pw[k]))[None, :] * ones)
    return jnp.stack(fwd), jnp.stack(bwd)


_EYE = np.eye(GROUPS_PER_SUPER, dtype=np.float32)


def _b_matrix(bb_re, bb_im):
    bb = jnp.stack([bb_re, bb_im]).reshape(2, N_SUPER, GROUPS_PER_SUPER, SSM_STATE, SSM_GROUP)
    m = jnp.einsum('rsgpc,gh->sgcrhp', bb, _EYE)
    return m.reshape(N_SUPER, SUPER_IN, SUPER_W)


def _b_matrix_grad(dm):
    d = dm.reshape(N_SUPER, GROUPS_PER_SUPER, SSM_GROUP, 2, GROUPS_PER_SUPER, SSM_STATE)
    d = jnp.sum(d * _EYE[None, :, None, None, :, None], axis=4)
    d = d.transpose(3, 0, 1, 4, 2).reshape(2, SSM_GROUPS, SSM_STATE, SSM_GROUP)
    return d[0], d[1]


def _c_matrix(c_re, c_im):
    cc = jnp.stack([c_re, -c_im]).reshape(2, N_SUPER, GROUPS_PER_SUPER, SSM_GROUP, SSM_STATE)
    m = jnp.einsum('rsgcp,gh->srgphc', cc, _EYE)
    return m.reshape(N_SUPER, SUPER_W, SUPER_IN)


def _c_matrix_grad(dm):
    d = dm.reshape(N_SUPER, 2, GROUPS_PER_SUPER, SSM_STATE, GROUPS_PER_SUPER, SSM_GROUP)
    d = jnp.sum(d * _EYE[None, None, :, None, :, None], axis=4)
    d = d.transpose(1, 0, 2, 4, 3).reshape(2, SSM_GROUPS, SSM_GROUP, SSM_STATE)
    return d[0], -d[1]


def _cmul_add(xr, xi, ar, ai, sr, si):
    return xr + ar * sr - ai * si, xi + ar * si + ai * sr


def _scan_rows(buf_ref, tab_ref, carry_ref, n_windows, reverse, h_ref=None, da_ref=None):
    order = list(range(SEG - 1, -1, -1)) if reverse else list(range(SEG))
    near = SUBLANES - 1 if reverse else 0
    far = 0 if reverse else SUBLANES - 1
    s_in = SUBLANES - 1 if reverse else 1
    lanes = lambda tile: pl.ds(tile * LANES, LANES)

    def window(w0, tile_re, tile_im, c_re, c_im, acc):
        rows = lambda t: pl.ds(w0 + t, SUBLANES, stride=SEG)
        get = lambda ref, t: (ref.at[tile_re][rows(t), :], ref.at[tile_im][rows(t), :])
        tab = lambda k: (tab_ref[k, :, lanes(tile_re)], tab_ref[k, :, lanes(tile_im)])

        def put(t, xr, xi):
            buf_ref.at[tile_re][rows(t), :] = xr
            buf_ref.at[tile_im][rows(t), :] = xi

        a1 = tab(4)
        er, ei = get(buf_ref, order[0])
        for t in order[1:]:
            er, ei = _cmul_add(*get(buf_ref, t), *a1, er, ei)
            if t != order[-1]:
                put(t, er, ei)
        for k, shift in enumerate((1, 2, 4)):
            s = (SUBLANES - shift) if reverse else shift
            er, ei = _cmul_add(er, ei, *tab(k), pltpu.roll(er, s, 0), pltpu.roll(ei, s, 0))
        er, ei = _cmul_add(er, ei, *tab(3), c_re, c_im)
        put(order[-1], er, ei)
        sub = lax.broadcasted_iota(jnp.int32, er.shape, 0)
        in_re = jnp.where(sub == near, c_re, pltpu.roll(er, s_in, 0))
        in_im = jnp.where(sub == near, c_im, pltpu.roll(ei, s_in, 0))
        true = {order[-1]: (er, ei)}
        for idx, t in enumerate(order[:-1]):
            true[t] = _cmul_add(*get(buf_ref, t), *tab(4 + idx), in_re, in_im)
            put(t, *true[t])
        carry = (jnp.broadcast_to(er[far:far + 1], er.shape), jnp.broadcast_to(ei[far:far + 1], ei.shape))
        if acc is None:
            return carry, None
        acc_re, acc_im = acc
        for t in range(SEG):
            if t + 1 < SEG:
                gr, gim = true[t + 1]
            else:
                gr = jnp.where(sub == SUBLANES - 1, c_re, pltpu.roll(true[0][0], SUBLANES - 1, 0))
                gim = jnp.where(sub == SUBLANES - 1, c_im, pltpu.roll(true[0][1], SUBLANES - 1, 0))
            hr, hi = get(h_ref, t)
            acc_re = acc_re + gr * hr + gim * hi
            acc_im = acc_im + gim * hr - gr * hi
        return carry, (acc_re, acc_im)

    half = SUPER_HALF // LANES
    per = 2 if h_ref is None else 4
    for sb in range(N_SUPER):
        pairs = [(2 * half * sb + j, 2 * half * sb + half + j) for j in range(half)]

        def step(wi, state, pairs=pairs):
            w = (n_windows - 1 - wi) if reverse else wi
            w0 = pl.multiple_of(w * WINDOW, WINDOW)
            out = []
            for j, (tile_re, tile_im) in enumerate(pairs):
                mine = state[per * j:per * (j + 1)]
                carry, acc = window(w0, tile_re, tile_im, mine[0], mine[1], mine[2:] or None)
                out += list(carry) + list(acc or ())
            return tuple(out)

        init = []
        for tile_re, tile_im in pairs:
            init += [carry_ref[:, lanes(tile_re)], carry_ref[:, lanes(tile_im)]]
            if h_ref is not None:
                init += [da_ref[:, lanes(tile_re)], da_ref[:, lanes(tile_im)]]
        fin = lax.fori_loop(0, n_windows, step, tuple(init))
        for j, (tile_re, tile_im) in enumerate(pairs):
            carry_ref[:, lanes(tile_re)] = fin[per * j]
            carry_ref[:, lanes(tile_im)] = fin[per * j + 1]
            if h_ref is not None:
                da_ref[:, lanes(tile_re)] = fin[per * j + 2]
                da_ref[:, lanes(tile_im)] = fin[per * j + 3]


def _put_tiles(ref, sb, block):
    for j in range(SUPER_TILES):
        ref[sb * SUPER_TILES + j] = block[:, j * LANES:(j + 1) * LANES]


def _get_tiles(ref, sb):
    return jnp.concatenate([ref[sb * SUPER_TILES + j] for j in range(SUPER_TILES)], axis=1)


def _ssm_fwd(u, bmat, cmat, tab, d_skip, tb, exchange=None):
    seq = u.shape[0]

    def body(u_ref, b_ref, c_ref, t_ref, d_ref, s_ref, h_ref, carry_ref):
        @pl.when(pl.program_id(0) == 0)
        def _():
            carry_ref[...] = jnp.zeros_like(carry_ref)

        u_blk = u_ref[...]
        ub = _bf(u_blk)
        for sb in range(N_SUPER):
            _put_tiles(h_ref, sb, _mm(ub[:, sb * SUPER_IN:(sb + 1) * SUPER_IN], b_ref[sb]))
        _scan_rows(h_ref, t_ref, carry_ref, tb // WINDOW, False)
        ys = [_mm(_bf(_get_tiles(h_ref, sb)), c_ref[sb]) for sb in range(N_SUPER)]
        s_ref[...] = jnp.concatenate(ys, axis=1) + d_ref[...] * u_blk

    return _rowcall("ssm_fwd", body, seq, tb, [u], [bmat, cmat, tab, d_skip],
                    [(SSM_W, F32), ((STATE_TILES, LANES), F32)], [],
                    scratch=[pltpu.VMEM((SUBLANES, STATE_COLS), F32)], vmem=VMEM_BIG, exchange=exchange)


def _ssm_bwd(ds, u, h, bmat_t, cmat_t, tab, d_skip, tb, exchange=None):
    seq = u.shape[0]

    def body(ds_ref, u_ref, h_ref, bt_ref, ct_ref, t_ref, d_ref,
             du_ref, db_ref, dc_ref, da_ref, dd_ref, g_ref, carry_ref):
        @pl.when(pl.program_id(0) == 0)
        def _():
            carry_ref[...] = jnp.zeros_like(carry_ref)
            db_ref[...] = jnp.zeros_like(db_ref)
            dc_ref[...] = jnp.zeros_like(dc_ref)
            da_ref[...] = jnp.zeros_like(da_ref)
            dd_ref[...] = jnp.zeros_like(dd_ref)

        ds_blk = ds_ref[...]
        dsb = _bf(ds_blk)
        u_blk = u_ref[...]
        ub = _bf(u_blk)
        for sb in range(N_SUPER):
            _put_tiles(g_ref, sb, _mm(dsb[:, sb * SUPER_IN:(sb + 1) * SUPER_IN], ct_ref[sb]))
        _scan_rows(g_ref, t_ref, carry_ref, tb // WINDOW, True, h_ref=h_ref, da_ref=da_ref)
        dus = []
        for sb in range(N_SUPER):
            gb = _bf(_get_tiles(g_ref, sb))
            dus.append(_mm(gb, bt_ref[sb]))
            db_ref[sb] += _mm_tn(ub[:, sb * SUPER_IN:(sb + 1) * SUPER_IN], gb)
            dc_ref[sb] += _mm_tn(_bf(_get_tiles(h_ref, sb)), dsb[:, sb * SUPER_IN:(sb + 1) * SUPER_IN])
        du_ref[...] = jnp.concatenate(dus, axis=1) + d_ref[...] * ds_blk
        dd_ref[...] += jnp.sum(ds_blk * u_blk, axis=0, keepdims=True)

    return _rowcall("ssm_bwd", body, seq, tb, [ds, u, h], [bmat_t, cmat_t, tab, d_skip],
                    [(SSM_W, F32)],
                    [((N_SUPER, SUPER_IN, SUPER_W), F32), ((N_SUPER, SUPER_W, SUPER_IN), F32),
                     ((SUBLANES, STATE_COLS), F32), ((1, SSM_W), F32)],
                    scratch=[pltpu.VMEM((STATE_TILES, tb, LANES), F32), pltpu.VMEM((SUBLANES, STATE_COLS), F32)],
                    reverse=True, vmem=VMEM_BIG, exchange=exchange)


def _merge_core(s, attb, ga, gs, wg_ref, wab_ref, wsb_ref, wout_ref):
    zg, dgelu = _gelu_and_grad(s)
    zgb = _bf(zg)
    sg = _sig(_mm(zgb, wg_ref[...]))
    z = zg * sg
    zb = _bf(z)
    ys = jnp.concatenate([_mm(zb, wsb_ref[j]) for j in range(N_CHIPS)], axis=1)
    ya = jnp.concatenate([_mm(attb, wab_ref[j]) for j in range(N_CHIPS)], axis=1)
    sa = _sig(ga)
    ss = _sig(gs)
    mgb = _bf(sa * ya + ss * ys)
    o = _mm(mgb, wout_ref[...])
    return dict(zg=zg, dgelu=dgelu, zgb=zgb, sg=sg, zb=zb, ys=ys, ya=ya, sa=sa, ss=ss, mgb=mgb, o=o)


def _merge_fwd(x, s, att, ga, gs, g2, w_glu, w_ab, w_sb, w_out, tb):
    seq = x.shape[0]

    def body(x_ref, s_ref, att_ref, ga_ref, gs_ref, g_ref, wg_ref, wab_ref, wsb_ref, wout_ref, x2_ref):
        f = _merge_core(s_ref[...], att_ref[...], ga_ref[...], gs_ref[...], wg_ref, wab_ref, wsb_ref, wout_ref)
        n, _, _ = _rms(f["o"], g_ref[...])
        x2_ref[...] = x_ref[...] + n

    return _rowcall("merge_fwd", body, seq, tb, [x, s, att, ga, gs], [g2, w_glu, w_ab, w_sb, w_out],
                    [(D_MODEL, F32)], [], vmem=VMEM_BIG)[0]


def _merge_bwd(dx2, s, att, ga, gs, g2, w_glu, w_ab, w_sb, w_out, tb, exchange=None):
    seq = s.shape[0]
    cw = D_MODEL // N_CHIPS
    last = seq // tb - 1

    def body(dx2_ref, s_ref, att_ref, ga_ref, gs_ref, g_ref, wg_ref, wab_ref, wsb_ref, wout_ref,
             ds_ref, datt_ref, dga_ref, dgs_ref, dg_ref, dwg_ref, dwab_ref, dwsb_ref, dwout_ref,
             bwg_ref, bwab_ref, bwsb_ref, bwout_ref):
        @pl.when(pl.program_id(0) == 0)
        def _():
            for r in (dg_ref, dwg_ref, dwab_ref, dwsb_ref, dwout_ref):
                r[...] = jnp.zeros_like(r)

        attb = att_ref[...]
        f = _merge_core(s_ref[...], attb, ga_ref[...], gs_ref[...], wg_ref, wab_ref, wsb_ref, wout_ref)
        g = g_ref[...]
        _, oh, r2 = _rms(f["o"], g)
        do, dg = _rms_bwd(dx2_ref[...], oh, r2, g)
        dg_ref[...] += dg
        dob = _bf(do)
        dwout_ref[...] += _mm_tn(f["mgb"], dob)
        dmg = _mm_nt(dob, wout_ref[...])
        sa, ss = f["sa"], f["ss"]
        dyab = _bf(dmg * sa)
        dysb = _bf(dmg * ss)
        dga_ref[...] = _bf(dmg * f["ya"] * sa * (1.0 - sa))
        dgs_ref[...] = _bf(dmg * f["ys"] * ss * (1.0 - ss))
        dwab = _mm_tn(attb, dyab)
        dwsb = _mm_tn(f["zb"], dysb)
        datt = jnp.zeros((tb, ATTN_W), F32)
        dz = jnp.zeros((tb, SSM_W), F32)
        for j in range(N_CHIPS):
            dwab_ref[j] += dwab[:, j * cw:(j + 1) * cw]
            dwsb_ref[j] += dwsb[:, j * cw:(j + 1) * cw]
            datt = datt + _mm_nt(dyab[:, j * cw:(j + 1) * cw], wab_ref[j])
            dz = dz + _mm_nt(dysb[:, j * cw:(j + 1) * cw], wsb_ref[j])
        datt_ref[...] = _bf(datt)
        sg, zg = f["sg"], f["zg"]
        dglb = _bf(dz * zg * sg * (1.0 - sg))
        dwg_ref[...] += _mm_tn(f["zgb"], dglb)
        dzg = dz * sg + _mm_nt(dglb, wg_ref[...])
        ds_ref[...] = dzg * f["dgelu"]

        @pl.when(pl.program_id(0) == last)
        def _():
            for dst, src in ((bwg_ref, dwg_ref), (bwab_ref, dwab_ref), (bwsb_ref, dwsb_ref), (bwout_ref, dwout_ref)):
                dst[...] = _bf(src[...])

    shapes = [w_glu.shape, w_ab.shape, w_sb.shape, w_out.shape]
    return _rowcall("merge_bwd", body, seq, tb, [dx2, s, att, ga, gs], [g2, w_glu, w_ab, w_sb, w_out],
                    [(SSM_W, F32), (ATTN_W, BF16), (D_MODEL, BF16), (D_MODEL, BF16)],
                    [((1, D_MODEL), F32)] + [(sh, F32) for sh in shapes] + [(sh, BF16) for sh in shapes],
                    vmem=VMEM_BIG, exchange=exchange)


def _mlp_fwd_loss(x2, target, g3, g4, w_ffi, w_ffo, tb):
    seq = x2.shape[0]

    def body(x2_ref, t_ref, g3_ref, g4_ref, wi_ref, wo_ref, dy_ref, df_ref, h_ref, loss_ref, dg_ref):
        @pl.when(pl.program_id(0) == 0)
        def _():
            loss_ref[...] = jnp.zeros_like(loss_ref)
            dg_ref[...] = jnp.zeros_like(dg_ref)

        x2_blk = x2_ref[...]
        h3, _, _ = _rms(x2_blk, g3_ref[...])
        hb = _bf(h3)
        h_ref[...] = hb
        f = jnp.zeros((tb, D_MODEL), F32)
        for j in range(FF_CHUNKS):
            a = _mm(hb, wi_ref[j])
            f = f + _mm(_bf(jnp.square(jnp.maximum(a, 0.0))), wo_ref[j])
        g4 = g4_ref[...]
        n4, fh, r4 = _rms(f, g4)
        e = (x2_blk + n4) - t_ref[...]
        loss_ref[...] += 0.5 * jnp.sum(jnp.mean(e * e, axis=-1, keepdims=True))
        dy = e * (1.0 / D_MODEL)
        dy_ref[...] = dy
        df, dg = _rms_bwd(dy, fh, r4, g4)
        df_ref[...] = _bf(df)
        dg_ref[...] += dg

    return _rowcall("mlp_fwd_loss", body, seq, tb, [x2, target], [g3, g4, w_ffi, w_ffo],
                    [(D_MODEL, F32), (D_MODEL, BF16), (D_MODEL, BF16)],
                    [((SUBLANES, 128), F32), ((1, D_MODEL), F32)], vmem=VMEM_BIG)


def _mlp_bwd(x2, dy, df, h3, g3, w_ffi, w_ffo, tb):
    seq = x2.shape[0]
    cw = D_FF // FF_CHUNKS

    def body(x2_ref, dy_ref, df_ref, h_ref, g3_ref, wi_ref, wo_ref, dx_ref, act_ref, da_ref, dg_ref):
        @pl.when(pl.program_id(0) == 0)
        def _():
            dg_ref[...] = jnp.zeros_like(dg_ref)

        hb = h_ref[...]
        dfb = df_ref[...]
        dh = jnp.zeros((tb, D_MODEL), F32)
        for j in range(FF_CHUNKS):
            ra = jnp.maximum(_mm(hb, wi_ref[j]), 0.0)
            act_ref[:, j * cw:(j + 1) * cw] = _bf(ra * ra)
            dab = _bf(_mm_nt(dfb, wo_ref[j]) * (2.0 * ra))
            da_ref[:, j * cw:(j + 1) * cw] = dab
            dh = dh + _mm_nt(dab, wi_ref[j])
        g3 = g3_ref[...]
        _, xh, r3 = _rms(x2_ref[...], g3)
        dxn, dg = _rms_bwd(dh, xh, r3, g3)
        dx_ref[...] = dy_ref[...] + dxn
        dg_ref[...] += dg

    return _rowcall("mlp_bwd", body, seq, tb, [x2, dy, df, h3], [g3, w_ffi, w_ffo],
                    [(D_MODEL, F32), (D_FF, BF16), (D_FF, BF16)], [((1, D_MODEL), F32)], vmem=VMEM_BIG)


def _matmul_tn(name, a, b, tk, tn, tl, chunk_major, exchange=None):
    seq, kdim = a.shape
    ndim = b.shape[1]
    last = seq // tl - 1

    def body(a_ref, b_ref, o_ref, ob_ref):
        @pl.when(pl.program_id(2) == 0)
        def _():
            o_ref[...] = jnp.zeros_like(o_ref)

        o_ref[...] += _mm_tn(a_ref[...], b_ref[...])

        @pl.when(pl.program_id(2) == last)
        def _():
            ob_ref[...] = _bf(o_ref[...])

    if chunk_major:
        shape = (ndim // tn, kdim, tn)
        out_spec = pl.BlockSpec((None, tk, tn), lambda k, n, l: (n, k, 0))
    else:
        shape = (kdim, ndim)
        out_spec = pl.BlockSpec((tk, tn), lambda k, n, l: (k, n))
    return _fused_call(
        name, body, (kdim // tk, ndim // tn, seq // tl),
        [pl.BlockSpec((tl, tk), lambda k, n, l: (l, k)), pl.BlockSpec((tl, tn), lambda k, n, l: (l, n))],
        [out_spec, out_spec], [SDS(shape, F32), SDS(shape, BF16)], [], [a, b], exchange, _params(3, VMEM_BIG))


def _ew_call(name, fn, ins, n_out, exchange=None):
    rows, cols = ins[0].shape
    tr = rows
    while tr * cols * 4 > min(1 << 20, (9 << 20) // (len(ins) + n_out)) and tr % 16 == 0:
        tr //= 2
    spec = pl.BlockSpec((tr, cols), lambda i: (i, 0))

    def body(*refs):
        outs = fn(*[r[...] for r in refs[:len(ins)]])
        for r, o in zip(refs[len(ins):], outs):
            r[...] = o

    return _fused_call(name, body, (rows // tr,), [spec] * len(ins), [spec] * n_out,
                       [SDS((rows, cols), F32)] * n_out, [], list(ins), exchange, _params(1))


def _adam_math(w, g, m, v):
    m2 = ADAM_B1 * m + (1.0 - ADAM_B1) * g
    v2 = ADAM_B2 * v + (1.0 - ADAM_B2) * (g * g)
    m_hat = m2 / (1.0 - ADAM_B1 ** ADAM_STEP)
    v_hat = v2 / (1.0 - ADAM_B2 ** ADAM_STEP)
    delta = -ADAM_LR * (m_hat / (jnp.sqrt(v_hat) + ADAM_EPS) + ADAM_WD * w)
    return delta, m2, v2


def _sum4(name, own, recv, idx):
    _, rows, cols = own.shape
    tr = rows
    while tr * cols * 4 > (1 << 20) and tr % 16 == 0:
        tr //= 2

    def body(idx_ref, o_ref, r0_ref, r1_ref, r2_ref, out_ref):
        out_ref[...] = ((o_ref[...] + r0_ref[...].astype(F32)) + r1_ref[...].astype(F32)) + r2_ref[...].astype(F32)

    blk = (None, tr, cols)
    grid_spec = pltpu.PrefetchScalarGridSpec(
        num_scalar_prefetch=1, grid=(rows // tr,),
        in_specs=[pl.BlockSpec(blk, lambda i, s: (s[0], i, 0)), pl.BlockSpec(blk, lambda i, s: (0, i, 0)),
                  pl.BlockSpec(blk, lambda i, s: (1, i, 0)), pl.BlockSpec(blk, lambda i, s: (2, i, 0))],
        out_specs=pl.BlockSpec((tr, cols), lambda i, s: (i, 0)))
    return pl.pallas_call(body, grid_spec=grid_spec, out_shape=SDS((rows, cols), F32), name=name,
                          compiler_params=_params(1))(jnp.reshape(idx, (1,)).astype(jnp.int32), own, recv, recv, recv)


def _adam_pair(name, items, exchange=None):
    def fn(*vals):
        outs = ()
        for i in range(len(items)):
            w_, a, b, m_, v_ = vals[5 * i:5 * i + 5]
            g = a + b
            outs += (g,) + _adam_math(w_, g, m_, v_)
        return outs

    flat = _ew_call(name, fn, [a for item in items for a in item], 4 * len(items), exchange)
    return [flat[4 * i:4 * i + 4] for i in range(len(items))] + [list(flat[4 * len(items):])]


def _place():
    return lax.axis_index("x"), lax.axis_index("y"), lax.axis_index("c")


def _other_chips(x, y):
    return [(1 - x, y), (x, 1 - y), (1 - x, 1 - y)]


def _gather_chips(shards):
    n = len(shards)

    def copies(ins, outs, sems):
        send, recv, fwd_send, fwd_recv, loc = sems
        x, y, c = _place()
        me = 2 * x + y
        peers = _other_chips(x, y)
        local = [pltpu.make_async_copy(ins[a], outs[a].at[me], loc.at[a]) for a in range(n)]
        sends, recvs, passes, passed = [], [], [], []
        for a in range(n):
            half = shards[a].shape[0] // 2
            mine = pl.ds(c * half, half)
            theirs = pl.ds((1 - c) * half, half)
            for j, (px, py) in enumerate(peers):
                far = 2 * px + py
                sends.append(pltpu.make_async_remote_copy(
                    src_ref=ins[a].at[mine], dst_ref=outs[a].at[me, mine], send_sem=send.at[a, j],
                    recv_sem=recv.at[a, j], device_id=(px, py, c), device_id_type=MESH_ID))
                recvs.append(pltpu.make_async_remote_copy(
                    src_ref=ins[a].at[mine], dst_ref=outs[a].at[far, mine], send_sem=send.at[a, j],
                    recv_sem=recv.at[a, j], device_id=(px, py, c), device_id_type=MESH_ID))
                passes.append(pltpu.make_async_remote_copy(
                    src_ref=outs[a].at[far, mine], dst_ref=outs[a].at[far, mine], send_sem=fwd_send.at[a, j],
                    recv_sem=fwd_recv.at[a, j], device_id=(x, y, 1 - c), device_id_type=MESH_ID))
                passed.append(pltpu.make_async_remote_copy(
                    src_ref=outs[a].at[far, theirs], dst_ref=outs[a].at[far, theirs], send_sem=fwd_send.at[a, j],
                    recv_sem=fwd_recv.at[a, j], device_id=(x, y, 1 - c), device_id_type=MESH_ID))
        return local, sends, recvs, passes, passed

    def start(ins, outs, sems):
        local, sends, _, _, _ = copies(ins, outs, sems)
        for cp in local + sends:
            cp.start()

    def wait(ins, outs, sems):
        local, sends, recvs, passes, passed = copies(ins, outs, sems)
        for got, on in zip(recvs, passes):
            got.wait_recv()
            on.start()
        for cp in passed:
            cp.wait_recv()
        for cp in passes + sends:
            cp.wait_send()
        for cp in local:
            cp.wait()

    assert all(s.shape[0] % 32 == 0 for s in shards)
    pair = pltpu.SemaphoreType.DMA((n, 3))
    return _Exchange(shards, [SDS((N_CHIPS,) + s.shape, s.dtype) for s in shards],
                     [pair, pair, pair, pair, pltpu.SemaphoreType.DMA((n,))], start, wait)


def _scatter_chips(chunks):
    n = len(chunks)

    def copies(ins, outs, sems):
        send, recv = sems
        x, y, c = _place()
        return [pltpu.make_async_remote_copy(
            src_ref=ins[a].at[2 * px + py], dst_ref=outs[a].at[j], send_sem=send.at[a, j],
            recv_sem=recv.at[a, j], device_id=(px, py, c), device_id_type=MESH_ID)
            for a in range(n) for j, (px, py) in enumerate(_other_chips(x, y))]

    def start(ins, outs, sems):
        for cp in copies(ins, outs, sems):
            cp.start()

    def wait(ins, outs, sems):
        cps = copies(ins, outs, sems)
        for cp in cps:
            cp.wait_recv()
        for cp in cps:
            cp.wait_send()

    return _Exchange(chunks, [SDS((3,) + s.shape[1:], s.dtype) for s in chunks],
                     [pltpu.SemaphoreType.DMA((n, 3)), pltpu.SemaphoreType.DMA((n, 3))], start, wait)


def _swap_sibling(arrs):
    n = len(arrs)

    def copies(ins, outs, sems):
        send, recv = sems
        x, y, c = _place()
        return [pltpu.make_async_remote_copy(
            src_ref=ins[a], dst_ref=outs[a], send_sem=send.at[a], recv_sem=recv.at[a],
            device_id=(x, y, 1 - c), device_id_type=MESH_ID) for a in range(n)]

    def start(ins, outs, sems):
        for cp in copies(ins, outs, sems):
            cp.start()

    def wait(ins, outs, sems):
        cps = copies(ins, outs, sems)
        for cp in cps:
            cp.wait_recv()
        for cp in cps:
            cp.wait_send()

    return _Exchange(arrs, [SDS(s.shape, s.dtype) for s in arrs],
                     [pltpu.SemaphoreType.DMA((n,)), pltpu.SemaphoreType.DMA((n,))], start, wait)


N_DEV = 8


def _gather_devices(block):
    def copies(ins, outs, sems):
        send, recv, loc = sems
        x, y, c = _place()
        me = 4 * x + 2 * y + c
        local = pltpu.make_async_copy(ins[0], outs[0].at[me], loc.at[0])
        sends, recvs = [], []
        for k in range(1, N_DEV):
            peer = (x ^ (k >> 2), y ^ ((k >> 1) & 1), c ^ (k & 1))
            for group, slot in ((sends, me), (recvs, me ^ k)):
                group.append(pltpu.make_async_remote_copy(
                    src_ref=ins[0], dst_ref=outs[0].at[slot], send_sem=send.at[k - 1], recv_sem=recv.at[k - 1],
                    device_id=peer, device_id_type=MESH_ID))
        return local, sends, recvs

    def start(ins, outs, sems):
        local, sends, _ = copies(ins, outs, sems)
        for cp in [local] + sends:
            cp.start()

    def wait(ins, outs, sems):
        local, sends, recvs = copies(ins, outs, sems)
        for cp in recvs:
            cp.wait_recv()
        for cp in sends:
            cp.wait_send()
        local.wait()

    return _Exchange([block], [SDS((N_DEV,) + block.shape, block.dtype)],
                     [pltpu.SemaphoreType.DMA((N_DEV - 1,)), pltpu.SemaphoreType.DMA((N_DEV - 1,)),
                      pltpu.SemaphoreType.DMA((1,))], start, wait)


def _both(ex_a, ex_b):
    na_i, na_o, na_s = len(ex_a.ins), len(ex_a.outs), len(ex_a.sems)

    def start(ins, outs, sems):
        ex_a.start(ins[:na_i], outs[:na_o], sems[:na_s])
        ex_b.start(ins[na_i:], outs[na_o:], sems[na_s:])

    def wait(ins, outs, sems):
        ex_a.wait(ins[:na_i], outs[:na_o], sems[:na_s])
        ex_b.wait(ins[na_i:], outs[na_o:], sems[na_s:])

    return _Exchange(ex_a.ins + ex_b.ins, ex_a.outs + ex_b.outs, ex_a.sems + ex_b.sems, start, wait)


def _sum_devices(slots):
    def body(s_ref, o_ref):
        acc = s_ref[0]
        for d in range(1, N_DEV):
            acc = acc + s_ref[d]
        o_ref[...] = acc

    return pl.pallas_call(
        body, in_specs=[pl.BlockSpec(memory_space=pltpu.VMEM)], out_specs=pl.BlockSpec(memory_space=pltpu.VMEM),
        out_shape=SDS(slots.shape[1:], F32), name="sum_small",
        compiler_params=pltpu.CompilerParams(vmem_limit_bytes=32 * 1024 * 1024))(slots)


def _adam_small(ws, gs, ms, vs):
    n = len(ws)

    def body(*refs):
        for i in range(n):
            w_ref, g_ref, m_ref, v_ref = (refs[k * n + i] for k in range(4))
            outs = _adam_math(w_ref[...], g_ref[...], m_ref[...], v_ref[...])
            for k in range(3):
                refs[(4 + k) * n + i][...] = outs[k]

    vmem = pl.BlockSpec(memory_space=pltpu.VMEM)
    return pl.pallas_call(
        body, in_specs=[vmem] * (4 * n), out_specs=[vmem] * (3 * n),
        out_shape=[SDS(w.shape, F32) for w in ws] * 3, name="adam_small",
        compiler_params=pltpu.CompilerParams(vmem_limit_bytes=32 * 1024 * 1024))(*ws, *gs, *ms, *vs)


def _local_step(x, target, small, big, tb, distributed):
    g1, g2, g3, g4 = small["norm_mix_pre"], small["norm_mix_post"], small["norm_mlp_pre"], small["norm_mlp_post"]
    dist = distributed
    me = (2 * lax.axis_index("x") + lax.axis_index("y")) if dist else 0
    tb_ssm = min(tb, 256)
    bucket = jnp.asarray(_bucket_table())

    keys_first = lambda t: jnp.swapaxes(t, -1, -2)
    bias = keys_first(_pair_layout(_bias_table(small["rel_bias"], bucket)))
    sink_rows = keys_first(_pair_layout(jnp.broadcast_to(small["sinks"].reshape(N_HEADS, 1, 1), (N_HEADS, BLOCK, 1))))
    disc_args = (small["lam_re"], small["lam_im"], small["log_dt"], small["b_re"], small["b_im"])
    (ab_re, ab_im, bb_re, bb_im), disc_vjp = jax.vjp(_ssm_discretize, *disc_args)
    tab_f, tab_b = _scan_tables(ab_re, ab_im)
    bmat = _bf(_b_matrix(bb_re, bb_im))
    cmat = _bf(_c_matrix(small["c_re"], small["c_im"]))
    d_skip = small["d_skip"]

    if dist:
        (g_in,) = _exchange_alone("gather_w_in", _gather_chips([big["w_in"]]))
        w_in = g_in.reshape(IN_W, D_MODEL)
    else:
        w_in = big["w_in"]
    mix = ("w_glu", "w_attn_branch", "w_ssm_branch", "w_out")
    outs = _inproj_fwd(x, g1, w_in, tb, _gather_chips([big[n] for n in mix]) if dist else None)
    h1, q, k, v, u, ga, gs = outs[:7]
    w_glu, w_ab, w_sb, w_out = outs[7:] if dist else [big[n] for n in mix]
    w_glu = w_glu.reshape(SSM_W, SSM_W)
    w_out = w_out.reshape(D_MODEL, D_MODEL)
    outs = _attn_fwd(q, k, v, bias, sink_rows, _gather_chips([big["w_ff_in"]]) if dist else None)
    att = outs[0]
    w_ffi = outs[1] if dist else big["w_ff_in"]
    outs = _ssm_fwd(u, bmat, cmat, tab_f, d_skip, tb_ssm, _gather_chips([big["w_ff_out"]]) if dist else None)
    s, h = outs[:2]
    w_ffo = outs[2] if dist else big["w_ff_out"]
    x2 = _merge_fwd(x, s, att, ga, gs, g2, w_glu, w_ab, w_sb, w_out, tb)
    dy, df, h3, loss_acc, dg4 = _mlp_fwd_loss(x2, target, g3, g4, w_ffi, w_ffo, tb)

    dx2, act, da, dg3 = _mlp_bwd(x2, dy, df, h3, g3, w_ffi, w_ffo, tb)
    tl = min(2048, x.shape[0])
    chunked = (N_CHIPS, D_FF // N_CHIPS, D_MODEL)
    d_ffi, b_ffi = _matmul_tn("grad_w_ff_in", h3, da, D_MODEL, D_FF // FF_CHUNKS, tl, True)
    d_ffo, b_ffo = _matmul_tn("grad_w_ff_out", act, df, D_FF // FF_CHUNKS, D_MODEL, tl, False)
    d_ffo, b_ffo = d_ffo.reshape(chunked), b_ffo.reshape(chunked)
    outs = _merge_bwd(dx2, s, att, ga, gs, g2, w_glu, w_ab, w_sb, w_out, tb_ssm,
                      _scatter_chips([b_ffi]) if dist else None)
    ds, datt, dga, dgs, dg2, d_glu, d_ab, d_sb, d_out, b_glu, b_ab, b_sb, b_out = outs[:13]
    r_ffi = outs[13:]
    glu4, out4 = (N_CHIPS, SSM_W // N_CHIPS, SSM_W), (N_CHIPS, D_MODEL // N_CHIPS, D_MODEL)
    d_mix = [d_glu.reshape(glu4), d_ab, d_sb, d_out.reshape(out4)]
    b_mix = [b_glu.reshape(glu4), b_ab, b_sb, b_out.reshape(out4)]
    outs = _ssm_bwd(ds, u, h, bmat.transpose(0, 2, 1), cmat.transpose(0, 2, 1), tab_b, d_skip, tb_ssm,
                    _scatter_chips([b_ffo]) if dist else None)
    du, d_bmat, d_cmat, da_acc, dd_skip = outs[:5]
    r_ffo = outs[5:]
    outs = _attn_bwd(q, k, v, datt, bias, sink_rows, _scatter_chips(b_mix) if dist else None)
    dq, dk, dv, dbias, dsink_rows = outs[:5]
    r_mix = outs[5:]
    if dist:
        p_ffi = _sum4("sum_w_ff_in", d_ffi, r_ffi[0], me)
        p_ffo = _sum4("sum_w_ff_out", d_ffo, r_ffo[0], me)
    dx, dpj, dg1 = _inproj_bwd(x, dx2, dq, dk, dv, du, dga, dgs, g1, w_in, tb)

    dab_re, dab_im = _state_unlayout(jnp.sum(da_acc, axis=0))
    dbb_re, dbb_im = _b_matrix_grad(d_bmat)
    d_lam_re, d_lam_im, d_log_dt, d_b_re, d_b_im = disc_vjp((dab_re, dab_im, dbb_re, dbb_im))
    d_c_re, d_c_im = _c_matrix_grad(d_cmat)
    d_rel = _bias_grad(_pair_unlayout(keys_first(dbias)), bucket)
    d_sinks = jnp.sum(_pair_unlayout(keys_first(dsink_rows)), axis=(1, 2))
    small_grads = dict(
        norm_mix_pre=dg1, norm_mix_post=dg2, norm_mlp_pre=dg3, norm_mlp_post=dg4, rel_bias=d_rel, sinks=d_sinks,
        lam_re=d_lam_re, lam_im=d_lam_im, log_dt=d_log_dt, b_re=d_b_re, b_im=d_b_im, c_re=d_c_re, c_im=d_c_im,
        d_skip=dd_skip)
    ride = _both(_swap_sibling([p_ffi, p_ffo]), _gather_devices(_pack(small_grads, loss_acc))) if dist else None
    outs = _matmul_tn("grad_w_in", dpj, h1, IN_W // 2, D_MODEL, tl, False, ride)
    in4 = (N_CHIPS, IN_W // N_CHIPS, D_MODEL)
    d_in, b_in = outs[0].reshape(in4), outs[1].reshape(in4)
    if not dist:
        return loss_acc, dx, small_grads, dict(zip(BIG, [d_in] + d_mix + [d_ffi, d_ffo]))
    s_ffi, s_ffo, slots = outs[2:]
    p_mix = [_sum4("sum_" + n, d, r, me) for n, d, r in zip(mix, d_mix, r_mix)]
    pending = dict(d_in=d_in, b_in=b_in, p_mix=p_mix, w_ff_in=(p_ffi, s_ffi), w_ff_out=(p_ffo, s_ffo), me=me)
    return loss_acc, dx, _sum_devices(slots), pending


SMALL = ['norm_mix_pre', 'norm_mix_post', 'norm_mlp_pre', 'norm_mlp_post', 'rel_bias', 'sinks', 'lam_re', 'lam_im',
         'log_dt', 'b_re', 'b_im', 'c_re', 'c_im', 'd_skip']
BIG = ['w_in', 'w_glu', 'w_attn_branch', 'w_ssm_branch', 'w_out', 'w_ff_in', 'w_ff_out']
WEIGHTS = ['norm_mix_pre', 'norm_mix_post', 'norm_mlp_pre', 'norm_mlp_post', 'w_in', 'rel_bias', 'sinks', 'lam_re',
           'lam_im', 'log_dt', 'b_re', 'b_im', 'c_re', 'c_im', 'd_skip', 'w_glu', 'w_attn_branch', 'w_ssm_branch',
           'w_out', 'w_ff_in', 'w_ff_out']
PACK_COLS = 1024
PACK_ORDER = ['b_re', 'b_im', 'c_re', 'c_im', 'lam_re', 'lam_im', 'norm_mix_pre', 'norm_mix_post', 'norm_mlp_pre',
              'norm_mlp_post', 'rel_bias', 'sinks', 'log_dt', 'd_skip']


STATE_MINOR = ('b_re', 'b_im')
PACK_ROWS = 144
LOSS_ROW = 140


def _pack(named, loss_acc):
    parts = []
    for n in PACK_ORDER:
        a = jnp.swapaxes(named[n], -1, -2) if n in STATE_MINOR else named[n]
        flat = a.reshape(-1)
        rows = -(-flat.shape[0] // PACK_COLS)
        parts.append(jnp.pad(flat, (0, rows * PACK_COLS - flat.shape[0])).reshape(rows, PACK_COLS))
    assert sum(p.shape[0] for p in parts) == LOSS_ROW
    parts.append(jnp.pad(loss_acc[0:1], ((0, PACK_ROWS - LOSS_ROW - 1), (0, PACK_COLS - loss_acc.shape[1]))))
    return jnp.concatenate(parts, axis=0)


def _unpack(packed, shapes):
    out, at = {}, 0
    for n in PACK_ORDER:
        shape = shapes[n][:-2] + (shapes[n][-1], shapes[n][-2]) if n in STATE_MINOR else shapes[n]
        size = int(np.prod(shape))
        rows = -(-size // PACK_COLS)
        blk = packed[at:at + rows]
        out[n] = (blk.reshape(-1)[:size] if size % PACK_COLS else blk).reshape(shape)
        at += rows
    return out


def kernel(x, norm_mix_pre, norm_mix_post, norm_mlp_pre, norm_mlp_post, w_in, rel_bias, sinks, lam_re, lam_im, log_dt, b_re, b_im, c_re, c_im, d_skip, w_glu, w_attn_branch, w_ssm_branch, w_out, w_ff_in, w_ff_out, loss_target, m_norm_mix_pre, m_norm_mix_post, m_norm_mlp_pre, m_norm_mlp_post, m_w_in, m_rel_bias, m_sinks, m_lam_re, m_lam_im, m_log_dt, m_b_re, m_b_im, m_c_re, m_c_im, m_d_skip, m_w_glu, m_w_attn_branch, m_w_ssm_branch, m_w_out, m_w_ff_in, m_w_ff_out, v_norm_mix_pre, v_norm_mix_post, v_norm_mlp_pre, v_norm_mlp_post, v_w_in, v_rel_bias, v_sinks, v_lam_re, v_lam_im, v_log_dt, v_b_re, v_b_im, v_c_re, v_c_im, v_d_skip, v_w_glu, v_w_attn_branch, v_w_ssm_branch, v_w_out, v_w_ff_in, v_w_ff_out):
    env = dict(locals())
    w = {n: env[n] for n in WEIGHTS}
    m = {n: env["m_" + n] for n in WEIGHTS}
    v = {n: env["v_" + n] for n in WEIGHTS}
    seq = x.shape[1]
    tb = min(512, seq)

    small = {n: w[n] for n in ('norm_mix_pre', 'norm_mix_post', 'norm_mlp_pre', 'norm_mlp_post', 'rel_bias')}
    small.update({n: w[n][0] for n in ('sinks', 'lam_re', 'lam_im', 'log_dt', 'b_re', 'b_im', 'c_re', 'c_im')})
    small['d_skip'] = w['d_skip']
    shard = lambda t, n: t[n][0].T if n == 'w_in' else t[n][0]
    unshard = lambda a, n: (a.T if n == 'w_in' else a)[None]
    _, dx, small_g, pending = _local_step(
        x[0], loss_target[0], small, {n: _bf(shard(w, n)) for n in BIG}, tb, True)

    loss = small_g[LOSS_ROW, 0]

    grads, deltas, new_m, new_v = {}, {}, {}, {}

    def adam(name, names, partials, exchange=None):
        items = [(shard(w, n), *partials[n], shard(m, n), shard(v, n)) for n in names]
        outs = _adam_pair(name, items, exchange)
        for n, res in zip(names, outs):
            grads[n], deltas[n], new_m[n], new_v[n] = [unshard(a, n) for a in res[:4]]
        return outs[len(names):]

    (r_in,), = adam("adam_w_ff", ("w_ff_in", "w_ff_out"), pending, _scatter_chips([pending["b_in"]]))
    mix = ("w_glu", "w_attn_branch", "w_ssm_branch", "w_out")
    parts = [_sum4("sum_w_in", pending["d_in"], r_in, pending["me"])] + pending["p_mix"]
    sibs = _exchange_alone("swap_rest", _swap_sibling(parts))
    partials = dict(zip(("w_in",) + mix, zip(parts, sibs)))
    for n in ("w_in",) + mix:
        adam("adam_" + n, (n,), partials)

    minor = lambda t, n: jnp.swapaxes(t, -1, -2) if n in STATE_MINOR else t
    g_small = _unpack(small_g, {n: w[n].shape for n in SMALL})
    outs = _adam_small([minor(w[n], n) for n in SMALL], [g_small[n] for n in SMALL],
                       [minor(m[n], n) for n in SMALL], [minor(v[n], n) for n in SMALL])
    grads.update({n: minor(g_small[n], n) for n in SMALL})
    for k, dst in enumerate((deltas, new_m, new_v)):
        dst.update({n: minor(a, n) for n, a in zip(SMALL, outs[k * len(SMALL):(k + 1) * len(SMALL)])})

    return (loss, dx[None], *[grads[n] for n in WEIGHTS], *[deltas[n] for n in WEIGHTS],
            *[new_m[n] for n in WEIGHTS], *[new_v[n] for n in WEIGHTS])
```

```python
import functools
import math

import numpy as np
import jax
import jax.numpy as jnp
from jax import lax
from jax.experimental import pallas as pl
from jax.experimental.pallas import tpu as pltpu

F32 = jnp.float32
BF16 = jnp.bfloat16

D_MODEL = 1024
N_HEADS = 8
N_KV = 2
Q_GROUP = 4
HEAD_DIM = 64
ATTN_W = 512
KV_W = 128
BLOCK = 128
N_BUCKETS = 32
MAX_DISTANCE = 128
NEG_INF = -1e30
SSM_W = 512
SSM_GROUP = 16
SSM_GROUPS = 32
SSM_STATE = 64
N_SUPER = 4
GROUPS_PER_SUPER = SSM_GROUPS // N_SUPER
SUPER_IN = GROUPS_PER_SUPER * SSM_GROUP
SUPER_HALF = GROUPS_PER_SUPER * SSM_STATE
SUPER_W = 2 * SUPER_HALF
STATE_COLS = N_SUPER * SUPER_W
D_FF = 4096
FF_CHUNKS = 4
IN_W = 3328
SPLITS = (0, 512, 640, 768, 1280, 2304, 3328)
RMS_EPS = 1e-6
N_CHIPS = 4
SUBLANES = 8
LANES = 128
STATE_TILES = STATE_COLS // LANES
SUPER_TILES = SUPER_W // LANES

ADAM_LR = 0.001
ADAM_B1 = 0.9
ADAM_B2 = 0.999
ADAM_EPS = 1e-08
ADAM_WD = 0.01
ADAM_STEP = 10

VMEM_BIG = 56 * 1024 * 1024
SDS = jax.ShapeDtypeStruct
MESH_ID = pl.DeviceIdType.MESH
ANY = pl.BlockSpec(memory_space=pl.ANY)


def _bf(x):
    return x.astype(BF16)


def _mm(a, b):
    return jnp.dot(a, b, preferred_element_type=F32)


def _mm_nt(a, b):
    return lax.dot_general(a, b, (((1,), (1,)), ((), ())), preferred_element_type=F32)


def _mm_tn(a, b):
    return lax.dot_general(a, b, (((0,), (0,)), ((), ())), preferred_element_type=F32)


def _sig(x):
    return 1.0 / (1.0 + jnp.exp(-x))


def _rms(x, g):
    r = lax.rsqrt(jnp.mean(x * x, axis=-1, keepdims=True) + RMS_EPS)
    xh = x * r
    return xh * g, xh, r


def _rms_bwd(dout, xh, r, g):
    dg = jnp.sum(dout * xh, axis=0, keepdims=True)
    dxh = dout * g
    dx = r * (dxh - xh * jnp.mean(dxh * xh, axis=-1, keepdims=True))
    return dx, dg


_GELU_C = math.sqrt(2.0 / math.pi)


def _gelu_and_grad(x):
    x2 = x * x
    inner = _GELU_C * (x + 0.044715 * (x2 * x))
    t = jnp.tanh(inner)
    y = 0.5 * x * (1.0 + t)
    dy = 0.5 * (1.0 + t) + 0.5 * x * (1.0 - t * t) * (_GELU_C * (1.0 + 3.0 * 0.044715 * x2))
    return y, dy


def _zero_map(nd, *_):
    return (0,) * nd


def _params(n_axes, vmem=None):
    return pltpu.CompilerParams(dimension_semantics=("arbitrary",) * n_axes, vmem_limit_bytes=vmem)


class _Exchange:
    def __init__(self, ins, outs, sems, start, wait):
        self.ins, self.outs, self.sems, self.start, self.wait = list(ins), list(outs), list(sems), start, wait


def _fused_call(name, body, grid, in_specs, out_specs, out_shape, scratch, args, exchange, params):
    n_in, n_out, n_scr = len(in_specs), len(out_specs), len(scratch)
    if exchange is None:
        fn = body
    else:
        ex = exchange
        n_xi, n_xo = len(ex.ins), len(ex.outs)

        def fn(*refs):
            at = 0
            parts = []
            for n in (n_in, n_xi, n_out, n_xo, n_scr, len(ex.sems)):
                parts.append(refs[at:at + n])
                at += n
            ins, x_in, outs, x_out, scr, x_sem = parts
            ids = [pl.program_id(a) for a in range(len(grid))]
            first = functools.reduce(jnp.logical_and, [i == 0 for i in ids])
            last = functools.reduce(jnp.logical_and, [i == g - 1 for i, g in zip(ids, grid)])

            @pl.when(first)
            def _():
                ex.start(x_in, x_out, x_sem)

            body(*ins, *outs, *scr)

            @pl.when(last)
            def _():
                ex.wait(x_in, x_out, x_sem)

        in_specs = list(in_specs) + [ANY] * n_xi
        out_specs = list(out_specs) + [ANY] * n_xo
        out_shape = list(out_shape) + ex.outs
        scratch = list(scratch) + ex.sems
        args = list(args) + ex.ins
    return pl.pallas_call(fn, grid=grid, in_specs=in_specs, out_specs=out_specs, out_shape=out_shape,
                          scratch_shapes=list(scratch), name=name, compiler_params=params)(*args)


def _exchange_alone(name, ex):
    def body(*refs):
        n_xi, n_xo = len(ex.ins), len(ex.outs)
        x_in, x_out, x_sem = refs[:n_xi], refs[n_xi:n_xi + n_xo], refs[n_xi + n_xo:]
        ex.start(x_in, x_out, x_sem)
        ex.wait(x_in, x_out, x_sem)

    return pl.pallas_call(body, in_specs=[ANY] * len(ex.ins), out_specs=[ANY] * len(ex.outs), out_shape=ex.outs,
                          scratch_shapes=ex.sems, name=name)(*ex.ins)


def _rowcall(name, body, seq, tb, rows, consts, row_outs, acc_outs, scratch=(), reverse=False, vmem=None,
             exchange=None):
    nb = seq // tb
    rmap = (lambda i: (nb - 1 - i, 0)) if reverse else (lambda i: (i, 0))
    tmap = lambda i: (0,) + rmap(i)

    def row_spec(width):
        if isinstance(width, tuple):
            return pl.BlockSpec((width[0], tb, width[1]), tmap)
        return pl.BlockSpec((tb, width), rmap)

    def row_shape(width):
        return (width[0], seq, width[1]) if isinstance(width, tuple) else (seq, width)

    in_specs = [row_spec(a.shape[1] if a.ndim == 2 else (a.shape[0], a.shape[2])) for a in rows]
    in_specs += [pl.BlockSpec(a.shape, functools.partial(_zero_map, a.ndim), pipeline_mode=pl.Buffered(1))
                 for a in consts]
    out_specs = [row_spec(c) for c, _ in row_outs] + [ANY] * len(acc_outs)
    out_shape = [SDS(row_shape(c), dt) for c, dt in row_outs] + [SDS(s, dt) for s, dt in acc_outs]
    n_main = len(rows) + len(consts) + len(row_outs)
    n_acc = len(acc_outs)

    def fn(*refs):
        main, acc_hbm, rest = refs[:n_main], refs[n_main:n_main + n_acc], refs[n_main + n_acc:]
        acc_vmem, own = rest[:n_acc], rest[n_acc:]
        body(*main, *acc_vmem, *own)

        @pl.when(pl.program_id(0) == nb - 1)
        def _():
            for src, dst in zip(acc_vmem, acc_hbm):
                pltpu.sync_copy(src, dst)

    buffers = [pltpu.VMEM(s, dt) for s, dt in acc_outs] + list(scratch)
    return _fused_call(name, fn if acc_outs else body, (nb,), in_specs, out_specs, out_shape, buffers,
                       [*rows, *consts], exchange, _params(1, vmem))


def _inproj_fwd(x, g1, w_in, tb, exchange=None):
    seq = x.shape[0]

    def body(x_ref, g_ref, w_ref, h_ref, q_ref, k_ref, v_ref, u_ref, ga_ref, gs_ref):
        h, _, _ = _rms(x_ref[...], g_ref[...])
        hb = _bf(h)
        h_ref[...] = hb
        pj = _mm_nt(hb, w_ref[...])
        q_ref[...] = _bf(pj[:, SPLITS[0]:SPLITS[1]])
        k_ref[...] = _bf(pj[:, SPLITS[1]:SPLITS[2]])
        v_ref[...] = _bf(pj[:, SPLITS[2]:SPLITS[3]])
        u_ref[...] = pj[:, SPLITS[3]:SPLITS[4]]
        ga_ref[...] = pj[:, SPLITS[4]:SPLITS[5]]
        gs_ref[...] = pj[:, SPLITS[5]:SPLITS[6]]

    return _rowcall("inproj_fwd", body, seq, tb, [x], [g1, w_in],
                    [(D_MODEL, BF16), (ATTN_W, BF16), (KV_W, BF16), (KV_W, BF16), (SSM_W, F32),
                     (D_MODEL, F32), (D_MODEL, F32)], [], vmem=VMEM_BIG, exchange=exchange)


def _inproj_bwd(x, dx2, dq, dk, dv, du, dga, dgs, g1, w_in, tb, exchange=None):
    seq = x.shape[0]

    def body(x_ref, dx2_ref, dq_ref, dk_ref, dv_ref, du_ref, dga_ref, dgs_ref, g_ref, w_ref,
             dx_ref, dpj_ref, dg_ref):
        @pl.when(pl.program_id(0) == 0)
        def _():
            dg_ref[...] = jnp.zeros_like(dg_ref)

        dpj = jnp.concatenate([dq_ref[...], dk_ref[...], dv_ref[...], _bf(du_ref[...]),
                               dga_ref[...], dgs_ref[...]], axis=1)
        dpj_ref[...] = dpj
        dh = _mm(dpj, w_ref[...])
        g = g_ref[...]
        _, xh, r = _rms(x_ref[...], g)
        dxn, dg = _rms_bwd(dh, xh, r, g)
        dx_ref[...] = dx2_ref[...] + dxn
        dg_ref[...] += dg

    return _rowcall("inproj_bwd", body, seq, tb, [x, dx2, dq, dk, dv, du, dga, dgs], [g1, w_in],
                    [(D_MODEL, F32), (IN_W, BF16)], [((1, D_MODEL), F32)], vmem=VMEM_BIG, exchange=exchange)


def _bucket_table():
    qi = np.arange(BLOCK)[:, None]
    kj = np.arange(2 * BLOCK)[None, :]
    dist = qi + BLOCK - kj
    max_exact = N_BUCKETS // 2
    d = np.maximum(dist, 0)
    df = np.maximum(d, 1).astype(np.float32)
    large = max_exact + (np.log(df / np.float32(max_exact)) / np.float32(math.log(MAX_DISTANCE / max_exact))
                         * np.float32(N_BUCKETS - max_exact)).astype(np.int32)
    large = np.minimum(large, N_BUCKETS - 1)
    bucket = np.where(d < max_exact, d, large)
    valid = (dist >= 0) & (dist < BLOCK)
    return np.where(valid, bucket, -1).astype(np.int32)


def _bias_table(rel_bias, bucket):
    def body(rb_ref, bk_ref, o_ref):
        bk = bk_ref[...]
        has_prev = lax.broadcasted_iota(jnp.int32, bk.shape, 1) >= BLOCK
        for h in range(N_HEADS):
            kh, j, par = h // Q_GROUP, (h // 2) % 2, h % 2
            acc = jnp.full((BLOCK, 2 * BLOCK), NEG_INF, F32)
            for b in range(N_BUCKETS):
                acc = jnp.where(bk == b, rb_ref[b, h], acc)
            o_ref[0, kh, par, :, j * BLOCK:(j + 1) * BLOCK] = jnp.where(has_prev, acc, NEG_INF).T
            o_ref[1, kh, par, :, j * BLOCK:(j + 1) * BLOCK] = acc.T

    return pl.pallas_call(
        body, out_shape=SDS((2, N_KV, 2, 2 * BLOCK, 2 * BLOCK), F32),
        in_specs=[pl.BlockSpec(memory_space=pltpu.SMEM), pl.BlockSpec(memory_space=pltpu.VMEM)],
        out_specs=pl.BlockSpec(memory_space=pltpu.VMEM), name="bias_table",
    )(rel_bias, bucket)


def _bias_grad(dbias, bucket):
    def body(db_ref, bk_ref, o_ref):
        bk = bk_ref[...]
        for h in range(N_HEADS):
            kh, j, par = h // Q_GROUP, (h // 2) % 2, h % 2
            db = db_ref[kh, par, :, j * BLOCK:(j + 1) * BLOCK].T
            for b in range(N_BUCKETS):
                o_ref[b, h] = jnp.sum(jnp.where(bk == b, db, 0.0))

    return pl.pallas_call(
        body, out_shape=SDS((N_BUCKETS, N_HEADS), F32),
        in_specs=[pl.BlockSpec(memory_space=pltpu.VMEM), pl.BlockSpec(memory_space=pltpu.VMEM)],
        out_specs=pl.BlockSpec(memory_space=pltpu.SMEM), name="bias_grad",
    )(dbias, bucket)


TILE = 2 * HEAD_DIM


def _pair_layout(t):
    lead = t.shape[:-3]
    t = t.reshape(lead + (N_KV, 2, 2) + t.shape[-2:])
    nl = len(lead)
    t = jnp.transpose(t, tuple(range(nl)) + (nl, nl + 2, nl + 1, nl + 3, nl + 4))
    return t.reshape(lead + (N_KV, 2, 2 * BLOCK, t.shape[-1]))


def _pair_unlayout(t):
    t = t.reshape(N_KV, 2, 2, BLOCK, t.shape[-1]).transpose(0, 2, 1, 3, 4)
    return t.reshape(N_HEADS, BLOCK, t.shape[-1])


def _halves(t):
    tf = t.astype(F32)
    low = lax.broadcasted_iota(jnp.int32, tf.shape, 1) < HEAD_DIM
    swapped = pltpu.roll(tf, HEAD_DIM, 1)
    zero = jnp.zeros_like(tf)
    return ((_bf(jnp.where(low, tf, zero)), _bf(jnp.where(low, zero, swapped))),
            (_bf(jnp.where(low, swapped, zero)), _bf(jnp.where(low, zero, tf))))


def _fold_halves(even, odd):
    low = lax.broadcasted_iota(jnp.int32, even.shape, 1) < HEAD_DIM
    comb = jnp.where(low, even, odd)
    return comb + pltpu.roll(comb, HEAD_DIM, 1)


def _tile_rows(ref, kh):
    return jnp.concatenate([ref[:, (2 * kh) * TILE:(2 * kh + 1) * TILE],
                            ref[:, (2 * kh + 1) * TILE:(2 * kh + 2) * TILE]], axis=0)


def _halves_t(t):
    tt = t.astype(F32).T
    top = lax.broadcasted_iota(jnp.int32, tt.shape, 0) < HEAD_DIM
    swapped = jnp.concatenate([tt[HEAD_DIM:], tt[:HEAD_DIM]], axis=0)
    zero = jnp.zeros_like(tt)
    return ((_bf(jnp.where(top, tt, zero)), _bf(jnp.where(top, zero, swapped))),
            (_bf(jnp.where(top, swapped, zero)), _bf(jnp.where(top, zero, tt))))


def _attn_probs(km, qk, bias, sink):
    lg = _mm_nt(km, qk) * (HEAD_DIM ** -0.5) + bias
    m = jnp.maximum(jnp.max(lg, axis=0, keepdims=True), sink)
    p = jnp.exp(lg - m)
    es = jnp.exp(sink - m)
    inv = 1.0 / (jnp.sum(p, axis=0, keepdims=True) + es)
    return p * inv, es * inv


def _attn_fwd(q, k, v, bias, sink_rows, exchange=None):
    seq = q.shape[0]
    nblk = seq // BLOCK

    def body(q_ref, kp_ref, kc_ref, vp_ref, vc_ref, b_ref, s_ref, o_ref):
        which = jnp.minimum(pl.program_id(0), 1)
        kms = _halves(jnp.concatenate([kp_ref[...], kc_ref[...]], axis=0))
        vts = _halves_t(jnp.concatenate([vp_ref[...], vc_ref[...]], axis=0))
        for kh in range(N_KV):
            qk = _tile_rows(q_ref, kh)
            acc = jnp.zeros((TILE, 2 * BLOCK), F32)
            for par in range(2):
                pr, _ = _attn_probs(kms[kh][par], qk, b_ref[which, kh, par], s_ref[kh, par])
                acc = acc + _mm(vts[kh][par], _bf(pr))
            acc = acc.T
            o_ref[:, (2 * kh) * TILE:(2 * kh + 1) * TILE] = _bf(acc[:BLOCK])
            o_ref[:, (2 * kh + 1) * TILE:(2 * kh + 2) * TILE] = _bf(acc[BLOCK:])

    cur = lambda n: (n, 0)
    prev = lambda n: (jnp.maximum(n - 1, 0), 0)
    return _fused_call(
        "attn_fwd", body, (nblk,),
        [pl.BlockSpec((BLOCK, ATTN_W), cur),
         pl.BlockSpec((BLOCK, KV_W), prev), pl.BlockSpec((BLOCK, KV_W), cur),
         pl.BlockSpec((BLOCK, KV_W), prev), pl.BlockSpec((BLOCK, KV_W), cur),
         pl.BlockSpec(bias.shape, functools.partial(_zero_map, bias.ndim)),
         pl.BlockSpec(sink_rows.shape, functools.partial(_zero_map, sink_rows.ndim))],
        [pl.BlockSpec((BLOCK, ATTN_W), cur)], [SDS((seq, ATTN_W), BF16)], [],
        [q, k, k, v, v, bias, sink_rows], exchange, _params(1))


def _attn_bwd(q, k, v, d_out, bias, sink_rows, exchange=None):
    seq = q.shape[0]
    nblk = seq // BLOCK

    def body(q_ref, kp_ref, kc_ref, vp_ref, vc_ref, do_ref, b_ref, s_ref,
             dq_ref, dk_ref, dv_ref, db_ref, ds_ref, ck_ref, cv_ref):
        n = pl.program_id(0)

        @pl.when(n == 0)
        def _():
            db_ref[...] = jnp.zeros_like(db_ref)
            ds_ref[...] = jnp.zeros_like(ds_ref)
            ck_ref[...] = jnp.zeros_like(ck_ref)
            cv_ref[...] = jnp.zeros_like(cv_ref)

        @pl.when(n < nblk)
        def _():
            which = jnp.minimum(n, 1)
            scale = HEAD_DIM ** -0.5
            kcat = jnp.concatenate([kp_ref[...], kc_ref[...]], axis=0)
            kms = _halves(kcat)
            kts = _halves_t(kcat)
            vms = _halves(jnp.concatenate([vp_ref[...], vc_ref[...]], axis=0))
            dks, dvs = [], []
            for kh in range(N_KV):
                qk = _tile_rows(q_ref, kh)
                dok = _tile_rows(do_ref, kh)
                dq = jnp.zeros((TILE, 2 * BLOCK), F32)
                dkp, dvp = [], []
                for par in range(2):
                    pr, ps = _attn_probs(kms[kh][par], qk, b_ref[which, kh, par], s_ref[kh, par])
                    dp = _mm_nt(vms[kh][par], dok)
                    rs = jnp.sum(pr * dp, axis=0, keepdims=True)
                    dlg = pr * (dp - rs)
                    ds_ref[kh, par] += -ps * rs
                    db_ref[kh, par] += dlg
                    dlb = _bf(dlg)
                    dq = dq + _mm(kts[kh][par], dlb)
                    dkp.append(_mm(dlb, qk))
                    dvp.append(_mm(_bf(pr), dok))
                dq = _bf((dq * scale).T)
                dq_ref[:, (2 * kh) * TILE:(2 * kh + 1) * TILE] = dq[:BLOCK]
                dq_ref[:, (2 * kh + 1) * TILE:(2 * kh + 2) * TILE] = dq[BLOCK:]
                dks.append(_fold_halves(*dkp))
                dvs.append(_fold_halves(*dvp))
            low = lax.broadcasted_iota(jnp.int32, (2 * BLOCK, TILE), 1) < HEAD_DIM
            dkk = jnp.where(low, dks[0], dks[1]) * scale
            dvv = jnp.where(low, dvs[0], dvs[1])
            dk_ref[...] = _bf(ck_ref[...] + dkk[:BLOCK])
            ck_ref[...] = dkk[BLOCK:]
            dv_ref[...] = _bf(cv_ref[...] + dvv[:BLOCK])
            cv_ref[...] = dvv[BLOCK:]

        @pl.when(n == nblk)
        def _():
            dk_ref[...] = _bf(ck_ref[...])
            dv_ref[...] = _bf(cv_ref[...])

    cur = lambda n: (jnp.minimum(n, nblk - 1), 0)
    prev = lambda n: (jnp.maximum(jnp.minimum(n, nblk - 1) - 1, 0), 0)
    late = lambda n: (jnp.maximum(n - 1, 0), 0)
    kv_spec = lambda m: pl.BlockSpec((BLOCK, KV_W), m)
    acc_b = pl.BlockSpec(bias.shape[1:], functools.partial(_zero_map, bias.ndim - 1))
    acc_s = pl.BlockSpec(sink_rows.shape, functools.partial(_zero_map, sink_rows.ndim))
    return _fused_call(
        "attn_bwd", body, (nblk + 1,),
        [pl.BlockSpec((BLOCK, ATTN_W), cur), kv_spec(prev), kv_spec(cur), kv_spec(prev), kv_spec(cur),
         pl.BlockSpec((BLOCK, ATTN_W), cur),
         pl.BlockSpec(bias.shape, functools.partial(_zero_map, bias.ndim)), acc_s],
        [pl.BlockSpec((BLOCK, ATTN_W), cur), kv_spec(late), kv_spec(late), acc_b, acc_s],
        [SDS((seq, ATTN_W), BF16), SDS((seq, KV_W), BF16), SDS((seq, KV_W), BF16),
         SDS(bias.shape[1:], F32), SDS(sink_rows.shape, F32)],
        [pltpu.VMEM((BLOCK, KV_W), F32), pltpu.VMEM((BLOCK, KV_W), F32)],
        [q, k, k, v, v, d_out, bias, sink_rows], exchange, _params(1))


def _ssm_discretize(lam_re, lam_im, log_dt, b_re, b_im):
    dt = jnp.exp(log_dt)[:, None]
    mag = jnp.exp(lam_re * dt)
    ab_re = mag * jnp.cos(lam_im * dt)
    ab_im = mag * jnp.sin(lam_im * dt)
    nr = ab_re - 1.0
    den = lam_re * lam_re + lam_im * lam_im
    f_re = (nr * lam_re + ab_im * lam_im) / den
    f_im = (ab_im * lam_re - nr * lam_im) / den
    bb_re = f_re[..., None] * b_re - f_im[..., None] * b_im
    bb_im = f_re[..., None] * b_im + f_im[..., None] * b_re
    return ab_re, ab_im, bb_re, bb_im


def _state_layout(re, im):
    z = jnp.stack([re, im]).reshape(2, N_SUPER, GROUPS_PER_SUPER, SSM_STATE)
    return z.transpose(1, 0, 2, 3).reshape(STATE_COLS)


def _state_unlayout(vec):
    z = vec.reshape(N_SUPER, 2, GROUPS_PER_SUPER, SSM_STATE).transpose(1, 0, 2, 3)
    z = z.reshape(2, SSM_GROUPS, SSM_STATE)
    return z[0], z[1]


SEG = 4
WINDOW = SEG * SUBLANES


def _scan_tables(ab_re, ab_im):
    pw = [None, (ab_re, ab_im)]
    for _ in range(2, WINDOW + 1):
        pr, pi_ = pw[-1]
        pw.append((pr * ab_re - pi_ * ab_im, pr * ab_im + pi_ * ab_re))
    rows = np.arange(SUBLANES)[:, None]
    ones = np.ones((SUBLANES, 1), np.float32)
    conj = lambda p: (p[0], -p[1])
    fwd, bwd = [], []
    for shift in (1, 2, 4):
        fwd.append(_state_layout(*pw[SEG * shift])[None, :] * (rows >= shift).astype(np.float32))
        bwd.append(_state_layout(*conj(pw[SEG * shift]))[None, :] * (rows < SUBLANES - shift).astype(np.float32))
    fwd.append(jnp.stack([_state_layout(*pw[SEG * (r + 1)]) for r in range(SUBLANES)]))
    bwd.append(jnp.stack([_state_layout(*conj(pw[SEG * (SUBLANES - r)])) for r in range(SUBLANES)]))
    for k in range(1, SEG):
        fwd.append(_state_layout(*pw[k])[None, :] * ones)
        bwd.append(_state_layout(*conj(pw[k]))[None, :] * ones)
    return jnp.stack(fwd), jnp.stack(bwd)


_EYE = np.eye(GROUPS_PER_SUPER, dtype=np.float32)


def _b_matrix(bb_re, bb_im):
    bb = jnp.stack([bb_re, bb_im]).reshape(2, N_SUPER, GROUPS_PER_SUPER, SSM_STATE, SSM_GROUP)
    m = jnp.einsum('rsgpc,gh->sgcrhp', bb, _EYE)
    return m.reshape(N_SUPER, SUPER_IN, SUPER_W)


def _b_matrix_grad(dm):
    d = dm.reshape(N_SUPER, GROUPS_PER_SUPER, SSM_GROUP, 2, GROUPS_PER_SUPER, SSM_STATE)
    d = jnp.sum(d * _EYE[None, :, None, None, :, None], axis=4)
    d = d.transpose(3, 0, 1, 4, 2).reshape(2, SSM_GROUPS, SSM_STATE, SSM_GROUP)
    return d[0], d[1]


def _c_matrix(c_re, c_im):
    cc = jnp.stack([c_re, -c_im]).reshape(2, N_SUPER, GROUPS_PER_SUPER, SSM_GROUP, SSM_STATE)
    m = jnp.einsum('rsgcp,gh->srgphc', cc, _EYE)
    return m.reshape(N_SUPER, SUPER_W, SUPER_IN)


def _c_matrix_grad(dm):
    d = dm.reshape(N_SUPER, 2, GROUPS_PER_SUPER, SSM_STATE, GROUPS_PER_SUPER, SSM_GROUP)
    d = jnp.sum(d * _EYE[None, None, :, None, :, None], axis=4)
    d = d.transpose(1, 0, 2, 4, 3).reshape(2, SSM_GROUPS, SSM_GROUP, SSM_STATE)
    return d[0], -d[1]


def _cmul_add(xr, xi, ar, ai, sr, si):
    return xr + ar * sr - ai * si, xi + ar * si + ai * sr


def _scan_rows(buf_ref, tab_ref, carry_ref, n_windows, reverse, h_ref=None, da_ref=None):
    order = list(range(SEG - 1, -1, -1)) if reverse else list(range(SEG))
    near = SUBLANES - 1 if reverse else 0
    far = 0 if reverse else SUBLANES - 1
    s_in = SUBLANES - 1 if reverse else 1
    lanes = lambda tile: pl.ds(tile * LANES, LANES)

    def window(w0, tile_re, tile_im, c_re, c_im, acc):
        rows = lambda t: pl.ds(w0 + t, SUBLANES, stride=SEG)
        get = lambda ref, t: (ref.at[tile_re][rows(t), :], ref.at[tile_im][rows(t), :])
        tab = lambda k: (tab_ref[k, :, lanes(tile_re)], tab_ref[k, :, lanes(tile_im)])

        def put(t, xr, xi):
            buf_ref.at[tile_re][rows(t), :] = xr
            buf_ref.at[tile_im][rows(t), :] = xi

        a1 = tab(4)
        er, ei = get(buf_ref, order[0])
        for t in order[1:]:
            er, ei = _cmul_add(*get(buf_ref, t), *a1, er, ei)
            if t != order[-1]:
                put(t, er, ei)
        for k, shift in enumerate((1, 2, 4)):
            s = (SUBLANES - shift) if reverse else shift
            er, ei = _cmul_add(er, ei, *tab(k), pltpu.roll(er, s, 0), pltpu.roll(ei, s, 0))
        er, ei = _cmul_add(er, ei, *tab(3), c_re, c_im)
        put(order[-1], er, ei)
        sub = lax.broadcasted_iota(jnp.int32, er.shape, 0)
        in_re = jnp.where(sub == near, c_re, pltpu.roll(er, s_in, 0))
        in_im = jnp.where(sub == near, c_im, pltpu.roll(ei, s_in, 0))
        true = {order[-1]: (er, ei)}
        for idx, t in enumerate(order[:-1]):
            true[t] = _cmul_add(*get(buf_ref, t), *tab(4 + idx), in_re, in_im)
            put(t, *true[t])
        carry = (jnp.broadcast_to(er[far:far + 1], er.shape), jnp.broadcast_to(ei[far:far + 1], ei.shape))
        if acc is None:
            return carry, None
        acc_re, acc_im = acc
        for t in range(SEG):
            if t + 1 < SEG:
                gr, gim = true[t + 1]
            else:
                gr = jnp.where(sub == SUBLANES - 1, c_re, pltpu.roll(true[0][0], SUBLANES - 1, 0))
                gim = jnp.where(sub == SUBLANES - 1, c_im, pltpu.roll(true[0][1], SUBLANES - 1, 0))
            hr, hi = get(h_ref, t)
            acc_re = acc_re + gr * hr + gim * hi
            acc_im = acc_im + gim * hr - gr * hi
        return carry, (acc_re, acc_im)

    half = SUPER_HALF // LANES
    per = 2 if h_ref is None else 4
    for sb in range(N_SUPER):
        pairs = [(2 * half * sb + j, 2 * half * sb + half + j) for j in range(half)]

        def step(wi, state, pairs=pairs):
            w = (n_windows - 1 - wi) if reverse else wi
            w0 = pl.multiple_of(w * WINDOW, WINDOW)
            out = []
            for j, (tile_re, tile_im) in enumerate(pairs):
                mine = state[per * j:per * (j + 1)]
                carry, acc = window(w0, tile_re, tile_im, mine[0], mine[1], mine[2:] or None)
                out += list(carry) + list(acc or ())
            return tuple(out)

        init = []
        for tile_re, tile_im in pairs:
            init += [carry_ref[:, lanes(tile_re)], carry_ref[:, lanes(tile_im)]]
            if h_ref is not None:
                init += [da_ref[:, lanes(tile_re)], da_ref[:, lanes(tile_im)]]
        fin = lax.fori_loop(0, n_windows, step, tuple(init))
        for j, (tile_re, tile_im) in enumerate(pairs):
            carry_ref[:, lanes(tile_re)] = fin[per * j]
            carry_ref[:, lanes(tile_im)] = fin[per * j + 1]
            if h_ref is not None:
                da_ref[:, lanes(tile_re)] = fin[per * j + 2]
                da_ref[:, lanes(tile_im)] = fin[per * j + 3]


def _put_tiles(ref, sb, block):
    for j in range(SUPER_TILES):
        ref[sb * SUPER_TILES + j] = block[:, j * LANES:(j + 1) * LANES]


def _get_tiles(ref, sb):
    return jnp.concatenate([ref[sb * SUPER_TILES + j] for j in range(SUPER_TILES)], axis=1)


def _ssm_fwd(u, bmat, cmat, tab, d_skip, tb, exchange=None):
    seq = u.shape[0]

    def body(u_ref, b_ref, c_ref, t_ref, d_ref, s_ref, h_ref, carry_ref):
        @pl.when(pl.program_id(0) == 0)
        def _():
            carry_ref[...] = jnp.zeros_like(carry_ref)

        u_blk = u_ref[...]
        ub = _bf(u_blk)
        for sb in range(N_SUPER):
            _put_tiles(h_ref, sb, _mm(ub[:, sb * SUPER_IN:(sb + 1) * SUPER_IN], b_ref[sb]))
        _scan_rows(h_ref, t_ref, carry_ref, tb // WINDOW, False)
        ys = [_mm(_bf(_get_tiles(h_ref, sb)), c_ref[sb]) for sb in range(N_SUPER)]
        s_ref[...] = jnp.concatenate(ys, axis=1) + d_ref[...] * u_blk

    return _rowcall("ssm_fwd", body, seq, tb, [u], [bmat, cmat, tab, d_skip],
                    [(SSM_W, F32), ((STATE_TILES, LANES), F32)], [],
                    scratch=[pltpu.VMEM((SUBLANES, STATE_COLS), F32)], vmem=VMEM_BIG, exchange=exchange)


def _ssm_bwd(ds, u, h, bmat_t, cmat_t, tab, d_skip, tb, exchange=None):
    seq = u.shape[0]

    def body(ds_ref, u_ref, h_ref, bt_ref, ct_ref, t_ref, d_ref,
             du_ref, db_ref, dc_ref, da_ref, dd_ref, g_ref, carry_ref):
        @pl.when(pl.program_id(0) == 0)
        def _():
            carry_ref[...] = jnp.zeros_like(carry_ref)
            db_ref[...] = jnp.zeros_like(db_ref)
            dc_ref[...] = jnp.zeros_like(dc_ref)
            da_ref[...] = jnp.zeros_like(da_ref)
            dd_ref[...] = jnp.zeros_like(dd_ref)

        ds_blk = ds_ref[...]
        dsb = _bf(ds_blk)
        u_blk = u_ref[...]
        ub = _bf(u_blk)
        for sb in range(N_SUPER):
            _put_tiles(g_ref, sb, _mm(dsb[:, sb * SUPER_IN:(sb + 1) * SUPER_IN], ct_ref[sb]))
        _scan_rows(g_ref, t_ref, carry_ref, tb // WINDOW, True, h_ref=h_ref, da_ref=da_ref)
        dus = []
        for sb in range(N_SUPER):
            gb = _bf(_get_tiles(g_ref, sb))
            dus.append(_mm(gb, bt_ref[sb]))
            db_ref[sb] += _mm_tn(ub[:, sb * SUPER_IN:(sb + 1) * SUPER_IN], gb)
            dc_ref[sb] += _mm_tn(_bf(_get_tiles(h_ref, sb)), dsb[:, sb * SUPER_IN:(sb + 1) * SUPER_IN])
        du_ref[...] = jnp.concatenate(dus, axis=1) + d_ref[...] * ds_blk
        dd_ref[...] += jnp.sum(ds_blk * u_blk, axis=0, keepdims=True)

    return _rowcall("ssm_bwd", body, seq, tb, [ds, u, h], [bmat_t, cmat_t, tab, d_skip],
                    [(SSM_W, F32)],
                    [((N_SUPER, SUPER_IN, SUPER_W), F32), ((N_SUPER, SUPER_W, SUPER_IN), F32),
                     ((SUBLANES, STATE_COLS), F32), ((1, SSM_W), F32)],
                    scratch=[pltpu.VMEM((STATE_TILES, tb, LANES), F32), pltpu.VMEM((SUBLANES, STATE_COLS), F32)],
                    reverse=True, vmem=VMEM_BIG, exchange=exchange)


def _merge_core(s, attb, ga, gs, wg_ref, wab_ref, wsb_ref, wout_ref):
    zg, dgelu = _gelu_and_grad(s)
    zgb = _bf(zg)
    sg = _sig(_mm(zgb, wg_ref[...]))
    z = zg * sg
    zb = _bf(z)
    ys = jnp.concatenate([_mm(zb, wsb_ref[j]) for j in range(N_CHIPS)], axis=1)
    ya = jnp.concatenate([_mm(attb, wab_ref[j]) for j in range(N_CHIPS)], axis=1)
    sa = _sig(ga)
    ss = _sig(gs)
    mgb = _bf(sa * ya + ss * ys)
    o = _mm(mgb, wout_ref[...])
    return dict(zg=zg, dgelu=dgelu, zgb=zgb, sg=sg, zb=zb, ys=ys, ya=ya, sa=sa, ss=ss, mgb=mgb, o=o)


def _merge_fwd(x, s, att, ga, gs, g2, w_glu, w_ab, w_sb, w_out, tb):
    seq = x.shape[0]

    def body(x_ref, s_ref, att_ref, ga_ref, gs_ref, g_ref, wg_ref, wab_ref, wsb_ref, wout_ref, x2_ref):
        f = _merge_core(s_ref[...], att_ref[...], ga_ref[...], gs_ref[...], wg_ref, wab_ref, wsb_ref, wout_ref)
        n, _, _ = _rms(f["o"], g_ref[...])
        x2_ref[...] = x_ref[...] + n

    return _rowcall("merge_fwd", body, seq, tb, [x, s, att, ga, gs], [g2, w_glu, w_ab, w_sb, w_out],
                    [(D_MODEL, F32)], [], vmem=VMEM_BIG)[0]


def _merge_bwd(dx2, s, att, ga, gs, g2, w_glu, w_ab, w_sb, w_out, tb, exchange=None):
    seq = s.shape[0]
    cw = D_MODEL // N_CHIPS
    last = seq // tb - 1

    def body(dx2_ref, s_ref, att_ref, ga_ref, gs_ref, g_ref, wg_ref, wab_ref, wsb_ref, wout_ref,
             ds_ref, datt_ref, dga_ref, dgs_ref, dg_ref, dwg_ref, dwab_ref, dwsb_ref, dwout_ref,
             bwg_ref, bwab_ref, bwsb_ref, bwout_ref):
        @pl.when(pl.program_id(0) == 0)
        def _():
            for r in (dg_ref, dwg_ref, dwab_ref, dwsb_ref, dwout_ref):
                r[...] = jnp.zeros_like(r)

        attb = att_ref[...]
        f = _merge_core(s_ref[...], attb, ga_ref[...], gs_ref[...], wg_ref, wab_ref, wsb_ref, wout_ref)
        g = g_ref[...]
        _, oh, r2 = _rms(f["o"], g)
        do, dg = _rms_bwd(dx2_ref[...], oh, r2, g)
        dg_ref[...] += dg
        dob = _bf(do)
        dwout_ref[...] += _mm_tn(f["mgb"], dob)
        dmg = _mm_nt(dob, wout_ref[...])
        sa, ss = f["sa"], f["ss"]
        dyab = _bf(dmg * sa)
        dysb = _bf(dmg * ss)
        dga_ref[...] = _bf(dmg * f["ya"] * sa * (1.0 - sa))
        dgs_ref[...] = _bf(dmg * f["ys"] * ss * (1.0 - ss))
        dwab = _mm_tn(attb, dyab)
        dwsb = _mm_tn(f["zb"], dysb)
        datt = jnp.zeros((tb, ATTN_W), F32)
        dz = jnp.zeros((tb, SSM_W), F32)
        for j in range(N_CHIPS):
            dwab_ref[j] += dwab[:, j * cw:(j + 1) * cw]
            dwsb_ref[j] += dwsb[:, j * cw:(j + 1) * cw]
            datt = datt + _mm_nt(dyab[:, j * cw:(j + 1) * cw], wab_ref[j])
            dz = dz + _mm_nt(dysb[:, j * cw:(j + 1) * cw], wsb_ref[j])
        datt_ref[...] = _bf(datt)
        sg, zg = f["sg"], f["zg"]
        dglb = _bf(dz * zg * sg * (1.0 - sg))
        dwg_ref[...] += _mm_tn(f["zgb"], dglb)
        dzg = dz * sg + _mm_nt(dglb, wg_ref[...])
        ds_ref[...] = dzg * f["dgelu"]

        @pl.when(pl.program_id(0) == last)
        def _():
            for dst, src in ((bwg_ref, dwg_ref), (bwab_ref, dwab_ref), (bwsb_ref, dwsb_ref), (bwout_ref, dwout_ref)):
                dst[...] = _bf(src[...])

    shapes = [w_glu.shape, w_ab.shape, w_sb.shape, w_out.shape]
    return _rowcall("merge_bwd", body, seq, tb, [dx2, s, att, ga, gs], [g2, w_glu, w_ab, w_sb, w_out],
                    [(SSM_W, F32), (ATTN_W, BF16), (D_MODEL, BF16), (D_MODEL, BF16)],
                    [((1, D_MODEL), F32)] + [(sh, F32) for sh in shapes] + [(sh, BF16) for sh in shapes],
                    vmem=VMEM_BIG, exchange=exchange)


def _mlp_fwd_loss(x2, target, g3, g4, w_ffi, w_ffo, tb):
    seq = x2.shape[0]
    n_slab = len(w_ffi)
    sw = D_FF // FF_CHUNKS // n_slab

    def body(x2_ref, t_ref, g3_ref, g4_ref, *rest):
        wi_refs, (wo_ref, dy_ref, df_ref, h_ref, loss_ref, dg_ref) = rest[:n_slab], rest[n_slab:]

        @pl.when(pl.program_id(0) == 0)
        def _():
            loss_ref[...] = jnp.zeros_like(loss_ref)
            dg_ref[...] = jnp.zeros_like(dg_ref)

        x2_blk = x2_ref[...]
        h3, _, _ = _rms(x2_blk, g3_ref[...])
        hb = _bf(h3)
        h_ref[...] = hb
        f = jnp.zeros((tb, D_MODEL), F32)
        for j in range(FF_CHUNKS):
            for k in range(n_slab):
                a = _mm(hb, wi_refs[k][j])
                f = f + _mm(_bf(jnp.square(jnp.maximum(a, 0.0))), wo_ref[j, pl.ds(k * sw, sw), :])
        g4 = g4_ref[...]
        n4, fh, r4 = _rms(f, g4)
        e = (x2_blk + n4) - t_ref[...]
        loss_ref[...] += 0.5 * jnp.sum(jnp.mean(e * e, axis=-1, keepdims=True))
        dy = e * (1.0 / D_MODEL)
        dy_ref[...] = dy
        df, dg = _rms_bwd(dy, fh, r4, g4)
        df_ref[...] = _bf(df)
        dg_ref[...] += dg

    return _rowcall("mlp_fwd_loss", body, seq, tb, [x2, target], [g3, g4, *w_ffi, w_ffo],
                    [(D_MODEL, F32), (D_MODEL, BF16), (D_MODEL, BF16)],
                    [((SUBLANES, 128), F32), ((1, D_MODEL), F32)], vmem=VMEM_BIG)


def _mlp_bwd(x2, dy, df, h3, g3, w_ffi, w_ffo, tb):
    seq = x2.shape[0]
    n_slab = len(w_ffi)
    sw = D_FF // FF_CHUNKS // n_slab

    def body(x2_ref, dy_ref, df_ref, h_ref, g3_ref, *rest):
        wi_refs, (wo_ref, dx_ref, act_ref, da_ref, dg_ref) = rest[:n_slab], rest[n_slab:]

        @pl.when(pl.program_id(0) == 0)
        def _():
            dg_ref[...] = jnp.zeros_like(dg_ref)

        hb = h_ref[...]
        dfb = df_ref[...]
        dh = jnp.zeros((tb, D_MODEL), F32)
        for j in range(FF_CHUNKS):
            for k in range(n_slab):
                cols = pl.ds((j * n_slab + k) * sw, sw)
                ra = jnp.maximum(_mm(hb, wi_refs[k][j]), 0.0)
                act_ref[:, cols] = _bf(ra * ra)
                dab = _bf(_mm_nt(dfb, wo_ref[j, pl.ds(k * sw, sw), :]) * (2.0 * ra))
                da_ref[:, cols] = dab
                dh = dh + _mm_nt(dab, wi_refs[k][j])
        g3 = g3_ref[...]
        _, xh, r3 = _rms(x2_ref[...], g3)
        dxn, dg = _rms_bwd(dh, xh, r3, g3)
        dx_ref[...] = dy_ref[...] + dxn
        dg_ref[...] += dg

    return _rowcall("mlp_bwd", body, seq, tb, [x2, dy, df, h3], [g3, *w_ffi, w_ffo],
                    [(D_MODEL, F32), (D_FF, BF16), (D_FF, BF16)], [((1, D_MODEL), F32)], vmem=VMEM_BIG)


def _matmul_tn(name, a, b, tk, tn, tl, chunk_major, exchange=None):
    seq, kdim = a.shape
    ndim = b.shape[1]
    last = seq // tl - 1

    def body(a_ref, b_ref, o_ref, ob_ref):
        @pl.when(pl.program_id(2) == 0)
        def _():
            o_ref[...] = jnp.zeros_like(o_ref)

        o_ref[...] += _mm_tn(a_ref[...], b_ref[...])

        @pl.when(pl.program_id(2) == last)
        def _():
            ob_ref[...] = _bf(o_ref[...])

    if chunk_major:
        shape = (ndim // tn, kdim, tn)
        out_spec = pl.BlockSpec((None, tk, tn), lambda k, n, l: (n, k, 0))
    else:
        shape = (kdim, ndim)
        out_spec = pl.BlockSpec((tk, tn), lambda k, n, l: (k, n))
    return _fused_call(
        name, body, (kdim // tk, ndim // tn, seq // tl),
        [pl.BlockSpec((tl, tk), lambda k, n, l: (l, k)), pl.BlockSpec((tl, tn), lambda k, n, l: (l, n))],
        [out_spec, out_spec], [SDS(shape, F32), SDS(shape, BF16)], [], [a, b], exchange, _params(3, VMEM_BIG))


def _ew_call(name, fn, ins, n_out, exchange=None):
    rows, cols = ins[0].shape
    tr = rows
    while tr * cols * 4 > min(1 << 20, (9 << 20) // (len(ins) + n_out)) and tr % 16 == 0:
        tr //= 2
    spec = pl.BlockSpec((tr, cols), lambda i: (i, 0))

    def body(*refs):
        outs = fn(*[r[...] for r in refs[:len(ins)]])
        for r, o in zip(refs[len(ins):], outs):
            r[...] = o

    return _fused_call(name, body, (rows // tr,), [spec] * len(ins), [spec] * n_out,
                       [SDS((rows, cols), F32)] * n_out, [], list(ins), exchange, _params(1))


def _adam_math(w, g, m, v):
    m2 = ADAM_B1 * m + (1.0 - ADAM_B1) * g
    v2 = ADAM_B2 * v + (1.0 - ADAM_B2) * (g * g)
    m_hat = m2 / (1.0 - ADAM_B1 ** ADAM_STEP)
    v_hat = v2 / (1.0 - ADAM_B2 ** ADAM_STEP)
    delta = -ADAM_LR * (m_hat / (jnp.sqrt(v_hat) + ADAM_EPS) + ADAM_WD * w)
    return delta, m2, v2


def _sum4(name, own, recv, idx):
    _, rows, cols = own.shape
    tr = rows
    while tr * cols * 4 > (1 << 20) and tr % 16 == 0:
        tr //= 2

    def body(idx_ref, o_ref, r0_ref, r1_ref, r2_ref, out_ref):
        out_ref[...] = ((o_ref[...] + r0_ref[...].astype(F32)) + r1_ref[...].astype(F32)) + r2_ref[...].astype(F32)

    blk = (None, tr, cols)
    grid_spec = pltpu.PrefetchScalarGridSpec(
        num_scalar_prefetch=1, grid=(rows // tr,),
        in_specs=[pl.BlockSpec(blk, lambda i, s: (s[0], i, 0)), pl.BlockSpec(blk, lambda i, s: (0, i, 0)),
                  pl.BlockSpec(blk, lambda i, s: (1, i, 0)), pl.BlockSpec(blk, lambda i, s: (2, i, 0))],
        out_specs=pl.BlockSpec((tr, cols), lambda i, s: (i, 0)))
    return pl.pallas_call(body, grid_spec=grid_spec, out_shape=SDS((rows, cols), F32), name=name,
                          compiler_params=_params(1))(jnp.reshape(idx, (1,)).astype(jnp.int32), own, recv, recv, recv)


def _adam_pair(name, items, exchange=None):
    def fn(*vals):
        outs = ()
        for i in range(len(items)):
            w_, a, b, m_, v_ = vals[5 * i:5 * i + 5]
            g = a + b
            outs += (g,) + _adam_math(w_, g, m_, v_)
        return outs

    flat = _ew_call(name, fn, [a for item in items for a in item], 4 * len(items), exchange)
    return [flat[4 * i:4 * i + 4] for i in range(len(items))] + [list(flat[4 * len(items):])]


def _place():
    return lax.axis_index("x"), lax.axis_index("y"), lax.axis_index("c")


def _other_chips(x, y):
    return [(1 - x, y), (x, 1 - y), (1 - x, 1 - y)]


def _gather_chips(shards):
    n = len(shards)

    def copies(ins, outs, sems):
        send, recv, fwd_send, fwd_recv, loc = sems
        x, y, c = _place()
        me = 2 * x + y
        peers = _other_chips(x, y)
        local = [pltpu.make_async_copy(ins[a], outs[a].at[me], loc.at[a]) for a in range(n)]
        sends, recvs, passes, passed = [], [], [], []
        for a in range(n):
            half = shards[a].shape[0] // 2
            mine = pl.ds(c * half, half)
            theirs = pl.ds((1 - c) * half, half)
            for j, (px, py) in enumerate(peers):
                far = 2 * px + py
                sends.append(pltpu.make_async_remote_copy(
                    src_ref=ins[a].at[mine], dst_ref=outs[a].at[me, mine], send_sem=send.at[a, j],
                    recv_sem=recv.at[a, j], device_id=(px, py, c), device_id_type=MESH_ID))
                recvs.append(pltpu.make_async_remote_copy(
                    src_ref=ins[a].at[mine], dst_ref=outs[a].at[far, mine], send_sem=send.at[a, j],
                    recv_sem=recv.at[a, j], device_id=(px, py, c), device_id_type=MESH_ID))
                passes.append(pltpu.make_async_remote_copy(
                    src_ref=outs[a].at[far, mine], dst_ref=outs[a].at[far, mine], send_sem=fwd_send.at[a, j],
                    recv_sem=fwd_recv.at[a, j], device_id=(x, y, 1 - c), device_id_type=MESH_ID))
                passed.append(pltpu.make_async_remote_copy(
                    src_ref=outs[a].at[far, theirs], dst_ref=outs[a].at[far, theirs], send_sem=fwd_send.at[a, j],
                    recv_sem=fwd_recv.at[a, j], device_id=(x, y, 1 - c), device_id_type=MESH_ID))
        return local, sends, recvs, passes, passed

    def start(ins, outs, sems):
        local, sends, _, _, _ = copies(ins, outs, sems)
        for cp in local + sends:
            cp.start()

    def wait(ins, outs, sems):
        local, sends, recvs, passes, passed = copies(ins, outs, sems)
        for got, on in zip(recvs, passes):
            got.wait_recv()
            on.start()
        for cp in passed:
            cp.wait_recv()
        for cp in passes + sends:
            cp.wait_send()
        for cp in local:
            cp.wait()

    assert all(s.shape[0] % 32 == 0 for s in shards)
    pair = pltpu.SemaphoreType.DMA((n, 3))
    return _Exchange(shards, [SDS((N_CHIPS,) + s.shape, s.dtype) for s in shards],
                     [pair, pair, pair, pair, pltpu.SemaphoreType.DMA((n,))], start, wait)


def _scatter_chips(chunks):
    n = len(chunks)

    def copies(ins, outs, sems):
        send, recv = sems
        x, y, c = _place()
        return [pltpu.make_async_remote_copy(
            src_ref=ins[a].at[2 * px + py], dst_ref=outs[a].at[j], send_sem=send.at[a, j],
            recv_sem=recv.at[a, j], device_id=(px, py, c), device_id_type=MESH_ID)
            for a in range(n) for j, (px, py) in enumerate(_other_chips(x, y))]

    def start(ins, outs, sems):
        for cp in copies(ins, outs, sems):
            cp.start()

    def wait(ins, outs, sems):
        cps = copies(ins, outs, sems)
        for cp in cps:
            cp.wait_recv()
        for cp in cps:
            cp.wait_send()

    return _Exchange(chunks, [SDS((3,) + s.shape[1:], s.dtype) for s in chunks],
                     [pltpu.SemaphoreType.DMA((n, 3)), pltpu.SemaphoreType.DMA((n, 3))], start, wait)


def _swap_sibling(arrs):
    n = len(arrs)

    def copies(ins, outs, sems):
        send, recv = sems
        x, y, c = _place()
        return [pltpu.make_async_remote_copy(
            src_ref=ins[a], dst_ref=outs[a], send_sem=send.at[a], recv_sem=recv.at[a],
            device_id=(x, y, 1 - c), device_id_type=MESH_ID) for a in range(n)]

    def start(ins, outs, sems):
        for cp in copies(ins, outs, sems):
            cp.start()

    def wait(ins, outs, sems):
        cps = copies(ins, outs, sems)
        for cp in cps:
            cp.wait_recv()
        for cp in cps:
            cp.wait_send()

    return _Exchange(arrs, [SDS(s.shape, s.dtype) for s in arrs],
                     [pltpu.SemaphoreType.DMA((n,)), pltpu.SemaphoreType.DMA((n,))], start, wait)


N_DEV = 8


def _gather_devices(block):
    def copies(ins, outs, sems):
        send, recv, loc = sems
        x, y, c = _place()
        me = 4 * x + 2 * y + c
        local = pltpu.make_async_copy(ins[0], outs[0].at[me], loc.at[0])
        sends, recvs = [], []
        for k in range(1, N_DEV):
            peer = (x ^ (k >> 2), y ^ ((k >> 1) & 1), c ^ (k & 1))
            for group, slot in ((sends, me), (recvs, me ^ k)):
                group.append(pltpu.make_async_remote_copy(
                    src_ref=ins[0], dst_ref=outs[0].at[slot], send_sem=send.at[k - 1], recv_sem=recv.at[k - 1],
                    device_id=peer, device_id_type=MESH_ID))
        return local, sends, recvs

    def start(ins, outs, sems):
        local, sends, _ = copies(ins, outs, sems)
        for cp in [local] + sends:
            cp.start()

    def wait(ins, outs, sems):
        local, sends, recvs = copies(ins, outs, sems)
        for cp in recvs:
            cp.wait_recv()
        for cp in sends:
            cp.wait_send()
        local.wait()

    return _Exchange([block], [SDS((N_DEV,) + block.shape, block.dtype)],
                     [pltpu.SemaphoreType.DMA((N_DEV - 1,)), pltpu.SemaphoreType.DMA((N_DEV - 1,)),
                      pltpu.SemaphoreType.DMA((1,))], start, wait)


def _both(ex_a, ex_b):
    na_i, na_o, na_s = len(ex_a.ins), len(ex_a.outs), len(ex_a.sems)

    def start(ins, outs, sems):
        ex_a.start(ins[:na_i], outs[:na_o], sems[:na_s])
        ex_b.start(ins[na_i:], outs[na_o:], sems[na_s:])

    def wait(ins, outs, sems):
        ex_a.wait(ins[:na_i], outs[:na_o], sems[:na_s])
        ex_b.wait(ins[na_i:], outs[na_o:], sems[na_s:])

    return _Exchange(ex_a.ins + ex_b.ins, ex_a.outs + ex_b.outs, ex_a.sems + ex_b.sems, start, wait)


def _sum_devices(slots):
    def body(s_ref, o_ref):
        acc = s_ref[0]
        for d in range(1, N_DEV):
            acc = acc + s_ref[d]
        o_ref[...] = acc

    return pl.pallas_call(
        body, in_specs=[pl.BlockSpec(memory_space=pltpu.VMEM)], out_specs=pl.BlockSpec(memory_space=pltpu.VMEM),
        out_shape=SDS(slots.shape[1:], F32), name="sum_small",
        compiler_params=pltpu.CompilerParams(vmem_limit_bytes=32 * 1024 * 1024))(slots)


def _adam_small(ws, gs, ms, vs):
    n = len(ws)

    def body(*refs):
        for i in range(n):
            w_ref, g_ref, m_ref, v_ref = (refs[k * n + i] for k in range(4))
            outs = _adam_math(w_ref[...], g_ref[...], m_ref[...], v_ref[...])
            for k in range(3):
                refs[(4 + k) * n + i][...] = outs[k]

    vmem = pl.BlockSpec(memory_space=pltpu.VMEM)
    return pl.pallas_call(
        body, in_specs=[vmem] * (4 * n), out_specs=[vmem] * (3 * n),
        out_shape=[SDS(w.shape, F32) for w in ws] * 3, name="adam_small",
        compiler_params=pltpu.CompilerParams(vmem_limit_bytes=32 * 1024 * 1024))(*ws, *gs, *ms, *vs)


def _local_step(x, target, small, big, tb, distributed):
    g1, g2, g3, g4 = small["norm_mix_pre"], small["norm_mix_post"], small["norm_mlp_pre"], small["norm_mlp_post"]
    dist = distributed
    me = (2 * lax.axis_index("x") + lax.axis_index("y")) if dist else 0
    tb_ssm = min(tb, 256)
    bucket = jnp.asarray(_bucket_table())

    keys_first = lambda t: jnp.swapaxes(t, -1, -2)
    bias = _bias_table(small["rel_bias"], bucket)
    sink_rows = keys_first(_pair_layout(jnp.broadcast_to(small["sinks"].reshape(N_HEADS, 1, 1), (N_HEADS, BLOCK, 1))))
    disc_args = (small["lam_re"], small["lam_im"], small["log_dt"], small["b_re"], small["b_im"])
    (ab_re, ab_im, bb_re, bb_im), disc_vjp = jax.vjp(_ssm_discretize, *disc_args)
    tab_f, tab_b = _scan_tables(ab_re, ab_im)
    bmat = _bf(_b_matrix(bb_re, bb_im))
    cmat = _bf(_c_matrix(small["c_re"], small["c_im"]))
    d_skip = small["d_skip"]

    if dist:
        (g_in,) = _exchange_alone("gather_w_in", _gather_chips([big["w_in"]]))
        w_in = g_in.reshape(IN_W, D_MODEL)
    else:
        w_in = big["w_in"]
    mix = ("w_glu", "w_attn_branch", "w_ssm_branch", "w_out")
    slab = D_FF // N_CHIPS // 2
    ffi_l, ffi_r = big["w_ff_in"][..., :slab], big["w_ff_in"][..., slab:]
    outs = _inproj_fwd(x, g1, w_in, tb, _gather_chips([big[n] for n in mix] + [ffi_l]) if dist else None)
    h1, q, k, v, u, ga, gs = outs[:7]
    w_glu, w_ab, w_sb, w_out, ffi_l = outs[7:] if dist else [big[n] for n in mix] + [ffi_l]
    w_glu = w_glu.reshape(SSM_W, SSM_W)
    w_out = w_out.reshape(D_MODEL, D_MODEL)
    outs = _attn_fwd(q, k, v, bias, sink_rows, _gather_chips([ffi_r]) if dist else None)
    att = outs[0]
    w_ffi = [ffi_l, outs[1] if dist else ffi_r]
    outs = _ssm_fwd(u, bmat, cmat, tab_f, d_skip, tb_ssm, _gather_chips([big["w_ff_out"]]) if dist else None)
    s, h = outs[:2]
    w_ffo = outs[2] if dist else big["w_ff_out"]
    x2 = _merge_fwd(x, s, att, ga, gs, g2, w_glu, w_ab, w_sb, w_out, tb)
    dy, df, h3, loss_acc, dg4 = _mlp_fwd_loss(x2, target, g3, g4, w_ffi, w_ffo, tb)

    dx2, act, da, dg3 = _mlp_bwd(x2, dy, df, h3, g3, w_ffi, w_ffo, tb)
    tl = min(2048, x.shape[0])
    chunked = (N_CHIPS, D_FF // N_CHIPS, D_MODEL)
    d_ffi, b_ffi = _matmul_tn("grad_w_ff_in", h3, da, D_MODEL, D_FF // FF_CHUNKS, tl, True)
    d_ffo, b_ffo = _matmul_tn("grad_w_ff_out", act, df, D_FF // FF_CHUNKS, D_MODEL, tl, False)
    d_ffo, b_ffo = d_ffo.reshape(chunked), b_ffo.reshape(chunked)
    outs = _merge_bwd(dx2, s, att, ga, gs, g2, w_glu, w_ab, w_sb, w_out, tb_ssm,
                      _scatter_chips([b_ffi]) if dist else None)
    ds, datt, dga, dgs, dg2, d_glu, d_ab, d_sb, d_out, b_glu, b_ab, b_sb, b_out = outs[:13]
    r_ffi = outs[13:]
    glu4, out4 = (N_CHIPS, SSM_W // N_CHIPS, SSM_W), (N_CHIPS, D_MODEL // N_CHIPS, D_MODEL)
    d_mix = [d_glu.reshape(glu4), d_ab, d_sb, d_out.reshape(out4)]
    b_mix = [b_glu.reshape(glu4), b_ab, b_sb, b_out.reshape(out4)]
    outs = _ssm_bwd(ds, u, h, bmat.transpose(0, 2, 1), cmat.transpose(0, 2, 1), tab_b, d_skip, tb_ssm,
                    _scatter_chips([b_ffo]) if dist else None)
    du, d_bmat, d_cmat, da_acc, dd_skip = outs[:5]
    r_ffo = outs[5:]
    outs = _attn_bwd(q, k, v, datt, bias, sink_rows, _scatter_chips(b_mix) if dist else None)
    dq, dk, dv, dbias, dsink_rows = outs[:5]
    r_mix = outs[5:]
    if dist:
        p_ffi = _sum4("sum_w_ff_in", d_ffi, r_ffi[0], me)
        p_ffo = _sum4("sum_w_ff_out", d_ffo, r_ffo[0], me)
    dx, dpj, dg1 = _inproj_bwd(x, dx2, dq, dk, dv, du, dga, dgs, g1, w_in, tb)

    dab_re, dab_im = _state_unlayout(jnp.sum(da_acc, axis=0))
    dbb_re, dbb_im = _b_matrix_grad(d_bmat)
    d_lam_re, d_lam_im, d_log_dt, d_b_re, d_b_im = disc_vjp((dab_re, dab_im, dbb_re, dbb_im))
    d_c_re, d_c_im = _c_matrix_grad(d_cmat)
    d_rel = _bias_grad(dbias, bucket)
    d_sinks = jnp.sum(_pair_unlayout(keys_first(dsink_rows)), axis=(1, 2))
    small_grads = dict(
        norm_mix_pre=dg1, norm_mix_post=dg2, norm_mlp_pre=dg3, norm_mlp_post=dg4, rel_bias=d_rel, sinks=d_sinks,
        lam_re=d_lam_re, lam_im=d_lam_im, log_dt=d_log_dt, b_re=d_b_re, b_im=d_b_im, c_re=d_c_re, c_im=d_c_im,
        d_skip=dd_skip)
    ride = _both(_swap_sibling([p_ffi, p_ffo]), _gather_devices(_pack(small_grads, loss_acc))) if dist else None
    outs = _matmul_tn("grad_w_in", dpj, h1, IN_W // 2, D_MODEL, tl, False, ride)
    in4 = (N_CHIPS, IN_W // N_CHIPS, D_MODEL)
    d_in, b_in = outs[0].reshape(in4), outs[1].reshape(in4)
    if not dist:
        return loss_acc, dx, small_grads, dict(zip(BIG, [d_in] + d_mix + [d_ffi, d_ffo]))
    s_ffi, s_ffo, slots = outs[2:]
    p_mix = [_sum4("sum_" + n, d, r, me) for n, d, r in zip(mix, d_mix, r_mix)]
    pending = dict(d_in=d_in, b_in=b_in, p_mix=p_mix, w_ff_in=(p_ffi, s_ffi), w_ff_out=(p_ffo, s_ffo), me=me)
    return loss_acc, dx, _sum_devices(slots), pending


SMALL = ['norm_mix_pre', 'norm_mix_post', 'norm_mlp_pre', 'norm_mlp_post', 'rel_bias', 'sinks', 'lam_re', 'lam_im',
         'log_dt', 'b_re', 'b_im', 'c_re', 'c_im', 'd_skip']
BIG = ['w_in', 'w_glu', 'w_attn_branch', 'w_ssm_branch', 'w_out', 'w_ff_in', 'w_ff_out']
WEIGHTS = ['norm_mix_pre', 'norm_mix_post', 'norm_mlp_pre', 'norm_mlp_post', 'w_in', 'rel_bias', 'sinks', 'lam_re',
           'lam_im', 'log_dt', 'b_re', 'b_im', 'c_re', 'c_im', 'd_skip', 'w_glu', 'w_attn_branch', 'w_ssm_branch',
           'w_out', 'w_ff_in', 'w_ff_out']
PACK_COLS = 1024
PACK_ORDER = ['b_re', 'b_im', 'c_re', 'c_im', 'lam_re', 'lam_im', 'norm_mix_pre', 'norm_mix_post', 'norm_mlp_pre',
              'norm_mlp_post', 'rel_bias', 'sinks', 'log_dt', 'd_skip']


STATE_MINOR = ('b_re', 'b_im')
PACK_ROWS = 144
LOSS_ROW = 140


def _pack(named, loss_acc):
    parts = []
    for n in PACK_ORDER:
        a = jnp.swapaxes(named[n], -1, -2) if n in STATE_MINOR else named[n]
        flat = a.reshape(-1)
        rows = -(-flat.shape[0] // PACK_COLS)
        parts.append(jnp.pad(flat, (0, rows * PACK_COLS - flat.shape[0])).reshape(rows, PACK_COLS))
    assert sum(p.shape[0] for p in parts) == LOSS_ROW
    parts.append(jnp.pad(loss_acc[0:1], ((0, PACK_ROWS - LOSS_ROW - 1), (0, PACK_COLS - loss_acc.shape[1]))))
    return jnp.concatenate(parts, axis=0)


def _unpack(packed, shapes):
    out, at = {}, 0
    for n in PACK_ORDER:
        shape = shapes[n][:-2] + (shapes[n][-1], shapes[n][-2]) if n in STATE_MINOR else shapes[n]
        size = int(np.prod(shape))
        rows = -(-size // PACK_COLS)
        blk = packed[at:at + rows]
        out[n] = (blk.reshape(-1)[:size] if size % PACK_COLS else blk).reshape(shape)
        at += rows
    return out


def kernel(x, norm_mix_pre, norm_mix_post, norm_mlp_pre, norm_mlp_post, w_in, rel_bias, sinks, lam_re, lam_im, log_dt, b_re, b_im, c_re, c_im, d_skip, w_glu, w_attn_branch, w_ssm_branch, w_out, w_ff_in, w_ff_out, loss_target, m_norm_mix_pre, m_norm_mix_post, m_norm_mlp_pre, m_norm_mlp_post, m_w_in, m_rel_bias, m_sinks, m_lam_re, m_lam_im, m_log_dt, m_b_re, m_b_im, m_c_re, m_c_im, m_d_skip, m_w_glu, m_w_attn_branch, m_w_ssm_branch, m_w_out, m_w_ff_in, m_w_ff_out, v_norm_mix_pre, v_norm_mix_post, v_norm_mlp_pre, v_norm_mlp_post, v_w_in, v_rel_bias, v_sinks, v_lam_re, v_lam_im, v_log_dt, v_b_re, v_b_im, v_c_re, v_c_im, v_d_skip, v_w_glu, v_w_attn_branch, v_w_ssm_branch, v_w_out, v_w_ff_in, v_w_ff_out):
    env = dict(locals())
    w = {n: env[n] for n in WEIGHTS}
    m = {n: env["m_" + n] for n in WEIGHTS}
    v = {n: env["v_" + n] for n in WEIGHTS}
    seq = x.shape[1]
    tb = min(512, seq)

    small = {n: w[n] for n in ('norm_mix_pre', 'norm_mix_post', 'norm_mlp_pre', 'norm_mlp_post', 'rel_bias')}
    small.update({n: w[n][0] for n in ('sinks', 'lam_re', 'lam_im', 'log_dt', 'b_re', 'b_im', 'c_re', 'c_im')})
    small['d_skip'] = w['d_skip']
    shard = lambda t, n: t[n][0].T if n == 'w_in' else t[n][0]
    unshard = lambda a, n: (a.T if n == 'w_in' else a)[None]
    _, dx, small_g, pending = _local_step(
        x[0], loss_target[0], small, {n: _bf(shard(w, n)) for n in BIG}, tb, True)

    loss = small_g[LOSS_ROW, 0]

    grads, deltas, new_m, new_v = {}, {}, {}, {}

    def adam(name, names, partials, exchange=None):
        items = [(shard(w, n), *partials[n], shard(m, n), shard(v, n)) for n in names]
        outs = _adam_pair(name, items, exchange)
        for n, res in zip(names, outs):
            grads[n], deltas[n], new_m[n], new_v[n] = [unshard(a, n) for a in res[:4]]
        return outs[len(names):]

    (r_in,), = adam("adam_w_ff", ("w_ff_in", "w_ff_out"), pending, _scatter_chips([pending["b_in"]]))
    mix = ("w_glu", "w_attn_branch", "w_ssm_branch", "w_out")
    parts = [_sum4("sum_w_in", pending["d_in"], r_in, pending["me"])] + pending["p_mix"]
    sibs = _exchange_alone("swap_rest", _swap_sibling(parts))
    partials = dict(zip(("w_in",) + mix, zip(parts, sibs)))
    for n in ("w_in",) + mix:
        adam("adam_" + n, (n,), partials)

    minor = lambda t, n: jnp.swapaxes(t, -1, -2) if n in STATE_MINOR else t
    g_small = _unpack(small_g, {n: w[n].shape for n in SMALL})
    outs = _adam_small([minor(w[n], n) for n in SMALL], [g_small[n] for n in SMALL],
                       [minor(m[n], n) for n in SMALL], [minor(v[n], n) for n in SMALL])
    grads.update({n: minor(g_small[n], n) for n in SMALL})
    for k, dst in enumerate((deltas, new_m, new_v)):
        dst.update({n: minor(a, n) for n, a in zip(SMALL, outs[k * len(SMALL):(k + 1) * len(SMALL)])})

    return (loss, dx[None], *[grads[n] for n in WEIGHTS], *[deltas[n] for n in WEIGHTS],
            *[new_m[n] for n in WEIGHTS], *[new_v[n] for n in WEIGHTS])
```

```python
import functools
import math

import numpy as np
import jax
import jax.numpy as jnp
from jax import lax
from jax.experimental import pallas as pl
from jax.experimental.pallas import tpu as pltpu

F32 = jnp.float32
BF16 = jnp.bfloat16

D_MODEL = 1024
N_HEADS = 8
N_KV = 2
Q_GROUP = 4
HEAD_DIM = 64
ATTN_W = 512
KV_W = 128
BLOCK = 128
N_BUCKETS = 32
MAX_DISTANCE = 128
NEG_INF = -1e30
SSM_W = 512
SSM_GROUP = 16
SSM_GROUPS = 32
SSM_STATE = 64
N_SUPER = 4
GROUPS_PER_SUPER = SSM_GROUPS // N_SUPER
SUPER_IN = GROUPS_PER_SUPER * SSM_GROUP
SUPER_HALF = GROUPS_PER_SUPER * SSM_STATE
SUPER_W = 2 * SUPER_HALF
STATE_COLS = N_SUPER * SUPER_W
D_FF = 4096
FF_CHUNKS = 4
IN_W = 3328
SPLITS = (0, 512, 640, 768, 1280, 2304, 3328)
RMS_EPS = 1e-6
N_CHIPS = 4
SUBLANES = 8
LANES = 128
STATE_TILES = STATE_COLS // LANES
SUPER_TILES = SUPER_W // LANES

ADAM_LR = 0.001
ADAM_B1 = 0.9
ADAM_B2 = 0.999
ADAM_EPS = 1e-08
ADAM_WD = 0.01
ADAM_STEP = 10

VMEM_BIG = 56 * 1024 * 1024
SDS = jax.ShapeDtypeStruct
MESH_ID = pl.DeviceIdType.MESH
ANY = pl.BlockSpec(memory_space=pl.ANY)


def _bf(x):
    return x.astype(BF16)


def _mm(a, b):
    return jnp.dot(a, b, preferred_element_type=F32)


def _mm_nt(a, b):
    return lax.dot_general(a, b, (((1,), (1,)), ((), ())), preferred_element_type=F32)


def _mm_tn(a, b):
    return lax.dot_general(a, b, (((0,), (0,)), ((), ())), preferred_element_type=F32)


def _sig(x):
    return 1.0 / (1.0 + jnp.exp(-x))


def _rms(x, g):
    r = lax.rsqrt(jnp.mean(x * x, axis=-1, keepdims=True) + RMS_EPS)
    xh = x * r
    return xh * g, xh, r


def _rms_bwd(dout, xh, r, g):
    dg = jnp.sum(dout * xh, axis=0, keepdims=True)
    dxh = dout * g
    dx = r * (dxh - xh * jnp.mean(dxh * xh, axis=-1, keepdims=True))
    return dx, dg


_GELU_C = math.sqrt(2.0 / math.pi)


def _gelu_and_grad(x):
    x2 = x * x
    inner = _GELU_C * (x + 0.044715 * (x2 * x))
    t = jnp.tanh(inner)
    y = 0.5 * x * (1.0 + t)
    dy = 0.5 * (1.0 + t) + 0.5 * x * (1.0 - t * t) * (_GELU_C * (1.0 + 3.0 * 0.044715 * x2))
    return y, dy


def _zero_map(nd, *_):
    return (0,) * nd


def _params(n_axes, vmem=None):
    return pltpu.CompilerParams(dimension_semantics=("arbitrary",) * n_axes, vmem_limit_bytes=vmem)


class _Exchange:
    def __init__(self, ins, outs, sems, start, wait):
        self.ins, self.outs, self.sems, self.start, self.wait = list(ins), list(outs), list(sems), start, wait


def _fused_call(name, body, grid, in_specs, out_specs, out_shape, scratch, args, exchange, params):
    n_in, n_out, n_scr = len(in_specs), len(out_specs), len(scratch)
    if exchange is None:
        fn = body
    else:
        ex = exchange
        n_xi, n_xo = len(ex.ins), len(ex.outs)

        def fn(*refs):
            at = 0
            parts = []
            for n in (n_in, n_xi, n_out, n_xo, n_scr, len(ex.sems)):
                parts.append(refs[at:at + n])
                at += n
            ins, x_in, outs, x_out, scr, x_sem = parts
            ids = [pl.program_id(a) for a in range(len(grid))]
            first = functools.reduce(jnp.logical_and, [i == 0 for i in ids])
            last = functools.reduce(jnp.logical_and, [i == g - 1 for i, g in zip(ids, grid)])

            @pl.when(first)
            def _():
                ex.start(x_in, x_out, x_sem)

            body(*ins, *outs, *scr)

            @pl.when(last)
            def _():
                ex.wait(x_in, x_out, x_sem)

        in_specs = list(in_specs) + [ANY] * n_xi
        out_specs = list(out_specs) + [ANY] * n_xo
        out_shape = list(out_shape) + ex.outs
        scratch = list(scratch) + ex.sems
        args = list(args) + ex.ins
    return pl.pallas_call(fn, grid=grid, in_specs=in_specs, out_specs=out_specs, out_shape=out_shape,
                          scratch_shapes=list(scratch), name=name, compiler_params=params)(*args)


def _exchange_alone(name, ex):
    def body(*refs):
        n_xi, n_xo = len(ex.ins), len(ex.outs)
        x_in, x_out, x_sem = refs[:n_xi], refs[n_xi:n_xi + n_xo], refs[n_xi + n_xo:]
        ex.start(x_in, x_out, x_sem)
        ex.wait(x_in, x_out, x_sem)

    return pl.pallas_call(body, in_specs=[ANY] * len(ex.ins), out_specs=[ANY] * len(ex.outs), out_shape=ex.outs,
                          scratch_shapes=ex.sems, name=name)(*ex.ins)


def _rowcall(name, body, seq, tb, rows, consts, row_outs, acc_outs, scratch=(), reverse=False, vmem=None,
             exchange=None):
    nb = seq // tb
    rmap = (lambda i: (nb - 1 - i, 0)) if reverse else (lambda i: (i, 0))
    tmap = lambda i: (0,) + rmap(i)

    def row_spec(width):
        if isinstance(width, tuple):
            return pl.BlockSpec((width[0], tb, width[1]), tmap)
        return pl.BlockSpec((tb, width), rmap)

    def row_shape(width):
        return (width[0], seq, width[1]) if isinstance(width, tuple) else (seq, width)

    in_specs = [row_spec(a.shape[1] if a.ndim == 2 else (a.shape[0], a.shape[2])) for a in rows]
    in_specs += [pl.BlockSpec(a.shape, functools.partial(_zero_map, a.ndim), pipeline_mode=pl.Buffered(1))
                 for a in consts]
    out_specs = [row_spec(c) for c, _ in row_outs] + [ANY] * len(acc_outs)
    out_shape = [SDS(row_shape(c), dt) for c, dt in row_outs] + [SDS(s, dt) for s, dt in acc_outs]
    n_main = len(rows) + len(consts) + len(row_outs)
    n_acc = len(acc_outs)

    def fn(*refs):
        main, acc_hbm, rest = refs[:n_main], refs[n_main:n_main + n_acc], refs[n_main + n_acc:]
        acc_vmem, own = rest[:n_acc], rest[n_acc:]
        body(*main, *acc_vmem, *own)

        @pl.when(pl.program_id(0) == nb - 1)
        def _():
            for src, dst in zip(acc_vmem, acc_hbm):
                pltpu.sync_copy(src, dst)

    buffers = [pltpu.VMEM(s, dt) for s, dt in acc_outs] + list(scratch)
    return _fused_call(name, fn if acc_outs else body, (nb,), in_specs, out_specs, out_shape, buffers,
                       [*rows, *consts], exchange, _params(1, vmem))


def _inproj_fwd(x, g1, w_in, tb, exchange=None):
    seq = x.shape[0]

    def body(x_ref, g_ref, w_ref, h_ref, q_ref, k_ref, v_ref, u_ref, ga_ref, gs_ref):
        h, _, _ = _rms(x_ref[...], g_ref[...])
        hb = _bf(h)
        h_ref[...] = hb
        pj = _mm_nt(hb, w_ref[...])
        q_ref[...] = _bf(pj[:, SPLITS[0]:SPLITS[1]])
        k_ref[...] = _bf(pj[:, SPLITS[1]:SPLITS[2]])
        v_ref[...] = _bf(pj[:, SPLITS[2]:SPLITS[3]])
        u_ref[...] = pj[:, SPLITS[3]:SPLITS[4]]
        ga_ref[...] = pj[:, SPLITS[4]:SPLITS[5]]
        gs_ref[...] = pj[:, SPLITS[5]:SPLITS[6]]

    return _rowcall("inproj_fwd", body, seq, tb, [x], [g1, w_in],
                    [(D_MODEL, BF16), (ATTN_W, BF16), (KV_W, BF16), (KV_W, BF16), (SSM_W, F32),
                     (D_MODEL, F32), (D_MODEL, F32)], [], vmem=VMEM_BIG, exchange=exchange)


def _inproj_bwd(x, dx2, dq, dk, dv, du, dga, dgs, g1, w_in, tb, exchange=None):
    seq = x.shape[0]

    def body(x_ref, dx2_ref, dq_ref, dk_ref, dv_ref, du_ref, dga_ref, dgs_ref, g_ref, w_ref,
             dx_ref, dpj_ref, dg_ref):
        @pl.when(pl.program_id(0) == 0)
        def _():
            dg_ref[...] = jnp.zeros_like(dg_ref)

        dpj = jnp.concatenate([dq_ref[...], dk_ref[...], dv_ref[...], _bf(du_ref[...]),
                               dga_ref[...], dgs_ref[...]], axis=1)
        dpj_ref[...] = dpj
        dh = _mm(dpj, w_ref[...])
        g = g_ref[...]
        _, xh, r = _rms(x_ref[...], g)
        dxn, dg = _rms_bwd(dh, xh, r, g)
        dx_ref[...] = dx2_ref[...] + dxn
        dg_ref[...] += dg

    return _rowcall("inproj_bwd", body, seq, tb, [x, dx2, dq, dk, dv, du, dga, dgs], [g1, w_in],
                    [(D_MODEL, F32), (IN_W, BF16)], [((1, D_MODEL), F32)], vmem=VMEM_BIG, exchange=exchange)


def _bucket_table():
    qi = np.arange(BLOCK)[:, None]
    kj = np.arange(2 * BLOCK)[None, :]
    dist = qi + BLOCK - kj
    max_exact = N_BUCKETS // 2
    d = np.maximum(dist, 0)
    df = np.maximum(d, 1).astype(np.float32)
    large = max_exact + (np.log(df / np.float32(max_exact)) / np.float32(math.log(MAX_DISTANCE / max_exact))
                         * np.float32(N_BUCKETS - max_exact)).astype(np.int32)
    large = np.minimum(large, N_BUCKETS - 1)
    bucket = np.where(d < max_exact, d, large)
    valid = (dist >= 0) & (dist < BLOCK)
    return np.where(valid, bucket, -1).astype(np.int32)


def _bias_table(rel_bias, bucket):
    def body(rb_ref, bk_ref, o_ref):
        bk = bk_ref[...]
        has_prev = lax.broadcasted_iota(jnp.int32, bk.shape, 1) >= BLOCK
        for h in range(N_HEADS):
            kh, j, par = h // Q_GROUP, (h // 2) % 2, h % 2
            acc = jnp.full((BLOCK, 2 * BLOCK), NEG_INF, F32)
            for b in range(N_BUCKETS):
                acc = jnp.where(bk == b, rb_ref[b, h], acc)
            o_ref[0, kh, par, :, j * BLOCK:(j + 1) * BLOCK] = jnp.where(has_prev, acc, NEG_INF).T
            o_ref[1, kh, par, :, j * BLOCK:(j + 1) * BLOCK] = acc.T

    return pl.pallas_call(
        body, out_shape=SDS((2, N_KV, 2, 2 * BLOCK, 2 * BLOCK), F32),
        in_specs=[pl.BlockSpec(memory_space=pltpu.SMEM), pl.BlockSpec(memory_space=pltpu.VMEM)],
        out_specs=pl.BlockSpec(memory_space=pltpu.VMEM), name="bias_table",
    )(rel_bias, bucket)


def _bias_grad(dbias, bucket):
    def body(db_ref, bk_ref, o_ref):
        bk = bk_ref[...]
        for h in range(N_HEADS):
            kh, j, par = h // Q_GROUP, (h // 2) % 2, h % 2
            db = db_ref[kh, par, :, j * BLOCK:(j + 1) * BLOCK].T
            for b in range(N_BUCKETS):
                o_ref[b, h] = jnp.sum(jnp.where(bk == b, db, 0.0))

    return pl.pallas_call(
        body, out_shape=SDS((N_BUCKETS, N_HEADS), F32),
        in_specs=[pl.BlockSpec(memory_space=pltpu.VMEM), pl.BlockSpec(memory_space=pltpu.VMEM)],
        out_specs=pl.BlockSpec(memory_space=pltpu.SMEM), name="bias_grad",
    )(dbias, bucket)


TILE = 2 * HEAD_DIM


def _pair_layout(t):
    lead = t.shape[:-3]
    t = t.reshape(lead + (N_KV, 2, 2) + t.shape[-2:])
    nl = len(lead)
    t = jnp.transpose(t, tuple(range(nl)) + (nl, nl + 2, nl + 1, nl + 3, nl + 4))
    return t.reshape(lead + (N_KV, 2, 2 * BLOCK, t.shape[-1]))


def _pair_unlayout(t):
    t = t.reshape(N_KV, 2, 2, BLOCK, t.shape[-1]).transpose(0, 2, 1, 3, 4)
    return t.reshape(N_HEADS, BLOCK, t.shape[-1])


def _halves(t):
    tf = t.astype(F32)
    low = lax.broadcasted_iota(jnp.int32, tf.shape, 1) < HEAD_DIM
    swapped = pltpu.roll(tf, HEAD_DIM, 1)
    zero = jnp.zeros_like(tf)
    return ((_bf(jnp.where(low, tf, zero)), _bf(jnp.where(low, zero, swapped))),
            (_bf(jnp.where(low, swapped, zero)), _bf(jnp.where(low, zero, tf))))


def _fold_halves(even, odd):
    low = lax.broadcasted_iota(jnp.int32, even.shape, 1) < HEAD_DIM
    comb = jnp.where(low, even, odd)
    return comb + pltpu.roll(comb, HEAD_DIM, 1)


def _tile_rows(ref, kh):
    return jnp.concatenate([ref[:, (2 * kh) * TILE:(2 * kh + 1) * TILE],
                            ref[:, (2 * kh + 1) * TILE:(2 * kh + 2) * TILE]], axis=0)


def _halves_t(t):
    tt = t.astype(F32).T
    top = lax.broadcasted_iota(jnp.int32, tt.shape, 0) < HEAD_DIM
    swapped = jnp.concatenate([tt[HEAD_DIM:], tt[:HEAD_DIM]], axis=0)
    zero = jnp.zeros_like(tt)
    return ((_bf(jnp.where(top, tt, zero)), _bf(jnp.where(top, zero, swapped))),
            (_bf(jnp.where(top, swapped, zero)), _bf(jnp.where(top, zero, tt))))


def _attn_probs(km, qk, bias, sink):
    lg = _mm_nt(km, qk) * (HEAD_DIM ** -0.5) + bias
    m = jnp.maximum(jnp.max(lg, axis=0, keepdims=True), sink)
    p = jnp.exp(lg - m)
    es = jnp.exp(sink - m)
    inv = 1.0 / (jnp.sum(p, axis=0, keepdims=True) + es)
    return p * inv, es * inv


def _attn_fwd(q, k, v, bias, sink_rows, exchange=None):
    seq = q.shape[0]
    nblk = seq // BLOCK

    def body(q_ref, kp_ref, kc_ref, vp_ref, vc_ref, b_ref, s_ref, o_ref):
        which = jnp.minimum(pl.program_id(0), 1)
        kms = _halves(jnp.concatenate([kp_ref[...], kc_ref[...]], axis=0))
        vts = _halves_t(jnp.concatenate([vp_ref[...], vc_ref[...]], axis=0))
        for kh in range(N_KV):
            qk = _tile_rows(q_ref, kh)
            acc = jnp.zeros((TILE, 2 * BLOCK), F32)
            for par in range(2):
                pr, _ = _attn_probs(kms[kh][par], qk, b_ref[which, kh, par], s_ref[kh, par])
                acc = acc + _mm(vts[kh][par], _bf(pr))
            acc = acc.T
            o_ref[:, (2 * kh) * TILE:(2 * kh + 1) * TILE] = _bf(acc[:BLOCK])
            o_ref[:, (2 * kh + 1) * TILE:(2 * kh + 2) * TILE] = _bf(acc[BLOCK:])

    cur = lambda n: (n, 0)
    prev = lambda n: (jnp.maximum(n - 1, 0), 0)
    return _fused_call(
        "attn_fwd", body, (nblk,),
        [pl.BlockSpec((BLOCK, ATTN_W), cur),
         pl.BlockSpec((BLOCK, KV_W), prev), pl.BlockSpec((BLOCK, KV_W), cur),
         pl.BlockSpec((BLOCK, KV_W), prev), pl.BlockSpec((BLOCK, KV_W), cur),
         pl.BlockSpec(bias.shape, functools.partial(_zero_map, bias.ndim)),
         pl.BlockSpec(sink_rows.shape, functools.partial(_zero_map, sink_rows.ndim))],
        [pl.BlockSpec((BLOCK, ATTN_W), cur)], [SDS((seq, ATTN_W), BF16)], [],
        [q, k, k, v, v, bias, sink_rows], exchange, _params(1))


def _attn_bwd(q, k, v, d_out, bias, sink_rows, exchange=None):
    seq = q.shape[0]
    nblk = seq // BLOCK

    def body(q_ref, kp_ref, kc_ref, vp_ref, vc_ref, do_ref, b_ref, s_ref,
             dq_ref, dk_ref, dv_ref, db_ref, ds_ref, ck_ref, cv_ref):
        n = pl.program_id(0)

        @pl.when(n == 0)
        def _():
            db_ref[...] = jnp.zeros_like(db_ref)
            ds_ref[...] = jnp.zeros_like(ds_ref)
            ck_ref[...] = jnp.zeros_like(ck_ref)
            cv_ref[...] = jnp.zeros_like(cv_ref)

        @pl.when(n < nblk)
        def _():
            which = jnp.minimum(n, 1)
            scale = HEAD_DIM ** -0.5
            kcat = jnp.concatenate([kp_ref[...], kc_ref[...]], axis=0)
            kms = _halves(kcat)
            kts = _halves_t(kcat)
            vms = _halves(jnp.concatenate([vp_ref[...], vc_ref[...]], axis=0))
            dks, dvs = [], []
            for kh in range(N_KV):
                qk = _tile_rows(q_ref, kh)
                dok = _tile_rows(do_ref, kh)
                dq = jnp.zeros((TILE, 2 * BLOCK), F32)
                dkp, dvp = [], []
                for par in range(2):
                    pr, ps = _attn_probs(kms[kh][par], qk, b_ref[which, kh, par], s_ref[kh, par])
                    dp = _mm_nt(vms[kh][par], dok)
                    rs = jnp.sum(pr * dp, axis=0, keepdims=True)
                    dlg = pr * (dp - rs)
                    ds_ref[kh, par] += -ps * rs
                    db_ref[kh, par] += dlg
                    dlb = _bf(dlg)
                    dq = dq + _mm(kts[kh][par], dlb)
                    dkp.append(_mm(dlb, qk))
                    dvp.append(_mm(_bf(pr), dok))
                dq = _bf((dq * scale).T)
                dq_ref[:, (2 * kh) * TILE:(2 * kh + 1) * TILE] = dq[:BLOCK]
                dq_ref[:, (2 * kh + 1) * TILE:(2 * kh + 2) * TILE] = dq[BLOCK:]
                dks.append(_fold_halves(*dkp))
                dvs.append(_fold_halves(*dvp))
            low = lax.broadcasted_iota(jnp.int32, (2 * BLOCK, TILE), 1) < HEAD_DIM
            dkk = jnp.where(low, dks[0], dks[1]) * scale
            dvv = jnp.where(low, dvs[0], dvs[1])
            dk_ref[...] = _bf(ck_ref[...] + dkk[:BLOCK])
            ck_ref[...] = dkk[BLOCK:]
            dv_ref[...] = _bf(cv_ref[...] + dvv[:BLOCK])
            cv_ref[...] = dvv[BLOCK:]

        @pl.when(n == nblk)
        def _():
            dk_ref[...] = _bf(ck_ref[...])
            dv_ref[...] = _bf(cv_ref[...])

    cur = lambda n: (jnp.minimum(n, nblk - 1), 0)
    prev = lambda n: (jnp.maximum(jnp.minimum(n, nblk - 1) - 1, 0), 0)
    late = lambda n: (jnp.maximum(n - 1, 0), 0)
    kv_spec = lambda m: pl.BlockSpec((BLOCK, KV_W), m)
    acc_b = pl.BlockSpec(bias.shape[1:], functools.partial(_zero_map, bias.ndim - 1))
    acc_s = pl.BlockSpec(sink_rows.shape, functools.partial(_zero_map, sink_rows.ndim))
    return _fused_call(
        "attn_bwd", body, (nblk + 1,),
        [pl.BlockSpec((BLOCK, ATTN_W), cur), kv_spec(prev), kv_spec(cur), kv_spec(prev), kv_spec(cur),
         pl.BlockSpec((BLOCK, ATTN_W), cur),
         pl.BlockSpec(bias.shape, functools.partial(_zero_map, bias.ndim)), acc_s],
        [pl.BlockSpec((BLOCK, ATTN_W), cur), kv_spec(late), kv_spec(late), acc_b, acc_s],
        [SDS((seq, ATTN_W), BF16), SDS((seq, KV_W), BF16), SDS((seq, KV_W), BF16),
         SDS(bias.shape[1:], F32), SDS(sink_rows.shape, F32)],
        [pltpu.VMEM((BLOCK, KV_W), F32), pltpu.VMEM((BLOCK, KV_W), F32)],
        [q, k, k, v, v, d_out, bias, sink_rows], exchange, _params(1))


def _ssm_discretize(lam_re, lam_im, log_dt, b_re, b_im):
    dt = jnp.exp(log_dt)[:, None]
    mag = jnp.exp(lam_re * dt)
    ab_re = mag * jnp.cos(lam_im * dt)
    ab_im = mag * jnp.sin(lam_im * dt)
    nr = ab_re - 1.0
    den = lam_re * lam_re + lam_im * lam_im
    f_re = (nr * lam_re + ab_im * lam_im) / den
    f_im = (ab_im * lam_re - nr * lam_im) / den
    bb_re = f_re[..., None] * b_re - f_im[..., None] * b_im
    bb_im = f_re[..., None] * b_im + f_im[..., None] * b_re
    return ab_re, ab_im, bb_re, bb_im


def _state_layout(re, im):
    z = jnp.stack([re, im]).reshape(2, N_SUPER, GROUPS_PER_SUPER, SSM_STATE)
    return z.transpose(1, 0, 2, 3).reshape(STATE_COLS)


def _state_unlayout(vec):
    z = vec.reshape(N_SUPER, 2, GROUPS_PER_SUPER, SSM_STATE).transpose(1, 0, 2, 3)
    z = z.reshape(2, SSM_GROUPS, SSM_STATE)
    return z[0], z[1]


SEG = 4
WINDOW = SEG * SUBLANES


def _scan_tables(ab_re, ab_im):
    pw = [None, (ab_re, ab_im)]
    for _ in range(2, WINDOW + 1):
        pr, pi_ = pw[-1]
        pw.append((pr * ab_re - pi_ * ab_im, pr * ab_im + pi_ * ab_re))
    rows = np.arange(SUBLANES)[:, None]
    ones = np.ones((SUBLANES, 1), np.float32)
    conj = lambda p: (p[0], -p[1])
    fwd, bwd = [], []
    for shift in (1, 2, 4):
        fwd.append(_state_layout(*pw[SEG * shift])[None, :] * (rows >= shift).astype(np.float32))
        bwd.append(_state_layout(*conj(pw[SEG * shift]))[None, :] * (rows < SUBLANES - shift).astype(np.float32))
    fwd.append(jnp.stack([_state_layout(*pw[SEG * (r + 1)]) for r in range(SUBLANES)]))
    bwd.append(jnp.stack([_state_layout(*conj(pw[SEG * (SUBLANES - r)])) for r in range(SUBLANES)]))
    for k in range(1, SEG):
        fwd.append(_state_layout(*pw[k])[None, :] * ones)
        bwd.append(_state_layout(*conj(pw[k]))[None, :] * ones)
    return jnp.stack(fwd), jnp.stack(bwd)


_EYE = np.eye(GROUPS_PER_SUPER, dtype=np.float32)


def _b_matrix(bb_re, bb_im):
    bb = jnp.stack([bb_re, bb_im]).reshape(2, N_SUPER, GROUPS_PER_SUPER, SSM_STATE, SSM_GROUP)
    m = jnp.einsum('rsgpc,gh->sgcrhp', bb, _EYE)
    return m.reshape(N_SUPER, SUPER_IN, SUPER_W)


def _b_matrix_grad(dm):
    d = dm.reshape(N_SUPER, GROUPS_PER_SUPER, SSM_GROUP, 2, GROUPS_PER_SUPER, SSM_STATE)
    d = jnp.sum(d * _EYE[None, :, None, None, :, None], axis=4)
    d = d.transpose(3, 0, 1, 4, 2).reshape(2, SSM_GROUPS, SSM_STATE, SSM_GROUP)
    return d[0], d[1]


def _c_matrix(c_re, c_im):
    cc = jnp.stack([c_re, -c_im]).reshape(2, N_SUPER, GROUPS_PER_SUPER, SSM_GROUP, SSM_STATE)
    m = jnp.einsum('rsgcp,gh->srgphc', cc, _EYE)
    return m.reshape(N_SUPER, SUPER_W, SUPER_IN)


def _c_matrix_grad(dm):
    d = dm.reshape(N_SUPER, 2, GROUPS_PER_SUPER, SSM_STATE, GROUPS_PER_SUPER, SSM_GROUP)
    d = jnp.sum(d * _EYE[None, None, :, None, :, None], axis=4)
    d = d.transpose(1, 0, 2, 4, 3).reshape(2, SSM_GROUPS, SSM_GROUP, SSM_STATE)
    return d[0], -d[1]


def _cmul_add(xr, xi, ar, ai, sr, si):
    return xr + ar * sr - ai * si, xi + ar * si + ai * sr


def _scan_rows(buf_ref, tab_ref, carry_ref, n_windows, reverse, h_ref=None, da_ref=None):
    order = list(range(SEG - 1, -1, -1)) if reverse else list(range(SEG))
    near = SUBLANES - 1 if reverse else 0
    far = 0 if reverse else SUBLANES - 1
    s_in = SUBLANES - 1 if reverse else 1
    lanes = lambda tile: pl.ds(tile * LANES, LANES)

    def window(w0, tile_re, tile_im, c_re, c_im, acc):
        rows = lambda t: pl.ds(w0 + t, SUBLANES, stride=SEG)
        get = lambda ref, t: (ref.at[tile_re][rows(t), :], ref.at[tile_im][rows(t), :])
        tab = lambda k: (tab_ref[k, :, lanes(tile_re)], tab_ref[k, :, lanes(tile_im)])

        def put(t, xr, xi):
            buf_ref.at[tile_re][rows(t), :] = xr
            buf_ref.at[tile_im][rows(t), :] = xi

        a1 = tab(4)
        er, ei = get(buf_ref, order[0])
        for t in order[1:]:
            er, ei = _cmul_add(*get(buf_ref, t), *a1, er, ei)
            if t != order[-1]:
                put(t, er, ei)
        for k, shift in enumerate((1, 2, 4)):
            s = (SUBLANES - shift) if reverse else shift
            er, ei = _cmul_add(er, ei, *tab(k), pltpu.roll(er, s, 0), pltpu.roll(ei, s, 0))
        er, ei = _cmul_add(er, ei, *tab(3), c_re, c_im)
        put(order[-1], er, ei)
        sub = lax.broadcasted_iota(jnp.int32, er.shape, 0)
        in_re = jnp.where(sub == near, c_re, pltpu.roll(er, s_in, 0))
        in_im = jnp.where(sub == near, c_im, pltpu.roll(ei, s_in, 0))
        true = {order[-1]: (er, ei)}
        for idx, t in enumerate(order[:-1]):
            true[t] = _cmul_add(*get(buf_ref, t), *tab(4 + idx), in_re, in_im)
            put(t, *true[t])
        carry = (jnp.broadcast_to(er[far:far + 1], er.shape), jnp.broadcast_to(ei[far:far + 1], ei.shape))
        if acc is None:
            return carry, None
        acc_re, acc_im = acc
        for t in range(SEG):
            if t + 1 < SEG:
                gr, gim = true[t + 1]
            else:
                gr = jnp.where(sub == SUBLANES - 1, c_re, pltpu.roll(true[0][0], SUBLANES - 1, 0))
                gim = jnp.where(sub == SUBLANES - 1, c_im, pltpu.roll(true[0][1], SUBLANES - 1, 0))
            hr, hi = get(h_ref, t)
            acc_re = acc_re + gr * hr + gim * hi
            acc_im = acc_im + gim * hr - gr * hi
        return carry, (acc_re, acc_im)

    half = SUPER_HALF // LANES
    per = 2 if h_ref is None else 4
    for sb in range(N_SUPER):
        pairs = [(2 * half * sb + j, 2 * half * sb + half + j) for j in range(half)]

        def step(wi, state, pairs=pairs):
            w = (n_windows - 1 - wi) if reverse else wi
            w0 = pl.multiple_of(w * WINDOW, WINDOW)
            out = []
            for j, (tile_re, tile_im) in enumerate(pairs):
                mine = state[per * j:per * (j + 1)]
                carry, acc = window(w0, tile_re, tile_im, mine[0], mine[1], mine[2:] or None)
                out += list(carry) + list(acc or ())
            return tuple(out)

        init = []
        for tile_re, tile_im in pairs:
            init += [carry_ref[:, lanes(tile_re)], carry_ref[:, lanes(tile_im)]]
            if h_ref is not None:
                init += [da_ref[:, lanes(tile_re)], da_ref[:, lanes(tile_im)]]
        fin = lax.fori_loop(0, n_windows, step, tuple(init))
        for j, (tile_re, tile_im) in enumerate(pairs):
            carry_ref[:, lanes(tile_re)] = fin[per * j]
            carry_ref[:, lanes(tile_im)] = fin[per * j + 1]
            if h_ref is not None:
                da_ref[:, lanes(tile_re)] = fin[per * j + 2]
                da_ref[:, lanes(tile_im)] = fin[per * j + 3]


def _put_tiles(ref, sb, block):
    for j in range(SUPER_TILES):
        ref[sb * SUPER_TILES + j] = block[:, j * LANES:(j + 1) * LANES]


def _get_tiles(ref, sb):
    return jnp.concatenate([ref[sb * SUPER_TILES + j] for j in range(SUPER_TILES)], axis=1)


def _ssm_fwd(u, bmat, cmat, tab, d_skip, tb, exchange=None):
    seq = u.shape[0]

    def body(u_ref, b_ref, c_ref, t_ref, d_ref, s_ref, h_ref, carry_ref):
        @pl.when(pl.program_id(0) == 0)
        def _():
            carry_ref[...] = jnp.zeros_like(carry_ref)

        u_blk = u_ref[...]
        ub = _bf(u_blk)
        for sb in range(N_SUPER):
            _put_tiles(h_ref, sb, _mm(ub[:, sb * SUPER_IN:(sb + 1) * SUPER_IN], b_ref[sb]))
        _scan_rows(h_ref, t_ref, carry_ref, tb // WINDOW, False)
        ys = [_mm(_bf(_get_tiles(h_ref, sb)), c_ref[sb]) for sb in range(N_SUPER)]
        s_ref[...] = jnp.concatenate(ys, axis=1) + d_ref[...] * u_blk

    return _rowcall("ssm_fwd", body, seq, tb, [u], [bmat, cmat, tab, d_skip],
                    [(SSM_W, F32), ((STATE_TILES, LANES), F32)], [],
                    scratch=[pltpu.VMEM((SUBLANES, STATE_COLS), F32)], vmem=VMEM_BIG, exchange=exchange)


def _ssm_bwd(ds, u, h, bmat_t, cmat_t, tab, d_skip, tb, exchange=None):
    seq = u.shape[0]

    def body(ds_ref, u_ref, h_ref, bt_ref, ct_ref, t_ref, d_ref,
             du_ref, db_ref, dc_ref, da_ref, dd_ref, g_ref, carry_ref):
        @pl.when(pl.program_id(0) == 0)
        def _():
            carry_ref[...] = jnp.zeros_like(carry_ref)
            db_ref[...] = jnp.zeros_like(db_ref)
            dc_ref[...] = jnp.zeros_like(dc_ref)
            da_ref[...] = jnp.zeros_like(da_ref)
            dd_ref[...] = jnp.zeros_like(dd_ref)

        ds_blk = ds_ref[...]
        dsb = _bf(ds_blk)
        u_blk = u_ref[...]
        ub = _bf(u_blk)
        for sb in range(N_SUPER):
            _put_tiles(g_ref, sb, _mm(dsb[:, sb * SUPER_IN:(sb + 1) * SUPER_IN], ct_ref[sb]))
        _scan_rows(g_ref, t_ref, carry_ref, tb // WINDOW, True, h_ref=h_ref, da_ref=da_ref)
        dus = []
        for sb in range(N_SUPER):
            gb = _bf(_get_tiles(g_ref, sb))
            dus.append(_mm(gb, bt_ref[sb]))
            db_ref[sb] += _mm_tn(ub[:, sb * SUPER_IN:(sb + 1) * SUPER_IN], gb)
            dc_ref[sb] += _mm_tn(_bf(_get_tiles(h_ref, sb)), dsb[:, sb * SUPER_IN:(sb + 1) * SUPER_IN])
        du_ref[...] = jnp.concatenate(dus, axis=1) + d_ref[...] * ds_blk
        dd_ref[...] += jnp.sum(ds_blk * u_blk, axis=0, keepdims=True)

    return _rowcall("ssm_bwd", body, seq, tb, [ds, u, h], [bmat_t, cmat_t, tab, d_skip],
                    [(SSM_W, F32)],
                    [((N_SUPER, SUPER_IN, SUPER_W), F32), ((N_SUPER, SUPER_W, SUPER_IN), F32),
                     ((SUBLANES, STATE_COLS), F32), ((1, SSM_W), F32)],
                    scratch=[pltpu.VMEM((STATE_TILES, tb, LANES), F32), pltpu.VMEM((SUBLANES, STATE_COLS), F32)],
                    reverse=True, vmem=VMEM_BIG, exchange=exchange)


def _merge_core(s, attb, ga, gs, wg_ref, wab_ref, wsb_ref, wout_ref):
    zg, dgelu = _gelu_and_grad(s)
    zgb = _bf(zg)
    sg = _sig(_mm(zgb, wg_ref[...]))
    z = zg * sg
    zb = _bf(z)
    ys = jnp.concatenate([_mm(zb, wsb_ref[j]) for j in range(N_CHIPS)], axis=1)
    ya = jnp.concatenate([_mm(attb, wab_ref[j]) for j in range(N_CHIPS)], axis=1)
    sa = _sig(ga)
    ss = _sig(gs)
    mgb = _bf(sa * ya + ss * ys)
    o = _mm(mgb, wout_ref[...])
    return dict(zg=zg, dgelu=dgelu, zgb=zgb, sg=sg, zb=zb, ys=ys, ya=ya, sa=sa, ss=ss, mgb=mgb, o=o)


def _merge_fwd(x, s, att, ga, gs, g2, w_glu, w_ab, w_sb, w_out, tb):
    seq = x.shape[0]

    def body(x_ref, s_ref, att_ref, ga_ref, gs_ref, g_ref, wg_ref, wab_ref, wsb_ref, wout_ref, x2_ref):
        f = _merge_core(s_ref[...], att_ref[...], ga_ref[...], gs_ref[...], wg_ref, wab_ref, wsb_ref, wout_ref)
        n, _, _ = _rms(f["o"], g_ref[...])
        x2_ref[...] = x_ref[...] + n

    return _rowcall("merge_fwd", body, seq, tb, [x, s, att, ga, gs], [g2, w_glu, w_ab, w_sb, w_out],
                    [(D_MODEL, F32)], [], vmem=VMEM_BIG)[0]


def _merge_bwd(dx2, s, att, ga, gs, g2, w_glu, w_ab, w_sb, w_out, tb, exchange=None):
    seq = s.shape[0]
    cw = D_MODEL // N_CHIPS
    last = seq // tb - 1

    def body(dx2_ref, s_ref, att_ref, ga_ref, gs_ref, g_ref, wg_ref, wab_ref, wsb_ref, wout_ref,
             ds_ref, datt_ref, dga_ref, dgs_ref, dg_ref, dwg_ref, dwab_ref, dwsb_ref, dwout_ref,
             bwg_ref, bwab_ref, bwsb_ref, bwout_ref):
        @pl.when(pl.program_id(0) == 0)
        def _():
            for r in (dg_ref, dwg_ref, dwab_ref, dwsb_ref, dwout_ref):
                r[...] = jnp.zeros_like(r)

        attb = att_ref[...]
        f = _merge_core(s_ref[...], attb, ga_ref[...], gs_ref[...], wg_ref, wab_ref, wsb_ref, wout_ref)
        g = g_ref[...]
        _, oh, r2 = _rms(f["o"], g)
        do, dg = _rms_bwd(dx2_ref[...], oh, r2, g)
        dg_ref[...] += dg
        dob = _bf(do)
        dwout_ref[...] += _mm_tn(f["mgb"], dob)
        dmg = _mm_nt(dob, wout_ref[...])
        sa, ss = f["sa"], f["ss"]
        dyab = _bf(dmg * sa)
        dysb = _bf(dmg * ss)
        dga_ref[...] = _bf(dmg * f["ya"] * sa * (1.0 - sa))
        dgs_ref[...] = _bf(dmg * f["ys"] * ss * (1.0 - ss))
        dwab = _mm_tn(attb, dyab)
        dwsb = _mm_tn(f["zb"], dysb)
        datt = jnp.zeros((tb, ATTN_W), F32)
        dz = jnp.zeros((tb, SSM_W), F32)
        for j in range(N_CHIPS):
            dwab_ref[j] += dwab[:, j * cw:(j + 1) * cw]
            dwsb_ref[j] += dwsb[:, j * cw:(j + 1) * cw]
            datt = datt + _mm_nt(dyab[:, j * cw:(j + 1) * cw], wab_ref[j])
            dz = dz + _mm_nt(dysb[:, j * cw:(j + 1) * cw], wsb_ref[j])
        datt_ref[...] = _bf(datt)
        sg, zg = f["sg"], f["zg"]
        dglb = _bf(dz * zg * sg * (1.0 - sg))
        dwg_ref[...] += _mm_tn(f["zgb"], dglb)
        dzg = dz * sg + _mm_nt(dglb, wg_ref[...])
        ds_ref[...] = dzg * f["dgelu"]

        @pl.when(pl.program_id(0) == last)
        def _():
            for dst, src in ((bwg_ref, dwg_ref), (bwab_ref, dwab_ref), (bwsb_ref, dwsb_ref), (bwout_ref, dwout_ref)):
                dst[...] = _bf(src[...])

    shapes = [w_glu.shape, w_ab.shape, w_sb.shape, w_out.shape]
    return _rowcall("merge_bwd", body, seq, tb, [dx2, s, att, ga, gs], [g2, w_glu, w_ab, w_sb, w_out],
                    [(SSM_W, F32), (ATTN_W, BF16), (D_MODEL, BF16), (D_MODEL, BF16)],
                    [((1, D_MODEL), F32)] + [(sh, F32) for sh in shapes] + [(sh, BF16) for sh in shapes],
                    vmem=VMEM_BIG, exchange=exchange)


def _mlp_fwd_loss(x2, target, g3, g4, w_ffi, w_ffo, tb):
    seq = x2.shape[0]
    n_slab = len(w_ffi)
    sw = D_FF // FF_CHUNKS // n_slab

    def body(x2_ref, t_ref, g3_ref, g4_ref, *rest):
        wi_refs, (wo_ref, dy_ref, df_ref, h_ref, loss_ref, dg_ref) = rest[:n_slab], rest[n_slab:]

        @pl.when(pl.program_id(0) == 0)
        def _():
            loss_ref[...] = jnp.zeros_like(loss_ref)
            dg_ref[...] = jnp.zeros_like(dg_ref)

        x2_blk = x2_ref[...]
        h3, _, _ = _rms(x2_blk, g3_ref[...])
        hb = _bf(h3)
        h_ref[...] = hb
        f = jnp.zeros((tb, D_MODEL), F32)
        for j in range(FF_CHUNKS):
            for k in range(n_slab):
                a = _mm(hb, wi_refs[k][j])
                f = f + _mm(_bf(jnp.square(jnp.maximum(a, 0.0))), wo_ref[j, pl.ds(k * sw, sw), :])
        g4 = g4_ref[...]
        n4, fh, r4 = _rms(f, g4)
        e = (x2_blk + n4) - t_ref[...]
        loss_ref[...] += 0.5 * jnp.sum(jnp.mean(e * e, axis=-1, keepdims=True))
        dy = e * (1.0 / D_MODEL)
        dy_ref[...] = dy
        df, dg = _rms_bwd(dy, fh, r4, g4)
        df_ref[...] = _bf(df)
        dg_ref[...] += dg

    return _rowcall("mlp_fwd_loss", body, seq, tb, [x2, target], [g3, g4, *w_ffi, w_ffo],
                    [(D_MODEL, F32), (D_MODEL, BF16), (D_MODEL, BF16)],
                    [((SUBLANES, 128), F32), ((1, D_MODEL), F32)], vmem=VMEM_BIG)


def _mlp_bwd(x2, dy, df, h3, g3, w_ffi, w_ffo, tb):
    seq = x2.shape[0]
    n_slab = len(w_ffi)
    sw = D_FF // FF_CHUNKS // n_slab

    def body(x2_ref, dy_ref, df_ref, h_ref, g3_ref, *rest):
        wi_refs, (wo_ref, dx_ref, act_ref, da_ref, dg_ref) = rest[:n_slab], rest[n_slab:]

        @pl.when(pl.program_id(0) == 0)
        def _():
            dg_ref[...] = jnp.zeros_like(dg_ref)

        hb = h_ref[...]
        dfb = df_ref[...]
        dh = jnp.zeros((tb, D_MODEL), F32)
        for j in range(FF_CHUNKS):
            for k in range(n_slab):
                cols = pl.ds((j * n_slab + k) * sw, sw)
                ra = jnp.maximum(_mm(hb, wi_refs[k][j]), 0.0)
                act_ref[:, cols] = _bf(ra * ra)
                dab = _bf(_mm_nt(dfb, wo_ref[j, pl.ds(k * sw, sw), :]) * (2.0 * ra))
                da_ref[:, cols] = dab
                dh = dh + _mm_nt(dab, wi_refs[k][j])
        g3 = g3_ref[...]
        _, xh, r3 = _rms(x2_ref[...], g3)
        dxn, dg = _rms_bwd(dh, xh, r3, g3)
        dx_ref[...] = dy_ref[...] + dxn
        dg_ref[...] += dg

    return _rowcall("mlp_bwd", body, seq, tb, [x2, dy, df, h3], [g3, *w_ffi, w_ffo],
                    [(D_MODEL, F32), (D_FF, BF16), (D_FF, BF16)], [((1, D_MODEL), F32)], vmem=VMEM_BIG)


def _matmul_tn(name, a, b, tk, tn, tl, chunk_major, exchange=None):
    seq, kdim = a.shape
    ndim = b.shape[1]
    last = seq // tl - 1

    def body(a_ref, b_ref, o_ref, ob_ref):
        @pl.when(pl.program_id(2) == 0)
        def _():
            o_ref[...] = jnp.zeros_like(o_ref)

        o_ref[...] += _mm_tn(a_ref[...], b_ref[...])

        @pl.when(pl.program_id(2) == last)
        def _():
            ob_ref[...] = _bf(o_ref[...])

    if chunk_major:
        shape = (ndim // tn, kdim, tn)
        out_spec = pl.BlockSpec((None, tk, tn), lambda k, n, l: (n, k, 0))
    else:
        shape = (kdim, ndim)
        out_spec = pl.BlockSpec((tk, tn), lambda k, n, l: (k, n))
    return _fused_call(
        name, body, (kdim // tk, ndim // tn, seq // tl),
        [pl.BlockSpec((tl, tk), lambda k, n, l: (l, k)), pl.BlockSpec((tl, tn), lambda k, n, l: (l, n))],
        [out_spec, out_spec], [SDS(shape, F32), SDS(shape, BF16)], [], [a, b], exchange, _params(3, VMEM_BIG))


def _ew_call(name, fn, ins, n_out, after=None):
    rows, cols = ins[0].shape
    tr = rows
    while tr * cols * 4 > min(1 << 20, (9 << 20) // (len(ins) + n_out)) and tr % 16 == 0:
        tr //= 2
    spec = pl.BlockSpec((tr, cols), lambda i: (i, 0))
    extra = [] if after is None else [after]

    def body(*refs):
        outs = fn(*[r[...] for r in refs[:len(ins)]])
        for r, o in zip(refs[len(ins) + len(extra):], outs):
            r[...] = o

    return pl.pallas_call(
        body, grid=(rows // tr,), in_specs=[spec] * len(ins) + [ANY] * len(extra), out_specs=[spec] * n_out,
        out_shape=[SDS((rows, cols), F32)] * n_out, name=name, compiler_params=_params(1))(*ins, *extra)


def _adam_math(w, g, m, v):
    m2 = ADAM_B1 * m + (1.0 - ADAM_B1) * g
    v2 = ADAM_B2 * v + (1.0 - ADAM_B2) * (g * g)
    m_hat = m2 / (1.0 - ADAM_B1 ** ADAM_STEP)
    v_hat = v2 / (1.0 - ADAM_B2 ** ADAM_STEP)
    delta = -ADAM_LR * (m_hat / (jnp.sqrt(v_hat) + ADAM_EPS) + ADAM_WD * w)
    return delta, m2, v2


def _sum4(name, own, recv, idx):
    _, rows, cols = own.shape
    tr = rows
    while tr * cols * 4 > (1 << 20) and tr % 16 == 0:
        tr //= 2

    def body(idx_ref, o_ref, r0_ref, r1_ref, r2_ref, out_ref):
        out_ref[...] = ((o_ref[...] + r0_ref[...].astype(F32)) + r1_ref[...].astype(F32)) + r2_ref[...].astype(F32)

    blk = (None, tr, cols)
    grid_spec = pltpu.PrefetchScalarGridSpec(
        num_scalar_prefetch=1, grid=(rows // tr,),
        in_specs=[pl.BlockSpec(blk, lambda i, s: (s[0], i, 0)), pl.BlockSpec(blk, lambda i, s: (0, i, 0)),
                  pl.BlockSpec(blk, lambda i, s: (1, i, 0)), pl.BlockSpec(blk, lambda i, s: (2, i, 0))],
        out_specs=pl.BlockSpec((tr, cols), lambda i, s: (i, 0)))
    return pl.pallas_call(body, grid_spec=grid_spec, out_shape=SDS((rows, cols), F32), name=name,
                          compiler_params=_params(1))(jnp.reshape(idx, (1,)).astype(jnp.int32), own, recv, recv, recv)


def _adam_pair(name, item, after=None):
    def fn(w_, a, b, m_, v_):
        g = a + b
        return (g,) + _adam_math(w_, g, m_, v_)

    return _ew_call(name, fn, list(item), 4, after)


def _place():
    return lax.axis_index("x"), lax.axis_index("y"), lax.axis_index("c")


def _other_chips(x, y):
    return [(1 - x, y), (x, 1 - y), (1 - x, 1 - y)]


def _gather_chips(shards):
    n = len(shards)

    def copies(ins, outs, sems):
        send, recv, fwd_send, fwd_recv, loc = sems
        x, y, c = _place()
        me = 2 * x + y
        peers = _other_chips(x, y)
        local = [pltpu.make_async_copy(ins[a], outs[a].at[me], loc.at[a]) for a in range(n)]
        sends, recvs, passes, passed = [], [], [], []
        for a in range(n):
            half = shards[a].shape[0] // 2
            mine = pl.ds(c * half, half)
            theirs = pl.ds((1 - c) * half, half)
            for j, (px, py) in enumerate(peers):
                far = 2 * px + py
                sends.append(pltpu.make_async_remote_copy(
                    src_ref=ins[a].at[mine], dst_ref=outs[a].at[me, mine], send_sem=send.at[a, j],
                    recv_sem=recv.at[a, j], device_id=(px, py, c), device_id_type=MESH_ID))
                recvs.append(pltpu.make_async_remote_copy(
                    src_ref=ins[a].at[mine], dst_ref=outs[a].at[far, mine], send_sem=send.at[a, j],
                    recv_sem=recv.at[a, j], device_id=(px, py, c), device_id_type=MESH_ID))
                passes.append(pltpu.make_async_remote_copy(
                    src_ref=outs[a].at[far, mine], dst_ref=outs[a].at[far, mine], send_sem=fwd_send.at[a, j],
                    recv_sem=fwd_recv.at[a, j], device_id=(x, y, 1 - c), device_id_type=MESH_ID))
                passed.append(pltpu.make_async_remote_copy(
                    src_ref=outs[a].at[far, theirs], dst_ref=outs[a].at[far, theirs], send_sem=fwd_send.at[a, j],
                    recv_sem=fwd_recv.at[a, j], device_id=(x, y, 1 - c), device_id_type=MESH_ID))
        return local, sends, recvs, passes, passed

    def start(ins, outs, sems):
        local, sends, _, _, _ = copies(ins, outs, sems)
        for cp in local + sends:
            cp.start()

    def wait(ins, outs, sems):
        local, sends, recvs, passes, passed = copies(ins, outs, sems)
        for got, on in zip(recvs, passes):
            got.wait_recv()
            on.start()
        for cp in passed:
            cp.wait_recv()
        for cp in passes + sends:
            cp.wait_send()
        for cp in local:
            cp.wait()

    assert all(s.shape[0] % 32 == 0 for s in shards)
    pair = pltpu.SemaphoreType.DMA((n, 3))
    return _Exchange(shards, [SDS((N_CHIPS,) + s.shape, s.dtype) for s in shards],
                     [pair, pair, pair, pair, pltpu.SemaphoreType.DMA((n,))], start, wait)


def _scatter_chips(chunks):
    n = len(chunks)

    def copies(ins, outs, sems):
        send, recv = sems
        x, y, c = _place()
        return [pltpu.make_async_remote_copy(
            src_ref=ins[a].at[2 * px + py], dst_ref=outs[a].at[j], send_sem=send.at[a, j],
            recv_sem=recv.at[a, j], device_id=(px, py, c), device_id_type=MESH_ID)
            for a in range(n) for j, (px, py) in enumerate(_other_chips(x, y))]

    def start(ins, outs, sems):
        for cp in copies(ins, outs, sems):
            cp.start()

    def wait(ins, outs, sems):
        cps = copies(ins, outs, sems)
        for cp in cps:
            cp.wait_recv()
        for cp in cps:
            cp.wait_send()

    return _Exchange(chunks, [SDS((3,) + s.shape[1:], s.dtype) for s in chunks],
                     [pltpu.SemaphoreType.DMA((n, 3)), pltpu.SemaphoreType.DMA((n, 3))], start, wait)


HBM = pl.BlockSpec(memory_space=pltpu.HBM)
SEM = pl.BlockSpec(memory_space=pltpu.SEMAPHORE)
DATAFLOW = pltpu.SideEffectType.DATAFLOW_SIDE_EFFECTING


def _chunk_copies(src_ref, land_ref, send_sems, recv_sems):
    x, y, c = _place()
    return [pltpu.make_async_remote_copy(
        src_ref=src_ref.at[2 * px + py], dst_ref=land_ref.at[k], send_sem=send_sems.at[k], recv_sem=recv_sems.at[k],
        device_id=(px, py, c), device_id_type=MESH_ID) for k, (px, py) in enumerate(_other_chips(x, y))]


def _scatter_start(name, chunks):
    def body(src_ref, land_ref, send_sems, recv_sems, src_thru, land_thru, token):
        for cp in _chunk_copies(src_ref, land_ref, send_sems, recv_sems):
            cp.start()
        token[...] = jnp.zeros_like(token)

    land = (3,) + chunks.shape[1:]
    return pl.pallas_call(
        body, name=name,
        out_shape=(pltpu.SemaphoreType.DMA((3,)), pltpu.SemaphoreType.DMA((3,)), pltpu.HBM(chunks.shape, chunks.dtype),
                   pltpu.HBM(land, chunks.dtype), SDS((SUBLANES, LANES), F32)),
        in_specs=(HBM, HBM), out_specs=(SEM, SEM, HBM, HBM, pl.BlockSpec(memory_space=pltpu.VMEM)),
        input_output_aliases={0: 2, 1: 3}, compiler_params=pltpu.CompilerParams(has_side_effects=DATAFLOW),
    )(pltpu.with_memory_space_constraint(chunks, pltpu.HBM),
      pltpu.with_memory_space_constraint(lax.empty(land, chunks.dtype), pltpu.HBM))


def _scatter_wait(name, send_sems, recv_sems, src_thru, land_thru, after):
    def body(src_ref, land_ref, send_sems, recv_sems, after_ref, src_dead, got_ref):
        for cp in _chunk_copies(src_ref, land_ref, send_sems, recv_sems):
            cp.wait_send()
            cp.wait_recv()

    return pl.pallas_call(
        body, name=name,
        out_shape=(pltpu.HBM(src_thru.shape, src_thru.dtype), pltpu.HBM(land_thru.shape, land_thru.dtype)),
        in_specs=(HBM, HBM, SEM, SEM, ANY), out_specs=(HBM, HBM), input_output_aliases={0: 0, 1: 1},
        compiler_params=pltpu.CompilerParams(has_side_effects=DATAFLOW),
    )(src_thru, land_thru, send_sems, recv_sems, after)[1]


def _swap_sibling(arrs):
    n = len(arrs)

    def copies(ins, outs, sems):
        send, recv = sems
        x, y, c = _place()
        return [pltpu.make_async_remote_copy(
            src_ref=ins[a], dst_ref=outs[a], send_sem=send.at[a], recv_sem=recv.at[a],
            device_id=(x, y, 1 - c), device_id_type=MESH_ID) for a in range(n)]

    def start(ins, outs, sems):
        for cp in copies(ins, outs, sems):
            cp.start()

    def wait(ins, outs, sems):
        cps = copies(ins, outs, sems)
        for cp in cps:
            cp.wait_recv()
        for cp in cps:
            cp.wait_send()

    return _Exchange(arrs, [SDS(s.shape, s.dtype) for s in arrs],
                     [pltpu.SemaphoreType.DMA((n,)), pltpu.SemaphoreType.DMA((n,))], start, wait)


N_DEV = 8


def _gather_devices(block):
    def copies(ins, outs, sems):
        send, recv, loc = sems
        x, y, c = _place()
        me = 4 * x + 2 * y + c
        local = pltpu.make_async_copy(ins[0], outs[0].at[me], loc.at[0])
        sends, recvs = [], []
        for k in range(1, N_DEV):
            peer = (x ^ (k >> 2), y ^ ((k >> 1) & 1), c ^ (k & 1))
            for group, slot in ((sends, me), (recvs, me ^ k)):
                group.append(pltpu.make_async_remote_copy(
                    src_ref=ins[0], dst_ref=outs[0].at[slot], send_sem=send.at[k - 1], recv_sem=recv.at[k - 1],
                    device_id=peer, device_id_type=MESH_ID))
        return local, sends, recvs

    def start(ins, outs, sems):
        local, sends, _ = copies(ins, outs, sems)
        for cp in [local] + sends:
            cp.start()

    def wait(ins, outs, sems):
        local, sends, recvs = copies(ins, outs, sems)
        for cp in recvs:
            cp.wait_recv()
        for cp in sends:
            cp.wait_send()
        local.wait()

    return _Exchange([block], [SDS((N_DEV,) + block.shape, block.dtype)],
                     [pltpu.SemaphoreType.DMA((N_DEV - 1,)), pltpu.SemaphoreType.DMA((N_DEV - 1,)),
                      pltpu.SemaphoreType.DMA((1,))], start, wait)


def _both(ex_a, ex_b):
    na_i, na_o, na_s = len(ex_a.ins), len(ex_a.outs), len(ex_a.sems)

    def start(ins, outs, sems):
        ex_a.start(ins[:na_i], outs[:na_o], sems[:na_s])
        ex_b.start(ins[na_i:], outs[na_o:], sems[na_s:])

    def wait(ins, outs, sems):
        ex_a.wait(ins[:na_i], outs[:na_o], sems[:na_s])
        ex_b.wait(ins[na_i:], outs[na_o:], sems[na_s:])

    return _Exchange(ex_a.ins + ex_b.ins, ex_a.outs + ex_b.outs, ex_a.sems + ex_b.sems, start, wait)


def _sum_devices(slots):
    def body(s_ref, o_ref):
        acc = s_ref[0]
        for d in range(1, N_DEV):
            acc = acc + s_ref[d]
        o_ref[...] = acc

    return pl.pallas_call(
        body, in_specs=[pl.BlockSpec(memory_space=pltpu.VMEM)], out_specs=pl.BlockSpec(memory_space=pltpu.VMEM),
        out_shape=SDS(slots.shape[1:], F32), name="sum_small",
        compiler_params=pltpu.CompilerParams(vmem_limit_bytes=32 * 1024 * 1024))(slots)


def _adam_small(ws, gs, ms, vs):
    n = len(ws)

    def body(*refs):
        for i in range(n):
            w_ref, g_ref, m_ref, v_ref = (refs[k * n + i] for k in range(4))
            outs = _adam_math(w_ref[...], g_ref[...], m_ref[...], v_ref[...])
            for k in range(3):
                refs[(4 + k) * n + i][...] = outs[k]

    vmem = pl.BlockSpec(memory_space=pltpu.VMEM)
    return pl.pallas_call(
        body, in_specs=[vmem] * (4 * n), out_specs=[vmem] * (3 * n),
        out_shape=[SDS(w.shape, F32) for w in ws] * 3, name="adam_small",
        compiler_params=pltpu.CompilerParams(vmem_limit_bytes=32 * 1024 * 1024))(*ws, *gs, *ms, *vs)


def _local_step(x, target, small, big, tb, distributed):
    g1, g2, g3, g4 = small["norm_mix_pre"], small["norm_mix_post"], small["norm_mlp_pre"], small["norm_mlp_post"]
    dist = distributed
    me = (2 * lax.axis_index("x") + lax.axis_index("y")) if dist else 0
    tb_ssm = min(tb, 256)
    bucket = jnp.asarray(_bucket_table())

    keys_first = lambda t: jnp.swapaxes(t, -1, -2)
    bias = _bias_table(small["rel_bias"], bucket)
    sink_rows = keys_first(_pair_layout(jnp.broadcast_to(small["sinks"].reshape(N_HEADS, 1, 1), (N_HEADS, BLOCK, 1))))
    disc_args = (small["lam_re"], small["lam_im"], small["log_dt"], small["b_re"], small["b_im"])
    (ab_re, ab_im, bb_re, bb_im), disc_vjp = jax.vjp(_ssm_discretize, *disc_args)
    tab_f, tab_b = _scan_tables(ab_re, ab_im)
    bmat = _bf(_b_matrix(bb_re, bb_im))
    cmat = _bf(_c_matrix(small["c_re"], small["c_im"]))
    d_skip = small["d_skip"]

    if dist:
        (g_in,) = _exchange_alone("gather_w_in", _gather_chips([big["w_in"]]))
        w_in = g_in.reshape(IN_W, D_MODEL)
    else:
        w_in = big["w_in"]
    mix = ("w_glu", "w_attn_branch", "w_ssm_branch", "w_out")
    outs = _inproj_fwd(x, g1, w_in, tb, _gather_chips([big[n] for n in mix]) if dist else None)
    h1, q, k, v, u, ga, gs = outs[:7]
    w_glu, w_ab, w_sb, w_out = outs[7:] if dist else [big[n] for n in mix]
    w_glu = w_glu.reshape(SSM_W, SSM_W)
    w_out = w_out.reshape(D_MODEL, D_MODEL)
    outs = _attn_fwd(q, k, v, bias, sink_rows, _gather_chips([big["w_ff_in"]]) if dist else None)
    att = outs[0]
    w_ffi = [outs[1] if dist else big["w_ff_in"]]
    outs = _ssm_fwd(u, bmat, cmat, tab_f, d_skip, tb_ssm, _gather_chips([big["w_ff_out"]]) if dist else None)
    s, h = outs[:2]
    w_ffo = outs[2] if dist else big["w_ff_out"]
    x2 = _merge_fwd(x, s, att, ga, gs, g2, w_glu, w_ab, w_sb, w_out, tb)
    dy, df, h3, loss_acc, dg4 = _mlp_fwd_loss(x2, target, g3, g4, w_ffi, w_ffo, tb)

    dx2, act, da, dg3 = _mlp_bwd(x2, dy, df, h3, g3, w_ffi, w_ffo, tb)
    tl = min(2048, x.shape[0])
    chunked = (N_CHIPS, D_FF // N_CHIPS, D_MODEL)
    d_ffi, b_ffi = _matmul_tn("grad_w_ff_in", h3, da, D_MODEL, D_FF // FF_CHUNKS, tl, True)
    d_ffo, b_ffo = _matmul_tn("grad_w_ff_out", act, df, D_FF // FF_CHUNKS, D_MODEL, tl, False)
    d_ffo, b_ffo = d_ffo.reshape(chunked), b_ffo.reshape(chunked)
    outs = _merge_bwd(dx2, s, att, ga, gs, g2, w_glu, w_ab, w_sb, w_out, tb_ssm,
                      _scatter_chips([b_ffi]) if dist else None)
    ds, datt, dga, dgs, dg2, d_glu, d_ab, d_sb, d_out, b_glu, b_ab, b_sb, b_out = outs[:13]
    r_ffi = outs[13:]
    glu4, out4 = (N_CHIPS, SSM_W // N_CHIPS, SSM_W), (N_CHIPS, D_MODEL // N_CHIPS, D_MODEL)
    d_mix = [d_glu.reshape(glu4), d_ab, d_sb, d_out.reshape(out4)]
    b_mix = [b_glu.reshape(glu4), b_ab, b_sb, b_out.reshape(out4)]
    outs = _ssm_bwd(ds, u, h, bmat.transpose(0, 2, 1), cmat.transpose(0, 2, 1), tab_b, d_skip, tb_ssm,
                    _scatter_chips([b_ffo]) if dist else None)
    du, d_bmat, d_cmat, da_acc, dd_skip = outs[:5]
    r_ffo = outs[5:]
    outs = _attn_bwd(q, k, v, datt, bias, sink_rows, _scatter_chips(b_mix) if dist else None)
    dq, dk, dv, dbias, dsink_rows = outs[:5]
    r_mix = outs[5:]
    if dist:
        p_ffi = _sum4("sum_w_ff_in", d_ffi, r_ffi[0], me)
        p_ffo = _sum4("sum_w_ff_out", d_ffo, r_ffo[0], me)
    dx, dpj, dg1 = _inproj_bwd(x, dx2, dq, dk, dv, du, dga, dgs, g1, w_in, tb)

    dab_re, dab_im = _state_unlayout(jnp.sum(da_acc, axis=0))
    dbb_re, dbb_im = _b_matrix_grad(d_bmat)
    d_lam_re, d_lam_im, d_log_dt, d_b_re, d_b_im = disc_vjp((dab_re, dab_im, dbb_re, dbb_im))
    d_c_re, d_c_im = _c_matrix_grad(d_cmat)
    d_rel = _bias_grad(dbias, bucket)
    d_sinks = jnp.sum(_pair_unlayout(keys_first(dsink_rows)), axis=(1, 2))
    small_grads = dict(
        norm_mix_pre=dg1, norm_mix_post=dg2, norm_mlp_pre=dg3, norm_mlp_post=dg4, rel_bias=d_rel, sinks=d_sinks,
        lam_re=d_lam_re, lam_im=d_lam_im, log_dt=d_log_dt, b_re=d_b_re, b_im=d_b_im, c_re=d_c_re, c_im=d_c_im,
        d_skip=dd_skip)
    ride = _both(_swap_sibling([p_ffi, p_ffo]), _gather_devices(_pack(small_grads, loss_acc))) if dist else None
    outs = _matmul_tn("grad_w_in", dpj, h1, IN_W // 2, D_MODEL, tl, False, ride)
    in4 = (N_CHIPS, IN_W // N_CHIPS, D_MODEL)
    d_in, b_in = outs[0].reshape(in4), outs[1].reshape(in4)
    if not dist:
        return loss_acc, dx, small_grads, dict(zip(BIG, [d_in] + d_mix + [d_ffi, d_ffo]))
    s_ffi, s_ffo, slots = outs[2:]
    p_mix = [_sum4("sum_" + n, d, r, me) for n, d, r in zip(mix, d_mix, r_mix)]
    pending = dict(d_in=d_in, b_in=b_in, p_mix=p_mix, w_ff_in=(p_ffi, s_ffi), w_ff_out=(p_ffo, s_ffo), me=me)
    return loss_acc, dx, _sum_devices(slots), pending


SMALL = ['norm_mix_pre', 'norm_mix_post', 'norm_mlp_pre', 'norm_mlp_post', 'rel_bias', 'sinks', 'lam_re', 'lam_im',
         'log_dt', 'b_re', 'b_im', 'c_re', 'c_im', 'd_skip']
BIG = ['w_in', 'w_glu', 'w_attn_branch', 'w_ssm_branch', 'w_out', 'w_ff_in', 'w_ff_out']
WEIGHTS = ['norm_mix_pre', 'norm_mix_post', 'norm_mlp_pre', 'norm_mlp_post', 'w_in', 'rel_bias', 'sinks', 'lam_re',
           'lam_im', 'log_dt', 'b_re', 'b_im', 'c_re', 'c_im', 'd_skip', 'w_glu', 'w_attn_branch', 'w_ssm_branch',
           'w_out', 'w_ff_in', 'w_ff_out']
PACK_COLS = 1024
PACK_ORDER = ['b_re', 'b_im', 'c_re', 'c_im', 'lam_re', 'lam_im', 'norm_mix_pre', 'norm_mix_post', 'norm_mlp_pre',
              'norm_mlp_post', 'rel_bias', 'sinks', 'log_dt', 'd_skip']


STATE_MINOR = ('b_re', 'b_im')
PACK_ROWS = 144
LOSS_ROW = 140


def _pack(named, loss_acc):
    parts = []
    for n in PACK_ORDER:
        a = jnp.swapaxes(named[n], -1, -2) if n in STATE_MINOR else named[n]
        flat = a.reshape(-1)
        rows = -(-flat.shape[0] // PACK_COLS)
        parts.append(jnp.pad(flat, (0, rows * PACK_COLS - flat.shape[0])).reshape(rows, PACK_COLS))
    assert sum(p.shape[0] for p in parts) == LOSS_ROW
    parts.append(jnp.pad(loss_acc[0:1], ((0, PACK_ROWS - LOSS_ROW - 1), (0, PACK_COLS - loss_acc.shape[1]))))
    return jnp.concatenate(parts, axis=0)


def _unpack(packed, shapes):
    out, at = {}, 0
    for n in PACK_ORDER:
        shape = shapes[n][:-2] + (shapes[n][-1], shapes[n][-2]) if n in STATE_MINOR else shapes[n]
        size = int(np.prod(shape))
        rows = -(-size // PACK_COLS)
        blk = packed[at:at + rows]
        out[n] = (blk.reshape(-1)[:size] if size % PACK_COLS else blk).reshape(shape)
        at += rows
    return out


def kernel(x, norm_mix_pre, norm_mix_post, norm_mlp_pre, norm_mlp_post, w_in, rel_bias, sinks, lam_re, lam_im, log_dt, b_re, b_im, c_re, c_im, d_skip, w_glu, w_attn_branch, w_ssm_branch, w_out, w_ff_in, w_ff_out, loss_target, m_norm_mix_pre, m_norm_mix_post, m_norm_mlp_pre, m_norm_mlp_post, m_w_in, m_rel_bias, m_sinks, m_lam_re, m_lam_im, m_log_dt, m_b_re, m_b_im, m_c_re, m_c_im, m_d_skip, m_w_glu, m_w_attn_branch, m_w_ssm_branch, m_w_out, m_w_ff_in, m_w_ff_out, v_norm_mix_pre, v_norm_mix_post, v_norm_mlp_pre, v_norm_mlp_post, v_w_in, v_rel_bias, v_sinks, v_lam_re, v_lam_im, v_log_dt, v_b_re, v_b_im, v_c_re, v_c_im, v_d_skip, v_w_glu, v_w_attn_branch, v_w_ssm_branch, v_w_out, v_w_ff_in, v_w_ff_out):
    env = dict(locals())
    w = {n: env[n] for n in WEIGHTS}
    m = {n: env["m_" + n] for n in WEIGHTS}
    v = {n: env["v_" + n] for n in WEIGHTS}
    seq = x.shape[1]
    tb = min(512, seq)

    small = {n: w[n] for n in ('norm_mix_pre', 'norm_mix_post', 'norm_mlp_pre', 'norm_mlp_post', 'rel_bias')}
    small.update({n: w[n][0] for n in ('sinks', 'lam_re', 'lam_im', 'log_dt', 'b_re', 'b_im', 'c_re', 'c_im')})
    small['d_skip'] = w['d_skip']
    shard = lambda t, n: t[n][0].T if n == 'w_in' else t[n][0]
    unshard = lambda a, n: (a.T if n == 'w_in' else a)[None]
    _, dx, small_g, pending = _local_step(
        x[0], loss_target[0], small, {n: _bf(shard(w, n)) for n in BIG}, tb, True)

    loss = small_g[LOSS_ROW, 0]

    grads, deltas, new_m, new_v = {}, {}, {}, {}

    def adam(n, partials, after=None):
        outs = _adam_pair("adam_" + n, (shard(w, n), *partials, shard(m, n), shard(v, n)), after)
        grads[n], deltas[n], new_m[n], new_v[n] = [unshard(a, n) for a in outs]
        return outs[3]

    mix = ("w_glu", "w_attn_branch", "w_ssm_branch", "w_out")
    *in_flight, token = _scatter_start("scatter_w_in_start", pending["b_in"])
    sib_mix = _exchange_alone("swap_mix", _swap_sibling(pending["p_mix"]))
    last = None
    for n, partials in [(n, pending[n]) for n in ("w_ff_in", "w_ff_out")] + list(zip(mix, zip(pending["p_mix"], sib_mix))):
        last = adam(n, partials, token)
    r_in = _scatter_wait("scatter_w_in_wait", *in_flight, last)
    p_in = _sum4("sum_w_in", pending["d_in"], r_in, pending["me"])
    (s_in,) = _exchange_alone("swap_w_in", _swap_sibling([p_in]))
    adam("w_in", (p_in, s_in))

    minor = lambda t, n: jnp.swapaxes(t, -1, -2) if n in STATE_MINOR else t
    g_small = _unpack(small_g, {n: w[n].shape for n in SMALL})
    outs = _adam_small([minor(w[n], n) for n in SMALL], [g_small[n] for n in SMALL],
                       [minor(m[n], n) for n in SMALL], [minor(v[n], n) for n in SMALL])
    grads.update({n: minor(g_small[n], n) for n in SMALL})
    for k, dst in enumerate((deltas, new_m, new_v)):
        dst.update({n: minor(a, n) for n, a in zip(SMALL, outs[k * len(SMALL):(k + 1) * len(SMALL)])})

    return (loss, dx[None], *[grads[n] for n in WEIGHTS], *[deltas[n] for n in WEIGHTS],
            *[new_m[n] for n in WEIGHTS], *[new_v[n] for n in WEIGHTS])
```

```python
import functools
import math

import numpy as np
import jax
import jax.numpy as jnp
from jax import lax
from jax.experimental import pallas as pl
from jax.experimental.pallas import tpu as pltpu

F32 = jnp.float32
BF16 = jnp.bfloat16

D_MODEL = 1024
N_HEADS = 8
N_KV = 2
Q_GROUP = 4
HEAD_DIM = 64
ATTN_W = 512
KV_W = 128
BLOCK = 128
N_BUCKETS = 32
MAX_DISTANCE = 128
NEG_INF = -1e30
SSM_W = 512
SSM_GROUP = 16
SSM_GROUPS = 32
SSM_STATE = 64
N_SUPER = 4
GROUPS_PER_SUPER = SSM_GROUPS // N_SUPER
SUPER_IN = GROUPS_PER_SUPER * SSM_GROUP
SUPER_HALF = GROUPS_PER_SUPER * SSM_STATE
SUPER_W = 2 * SUPER_HALF
STATE_COLS = N_SUPER * SUPER_W
D_FF = 4096
FF_CHUNKS = 4
IN_W = 3328
SPLITS = (0, 512, 640, 768, 1280, 2304, 3328)
RMS_EPS = 1e-6
N_CHIPS = 4
SUBLANES = 8
LANES = 128
STATE_TILES = STATE_COLS // LANES
SUPER_TILES = SUPER_W // LANES

ADAM_LR = 0.001
ADAM_B1 = 0.9
ADAM_B2 = 0.999
ADAM_EPS = 1e-08
ADAM_WD = 0.01
ADAM_STEP = 10

VMEM_BIG = 56 * 1024 * 1024
SDS = jax.ShapeDtypeStruct
MESH_ID = pl.DeviceIdType.MESH
ANY = pl.BlockSpec(memory_space=pl.ANY)


def _bf(x):
    return x.astype(BF16)


def _mm(a, b):
    return jnp.dot(a, b, preferred_element_type=F32)


def _mm_nt(a, b):
    return lax.dot_general(a, b, (((1,), (1,)), ((), ())), preferred_element_type=F32)


def _mm_tn(a, b):
    return lax.dot_general(a, b, (((0,), (0,)), ((), ())), preferred_element_type=F32)


def _sig(x):
    return 1.0 / (1.0 + jnp.exp(-x))


def _rms(x, g):
    r = lax.rsqrt(jnp.mean(x * x, axis=-1, keepdims=True) + RMS_EPS)
    xh = x * r
    return xh * g, xh, r


def _rms_bwd(dout, xh, r, g):
    dg = jnp.sum(dout * xh, axis=0, keepdims=True)
    dxh = dout * g
    dx = r * (dxh - xh * jnp.mean(dxh * xh, axis=-1, keepdims=True))
    return dx, dg


_GELU_C = math.sqrt(2.0 / math.pi)


def _gelu_and_grad(x):
    x2 = x * x
    inner = _GELU_C * (x + 0.044715 * (x2 * x))
    t = jnp.tanh(inner)
    y = 0.5 * x * (1.0 + t)
    dy = 0.5 * (1.0 + t) + 0.5 * x * (1.0 - t * t) * (_GELU_C * (1.0 + 3.0 * 0.044715 * x2))
    return y, dy


def _zero_map(nd, *_):
    return (0,) * nd


def _params(n_axes, vmem=None):
    return pltpu.CompilerParams(dimension_semantics=("arbitrary",) * n_axes, vmem_limit_bytes=vmem)


class _Exchange:
    def __init__(self, ins, outs, sems, start, wait):
        self.ins, self.outs, self.sems, self.start, self.wait = list(ins), list(outs), list(sems), start, wait


def _fused_call(name, body, grid, in_specs, out_specs, out_shape, scratch, args, exchange, params):
    n_in, n_out, n_scr = len(in_specs), len(out_specs), len(scratch)
    if exchange is None:
        fn = body
    else:
        ex = exchange
        n_xi, n_xo = len(ex.ins), len(ex.outs)

        def fn(*refs):
            at = 0
            parts = []
            for n in (n_in, n_xi, n_out, n_xo, n_scr, len(ex.sems)):
                parts.append(refs[at:at + n])
                at += n
            ins, x_in, outs, x_out, scr, x_sem = parts
            ids = [pl.program_id(a) for a in range(len(grid))]
            first = functools.reduce(jnp.logical_and, [i == 0 for i in ids])
            last = functools.reduce(jnp.logical_and, [i == g - 1 for i, g in zip(ids, grid)])

            @pl.when(first)
            def _():
                ex.start(x_in, x_out, x_sem)

            body(*ins, *outs, *scr)

            @pl.when(last)
            def _():
                ex.wait(x_in, x_out, x_sem)

        in_specs = list(in_specs) + [ANY] * n_xi
        out_specs = list(out_specs) + [ANY] * n_xo
        out_shape = list(out_shape) + ex.outs
        scratch = list(scratch) + ex.sems
        args = list(args) + ex.ins
    return pl.pallas_call(fn, grid=grid, in_specs=in_specs, out_specs=out_specs, out_shape=out_shape,
                          scratch_shapes=list(scratch), name=name, compiler_params=params)(*args)


def _exchange_alone(name, ex):
    def body(*refs):
        n_xi, n_xo = len(ex.ins), len(ex.outs)
        x_in, x_out, x_sem = refs[:n_xi], refs[n_xi:n_xi + n_xo], refs[n_xi + n_xo:]
        ex.start(x_in, x_out, x_sem)
        ex.wait(x_in, x_out, x_sem)

    return pl.pallas_call(body, in_specs=[ANY] * len(ex.ins), out_specs=[ANY] * len(ex.outs), out_shape=ex.outs,
                          scratch_shapes=ex.sems, name=name)(*ex.ins)


def _rowcall(name, body, seq, tb, rows, consts, row_outs, acc_outs, scratch=(), reverse=False, vmem=None,
             exchange=None):
    nb = seq // tb
    rmap = (lambda i: (nb - 1 - i, 0)) if reverse else (lambda i: (i, 0))
    tmap = lambda i: (0,) + rmap(i)

    def row_spec(width):
        if isinstance(width, tuple):
            return pl.BlockSpec((width[0], tb, width[1]), tmap)
        return pl.BlockSpec((tb, width), rmap)

    def row_shape(width):
        return (width[0], seq, width[1]) if isinstance(width, tuple) else (seq, width)

    in_specs = [row_spec(a.shape[1] if a.ndim == 2 else (a.shape[0], a.shape[2])) for a in rows]
    in_specs += [pl.BlockSpec(a.shape, functools.partial(_zero_map, a.ndim), pipeline_mode=pl.Buffered(1))
                 for a in consts]
    out_specs = [row_spec(c) for c, _ in row_outs] + [ANY] * len(acc_outs)
    out_shape = [SDS(row_shape(c), dt) for c, dt in row_outs] + [SDS(s, dt) for s, dt in acc_outs]
    n_main = len(rows) + len(consts) + len(row_outs)
    n_acc = len(acc_outs)

    def fn(*refs):
        main, acc_hbm, rest = refs[:n_main], refs[n_main:n_main + n_acc], refs[n_main + n_acc:]
        acc_vmem, own = rest[:n_acc], rest[n_acc:]
        body(*main, *acc_vmem, *own)

        @pl.when(pl.program_id(0) == nb - 1)
        def _():
            for src, dst in zip(acc_vmem, acc_hbm):
                pltpu.sync_copy(src, dst)

    buffers = [pltpu.VMEM(s, dt) for s, dt in acc_outs] + list(scratch)
    return _fused_call(name, fn if acc_outs else body, (nb,), in_specs, out_specs, out_shape, buffers,
                       [*rows, *consts], exchange, _params(1, vmem))


def _inproj_fwd(x, g1, w_in, tb, exchange=None):
    seq = x.shape[0]

    def body(x_ref, g_ref, w_ref, h_ref, q_ref, k_ref, v_ref, u_ref, ga_ref, gs_ref):
        h, _, _ = _rms(x_ref[...], g_ref[...])
        hb = _bf(h)
        h_ref[...] = hb
        pj = _mm_nt(hb, w_ref[...])
        q_ref[...] = _bf(pj[:, SPLITS[0]:SPLITS[1]])
        k_ref[...] = _bf(pj[:, SPLITS[1]:SPLITS[2]])
        v_ref[...] = _bf(pj[:, SPLITS[2]:SPLITS[3]])
        u_ref[...] = pj[:, SPLITS[3]:SPLITS[4]]
        ga_ref[...] = pj[:, SPLITS[4]:SPLITS[5]]
        gs_ref[...] = pj[:, SPLITS[5]:SPLITS[6]]

    return _rowcall("inproj_fwd", body, seq, tb, [x], [g1, w_in],
                    [(D_MODEL, BF16), (ATTN_W, BF16), (KV_W, BF16), (KV_W, BF16), (SSM_W, F32),
                     (D_MODEL, F32), (D_MODEL, F32)], [], vmem=VMEM_BIG, exchange=exchange)


def _inproj_bwd(x, dx2, dq, dk, dv, du, dga, dgs, g1, w_in, tb, exchange=None):
    seq = x.shape[0]

    def body(x_ref, dx2_ref, dq_ref, dk_ref, dv_ref, du_ref, dga_ref, dgs_ref, g_ref, w_ref,
             dx_ref, dpj_ref, dg_ref):
        @pl.when(pl.program_id(0) == 0)
        def _():
            dg_ref[...] = jnp.zeros_like(dg_ref)

        dpj = jnp.concatenate([dq_ref[...], dk_ref[...], dv_ref[...], _bf(du_ref[...]),
                               dga_ref[...], dgs_ref[...]], axis=1)
        dpj_ref[...] = dpj
        dh = _mm(dpj, w_ref[...])
        g = g_ref[...]
        _, xh, r = _rms(x_ref[...], g)
        dxn, dg = _rms_bwd(dh, xh, r, g)
        dx_ref[...] = dx2_ref[...] + dxn
        dg_ref[...] += dg

    return _rowcall("inproj_bwd", body, seq, tb, [x, dx2, dq, dk, dv, du, dga, dgs], [g1, w_in],
                    [(D_MODEL, F32), (IN_W, BF16)], [((1, D_MODEL), F32)], vmem=VMEM_BIG, exchange=exchange)


def _bucket_table():
    qi = np.arange(BLOCK)[:, None]
    kj = np.arange(2 * BLOCK)[None, :]
    dist = qi + BLOCK - kj
    max_exact = N_BUCKETS // 2
    d = np.maximum(dist, 0)
    df = np.maximum(d, 1).astype(np.float32)
    large = max_exact + (np.log(df / np.float32(max_exact)) / np.float32(math.log(MAX_DISTANCE / max_exact))
                         * np.float32(N_BUCKETS - max_exact)).astype(np.int32)
    large = np.minimum(large, N_BUCKETS - 1)
    bucket = np.where(d < max_exact, d, large)
    valid = (dist >= 0) & (dist < BLOCK)
    return np.where(valid, bucket, -1).astype(np.int32)


def _bias_table(rel_bias, bucket):
    def body(rb_ref, bk_ref, o_ref):
        bk = bk_ref[...]
        has_prev = lax.broadcasted_iota(jnp.int32, bk.shape, 1) >= BLOCK
        for h in range(N_HEADS):
            kh, j, par = h // Q_GROUP, (h // 2) % 2, h % 2
            acc = jnp.full((BLOCK, 2 * BLOCK), NEG_INF, F32)
            for b in range(N_BUCKETS):
                acc = jnp.where(bk == b, rb_ref[b, h], acc)
            o_ref[0, kh, par, :, j * BLOCK:(j + 1) * BLOCK] = jnp.where(has_prev, acc, NEG_INF).T
            o_ref[1, kh, par, :, j * BLOCK:(j + 1) * BLOCK] = acc.T

    return pl.pallas_call(
        body, out_shape=SDS((2, N_KV, 2, 2 * BLOCK, 2 * BLOCK), F32),
        in_specs=[pl.BlockSpec(memory_space=pltpu.SMEM), pl.BlockSpec(memory_space=pltpu.VMEM)],
        out_specs=pl.BlockSpec(memory_space=pltpu.VMEM), name="bias_table",
    )(rel_bias, bucket)


def _bias_grad(dbias, bucket):
    def body(db_ref, bk_ref, o_ref):
        bk = bk_ref[...]
        for h in range(N_HEADS):
            kh, j, par = h // Q_GROUP, (h // 2) % 2, h % 2
            db = db_ref[kh, par, :, j * BLOCK:(j + 1) * BLOCK].T
            for b in range(N_BUCKETS):
                o_ref[b, h] = jnp.sum(jnp.where(bk == b, db, 0.0))

    return pl.pallas_call(
        body, out_shape=SDS((N_BUCKETS, N_HEADS), F32),
        in_specs=[pl.BlockSpec(memory_space=pltpu.VMEM), pl.BlockSpec(memory_space=pltpu.VMEM)],
        out_specs=pl.BlockSpec(memory_space=pltpu.SMEM), name="bias_grad",
    )(dbias, bucket)


TILE = 2 * HEAD_DIM


def _pair_layout(t):
    lead = t.shape[:-3]
    t = t.reshape(lead + (N_KV, 2, 2) + t.shape[-2:])
    nl = len(lead)
    t = jnp.transpose(t, tuple(range(nl)) + (nl, nl + 2, nl + 1, nl + 3, nl + 4))
    return t.reshape(lead + (N_KV, 2, 2 * BLOCK, t.shape[-1]))


def _pair_unlayout(t):
    t = t.reshape(N_KV, 2, 2, BLOCK, t.shape[-1]).transpose(0, 2, 1, 3, 4)
    return t.reshape(N_HEADS, BLOCK, t.shape[-1])


def _halves(t):
    tf = t.astype(F32)
    low = lax.broadcasted_iota(jnp.int32, tf.shape, 1) < HEAD_DIM
    swapped = pltpu.roll(tf, HEAD_DIM, 1)
    zero = jnp.zeros_like(tf)
    return ((_bf(jnp.where(low, tf, zero)), _bf(jnp.where(low, zero, swapped))),
            (_bf(jnp.where(low, swapped, zero)), _bf(jnp.where(low, zero, tf))))


def _fold_halves(even, odd):
    low = lax.broadcasted_iota(jnp.int32, even.shape, 1) < HEAD_DIM
    comb = jnp.where(low, even, odd)
    return comb + pltpu.roll(comb, HEAD_DIM, 1)


def _tile_rows(ref, kh):
    return jnp.concatenate([ref[:, (2 * kh) * TILE:(2 * kh + 1) * TILE],
                            ref[:, (2 * kh + 1) * TILE:(2 * kh + 2) * TILE]], axis=0)


def _halves_t(t):
    tt = t.astype(F32).T
    top = lax.broadcasted_iota(jnp.int32, tt.shape, 0) < HEAD_DIM
    swapped = jnp.concatenate([tt[HEAD_DIM:], tt[:HEAD_DIM]], axis=0)
    zero = jnp.zeros_like(tt)
    return ((_bf(jnp.where(top, tt, zero)), _bf(jnp.where(top, zero, swapped))),
            (_bf(jnp.where(top, swapped, zero)), _bf(jnp.where(top, zero, tt))))


def _attn_probs(km, qk, bias, sink):
    lg = _mm_nt(km, qk) * (HEAD_DIM ** -0.5) + bias
    m = jnp.maximum(jnp.max(lg, axis=0, keepdims=True), sink)
    p = jnp.exp(lg - m)
    es = jnp.exp(sink - m)
    inv = 1.0 / (jnp.sum(p, axis=0, keepdims=True) + es)
    return p * inv, es * inv


def _attn_fwd(q, k, v, bias, sink_rows, exchange=None):
    seq = q.shape[0]
    nblk = seq // BLOCK

    def body(q_ref, kp_ref, kc_ref, vp_ref, vc_ref, b_ref, s_ref, o_ref):
        which = jnp.minimum(pl.program_id(0), 1)
        kms = _halves(jnp.concatenate([kp_ref[...], kc_ref[...]], axis=0))
        vts = _halves_t(jnp.concatenate([vp_ref[...], vc_ref[...]], axis=0))
        for kh in range(N_KV):
            qk = _tile_rows(q_ref, kh)
            acc = jnp.zeros((TILE, 2 * BLOCK), F32)
            for par in range(2):
                pr, _ = _attn_probs(kms[kh][par], qk, b_ref[which, kh, par], s_ref[kh, par])
                acc = acc + _mm(vts[kh][par], _bf(pr))
            acc = acc.T
            o_ref[:, (2 * kh) * TILE:(2 * kh + 1) * TILE] = _bf(acc[:BLOCK])
            o_ref[:, (2 * kh + 1) * TILE:(2 * kh + 2) * TILE] = _bf(acc[BLOCK:])

    cur = lambda n: (n, 0)
    prev = lambda n: (jnp.maximum(n - 1, 0), 0)
    return _fused_call(
        "attn_fwd", body, (nblk,),
        [pl.BlockSpec((BLOCK, ATTN_W), cur),
         pl.BlockSpec((BLOCK, KV_W), prev), pl.BlockSpec((BLOCK, KV_W), cur),
         pl.BlockSpec((BLOCK, KV_W), prev), pl.BlockSpec((BLOCK, KV_W), cur),
         pl.BlockSpec(bias.shape, functools.partial(_zero_map, bias.ndim)),
         pl.BlockSpec(sink_rows.shape, functools.partial(_zero_map, sink_rows.ndim))],
        [pl.BlockSpec((BLOCK, ATTN_W), cur)], [SDS((seq, ATTN_W), BF16)], [],
        [q, k, k, v, v, bias, sink_rows], exchange, _params(1))


def _attn_bwd(q, k, v, d_out, bias, sink_rows, exchange=None):
    seq = q.shape[0]
    nblk = seq // BLOCK

    def body(q_ref, kp_ref, kc_ref, vp_ref, vc_ref, do_ref, b_ref, s_ref,
             dq_ref, dk_ref, dv_ref, db_ref, ds_ref, ck_ref, cv_ref):
        n = pl.program_id(0)

        @pl.when(n == 0)
        def _():
            db_ref[...] = jnp.zeros_like(db_ref)
            ds_ref[...] = jnp.zeros_like(ds_ref)
            ck_ref[...] = jnp.zeros_like(ck_ref)
            cv_ref[...] = jnp.zeros_like(cv_ref)

        @pl.when(n < nblk)
        def _():
            which = jnp.minimum(n, 1)
            scale = HEAD_DIM ** -0.5
            kcat = jnp.concatenate([kp_ref[...], kc_ref[...]], axis=0)
            kms = _halves(kcat)
            kts = _halves_t(kcat)
            vms = _halves(jnp.concatenate([vp_ref[...], vc_ref[...]], axis=0))
            dks, dvs = [], []
            for kh in range(N_KV):
                qk = _tile_rows(q_ref, kh)
                dok = _tile_rows(do_ref, kh)
                dq = jnp.zeros((TILE, 2 * BLOCK), F32)
                dkp, dvp = [], []
                for par in range(2):
                    pr, ps = _attn_probs(kms[kh][par], qk, b_ref[which, kh, par], s_ref[kh, par])
                    dp = _mm_nt(vms[kh][par], dok)
                    rs = jnp.sum(pr * dp, axis=0, keepdims=True)
                    dlg = pr * (dp - rs)
                    ds_ref[kh, par] += -ps * rs
                    db_ref[kh, par] += dlg
                    dlb = _bf(dlg)
                    dq = dq + _mm(kts[kh][par], dlb)
                    dkp.append(_mm(dlb, qk))
                    dvp.append(_mm(_bf(pr), dok))
                dq = _bf((dq * scale).T)
                dq_ref[:, (2 * kh) * TILE:(2 * kh + 1) * TILE] = dq[:BLOCK]
                dq_ref[:, (2 * kh + 1) * TILE:(2 * kh + 2) * TILE] = dq[BLOCK:]
                dks.append(_fold_halves(*dkp))
                dvs.append(_fold_halves(*dvp))
            low = lax.broadcasted_iota(jnp.int32, (2 * BLOCK, TILE), 1) < HEAD_DIM
            dkk = jnp.where(low, dks[0], dks[1]) * scale
            dvv = jnp.where(low, dvs[0], dvs[1])
            dk_ref[...] = _bf(ck_ref[...] + dkk[:BLOCK])
            ck_ref[...] = dkk[BLOCK:]
            dv_ref[...] = _bf(cv_ref[...] + dvv[:BLOCK])
            cv_ref[...] = dvv[BLOCK:]

        @pl.when(n == nblk)
        def _():
            dk_ref[...] = _bf(ck_ref[...])
            dv_ref[...] = _bf(cv_ref[...])

    cur = lambda n: (jnp.minimum(n, nblk - 1), 0)
    prev = lambda n: (jnp.maximum(jnp.minimum(n, nblk - 1) - 1, 0), 0)
    late = lambda n: (jnp.maximum(n - 1, 0), 0)
    kv_spec = lambda m: pl.BlockSpec((BLOCK, KV_W), m)
    acc_b = pl.BlockSpec(bias.shape[1:], functools.partial(_zero_map, bias.ndim - 1))
    acc_s = pl.BlockSpec(sink_rows.shape, functools.partial(_zero_map, sink_rows.ndim))
    return _fused_call(
        "attn_bwd", body, (nblk + 1,),
        [pl.BlockSpec((BLOCK, ATTN_W), cur), kv_spec(prev), kv_spec(cur), kv_spec(prev), kv_spec(cur),
         pl.BlockSpec((BLOCK, ATTN_W), cur),
         pl.BlockSpec(bias.shape, functools.partial(_zero_map, bias.ndim)), acc_s],
        [pl.BlockSpec((BLOCK, ATTN_W), cur), kv_spec(late), kv_spec(late), acc_b, acc_s],
        [SDS((seq, ATTN_W), BF16), SDS((seq, KV_W), BF16), SDS((seq, KV_W), BF16),
         SDS(bias.shape[1:], F32), SDS(sink_rows.shape, F32)],
        [pltpu.VMEM((BLOCK, KV_W), F32), pltpu.VMEM((BLOCK, KV_W), F32)],
        [q, k, k, v, v, d_out, bias, sink_rows], exchange, _params(1))


def _ssm_discretize(lam_re, lam_im, log_dt, b_re, b_im):
    dt = jnp.exp(log_dt)[:, None]
    mag = jnp.exp(lam_re * dt)
    ab_re = mag * jnp.cos(lam_im * dt)
    ab_im = mag * jnp.sin(lam_im * dt)
    nr = ab_re - 1.0
    den = lam_re * lam_re + lam_im * lam_im
    f_re = (nr * lam_re + ab_im * lam_im) / den
    f_im = (ab_im * lam_re - nr * lam_im) / den
    bb_re = f_re[..., None] * b_re - f_im[..., None] * b_im
    bb_im = f_re[..., None] * b_im + f_im[..., None] * b_re
    return ab_re, ab_im, bb_re, bb_im


def _state_layout(re, im):
    z = jnp.stack([re, im]).reshape(2, N_SUPER, GROUPS_PER_SUPER, SSM_STATE)
    return z.transpose(1, 0, 2, 3).reshape(STATE_COLS)


def _state_unlayout(vec):
    z = vec.reshape(N_SUPER, 2, GROUPS_PER_SUPER, SSM_STATE).transpose(1, 0, 2, 3)
    z = z.reshape(2, SSM_GROUPS, SSM_STATE)
    return z[0], z[1]


SEG = 4
WINDOW = SEG * SUBLANES


def _scan_tables(ab_re, ab_im):
    pw = [None, (ab_re, ab_im)]
    for _ in range(2, WINDOW + 1):
        pr, pi_ = pw[-1]
        pw.append((pr * ab_re - pi_ * ab_im, pr * ab_im + pi_ * ab_re))
    rows = np.arange(SUBLANES)[:, None]
    ones = np.ones((SUBLANES, 1), np.float32)
    conj = lambda p: (p[0], -p[1])
    fwd, bwd = [], []
    for shift in (1, 2, 4):
        fwd.append(_state_layout(*pw[SEG * shift])[None, :] * (rows >= shift).astype(np.float32))
        bwd.append(_state_layout(*conj(pw[SEG * shift]))[None, :] * (rows < SUBLANES - shift).astype(np.float32))
    fwd.append(jnp.stack([_state_layout(*pw[SEG * (r + 1)]) for r in range(SUBLANES)]))
    bwd.append(jnp.stack([_state_layout(*conj(pw[SEG * (SUBLANES - r)])) for r in range(SUBLANES)]))
    for k in range(1, SEG):
        fwd.append(_state_layout(*pw[k])[None, :] * ones)
        bwd.append(_state_layout(*conj(pw[k]))[None, :] * ones)
    return jnp.stack(fwd), jnp.stack(bwd)


_EYE = np.eye(GROUPS_PER_SUPER, dtype=np.float32)


def _b_matrix(bb_re, bb_im):
    bb = jnp.stack([bb_re, bb_im]).reshape(2, N_SUPER, GROUPS_PER_SUPER, SSM_STATE, SSM_GROUP)
    m = jnp.einsum('rsgpc,gh->sgcrhp', bb, _EYE)
    return m.reshape(N_SUPER, SUPER_IN, SUPER_W)


def _b_matrix_grad(dm):
    d = dm.reshape(N_SUPER, GROUPS_PER_SUPER, SSM_GROUP, 2, GROUPS_PER_SUPER, SSM_STATE)
    d = jnp.sum(d * _EYE[None, :, None, None, :, None], axis=4)
    d = d.transpose(3, 0, 1, 4, 2).reshape(2, SSM_GROUPS, SSM_STATE, SSM_GROUP)
    return d[0], d[1]


def _c_matrix(c_re, c_im):
    cc = jnp.stack([c_re, -c_im]).reshape(2, N_SUPER, GROUPS_PER_SUPER, SSM_GROUP, SSM_STATE)
    m = jnp.einsum('rsgcp,gh->srgphc', cc, _EYE)
    return m.reshape(N_SUPER, SUPER_W, SUPER_IN)


def _c_matrix_grad(dm):
    d = dm.reshape(N_SUPER, 2, GROUPS_PER_SUPER, SSM_STATE, GROUPS_PER_SUPER, SSM_GROUP)
    d = jnp.sum(d * _EYE[None, None, :, None, :, None], axis=4)
    d = d.transpose(1, 0, 2, 4, 3).reshape(2, SSM_GROUPS, SSM_GROUP, SSM_STATE)
    return d[0], -d[1]


def _cmul_add(xr, xi, ar, ai, sr, si):
    return xr + ar * sr - ai * si, xi + ar * si + ai * sr


def _scan_rows(buf_ref, tab_ref, carry_ref, n_windows, reverse, h_ref=None, da_ref=None):
    order = list(range(SEG - 1, -1, -1)) if reverse else list(range(SEG))
    near = SUBLANES - 1 if reverse else 0
    far = 0 if reverse else SUBLANES - 1
    s_in = SUBLANES - 1 if reverse else 1
    lanes = lambda tile: pl.ds(tile * LANES, LANES)

    def window(w0, tile_re, tile_im, c_re, c_im, acc):
        rows = lambda t: pl.ds(w0 + t, SUBLANES, stride=SEG)
        get = lambda ref, t: (ref.at[tile_re][rows(t), :], ref.at[tile_im][rows(t), :])
        tab = lambda k: (tab_ref[k, :, lanes(tile_re)], tab_ref[k, :, lanes(tile_im)])

        def put(t, xr, xi):
            buf_ref.at[tile_re][rows(t), :] = xr
            buf_ref.at[tile_im][rows(t), :] = xi

        a1 = tab(4)
        er, ei = get(buf_ref, order[0])
        for t in order[1:]:
            er, ei = _cmul_add(*get(buf_ref, t), *a1, er, ei)
            if t != order[-1]:
                put(t, er, ei)
        for k, shift in enumerate((1, 2, 4)):
            s = (SUBLANES - shift) if reverse else shift
            er, ei = _cmul_add(er, ei, *tab(k), pltpu.roll(er, s, 0), pltpu.roll(ei, s, 0))
        er, ei = _cmul_add(er, ei, *tab(3), c_re, c_im)
        put(order[-1], er, ei)
        sub = lax.broadcasted_iota(jnp.int32, er.shape, 0)
        in_re = jnp.where(sub == near, c_re, pltpu.roll(er, s_in, 0))
        in_im = jnp.where(sub == near, c_im, pltpu.roll(ei, s_in, 0))
        true = {order[-1]: (er, ei)}
        for idx, t in enumerate(order[:-1]):
            true[t] = _cmul_add(*get(buf_ref, t), *tab(4 + idx), in_re, in_im)
            put(t, *true[t])
        carry = (jnp.broadcast_to(er[far:far + 1], er.shape), jnp.broadcast_to(ei[far:far + 1], ei.shape))
        if acc is None:
            return carry, None
        acc_re, acc_im = acc
        for t in range(SEG):
            if t + 1 < SEG:
                gr, gim = true[t + 1]
            else:
                gr = jnp.where(sub == SUBLANES - 1, c_re, pltpu.roll(true[0][0], SUBLANES - 1, 0))
                gim = jnp.where(sub == SUBLANES - 1, c_im, pltpu.roll(true[0][1], SUBLANES - 1, 0))
            hr, hi = get(h_ref, t)
            acc_re = acc_re + gr * hr + gim * hi
            acc_im = acc_im + gim * hr - gr * hi
        return carry, (acc_re, acc_im)

    half = SUPER_HALF // LANES
    per = 2 if h_ref is None else 4
    for sb in range(N_SUPER):
        pairs = [(2 * half * sb + j, 2 * half * sb + half + j) for j in range(half)]

        def step(wi, state, pairs=pairs):
            w = (n_windows - 1 - wi) if reverse else wi
            w0 = pl.multiple_of(w * WINDOW, WINDOW)
            out = []
            for j, (tile_re, tile_im) in enumerate(pairs):
                mine = state[per * j:per * (j + 1)]
                carry, acc = window(w0, tile_re, tile_im, mine[0], mine[1], mine[2:] or None)
                out += list(carry) + list(acc or ())
            return tuple(out)

        init = []
        for tile_re, tile_im in pairs:
            init += [carry_ref[:, lanes(tile_re)], carry_ref[:, lanes(tile_im)]]
            if h_ref is not None:
                init += [da_ref[:, lanes(tile_re)], da_ref[:, lanes(tile_im)]]
        fin = lax.fori_loop(0, n_windows, step, tuple(init))
        for j, (tile_re, tile_im) in enumerate(pairs):
            carry_ref[:, lanes(tile_re)] = fin[per * j]
            carry_ref[:, lanes(tile_im)] = fin[per * j + 1]
            if h_ref is not None:
                da_ref[:, lanes(tile_re)] = fin[per * j + 2]
                da_ref[:, lanes(tile_im)] = fin[per * j + 3]


def _put_tiles(ref, sb, block):
    for j in range(SUPER_TILES):
        ref[sb * SUPER_TILES + j] = block[:, j * LANES:(j + 1) * LANES]


def _get_tiles(ref, sb):
    return jnp.concatenate([ref[sb * SUPER_TILES + j] for j in range(SUPER_TILES)], axis=1)


def _ssm_fwd(u, bmat, cmat, tab, d_skip, tb, exchange=None):
    seq = u.shape[0]

    def body(u_ref, b_ref, c_ref, t_ref, d_ref, s_ref, h_ref, carry_ref):
        @pl.when(pl.program_id(0) == 0)
        def _():
            carry_ref[...] = jnp.zeros_like(carry_ref)

        u_blk = u_ref[...]
        ub = _bf(u_blk)
        for sb in range(N_SUPER):
            _put_tiles(h_ref, sb, _mm(ub[:, sb * SUPER_IN:(sb + 1) * SUPER_IN], b_ref[sb]))
        _scan_rows(h_ref, t_ref, carry_ref, tb // WINDOW, False)
        ys = [_mm(_bf(_get_tiles(h_ref, sb)), c_ref[sb]) for sb in range(N_SUPER)]
        s_ref[...] = jnp.concatenate(ys, axis=1) + d_ref[...] * u_blk

    return _rowcall("ssm_fwd", body, seq, tb, [u], [bmat, cmat, tab, d_skip],
                    [(SSM_W, F32), ((STATE_TILES, LANES), F32)], [],
                    scratch=[pltpu.VMEM((SUBLANES, STATE_COLS), F32)], vmem=VMEM_BIG, exchange=exchange)


def _ssm_bwd(ds, u, h, bmat_t, cmat_t, tab, d_skip, tb, exchange=None):
    seq = u.shape[0]

    def body(ds_ref, u_ref, h_ref, bt_ref, ct_ref, t_ref, d_ref,
             du_ref, db_ref, dc_ref, da_ref, dd_ref, g_ref, carry_ref):
        @pl.when(pl.program_id(0) == 0)
        def _():
            carry_ref[...] = jnp.zeros_like(carry_ref)
            db_ref[...] = jnp.zeros_like(db_ref)
            dc_ref[...] = jnp.zeros_like(dc_ref)
            da_ref[...] = jnp.zeros_like(da_ref)
            dd_ref[...] = jnp.zeros_like(dd_ref)

        ds_blk = ds_ref[...]
        dsb = _bf(ds_blk)
        u_blk = u_ref[...]
        ub = _bf(u_blk)
        for sb in range(N_SUPER):
            _put_tiles(g_ref, sb, _mm(dsb[:, sb * SUPER_IN:(sb + 1) * SUPER_IN], ct_ref[sb]))
        _scan_rows(g_ref, t_ref, carry_ref, tb // WINDOW, True, h_ref=h_ref, da_ref=da_ref)
        dus = []
        for sb in range(N_SUPER):
            gb = _bf(_get_tiles(g_ref, sb))
            dus.append(_mm(gb, bt_ref[sb]))
            db_ref[sb] += _mm_tn(ub[:, sb * SUPER_IN:(sb + 1) * SUPER_IN], gb)
            dc_ref[sb] += _mm_tn(_bf(_get_tiles(h_ref, sb)), dsb[:, sb * SUPER_IN:(sb + 1) * SUPER_IN])
        du_ref[...] = jnp.concatenate(dus, axis=1) + d_ref[...] * ds_blk
        dd_ref[...] += jnp.sum(ds_blk * u_blk, axis=0, keepdims=True)

    return _rowcall("ssm_bwd", body, seq, tb, [ds, u, h], [bmat_t, cmat_t, tab, d_skip],
                    [(SSM_W, F32)],
                    [((N_SUPER, SUPER_IN, SUPER_W), F32), ((N_SUPER, SUPER_W, SUPER_IN), F32),
                     ((SUBLANES, STATE_COLS), F32), ((1, SSM_W), F32)],
                    scratch=[pltpu.VMEM((STATE_TILES, tb, LANES), F32), pltpu.VMEM((SUBLANES, STATE_COLS), F32)],
                    reverse=True, vmem=VMEM_BIG, exchange=exchange)


def _merge_core(s, attb, ga, gs, wg_ref, wab_ref, wsb_ref, wout_ref):
    zg, dgelu = _gelu_and_grad(s)
    zgb = _bf(zg)
    sg = _sig(_mm(zgb, wg_ref[...]))
    z = zg * sg
    zb = _bf(z)
    ys = jnp.concatenate([_mm(zb, wsb_ref[j]) for j in range(N_CHIPS)], axis=1)
    ya = jnp.concatenate([_mm(attb, wab_ref[j]) for j in range(N_CHIPS)], axis=1)
    sa = _sig(ga)
    ss = _sig(gs)
    mgb = _bf(sa * ya + ss * ys)
    o = _mm(mgb, wout_ref[...])
    return dict(zg=zg, dgelu=dgelu, zgb=zgb, sg=sg, zb=zb, ys=ys, ya=ya, sa=sa, ss=ss, mgb=mgb, o=o)


def _merge_fwd(x, s, att, ga, gs, g2, w_glu, w_ab, w_sb, w_out, tb):
    seq = x.shape[0]

    def body(x_ref, s_ref, att_ref, ga_ref, gs_ref, g_ref, wg_ref, wab_ref, wsb_ref, wout_ref, x2_ref):
        f = _merge_core(s_ref[...], att_ref[...], ga_ref[...], gs_ref[...], wg_ref, wab_ref, wsb_ref, wout_ref)
        n, _, _ = _rms(f["o"], g_ref[...])
        x2_ref[...] = x_ref[...] + n

    return _rowcall("merge_fwd", body, seq, tb, [x, s, att, ga, gs], [g2, w_glu, w_ab, w_sb, w_out],
                    [(D_MODEL, F32)], [], vmem=VMEM_BIG)[0]


def _merge_bwd(dx2, s, att, ga, gs, g2, w_glu, w_ab, w_sb, w_out, tb, exchange=None):
    seq = s.shape[0]
    cw = D_MODEL // N_CHIPS
    last = seq // tb - 1

    def body(dx2_ref, s_ref, att_ref, ga_ref, gs_ref, g_ref, wg_ref, wab_ref, wsb_ref, wout_ref,
             ds_ref, datt_ref, dga_ref, dgs_ref, dg_ref, dwg_ref, dwab_ref, dwsb_ref, dwout_ref,
             bwg_ref, bwab_ref, bwsb_ref, bwout_ref):
        @pl.when(pl.program_id(0) == 0)
        def _():
            for r in (dg_ref, dwg_ref, dwab_ref, dwsb_ref, dwout_ref):
                r[...] = jnp.zeros_like(r)

        attb = att_ref[...]
        f = _merge_core(s_ref[...], attb, ga_ref[...], gs_ref[...], wg_ref, wab_ref, wsb_ref, wout_ref)
        g = g_ref[...]
        _, oh, r2 = _rms(f["o"], g)
        do, dg = _rms_bwd(dx2_ref[...], oh, r2, g)
        dg_ref[...] += dg
        dob = _bf(do)
        dwout_ref[...] += _mm_tn(f["mgb"], dob)
        dmg = _mm_nt(dob, wout_ref[...])
        sa, ss = f["sa"], f["ss"]
        dyab = _bf(dmg * sa)
        dysb = _bf(dmg * ss)
        dga_ref[...] = _bf(dmg * f["ya"] * sa * (1.0 - sa))
        dgs_ref[...] = _bf(dmg * f["ys"] * ss * (1.0 - ss))
        dwab = _mm_tn(attb, dyab)
        dwsb = _mm_tn(f["zb"], dysb)
        datt = jnp.zeros((tb, ATTN_W), F32)
        dz = jnp.zeros((tb, SSM_W), F32)
        for j in range(N_CHIPS):
            dwab_ref[j] += dwab[:, j * cw:(j + 1) * cw]
            dwsb_ref[j] += dwsb[:, j * cw:(j + 1) * cw]
            datt = datt + _mm_nt(dyab[:, j * cw:(j + 1) * cw], wab_ref[j])
            dz = dz + _mm_nt(dysb[:, j * cw:(j + 1) * cw], wsb_ref[j])
        datt_ref[...] = _bf(datt)
        sg, zg = f["sg"], f["zg"]
        dglb = _bf(dz * zg * sg * (1.0 - sg))
        dwg_ref[...] += _mm_tn(f["zgb"], dglb)
        dzg = dz * sg + _mm_nt(dglb, wg_ref[...])
        ds_ref[...] = dzg * f["dgelu"]

        @pl.when(pl.program_id(0) == last)
        def _():
            for dst, src in ((bwg_ref, dwg_ref), (bwab_ref, dwab_ref), (bwsb_ref, dwsb_ref), (bwout_ref, dwout_ref)):
                dst[...] = _bf(src[...])

    shapes = [w_glu.shape, w_ab.shape, w_sb.shape, w_out.shape]
    return _rowcall("merge_bwd", body, seq, tb, [dx2, s, att, ga, gs], [g2, w_glu, w_ab, w_sb, w_out],
                    [(SSM_W, F32), (ATTN_W, BF16), (D_MODEL, BF16), (D_MODEL, BF16)],
                    [((1, D_MODEL), F32)] + [(sh, F32) for sh in shapes] + [(sh, BF16) for sh in shapes],
                    vmem=VMEM_BIG, exchange=exchange)


def _mlp_fwd_loss(x2, target, g3, g4, w_ffi, w_ffo, tb):
    seq = x2.shape[0]
    n_slab = len(w_ffi)
    sw = D_FF // FF_CHUNKS // n_slab

    def body(x2_ref, t_ref, g3_ref, g4_ref, *rest):
        wi_refs, (wo_ref, dy_ref, df_ref, h_ref, loss_ref, dg_ref) = rest[:n_slab], rest[n_slab:]

        @pl.when(pl.program_id(0) == 0)
        def _():
            loss_ref[...] = jnp.zeros_like(loss_ref)
            dg_ref[...] = jnp.zeros_like(dg_ref)

        x2_blk = x2_ref[...]
        h3, _, _ = _rms(x2_blk, g3_ref[...])
        hb = _bf(h3)
        h_ref[...] = hb
        f = jnp.zeros((tb, D_MODEL), F32)
        for j in range(FF_CHUNKS):
            for k in range(n_slab):
                a = _mm(hb, wi_refs[k][j])
                f = f + _mm(_bf(jnp.square(jnp.maximum(a, 0.0))), wo_ref[j, pl.ds(k * sw, sw), :])
        g4 = g4_ref[...]
        n4, fh, r4 = _rms(f, g4)
        e = (x2_blk + n4) - t_ref[...]
        loss_ref[...] += 0.5 * jnp.sum(jnp.mean(e * e, axis=-1, keepdims=True))
        dy = e * (1.0 / D_MODEL)
        dy_ref[...] = dy
        df, dg = _rms_bwd(dy, fh, r4, g4)
        df_ref[...] = _bf(df)
        dg_ref[...] += dg

    return _rowcall("mlp_fwd_loss", body, seq, tb, [x2, target], [g3, g4, *w_ffi, w_ffo],
                    [(D_MODEL, F32), (D_MODEL, BF16), (D_MODEL, BF16)],
                    [((SUBLANES, 128), F32), ((1, D_MODEL), F32)], vmem=VMEM_BIG)


def _mlp_bwd(x2, dy, df, h3, g3, w_ffi, w_ffo, tb):
    seq = x2.shape[0]
    n_slab = len(w_ffi)
    sw = D_FF // FF_CHUNKS // n_slab

    def body(x2_ref, dy_ref, df_ref, h_ref, g3_ref, *rest):
        wi_refs, (wo_ref, dx_ref, act_ref, da_ref, dg_ref) = rest[:n_slab], rest[n_slab:]

        @pl.when(pl.program_id(0) == 0)
        def _():
            dg_ref[...] = jnp.zeros_like(dg_ref)

        hb = h_ref[...]
        dfb = df_ref[...]
        dh = jnp.zeros((tb, D_MODEL), F32)
        for j in range(FF_CHUNKS):
            for k in range(n_slab):
                cols = pl.ds((j * n_slab + k) * sw, sw)
                ra = jnp.maximum(_mm(hb, wi_refs[k][j]), 0.0)
                act_ref[:, cols] = _bf(ra * ra)
                dab = _bf(_mm_nt(dfb, wo_ref[j, pl.ds(k * sw, sw), :]) * (2.0 * ra))
                da_ref[:, cols] = dab
                dh = dh + _mm_nt(dab, wi_refs[k][j])
        g3 = g3_ref[...]
        _, xh, r3 = _rms(x2_ref[...], g3)
        dxn, dg = _rms_bwd(dh, xh, r3, g3)
        dx_ref[...] = dy_ref[...] + dxn
        dg_ref[...] += dg

    return _rowcall("mlp_bwd", body, seq, tb, [x2, dy, df, h3], [g3, *w_ffi, w_ffo],
                    [(D_MODEL, F32), (D_FF, BF16), (D_FF, BF16)], [((1, D_MODEL), F32)], vmem=VMEM_BIG)


def _matmul_tn(name, a, b, tk, tn, tl, chunk_major, exchange=None):
    seq, kdim = a.shape
    ndim = b.shape[1]
    last = seq // tl - 1

    def body(a_ref, b_ref, o_ref, ob_ref):
        @pl.when(pl.program_id(2) == 0)
        def _():
            o_ref[...] = jnp.zeros_like(o_ref)

        o_ref[...] += _mm_tn(a_ref[...], b_ref[...])

        @pl.when(pl.program_id(2) == last)
        def _():
            ob_ref[...] = _bf(o_ref[...])

    if chunk_major:
        shape = (ndim // tn, kdim, tn)
        out_spec = pl.BlockSpec((None, tk, tn), lambda k, n, l: (n, k, 0))
    else:
        shape = (kdim, ndim)
        out_spec = pl.BlockSpec((tk, tn), lambda k, n, l: (k, n))
    return _fused_call(
        name, body, (kdim // tk, ndim // tn, seq // tl),
        [pl.BlockSpec((tl, tk), lambda k, n, l: (l, k)), pl.BlockSpec((tl, tn), lambda k, n, l: (l, n))],
        [out_spec, out_spec], [SDS(shape, F32), SDS(shape, BF16)], [], [a, b], exchange, _params(3, VMEM_BIG))


def _ew_call(name, fn, ins, n_out, after=None):
    rows, cols = ins[0].shape
    tr = rows
    while tr * cols * 4 > min(1 << 20, (9 << 20) // (len(ins) + n_out)) and tr % 16 == 0:
        tr //= 2
    spec = pl.BlockSpec((tr, cols), lambda i: (i, 0))
    extra = [] if after is None else [after]

    def body(*refs):
        outs = fn(*[r[...] for r in refs[:len(ins)]])
        for r, o in zip(refs[len(ins) + len(extra):], outs):
            r[...] = o

    return pl.pallas_call(
        body, grid=(rows // tr,), in_specs=[spec] * len(ins) + [ANY] * len(extra), out_specs=[spec] * n_out,
        out_shape=[SDS((rows, cols), F32)] * n_out, name=name, compiler_params=_params(1))(*ins, *extra)


def _adam_math(w, g, m, v):
    m2 = ADAM_B1 * m + (1.0 - ADAM_B1) * g
    v2 = ADAM_B2 * v + (1.0 - ADAM_B2) * (g * g)
    m_hat = m2 / (1.0 - ADAM_B1 ** ADAM_STEP)
    v_hat = v2 / (1.0 - ADAM_B2 ** ADAM_STEP)
    delta = -ADAM_LR * (m_hat / (jnp.sqrt(v_hat) + ADAM_EPS) + ADAM_WD * w)
    return delta, m2, v2


def _sum4(name, own, recv, idx):
    _, rows, cols = own.shape
    tr = rows
    while tr * cols * 4 > (1 << 20) and tr % 16 == 0:
        tr //= 2

    def body(idx_ref, o_ref, r0_ref, r1_ref, r2_ref, out_ref):
        out_ref[...] = ((o_ref[...] + r0_ref[...].astype(F32)) + r1_ref[...].astype(F32)) + r2_ref[...].astype(F32)

    blk = (None, tr, cols)
    grid_spec = pltpu.PrefetchScalarGridSpec(
        num_scalar_prefetch=1, grid=(rows // tr,),
        in_specs=[pl.BlockSpec(blk, lambda i, s: (s[0], i, 0)), pl.BlockSpec(blk, lambda i, s: (0, i, 0)),
                  pl.BlockSpec(blk, lambda i, s: (1, i, 0)), pl.BlockSpec(blk, lambda i, s: (2, i, 0))],
        out_specs=pl.BlockSpec((tr, cols), lambda i, s: (i, 0)))
    return pl.pallas_call(body, grid_spec=grid_spec, out_shape=SDS((rows, cols), F32), name=name,
                          compiler_params=_params(1))(jnp.reshape(idx, (1,)).astype(jnp.int32), own, recv, recv, recv)


def _adam_pair(name, item, after=None):
    def fn(w_, a, b, m_, v_):
        g = a + b
        return (g,) + _adam_math(w_, g, m_, v_)

    return _ew_call(name, fn, list(item), 4, after)


def _place():
    return lax.axis_index("x"), lax.axis_index("y"), lax.axis_index("c")


def _other_chips(x, y):
    return [(1 - x, y), (x, 1 - y), (1 - x, 1 - y)]


def _gather_chips(shards):
    n = len(shards)

    def copies(ins, outs, sems):
        send, recv, fwd_send, fwd_recv, loc = sems
        x, y, c = _place()
        me = 2 * x + y
        peers = _other_chips(x, y)
        local = [pltpu.make_async_copy(ins[a], outs[a].at[me], loc.at[a]) for a in range(n)]
        sends, recvs, passes, passed = [], [], [], []
        for a in range(n):
            half = shards[a].shape[0] // 2
            mine = pl.ds(c * half, half)
            theirs = pl.ds((1 - c) * half, half)
            for j, (px, py) in enumerate(peers):
                far = 2 * px + py
                sends.append(pltpu.make_async_remote_copy(
                    src_ref=ins[a].at[mine], dst_ref=outs[a].at[me, mine], send_sem=send.at[a, j],
                    recv_sem=recv.at[a, j], device_id=(px, py, c), device_id_type=MESH_ID))
                recvs.append(pltpu.make_async_remote_copy(
                    src_ref=ins[a].at[mine], dst_ref=outs[a].at[far, mine], send_sem=send.at[a, j],
                    recv_sem=recv.at[a, j], device_id=(px, py, c), device_id_type=MESH_ID))
                passes.append(pltpu.make_async_remote_copy(
                    src_ref=outs[a].at[far, mine], dst_ref=outs[a].at[far, mine], send_sem=fwd_send.at[a, j],
                    recv_sem=fwd_recv.at[a, j], device_id=(x, y, 1 - c), device_id_type=MESH_ID))
                passed.append(pltpu.make_async_remote_copy(
                    src_ref=outs[a].at[far, theirs], dst_ref=outs[a].at[far, theirs], send_sem=fwd_send.at[a, j],
                    recv_sem=fwd_recv.at[a, j], device_id=(x, y, 1 - c), device_id_type=MESH_ID))
        return local, sends, recvs, passes, passed

    def start(ins, outs, sems):
        local, sends, _, _, _ = copies(ins, outs, sems)
        for cp in local + sends:
            cp.start()

    def wait(ins, outs, sems):
        local, sends, recvs, passes, passed = copies(ins, outs, sems)
        for got, on in zip(recvs, passes):
            got.wait_recv()
            on.start()
        for cp in passed:
            cp.wait_recv()
        for cp in passes + sends:
            cp.wait_send()
        for cp in local:
            cp.wait()

    assert all(s.shape[0] % 32 == 0 for s in shards)
    pair = pltpu.SemaphoreType.DMA((n, 3))
    return _Exchange(shards, [SDS((N_CHIPS,) + s.shape, s.dtype) for s in shards],
                     [pair, pair, pair, pair, pltpu.SemaphoreType.DMA((n,))], start, wait)


def _scatter_chips(chunks):
    n = len(chunks)

    def copies(ins, outs, sems):
        send, recv = sems
        x, y, c = _place()
        return [pltpu.make_async_remote_copy(
            src_ref=ins[a].at[2 * px + py], dst_ref=outs[a].at[j], send_sem=send.at[a, j],
            recv_sem=recv.at[a, j], device_id=(px, py, c), device_id_type=MESH_ID)
            for a in range(n) for j, (px, py) in enumerate(_other_chips(x, y))]

    def start(ins, outs, sems):
        for cp in copies(ins, outs, sems):
            cp.start()

    def wait(ins, outs, sems):
        cps = copies(ins, outs, sems)
        for cp in cps:
            cp.wait_recv()
        for cp in cps:
            cp.wait_send()

    return _Exchange(chunks, [SDS((3,) + s.shape[1:], s.dtype) for s in chunks],
                     [pltpu.SemaphoreType.DMA((n, 3)), pltpu.SemaphoreType.DMA((n, 3))], start, wait)


HBM = pl.BlockSpec(memory_space=pltpu.HBM)
SEM = pl.BlockSpec(memory_space=pltpu.SEMAPHORE)
DATAFLOW = pltpu.SideEffectType.DATAFLOW_SIDE_EFFECTING


def _chunk_copies(src_ref, land_ref, send_sems, recv_sems):
    x, y, c = _place()
    return [pltpu.make_async_remote_copy(
        src_ref=src_ref.at[2 * px + py], dst_ref=land_ref.at[k], send_sem=send_sems.at[k], recv_sem=recv_sems.at[k],
        device_id=(px, py, c), device_id_type=MESH_ID) for k, (px, py) in enumerate(_other_chips(x, y))]


def _scatter_start(name, chunks):
    def body(src_ref, land_ref, send_sems, recv_sems, src_thru, land_thru, token):
        for cp in _chunk_copies(src_ref, land_ref, send_sems, recv_sems):
            cp.start()
        token[...] = jnp.zeros_like(token)

    land = (3,) + chunks.shape[1:]
    return pl.pallas_call(
        body, name=name,
        out_shape=(pltpu.SemaphoreType.DMA((3,)), pltpu.SemaphoreType.DMA((3,)), pltpu.HBM(chunks.shape, chunks.dtype),
                   pltpu.HBM(land, chunks.dtype), SDS((SUBLANES, LANES), F32)),
        in_specs=(HBM, HBM), out_specs=(SEM, SEM, HBM, HBM, pl.BlockSpec(memory_space=pltpu.VMEM)),
        input_output_aliases={0: 2, 1: 3}, compiler_params=pltpu.CompilerParams(has_side_effects=DATAFLOW),
    )(pltpu.with_memory_space_constraint(chunks, pltpu.HBM),
      pltpu.with_memory_space_constraint(lax.empty(land, chunks.dtype), pltpu.HBM))


def _scatter_wait(name, send_sems, recv_sems, src_thru, land_thru, after):
    def body(src_ref, land_ref, send_sems, recv_sems, after_ref, src_dead, got_ref):
        for cp in _chunk_copies(src_ref, land_ref, send_sems, recv_sems):
            cp.wait_send()
            cp.wait_recv()

    return pl.pallas_call(
        body, name=name,
        out_shape=(pltpu.HBM(src_thru.shape, src_thru.dtype), pltpu.HBM(land_thru.shape, land_thru.dtype)),
        in_specs=(HBM, HBM, SEM, SEM, ANY), out_specs=(HBM, HBM), input_output_aliases={0: 0, 1: 1},
        compiler_params=pltpu.CompilerParams(has_side_effects=DATAFLOW),
    )(src_thru, land_thru, send_sems, recv_sems, after)[1]


def _half_rows(shape, c, other=False):
    half = shape[0] // 2
    return pl.ds(((1 - c) if other else c) * half, half)


def _gather_start(name, shards, lands):
    n = len(shards)

    def body(*refs):
        src, land, (send, recv) = refs[:n], refs[n:2 * n], refs[2 * n:2 * n + 2]
        x, y, c = _place()
        me = 2 * x + y
        for a in range(n):
            mine = _half_rows(shards[a].shape, c)
            for j, (px, py) in enumerate(_other_chips(x, y)):
                pltpu.make_async_remote_copy(
                    src_ref=src[a].at[mine], dst_ref=land[a].at[me, mine], send_sem=send.at[3 * a + j],
                    recv_sem=recv.at[3 * a + j], device_id=(px, py, c), device_id_type=MESH_ID).start()
        token = refs[-1]
        token[...] = jnp.zeros_like(token)

    mem = lambda t: pltpu.HBM(t.shape, t.dtype)
    pair = pltpu.SemaphoreType.DMA((3 * n,))
    outs = pl.pallas_call(
        body, name=name,
        out_shape=(pair, pair, *map(mem, shards), *map(mem, lands), SDS((SUBLANES, LANES), F32)),
        in_specs=[HBM] * (2 * n), out_specs=(SEM, SEM, *[HBM] * (2 * n), pl.BlockSpec(memory_space=pltpu.VMEM)),
        input_output_aliases={i: 2 + i for i in range(2 * n)},
        compiler_params=pltpu.CompilerParams(has_side_effects=DATAFLOW),
    )(*[pltpu.with_memory_space_constraint(t, pltpu.HBM) for t in (*shards, *lands)])
    return outs[0], outs[1], list(outs[2:2 + n]), list(outs[2 + n:2 + 2 * n]), outs[-1]


def _gather_pass(name, send, recv, shards, lands, after):
    n = len(shards)

    def body(*refs):
        src, land, (send, recv, _) = refs[:n], refs[n:2 * n], refs[2 * n:2 * n + 3]
        fsend, frecv = refs[2 * n + 3], refs[2 * n + 4]
        x, y, c = _place()
        me = 2 * x + y
        for a in range(n):
            mine = _half_rows(shards[a].shape, c)
            for j, (px, py) in enumerate(_other_chips(x, y)):
                far = 2 * px + py
                ici = pltpu.make_async_remote_copy(
                    src_ref=src[a].at[mine], dst_ref=land[a].at[far, mine], send_sem=send.at[3 * a + j],
                    recv_sem=recv.at[3 * a + j], device_id=(px, py, c), device_id_type=MESH_ID)
                ici.wait_recv()
                ici.wait_send()
                pltpu.make_async_remote_copy(
                    src_ref=land[a].at[far, mine], dst_ref=land[a].at[far, mine], send_sem=fsend.at[3 * a + j],
                    recv_sem=frecv.at[3 * a + j], device_id=(x, y, 1 - c), device_id_type=MESH_ID).start()
        token = refs[-1]
        token[...] = jnp.zeros_like(token)

    mem = lambda t: pltpu.HBM(t.shape, t.dtype)
    pair = pltpu.SemaphoreType.DMA((3 * n,))
    outs = pl.pallas_call(
        body, name=name,
        out_shape=(pair, pair, *map(mem, lands), SDS((SUBLANES, LANES), F32)),
        in_specs=[HBM] * (2 * n) + [SEM, SEM, ANY],
        out_specs=(SEM, SEM, *[HBM] * n, pl.BlockSpec(memory_space=pltpu.VMEM)),
        input_output_aliases={n + i: 2 + i for i in range(n)},
        compiler_params=pltpu.CompilerParams(has_side_effects=DATAFLOW),
    )(*shards, *lands, send, recv, after)
    return outs[0], outs[1], list(outs[2:2 + n]), outs[-1]


def _gather_wait(name, fsend, frecv, lands, after):
    n = len(lands)

    def body(*refs):
        land, (fsend, frecv, _) = refs[:n], refs[n:n + 3]
        x, y, c = _place()
        for a in range(n):
            for j, (px, py) in enumerate(_other_chips(x, y)):
                far = 2 * px + py
                mine = _half_rows(lands[a].shape[1:], c)
                theirs = _half_rows(lands[a].shape[1:], c, other=True)
                pltpu.make_async_remote_copy(
                    src_ref=land[a].at[far, mine], dst_ref=land[a].at[far, mine], send_sem=fsend.at[3 * a + j],
                    recv_sem=frecv.at[3 * a + j], device_id=(x, y, 1 - c), device_id_type=MESH_ID).wait_send()
                pltpu.make_async_remote_copy(
                    src_ref=land[a].at[far, theirs], dst_ref=land[a].at[far, theirs], send_sem=fsend.at[3 * a + j],
                    recv_sem=frecv.at[3 * a + j], device_id=(x, y, 1 - c), device_id_type=MESH_ID).wait_recv()

    mem = lambda t: pltpu.HBM(t.shape, t.dtype)
    return list(pl.pallas_call(
        body, name=name, out_shape=tuple(map(mem, lands)), in_specs=[HBM] * n + [SEM, SEM, ANY],
        out_specs=tuple([HBM] * n), input_output_aliases={i: i for i in range(n)},
        compiler_params=pltpu.CompilerParams(has_side_effects=DATAFLOW),
    )(*lands, fsend, frecv, after))


def _after(token):
    return _Exchange([token], [], [], lambda *_: None, lambda *_: None)


def _swap_sibling(arrs):
    n = len(arrs)

    def copies(ins, outs, sems):
        send, recv = sems
        x, y, c = _place()
        return [pltpu.make_async_remote_copy(
            src_ref=ins[a], dst_ref=outs[a], send_sem=send.at[a], recv_sem=recv.at[a],
            device_id=(x, y, 1 - c), device_id_type=MESH_ID) for a in range(n)]

    def start(ins, outs, sems):
        for cp in copies(ins, outs, sems):
            cp.start()

    def wait(ins, outs, sems):
        cps = copies(ins, outs, sems)
        for cp in cps:
            cp.wait_recv()
        for cp in cps:
            cp.wait_send()

    return _Exchange(arrs, [SDS(s.shape, s.dtype) for s in arrs],
                     [pltpu.SemaphoreType.DMA((n,)), pltpu.SemaphoreType.DMA((n,))], start, wait)


N_DEV = 8


def _gather_devices(block):
    def copies(ins, outs, sems):
        send, recv, loc = sems
        x, y, c = _place()
        me = 4 * x + 2 * y + c
        local = pltpu.make_async_copy(ins[0], outs[0].at[me], loc.at[0])
        sends, recvs = [], []
        for k in range(1, N_DEV):
            peer = (x ^ (k >> 2), y ^ ((k >> 1) & 1), c ^ (k & 1))
            for group, slot in ((sends, me), (recvs, me ^ k)):
                group.append(pltpu.make_async_remote_copy(
                    src_ref=ins[0], dst_ref=outs[0].at[slot], send_sem=send.at[k - 1], recv_sem=recv.at[k - 1],
                    device_id=peer, device_id_type=MESH_ID))
        return local, sends, recvs

    def start(ins, outs, sems):
        local, sends, _ = copies(ins, outs, sems)
        for cp in [local] + sends:
            cp.start()

    def wait(ins, outs, sems):
        local, sends, recvs = copies(ins, outs, sems)
        for cp in recvs:
            cp.wait_recv()
        for cp in sends:
            cp.wait_send()
        local.wait()

    return _Exchange([block], [SDS((N_DEV,) + block.shape, block.dtype)],
                     [pltpu.SemaphoreType.DMA((N_DEV - 1,)), pltpu.SemaphoreType.DMA((N_DEV - 1,)),
                      pltpu.SemaphoreType.DMA((1,))], start, wait)


def _both(ex_a, ex_b):
    na_i, na_o, na_s = len(ex_a.ins), len(ex_a.outs), len(ex_a.sems)

    def start(ins, outs, sems):
        ex_a.start(ins[:na_i], outs[:na_o], sems[:na_s])
        ex_b.start(ins[na_i:], outs[na_o:], sems[na_s:])

    def wait(ins, outs, sems):
        ex_a.wait(ins[:na_i], outs[:na_o], sems[:na_s])
        ex_b.wait(ins[na_i:], outs[na_o:], sems[na_s:])

    return _Exchange(ex_a.ins + ex_b.ins, ex_a.outs + ex_b.outs, ex_a.sems + ex_b.sems, start, wait)


def _sum_devices(slots):
    def body(s_ref, o_ref):
        acc = s_ref[0]
        for d in range(1, N_DEV):
            acc = acc + s_ref[d]
        o_ref[...] = acc

    return pl.pallas_call(
        body, in_specs=[pl.BlockSpec(memory_space=pltpu.VMEM)], out_specs=pl.BlockSpec(memory_space=pltpu.VMEM),
        out_shape=SDS(slots.shape[1:], F32), name="sum_small",
        compiler_params=pltpu.CompilerParams(vmem_limit_bytes=32 * 1024 * 1024))(slots)


def _adam_small(ws, gs, ms, vs):
    n = len(ws)

    def body(*refs):
        for i in range(n):
            w_ref, g_ref, m_ref, v_ref = (refs[k * n + i] for k in range(4))
            outs = _adam_math(w_ref[...], g_ref[...], m_ref[...], v_ref[...])
            for k in range(3):
                refs[(4 + k) * n + i][...] = outs[k]

    vmem = pl.BlockSpec(memory_space=pltpu.VMEM)
    return pl.pallas_call(
        body, in_specs=[vmem] * (4 * n), out_specs=[vmem] * (3 * n),
        out_shape=[SDS(w.shape, F32) for w in ws] * 3, name="adam_small",
        compiler_params=pltpu.CompilerParams(vmem_limit_bytes=32 * 1024 * 1024))(*ws, *gs, *ms, *vs)


def _local_step(x, target, small, big, tb, distributed):
    g1, g2, g3, g4 = small["norm_mix_pre"], small["norm_mix_post"], small["norm_mlp_pre"], small["norm_mlp_post"]
    dist = distributed
    me = (2 * lax.axis_index("x") + lax.axis_index("y")) if dist else 0
    tb_ssm = min(tb, 256)
    bucket = jnp.asarray(_bucket_table())

    keys_first = lambda t: jnp.swapaxes(t, -1, -2)
    bias = _bias_table(small["rel_bias"], bucket)
    sink_rows = keys_first(_pair_layout(jnp.broadcast_to(small["sinks"].reshape(N_HEADS, 1, 1), (N_HEADS, BLOCK, 1))))
    disc_args = (small["lam_re"], small["lam_im"], small["log_dt"], small["b_re"], small["b_im"])
    (ab_re, ab_im, bb_re, bb_im), disc_vjp = jax.vjp(_ssm_discretize, *disc_args)
    tab_f, tab_b = _scan_tables(ab_re, ab_im)
    bmat = _bf(_b_matrix(bb_re, bb_im))
    cmat = _bf(_c_matrix(small["c_re"], small["c_im"]))
    d_skip = small["d_skip"]

    if dist:
        (g_in,) = _exchange_alone("gather_w_in", _gather_chips([big["w_in"]]))
        w_in = g_in.reshape(IN_W, D_MODEL)
    else:
        w_in = big["w_in"]
    mix = ("w_glu", "w_attn_branch", "w_ssm_branch", "w_out")
    ride = None
    if dist:
        ff = [big["w_ff_in"], big["w_ff_out"]]
        lands = [lax.dynamic_update_index_in_dim(lax.empty((N_CHIPS,) + t.shape, t.dtype), t, me, 0) for t in ff]
        send, recv, ff, lands, token = _gather_start("gather_ff_start", ff, lands)
        ride = _both(_gather_chips([big[n] for n in mix]), _after(token))
    outs = _inproj_fwd(x, g1, w_in, tb, ride)
    h1, q, k, v, u, ga, gs = outs[:7]
    w_glu, w_ab, w_sb, w_out = outs[7:] if dist else [big[n] for n in mix]
    w_glu = w_glu.reshape(SSM_W, SSM_W)
    w_out = w_out.reshape(D_MODEL, D_MODEL)
    att = _attn_fwd(q, k, v, bias, sink_rows)[0]
    if dist:
        send, recv, lands, token = _gather_pass("gather_ff_pass", send, recv, ff, lands, att)
    s, h = _ssm_fwd(u, bmat, cmat, tab_f, d_skip, tb_ssm, _after(token) if dist else None)[:2]
    w_ffi, w_ffo = _gather_wait("gather_ff_wait", send, recv, lands, s) if dist else (big["w_ff_in"], big["w_ff_out"])
    w_ffi = [w_ffi]
    x2 = _merge_fwd(x, s, att, ga, gs, g2, w_glu, w_ab, w_sb, w_out, tb)
    dy, df, h3, loss_acc, dg4 = _mlp_fwd_loss(x2, target, g3, g4, w_ffi, w_ffo, tb)

    dx2, act, da, dg3 = _mlp_bwd(x2, dy, df, h3, g3, w_ffi, w_ffo, tb)
    tl = min(2048, x.shape[0])
    chunked = (N_CHIPS, D_FF // N_CHIPS, D_MODEL)
    d_ffi, b_ffi = _matmul_tn("grad_w_ff_in", h3, da, D_MODEL, D_FF // FF_CHUNKS, tl, True)
    d_ffo, b_ffo = _matmul_tn("grad_w_ff_out", act, df, D_FF // FF_CHUNKS, D_MODEL, tl, False)
    d_ffo, b_ffo = d_ffo.reshape(chunked), b_ffo.reshape(chunked)
    outs = _merge_bwd(dx2, s, att, ga, gs, g2, w_glu, w_ab, w_sb, w_out, tb_ssm,
                      _scatter_chips([b_ffi]) if dist else None)
    ds, datt, dga, dgs, dg2, d_glu, d_ab, d_sb, d_out, b_glu, b_ab, b_sb, b_out = outs[:13]
    r_ffi = outs[13:]
    glu4, out4 = (N_CHIPS, SSM_W // N_CHIPS, SSM_W), (N_CHIPS, D_MODEL // N_CHIPS, D_MODEL)
    d_mix = [d_glu.reshape(glu4), d_ab, d_sb, d_out.reshape(out4)]
    b_mix = [b_glu.reshape(glu4), b_ab, b_sb, b_out.reshape(out4)]
    outs = _ssm_bwd(ds, u, h, bmat.transpose(0, 2, 1), cmat.transpose(0, 2, 1), tab_b, d_skip, tb_ssm,
                    _scatter_chips([b_ffo]) if dist else None)
    du, d_bmat, d_cmat, da_acc, dd_skip = outs[:5]
    r_ffo = outs[5:]
    outs = _attn_bwd(q, k, v, datt, bias, sink_rows, _scatter_chips(b_mix) if dist else None)
    dq, dk, dv, dbias, dsink_rows = outs[:5]
    r_mix = outs[5:]
    if dist:
        p_ffi = _sum4("sum_w_ff_in", d_ffi, r_ffi[0], me)
        p_ffo = _sum4("sum_w_ff_out", d_ffo, r_ffo[0], me)
    dx, dpj, dg1 = _inproj_bwd(x, dx2, dq, dk, dv, du, dga, dgs, g1, w_in, tb)

    dab_re, dab_im = _state_unlayout(jnp.sum(da_acc, axis=0))
    dbb_re, dbb_im = _b_matrix_grad(d_bmat)
    d_lam_re, d_lam_im, d_log_dt, d_b_re, d_b_im = disc_vjp((dab_re, dab_im, dbb_re, dbb_im))
    d_c_re, d_c_im = _c_matrix_grad(d_cmat)
    d_rel = _bias_grad(dbias, bucket)
    d_sinks = jnp.sum(_pair_unlayout(keys_first(dsink_rows)), axis=(1, 2))
    small_grads = dict(
        norm_mix_pre=dg1, norm_mix_post=dg2, norm_mlp_pre=dg3, norm_mlp_post=dg4, rel_bias=d_rel, sinks=d_sinks,
        lam_re=d_lam_re, lam_im=d_lam_im, log_dt=d_log_dt, b_re=d_b_re, b_im=d_b_im, c_re=d_c_re, c_im=d_c_im,
        d_skip=dd_skip)
    ride = _both(_swap_sibling([p_ffi, p_ffo]), _gather_devices(_pack(small_grads, loss_acc))) if dist else None
    outs = _matmul_tn("grad_w_in", dpj, h1, IN_W // 2, D_MODEL, tl, False, ride)
    in4 = (N_CHIPS, IN_W // N_CHIPS, D_MODEL)
    d_in, b_in = outs[0].reshape(in4), outs[1].reshape(in4)
    if not dist:
        return loss_acc, dx, small_grads, dict(zip(BIG, [d_in] + d_mix + [d_ffi, d_ffo]))
    s_ffi, s_ffo, slots = outs[2:]
    p_mix = [_sum4("sum_" + n, d, r, me) for n, d, r in zip(mix, d_mix, r_mix)]
    pending = dict(d_in=d_in, b_in=b_in, p_mix=p_mix, w_ff_in=(p_ffi, s_ffi), w_ff_out=(p_ffo, s_ffo), me=me)
    return loss_acc, dx, _sum_devices(slots), pending


SMALL = ['norm_mix_pre', 'norm_mix_post', 'norm_mlp_pre', 'norm_mlp_post', 'rel_bias', 'sinks', 'lam_re', 'lam_im',
         'log_dt', 'b_re', 'b_im', 'c_re', 'c_im', 'd_skip']
BIG = ['w_in', 'w_glu', 'w_attn_branch', 'w_ssm_branch', 'w_out', 'w_ff_in', 'w_ff_out']
WEIGHTS = ['norm_mix_pre', 'norm_mix_post', 'norm_mlp_pre', 'norm_mlp_post', 'w_in', 'rel_bias', 'sinks', 'lam_re',
           'lam_im', 'log_dt', 'b_re', 'b_im', 'c_re', 'c_im', 'd_skip', 'w_glu', 'w_attn_branch', 'w_ssm_branch',
           'w_out', 'w_ff_in', 'w_ff_out']
PACK_COLS = 1024
PACK_ORDER = ['b_re', 'b_im', 'c_re', 'c_im', 'lam_re', 'lam_im', 'norm_mix_pre', 'norm_mix_post', 'norm_mlp_pre',
              'norm_mlp_post', 'rel_bias', 'sinks', 'log_dt', 'd_skip']


STATE_MINOR = ('b_re', 'b_im')
PACK_ROWS = 144
LOSS_ROW = 140


def _pack(named, loss_acc):
    parts = []
    for n in PACK_ORDER:
        a = jnp.swapaxes(named[n], -1, -2) if n in STATE_MINOR else named[n]
        flat = a.reshape(-1)
        rows = -(-flat.shape[0] // PACK_COLS)
        parts.append(jnp.pad(flat, (0, rows * PACK_COLS - flat.shape[0])).reshape(rows, PACK_COLS))
    assert sum(p.shape[0] for p in parts) == LOSS_ROW
    parts.append(jnp.pad(loss_acc[0:1], ((0, PACK_ROWS - LOSS_ROW - 1), (0, PACK_COLS - loss_acc.shape[1]))))
    return jnp.concatenate(parts, axis=0)


def _unpack(packed, shapes):
    out, at = {}, 0
    for n in PACK_ORDER:
        shape = shapes[n][:-2] + (shapes[n][-1], shapes[n][-2]) if n in STATE_MINOR else shapes[n]
        size = int(np.prod(shape))
        rows = -(-size // PACK_COLS)
        blk = packed[at:at + rows]
        out[n] = (blk.reshape(-1)[:size] if size % PACK_COLS else blk).reshape(shape)
        at += rows
    return out


def kernel(x, norm_mix_pre, norm_mix_post, norm_mlp_pre, norm_mlp_post, w_in, rel_bias, sinks, lam_re, lam_im, log_dt, b_re, b_im, c_re, c_im, d_skip, w_glu, w_attn_branch, w_ssm_branch, w_out, w_ff_in, w_ff_out, loss_target, m_norm_mix_pre, m_norm_mix_post, m_norm_mlp_pre, m_norm_mlp_post, m_w_in, m_rel_bias, m_sinks, m_lam_re, m_lam_im, m_log_dt, m_b_re, m_b_im, m_c_re, m_c_im, m_d_skip, m_w_glu, m_w_attn_branch, m_w_ssm_branch, m_w_out, m_w_ff_in, m_w_ff_out, v_norm_mix_pre, v_norm_mix_post, v_norm_mlp_pre, v_norm_mlp_post, v_w_in, v_rel_bias, v_sinks, v_lam_re, v_lam_im, v_log_dt, v_b_re, v_b_im, v_c_re, v_c_im, v_d_skip, v_w_glu, v_w_attn_branch, v_w_ssm_branch, v_w_out, v_w_ff_in, v_w_ff_out):
    env = dict(locals())
    w = {n: env[n] for n in WEIGHTS}
    m = {n: env["m_" + n] for n in WEIGHTS}
    v = {n: env["v_" + n] for n in WEIGHTS}
    seq = x.shape[1]
    tb = min(512, seq)

    small = {n: w[n] for n in ('norm_mix_pre', 'norm_mix_post', 'norm_mlp_pre', 'norm_mlp_post', 'rel_bias')}
    small.update({n: w[n][0] for n in ('sinks', 'lam_re', 'lam_im', 'log_dt', 'b_re', 'b_im', 'c_re', 'c_im')})
    small['d_skip'] = w['d_skip']
    shard = lambda t, n: t[n][0].T if n == 'w_in' else t[n][0]
    unshard = lambda a, n: (a.T if n == 'w_in' else a)[None]
    _, dx, small_g, pending = _local_step(
        x[0], loss_target[0], small, {n: _bf(shard(w, n)) for n in BIG}, tb, True)

    loss = small_g[LOSS_ROW, 0]

    grads, deltas, new_m, new_v = {}, {}, {}, {}

    def adam(n, partials, after=None):
        outs = _adam_pair("adam_" + n, (shard(w, n), *partials, shard(m, n), shard(v, n)), after)
        grads[n], deltas[n], new_m[n], new_v[n] = [unshard(a, n) for a in outs]
        return outs[3]

    mix = ("w_glu", "w_attn_branch", "w_ssm_branch", "w_out")
    *in_flight, token = _scatter_start("scatter_w_in_start", pending["b_in"])
    sib_mix = _exchange_alone("swap_mix", _swap_sibling(pending["p_mix"]))
    last = None
    for n, partials in [(n, pending[n]) for n in ("w_ff_in", "w_ff_out")] + list(zip(mix, zip(pending["p_mix"], sib_mix))):
        last = adam(n, partials, token)
    r_in = _scatter_wait("scatter_w_in_wait", *in_flight, last)
    p_in = _sum4("sum_w_in", pending["d_in"], r_in, pending["me"])
    (s_in,) = _exchange_alone("swap_w_in", _swap_sibling([p_in]))
    adam("w_in", (p_in, s_in))

    minor = lambda t, n: jnp.swapaxes(t, -1, -2) if n in STATE_MINOR else t
    g_small = _unpack(small_g, {n: w[n].shape for n in SMALL})
    outs = _adam_small([minor(w[n], n) for n in SMALL], [g_small[n] for n in SMALL],
                       [minor(m[n], n) for n in SMALL], [minor(v[n], n) for n in SMALL])
    grads.update({n: minor(g_small[n], n) for n in SMALL})
    for k, dst in enumerate((deltas, new_m, new_v)):
        dst.update({n: minor(a, n) for n, a in zip(SMALL, outs[k * len(SMALL):(k + 1) * len(SMALL)])})

    return (loss, dx[None], *[grads[n] for n in WEIGHTS], *[deltas[n] for n in WEIGHTS],
            *[new_m[n] for n in WEIGHTS], *[new_v[n] for n in WEIGHTS])
```

```python
import functools
import math

import numpy as np
import jax
import jax.numpy as jnp
from jax import lax
from jax.experimental import pallas as pl
from jax.experimental.pallas import tpu as pltpu

F32 = jnp.float32
BF16 = jnp.bfloat16

D_MODEL = 1024
N_HEADS = 8
N_KV = 2
Q_GROUP = 4
HEAD_DIM = 64
ATTN_W = 512
KV_W = 128
BLOCK = 128
N_BUCKETS = 32
MAX_DISTANCE = 128
NEG_INF = -1e30
SSM_W = 512
SSM_GROUP = 16
SSM_GROUPS = 32
SSM_STATE = 64
N_SUPER = 4
GROUPS_PER_SUPER = SSM_GROUPS // N_SUPER
SUPER_IN = GROUPS_PER_SUPER * SSM_GROUP
SUPER_HALF = GROUPS_PER_SUPER * SSM_STATE
SUPER_W = 2 * SUPER_HALF
STATE_COLS = N_SUPER * SUPER_W
D_FF = 4096
FF_CHUNKS = 4
IN_W = 3328
SPLITS = (0, 512, 640, 768, 1280, 2304, 3328)
RMS_EPS = 1e-6
N_CHIPS = 4
SUBLANES = 8
LANES = 128
STATE_TILES = STATE_COLS // LANES
SUPER_TILES = SUPER_W // LANES

ADAM_LR = 0.001
ADAM_B1 = 0.9
ADAM_B2 = 0.999
ADAM_EPS = 1e-08
ADAM_WD = 0.01
ADAM_STEP = 10

VMEM_BIG = 56 * 1024 * 1024
SDS = jax.ShapeDtypeStruct
MESH_ID = pl.DeviceIdType.MESH
ANY = pl.BlockSpec(memory_space=pl.ANY)


def _bf(x):
    return x.astype(BF16)


def _mm(a, b):
    return jnp.dot(a, b, preferred_element_type=F32)


def _mm_nt(a, b):
    return lax.dot_general(a, b, (((1,), (1,)), ((), ())), preferred_element_type=F32)


def _mm_tn(a, b):
    return lax.dot_general(a, b, (((0,), (0,)), ((), ())), preferred_element_type=F32)


def _sig(x):
    return 1.0 / (1.0 + jnp.exp(-x))


def _rms(x, g):
    r = lax.rsqrt(jnp.mean(x * x, axis=-1, keepdims=True) + RMS_EPS)
    xh = x * r
    return xh * g, xh, r


def _rms_bwd(dout, xh, r, g):
    dg = jnp.sum(dout * xh, axis=0, keepdims=True)
    dxh = dout * g
    dx = r * (dxh - xh * jnp.mean(dxh * xh, axis=-1, keepdims=True))
    return dx, dg


_GELU_C = math.sqrt(2.0 / math.pi)


def _gelu_and_grad(x):
    x2 = x * x
    inner = _GELU_C * (x + 0.044715 * (x2 * x))
    t = jnp.tanh(inner)
    y = 0.5 * x * (1.0 + t)
    dy = 0.5 * (1.0 + t) + 0.5 * x * (1.0 - t * t) * (_GELU_C * (1.0 + 3.0 * 0.044715 * x2))
    return y, dy


def _zero_map(nd, *_):
    return (0,) * nd


def _params(n_axes, vmem=None):
    return pltpu.CompilerParams(dimension_semantics=("arbitrary",) * n_axes, vmem_limit_bytes=vmem)


class _Exchange:
    def __init__(self, ins, outs, sems, start, wait):
        self.ins, self.outs, self.sems, self.start, self.wait = list(ins), list(outs), list(sems), start, wait


def _fused_call(name, body, grid, in_specs, out_specs, out_shape, scratch, args, exchange, params):
    n_in, n_out, n_scr = len(in_specs), len(out_specs), len(scratch)
    if exchange is None:
        fn = body
    else:
        ex = exchange
        n_xi, n_xo = len(ex.ins), len(ex.outs)

        def fn(*refs):
            at = 0
            parts = []
            for n in (n_in, n_xi, n_out, n_xo, n_scr, len(ex.sems)):
                parts.append(refs[at:at + n])
                at += n
            ins, x_in, outs, x_out, scr, x_sem = parts
            ids = [pl.program_id(a) for a in range(len(grid))]
            first = functools.reduce(jnp.logical_and, [i == 0 for i in ids])
            last = functools.reduce(jnp.logical_and, [i == g - 1 for i, g in zip(ids, grid)])

            @pl.when(first)
            def _():
                ex.start(x_in, x_out, x_sem)

            body(*ins, *outs, *scr)

            @pl.when(last)
            def _():
                ex.wait(x_in, x_out, x_sem)

        in_specs = list(in_specs) + [ANY] * n_xi
        out_specs = list(out_specs) + [ANY] * n_xo
        out_shape = list(out_shape) + ex.outs
        scratch = list(scratch) + ex.sems
        args = list(args) + ex.ins
    return pl.pallas_call(fn, grid=grid, in_specs=in_specs, out_specs=out_specs, out_shape=out_shape,
                          scratch_shapes=list(scratch), name=name, compiler_params=params)(*args)


def _exchange_alone(name, ex):
    def body(*refs):
        n_xi, n_xo = len(ex.ins), len(ex.outs)
        x_in, x_out, x_sem = refs[:n_xi], refs[n_xi:n_xi + n_xo], refs[n_xi + n_xo:]
        ex.start(x_in, x_out, x_sem)
        ex.wait(x_in, x_out, x_sem)

    return pl.pallas_call(body, in_specs=[ANY] * len(ex.ins), out_specs=[ANY] * len(ex.outs), out_shape=ex.outs,
                          scratch_shapes=ex.sems, name=name)(*ex.ins)


def _rowcall(name, body, seq, tb, rows, consts, row_outs, acc_outs, scratch=(), reverse=False, vmem=None,
             exchange=None):
    nb = seq // tb
    rmap = (lambda i: (nb - 1 - i, 0)) if reverse else (lambda i: (i, 0))
    tmap = lambda i: (0,) + rmap(i)

    def row_spec(width):
        if isinstance(width, tuple):
            return pl.BlockSpec((width[0], tb, width[1]), tmap)
        return pl.BlockSpec((tb, width), rmap)

    def row_shape(width):
        return (width[0], seq, width[1]) if isinstance(width, tuple) else (seq, width)

    in_specs = [row_spec(a.shape[1] if a.ndim == 2 else (a.shape[0], a.shape[2])) for a in rows]
    in_specs += [pl.BlockSpec(a.shape, functools.partial(_zero_map, a.ndim), pipeline_mode=pl.Buffered(1))
                 for a in consts]
    out_specs = [row_spec(c) for c, _ in row_outs] + [ANY] * len(acc_outs)
    out_shape = [SDS(row_shape(c), dt) for c, dt in row_outs] + [SDS(s, dt) for s, dt in acc_outs]
    n_main = len(rows) + len(consts) + len(row_outs)
    n_acc = len(acc_outs)

    def fn(*refs):
        main, acc_hbm, rest = refs[:n_main], refs[n_main:n_main + n_acc], refs[n_main + n_acc:]
        acc_vmem, own = rest[:n_acc], rest[n_acc:]
        body(*main, *acc_vmem, *own)

        @pl.when(pl.program_id(0) == nb - 1)
        def _():
            for src, dst in zip(acc_vmem, acc_hbm):
                pltpu.sync_copy(src, dst)

    buffers = [pltpu.VMEM(s, dt) for s, dt in acc_outs] + list(scratch)
    return _fused_call(name, fn if acc_outs else body, (nb,), in_specs, out_specs, out_shape, buffers,
                       [*rows, *consts], exchange, _params(1, vmem))


def _inproj_fwd(x, g1, w_in, tb, exchange=None):
    seq = x.shape[0]

    def body(x_ref, g_ref, w_ref, h_ref, q_ref, k_ref, v_ref, u_ref, ga_ref, gs_ref):
        h, _, _ = _rms(x_ref[...], g_ref[...])
        hb = _bf(h)
        h_ref[...] = hb
        pj = _mm_nt(hb, w_ref[...])
        q_ref[...] = _bf(pj[:, SPLITS[0]:SPLITS[1]])
        k_ref[...] = _bf(pj[:, SPLITS[1]:SPLITS[2]])
        v_ref[...] = _bf(pj[:, SPLITS[2]:SPLITS[3]])
        u_ref[...] = pj[:, SPLITS[3]:SPLITS[4]]
        ga_ref[...] = pj[:, SPLITS[4]:SPLITS[5]]
        gs_ref[...] = pj[:, SPLITS[5]:SPLITS[6]]

    return _rowcall("inproj_fwd", body, seq, tb, [x], [g1, w_in],
                    [(D_MODEL, BF16), (ATTN_W, BF16), (KV_W, BF16), (KV_W, BF16), (SSM_W, F32),
                     (D_MODEL, F32), (D_MODEL, F32)], [], vmem=VMEM_BIG, exchange=exchange)


def _inproj_bwd(x, dx2, dq, dk, dv, du, dga, dgs, g1, w_in, tb, exchange=None):
    seq = x.shape[0]

    def body(x_ref, dx2_ref, dq_ref, dk_ref, dv_ref, du_ref, dga_ref, dgs_ref, g_ref, w_ref,
             dx_ref, dpj_ref, dg_ref):
        @pl.when(pl.program_id(0) == 0)
        def _():
            dg_ref[...] = jnp.zeros_like(dg_ref)

        dpj = jnp.concatenate([dq_ref[...], dk_ref[...], dv_ref[...], _bf(du_ref[...]),
                               dga_ref[...], dgs_ref[...]], axis=1)
        dpj_ref[...] = dpj
        dh = _mm(dpj, w_ref[...])
        g = g_ref[...]
        _, xh, r = _rms(x_ref[...], g)
        dxn, dg = _rms_bwd(dh, xh, r, g)
        dx_ref[...] = dx2_ref[...] + dxn
        dg_ref[...] += dg

    return _rowcall("inproj_bwd", body, seq, tb, [x, dx2, dq, dk, dv, du, dga, dgs], [g1, w_in],
                    [(D_MODEL, F32), (IN_W, BF16)], [((1, D_MODEL), F32)], vmem=VMEM_BIG, exchange=exchange)


def _bucket_table():
    qi = np.arange(BLOCK)[:, None]
    kj = np.arange(2 * BLOCK)[None, :]
    dist = qi + BLOCK - kj
    max_exact = N_BUCKETS // 2
    d = np.maximum(dist, 0)
    df = np.maximum(d, 1).astype(np.float32)
    large = max_exact + (np.log(df / np.float32(max_exact)) / np.float32(math.log(MAX_DISTANCE / max_exact))
                         * np.float32(N_BUCKETS - max_exact)).astype(np.int32)
    large = np.minimum(large, N_BUCKETS - 1)
    bucket = np.where(d < max_exact, d, large)
    valid = (dist >= 0) & (dist < BLOCK)
    return np.where(valid, bucket, -1).astype(np.int32)


def _bias_table(rel_bias, bucket):
    def body(rb_ref, bk_ref, o_ref):
        bk = bk_ref[...]
        has_prev = lax.broadcasted_iota(jnp.int32, bk.shape, 1) >= BLOCK
        for h in range(N_HEADS):
            kh, j, par = h // Q_GROUP, (h // 2) % 2, h % 2
            acc = jnp.full((BLOCK, 2 * BLOCK), NEG_INF, F32)
            for b in range(N_BUCKETS):
                acc = jnp.where(bk == b, rb_ref[b, h], acc)
            o_ref[0, kh, par, :, j * BLOCK:(j + 1) * BLOCK] = jnp.where(has_prev, acc, NEG_INF).T
            o_ref[1, kh, par, :, j * BLOCK:(j + 1) * BLOCK] = acc.T

    return pl.pallas_call(
        body, out_shape=SDS((2, N_KV, 2, 2 * BLOCK, 2 * BLOCK), F32),
        in_specs=[pl.BlockSpec(memory_space=pltpu.SMEM), pl.BlockSpec(memory_space=pltpu.VMEM)],
        out_specs=pl.BlockSpec(memory_space=pltpu.VMEM), name="bias_table",
    )(rel_bias, bucket)


def _bias_grad(dbias, bucket):
    def body(db_ref, bk_ref, o_ref):
        bk = bk_ref[...]
        for h in range(N_HEADS):
            kh, j, par = h // Q_GROUP, (h // 2) % 2, h % 2
            db = db_ref[kh, par, :, j * BLOCK:(j + 1) * BLOCK].T
            for b in range(N_BUCKETS):
                o_ref[b, h] = jnp.sum(jnp.where(bk == b, db, 0.0))

    return pl.pallas_call(
        body, out_shape=SDS((N_BUCKETS, N_HEADS), F32),
        in_specs=[pl.BlockSpec(memory_space=pltpu.VMEM), pl.BlockSpec(memory_space=pltpu.VMEM)],
        out_specs=pl.BlockSpec(memory_space=pltpu.SMEM), name="bias_grad",
    )(dbias, bucket)


TILE = 2 * HEAD_DIM


def _pair_layout(t):
    lead = t.shape[:-3]
    t = t.reshape(lead + (N_KV, 2, 2) + t.shape[-2:])
    nl = len(lead)
    t = jnp.transpose(t, tuple(range(nl)) + (nl, nl + 2, nl + 1, nl + 3, nl + 4))
    return t.reshape(lead + (N_KV, 2, 2 * BLOCK, t.shape[-1]))


def _pair_unlayout(t):
    t = t.reshape(N_KV, 2, 2, BLOCK, t.shape[-1]).transpose(0, 2, 1, 3, 4)
    return t.reshape(N_HEADS, BLOCK, t.shape[-1])


def _halves(t):
    tf = t.astype(F32)
    low = lax.broadcasted_iota(jnp.int32, tf.shape, 1) < HEAD_DIM
    swapped = pltpu.roll(tf, HEAD_DIM, 1)
    zero = jnp.zeros_like(tf)
    return ((_bf(jnp.where(low, tf, zero)), _bf(jnp.where(low, zero, swapped))),
            (_bf(jnp.where(low, swapped, zero)), _bf(jnp.where(low, zero, tf))))


def _fold_halves(even, odd):
    low = lax.broadcasted_iota(jnp.int32, even.shape, 1) < HEAD_DIM
    comb = jnp.where(low, even, odd)
    return comb + pltpu.roll(comb, HEAD_DIM, 1)


def _tile_rows(ref, kh):
    return jnp.concatenate([ref[:, (2 * kh) * TILE:(2 * kh + 1) * TILE],
                            ref[:, (2 * kh + 1) * TILE:(2 * kh + 2) * TILE]], axis=0)


def _halves_t(t):
    tt = t.astype(F32).T
    top = lax.broadcasted_iota(jnp.int32, tt.shape, 0) < HEAD_DIM
    swapped = jnp.concatenate([tt[HEAD_DIM:], tt[:HEAD_DIM]], axis=0)
    zero = jnp.zeros_like(tt)
    return ((_bf(jnp.where(top, tt, zero)), _bf(jnp.where(top, zero, swapped))),
            (_bf(jnp.where(top, swapped, zero)), _bf(jnp.where(top, zero, tt))))


def _attn_probs(km, qk, bias, sink):
    lg = _mm_nt(km, qk) * (HEAD_DIM ** -0.5) + bias
    m = jnp.maximum(jnp.max(lg, axis=0, keepdims=True), sink)
    p = jnp.exp(lg - m)
    es = jnp.exp(sink - m)
    inv = 1.0 / (jnp.sum(p, axis=0, keepdims=True) + es)
    return p * inv, es * inv


def _attn_fwd(q, k, v, bias, sink_rows, exchange=None):
    seq = q.shape[0]
    nblk = seq // BLOCK

    def body(q_ref, kp_ref, kc_ref, vp_ref, vc_ref, b_ref, s_ref, o_ref):
        which = jnp.minimum(pl.program_id(0), 1)
        kms = _halves(jnp.concatenate([kp_ref[...], kc_ref[...]], axis=0))
        vts = _halves_t(jnp.concatenate([vp_ref[...], vc_ref[...]], axis=0))
        for kh in range(N_KV):
            qk = _tile_rows(q_ref, kh)
            acc = jnp.zeros((TILE, 2 * BLOCK), F32)
            for par in range(2):
                pr, _ = _attn_probs(kms[kh][par], qk, b_ref[which, kh, par], s_ref[kh, par])
                acc = acc + _mm(vts[kh][par], _bf(pr))
            acc = acc.T
            o_ref[:, (2 * kh) * TILE:(2 * kh + 1) * TILE] = _bf(acc[:BLOCK])
            o_ref[:, (2 * kh + 1) * TILE:(2 * kh + 2) * TILE] = _bf(acc[BLOCK:])

    cur = lambda n: (n, 0)
    prev = lambda n: (jnp.maximum(n - 1, 0), 0)
    return _fused_call(
        "attn_fwd", body, (nblk,),
        [pl.BlockSpec((BLOCK, ATTN_W), cur),
         pl.BlockSpec((BLOCK, KV_W), prev), pl.BlockSpec((BLOCK, KV_W), cur),
         pl.BlockSpec((BLOCK, KV_W), prev), pl.BlockSpec((BLOCK, KV_W), cur),
         pl.BlockSpec(bias.shape, functools.partial(_zero_map, bias.ndim)),
         pl.BlockSpec(sink_rows.shape, functools.partial(_zero_map, sink_rows.ndim))],
        [pl.BlockSpec((BLOCK, ATTN_W), cur)], [SDS((seq, ATTN_W), BF16)], [],
        [q, k, k, v, v, bias, sink_rows], exchange, _params(1))


def _attn_bwd(q, k, v, d_out, bias, sink_rows, exchange=None):
    seq = q.shape[0]
    nblk = seq // BLOCK

    def body(q_ref, kp_ref, kc_ref, vp_ref, vc_ref, do_ref, b_ref, s_ref,
             dq_ref, dk_ref, dv_ref, db_ref, ds_ref, ck_ref, cv_ref):
        n = pl.program_id(0)

        @pl.when(n == 0)
        def _():
            db_ref[...] = jnp.zeros_like(db_ref)
            ds_ref[...] = jnp.zeros_like(ds_ref)
            ck_ref[...] = jnp.zeros_like(ck_ref)
            cv_ref[...] = jnp.zeros_like(cv_ref)

        @pl.when(n < nblk)
        def _():
            which = jnp.minimum(n, 1)
            scale = HEAD_DIM ** -0.5
            kcat = jnp.concatenate([kp_ref[...], kc_ref[...]], axis=0)
            kms = _halves(kcat)
            kts = _halves_t(kcat)
            vms = _halves(jnp.concatenate([vp_ref[...], vc_ref[...]], axis=0))
            dks, dvs = [], []
            for kh in range(N_KV):
                qk = _tile_rows(q_ref, kh)
                dok = _tile_rows(do_ref, kh)
                dq = jnp.zeros((TILE, 2 * BLOCK), F32)
                dkp, dvp = [], []
                for par in range(2):
                    pr, ps = _attn_probs(kms[kh][par], qk, b_ref[which, kh, par], s_ref[kh, par])
                    dp = _mm_nt(vms[kh][par], dok)
                    rs = jnp.sum(pr * dp, axis=0, keepdims=True)
                    dlg = pr * (dp - rs)
                    ds_ref[kh, par] += -ps * rs
                    db_ref[kh, par] += dlg
                    dlb = _bf(dlg)
                    dq = dq + _mm(kts[kh][par], dlb)
                    dkp.append(_mm(dlb, qk))
                    dvp.append(_mm(_bf(pr), dok))
                dq = _bf((dq * scale).T)
                dq_ref[:, (2 * kh) * TILE:(2 * kh + 1) * TILE] = dq[:BLOCK]
                dq_ref[:, (2 * kh + 1) * TILE:(2 * kh + 2) * TILE] = dq[BLOCK:]
                dks.append(_fold_halves(*dkp))
                dvs.append(_fold_halves(*dvp))
            low = lax.broadcasted_iota(jnp.int32, (2 * BLOCK, TILE), 1) < HEAD_DIM
            dkk = jnp.where(low, dks[0], dks[1]) * scale
            dvv = jnp.where(low, dvs[0], dvs[1])
            dk_ref[...] = _bf(ck_ref[...] + dkk[:BLOCK])
            ck_ref[...] = dkk[BLOCK:]
            dv_ref[...] = _bf(cv_ref[...] + dvv[:BLOCK])
            cv_ref[...] = dvv[BLOCK:]

        @pl.when(n == nblk)
        def _():
            dk_ref[...] = _bf(ck_ref[...])
            dv_ref[...] = _bf(cv_ref[...])

    cur = lambda n: (jnp.minimum(n, nblk - 1), 0)
    prev = lambda n: (jnp.maximum(jnp.minimum(n, nblk - 1) - 1, 0), 0)
    late = lambda n: (jnp.maximum(n - 1, 0), 0)
    kv_spec = lambda m: pl.BlockSpec((BLOCK, KV_W), m)
    acc_b = pl.BlockSpec(bias.shape[1:], functools.partial(_zero_map, bias.ndim - 1))
    acc_s = pl.BlockSpec(sink_rows.shape, functools.partial(_zero_map, sink_rows.ndim))
    return _fused_call(
        "attn_bwd", body, (nblk + 1,),
        [pl.BlockSpec((BLOCK, ATTN_W), cur), kv_spec(prev), kv_spec(cur), kv_spec(prev), kv_spec(cur),
         pl.BlockSpec((BLOCK, ATTN_W), cur),
         pl.BlockSpec(bias.shape, functools.partial(_zero_map, bias.ndim)), acc_s],
        [pl.BlockSpec((BLOCK, ATTN_W), cur), kv_spec(late), kv_spec(late), acc_b, acc_s],
        [SDS((seq, ATTN_W), BF16), SDS((seq, KV_W), BF16), SDS((seq, KV_W), BF16),
         SDS(bias.shape[1:], F32), SDS(sink_rows.shape, F32)],
        [pltpu.VMEM((BLOCK, KV_W), F32), pltpu.VMEM((BLOCK, KV_W), F32)],
        [q, k, k, v, v, d_out, bias, sink_rows], exchange, _params(1))


def _ssm_discretize(lam_re, lam_im, log_dt, b_re, b_im):
    dt = jnp.exp(log_dt)[:, None]
    mag = jnp.exp(lam_re * dt)
    ab_re = mag * jnp.cos(lam_im * dt)
    ab_im = mag * jnp.sin(lam_im * dt)
    nr = ab_re - 1.0
    den = lam_re * lam_re + lam_im * lam_im
    f_re = (nr * lam_re + ab_im * lam_im) / den
    f_im = (ab_im * lam_re - nr * lam_im) / den
    bb_re = f_re[..., None] * b_re - f_im[..., None] * b_im
    bb_im = f_re[..., None] * b_im + f_im[..., None] * b_re
    return ab_re, ab_im, bb_re, bb_im


def _state_layout(re, im):
    z = jnp.stack([re, im]).reshape(2, N_SUPER, GROUPS_PER_SUPER, SSM_STATE)
    return z.transpose(1, 0, 2, 3).reshape(STATE_COLS)


def _state_unlayout(vec):
    z = vec.reshape(N_SUPER, 2, GROUPS_PER_SUPER, SSM_STATE).transpose(1, 0, 2, 3)
    z = z.reshape(2, SSM_GROUPS, SSM_STATE)
    return z[0], z[1]


SEG = 4
WINDOW = SEG * SUBLANES


def _scan_tables(ab_re, ab_im):
    pw = [None, (ab_re, ab_im)]
    for _ in range(2, WINDOW + 1):
        pr, pi_ = pw[-1]
        pw.append((pr * ab_re - pi_ * ab_im, pr * ab_im + pi_ * ab_re))
    rows = np.arange(SUBLANES)[:, None]
    ones = np.ones((SUBLANES, 1), np.float32)
    conj = lambda p: (p[0], -p[1])
    fwd, bwd = [], []
    for shift in (1, 2, 4):
        fwd.append(_state_layout(*pw[SEG * shift])[None, :] * (rows >= shift).astype(np.float32))
        bwd.append(_state_layout(*conj(pw[SEG * shift]))[None, :] * (rows < SUBLANES - shift).astype(np.float32))
    fwd.append(jnp.stack([_state_layout(*pw[SEG * (r + 1)]) for r in range(SUBLANES)]))
    bwd.append(jnp.stack([_state_layout(*conj(pw[SEG * (SUBLANES - r)])) for r in range(SUBLANES)]))
    for k in range(1, SEG):
        fwd.append(_state_layout(*pw[k])[None, :] * ones)
        bwd.append(_state_layout(*conj(pw[k]))[None, :] * ones)
    return jnp.stack(fwd), jnp.stack(bwd)


_EYE = np.eye(GROUPS_PER_SUPER, dtype=np.float32)


def _b_matrix(bb_re, bb_im):
    bb = jnp.stack([bb_re, bb_im]).reshape(2, N_SUPER, GROUPS_PER_SUPER, SSM_STATE, SSM_GROUP)
    m = jnp.einsum('rsgpc,gh->sgcrhp', bb, _EYE)
    return m.reshape(N_SUPER, SUPER_IN, SUPER_W)


def _b_matrix_grad(dm):
    d = dm.reshape(N_SUPER, GROUPS_PER_SUPER, SSM_GROUP, 2, GROUPS_PER_SUPER, SSM_STATE)
    d = jnp.sum(d * _EYE[None, :, None, None, :, None], axis=4)
    d = d.transpose(3, 0, 1, 4, 2).reshape(2, SSM_GROUPS, SSM_STATE, SSM_GROUP)
    return d[0], d[1]


def _c_matrix(c_re, c_im):
    cc = jnp.stack([c_re, -c_im]).reshape(2, N_SUPER, GROUPS_PER_SUPER, SSM_GROUP, SSM_STATE)
    m = jnp.einsum('rsgcp,gh->srgphc', cc, _EYE)
    return m.reshape(N_SUPER, SUPER_W, SUPER_IN)


def _c_matrix_grad(dm):
    d = dm.reshape(N_SUPER, 2, GROUPS_PER_SUPER, SSM_STATE, GROUPS_PER_SUPER, SSM_GROUP)
    d = jnp.sum(d * _EYE[None, None, :, None, :, None], axis=4)
    d = d.transpose(1, 0, 2, 4, 3).reshape(2, SSM_GROUPS, SSM_GROUP, SSM_STATE)
    return d[0], -d[1]


def _cmul_add(xr, xi, ar, ai, sr, si):
    return xr + ar * sr - ai * si, xi + ar * si + ai * sr


def _scan_rows(buf_ref, tab_ref, carry_ref, n_windows, reverse, h_ref=None, da_ref=None):
    order = list(range(SEG - 1, -1, -1)) if reverse else list(range(SEG))
    near = SUBLANES - 1 if reverse else 0
    far = 0 if reverse else SUBLANES - 1
    s_in = SUBLANES - 1 if reverse else 1
    lanes = lambda tile: pl.ds(tile * LANES, LANES)

    def window(w0, tile_re, tile_im, c_re, c_im, acc):
        rows = lambda t: pl.ds(w0 + t, SUBLANES, stride=SEG)
        get = lambda ref, t: (ref.at[tile_re][rows(t), :], ref.at[tile_im][rows(t), :])
        tab = lambda k: (tab_ref[k, :, lanes(tile_re)], tab_ref[k, :, lanes(tile_im)])

        def put(t, xr, xi):
            buf_ref.at[tile_re][rows(t), :] = xr
            buf_ref.at[tile_im][rows(t), :] = xi

        a1 = tab(4)
        er, ei = get(buf_ref, order[0])
        for t in order[1:]:
            er, ei = _cmul_add(*get(buf_ref, t), *a1, er, ei)
            if t != order[-1]:
                put(t, er, ei)
        for k, shift in enumerate((1, 2, 4)):
            s = (SUBLANES - shift) if reverse else shift
            er, ei = _cmul_add(er, ei, *tab(k), pltpu.roll(er, s, 0), pltpu.roll(ei, s, 0))
        er, ei = _cmul_add(er, ei, *tab(3), c_re, c_im)
        put(order[-1], er, ei)
        sub = lax.broadcasted_iota(jnp.int32, er.shape, 0)
        in_re = jnp.where(sub == near, c_re, pltpu.roll(er, s_in, 0))
        in_im = jnp.where(sub == near, c_im, pltpu.roll(ei, s_in, 0))
        true = {order[-1]: (er, ei)}
        for idx, t in enumerate(order[:-1]):
            true[t] = _cmul_add(*get(buf_ref, t), *tab(4 + idx), in_re, in_im)
            put(t, *true[t])
        carry = (jnp.broadcast_to(er[far:far + 1], er.shape), jnp.broadcast_to(ei[far:far + 1], ei.shape))
        if acc is None:
            return carry, None
        acc_re, acc_im = acc
        for t in range(SEG):
            if t + 1 < SEG:
                gr, gim = true[t + 1]
            else:
                gr = jnp.where(sub == SUBLANES - 1, c_re, pltpu.roll(true[0][0], SUBLANES - 1, 0))
                gim = jnp.where(sub == SUBLANES - 1, c_im, pltpu.roll(true[0][1], SUBLANES - 1, 0))
            hr, hi = get(h_ref, t)
            acc_re = acc_re + gr * hr + gim * hi
            acc_im = acc_im + gim * hr - gr * hi
        return carry, (acc_re, acc_im)

    half = SUPER_HALF // LANES
    per = 2 if h_ref is None else 4
    for sb in range(N_SUPER):
        pairs = [(2 * half * sb + j, 2 * half * sb + half + j) for j in range(half)]

        def step(wi, state, pairs=pairs):
            w = (n_windows - 1 - wi) if reverse else wi
            w0 = pl.multiple_of(w * WINDOW, WINDOW)
            out = []
            for j, (tile_re, tile_im) in enumerate(pairs):
                mine = state[per * j:per * (j + 1)]
                carry, acc = window(w0, tile_re, tile_im, mine[0], mine[1], mine[2:] or None)
                out += list(carry) + list(acc or ())
            return tuple(out)

        init = []
        for tile_re, tile_im in pairs:
            init += [carry_ref[:, lanes(tile_re)], carry_ref[:, lanes(tile_im)]]
            if h_ref is not None:
                init += [da_ref[:, lanes(tile_re)], da_ref[:, lanes(tile_im)]]
        fin = lax.fori_loop(0, n_windows, step, tuple(init))
        for j, (tile_re, tile_im) in enumerate(pairs):
            carry_ref[:, lanes(tile_re)] = fin[per * j]
            carry_ref[:, lanes(tile_im)] = fin[per * j + 1]
            if h_ref is not None:
                da_ref[:, lanes(tile_re)] = fin[per * j + 2]
                da_ref[:, lanes(tile_im)] = fin[per * j + 3]


def _put_tiles(ref, sb, block):
    for j in range(SUPER_TILES):
        ref[sb * SUPER_TILES + j] = block[:, j * LANES:(j + 1) * LANES]


def _get_tiles(ref, sb):
    return jnp.concatenate([ref[sb * SUPER_TILES + j] for j in range(SUPER_TILES)], axis=1)


def _ssm_fwd(u, bmat, cmat, tab, d_skip, tb, exchange=None):
    seq = u.shape[0]

    def body(u_ref, b_ref, c_ref, t_ref, d_ref, s_ref, h_ref, carry_ref):
        @pl.when(pl.program_id(0) == 0)
        def _():
            carry_ref[...] = jnp.zeros_like(carry_ref)

        u_blk = u_ref[...]
        ub = _bf(u_blk)
        for sb in range(N_SUPER):
            _put_tiles(h_ref, sb, _mm(ub[:, sb * SUPER_IN:(sb + 1) * SUPER_IN], b_ref[sb]))
        _scan_rows(h_ref, t_ref, carry_ref, tb // WINDOW, False)
        ys = [_mm(_bf(_get_tiles(h_ref, sb)), c_ref[sb]) for sb in range(N_SUPER)]
        s_ref[...] = jnp.concatenate(ys, axis=1) + d_ref[...] * u_blk

    return _rowcall("ssm_fwd", body, seq, tb, [u], [bmat, cmat, tab, d_skip],
                    [(SSM_W, F32), ((STATE_TILES, LANES), F32)], [],
                    scratch=[pltpu.VMEM((SUBLANES, STATE_COLS), F32)], vmem=VMEM_BIG, exchange=exchange)


def _ssm_bwd(ds, u, h, bmat_t, cmat_t, tab, d_skip, tb, exchange=None):
    seq = u.shape[0]

    def body(ds_ref, u_ref, h_ref, bt_ref, ct_ref, t_ref, d_ref,
             du_ref, db_ref, dc_ref, da_ref, dd_ref, g_ref, carry_ref):
        @pl.when(pl.program_id(0) == 0)
        def _():
            carry_ref[...] = jnp.zeros_like(carry_ref)
            db_ref[...] = jnp.zeros_like(db_ref)
            dc_ref[...] = jnp.zeros_like(dc_ref)
            da_ref[...] = jnp.zeros_like(da_ref)
            dd_ref[...] = jnp.zeros_like(dd_ref)

        ds_blk = ds_ref[...]
        dsb = _bf(ds_blk)
        u_blk = u_ref[...]
        ub = _bf(u_blk)
        for sb in range(N_SUPER):
            _put_tiles(g_ref, sb, _mm(dsb[:, sb * SUPER_IN:(sb + 1) * SUPER_IN], ct_ref[sb]))
        _scan_rows(g_ref, t_ref, carry_ref, tb // WINDOW, True, h_ref=h_ref, da_ref=da_ref)
        dus = []
        for sb in range(N_SUPER):
            gb = _bf(_get_tiles(g_ref, sb))
            dus.append(_mm(gb, bt_ref[sb]))
            db_ref[sb] += _mm_tn(ub[:, sb * SUPER_IN:(sb + 1) * SUPER_IN], gb)
            dc_ref[sb] += _mm_tn(_bf(_get_tiles(h_ref, sb)), dsb[:, sb * SUPER_IN:(sb + 1) * SUPER_IN])
        du_ref[...] = jnp.concatenate(dus, axis=1) + d_ref[...] * ds_blk
        dd_ref[...] += jnp.sum(ds_blk * u_blk, axis=0, keepdims=True)

    return _rowcall("ssm_bwd", body, seq, tb, [ds, u, h], [bmat_t, cmat_t, tab, d_skip],
                    [(SSM_W, F32)],
                    [((N_SUPER, SUPER_IN, SUPER_W), F32), ((N_SUPER, SUPER_W, SUPER_IN), F32),
                     ((SUBLANES, STATE_COLS), F32), ((1, SSM_W), F32)],
                    scratch=[pltpu.VMEM((STATE_TILES, tb, LANES), F32), pltpu.VMEM((SUBLANES, STATE_COLS), F32)],
                    reverse=True, vmem=VMEM_BIG, exchange=exchange)


def _merge_core(s, attb, ga, gs, wg_ref, wab_ref, wsb_ref, wout_ref):
    zg, dgelu = _gelu_and_grad(s)
    zgb = _bf(zg)
    sg = _sig(_mm(zgb, wg_ref[...]))
    z = zg * sg
    zb = _bf(z)
    ys = jnp.concatenate([_mm(zb, wsb_ref[j]) for j in range(N_CHIPS)], axis=1)
    ya = jnp.concatenate([_mm(attb, wab_ref[j]) for j in range(N_CHIPS)], axis=1)
    sa = _sig(ga)
    ss = _sig(gs)
    mgb = _bf(sa * ya + ss * ys)
    o = _mm(mgb, wout_ref[...])
    return dict(zg=zg, dgelu=dgelu, zgb=zgb, sg=sg, zb=zb, ys=ys, ya=ya, sa=sa, ss=ss, mgb=mgb, o=o)


def _merge_fwd(x, s, att, ga, gs, g2, w_glu, w_ab, w_sb, w_out, tb):
    seq = x.shape[0]

    def body(x_ref, s_ref, att_ref, ga_ref, gs_ref, g_ref, wg_ref, wab_ref, wsb_ref, wout_ref, x2_ref):
        f = _merge_core(s_ref[...], att_ref[...], ga_ref[...], gs_ref[...], wg_ref, wab_ref, wsb_ref, wout_ref)
        n, _, _ = _rms(f["o"], g_ref[...])
        x2_ref[...] = x_ref[...] + n

    return _rowcall("merge_fwd", body, seq, tb, [x, s, att, ga, gs], [g2, w_glu, w_ab, w_sb, w_out],
                    [(D_MODEL, F32)], [], vmem=VMEM_BIG)[0]


def _merge_bwd(dx2, s, att, ga, gs, g2, w_glu, w_ab, w_sb, w_out, tb, exchange=None):
    seq = s.shape[0]
    cw = D_MODEL // N_CHIPS
    last = seq // tb - 1

    def body(dx2_ref, s_ref, att_ref, ga_ref, gs_ref, g_ref, wg_ref, wab_ref, wsb_ref, wout_ref,
             ds_ref, datt_ref, dga_ref, dgs_ref, dg_ref, dwg_ref, dwab_ref, dwsb_ref, dwout_ref,
             bwg_ref, bwab_ref, bwsb_ref, bwout_ref):
        @pl.when(pl.program_id(0) == 0)
        def _():
            for r in (dg_ref, dwg_ref, dwab_ref, dwsb_ref, dwout_ref):
                r[...] = jnp.zeros_like(r)

        attb = att_ref[...]
        f = _merge_core(s_ref[...], attb, ga_ref[...], gs_ref[...], wg_ref, wab_ref, wsb_ref, wout_ref)
        g = g_ref[...]
        _, oh, r2 = _rms(f["o"], g)
        do, dg = _rms_bwd(dx2_ref[...], oh, r2, g)
        dg_ref[...] += dg
        dob = _bf(do)
        dwout_ref[...] += _mm_tn(f["mgb"], dob)
        dmg = _mm_nt(dob, wout_ref[...])
        sa, ss = f["sa"], f["ss"]
        dyab = _bf(dmg * sa)
        dysb = _bf(dmg * ss)
        dga_ref[...] = _bf(dmg * f["ya"] * sa * (1.0 - sa))
        dgs_ref[...] = _bf(dmg * f["ys"] * ss * (1.0 - ss))
        dwab = _mm_tn(attb, dyab)
        dwsb = _mm_tn(f["zb"], dysb)
        datt = jnp.zeros((tb, ATTN_W), F32)
        dz = jnp.zeros((tb, SSM_W), F32)
        for j in range(N_CHIPS):
            dwab_ref[j] += dwab[:, j * cw:(j + 1) * cw]
            dwsb_ref[j] += dwsb[:, j * cw:(j + 1) * cw]
            datt = datt + _mm_nt(dyab[:, j * cw:(j + 1) * cw], wab_ref[j])
            dz = dz + _mm_nt(dysb[:, j * cw:(j + 1) * cw], wsb_ref[j])
        datt_ref[...] = _bf(datt)
        sg, zg = f["sg"], f["zg"]
        dglb = _bf(dz * zg * sg * (1.0 - sg))
        dwg_ref[...] += _mm_tn(f["zgb"], dglb)
        dzg = dz * sg + _mm_nt(dglb, wg_ref[...])
        ds_ref[...] = dzg * f["dgelu"]

        @pl.when(pl.program_id(0) == last)
        def _():
            for dst, src in ((bwg_ref, dwg_ref), (bwab_ref, dwab_ref), (bwsb_ref, dwsb_ref), (bwout_ref, dwout_ref)):
                dst[...] = _bf(src[...])

    shapes = [w_glu.shape, w_ab.shape, w_sb.shape, w_out.shape]
    return _rowcall("merge_bwd", body, seq, tb, [dx2, s, att, ga, gs], [g2, w_glu, w_ab, w_sb, w_out],
                    [(SSM_W, F32), (ATTN_W, BF16), (D_MODEL, BF16), (D_MODEL, BF16)],
                    [((1, D_MODEL), F32)] + [(sh, F32) for sh in shapes] + [(sh, BF16) for sh in shapes],
                    vmem=VMEM_BIG, exchange=exchange)


def _mlp_fwd_loss(x2, target, g3, g4, w_ffi, w_ffo, tb):
    seq = x2.shape[0]
    n_slab = len(w_ffi)
    sw = D_FF // FF_CHUNKS // n_slab

    def body(x2_ref, t_ref, g3_ref, g4_ref, *rest):
        wi_refs, (wo_ref, dy_ref, df_ref, h_ref, loss_ref, dg_ref) = rest[:n_slab], rest[n_slab:]

        @pl.when(pl.program_id(0) == 0)
        def _():
            loss_ref[...] = jnp.zeros_like(loss_ref)
            dg_ref[...] = jnp.zeros_like(dg_ref)

        x2_blk = x2_ref[...]
        h3, _, _ = _rms(x2_blk, g3_ref[...])
        hb = _bf(h3)
        h_ref[...] = hb
        f = jnp.zeros((tb, D_MODEL), F32)
        for j in range(FF_CHUNKS):
            for k in range(n_slab):
                a = _mm(hb, wi_refs[k][j])
                f = f + _mm(_bf(jnp.square(jnp.maximum(a, 0.0))), wo_ref[j, pl.ds(k * sw, sw), :])
        g4 = g4_ref[...]
        n4, fh, r4 = _rms(f, g4)
        e = (x2_blk + n4) - t_ref[...]
        loss_ref[...] += 0.5 * jnp.sum(jnp.mean(e * e, axis=-1, keepdims=True))
        dy = e * (1.0 / D_MODEL)
        dy_ref[...] = dy
        df, dg = _rms_bwd(dy, fh, r4, g4)
        df_ref[...] = _bf(df)
        dg_ref[...] += dg

    return _rowcall("mlp_fwd_loss", body, seq, tb, [x2, target], [g3, g4, *w_ffi, w_ffo],
                    [(D_MODEL, F32), (D_MODEL, BF16), (D_MODEL, BF16)],
                    [((SUBLANES, 128), F32), ((1, D_MODEL), F32)], vmem=VMEM_BIG)


def _mlp_bwd(x2, dy, df, h3, g3, w_ffi, w_ffo, tb):
    seq = x2.shape[0]
    n_slab = len(w_ffi)
    sw = D_FF // FF_CHUNKS // n_slab

    def body(x2_ref, dy_ref, df_ref, h_ref, g3_ref, *rest):
        wi_refs, (wo_ref, dx_ref, act_ref, da_ref, dg_ref) = rest[:n_slab], rest[n_slab:]

        @pl.when(pl.program_id(0) == 0)
        def _():
            dg_ref[...] = jnp.zeros_like(dg_ref)

        hb = h_ref[...]
        dfb = df_ref[...]
        dh = jnp.zeros((tb, D_MODEL), F32)
        for j in range(FF_CHUNKS):
            for k in range(n_slab):
                cols = pl.ds((j * n_slab + k) * sw, sw)
                ra = jnp.maximum(_mm(hb, wi_refs[k][j]), 0.0)
                act_ref[:, cols] = _bf(ra * ra)
                dab = _bf(_mm_nt(dfb, wo_ref[j, pl.ds(k * sw, sw), :]) * (2.0 * ra))
                da_ref[:, cols] = dab
                dh = dh + _mm_nt(dab, wi_refs[k][j])
        g3 = g3_ref[...]
        _, xh, r3 = _rms(x2_ref[...], g3)
        dxn, dg = _rms_bwd(dh, xh, r3, g3)
        dx_ref[...] = dy_ref[...] + dxn
        dg_ref[...] += dg

    return _rowcall("mlp_bwd", body, seq, tb, [x2, dy, df, h3], [g3, *w_ffi, w_ffo],
                    [(D_MODEL, F32), (D_FF, BF16), (D_FF, BF16)], [((1, D_MODEL), F32)], vmem=VMEM_BIG)


def _matmul_tn(name, a, b, tk, tn, tl, chunk_major, exchange=None):
    seq, kdim = a.shape
    ndim = b.shape[1]
    last = seq // tl - 1

    def body(a_ref, b_ref, o_ref, ob_ref):
        @pl.when(pl.program_id(2) == 0)
        def _():
            o_ref[...] = jnp.zeros_like(o_ref)

        o_ref[...] += _mm_tn(a_ref[...], b_ref[...])

        @pl.when(pl.program_id(2) == last)
        def _():
            ob_ref[...] = _bf(o_ref[...])

    if chunk_major:
        shape = (ndim // tn, kdim, tn)
        out_spec = pl.BlockSpec((None, tk, tn), lambda k, n, l: (n, k, 0))
    else:
        shape = (kdim, ndim)
        out_spec = pl.BlockSpec((tk, tn), lambda k, n, l: (k, n))
    return _fused_call(
        name, body, (kdim // tk, ndim // tn, seq // tl),
        [pl.BlockSpec((tl, tk), lambda k, n, l: (l, k)), pl.BlockSpec((tl, tn), lambda k, n, l: (l, n))],
        [out_spec, out_spec], [SDS(shape, F32), SDS(shape, BF16)], [], [a, b], exchange, _params(3, VMEM_BIG))


def _ew_call(name, fn, ins, n_out, after=None):
    rows, cols = ins[0].shape
    tr = rows
    while tr * cols * 4 > min(1 << 20, (9 << 20) // (len(ins) + n_out)) and tr % 16 == 0:
        tr //= 2
    spec = pl.BlockSpec((tr, cols), lambda i: (i, 0))
    extra = [] if after is None else [after]

    def body(*refs):
        outs = fn(*[r[...] for r in refs[:len(ins)]])
        for r, o in zip(refs[len(ins) + len(extra):], outs):
            r[...] = o

    return pl.pallas_call(
        body, grid=(rows // tr,), in_specs=[spec] * len(ins) + [ANY] * len(extra), out_specs=[spec] * n_out,
        out_shape=[SDS((rows, cols), F32)] * n_out, name=name, compiler_params=_params(1))(*ins, *extra)


def _adam_math(w, g, m, v):
    m2 = ADAM_B1 * m + (1.0 - ADAM_B1) * g
    v2 = ADAM_B2 * v + (1.0 - ADAM_B2) * (g * g)
    m_hat = m2 / (1.0 - ADAM_B1 ** ADAM_STEP)
    v_hat = v2 / (1.0 - ADAM_B2 ** ADAM_STEP)
    delta = -ADAM_LR * (m_hat / (jnp.sqrt(v_hat) + ADAM_EPS) + ADAM_WD * w)
    return delta, m2, v2


def _sum4(name, own, recv, idx):
    _, rows, cols = own.shape
    tr = rows
    while tr * cols * 4 > (1 << 20) and tr % 16 == 0:
        tr //= 2

    def body(idx_ref, o_ref, r0_ref, r1_ref, r2_ref, out_ref):
        out_ref[...] = ((o_ref[...] + r0_ref[...].astype(F32)) + r1_ref[...].astype(F32)) + r2_ref[...].astype(F32)

    blk = (None, tr, cols)
    grid_spec = pltpu.PrefetchScalarGridSpec(
        num_scalar_prefetch=1, grid=(rows // tr,),
        in_specs=[pl.BlockSpec(blk, lambda i, s: (s[0], i, 0)), pl.BlockSpec(blk, lambda i, s: (0, i, 0)),
                  pl.BlockSpec(blk, lambda i, s: (1, i, 0)), pl.BlockSpec(blk, lambda i, s: (2, i, 0))],
        out_specs=pl.BlockSpec((tr, cols), lambda i, s: (i, 0)))
    return pl.pallas_call(body, grid_spec=grid_spec, out_shape=SDS((rows, cols), F32), name=name,
                          compiler_params=_params(1))(jnp.reshape(idx, (1,)).astype(jnp.int32), own, recv, recv, recv)


def _adam_pair(name, item, after=None):
    def fn(w_, a, b, m_, v_):
        g = a + b
        return (g,) + _adam_math(w_, g, m_, v_)

    return _ew_call(name, fn, list(item), 4, after)


def _place():
    return lax.axis_index("x"), lax.axis_index("y"), lax.axis_index("c")


def _other_chips(x, y):
    return [(1 - x, y), (x, 1 - y), (1 - x, 1 - y)]


def _gather_chips(shards):
    n = len(shards)

    def copies(ins, outs, sems):
        send, recv, fwd_send, fwd_recv, loc = sems
        x, y, c = _place()
        me = 2 * x + y
        peers = _other_chips(x, y)
        local = [pltpu.make_async_copy(ins[a], outs[a].at[me], loc.at[a]) for a in range(n)]
        sends, recvs, passes, passed = [], [], [], []
        for a in range(n):
            half = shards[a].shape[0] // 2
            mine = pl.ds(c * half, half)
            theirs = pl.ds((1 - c) * half, half)
            for j, (px, py) in enumerate(peers):
                far = 2 * px + py
                sends.append(pltpu.make_async_remote_copy(
                    src_ref=ins[a].at[mine], dst_ref=outs[a].at[me, mine], send_sem=send.at[a, j],
                    recv_sem=recv.at[a, j], device_id=(px, py, c), device_id_type=MESH_ID))
                recvs.append(pltpu.make_async_remote_copy(
                    src_ref=ins[a].at[mine], dst_ref=outs[a].at[far, mine], send_sem=send.at[a, j],
                    recv_sem=recv.at[a, j], device_id=(px, py, c), device_id_type=MESH_ID))
                passes.append(pltpu.make_async_remote_copy(
                    src_ref=outs[a].at[far, mine], dst_ref=outs[a].at[far, mine], send_sem=fwd_send.at[a, j],
                    recv_sem=fwd_recv.at[a, j], device_id=(x, y, 1 - c), device_id_type=MESH_ID))
                passed.append(pltpu.make_async_remote_copy(
                    src_ref=outs[a].at[far, theirs], dst_ref=outs[a].at[far, theirs], send_sem=fwd_send.at[a, j],
                    recv_sem=fwd_recv.at[a, j], device_id=(x, y, 1 - c), device_id_type=MESH_ID))
        return local, sends, recvs, passes, passed

    def start(ins, outs, sems):
        local, sends, _, _, _ = copies(ins, outs, sems)
        for cp in local + sends:
            cp.start()

    def wait(ins, outs, sems):
        local, sends, recvs, passes, passed = copies(ins, outs, sems)
        for got, on in zip(recvs, passes):
            got.wait_recv()
            on.start()
        for cp in passed:
            cp.wait_recv()
        for cp in passes + sends:
            cp.wait_send()
        for cp in local:
            cp.wait()

    assert all(s.shape[0] % 32 == 0 for s in shards)
    pair = pltpu.SemaphoreType.DMA((n, 3))
    return _Exchange(shards, [SDS((N_CHIPS,) + s.shape, s.dtype) for s in shards],
                     [pair, pair, pair, pair, pltpu.SemaphoreType.DMA((n,))], start, wait)


def _scatter_chips(chunks):
    n = len(chunks)

    def copies(ins, outs, sems):
        send, recv = sems
        x, y, c = _place()
        return [pltpu.make_async_remote_copy(
            src_ref=ins[a].at[2 * px + py], dst_ref=outs[a].at[j], send_sem=send.at[a, j],
            recv_sem=recv.at[a, j], device_id=(px, py, c), device_id_type=MESH_ID)
            for a in range(n) for j, (px, py) in enumerate(_other_chips(x, y))]

    def start(ins, outs, sems):
        for cp in copies(ins, outs, sems):
            cp.start()

    def wait(ins, outs, sems):
        cps = copies(ins, outs, sems)
        for cp in cps:
            cp.wait_recv()
        for cp in cps:
            cp.wait_send()

    return _Exchange(chunks, [SDS((3,) + s.shape[1:], s.dtype) for s in chunks],
                     [pltpu.SemaphoreType.DMA((n, 3)), pltpu.SemaphoreType.DMA((n, 3))], start, wait)


HBM = pl.BlockSpec(memory_space=pltpu.HBM)
SEM = pl.BlockSpec(memory_space=pltpu.SEMAPHORE)
DATAFLOW = pltpu.SideEffectType.DATAFLOW_SIDE_EFFECTING


def _chunk_copies(src_ref, land_ref, send_sems, recv_sems):
    x, y, c = _place()
    return [pltpu.make_async_remote_copy(
        src_ref=src_ref.at[2 * px + py], dst_ref=land_ref.at[k], send_sem=send_sems.at[k], recv_sem=recv_sems.at[k],
        device_id=(px, py, c), device_id_type=MESH_ID) for k, (px, py) in enumerate(_other_chips(x, y))]


def _scatter_start(name, chunks):
    def body(src_ref, land_ref, send_sems, recv_sems, src_thru, land_thru, token):
        for cp in _chunk_copies(src_ref, land_ref, send_sems, recv_sems):
            cp.start()
        token[...] = jnp.zeros_like(token)

    land = (3,) + chunks.shape[1:]
    return pl.pallas_call(
        body, name=name,
        out_shape=(pltpu.SemaphoreType.DMA((3,)), pltpu.SemaphoreType.DMA((3,)), pltpu.HBM(chunks.shape, chunks.dtype),
                   pltpu.HBM(land, chunks.dtype), SDS((SUBLANES, LANES), F32)),
        in_specs=(HBM, HBM), out_specs=(SEM, SEM, HBM, HBM, pl.BlockSpec(memory_space=pltpu.VMEM)),
        input_output_aliases={0: 2, 1: 3}, compiler_params=pltpu.CompilerParams(has_side_effects=DATAFLOW),
    )(pltpu.with_memory_space_constraint(chunks, pltpu.HBM),
      pltpu.with_memory_space_constraint(lax.empty(land, chunks.dtype), pltpu.HBM))


def _scatter_wait(name, send_sems, recv_sems, src_thru, land_thru, after):
    def body(src_ref, land_ref, send_sems, recv_sems, after_ref, src_dead, got_ref):
        for cp in _chunk_copies(src_ref, land_ref, send_sems, recv_sems):
            cp.wait_send()
            cp.wait_recv()

    return pl.pallas_call(
        body, name=name,
        out_shape=(pltpu.HBM(src_thru.shape, src_thru.dtype), pltpu.HBM(land_thru.shape, land_thru.dtype)),
        in_specs=(HBM, HBM, SEM, SEM, ANY), out_specs=(HBM, HBM), input_output_aliases={0: 0, 1: 1},
        compiler_params=pltpu.CompilerParams(has_side_effects=DATAFLOW),
    )(src_thru, land_thru, send_sems, recv_sems, after)[1]


def _half_rows(shape, c, other=False):
    half = shape[0] // 2
    return pl.ds(((1 - c) if other else c) * half, half)


def _gather_start(name, shards, lands, after):
    n = len(shards)

    def body(*refs):
        src, land, (send, recv) = refs[:n], refs[n:2 * n], refs[2 * n + 1:2 * n + 3]
        x, y, c = _place()
        me = 2 * x + y
        for a in range(n):
            mine = _half_rows(shards[a].shape, c)
            for j, (px, py) in enumerate(_other_chips(x, y)):
                pltpu.make_async_remote_copy(
                    src_ref=src[a].at[mine], dst_ref=land[a].at[me, mine], send_sem=send.at[3 * a + j],
                    recv_sem=recv.at[3 * a + j], device_id=(px, py, c), device_id_type=MESH_ID).start()
        token = refs[-1]
        token[...] = jnp.zeros_like(token)

    mem = lambda t: pltpu.HBM(t.shape, t.dtype)
    pair = pltpu.SemaphoreType.DMA((3 * n,))
    outs = pl.pallas_call(
        body, name=name,
        out_shape=(pair, pair, *map(mem, shards), *map(mem, lands), SDS((SUBLANES, LANES), F32)),
        in_specs=[HBM] * (2 * n) + [ANY],
        out_specs=(SEM, SEM, *[HBM] * (2 * n), pl.BlockSpec(memory_space=pltpu.VMEM)),
        input_output_aliases={i: 2 + i for i in range(2 * n)},
        compiler_params=pltpu.CompilerParams(has_side_effects=DATAFLOW),
    )(*[pltpu.with_memory_space_constraint(t, pltpu.HBM) for t in (*shards, *lands)], after)
    return outs[0], outs[1], list(outs[2:2 + n]), list(outs[2 + n:2 + 2 * n]), outs[-1]


def _gather_pass(name, send, recv, shards, lands, after):
    n = len(shards)

    def body(*refs):
        src, land, (send, recv, _) = refs[:n], refs[n:2 * n], refs[2 * n:2 * n + 3]
        fsend, frecv = refs[2 * n + 3], refs[2 * n + 4]
        x, y, c = _place()
        me = 2 * x + y
        for a in range(n):
            mine = _half_rows(shards[a].shape, c)
            for j, (px, py) in enumerate(_other_chips(x, y)):
                far = 2 * px + py
                ici = pltpu.make_async_remote_copy(
                    src_ref=src[a].at[mine], dst_ref=land[a].at[far, mine], send_sem=send.at[3 * a + j],
                    recv_sem=recv.at[3 * a + j], device_id=(px, py, c), device_id_type=MESH_ID)
                ici.wait_recv()
                ici.wait_send()
                pltpu.make_async_remote_copy(
                    src_ref=land[a].at[far, mine], dst_ref=land[a].at[far, mine], send_sem=fsend.at[3 * a + j],
                    recv_sem=frecv.at[3 * a + j], device_id=(x, y, 1 - c), device_id_type=MESH_ID).start()
        token = refs[-1]
        token[...] = jnp.zeros_like(token)

    mem = lambda t: pltpu.HBM(t.shape, t.dtype)
    pair = pltpu.SemaphoreType.DMA((3 * n,))
    outs = pl.pallas_call(
        body, name=name,
        out_shape=(pair, pair, *map(mem, lands), SDS((SUBLANES, LANES), F32)),
        in_specs=[HBM] * (2 * n) + [SEM, SEM, ANY],
        out_specs=(SEM, SEM, *[HBM] * n, pl.BlockSpec(memory_space=pltpu.VMEM)),
        input_output_aliases={n + i: 2 + i for i in range(n)},
        compiler_params=pltpu.CompilerParams(has_side_effects=DATAFLOW),
    )(*shards, *lands, send, recv, after)
    return outs[0], outs[1], list(outs[2:2 + n]), outs[-1]


def _gather_wait(name, fsend, frecv, lands, after):
    n = len(lands)

    def body(*refs):
        land, (fsend, frecv, _) = refs[:n], refs[n:n + 3]
        x, y, c = _place()
        for a in range(n):
            for j, (px, py) in enumerate(_other_chips(x, y)):
                far = 2 * px + py
                mine = _half_rows(lands[a].shape[1:], c)
                theirs = _half_rows(lands[a].shape[1:], c, other=True)
                pltpu.make_async_remote_copy(
                    src_ref=land[a].at[far, mine], dst_ref=land[a].at[far, mine], send_sem=fsend.at[3 * a + j],
                    recv_sem=frecv.at[3 * a + j], device_id=(x, y, 1 - c), device_id_type=MESH_ID).wait_send()
                pltpu.make_async_remote_copy(
                    src_ref=land[a].at[far, theirs], dst_ref=land[a].at[far, theirs], send_sem=fsend.at[3 * a + j],
                    recv_sem=frecv.at[3 * a + j], device_id=(x, y, 1 - c), device_id_type=MESH_ID).wait_recv()

    mem = lambda t: pltpu.HBM(t.shape, t.dtype)
    return list(pl.pallas_call(
        body, name=name, out_shape=tuple(map(mem, lands)), in_specs=[HBM] * n + [SEM, SEM, ANY],
        out_specs=tuple([HBM] * n), input_output_aliases={i: i for i in range(n)},
        compiler_params=pltpu.CompilerParams(has_side_effects=DATAFLOW),
    )(*lands, fsend, frecv, after))


def _after(token):
    return _Exchange([token], [], [], lambda *_: None, lambda *_: None)


def _swap_sibling(arrs):
    n = len(arrs)

    def copies(ins, outs, sems):
        send, recv = sems
        x, y, c = _place()
        return [pltpu.make_async_remote_copy(
            src_ref=ins[a], dst_ref=outs[a], send_sem=send.at[a], recv_sem=recv.at[a],
            device_id=(x, y, 1 - c), device_id_type=MESH_ID) for a in range(n)]

    def start(ins, outs, sems):
        for cp in copies(ins, outs, sems):
            cp.start()

    def wait(ins, outs, sems):
        cps = copies(ins, outs, sems)
        for cp in cps:
            cp.wait_recv()
        for cp in cps:
            cp.wait_send()

    return _Exchange(arrs, [SDS(s.shape, s.dtype) for s in arrs],
                     [pltpu.SemaphoreType.DMA((n,)), pltpu.SemaphoreType.DMA((n,))], start, wait)


N_DEV = 8


def _gather_devices(block):
    def copies(ins, outs, sems):
        send, recv, loc = sems
        x, y, c = _place()
        me = 4 * x + 2 * y + c
        local = pltpu.make_async_copy(ins[0], outs[0].at[me], loc.at[0])
        sends, recvs = [], []
        for k in range(1, N_DEV):
            peer = (x ^ (k >> 2), y ^ ((k >> 1) & 1), c ^ (k & 1))
            for group, slot in ((sends, me), (recvs, me ^ k)):
                group.append(pltpu.make_async_remote_copy(
                    src_ref=ins[0], dst_ref=outs[0].at[slot], send_sem=send.at[k - 1], recv_sem=recv.at[k - 1],
                    device_id=peer, device_id_type=MESH_ID))
        return local, sends, recvs

    def start(ins, outs, sems):
        local, sends, _ = copies(ins, outs, sems)
        for cp in [local] + sends:
            cp.start()

    def wait(ins, outs, sems):
        local, sends, recvs = copies(ins, outs, sems)
        for cp in recvs:
            cp.wait_recv()
        for cp in sends:
            cp.wait_send()
        local.wait()

    return _Exchange([block], [SDS((N_DEV,) + block.shape, block.dtype)],
                     [pltpu.SemaphoreType.DMA((N_DEV - 1,)), pltpu.SemaphoreType.DMA((N_DEV - 1,)),
                      pltpu.SemaphoreType.DMA((1,))], start, wait)


def _both(ex_a, ex_b):
    na_i, na_o, na_s = len(ex_a.ins), len(ex_a.outs), len(ex_a.sems)

    def start(ins, outs, sems):
        ex_a.start(ins[:na_i], outs[:na_o], sems[:na_s])
        ex_b.start(ins[na_i:], outs[na_o:], sems[na_s:])

    def wait(ins, outs, sems):
        ex_a.wait(ins[:na_i], outs[:na_o], sems[:na_s])
        ex_b.wait(ins[na_i:], outs[na_o:], sems[na_s:])

    return _Exchange(ex_a.ins + ex_b.ins, ex_a.outs + ex_b.outs, ex_a.sems + ex_b.sems, start, wait)


def _sum_devices(slots):
    def body(s_ref, o_ref):
        acc = s_ref[0]
        for d in range(1, N_DEV):
            acc = acc + s_ref[d]
        o_ref[...] = acc

    return pl.pallas_call(
        body, in_specs=[pl.BlockSpec(memory_space=pltpu.VMEM)], out_specs=pl.BlockSpec(memory_space=pltpu.VMEM),
        out_shape=SDS(slots.shape[1:], F32), name="sum_small",
        compiler_params=pltpu.CompilerParams(vmem_limit_bytes=32 * 1024 * 1024))(slots)


def _adam_small(ws, gs, ms, vs):
    n = len(ws)

    def body(*refs):
        for i in range(n):
            w_ref, g_ref, m_ref, v_ref = (refs[k * n + i] for k in range(4))
            outs = _adam_math(w_ref[...], g_ref[...], m_ref[...], v_ref[...])
            for k in range(3):
                refs[(4 + k) * n + i][...] = outs[k]

    vmem = pl.BlockSpec(memory_space=pltpu.VMEM)
    return pl.pallas_call(
        body, in_specs=[vmem] * (4 * n), out_specs=[vmem] * (3 * n),
        out_shape=[SDS(w.shape, F32) for w in ws] * 3, name="adam_small",
        compiler_params=pltpu.CompilerParams(vmem_limit_bytes=32 * 1024 * 1024))(*ws, *gs, *ms, *vs)


def _local_step(x, target, small, big, tb, distributed):
    g1, g2, g3, g4 = small["norm_mix_pre"], small["norm_mix_post"], small["norm_mlp_pre"], small["norm_mlp_post"]
    dist = distributed
    me = (2 * lax.axis_index("x") + lax.axis_index("y")) if dist else 0
    tb_ssm = min(tb, 256)
    bucket = jnp.asarray(_bucket_table())

    keys_first = lambda t: jnp.swapaxes(t, -1, -2)
    bias = _bias_table(small["rel_bias"], bucket)
    sink_rows = keys_first(_pair_layout(jnp.broadcast_to(small["sinks"].reshape(N_HEADS, 1, 1), (N_HEADS, BLOCK, 1))))
    disc_args = (small["lam_re"], small["lam_im"], small["log_dt"], small["b_re"], small["b_im"])
    (ab_re, ab_im, bb_re, bb_im), disc_vjp = jax.vjp(_ssm_discretize, *disc_args)
    tab_f, tab_b = _scan_tables(ab_re, ab_im)
    bmat = _bf(_b_matrix(bb_re, bb_im))
    cmat = _bf(_c_matrix(small["c_re"], small["c_im"]))
    d_skip = small["d_skip"]

    if dist:
        (g_in,) = _exchange_alone("gather_w_in", _gather_chips([big["w_in"]]))
        w_in = g_in.reshape(IN_W, D_MODEL)
    else:
        w_in = big["w_in"]
    mix = ("w_glu", "w_attn_branch", "w_ssm_branch", "w_out")
    rest = [big[n] for n in mix + ("w_ff_in", "w_ff_out")]
    token = None
    if dist:
        lands = [lax.dynamic_update_index_in_dim(lax.empty((N_CHIPS,) + t.shape, t.dtype), t, me, 0) for t in rest]
        send, recv, rest, lands, token = _gather_start("gather_rest_start", rest, lands, g_in)
    h1, q, k, v, u, ga, gs = _inproj_fwd(x, g1, w_in, tb, _after(token) if dist else None)
    s, h = _ssm_fwd(u, bmat, cmat, tab_f, d_skip, tb_ssm)
    if dist:
        send, recv, lands, token = _gather_pass("gather_rest_pass", send, recv, rest, lands, s)
    att = _attn_fwd(q, k, v, bias, sink_rows, _after(token) if dist else None)[0]
    if dist:
        rest = _gather_wait("gather_rest_wait", send, recv, lands, att)
    w_glu, w_ab, w_sb, w_out, w_ffi, w_ffo = rest
    w_glu = w_glu.reshape(SSM_W, SSM_W)
    w_out = w_out.reshape(D_MODEL, D_MODEL)
    w_ffi = [w_ffi]
    x2 = _merge_fwd(x, s, att, ga, gs, g2, w_glu, w_ab, w_sb, w_out, tb)
    dy, df, h3, loss_acc, dg4 = _mlp_fwd_loss(x2, target, g3, g4, w_ffi, w_ffo, tb)

    dx2, act, da, dg3 = _mlp_bwd(x2, dy, df, h3, g3, w_ffi, w_ffo, tb)
    tl = min(2048, x.shape[0])
    chunked = (N_CHIPS, D_FF // N_CHIPS, D_MODEL)
    d_ffi, b_ffi = _matmul_tn("grad_w_ff_in", h3, da, D_MODEL, D_FF // FF_CHUNKS, tl, True)
    d_ffo, b_ffo = _matmul_tn("grad_w_ff_out", act, df, D_FF // FF_CHUNKS, D_MODEL, tl, False)
    d_ffo, b_ffo = d_ffo.reshape(chunked), b_ffo.reshape(chunked)
    outs = _merge_bwd(dx2, s, att, ga, gs, g2, w_glu, w_ab, w_sb, w_out, tb_ssm,
                      _scatter_chips([b_ffi]) if dist else None)
    ds, datt, dga, dgs, dg2, d_glu, d_ab, d_sb, d_out, b_glu, b_ab, b_sb, b_out = outs[:13]
    r_ffi = outs[13:]
    glu4, out4 = (N_CHIPS, SSM_W // N_CHIPS, SSM_W), (N_CHIPS, D_MODEL // N_CHIPS, D_MODEL)
    d_mix = [d_glu.reshape(glu4), d_ab, d_sb, d_out.reshape(out4)]
    b_mix = [b_glu.reshape(glu4), b_ab, b_sb, b_out.reshape(out4)]
    outs = _ssm_bwd(ds, u, h, bmat.transpose(0, 2, 1), cmat.transpose(0, 2, 1), tab_b, d_skip, tb_ssm,
                    _scatter_chips([b_ffo]) if dist else None)
    du, d_bmat, d_cmat, da_acc, dd_skip = outs[:5]
    r_ffo = outs[5:]
    outs = _attn_bwd(q, k, v, datt, bias, sink_rows, _scatter_chips(b_mix) if dist else None)
    dq, dk, dv, dbias, dsink_rows = outs[:5]
    r_mix = outs[5:]
    if dist:
        p_ffi = _sum4("sum_w_ff_in", d_ffi, r_ffi[0], me)
        p_ffo = _sum4("sum_w_ff_out", d_ffo, r_ffo[0], me)
    dx, dpj, dg1 = _inproj_bwd(x, dx2, dq, dk, dv, du, dga, dgs, g1, w_in, tb)

    dab_re, dab_im = _state_unlayout(jnp.sum(da_acc, axis=0))
    dbb_re, dbb_im = _b_matrix_grad(d_bmat)
    d_lam_re, d_lam_im, d_log_dt, d_b_re, d_b_im = disc_vjp((dab_re, dab_im, dbb_re, dbb_im))
    d_c_re, d_c_im = _c_matrix_grad(d_cmat)
    d_rel = _bias_grad(dbias, bucket)
    d_sinks = jnp.sum(_pair_unlayout(keys_first(dsink_rows)), axis=(1, 2))
    small_grads = dict(
        norm_mix_pre=dg1, norm_mix_post=dg2, norm_mlp_pre=dg3, norm_mlp_post=dg4, rel_bias=d_rel, sinks=d_sinks,
        lam_re=d_lam_re, lam_im=d_lam_im, log_dt=d_log_dt, b_re=d_b_re, b_im=d_b_im, c_re=d_c_re, c_im=d_c_im,
        d_skip=dd_skip)
    ride = _both(_swap_sibling([p_ffi, p_ffo]), _gather_devices(_pack(small_grads, loss_acc))) if dist else None
    outs = _matmul_tn("grad_w_in", dpj, h1, IN_W // 2, D_MODEL, tl, False, ride)
    in4 = (N_CHIPS, IN_W // N_CHIPS, D_MODEL)
    d_in, b_in = outs[0].reshape(in4), outs[1].reshape(in4)
    if not dist:
        return loss_acc, dx, small_grads, dict(zip(BIG, [d_in] + d_mix + [d_ffi, d_ffo]))
    s_ffi, s_ffo, slots = outs[2:]
    p_mix = [_sum4("sum_" + n, d, r, me) for n, d, r in zip(mix, d_mix, r_mix)]
    pending = dict(d_in=d_in, b_in=b_in, p_mix=p_mix, w_ff_in=(p_ffi, s_ffi), w_ff_out=(p_ffo, s_ffo), me=me)
    return loss_acc, dx, _sum_devices(slots), pending


SMALL = ['norm_mix_pre', 'norm_mix_post', 'norm_mlp_pre', 'norm_mlp_post', 'rel_bias', 'sinks', 'lam_re', 'lam_im',
         'log_dt', 'b_re', 'b_im', 'c_re', 'c_im', 'd_skip']
BIG = ['w_in', 'w_glu', 'w_attn_branch', 'w_ssm_branch', 'w_out', 'w_ff_in', 'w_ff_out']
WEIGHTS = ['norm_mix_pre', 'norm_mix_post', 'norm_mlp_pre', 'norm_mlp_post', 'w_in', 'rel_bias', 'sinks', 'lam_re',
           'lam_im', 'log_dt', 'b_re', 'b_im', 'c_re', 'c_im', 'd_skip', 'w_glu', 'w_attn_branch', 'w_ssm_branch',
           'w_out', 'w_ff_in', 'w_ff_out']
PACK_COLS = 1024
PACK_ORDER = ['b_re', 'b_im', 'c_re', 'c_im', 'lam_re', 'lam_im', 'norm_mix_pre', 'norm_mix_post', 'norm_mlp_pre',
              'norm_mlp_post', 'rel_bias', 'sinks', 'log_dt', 'd_skip']


STATE_MINOR = ('b_re', 'b_im')
PACK_ROWS = 144
LOSS_ROW = 140


def _pack(named, loss_acc):
    parts = []
    for n in PACK_ORDER:
        a = jnp.swapaxes(named[n], -1, -2) if n in STATE_MINOR else named[n]
        flat = a.reshape(-1)
        rows = -(-flat.shape[0] // PACK_COLS)
        parts.append(jnp.pad(flat, (0, rows * PACK_COLS - flat.shape[0])).reshape(rows, PACK_COLS))
    assert sum(p.shape[0] for p in parts) == LOSS_ROW
    parts.append(jnp.pad(loss_acc[0:1], ((0, PACK_ROWS - LOSS_ROW - 1), (0, PACK_COLS - loss_acc.shape[1]))))
    return jnp.concatenate(parts, axis=0)


def _unpack(packed, shapes):
    out, at = {}, 0
    for n in PACK_ORDER:
        shape = shapes[n][:-2] + (shapes[n][-1], shapes[n][-2]) if n in STATE_MINOR else shapes[n]
        size = int(np.prod(shape))
        rows = -(-size // PACK_COLS)
        blk = packed[at:at + rows]
        out[n] = (blk.reshape(-1)[:size] if size % PACK_COLS else blk).reshape(shape)
        at += rows
    return out


def kernel(x, norm_mix_pre, norm_mix_post, norm_mlp_pre, norm_mlp_post, w_in, rel_bias, sinks, lam_re, lam_im, log_dt, b_re, b_im, c_re, c_im, d_skip, w_glu, w_attn_branch, w_ssm_branch, w_out, w_ff_in, w_ff_out, loss_target, m_norm_mix_pre, m_norm_mix_post, m_norm_mlp_pre, m_norm_mlp_post, m_w_in, m_rel_bias, m_sinks, m_lam_re, m_lam_im, m_log_dt, m_b_re, m_b_im, m_c_re, m_c_im, m_d_skip, m_w_glu, m_w_attn_branch, m_w_ssm_branch, m_w_out, m_w_ff_in, m_w_ff_out, v_norm_mix_pre, v_norm_mix_post, v_norm_mlp_pre, v_norm_mlp_post, v_w_in, v_rel_bias, v_sinks, v_lam_re, v_lam_im, v_log_dt, v_b_re, v_b_im, v_c_re, v_c_im, v_d_skip, v_w_glu, v_w_attn_branch, v_w_ssm_branch, v_w_out, v_w_ff_in, v_w_ff_out):
    env = dict(locals())
    w = {n: env[n] for n in WEIGHTS}
    m = {n: env["m_" + n] for n in WEIGHTS}
    v = {n: env["v_" + n] for n in WEIGHTS}
    seq = x.shape[1]
    tb = min(512, seq)

    small = {n: w[n] for n in ('norm_mix_pre', 'norm_mix_post', 'norm_mlp_pre', 'norm_mlp_post', 'rel_bias')}
    small.update({n: w[n][0] for n in ('sinks', 'lam_re', 'lam_im', 'log_dt', 'b_re', 'b_im', 'c_re', 'c_im')})
    small['d_skip'] = w['d_skip']
    shard = lambda t, n: t[n][0].T if n == 'w_in' else t[n][0]
    unshard = lambda a, n: (a.T if n == 'w_in' else a)[None]
    _, dx, small_g, pending = _local_step(
        x[0], loss_target[0], small, {n: _bf(shard(w, n)) for n in BIG}, tb, True)

    loss = small_g[LOSS_ROW, 0]

    grads, deltas, new_m, new_v = {}, {}, {}, {}

    def adam(n, partials, after=None):
        outs = _adam_pair("adam_" + n, (shard(w, n), *partials, shard(m, n), shard(v, n)), after)
        grads[n], deltas[n], new_m[n], new_v[n] = [unshard(a, n) for a in outs]
        return outs[3]

    mix = ("w_glu", "w_attn_branch", "w_ssm_branch", "w_out")
    *in_flight, token = _scatter_start("scatter_w_in_start", pending["b_in"])
    sib_mix = _exchange_alone("swap_mix", _swap_sibling(pending["p_mix"]))
    last = None
    for n, partials in [(n, pending[n]) for n in ("w_ff_in", "w_ff_out")] + list(zip(mix, zip(pending["p_mix"], sib_mix))):
        last = adam(n, partials, token)
    r_in = _scatter_wait("scatter_w_in_wait", *in_flight, last)
    p_in = _sum4("sum_w_in", pending["d_in"], r_in, pending["me"])
    (s_in,) = _exchange_alone("swap_w_in", _swap_sibling([p_in]))
    adam("w_in", (p_in, s_in))

    minor = lambda t, n: jnp.swapaxes(t, -1, -2) if n in STATE_MINOR else t
    g_small = _unpack(small_g, {n: w[n].shape for n in SMALL})
    outs = _adam_small([minor(w[n], n) for n in SMALL], [g_small[n] for n in SMALL],
                       [minor(m[n], n) for n in SMALL], [minor(v[n], n) for n in SMALL])
    grads.update({n: minor(g_small[n], n) for n in SMALL})
    for k, dst in enumerate((deltas, new_m, new_v)):
        dst.update({n: minor(a, n) for n, a in zip(SMALL, outs[k * len(SMALL):(k + 1) * len(SMALL)])})

    return (loss, dx[None], *[grads[n] for n in WEIGHTS], *[deltas[n] for n in WEIGHTS],
            *[new_m[n] for n in WEIGHTS], *[new_v[n] for n in WEIGHTS])
```

```python
import functools
import math

import numpy as np
import jax
import jax.numpy as jnp
from jax import lax
from jax.experimental import pallas as pl
from jax.experimental.pallas import tpu as pltpu

F32 = jnp.float32
BF16 = jnp.bfloat16

D_MODEL = 1024
N_HEADS = 8
N_KV = 2
Q_GROUP = 4
HEAD_DIM = 64
ATTN_W = 512
KV_W = 128
BLOCK = 128
N_BUCKETS = 32
MAX_DISTANCE = 128
NEG_INF = -1e30
SSM_W = 512
SSM_GROUP = 16
SSM_GROUPS = 32
SSM_STATE = 64
N_SUPER = 4
GROUPS_PER_SUPER = SSM_GROUPS // N_SUPER
SUPER_IN = GROUPS_PER_SUPER * SSM_GROUP
SUPER_HALF = GROUPS_PER_SUPER * SSM_STATE
SUPER_W = 2 * SUPER_HALF
STATE_COLS = N_SUPER * SUPER_W
D_FF = 4096
FF_CHUNKS = 4
IN_W = 3328
SPLITS = (0, 512, 640, 768, 1280, 2304, 3328)
RMS_EPS = 1e-6
N_CHIPS = 4
SUBLANES = 8
LANES = 128
STATE_TILES = STATE_COLS // LANES
SUPER_TILES = SUPER_W // LANES

ADAM_LR = 0.001
ADAM_B1 = 0.9
ADAM_B2 = 0.999
ADAM_EPS = 1e-08
ADAM_WD = 0.01
ADAM_STEP = 10

VMEM_BIG = 56 * 1024 * 1024
SDS = jax.ShapeDtypeStruct
MESH_ID = pl.DeviceIdType.MESH
ANY = pl.BlockSpec(memory_space=pl.ANY)


def _bf(x):
    return x.astype(BF16)


def _mm(a, b):
    return jnp.dot(a, b, preferred_element_type=F32)


def _mm_nt(a, b):
    return lax.dot_general(a, b, (((1,), (1,)), ((), ())), preferred_element_type=F32)


def _mm_tn(a, b):
    return lax.dot_general(a, b, (((0,), (0,)), ((), ())), preferred_element_type=F32)


def _sig(x):
    return 1.0 / (1.0 + jnp.exp(-x))


def _rms(x, g):
    r = lax.rsqrt(jnp.mean(x * x, axis=-1, keepdims=True) + RMS_EPS)
    xh = x * r
    return xh * g, xh, r


def _rms_bwd(dout, xh, r, g):
    dg = jnp.sum(dout * xh, axis=0, keepdims=True)
    dxh = dout * g
    dx = r * (dxh - xh * jnp.mean(dxh * xh, axis=-1, keepdims=True))
    return dx, dg


_GELU_C = math.sqrt(2.0 / math.pi)


def _gelu_and_grad(x):
    x2 = x * x
    inner = _GELU_C * (x + 0.044715 * (x2 * x))
    t = jnp.tanh(inner)
    y = 0.5 * x * (1.0 + t)
    dy = 0.5 * (1.0 + t) + 0.5 * x * (1.0 - t * t) * (_GELU_C * (1.0 + 3.0 * 0.044715 * x2))
    return y, dy


def _zero_map(nd, *_):
    return (0,) * nd


def _params(n_axes, vmem=None):
    return pltpu.CompilerParams(dimension_semantics=("arbitrary",) * n_axes, vmem_limit_bytes=vmem)


class _Exchange:
    def __init__(self, ins, outs, sems, start, wait):
        self.ins, self.outs, self.sems, self.start, self.wait = list(ins), list(outs), list(sems), start, wait


def _fused_call(name, body, grid, in_specs, out_specs, out_shape, scratch, args, exchange, params):
    n_in, n_out, n_scr = len(in_specs), len(out_specs), len(scratch)
    if exchange is None:
        fn = body
    else:
        ex = exchange
        n_xi, n_xo = len(ex.ins), len(ex.outs)

        def fn(*refs):
            at = 0
            parts = []
            for n in (n_in, n_xi, n_out, n_xo, n_scr, len(ex.sems)):
                parts.append(refs[at:at + n])
                at += n
            ins, x_in, outs, x_out, scr, x_sem = parts
            ids = [pl.program_id(a) for a in range(len(grid))]
            first = functools.reduce(jnp.logical_and, [i == 0 for i in ids])
            last = functools.reduce(jnp.logical_and, [i == g - 1 for i, g in zip(ids, grid)])

            @pl.when(first)
            def _():
                ex.start(x_in, x_out, x_sem)

            body(*ins, *outs, *scr)

            @pl.when(last)
            def _():
                ex.wait(x_in, x_out, x_sem)

        in_specs = list(in_specs) + [ANY] * n_xi
        out_specs = list(out_specs) + [ANY] * n_xo
        out_shape = list(out_shape) + ex.outs
        scratch = list(scratch) + ex.sems
        args = list(args) + ex.ins
    return pl.pallas_call(fn, grid=grid, in_specs=in_specs, out_specs=out_specs, out_shape=out_shape,
                          scratch_shapes=list(scratch), name=name, compiler_params=params)(*args)


def _exchange_alone(name, ex):
    def body(*refs):
        n_xi, n_xo = len(ex.ins), len(ex.outs)
        x_in, x_out, x_sem = refs[:n_xi], refs[n_xi:n_xi + n_xo], refs[n_xi + n_xo:]
        ex.start(x_in, x_out, x_sem)
        ex.wait(x_in, x_out, x_sem)

    return pl.pallas_call(body, in_specs=[ANY] * len(ex.ins), out_specs=[ANY] * len(ex.outs), out_shape=ex.outs,
                          scratch_shapes=ex.sems, name=name)(*ex.ins)


def _rowcall(name, body, seq, tb, rows, consts, row_outs, acc_outs, scratch=(), reverse=False, vmem=None,
             exchange=None):
    nb = seq // tb
    rmap = (lambda i: (nb - 1 - i, 0)) if reverse else (lambda i: (i, 0))
    tmap = lambda i: (0,) + rmap(i)

    def row_spec(width):
        if isinstance(width, tuple):
            return pl.BlockSpec((width[0], tb, width[1]), tmap)
        return pl.BlockSpec((tb, width), rmap)

    def row_shape(width):
        return (width[0], seq, width[1]) if isinstance(width, tuple) else (seq, width)

    in_specs = [row_spec(a.shape[1] if a.ndim == 2 else (a.shape[0], a.shape[2])) for a in rows]
    in_specs += [pl.BlockSpec(a.shape, functools.partial(_zero_map, a.ndim), pipeline_mode=pl.Buffered(1))
                 for a in consts]
    out_specs = [row_spec(c) for c, _ in row_outs] + [ANY] * len(acc_outs)
    out_shape = [SDS(row_shape(c), dt) for c, dt in row_outs] + [SDS(s, dt) for s, dt in acc_outs]
    n_main = len(rows) + len(consts) + len(row_outs)
    n_acc = len(acc_outs)

    def fn(*refs):
        main, acc_hbm, rest = refs[:n_main], refs[n_main:n_main + n_acc], refs[n_main + n_acc:]
        acc_vmem, own = rest[:n_acc], rest[n_acc:]
        body(*main, *acc_vmem, *own)

        @pl.when(pl.program_id(0) == nb - 1)
        def _():
            for src, dst in zip(acc_vmem, acc_hbm):
                pltpu.sync_copy(src, dst)

    buffers = [pltpu.VMEM(s, dt) for s, dt in acc_outs] + list(scratch)
    return _fused_call(name, fn if acc_outs else body, (nb,), in_specs, out_specs, out_shape, buffers,
                       [*rows, *consts], exchange, _params(1, vmem))


def _inproj_fwd(x, g1, w_in, tb, exchange=None):
    seq = x.shape[0]

    def body(x_ref, g_ref, w_ref, h_ref, q_ref, k_ref, v_ref, u_ref, ga_ref, gs_ref):
        h, _, _ = _rms(x_ref[...], g_ref[...])
        hb = _bf(h)
        h_ref[...] = hb
        pj = _mm_nt(hb, w_ref[...])
        q_ref[...] = _bf(pj[:, SPLITS[0]:SPLITS[1]])
        k_ref[...] = _bf(pj[:, SPLITS[1]:SPLITS[2]])
        v_ref[...] = _bf(pj[:, SPLITS[2]:SPLITS[3]])
        u_ref[...] = pj[:, SPLITS[3]:SPLITS[4]]
        ga_ref[...] = pj[:, SPLITS[4]:SPLITS[5]]
        gs_ref[...] = pj[:, SPLITS[5]:SPLITS[6]]

    return _rowcall("inproj_fwd", body, seq, tb, [x], [g1, w_in],
                    [(D_MODEL, BF16), (ATTN_W, BF16), (KV_W, BF16), (KV_W, BF16), (SSM_W, F32),
                     (D_MODEL, F32), (D_MODEL, F32)], [], vmem=VMEM_BIG, exchange=exchange)


def _inproj_bwd(x, dx2, dq, dk, dv, du, dga, dgs, g1, w_in, tb, exchange=None):
    seq = x.shape[0]

    def body(x_ref, dx2_ref, dq_ref, dk_ref, dv_ref, du_ref, dga_ref, dgs_ref, g_ref, w_ref,
             dx_ref, dpj_ref, dg_ref):
        @pl.when(pl.program_id(0) == 0)
        def _():
            dg_ref[...] = jnp.zeros_like(dg_ref)

        dpj = jnp.concatenate([dq_ref[...], dk_ref[...], dv_ref[...], _bf(du_ref[...]),
                               dga_ref[...], dgs_ref[...]], axis=1)
        dpj_ref[...] = dpj
        dh = _mm(dpj, w_ref[...])
        g = g_ref[...]
        _, xh, r = _rms(x_ref[...], g)
        dxn, dg = _rms_bwd(dh, xh, r, g)
        dx_ref[...] = dx2_ref[...] + dxn
        dg_ref[...] += dg

    return _rowcall("inproj_bwd", body, seq, tb, [x, dx2, dq, dk, dv, du, dga, dgs], [g1, w_in],
                    [(D_MODEL, F32), (IN_W, BF16)], [((1, D_MODEL), F32)], vmem=VMEM_BIG, exchange=exchange)


def _bucket_table():
    qi = np.arange(BLOCK)[:, None]
    kj = np.arange(2 * BLOCK)[None, :]
    dist = qi + BLOCK - kj
    max_exact = N_BUCKETS // 2
    d = np.maximum(dist, 0)
    df = np.maximum(d, 1).astype(np.float32)
    large = max_exact + (np.log(df / np.float32(max_exact)) / np.float32(math.log(MAX_DISTANCE / max_exact))
                         * np.float32(N_BUCKETS - max_exact)).astype(np.int32)
    large = np.minimum(large, N_BUCKETS - 1)
    bucket = np.where(d < max_exact, d, large)
    valid = (dist >= 0) & (dist < BLOCK)
    return np.where(valid, bucket, -1).astype(np.int32)


def _bias_table(rel_bias, bucket):
    def body(rb_ref, bk_ref, o_ref):
        bk = bk_ref[...]
        has_prev = lax.broadcasted_iota(jnp.int32, bk.shape, 1) >= BLOCK
        for h in range(N_HEADS):
            kh, j, par = h // Q_GROUP, (h // 2) % 2, h % 2
            acc = jnp.full((BLOCK, 2 * BLOCK), NEG_INF, F32)
            for b in range(N_BUCKETS):
                acc = jnp.where(bk == b, rb_ref[b, h], acc)
            o_ref[0, kh, par, :, j * BLOCK:(j + 1) * BLOCK] = jnp.where(has_prev, acc, NEG_INF).T
            o_ref[1, kh, par, :, j * BLOCK:(j + 1) * BLOCK] = acc.T

    return pl.pallas_call(
        body, out_shape=SDS((2, N_KV, 2, 2 * BLOCK, 2 * BLOCK), F32),
        in_specs=[pl.BlockSpec(memory_space=pltpu.SMEM), pl.BlockSpec(memory_space=pltpu.VMEM)],
        out_specs=pl.BlockSpec(memory_space=pltpu.VMEM), name="bias_table",
    )(rel_bias, bucket)


def _bias_grad(dbias, bucket):
    def body(db_ref, bk_ref, o_ref):
        bk = bk_ref[...]
        for h in range(N_HEADS):
            kh, j, par = h // Q_GROUP, (h // 2) % 2, h % 2
            db = db_ref[kh, par, :, j * BLOCK:(j + 1) * BLOCK].T
            for b in range(N_BUCKETS):
                o_ref[b, h] = jnp.sum(jnp.where(bk == b, db, 0.0))

    return pl.pallas_call(
        body, out_shape=SDS((N_BUCKETS, N_HEADS), F32),
        in_specs=[pl.BlockSpec(memory_space=pltpu.VMEM), pl.BlockSpec(memory_space=pltpu.VMEM)],
        out_specs=pl.BlockSpec(memory_space=pltpu.SMEM), name="bias_grad",
    )(dbias, bucket)


TILE = 2 * HEAD_DIM


def _pair_layout(t):
    lead = t.shape[:-3]
    t = t.reshape(lead + (N_KV, 2, 2) + t.shape[-2:])
    nl = len(lead)
    t = jnp.transpose(t, tuple(range(nl)) + (nl, nl + 2, nl + 1, nl + 3, nl + 4))
    return t.reshape(lead + (N_KV, 2, 2 * BLOCK, t.shape[-1]))


def _pair_unlayout(t):
    t = t.reshape(N_KV, 2, 2, BLOCK, t.shape[-1]).transpose(0, 2, 1, 3, 4)
    return t.reshape(N_HEADS, BLOCK, t.shape[-1])


def _halves(t):
    tf = t.astype(F32)
    low = lax.broadcasted_iota(jnp.int32, tf.shape, 1) < HEAD_DIM
    swapped = pltpu.roll(tf, HEAD_DIM, 1)
    zero = jnp.zeros_like(tf)
    return ((_bf(jnp.where(low, tf, zero)), _bf(jnp.where(low, zero, swapped))),
            (_bf(jnp.where(low, swapped, zero)), _bf(jnp.where(low, zero, tf))))


def _fold_halves(even, odd):
    low = lax.broadcasted_iota(jnp.int32, even.shape, 1) < HEAD_DIM
    comb = jnp.where(low, even, odd)
    return comb + pltpu.roll(comb, HEAD_DIM, 1)


def _tile_rows(ref, kh):
    return jnp.concatenate([ref[:, (2 * kh) * TILE:(2 * kh + 1) * TILE],
                            ref[:, (2 * kh + 1) * TILE:(2 * kh + 2) * TILE]], axis=0)


def _halves_t(t):
    tt = t.astype(F32).T
    top = lax.broadcasted_iota(jnp.int32, tt.shape, 0) < HEAD_DIM
    swapped = jnp.concatenate([tt[HEAD_DIM:], tt[:HEAD_DIM]], axis=0)
    zero = jnp.zeros_like(tt)
    return ((_bf(jnp.where(top, tt, zero)), _bf(jnp.where(top, zero, swapped))),
            (_bf(jnp.where(top, swapped, zero)), _bf(jnp.where(top, zero, tt))))


def _attn_probs(km, qk, bias, sink):
    lg = _mm_nt(km, qk) * (HEAD_DIM ** -0.5) + bias
    m = jnp.maximum(jnp.max(lg, axis=0, keepdims=True), sink)
    p = jnp.exp(lg - m)
    es = jnp.exp(sink - m)
    inv = 1.0 / (jnp.sum(p, axis=0, keepdims=True) + es)
    return p * inv, es * inv


def _attn_fwd(q, k, v, bias, sink_rows, exchange=None):
    seq = q.shape[0]
    nblk = seq // BLOCK

    def body(q_ref, kp_ref, kc_ref, vp_ref, vc_ref, b_ref, s_ref, o_ref):
        which = jnp.minimum(pl.program_id(0), 1)
        kms = _halves(jnp.concatenate([kp_ref[...], kc_ref[...]], axis=0))
        vts = _halves_t(jnp.concatenate([vp_ref[...], vc_ref[...]], axis=0))
        for kh in range(N_KV):
            qk = _tile_rows(q_ref, kh)
            acc = jnp.zeros((TILE, 2 * BLOCK), F32)
            for par in range(2):
                pr, _ = _attn_probs(kms[kh][par], qk, b_ref[which, kh, par], s_ref[kh, par])
                acc = acc + _mm(vts[kh][par], _bf(pr))
            acc = acc.T
            o_ref[:, (2 * kh) * TILE:(2 * kh + 1) * TILE] = _bf(acc[:BLOCK])
            o_ref[:, (2 * kh + 1) * TILE:(2 * kh + 2) * TILE] = _bf(acc[BLOCK:])

    cur = lambda n: (n, 0)
    prev = lambda n: (jnp.maximum(n - 1, 0), 0)
    return _fused_call(
        "attn_fwd", body, (nblk,),
        [pl.BlockSpec((BLOCK, ATTN_W), cur),
         pl.BlockSpec((BLOCK, KV_W), prev), pl.BlockSpec((BLOCK, KV_W), cur),
         pl.BlockSpec((BLOCK, KV_W), prev), pl.BlockSpec((BLOCK, KV_W), cur),
         pl.BlockSpec(bias.shape, functools.partial(_zero_map, bias.ndim)),
         pl.BlockSpec(sink_rows.shape, functools.partial(_zero_map, sink_rows.ndim))],
        [pl.BlockSpec((BLOCK, ATTN_W), cur)], [SDS((seq, ATTN_W), BF16)], [],
        [q, k, k, v, v, bias, sink_rows], exchange, _params(1))


def _attn_bwd(q, k, v, d_out, bias, sink_rows, exchange=None):
    seq = q.shape[0]
    nblk = seq // BLOCK

    def body(q_ref, kp_ref, kc_ref, vp_ref, vc_ref, do_ref, b_ref, s_ref,
             dq_ref, dk_ref, dv_ref, db_ref, ds_ref, ck_ref, cv_ref):
        n = pl.program_id(0)

        @pl.when(n == 0)
        def _():
            db_ref[...] = jnp.zeros_like(db_ref)
            ds_ref[...] = jnp.zeros_like(ds_ref)
            ck_ref[...] = jnp.zeros_like(ck_ref)
            cv_ref[...] = jnp.zeros_like(cv_ref)

        @pl.when(n < nblk)
        def _():
            which = jnp.minimum(n, 1)
            scale = HEAD_DIM ** -0.5
            kcat = jnp.concatenate([kp_ref[...], kc_ref[...]], axis=0)
            kms = _halves(kcat)
            kts = _halves_t(kcat)
            vms = _halves(jnp.concatenate([vp_ref[...], vc_ref[...]], axis=0))
            dks, dvs = [], []
            for kh in range(N_KV):
                qk = _tile_rows(q_ref, kh)
                dok = _tile_rows(do_ref, kh)
                dq = jnp.zeros((TILE, 2 * BLOCK), F32)
                dkp, dvp = [], []
                for par in range(2):
                    pr, ps = _attn_probs(kms[kh][par], qk, b_ref[which, kh, par], s_ref[kh, par])
                    dp = _mm_nt(vms[kh][par], dok)
                    rs = jnp.sum(pr * dp, axis=0, keepdims=True)
                    dlg = pr * (dp - rs)
                    ds_ref[kh, par] += -ps * rs
                    db_ref[kh, par] += dlg
                    dlb = _bf(dlg)
                    dq = dq + _mm(kts[kh][par], dlb)
                    dkp.append(_mm(dlb, qk))
                    dvp.append(_mm(_bf(pr), dok))
                dq = _bf((dq * scale).T)
                dq_ref[:, (2 * kh) * TILE:(2 * kh + 1) * TILE] = dq[:BLOCK]
                dq_ref[:, (2 * kh + 1) * TILE:(2 * kh + 2) * TILE] = dq[BLOCK:]
                dks.append(_fold_halves(*dkp))
                dvs.append(_fold_halves(*dvp))
            low = lax.broadcasted_iota(jnp.int32, (2 * BLOCK, TILE), 1) < HEAD_DIM
            dkk = jnp.where(low, dks[0], dks[1]) * scale
            dvv = jnp.where(low, dvs[0], dvs[1])
            dk_ref[...] = _bf(ck_ref[...] + dkk[:BLOCK])
            ck_ref[...] = dkk[BLOCK:]
            dv_ref[...] = _bf(cv_ref[...] + dvv[:BLOCK])
            cv_ref[...] = dvv[BLOCK:]

        @pl.when(n == nblk)
        def _():
            dk_ref[...] = _bf(ck_ref[...])
            dv_ref[...] = _bf(cv_ref[...])

    cur = lambda n: (jnp.minimum(n, nblk - 1), 0)
    prev = lambda n: (jnp.maximum(jnp.minimum(n, nblk - 1) - 1, 0), 0)
    late = lambda n: (jnp.maximum(n - 1, 0), 0)
    kv_spec = lambda m: pl.BlockSpec((BLOCK, KV_W), m)
    acc_b = pl.BlockSpec(bias.shape[1:], functools.partial(_zero_map, bias.ndim - 1))
    acc_s = pl.BlockSpec(sink_rows.shape, functools.partial(_zero_map, sink_rows.ndim))
    return _fused_call(
        "attn_bwd", body, (nblk + 1,),
        [pl.BlockSpec((BLOCK, ATTN_W), cur), kv_spec(prev), kv_spec(cur), kv_spec(prev), kv_spec(cur),
         pl.BlockSpec((BLOCK, ATTN_W), cur),
         pl.BlockSpec(bias.shape, functools.partial(_zero_map, bias.ndim)), acc_s],
        [pl.BlockSpec((BLOCK, ATTN_W), cur), kv_spec(late), kv_spec(late), acc_b, acc_s],
        [SDS((seq, ATTN_W), BF16), SDS((seq, KV_W), BF16), SDS((seq, KV_W), BF16),
         SDS(bias.shape[1:], F32), SDS(sink_rows.shape, F32)],
        [pltpu.VMEM((BLOCK, KV_W), F32), pltpu.VMEM((BLOCK, KV_W), F32)],
        [q, k, k, v, v, d_out, bias, sink_rows], exchange, _params(1))


def _ssm_discretize(lam_re, lam_im, log_dt, b_re, b_im):
    dt = jnp.exp(log_dt)[:, None]
    mag = jnp.exp(lam_re * dt)
    ab_re = mag * jnp.cos(lam_im * dt)
    ab_im = mag * jnp.sin(lam_im * dt)
    nr = ab_re - 1.0
    den = lam_re * lam_re + lam_im * lam_im
    f_re = (nr * lam_re + ab_im * lam_im) / den
    f_im = (ab_im * lam_re - nr * lam_im) / den
    bb_re = f_re[..., None] * b_re - f_im[..., None] * b_im
    bb_im = f_re[..., None] * b_im + f_im[..., None] * b_re
    return ab_re, ab_im, bb_re, bb_im


def _state_layout(re, im):
    z = jnp.stack([re, im]).reshape(2, N_SUPER, GROUPS_PER_SUPER, SSM_STATE)
    return z.transpose(1, 0, 2, 3).reshape(STATE_COLS)


def _state_unlayout(vec):
    z = vec.reshape(N_SUPER, 2, GROUPS_PER_SUPER, SSM_STATE).transpose(1, 0, 2, 3)
    z = z.reshape(2, SSM_GROUPS, SSM_STATE)
    return z[0], z[1]


SEG = 4
WINDOW = SEG * SUBLANES


def _scan_tables(ab_re, ab_im):
    pw = [None, (ab_re, ab_im)]
    for _ in range(2, WINDOW + 1):
        pr, pi_ = pw[-1]
        pw.append((pr * ab_re - pi_ * ab_im, pr * ab_im + pi_ * ab_re))
    rows = np.arange(SUBLANES)[:, None]
    ones = np.ones((SUBLANES, 1), np.float32)
    conj = lambda p: (p[0], -p[1])
    fwd, bwd = [], []
    for shift in (1, 2, 4):
        fwd.append(_state_layout(*pw[SEG * shift])[None, :] * (rows >= shift).astype(np.float32))
        bwd.append(_state_layout(*conj(pw[SEG * shift]))[None, :] * (rows < SUBLANES - shift).astype(np.float32))
    fwd.append(jnp.stack([_state_layout(*pw[SEG * (r + 1)]) for r in range(SUBLANES)]))
    bwd.append(jnp.stack([_state_layout(*conj(pw[SEG * (SUBLANES - r)])) for r in range(SUBLANES)]))
    for k in range(1, SEG):
        fwd.append(_state_layout(*pw[k])[None, :] * ones)
        bwd.append(_state_layout(*conj(pw[k]))[None, :] * ones)
    return jnp.stack(fwd), jnp.stack(bwd)


_EYE = np.eye(GROUPS_PER_SUPER, dtype=np.float32)


def _b_matrix(bb_re, bb_im):
    bb = jnp.stack([bb_re, bb_im]).reshape(2, N_SUPER, GROUPS_PER_SUPER, SSM_STATE, SSM_GROUP)
    m = jnp.einsum('rsgpc,gh->sgcrhp', bb, _EYE)
    return m.reshape(N_SUPER, SUPER_IN, SUPER_W)


def _b_matrix_grad(dm):
    d = dm.reshape(N_SUPER, GROUPS_PER_SUPER, SSM_GROUP, 2, GROUPS_PER_SUPER, SSM_STATE)
    d = jnp.sum(d * _EYE[None, :, None, None, :, None], axis=4)
    d = d.transpose(3, 0, 1, 4, 2).reshape(2, SSM_GROUPS, SSM_STATE, SSM_GROUP)
    return d[0], d[1]


def _c_matrix(c_re, c_im):
    cc = jnp.stack([c_re, -c_im]).reshape(2, N_SUPER, GROUPS_PER_SUPER, SSM_GROUP, SSM_STATE)
    m = jnp.einsum('rsgcp,gh->srgphc', cc, _EYE)
    return m.reshape(N_SUPER, SUPER_W, SUPER_IN)


def _c_matrix_grad(dm):
    d = dm.reshape(N_SUPER, 2, GROUPS_PER_SUPER, SSM_STATE, GROUPS_PER_SUPER, SSM_GROUP)
    d = jnp.sum(d * _EYE[None, None, :, None, :, None], axis=4)
    d = d.transpose(1, 0, 2, 4, 3).reshape(2, SSM_GROUPS, SSM_GROUP, SSM_STATE)
    return d[0], -d[1]


def _cmul_add(xr, xi, ar, ai, sr, si):
    return xr + ar * sr - ai * si, xi + ar * si + ai * sr


def _scan_rows(buf_ref, tab_ref, carry_ref, n_windows, reverse, h_ref=None, da_ref=None):
    order = list(range(SEG - 1, -1, -1)) if reverse else list(range(SEG))
    near = SUBLANES - 1 if reverse else 0
    far = 0 if reverse else SUBLANES - 1
    s_in = SUBLANES - 1 if reverse else 1
    lanes = lambda tile: pl.ds(tile * LANES, LANES)

    def window(w0, tile_re, tile_im, c_re, c_im, acc):
        rows = lambda t: pl.ds(w0 + t, SUBLANES, stride=SEG)
        get = lambda ref, t: (ref.at[tile_re][rows(t), :], ref.at[tile_im][rows(t), :])
        tab = lambda k: (tab_ref[k, :, lanes(tile_re)], tab_ref[k, :, lanes(tile_im)])

        def put(t, xr, xi):
            buf_ref.at[tile_re][rows(t), :] = xr
            buf_ref.at[tile_im][rows(t), :] = xi

        a1 = tab(4)
        er, ei = get(buf_ref, order[0])
        for t in order[1:]:
            er, ei = _cmul_add(*get(buf_ref, t), *a1, er, ei)
            if t != order[-1]:
                put(t, er, ei)
        for k, shift in enumerate((1, 2, 4)):
            s = (SUBLANES - shift) if reverse else shift
            er, ei = _cmul_add(er, ei, *tab(k), pltpu.roll(er, s, 0), pltpu.roll(ei, s, 0))
        er, ei = _cmul_add(er, ei, *tab(3), c_re, c_im)
        put(order[-1], er, ei)
        sub = lax.broadcasted_iota(jnp.int32, er.shape, 0)
        in_re = jnp.where(sub == near, c_re, pltpu.roll(er, s_in, 0))
        in_im = jnp.where(sub == near, c_im, pltpu.roll(ei, s_in, 0))
        true = {order[-1]: (er, ei)}
        for idx, t in enumerate(order[:-1]):
            true[t] = _cmul_add(*get(buf_ref, t), *tab(4 + idx), in_re, in_im)
            put(t, *true[t])
        carry = (jnp.broadcast_to(er[far:far + 1], er.shape), jnp.broadcast_to(ei[far:far + 1], ei.shape))
        if acc is None:
            return carry, None
        acc_re, acc_im = acc
        for t in range(SEG):
            if t + 1 < SEG:
                gr, gim = true[t + 1]
            else:
                gr = jnp.where(sub == SUBLANES - 1, c_re, pltpu.roll(true[0][0], SUBLANES - 1, 0))
                gim = jnp.where(sub == SUBLANES - 1, c_im, pltpu.roll(true[0][1], SUBLANES - 1, 0))
            hr, hi = get(h_ref, t)
            acc_re = acc_re + gr * hr + gim * hi
            acc_im = acc_im + gim * hr - gr * hi
        return carry, (acc_re, acc_im)

    half = SUPER_HALF // LANES
    per = 2 if h_ref is None else 4
    for sb in range(N_SUPER):
        pairs = [(2 * half * sb + j, 2 * half * sb + half + j) for j in range(half)]

        def step(wi, state, pairs=pairs):
            w = (n_windows - 1 - wi) if reverse else wi
            w0 = pl.multiple_of(w * WINDOW, WINDOW)
            out = []
            for j, (tile_re, tile_im) in enumerate(pairs):
                mine = state[per * j:per * (j + 1)]
                carry, acc = window(w0, tile_re, tile_im, mine[0], mine[1], mine[2:] or None)
                out += list(carry) + list(acc or ())
            return tuple(out)

        init = []
        for tile_re, tile_im in pairs:
            init += [carry_ref[:, lanes(tile_re)], carry_ref[:, lanes(tile_im)]]
            if h_ref is not None:
                init += [da_ref[:, lanes(tile_re)], da_ref[:, lanes(tile_im)]]
        fin = lax.fori_loop(0, n_windows, step, tuple(init))
        for j, (tile_re, tile_im) in enumerate(pairs):
            carry_ref[:, lanes(tile_re)] = fin[per * j]
            carry_ref[:, lanes(tile_im)] = fin[per * j + 1]
            if h_ref is not None:
                da_ref[:, lanes(tile_re)] = fin[per * j + 2]
                da_ref[:, lanes(tile_im)] = fin[per * j + 3]


def _put_tiles(ref, sb, block):
    for j in range(SUPER_TILES):
        ref[sb * SUPER_TILES + j] = block[:, j * LANES:(j + 1) * LANES]


def _get_tiles(ref, sb):
    return jnp.concatenate([ref[sb * SUPER_TILES + j] for j in range(SUPER_TILES)], axis=1)


def _ssm_fwd(u, bmat, cmat, tab, d_skip, tb, exchange=None):
    seq = u.shape[0]

    def body(u_ref, b_ref, c_ref, t_ref, d_ref, s_ref, h_ref, carry_ref):
        @pl.when(pl.program_id(0) == 0)
        def _():
            carry_ref[...] = jnp.zeros_like(carry_ref)

        u_blk = u_ref[...]
        ub = _bf(u_blk)
        for sb in range(N_SUPER):
            _put_tiles(h_ref, sb, _mm(ub[:, sb * SUPER_IN:(sb + 1) * SUPER_IN], b_ref[sb]))
        _scan_rows(h_ref, t_ref, carry_ref, tb // WINDOW, False)
        ys = [_mm(_bf(_get_tiles(h_ref, sb)), c_ref[sb]) for sb in range(N_SUPER)]
        s_ref[...] = jnp.concatenate(ys, axis=1) + d_ref[...] * u_blk

    return _rowcall("ssm_fwd", body, seq, tb, [u], [bmat, cmat, tab, d_skip],
                    [(SSM_W, F32), ((STATE_TILES, LANES), F32)], [],
                    scratch=[pltpu.VMEM((SUBLANES, STATE_COLS), F32)], vmem=VMEM_BIG, exchange=exchange)


def _ssm_bwd(ds, u, h, bmat_t, cmat_t, tab, d_skip, tb, exchange=None):
    seq = u.shape[0]

    def body(ds_ref, u_ref, h_ref, bt_ref, ct_ref, t_ref, d_ref,
             du_ref, db_ref, dc_ref, da_ref, dd_ref, g_ref, carry_ref):
        @pl.when(pl.program_id(0) == 0)
        def _():
            carry_ref[...] = jnp.zeros_like(carry_ref)
            db_ref[...] = jnp.zeros_like(db_ref)
            dc_ref[...] = jnp.zeros_like(dc_ref)
            da_ref[...] = jnp.zeros_like(da_ref)
            dd_ref[...] = jnp.zeros_like(dd_ref)

        ds_blk = ds_ref[...]
        dsb = _bf(ds_blk)
        u_blk = u_ref[...]
        ub = _bf(u_blk)
        for sb in range(N_SUPER):
            _put_tiles(g_ref, sb, _mm(dsb[:, sb * SUPER_IN:(sb + 1) * SUPER_IN], ct_ref[sb]))
        _scan_rows(g_ref, t_ref, carry_ref, tb // WINDOW, True, h_ref=h_ref, da_ref=da_ref)
        dus = []
        for sb in range(N_SUPER):
            gb = _bf(_get_tiles(g_ref, sb))
            dus.append(_mm(gb, bt_ref[sb]))
            db_ref[sb] += _mm_tn(ub[:, sb * SUPER_IN:(sb + 1) * SUPER_IN], gb)
            dc_ref[sb] += _mm_tn(_bf(_get_tiles(h_ref, sb)), dsb[:, sb * SUPER_IN:(sb + 1) * SUPER_IN])
        du_ref[...] = jnp.concatenate(dus, axis=1) + d_ref[...] * ds_blk
        dd_ref[...] += jnp.sum(ds_blk * u_blk, axis=0, keepdims=True)

    return _rowcall("ssm_bwd", body, seq, tb, [ds, u, h], [bmat_t, cmat_t, tab, d_skip],
                    [(SSM_W, F32)],
                    [((N_SUPER, SUPER_IN, SUPER_W), F32), ((N_SUPER, SUPER_W, SUPER_IN), F32),
                     ((SUBLANES, STATE_COLS), F32), ((1, SSM_W), F32)],
                    scratch=[pltpu.VMEM((STATE_TILES, tb, LANES), F32), pltpu.VMEM((SUBLANES, STATE_COLS), F32)],
                    reverse=True, vmem=VMEM_BIG, exchange=exchange)


def _merge_core(s, attb, ga, gs, wg_ref, wab_ref, wsb_ref, wout_ref):
    zg, dgelu = _gelu_and_grad(s)
    zgb = _bf(zg)
    sg = _sig(_mm(zgb, wg_ref[...]))
    z = zg * sg
    zb = _bf(z)
    ys = jnp.concatenate([_mm(zb, wsb_ref[j]) for j in range(N_CHIPS)], axis=1)
    ya = jnp.concatenate([_mm(attb, wab_ref[j]) for j in range(N_CHIPS)], axis=1)
    sa = _sig(ga)
    ss = _sig(gs)
    mgb = _bf(sa * ya + ss * ys)
    o = _mm(mgb, wout_ref[...])
    return dict(zg=zg, dgelu=dgelu, zgb=zgb, sg=sg, zb=zb, ys=ys, ya=ya, sa=sa, ss=ss, mgb=mgb, o=o)


def _merge_fwd(x, s, att, ga, gs, g2, w_glu, w_ab, w_sb, w_out, tb):
    seq = x.shape[0]

    def body(x_ref, s_ref, att_ref, ga_ref, gs_ref, g_ref, wg_ref, wab_ref, wsb_ref, wout_ref, x2_ref):
        f = _merge_core(s_ref[...], att_ref[...], ga_ref[...], gs_ref[...], wg_ref, wab_ref, wsb_ref, wout_ref)
        n, _, _ = _rms(f["o"], g_ref[...])
        x2_ref[...] = x_ref[...] + n

    return _rowcall("merge_fwd", body, seq, tb, [x, s, att, ga, gs], [g2, w_glu, w_ab, w_sb, w_out],
                    [(D_MODEL, F32)], [], vmem=VMEM_BIG)[0]


def _merge_bwd(dx2, s, att, ga, gs, g2, w_glu, w_ab, w_sb, w_out, tb, exchange=None):
    seq = s.shape[0]
    cw = D_MODEL // N_CHIPS
    last = seq // tb - 1

    def body(dx2_ref, s_ref, att_ref, ga_ref, gs_ref, g_ref, wg_ref, wab_ref, wsb_ref, wout_ref,
             ds_ref, datt_ref, dga_ref, dgs_ref, dg_ref, dwg_ref, dwab_ref, dwsb_ref, dwout_ref,
             bwg_ref, bwab_ref, bwsb_ref, bwout_ref):
        @pl.when(pl.program_id(0) == 0)
        def _():
            for r in (dg_ref, dwg_ref, dwab_ref, dwsb_ref, dwout_ref):
                r[...] = jnp.zeros_like(r)

        attb = att_ref[...]
        f = _merge_core(s_ref[...], attb, ga_ref[...], gs_ref[...], wg_ref, wab_ref, wsb_ref, wout_ref)
        g = g_ref[...]
        _, oh, r2 = _rms(f["o"], g)
        do, dg = _rms_bwd(dx2_ref[...], oh, r2, g)
        dg_ref[...] += dg
        dob = _bf(do)
        dwout_ref[...] += _mm_tn(f["mgb"], dob)
        dmg = _mm_nt(dob, wout_ref[...])
        sa, ss = f["sa"], f["ss"]
        dyab = _bf(dmg * sa)
        dysb = _bf(dmg * ss)
        dga_ref[...] = _bf(dmg * f["ya"] * sa * (1.0 - sa))
        dgs_ref[...] = _bf(dmg * f["ys"] * ss * (1.0 - ss))
        dwab = _mm_tn(attb, dyab)
        dwsb = _mm_tn(f["zb"], dysb)
        datt = jnp.zeros((tb, ATTN_W), F32)
        dz = jnp.zeros((tb, SSM_W), F32)
        for j in range(N_CHIPS):
            dwab_ref[j] += dwab[:, j * cw:(j + 1) * cw]
            dwsb_ref[j] += dwsb[:, j * cw:(j + 1) * cw]
            datt = datt + _mm_nt(dyab[:, j * cw:(j + 1) * cw], wab_ref[j])
            dz = dz + _mm_nt(dysb[:, j * cw:(j + 1) * cw], wsb_ref[j])
        datt_ref[...] = _bf(datt)
        sg, zg = f["sg"], f["zg"]
        dglb = _bf(dz * zg * sg * (1.0 - sg))
        dwg_ref[...] += _mm_tn(f["zgb"], dglb)
        dzg = dz * sg + _mm_nt(dglb, wg_ref[...])
        ds_ref[...] = dzg * f["dgelu"]

        @pl.when(pl.program_id(0) == last)
        def _():
            for dst, src in ((bwg_ref, dwg_ref), (bwab_ref, dwab_ref), (bwsb_ref, dwsb_ref), (bwout_ref, dwout_ref)):
                dst[...] = _bf(src[...])

    shapes = [w_glu.shape, w_ab.shape, w_sb.shape, w_out.shape]
    return _rowcall("merge_bwd", body, seq, tb, [dx2, s, att, ga, gs], [g2, w_glu, w_ab, w_sb, w_out],
                    [(SSM_W, F32), (ATTN_W, BF16), (D_MODEL, BF16), (D_MODEL, BF16)],
                    [((1, D_MODEL), F32)] + [(sh, F32) for sh in shapes] + [(sh, BF16) for sh in shapes],
                    vmem=VMEM_BIG, exchange=exchange)


def _mlp_fwd_loss(x2, target, g3, g4, w_ffi, w_ffo, tb):
    seq = x2.shape[0]
    n_slab = len(w_ffi)
    sw = D_FF // FF_CHUNKS // n_slab

    def body(x2_ref, t_ref, g3_ref, g4_ref, *rest):
        wi_refs, (wo_ref, dy_ref, df_ref, h_ref, loss_ref, dg_ref) = rest[:n_slab], rest[n_slab:]

        @pl.when(pl.program_id(0) == 0)
        def _():
            loss_ref[...] = jnp.zeros_like(loss_ref)
            dg_ref[...] = jnp.zeros_like(dg_ref)

        x2_blk = x2_ref[...]
        h3, _, _ = _rms(x2_blk, g3_ref[...])
        hb = _bf(h3)
        h_ref[...] = hb
        f = jnp.zeros((tb, D_MODEL), F32)
        for j in range(FF_CHUNKS):
            for k in range(n_slab):
                a = _mm(hb, wi_refs[k][j])
                f = f + _mm(_bf(jnp.square(jnp.maximum(a, 0.0))), wo_ref[j, pl.ds(k * sw, sw), :])
        g4 = g4_ref[...]
        n4, fh, r4 = _rms(f, g4)
        e = (x2_blk + n4) - t_ref[...]
        loss_ref[...] += 0.5 * jnp.sum(jnp.mean(e * e, axis=-1, keepdims=True))
        dy = e * (1.0 / D_MODEL)
        dy_ref[...] = dy
        df, dg = _rms_bwd(dy, fh, r4, g4)
        df_ref[...] = _bf(df)
        dg_ref[...] += dg

    return _rowcall("mlp_fwd_loss", body, seq, tb, [x2, target], [g3, g4, *w_ffi, w_ffo],
                    [(D_MODEL, F32), (D_MODEL, BF16), (D_MODEL, BF16)],
                    [((SUBLANES, 128), F32), ((1, D_MODEL), F32)], vmem=VMEM_BIG)


def _mlp_bwd(x2, dy, df, h3, g3, w_ffi, w_ffo, tb):
    seq = x2.shape[0]
    n_slab = len(w_ffi)
    sw = D_FF // FF_CHUNKS // n_slab

    def body(x2_ref, dy_ref, df_ref, h_ref, g3_ref, *rest):
        wi_refs, (wo_ref, dx_ref, act_ref, da_ref, dg_ref) = rest[:n_slab], rest[n_slab:]

        @pl.when(pl.program_id(0) == 0)
        def _():
            dg_ref[...] = jnp.zeros_like(dg_ref)

        hb = h_ref[...]
        dfb = df_ref[...]
        dh = jnp.zeros((tb, D_MODEL), F32)
        for j in range(FF_CHUNKS):
            for k in range(n_slab):
                cols = pl.ds((j * n_slab + k) * sw, sw)
                ra = jnp.maximum(_mm(hb, wi_refs[k][j]), 0.0)
                act_ref[:, cols] = _bf(ra * ra)
                dab = _bf(_mm_nt(dfb, wo_ref[j, pl.ds(k * sw, sw), :]) * (2.0 * ra))
                da_ref[:, cols] = dab
                dh = dh + _mm_nt(dab, wi_refs[k][j])
        g3 = g3_ref[...]
        _, xh, r3 = _rms(x2_ref[...], g3)
        dxn, dg = _rms_bwd(dh, xh, r3, g3)
        dx_ref[...] = dy_ref[...] + dxn
        dg_ref[...] += dg

    return _rowcall("mlp_bwd", body, seq, tb, [x2, dy, df, h3], [g3, *w_ffi, w_ffo],
                    [(D_MODEL, F32), (D_FF, BF16), (D_FF, BF16)], [((1, D_MODEL), F32)], vmem=VMEM_BIG)


def _matmul_tn(name, a, b, tk, tn, tl, chunk_major, exchange=None):
    seq, kdim = a.shape
    ndim = b.shape[1]
    last = seq // tl - 1

    def body(a_ref, b_ref, o_ref, ob_ref):
        @pl.when(pl.program_id(2) == 0)
        def _():
            o_ref[...] = jnp.zeros_like(o_ref)

        o_ref[...] += _mm_tn(a_ref[...], b_ref[...])

        @pl.when(pl.program_id(2) == last)
        def _():
            ob_ref[...] = _bf(o_ref[...])

    if chunk_major:
        shape = (ndim // tn, kdim, tn)
        out_spec = pl.BlockSpec((None, tk, tn), lambda k, n, l: (n, k, 0))
    else:
        shape = (kdim, ndim)
        out_spec = pl.BlockSpec((tk, tn), lambda k, n, l: (k, n))
    return _fused_call(
        name, body, (kdim // tk, ndim // tn, seq // tl),
        [pl.BlockSpec((tl, tk), lambda k, n, l: (l, k)), pl.BlockSpec((tl, tn), lambda k, n, l: (l, n))],
        [out_spec, out_spec], [SDS(shape, F32), SDS(shape, BF16)], [], [a, b], exchange, _params(3, VMEM_BIG))


def _ew_call(name, fn, ins, n_out, after=None):
    rows, cols = ins[0].shape
    tr = rows
    while tr * cols * 4 > min(1 << 20, (9 << 20) // (len(ins) + n_out)) and tr % 16 == 0:
        tr //= 2
    spec = pl.BlockSpec((tr, cols), lambda i: (i, 0))
    extra = [] if after is None else [after]

    def body(*refs):
        outs = fn(*[r[...] for r in refs[:len(ins)]])
        for r, o in zip(refs[len(ins) + len(extra):], outs):
            r[...] = o

    return pl.pallas_call(
        body, grid=(rows // tr,), in_specs=[spec] * len(ins) + [ANY] * len(extra), out_specs=[spec] * n_out,
        out_shape=[SDS((rows, cols), F32)] * n_out, name=name, compiler_params=_params(1))(*ins, *extra)


def _adam_math(w, g, m, v):
    m2 = ADAM_B1 * m + (1.0 - ADAM_B1) * g
    v2 = ADAM_B2 * v + (1.0 - ADAM_B2) * (g * g)
    m_hat = m2 / (1.0 - ADAM_B1 ** ADAM_STEP)
    v_hat = v2 / (1.0 - ADAM_B2 ** ADAM_STEP)
    delta = -ADAM_LR * (m_hat / (jnp.sqrt(v_hat) + ADAM_EPS) + ADAM_WD * w)
    return delta, m2, v2


def _sum4(name, own, recv, idx):
    _, rows, cols = own.shape
    tr = rows
    while tr * cols * 4 > (1 << 20) and tr % 16 == 0:
        tr //= 2

    def body(idx_ref, o_ref, r0_ref, r1_ref, r2_ref, out_ref):
        out_ref[...] = ((o_ref[...] + r0_ref[...].astype(F32)) + r1_ref[...].astype(F32)) + r2_ref[...].astype(F32)

    blk = (None, tr, cols)
    grid_spec = pltpu.PrefetchScalarGridSpec(
        num_scalar_prefetch=1, grid=(rows // tr,),
        in_specs=[pl.BlockSpec(blk, lambda i, s: (s[0], i, 0)), pl.BlockSpec(blk, lambda i, s: (0, i, 0)),
                  pl.BlockSpec(blk, lambda i, s: (1, i, 0)), pl.BlockSpec(blk, lambda i, s: (2, i, 0))],
        out_specs=pl.BlockSpec((tr, cols), lambda i, s: (i, 0)))
    return pl.pallas_call(body, grid_spec=grid_spec, out_shape=SDS((rows, cols), F32), name=name,
                          compiler_params=_params(1))(jnp.reshape(idx, (1,)).astype(jnp.int32), own, recv, recv, recv)


def _adam_pair(name, item, after=None):
    def fn(w_, a, b, m_, v_):
        g = a + b
        return (g,) + _adam_math(w_, g, m_, v_)

    return _ew_call(name, fn, list(item), 4, after)


def _place():
    return lax.axis_index("x"), lax.axis_index("y"), lax.axis_index("c")


def _other_chips(x, y):
    return [(1 - x, y), (x, 1 - y), (1 - x, 1 - y)]


def _gather_chips(shards):
    n = len(shards)

    def copies(ins, outs, sems):
        send, recv, fwd_send, fwd_recv, loc = sems
        x, y, c = _place()
        me = 2 * x + y
        peers = _other_chips(x, y)
        local = [pltpu.make_async_copy(ins[a], outs[a].at[me], loc.at[a]) for a in range(n)]
        sends, recvs, passes, passed = [], [], [], []
        for a in range(n):
            half = shards[a].shape[0] // 2
            mine = pl.ds(c * half, half)
            theirs = pl.ds((1 - c) * half, half)
            for j, (px, py) in enumerate(peers):
                far = 2 * px + py
                sends.append(pltpu.make_async_remote_copy(
                    src_ref=ins[a].at[mine], dst_ref=outs[a].at[me, mine], send_sem=send.at[a, j],
                    recv_sem=recv.at[a, j], device_id=(px, py, c), device_id_type=MESH_ID))
                recvs.append(pltpu.make_async_remote_copy(
                    src_ref=ins[a].at[mine], dst_ref=outs[a].at[far, mine], send_sem=send.at[a, j],
                    recv_sem=recv.at[a, j], device_id=(px, py, c), device_id_type=MESH_ID))
                passes.append(pltpu.make_async_remote_copy(
                    src_ref=outs[a].at[far, mine], dst_ref=outs[a].at[far, mine], send_sem=fwd_send.at[a, j],
                    recv_sem=fwd_recv.at[a, j], device_id=(x, y, 1 - c), device_id_type=MESH_ID))
                passed.append(pltpu.make_async_remote_copy(
                    src_ref=outs[a].at[far, theirs], dst_ref=outs[a].at[far, theirs], send_sem=fwd_send.at[a, j],
                    recv_sem=fwd_recv.at[a, j], device_id=(x, y, 1 - c), device_id_type=MESH_ID))
        return local, sends, recvs, passes, passed

    def start(ins, outs, sems):
        local, sends, _, _, _ = copies(ins, outs, sems)
        for cp in local + sends:
            cp.start()

    def wait(ins, outs, sems):
        local, sends, recvs, passes, passed = copies(ins, outs, sems)
        for got, on in zip(recvs, passes):
            got.wait_recv()
            on.start()
        for cp in passed:
            cp.wait_recv()
        for cp in passes + sends:
            cp.wait_send()
        for cp in local:
            cp.wait()

    assert all(s.shape[0] % 32 == 0 for s in shards)
    pair = pltpu.SemaphoreType.DMA((n, 3))
    return _Exchange(shards, [SDS((N_CHIPS,) + s.shape, s.dtype) for s in shards],
                     [pair, pair, pair, pair, pltpu.SemaphoreType.DMA((n,))], start, wait)


def _scatter_chips(chunks):
    n = len(chunks)

    def copies(ins, outs, sems):
        send, recv = sems
        x, y, c = _place()
        return [pltpu.make_async_remote_copy(
            src_ref=ins[a].at[2 * px + py], dst_ref=outs[a].at[j], send_sem=send.at[a, j],
            recv_sem=recv.at[a, j], device_id=(px, py, c), device_id_type=MESH_ID)
            for a in range(n) for j, (px, py) in enumerate(_other_chips(x, y))]

    def start(ins, outs, sems):
        for cp in copies(ins, outs, sems):
            cp.start()

    def wait(ins, outs, sems):
        cps = copies(ins, outs, sems)
        for cp in cps:
            cp.wait_recv()
        for cp in cps:
            cp.wait_send()

    return _Exchange(chunks, [SDS((3,) + s.shape[1:], s.dtype) for s in chunks],
                     [pltpu.SemaphoreType.DMA((n, 3)), pltpu.SemaphoreType.DMA((n, 3))], start, wait)


HBM = pl.BlockSpec(memory_space=pltpu.HBM)
SEM = pl.BlockSpec(memory_space=pltpu.SEMAPHORE)
DATAFLOW = pltpu.SideEffectType.DATAFLOW_SIDE_EFFECTING


class _Flight:
    def __init__(self, copies, n_copies, send, recv, srcs, lands, token):
        self.copies, self.n, self.send, self.recv = copies, n_copies, send, recv
        self.srcs, self.lands, self.token = list(srcs), list(lands), token


def _take_off(name, srcs, lands, copies, n_copies, after):
    n_s, n_l = len(srcs), len(lands)

    def body(*refs):
        src, land = refs[:n_s], refs[n_s:n_s + n_l]
        send, recv = refs[n_s + n_l + 1:n_s + n_l + 3]
        for cp in copies(src, land, send, recv):
            cp.start()
        refs[-1][...] = jnp.zeros_like(refs[-1])

    mem = lambda t: pltpu.HBM(t.shape, t.dtype)
    sems = pltpu.SemaphoreType.DMA((n_copies,))
    outs = pl.pallas_call(
        body, name=name,
        out_shape=(sems, sems, *map(mem, srcs), *map(mem, lands), SDS((SUBLANES, LANES), F32)),
        in_specs=[HBM] * (n_s + n_l) + [ANY],
        out_specs=(SEM, SEM, *[HBM] * (n_s + n_l), pl.BlockSpec(memory_space=pltpu.VMEM)),
        input_output_aliases={i: 2 + i for i in range(n_s + n_l)},
        compiler_params=pltpu.CompilerParams(has_side_effects=DATAFLOW),
    )(*[pltpu.with_memory_space_constraint(t, pltpu.HBM) for t in (*srcs, *lands)], after)
    return _Flight(copies, n_copies, outs[0], outs[1], outs[2:2 + n_s], outs[2 + n_s:2 + n_s + n_l], outs[-1])


def _land(name, flight, after):
    n_s, n_l = len(flight.srcs), len(flight.lands)

    def body(*refs):
        src, land = refs[:n_s], refs[n_s:n_s + n_l]
        send, recv = refs[n_s + n_l:n_s + n_l + 2]
        for cp in flight.copies(src, land, send, recv):
            cp.wait_send()
            cp.wait_recv()

    mem = lambda t: pltpu.HBM(t.shape, t.dtype)
    outs = pl.pallas_call(
        body, name=name, out_shape=(*map(mem, flight.srcs), *map(mem, flight.lands)),
        in_specs=[HBM] * (n_s + n_l) + [SEM, SEM, ANY], out_specs=tuple([HBM] * (n_s + n_l)),
        input_output_aliases={i: i for i in range(n_s + n_l)},
        compiler_params=pltpu.CompilerParams(has_side_effects=DATAFLOW),
    )(*flight.srcs, *flight.lands, flight.send, flight.recv, after)
    return list(outs[n_s:])


def _empty_like(shapes_from, lead):
    return [lax.empty((lead,) + t.shape[1:], t.dtype) for t in shapes_from]


def _scatter_off(name, chunks, after):
    def copies(src, land, send, recv):
        x, y, c = _place()
        return [pltpu.make_async_remote_copy(
            src_ref=src[a].at[2 * px + py], dst_ref=land[a].at[k], send_sem=send.at[3 * a + k],
            recv_sem=recv.at[3 * a + k], device_id=(px, py, c), device_id_type=MESH_ID)
            for a in range(len(chunks)) for k, (px, py) in enumerate(_other_chips(x, y))]

    return _take_off(name, chunks, _empty_like(chunks, 3), copies, 3 * len(chunks), after)


def _swap_off(name, arrs, after):
    def copies(src, land, send, recv):
        x, y, c = _place()
        return [pltpu.make_async_remote_copy(
            src_ref=src[a], dst_ref=land[a], send_sem=send.at[a], recv_sem=recv.at[a],
            device_id=(x, y, 1 - c), device_id_type=MESH_ID) for a in range(len(arrs))]

    return _take_off(name, arrs, [lax.empty(t.shape, t.dtype) for t in arrs], copies, len(arrs), after)


def _devices_off(name, block, after):
    me = 4 * lax.axis_index("x") + 2 * lax.axis_index("y") + lax.axis_index("c")
    land = lax.dynamic_update_index_in_dim(lax.empty((N_DEV,) + block.shape, block.dtype), block, me, 0)

    def copies(src, land, send, recv):
        x, y, c = _place()
        mine = 4 * x + 2 * y + c
        return [pltpu.make_async_remote_copy(
            src_ref=src[0], dst_ref=land[0].at[mine], send_sem=send.at[k - 1], recv_sem=recv.at[k - 1],
            device_id=(x ^ (k >> 2), y ^ ((k >> 1) & 1), c ^ (k & 1)), device_id_type=MESH_ID)
            for k in range(1, N_DEV)]

    return _take_off(name, [block], [land], copies, N_DEV - 1, after)


def _half_rows(shape, c, other=False):
    half = shape[0] // 2
    return pl.ds(((1 - c) if other else c) * half, half)


def _gather_start(name, shards, lands, after):
    n = len(shards)

    def body(*refs):
        src, land, (send, recv) = refs[:n], refs[n:2 * n], refs[2 * n + 1:2 * n + 3]
        x, y, c = _place()
        me = 2 * x + y
        for a in range(n):
            mine = _half_rows(shards[a].shape, c)
            for j, (px, py) in enumerate(_other_chips(x, y)):
                pltpu.make_async_remote_copy(
                    src_ref=src[a].at[mine], dst_ref=land[a].at[me, mine], send_sem=send.at[3 * a + j],
                    recv_sem=recv.at[3 * a + j], device_id=(px, py, c), device_id_type=MESH_ID).start()
        token = refs[-1]
        token[...] = jnp.zeros_like(token)

    mem = lambda t: pltpu.HBM(t.shape, t.dtype)
    pair = pltpu.SemaphoreType.DMA((3 * n,))
    outs = pl.pallas_call(
        body, name=name,
        out_shape=(pair, pair, *map(mem, shards), *map(mem, lands), SDS((SUBLANES, LANES), F32)),
        in_specs=[HBM] * (2 * n) + [ANY],
        out_specs=(SEM, SEM, *[HBM] * (2 * n), pl.BlockSpec(memory_space=pltpu.VMEM)),
        input_output_aliases={i: 2 + i for i in range(2 * n)},
        compiler_params=pltpu.CompilerParams(has_side_effects=DATAFLOW),
    )(*[pltpu.with_memory_space_constraint(t, pltpu.HBM) for t in (*shards, *lands)], after)
    return outs[0], outs[1], list(outs[2:2 + n]), list(outs[2 + n:2 + 2 * n]), outs[-1]


def _gather_pass(name, send, recv, shards, lands, after):
    n = len(shards)

    def body(*refs):
        src, land, (send, recv, _) = refs[:n], refs[n:2 * n], refs[2 * n:2 * n + 3]
        fsend, frecv = refs[2 * n + 3], refs[2 * n + 4]
        x, y, c = _place()
        me = 2 * x + y
        for a in range(n):
            mine = _half_rows(shards[a].shape, c)
            for j, (px, py) in enumerate(_other_chips(x, y)):
                far = 2 * px + py
                ici = pltpu.make_async_remote_copy(
                    src_ref=src[a].at[mine], dst_ref=land[a].at[far, mine], send_sem=send.at[3 * a + j],
                    recv_sem=recv.at[3 * a + j], device_id=(px, py, c), device_id_type=MESH_ID)
                ici.wait_recv()
                ici.wait_send()
                pltpu.make_async_remote_copy(
                    src_ref=land[a].at[far, mine], dst_ref=land[a].at[far, mine], send_sem=fsend.at[3 * a + j],
                    recv_sem=frecv.at[3 * a + j], device_id=(x, y, 1 - c), device_id_type=MESH_ID).start()
        token = refs[-1]
        token[...] = jnp.zeros_like(token)

    mem = lambda t: pltpu.HBM(t.shape, t.dtype)
    pair = pltpu.SemaphoreType.DMA((3 * n,))
    outs = pl.pallas_call(
        body, name=name,
        out_shape=(pair, pair, *map(mem, lands), SDS((SUBLANES, LANES), F32)),
        in_specs=[HBM] * (2 * n) + [SEM, SEM, ANY],
        out_specs=(SEM, SEM, *[HBM] * n, pl.BlockSpec(memory_space=pltpu.VMEM)),
        input_output_aliases={n + i: 2 + i for i in range(n)},
        compiler_params=pltpu.CompilerParams(has_side_effects=DATAFLOW),
    )(*shards, *lands, send, recv, after)
    return outs[0], outs[1], list(outs[2:2 + n]), outs[-1]


def _gather_wait(name, fsend, frecv, lands, after):
    n = len(lands)

    def body(*refs):
        land, (fsend, frecv, _) = refs[:n], refs[n:n + 3]
        x, y, c = _place()
        for a in range(n):
            for j, (px, py) in enumerate(_other_chips(x, y)):
                far = 2 * px + py
                mine = _half_rows(lands[a].shape[1:], c)
                theirs = _half_rows(lands[a].shape[1:], c, other=True)
                pltpu.make_async_remote_copy(
                    src_ref=land[a].at[far, mine], dst_ref=land[a].at[far, mine], send_sem=fsend.at[3 * a + j],
                    recv_sem=frecv.at[3 * a + j], device_id=(x, y, 1 - c), device_id_type=MESH_ID).wait_send()
                pltpu.make_async_remote_copy(
                    src_ref=land[a].at[far, theirs], dst_ref=land[a].at[far, theirs], send_sem=fsend.at[3 * a + j],
                    recv_sem=frecv.at[3 * a + j], device_id=(x, y, 1 - c), device_id_type=MESH_ID).wait_recv()

    mem = lambda t: pltpu.HBM(t.shape, t.dtype)
    return list(pl.pallas_call(
        body, name=name, out_shape=tuple(map(mem, lands)), in_specs=[HBM] * n + [SEM, SEM, ANY],
        out_specs=tuple([HBM] * n), input_output_aliases={i: i for i in range(n)},
        compiler_params=pltpu.CompilerParams(has_side_effects=DATAFLOW),
    )(*lands, fsend, frecv, after))


def _after(token):
    return _Exchange([token], [], [], lambda *_: None, lambda *_: None)


def _swap_sibling(arrs):
    n = len(arrs)

    def copies(ins, outs, sems):
        send, recv = sems
        x, y, c = _place()
        return [pltpu.make_async_remote_copy(
            src_ref=ins[a], dst_ref=outs[a], send_sem=send.at[a], recv_sem=recv.at[a],
            device_id=(x, y, 1 - c), device_id_type=MESH_ID) for a in range(n)]

    def start(ins, outs, sems):
        for cp in copies(ins, outs, sems):
            cp.start()

    def wait(ins, outs, sems):
        cps = copies(ins, outs, sems)
        for cp in cps:
            cp.wait_recv()
        for cp in cps:
            cp.wait_send()

    return _Exchange(arrs, [SDS(s.shape, s.dtype) for s in arrs],
                     [pltpu.SemaphoreType.DMA((n,)), pltpu.SemaphoreType.DMA((n,))], start, wait)


N_DEV = 8


def _gather_devices(block):
    def copies(ins, outs, sems):
        send, recv, loc = sems
        x, y, c = _place()
        me = 4 * x + 2 * y + c
        local = pltpu.make_async_copy(ins[0], outs[0].at[me], loc.at[0])
        sends, recvs = [], []
        for k in range(1, N_DEV):
            peer = (x ^ (k >> 2), y ^ ((k >> 1) & 1), c ^ (k & 1))
            for group, slot in ((sends, me), (recvs, me ^ k)):
                group.append(pltpu.make_async_remote_copy(
                    src_ref=ins[0], dst_ref=outs[0].at[slot], send_sem=send.at[k - 1], recv_sem=recv.at[k - 1],
                    device_id=peer, device_id_type=MESH_ID))
        return local, sends, recvs

    def start(ins, outs, sems):
        local, sends, _ = copies(ins, outs, sems)
        for cp in [local] + sends:
            cp.start()

    def wait(ins, outs, sems):
        local, sends, recvs = copies(ins, outs, sems)
        for cp in recvs:
            cp.wait_recv()
        for cp in sends:
            cp.wait_send()
        local.wait()

    return _Exchange([block], [SDS((N_DEV,) + block.shape, block.dtype)],
                     [pltpu.SemaphoreType.DMA((N_DEV - 1,)), pltpu.SemaphoreType.DMA((N_DEV - 1,)),
                      pltpu.SemaphoreType.DMA((1,))], start, wait)


def _both(ex_a, ex_b):
    na_i, na_o, na_s = len(ex_a.ins), len(ex_a.outs), len(ex_a.sems)

    def start(ins, outs, sems):
        ex_a.start(ins[:na_i], outs[:na_o], sems[:na_s])
        ex_b.start(ins[na_i:], outs[na_o:], sems[na_s:])

    def wait(ins, outs, sems):
        ex_a.wait(ins[:na_i], outs[:na_o], sems[:na_s])
        ex_b.wait(ins[na_i:], outs[na_o:], sems[na_s:])

    return _Exchange(ex_a.ins + ex_b.ins, ex_a.outs + ex_b.outs, ex_a.sems + ex_b.sems, start, wait)


def _sum_devices(slots):
    def body(s_ref, o_ref):
        acc = s_ref[0]
        for d in range(1, N_DEV):
            acc = acc + s_ref[d]
        o_ref[...] = acc

    return pl.pallas_call(
        body, in_specs=[pl.BlockSpec(memory_space=pltpu.VMEM)], out_specs=pl.BlockSpec(memory_space=pltpu.VMEM),
        out_shape=SDS(slots.shape[1:], F32), name="sum_small",
        compiler_params=pltpu.CompilerParams(vmem_limit_bytes=32 * 1024 * 1024))(slots)


def _adam_small(ws, gs, ms, vs):
    n = len(ws)

    def body(*refs):
        for i in range(n):
            w_ref, g_ref, m_ref, v_ref = (refs[k * n + i] for k in range(4))
            outs = _adam_math(w_ref[...], g_ref[...], m_ref[...], v_ref[...])
            for k in range(3):
                refs[(4 + k) * n + i][...] = outs[k]

    vmem = pl.BlockSpec(memory_space=pltpu.VMEM)
    return pl.pallas_call(
        body, in_specs=[vmem] * (4 * n), out_specs=[vmem] * (3 * n),
        out_shape=[SDS(w.shape, F32) for w in ws] * 3, name="adam_small",
        compiler_params=pltpu.CompilerParams(vmem_limit_bytes=32 * 1024 * 1024))(*ws, *gs, *ms, *vs)


def _local_step(x, target, small, big, tb, distributed):
    g1, g2, g3, g4 = small["norm_mix_pre"], small["norm_mix_post"], small["norm_mlp_pre"], small["norm_mlp_post"]
    dist = distributed
    me = (2 * lax.axis_index("x") + lax.axis_index("y")) if dist else 0
    tb_ssm = min(tb, 256)
    bucket = jnp.asarray(_bucket_table())

    keys_first = lambda t: jnp.swapaxes(t, -1, -2)
    bias = _bias_table(small["rel_bias"], bucket)
    sink_rows = keys_first(_pair_layout(jnp.broadcast_to(small["sinks"].reshape(N_HEADS, 1, 1), (N_HEADS, BLOCK, 1))))
    disc_args = (small["lam_re"], small["lam_im"], small["log_dt"], small["b_re"], small["b_im"])
    (ab_re, ab_im, bb_re, bb_im), disc_vjp = jax.vjp(_ssm_discretize, *disc_args)
    tab_f, tab_b = _scan_tables(ab_re, ab_im)
    bmat = _bf(_b_matrix(bb_re, bb_im))
    cmat = _bf(_c_matrix(small["c_re"], small["c_im"]))
    d_skip = small["d_skip"]

    place_own = lambda t: lax.dynamic_update_index_in_dim(lax.empty((N_CHIPS,) + t.shape, t.dtype), t, me, 0)
    if dist:
        send, recv, src, lands, _ = _gather_start("gather_in_start", [big["w_in"]], [place_own(big["w_in"])], d_skip)
        prepared = (tab_b[0, :1, :LANES] + bias[0, 0, 0, :1, :LANES] + sink_rows[0, 0, :, :LANES]
                    + bmat[0, :1, :LANES].astype(F32) + cmat[0, :1, :LANES].astype(F32))
        send, recv, lands, token = _gather_pass("gather_in_pass", send, recv, src, lands, prepared)
        (g_in,) = _gather_wait("gather_in_wait", send, recv, lands, token)
        w_in = g_in.reshape(IN_W, D_MODEL)
    else:
        w_in = big["w_in"]
    mix = ("w_glu", "w_attn_branch", "w_ssm_branch", "w_out")
    rest = [big[n] for n in mix + ("w_ff_in", "w_ff_out")]
    token = None
    if dist:
        send, recv, rest, lands, token = _gather_start("gather_rest_start", rest, [place_own(t) for t in rest], g_in)
    h1, q, k, v, u, ga, gs = _inproj_fwd(x, g1, w_in, tb, _after(token) if dist else None)
    s, h = _ssm_fwd(u, bmat, cmat, tab_f, d_skip, tb_ssm)
    if dist:
        send, recv, lands, token = _gather_pass("gather_rest_pass", send, recv, rest, lands, s)
    att = _attn_fwd(q, k, v, bias, sink_rows, _after(token) if dist else None)[0]
    if dist:
        rest = _gather_wait("gather_rest_wait", send, recv, lands, att)
    w_glu, w_ab, w_sb, w_out, w_ffi, w_ffo = rest
    w_glu = w_glu.reshape(SSM_W, SSM_W)
    w_out = w_out.reshape(D_MODEL, D_MODEL)
    w_ffi = [w_ffi]
    x2 = _merge_fwd(x, s, att, ga, gs, g2, w_glu, w_ab, w_sb, w_out, tb)
    dy, df, h3, loss_acc, dg4 = _mlp_fwd_loss(x2, target, g3, g4, w_ffi, w_ffo, tb)

    dx2, act, da, dg3 = _mlp_bwd(x2, dy, df, h3, g3, w_ffi, w_ffo, tb)
    tl = min(2048, x.shape[0])
    chunked = (N_CHIPS, D_FF // N_CHIPS, D_MODEL)
    d_ffi, b_ffi = _matmul_tn("grad_w_ff_in", h3, da, D_MODEL, D_FF // FF_CHUNKS, tl, True)
    d_ffo, b_ffo = _matmul_tn("grad_w_ff_out", act, df, D_FF // FF_CHUNKS, D_MODEL, tl, False)
    d_ffo, b_ffo = d_ffo.reshape(chunked), b_ffo.reshape(chunked)
    behind = lambda flight: _after(flight.token) if dist else None
    ff_fl = _scatter_off("scatter_ff_off", [b_ffi, b_ffo], b_ffo) if dist else None
    outs = _merge_bwd(dx2, s, att, ga, gs, g2, w_glu, w_ab, w_sb, w_out, tb_ssm, behind(ff_fl))
    ds, datt, dga, dgs, dg2, d_glu, d_ab, d_sb, d_out, b_glu, b_ab, b_sb, b_out = outs
    glu4, out4 = (N_CHIPS, SSM_W // N_CHIPS, SSM_W), (N_CHIPS, D_MODEL // N_CHIPS, D_MODEL)
    d_mix = [d_glu.reshape(glu4), d_ab, d_sb, d_out.reshape(out4)]
    b_mix = [b_glu.reshape(glu4), b_ab, b_sb, b_out.reshape(out4)]
    mix_fl = _scatter_off("scatter_mix_off", b_mix, b_mix[-1]) if dist else None
    du, d_bmat, d_cmat, da_acc, dd_skip = _ssm_bwd(
        ds, u, h, bmat.transpose(0, 2, 1), cmat.transpose(0, 2, 1), tab_b, d_skip, tb_ssm, behind(mix_fl))
    dq, dk, dv, dbias, dsink_rows = _attn_bwd(q, k, v, datt, bias, sink_rows)
    swap_fl = None
    if dist:
        r_ffi, r_ffo = _land("scatter_ff_land", ff_fl, dq)
        p_ffi = _sum4("sum_w_ff_in", d_ffi, r_ffi, me)
        p_ffo = _sum4("sum_w_ff_out", d_ffo, r_ffo, me)
        swap_fl = _swap_off("swap_ff_off", [p_ffi, p_ffo], p_ffo)
    dx, dpj, dg1 = _inproj_bwd(x, dx2, dq, dk, dv, du, dga, dgs, g1, w_in, tb, behind(swap_fl))

    dab_re, dab_im = _state_unlayout(jnp.sum(da_acc, axis=0))
    dbb_re, dbb_im = _b_matrix_grad(d_bmat)
    d_lam_re, d_lam_im, d_log_dt, d_b_re, d_b_im = disc_vjp((dab_re, dab_im, dbb_re, dbb_im))
    d_c_re, d_c_im = _c_matrix_grad(d_cmat)
    d_rel = _bias_grad(dbias, bucket)
    d_sinks = jnp.sum(_pair_unlayout(keys_first(dsink_rows)), axis=(1, 2))
    small_grads = dict(
        norm_mix_pre=dg1, norm_mix_post=dg2, norm_mlp_pre=dg3, norm_mlp_post=dg4, rel_bias=d_rel, sinks=d_sinks,
        lam_re=d_lam_re, lam_im=d_lam_im, log_dt=d_log_dt, b_re=d_b_re, b_im=d_b_im, c_re=d_c_re, c_im=d_c_im,
        d_skip=dd_skip)
    small_fl = _devices_off("small_off", _pack(small_grads, loss_acc), swap_fl.token) if dist else None
    outs = _matmul_tn("grad_w_in", dpj, h1, IN_W // 2, D_MODEL, tl, False, behind(small_fl))
    in4 = (N_CHIPS, IN_W // N_CHIPS, D_MODEL)
    d_in, b_in = outs[0].reshape(in4), outs[1].reshape(in4)
    if not dist:
        return loss_acc, dx, small_grads, dict(zip(BIG, [d_in] + d_mix + [d_ffi, d_ffo]))
    s_ffi, s_ffo = _land("swap_ff_land", swap_fl, b_in)
    (slots,) = _land("small_land", small_fl, b_in)
    r_mix = _land("scatter_mix_land", mix_fl, b_in)
    p_mix = [_sum4("sum_" + n, d, r, me) for n, d, r in zip(mix, d_mix, r_mix)]
    pending = dict(d_in=d_in, b_in=b_in, p_mix=p_mix, w_ff_in=(p_ffi, s_ffi), w_ff_out=(p_ffo, s_ffo), me=me)
    return loss_acc, dx, _sum_devices(slots), pending


SMALL = ['norm_mix_pre', 'norm_mix_post', 'norm_mlp_pre', 'norm_mlp_post', 'rel_bias', 'sinks', 'lam_re', 'lam_im',
         'log_dt', 'b_re', 'b_im', 'c_re', 'c_im', 'd_skip']
BIG = ['w_in', 'w_glu', 'w_attn_branch', 'w_ssm_branch', 'w_out', 'w_ff_in', 'w_ff_out']
WEIGHTS = ['norm_mix_pre', 'norm_mix_post', 'norm_mlp_pre', 'norm_mlp_post', 'w_in', 'rel_bias', 'sinks', 'lam_re',
           'lam_im', 'log_dt', 'b_re', 'b_im', 'c_re', 'c_im', 'd_skip', 'w_glu', 'w_attn_branch', 'w_ssm_branch',
           'w_out', 'w_ff_in', 'w_ff_out']
PACK_COLS = 1024
PACK_ORDER = ['b_re', 'b_im', 'c_re', 'c_im', 'lam_re', 'lam_im', 'norm_mix_pre', 'norm_mix_post', 'norm_mlp_pre',
              'norm_mlp_post', 'rel_bias', 'sinks', 'log_dt', 'd_skip']


STATE_MINOR = ('b_re', 'b_im')
PACK_ROWS = 144
LOSS_ROW = 140


def _pack(named, loss_acc):
    parts = []
    for n in PACK_ORDER:
        a = jnp.swapaxes(named[n], -1, -2) if n in STATE_MINOR else named[n]
        flat = a.reshape(-1)
        rows = -(-flat.shape[0] // PACK_COLS)
        parts.append(jnp.pad(flat, (0, rows * PACK_COLS - flat.shape[0])).reshape(rows, PACK_COLS))
    assert sum(p.shape[0] for p in parts) == LOSS_ROW
    parts.append(jnp.pad(loss_acc[0:1], ((0, PACK_ROWS - LOSS_ROW - 1), (0, PACK_COLS - loss_acc.shape[1]))))
    return jnp.concatenate(parts, axis=0)


def _unpack(packed, shapes):
    out, at = {}, 0
    for n in PACK_ORDER:
        shape = shapes[n][:-2] + (shapes[n][-1], shapes[n][-2]) if n in STATE_MINOR else shapes[n]
        size = int(np.prod(shape))
        rows = -(-size // PACK_COLS)
        blk = packed[at:at + rows]
        out[n] = (blk.reshape(-1)[:size] if size % PACK_COLS else blk).reshape(shape)
        at += rows
    return out


def kernel(x, norm_mix_pre, norm_mix_post, norm_mlp_pre, norm_mlp_post, w_in, rel_bias, sinks, lam_re, lam_im, log_dt, b_re, b_im, c_re, c_im, d_skip, w_glu, w_attn_branch, w_ssm_branch, w_out, w_ff_in, w_ff_out, loss_target, m_norm_mix_pre, m_norm_mix_post, m_norm_mlp_pre, m_norm_mlp_post, m_w_in, m_rel_bias, m_sinks, m_lam_re, m_lam_im, m_log_dt, m_b_re, m_b_im, m_c_re, m_c_im, m_d_skip, m_w_glu, m_w_attn_branch, m_w_ssm_branch, m_w_out, m_w_ff_in, m_w_ff_out, v_norm_mix_pre, v_norm_mix_post, v_norm_mlp_pre, v_norm_mlp_post, v_w_in, v_rel_bias, v_sinks, v_lam_re, v_lam_im, v_log_dt, v_b_re, v_b_im, v_c_re, v_c_im, v_d_skip, v_w_glu, v_w_attn_branch, v_w_ssm_branch, v_w_out, v_w_ff_in, v_w_ff_out):
    env = dict(locals())
    w = {n: env[n] for n in WEIGHTS}
    m = {n: env["m_" + n] for n in WEIGHTS}
    v = {n: env["v_" + n] for n in WEIGHTS}
    seq = x.shape[1]
    tb = min(512, seq)

    small = {n: w[n] for n in ('norm_mix_pre', 'norm_mix_post', 'norm_mlp_pre', 'norm_mlp_post', 'rel_bias')}
    small.update({n: w[n][0] for n in ('sinks', 'lam_re', 'lam_im', 'log_dt', 'b_re', 'b_im', 'c_re', 'c_im')})
    small['d_skip'] = w['d_skip']
    shard = lambda t, n: t[n][0].T if n == 'w_in' else t[n][0]
    unshard = lambda a, n: (a.T if n == 'w_in' else a)[None]
    _, dx, small_g, pending = _local_step(
        x[0], loss_target[0], small, {n: _bf(shard(w, n)) for n in BIG}, tb, True)

    loss = small_g[LOSS_ROW, 0]

    grads, deltas, new_m, new_v = {}, {}, {}, {}

    def adam(n, partials, after=None):
        outs = _adam_pair("adam_" + n, (shard(w, n), *partials, shard(m, n), shard(v, n)), after)
        grads[n], deltas[n], new_m[n], new_v[n] = [unshard(a, n) for a in outs]
        return outs[3]

    mix = ("w_glu", "w_attn_branch", "w_ssm_branch", "w_out")
    in_fl = _scatter_off("scatter_w_in_off", [pending["b_in"]], pending["b_in"])
    sib_mix = _exchange_alone("swap_mix", _swap_sibling(pending["p_mix"]))
    last = None
    for n, partials in [(n, pending[n]) for n in ("w_ff_in", "w_ff_out")] + list(zip(mix, zip(pending["p_mix"], sib_mix))):
        last = adam(n, partials, in_fl.token)
    (r_in,) = _land("scatter_w_in_land", in_fl, last)
    p_in = _sum4("sum_w_in", pending["d_in"], r_in, pending["me"])
    (s_in,) = _exchange_alone("swap_w_in", _swap_sibling([p_in]))
    adam("w_in", (p_in, s_in))

    minor = lambda t, n: jnp.swapaxes(t, -1, -2) if n in STATE_MINOR else t
    g_small = _unpack(small_g, {n: w[n].shape for n in SMALL})
    outs = _adam_small([minor(w[n], n) for n in SMALL], [g_small[n] for n in SMALL],
                       [minor(m[n], n) for n in SMALL], [minor(v[n], n) for n in SMALL])
    grads.update({n: minor(g_small[n], n) for n in SMALL})
    for k, dst in enumerate((deltas, new_m, new_v)):
        dst.update({n: minor(a, n) for n, a in zip(SMALL, outs[k * len(SMALL):(k + 1) * len(SMALL)])})

    return (loss, dx[None], *[grads[n] for n in WEIGHTS], *[deltas[n] for n in WEIGHTS],
            *[new_m[n] for n in WEIGHTS], *[new_v[n] for n in WEIGHTS])
```

```python
import functools
import math

import numpy as np
import jax
import jax.numpy as jnp
from jax import lax
from jax.experimental import pallas as pl
from jax.experimental.pallas import tpu as pltpu

F32 = jnp.float32
BF16 = jnp.bfloat16

D_MODEL = 1024
N_HEADS = 8
N_KV = 2
Q_GROUP = 4
HEAD_DIM = 64
ATTN_W = 512
KV_W = 128
BLOCK = 128
N_BUCKETS = 32
MAX_DISTANCE = 128
NEG_INF = -1e30
SSM_W = 512
SSM_GROUP = 16
SSM_GROUPS = 32
SSM_STATE = 64
N_SUPER = 4
GROUPS_PER_SUPER = SSM_GROUPS // N_SUPER
SUPER_IN = GROUPS_PER_SUPER * SSM_GROUP
SUPER_HALF = GROUPS_PER_SUPER * SSM_STATE
SUPER_W = 2 * SUPER_HALF
STATE_COLS = N_SUPER * SUPER_W
D_FF = 4096
FF_CHUNKS = 4
IN_W = 3328
SPLITS = (0, 512, 640, 768, 1280, 2304, 3328)
RMS_EPS = 1e-6
N_CHIPS = 4
SUBLANES = 8
LANES = 128
STATE_TILES = STATE_COLS // LANES
SUPER_TILES = SUPER_W // LANES

ADAM_LR = 0.001
ADAM_B1 = 0.9
ADAM_B2 = 0.999
ADAM_EPS = 1e-08
ADAM_WD = 0.01
ADAM_STEP = 10

VMEM_BIG = 56 * 1024 * 1024
SDS = jax.ShapeDtypeStruct
MESH_ID = pl.DeviceIdType.MESH
ANY = pl.BlockSpec(memory_space=pl.ANY)


def _bf(x):
    return x.astype(BF16)


def _mm(a, b):
    return jnp.dot(a, b, preferred_element_type=F32)


def _mm_nt(a, b):
    return lax.dot_general(a, b, (((1,), (1,)), ((), ())), preferred_element_type=F32)


def _mm_tn(a, b):
    return lax.dot_general(a, b, (((0,), (0,)), ((), ())), preferred_element_type=F32)


def _sig(x):
    return 1.0 / (1.0 + jnp.exp(-x))


def _rms(x, g):
    r = lax.rsqrt(jnp.mean(x * x, axis=-1, keepdims=True) + RMS_EPS)
    xh = x * r
    return xh * g, xh, r


def _rms_bwd(dout, xh, r, g):
    dg = jnp.sum(dout * xh, axis=0, keepdims=True)
    dxh = dout * g
    dx = r * (dxh - xh * jnp.mean(dxh * xh, axis=-1, keepdims=True))
    return dx, dg


_GELU_C = math.sqrt(2.0 / math.pi)


def _gelu_and_grad(x):
    x2 = x * x
    inner = _GELU_C * (x + 0.044715 * (x2 * x))
    t = jnp.tanh(inner)
    y = 0.5 * x * (1.0 + t)
    dy = 0.5 * (1.0 + t) + 0.5 * x * (1.0 - t * t) * (_GELU_C * (1.0 + 3.0 * 0.044715 * x2))
    return y, dy


def _zero_map(nd, *_):
    return (0,) * nd


def _params(n_axes, vmem=None):
    return pltpu.CompilerParams(dimension_semantics=("arbitrary",) * n_axes, vmem_limit_bytes=vmem)


class _Exchange:
    def __init__(self, ins, outs, sems, start, wait):
        self.ins, self.outs, self.sems, self.start, self.wait = list(ins), list(outs), list(sems), start, wait


def _fused_call(name, body, grid, in_specs, out_specs, out_shape, scratch, args, exchange, params):
    n_in, n_out, n_scr = len(in_specs), len(out_specs), len(scratch)
    if exchange is None:
        fn = body
    else:
        ex = exchange
        n_xi, n_xo = len(ex.ins), len(ex.outs)

        def fn(*refs):
            at = 0
            parts = []
            for n in (n_in, n_xi, n_out, n_xo, n_scr, len(ex.sems)):
                parts.append(refs[at:at + n])
                at += n
            ins, x_in, outs, x_out, scr, x_sem = parts
            ids = [pl.program_id(a) for a in range(len(grid))]
            first = functools.reduce(jnp.logical_and, [i == 0 for i in ids])
            last = functools.reduce(jnp.logical_and, [i == g - 1 for i, g in zip(ids, grid)])

            @pl.when(first)
            def _():
                ex.start(x_in, x_out, x_sem)

            body(*ins, *outs, *scr)

            @pl.when(last)
            def _():
                ex.wait(x_in, x_out, x_sem)

        in_specs = list(in_specs) + [ANY] * n_xi
        out_specs = list(out_specs) + [ANY] * n_xo
        out_shape = list(out_shape) + ex.outs
        scratch = list(scratch) + ex.sems
        args = list(args) + ex.ins
    return pl.pallas_call(fn, grid=grid, in_specs=in_specs, out_specs=out_specs, out_shape=out_shape,
                          scratch_shapes=list(scratch), name=name, compiler_params=params)(*args)


def _exchange_alone(name, ex):
    def body(*refs):
        n_xi, n_xo = len(ex.ins), len(ex.outs)
        x_in, x_out, x_sem = refs[:n_xi], refs[n_xi:n_xi + n_xo], refs[n_xi + n_xo:]
        ex.start(x_in, x_out, x_sem)
        ex.wait(x_in, x_out, x_sem)

    return pl.pallas_call(body, in_specs=[ANY] * len(ex.ins), out_specs=[ANY] * len(ex.outs), out_shape=ex.outs,
                          scratch_shapes=ex.sems, name=name)(*ex.ins)


def _rowcall(name, body, seq, tb, rows, consts, row_outs, acc_outs, scratch=(), reverse=False, vmem=None,
             exchange=None):
    nb = seq // tb
    rmap = (lambda i: (nb - 1 - i, 0)) if reverse else (lambda i: (i, 0))
    tmap = lambda i: (0,) + rmap(i)

    def row_spec(width):
        if isinstance(width, tuple):
            return pl.BlockSpec((width[0], tb, width[1]), tmap)
        return pl.BlockSpec((tb, width), rmap)

    def row_shape(width):
        return (width[0], seq, width[1]) if isinstance(width, tuple) else (seq, width)

    in_specs = [row_spec(a.shape[1] if a.ndim == 2 else (a.shape[0], a.shape[2])) for a in rows]
    in_specs += [pl.BlockSpec(a.shape, functools.partial(_zero_map, a.ndim), pipeline_mode=pl.Buffered(1))
                 for a in consts]
    out_specs = [row_spec(c) for c, _ in row_outs] + [ANY] * len(acc_outs)
    out_shape = [SDS(row_shape(c), dt) for c, dt in row_outs] + [SDS(s, dt) for s, dt in acc_outs]
    n_main = len(rows) + len(consts) + len(row_outs)
    n_acc = len(acc_outs)

    def fn(*refs):
        main, acc_hbm, rest = refs[:n_main], refs[n_main:n_main + n_acc], refs[n_main + n_acc:]
        acc_vmem, own = rest[:n_acc], rest[n_acc:]
        body(*main, *acc_vmem, *own)

        @pl.when(pl.program_id(0) == nb - 1)
        def _():
            for src, dst in zip(acc_vmem, acc_hbm):
                pltpu.sync_copy(src, dst)

    buffers = [pltpu.VMEM(s, dt) for s, dt in acc_outs] + list(scratch)
    return _fused_call(name, fn if acc_outs else body, (nb,), in_specs, out_specs, out_shape, buffers,
                       [*rows, *consts], exchange, _params(1, vmem))


def _inproj_fwd(x, g1, w_in, tb, exchange=None):
    seq = x.shape[0]

    def body(x_ref, g_ref, w_ref, h_ref, q_ref, k_ref, v_ref, u_ref, ga_ref, gs_ref):
        h, _, _ = _rms(x_ref[...], g_ref[...])
        hb = _bf(h)
        h_ref[...] = hb
        pj = _mm_nt(hb, w_ref[...])
        q_ref[...] = _bf(pj[:, SPLITS[0]:SPLITS[1]])
        k_ref[...] = _bf(pj[:, SPLITS[1]:SPLITS[2]])
        v_ref[...] = _bf(pj[:, SPLITS[2]:SPLITS[3]])
        u_ref[...] = pj[:, SPLITS[3]:SPLITS[4]]
        ga_ref[...] = pj[:, SPLITS[4]:SPLITS[5]]
        gs_ref[...] = pj[:, SPLITS[5]:SPLITS[6]]

    return _rowcall("inproj_fwd", body, seq, tb, [x], [g1, w_in],
                    [(D_MODEL, BF16), (ATTN_W, BF16), (KV_W, BF16), (KV_W, BF16), (SSM_W, F32),
                     (D_MODEL, F32), (D_MODEL, F32)], [], vmem=VMEM_BIG, exchange=exchange)


def _inproj_bwd(x, dx2, dq, dk, dv, du, dga, dgs, g1, w_in, tb, exchange=None):
    seq = x.shape[0]

    def body(x_ref, dx2_ref, dq_ref, dk_ref, dv_ref, du_ref, dga_ref, dgs_ref, g_ref, w_ref,
             dx_ref, dpj_ref, dg_ref):
        @pl.when(pl.program_id(0) == 0)
        def _():
            dg_ref[...] = jnp.zeros_like(dg_ref)

        dpj = jnp.concatenate([dq_ref[...], dk_ref[...], dv_ref[...], _bf(du_ref[...]),
                               dga_ref[...], dgs_ref[...]], axis=1)
        dpj_ref[...] = dpj
        dh = _mm(dpj, w_ref[...])
        g = g_ref[...]
        _, xh, r = _rms(x_ref[...], g)
        dxn, dg = _rms_bwd(dh, xh, r, g)
        dx_ref[...] = dx2_ref[...] + dxn
        dg_ref[...] += dg

    return _rowcall("inproj_bwd", body, seq, tb, [x, dx2, dq, dk, dv, du, dga, dgs], [g1, w_in],
                    [(D_MODEL, F32), (IN_W, BF16)], [((1, D_MODEL), F32)], vmem=VMEM_BIG, exchange=exchange)


def _bucket_table():
    qi = np.arange(BLOCK)[:, None]
    kj = np.arange(2 * BLOCK)[None, :]
    dist = qi + BLOCK - kj
    max_exact = N_BUCKETS // 2
    d = np.maximum(dist, 0)
    df = np.maximum(d, 1).astype(np.float32)
    large = max_exact + (np.log(df / np.float32(max_exact)) / np.float32(math.log(MAX_DISTANCE / max_exact))
                         * np.float32(N_BUCKETS - max_exact)).astype(np.int32)
    large = np.minimum(large, N_BUCKETS - 1)
    bucket = np.where(d < max_exact, d, large)
    valid = (dist >= 0) & (dist < BLOCK)
    return np.where(valid, bucket, -1).astype(np.int32)


def _bias_table(rel_bias, bucket):
    def body(rb_ref, bk_ref, o_ref):
        bk = bk_ref[...]
        has_prev = lax.broadcasted_iota(jnp.int32, bk.shape, 1) >= BLOCK
        for h in range(N_HEADS):
            kh, j, par = h // Q_GROUP, (h // 2) % 2, h % 2
            acc = jnp.full((BLOCK, 2 * BLOCK), NEG_INF, F32)
            for b in range(N_BUCKETS):
                acc = jnp.where(bk == b, rb_ref[b, h], acc)
            o_ref[0, kh, par, :, j * BLOCK:(j + 1) * BLOCK] = jnp.where(has_prev, acc, NEG_INF).T
            o_ref[1, kh, par, :, j * BLOCK:(j + 1) * BLOCK] = acc.T

    return pl.pallas_call(
        body, out_shape=SDS((2, N_KV, 2, 2 * BLOCK, 2 * BLOCK), F32),
        in_specs=[pl.BlockSpec(memory_space=pltpu.SMEM), pl.BlockSpec(memory_space=pltpu.VMEM)],
        out_specs=pl.BlockSpec(memory_space=pltpu.VMEM), name="bias_table",
    )(rel_bias, bucket)


def _bias_grad(dbias, bucket):
    def body(db_ref, bk_ref, o_ref):
        bk = bk_ref[...]
        for h in range(N_HEADS):
            kh, j, par = h // Q_GROUP, (h // 2) % 2, h % 2
            db = db_ref[kh, par, :, j * BLOCK:(j + 1) * BLOCK].T
            for b in range(N_BUCKETS):
                o_ref[b, h] = jnp.sum(jnp.where(bk == b, db, 0.0))

    return pl.pallas_call(
        body, out_shape=SDS((N_BUCKETS, N_HEADS), F32),
        in_specs=[pl.BlockSpec(memory_space=pltpu.VMEM), pl.BlockSpec(memory_space=pltpu.VMEM)],
        out_specs=pl.BlockSpec(memory_space=pltpu.SMEM), name="bias_grad",
    )(dbias, bucket)


TILE = 2 * HEAD_DIM


def _pair_layout(t):
    lead = t.shape[:-3]
    t = t.reshape(lead + (N_KV, 2, 2) + t.shape[-2:])
    nl = len(lead)
    t = jnp.transpose(t, tuple(range(nl)) + (nl, nl + 2, nl + 1, nl + 3, nl + 4))
    return t.reshape(lead + (N_KV, 2, 2 * BLOCK, t.shape[-1]))


def _pair_unlayout(t):
    t = t.reshape(N_KV, 2, 2, BLOCK, t.shape[-1]).transpose(0, 2, 1, 3, 4)
    return t.reshape(N_HEADS, BLOCK, t.shape[-1])


def _halves(t):
    tf = t.astype(F32)
    low = lax.broadcasted_iota(jnp.int32, tf.shape, 1) < HEAD_DIM
    swapped = pltpu.roll(tf, HEAD_DIM, 1)
    zero = jnp.zeros_like(tf)
    return ((_bf(jnp.where(low, tf, zero)), _bf(jnp.where(low, zero, swapped))),
            (_bf(jnp.where(low, swapped, zero)), _bf(jnp.where(low, zero, tf))))


def _fold_halves(even, odd):
    low = lax.broadcasted_iota(jnp.int32, even.shape, 1) < HEAD_DIM
    comb = jnp.where(low, even, odd)
    return comb + pltpu.roll(comb, HEAD_DIM, 1)


def _tile_rows(ref, kh):
    return jnp.concatenate([ref[:, (2 * kh) * TILE:(2 * kh + 1) * TILE],
                            ref[:, (2 * kh + 1) * TILE:(2 * kh + 2) * TILE]], axis=0)


def _halves_t(t):
    tt = t.astype(F32).T
    top = lax.broadcasted_iota(jnp.int32, tt.shape, 0) < HEAD_DIM
    swapped = jnp.concatenate([tt[HEAD_DIM:], tt[:HEAD_DIM]], axis=0)
    zero = jnp.zeros_like(tt)
    return ((_bf(jnp.where(top, tt, zero)), _bf(jnp.where(top, zero, swapped))),
            (_bf(jnp.where(top, swapped, zero)), _bf(jnp.where(top, zero, tt))))


def _attn_probs(km, qk, bias, sink):
    lg = _mm_nt(km, qk) * (HEAD_DIM ** -0.5) + bias
    m = jnp.maximum(jnp.max(lg, axis=0, keepdims=True), sink)
    p = jnp.exp(lg - m)
    es = jnp.exp(sink - m)
    inv = 1.0 / (jnp.sum(p, axis=0, keepdims=True) + es)
    return p * inv, es * inv


def _attn_fwd(q, k, v, bias, sink_rows, exchange=None):
    seq = q.shape[0]
    nblk = seq // BLOCK

    def body(q_ref, kp_ref, kc_ref, vp_ref, vc_ref, b_ref, s_ref, o_ref):
        which = jnp.minimum(pl.program_id(0), 1)
        kms = _halves(jnp.concatenate([kp_ref[...], kc_ref[...]], axis=0))
        vts = _halves_t(jnp.concatenate([vp_ref[...], vc_ref[...]], axis=0))
        for kh in range(N_KV):
            qk = _tile_rows(q_ref, kh)
            acc = jnp.zeros((TILE, 2 * BLOCK), F32)
            for par in range(2):
                pr, _ = _attn_probs(kms[kh][par], qk, b_ref[which, kh, par], s_ref[kh, par])
                acc = acc + _mm(vts[kh][par], _bf(pr))
            acc = acc.T
            o_ref[:, (2 * kh) * TILE:(2 * kh + 1) * TILE] = _bf(acc[:BLOCK])
            o_ref[:, (2 * kh + 1) * TILE:(2 * kh + 2) * TILE] = _bf(acc[BLOCK:])

    cur = lambda n: (n, 0)
    prev = lambda n: (jnp.maximum(n - 1, 0), 0)
    return _fused_call(
        "attn_fwd", body, (nblk,),
        [pl.BlockSpec((BLOCK, ATTN_W), cur),
         pl.BlockSpec((BLOCK, KV_W), prev), pl.BlockSpec((BLOCK, KV_W), cur),
         pl.BlockSpec((BLOCK, KV_W), prev), pl.BlockSpec((BLOCK, KV_W), cur),
         pl.BlockSpec(bias.shape, functools.partial(_zero_map, bias.ndim)),
         pl.BlockSpec(sink_rows.shape, functools.partial(_zero_map, sink_rows.ndim))],
        [pl.BlockSpec((BLOCK, ATTN_W), cur)], [SDS((seq, ATTN_W), BF16)], [],
        [q, k, k, v, v, bias, sink_rows], exchange, _params(1))


def _attn_bwd(q, k, v, d_out, bias, sink_rows, exchange=None):
    seq = q.shape[0]
    nblk = seq // BLOCK

    def body(q_ref, kp_ref, kc_ref, vp_ref, vc_ref, do_ref, b_ref, s_ref,
             dq_ref, dk_ref, dv_ref, db_ref, ds_ref, ck_ref, cv_ref):
        n = pl.program_id(0)

        @pl.when(n == 0)
        def _():
            db_ref[...] = jnp.zeros_like(db_ref)
            ds_ref[...] = jnp.zeros_like(ds_ref)
            ck_ref[...] = jnp.zeros_like(ck_ref)
            cv_ref[...] = jnp.zeros_like(cv_ref)

        @pl.when(n < nblk)
        def _():
            which = jnp.minimum(n, 1)
            scale = HEAD_DIM ** -0.5
            kcat = jnp.concatenate([kp_ref[...], kc_ref[...]], axis=0)
            kms = _halves(kcat)
            kts = _halves_t(kcat)
            vms = _halves(jnp.concatenate([vp_ref[...], vc_ref[...]], axis=0))
            dks, dvs = [], []
            for kh in range(N_KV):
                qk = _tile_rows(q_ref, kh)
                dok = _tile_rows(do_ref, kh)
                dq = jnp.zeros((TILE, 2 * BLOCK), F32)
                dkp, dvp = [], []
                for par in range(2):
                    pr, ps = _attn_probs(kms[kh][par], qk, b_ref[which, kh, par], s_ref[kh, par])
                    dp = _mm_nt(vms[kh][par], dok)
                    rs = jnp.sum(pr * dp, axis=0, keepdims=True)
                    dlg = pr * (dp - rs)
                    ds_ref[kh, par] += -ps * rs
                    db_ref[kh, par] += dlg
                    dlb = _bf(dlg)
                    dq = dq + _mm(kts[kh][par], dlb)
                    dkp.append(_mm(dlb, qk))
                    dvp.append(_mm(_bf(pr), dok))
                dq = _bf((dq * scale).T)
                dq_ref[:, (2 * kh) * TILE:(2 * kh + 1) * TILE] = dq[:BLOCK]
                dq_ref[:, (2 * kh + 1) * TILE:(2 * kh + 2) * TILE] = dq[BLOCK:]
                dks.append(_fold_halves(*dkp))
                dvs.append(_fold_halves(*dvp))
            low = lax.broadcasted_iota(jnp.int32, (2 * BLOCK, TILE), 1) < HEAD_DIM
            dkk = jnp.where(low, dks[0], dks[1]) * scale
            dvv = jnp.where(low, dvs[0], dvs[1])
            dk_ref[...] = _bf(ck_ref[...] + dkk[:BLOCK])
            ck_ref[...] = dkk[BLOCK:]
            dv_ref[...] = _bf(cv_ref[...] + dvv[:BLOCK])
            cv_ref[...] = dvv[BLOCK:]

        @pl.when(n == nblk)
        def _():
            dk_ref[...] = _bf(ck_ref[...])
            dv_ref[...] = _bf(cv_ref[...])

    cur = lambda n: (jnp.minimum(n, nblk - 1), 0)
    prev = lambda n: (jnp.maximum(jnp.minimum(n, nblk - 1) - 1, 0), 0)
    late = lambda n: (jnp.maximum(n - 1, 0), 0)
    kv_spec = lambda m: pl.BlockSpec((BLOCK, KV_W), m)
    acc_b = pl.BlockSpec(bias.shape[1:], functools.partial(_zero_map, bias.ndim - 1))
    acc_s = pl.BlockSpec(sink_rows.shape, functools.partial(_zero_map, sink_rows.ndim))
    return _fused_call(
        "attn_bwd", body, (nblk + 1,),
        [pl.BlockSpec((BLOCK, ATTN_W), cur), kv_spec(prev), kv_spec(cur), kv_spec(prev), kv_spec(cur),
         pl.BlockSpec((BLOCK, ATTN_W), cur),
         pl.BlockSpec(bias.shape, functools.partial(_zero_map, bias.ndim)), acc_s],
        [pl.BlockSpec((BLOCK, ATTN_W), cur), kv_spec(late), kv_spec(late), acc_b, acc_s],
        [SDS((seq, ATTN_W), BF16), SDS((seq, KV_W), BF16), SDS((seq, KV_W), BF16),
         SDS(bias.shape[1:], F32), SDS(sink_rows.shape, F32)],
        [pltpu.VMEM((BLOCK, KV_W), F32), pltpu.VMEM((BLOCK, KV_W), F32)],
        [q, k, k, v, v, d_out, bias, sink_rows], exchange, _params(1))


def _ssm_discretize(lam_re, lam_im, log_dt, b_re, b_im):
    dt = jnp.exp(log_dt)[:, None]
    mag = jnp.exp(lam_re * dt)
    ab_re = mag * jnp.cos(lam_im * dt)
    ab_im = mag * jnp.sin(lam_im * dt)
    nr = ab_re - 1.0
    den = lam_re * lam_re + lam_im * lam_im
    f_re = (nr * lam_re + ab_im * lam_im) / den
    f_im = (ab_im * lam_re - nr * lam_im) / den
    bb_re = f_re[..., None] * b_re - f_im[..., None] * b_im
    bb_im = f_re[..., None] * b_im + f_im[..., None] * b_re
    return ab_re, ab_im, bb_re, bb_im


def _state_layout(re, im):
    z = jnp.stack([re, im]).reshape(2, N_SUPER, GROUPS_PER_SUPER, SSM_STATE)
    return z.transpose(1, 0, 2, 3).reshape(STATE_COLS)


def _state_unlayout(vec):
    z = vec.reshape(N_SUPER, 2, GROUPS_PER_SUPER, SSM_STATE).transpose(1, 0, 2, 3)
    z = z.reshape(2, SSM_GROUPS, SSM_STATE)
    return z[0], z[1]


SEG = 4
WINDOW = SEG * SUBLANES


def _scan_tables(ab_re, ab_im):
    pw = [None, (ab_re, ab_im)]
    for _ in range(2, WINDOW + 1):
        pr, pi_ = pw[-1]
        pw.append((pr * ab_re - pi_ * ab_im, pr * ab_im + pi_ * ab_re))
    rows = np.arange(SUBLANES)[:, None]
    ones = np.ones((SUBLANES, 1), np.float32)
    conj = lambda p: (p[0], -p[1])
    fwd, bwd = [], []
    for shift in (1, 2, 4):
        fwd.append(_state_layout(*pw[SEG * shift])[None, :] * (rows >= shift).astype(np.float32))
        bwd.append(_state_layout(*conj(pw[SEG * shift]))[None, :] * (rows < SUBLANES - shift).astype(np.float32))
    fwd.append(jnp.stack([_state_layout(*pw[SEG * (r + 1)]) for r in range(SUBLANES)]))
    bwd.append(jnp.stack([_state_layout(*conj(pw[SEG * (SUBLANES - r)])) for r in range(SUBLANES)]))
    for k in range(1, SEG):
        fwd.append(_state_layout(*pw[k])[None, :] * ones)
        bwd.append(_state_layout(*conj(pw[k]))[None, :] * ones)
    return jnp.stack(fwd), jnp.stack(bwd)


_EYE = np.eye(GROUPS_PER_SUPER, dtype=np.float32)


def _b_matrix(bb_re, bb_im):
    bb = jnp.stack([bb_re, bb_im]).reshape(2, N_SUPER, GROUPS_PER_SUPER, SSM_STATE, SSM_GROUP)
    m = jnp.einsum('rsgpc,gh->sgcrhp', bb, _EYE)
    return m.reshape(N_SUPER, SUPER_IN, SUPER_W)


def _b_matrix_grad(dm):
    d = dm.reshape(N_SUPER, GROUPS_PER_SUPER, SSM_GROUP, 2, GROUPS_PER_SUPER, SSM_STATE)
    d = jnp.sum(d * _EYE[None, :, None, None, :, None], axis=4)
    d = d.transpose(3, 0, 1, 4, 2).reshape(2, SSM_GROUPS, SSM_STATE, SSM_GROUP)
    return d[0], d[1]


def _c_matrix(c_re, c_im):
    cc = jnp.stack([c_re, -c_im]).reshape(2, N_SUPER, GROUPS_PER_SUPER, SSM_GROUP, SSM_STATE)
    m = jnp.einsum('rsgcp,gh->srgphc', cc, _EYE)
    return m.reshape(N_SUPER, SUPER_W, SUPER_IN)


def _c_matrix_grad(dm):
    d = dm.reshape(N_SUPER, 2, GROUPS_PER_SUPER, SSM_STATE, GROUPS_PER_SUPER, SSM_GROUP)
    d = jnp.sum(d * _EYE[None, None, :, None, :, None], axis=4)
    d = d.transpose(1, 0, 2, 4, 3).reshape(2, SSM_GROUPS, SSM_GROUP, SSM_STATE)
    return d[0], -d[1]


def _cmul_add(xr, xi, ar, ai, sr, si):
    return xr + ar * sr - ai * si, xi + ar * si + ai * sr


def _scan_rows(buf_ref, tab_ref, carry_ref, n_windows, reverse, h_ref=None, da_ref=None):
    order = list(range(SEG - 1, -1, -1)) if reverse else list(range(SEG))
    near = SUBLANES - 1 if reverse else 0
    far = 0 if reverse else SUBLANES - 1
    s_in = SUBLANES - 1 if reverse else 1
    lanes = lambda tile: pl.ds(tile * LANES, LANES)

    def window(w0, tile_re, tile_im, c_re, c_im, acc):
        rows = lambda t: pl.ds(w0 + t, SUBLANES, stride=SEG)
        get = lambda ref, t: (ref.at[tile_re][rows(t), :], ref.at[tile_im][rows(t), :])
        tab = lambda k: (tab_ref[k, :, lanes(tile_re)], tab_ref[k, :, lanes(tile_im)])

        def put(t, xr, xi):
            buf_ref.at[tile_re][rows(t), :] = xr
            buf_ref.at[tile_im][rows(t), :] = xi

        a1 = tab(4)
        er, ei = get(buf_ref, order[0])
        for t in order[1:]:
            er, ei = _cmul_add(*get(buf_ref, t), *a1, er, ei)
            if t != order[-1]:
                put(t, er, ei)
        for k, shift in enumerate((1, 2, 4)):
            s = (SUBLANES - shift) if reverse else shift
            er, ei = _cmul_add(er, ei, *tab(k), pltpu.roll(er, s, 0), pltpu.roll(ei, s, 0))
        er, ei = _cmul_add(er, ei, *tab(3), c_re, c_im)
        put(order[-1], er, ei)
        sub = lax.broadcasted_iota(jnp.int32, er.shape, 0)
        in_re = jnp.where(sub == near, c_re, pltpu.roll(er, s_in, 0))
        in_im = jnp.where(sub == near, c_im, pltpu.roll(ei, s_in, 0))
        true = {order[-1]: (er, ei)}
        for idx, t in enumerate(order[:-1]):
            true[t] = _cmul_add(*get(buf_ref, t), *tab(4 + idx), in_re, in_im)
            put(t, *true[t])
        carry = (jnp.broadcast_to(er[far:far + 1], er.shape), jnp.broadcast_to(ei[far:far + 1], ei.shape))
        if acc is None:
            return carry, None
        acc_re, acc_im = acc
        for t in range(SEG):
            if t + 1 < SEG:
                gr, gim = true[t + 1]
            else:
                gr = jnp.where(sub == SUBLANES - 1, c_re, pltpu.roll(true[0][0], SUBLANES - 1, 0))
                gim = jnp.where(sub == SUBLANES - 1, c_im, pltpu.roll(true[0][1], SUBLANES - 1, 0))
            hr, hi = get(h_ref, t)
            acc_re = acc_re + gr * hr + gim * hi
            acc_im = acc_im + gim * hr - gr * hi
        return carry, (acc_re, acc_im)

    half = SUPER_HALF // LANES
    per = 2 if h_ref is None else 4
    for sb in range(N_SUPER):
        pairs = [(2 * half * sb + j, 2 * half * sb + half + j) for j in range(half)]

        def step(wi, state, pairs=pairs):
            w = (n_windows - 1 - wi) if reverse else wi
            w0 = pl.multiple_of(w * WINDOW, WINDOW)
            out = []
            for j, (tile_re, tile_im) in enumerate(pairs):
                mine = state[per * j:per * (j + 1)]
                carry, acc = window(w0, tile_re, tile_im, mine[0], mine[1], mine[2:] or None)
                out += list(carry) + list(acc or ())
            return tuple(out)

        init = []
        for tile_re, tile_im in pairs:
            init += [carry_ref[:, lanes(tile_re)], carry_ref[:, lanes(tile_im)]]
            if h_ref is not None:
                init += [da_ref[:, lanes(tile_re)], da_ref[:, lanes(tile_im)]]
        fin = lax.fori_loop(0, n_windows, step, tuple(init))
        for j, (tile_re, tile_im) in enumerate(pairs):
            carry_ref[:, lanes(tile_re)] = fin[per * j]
            carry_ref[:, lanes(tile_im)] = fin[per * j + 1]
            if h_ref is not None:
                da_ref[:, lanes(tile_re)] = fin[per * j + 2]
                da_ref[:, lanes(tile_im)] = fin[per * j + 3]


def _put_tiles(ref, sb, block):
    for j in range(SUPER_TILES):
        ref[sb * SUPER_TILES + j] = block[:, j * LANES:(j + 1) * LANES]


def _get_tiles(ref, sb):
    return jnp.concatenate([ref[sb * SUPER_TILES + j] for j in range(SUPER_TILES)], axis=1)


def _ssm_fwd(u, bmat, cmat, tab, d_skip, tb, exchange=None):
    seq = u.shape[0]

    def body(u_ref, b_ref, c_ref, t_ref, d_ref, s_ref, h_ref, carry_ref):
        @pl.when(pl.program_id(0) == 0)
        def _():
            carry_ref[...] = jnp.zeros_like(carry_ref)

        u_blk = u_ref[...]
        ub = _bf(u_blk)
        for sb in range(N_SUPER):
            _put_tiles(h_ref, sb, _mm(ub[:, sb * SUPER_IN:(sb + 1) * SUPER_IN], b_ref[sb]))
        _scan_rows(h_ref, t_ref, carry_ref, tb // WINDOW, False)
        ys = [_mm(_bf(_get_tiles(h_ref, sb)), c_ref[sb]) for sb in range(N_SUPER)]
        s_ref[...] = jnp.concatenate(ys, axis=1) + d_ref[...] * u_blk

    return _rowcall("ssm_fwd", body, seq, tb, [u], [bmat, cmat, tab, d_skip],
                    [(SSM_W, F32), ((STATE_TILES, LANES), F32)], [],
                    scratch=[pltpu.VMEM((SUBLANES, STATE_COLS), F32)], vmem=VMEM_BIG, exchange=exchange)


def _ssm_bwd(ds, u, h, bmat_t, cmat_t, tab, d_skip, tb, exchange=None):
    seq = u.shape[0]

    def body(ds_ref, u_ref, h_ref, bt_ref, ct_ref, t_ref, d_ref,
             du_ref, db_ref, dc_ref, da_ref, dd_ref, g_ref, carry_ref):
        @pl.when(pl.program_id(0) == 0)
        def _():
            carry_ref[...] = jnp.zeros_like(carry_ref)
            db_ref[...] = jnp.zeros_like(db_ref)
            dc_ref[...] = jnp.zeros_like(dc_ref)
            da_ref[...] = jnp.zeros_like(da_ref)
            dd_ref[...] = jnp.zeros_like(dd_ref)

        ds_blk = ds_ref[...]
        dsb = _bf(ds_blk)
        u_blk = u_ref[...]
        ub = _bf(u_blk)
        for sb in range(N_SUPER):
            _put_tiles(g_ref, sb, _mm(dsb[:, sb * SUPER_IN:(sb + 1) * SUPER_IN], ct_ref[sb]))
        _scan_rows(g_ref, t_ref, carry_ref, tb // WINDOW, True, h_ref=h_ref, da_ref=da_ref)
        dus = []
        for sb in range(N_SUPER):
            gb = _bf(_get_tiles(g_ref, sb))
            dus.append(_mm(gb, bt_ref[sb]))
            db_ref[sb] += _mm_tn(ub[:, sb * SUPER_IN:(sb + 1) * SUPER_IN], gb)
            dc_ref[sb] += _mm_tn(_bf(_get_tiles(h_ref, sb)), dsb[:, sb * SUPER_IN:(sb + 1) * SUPER_IN])
        du_ref[...] = jnp.concatenate(dus, axis=1) + d_ref[...] * ds_blk
        dd_ref[...] += jnp.sum(ds_blk * u_blk, axis=0, keepdims=True)

    return _rowcall("ssm_bwd", body, seq, tb, [ds, u, h], [bmat_t, cmat_t, tab, d_skip],
                    [(SSM_W, F32)],
                    [((N_SUPER, SUPER_IN, SUPER_W), F32), ((N_SUPER, SUPER_W, SUPER_IN), F32),
                     ((SUBLANES, STATE_COLS), F32), ((1, SSM_W), F32)],
                    scratch=[pltpu.VMEM((STATE_TILES, tb, LANES), F32), pltpu.VMEM((SUBLANES, STATE_COLS), F32)],
                    reverse=True, vmem=VMEM_BIG, exchange=exchange)


def _merge_core(s, attb, ga, gs, wg_ref, wab_ref, wsb_ref, wout_ref):
    zg, dgelu = _gelu_and_grad(s)
    zgb = _bf(zg)
    sg = _sig(_mm(zgb, wg_ref[...]))
    z = zg * sg
    zb = _bf(z)
    ys = jnp.concatenate([_mm(zb, wsb_ref[j]) for j in range(N_CHIPS)], axis=1)
    ya = jnp.concatenate([_mm(attb, wab_ref[j]) for j in range(N_CHIPS)], axis=1)
    sa = _sig(ga)
    ss = _sig(gs)
    mgb = _bf(sa * ya + ss * ys)
    o = _mm(mgb, wout_ref[...])
    return dict(zg=zg, dgelu=dgelu, zgb=zgb, sg=sg, zb=zb, ys=ys, ya=ya, sa=sa, ss=ss, mgb=mgb, o=o)


def _merge_fwd(x, s, att, ga, gs, g2, w_glu, w_ab, w_sb, w_out, tb):
    seq = x.shape[0]

    def body(x_ref, s_ref, att_ref, ga_ref, gs_ref, g_ref, wg_ref, wab_ref, wsb_ref, wout_ref, x2_ref):
        f = _merge_core(s_ref[...], att_ref[...], ga_ref[...], gs_ref[...], wg_ref, wab_ref, wsb_ref, wout_ref)
        n, _, _ = _rms(f["o"], g_ref[...])
        x2_ref[...] = x_ref[...] + n

    return _rowcall("merge_fwd", body, seq, tb, [x, s, att, ga, gs], [g2, w_glu, w_ab, w_sb, w_out],
                    [(D_MODEL, F32)], [], vmem=VMEM_BIG)[0]


def _merge_bwd(dx2, s, att, ga, gs, g2, w_glu, w_ab, w_sb, w_out, tb, exchange=None):
    seq = s.shape[0]
    cw = D_MODEL // N_CHIPS
    last = seq // tb - 1

    def body(dx2_ref, s_ref, att_ref, ga_ref, gs_ref, g_ref, wg_ref, wab_ref, wsb_ref, wout_ref,
             ds_ref, datt_ref, dga_ref, dgs_ref, dg_ref, dwg_ref, dwab_ref, dwsb_ref, dwout_ref,
             bwg_ref, bwab_ref, bwsb_ref, bwout_ref):
        @pl.when(pl.program_id(0) == 0)
        def _():
            for r in (dg_ref, dwg_ref, dwab_ref, dwsb_ref, dwout_ref):
                r[...] = jnp.zeros_like(r)

        attb = att_ref[...]
        f = _merge_core(s_ref[...], attb, ga_ref[...], gs_ref[...], wg_ref, wab_ref, wsb_ref, wout_ref)
        g = g_ref[...]
        _, oh, r2 = _rms(f["o"], g)
        do, dg = _rms_bwd(dx2_ref[...], oh, r2, g)
        dg_ref[...] += dg
        dob = _bf(do)
        dwout_ref[...] += _mm_tn(f["mgb"], dob)
        dmg = _mm_nt(dob, wout_ref[...])
        sa, ss = f["sa"], f["ss"]
        dyab = _bf(dmg * sa)
        dysb = _bf(dmg * ss)
        dga_ref[...] = _bf(dmg * f["ya"] * sa * (1.0 - sa))
        dgs_ref[...] = _bf(dmg * f["ys"] * ss * (1.0 - ss))
        dwab = _mm_tn(attb, dyab)
        dwsb = _mm_tn(f["zb"], dysb)
        datt = jnp.zeros((tb, ATTN_W), F32)
        dz = jnp.zeros((tb, SSM_W), F32)
        for j in range(N_CHIPS):
            dwab_ref[j] += dwab[:, j * cw:(j + 1) * cw]
            dwsb_ref[j] += dwsb[:, j * cw:(j + 1) * cw]
            datt = datt + _mm_nt(dyab[:, j * cw:(j + 1) * cw], wab_ref[j])
            dz = dz + _mm_nt(dysb[:, j * cw:(j + 1) * cw], wsb_ref[j])
        datt_ref[...] = _bf(datt)
        sg, zg = f["sg"], f["zg"]
        dglb = _bf(dz * zg * sg * (1.0 - sg))
        dwg_ref[...] += _mm_tn(f["zgb"], dglb)
        dzg = dz * sg + _mm_nt(dglb, wg_ref[...])
        ds_ref[...] = dzg * f["dgelu"]

        @pl.when(pl.program_id(0) == last)
        def _():
            for dst, src in ((bwg_ref, dwg_ref), (bwab_ref, dwab_ref), (bwsb_ref, dwsb_ref), (bwout_ref, dwout_ref)):
                dst[...] = _bf(src[...])

    shapes = [w_glu.shape, w_ab.shape, w_sb.shape, w_out.shape]
    return _rowcall("merge_bwd", body, seq, tb, [dx2, s, att, ga, gs], [g2, w_glu, w_ab, w_sb, w_out],
                    [(SSM_W, F32), (ATTN_W, BF16), (D_MODEL, BF16), (D_MODEL, BF16)],
                    [((1, D_MODEL), F32)] + [(sh, F32) for sh in shapes] + [(sh, BF16) for sh in shapes],
                    vmem=VMEM_BIG, exchange=exchange)


def _mlp_fwd_loss(x2, target, g3, g4, w_ffi, w_ffo, tb):
    seq = x2.shape[0]
    n_slab = len(w_ffi)
    sw = D_FF // FF_CHUNKS // n_slab

    def body(x2_ref, t_ref, g3_ref, g4_ref, *rest):
        wi_refs, (wo_ref, dy_ref, df_ref, h_ref, loss_ref, dg_ref) = rest[:n_slab], rest[n_slab:]

        @pl.when(pl.program_id(0) == 0)
        def _():
            loss_ref[...] = jnp.zeros_like(loss_ref)
            dg_ref[...] = jnp.zeros_like(dg_ref)

        x2_blk = x2_ref[...]
        h3, _, _ = _rms(x2_blk, g3_ref[...])
        hb = _bf(h3)
        h_ref[...] = hb
        f = jnp.zeros((tb, D_MODEL), F32)
        for j in range(FF_CHUNKS):
            for k in range(n_slab):
                a = _mm(hb, wi_refs[k][j])
                f = f + _mm(_bf(jnp.square(jnp.maximum(a, 0.0))), wo_ref[j, pl.ds(k * sw, sw), :])
        g4 = g4_ref[...]
        n4, fh, r4 = _rms(f, g4)
        e = (x2_blk + n4) - t_ref[...]
        loss_ref[...] += 0.5 * jnp.sum(jnp.mean(e * e, axis=-1, keepdims=True))
        dy = e * (1.0 / D_MODEL)
        dy_ref[...] = dy
        df, dg = _rms_bwd(dy, fh, r4, g4)
        df_ref[...] = _bf(df)
        dg_ref[...] += dg

    return _rowcall("mlp_fwd_loss", body, seq, tb, [x2, target], [g3, g4, *w_ffi, w_ffo],
                    [(D_MODEL, F32), (D_MODEL, BF16), (D_MODEL, BF16)],
                    [((SUBLANES, 128), F32), ((1, D_MODEL), F32)], vmem=VMEM_BIG)


def _mlp_bwd(x2, dy, df, h3, g3, w_ffi, w_ffo, tb):
    seq = x2.shape[0]
    n_slab = len(w_ffi)
    sw = D_FF // FF_CHUNKS // n_slab

    def body(x2_ref, dy_ref, df_ref, h_ref, g3_ref, *rest):
        wi_refs, (wo_ref, dx_ref, act_ref, da_ref, dg_ref) = rest[:n_slab], rest[n_slab:]

        @pl.when(pl.program_id(0) == 0)
        def _():
            dg_ref[...] = jnp.zeros_like(dg_ref)

        hb = h_ref[...]
        dfb = df_ref[...]
        dh = jnp.zeros((tb, D_MODEL), F32)
        for j in range(FF_CHUNKS):
            for k in range(n_slab):
                cols = pl.ds((j * n_slab + k) * sw, sw)
                ra = jnp.maximum(_mm(hb, wi_refs[k][j]), 0.0)
                act_ref[:, cols] = _bf(ra * ra)
                dab = _bf(_mm_nt(dfb, wo_ref[j, pl.ds(k * sw, sw), :]) * (2.0 * ra))
                da_ref[:, cols] = dab
                dh = dh + _mm_nt(dab, wi_refs[k][j])
        g3 = g3_ref[...]
        _, xh, r3 = _rms(x2_ref[...], g3)
        dxn, dg = _rms_bwd(dh, xh, r3, g3)
        dx_ref[...] = dy_ref[...] + dxn
        dg_ref[...] += dg

    return _rowcall("mlp_bwd", body, seq, tb, [x2, dy, df, h3], [g3, *w_ffi, w_ffo],
                    [(D_MODEL, F32), (D_FF, BF16), (D_FF, BF16)], [((1, D_MODEL), F32)], vmem=VMEM_BIG)


def _matmul_tn(name, a, b, tk, tn, tl, chunk_major, exchange=None):
    seq, kdim = a.shape
    ndim = b.shape[1]
    last = seq // tl - 1

    def body(a_ref, b_ref, o_ref, ob_ref):
        @pl.when(pl.program_id(2) == 0)
        def _():
            o_ref[...] = jnp.zeros_like(o_ref)

        o_ref[...] += _mm_tn(a_ref[...], b_ref[...])

        @pl.when(pl.program_id(2) == last)
        def _():
            ob_ref[...] = _bf(o_ref[...])

    if chunk_major:
        shape = (ndim // tn, kdim, tn)
        out_spec = pl.BlockSpec((None, tk, tn), lambda k, n, l: (n, k, 0))
    else:
        shape = (kdim, ndim)
        out_spec = pl.BlockSpec((tk, tn), lambda k, n, l: (k, n))
    return _fused_call(
        name, body, (kdim // tk, ndim // tn, seq // tl),
        [pl.BlockSpec((tl, tk), lambda k, n, l: (l, k)), pl.BlockSpec((tl, tn), lambda k, n, l: (l, n))],
        [out_spec, out_spec], [SDS(shape, F32), SDS(shape, BF16)], [], [a, b], exchange, _params(3, VMEM_BIG))


def _ew_call(name, fn, ins, n_out, after=None):
    rows, cols = ins[0].shape
    tr = rows
    while tr * cols * 4 > min(1 << 20, (9 << 20) // (len(ins) + n_out)) and tr % 16 == 0:
        tr //= 2
    spec = pl.BlockSpec((tr, cols), lambda i: (i, 0))
    extra = [] if after is None else [after]

    def body(*refs):
        outs = fn(*[r[...] for r in refs[:len(ins)]])
        for r, o in zip(refs[len(ins) + len(extra):], outs):
            r[...] = o

    return pl.pallas_call(
        body, grid=(rows // tr,), in_specs=[spec] * len(ins) + [ANY] * len(extra), out_specs=[spec] * n_out,
        out_shape=[SDS((rows, cols), F32)] * n_out, name=name, compiler_params=_params(1))(*ins, *extra)


def _adam_math(w, g, m, v):
    m2 = ADAM_B1 * m + (1.0 - ADAM_B1) * g
    v2 = ADAM_B2 * v + (1.0 - ADAM_B2) * (g * g)
    m_hat = m2 / (1.0 - ADAM_B1 ** ADAM_STEP)
    v_hat = v2 / (1.0 - ADAM_B2 ** ADAM_STEP)
    delta = -ADAM_LR * (m_hat / (jnp.sqrt(v_hat) + ADAM_EPS) + ADAM_WD * w)
    return delta, m2, v2


def _sum4(name, own, recv, idx):
    _, rows, cols = own.shape
    tr = rows
    while tr * cols * 4 > (1 << 20) and tr % 16 == 0:
        tr //= 2

    def body(idx_ref, o_ref, r0_ref, r1_ref, r2_ref, out_ref):
        out_ref[...] = ((o_ref[...] + r0_ref[...].astype(F32)) + r1_ref[...].astype(F32)) + r2_ref[...].astype(F32)

    blk = (None, tr, cols)
    grid_spec = pltpu.PrefetchScalarGridSpec(
        num_scalar_prefetch=1, grid=(rows // tr,),
        in_specs=[pl.BlockSpec(blk, lambda i, s: (s[0], i, 0)), pl.BlockSpec(blk, lambda i, s: (0, i, 0)),
                  pl.BlockSpec(blk, lambda i, s: (1, i, 0)), pl.BlockSpec(blk, lambda i, s: (2, i, 0))],
        out_specs=pl.BlockSpec((tr, cols), lambda i, s: (i, 0)))
    return pl.pallas_call(body, grid_spec=grid_spec, out_shape=SDS((rows, cols), F32), name=name,
                          compiler_params=_params(1))(jnp.reshape(idx, (1,)).astype(jnp.int32), own, recv, recv, recv)


def _adam_pair(name, item, after=None):
    def fn(w_, a, b, m_, v_):
        g = a + b
        return (g,) + _adam_math(w_, g, m_, v_)

    return _ew_call(name, fn, list(item), 4, after)


def _place():
    return lax.axis_index("x"), lax.axis_index("y"), lax.axis_index("c")


def _other_chips(x, y):
    return [(1 - x, y), (x, 1 - y), (1 - x, 1 - y)]


def _gather_chips(shards):
    n = len(shards)

    def copies(ins, outs, sems):
        send, recv, fwd_send, fwd_recv, loc = sems
        x, y, c = _place()
        me = 2 * x + y
        peers = _other_chips(x, y)
        local = [pltpu.make_async_copy(ins[a], outs[a].at[me], loc.at[a]) for a in range(n)]
        sends, recvs, passes, passed = [], [], [], []
        for a in range(n):
            half = shards[a].shape[0] // 2
            mine = pl.ds(c * half, half)
            theirs = pl.ds((1 - c) * half, half)
            for j, (px, py) in enumerate(peers):
                far = 2 * px + py
                sends.append(pltpu.make_async_remote_copy(
                    src_ref=ins[a].at[mine], dst_ref=outs[a].at[me, mine], send_sem=send.at[a, j],
                    recv_sem=recv.at[a, j], device_id=(px, py, c), device_id_type=MESH_ID))
                recvs.append(pltpu.make_async_remote_copy(
                    src_ref=ins[a].at[mine], dst_ref=outs[a].at[far, mine], send_sem=send.at[a, j],
                    recv_sem=recv.at[a, j], device_id=(px, py, c), device_id_type=MESH_ID))
                passes.append(pltpu.make_async_remote_copy(
                    src_ref=outs[a].at[far, mine], dst_ref=outs[a].at[far, mine], send_sem=fwd_send.at[a, j],
                    recv_sem=fwd_recv.at[a, j], device_id=(x, y, 1 - c), device_id_type=MESH_ID))
                passed.append(pltpu.make_async_remote_copy(
                    src_ref=outs[a].at[far, theirs], dst_ref=outs[a].at[far, theirs], send_sem=fwd_send.at[a, j],
                    recv_sem=fwd_recv.at[a, j], device_id=(x, y, 1 - c), device_id_type=MESH_ID))
        return local, sends, recvs, passes, passed

    def start(ins, outs, sems):
        local, sends, _, _, _ = copies(ins, outs, sems)
        for cp in local + sends:
            cp.start()

    def wait(ins, outs, sems):
        local, sends, recvs, passes, passed = copies(ins, outs, sems)
        for got, on in zip(recvs, passes):
            got.wait_recv()
            on.start()
        for cp in passed:
            cp.wait_recv()
        for cp in passes + sends:
            cp.wait_send()
        for cp in local:
            cp.wait()

    assert all(s.shape[0] % 32 == 0 for s in shards)
    pair = pltpu.SemaphoreType.DMA((n, 3))
    return _Exchange(shards, [SDS((N_CHIPS,) + s.shape, s.dtype) for s in shards],
                     [pair, pair, pair, pair, pltpu.SemaphoreType.DMA((n,))], start, wait)


def _scatter_chips(chunks):
    n = len(chunks)

    def copies(ins, outs, sems):
        send, recv = sems
        x, y, c = _place()
        return [pltpu.make_async_remote_copy(
            src_ref=ins[a].at[2 * px + py], dst_ref=outs[a].at[j], send_sem=send.at[a, j],
            recv_sem=recv.at[a, j], device_id=(px, py, c), device_id_type=MESH_ID)
            for a in range(n) for j, (px, py) in enumerate(_other_chips(x, y))]

    def start(ins, outs, sems):
        for cp in copies(ins, outs, sems):
            cp.start()

    def wait(ins, outs, sems):
        cps = copies(ins, outs, sems)
        for cp in cps:
            cp.wait_recv()
        for cp in cps:
            cp.wait_send()

    return _Exchange(chunks, [SDS((3,) + s.shape[1:], s.dtype) for s in chunks],
                     [pltpu.SemaphoreType.DMA((n, 3)), pltpu.SemaphoreType.DMA((n, 3))], start, wait)


HBM = pl.BlockSpec(memory_space=pltpu.HBM)
SEM = pl.BlockSpec(memory_space=pltpu.SEMAPHORE)
DATAFLOW = pltpu.SideEffectType.DATAFLOW_SIDE_EFFECTING


class _Flight:
    def __init__(self, copies, n_copies, send, recv, srcs, lands, token):
        self.copies, self.n, self.send, self.recv = copies, n_copies, send, recv
        self.srcs, self.lands, self.token = list(srcs), list(lands), token


def _take_off(name, srcs, lands, copies, n_copies, after):
    n_s, n_l = len(srcs), len(lands)

    def body(*refs):
        src, land = refs[:n_s], refs[n_s:n_s + n_l]
        send, recv = refs[n_s + n_l + 1:n_s + n_l + 3]
        for cp in copies(src, land, send, recv):
            cp.start()
        refs[-1][...] = jnp.zeros_like(refs[-1])

    mem = lambda t: pltpu.HBM(t.shape, t.dtype)
    sems = pltpu.SemaphoreType.DMA((n_copies,))
    outs = pl.pallas_call(
        body, name=name,
        out_shape=(sems, sems, *map(mem, srcs), *map(mem, lands), SDS((SUBLANES, LANES), F32)),
        in_specs=[HBM] * (n_s + n_l) + [ANY],
        out_specs=(SEM, SEM, *[HBM] * (n_s + n_l), pl.BlockSpec(memory_space=pltpu.VMEM)),
        input_output_aliases={i: 2 + i for i in range(n_s + n_l)},
        compiler_params=pltpu.CompilerParams(has_side_effects=DATAFLOW),
    )(*[pltpu.with_memory_space_constraint(t, pltpu.HBM) for t in (*srcs, *lands)], after)
    return _Flight(copies, n_copies, outs[0], outs[1], outs[2:2 + n_s], outs[2 + n_s:2 + n_s + n_l], outs[-1])


def _land(name, flight, after):
    n_s, n_l = len(flight.srcs), len(flight.lands)

    def body(*refs):
        src, land = refs[:n_s], refs[n_s:n_s + n_l]
        send, recv = refs[n_s + n_l:n_s + n_l + 2]
        for cp in flight.copies(src, land, send, recv):
            cp.wait_send()
            cp.wait_recv()

    mem = lambda t: pltpu.HBM(t.shape, t.dtype)
    outs = pl.pallas_call(
        body, name=name, out_shape=(*map(mem, flight.srcs), *map(mem, flight.lands)),
        in_specs=[HBM] * (n_s + n_l) + [SEM, SEM, ANY], out_specs=tuple([HBM] * (n_s + n_l)),
        input_output_aliases={i: i for i in range(n_s + n_l)},
        compiler_params=pltpu.CompilerParams(has_side_effects=DATAFLOW),
    )(*flight.srcs, *flight.lands, flight.send, flight.recv, after)
    return list(outs[:n_s]), list(outs[n_s:])


def _empty_like(shapes_from, lead):
    return [lax.empty((lead,) + t.shape[1:], t.dtype) for t in shapes_from]


def _scatter_off(name, chunks, after):
    def copies(src, land, send, recv):
        x, y, c = _place()
        return [pltpu.make_async_remote_copy(
            src_ref=src[a].at[2 * px + py], dst_ref=land[a].at[k], send_sem=send.at[3 * a + k],
            recv_sem=recv.at[3 * a + k], device_id=(px, py, c), device_id_type=MESH_ID)
            for a in range(len(chunks)) for k, (px, py) in enumerate(_other_chips(x, y))]

    return _take_off(name, chunks, _empty_like(chunks, 3), copies, 3 * len(chunks), after)


def _swap_off(name, arrs, after):
    def copies(src, land, send, recv):
        x, y, c = _place()
        return [pltpu.make_async_remote_copy(
            src_ref=src[a], dst_ref=land[a], send_sem=send.at[a], recv_sem=recv.at[a],
            device_id=(x, y, 1 - c), device_id_type=MESH_ID) for a in range(len(arrs))]

    return _take_off(name, arrs, [lax.empty(t.shape, t.dtype) for t in arrs], copies, len(arrs), after)


def _devices_off(name, block, after):
    me = 4 * lax.axis_index("x") + 2 * lax.axis_index("y") + lax.axis_index("c")
    land = lax.dynamic_update_index_in_dim(lax.empty((N_DEV,) + block.shape, block.dtype), block, me, 0)

    def copies(src, land, send, recv):
        x, y, c = _place()
        mine = 4 * x + 2 * y + c
        return [pltpu.make_async_remote_copy(
            src_ref=src[0], dst_ref=land[0].at[mine], send_sem=send.at[k - 1], recv_sem=recv.at[k - 1],
            device_id=(x ^ (k >> 2), y ^ ((k >> 1) & 1), c ^ (k & 1)), device_id_type=MESH_ID)
            for k in range(1, N_DEV)]

    return _take_off(name, [block], [land], copies, N_DEV - 1, after)


def _half_rows(shape, c, other=False):
    half = shape[0] // 2
    return pl.ds(((1 - c) if other else c) * half, half)


def _gather_start(name, shards, lands, after):
    n = len(shards)

    def body(*refs):
        src, land, (send, recv) = refs[:n], refs[n:2 * n], refs[2 * n + 1:2 * n + 3]
        x, y, c = _place()
        me = 2 * x + y
        for a in range(n):
            mine = _half_rows(shards[a].shape, c)
            for j, (px, py) in enumerate(_other_chips(x, y)):
                pltpu.make_async_remote_copy(
                    src_ref=src[a].at[mine], dst_ref=land[a].at[me, mine], send_sem=send.at[3 * a + j],
                    recv_sem=recv.at[3 * a + j], device_id=(px, py, c), device_id_type=MESH_ID).start()
        token = refs[-1]
        token[...] = jnp.zeros_like(token)

    mem = lambda t: pltpu.HBM(t.shape, t.dtype)
    pair = pltpu.SemaphoreType.DMA((3 * n,))
    outs = pl.pallas_call(
        body, name=name,
        out_shape=(pair, pair, *map(mem, shards), *map(mem, lands), SDS((SUBLANES, LANES), F32)),
        in_specs=[HBM] * (2 * n) + [ANY],
        out_specs=(SEM, SEM, *[HBM] * (2 * n), pl.BlockSpec(memory_space=pltpu.VMEM)),
        input_output_aliases={i: 2 + i for i in range(2 * n)},
        compiler_params=pltpu.CompilerParams(has_side_effects=DATAFLOW),
    )(*[pltpu.with_memory_space_constraint(t, pltpu.HBM) for t in (*shards, *lands)], after)
    return outs[0], outs[1], list(outs[2:2 + n]), list(outs[2 + n:2 + 2 * n]), outs[-1]


def _gather_pass(name, send, recv, shards, lands, after):
    n = len(shards)

    def body(*refs):
        src, land, (send, recv, _) = refs[:n], refs[n:2 * n], refs[2 * n:2 * n + 3]
        fsend, frecv = refs[2 * n + 3], refs[2 * n + 4]
        x, y, c = _place()
        me = 2 * x + y
        for a in range(n):
            mine = _half_rows(shards[a].shape, c)
            for j, (px, py) in enumerate(_other_chips(x, y)):
                far = 2 * px + py
                ici = pltpu.make_async_remote_copy(
                    src_ref=src[a].at[mine], dst_ref=land[a].at[far, mine], send_sem=send.at[3 * a + j],
                    recv_sem=recv.at[3 * a + j], device_id=(px, py, c), device_id_type=MESH_ID)
                ici.wait_recv()
                ici.wait_send()
                pltpu.make_async_remote_copy(
                    src_ref=land[a].at[far, mine], dst_ref=land[a].at[far, mine], send_sem=fsend.at[3 * a + j],
                    recv_sem=frecv.at[3 * a + j], device_id=(x, y, 1 - c), device_id_type=MESH_ID).start()
        token = refs[-1]
        token[...] = jnp.zeros_like(token)

    mem = lambda t: pltpu.HBM(t.shape, t.dtype)
    pair = pltpu.SemaphoreType.DMA((3 * n,))
    outs = pl.pallas_call(
        body, name=name,
        out_shape=(pair, pair, *map(mem, lands), SDS((SUBLANES, LANES), F32)),
        in_specs=[HBM] * (2 * n) + [SEM, SEM, ANY],
        out_specs=(SEM, SEM, *[HBM] * n, pl.BlockSpec(memory_space=pltpu.VMEM)),
        input_output_aliases={n + i: 2 + i for i in range(n)},
        compiler_params=pltpu.CompilerParams(has_side_effects=DATAFLOW),
    )(*shards, *lands, send, recv, after)
    return outs[0], outs[1], list(outs[2:2 + n]), outs[-1]


def _gather_wait(name, fsend, frecv, lands, after):
    n = len(lands)

    def body(*refs):
        land, (fsend, frecv, _) = refs[:n], refs[n:n + 3]
        x, y, c = _place()
        for a in range(n):
            for j, (px, py) in enumerate(_other_chips(x, y)):
                far = 2 * px + py
                mine = _half_rows(lands[a].shape[1:], c)
                theirs = _half_rows(lands[a].shape[1:], c, other=True)
                pltpu.make_async_remote_copy(
                    src_ref=land[a].at[far, mine], dst_ref=land[a].at[far, mine], send_sem=fsend.at[3 * a + j],
                    recv_sem=frecv.at[3 * a + j], device_id=(x, y, 1 - c), device_id_type=MESH_ID).wait_send()
                pltpu.make_async_remote_copy(
                    src_ref=land[a].at[far, theirs], dst_ref=land[a].at[far, theirs], send_sem=fsend.at[3 * a + j],
                    recv_sem=frecv.at[3 * a + j], device_id=(x, y, 1 - c), device_id_type=MESH_ID).wait_recv()

    mem = lambda t: pltpu.HBM(t.shape, t.dtype)
    return list(pl.pallas_call(
        body, name=name, out_shape=tuple(map(mem, lands)), in_specs=[HBM] * n + [SEM, SEM, ANY],
        out_specs=tuple([HBM] * n), input_output_aliases={i: i for i in range(n)},
        compiler_params=pltpu.CompilerParams(has_side_effects=DATAFLOW),
    )(*lands, fsend, frecv, after))


def _after(token):
    return _Exchange([token], [], [], lambda *_: None, lambda *_: None)


def _swap_sibling(arrs):
    n = len(arrs)

    def copies(ins, outs, sems):
        send, recv = sems
        x, y, c = _place()
        return [pltpu.make_async_remote_copy(
            src_ref=ins[a], dst_ref=outs[a], send_sem=send.at[a], recv_sem=recv.at[a],
            device_id=(x, y, 1 - c), device_id_type=MESH_ID) for a in range(n)]

    def start(ins, outs, sems):
        for cp in copies(ins, outs, sems):
            cp.start()

    def wait(ins, outs, sems):
        cps = copies(ins, outs, sems)
        for cp in cps:
            cp.wait_recv()
        for cp in cps:
            cp.wait_send()

    return _Exchange(arrs, [SDS(s.shape, s.dtype) for s in arrs],
                     [pltpu.SemaphoreType.DMA((n,)), pltpu.SemaphoreType.DMA((n,))], start, wait)


N_DEV = 8


def _gather_devices(block):
    def copies(ins, outs, sems):
        send, recv, loc = sems
        x, y, c = _place()
        me = 4 * x + 2 * y + c
        local = pltpu.make_async_copy(ins[0], outs[0].at[me], loc.at[0])
        sends, recvs = [], []
        for k in range(1, N_DEV):
            peer = (x ^ (k >> 2), y ^ ((k >> 1) & 1), c ^ (k & 1))
            for group, slot in ((sends, me), (recvs, me ^ k)):
                group.append(pltpu.make_async_remote_copy(
                    src_ref=ins[0], dst_ref=outs[0].at[slot], send_sem=send.at[k - 1], recv_sem=recv.at[k - 1],
                    device_id=peer, device_id_type=MESH_ID))
        return local, sends, recvs

    def start(ins, outs, sems):
        local, sends, _ = copies(ins, outs, sems)
        for cp in [local] + sends:
            cp.start()

    def wait(ins, outs, sems):
        local, sends, recvs = copies(ins, outs, sems)
        for cp in recvs:
            cp.wait_recv()
        for cp in sends:
            cp.wait_send()
        local.wait()

    return _Exchange([block], [SDS((N_DEV,) + block.shape, block.dtype)],
                     [pltpu.SemaphoreType.DMA((N_DEV - 1,)), pltpu.SemaphoreType.DMA((N_DEV - 1,)),
                      pltpu.SemaphoreType.DMA((1,))], start, wait)


def _both(ex_a, ex_b):
    na_i, na_o, na_s = len(ex_a.ins), len(ex_a.outs), len(ex_a.sems)

    def start(ins, outs, sems):
        ex_a.start(ins[:na_i], outs[:na_o], sems[:na_s])
        ex_b.start(ins[na_i:], outs[na_o:], sems[na_s:])

    def wait(ins, outs, sems):
        ex_a.wait(ins[:na_i], outs[:na_o], sems[:na_s])
        ex_b.wait(ins[na_i:], outs[na_o:], sems[na_s:])

    return _Exchange(ex_a.ins + ex_b.ins, ex_a.outs + ex_b.outs, ex_a.sems + ex_b.sems, start, wait)


def _sum_devices(slots):
    def body(s_ref, o_ref):
        acc = s_ref[0]
        for d in range(1, N_DEV):
            acc = acc + s_ref[d]
        o_ref[...] = acc

    return pl.pallas_call(
        body, in_specs=[pl.BlockSpec(memory_space=pltpu.VMEM)], out_specs=pl.BlockSpec(memory_space=pltpu.VMEM),
        out_shape=SDS(slots.shape[1:], F32), name="sum_small",
        compiler_params=pltpu.CompilerParams(vmem_limit_bytes=32 * 1024 * 1024))(slots)


def _adam_small(ws, gs, ms, vs):
    n = len(ws)

    def body(*refs):
        for i in range(n):
            w_ref, g_ref, m_ref, v_ref = (refs[k * n + i] for k in range(4))
            outs = _adam_math(w_ref[...], g_ref[...], m_ref[...], v_ref[...])
            for k in range(3):
                refs[(4 + k) * n + i][...] = outs[k]

    vmem = pl.BlockSpec(memory_space=pltpu.VMEM)
    return pl.pallas_call(
        body, in_specs=[vmem] * (4 * n), out_specs=[vmem] * (3 * n),
        out_shape=[SDS(w.shape, F32) for w in ws] * 3, name="adam_small",
        compiler_params=pltpu.CompilerParams(vmem_limit_bytes=32 * 1024 * 1024))(*ws, *gs, *ms, *vs)


def _local_step(x, target, small, big, tb, distributed):
    dist = distributed
    me = (2 * lax.axis_index("x") + lax.axis_index("y")) if dist else 0
    tb_ssm = min(tb, 256)
    bucket = jnp.asarray(_bucket_table())
    place_own = lambda t: lax.dynamic_update_index_in_dim(lax.empty((N_CHIPS,) + t.shape, t.dtype), t, me, 0)
    if dist:
        in_legs = _gather_start("gather_in_start", [big["w_in"]], [place_own(big["w_in"])], small["d_skip"])
        names = sorted(small)
        in_token, values = lax.optimization_barrier((in_legs[4], [small[n] for n in names]))
        small = dict(zip(names, values))
    g1, g2, g3, g4 = small["norm_mix_pre"], small["norm_mix_post"], small["norm_mlp_pre"], small["norm_mlp_post"]

    keys_first = lambda t: jnp.swapaxes(t, -1, -2)
    bias = _bias_table(small["rel_bias"], bucket)
    sink_rows = keys_first(_pair_layout(jnp.broadcast_to(small["sinks"].reshape(N_HEADS, 1, 1), (N_HEADS, BLOCK, 1))))
    disc_args = (small["lam_re"], small["lam_im"], small["log_dt"], small["b_re"], small["b_im"])
    (ab_re, ab_im, bb_re, bb_im), disc_vjp = jax.vjp(_ssm_discretize, *disc_args)
    tab_f, tab_b = _scan_tables(ab_re, ab_im)
    bmat = _bf(_b_matrix(bb_re, bb_im))
    cmat = _bf(_c_matrix(small["c_re"], small["c_im"]))
    d_skip = small["d_skip"]

    if dist:
        send, recv, src, lands, _ = in_legs
        prepared = (tab_b[0, :1, :LANES] + bias[0, 0, 0, :1, :LANES] + sink_rows[0, 0, :, :LANES]
                    + bmat[0, :1, :LANES].astype(F32) + cmat[0, :1, :LANES].astype(F32) + in_token[:1])
        send, recv, lands, token = _gather_pass("gather_in_pass", send, recv, src, lands, prepared)
        (g_in,) = _gather_wait("gather_in_wait", send, recv, lands, token)
        w_in = g_in.reshape(IN_W, D_MODEL)
    else:
        w_in = big["w_in"]
    mix = ("w_glu", "w_attn_branch", "w_ssm_branch", "w_out")
    rest = [big[n] for n in mix + ("w_ff_in", "w_ff_out")]
    token = None
    if dist:
        send, recv, rest, lands, token = _gather_start("gather_rest_start", rest, [place_own(t) for t in rest], g_in)
    h1, q, k, v, u, ga, gs = _inproj_fwd(x, g1, w_in, tb, _after(token) if dist else None)
    s, h = _ssm_fwd(u, bmat, cmat, tab_f, d_skip, tb_ssm)
    if dist:
        send, recv, lands, token = _gather_pass("gather_rest_pass", send, recv, rest, lands, s)
    att = _attn_fwd(q, k, v, bias, sink_rows, _after(token) if dist else None)[0]
    if dist:
        rest = _gather_wait("gather_rest_wait", send, recv, lands, att)
    w_glu, w_ab, w_sb, w_out, w_ffi, w_ffo = rest
    w_glu = w_glu.reshape(SSM_W, SSM_W)
    w_out = w_out.reshape(D_MODEL, D_MODEL)
    w_ffi = [w_ffi]
    x2 = _merge_fwd(x, s, att, ga, gs, g2, w_glu, w_ab, w_sb, w_out, tb)
    dy, df, h3, loss_acc, dg4 = _mlp_fwd_loss(x2, target, g3, g4, w_ffi, w_ffo, tb)

    dx2, act, da, dg3 = _mlp_bwd(x2, dy, df, h3, g3, w_ffi, w_ffo, tb)
    tl = min(2048, x.shape[0])
    chunked = (N_CHIPS, D_FF // N_CHIPS, D_MODEL)
    d_ffi, b_ffi = _matmul_tn("grad_w_ff_in", h3, da, D_MODEL, D_FF // FF_CHUNKS, tl, True)
    d_ffo, b_ffo = _matmul_tn("grad_w_ff_out", act, df, D_FF // FF_CHUNKS, D_MODEL, tl, False)
    d_ffo, b_ffo = d_ffo.reshape(chunked), b_ffo.reshape(chunked)
    behind = lambda flight: _after(flight.token) if dist else None
    ff_fl = _scatter_off("scatter_ff_off", [b_ffi, b_ffo], d_ffo) if dist else None
    outs = _merge_bwd(dx2, s, att, ga, gs, g2, w_glu, w_ab, w_sb, w_out, tb_ssm, behind(ff_fl))
    ds, datt, dga, dgs, dg2, d_glu, d_ab, d_sb, d_out, b_glu, b_ab, b_sb, b_out = outs
    glu4, out4 = (N_CHIPS, SSM_W // N_CHIPS, SSM_W), (N_CHIPS, D_MODEL // N_CHIPS, D_MODEL)
    d_mix = [d_glu.reshape(glu4), d_ab, d_sb, d_out.reshape(out4)]
    b_mix = [b_glu.reshape(glu4), b_ab, b_sb, b_out.reshape(out4)]
    mix_fl = _scatter_off("scatter_mix_off", b_mix, d_mix[-1]) if dist else None
    du, d_bmat, d_cmat, da_acc, dd_skip = _ssm_bwd(
        ds, u, h, bmat.transpose(0, 2, 1), cmat.transpose(0, 2, 1), tab_b, d_skip, tb_ssm, behind(mix_fl))
    dq, dk, dv, dbias, dsink_rows = _attn_bwd(q, k, v, datt, bias, sink_rows)
    swap_fl = None
    if dist:
        r_ffi, r_ffo = _land("scatter_ff_land", ff_fl, dq)[1]
        p_ffi = _sum4("sum_w_ff_in", d_ffi, r_ffi, me)
        p_ffo = _sum4("sum_w_ff_out", d_ffo, r_ffo, me)
        swap_fl = _swap_off("swap_ff_off", [p_ffi, p_ffo], r_ffo)
    dx, dpj, dg1 = _inproj_bwd(x, dx2, dq, dk, dv, du, dga, dgs, g1, w_in, tb, behind(swap_fl))

    dab_re, dab_im = _state_unlayout(jnp.sum(da_acc, axis=0))
    dbb_re, dbb_im = _b_matrix_grad(d_bmat)
    d_lam_re, d_lam_im, d_log_dt, d_b_re, d_b_im = disc_vjp((dab_re, dab_im, dbb_re, dbb_im))
    d_c_re, d_c_im = _c_matrix_grad(d_cmat)
    d_rel = _bias_grad(dbias, bucket)
    d_sinks = jnp.sum(_pair_unlayout(keys_first(dsink_rows)), axis=(1, 2))
    small_grads = dict(
        norm_mix_pre=dg1, norm_mix_post=dg2, norm_mlp_pre=dg3, norm_mlp_post=dg4, rel_bias=d_rel, sinks=d_sinks,
        lam_re=d_lam_re, lam_im=d_lam_im, log_dt=d_log_dt, b_re=d_b_re, b_im=d_b_im, c_re=d_c_re, c_im=d_c_im,
        d_skip=dd_skip)
    small_fl = _devices_off("small_off", _pack(small_grads, loss_acc), swap_fl.token) if dist else None
    outs = _matmul_tn("grad_w_in", dpj, h1, IN_W // 2, D_MODEL, tl, False, behind(small_fl))
    in4 = (N_CHIPS, IN_W // N_CHIPS, D_MODEL)
    d_in, b_in = outs[0].reshape(in4), outs[1].reshape(in4)
    if not dist:
        return loss_acc, dx, small_grads, dict(zip(BIG, [d_in] + d_mix + [d_ffi, d_ffo]))
    (p_ffi, p_ffo), (s_ffi, s_ffo) = _land("swap_ff_land", swap_fl, b_in)
    r_mix = _land("scatter_mix_land", mix_fl, b_in)[1]
    p_mix = [_sum4("sum_" + n, d, r, me) for n, d, r in zip(mix, d_mix, r_mix)]
    pending = dict(d_in=d_in, b_in=b_in, p_mix=p_mix, w_ff_in=(p_ffi, s_ffi), w_ff_out=(p_ffo, s_ffo), me=me)
    return loss_acc, dx, small_fl, pending


SMALL = ['norm_mix_pre', 'norm_mix_post', 'norm_mlp_pre', 'norm_mlp_post', 'rel_bias', 'sinks', 'lam_re', 'lam_im',
         'log_dt', 'b_re', 'b_im', 'c_re', 'c_im', 'd_skip']
BIG = ['w_in', 'w_glu', 'w_attn_branch', 'w_ssm_branch', 'w_out', 'w_ff_in', 'w_ff_out']
WEIGHTS = ['norm_mix_pre', 'norm_mix_post', 'norm_mlp_pre', 'norm_mlp_post', 'w_in', 'rel_bias', 'sinks', 'lam_re',
           'lam_im', 'log_dt', 'b_re', 'b_im', 'c_re', 'c_im', 'd_skip', 'w_glu', 'w_attn_branch', 'w_ssm_branch',
           'w_out', 'w_ff_in', 'w_ff_out']
PACK_COLS = 1024
PACK_ORDER = ['b_re', 'b_im', 'c_re', 'c_im', 'lam_re', 'lam_im', 'norm_mix_pre', 'norm_mix_post', 'norm_mlp_pre',
              'norm_mlp_post', 'rel_bias', 'sinks', 'log_dt', 'd_skip']


STATE_MINOR = ('b_re', 'b_im')
PACK_ROWS = 144
LOSS_ROW = 140


def _pack(named, loss_acc):
    parts = []
    for n in PACK_ORDER:
        a = jnp.swapaxes(named[n], -1, -2) if n in STATE_MINOR else named[n]
        flat = a.reshape(-1)
        rows = -(-flat.shape[0] // PACK_COLS)
        parts.append(jnp.pad(flat, (0, rows * PACK_COLS - flat.shape[0])).reshape(rows, PACK_COLS))
    assert sum(p.shape[0] for p in parts) == LOSS_ROW
    parts.append(jnp.pad(loss_acc[0:1], ((0, PACK_ROWS - LOSS_ROW - 1), (0, PACK_COLS - loss_acc.shape[1]))))
    return jnp.concatenate(parts, axis=0)


def _unpack(packed, shapes):
    out, at = {}, 0
    for n in PACK_ORDER:
        shape = shapes[n][:-2] + (shapes[n][-1], shapes[n][-2]) if n in STATE_MINOR else shapes[n]
        size = int(np.prod(shape))
        rows = -(-size // PACK_COLS)
        blk = packed[at:at + rows]
        out[n] = (blk.reshape(-1)[:size] if size % PACK_COLS else blk).reshape(shape)
        at += rows
    return out


def kernel(x, norm_mix_pre, norm_mix_post, norm_mlp_pre, norm_mlp_post, w_in, rel_bias, sinks, lam_re, lam_im, log_dt, b_re, b_im, c_re, c_im, d_skip, w_glu, w_attn_branch, w_ssm_branch, w_out, w_ff_in, w_ff_out, loss_target, m_norm_mix_pre, m_norm_mix_post, m_norm_mlp_pre, m_norm_mlp_post, m_w_in, m_rel_bias, m_sinks, m_lam_re, m_lam_im, m_log_dt, m_b_re, m_b_im, m_c_re, m_c_im, m_d_skip, m_w_glu, m_w_attn_branch, m_w_ssm_branch, m_w_out, m_w_ff_in, m_w_ff_out, v_norm_mix_pre, v_norm_mix_post, v_norm_mlp_pre, v_norm_mlp_post, v_w_in, v_rel_bias, v_sinks, v_lam_re, v_lam_im, v_log_dt, v_b_re, v_b_im, v_c_re, v_c_im, v_d_skip, v_w_glu, v_w_attn_branch, v_w_ssm_branch, v_w_out, v_w_ff_in, v_w_ff_out):
    env = dict(locals())
    w = {n: env[n] for n in WEIGHTS}
    m = {n: env["m_" + n] for n in WEIGHTS}
    v = {n: env["v_" + n] for n in WEIGHTS}
    seq = x.shape[1]
    tb = min(512, seq)

    small = {n: w[n] for n in ('norm_mix_pre', 'norm_mix_post', 'norm_mlp_pre', 'norm_mlp_post', 'rel_bias')}
    small.update({n: w[n][0] for n in ('sinks', 'lam_re', 'lam_im', 'log_dt', 'b_re', 'b_im', 'c_re', 'c_im')})
    small['d_skip'] = w['d_skip']
    shard = lambda t, n: t[n][0].T if n == 'w_in' else t[n][0]
    unshard = lambda a, n: (a.T if n == 'w_in' else a)[None]
    _, dx, small_fl, pending = _local_step(
        x[0], loss_target[0], small, {n: _bf(shard(w, n)) for n in BIG}, tb, True)

    grads, deltas, new_m, new_v = {}, {}, {}, {}

    def adam(n, partials, after=None):
        outs = _adam_pair("adam_" + n, (shard(w, n), *partials, shard(m, n), shard(v, n)), after)
        grads[n], deltas[n], new_m[n], new_v[n] = [unshard(a, n) for a in outs]
        return outs[3]

    mix = ("w_glu", "w_attn_branch", "w_ssm_branch", "w_out")
    in_fl = _scatter_off("scatter_w_in_off", [pending["b_in"]], pending["d_in"])
    sib_mix = _exchange_alone("swap_mix", _swap_sibling(pending["p_mix"]))
    last = None
    for n, partials in [(n, pending[n]) for n in ("w_ff_in", "w_ff_out")] + list(zip(mix, zip(pending["p_mix"], sib_mix))):
        last = adam(n, partials, in_fl.token)
    (r_in,) = _land("scatter_w_in_land", in_fl, last)[1]
    p_in = _sum4("sum_w_in", pending["d_in"], r_in, pending["me"])
    (s_in,) = _exchange_alone("swap_w_in", _swap_sibling([p_in]))
    adam("w_in", (p_in, s_in))

    small_g = _sum_devices(_land("small_land", small_fl, last)[1][0])
    loss = small_g[LOSS_ROW, 0]
    minor = lambda t, n: jnp.swapaxes(t, -1, -2) if n in STATE_MINOR else t
    g_small = _unpack(small_g, {n: w[n].shape for n in SMALL})
    outs = _adam_small([minor(w[n], n) for n in SMALL], [g_small[n] for n in SMALL],
                       [minor(m[n], n) for n in SMALL], [minor(v[n], n) for n in SMALL])
    grads.update({n: minor(g_small[n], n) for n in SMALL})
    for k, dst in enumerate((deltas, new_m, new_v)):
        dst.update({n: minor(a, n) for n, a in zip(SMALL, outs[k * len(SMALL):(k + 1) * len(SMALL)])})

    return (loss, dx[None], *[grads[n] for n in WEIGHTS], *[deltas[n] for n in WEIGHTS],
            *[new_m[n] for n in WEIGHTS], *[new_v[n] for n in WEIGHTS])
```

```python
import functools
import math

import numpy as np
import jax
import jax.numpy as jnp
from jax import lax
from jax.experimental import pallas as pl
from jax.experimental.pallas import tpu as pltpu

F32 = jnp.float32
BF16 = jnp.bfloat16

D_MODEL = 1024
N_HEADS = 8
N_KV = 2
Q_GROUP = 4
HEAD_DIM = 64
ATTN_W = 512
KV_W = 128
BLOCK = 128
N_BUCKETS = 32
MAX_DISTANCE = 128
NEG_INF = -1e30
SSM_W = 512
SSM_GROUP = 16
SSM_GROUPS = 32
SSM_STATE = 64
N_SUPER = 4
GROUPS_PER_SUPER = SSM_GROUPS // N_SUPER
SUPER_IN = GROUPS_PER_SUPER * SSM_GROUP
SUPER_HALF = GROUPS_PER_SUPER * SSM_STATE
SUPER_W = 2 * SUPER_HALF
STATE_COLS = N_SUPER * SUPER_W
D_FF = 4096
FF_CHUNKS = 4
IN_W = 3328
SPLITS = (0, 512, 640, 768, 1280, 2304, 3328)
RMS_EPS = 1e-6
N_CHIPS = 4
N_DEV = 8
SUBLANES = 8
LANES = 128
STATE_TILES = STATE_COLS // LANES
SUPER_TILES = SUPER_W // LANES

ADAM_LR = 0.001
ADAM_B1 = 0.9
ADAM_B2 = 0.999
ADAM_EPS = 1e-08
ADAM_WD = 0.01
ADAM_STEP = 10

VMEM_BIG = 56 * 1024 * 1024
SDS = jax.ShapeDtypeStruct
MESH_ID = pl.DeviceIdType.MESH
ANY = pl.BlockSpec(memory_space=pl.ANY)


def _bf(x):
    return x.astype(BF16)


def _mm(a, b):
    return jnp.dot(a, b, preferred_element_type=F32)


def _mm_nt(a, b):
    return lax.dot_general(a, b, (((1,), (1,)), ((), ())), preferred_element_type=F32)


def _mm_tn(a, b):
    return lax.dot_general(a, b, (((0,), (0,)), ((), ())), preferred_element_type=F32)


def _sig(x):
    return 1.0 / (1.0 + jnp.exp(-x))


def _rms(x, g):
    r = lax.rsqrt(jnp.mean(x * x, axis=-1, keepdims=True) + RMS_EPS)
    xh = x * r
    return xh * g, xh, r


def _rms_bwd(dout, xh, r, g):
    dg = jnp.sum(dout * xh, axis=0, keepdims=True)
    dxh = dout * g
    dx = r * (dxh - xh * jnp.mean(dxh * xh, axis=-1, keepdims=True))
    return dx, dg


_GELU_C = math.sqrt(2.0 / math.pi)


def _gelu_and_grad(x):
    x2 = x * x
    inner = _GELU_C * (x + 0.044715 * (x2 * x))
    t = jnp.tanh(inner)
    y = 0.5 * x * (1.0 + t)
    dy = 0.5 * (1.0 + t) + 0.5 * x * (1.0 - t * t) * (_GELU_C * (1.0 + 3.0 * 0.044715 * x2))
    return y, dy


def _zero_map(nd, *_):
    return (0,) * nd


def _params(n_axes, vmem=None):
    return pltpu.CompilerParams(dimension_semantics=("arbitrary",) * n_axes, vmem_limit_bytes=vmem)


class _Exchange:
    def __init__(self, ins, outs, sems, start, wait):
        self.ins, self.outs, self.sems, self.start, self.wait = list(ins), list(outs), list(sems), start, wait


def _fused_call(name, body, grid, in_specs, out_specs, out_shape, scratch, args, exchange, params):
    n_in, n_out, n_scr = len(in_specs), len(out_specs), len(scratch)
    if exchange is None:
        fn = body
    else:
        ex = exchange
        n_xi, n_xo = len(ex.ins), len(ex.outs)

        def fn(*refs):
            at = 0
            parts = []
            for n in (n_in, n_xi, n_out, n_xo, n_scr, len(ex.sems)):
                parts.append(refs[at:at + n])
                at += n
            ins, x_in, outs, x_out, scr, x_sem = parts
            ids = [pl.program_id(a) for a in range(len(grid))]
            first = functools.reduce(jnp.logical_and, [i == 0 for i in ids])
            last = functools.reduce(jnp.logical_and, [i == g - 1 for i, g in zip(ids, grid)])

            @pl.when(first)
            def _():
                ex.start(x_in, x_out, x_sem)

            body(*ins, *outs, *scr)

            @pl.when(last)
            def _():
                ex.wait(x_in, x_out, x_sem)

        in_specs = list(in_specs) + [ANY] * n_xi
        out_specs = list(out_specs) + [ANY] * n_xo
        out_shape = list(out_shape) + ex.outs
        scratch = list(scratch) + ex.sems
        args = list(args) + ex.ins
    return pl.pallas_call(fn, grid=grid, in_specs=in_specs, out_specs=out_specs, out_shape=out_shape,
                          scratch_shapes=list(scratch), name=name, compiler_params=params)(*args)


def _exchange_alone(name, ex):
    def body(*refs):
        n_xi, n_xo = len(ex.ins), len(ex.outs)
        x_in, x_out, x_sem = refs[:n_xi], refs[n_xi:n_xi + n_xo], refs[n_xi + n_xo:]
        ex.start(x_in, x_out, x_sem)
        ex.wait(x_in, x_out, x_sem)

    return pl.pallas_call(body, in_specs=[ANY] * len(ex.ins), out_specs=[ANY] * len(ex.outs), out_shape=ex.outs,
                          scratch_shapes=ex.sems, name=name)(*ex.ins)


def _rowcall(name, body, seq, tb, rows, consts, row_outs, acc_outs, scratch=(), reverse=False, vmem=None,
             exchange=None):
    nb = seq // tb
    rmap = (lambda i: (nb - 1 - i, 0)) if reverse else (lambda i: (i, 0))
    tmap = lambda i: (0,) + rmap(i)

    def row_spec(width):
        if isinstance(width, tuple):
            return pl.BlockSpec((width[0], tb, width[1]), tmap)
        return pl.BlockSpec((tb, width), rmap)

    def row_shape(width):
        return (width[0], seq, width[1]) if isinstance(width, tuple) else (seq, width)

    in_specs = [row_spec(a.shape[1] if a.ndim == 2 else (a.shape[0], a.shape[2])) for a in rows]
    in_specs += [pl.BlockSpec(a.shape, functools.partial(_zero_map, a.ndim), pipeline_mode=pl.Buffered(1))
                 for a in consts]
    out_specs = [row_spec(c) for c, _ in row_outs] + [ANY] * len(acc_outs)
    out_shape = [SDS(row_shape(c), dt) for c, dt in row_outs] + [SDS(s, dt) for s, dt in acc_outs]
    n_main = len(rows) + len(consts) + len(row_outs)
    n_acc = len(acc_outs)

    def fn(*refs):
        main, acc_hbm, rest = refs[:n_main], refs[n_main:n_main + n_acc], refs[n_main + n_acc:]
        acc_vmem, own = rest[:n_acc], rest[n_acc:]
        body(*main, *acc_vmem, *own)

        @pl.when(pl.program_id(0) == nb - 1)
        def _():
            for src, dst in zip(acc_vmem, acc_hbm):
                pltpu.sync_copy(src, dst)

    buffers = [pltpu.VMEM(s, dt) for s, dt in acc_outs] + list(scratch)
    return _fused_call(name, fn if acc_outs else body, (nb,), in_specs, out_specs, out_shape, buffers,
                       [*rows, *consts], exchange, _params(1, vmem))


def _inproj_fwd(x, g1, w_in, tb, exchange=None):
    seq = x.shape[0]

    def body(x_ref, g_ref, w_ref, h_ref, q_ref, k_ref, v_ref, u_ref, ga_ref, gs_ref):
        h, _, _ = _rms(x_ref[...], g_ref[...])
        hb = _bf(h)
        h_ref[...] = hb
        pj = _mm_nt(hb, w_ref[...])
        q_ref[...] = _bf(pj[:, SPLITS[0]:SPLITS[1]])
        k_ref[...] = _bf(pj[:, SPLITS[1]:SPLITS[2]])
        v_ref[...] = _bf(pj[:, SPLITS[2]:SPLITS[3]])
        u_ref[...] = pj[:, SPLITS[3]:SPLITS[4]]
        ga_ref[...] = pj[:, SPLITS[4]:SPLITS[5]]
        gs_ref[...] = pj[:, SPLITS[5]:SPLITS[6]]

    return _rowcall("inproj_fwd", body, seq, tb, [x], [g1, w_in],
                    [(D_MODEL, BF16), (ATTN_W, BF16), (KV_W, BF16), (KV_W, BF16), (SSM_W, F32),
                     (D_MODEL, F32), (D_MODEL, F32)], [], vmem=VMEM_BIG, exchange=exchange)


def _inproj_bwd(x, dx2, dq, dk, dv, du, dga, dgs, g1, w_in, tb, exchange=None):
    seq = x.shape[0]

    def body(x_ref, dx2_ref, dq_ref, dk_ref, dv_ref, du_ref, dga_ref, dgs_ref, g_ref, w_ref,
             dx_ref, dpj_ref, dg_ref):
        @pl.when(pl.program_id(0) == 0)
        def _():
            dg_ref[...] = jnp.zeros_like(dg_ref)

        dpj = jnp.concatenate([dq_ref[...], dk_ref[...], dv_ref[...], _bf(du_ref[...]),
                               dga_ref[...], dgs_ref[...]], axis=1)
        dpj_ref[...] = dpj
        dh = _mm(dpj, w_ref[...])
        g = g_ref[...]
        _, xh, r = _rms(x_ref[...], g)
        dxn, dg = _rms_bwd(dh, xh, r, g)
        dx_ref[...] = dx2_ref[...] + dxn
        dg_ref[...] += dg

    return _rowcall("inproj_bwd", body, seq, tb, [x, dx2, dq, dk, dv, du, dga, dgs], [g1, w_in],
                    [(D_MODEL, F32), (IN_W, BF16)], [((1, D_MODEL), F32)], vmem=VMEM_BIG, exchange=exchange)


def _bucket_table():
    qi = np.arange(BLOCK)[:, None]
    kj = np.arange(2 * BLOCK)[None, :]
    dist = qi + BLOCK - kj
    max_exact = N_BUCKETS // 2
    d = np.maximum(dist, 0)
    df = np.maximum(d, 1).astype(np.float32)
    large = max_exact + (np.log(df / np.float32(max_exact)) / np.float32(math.log(MAX_DISTANCE / max_exact))
                         * np.float32(N_BUCKETS - max_exact)).astype(np.int32)
    large = np.minimum(large, N_BUCKETS - 1)
    bucket = np.where(d < max_exact, d, large)
    valid = (dist >= 0) & (dist < BLOCK)
    return np.where(valid, bucket, -1).astype(np.int32)


def _bias_table(rel_bias, bucket):
    def body(rb_ref, bk_ref, o_ref):
        bk = bk_ref[...]
        has_prev = lax.broadcasted_iota(jnp.int32, bk.shape, 1) >= BLOCK
        for h in range(N_HEADS):
            kh, j, par = h // Q_GROUP, (h // 2) % 2, h % 2
            acc = jnp.full((BLOCK, 2 * BLOCK), NEG_INF, F32)
            for b in range(N_BUCKETS):
                acc = jnp.where(bk == b, rb_ref[b, h], acc)
            o_ref[0, kh, par, :, j * BLOCK:(j + 1) * BLOCK] = jnp.where(has_prev, acc, NEG_INF).T
            o_ref[1, kh, par, :, j * BLOCK:(j + 1) * BLOCK] = acc.T

    return pl.pallas_call(
        body, out_shape=SDS((2, N_KV, 2, 2 * BLOCK, 2 * BLOCK), F32),
        in_specs=[pl.BlockSpec(memory_space=pltpu.SMEM), pl.BlockSpec(memory_space=pltpu.VMEM)],
        out_specs=pl.BlockSpec(memory_space=pltpu.VMEM), name="bias_table",
    )(rel_bias, bucket)


def _bias_grad(dbias, bucket):
    def body(db_ref, bk_ref, o_ref):
        bk = bk_ref[...]
        for h in range(N_HEADS):
            kh, j, par = h // Q_GROUP, (h // 2) % 2, h % 2
            db = db_ref[kh, par, :, j * BLOCK:(j + 1) * BLOCK].T
            for b in range(N_BUCKETS):
                o_ref[b, h] = jnp.sum(jnp.where(bk == b, db, 0.0))

    return pl.pallas_call(
        body, out_shape=SDS((N_BUCKETS, N_HEADS), F32),
        in_specs=[pl.BlockSpec(memory_space=pltpu.VMEM), pl.BlockSpec(memory_space=pltpu.VMEM)],
        out_specs=pl.BlockSpec(memory_space=pltpu.SMEM), name="bias_grad",
    )(dbias, bucket)


TILE = 2 * HEAD_DIM


def _pair_layout(t):
    lead = t.shape[:-3]
    t = t.reshape(lead + (N_KV, 2, 2) + t.shape[-2:])
    nl = len(lead)
    t = jnp.transpose(t, tuple(range(nl)) + (nl, nl + 2, nl + 1, nl + 3, nl + 4))
    return t.reshape(lead + (N_KV, 2, 2 * BLOCK, t.shape[-1]))


def _pair_unlayout(t):
    t = t.reshape(N_KV, 2, 2, BLOCK, t.shape[-1]).transpose(0, 2, 1, 3, 4)
    return t.reshape(N_HEADS, BLOCK, t.shape[-1])


def _halves(t):
    tf = t.astype(F32)
    low = lax.broadcasted_iota(jnp.int32, tf.shape, 1) < HEAD_DIM
    swapped = pltpu.roll(tf, HEAD_DIM, 1)
    zero = jnp.zeros_like(tf)
    return ((_bf(jnp.where(low, tf, zero)), _bf(jnp.where(low, zero, swapped))),
            (_bf(jnp.where(low, swapped, zero)), _bf(jnp.where(low, zero, tf))))


def _fold_halves(even, odd):
    low = lax.broadcasted_iota(jnp.int32, even.shape, 1) < HEAD_DIM
    comb = jnp.where(low, even, odd)
    return comb + pltpu.roll(comb, HEAD_DIM, 1)


def _tile_rows(ref, kh):
    return jnp.concatenate([ref[:, (2 * kh) * TILE:(2 * kh + 1) * TILE],
                            ref[:, (2 * kh + 1) * TILE:(2 * kh + 2) * TILE]], axis=0)


def _halves_t(t):
    tt = t.astype(F32).T
    top = lax.broadcasted_iota(jnp.int32, tt.shape, 0) < HEAD_DIM
    swapped = jnp.concatenate([tt[HEAD_DIM:], tt[:HEAD_DIM]], axis=0)
    zero = jnp.zeros_like(tt)
    return ((_bf(jnp.where(top, tt, zero)), _bf(jnp.where(top, zero, swapped))),
            (_bf(jnp.where(top, swapped, zero)), _bf(jnp.where(top, zero, tt))))


def _attn_probs(km, qk, bias, sink):
    lg = _mm_nt(km, qk) * (HEAD_DIM ** -0.5) + bias
    m = jnp.maximum(jnp.max(lg, axis=0, keepdims=True), sink)
    p = jnp.exp(lg - m)
    es = jnp.exp(sink - m)
    inv = 1.0 / (jnp.sum(p, axis=0, keepdims=True) + es)
    return p * inv, es * inv


def _attn_fwd(q, k, v, bias, sink_rows, exchange=None):
    seq = q.shape[0]
    nblk = seq // BLOCK

    def body(q_ref, kp_ref, kc_ref, vp_ref, vc_ref, b_ref, s_ref, o_ref):
        which = jnp.minimum(pl.program_id(0), 1)
        kms = _halves(jnp.concatenate([kp_ref[...], kc_ref[...]], axis=0))
        vts = _halves_t(jnp.concatenate([vp_ref[...], vc_ref[...]], axis=0))
        for kh in range(N_KV):
            qk = _tile_rows(q_ref, kh)
            acc = jnp.zeros((TILE, 2 * BLOCK), F32)
            for par in range(2):
                pr, _ = _attn_probs(kms[kh][par], qk, b_ref[which, kh, par], s_ref[kh, par])
                acc = acc + _mm(vts[kh][par], _bf(pr))
            acc = acc.T
            o_ref[:, (2 * kh) * TILE:(2 * kh + 1) * TILE] = _bf(acc[:BLOCK])
            o_ref[:, (2 * kh + 1) * TILE:(2 * kh + 2) * TILE] = _bf(acc[BLOCK:])

    cur = lambda n: (n, 0)
    prev = lambda n: (jnp.maximum(n - 1, 0), 0)
    return _fused_call(
        "attn_fwd", body, (nblk,),
        [pl.BlockSpec((BLOCK, ATTN_W), cur),
         pl.BlockSpec((BLOCK, KV_W), prev), pl.BlockSpec((BLOCK, KV_W), cur),
         pl.BlockSpec((BLOCK, KV_W), prev), pl.BlockSpec((BLOCK, KV_W), cur),
         pl.BlockSpec(bias.shape, functools.partial(_zero_map, bias.ndim)),
         pl.BlockSpec(sink_rows.shape, functools.partial(_zero_map, sink_rows.ndim))],
        [pl.BlockSpec((BLOCK, ATTN_W), cur)], [SDS((seq, ATTN_W), BF16)], [],
        [q, k, k, v, v, bias, sink_rows], exchange, _params(1))


def _attn_bwd(q, k, v, d_out, bias, sink_rows, exchange=None):
    seq = q.shape[0]
    nblk = seq // BLOCK

    def body(q_ref, kp_ref, kc_ref, vp_ref, vc_ref, do_ref, b_ref, s_ref,
             dq_ref, dk_ref, dv_ref, db_ref, ds_ref, ck_ref, cv_ref):
        n = pl.program_id(0)

        @pl.when(n == 0)
        def _():
            db_ref[...] = jnp.zeros_like(db_ref)
            ds_ref[...] = jnp.zeros_like(ds_ref)
            ck_ref[...] = jnp.zeros_like(ck_ref)
            cv_ref[...] = jnp.zeros_like(cv_ref)

        @pl.when(n < nblk)
        def _():
            which = jnp.minimum(n, 1)
            scale = HEAD_DIM ** -0.5
            kcat = jnp.concatenate([kp_ref[...], kc_ref[...]], axis=0)
            kms = _halves(kcat)
            kts = _halves_t(kcat)
            vms = _halves(jnp.concatenate([vp_ref[...], vc_ref[...]], axis=0))
            dks, dvs = [], []
            for kh in range(N_KV):
                qk = _tile_rows(q_ref, kh)
                dok = _tile_rows(do_ref, kh)
                dq = jnp.zeros((TILE, 2 * BLOCK), F32)
                dkp, dvp = [], []
                for par in range(2):
                    pr, ps = _attn_probs(kms[kh][par], qk, b_ref[which, kh, par], s_ref[kh, par])
                    dp = _mm_nt(vms[kh][par], dok)
                    rs = jnp.sum(pr * dp, axis=0, keepdims=True)
                    dlg = pr * (dp - rs)
                    ds_ref[kh, par] += -ps * rs
                    db_ref[kh, par] += dlg
                    dlb = _bf(dlg)
                    dq = dq + _mm(kts[kh][par], dlb)
                    dkp.append(_mm(dlb, qk))
                    dvp.append(_mm(_bf(pr), dok))
                dq = _bf((dq * scale).T)
                dq_ref[:, (2 * kh) * TILE:(2 * kh + 1) * TILE] = dq[:BLOCK]
                dq_ref[:, (2 * kh + 1) * TILE:(2 * kh + 2) * TILE] = dq[BLOCK:]
                dks.append(_fold_halves(*dkp))
                dvs.append(_fold_halves(*dvp))
            low = lax.broadcasted_iota(jnp.int32, (2 * BLOCK, TILE), 1) < HEAD_DIM
            dkk = jnp.where(low, dks[0], dks[1]) * scale
            dvv = jnp.where(low, dvs[0], dvs[1])
            dk_ref[...] = _bf(ck_ref[...] + dkk[:BLOCK])
            ck_ref[...] = dkk[BLOCK:]
            dv_ref[...] = _bf(cv_ref[...] + dvv[:BLOCK])
            cv_ref[...] = dvv[BLOCK:]

        @pl.when(n == nblk)
        def _():
            dk_ref[...] = _bf(ck_ref[...])
            dv_ref[...] = _bf(cv_ref[...])

    cur = lambda n: (jnp.minimum(n, nblk - 1), 0)
    prev = lambda n: (jnp.maximum(jnp.minimum(n, nblk - 1) - 1, 0), 0)
    late = lambda n: (jnp.maximum(n - 1, 0), 0)
    kv_spec = lambda m: pl.BlockSpec((BLOCK, KV_W), m)
    acc_b = pl.BlockSpec(bias.shape[1:], functools.partial(_zero_map, bias.ndim - 1))
    acc_s = pl.BlockSpec(sink_rows.shape, functools.partial(_zero_map, sink_rows.ndim))
    return _fused_call(
        "attn_bwd", body, (nblk + 1,),
        [pl.BlockSpec((BLOCK, ATTN_W), cur), kv_spec(prev), kv_spec(cur), kv_spec(prev), kv_spec(cur),
         pl.BlockSpec((BLOCK, ATTN_W), cur),
         pl.BlockSpec(bias.shape, functools.partial(_zero_map, bias.ndim)), acc_s],
        [pl.BlockSpec((BLOCK, ATTN_W), cur), kv_spec(late), kv_spec(late), acc_b, acc_s],
        [SDS((seq, ATTN_W), BF16), SDS((seq, KV_W), BF16), SDS((seq, KV_W), BF16),
         SDS(bias.shape[1:], F32), SDS(sink_rows.shape, F32)],
        [pltpu.VMEM((BLOCK, KV_W), F32), pltpu.VMEM((BLOCK, KV_W), F32)],
        [q, k, k, v, v, d_out, bias, sink_rows], exchange, _params(1))


def _ssm_discretize(lam_re, lam_im, log_dt, b_re, b_im):
    dt = jnp.exp(log_dt)[:, None]
    mag = jnp.exp(lam_re * dt)
    ab_re = mag * jnp.cos(lam_im * dt)
    ab_im = mag * jnp.sin(lam_im * dt)
    nr = ab_re - 1.0
    den = lam_re * lam_re + lam_im * lam_im
    f_re = (nr * lam_re + ab_im * lam_im) / den
    f_im = (ab_im * lam_re - nr * lam_im) / den
    bb_re = f_re[..., None] * b_re - f_im[..., None] * b_im
    bb_im = f_re[..., None] * b_im + f_im[..., None] * b_re
    return ab_re, ab_im, bb_re, bb_im


def _state_layout(re, im):
    z = jnp.stack([re, im]).reshape(2, N_SUPER, GROUPS_PER_SUPER, SSM_STATE)
    return z.transpose(1, 0, 2, 3).reshape(STATE_COLS)


def _state_unlayout(vec):
    z = vec.reshape(N_SUPER, 2, GROUPS_PER_SUPER, SSM_STATE).transpose(1, 0, 2, 3)
    z = z.reshape(2, SSM_GROUPS, SSM_STATE)
    return z[0], z[1]


SEG = 4
WINDOW = SEG * SUBLANES


def _scan_tables(ab_re, ab_im):
    pw = [None, (ab_re, ab_im)]
    for _ in range(2, WINDOW + 1):
        pr, pi_ = pw[-1]
        pw.append((pr * ab_re - pi_ * ab_im, pr * ab_im + pi_ * ab_re))
    rows = np.arange(SUBLANES)[:, None]
    ones = np.ones((SUBLANES, 1), np.float32)
    conj = lambda p: (p[0], -p[1])
    fwd, bwd = [], []
    for shift in (1, 2, 4):
        fwd.append(_state_layout(*pw[SEG * shift])[None, :] * (rows >= shift).astype(np.float32))
        bwd.append(_state_layout(*conj(pw[SEG * shift]))[None, :] * (rows < SUBLANES - shift).astype(np.float32))
    fwd.append(jnp.stack([_state_layout(*pw[SEG * (r + 1)]) for r in range(SUBLANES)]))
    bwd.append(jnp.stack([_state_layout(*conj(pw[SEG * (SUBLANES - r)])) for r in range(SUBLANES)]))
    for k in range(1, SEG):
        fwd.append(_state_layout(*pw[k])[None, :] * ones)
        bwd.append(_state_layout(*conj(pw[k]))[None, :] * ones)
    return jnp.stack(fwd), jnp.stack(bwd)


_EYE = np.eye(GROUPS_PER_SUPER, dtype=np.float32)


def _b_matrix(bb_re, bb_im):
    bb = jnp.stack([bb_re, bb_im]).reshape(2, N_SUPER, GROUPS_PER_SUPER, SSM_STATE, SSM_GROUP)
    m = jnp.einsum('rsgpc,gh->sgcrhp', bb, _EYE)
    return m.reshape(N_SUPER, SUPER_IN, SUPER_W)


def _b_matrix_grad(dm):
    d = dm.reshape(N_SUPER, GROUPS_PER_SUPER, SSM_GROUP, 2, GROUPS_PER_SUPER, SSM_STATE)
    d = jnp.sum(d * _EYE[None, :, None, None, :, None], axis=4)
    d = d.transpose(3, 0, 1, 4, 2).reshape(2, SSM_GROUPS, SSM_STATE, SSM_GROUP)
    return d[0], d[1]


def _c_matrix(c_re, c_im):
    cc = jnp.stack([c_re, -c_im]).reshape(2, N_SUPER, GROUPS_PER_SUPER, SSM_GROUP, SSM_STATE)
    m = jnp.einsum('rsgcp,gh->srgphc', cc, _EYE)
    return m.reshape(N_SUPER, SUPER_W, SUPER_IN)


def _c_matrix_grad(dm):
    d = dm.reshape(N_SUPER, 2, GROUPS_PER_SUPER, SSM_STATE, GROUPS_PER_SUPER, SSM_GROUP)
    d = jnp.sum(d * _EYE[None, None, :, None, :, None], axis=4)
    d = d.transpose(1, 0, 2, 4, 3).reshape(2, SSM_GROUPS, SSM_GROUP, SSM_STATE)
    return d[0], -d[1]


def _cmul_add(xr, xi, ar, ai, sr, si):
    return xr + ar * sr - ai * si, xi + ar * si + ai * sr


def _scan_rows(buf_ref, tab_ref, carry_ref, n_windows, reverse, h_ref=None, da_ref=None):
    order = list(range(SEG - 1, -1, -1)) if reverse else list(range(SEG))
    near = SUBLANES - 1 if reverse else 0
    far = 0 if reverse else SUBLANES - 1
    s_in = SUBLANES - 1 if reverse else 1
    lanes = lambda tile: pl.ds(tile * LANES, LANES)

    def window(w0, tile_re, tile_im, c_re, c_im, acc):
        rows = lambda t: pl.ds(w0 + t, SUBLANES, stride=SEG)
        get = lambda ref, t: (ref.at[tile_re][rows(t), :], ref.at[tile_im][rows(t), :])
        tab = lambda k: (tab_ref[k, :, lanes(tile_re)], tab_ref[k, :, lanes(tile_im)])

        def put(t, xr, xi):
            buf_ref.at[tile_re][rows(t), :] = xr
            buf_ref.at[tile_im][rows(t), :] = xi

        a1 = tab(4)
        er, ei = get(buf_ref, order[0])
        for t in order[1:]:
            er, ei = _cmul_add(*get(buf_ref, t), *a1, er, ei)
            if t != order[-1]:
                put(t, er, ei)
        for k, shift in enumerate((1, 2, 4)):
            s = (SUBLANES - shift) if reverse else shift
            er, ei = _cmul_add(er, ei, *tab(k), pltpu.roll(er, s, 0), pltpu.roll(ei, s, 0))
        er, ei = _cmul_add(er, ei, *tab(3), c_re, c_im)
        put(order[-1], er, ei)
        sub = lax.broadcasted_iota(jnp.int32, er.shape, 0)
        in_re = jnp.where(sub == near, c_re, pltpu.roll(er, s_in, 0))
        in_im = jnp.where(sub == near, c_im, pltpu.roll(ei, s_in, 0))
        true = {order[-1]: (er, ei)}
        for idx, t in enumerate(order[:-1]):
            true[t] = _cmul_add(*get(buf_ref, t), *tab(4 + idx), in_re, in_im)
            put(t, *true[t])
        carry = (jnp.broadcast_to(er[far:far + 1], er.shape), jnp.broadcast_to(ei[far:far + 1], ei.shape))
        if acc is None:
            return carry, None
        acc_re, acc_im = acc
        for t in range(SEG):
            if t + 1 < SEG:
                gr, gim = true[t + 1]
            else:
                gr = jnp.where(sub == SUBLANES - 1, c_re, pltpu.roll(true[0][0], SUBLANES - 1, 0))
                gim = jnp.where(sub == SUBLANES - 1, c_im, pltpu.roll(true[0][1], SUBLANES - 1, 0))
            hr, hi = get(h_ref, t)
            acc_re = acc_re + gr * hr + gim * hi
            acc_im = acc_im + gim * hr - gr * hi
        return carry, (acc_re, acc_im)

    half = SUPER_HALF // LANES
    per = 2 if h_ref is None else 4
    for sb in range(N_SUPER):
        pairs = [(2 * half * sb + j, 2 * half * sb + half + j) for j in range(half)]

        def step(wi, state, pairs=pairs):
            w = (n_windows - 1 - wi) if reverse else wi
            w0 = pl.multiple_of(w * WINDOW, WINDOW)
            out = []
            for j, (tile_re, tile_im) in enumerate(pairs):
                mine = state[per * j:per * (j + 1)]
                carry, acc = window(w0, tile_re, tile_im, mine[0], mine[1], mine[2:] or None)
                out += list(carry) + list(acc or ())
            return tuple(out)

        init = []
        for tile_re, tile_im in pairs:
            init += [carry_ref[:, lanes(tile_re)], carry_ref[:, lanes(tile_im)]]
            if h_ref is not None:
                init += [da_ref[:, lanes(tile_re)], da_ref[:, lanes(tile_im)]]
        fin = lax.fori_loop(0, n_windows, step, tuple(init))
        for j, (tile_re, tile_im) in enumerate(pairs):
            carry_ref[:, lanes(tile_re)] = fin[per * j]
            carry_ref[:, lanes(tile_im)] = fin[per * j + 1]
            if h_ref is not None:
                da_ref[:, lanes(tile_re)] = fin[per * j + 2]
                da_ref[:, lanes(tile_im)] = fin[per * j + 3]


def _put_tiles(ref, sb, block):
    for j in range(SUPER_TILES):
        ref[sb * SUPER_TILES + j] = block[:, j * LANES:(j + 1) * LANES]


def _get_tiles(ref, sb):
    return jnp.concatenate([ref[sb * SUPER_TILES + j] for j in range(SUPER_TILES)], axis=1)


def _ssm_fwd(u, bmat, cmat, tab, d_skip, tb, exchange=None):
    seq = u.shape[0]

    def body(u_ref, b_ref, c_ref, t_ref, d_ref, s_ref, h_ref, carry_ref):
        @pl.when(pl.program_id(0) == 0)
        def _():
            carry_ref[...] = jnp.zeros_like(carry_ref)

        u_blk = u_ref[...]
        ub = _bf(u_blk)
        for sb in range(N_SUPER):
            _put_tiles(h_ref, sb, _mm(ub[:, sb * SUPER_IN:(sb + 1) * SUPER_IN], b_ref[sb]))
        _scan_rows(h_ref, t_ref, carry_ref, tb // WINDOW, False)
        ys = [_mm(_bf(_get_tiles(h_ref, sb)), c_ref[sb]) for sb in range(N_SUPER)]
        s_ref[...] = jnp.concatenate(ys, axis=1) + d_ref[...] * u_blk

    return _rowcall("ssm_fwd", body, seq, tb, [u], [bmat, cmat, tab, d_skip],
                    [(SSM_W, F32), ((STATE_TILES, LANES), F32)], [],
                    scratch=[pltpu.VMEM((SUBLANES, STATE_COLS), F32)], vmem=VMEM_BIG, exchange=exchange)


def _ssm_bwd(ds, u, h, bmat_t, cmat_t, tab, d_skip, tb, exchange=None):
    seq = u.shape[0]

    def body(ds_ref, u_ref, h_ref, bt_ref, ct_ref, t_ref, d_ref,
             du_ref, db_ref, dc_ref, da_ref, dd_ref, g_ref, carry_ref):
        @pl.when(pl.program_id(0) == 0)
        def _():
            carry_ref[...] = jnp.zeros_like(carry_ref)
            db_ref[...] = jnp.zeros_like(db_ref)
            dc_ref[...] = jnp.zeros_like(dc_ref)
            da_ref[...] = jnp.zeros_like(da_ref)
            dd_ref[...] = jnp.zeros_like(dd_ref)

        ds_blk = ds_ref[...]
        dsb = _bf(ds_blk)
        u_blk = u_ref[...]
        ub = _bf(u_blk)
        for sb in range(N_SUPER):
            _put_tiles(g_ref, sb, _mm(dsb[:, sb * SUPER_IN:(sb + 1) * SUPER_IN], ct_ref[sb]))
        _scan_rows(g_ref, t_ref, carry_ref, tb // WINDOW, True, h_ref=h_ref, da_ref=da_ref)
        dus = []
        for sb in range(N_SUPER):
            gb = _bf(_get_tiles(g_ref, sb))
            dus.append(_mm(gb, bt_ref[sb]))
            db_ref[sb] += _mm_tn(ub[:, sb * SUPER_IN:(sb + 1) * SUPER_IN], gb)
            dc_ref[sb] += _mm_tn(_bf(_get_tiles(h_ref, sb)), dsb[:, sb * SUPER_IN:(sb + 1) * SUPER_IN])
        du_ref[...] = jnp.concatenate(dus, axis=1) + d_ref[...] * ds_blk
        dd_ref[...] += jnp.sum(ds_blk * u_blk, axis=0, keepdims=True)

    return _rowcall("ssm_bwd", body, seq, tb, [ds, u, h], [bmat_t, cmat_t, tab, d_skip],
                    [(SSM_W, F32)],
                    [((N_SUPER, SUPER_IN, SUPER_W), F32), ((N_SUPER, SUPER_W, SUPER_IN), F32),
                     ((SUBLANES, STATE_COLS), F32), ((1, SSM_W), F32)],
                    scratch=[pltpu.VMEM((STATE_TILES, tb, LANES), F32), pltpu.VMEM((SUBLANES, STATE_COLS), F32)],
                    reverse=True, vmem=VMEM_BIG, exchange=exchange)


def _merge_core(s, attb, ga, gs, wg_ref, wab_ref, wsb_ref, wout_ref):
    zg, dgelu = _gelu_and_grad(s)
    zgb = _bf(zg)
    sg = _sig(_mm(zgb, wg_ref[...]))
    z = zg * sg
    zb = _bf(z)
    ys = jnp.concatenate([_mm(zb, wsb_ref[j]) for j in range(N_CHIPS)], axis=1)
    ya = jnp.concatenate([_mm(attb, wab_ref[j]) for j in range(N_CHIPS)], axis=1)
    sa = _sig(ga)
    ss = _sig(gs)
    mgb = _bf(sa * ya + ss * ys)
    o = _mm(mgb, wout_ref[...])
    return dict(zg=zg, dgelu=dgelu, zgb=zgb, sg=sg, zb=zb, ys=ys, ya=ya, sa=sa, ss=ss, mgb=mgb, o=o)


def _merge_fwd(x, s, att, ga, gs, g2, w_glu, w_ab, w_sb, w_out, tb):
    seq = x.shape[0]

    def body(x_ref, s_ref, att_ref, ga_ref, gs_ref, g_ref, wg_ref, wab_ref, wsb_ref, wout_ref, x2_ref):
        f = _merge_core(s_ref[...], att_ref[...], ga_ref[...], gs_ref[...], wg_ref, wab_ref, wsb_ref, wout_ref)
        n, _, _ = _rms(f["o"], g_ref[...])
        x2_ref[...] = x_ref[...] + n

    return _rowcall("merge_fwd", body, seq, tb, [x, s, att, ga, gs], [g2, w_glu, w_ab, w_sb, w_out],
                    [(D_MODEL, F32)], [], vmem=VMEM_BIG)[0]


def _merge_bwd(dx2, s, att, ga, gs, g2, w_glu, w_ab, w_sb, w_out, tb, exchange=None):
    seq = s.shape[0]
    cw = D_MODEL // N_CHIPS
    last = seq // tb - 1

    def body(dx2_ref, s_ref, att_ref, ga_ref, gs_ref, g_ref, wg_ref, wab_ref, wsb_ref, wout_ref,
             ds_ref, datt_ref, dga_ref, dgs_ref, dg_ref, dwg_ref, dwab_ref, dwsb_ref, dwout_ref,
             bwg_ref, bwab_ref, bwsb_ref, bwout_ref):
        @pl.when(pl.program_id(0) == 0)
        def _():
            for r in (dg_ref, dwg_ref, dwab_ref, dwsb_ref, dwout_ref):
                r[...] = jnp.zeros_like(r)

        attb = att_ref[...]
        f = _merge_core(s_ref[...], attb, ga_ref[...], gs_ref[...], wg_ref, wab_ref, wsb_ref, wout_ref)
        g = g_ref[...]
        _, oh, r2 = _rms(f["o"], g)
        do, dg = _rms_bwd(dx2_ref[...], oh, r2, g)
        dg_ref[...] += dg
        dob = _bf(do)
        dwout_ref[...] += _mm_tn(f["mgb"], dob)
        dmg = _mm_nt(dob, wout_ref[...])
        sa, ss = f["sa"], f["ss"]
        dyab = _bf(dmg * sa)
        dysb = _bf(dmg * ss)
        dga_ref[...] = _bf(dmg * f["ya"] * sa * (1.0 - sa))
        dgs_ref[...] = _bf(dmg * f["ys"] * ss * (1.0 - ss))
        dwab = _mm_tn(attb, dyab)
        dwsb = _mm_tn(f["zb"], dysb)
        datt = jnp.zeros((tb, ATTN_W), F32)
        dz = jnp.zeros((tb, SSM_W), F32)
        for j in range(N_CHIPS):
            dwab_ref[j] += dwab[:, j * cw:(j + 1) * cw]
            dwsb_ref[j] += dwsb[:, j * cw:(j + 1) * cw]
            datt = datt + _mm_nt(dyab[:, j * cw:(j + 1) * cw], wab_ref[j])
            dz = dz + _mm_nt(dysb[:, j * cw:(j + 1) * cw], wsb_ref[j])
        datt_ref[...] = _bf(datt)
        sg, zg = f["sg"], f["zg"]
        dglb = _bf(dz * zg * sg * (1.0 - sg))
        dwg_ref[...] += _mm_tn(f["zgb"], dglb)
        dzg = dz * sg + _mm_nt(dglb, wg_ref[...])
        ds_ref[...] = dzg * f["dgelu"]

        @pl.when(pl.program_id(0) == last)
        def _():
            for dst, src in ((bwg_ref, dwg_ref), (bwab_ref, dwab_ref), (bwsb_ref, dwsb_ref), (bwout_ref, dwout_ref)):
                dst[...] = _bf(src[...])

    shapes = [w_glu.shape, w_ab.shape, w_sb.shape, w_out.shape]
    return _rowcall("merge_bwd", body, seq, tb, [dx2, s, att, ga, gs], [g2, w_glu, w_ab, w_sb, w_out],
                    [(SSM_W, F32), (ATTN_W, BF16), (D_MODEL, BF16), (D_MODEL, BF16)],
                    [((1, D_MODEL), F32)] + [(sh, F32) for sh in shapes] + [(sh, BF16) for sh in shapes],
                    vmem=VMEM_BIG, exchange=exchange)


def _mlp_fwd_loss(x2, target, g3, g4, w_ffi, w_ffo, tb):
    seq = x2.shape[0]
    n_slab = len(w_ffi)
    sw = D_FF // FF_CHUNKS // n_slab

    def body(x2_ref, t_ref, g3_ref, g4_ref, *rest):
        wi_refs, (wo_ref, dy_ref, df_ref, h_ref, loss_ref, dg_ref) = rest[:n_slab], rest[n_slab:]

        @pl.when(pl.program_id(0) == 0)
        def _():
            loss_ref[...] = jnp.zeros_like(loss_ref)
            dg_ref[...] = jnp.zeros_like(dg_ref)

        x2_blk = x2_ref[...]
        h3, _, _ = _rms(x2_blk, g3_ref[...])
        hb = _bf(h3)
        h_ref[...] = hb
        f = jnp.zeros((tb, D_MODEL), F32)
        for j in range(FF_CHUNKS):
            for k in range(n_slab):
                a = _mm(hb, wi_refs[k][j])
                f = f + _mm(_bf(jnp.square(jnp.maximum(a, 0.0))), wo_ref[j, pl.ds(k * sw, sw), :])
        g4 = g4_ref[...]
        n4, fh, r4 = _rms(f, g4)
        e = (x2_blk + n4) - t_ref[...]
        loss_ref[...] += 0.5 * jnp.sum(jnp.mean(e * e, axis=-1, keepdims=True))
        dy = e * (1.0 / D_MODEL)
        dy_ref[...] = dy
        df, dg = _rms_bwd(dy, fh, r4, g4)
        df_ref[...] = _bf(df)
        dg_ref[...] += dg

    return _rowcall("mlp_fwd_loss", body, seq, tb, [x2, target], [g3, g4, *w_ffi, w_ffo],
                    [(D_MODEL, F32), (D_MODEL, BF16), (D_MODEL, BF16)],
                    [((SUBLANES, 128), F32), ((1, D_MODEL), F32)], vmem=VMEM_BIG)


def _mlp_bwd(x2, dy, df, h3, g3, w_ffi, w_ffo, tb):
    seq = x2.shape[0]
    n_slab = len(w_ffi)
    sw = D_FF // FF_CHUNKS // n_slab

    def body(x2_ref, dy_ref, df_ref, h_ref, g3_ref, *rest):
        wi_refs, (wo_ref, dx_ref, act_ref, da_ref, dg_ref) = rest[:n_slab], rest[n_slab:]

        @pl.when(pl.program_id(0) == 0)
        def _():
            dg_ref[...] = jnp.zeros_like(dg_ref)

        hb = h_ref[...]
        dfb = df_ref[...]
        dh = jnp.zeros((tb, D_MODEL), F32)
        for j in range(FF_CHUNKS):
            for k in range(n_slab):
                cols = pl.ds((j * n_slab + k) * sw, sw)
                ra = jnp.maximum(_mm(hb, wi_refs[k][j]), 0.0)
                act_ref[:, cols] = _bf(ra * ra)
                dab = _bf(_mm_nt(dfb, wo_ref[j, pl.ds(k * sw, sw), :]) * (2.0 * ra))
                da_ref[:, cols] = dab
                dh = dh + _mm_nt(dab, wi_refs[k][j])
        g3 = g3_ref[...]
        _, xh, r3 = _rms(x2_ref[...], g3)
        dxn, dg = _rms_bwd(dh, xh, r3, g3)
        dx_ref[...] = dy_ref[...] + dxn
        dg_ref[...] += dg

    return _rowcall("mlp_bwd", body, seq, tb, [x2, dy, df, h3], [g3, *w_ffi, w_ffo],
                    [(D_MODEL, F32), (D_FF, BF16), (D_FF, BF16)], [((1, D_MODEL), F32)], vmem=VMEM_BIG)


def _matmul_tn(name, a, b, tk, tn, tl, chunk_major, exchange=None):
    seq, kdim = a.shape
    ndim = b.shape[1]
    last = seq // tl - 1

    def body(a_ref, b_ref, o_ref, ob_ref):
        @pl.when(pl.program_id(2) == 0)
        def _():
            o_ref[...] = jnp.zeros_like(o_ref)

        o_ref[...] += _mm_tn(a_ref[...], b_ref[...])

        @pl.when(pl.program_id(2) == last)
        def _():
            ob_ref[...] = _bf(o_ref[...])

    if chunk_major:
        shape = (ndim // tn, kdim, tn)
        out_spec = pl.BlockSpec((None, tk, tn), lambda k, n, l: (n, k, 0))
    else:
        shape = (kdim, ndim)
        out_spec = pl.BlockSpec((tk, tn), lambda k, n, l: (k, n))
    return _fused_call(
        name, body, (kdim // tk, ndim // tn, seq // tl),
        [pl.BlockSpec((tl, tk), lambda k, n, l: (l, k)), pl.BlockSpec((tl, tn), lambda k, n, l: (l, n))],
        [out_spec, out_spec], [SDS(shape, F32), SDS(shape, BF16)], [], [a, b], exchange, _params(3, VMEM_BIG))


def _ew_call(name, fn, ins, n_out, after=None):
    rows, cols = ins[0].shape
    tr = rows
    while tr * cols * 4 > min(1 << 20, (9 << 20) // (len(ins) + n_out)) and tr % 16 == 0:
        tr //= 2
    spec = pl.BlockSpec((tr, cols), lambda i: (i, 0))
    extra = [] if after is None else [after]

    def body(*refs):
        outs = fn(*[r[...] for r in refs[:len(ins)]])
        for r, o in zip(refs[len(ins) + len(extra):], outs):
            r[...] = o

    return pl.pallas_call(
        body, grid=(rows // tr,), in_specs=[spec] * len(ins) + [ANY] * len(extra), out_specs=[spec] * n_out,
        out_shape=[SDS((rows, cols), F32)] * n_out, name=name, compiler_params=_params(1))(*ins, *extra)


def _adam_math(w, g, m, v):
    m2 = ADAM_B1 * m + (1.0 - ADAM_B1) * g
    v2 = ADAM_B2 * v + (1.0 - ADAM_B2) * (g * g)
    m_hat = m2 / (1.0 - ADAM_B1 ** ADAM_STEP)
    v_hat = v2 / (1.0 - ADAM_B2 ** ADAM_STEP)
    delta = -ADAM_LR * (m_hat / (jnp.sqrt(v_hat) + ADAM_EPS) + ADAM_WD * w)
    return delta, m2, v2


def _sum4(name, own, recv, idx):
    _, rows, cols = own.shape
    tr = rows
    while tr * cols * 4 > (1 << 20) and tr % 16 == 0:
        tr //= 2

    def body(idx_ref, o_ref, r0_ref, r1_ref, r2_ref, out_ref):
        out_ref[...] = ((o_ref[...] + r0_ref[...].astype(F32)) + r1_ref[...].astype(F32)) + r2_ref[...].astype(F32)

    blk = (None, tr, cols)
    grid_spec = pltpu.PrefetchScalarGridSpec(
        num_scalar_prefetch=1, grid=(rows // tr,),
        in_specs=[pl.BlockSpec(blk, lambda i, s: (s[0], i, 0)), pl.BlockSpec(blk, lambda i, s: (0, i, 0)),
                  pl.BlockSpec(blk, lambda i, s: (1, i, 0)), pl.BlockSpec(blk, lambda i, s: (2, i, 0))],
        out_specs=pl.BlockSpec((tr, cols), lambda i, s: (i, 0)))
    return pl.pallas_call(body, grid_spec=grid_spec, out_shape=SDS((rows, cols), F32), name=name,
                          compiler_params=_params(1))(jnp.reshape(idx, (1,)).astype(jnp.int32), own, recv, recv, recv)


def _adam_pair(name, item, after=None):
    def fn(w_, a, b, m_, v_):
        g = a + b
        return (g,) + _adam_math(w_, g, m_, v_)

    return _ew_call(name, fn, list(item), 4, after)


def _place():
    return lax.axis_index("x"), lax.axis_index("y"), lax.axis_index("c")


def _other_chips(x, y):
    return [(1 - x, y), (x, 1 - y), (1 - x, 1 - y)]


HBM = pl.BlockSpec(memory_space=pltpu.HBM)
SEM = pl.BlockSpec(memory_space=pltpu.SEMAPHORE)
DATAFLOW = pltpu.SideEffectType.DATAFLOW_SIDE_EFFECTING


class _Flight:
    def __init__(self, copies, n_copies, send, recv, srcs, lands, token):
        self.copies, self.n, self.send, self.recv = copies, n_copies, send, recv
        self.srcs, self.lands, self.token = list(srcs), list(lands), token


def _take_off(name, srcs, lands, copies, n_copies, after):
    n_s, n_l = len(srcs), len(lands)

    def body(*refs):
        src, land = refs[:n_s], refs[n_s:n_s + n_l]
        send, recv = refs[n_s + n_l + 1:n_s + n_l + 3]
        for cp in copies(src, land, send, recv):
            cp.start()
        refs[-1][...] = jnp.zeros_like(refs[-1])

    mem = lambda t: pltpu.HBM(t.shape, t.dtype)
    sems = pltpu.SemaphoreType.DMA((n_copies,))
    outs = pl.pallas_call(
        body, name=name,
        out_shape=(sems, sems, *map(mem, srcs), *map(mem, lands), SDS((SUBLANES, LANES), F32)),
        in_specs=[HBM] * (n_s + n_l) + [ANY],
        out_specs=(SEM, SEM, *[HBM] * (n_s + n_l), pl.BlockSpec(memory_space=pltpu.VMEM)),
        input_output_aliases={i: 2 + i for i in range(n_s + n_l)},
        compiler_params=pltpu.CompilerParams(has_side_effects=DATAFLOW),
    )(*[pltpu.with_memory_space_constraint(t, pltpu.HBM) for t in (*srcs, *lands)], after)
    return _Flight(copies, n_copies, outs[0], outs[1], outs[2:2 + n_s], outs[2 + n_s:2 + n_s + n_l], outs[-1])


def _land(name, flight, after):
    n_s, n_l = len(flight.srcs), len(flight.lands)

    def body(*refs):
        src, land = refs[:n_s], refs[n_s:n_s + n_l]
        send, recv = refs[n_s + n_l:n_s + n_l + 2]
        for cp in flight.copies(src, land, send, recv):
            cp.wait_send()
            cp.wait_recv()

    mem = lambda t: pltpu.HBM(t.shape, t.dtype)
    outs = pl.pallas_call(
        body, name=name, out_shape=(*map(mem, flight.srcs), *map(mem, flight.lands)),
        in_specs=[HBM] * (n_s + n_l) + [SEM, SEM, ANY], out_specs=tuple([HBM] * (n_s + n_l)),
        input_output_aliases={i: i for i in range(n_s + n_l)},
        compiler_params=pltpu.CompilerParams(has_side_effects=DATAFLOW),
    )(*flight.srcs, *flight.lands, flight.send, flight.recv, after)
    return list(outs[:n_s]), list(outs[n_s:])


def _empty_like(shapes_from, lead):
    return [lax.empty((lead,) + t.shape[1:], t.dtype) for t in shapes_from]


def _scatter_off(name, chunks, after):
    def copies(src, land, send, recv):
        x, y, c = _place()
        return [pltpu.make_async_remote_copy(
            src_ref=src[a].at[2 * px + py], dst_ref=land[a].at[k], send_sem=send.at[3 * a + k],
            recv_sem=recv.at[3 * a + k], device_id=(px, py, c), device_id_type=MESH_ID)
            for a in range(len(chunks)) for k, (px, py) in enumerate(_other_chips(x, y))]

    return _take_off(name, chunks, _empty_like(chunks, 3), copies, 3 * len(chunks), after)


def _swap_off(name, arrs, after):
    def copies(src, land, send, recv):
        x, y, c = _place()
        return [pltpu.make_async_remote_copy(
            src_ref=src[a], dst_ref=land[a], send_sem=send.at[a], recv_sem=recv.at[a],
            device_id=(x, y, 1 - c), device_id_type=MESH_ID) for a in range(len(arrs))]

    return _take_off(name, arrs, [lax.empty(t.shape, t.dtype) for t in arrs], copies, len(arrs), after)


def _devices_off(name, block, after):
    me = 4 * lax.axis_index("x") + 2 * lax.axis_index("y") + lax.axis_index("c")
    land = lax.dynamic_update_index_in_dim(lax.empty((N_DEV,) + block.shape, block.dtype), block, me, 0)

    def copies(src, land, send, recv):
        x, y, c = _place()
        mine = 4 * x + 2 * y + c
        return [pltpu.make_async_remote_copy(
            src_ref=src[0], dst_ref=land[0].at[mine], send_sem=send.at[k - 1], recv_sem=recv.at[k - 1],
            device_id=(x ^ (k >> 2), y ^ ((k >> 1) & 1), c ^ (k & 1)), device_id_type=MESH_ID)
            for k in range(1, N_DEV)]

    return _take_off(name, [block], [land], copies, N_DEV - 1, after)


def _half_rows(shape, c, other=False):
    half = shape[0] // 2
    return pl.ds(((1 - c) if other else c) * half, half)


def _gather_start(name, shards, lands, after):
    n = len(shards)

    def body(*refs):
        src, land, (send, recv) = refs[:n], refs[n:2 * n], refs[2 * n + 1:2 * n + 3]
        x, y, c = _place()
        me = 2 * x + y
        for a in range(n):
            mine = _half_rows(shards[a].shape, c)
            for j, (px, py) in enumerate(_other_chips(x, y)):
                pltpu.make_async_remote_copy(
                    src_ref=src[a].at[mine], dst_ref=land[a].at[me, mine], send_sem=send.at[3 * a + j],
                    recv_sem=recv.at[3 * a + j], device_id=(px, py, c), device_id_type=MESH_ID).start()
        token = refs[-1]
        token[...] = jnp.zeros_like(token)

    mem = lambda t: pltpu.HBM(t.shape, t.dtype)
    pair = pltpu.SemaphoreType.DMA((3 * n,))
    outs = pl.pallas_call(
        body, name=name,
        out_shape=(pair, pair, *map(mem, shards), *map(mem, lands), SDS((SUBLANES, LANES), F32)),
        in_specs=[HBM] * (2 * n) + [ANY],
        out_specs=(SEM, SEM, *[HBM] * (2 * n), pl.BlockSpec(memory_space=pltpu.VMEM)),
        input_output_aliases={i: 2 + i for i in range(2 * n)},
        compiler_params=pltpu.CompilerParams(has_side_effects=DATAFLOW),
    )(*[pltpu.with_memory_space_constraint(t, pltpu.HBM) for t in (*shards, *lands)], after)
    return outs[0], outs[1], list(outs[2:2 + n]), list(outs[2 + n:2 + 2 * n]), outs[-1]


def _gather_pass(name, send, recv, shards, lands, after):
    n = len(shards)

    def body(*refs):
        src, land, (send, recv, _) = refs[:n], refs[n:2 * n], refs[2 * n:2 * n + 3]
        fsend, frecv = refs[2 * n + 3], refs[2 * n + 4]
        x, y, c = _place()
        me = 2 * x + y
        for a in range(n):
            mine = _half_rows(shards[a].shape, c)
            for j, (px, py) in enumerate(_other_chips(x, y)):
                far = 2 * px + py
                ici = pltpu.make_async_remote_copy(
                    src_ref=src[a].at[mine], dst_ref=land[a].at[far, mine], send_sem=send.at[3 * a + j],
                    recv_sem=recv.at[3 * a + j], device_id=(px, py, c), device_id_type=MESH_ID)
                ici.wait_recv()
                ici.wait_send()
                pltpu.make_async_remote_copy(
                    src_ref=land[a].at[far, mine], dst_ref=land[a].at[far, mine], send_sem=fsend.at[3 * a + j],
                    recv_sem=frecv.at[3 * a + j], device_id=(x, y, 1 - c), device_id_type=MESH_ID).start()
        token = refs[-1]
        token[...] = jnp.zeros_like(token)

    mem = lambda t: pltpu.HBM(t.shape, t.dtype)
    pair = pltpu.SemaphoreType.DMA((3 * n,))
    outs = pl.pallas_call(
        body, name=name,
        out_shape=(pair, pair, *map(mem, lands), SDS((SUBLANES, LANES), F32)),
        in_specs=[HBM] * (2 * n) + [SEM, SEM, ANY],
        out_specs=(SEM, SEM, *[HBM] * n, pl.BlockSpec(memory_space=pltpu.VMEM)),
        input_output_aliases={n + i: 2 + i for i in range(n)},
        compiler_params=pltpu.CompilerParams(has_side_effects=DATAFLOW),
    )(*shards, *lands, send, recv, after)
    return outs[0], outs[1], list(outs[2:2 + n]), outs[-1]


def _gather_wait(name, fsend, frecv, lands, after):
    n = len(lands)

    def body(*refs):
        land, (fsend, frecv, _) = refs[:n], refs[n:n + 3]
        x, y, c = _place()
        for a in range(n):
            for j, (px, py) in enumerate(_other_chips(x, y)):
                far = 2 * px + py
                mine = _half_rows(lands[a].shape[1:], c)
                theirs = _half_rows(lands[a].shape[1:], c, other=True)
                pltpu.make_async_remote_copy(
                    src_ref=land[a].at[far, mine], dst_ref=land[a].at[far, mine], send_sem=fsend.at[3 * a + j],
                    recv_sem=frecv.at[3 * a + j], device_id=(x, y, 1 - c), device_id_type=MESH_ID).wait_send()
                pltpu.make_async_remote_copy(
                    src_ref=land[a].at[far, theirs], dst_ref=land[a].at[far, theirs], send_sem=fsend.at[3 * a + j],
                    recv_sem=frecv.at[3 * a + j], device_id=(x, y, 1 - c), device_id_type=MESH_ID).wait_recv()

    mem = lambda t: pltpu.HBM(t.shape, t.dtype)
    return list(pl.pallas_call(
        body, name=name, out_shape=tuple(map(mem, lands)), in_specs=[HBM] * n + [SEM, SEM, ANY],
        out_specs=tuple([HBM] * n), input_output_aliases={i: i for i in range(n)},
        compiler_params=pltpu.CompilerParams(has_side_effects=DATAFLOW),
    )(*lands, fsend, frecv, after))


def _after(token):
    return _Exchange([token], [], [], lambda *_: None, lambda *_: None)


def _swap_sibling(arrs):
    n = len(arrs)

    def copies(ins, outs, sems):
        send, recv = sems
        x, y, c = _place()
        return [pltpu.make_async_remote_copy(
            src_ref=ins[a], dst_ref=outs[a], send_sem=send.at[a], recv_sem=recv.at[a],
            device_id=(x, y, 1 - c), device_id_type=MESH_ID) for a in range(n)]

    def start(ins, outs, sems):
        for cp in copies(ins, outs, sems):
            cp.start()

    def wait(ins, outs, sems):
        cps = copies(ins, outs, sems)
        for cp in cps:
            cp.wait_recv()
        for cp in cps:
            cp.wait_send()

    return _Exchange(arrs, [SDS(s.shape, s.dtype) for s in arrs],
                     [pltpu.SemaphoreType.DMA((n,)), pltpu.SemaphoreType.DMA((n,))], start, wait)


def _sum_devices(slots):
    def body(s_ref, o_ref):
        acc = s_ref[0]
        for d in range(1, N_DEV):
            acc = acc + s_ref[d]
        o_ref[...] = acc

    return pl.pallas_call(
        body, in_specs=[pl.BlockSpec(memory_space=pltpu.VMEM)], out_specs=pl.BlockSpec(memory_space=pltpu.VMEM),
        out_shape=SDS(slots.shape[1:], F32), name="sum_small",
        compiler_params=pltpu.CompilerParams(vmem_limit_bytes=32 * 1024 * 1024))(slots)


def _adam_small(ws, gs, ms, vs):
    n = len(ws)

    def body(*refs):
        for i in range(n):
            w_ref, g_ref, m_ref, v_ref = (refs[k * n + i] for k in range(4))
            outs = _adam_math(w_ref[...], g_ref[...], m_ref[...], v_ref[...])
            for k in range(3):
                refs[(4 + k) * n + i][...] = outs[k]

    vmem = pl.BlockSpec(memory_space=pltpu.VMEM)
    return pl.pallas_call(
        body, in_specs=[vmem] * (4 * n), out_specs=[vmem] * (3 * n),
        out_shape=[SDS(w.shape, F32) for w in ws] * 3, name="adam_small",
        compiler_params=pltpu.CompilerParams(vmem_limit_bytes=32 * 1024 * 1024))(*ws, *gs, *ms, *vs)


def _local_step(x, target, small, big, tb, distributed):
    dist = distributed
    me = (2 * lax.axis_index("x") + lax.axis_index("y")) if dist else 0
    tb_ssm = min(tb, 256)
    bucket = jnp.asarray(_bucket_table())
    place_own = lambda t: lax.dynamic_update_index_in_dim(lax.empty((N_CHIPS,) + t.shape, t.dtype), t, me, 0)
    if dist:
        in_legs = _gather_start("gather_in_start", [big["w_in"]], [place_own(big["w_in"])], small["d_skip"])
        names = sorted(small)
        in_token, values = lax.optimization_barrier((in_legs[4], [small[n] for n in names]))
        small = dict(zip(names, values))
    g1, g2, g3, g4 = small["norm_mix_pre"], small["norm_mix_post"], small["norm_mlp_pre"], small["norm_mlp_post"]

    keys_first = lambda t: jnp.swapaxes(t, -1, -2)
    bias = _bias_table(small["rel_bias"], bucket)
    sink_rows = keys_first(_pair_layout(jnp.broadcast_to(small["sinks"].reshape(N_HEADS, 1, 1), (N_HEADS, BLOCK, 1))))
    disc_args = (small["lam_re"], small["lam_im"], small["log_dt"], small["b_re"], small["b_im"])
    (ab_re, ab_im, bb_re, bb_im), disc_vjp = jax.vjp(_ssm_discretize, *disc_args)
    tab_f, tab_b = _scan_tables(ab_re, ab_im)
    bmat = _bf(_b_matrix(bb_re, bb_im))
    cmat = _bf(_c_matrix(small["c_re"], small["c_im"]))
    d_skip = small["d_skip"]

    mix = ("w_glu", "w_attn_branch", "w_ssm_branch", "w_out")
    rest = [big[n] for n in mix + ("w_ff_in", "w_ff_out")]
    if dist:
        send, recv, src, lands, _ = in_legs
        rest_lands = [place_own(t) for t in rest]
        corner = lambda t: t.reshape(-1, t.shape[-1])[:1, :LANES].astype(F32)
        prepared = sum(map(corner, [tab_b, bias, sink_rows, bmat, cmat] + rest_lands), in_token[:1])
        send, recv, lands, token = _gather_pass("gather_in_pass", send, recv, src, lands, prepared)
        (g_in,) = _gather_wait("gather_in_wait", send, recv, lands, token)
        w_in = g_in.reshape(IN_W, D_MODEL)
    else:
        w_in = big["w_in"]
    token = None
    if dist:
        send, recv, rest, lands, token = _gather_start("gather_rest_start", rest, rest_lands, g_in)
    h1, q, k, v, u, ga, gs = _inproj_fwd(x, g1, w_in, tb, _after(token) if dist else None)
    s, h = _ssm_fwd(u, bmat, cmat, tab_f, d_skip, tb_ssm)
    if dist:
        send, recv, lands, token = _gather_pass("gather_rest_pass", send, recv, rest, lands, s)
    att = _attn_fwd(q, k, v, bias, sink_rows, _after(token) if dist else None)[0]
    if dist:
        rest = _gather_wait("gather_rest_wait", send, recv, lands, att)
    w_glu, w_ab, w_sb, w_out, w_ffi, w_ffo = rest
    w_glu = w_glu.reshape(SSM_W, SSM_W)
    w_out = w_out.reshape(D_MODEL, D_MODEL)
    w_ffi = [w_ffi]
    x2 = _merge_fwd(x, s, att, ga, gs, g2, w_glu, w_ab, w_sb, w_out, tb)
    dy, df, h3, loss_acc, dg4 = _mlp_fwd_loss(x2, target, g3, g4, w_ffi, w_ffo, tb)

    dx2, act, da, dg3 = _mlp_bwd(x2, dy, df, h3, g3, w_ffi, w_ffo, tb)
    tl = min(2048, x.shape[0])
    chunked = (N_CHIPS, D_FF // N_CHIPS, D_MODEL)
    d_ffi, b_ffi = _matmul_tn("grad_w_ff_in", h3, da, D_MODEL, D_FF // FF_CHUNKS, tl, True)
    d_ffo, b_ffo = _matmul_tn("grad_w_ff_out", act, df, D_FF // FF_CHUNKS, D_MODEL, tl, False)
    d_ffo, b_ffo = d_ffo.reshape(chunked), b_ffo.reshape(chunked)
    behind = lambda flight: _after(flight.token) if dist else None
    ff_fl = _scatter_off("scatter_ff_off", [b_ffi, b_ffo], d_ffo) if dist else None
    outs = _merge_bwd(dx2, s, att, ga, gs, g2, w_glu, w_ab, w_sb, w_out, tb_ssm, behind(ff_fl))
    ds, datt, dga, dgs, dg2, d_glu, d_ab, d_sb, d_out, b_glu, b_ab, b_sb, b_out = outs
    glu4, out4 = (N_CHIPS, SSM_W // N_CHIPS, SSM_W), (N_CHIPS, D_MODEL // N_CHIPS, D_MODEL)
    d_mix = [d_glu.reshape(glu4), d_ab, d_sb, d_out.reshape(out4)]
    b_mix = [b_glu.reshape(glu4), b_ab, b_sb, b_out.reshape(out4)]
    mix_fl = _scatter_off("scatter_mix_off", b_mix, d_mix[-1]) if dist else None
    du, d_bmat, d_cmat, da_acc, dd_skip = _ssm_bwd(
        ds, u, h, bmat.transpose(0, 2, 1), cmat.transpose(0, 2, 1), tab_b, d_skip, tb_ssm, behind(mix_fl))
    dq, dk, dv, dbias, dsink_rows = _attn_bwd(q, k, v, datt, bias, sink_rows)
    swap_fl = None
    if dist:
        r_ffi, r_ffo = _land("scatter_ff_land", ff_fl, dq)[1]
        p_ffi = _sum4("sum_w_ff_in", d_ffi, r_ffi, me)
        p_ffo = _sum4("sum_w_ff_out", d_ffo, r_ffo, me)
        swap_fl = _swap_off("swap_ff_off", [p_ffi, p_ffo], r_ffo)
    dx, dpj, dg1 = _inproj_bwd(x, dx2, dq, dk, dv, du, dga, dgs, g1, w_in, tb, behind(swap_fl))

    dab_re, dab_im = _state_unlayout(jnp.sum(da_acc, axis=0))
    dbb_re, dbb_im = _b_matrix_grad(d_bmat)
    d_lam_re, d_lam_im, d_log_dt, d_b_re, d_b_im = disc_vjp((dab_re, dab_im, dbb_re, dbb_im))
    d_c_re, d_c_im = _c_matrix_grad(d_cmat)
    d_rel = _bias_grad(dbias, bucket)
    d_sinks = jnp.sum(_pair_unlayout(keys_first(dsink_rows)), axis=(1, 2))
    small_grads = dict(
        norm_mix_pre=dg1, norm_mix_post=dg2, norm_mlp_pre=dg3, norm_mlp_post=dg4, rel_bias=d_rel, sinks=d_sinks,
        lam_re=d_lam_re, lam_im=d_lam_im, log_dt=d_log_dt, b_re=d_b_re, b_im=d_b_im, c_re=d_c_re, c_im=d_c_im,
        d_skip=dd_skip)
    small_fl = _devices_off("small_off", _pack(small_grads, loss_acc), swap_fl.token) if dist else None
    outs = _matmul_tn("grad_w_in", dpj, h1, IN_W // 2, D_MODEL, tl, False, behind(small_fl))
    in4 = (N_CHIPS, IN_W // N_CHIPS, D_MODEL)
    d_in, b_in = outs[0].reshape(in4), outs[1].reshape(in4)
    if not dist:
        return loss_acc, dx, small_grads, dict(zip(BIG, [d_in] + d_mix + [d_ffi, d_ffo]))
    (p_ffi, p_ffo), (s_ffi, s_ffo) = _land("swap_ff_land", swap_fl, b_in)
    r_mix = _land("scatter_mix_land", mix_fl, b_in)[1]
    p_mix = [_sum4("sum_" + n, d, r, me) for n, d, r in zip(mix, d_mix, r_mix)]
    pending = dict(d_in=d_in, b_in=b_in, p_mix=p_mix, w_ff_in=(p_ffi, s_ffi), w_ff_out=(p_ffo, s_ffo), me=me)
    return loss_acc, dx, small_fl, pending


SMALL = ['norm_mix_pre', 'norm_mix_post', 'norm_mlp_pre', 'norm_mlp_post', 'rel_bias', 'sinks', 'lam_re', 'lam_im',
         'log_dt', 'b_re', 'b_im', 'c_re', 'c_im', 'd_skip']
BIG = ['w_in', 'w_glu', 'w_attn_branch', 'w_ssm_branch', 'w_out', 'w_ff_in', 'w_ff_out']
WEIGHTS = ['norm_mix_pre', 'norm_mix_post', 'norm_mlp_pre', 'norm_mlp_post', 'w_in', 'rel_bias', 'sinks', 'lam_re',
           'lam_im', 'log_dt', 'b_re', 'b_im', 'c_re', 'c_im', 'd_skip', 'w_glu', 'w_attn_branch', 'w_ssm_branch',
           'w_out', 'w_ff_in', 'w_ff_out']
PACK_COLS = 1024
PACK_ORDER = ['b_re', 'b_im', 'c_re', 'c_im', 'lam_re', 'lam_im', 'norm_mix_pre', 'norm_mix_post', 'norm_mlp_pre',
              'norm_mlp_post', 'rel_bias', 'sinks', 'log_dt', 'd_skip']


STATE_MINOR = ('b_re', 'b_im')
PACK_ROWS = 144
LOSS_ROW = 140


def _pack(named, loss_acc):
    parts = []
    for n in PACK_ORDER:
        a = jnp.swapaxes(named[n], -1, -2) if n in STATE_MINOR else named[n]
        flat = a.reshape(-1)
        rows = -(-flat.shape[0] // PACK_COLS)
        parts.append(jnp.pad(flat, (0, rows * PACK_COLS - flat.shape[0])).reshape(rows, PACK_COLS))
    assert sum(p.shape[0] for p in parts) == LOSS_ROW
    parts.append(jnp.pad(loss_acc[0:1], ((0, PACK_ROWS - LOSS_ROW - 1), (0, PACK_COLS - loss_acc.shape[1]))))
    return jnp.concatenate(parts, axis=0)


def _unpack(packed, shapes):
    out, at = {}, 0
    for n in PACK_ORDER:
        shape = shapes[n][:-2] + (shapes[n][-1], shapes[n][-2]) if n in STATE_MINOR else shapes[n]
        size = int(np.prod(shape))
        rows = -(-size // PACK_COLS)
        blk = packed[at:at + rows]
        out[n] = (blk.reshape(-1)[:size] if size % PACK_COLS else blk).reshape(shape)
        at += rows
    return out


def kernel(x, norm_mix_pre, norm_mix_post, norm_mlp_pre, norm_mlp_post, w_in, rel_bias, sinks, lam_re, lam_im, log_dt, b_re, b_im, c_re, c_im, d_skip, w_glu, w_attn_branch, w_ssm_branch, w_out, w_ff_in, w_ff_out, loss_target, m_norm_mix_pre, m_norm_mix_post, m_norm_mlp_pre, m_norm_mlp_post, m_w_in, m_rel_bias, m_sinks, m_lam_re, m_lam_im, m_log_dt, m_b_re, m_b_im, m_c_re, m_c_im, m_d_skip, m_w_glu, m_w_attn_branch, m_w_ssm_branch, m_w_out, m_w_ff_in, m_w_ff_out, v_norm_mix_pre, v_norm_mix_post, v_norm_mlp_pre, v_norm_mlp_post, v_w_in, v_rel_bias, v_sinks, v_lam_re, v_lam_im, v_log_dt, v_b_re, v_b_im, v_c_re, v_c_im, v_d_skip, v_w_glu, v_w_attn_branch, v_w_ssm_branch, v_w_out, v_w_ff_in, v_w_ff_out):
    env = dict(locals())
    w = {n: env[n] for n in WEIGHTS}
    m = {n: env["m_" + n] for n in WEIGHTS}
    v = {n: env["v_" + n] for n in WEIGHTS}
    seq = x.shape[1]
    tb = min(512, seq)

    small = {n: w[n] for n in ('norm_mix_pre', 'norm_mix_post', 'norm_mlp_pre', 'norm_mlp_post', 'rel_bias')}
    small.update({n: w[n][0] for n in ('sinks', 'lam_re', 'lam_im', 'log_dt', 'b_re', 'b_im', 'c_re', 'c_im')})
    small['d_skip'] = w['d_skip']
    shard = lambda t, n: t[n][0].T if n == 'w_in' else t[n][0]
    unshard = lambda a, n: (a.T if n == 'w_in' else a)[None]
    _, dx, small_fl, pending = _local_step(
        x[0], loss_target[0], small, {n: _bf(shard(w, n)) for n in BIG}, tb, True)

    grads, deltas, new_m, new_v = {}, {}, {}, {}

    def adam(n, partials, after=None):
        outs = _adam_pair("adam_" + n, (shard(w, n), *partials, shard(m, n), shard(v, n)), after)
        grads[n], deltas[n], new_m[n], new_v[n] = [unshard(a, n) for a in outs]
        return outs[3]

    mix = ("w_glu", "w_attn_branch", "w_ssm_branch", "w_out")
    in_fl = _scatter_off("scatter_w_in_off", [pending["b_in"]], pending["d_in"])
    sib_mix = _exchange_alone("swap_mix", _swap_sibling(pending["p_mix"]))
    last = None
    for n, partials in [(n, pending[n]) for n in ("w_ff_in", "w_ff_out")] + list(zip(mix, zip(pending["p_mix"], sib_mix))):
        last = adam(n, partials, in_fl.token)
    (r_in,) = _land("scatter_w_in_land", in_fl, last)[1]
    p_in = _sum4("sum_w_in", pending["d_in"], r_in, pending["me"])
    (s_in,) = _exchange_alone("swap_w_in", _swap_sibling([p_in]))
    adam("w_in", (p_in, s_in))

    small_g = _sum_devices(_land("small_land", small_fl, last)[1][0])
    loss = small_g[LOSS_ROW, 0]
    minor = lambda t, n: jnp.swapaxes(t, -1, -2) if n in STATE_MINOR else t
    g_small = _unpack(small_g, {n: w[n].shape for n in SMALL})
    outs = _adam_small([minor(w[n], n) for n in SMALL], [g_small[n] for n in SMALL],
                       [minor(m[n], n) for n in SMALL], [minor(v[n], n) for n in SMALL])
    grads.update({n: minor(g_small[n], n) for n in SMALL})
    for k, dst in enumerate((deltas, new_m, new_v)):
        dst.update({n: minor(a, n) for n, a in zip(SMALL, outs[k * len(SMALL):(k + 1) * len(SMALL)])})

    return (loss, dx[None], *[grads[n] for n in WEIGHTS], *[deltas[n] for n in WEIGHTS],
            *[new_m[n] for n in WEIGHTS], *[new_v[n] for n in WEIGHTS])
```

```python
import functools
import math

import numpy as np
import jax
import jax.numpy as jnp
from jax import lax
from jax.experimental import pallas as pl
from jax.experimental.pallas import tpu as pltpu

F32 = jnp.float32
BF16 = jnp.bfloat16

D_MODEL = 1024
N_HEADS = 8
N_KV = 2
Q_GROUP = 4
HEAD_DIM = 64
ATTN_W = 512
KV_W = 128
BLOCK = 128
N_BUCKETS = 32
MAX_DISTANCE = 128
NEG_INF = -1e30
SSM_W = 512
SSM_GROUP = 16
SSM_GROUPS = 32
SSM_STATE = 64
N_SUPER = 4
GROUPS_PER_SUPER = SSM_GROUPS // N_SUPER
SUPER_IN = GROUPS_PER_SUPER * SSM_GROUP
SUPER_HALF = GROUPS_PER_SUPER * SSM_STATE
SUPER_W = 2 * SUPER_HALF
STATE_COLS = N_SUPER * SUPER_W
D_FF = 4096
FF_CHUNKS = 4
IN_W = 3328
SPLITS = (0, 512, 640, 768, 1280, 2304, 3328)
RMS_EPS = 1e-6
N_CHIPS = 4
N_DEV = 8
SUBLANES = 8
LANES = 128
STATE_TILES = STATE_COLS // LANES
SUPER_TILES = SUPER_W // LANES

ADAM_LR = 0.001
ADAM_B1 = 0.9
ADAM_B2 = 0.999
ADAM_EPS = 1e-08
ADAM_WD = 0.01
ADAM_STEP = 10

VMEM_BIG = 56 * 1024 * 1024
SDS = jax.ShapeDtypeStruct
MESH_ID = pl.DeviceIdType.MESH
ANY = pl.BlockSpec(memory_space=pl.ANY)


def _bf(x):
    return x.astype(BF16)


def _mm(a, b):
    return jnp.dot(a, b, preferred_element_type=F32)


def _mm_nt(a, b):
    return lax.dot_general(a, b, (((1,), (1,)), ((), ())), preferred_element_type=F32)


def _mm_tn(a, b):
    return lax.dot_general(a, b, (((0,), (0,)), ((), ())), preferred_element_type=F32)


def _sig(x):
    return 1.0 / (1.0 + jnp.exp(-x))


def _rms(x, g):
    r = lax.rsqrt(jnp.mean(x * x, axis=-1, keepdims=True) + RMS_EPS)
    xh = x * r
    return xh * g, xh, r


def _rms_bwd(dout, xh, r, g):
    dg = jnp.sum(dout * xh, axis=0, keepdims=True)
    dxh = dout * g
    dx = r * (dxh - xh * jnp.mean(dxh * xh, axis=-1, keepdims=True))
    return dx, dg


_GELU_C = math.sqrt(2.0 / math.pi)


def _gelu_and_grad(x):
    x2 = x * x
    inner = _GELU_C * (x + 0.044715 * (x2 * x))
    t = jnp.tanh(inner)
    y = 0.5 * x * (1.0 + t)
    dy = 0.5 * (1.0 + t) + 0.5 * x * (1.0 - t * t) * (_GELU_C * (1.0 + 3.0 * 0.044715 * x2))
    return y, dy


def _zero_map(nd, *_):
    return (0,) * nd


def _params(n_axes, vmem=None):
    return pltpu.CompilerParams(dimension_semantics=("arbitrary",) * n_axes, vmem_limit_bytes=vmem)


class _Exchange:
    def __init__(self, ins, outs, sems, start, wait):
        self.ins, self.outs, self.sems, self.start, self.wait = list(ins), list(outs), list(sems), start, wait


def _fused_call(name, body, grid, in_specs, out_specs, out_shape, scratch, args, exchange, params):
    n_in, n_out, n_scr = len(in_specs), len(out_specs), len(scratch)
    if exchange is None:
        fn = body
    else:
        ex = exchange
        n_xi, n_xo = len(ex.ins), len(ex.outs)

        def fn(*refs):
            at = 0
            parts = []
            for n in (n_in, n_xi, n_out, n_xo, n_scr, len(ex.sems)):
                parts.append(refs[at:at + n])
                at += n
            ins, x_in, outs, x_out, scr, x_sem = parts
            ids = [pl.program_id(a) for a in range(len(grid))]
            first = functools.reduce(jnp.logical_and, [i == 0 for i in ids])
            last = functools.reduce(jnp.logical_and, [i == g - 1 for i, g in zip(ids, grid)])

            @pl.when(first)
            def _():
                ex.start(x_in, x_out, x_sem)

            body(*ins, *outs, *scr)

            @pl.when(last)
            def _():
                ex.wait(x_in, x_out, x_sem)

        in_specs = list(in_specs) + [ANY] * n_xi
        out_specs = list(out_specs) + [ANY] * n_xo
        out_shape = list(out_shape) + ex.outs
        scratch = list(scratch) + ex.sems
        args = list(args) + ex.ins
    return pl.pallas_call(fn, grid=grid, in_specs=in_specs, out_specs=out_specs, out_shape=out_shape,
                          scratch_shapes=list(scratch), name=name, compiler_params=params)(*args)


def _exchange_alone(name, ex):
    def body(*refs):
        n_xi, n_xo = len(ex.ins), len(ex.outs)
        x_in, x_out, x_sem = refs[:n_xi], refs[n_xi:n_xi + n_xo], refs[n_xi + n_xo:]
        ex.start(x_in, x_out, x_sem)
        ex.wait(x_in, x_out, x_sem)

    return pl.pallas_call(body, in_specs=[ANY] * len(ex.ins), out_specs=[ANY] * len(ex.outs), out_shape=ex.outs,
                          scratch_shapes=ex.sems, name=name)(*ex.ins)


def _rowcall(name, body, seq, tb, rows, consts, row_outs, acc_outs, scratch=(), reverse=False, vmem=None,
             exchange=None):
    nb = seq // tb
    rmap = (lambda i: (nb - 1 - i, 0)) if reverse else (lambda i: (i, 0))
    tmap = lambda i: (0,) + rmap(i)

    def row_spec(width):
        if isinstance(width, tuple):
            return pl.BlockSpec((width[0], tb, width[1]), tmap)
        return pl.BlockSpec((tb, width), rmap)

    def row_shape(width):
        return (width[0], seq, width[1]) if isinstance(width, tuple) else (seq, width)

    in_specs = [row_spec(a.shape[1] if a.ndim == 2 else (a.shape[0], a.shape[2])) for a in rows]
    in_specs += [pl.BlockSpec(a.shape, functools.partial(_zero_map, a.ndim), pipeline_mode=pl.Buffered(1))
                 for a in consts]
    out_specs = [row_spec(c) for c, _ in row_outs] + [ANY] * len(acc_outs)
    out_shape = [SDS(row_shape(c), dt) for c, dt in row_outs] + [SDS(s, dt) for s, dt in acc_outs]
    n_main = len(rows) + len(consts) + len(row_outs)
    n_acc = len(acc_outs)

    def fn(*refs):
        main, acc_hbm, rest = refs[:n_main], refs[n_main:n_main + n_acc], refs[n_main + n_acc:]
        acc_vmem, own = rest[:n_acc], rest[n_acc:]
        body(*main, *acc_vmem, *own)

        @pl.when(pl.program_id(0) == nb - 1)
        def _():
            for src, dst in zip(acc_vmem, acc_hbm):
                pltpu.sync_copy(src, dst)

    buffers = [pltpu.VMEM(s, dt) for s, dt in acc_outs] + list(scratch)
    return _fused_call(name, fn if acc_outs else body, (nb,), in_specs, out_specs, out_shape, buffers,
                       [*rows, *consts], exchange, _params(1, vmem))


def _inproj_fwd(x, g1, w_in, tb, exchange=None):
    seq = x.shape[0]

    def body(x_ref, g_ref, w_ref, h_ref, q_ref, k_ref, v_ref, u_ref, ga_ref, gs_ref):
        h, _, _ = _rms(x_ref[...], g_ref[...])
        hb = _bf(h)
        h_ref[...] = hb
        pj = _mm_nt(hb, w_ref[...])
        q_ref[...] = _bf(pj[:, SPLITS[0]:SPLITS[1]])
        k_ref[...] = _bf(pj[:, SPLITS[1]:SPLITS[2]])
        v_ref[...] = _bf(pj[:, SPLITS[2]:SPLITS[3]])
        u_ref[...] = pj[:, SPLITS[3]:SPLITS[4]]
        ga_ref[...] = pj[:, SPLITS[4]:SPLITS[5]]
        gs_ref[...] = pj[:, SPLITS[5]:SPLITS[6]]

    return _rowcall("inproj_fwd", body, seq, tb, [x], [g1, w_in],
                    [(D_MODEL, BF16), (ATTN_W, BF16), (KV_W, BF16), (KV_W, BF16), (SSM_W, F32),
                     (D_MODEL, F32), (D_MODEL, F32)], [], vmem=VMEM_BIG, exchange=exchange)


def _inproj_bwd(x, dx2, dq, dk, dv, du, dga, dgs, g1, w_in, tb, exchange=None):
    seq = x.shape[0]

    def body(x_ref, dx2_ref, dq_ref, dk_ref, dv_ref, du_ref, dga_ref, dgs_ref, g_ref, w_ref,
             dx_ref, dpj_ref, dg_ref):
        @pl.when(pl.program_id(0) == 0)
        def _():
            dg_ref[...] = jnp.zeros_like(dg_ref)

        dpj = jnp.concatenate([dq_ref[...], dk_ref[...], dv_ref[...], _bf(du_ref[...]),
                               dga_ref[...], dgs_ref[...]], axis=1)
        dpj_ref[...] = dpj
        dh = _mm(dpj, w_ref[...])
        g = g_ref[...]
        _, xh, r = _rms(x_ref[...], g)
        dxn, dg = _rms_bwd(dh, xh, r, g)
        dx_ref[...] = dx2_ref[...] + dxn
        dg_ref[...] += dg

    return _rowcall("inproj_bwd", body, seq, tb, [x, dx2, dq, dk, dv, du, dga, dgs], [g1, w_in],
                    [(D_MODEL, F32), (IN_W, BF16)], [((1, D_MODEL), F32)], vmem=VMEM_BIG, exchange=exchange)


def _bucket_table():
    qi = np.arange(BLOCK)[:, None]
    kj = np.arange(2 * BLOCK)[None, :]
    dist = qi + BLOCK - kj
    max_exact = N_BUCKETS // 2
    d = np.maximum(dist, 0)
    df = np.maximum(d, 1).astype(np.float32)
    large = max_exact + (np.log(df / np.float32(max_exact)) / np.float32(math.log(MAX_DISTANCE / max_exact))
                         * np.float32(N_BUCKETS - max_exact)).astype(np.int32)
    large = np.minimum(large, N_BUCKETS - 1)
    bucket = np.where(d < max_exact, d, large)
    valid = (dist >= 0) & (dist < BLOCK)
    return np.where(valid, bucket, -1).astype(np.int32)


def _bias_table(rel_bias, bucket):
    def body(rb_ref, bk_ref, o_ref):
        bk = bk_ref[...]
        has_prev = lax.broadcasted_iota(jnp.int32, bk.shape, 1) >= BLOCK
        for h in range(N_HEADS):
            kh, j, par = h // Q_GROUP, (h // 2) % 2, h % 2
            acc = jnp.full((BLOCK, 2 * BLOCK), NEG_INF, F32)
            for b in range(N_BUCKETS):
                acc = jnp.where(bk == b, rb_ref[b, h], acc)
            o_ref[0, kh, par, :, j * BLOCK:(j + 1) * BLOCK] = jnp.where(has_prev, acc, NEG_INF).T
            o_ref[1, kh, par, :, j * BLOCK:(j + 1) * BLOCK] = acc.T

    return pl.pallas_call(
        body, out_shape=SDS((2, N_KV, 2, 2 * BLOCK, 2 * BLOCK), F32),
        in_specs=[pl.BlockSpec(memory_space=pltpu.SMEM), pl.BlockSpec(memory_space=pltpu.VMEM)],
        out_specs=pl.BlockSpec(memory_space=pltpu.VMEM), name="bias_table",
    )(rel_bias, bucket)


def _bias_grad(dbias, bucket):
    def body(db_ref, bk_ref, o_ref):
        bk = bk_ref[...]
        for h in range(N_HEADS):
            kh, j, par = h // Q_GROUP, (h // 2) % 2, h % 2
            db = db_ref[kh, par, :, j * BLOCK:(j + 1) * BLOCK].T
            for b in range(N_BUCKETS):
                o_ref[b, h] = jnp.sum(jnp.where(bk == b, db, 0.0))

    return pl.pallas_call(
        body, out_shape=SDS((N_BUCKETS, N_HEADS), F32),
        in_specs=[pl.BlockSpec(memory_space=pltpu.VMEM), pl.BlockSpec(memory_space=pltpu.VMEM)],
        out_specs=pl.BlockSpec(memory_space=pltpu.SMEM), name="bias_grad",
    )(dbias, bucket)


TILE = 2 * HEAD_DIM


def _pair_layout(t):
    lead = t.shape[:-3]
    t = t.reshape(lead + (N_KV, 2, 2) + t.shape[-2:])
    nl = len(lead)
    t = jnp.transpose(t, tuple(range(nl)) + (nl, nl + 2, nl + 1, nl + 3, nl + 4))
    return t.reshape(lead + (N_KV, 2, 2 * BLOCK, t.shape[-1]))


def _pair_unlayout(t):
    t = t.reshape(N_KV, 2, 2, BLOCK, t.shape[-1]).transpose(0, 2, 1, 3, 4)
    return t.reshape(N_HEADS, BLOCK, t.shape[-1])


def _halves(t):
    tf = t.astype(F32)
    low = lax.broadcasted_iota(jnp.int32, tf.shape, 1) < HEAD_DIM
    swapped = pltpu.roll(tf, HEAD_DIM, 1)
    zero = jnp.zeros_like(tf)
    return ((_bf(jnp.where(low, tf, zero)), _bf(jnp.where(low, zero, swapped))),
            (_bf(jnp.where(low, swapped, zero)), _bf(jnp.where(low, zero, tf))))


def _fold_halves(even, odd):
    low = lax.broadcasted_iota(jnp.int32, even.shape, 1) < HEAD_DIM
    comb = jnp.where(low, even, odd)
    return comb + pltpu.roll(comb, HEAD_DIM, 1)


def _tile_rows(ref, kh):
    return jnp.concatenate([ref[:, (2 * kh) * TILE:(2 * kh + 1) * TILE],
                            ref[:, (2 * kh + 1) * TILE:(2 * kh + 2) * TILE]], axis=0)


def _halves_t(t):
    tt = t.astype(F32).T
    top = lax.broadcasted_iota(jnp.int32, tt.shape, 0) < HEAD_DIM
    swapped = jnp.concatenate([tt[HEAD_DIM:], tt[:HEAD_DIM]], axis=0)
    zero = jnp.zeros_like(tt)
    return ((_bf(jnp.where(top, tt, zero)), _bf(jnp.where(top, zero, swapped))),
            (_bf(jnp.where(top, swapped, zero)), _bf(jnp.where(top, zero, tt))))


def _attn_probs(km, qk, bias, sink):
    lg = _mm_nt(km, qk) * (HEAD_DIM ** -0.5) + bias
    m = jnp.maximum(jnp.max(lg, axis=0, keepdims=True), sink)
    p = jnp.exp(lg - m)
    es = jnp.exp(sink - m)
    inv = 1.0 / (jnp.sum(p, axis=0, keepdims=True) + es)
    return p * inv, es * inv


def _attn_fwd(q, k, v, bias, sink_rows, exchange=None):
    seq = q.shape[0]
    nblk = seq // BLOCK

    def body(q_ref, kp_ref, kc_ref, vp_ref, vc_ref, b_ref, s_ref, o_ref):
        which = jnp.minimum(pl.program_id(0), 1)
        kms = _halves(jnp.concatenate([kp_ref[...], kc_ref[...]], axis=0))
        vts = _halves_t(jnp.concatenate([vp_ref[...], vc_ref[...]], axis=0))
        for kh in range(N_KV):
            qk = _tile_rows(q_ref, kh)
            acc = jnp.zeros((TILE, 2 * BLOCK), F32)
            for par in range(2):
                pr, _ = _attn_probs(kms[kh][par], qk, b_ref[which, kh, par], s_ref[kh, par])
                acc = acc + _mm(vts[kh][par], _bf(pr))
            acc = acc.T
            o_ref[:, (2 * kh) * TILE:(2 * kh + 1) * TILE] = _bf(acc[:BLOCK])
            o_ref[:, (2 * kh + 1) * TILE:(2 * kh + 2) * TILE] = _bf(acc[BLOCK:])

    cur = lambda n: (n, 0)
    prev = lambda n: (jnp.maximum(n - 1, 0), 0)
    return _fused_call(
        "attn_fwd", body, (nblk,),
        [pl.BlockSpec((BLOCK, ATTN_W), cur),
         pl.BlockSpec((BLOCK, KV_W), prev), pl.BlockSpec((BLOCK, KV_W), cur),
         pl.BlockSpec((BLOCK, KV_W), prev), pl.BlockSpec((BLOCK, KV_W), cur),
         pl.BlockSpec(bias.shape, functools.partial(_zero_map, bias.ndim)),
         pl.BlockSpec(sink_rows.shape, functools.partial(_zero_map, sink_rows.ndim))],
        [pl.BlockSpec((BLOCK, ATTN_W), cur)], [SDS((seq, ATTN_W), BF16)], [],
        [q, k, k, v, v, bias, sink_rows], exchange, _params(1))


def _attn_bwd(q, k, v, d_out, bias, sink_rows, exchange=None):
    seq = q.shape[0]
    nblk = seq // BLOCK

    def body(q_ref, kp_ref, kc_ref, vp_ref, vc_ref, do_ref, b_ref, s_ref,
             dq_ref, dk_ref, dv_ref, db_ref, ds_ref, ck_ref, cv_ref):
        n = pl.program_id(0)

        @pl.when(n == 0)
        def _():
            db_ref[...] = jnp.zeros_like(db_ref)
            ds_ref[...] = jnp.zeros_like(ds_ref)
            ck_ref[...] = jnp.zeros_like(ck_ref)
            cv_ref[...] = jnp.zeros_like(cv_ref)

        @pl.when(n < nblk)
        def _():
            which = jnp.minimum(n, 1)
            scale = HEAD_DIM ** -0.5
            kcat = jnp.concatenate([kp_ref[...], kc_ref[...]], axis=0)
            kms = _halves(kcat)
            kts = _halves_t(kcat)
            vms = _halves(jnp.concatenate([vp_ref[...], vc_ref[...]], axis=0))
            dks, dvs = [], []
            for kh in range(N_KV):
                qk = _tile_rows(q_ref, kh)
                dok = _tile_rows(do_ref, kh)
                dq = jnp.zeros((TILE, 2 * BLOCK), F32)
                dkp, dvp = [], []
                for par in range(2):
                    pr, ps = _attn_probs(kms[kh][par], qk, b_ref[which, kh, par], s_ref[kh, par])
                    dp = _mm_nt(vms[kh][par], dok)
                    rs = jnp.sum(pr * dp, axis=0, keepdims=True)
                    dlg = pr * (dp - rs)
                    ds_ref[kh, par] += -ps * rs
                    db_ref[kh, par] += dlg
                    dlb = _bf(dlg)
                    dq = dq + _mm(kts[kh][par], dlb)
                    dkp.append(_mm(dlb, qk))
                    dvp.append(_mm(_bf(pr), dok))
                dq = _bf((dq * scale).T)
                dq_ref[:, (2 * kh) * TILE:(2 * kh + 1) * TILE] = dq[:BLOCK]
                dq_ref[:, (2 * kh + 1) * TILE:(2 * kh + 2) * TILE] = dq[BLOCK:]
                dks.append(_fold_halves(*dkp))
                dvs.append(_fold_halves(*dvp))
            low = lax.broadcasted_iota(jnp.int32, (2 * BLOCK, TILE), 1) < HEAD_DIM
            dkk = jnp.where(low, dks[0], dks[1]) * scale
            dvv = jnp.where(low, dvs[0], dvs[1])
            dk_ref[...] = _bf(ck_ref[...] + dkk[:BLOCK])
            ck_ref[...] = dkk[BLOCK:]
            dv_ref[...] = _bf(cv_ref[...] + dvv[:BLOCK])
            cv_ref[...] = dvv[BLOCK:]

        @pl.when(n == nblk)
        def _():
            dk_ref[...] = _bf(ck_ref[...])
            dv_ref[...] = _bf(cv_ref[...])

    cur = lambda n: (jnp.minimum(n, nblk - 1), 0)
    prev = lambda n: (jnp.maximum(jnp.minimum(n, nblk - 1) - 1, 0), 0)
    late = lambda n: (jnp.maximum(n - 1, 0), 0)
    kv_spec = lambda m: pl.BlockSpec((BLOCK, KV_W), m)
    acc_b = pl.BlockSpec(bias.shape[1:], functools.partial(_zero_map, bias.ndim - 1))
    acc_s = pl.BlockSpec(sink_rows.shape, functools.partial(_zero_map, sink_rows.ndim))
    return _fused_call(
        "attn_bwd", body, (nblk + 1,),
        [pl.BlockSpec((BLOCK, ATTN_W), cur), kv_spec(prev), kv_spec(cur), kv_spec(prev), kv_spec(cur),
         pl.BlockSpec((BLOCK, ATTN_W), cur),
         pl.BlockSpec(bias.shape, functools.partial(_zero_map, bias.ndim)), acc_s],
        [pl.BlockSpec((BLOCK, ATTN_W), cur), kv_spec(late), kv_spec(late), acc_b, acc_s],
        [SDS((seq, ATTN_W), BF16), SDS((seq, KV_W), BF16), SDS((seq, KV_W), BF16),
         SDS(bias.shape[1:], F32), SDS(sink_rows.shape, F32)],
        [pltpu.VMEM((BLOCK, KV_W), F32), pltpu.VMEM((BLOCK, KV_W), F32)],
        [q, k, k, v, v, d_out, bias, sink_rows], exchange, _params(1))


def _ssm_discretize(lam_re, lam_im, log_dt, b_re, b_im):
    dt = jnp.exp(log_dt)[:, None]
    mag = jnp.exp(lam_re * dt)
    ab_re = mag * jnp.cos(lam_im * dt)
    ab_im = mag * jnp.sin(lam_im * dt)
    nr = ab_re - 1.0
    den = lam_re * lam_re + lam_im * lam_im
    f_re = (nr * lam_re + ab_im * lam_im) / den
    f_im = (ab_im * lam_re - nr * lam_im) / den
    bb_re = f_re[..., None] * b_re - f_im[..., None] * b_im
    bb_im = f_re[..., None] * b_im + f_im[..., None] * b_re
    return ab_re, ab_im, bb_re, bb_im


def _state_layout(re, im):
    z = jnp.stack([re, im]).reshape(2, N_SUPER, GROUPS_PER_SUPER, SSM_STATE)
    return z.transpose(1, 0, 2, 3).reshape(STATE_COLS)


def _state_unlayout(vec):
    z = vec.reshape(N_SUPER, 2, GROUPS_PER_SUPER, SSM_STATE).transpose(1, 0, 2, 3)
    z = z.reshape(2, SSM_GROUPS, SSM_STATE)
    return z[0], z[1]


SEG = 4
WINDOW = SEG * SUBLANES


def _scan_tables(ab_re, ab_im):
    pw = [None, (ab_re, ab_im)]
    for _ in range(2, WINDOW + 1):
        pr, pi_ = pw[-1]
        pw.append((pr * ab_re - pi_ * ab_im, pr * ab_im + pi_ * ab_re))
    rows = np.arange(SUBLANES)[:, None]
    ones = np.ones((SUBLANES, 1), np.float32)
    conj = lambda p: (p[0], -p[1])
    fwd, bwd = [], []
    for shift in (1, 2, 4):
        fwd.append(_state_layout(*pw[SEG * shift])[None, :] * (rows >= shift).astype(np.float32))
        bwd.append(_state_layout(*conj(pw[SEG * shift]))[None, :] * (rows < SUBLANES - shift).astype(np.float32))
    fwd.append(jnp.stack([_state_layout(*pw[SEG * (r + 1)]) for r in range(SUBLANES)]))
    bwd.append(jnp.stack([_state_layout(*conj(pw[SEG * (SUBLANES - r)])) for r in range(SUBLANES)]))
    for k in range(1, SEG):
        fwd.append(_state_layout(*pw[k])[None, :] * ones)
        bwd.append(_state_layout(*conj(pw[k]))[None, :] * ones)
    return jnp.stack(fwd), jnp.stack(bwd)


_EYE = np.eye(GROUPS_PER_SUPER, dtype=np.float32)


def _b_matrix(bb_re, bb_im):
    bb = jnp.stack([bb_re, bb_im]).reshape(2, N_SUPER, GROUPS_PER_SUPER, SSM_STATE, SSM_GROUP)
    m = jnp.einsum('rsgpc,gh->sgcrhp', bb, _EYE)
    return m.reshape(N_SUPER, SUPER_IN, SUPER_W)


def _b_matrix_grad(dm):
    d = dm.reshape(N_SUPER, GROUPS_PER_SUPER, SSM_GROUP, 2, GROUPS_PER_SUPER, SSM_STATE)
    d = jnp.sum(d * _EYE[None, :, None, None, :, None], axis=4)
    d = d.transpose(3, 0, 1, 4, 2).reshape(2, SSM_GROUPS, SSM_STATE, SSM_GROUP)
    return d[0], d[1]


def _c_matrix(c_re, c_im):
    cc = jnp.stack([c_re, -c_im]).reshape(2, N_SUPER, GROUPS_PER_SUPER, SSM_GROUP, SSM_STATE)
    m = jnp.einsum('rsgcp,gh->srgphc', cc, _EYE)
    return m.reshape(N_SUPER, SUPER_W, SUPER_IN)


def _c_matrix_grad(dm):
    d = dm.reshape(N_SUPER, 2, GROUPS_PER_SUPER, SSM_STATE, GROUPS_PER_SUPER, SSM_GROUP)
    d = jnp.sum(d * _EYE[None, None, :, None, :, None], axis=4)
    d = d.transpose(1, 0, 2, 4, 3).reshape(2, SSM_GROUPS, SSM_GROUP, SSM_STATE)
    return d[0], -d[1]


def _cmul_add(xr, xi, ar, ai, sr, si):
    return xr + ar * sr - ai * si, xi + ar * si + ai * sr


def _scan_rows(buf_ref, tab_ref, carry_ref, n_windows, reverse, h_ref=None, da_ref=None):
    order = list(range(SEG - 1, -1, -1)) if reverse else list(range(SEG))
    near = SUBLANES - 1 if reverse else 0
    far = 0 if reverse else SUBLANES - 1
    s_in = SUBLANES - 1 if reverse else 1
    lanes = lambda tile: pl.ds(tile * LANES, LANES)

    def window(w0, tile_re, tile_im, c_re, c_im, acc):
        rows = lambda t: pl.ds(w0 + t, SUBLANES, stride=SEG)
        get = lambda ref, t: (ref.at[tile_re][rows(t), :], ref.at[tile_im][rows(t), :])
        tab = lambda k: (tab_ref[k, :, lanes(tile_re)], tab_ref[k, :, lanes(tile_im)])

        def put(t, xr, xi):
            buf_ref.at[tile_re][rows(t), :] = xr
            buf_ref.at[tile_im][rows(t), :] = xi

        a1 = tab(4)
        er, ei = get(buf_ref, order[0])
        for t in order[1:]:
            er, ei = _cmul_add(*get(buf_ref, t), *a1, er, ei)
            if t != order[-1]:
                put(t, er, ei)
        for k, shift in enumerate((1, 2, 4)):
            s = (SUBLANES - shift) if reverse else shift
            er, ei = _cmul_add(er, ei, *tab(k), pltpu.roll(er, s, 0), pltpu.roll(ei, s, 0))
        er, ei = _cmul_add(er, ei, *tab(3), c_re, c_im)
        put(order[-1], er, ei)
        sub = lax.broadcasted_iota(jnp.int32, er.shape, 0)
        in_re = jnp.where(sub == near, c_re, pltpu.roll(er, s_in, 0))
        in_im = jnp.where(sub == near, c_im, pltpu.roll(ei, s_in, 0))
        true = {order[-1]: (er, ei)}
        for idx, t in enumerate(order[:-1]):
            true[t] = _cmul_add(*get(buf_ref, t), *tab(4 + idx), in_re, in_im)
            put(t, *true[t])
        carry = (jnp.broadcast_to(er[far:far + 1], er.shape), jnp.broadcast_to(ei[far:far + 1], ei.shape))
        if acc is None:
            return carry, None
        acc_re, acc_im = acc
        for t in range(SEG):
            if t + 1 < SEG:
                gr, gim = true[t + 1]
            else:
                gr = jnp.where(sub == SUBLANES - 1, c_re, pltpu.roll(true[0][0], SUBLANES - 1, 0))
                gim = jnp.where(sub == SUBLANES - 1, c_im, pltpu.roll(true[0][1], SUBLANES - 1, 0))
            hr, hi = get(h_ref, t)
            acc_re = acc_re + gr * hr + gim * hi
            acc_im = acc_im + gim * hr - gr * hi
        return carry, (acc_re, acc_im)

    half = SUPER_HALF // LANES
    per = 2 if h_ref is None else 4
    for sb in range(N_SUPER):
        pairs = [(2 * half * sb + j, 2 * half * sb + half + j) for j in range(half)]

        def step(wi, state, pairs=pairs):
            w = (n_windows - 1 - wi) if reverse else wi
            w0 = pl.multiple_of(w * WINDOW, WINDOW)
            out = []
            for j, (tile_re, tile_im) in enumerate(pairs):
                mine = state[per * j:per * (j + 1)]
                carry, acc = window(w0, tile_re, tile_im, mine[0], mine[1], mine[2:] or None)
                out += list(carry) + list(acc or ())
            return tuple(out)

        init = []
        for tile_re, tile_im in pairs:
            init += [carry_ref[:, lanes(tile_re)], carry_ref[:, lanes(tile_im)]]
            if h_ref is not None:
                init += [da_ref[:, lanes(tile_re)], da_ref[:, lanes(tile_im)]]
        fin = lax.fori_loop(0, n_windows, step, tuple(init))
        for j, (tile_re, tile_im) in enumerate(pairs):
            carry_ref[:, lanes(tile_re)] = fin[per * j]
            carry_ref[:, lanes(tile_im)] = fin[per * j + 1]
            if h_ref is not None:
                da_ref[:, lanes(tile_re)] = fin[per * j + 2]
                da_ref[:, lanes(tile_im)] = fin[per * j + 3]


def _put_tiles(ref, sb, block):
    for j in range(SUPER_TILES):
        ref[sb * SUPER_TILES + j] = block[:, j * LANES:(j + 1) * LANES]


def _get_tiles(ref, sb):
    return jnp.concatenate([ref[sb * SUPER_TILES + j] for j in range(SUPER_TILES)], axis=1)


def _ssm_fwd(u, bmat, cmat, tab, d_skip, tb, exchange=None):
    seq = u.shape[0]

    def body(u_ref, b_ref, c_ref, t_ref, d_ref, s_ref, h_ref, carry_ref):
        @pl.when(pl.program_id(0) == 0)
        def _():
            carry_ref[...] = jnp.zeros_like(carry_ref)

        u_blk = u_ref[...]
        ub = _bf(u_blk)
        for sb in range(N_SUPER):
            _put_tiles(h_ref, sb, _mm(ub[:, sb * SUPER_IN:(sb + 1) * SUPER_IN], b_ref[sb]))
        _scan_rows(h_ref, t_ref, carry_ref, tb // WINDOW, False)
        ys = [_mm(_bf(_get_tiles(h_ref, sb)), c_ref[sb]) for sb in range(N_SUPER)]
        s_ref[...] = jnp.concatenate(ys, axis=1) + d_ref[...] * u_blk

    return _rowcall("ssm_fwd", body, seq, tb, [u], [bmat, cmat, tab, d_skip],
                    [(SSM_W, F32), ((STATE_TILES, LANES), F32)], [],
                    scratch=[pltpu.VMEM((SUBLANES, STATE_COLS), F32)], vmem=VMEM_BIG, exchange=exchange)


def _ssm_bwd(ds, u, h, bmat_t, cmat_t, tab, d_skip, tb, exchange=None):
    seq = u.shape[0]

    def body(ds_ref, u_ref, h_ref, bt_ref, ct_ref, t_ref, d_ref,
             du_ref, db_ref, dc_ref, da_ref, dd_ref, g_ref, carry_ref):
        @pl.when(pl.program_id(0) == 0)
        def _():
            carry_ref[...] = jnp.zeros_like(carry_ref)
            db_ref[...] = jnp.zeros_like(db_ref)
            dc_ref[...] = jnp.zeros_like(dc_ref)
            da_ref[...] = jnp.zeros_like(da_ref)
            dd_ref[...] = jnp.zeros_like(dd_ref)

        ds_blk = ds_ref[...]
        dsb = _bf(ds_blk)
        u_blk = u_ref[...]
        ub = _bf(u_blk)
        for sb in range(N_SUPER):
            _put_tiles(g_ref, sb, _mm(dsb[:, sb * SUPER_IN:(sb + 1) * SUPER_IN], ct_ref[sb]))
        _scan_rows(g_ref, t_ref, carry_ref, tb // WINDOW, True, h_ref=h_ref, da_ref=da_ref)
        dus = []
        for sb in range(N_SUPER):
            gb = _bf(_get_tiles(g_ref, sb))
            dus.append(_mm(gb, bt_ref[sb]))
            db_ref[sb] += _mm_tn(ub[:, sb * SUPER_IN:(sb + 1) * SUPER_IN], gb)
            dc_ref[sb] += _mm_tn(_bf(_get_tiles(h_ref, sb)), dsb[:, sb * SUPER_IN:(sb + 1) * SUPER_IN])
        du_ref[...] = jnp.concatenate(dus, axis=1) + d_ref[...] * ds_blk
        dd_ref[...] += jnp.sum(ds_blk * u_blk, axis=0, keepdims=True)

    return _rowcall("ssm_bwd", body, seq, tb, [ds, u, h], [bmat_t, cmat_t, tab, d_skip],
                    [(SSM_W, F32)],
                    [((N_SUPER, SUPER_IN, SUPER_W), F32), ((N_SUPER, SUPER_W, SUPER_IN), F32),
                     ((SUBLANES, STATE_COLS), F32), ((1, SSM_W), F32)],
                    scratch=[pltpu.VMEM((STATE_TILES, tb, LANES), F32), pltpu.VMEM((SUBLANES, STATE_COLS), F32)],
                    reverse=True, vmem=VMEM_BIG, exchange=exchange)


def _merge_core(s, attb, ga, gs, wg_ref, wab_ref, wsb_ref, wout_ref):
    zg, dgelu = _gelu_and_grad(s)
    zgb = _bf(zg)
    sg = _sig(_mm(zgb, wg_ref[...]))
    z = zg * sg
    zb = _bf(z)
    ys = jnp.concatenate([_mm(zb, wsb_ref[j]) for j in range(N_CHIPS)], axis=1)
    ya = jnp.concatenate([_mm(attb, wab_ref[j]) for j in range(N_CHIPS)], axis=1)
    sa = _sig(ga)
    ss = _sig(gs)
    mgb = _bf(sa * ya + ss * ys)
    o = _mm(mgb, wout_ref[...])
    return dict(zg=zg, dgelu=dgelu, zgb=zgb, sg=sg, zb=zb, ys=ys, ya=ya, sa=sa, ss=ss, mgb=mgb, o=o)


def _merge_fwd(x, s, att, ga, gs, g2, w_glu, w_ab, w_sb, w_out, tb):
    seq = x.shape[0]

    def body(x_ref, s_ref, att_ref, ga_ref, gs_ref, g_ref, wg_ref, wab_ref, wsb_ref, wout_ref, x2_ref):
        f = _merge_core(s_ref[...], att_ref[...], ga_ref[...], gs_ref[...], wg_ref, wab_ref, wsb_ref, wout_ref)
        n, _, _ = _rms(f["o"], g_ref[...])
        x2_ref[...] = x_ref[...] + n

    return _rowcall("merge_fwd", body, seq, tb, [x, s, att, ga, gs], [g2, w_glu, w_ab, w_sb, w_out],
                    [(D_MODEL, F32)], [], vmem=VMEM_BIG)[0]


def _merge_bwd(dx2, s, att, ga, gs, g2, w_glu, w_ab, w_sb, w_out, tb, exchange=None):
    seq = s.shape[0]
    cw = D_MODEL // N_CHIPS
    last = seq // tb - 1

    def body(dx2_ref, s_ref, att_ref, ga_ref, gs_ref, g_ref, wg_ref, wab_ref, wsb_ref, wout_ref,
             ds_ref, datt_ref, dga_ref, dgs_ref, dg_ref, dwg_ref, dwab_ref, dwsb_ref, dwout_ref,
             bwg_ref, bwab_ref, bwsb_ref, bwout_ref):
        @pl.when(pl.program_id(0) == 0)
        def _():
            for r in (dg_ref, dwg_ref, dwab_ref, dwsb_ref, dwout_ref):
                r[...] = jnp.zeros_like(r)

        attb = att_ref[...]
        f = _merge_core(s_ref[...], attb, ga_ref[...], gs_ref[...], wg_ref, wab_ref, wsb_ref, wout_ref)
        g = g_ref[...]
        _, oh, r2 = _rms(f["o"], g)
        do, dg = _rms_bwd(dx2_ref[...], oh, r2, g)
        dg_ref[...] += dg
        dob = _bf(do)
        dwout_ref[...] += _mm_tn(f["mgb"], dob)
        dmg = _mm_nt(dob, wout_ref[...])
        sa, ss = f["sa"], f["ss"]
        dyab = _bf(dmg * sa)
        dysb = _bf(dmg * ss)
        dga_ref[...] = _bf(dmg * f["ya"] * sa * (1.0 - sa))
        dgs_ref[...] = _bf(dmg * f["ys"] * ss * (1.0 - ss))
        dwab = _mm_tn(attb, dyab)
        dwsb = _mm_tn(f["zb"], dysb)
        datt = jnp.zeros((tb, ATTN_W), F32)
        dz = jnp.zeros((tb, SSM_W), F32)
        for j in range(N_CHIPS):
            dwab_ref[j] += dwab[:, j * cw:(j + 1) * cw]
            dwsb_ref[j] += dwsb[:, j * cw:(j + 1) * cw]
            datt = datt + _mm_nt(dyab[:, j * cw:(j + 1) * cw], wab_ref[j])
            dz = dz + _mm_nt(dysb[:, j * cw:(j + 1) * cw], wsb_ref[j])
        datt_ref[...] = _bf(datt)
        sg, zg = f["sg"], f["zg"]
        dglb = _bf(dz * zg * sg * (1.0 - sg))
        dwg_ref[...] += _mm_tn(f["zgb"], dglb)
        dzg = dz * sg + _mm_nt(dglb, wg_ref[...])
        ds_ref[...] = dzg * f["dgelu"]

        @pl.when(pl.program_id(0) == last)
        def _():
            for dst, src in ((bwg_ref, dwg_ref), (bwab_ref, dwab_ref), (bwsb_ref, dwsb_ref), (bwout_ref, dwout_ref)):
                dst[...] = _bf(src[...])

    shapes = [w_glu.shape, w_ab.shape, w_sb.shape, w_out.shape]
    return _rowcall("merge_bwd", body, seq, tb, [dx2, s, att, ga, gs], [g2, w_glu, w_ab, w_sb, w_out],
                    [(SSM_W, F32), (ATTN_W, BF16), (D_MODEL, BF16), (D_MODEL, BF16)],
                    [((1, D_MODEL), F32)] + [(sh, F32) for sh in shapes] + [(sh, BF16) for sh in shapes],
                    vmem=VMEM_BIG, exchange=exchange)


def _mlp_fwd_loss(x2, target, g3, g4, w_ffi, w_ffo, tb):
    seq = x2.shape[0]
    n_slab = len(w_ffi)
    sw = D_FF // FF_CHUNKS // n_slab

    def body(x2_ref, t_ref, g3_ref, g4_ref, *rest):
        wi_refs, (wo_ref, dy_ref, df_ref, h_ref, loss_ref, dg_ref) = rest[:n_slab], rest[n_slab:]

        @pl.when(pl.program_id(0) == 0)
        def _():
            loss_ref[...] = jnp.zeros_like(loss_ref)
            dg_ref[...] = jnp.zeros_like(dg_ref)

        x2_blk = x2_ref[...]
        h3, _, _ = _rms(x2_blk, g3_ref[...])
        hb = _bf(h3)
        h_ref[...] = hb
        f = jnp.zeros((tb, D_MODEL), F32)
        for j in range(FF_CHUNKS):
            for k in range(n_slab):
                a = _mm(hb, wi_refs[k][j])
                f = f + _mm(_bf(jnp.square(jnp.maximum(a, 0.0))), wo_ref[j, pl.ds(k * sw, sw), :])
        g4 = g4_ref[...]
        n4, fh, r4 = _rms(f, g4)
        e = (x2_blk + n4) - t_ref[...]
        loss_ref[...] += 0.5 * jnp.sum(jnp.mean(e * e, axis=-1, keepdims=True))
        dy = e * (1.0 / D_MODEL)
        dy_ref[...] = dy
        df, dg = _rms_bwd(dy, fh, r4, g4)
        df_ref[...] = _bf(df)
        dg_ref[...] += dg

    return _rowcall("mlp_fwd_loss", body, seq, tb, [x2, target], [g3, g4, *w_ffi, w_ffo],
                    [(D_MODEL, F32), (D_MODEL, BF16), (D_MODEL, BF16)],
                    [((SUBLANES, 128), F32), ((1, D_MODEL), F32)], vmem=VMEM_BIG)


def _mlp_bwd(x2, dy, df, h3, g3, w_ffi, w_ffo, tb):
    seq = x2.shape[0]
    n_slab = len(w_ffi)
    sw = D_FF // FF_CHUNKS // n_slab

    def body(x2_ref, dy_ref, df_ref, h_ref, g3_ref, *rest):
        wi_refs, (wo_ref, dx_ref, act_ref, da_ref, dg_ref) = rest[:n_slab], rest[n_slab:]

        @pl.when(pl.program_id(0) == 0)
        def _():
            dg_ref[...] = jnp.zeros_like(dg_ref)

        hb = h_ref[...]
        dfb = df_ref[...]
        dh = jnp.zeros((tb, D_MODEL), F32)
        for j in range(FF_CHUNKS):
            for k in range(n_slab):
                cols = pl.ds((j * n_slab + k) * sw, sw)
                ra = jnp.maximum(_mm(hb, wi_refs[k][j]), 0.0)
                act_ref[:, cols] = _bf(ra * ra)
                dab = _bf(_mm_nt(dfb, wo_ref[j, pl.ds(k * sw, sw), :]) * (2.0 * ra))
                da_ref[:, cols] = dab
                dh = dh + _mm_nt(dab, wi_refs[k][j])
        g3 = g3_ref[...]
        _, xh, r3 = _rms(x2_ref[...], g3)
        dxn, dg = _rms_bwd(dh, xh, r3, g3)
        dx_ref[...] = dy_ref[...] + dxn
        dg_ref[...] += dg

    return _rowcall("mlp_bwd", body, seq, tb, [x2, dy, df, h3], [g3, *w_ffi, w_ffo],
                    [(D_MODEL, F32), (D_FF, BF16), (D_FF, BF16)], [((1, D_MODEL), F32)], vmem=VMEM_BIG)


def _matmul_tn(name, a, b, tk, tn, tl, chunk_major, exchange=None):
    seq, kdim = a.shape
    ndim = b.shape[1]
    last = seq // tl - 1

    def body(a_ref, b_ref, o_ref, ob_ref):
        @pl.when(pl.program_id(2) == 0)
        def _():
            o_ref[...] = jnp.zeros_like(o_ref)

        o_ref[...] += _mm_tn(a_ref[...], b_ref[...])

        @pl.when(pl.program_id(2) == last)
        def _():
            ob_ref[...] = _bf(o_ref[...])

    if chunk_major:
        shape = (ndim // tn, kdim, tn)
        out_spec = pl.BlockSpec((None, tk, tn), lambda k, n, l: (n, k, 0))
    else:
        shape = (kdim, ndim)
        out_spec = pl.BlockSpec((tk, tn), lambda k, n, l: (k, n))
    return _fused_call(
        name, body, (kdim // tk, ndim // tn, seq // tl),
        [pl.BlockSpec((tl, tk), lambda k, n, l: (l, k)), pl.BlockSpec((tl, tn), lambda k, n, l: (l, n))],
        [out_spec, out_spec], [SDS(shape, F32), SDS(shape, BF16)], [], [a, b], exchange, _params(3, VMEM_BIG))


def _ew_call(name, fn, ins, n_out, after=None):
    rows, cols = ins[0].shape
    tr = rows
    while tr * cols * 4 > min(1 << 20, (9 << 20) // (len(ins) + n_out)) and tr % 16 == 0:
        tr //= 2
    spec = pl.BlockSpec((tr, cols), lambda i: (i, 0))
    extra = [] if after is None else [after]

    def body(*refs):
        outs = fn(*[r[...] for r in refs[:len(ins)]])
        for r, o in zip(refs[len(ins) + len(extra):], outs):
            r[...] = o

    return pl.pallas_call(
        body, grid=(rows // tr,), in_specs=[spec] * len(ins) + [ANY] * len(extra), out_specs=[spec] * n_out,
        out_shape=[SDS((rows, cols), F32)] * n_out, name=name, compiler_params=_params(1))(*ins, *extra)


def _adam_math(w, g, m, v):
    m2 = ADAM_B1 * m + (1.0 - ADAM_B1) * g
    v2 = ADAM_B2 * v + (1.0 - ADAM_B2) * (g * g)
    m_hat = m2 / (1.0 - ADAM_B1 ** ADAM_STEP)
    v_hat = v2 / (1.0 - ADAM_B2 ** ADAM_STEP)
    delta = -ADAM_LR * (m_hat / (jnp.sqrt(v_hat) + ADAM_EPS) + ADAM_WD * w)
    return delta, m2, v2


def _sum4(name, own, recv, idx):
    _, rows, cols = own.shape
    tr = rows
    while tr * cols * 4 > (1 << 20) and tr % 16 == 0:
        tr //= 2

    def body(idx_ref, o_ref, r0_ref, r1_ref, r2_ref, out_ref):
        out_ref[...] = ((o_ref[...] + r0_ref[...].astype(F32)) + r1_ref[...].astype(F32)) + r2_ref[...].astype(F32)

    blk = (None, tr, cols)
    grid_spec = pltpu.PrefetchScalarGridSpec(
        num_scalar_prefetch=1, grid=(rows // tr,),
        in_specs=[pl.BlockSpec(blk, lambda i, s: (s[0], i, 0)), pl.BlockSpec(blk, lambda i, s: (0, i, 0)),
                  pl.BlockSpec(blk, lambda i, s: (1, i, 0)), pl.BlockSpec(blk, lambda i, s: (2, i, 0))],
        out_specs=pl.BlockSpec((tr, cols), lambda i, s: (i, 0)))
    return pl.pallas_call(body, grid_spec=grid_spec, out_shape=SDS((rows, cols), F32), name=name,
                          compiler_params=_params(1))(jnp.reshape(idx, (1,)).astype(jnp.int32), own, recv, recv, recv)


def _adam_pair(name, item, after=None):
    def fn(w_, a, b, m_, v_):
        g = a + b
        return (g,) + _adam_math(w_, g, m_, v_)

    return _ew_call(name, fn, list(item), 4, after)


def _place():
    return lax.axis_index("x"), lax.axis_index("y"), lax.axis_index("c")


def _other_chips(x, y):
    return [(1 - x, y), (x, 1 - y), (1 - x, 1 - y)]


HBM = pl.BlockSpec(memory_space=pltpu.HBM)
SEM = pl.BlockSpec(memory_space=pltpu.SEMAPHORE)
DATAFLOW = pltpu.SideEffectType.DATAFLOW_SIDE_EFFECTING


class _Flight:
    def __init__(self, copies, n_copies, send, recv, srcs, lands, token):
        self.copies, self.n, self.send, self.recv = copies, n_copies, send, recv
        self.srcs, self.lands, self.token = list(srcs), list(lands), token


def _take_off(name, srcs, lands, copies, n_copies, after):
    n_s, n_l = len(srcs), len(lands)

    def body(*refs):
        src, land = refs[:n_s], refs[n_s:n_s + n_l]
        send, recv = refs[n_s + n_l + 1:n_s + n_l + 3]
        for cp in copies(src, land, send, recv):
            cp.start()
        refs[-1][...] = jnp.zeros_like(refs[-1])

    mem = lambda t: pltpu.HBM(t.shape, t.dtype)
    sems = pltpu.SemaphoreType.DMA((n_copies,))
    outs = pl.pallas_call(
        body, name=name,
        out_shape=(sems, sems, *map(mem, srcs), *map(mem, lands), SDS((SUBLANES, LANES), F32)),
        in_specs=[HBM] * (n_s + n_l) + [ANY],
        out_specs=(SEM, SEM, *[HBM] * (n_s + n_l), pl.BlockSpec(memory_space=pltpu.VMEM)),
        input_output_aliases={i: 2 + i for i in range(n_s + n_l)},
        compiler_params=pltpu.CompilerParams(has_side_effects=DATAFLOW),
    )(*[pltpu.with_memory_space_constraint(t, pltpu.HBM) for t in (*srcs, *lands)], after)
    return _Flight(copies, n_copies, outs[0], outs[1], outs[2:2 + n_s], outs[2 + n_s:2 + n_s + n_l], outs[-1])


def _land(name, flight, after):
    n_s, n_l = len(flight.srcs), len(flight.lands)

    def body(*refs):
        src, land = refs[:n_s], refs[n_s:n_s + n_l]
        send, recv = refs[n_s + n_l:n_s + n_l + 2]
        for cp in flight.copies(src, land, send, recv):
            cp.wait_send()
            cp.wait_recv()

    mem = lambda t: pltpu.HBM(t.shape, t.dtype)
    outs = pl.pallas_call(
        body, name=name, out_shape=(*map(mem, flight.srcs), *map(mem, flight.lands)),
        in_specs=[HBM] * (n_s + n_l) + [SEM, SEM, ANY], out_specs=tuple([HBM] * (n_s + n_l)),
        input_output_aliases={i: i for i in range(n_s + n_l)},
        compiler_params=pltpu.CompilerParams(has_side_effects=DATAFLOW),
    )(*flight.srcs, *flight.lands, flight.send, flight.recv, after)
    return list(outs[:n_s]), list(outs[n_s:])


def _empty_like(shapes_from, lead):
    return [lax.empty((lead,) + t.shape[1:], t.dtype) for t in shapes_from]


def _scatter_off(name, chunks, after):
    def copies(src, land, send, recv):
        x, y, c = _place()
        return [pltpu.make_async_remote_copy(
            src_ref=src[a].at[2 * px + py], dst_ref=land[a].at[k], send_sem=send.at[3 * a + k],
            recv_sem=recv.at[3 * a + k], device_id=(px, py, c), device_id_type=MESH_ID)
            for a in range(len(chunks)) for k, (px, py) in enumerate(_other_chips(x, y))]

    return _take_off(name, chunks, _empty_like(chunks, 3), copies, 3 * len(chunks), after)


def _swap_off(name, arrs, after):
    def copies(src, land, send, recv):
        x, y, c = _place()
        return [pltpu.make_async_remote_copy(
            src_ref=src[a], dst_ref=land[a], send_sem=send.at[a], recv_sem=recv.at[a],
            device_id=(x, y, 1 - c), device_id_type=MESH_ID) for a in range(len(arrs))]

    return _take_off(name, arrs, [lax.empty(t.shape, t.dtype) for t in arrs], copies, len(arrs), after)


def _devices_off(name, block, after):
    me = 4 * lax.axis_index("x") + 2 * lax.axis_index("y") + lax.axis_index("c")
    land = lax.dynamic_update_index_in_dim(lax.empty((N_DEV,) + block.shape, block.dtype), block, me, 0)

    def copies(src, land, send, recv):
        x, y, c = _place()
        mine = 4 * x + 2 * y + c
        return [pltpu.make_async_remote_copy(
            src_ref=src[0], dst_ref=land[0].at[mine], send_sem=send.at[k - 1], recv_sem=recv.at[k - 1],
            device_id=(x ^ (k >> 2), y ^ ((k >> 1) & 1), c ^ (k & 1)), device_id_type=MESH_ID)
            for k in range(1, N_DEV)]

    return _take_off(name, [block], [land], copies, N_DEV - 1, after)


def _half_rows(shape, c, other=False):
    half = shape[0] // 2
    return pl.ds(((1 - c) if other else c) * half, half)


def _gather_start(name, shards, lands, after):
    n = len(shards)

    def body(*refs):
        src, land, (send, recv) = refs[:n], refs[n:2 * n], refs[2 * n + 1:2 * n + 3]
        x, y, c = _place()
        me = 2 * x + y
        for a in range(n):
            mine = _half_rows(shards[a].shape, c)
            for j, (px, py) in enumerate(_other_chips(x, y)):
                pltpu.make_async_remote_copy(
                    src_ref=src[a].at[mine], dst_ref=land[a].at[me, mine], send_sem=send.at[3 * a + j],
                    recv_sem=recv.at[3 * a + j], device_id=(px, py, c), device_id_type=MESH_ID).start()
        token = refs[-1]
        token[...] = jnp.zeros_like(token)

    mem = lambda t: pltpu.HBM(t.shape, t.dtype)
    pair = pltpu.SemaphoreType.DMA((3 * n,))
    outs = pl.pallas_call(
        body, name=name,
        out_shape=(pair, pair, *map(mem, shards), *map(mem, lands), SDS((SUBLANES, LANES), F32)),
        in_specs=[HBM] * (2 * n) + [ANY],
        out_specs=(SEM, SEM, *[HBM] * (2 * n), pl.BlockSpec(memory_space=pltpu.VMEM)),
        input_output_aliases={i: 2 + i for i in range(2 * n)},
        compiler_params=pltpu.CompilerParams(has_side_effects=DATAFLOW),
    )(*[pltpu.with_memory_space_constraint(t, pltpu.HBM) for t in (*shards, *lands)], after)
    return outs[0], outs[1], list(outs[2:2 + n]), list(outs[2 + n:2 + 2 * n]), outs[-1]


def _gather_pass(name, send, recv, shards, lands, after):
    n = len(shards)

    def body(*refs):
        src, land, (send, recv, _) = refs[:n], refs[n:2 * n], refs[2 * n:2 * n + 3]
        fsend, frecv = refs[2 * n + 3], refs[2 * n + 4]
        x, y, c = _place()
        me = 2 * x + y
        for a in range(n):
            mine = _half_rows(shards[a].shape, c)
            for j, (px, py) in enumerate(_other_chips(x, y)):
                far = 2 * px + py
                ici = pltpu.make_async_remote_copy(
                    src_ref=src[a].at[mine], dst_ref=land[a].at[far, mine], send_sem=send.at[3 * a + j],
                    recv_sem=recv.at[3 * a + j], device_id=(px, py, c), device_id_type=MESH_ID)
                ici.wait_recv()
                ici.wait_send()
                pltpu.make_async_remote_copy(
                    src_ref=land[a].at[far, mine], dst_ref=land[a].at[far, mine], send_sem=fsend.at[3 * a + j],
                    recv_sem=frecv.at[3 * a + j], device_id=(x, y, 1 - c), device_id_type=MESH_ID).start()
        token = refs[-1]
        token[...] = jnp.zeros_like(token)

    mem = lambda t: pltpu.HBM(t.shape, t.dtype)
    pair = pltpu.SemaphoreType.DMA((3 * n,))
    outs = pl.pallas_call(
        body, name=name,
        out_shape=(pair, pair, *map(mem, lands), SDS((SUBLANES, LANES), F32)),
        in_specs=[HBM] * (2 * n) + [SEM, SEM, ANY],
        out_specs=(SEM, SEM, *[HBM] * n, pl.BlockSpec(memory_space=pltpu.VMEM)),
        input_output_aliases={n + i: 2 + i for i in range(n)},
        compiler_params=pltpu.CompilerParams(has_side_effects=DATAFLOW),
    )(*shards, *lands, send, recv, after)
    return outs[0], outs[1], list(outs[2:2 + n]), outs[-1]


def _gather_wait(name, fsend, frecv, lands, after):
    n = len(lands)

    def body(*refs):
        land, (fsend, frecv, _) = refs[:n], refs[n:n + 3]
        x, y, c = _place()
        for a in range(n):
            for j, (px, py) in enumerate(_other_chips(x, y)):
                far = 2 * px + py
                mine = _half_rows(lands[a].shape[1:], c)
                theirs = _half_rows(lands[a].shape[1:], c, other=True)
                pltpu.make_async_remote_copy(
                    src_ref=land[a].at[far, mine], dst_ref=land[a].at[far, mine], send_sem=fsend.at[3 * a + j],
                    recv_sem=frecv.at[3 * a + j], device_id=(x, y, 1 - c), device_id_type=MESH_ID).wait_send()
                pltpu.make_async_remote_copy(
                    src_ref=land[a].at[far, theirs], dst_ref=land[a].at[far, theirs], send_sem=fsend.at[3 * a + j],
                    recv_sem=frecv.at[3 * a + j], device_id=(x, y, 1 - c), device_id_type=MESH_ID).wait_recv()

    mem = lambda t: pltpu.HBM(t.shape, t.dtype)
    return list(pl.pallas_call(
        body, name=name, out_shape=tuple(map(mem, lands)), in_specs=[HBM] * n + [SEM, SEM, ANY],
        out_specs=tuple([HBM] * n), input_output_aliases={i: i for i in range(n)},
        compiler_params=pltpu.CompilerParams(has_side_effects=DATAFLOW),
    )(*lands, fsend, frecv, after))


def _after(token):
    return _Exchange([token], [], [], lambda *_: None, lambda *_: None)


def _swap_sibling(arrs):
    n = len(arrs)

    def copies(ins, outs, sems):
        send, recv = sems
        x, y, c = _place()
        return [pltpu.make_async_remote_copy(
            src_ref=ins[a], dst_ref=outs[a], send_sem=send.at[a], recv_sem=recv.at[a],
            device_id=(x, y, 1 - c), device_id_type=MESH_ID) for a in range(n)]

    def start(ins, outs, sems):
        for cp in copies(ins, outs, sems):
            cp.start()

    def wait(ins, outs, sems):
        cps = copies(ins, outs, sems)
        for cp in cps:
            cp.wait_recv()
        for cp in cps:
            cp.wait_send()

    return _Exchange(arrs, [SDS(s.shape, s.dtype) for s in arrs],
                     [pltpu.SemaphoreType.DMA((n,)), pltpu.SemaphoreType.DMA((n,))], start, wait)


def _sum_devices(slots):
    def body(s_ref, o_ref):
        acc = s_ref[0]
        for d in range(1, N_DEV):
            acc = acc + s_ref[d]
        o_ref[...] = acc

    return pl.pallas_call(
        body, in_specs=[pl.BlockSpec(memory_space=pltpu.VMEM)], out_specs=pl.BlockSpec(memory_space=pltpu.VMEM),
        out_shape=SDS(slots.shape[1:], F32), name="sum_small",
        compiler_params=pltpu.CompilerParams(vmem_limit_bytes=32 * 1024 * 1024))(slots)


def _adam_small(ws, gs, ms, vs):
    n = len(ws)

    def body(*refs):
        for i in range(n):
            w_ref, g_ref, m_ref, v_ref = (refs[k * n + i] for k in range(4))
            outs = _adam_math(w_ref[...], g_ref[...], m_ref[...], v_ref[...])
            for k in range(3):
                refs[(4 + k) * n + i][...] = outs[k]

    vmem = pl.BlockSpec(memory_space=pltpu.VMEM)
    return pl.pallas_call(
        body, in_specs=[vmem] * (4 * n), out_specs=[vmem] * (3 * n),
        out_shape=[SDS(w.shape, F32) for w in ws] * 3, name="adam_small",
        compiler_params=pltpu.CompilerParams(vmem_limit_bytes=32 * 1024 * 1024))(*ws, *gs, *ms, *vs)


def _local_step(x, target, small, big, tb, distributed):
    dist = distributed
    me = (2 * lax.axis_index("x") + lax.axis_index("y")) if dist else 0
    tb_ssm = min(tb, 256)
    bucket = jnp.asarray(_bucket_table())
    place_own = lambda t: lax.dynamic_update_index_in_dim(lax.empty((N_CHIPS,) + t.shape, t.dtype), t, me, 0)
    if dist:
        in_legs = _gather_start("gather_in_start", [big["w_in"]], [place_own(big["w_in"])], small["d_skip"])
        names = sorted(small)
        in_token, values = lax.optimization_barrier((in_legs[4], [small[n] for n in names]))
        small = dict(zip(names, values))
    g1, g2, g3, g4 = small["norm_mix_pre"], small["norm_mix_post"], small["norm_mlp_pre"], small["norm_mlp_post"]

    keys_first = lambda t: jnp.swapaxes(t, -1, -2)
    bias = _bias_table(small["rel_bias"], bucket)
    sink_rows = keys_first(_pair_layout(jnp.broadcast_to(small["sinks"].reshape(N_HEADS, 1, 1), (N_HEADS, BLOCK, 1))))
    disc_args = (small["lam_re"], small["lam_im"], small["log_dt"], small["b_re"], small["b_im"])
    (ab_re, ab_im, bb_re, bb_im), disc_vjp = jax.vjp(_ssm_discretize, *disc_args)
    tab_f, tab_b = _scan_tables(ab_re, ab_im)
    bmat = _bf(_b_matrix(bb_re, bb_im))
    cmat = _bf(_c_matrix(small["c_re"], small["c_im"]))
    d_skip = small["d_skip"]

    mix = ("w_glu", "w_attn_branch", "w_ssm_branch", "w_out")
    rest = [big[n] for n in mix + ("w_ff_in", "w_ff_out")]
    if dist:
        send, recv, src, lands, _ = in_legs
        rest_lands = [place_own(t) for t in rest]
        corner = lambda t: t.reshape(-1, t.shape[-1])[:1, :LANES].astype(F32)
        prepared = sum(map(corner, [tab_b, bias, sink_rows, bmat, cmat] + rest_lands), in_token[:1])
        send, recv, lands, in_passed = _gather_pass("gather_in_pass", send, recv, src, lands, prepared)
        (g_in,) = _gather_wait("gather_in_wait", send, recv, lands, in_passed)
        w_in = g_in.reshape(IN_W, D_MODEL)
    else:
        w_in = big["w_in"]
    token = None
    if dist:
        send, recv, rest, lands, token = _gather_start("gather_rest_start", rest, rest_lands, in_passed)
    h1, q, k, v, u, ga, gs = _inproj_fwd(x, g1, w_in, tb, _after(token) if dist else None)
    s, h = _ssm_fwd(u, bmat, cmat, tab_f, d_skip, tb)
    if dist:
        send, recv, lands, token = _gather_pass("gather_rest_pass", send, recv, rest, lands, s)
    att = _attn_fwd(q, k, v, bias, sink_rows, _after(token) if dist else None)[0]
    if dist:
        rest = _gather_wait("gather_rest_wait", send, recv, lands, att)
    w_glu, w_ab, w_sb, w_out, w_ffi, w_ffo = rest
    w_glu = w_glu.reshape(SSM_W, SSM_W)
    w_out = w_out.reshape(D_MODEL, D_MODEL)
    w_ffi = [w_ffi]
    x2 = _merge_fwd(x, s, att, ga, gs, g2, w_glu, w_ab, w_sb, w_out, tb)
    dy, df, h3, loss_acc, dg4 = _mlp_fwd_loss(x2, target, g3, g4, w_ffi, w_ffo, tb)

    dx2, act, da, dg3 = _mlp_bwd(x2, dy, df, h3, g3, w_ffi, w_ffo, tb)
    tl = min(2048, x.shape[0])
    chunked = (N_CHIPS, D_FF // N_CHIPS, D_MODEL)
    d_ffi, b_ffi = _matmul_tn("grad_w_ff_in", h3, da, D_MODEL, D_FF // FF_CHUNKS, tl, True)
    d_ffo, b_ffo = _matmul_tn("grad_w_ff_out", act, df, D_FF // FF_CHUNKS, D_MODEL, tl, False)
    d_ffo, b_ffo = d_ffo.reshape(chunked), b_ffo.reshape(chunked)
    behind = lambda flight: _after(flight.token) if dist else None
    ff_fl = _scatter_off("scatter_ff_off", [b_ffi, b_ffo], d_ffo) if dist else None
    outs = _merge_bwd(dx2, s, att, ga, gs, g2, w_glu, w_ab, w_sb, w_out, tb_ssm, behind(ff_fl))
    ds, datt, dga, dgs, dg2, d_glu, d_ab, d_sb, d_out, b_glu, b_ab, b_sb, b_out = outs
    glu4, out4 = (N_CHIPS, SSM_W // N_CHIPS, SSM_W), (N_CHIPS, D_MODEL // N_CHIPS, D_MODEL)
    d_mix = [d_glu.reshape(glu4), d_ab, d_sb, d_out.reshape(out4)]
    b_mix = [b_glu.reshape(glu4), b_ab, b_sb, b_out.reshape(out4)]
    mix_fl = _scatter_off("scatter_mix_off", b_mix, d_mix[-1]) if dist else None
    du, d_bmat, d_cmat, da_acc, dd_skip = _ssm_bwd(
        ds, u, h, bmat.transpose(0, 2, 1), cmat.transpose(0, 2, 1), tab_b, d_skip, tb, behind(mix_fl))
    dq, dk, dv, dbias, dsink_rows = _attn_bwd(q, k, v, datt, bias, sink_rows)
    swap_fl = None
    if dist:
        r_ffi, r_ffo = _land("scatter_ff_land", ff_fl, dq)[1]
        p_ffi = _sum4("sum_w_ff_in", d_ffi, r_ffi, me)
        p_ffo = _sum4("sum_w_ff_out", d_ffo, r_ffo, me)
        swap_fl = _swap_off("swap_ff_off", [p_ffi, p_ffo], r_ffo)
    dx, dpj, dg1 = _inproj_bwd(x, dx2, dq, dk, dv, du, dga, dgs, g1, w_in, tb, behind(swap_fl))

    dab_re, dab_im = _state_unlayout(jnp.sum(da_acc, axis=0))
    dbb_re, dbb_im = _b_matrix_grad(d_bmat)
    d_lam_re, d_lam_im, d_log_dt, d_b_re, d_b_im = disc_vjp((dab_re, dab_im, dbb_re, dbb_im))
    d_c_re, d_c_im = _c_matrix_grad(d_cmat)
    d_rel = _bias_grad(dbias, bucket)
    d_sinks = jnp.sum(_pair_unlayout(keys_first(dsink_rows)), axis=(1, 2))
    small_grads = dict(
        norm_mix_pre=dg1, norm_mix_post=dg2, norm_mlp_pre=dg3, norm_mlp_post=dg4, rel_bias=d_rel, sinks=d_sinks,
        lam_re=d_lam_re, lam_im=d_lam_im, log_dt=d_log_dt, b_re=d_b_re, b_im=d_b_im, c_re=d_c_re, c_im=d_c_im,
        d_skip=dd_skip)
    small_fl = _devices_off("small_off", _pack(small_grads, loss_acc), swap_fl.token) if dist else None
    outs = _matmul_tn("grad_w_in", dpj, h1, IN_W // 2, D_MODEL, tl, False, behind(small_fl))
    in4 = (N_CHIPS, IN_W // N_CHIPS, D_MODEL)
    d_in, b_in = outs[0].reshape(in4), outs[1].reshape(in4)
    if not dist:
        return loss_acc, dx, small_grads, dict(zip(BIG, [d_in] + d_mix + [d_ffi, d_ffo]))
    (p_ffi, p_ffo), (s_ffi, s_ffo) = _land("swap_ff_land", swap_fl, b_in)
    r_mix = _land("scatter_mix_land", mix_fl, b_in)[1]
    p_mix = [_sum4("sum_" + n, d, r, me) for n, d, r in zip(mix, d_mix, r_mix)]
    pending = dict(d_in=d_in, b_in=b_in, p_mix=p_mix, w_ff_in=(p_ffi, s_ffi), w_ff_out=(p_ffo, s_ffo), me=me)
    return loss_acc, dx, small_fl, pending


SMALL = ['norm_mix_pre', 'norm_mix_post', 'norm_mlp_pre', 'norm_mlp_post', 'rel_bias', 'sinks', 'lam_re', 'lam_im',
         'log_dt', 'b_re', 'b_im', 'c_re', 'c_im', 'd_skip']
BIG = ['w_in', 'w_glu', 'w_attn_branch', 'w_ssm_branch', 'w_out', 'w_ff_in', 'w_ff_out']
WEIGHTS = ['norm_mix_pre', 'norm_mix_post', 'norm_mlp_pre', 'norm_mlp_post', 'w_in', 'rel_bias', 'sinks', 'lam_re',
           'lam_im', 'log_dt', 'b_re', 'b_im', 'c_re', 'c_im', 'd_skip', 'w_glu', 'w_attn_branch', 'w_ssm_branch',
           'w_out', 'w_ff_in', 'w_ff_out']
PACK_COLS = 1024
PACK_ORDER = ['b_re', 'b_im', 'c_re', 'c_im', 'lam_re', 'lam_im', 'norm_mix_pre', 'norm_mix_post', 'norm_mlp_pre',
              'norm_mlp_post', 'rel_bias', 'sinks', 'log_dt', 'd_skip']


STATE_MINOR = ('b_re', 'b_im')
PACK_ROWS = 144
LOSS_ROW = 140


def _pack(named, loss_acc):
    parts = []
    for n in PACK_ORDER:
        a = jnp.swapaxes(named[n], -1, -2) if n in STATE_MINOR else named[n]
        flat = a.reshape(-1)
        rows = -(-flat.shape[0] // PACK_COLS)
        parts.append(jnp.pad(flat, (0, rows * PACK_COLS - flat.shape[0])).reshape(rows, PACK_COLS))
    assert sum(p.shape[0] for p in parts) == LOSS_ROW
    parts.append(jnp.pad(loss_acc[0:1], ((0, PACK_ROWS - LOSS_ROW - 1), (0, PACK_COLS - loss_acc.shape[1]))))
    return jnp.concatenate(parts, axis=0)


def _unpack(packed, shapes):
    out, at = {}, 0
    for n in PACK_ORDER:
        shape = shapes[n][:-2] + (shapes[n][-1], shapes[n][-2]) if n in STATE_MINOR else shapes[n]
        size = int(np.prod(shape))
        rows = -(-size // PACK_COLS)
        blk = packed[at:at + rows]
        out[n] = (blk.reshape(-1)[:size] if size % PACK_COLS else blk).reshape(shape)
        at += rows
    return out


def kernel(x, norm_mix_pre, norm_mix_post, norm_mlp_pre, norm_mlp_post, w_in, rel_bias, sinks, lam_re, lam_im, log_dt, b_re, b_im, c_re, c_im, d_skip, w_glu, w_attn_branch, w_ssm_branch, w_out, w_ff_in, w_ff_out, loss_target, m_norm_mix_pre, m_norm_mix_post, m_norm_mlp_pre, m_norm_mlp_post, m_w_in, m_rel_bias, m_sinks, m_lam_re, m_lam_im, m_log_dt, m_b_re, m_b_im, m_c_re, m_c_im, m_d_skip, m_w_glu, m_w_attn_branch, m_w_ssm_branch, m_w_out, m_w_ff_in, m_w_ff_out, v_norm_mix_pre, v_norm_mix_post, v_norm_mlp_pre, v_norm_mlp_post, v_w_in, v_rel_bias, v_sinks, v_lam_re, v_lam_im, v_log_dt, v_b_re, v_b_im, v_c_re, v_c_im, v_d_skip, v_w_glu, v_w_attn_branch, v_w_ssm_branch, v_w_out, v_w_ff_in, v_w_ff_out):
    env = dict(locals())
    w = {n: env[n] for n in WEIGHTS}
    m = {n: env["m_" + n] for n in WEIGHTS}
    v = {n: env["v_" + n] for n in WEIGHTS}
    seq = x.shape[1]
    tb = min(512, seq)

    small = {n: w[n] for n in ('norm_mix_pre', 'norm_mix_post', 'norm_mlp_pre', 'norm_mlp_post', 'rel_bias')}
    small.update({n: w[n][0] for n in ('sinks', 'lam_re', 'lam_im', 'log_dt', 'b_re', 'b_im', 'c_re', 'c_im')})
    small['d_skip'] = w['d_skip']
    shard = lambda t, n: t[n][0].T if n == 'w_in' else t[n][0]
    unshard = lambda a, n: (a.T if n == 'w_in' else a)[None]
    _, dx, small_fl, pending = _local_step(
        x[0], loss_target[0], small, {n: _bf(shard(w, n)) for n in BIG}, tb, True)

    grads, deltas, new_m, new_v = {}, {}, {}, {}

    def adam(n, partials, after=None):
        outs = _adam_pair("adam_" + n, (shard(w, n), *partials, shard(m, n), shard(v, n)), after)
        grads[n], deltas[n], new_m[n], new_v[n] = [unshard(a, n) for a in outs]
        return outs[3]

    mix = ("w_glu", "w_attn_branch", "w_ssm_branch", "w_out")
    in_fl = _scatter_off("scatter_w_in_off", [pending["b_in"]], pending["d_in"])
    sib_mix = _exchange_alone("swap_mix", _swap_sibling(pending["p_mix"]))
    last = None
    for n, partials in [(n, pending[n]) for n in ("w_ff_in", "w_ff_out")] + list(zip(mix, zip(pending["p_mix"], sib_mix))):
        last = adam(n, partials, in_fl.token)

    small_g = _sum_devices(_land("small_land", small_fl, last)[1][0])
    loss = small_g[LOSS_ROW, 0]
    minor = lambda t, n: jnp.swapaxes(t, -1, -2) if n in STATE_MINOR else t
    g_small = _unpack(small_g, {n: w[n].shape for n in SMALL})
    outs = _adam_small([minor(w[n], n) for n in SMALL], [g_small[n] for n in SMALL],
                       [minor(m[n], n) for n in SMALL], [minor(v[n], n) for n in SMALL])
    grads.update({n: minor(g_small[n], n) for n in SMALL})
    for k, dst in enumerate((deltas, new_m, new_v)):
        dst.update({n: minor(a, n) for n, a in zip(SMALL, outs[k * len(SMALL):(k + 1) * len(SMALL)])})

    (r_in,) = _land("scatter_w_in_land", in_fl, outs[0])[1]
    p_in = _sum4("sum_w_in", pending["d_in"], r_in, pending["me"])
    (s_in,) = _exchange_alone("swap_w_in", _swap_sibling([p_in]))
    adam("w_in", (p_in, s_in))

    return (loss, dx[None], *[grads[n] for n in WEIGHTS], *[deltas[n] for n in WEIGHTS],
            *[new_m[n] for n in WEIGHTS], *[new_v[n] for n in WEIGHTS])
```

```python
import functools
import math

import numpy as np
import jax
import jax.numpy as jnp
from jax import lax
from jax.experimental import pallas as pl
from jax.experimental.pallas import tpu as pltpu

F32 = jnp.float32
BF16 = jnp.bfloat16

D_MODEL = 1024
N_HEADS = 8
N_KV = 2
Q_GROUP = 4
HEAD_DIM = 64
ATTN_W = 512
KV_W = 128
BLOCK = 128
N_BUCKETS = 32
MAX_DISTANCE = 128
NEG_INF = -1e30
SSM_W = 512
SSM_GROUP = 16
SSM_GROUPS = 32
SSM_STATE = 64
N_SUPER = 4
GROUPS_PER_SUPER = SSM_GROUPS // N_SUPER
SUPER_IN = GROUPS_PER_SUPER * SSM_GROUP
SUPER_HALF = GROUPS_PER_SUPER * SSM_STATE
SUPER_W = 2 * SUPER_HALF
STATE_COLS = N_SUPER * SUPER_W
D_FF = 4096
FF_CHUNKS = 4
IN_W = 3328
SPLITS = (0, 512, 640, 768, 1280, 2304, 3328)
RMS_EPS = 1e-6
N_CHIPS = 4
N_DEV = 8
SUBLANES = 8
LANES = 128
STATE_TILES = STATE_COLS // LANES
SUPER_TILES = SUPER_W // LANES

ADAM_LR = 0.001
ADAM_B1 = 0.9
ADAM_B2 = 0.999
ADAM_EPS = 1e-08
ADAM_WD = 0.01
ADAM_STEP = 10

VMEM_BIG = 56 * 1024 * 1024
SDS = jax.ShapeDtypeStruct
MESH_ID = pl.DeviceIdType.MESH
ANY = pl.BlockSpec(memory_space=pl.ANY)


def _bf(x):
    return x.astype(BF16)


def _mm(a, b):
    return jnp.dot(a, b, preferred_element_type=F32)


def _mm_nt(a, b):
    return lax.dot_general(a, b, (((1,), (1,)), ((), ())), preferred_element_type=F32)


def _mm_tn(a, b):
    return lax.dot_general(a, b, (((0,), (0,)), ((), ())), preferred_element_type=F32)


def _sig(x):
    return 1.0 / (1.0 + jnp.exp(-x))


def _rms(x, g):
    r = lax.rsqrt(jnp.mean(x * x, axis=-1, keepdims=True) + RMS_EPS)
    xh = x * r
    return xh * g, xh, r


def _rms_bwd(dout, xh, r, g):
    dg = jnp.sum(dout * xh, axis=0, keepdims=True)
    dxh = dout * g
    dx = r * (dxh - xh * jnp.mean(dxh * xh, axis=-1, keepdims=True))
    return dx, dg


_GELU_C = math.sqrt(2.0 / math.pi)


def _gelu_and_grad(x):
    x2 = x * x
    inner = _GELU_C * (x + 0.044715 * (x2 * x))
    t = jnp.tanh(inner)
    y = 0.5 * x * (1.0 + t)
    dy = 0.5 * (1.0 + t) + 0.5 * x * (1.0 - t * t) * (_GELU_C * (1.0 + 3.0 * 0.044715 * x2))
    return y, dy


def _zero_map(nd, *_):
    return (0,) * nd


def _params(n_axes, vmem=None):
    return pltpu.CompilerParams(dimension_semantics=("arbitrary",) * n_axes, vmem_limit_bytes=vmem)


class _Exchange:
    def __init__(self, ins, outs, sems, start, wait):
        self.ins, self.outs, self.sems, self.start, self.wait = list(ins), list(outs), list(sems), start, wait


def _fused_call(name, body, grid, in_specs, out_specs, out_shape, scratch, args, exchange, params):
    n_in, n_out, n_scr = len(in_specs), len(out_specs), len(scratch)
    if exchange is None:
        fn = body
    else:
        ex = exchange
        n_xi, n_xo = len(ex.ins), len(ex.outs)

        def fn(*refs):
            at = 0
            parts = []
            for n in (n_in, n_xi, n_out, n_xo, n_scr, len(ex.sems)):
                parts.append(refs[at:at + n])
                at += n
            ins, x_in, outs, x_out, scr, x_sem = parts
            ids = [pl.program_id(a) for a in range(len(grid))]
            first = functools.reduce(jnp.logical_and, [i == 0 for i in ids])
            last = functools.reduce(jnp.logical_and, [i == g - 1 for i, g in zip(ids, grid)])

            @pl.when(first)
            def _():
                ex.start(x_in, x_out, x_sem)

            body(*ins, *outs, *scr)

            @pl.when(last)
            def _():
                ex.wait(x_in, x_out, x_sem)

        in_specs = list(in_specs) + [ANY] * n_xi
        out_specs = list(out_specs) + [ANY] * n_xo
        out_shape = list(out_shape) + ex.outs
        scratch = list(scratch) + ex.sems
        args = list(args) + ex.ins
    return pl.pallas_call(fn, grid=grid, in_specs=in_specs, out_specs=out_specs, out_shape=out_shape,
                          scratch_shapes=list(scratch), name=name, compiler_params=params)(*args)


def _exchange_alone(name, ex):
    def body(*refs):
        n_xi, n_xo = len(ex.ins), len(ex.outs)
        x_in, x_out, x_sem = refs[:n_xi], refs[n_xi:n_xi + n_xo], refs[n_xi + n_xo:]
        ex.start(x_in, x_out, x_sem)
        ex.wait(x_in, x_out, x_sem)

    return pl.pallas_call(body, in_specs=[ANY] * len(ex.ins), out_specs=[ANY] * len(ex.outs), out_shape=ex.outs,
                          scratch_shapes=ex.sems, name=name)(*ex.ins)


def _rowcall(name, body, seq, tb, rows, consts, row_outs, acc_outs, scratch=(), reverse=False, vmem=None,
             exchange=None):
    nb = seq // tb
    rmap = (lambda i: (nb - 1 - i, 0)) if reverse else (lambda i: (i, 0))
    tmap = lambda i: (0,) + rmap(i)

    def row_spec(width):
        if isinstance(width, tuple):
            return pl.BlockSpec((width[0], tb, width[1]), tmap)
        return pl.BlockSpec((tb, width), rmap)

    def row_shape(width):
        return (width[0], seq, width[1]) if isinstance(width, tuple) else (seq, width)

    in_specs = [row_spec(a.shape[1] if a.ndim == 2 else (a.shape[0], a.shape[2])) for a in rows]
    in_specs += [pl.BlockSpec(a.shape, functools.partial(_zero_map, a.ndim), pipeline_mode=pl.Buffered(1))
                 for a in consts]
    out_specs = [row_spec(c) for c, _ in row_outs] + [ANY] * len(acc_outs)
    out_shape = [SDS(row_shape(c), dt) for c, dt in row_outs] + [SDS(s, dt) for s, dt in acc_outs]
    n_main = len(rows) + len(consts) + len(row_outs)
    n_acc = len(acc_outs)

    def fn(*refs):
        main, acc_hbm, rest = refs[:n_main], refs[n_main:n_main + n_acc], refs[n_main + n_acc:]
        acc_vmem, own = rest[:n_acc], rest[n_acc:]
        body(*main, *acc_vmem, *own)

        @pl.when(pl.program_id(0) == nb - 1)
        def _():
            for src, dst in zip(acc_vmem, acc_hbm):
                pltpu.sync_copy(src, dst)

    buffers = [pltpu.VMEM(s, dt) for s, dt in acc_outs] + list(scratch)
    return _fused_call(name, fn if acc_outs else body, (nb,), in_specs, out_specs, out_shape, buffers,
                       [*rows, *consts], exchange, _params(1, vmem))


def _inproj_fwd(x, g1, w_in, tb, exchange=None):
    seq = x.shape[0]

    def body(x_ref, g_ref, w_ref, h_ref, q_ref, k_ref, v_ref, u_ref, ga_ref, gs_ref):
        h, _, _ = _rms(x_ref[...], g_ref[...])
        hb = _bf(h)
        h_ref[...] = hb
        pj = _mm_nt(hb, w_ref[...])
        q_ref[...] = _bf(pj[:, SPLITS[0]:SPLITS[1]])
        k_ref[...] = _bf(pj[:, SPLITS[1]:SPLITS[2]])
        v_ref[...] = _bf(pj[:, SPLITS[2]:SPLITS[3]])
        u_ref[...] = pj[:, SPLITS[3]:SPLITS[4]]
        ga_ref[...] = pj[:, SPLITS[4]:SPLITS[5]]
        gs_ref[...] = pj[:, SPLITS[5]:SPLITS[6]]

    return _rowcall("inproj_fwd", body, seq, tb, [x], [g1, w_in],
                    [(D_MODEL, BF16), (ATTN_W, BF16), (KV_W, BF16), (KV_W, BF16), (SSM_W, F32),
                     (D_MODEL, F32), (D_MODEL, F32)], [], vmem=VMEM_BIG, exchange=exchange)


def _inproj_bwd(x, dx2, dq, dk, dv, du, dga, dgs, g1, w_in, tb, exchange=None):
    seq = x.shape[0]

    def body(x_ref, dx2_ref, dq_ref, dk_ref, dv_ref, du_ref, dga_ref, dgs_ref, g_ref, w_ref,
             dx_ref, dpj_ref, dg_ref):
        @pl.when(pl.program_id(0) == 0)
        def _():
            dg_ref[...] = jnp.zeros_like(dg_ref)

        dpj = jnp.concatenate([dq_ref[...], dk_ref[...], dv_ref[...], _bf(du_ref[...]),
                               dga_ref[...], dgs_ref[...]], axis=1)
        dpj_ref[...] = dpj
        dh = _mm(dpj, w_ref[...])
        g = g_ref[...]
        _, xh, r = _rms(x_ref[...], g)
        dxn, dg = _rms_bwd(dh, xh, r, g)
        dx_ref[...] = dx2_ref[...] + dxn
        dg_ref[...] += dg

    return _rowcall("inproj_bwd", body, seq, tb, [x, dx2, dq, dk, dv, du, dga, dgs], [g1, w_in],
                    [(D_MODEL, F32), (IN_W, BF16)], [((1, D_MODEL), F32)], vmem=VMEM_BIG, exchange=exchange)


def _bucket_table():
    qi = np.arange(BLOCK)[:, None]
    kj = np.arange(2 * BLOCK)[None, :]
    dist = qi + BLOCK - kj
    max_exact = N_BUCKETS // 2
    d = np.maximum(dist, 0)
    df = np.maximum(d, 1).astype(np.float32)
    large = max_exact + (np.log(df / np.float32(max_exact)) / np.float32(math.log(MAX_DISTANCE / max_exact))
                         * np.float32(N_BUCKETS - max_exact)).astype(np.int32)
    large = np.minimum(large, N_BUCKETS - 1)
    bucket = np.where(d < max_exact, d, large)
    valid = (dist >= 0) & (dist < BLOCK)
    return np.where(valid, bucket, -1).astype(np.int32)


def _bias_table(rel_bias, bucket):
    def body(rb_ref, bk_ref, o_ref):
        bk = bk_ref[...]
        has_prev = lax.broadcasted_iota(jnp.int32, bk.shape, 1) >= BLOCK
        for h in range(N_HEADS):
            kh, j, par = h // Q_GROUP, (h // 2) % 2, h % 2
            acc = jnp.full((BLOCK, 2 * BLOCK), NEG_INF, F32)
            for b in range(N_BUCKETS):
                acc = jnp.where(bk == b, rb_ref[b, h], acc)
            o_ref[0, kh, par, :, j * BLOCK:(j + 1) * BLOCK] = jnp.where(has_prev, acc, NEG_INF).T
            o_ref[1, kh, par, :, j * BLOCK:(j + 1) * BLOCK] = acc.T

    return pl.pallas_call(
        body, out_shape=SDS((2, N_KV, 2, 2 * BLOCK, 2 * BLOCK), F32),
        in_specs=[pl.BlockSpec(memory_space=pltpu.SMEM), pl.BlockSpec(memory_space=pltpu.VMEM)],
        out_specs=pl.BlockSpec(memory_space=pltpu.VMEM), name="bias_table",
    )(rel_bias, bucket)


def _bias_grad(dbias, bucket):
    def body(db_ref, bk_ref, o_ref):
        bk = bk_ref[...]
        for h in range(N_HEADS):
            kh, j, par = h // Q_GROUP, (h // 2) % 2, h % 2
            db = db_ref[kh, par, :, j * BLOCK:(j + 1) * BLOCK].T
            for b in range(N_BUCKETS):
                o_ref[b, h] = jnp.sum(jnp.where(bk == b, db, 0.0))

    return pl.pallas_call(
        body, out_shape=SDS((N_BUCKETS, N_HEADS), F32),
        in_specs=[pl.BlockSpec(memory_space=pltpu.VMEM), pl.BlockSpec(memory_space=pltpu.VMEM)],
        out_specs=pl.BlockSpec(memory_space=pltpu.SMEM), name="bias_grad",
    )(dbias, bucket)


TILE = 2 * HEAD_DIM


def _pair_layout(t):
    lead = t.shape[:-3]
    t = t.reshape(lead + (N_KV, 2, 2) + t.shape[-2:])
    nl = len(lead)
    t = jnp.transpose(t, tuple(range(nl)) + (nl, nl + 2, nl + 1, nl + 3, nl + 4))
    return t.reshape(lead + (N_KV, 2, 2 * BLOCK, t.shape[-1]))


def _pair_unlayout(t):
    t = t.reshape(N_KV, 2, 2, BLOCK, t.shape[-1]).transpose(0, 2, 1, 3, 4)
    return t.reshape(N_HEADS, BLOCK, t.shape[-1])


def _halves(t):
    tf = t.astype(F32)
    low = lax.broadcasted_iota(jnp.int32, tf.shape, 1) < HEAD_DIM
    swapped = pltpu.roll(tf, HEAD_DIM, 1)
    zero = jnp.zeros_like(tf)
    return ((_bf(jnp.where(low, tf, zero)), _bf(jnp.where(low, zero, swapped))),
            (_bf(jnp.where(low, swapped, zero)), _bf(jnp.where(low, zero, tf))))


def _fold_halves(even, odd):
    low = lax.broadcasted_iota(jnp.int32, even.shape, 1) < HEAD_DIM
    comb = jnp.where(low, even, odd)
    return comb + pltpu.roll(comb, HEAD_DIM, 1)


def _tile_rows(ref, kh):
    return jnp.concatenate([ref[:, (2 * kh) * TILE:(2 * kh + 1) * TILE],
                            ref[:, (2 * kh + 1) * TILE:(2 * kh + 2) * TILE]], axis=0)


def _halves_t(t):
    tt = t.astype(F32).T
    top = lax.broadcasted_iota(jnp.int32, tt.shape, 0) < HEAD_DIM
    swapped = jnp.concatenate([tt[HEAD_DIM:], tt[:HEAD_DIM]], axis=0)
    zero = jnp.zeros_like(tt)
    return ((_bf(jnp.where(top, tt, zero)), _bf(jnp.where(top, zero, swapped))),
            (_bf(jnp.where(top, swapped, zero)), _bf(jnp.where(top, zero, tt))))


def _attn_probs(km, qk, bias, sink):
    lg = _mm_nt(km, qk) * (HEAD_DIM ** -0.5) + bias
    m = jnp.maximum(jnp.max(lg, axis=0, keepdims=True), sink)
    p = jnp.exp(lg - m)
    es = jnp.exp(sink - m)
    inv = 1.0 / (jnp.sum(p, axis=0, keepdims=True) + es)
    return p * inv, es * inv


def _attn_fwd(q, k, v, bias, sink_rows, exchange=None):
    seq = q.shape[0]
    nblk = seq // BLOCK

    def body(q_ref, kp_ref, kc_ref, vp_ref, vc_ref, b_ref, s_ref, o_ref):
        which = jnp.minimum(pl.program_id(0), 1)
        kms = _halves(jnp.concatenate([kp_ref[...], kc_ref[...]], axis=0))
        vts = _halves_t(jnp.concatenate([vp_ref[...], vc_ref[...]], axis=0))
        for kh in range(N_KV):
            qk = _tile_rows(q_ref, kh)
            acc = jnp.zeros((TILE, 2 * BLOCK), F32)
            for par in range(2):
                pr, _ = _attn_probs(kms[kh][par], qk, b_ref[which, kh, par], s_ref[kh, par])
                acc = acc + _mm(vts[kh][par], _bf(pr))
            acc = acc.T
            o_ref[:, (2 * kh) * TILE:(2 * kh + 1) * TILE] = _bf(acc[:BLOCK])
            o_ref[:, (2 * kh + 1) * TILE:(2 * kh + 2) * TILE] = _bf(acc[BLOCK:])

    cur = lambda n: (n, 0)
    prev = lambda n: (jnp.maximum(n - 1, 0), 0)
    return _fused_call(
        "attn_fwd", body, (nblk,),
        [pl.BlockSpec((BLOCK, ATTN_W), cur),
         pl.BlockSpec((BLOCK, KV_W), prev), pl.BlockSpec((BLOCK, KV_W), cur),
         pl.BlockSpec((BLOCK, KV_W), prev), pl.BlockSpec((BLOCK, KV_W), cur),
         pl.BlockSpec(bias.shape, functools.partial(_zero_map, bias.ndim)),
         pl.BlockSpec(sink_rows.shape, functools.partial(_zero_map, sink_rows.ndim))],
        [pl.BlockSpec((BLOCK, ATTN_W), cur)], [SDS((seq, ATTN_W), BF16)], [],
        [q, k, k, v, v, bias, sink_rows], exchange, _params(1))


def _attn_bwd(q, k, v, d_out, bias, sink_rows, exchange=None):
    seq = q.shape[0]
    nblk = seq // BLOCK

    def body(q_ref, kp_ref, kc_ref, vp_ref, vc_ref, do_ref, b_ref, s_ref,
             dq_ref, dk_ref, dv_ref, db_ref, ds_ref, ck_ref, cv_ref):
        n = pl.program_id(0)

        @pl.when(n == 0)
        def _():
            db_ref[...] = jnp.zeros_like(db_ref)
            ds_ref[...] = jnp.zeros_like(ds_ref)
            ck_ref[...] = jnp.zeros_like(ck_ref)
            cv_ref[...] = jnp.zeros_like(cv_ref)

        @pl.when(n < nblk)
        def _():
            which = jnp.minimum(n, 1)
            scale = HEAD_DIM ** -0.5
            kcat = jnp.concatenate([kp_ref[...], kc_ref[...]], axis=0)
            kms = _halves(kcat)
            kts = _halves_t(kcat)
            vms = _halves(jnp.concatenate([vp_ref[...], vc_ref[...]], axis=0))
            dks, dvs = [], []
            for kh in range(N_KV):
                qk = _tile_rows(q_ref, kh)
                dok = _tile_rows(do_ref, kh)
                dq = jnp.zeros((TILE, 2 * BLOCK), F32)
                dkp, dvp = [], []
                for par in range(2):
                    pr, ps = _attn_probs(kms[kh][par], qk, b_ref[which, kh, par], s_ref[kh, par])
                    dp = _mm_nt(vms[kh][par], dok)
                    rs = jnp.sum(pr * dp, axis=0, keepdims=True)
                    dlg = pr * (dp - rs)
                    ds_ref[kh, par] += -ps * rs
                    db_ref[kh, par] += dlg
                    dlb = _bf(dlg)
                    dq = dq + _mm(kts[kh][par], dlb)
                    dkp.append(_mm(dlb, qk))
                    dvp.append(_mm(_bf(pr), dok))
                dq = _bf((dq * scale).T)
                dq_ref[:, (2 * kh) * TILE:(2 * kh + 1) * TILE] = dq[:BLOCK]
                dq_ref[:, (2 * kh + 1) * TILE:(2 * kh + 2) * TILE] = dq[BLOCK:]
                dks.append(_fold_halves(*dkp))
                dvs.append(_fold_halves(*dvp))
            low = lax.broadcasted_iota(jnp.int32, (2 * BLOCK, TILE), 1) < HEAD_DIM
            dkk = jnp.where(low, dks[0], dks[1]) * scale
            dvv = jnp.where(low, dvs[0], dvs[1])
            dk_ref[...] = _bf(ck_ref[...] + dkk[:BLOCK])
            ck_ref[...] = dkk[BLOCK:]
            dv_ref[...] = _bf(cv_ref[...] + dvv[:BLOCK])
            cv_ref[...] = dvv[BLOCK:]

        @pl.when(n == nblk)
        def _():
            dk_ref[...] = _bf(ck_ref[...])
            dv_ref[...] = _bf(cv_ref[...])

    cur = lambda n: (jnp.minimum(n, nblk - 1), 0)
    prev = lambda n: (jnp.maximum(jnp.minimum(n, nblk - 1) - 1, 0), 0)
    late = lambda n: (jnp.maximum(n - 1, 0), 0)
    kv_spec = lambda m: pl.BlockSpec((BLOCK, KV_W), m)
    acc_b = pl.BlockSpec(bias.shape[1:], functools.partial(_zero_map, bias.ndim - 1))
    acc_s = pl.BlockSpec(sink_rows.shape, functools.partial(_zero_map, sink_rows.ndim))
    return _fused_call(
        "attn_bwd", body, (nblk + 1,),
        [pl.BlockSpec((BLOCK, ATTN_W), cur), kv_spec(prev), kv_spec(cur), kv_spec(prev), kv_spec(cur),
         pl.BlockSpec((BLOCK, ATTN_W), cur),
         pl.BlockSpec(bias.shape, functools.partial(_zero_map, bias.ndim)), acc_s],
        [pl.BlockSpec((BLOCK, ATTN_W), cur), kv_spec(late), kv_spec(late), acc_b, acc_s],
        [SDS((seq, ATTN_W), BF16), SDS((seq, KV_W), BF16), SDS((seq, KV_W), BF16),
         SDS(bias.shape[1:], F32), SDS(sink_rows.shape, F32)],
        [pltpu.VMEM((BLOCK, KV_W), F32), pltpu.VMEM((BLOCK, KV_W), F32)],
        [q, k, k, v, v, d_out, bias, sink_rows], exchange, _params(1))


def _ssm_discretize(lam_re, lam_im, log_dt, b_re, b_im):
    dt = jnp.exp(log_dt)[:, None]
    mag = jnp.exp(lam_re * dt)
    ab_re = mag * jnp.cos(lam_im * dt)
    ab_im = mag * jnp.sin(lam_im * dt)
    nr = ab_re - 1.0
    den = lam_re * lam_re + lam_im * lam_im
    f_re = (nr * lam_re + ab_im * lam_im) / den
    f_im = (ab_im * lam_re - nr * lam_im) / den
    bb_re = f_re[..., None] * b_re - f_im[..., None] * b_im
    bb_im = f_re[..., None] * b_im + f_im[..., None] * b_re
    return ab_re, ab_im, bb_re, bb_im


def _state_layout(re, im):
    z = jnp.stack([re, im]).reshape(2, N_SUPER, GROUPS_PER_SUPER, SSM_STATE)
    return z.transpose(1, 0, 2, 3).reshape(STATE_COLS)


def _state_unlayout(vec):
    z = vec.reshape(N_SUPER, 2, GROUPS_PER_SUPER, SSM_STATE).transpose(1, 0, 2, 3)
    z = z.reshape(2, SSM_GROUPS, SSM_STATE)
    return z[0], z[1]


SEG = 4
WINDOW = SEG * SUBLANES


def _scan_tables(ab_re, ab_im):
    pw = [None, (ab_re, ab_im)]
    for _ in range(2, WINDOW + 1):
        pr, pi_ = pw[-1]
        pw.append((pr * ab_re - pi_ * ab_im, pr * ab_im + pi_ * ab_re))
    rows = np.arange(SUBLANES)[:, None]
    ones = np.ones((SUBLANES, 1), np.float32)
    conj = lambda p: (p[0], -p[1])
    fwd, bwd = [], []
    for shift in (1, 2, 4):
        fwd.append(_state_layout(*pw[SEG * shift])[None, :] * (rows >= shift).astype(np.float32))
        bwd.append(_state_layout(*conj(pw[SEG * shift]))[None, :] * (rows < SUBLANES - shift).astype(np.float32))
    fwd.append(jnp.stack([_state_layout(*pw[SEG * (r + 1)]) for r in range(SUBLANES)]))
    bwd.append(jnp.stack([_state_layout(*conj(pw[SEG * (SUBLANES - r)])) for r in range(SUBLANES)]))
    for k in range(1, SEG):
        fwd.append(_state_layout(*pw[k])[None, :] * ones)
        bwd.append(_state_layout(*conj(pw[k]))[None, :] * ones)
    return jnp.stack(fwd), jnp.stack(bwd)


_EYE = np.eye(GROUPS_PER_SUPER, dtype=np.float32)


def _b_matrix(bb_re, bb_im):
    bb = jnp.stack([bb_re, bb_im]).reshape(2, N_SUPER, GROUPS_PER_SUPER, SSM_STATE, SSM_GROUP)
    m = jnp.einsum('rsgpc,gh->sgcrhp', bb, _EYE)
    return m.reshape(N_SUPER, SUPER_IN, SUPER_W)


def _b_matrix_grad(dm):
    d = dm.reshape(N_SUPER, GROUPS_PER_SUPER, SSM_GROUP, 2, GROUPS_PER_SUPER, SSM_STATE)
    d = jnp.sum(d * _EYE[None, :, None, None, :, None], axis=4)
    d = d.transpose(3, 0, 1, 4, 2).reshape(2, SSM_GROUPS, SSM_STATE, SSM_GROUP)
    return d[0], d[1]


def _c_matrix(c_re, c_im):
    cc = jnp.stack([c_re, -c_im]).reshape(2, N_SUPER, GROUPS_PER_SUPER, SSM_GROUP, SSM_STATE)
    m = jnp.einsum('rsgcp,gh->srgphc', cc, _EYE)
    return m.reshape(N_SUPER, SUPER_W, SUPER_IN)


def _c_matrix_grad(dm):
    d = dm.reshape(N_SUPER, 2, GROUPS_PER_SUPER, SSM_STATE, GROUPS_PER_SUPER, SSM_GROUP)
    d = jnp.sum(d * _EYE[None, None, :, None, :, None], axis=4)
    d = d.transpose(1, 0, 2, 4, 3).reshape(2, SSM_GROUPS, SSM_GROUP, SSM_STATE)
    return d[0], -d[1]


def _cmul_add(xr, xi, ar, ai, sr, si):
    return xr + ar * sr - ai * si, xi + ar * si + ai * sr


def _scan_rows(buf_ref, tab_ref, carry_ref, n_windows, reverse, h_ref=None, da_ref=None):
    order = list(range(SEG - 1, -1, -1)) if reverse else list(range(SEG))
    near = SUBLANES - 1 if reverse else 0
    far = 0 if reverse else SUBLANES - 1
    s_in = SUBLANES - 1 if reverse else 1
    lanes = lambda tile: pl.ds(tile * LANES, LANES)

    def window(w0, tile_re, tile_im, c_re, c_im, acc):
        rows = lambda t: pl.ds(w0 + t, SUBLANES, stride=SEG)
        get = lambda ref, t: (ref.at[tile_re][rows(t), :], ref.at[tile_im][rows(t), :])
        tab = lambda k: (tab_ref[k, :, lanes(tile_re)], tab_ref[k, :, lanes(tile_im)])

        def put(t, xr, xi):
            buf_ref.at[tile_re][rows(t), :] = xr
            buf_ref.at[tile_im][rows(t), :] = xi

        a1 = tab(4)
        er, ei = get(buf_ref, order[0])
        for t in order[1:]:
            er, ei = _cmul_add(*get(buf_ref, t), *a1, er, ei)
            if t != order[-1]:
                put(t, er, ei)
        for k, shift in enumerate((1, 2, 4)):
            s = (SUBLANES - shift) if reverse else shift
            er, ei = _cmul_add(er, ei, *tab(k), pltpu.roll(er, s, 0), pltpu.roll(ei, s, 0))
        er, ei = _cmul_add(er, ei, *tab(3), c_re, c_im)
        put(order[-1], er, ei)
        sub = lax.broadcasted_iota(jnp.int32, er.shape, 0)
        in_re = jnp.where(sub == near, c_re, pltpu.roll(er, s_in, 0))
        in_im = jnp.where(sub == near, c_im, pltpu.roll(ei, s_in, 0))
        true = {order[-1]: (er, ei)}
        for idx, t in enumerate(order[:-1]):
            true[t] = _cmul_add(*get(buf_ref, t), *tab(4 + idx), in_re, in_im)
            put(t, *true[t])
        carry = (jnp.broadcast_to(er[far:far + 1], er.shape), jnp.broadcast_to(ei[far:far + 1], ei.shape))
        if acc is None:
            return carry, None
        acc_re, acc_im = acc
        for t in range(SEG):
            if t + 1 < SEG:
                gr, gim = true[t + 1]
            else:
                gr = jnp.where(sub == SUBLANES - 1, c_re, pltpu.roll(true[0][0], SUBLANES - 1, 0))
                gim = jnp.where(sub == SUBLANES - 1, c_im, pltpu.roll(true[0][1], SUBLANES - 1, 0))
            hr, hi = get(h_ref, t)
            acc_re = acc_re + gr * hr + gim * hi
            acc_im = acc_im + gim * hr - gr * hi
        return carry, (acc_re, acc_im)

    half = SUPER_HALF // LANES
    per = 2 if h_ref is None else 4
    for sb in range(N_SUPER):
        pairs = [(2 * half * sb + j, 2 * half * sb + half + j) for j in range(half)]

        def step(wi, state, pairs=pairs):
            w = (n_windows - 1 - wi) if reverse else wi
            w0 = pl.multiple_of(w * WINDOW, WINDOW)
            out = []
            for j, (tile_re, tile_im) in enumerate(pairs):
                mine = state[per * j:per * (j + 1)]
                carry, acc = window(w0, tile_re, tile_im, mine[0], mine[1], mine[2:] or None)
                out += list(carry) + list(acc or ())
            return tuple(out)

        init = []
        for tile_re, tile_im in pairs:
            init += [carry_ref[:, lanes(tile_re)], carry_ref[:, lanes(tile_im)]]
            if h_ref is not None:
                init += [da_ref[:, lanes(tile_re)], da_ref[:, lanes(tile_im)]]
        fin = lax.fori_loop(0, n_windows, step, tuple(init))
        for j, (tile_re, tile_im) in enumerate(pairs):
            carry_ref[:, lanes(tile_re)] = fin[per * j]
            carry_ref[:, lanes(tile_im)] = fin[per * j + 1]
            if h_ref is not None:
                da_ref[:, lanes(tile_re)] = fin[per * j + 2]
                da_ref[:, lanes(tile_im)] = fin[per * j + 3]


def _put_tiles(ref, sb, block):
    for j in range(SUPER_TILES):
        ref[sb * SUPER_TILES + j] = block[:, j * LANES:(j + 1) * LANES]


def _get_tiles(ref, sb):
    return jnp.concatenate([ref[sb * SUPER_TILES + j] for j in range(SUPER_TILES)], axis=1)


def _ssm_fwd(u, bmat, cmat, tab, d_skip, tb, exchange=None):
    seq = u.shape[0]

    def body(u_ref, b_ref, c_ref, t_ref, d_ref, s_ref, h_ref, carry_ref):
        @pl.when(pl.program_id(0) == 0)
        def _():
            carry_ref[...] = jnp.zeros_like(carry_ref)

        u_blk = u_ref[...]
        ub = _bf(u_blk)
        for sb in range(N_SUPER):
            _put_tiles(h_ref, sb, _mm(ub[:, sb * SUPER_IN:(sb + 1) * SUPER_IN], b_ref[sb]))
        _scan_rows(h_ref, t_ref, carry_ref, tb // WINDOW, False)
        ys = [_mm(_bf(_get_tiles(h_ref, sb)), c_ref[sb]) for sb in range(N_SUPER)]
        s_ref[...] = jnp.concatenate(ys, axis=1) + d_ref[...] * u_blk

    return _rowcall("ssm_fwd", body, seq, tb, [u], [bmat, cmat, tab, d_skip],
                    [(SSM_W, F32), ((STATE_TILES, LANES), F32)], [],
                    scratch=[pltpu.VMEM((SUBLANES, STATE_COLS), F32)], vmem=VMEM_BIG, exchange=exchange)


def _ssm_bwd(ds, u, h, bmat_t, cmat_t, tab, d_skip, tb, exchange=None):
    seq = u.shape[0]

    def body(ds_ref, u_ref, h_ref, bt_ref, ct_ref, t_ref, d_ref,
             du_ref, db_ref, dc_ref, da_ref, dd_ref, g_ref, carry_ref):
        @pl.when(pl.program_id(0) == 0)
        def _():
            carry_ref[...] = jnp.zeros_like(carry_ref)
            db_ref[...] = jnp.zeros_like(db_ref)
            dc_ref[...] = jnp.zeros_like(dc_ref)
            da_ref[...] = jnp.zeros_like(da_ref)
            dd_ref[...] = jnp.zeros_like(dd_ref)

        ds_blk = ds_ref[...]
        dsb = _bf(ds_blk)
        u_blk = u_ref[...]
        ub = _bf(u_blk)
        for sb in range(N_SUPER):
            _put_tiles(g_ref, sb, _mm(dsb[:, sb * SUPER_IN:(sb + 1) * SUPER_IN], ct_ref[sb]))
        _scan_rows(g_ref, t_ref, carry_ref, tb // WINDOW, True, h_ref=h_ref, da_ref=da_ref)
        dus = []
        for sb in range(N_SUPER):
            gb = _bf(_get_tiles(g_ref, sb))
            dus.append(_mm(gb, bt_ref[sb]))
            db_ref[sb] += _mm_tn(ub[:, sb * SUPER_IN:(sb + 1) * SUPER_IN], gb)
            dc_ref[sb] += _mm_tn(_bf(_get_tiles(h_ref, sb)), dsb[:, sb * SUPER_IN:(sb + 1) * SUPER_IN])
        du_ref[...] = jnp.concatenate(dus, axis=1) + d_ref[...] * ds_blk
        dd_ref[...] += jnp.sum(ds_blk * u_blk, axis=0, keepdims=True)

    return _rowcall("ssm_bwd", body, seq, tb, [ds, u, h], [bmat_t, cmat_t, tab, d_skip],
                    [(SSM_W, F32)],
                    [((N_SUPER, SUPER_IN, SUPER_W), F32), ((N_SUPER, SUPER_W, SUPER_IN), F32),
                     ((SUBLANES, STATE_COLS), F32), ((1, SSM_W), F32)],
                    scratch=[pltpu.VMEM((STATE_TILES, tb, LANES), F32), pltpu.VMEM((SUBLANES, STATE_COLS), F32)],
                    reverse=True, vmem=VMEM_BIG, exchange=exchange)


def _merge_core(s, attb, ga, gs, wg_ref, wab_ref, wsb_ref, wout_ref):
    zg, dgelu = _gelu_and_grad(s)
    zgb = _bf(zg)
    sg = _sig(_mm(zgb, wg_ref[...]))
    z = zg * sg
    zb = _bf(z)
    ys = jnp.concatenate([_mm(zb, wsb_ref[j]) for j in range(N_CHIPS)], axis=1)
    ya = jnp.concatenate([_mm(attb, wab_ref[j]) for j in range(N_CHIPS)], axis=1)
    sa = _sig(ga)
    ss = _sig(gs)
    mgb = _bf(sa * ya + ss * ys)
    o = _mm(mgb, wout_ref[...])
    return dict(zg=zg, dgelu=dgelu, zgb=zgb, sg=sg, zb=zb, ys=ys, ya=ya, sa=sa, ss=ss, mgb=mgb, o=o)


def _merge_fwd(x, s, att, ga, gs, g2, w_glu, w_ab, w_sb, w_out, tb):
    seq = x.shape[0]

    def body(x_ref, s_ref, att_ref, ga_ref, gs_ref, g_ref, wg_ref, wab_ref, wsb_ref, wout_ref, x2_ref):
        f = _merge_core(s_ref[...], att_ref[...], ga_ref[...], gs_ref[...], wg_ref, wab_ref, wsb_ref, wout_ref)
        n, _, _ = _rms(f["o"], g_ref[...])
        x2_ref[...] = x_ref[...] + n

    return _rowcall("merge_fwd", body, seq, tb, [x, s, att, ga, gs], [g2, w_glu, w_ab, w_sb, w_out],
                    [(D_MODEL, F32)], [], vmem=VMEM_BIG)[0]


def _merge_bwd(dx2, s, att, ga, gs, g2, w_glu, w_ab, w_sb, w_out, tb, exchange=None):
    seq = s.shape[0]
    cw = D_MODEL // N_CHIPS
    last = seq // tb - 1
    sub = min(tb, 256)

    def body(dx2_ref, s_ref, att_ref, ga_ref, gs_ref, g_ref, wg_ref, wab_ref, wsb_ref, wout_ref,
             ds_ref, datt_ref, dga_ref, dgs_ref, dg_ref, dwg_ref, dwab_ref, dwsb_ref, dwout_ref,
             bwg_ref, bwab_ref, bwsb_ref, bwout_ref, mg_s, do_s, dya_s, dys_s, z_s, zg_s, dgl_s):
        @pl.when(pl.program_id(0) == 0)
        def _():
            for r in (dg_ref, dwg_ref, dwab_ref, dwsb_ref, dwout_ref):
                r[...] = jnp.zeros_like(r)

        def part(i, _):
            rows = pl.ds(pl.multiple_of(i * sub, sub), sub)
            f = _merge_core(s_ref[rows, :], att_ref[rows, :], ga_ref[rows, :], gs_ref[rows, :],
                            wg_ref, wab_ref, wsb_ref, wout_ref)
            g = g_ref[...]
            _, oh, r2 = _rms(f["o"], g)
            do, dg = _rms_bwd(dx2_ref[rows, :], oh, r2, g)
            dg_ref[...] += dg
            dob = _bf(do)
            dmg = _mm_nt(dob, wout_ref[...])
            sa, ss = f["sa"], f["ss"]
            dyab = _bf(dmg * sa)
            dysb = _bf(dmg * ss)
            dga_ref[rows, :] = _bf(dmg * f["ya"] * sa * (1.0 - sa))
            dgs_ref[rows, :] = _bf(dmg * f["ys"] * ss * (1.0 - ss))
            datt = jnp.zeros((sub, ATTN_W), F32)
            dz = jnp.zeros((sub, SSM_W), F32)
            for j in range(N_CHIPS):
                datt = datt + _mm_nt(dyab[:, j * cw:(j + 1) * cw], wab_ref[j])
                dz = dz + _mm_nt(dysb[:, j * cw:(j + 1) * cw], wsb_ref[j])
            datt_ref[rows, :] = _bf(datt)
            sg, zg = f["sg"], f["zg"]
            dglb = _bf(dz * zg * sg * (1.0 - sg))
            dzg = dz * sg + _mm_nt(dglb, wg_ref[...])
            ds_ref[rows, :] = dzg * f["dgelu"]
            for dst, val in ((mg_s, f["mgb"]), (do_s, dob), (dya_s, dyab), (dys_s, dysb), (z_s, f["zb"]),
                             (zg_s, f["zgb"]), (dgl_s, dglb)):
                dst[rows, :] = val
            return 0

        lax.fori_loop(0, tb // sub, part, 0)
        dwout_ref[...] += _mm_tn(mg_s[...], do_s[...])
        dwab = _mm_tn(att_ref[...], dya_s[...])
        dwsb = _mm_tn(z_s[...], dys_s[...])
        for j in range(N_CHIPS):
            dwab_ref[j] += dwab[:, j * cw:(j + 1) * cw]
            dwsb_ref[j] += dwsb[:, j * cw:(j + 1) * cw]
        dwg_ref[...] += _mm_tn(zg_s[...], dgl_s[...])

        @pl.when(pl.program_id(0) == last)
        def _():
            for dst, src in ((bwg_ref, dwg_ref), (bwab_ref, dwab_ref), (bwsb_ref, dwsb_ref), (bwout_ref, dwout_ref)):
                dst[...] = _bf(src[...])

    shapes = [w_glu.shape, w_ab.shape, w_sb.shape, w_out.shape]
    staged = [pltpu.VMEM((tb, c), BF16) for c in (D_MODEL, D_MODEL, D_MODEL, D_MODEL, SSM_W, SSM_W, SSM_W)]
    return _rowcall("merge_bwd", body, seq, tb, [dx2, s, att, ga, gs], [g2, w_glu, w_ab, w_sb, w_out],
                    [(SSM_W, F32), (ATTN_W, BF16), (D_MODEL, BF16), (D_MODEL, BF16)],
                    [((1, D_MODEL), F32)] + [(sh, F32) for sh in shapes] + [(sh, BF16) for sh in shapes],
                    scratch=staged, vmem=VMEM_BIG, exchange=exchange)


def _mlp_fwd_loss(x2, target, g3, g4, w_ffi, w_ffo, tb):
    seq = x2.shape[0]
    n_slab = len(w_ffi)
    sw = D_FF // FF_CHUNKS // n_slab

    def body(x2_ref, t_ref, g3_ref, g4_ref, *rest):
        wi_refs, (wo_ref, dy_ref, df_ref, h_ref, loss_ref, dg_ref) = rest[:n_slab], rest[n_slab:]

        @pl.when(pl.program_id(0) == 0)
        def _():
            loss_ref[...] = jnp.zeros_like(loss_ref)
            dg_ref[...] = jnp.zeros_like(dg_ref)

        x2_blk = x2_ref[...]
        h3, _, _ = _rms(x2_blk, g3_ref[...])
        hb = _bf(h3)
        h_ref[...] = hb
        f = jnp.zeros((tb, D_MODEL), F32)
        for j in range(FF_CHUNKS):
            for k in range(n_slab):
                a = _mm(hb, wi_refs[k][j])
                f = f + _mm(_bf(jnp.square(jnp.maximum(a, 0.0))), wo_ref[j, pl.ds(k * sw, sw), :])
        g4 = g4_ref[...]
        n4, fh, r4 = _rms(f, g4)
        e = (x2_blk + n4) - t_ref[...]
        loss_ref[...] += 0.5 * jnp.sum(jnp.mean(e * e, axis=-1, keepdims=True))
        dy = e * (1.0 / D_MODEL)
        dy_ref[...] = dy
        df, dg = _rms_bwd(dy, fh, r4, g4)
        df_ref[...] = _bf(df)
        dg_ref[...] += dg

    return _rowcall("mlp_fwd_loss", body, seq, tb, [x2, target], [g3, g4, *w_ffi, w_ffo],
                    [(D_MODEL, F32), (D_MODEL, BF16), (D_MODEL, BF16)],
                    [((SUBLANES, 128), F32), ((1, D_MODEL), F32)], vmem=VMEM_BIG)


def _mlp_bwd(x2, dy, df, h3, g3, w_ffi, w_ffo, tb):
    seq = x2.shape[0]
    n_slab = len(w_ffi)
    sw = D_FF // FF_CHUNKS // n_slab

    def body(x2_ref, dy_ref, df_ref, h_ref, g3_ref, *rest):
        wi_refs, (wo_ref, dx_ref, act_ref, da_ref, dg_ref) = rest[:n_slab], rest[n_slab:]

        @pl.when(pl.program_id(0) == 0)
        def _():
            dg_ref[...] = jnp.zeros_like(dg_ref)

        hb = h_ref[...]
        dfb = df_ref[...]
        dh = jnp.zeros((tb, D_MODEL), F32)
        for j in range(FF_CHUNKS):
            for k in range(n_slab):
                cols = pl.ds((j * n_slab + k) * sw, sw)
                ra = jnp.maximum(_mm(hb, wi_refs[k][j]), 0.0)
                act_ref[:, cols] = _bf(ra * ra)
                dab = _bf(_mm_nt(dfb, wo_ref[j, pl.ds(k * sw, sw), :]) * (2.0 * ra))
                da_ref[:, cols] = dab
                dh = dh + _mm_nt(dab, wi_refs[k][j])
        g3 = g3_ref[...]
        _, xh, r3 = _rms(x2_ref[...], g3)
        dxn, dg = _rms_bwd(dh, xh, r3, g3)
        dx_ref[...] = dy_ref[...] + dxn
        dg_ref[...] += dg

    return _rowcall("mlp_bwd", body, seq, tb, [x2, dy, df, h3], [g3, *w_ffi, w_ffo],
                    [(D_MODEL, F32), (D_FF, BF16), (D_FF, BF16)], [((1, D_MODEL), F32)], vmem=VMEM_BIG)


def _matmul_tn(name, a, b, tk, tn, tl, chunk_major, exchange=None):
    seq, kdim = a.shape
    ndim = b.shape[1]
    last = seq // tl - 1

    def body(a_ref, b_ref, o_ref, ob_ref):
        @pl.when(pl.program_id(2) == 0)
        def _():
            o_ref[...] = jnp.zeros_like(o_ref)

        o_ref[...] += _mm_tn(a_ref[...], b_ref[...])

        @pl.when(pl.program_id(2) == last)
        def _():
            ob_ref[...] = _bf(o_ref[...])

    if chunk_major:
        shape = (ndim // tn, kdim, tn)
        out_spec = pl.BlockSpec((None, tk, tn), lambda k, n, l: (n, k, 0))
    else:
        shape = (kdim, ndim)
        out_spec = pl.BlockSpec((tk, tn), lambda k, n, l: (k, n))
    return _fused_call(
        name, body, (kdim // tk, ndim // tn, seq // tl),
        [pl.BlockSpec((tl, tk), lambda k, n, l: (l, k)), pl.BlockSpec((tl, tn), lambda k, n, l: (l, n))],
        [out_spec, out_spec], [SDS(shape, F32), SDS(shape, BF16)], [], [a, b], exchange, _params(3, VMEM_BIG))


def _ew_call(name, fn, ins, n_out, after=None):
    rows, cols = ins[0].shape
    tr = rows
    while tr * cols * 4 > min(1 << 20, (9 << 20) // (len(ins) + n_out)) and tr % 16 == 0:
        tr //= 2
    spec = pl.BlockSpec((tr, cols), lambda i: (i, 0))
    extra = [] if after is None else [after]

    def body(*refs):
        outs = fn(*[r[...] for r in refs[:len(ins)]])
        for r, o in zip(refs[len(ins) + len(extra):], outs):
            r[...] = o

    return pl.pallas_call(
        body, grid=(rows // tr,), in_specs=[spec] * len(ins) + [ANY] * len(extra), out_specs=[spec] * n_out,
        out_shape=[SDS((rows, cols), F32)] * n_out, name=name, compiler_params=_params(1))(*ins, *extra)


def _adam_math(w, g, m, v):
    m2 = ADAM_B1 * m + (1.0 - ADAM_B1) * g
    v2 = ADAM_B2 * v + (1.0 - ADAM_B2) * (g * g)
    m_hat = m2 / (1.0 - ADAM_B1 ** ADAM_STEP)
    v_hat = v2 / (1.0 - ADAM_B2 ** ADAM_STEP)
    delta = -ADAM_LR * (m_hat / (jnp.sqrt(v_hat) + ADAM_EPS) + ADAM_WD * w)
    return delta, m2, v2


def _sum4(name, own, recv, idx):
    _, rows, cols = own.shape
    tr = rows
    while tr * cols * 4 > (1 << 20) and tr % 16 == 0:
        tr //= 2

    def body(idx_ref, o_ref, r0_ref, r1_ref, r2_ref, out_ref):
        out_ref[...] = ((o_ref[...] + r0_ref[...].astype(F32)) + r1_ref[...].astype(F32)) + r2_ref[...].astype(F32)

    blk = (None, tr, cols)
    grid_spec = pltpu.PrefetchScalarGridSpec(
        num_scalar_prefetch=1, grid=(rows // tr,),
        in_specs=[pl.BlockSpec(blk, lambda i, s: (s[0], i, 0)), pl.BlockSpec(blk, lambda i, s: (0, i, 0)),
                  pl.BlockSpec(blk, lambda i, s: (1, i, 0)), pl.BlockSpec(blk, lambda i, s: (2, i, 0))],
        out_specs=pl.BlockSpec((tr, cols), lambda i, s: (i, 0)))
    return pl.pallas_call(body, grid_spec=grid_spec, out_shape=SDS((rows, cols), F32), name=name,
                          compiler_params=_params(1))(jnp.reshape(idx, (1,)).astype(jnp.int32), own, recv, recv, recv)


def _adam_pair(name, item, after=None):
    def fn(w_, a, b, m_, v_):
        g = a + b
        return (g,) + _adam_math(w_, g, m_, v_)

    return _ew_call(name, fn, list(item), 4, after)


def _place():
    return lax.axis_index("x"), lax.axis_index("y"), lax.axis_index("c")


def _other_chips(x, y):
    return [(1 - x, y), (x, 1 - y), (1 - x, 1 - y)]


HBM = pl.BlockSpec(memory_space=pltpu.HBM)
SEM = pl.BlockSpec(memory_space=pltpu.SEMAPHORE)
DATAFLOW = pltpu.SideEffectType.DATAFLOW_SIDE_EFFECTING


class _Flight:
    def __init__(self, copies, n_copies, send, recv, srcs, lands, token):
        self.copies, self.n, self.send, self.recv = copies, n_copies, send, recv
        self.srcs, self.lands, self.token = list(srcs), list(lands), token


def _take_off(name, srcs, lands, copies, n_copies, after):
    n_s, n_l = len(srcs), len(lands)

    def body(*refs):
        src, land = refs[:n_s], refs[n_s:n_s + n_l]
        send, recv = refs[n_s + n_l + 1:n_s + n_l + 3]
        for cp in copies(src, land, send, recv):
            cp.start()
        refs[-1][...] = jnp.zeros_like(refs[-1])

    mem = lambda t: pltpu.HBM(t.shape, t.dtype)
    sems = pltpu.SemaphoreType.DMA((n_copies,))
    outs = pl.pallas_call(
        body, name=name,
        out_shape=(sems, sems, *map(mem, srcs), *map(mem, lands), SDS((SUBLANES, LANES), F32)),
        in_specs=[HBM] * (n_s + n_l) + [ANY],
        out_specs=(SEM, SEM, *[HBM] * (n_s + n_l), pl.BlockSpec(memory_space=pltpu.VMEM)),
        input_output_aliases={i: 2 + i for i in range(n_s + n_l)},
        compiler_params=pltpu.CompilerParams(has_side_effects=DATAFLOW),
    )(*[pltpu.with_memory_space_constraint(t, pltpu.HBM) for t in (*srcs, *lands)], after)
    return _Flight(copies, n_copies, outs[0], outs[1], outs[2:2 + n_s], outs[2 + n_s:2 + n_s + n_l], outs[-1])


def _land(name, flight, after):
    n_s, n_l = len(flight.srcs), len(flight.lands)

    def body(*refs):
        src, land = refs[:n_s], refs[n_s:n_s + n_l]
        send, recv = refs[n_s + n_l:n_s + n_l + 2]
        for cp in flight.copies(src, land, send, recv):
            cp.wait_send()
            cp.wait_recv()

    mem = lambda t: pltpu.HBM(t.shape, t.dtype)
    outs = pl.pallas_call(
        body, name=name, out_shape=(*map(mem, flight.srcs), *map(mem, flight.lands)),
        in_specs=[HBM] * (n_s + n_l) + [SEM, SEM, ANY], out_specs=tuple([HBM] * (n_s + n_l)),
        input_output_aliases={i: i for i in range(n_s + n_l)},
        compiler_params=pltpu.CompilerParams(has_side_effects=DATAFLOW),
    )(*flight.srcs, *flight.lands, flight.send, flight.recv, after)
    return list(outs[:n_s]), list(outs[n_s:])


def _empty_like(shapes_from, lead):
    return [lax.empty((lead,) + t.shape[1:], t.dtype) for t in shapes_from]


def _scatter_off(name, chunks, after):
    def copies(src, land, send, recv):
        x, y, c = _place()
        return [pltpu.make_async_remote_copy(
            src_ref=src[a].at[2 * px + py], dst_ref=land[a].at[k], send_sem=send.at[3 * a + k],
            recv_sem=recv.at[3 * a + k], device_id=(px, py, c), device_id_type=MESH_ID)
            for a in range(len(chunks)) for k, (px, py) in enumerate(_other_chips(x, y))]

    return _take_off(name, chunks, _empty_like(chunks, 3), copies, 3 * len(chunks), after)


def _swap_off(name, arrs, after):
    def copies(src, land, send, recv):
        x, y, c = _place()
        return [pltpu.make_async_remote_copy(
            src_ref=src[a], dst_ref=land[a], send_sem=send.at[a], recv_sem=recv.at[a],
            device_id=(x, y, 1 - c), device_id_type=MESH_ID) for a in range(len(arrs))]

    return _take_off(name, arrs, [lax.empty(t.shape, t.dtype) for t in arrs], copies, len(arrs), after)


def _devices_off(name, block, after):
    me = 4 * lax.axis_index("x") + 2 * lax.axis_index("y") + lax.axis_index("c")
    land = lax.dynamic_update_index_in_dim(lax.empty((N_DEV,) + block.shape, block.dtype), block, me, 0)

    def copies(src, land, send, recv):
        x, y, c = _place()
        mine = 4 * x + 2 * y + c
        return [pltpu.make_async_remote_copy(
            src_ref=src[0], dst_ref=land[0].at[mine], send_sem=send.at[k - 1], recv_sem=recv.at[k - 1],
            device_id=(x ^ (k >> 2), y ^ ((k >> 1) & 1), c ^ (k & 1)), device_id_type=MESH_ID)
            for k in range(1, N_DEV)]

    return _take_off(name, [block], [land], copies, N_DEV - 1, after)


def _half_rows(shape, c, other=False):
    half = shape[0] // 2
    return pl.ds(((1 - c) if other else c) * half, half)


def _gather_start(name, shards, lands, after):
    n = len(shards)

    def body(*refs):
        src, land, (send, recv) = refs[:n], refs[n:2 * n], refs[2 * n + 1:2 * n + 3]
        x, y, c = _place()
        me = 2 * x + y
        for a in range(n):
            mine = _half_rows(shards[a].shape, c)
            for j, (px, py) in enumerate(_other_chips(x, y)):
                pltpu.make_async_remote_copy(
                    src_ref=src[a].at[mine], dst_ref=land[a].at[me, mine], send_sem=send.at[3 * a + j],
                    recv_sem=recv.at[3 * a + j], device_id=(px, py, c), device_id_type=MESH_ID).start()
        token = refs[-1]
        token[...] = jnp.zeros_like(token)

    mem = lambda t: pltpu.HBM(t.shape, t.dtype)
    pair = pltpu.SemaphoreType.DMA((3 * n,))
    outs = pl.pallas_call(
        body, name=name,
        out_shape=(pair, pair, *map(mem, shards), *map(mem, lands), SDS((SUBLANES, LANES), F32)),
        in_specs=[HBM] * (2 * n) + [ANY],
        out_specs=(SEM, SEM, *[HBM] * (2 * n), pl.BlockSpec(memory_space=pltpu.VMEM)),
        input_output_aliases={i: 2 + i for i in range(2 * n)},
        compiler_params=pltpu.CompilerParams(has_side_effects=DATAFLOW),
    )(*[pltpu.with_memory_space_constraint(t, pltpu.HBM) for t in (*shards, *lands)], after)
    return outs[0], outs[1], list(outs[2:2 + n]), list(outs[2 + n:2 + 2 * n]), outs[-1]


def _gather_pass(name, send, recv, shards, lands, after):
    n = len(shards)

    def body(*refs):
        src, land, (send, recv, _) = refs[:n], refs[n:2 * n], refs[2 * n:2 * n + 3]
        fsend, frecv = refs[2 * n + 3], refs[2 * n + 4]
        x, y, c = _place()
        me = 2 * x + y
        for a in range(n):
            mine = _half_rows(shards[a].shape, c)
            for j, (px, py) in enumerate(_other_chips(x, y)):
                far = 2 * px + py
                ici = pltpu.make_async_remote_copy(
                    src_ref=src[a].at[mine], dst_ref=land[a].at[far, mine], send_sem=send.at[3 * a + j],
                    recv_sem=recv.at[3 * a + j], device_id=(px, py, c), device_id_type=MESH_ID)
                ici.wait_recv()
                ici.wait_send()
                pltpu.make_async_remote_copy(
                    src_ref=land[a].at[far, mine], dst_ref=land[a].at[far, mine], send_sem=fsend.at[3 * a + j],
                    recv_sem=frecv.at[3 * a + j], device_id=(x, y, 1 - c), device_id_type=MESH_ID).start()
        token = refs[-1]
        token[...] = jnp.zeros_like(token)

    mem = lambda t: pltpu.HBM(t.shape, t.dtype)
    pair = pltpu.SemaphoreType.DMA((3 * n,))
    outs = pl.pallas_call(
        body, name=name,
        out_shape=(pair, pair, *map(mem, lands), SDS((SUBLANES, LANES), F32)),
        in_specs=[HBM] * (2 * n) + [SEM, SEM, ANY],
        out_specs=(SEM, SEM, *[HBM] * n, pl.BlockSpec(memory_space=pltpu.VMEM)),
        input_output_aliases={n + i: 2 + i for i in range(n)},
        compiler_params=pltpu.CompilerParams(has_side_effects=DATAFLOW),
    )(*shards, *lands, send, recv, after)
    return outs[0], outs[1], list(outs[2:2 + n]), outs[-1]


def _gather_wait(name, fsend, frecv, lands, after):
    n = len(lands)

    def body(*refs):
        land, (fsend, frecv, _) = refs[:n], refs[n:n + 3]
        x, y, c = _place()
        for a in range(n):
            for j, (px, py) in enumerate(_other_chips(x, y)):
                far = 2 * px + py
                mine = _half_rows(lands[a].shape[1:], c)
                theirs = _half_rows(lands[a].shape[1:], c, other=True)
                pltpu.make_async_remote_copy(
                    src_ref=land[a].at[far, mine], dst_ref=land[a].at[far, mine], send_sem=fsend.at[3 * a + j],
                    recv_sem=frecv.at[3 * a + j], device_id=(x, y, 1 - c), device_id_type=MESH_ID).wait_send()
                pltpu.make_async_remote_copy(
                    src_ref=land[a].at[far, theirs], dst_ref=land[a].at[far, theirs], send_sem=fsend.at[3 * a + j],
                    recv_sem=frecv.at[3 * a + j], device_id=(x, y, 1 - c), device_id_type=MESH_ID).wait_recv()

    mem = lambda t: pltpu.HBM(t.shape, t.dtype)
    return list(pl.pallas_call(
        body, name=name, out_shape=tuple(map(mem, lands)), in_specs=[HBM] * n + [SEM, SEM, ANY],
        out_specs=tuple([HBM] * n), input_output_aliases={i: i for i in range(n)},
        compiler_params=pltpu.CompilerParams(has_side_effects=DATAFLOW),
    )(*lands, fsend, frecv, after))


def _after(token):
    return _Exchange([token], [], [], lambda *_: None, lambda *_: None)


def _swap_sibling(arrs):
    n = len(arrs)

    def copies(ins, outs, sems):
        send, recv = sems
        x, y, c = _place()
        return [pltpu.make_async_remote_copy(
            src_ref=ins[a], dst_ref=outs[a], send_sem=send.at[a], recv_sem=recv.at[a],
            device_id=(x, y, 1 - c), device_id_type=MESH_ID) for a in range(n)]

    def start(ins, outs, sems):
        for cp in copies(ins, outs, sems):
            cp.start()

    def wait(ins, outs, sems):
        cps = copies(ins, outs, sems)
        for cp in cps:
            cp.wait_recv()
        for cp in cps:
            cp.wait_send()

    return _Exchange(arrs, [SDS(s.shape, s.dtype) for s in arrs],
                     [pltpu.SemaphoreType.DMA((n,)), pltpu.SemaphoreType.DMA((n,))], start, wait)


def _sum_devices(slots):
    def body(s_ref, o_ref):
        acc = s_ref[0]
        for d in range(1, N_DEV):
            acc = acc + s_ref[d]
        o_ref[...] = acc

    return pl.pallas_call(
        body, in_specs=[pl.BlockSpec(memory_space=pltpu.VMEM)], out_specs=pl.BlockSpec(memory_space=pltpu.VMEM),
        out_shape=SDS(slots.shape[1:], F32), name="sum_small",
        compiler_params=pltpu.CompilerParams(vmem_limit_bytes=32 * 1024 * 1024))(slots)


def _adam_small(ws, gs, ms, vs):
    n = len(ws)

    def body(*refs):
        for i in range(n):
            w_ref, g_ref, m_ref, v_ref = (refs[k * n + i] for k in range(4))
            outs = _adam_math(w_ref[...], g_ref[...], m_ref[...], v_ref[...])
            for k in range(3):
                refs[(4 + k) * n + i][...] = outs[k]

    vmem = pl.BlockSpec(memory_space=pltpu.VMEM)
    return pl.pallas_call(
        body, in_specs=[vmem] * (4 * n), out_specs=[vmem] * (3 * n),
        out_shape=[SDS(w.shape, F32) for w in ws] * 3, name="adam_small",
        compiler_params=pltpu.CompilerParams(vmem_limit_bytes=32 * 1024 * 1024))(*ws, *gs, *ms, *vs)


def _local_step(x, target, small, big, tb, distributed):
    dist = distributed
    me = (2 * lax.axis_index("x") + lax.axis_index("y")) if dist else 0
    tb_ssm = min(tb, 256)
    bucket = jnp.asarray(_bucket_table())
    place_own = lambda t: lax.dynamic_update_index_in_dim(lax.empty((N_CHIPS,) + t.shape, t.dtype), t, me, 0)
    if dist:
        in_legs = _gather_start("gather_in_start", [big["w_in"]], [place_own(big["w_in"])], small["d_skip"])
        names = sorted(small)
        in_token, values = lax.optimization_barrier((in_legs[4], [small[n] for n in names]))
        small = dict(zip(names, values))
    g1, g2, g3, g4 = small["norm_mix_pre"], small["norm_mix_post"], small["norm_mlp_pre"], small["norm_mlp_post"]

    keys_first = lambda t: jnp.swapaxes(t, -1, -2)
    bias = _bias_table(small["rel_bias"], bucket)
    sink_rows = keys_first(_pair_layout(jnp.broadcast_to(small["sinks"].reshape(N_HEADS, 1, 1), (N_HEADS, BLOCK, 1))))
    disc_args = (small["lam_re"], small["lam_im"], small["log_dt"], small["b_re"], small["b_im"])
    (ab_re, ab_im, bb_re, bb_im), disc_vjp = jax.vjp(_ssm_discretize, *disc_args)
    tab_f, tab_b = _scan_tables(ab_re, ab_im)
    bmat = _bf(_b_matrix(bb_re, bb_im))
    cmat = _bf(_c_matrix(small["c_re"], small["c_im"]))
    d_skip = small["d_skip"]

    mix = ("w_glu", "w_attn_branch", "w_ssm_branch", "w_out")
    rest = [big[n] for n in mix + ("w_ff_in", "w_ff_out")]
    if dist:
        send, recv, src, lands, _ = in_legs
        rest_lands = [place_own(t) for t in rest]
        corner = lambda t: t.reshape(-1, t.shape[-1])[:1, :LANES].astype(F32)
        prepared = sum(map(corner, [tab_b, bias, sink_rows, bmat, cmat] + rest_lands), in_token[:1])
        send, recv, lands, in_passed = _gather_pass("gather_in_pass", send, recv, src, lands, prepared)
        (g_in,) = _gather_wait("gather_in_wait", send, recv, lands, in_passed)
        w_in = g_in.reshape(IN_W, D_MODEL)
    else:
        w_in = big["w_in"]
    token = None
    if dist:
        send, recv, rest, lands, token = _gather_start("gather_rest_start", rest, rest_lands, in_passed)
    h1, q, k, v, u, ga, gs = _inproj_fwd(x, g1, w_in, tb, _after(token) if dist else None)
    s, h = _ssm_fwd(u, bmat, cmat, tab_f, d_skip, tb)
    if dist:
        send, recv, lands, token = _gather_pass("gather_rest_pass", send, recv, rest, lands, s)
    att = _attn_fwd(q, k, v, bias, sink_rows, _after(token) if dist else None)[0]
    if dist:
        rest = _gather_wait("gather_rest_wait", send, recv, lands, att)
    w_glu, w_ab, w_sb, w_out, w_ffi, w_ffo = rest
    w_glu = w_glu.reshape(SSM_W, SSM_W)
    w_out = w_out.reshape(D_MODEL, D_MODEL)
    w_ffi = [w_ffi]
    x2 = _merge_fwd(x, s, att, ga, gs, g2, w_glu, w_ab, w_sb, w_out, tb)
    dy, df, h3, loss_acc, dg4 = _mlp_fwd_loss(x2, target, g3, g4, w_ffi, w_ffo, tb)

    dx2, act, da, dg3 = _mlp_bwd(x2, dy, df, h3, g3, w_ffi, w_ffo, tb)
    tl = min(2048, x.shape[0])
    chunked = (N_CHIPS, D_FF // N_CHIPS, D_MODEL)
    d_ffi, b_ffi = _matmul_tn("grad_w_ff_in", h3, da, D_MODEL, D_FF // FF_CHUNKS, tl, True)
    d_ffo, b_ffo = _matmul_tn("grad_w_ff_out", act, df, D_FF // FF_CHUNKS, D_MODEL, tl, False)
    d_ffo, b_ffo = d_ffo.reshape(chunked), b_ffo.reshape(chunked)
    behind = lambda flight: _after(flight.token) if dist else None
    ff_fl = _scatter_off("scatter_ff_off", [b_ffi, b_ffo], d_ffo) if dist else None
    outs = _merge_bwd(dx2, s, att, ga, gs, g2, w_glu, w_ab, w_sb, w_out, tb, behind(ff_fl))
    ds, datt, dga, dgs, dg2, d_glu, d_ab, d_sb, d_out, b_glu, b_ab, b_sb, b_out = outs
    glu4, out4 = (N_CHIPS, SSM_W // N_CHIPS, SSM_W), (N_CHIPS, D_MODEL // N_CHIPS, D_MODEL)
    d_mix = [d_glu.reshape(glu4), d_ab, d_sb, d_out.reshape(out4)]
    b_mix = [b_glu.reshape(glu4), b_ab, b_sb, b_out.reshape(out4)]
    mix_fl = _scatter_off("scatter_mix_off", b_mix, d_mix[-1]) if dist else None
    du, d_bmat, d_cmat, da_acc, dd_skip = _ssm_bwd(
        ds, u, h, bmat.transpose(0, 2, 1), cmat.transpose(0, 2, 1), tab_b, d_skip, tb, behind(mix_fl))
    dq, dk, dv, dbias, dsink_rows = _attn_bwd(q, k, v, datt, bias, sink_rows)
    swap_fl = None
    if dist:
        r_ffi, r_ffo = _land("scatter_ff_land", ff_fl, dq)[1]
        p_ffi = _sum4("sum_w_ff_in", d_ffi, r_ffi, me)
        p_ffo = _sum4("sum_w_ff_out", d_ffo, r_ffo, me)
        swap_fl = _swap_off("swap_ff_off", [p_ffi, p_ffo], r_ffo)
    dx, dpj, dg1 = _inproj_bwd(x, dx2, dq, dk, dv, du, dga, dgs, g1, w_in, tb, behind(swap_fl))

    dab_re, dab_im = _state_unlayout(jnp.sum(da_acc, axis=0))
    dbb_re, dbb_im = _b_matrix_grad(d_bmat)
    d_lam_re, d_lam_im, d_log_dt, d_b_re, d_b_im = disc_vjp((dab_re, dab_im, dbb_re, dbb_im))
    d_c_re, d_c_im = _c_matrix_grad(d_cmat)
    d_rel = _bias_grad(dbias, bucket)
    d_sinks = jnp.sum(_pair_unlayout(keys_first(dsink_rows)), axis=(1, 2))
    small_grads = dict(
        norm_mix_pre=dg1, norm_mix_post=dg2, norm_mlp_pre=dg3, norm_mlp_post=dg4, rel_bias=d_rel, sinks=d_sinks,
        lam_re=d_lam_re, lam_im=d_lam_im, log_dt=d_log_dt, b_re=d_b_re, b_im=d_b_im, c_re=d_c_re, c_im=d_c_im,
        d_skip=dd_skip)
    small_fl = _devices_off("small_off", _pack(small_grads, loss_acc), swap_fl.token) if dist else None
    outs = _matmul_tn("grad_w_in", dpj, h1, IN_W // 2, D_MODEL, tl, False, behind(small_fl))
    in4 = (N_CHIPS, IN_W // N_CHIPS, D_MODEL)
    d_in, b_in = outs[0].reshape(in4), outs[1].reshape(in4)
    if not dist:
        return loss_acc, dx, small_grads, dict(zip(BIG, [d_in] + d_mix + [d_ffi, d_ffo]))
    (p_ffi, p_ffo), (s_ffi, s_ffo) = _land("swap_ff_land", swap_fl, b_in)
    r_mix = _land("scatter_mix_land", mix_fl, b_in)[1]
    p_mix = [_sum4("sum_" + n, d, r, me) for n, d, r in zip(mix, d_mix, r_mix)]
    pending = dict(d_in=d_in, b_in=b_in, p_mix=p_mix, w_ff_in=(p_ffi, s_ffi), w_ff_out=(p_ffo, s_ffo), me=me)
    return loss_acc, dx, small_fl, pending


SMALL = ['norm_mix_pre', 'norm_mix_post', 'norm_mlp_pre', 'norm_mlp_post', 'rel_bias', 'sinks', 'lam_re', 'lam_im',
         'log_dt', 'b_re', 'b_im', 'c_re', 'c_im', 'd_skip']
BIG = ['w_in', 'w_glu', 'w_attn_branch', 'w_ssm_branch', 'w_out', 'w_ff_in', 'w_ff_out']
WEIGHTS = ['norm_mix_pre', 'norm_mix_post', 'norm_mlp_pre', 'norm_mlp_post', 'w_in', 'rel_bias', 'sinks', 'lam_re',
           'lam_im', 'log_dt', 'b_re', 'b_im', 'c_re', 'c_im', 'd_skip', 'w_glu', 'w_attn_branch', 'w_ssm_branch',
           'w_out', 'w_ff_in', 'w_ff_out']
PACK_COLS = 1024
PACK_ORDER = ['b_re', 'b_im', 'c_re', 'c_im', 'lam_re', 'lam_im', 'norm_mix_pre', 'norm_mix_post', 'norm_mlp_pre',
              'norm_mlp_post', 'rel_bias', 'sinks', 'log_dt', 'd_skip']


STATE_MINOR = ('b_re', 'b_im')
PACK_ROWS = 144
LOSS_ROW = 140


def _pack(named, loss_acc):
    parts = []
    for n in PACK_ORDER:
        a = jnp.swapaxes(named[n], -1, -2) if n in STATE_MINOR else named[n]
        flat = a.reshape(-1)
        rows = -(-flat.shape[0] // PACK_COLS)
        parts.append(jnp.pad(flat, (0, rows * PACK_COLS - flat.shape[0])).reshape(rows, PACK_COLS))
    assert sum(p.shape[0] for p in parts) == LOSS_ROW
    parts.append(jnp.pad(loss_acc[0:1], ((0, PACK_ROWS - LOSS_ROW - 1), (0, PACK_COLS - loss_acc.shape[1]))))
    return jnp.concatenate(parts, axis=0)


def _unpack(packed, shapes):
    out, at = {}, 0
    for n in PACK_ORDER:
        shape = shapes[n][:-2] + (shapes[n][-1], shapes[n][-2]) if n in STATE_MINOR else shapes[n]
        size = int(np.prod(shape))
        rows = -(-size // PACK_COLS)
        blk = packed[at:at + rows]
        out[n] = (blk.reshape(-1)[:size] if size % PACK_COLS else blk).reshape(shape)
        at += rows
    return out


def kernel(x, norm_mix_pre, norm_mix_post, norm_mlp_pre, norm_mlp_post, w_in, rel_bias, sinks, lam_re, lam_im, log_dt, b_re, b_im, c_re, c_im, d_skip, w_glu, w_attn_branch, w_ssm_branch, w_out, w_ff_in, w_ff_out, loss_target, m_norm_mix_pre, m_norm_mix_post, m_norm_mlp_pre, m_norm_mlp_post, m_w_in, m_rel_bias, m_sinks, m_lam_re, m_lam_im, m_log_dt, m_b_re, m_b_im, m_c_re, m_c_im, m_d_skip, m_w_glu, m_w_attn_branch, m_w_ssm_branch, m_w_out, m_w_ff_in, m_w_ff_out, v_norm_mix_pre, v_norm_mix_post, v_norm_mlp_pre, v_norm_mlp_post, v_w_in, v_rel_bias, v_sinks, v_lam_re, v_lam_im, v_log_dt, v_b_re, v_b_im, v_c_re, v_c_im, v_d_skip, v_w_glu, v_w_attn_branch, v_w_ssm_branch, v_w_out, v_w_ff_in, v_w_ff_out):
    env = dict(locals())
    w = {n: env[n] for n in WEIGHTS}
    m = {n: env["m_" + n] for n in WEIGHTS}
    v = {n: env["v_" + n] for n in WEIGHTS}
    seq = x.shape[1]
    tb = min(512, seq)

    small = {n: w[n] for n in ('norm_mix_pre', 'norm_mix_post', 'norm_mlp_pre', 'norm_mlp_post', 'rel_bias')}
    small.update({n: w[n][0] for n in ('sinks', 'lam_re', 'lam_im', 'log_dt', 'b_re', 'b_im', 'c_re', 'c_im')})
    small['d_skip'] = w['d_skip']
    shard = lambda t, n: t[n][0].T if n == 'w_in' else t[n][0]
    unshard = lambda a, n: (a.T if n == 'w_in' else a)[None]
    _, dx, small_fl, pending = _local_step(
        x[0], loss_target[0], small, {n: _bf(shard(w, n)) for n in BIG}, tb, True)

    grads, deltas, new_m, new_v = {}, {}, {}, {}

    def adam(n, partials, after=None):
        outs = _adam_pair("adam_" + n, (shard(w, n), *partials, shard(m, n), shard(v, n)), after)
        grads[n], deltas[n], new_m[n], new_v[n] = [unshard(a, n) for a in outs]
        return outs[3]

    mix = ("w_glu", "w_attn_branch", "w_ssm_branch", "w_out")
    in_fl = _scatter_off("scatter_w_in_off", [pending["b_in"]], pending["d_in"])
    sib_mix = _exchange_alone("swap_mix", _swap_sibling(pending["p_mix"]))
    last = None
    for n, partials in [(n, pending[n]) for n in ("w_ff_in", "w_ff_out")] + list(zip(mix, zip(pending["p_mix"], sib_mix))):
        last = adam(n, partials, in_fl.token)

    small_g = _sum_devices(_land("small_land", small_fl, last)[1][0])
    loss = small_g[LOSS_ROW, 0]
    minor = lambda t, n: jnp.swapaxes(t, -1, -2) if n in STATE_MINOR else t
    g_small = _unpack(small_g, {n: w[n].shape for n in SMALL})
    outs = _adam_small([minor(w[n], n) for n in SMALL], [g_small[n] for n in SMALL],
                       [minor(m[n], n) for n in SMALL], [minor(v[n], n) for n in SMALL])
    grads.update({n: minor(g_small[n], n) for n in SMALL})
    for k, dst in enumerate((deltas, new_m, new_v)):
        dst.update({n: minor(a, n) for n, a in zip(SMALL, outs[k * len(SMALL):(k + 1) * len(SMALL)])})

    (r_in,) = _land("scatter_w_in_land", in_fl, outs[0])[1]
    p_in = _sum4("sum_w_in", pending["d_in"], r_in, pending["me"])
    (s_in,) = _exchange_alone("swap_w_in", _swap_sibling([p_in]))
    adam("w_in", (p_in, s_in))

    return (loss, dx[None], *[grads[n] for n in WEIGHTS], *[deltas[n] for n in WEIGHTS],
            *[new_m[n] for n in WEIGHTS], *[new_v[n] for n in WEIGHTS])
```

```python
import functools
import math

import numpy as np
import jax
import jax.numpy as jnp
from jax import lax
from jax.experimental import pallas as pl
from jax.experimental.pallas import tpu as pltpu

F32 = jnp.float32
BF16 = jnp.bfloat16

D_MODEL = 1024
N_HEADS = 8
N_KV = 2
Q_GROUP = 4
HEAD_DIM = 64
ATTN_W = 512
KV_W = 128
BLOCK = 128
N_BUCKETS = 32
MAX_DISTANCE = 128
NEG_INF = -1e30
SSM_W = 512
SSM_GROUP = 16
SSM_GROUPS = 32
SSM_STATE = 64
N_SUPER = 4
GROUPS_PER_SUPER = SSM_GROUPS // N_SUPER
SUPER_IN = GROUPS_PER_SUPER * SSM_GROUP
SUPER_HALF = GROUPS_PER_SUPER * SSM_STATE
SUPER_W = 2 * SUPER_HALF
STATE_COLS = N_SUPER * SUPER_W
D_FF = 4096
FF_CHUNKS = 4
IN_W = 3328
SPLITS = (0, 512, 640, 768, 1280, 2304, 3328)
RMS_EPS = 1e-6
N_CHIPS = 4
N_DEV = 8
SUBLANES = 8
LANES = 128
STATE_TILES = STATE_COLS // LANES
SUPER_TILES = SUPER_W // LANES

ADAM_LR = 0.001
ADAM_B1 = 0.9
ADAM_B2 = 0.999
ADAM_EPS = 1e-08
ADAM_WD = 0.01
ADAM_STEP = 10

VMEM_BIG = 56 * 1024 * 1024
SDS = jax.ShapeDtypeStruct
MESH_ID = pl.DeviceIdType.MESH
ANY = pl.BlockSpec(memory_space=pl.ANY)


def _bf(x):
    return x.astype(BF16)


def _mm(a, b):
    return jnp.dot(a, b, preferred_element_type=F32)


def _mm_nt(a, b):
    return lax.dot_general(a, b, (((1,), (1,)), ((), ())), preferred_element_type=F32)


def _mm_tn(a, b):
    return lax.dot_general(a, b, (((0,), (0,)), ((), ())), preferred_element_type=F32)


def _sig(x):
    return 1.0 / (1.0 + jnp.exp(-x))


def _rms(x, g):
    r = lax.rsqrt(jnp.mean(x * x, axis=-1, keepdims=True) + RMS_EPS)
    xh = x * r
    return xh * g, xh, r


def _rms_bwd(dout, xh, r, g):
    dg = jnp.sum(dout * xh, axis=0, keepdims=True)
    dxh = dout * g
    dx = r * (dxh - xh * jnp.mean(dxh * xh, axis=-1, keepdims=True))
    return dx, dg


_GELU_C = math.sqrt(2.0 / math.pi)


def _gelu_and_grad(x):
    x2 = x * x
    inner = _GELU_C * (x + 0.044715 * (x2 * x))
    t = jnp.tanh(inner)
    y = 0.5 * x * (1.0 + t)
    dy = 0.5 * (1.0 + t) + 0.5 * x * (1.0 - t * t) * (_GELU_C * (1.0 + 3.0 * 0.044715 * x2))
    return y, dy


def _zero_map(nd, *_):
    return (0,) * nd


def _params(n_axes, vmem=None):
    return pltpu.CompilerParams(dimension_semantics=("arbitrary",) * n_axes, vmem_limit_bytes=vmem)


class _Exchange:
    def __init__(self, ins, outs, sems, start, wait):
        self.ins, self.outs, self.sems, self.start, self.wait = list(ins), list(outs), list(sems), start, wait


def _fused_call(name, body, grid, in_specs, out_specs, out_shape, scratch, args, exchange, params):
    n_in, n_out, n_scr = len(in_specs), len(out_specs), len(scratch)
    if exchange is None:
        fn = body
    else:
        ex = exchange
        n_xi, n_xo = len(ex.ins), len(ex.outs)

        def fn(*refs):
            at = 0
            parts = []
            for n in (n_in, n_xi, n_out, n_xo, n_scr, len(ex.sems)):
                parts.append(refs[at:at + n])
                at += n
            ins, x_in, outs, x_out, scr, x_sem = parts
            ids = [pl.program_id(a) for a in range(len(grid))]
            first = functools.reduce(jnp.logical_and, [i == 0 for i in ids])
            last = functools.reduce(jnp.logical_and, [i == g - 1 for i, g in zip(ids, grid)])

            @pl.when(first)
            def _():
                ex.start(x_in, x_out, x_sem)

            body(*ins, *outs, *scr)

            @pl.when(last)
            def _():
                ex.wait(x_in, x_out, x_sem)

        in_specs = list(in_specs) + [ANY] * n_xi
        out_specs = list(out_specs) + [ANY] * n_xo
        out_shape = list(out_shape) + ex.outs
        scratch = list(scratch) + ex.sems
        args = list(args) + ex.ins
    return pl.pallas_call(fn, grid=grid, in_specs=in_specs, out_specs=out_specs, out_shape=out_shape,
                          scratch_shapes=list(scratch), name=name, compiler_params=params)(*args)


def _exchange_alone(name, ex):
    def body(*refs):
        n_xi, n_xo = len(ex.ins), len(ex.outs)
        x_in, x_out, x_sem = refs[:n_xi], refs[n_xi:n_xi + n_xo], refs[n_xi + n_xo:]
        ex.start(x_in, x_out, x_sem)
        ex.wait(x_in, x_out, x_sem)

    return pl.pallas_call(body, in_specs=[ANY] * len(ex.ins), out_specs=[ANY] * len(ex.outs), out_shape=ex.outs,
                          scratch_shapes=ex.sems, name=name)(*ex.ins)


def _rowcall(name, body, seq, tb, rows, consts, row_outs, acc_outs, scratch=(), reverse=False, vmem=None,
             exchange=None):
    nb = seq // tb
    rmap = (lambda i: (nb - 1 - i, 0)) if reverse else (lambda i: (i, 0))
    tmap = lambda i: (0,) + rmap(i)

    def row_spec(width):
        if isinstance(width, tuple):
            return pl.BlockSpec((width[0], tb, width[1]), tmap)
        return pl.BlockSpec((tb, width), rmap)

    def row_shape(width):
        return (width[0], seq, width[1]) if isinstance(width, tuple) else (seq, width)

    in_specs = [row_spec(a.shape[1] if a.ndim == 2 else (a.shape[0], a.shape[2])) for a in rows]
    in_specs += [pl.BlockSpec(a.shape, functools.partial(_zero_map, a.ndim), pipeline_mode=pl.Buffered(1))
                 for a in consts]
    out_specs = [row_spec(c) for c, _ in row_outs] + [ANY] * len(acc_outs)
    out_shape = [SDS(row_shape(c), dt) for c, dt in row_outs] + [SDS(s, dt) for s, dt in acc_outs]
    n_main = len(rows) + len(consts) + len(row_outs)
    n_acc = len(acc_outs)

    def fn(*refs):
        main, acc_hbm, rest = refs[:n_main], refs[n_main:n_main + n_acc], refs[n_main + n_acc:]
        acc_vmem, own = rest[:n_acc], rest[n_acc:]
        body(*main, *acc_vmem, *own)

        @pl.when(pl.program_id(0) == nb - 1)
        def _():
            for src, dst in zip(acc_vmem, acc_hbm):
                pltpu.sync_copy(src, dst)

    buffers = [pltpu.VMEM(s, dt) for s, dt in acc_outs] + list(scratch)
    return _fused_call(name, fn if acc_outs else body, (nb,), in_specs, out_specs, out_shape, buffers,
                       [*rows, *consts], exchange, _params(1, vmem))


def _inproj_fwd(x, g1, w_in, tb, exchange=None):
    seq = x.shape[0]

    def body(x_ref, g_ref, w_ref, h_ref, q_ref, k_ref, v_ref, u_ref, ga_ref, gs_ref):
        h, _, _ = _rms(x_ref[...], g_ref[...])
        hb = _bf(h)
        h_ref[...] = hb
        pj = _mm_nt(hb, w_ref[...])
        q_ref[...] = _bf(pj[:, SPLITS[0]:SPLITS[1]])
        k_ref[...] = _bf(pj[:, SPLITS[1]:SPLITS[2]])
        v_ref[...] = _bf(pj[:, SPLITS[2]:SPLITS[3]])
        u_ref[...] = pj[:, SPLITS[3]:SPLITS[4]]
        ga_ref[...] = pj[:, SPLITS[4]:SPLITS[5]]
        gs_ref[...] = pj[:, SPLITS[5]:SPLITS[6]]

    return _rowcall("inproj_fwd", body, seq, tb, [x], [g1, w_in],
                    [(D_MODEL, BF16), (ATTN_W, BF16), (KV_W, BF16), (KV_W, BF16), (SSM_W, F32),
                     (D_MODEL, F32), (D_MODEL, F32)], [], vmem=VMEM_BIG, exchange=exchange)


def _inproj_bwd(x, dx2, dq, dk, dv, du, dga, dgs, g1, w_in, tb, exchange=None):
    seq = x.shape[0]

    def body(x_ref, dx2_ref, dq_ref, dk_ref, dv_ref, du_ref, dga_ref, dgs_ref, g_ref, w_ref,
             dx_ref, dpj_ref, dg_ref):
        @pl.when(pl.program_id(0) == 0)
        def _():
            dg_ref[...] = jnp.zeros_like(dg_ref)

        dpj = jnp.concatenate([dq_ref[...], dk_ref[...], dv_ref[...], _bf(du_ref[...]),
                               dga_ref[...], dgs_ref[...]], axis=1)
        dpj_ref[...] = dpj
        dh = _mm(dpj, w_ref[...])
        g = g_ref[...]
        _, xh, r = _rms(x_ref[...], g)
        dxn, dg = _rms_bwd(dh, xh, r, g)
        dx_ref[...] = dx2_ref[...] + dxn
        dg_ref[...] += dg

    return _rowcall("inproj_bwd", body, seq, tb, [x, dx2, dq, dk, dv, du, dga, dgs], [g1, w_in],
                    [(D_MODEL, F32), (IN_W, BF16)], [((1, D_MODEL), F32)], vmem=VMEM_BIG, exchange=exchange)


def _bucket_table():
    qi = np.arange(BLOCK)[:, None]
    kj = np.arange(2 * BLOCK)[None, :]
    dist = qi + BLOCK - kj
    max_exact = N_BUCKETS // 2
    d = np.maximum(dist, 0)
    df = np.maximum(d, 1).astype(np.float32)
    large = max_exact + (np.log(df / np.float32(max_exact)) / np.float32(math.log(MAX_DISTANCE / max_exact))
                         * np.float32(N_BUCKETS - max_exact)).astype(np.int32)
    large = np.minimum(large, N_BUCKETS - 1)
    bucket = np.where(d < max_exact, d, large)
    valid = (dist >= 0) & (dist < BLOCK)
    return np.where(valid, bucket, -1).astype(np.int32)


def _bias_table(rel_bias, bucket):
    def body(rb_ref, bk_ref, o_ref):
        bk = bk_ref[...]
        has_prev = lax.broadcasted_iota(jnp.int32, bk.shape, 1) >= BLOCK
        for h in range(N_HEADS):
            kh, j, par = h // Q_GROUP, (h // 2) % 2, h % 2
            acc = jnp.full((BLOCK, 2 * BLOCK), NEG_INF, F32)
            for b in range(N_BUCKETS):
                acc = jnp.where(bk == b, rb_ref[b, h], acc)
            o_ref[0, kh, par, :, j * BLOCK:(j + 1) * BLOCK] = jnp.where(has_prev, acc, NEG_INF).T
            o_ref[1, kh, par, :, j * BLOCK:(j + 1) * BLOCK] = acc.T

    return pl.pallas_call(
        body, out_shape=SDS((2, N_KV, 2, 2 * BLOCK, 2 * BLOCK), F32),
        in_specs=[pl.BlockSpec(memory_space=pltpu.SMEM), pl.BlockSpec(memory_space=pltpu.VMEM)],
        out_specs=pl.BlockSpec(memory_space=pltpu.VMEM), name="bias_table",
    )(rel_bias, bucket)


def _bias_grad(dbias, bucket):
    def body(db_ref, bk_ref, o_ref):
        bk = bk_ref[...]
        for h in range(N_HEADS):
            kh, j, par = h // Q_GROUP, (h // 2) % 2, h % 2
            db = db_ref[kh, par, :, j * BLOCK:(j + 1) * BLOCK].T
            for b in range(N_BUCKETS):
                o_ref[b, h] = jnp.sum(jnp.where(bk == b, db, 0.0))

    return pl.pallas_call(
        body, out_shape=SDS((N_BUCKETS, N_HEADS), F32),
        in_specs=[pl.BlockSpec(memory_space=pltpu.VMEM), pl.BlockSpec(memory_space=pltpu.VMEM)],
        out_specs=pl.BlockSpec(memory_space=pltpu.SMEM), name="bias_grad",
    )(dbias, bucket)


TILE = 2 * HEAD_DIM


def _pair_layout(t):
    lead = t.shape[:-3]
    t = t.reshape(lead + (N_KV, 2, 2) + t.shape[-2:])
    nl = len(lead)
    t = jnp.transpose(t, tuple(range(nl)) + (nl, nl + 2, nl + 1, nl + 3, nl + 4))
    return t.reshape(lead + (N_KV, 2, 2 * BLOCK, t.shape[-1]))


def _pair_unlayout(t):
    t = t.reshape(N_KV, 2, 2, BLOCK, t.shape[-1]).transpose(0, 2, 1, 3, 4)
    return t.reshape(N_HEADS, BLOCK, t.shape[-1])


def _halves(t):
    tf = t.astype(F32)
    low = lax.broadcasted_iota(jnp.int32, tf.shape, 1) < HEAD_DIM
    swapped = pltpu.roll(tf, HEAD_DIM, 1)
    zero = jnp.zeros_like(tf)
    return ((_bf(jnp.where(low, tf, zero)), _bf(jnp.where(low, zero, swapped))),
            (_bf(jnp.where(low, swapped, zero)), _bf(jnp.where(low, zero, tf))))


def _fold_halves(even, odd):
    low = lax.broadcasted_iota(jnp.int32, even.shape, 1) < HEAD_DIM
    comb = jnp.where(low, even, odd)
    return comb + pltpu.roll(comb, HEAD_DIM, 1)


def _tile_rows(ref, kh):
    return jnp.concatenate([ref[:, (2 * kh) * TILE:(2 * kh + 1) * TILE],
                            ref[:, (2 * kh + 1) * TILE:(2 * kh + 2) * TILE]], axis=0)


def _halves_t(t):
    tt = t.astype(F32).T
    top = lax.broadcasted_iota(jnp.int32, tt.shape, 0) < HEAD_DIM
    swapped = jnp.concatenate([tt[HEAD_DIM:], tt[:HEAD_DIM]], axis=0)
    zero = jnp.zeros_like(tt)
    return ((_bf(jnp.where(top, tt, zero)), _bf(jnp.where(top, zero, swapped))),
            (_bf(jnp.where(top, swapped, zero)), _bf(jnp.where(top, zero, tt))))


def _attn_probs(km, qk, bias, sink):
    lg = _mm_nt(km, qk) * (HEAD_DIM ** -0.5) + bias
    m = jnp.maximum(jnp.max(lg, axis=0, keepdims=True), sink)
    p = jnp.exp(lg - m)
    es = jnp.exp(sink - m)
    inv = 1.0 / (jnp.sum(p, axis=0, keepdims=True) + es)
    return p * inv, es * inv


def _attn_fwd(q, k, v, bias, sink_rows, exchange=None):
    seq = q.shape[0]
    nblk = seq // BLOCK

    def body(q_ref, kp_ref, kc_ref, vp_ref, vc_ref, b_ref, s_ref, o_ref):
        which = jnp.minimum(pl.program_id(0), 1)
        kms = _halves(jnp.concatenate([kp_ref[...], kc_ref[...]], axis=0))
        vts = _halves_t(jnp.concatenate([vp_ref[...], vc_ref[...]], axis=0))
        for kh in range(N_KV):
            qk = _tile_rows(q_ref, kh)
            acc = jnp.zeros((TILE, 2 * BLOCK), F32)
            for par in range(2):
                pr, _ = _attn_probs(kms[kh][par], qk, b_ref[which, kh, par], s_ref[kh, par])
                acc = acc + _mm(vts[kh][par], _bf(pr))
            acc = acc.T
            o_ref[:, (2 * kh) * TILE:(2 * kh + 1) * TILE] = _bf(acc[:BLOCK])
            o_ref[:, (2 * kh + 1) * TILE:(2 * kh + 2) * TILE] = _bf(acc[BLOCK:])

    cur = lambda n: (n, 0)
    prev = lambda n: (jnp.maximum(n - 1, 0), 0)
    return _fused_call(
        "attn_fwd", body, (nblk,),
        [pl.BlockSpec((BLOCK, ATTN_W), cur),
         pl.BlockSpec((BLOCK, KV_W), prev), pl.BlockSpec((BLOCK, KV_W), cur),
         pl.BlockSpec((BLOCK, KV_W), prev), pl.BlockSpec((BLOCK, KV_W), cur),
         pl.BlockSpec(bias.shape, functools.partial(_zero_map, bias.ndim)),
         pl.BlockSpec(sink_rows.shape, functools.partial(_zero_map, sink_rows.ndim))],
        [pl.BlockSpec((BLOCK, ATTN_W), cur)], [SDS((seq, ATTN_W), BF16)], [],
        [q, k, k, v, v, bias, sink_rows], exchange, _params(1))


def _attn_bwd(q, k, v, d_out, bias, sink_rows, exchange=None):
    seq = q.shape[0]
    nblk = seq // BLOCK

    def body(q_ref, kp_ref, kc_ref, vp_ref, vc_ref, do_ref, b_ref, s_ref,
             dq_ref, dk_ref, dv_ref, db_ref, ds_ref, ck_ref, cv_ref):
        n = pl.program_id(0)

        @pl.when(n == 0)
        def _():
            db_ref[...] = jnp.zeros_like(db_ref)
            ds_ref[...] = jnp.zeros_like(ds_ref)
            ck_ref[...] = jnp.zeros_like(ck_ref)
            cv_ref[...] = jnp.zeros_like(cv_ref)

        @pl.when(n < nblk)
        def _():
            which = jnp.minimum(n, 1)
            scale = HEAD_DIM ** -0.5
            kcat = jnp.concatenate([kp_ref[...], kc_ref[...]], axis=0)
            kms = _halves(kcat)
            kts = _halves_t(kcat)
            vms = _halves(jnp.concatenate([vp_ref[...], vc_ref[...]], axis=0))
            dks, dvs = [], []
            for kh in range(N_KV):
                qk = _tile_rows(q_ref, kh)
                dok = _tile_rows(do_ref, kh)
                dq = jnp.zeros((TILE, 2 * BLOCK), F32)
                dkp, dvp = [], []
                for par in range(2):
                    pr, ps = _attn_probs(kms[kh][par], qk, b_ref[which, kh, par], s_ref[kh, par])
                    dp = _mm_nt(vms[kh][par], dok)
                    rs = jnp.sum(pr * dp, axis=0, keepdims=True)
                    dlg = pr * (dp - rs)
                    ds_ref[kh, par] += -ps * rs
                    db_ref[kh, par] += dlg
                    dlb = _bf(dlg)
                    dq = dq + _mm(kts[kh][par], dlb)
                    dkp.append(_mm(dlb, qk))
                    dvp.append(_mm(_bf(pr), dok))
                dq = _bf((dq * scale).T)
                dq_ref[:, (2 * kh) * TILE:(2 * kh + 1) * TILE] = dq[:BLOCK]
                dq_ref[:, (2 * kh + 1) * TILE:(2 * kh + 2) * TILE] = dq[BLOCK:]
                dks.append(_fold_halves(*dkp))
                dvs.append(_fold_halves(*dvp))
            low = lax.broadcasted_iota(jnp.int32, (2 * BLOCK, TILE), 1) < HEAD_DIM
            dkk = jnp.where(low, dks[0], dks[1]) * scale
            dvv = jnp.where(low, dvs[0], dvs[1])
            dk_ref[...] = _bf(ck_ref[...] + dkk[:BLOCK])
            ck_ref[...] = dkk[BLOCK:]
            dv_ref[...] = _bf(cv_ref[...] + dvv[:BLOCK])
            cv_ref[...] = dvv[BLOCK:]

        @pl.when(n == nblk)
        def _():
            dk_ref[...] = _bf(ck_ref[...])
            dv_ref[...] = _bf(cv_ref[...])

    cur = lambda n: (jnp.minimum(n, nblk - 1), 0)
    prev = lambda n: (jnp.maximum(jnp.minimum(n, nblk - 1) - 1, 0), 0)
    late = lambda n: (jnp.maximum(n - 1, 0), 0)
    kv_spec = lambda m: pl.BlockSpec((BLOCK, KV_W), m)
    acc_b = pl.BlockSpec(bias.shape[1:], functools.partial(_zero_map, bias.ndim - 1))
    acc_s = pl.BlockSpec(sink_rows.shape, functools.partial(_zero_map, sink_rows.ndim))
    return _fused_call(
        "attn_bwd", body, (nblk + 1,),
        [pl.BlockSpec((BLOCK, ATTN_W), cur), kv_spec(prev), kv_spec(cur), kv_spec(prev), kv_spec(cur),
         pl.BlockSpec((BLOCK, ATTN_W), cur),
         pl.BlockSpec(bias.shape, functools.partial(_zero_map, bias.ndim)), acc_s],
        [pl.BlockSpec((BLOCK, ATTN_W), cur), kv_spec(late), kv_spec(late), acc_b, acc_s],
        [SDS((seq, ATTN_W), BF16), SDS((seq, KV_W), BF16), SDS((seq, KV_W), BF16),
         SDS(bias.shape[1:], F32), SDS(sink_rows.shape, F32)],
        [pltpu.VMEM((BLOCK, KV_W), F32), pltpu.VMEM((BLOCK, KV_W), F32)],
        [q, k, k, v, v, d_out, bias, sink_rows], exchange, _params(1))


def _ssm_discretize(lam_re, lam_im, log_dt, b_re, b_im):
    dt = jnp.exp(log_dt)[:, None]
    mag = jnp.exp(lam_re * dt)
    ab_re = mag * jnp.cos(lam_im * dt)
    ab_im = mag * jnp.sin(lam_im * dt)
    nr = ab_re - 1.0
    den = lam_re * lam_re + lam_im * lam_im
    f_re = (nr * lam_re + ab_im * lam_im) / den
    f_im = (ab_im * lam_re - nr * lam_im) / den
    bb_re = f_re[..., None] * b_re - f_im[..., None] * b_im
    bb_im = f_re[..., None] * b_im + f_im[..., None] * b_re
    return ab_re, ab_im, bb_re, bb_im


def _state_layout(re, im):
    z = jnp.stack([re, im]).reshape(2, N_SUPER, GROUPS_PER_SUPER, SSM_STATE)
    return z.transpose(1, 0, 2, 3).reshape(STATE_COLS)


def _state_unlayout(vec):
    z = vec.reshape(N_SUPER, 2, GROUPS_PER_SUPER, SSM_STATE).transpose(1, 0, 2, 3)
    z = z.reshape(2, SSM_GROUPS, SSM_STATE)
    return z[0], z[1]


SEG = 4
WINDOW = SEG * SUBLANES


def _scan_tables(ab_re, ab_im):
    pw = [None, (ab_re, ab_im)]
    for _ in range(2, WINDOW + 1):
        pr, pi_ = pw[-1]
        pw.append((pr * ab_re - pi_ * ab_im, pr * ab_im + pi_ * ab_re))
    rows = np.arange(SUBLANES)[:, None]
    ones = np.ones((SUBLANES, 1), np.float32)
    conj = lambda p: (p[0], -p[1])
    fwd, bwd = [], []
    for shift in (1, 2, 4):
        fwd.append(_state_layout(*pw[SEG * shift])[None, :] * (rows >= shift).astype(np.float32))
        bwd.append(_state_layout(*conj(pw[SEG * shift]))[None, :] * (rows < SUBLANES - shift).astype(np.float32))
    fwd.append(jnp.stack([_state_layout(*pw[SEG * (r + 1)]) for r in range(SUBLANES)]))
    bwd.append(jnp.stack([_state_layout(*conj(pw[SEG * (SUBLANES - r)])) for r in range(SUBLANES)]))
    for k in range(1, SEG):
        fwd.append(_state_layout(*pw[k])[None, :] * ones)
        bwd.append(_state_layout(*conj(pw[k]))[None, :] * ones)
    return jnp.stack(fwd), jnp.stack(bwd)


_EYE = np.eye(GROUPS_PER_SUPER, dtype=np.float32)


def _b_matrix(bb_re, bb_im):
    bb = jnp.stack([bb_re, bb_im]).reshape(2, N_SUPER, GROUPS_PER_SUPER, SSM_STATE, SSM_GROUP)
    m = jnp.einsum('rsgpc,gh->sgcrhp', bb, _EYE)
    return m.reshape(N_SUPER, SUPER_IN, SUPER_W)


def _b_matrix_grad(dm):
    d = dm.reshape(N_SUPER, GROUPS_PER_SUPER, SSM_GROUP, 2, GROUPS_PER_SUPER, SSM_STATE)
    d = jnp.sum(d * _EYE[None, :, None, None, :, None], axis=4)
    d = d.transpose(3, 0, 1, 4, 2).reshape(2, SSM_GROUPS, SSM_STATE, SSM_GROUP)
    return d[0], d[1]


def _c_matrix(c_re, c_im):
    cc = jnp.stack([c_re, -c_im]).reshape(2, N_SUPER, GROUPS_PER_SUPER, SSM_GROUP, SSM_STATE)
    m = jnp.einsum('rsgcp,gh->srgphc', cc, _EYE)
    return m.reshape(N_SUPER, SUPER_W, SUPER_IN)


def _c_matrix_grad(dm):
    d = dm.reshape(N_SUPER, 2, GROUPS_PER_SUPER, SSM_STATE, GROUPS_PER_SUPER, SSM_GROUP)
    d = jnp.sum(d * _EYE[None, None, :, None, :, None], axis=4)
    d = d.transpose(1, 0, 2, 4, 3).reshape(2, SSM_GROUPS, SSM_GROUP, SSM_STATE)
    return d[0], -d[1]


def _cmul_add(xr, xi, ar, ai, sr, si):
    return xr + ar * sr - ai * si, xi + ar * si + ai * sr


def _scan_rows(buf_ref, tab_ref, carry_ref, n_windows, reverse, h_ref=None, da_ref=None):
    order = list(range(SEG - 1, -1, -1)) if reverse else list(range(SEG))
    near = SUBLANES - 1 if reverse else 0
    far = 0 if reverse else SUBLANES - 1
    s_in = SUBLANES - 1 if reverse else 1
    lanes = lambda tile: pl.ds(tile * LANES, LANES)

    def window(w0, tile_re, tile_im, c_re, c_im, acc):
        rows = lambda t: pl.ds(w0 + t, SUBLANES, stride=SEG)
        get = lambda ref, t: (ref.at[tile_re][rows(t), :], ref.at[tile_im][rows(t), :])
        tab = lambda k: (tab_ref[k, :, lanes(tile_re)], tab_ref[k, :, lanes(tile_im)])

        def put(t, xr, xi):
            buf_ref.at[tile_re][rows(t), :] = xr
            buf_ref.at[tile_im][rows(t), :] = xi

        a1 = tab(4)
        er, ei = get(buf_ref, order[0])
        for t in order[1:]:
            er, ei = _cmul_add(*get(buf_ref, t), *a1, er, ei)
            if t != order[-1]:
                put(t, er, ei)
        for k, shift in enumerate((1, 2, 4)):
            s = (SUBLANES - shift) if reverse else shift
            er, ei = _cmul_add(er, ei, *tab(k), pltpu.roll(er, s, 0), pltpu.roll(ei, s, 0))
        er, ei = _cmul_add(er, ei, *tab(3), c_re, c_im)
        put(order[-1], er, ei)
        sub = lax.broadcasted_iota(jnp.int32, er.shape, 0)
        in_re = jnp.where(sub == near, c_re, pltpu.roll(er, s_in, 0))
        in_im = jnp.where(sub == near, c_im, pltpu.roll(ei, s_in, 0))
        true = {order[-1]: (er, ei)}
        for idx, t in enumerate(order[:-1]):
            true[t] = _cmul_add(*get(buf_ref, t), *tab(4 + idx), in_re, in_im)
            put(t, *true[t])
        carry = (jnp.broadcast_to(er[far:far + 1], er.shape), jnp.broadcast_to(ei[far:far + 1], ei.shape))
        if acc is None:
            return carry, None
        acc_re, acc_im = acc
        for t in range(SEG):
            if t + 1 < SEG:
                gr, gim = true[t + 1]
            else:
                gr = jnp.where(sub == SUBLANES - 1, c_re, pltpu.roll(true[0][0], SUBLANES - 1, 0))
                gim = jnp.where(sub == SUBLANES - 1, c_im, pltpu.roll(true[0][1], SUBLANES - 1, 0))
            hr, hi = get(h_ref, t)
            acc_re = acc_re + gr * hr + gim * hi
            acc_im = acc_im + gim * hr - gr * hi
        return carry, (acc_re, acc_im)

    half = SUPER_HALF // LANES
    per = 2 if h_ref is None else 4
    for sb in range(N_SUPER):
        pairs = [(2 * half * sb + j, 2 * half * sb + half + j) for j in range(half)]

        def step(wi, state, pairs=pairs):
            w = (n_windows - 1 - wi) if reverse else wi
            w0 = pl.multiple_of(w * WINDOW, WINDOW)
            out = []
            for j, (tile_re, tile_im) in enumerate(pairs):
                mine = state[per * j:per * (j + 1)]
                carry, acc = window(w0, tile_re, tile_im, mine[0], mine[1], mine[2:] or None)
                out += list(carry) + list(acc or ())
            return tuple(out)

        init = []
        for tile_re, tile_im in pairs:
            init += [carry_ref[:, lanes(tile_re)], carry_ref[:, lanes(tile_im)]]
            if h_ref is not None:
                init += [da_ref[:, lanes(tile_re)], da_ref[:, lanes(tile_im)]]
        fin = lax.fori_loop(0, n_windows, step, tuple(init))
        for j, (tile_re, tile_im) in enumerate(pairs):
            carry_ref[:, lanes(tile_re)] = fin[per * j]
            carry_ref[:, lanes(tile_im)] = fin[per * j + 1]
            if h_ref is not None:
                da_ref[:, lanes(tile_re)] = fin[per * j + 2]
                da_ref[:, lanes(tile_im)] = fin[per * j + 3]


def _put_tiles(ref, sb, block):
    for j in range(SUPER_TILES):
        ref[sb * SUPER_TILES + j] = block[:, j * LANES:(j + 1) * LANES]


def _get_tiles(ref, sb):
    return jnp.concatenate([ref[sb * SUPER_TILES + j] for j in range(SUPER_TILES)], axis=1)


def _ssm_fwd(u, bmat, cmat, tab, d_skip, tb, exchange=None):
    seq = u.shape[0]

    def body(u_ref, b_ref, c_ref, t_ref, d_ref, s_ref, h_ref, carry_ref):
        @pl.when(pl.program_id(0) == 0)
        def _():
            carry_ref[...] = jnp.zeros_like(carry_ref)

        u_blk = u_ref[...]
        ub = _bf(u_blk)
        for sb in range(N_SUPER):
            _put_tiles(h_ref, sb, _mm(ub[:, sb * SUPER_IN:(sb + 1) * SUPER_IN], b_ref[sb]))
        _scan_rows(h_ref, t_ref, carry_ref, tb // WINDOW, False)
        ys = [_mm(_bf(_get_tiles(h_ref, sb)), c_ref[sb]) for sb in range(N_SUPER)]
        s_ref[...] = jnp.concatenate(ys, axis=1) + d_ref[...] * u_blk

    return _rowcall("ssm_fwd", body, seq, tb, [u], [bmat, cmat, tab, d_skip],
                    [(SSM_W, F32), ((STATE_TILES, LANES), F32)], [],
                    scratch=[pltpu.VMEM((SUBLANES, STATE_COLS), F32)], vmem=VMEM_BIG, exchange=exchange)


def _ssm_bwd(ds, u, h, bmat_t, cmat_t, tab, d_skip, tb, exchange=None):
    seq = u.shape[0]

    def body(ds_ref, u_ref, h_ref, bt_ref, ct_ref, t_ref, d_ref,
             du_ref, db_ref, dc_ref, da_ref, dd_ref, g_ref, carry_ref):
        @pl.when(pl.program_id(0) == 0)
        def _():
            carry_ref[...] = jnp.zeros_like(carry_ref)
            db_ref[...] = jnp.zeros_like(db_ref)
            dc_ref[...] = jnp.zeros_like(dc_ref)
            da_ref[...] = jnp.zeros_like(da_ref)
            dd_ref[...] = jnp.zeros_like(dd_ref)

        ds_blk = ds_ref[...]
        dsb = _bf(ds_blk)
        u_blk = u_ref[...]
        ub = _bf(u_blk)
        for sb in range(N_SUPER):
            _put_tiles(g_ref, sb, _mm(dsb[:, sb * SUPER_IN:(sb + 1) * SUPER_IN], ct_ref[sb]))
        _scan_rows(g_ref, t_ref, carry_ref, tb // WINDOW, True, h_ref=h_ref, da_ref=da_ref)
        dus = []
        for sb in range(N_SUPER):
            gb = _bf(_get_tiles(g_ref, sb))
            dus.append(_mm(gb, bt_ref[sb]))
            db_ref[sb] += _mm_tn(ub[:, sb * SUPER_IN:(sb + 1) * SUPER_IN], gb)
            dc_ref[sb] += _mm_tn(_bf(_get_tiles(h_ref, sb)), dsb[:, sb * SUPER_IN:(sb + 1) * SUPER_IN])
        du_ref[...] = jnp.concatenate(dus, axis=1) + d_ref[...] * ds_blk
        dd_ref[...] += jnp.sum(ds_blk * u_blk, axis=0, keepdims=True)

    return _rowcall("ssm_bwd", body, seq, tb, [ds, u, h], [bmat_t, cmat_t, tab, d_skip],
                    [(SSM_W, F32)],
                    [((N_SUPER, SUPER_IN, SUPER_W), F32), ((N_SUPER, SUPER_W, SUPER_IN), F32),
                     ((SUBLANES, STATE_COLS), F32), ((1, SSM_W), F32)],
                    scratch=[pltpu.VMEM((STATE_TILES, tb, LANES), F32), pltpu.VMEM((SUBLANES, STATE_COLS), F32)],
                    reverse=True, vmem=VMEM_BIG, exchange=exchange)


def _merge_core(s, attb, ga, gs, wg_ref, wab_ref, wsb_ref, wout_ref):
    zg, dgelu = _gelu_and_grad(s)
    zgb = _bf(zg)
    sg = _sig(_mm(zgb, wg_ref[...]))
    z = zg * sg
    zb = _bf(z)
    ys = jnp.concatenate([_mm(zb, wsb_ref[j]) for j in range(N_CHIPS)], axis=1)
    ya = jnp.concatenate([_mm(attb, wab_ref[j]) for j in range(N_CHIPS)], axis=1)
    sa = _sig(ga)
    ss = _sig(gs)
    mgb = _bf(sa * ya + ss * ys)
    o = _mm(mgb, wout_ref[...])
    return dict(zg=zg, dgelu=dgelu, zgb=zgb, sg=sg, zb=zb, ys=ys, ya=ya, sa=sa, ss=ss, mgb=mgb, o=o)


def _merge_fwd(x, s, att, ga, gs, g2, w_glu, w_ab, w_sb, w_out, tb):
    seq = x.shape[0]

    def body(x_ref, s_ref, att_ref, ga_ref, gs_ref, g_ref, wg_ref, wab_ref, wsb_ref, wout_ref, x2_ref):
        f = _merge_core(s_ref[...], att_ref[...], ga_ref[...], gs_ref[...], wg_ref, wab_ref, wsb_ref, wout_ref)
        n, _, _ = _rms(f["o"], g_ref[...])
        x2_ref[...] = x_ref[...] + n

    return _rowcall("merge_fwd", body, seq, tb, [x, s, att, ga, gs], [g2, w_glu, w_ab, w_sb, w_out],
                    [(D_MODEL, F32)], [], vmem=VMEM_BIG)[0]


def _merge_bwd(dx2, s, att, ga, gs, g2, w_glu, w_ab, w_sb, w_out, tb, exchange=None):
    seq = s.shape[0]
    cw = D_MODEL // N_CHIPS
    last = seq // tb - 1

    def body(dx2_ref, s_ref, att_ref, ga_ref, gs_ref, g_ref, wg_ref, wab_ref, wsb_ref, wout_ref,
             ds_ref, datt_ref, dga_ref, dgs_ref, dg_ref, dwg_ref, dwab_ref, dwsb_ref, dwout_ref,
             bwg_ref, bwab_ref, bwsb_ref, bwout_ref):
        @pl.when(pl.program_id(0) == 0)
        def _():
            for r in (dg_ref, dwg_ref, dwab_ref, dwsb_ref, dwout_ref):
                r[...] = jnp.zeros_like(r)

        attb = att_ref[...]
        f = _merge_core(s_ref[...], attb, ga_ref[...], gs_ref[...], wg_ref, wab_ref, wsb_ref, wout_ref)
        g = g_ref[...]
        _, oh, r2 = _rms(f["o"], g)
        do, dg = _rms_bwd(dx2_ref[...], oh, r2, g)
        dg_ref[...] += dg
        dob = _bf(do)
        dwout_ref[...] += _mm_tn(f["mgb"], dob)
        dmg = _mm_nt(dob, wout_ref[...])
        sa, ss = f["sa"], f["ss"]
        dyab = _bf(dmg * sa)
        dysb = _bf(dmg * ss)
        dga_ref[...] = _bf(dmg * f["ya"] * sa * (1.0 - sa))
        dgs_ref[...] = _bf(dmg * f["ys"] * ss * (1.0 - ss))
        dwab = _mm_tn(attb, dyab)
        dwsb = _mm_tn(f["zb"], dysb)
        datt = jnp.zeros((tb, ATTN_W), F32)
        dz = jnp.zeros((tb, SSM_W), F32)
        for j in range(N_CHIPS):
            dwab_ref[j] += dwab[:, j * cw:(j + 1) * cw]
            dwsb_ref[j] += dwsb[:, j * cw:(j + 1) * cw]
            datt = datt + _mm_nt(dyab[:, j * cw:(j + 1) * cw], wab_ref[j])
            dz = dz + _mm_nt(dysb[:, j * cw:(j + 1) * cw], wsb_ref[j])
        datt_ref[...] = _bf(datt)
        sg, zg = f["sg"], f["zg"]
        dglb = _bf(dz * zg * sg * (1.0 - sg))
        dwg_ref[...] += _mm_tn(f["zgb"], dglb)
        dzg = dz * sg + _mm_nt(dglb, wg_ref[...])
        ds_ref[...] = dzg * f["dgelu"]

        @pl.when(pl.program_id(0) == last)
        def _():
            for dst, src in ((bwg_ref, dwg_ref), (bwab_ref, dwab_ref), (bwsb_ref, dwsb_ref), (bwout_ref, dwout_ref)):
                dst[...] = _bf(src[...])

    shapes = [w_glu.shape, w_ab.shape, w_sb.shape, w_out.shape]
    return _rowcall("merge_bwd", body, seq, tb, [dx2, s, att, ga, gs], [g2, w_glu, w_ab, w_sb, w_out],
                    [(SSM_W, F32), (ATTN_W, BF16), (D_MODEL, BF16), (D_MODEL, BF16)],
                    [((1, D_MODEL), F32)] + [(sh, F32) for sh in shapes] + [(sh, BF16) for sh in shapes],
                    vmem=VMEM_BIG, exchange=exchange)


def _mlp_fwd_loss(x2, target, g3, g4, w_ffi, w_ffo, tb):
    seq = x2.shape[0]
    n_slab = len(w_ffi)
    sw = D_FF // FF_CHUNKS // n_slab

    def body(x2_ref, t_ref, g3_ref, g4_ref, *rest):
        wi_refs, (wo_ref, dy_ref, df_ref, h_ref, ra_ref, loss_ref, dg_ref) = rest[:n_slab], rest[n_slab:]

        @pl.when(pl.program_id(0) == 0)
        def _():
            loss_ref[...] = jnp.zeros_like(loss_ref)
            dg_ref[...] = jnp.zeros_like(dg_ref)

        x2_blk = x2_ref[...]
        h3, _, _ = _rms(x2_blk, g3_ref[...])
        hb = _bf(h3)
        h_ref[...] = hb
        f = jnp.zeros((tb, D_MODEL), F32)
        for j in range(FF_CHUNKS):
            for k in range(n_slab):
                ra = jnp.maximum(_mm(hb, wi_refs[k][j]), 0.0)
                ra_ref[:, pl.ds((j * n_slab + k) * sw, sw)] = _bf(ra)
                f = f + _mm(_bf(ra * ra), wo_ref[j, pl.ds(k * sw, sw), :])
        g4 = g4_ref[...]
        n4, fh, r4 = _rms(f, g4)
        e = (x2_blk + n4) - t_ref[...]
        loss_ref[...] += 0.5 * jnp.sum(jnp.mean(e * e, axis=-1, keepdims=True))
        dy = e * (1.0 / D_MODEL)
        dy_ref[...] = dy
        df, dg = _rms_bwd(dy, fh, r4, g4)
        df_ref[...] = _bf(df)
        dg_ref[...] += dg

    return _rowcall("mlp_fwd_loss", body, seq, tb, [x2, target], [g3, g4, *w_ffi, w_ffo],
                    [(D_MODEL, F32), (D_MODEL, BF16), (D_MODEL, BF16), (D_FF, BF16)],
                    [((SUBLANES, 128), F32), ((1, D_MODEL), F32)], vmem=VMEM_BIG)


def _mlp_bwd(x2, dy, df, ra, g3, w_ffi, w_ffo, tb):
    seq = x2.shape[0]
    n_slab = len(w_ffi)
    sw = D_FF // FF_CHUNKS // n_slab

    def body(x2_ref, dy_ref, df_ref, ra_ref, g3_ref, *rest):
        wi_refs, (wo_ref, dx_ref, da_ref, dg_ref) = rest[:n_slab], rest[n_slab:]

        @pl.when(pl.program_id(0) == 0)
        def _():
            dg_ref[...] = jnp.zeros_like(dg_ref)

        dfb = df_ref[...]
        dh = jnp.zeros((tb, D_MODEL), F32)
        for j in range(FF_CHUNKS):
            for k in range(n_slab):
                cols = pl.ds((j * n_slab + k) * sw, sw)
                ra = ra_ref[:, cols].astype(F32)
                dab = _bf(_mm_nt(dfb, wo_ref[j, pl.ds(k * sw, sw), :]) * (2.0 * ra))
                da_ref[:, cols] = dab
                dh = dh + _mm_nt(dab, wi_refs[k][j])
        g3 = g3_ref[...]
        _, xh, r3 = _rms(x2_ref[...], g3)
        dxn, dg = _rms_bwd(dh, xh, r3, g3)
        dx_ref[...] = dy_ref[...] + dxn
        dg_ref[...] += dg

    return _rowcall("mlp_bwd", body, seq, tb, [x2, dy, df, ra], [g3, *w_ffi, w_ffo],
                    [(D_MODEL, F32), (D_FF, BF16)], [((1, D_MODEL), F32)], vmem=VMEM_BIG)


def _matmul_tn(name, a, b, tk, tn, tl, chunk_major, exchange=None, square_a=False):
    seq, kdim = a.shape
    ndim = b.shape[1]
    last = seq // tl - 1

    def body(a_ref, b_ref, o_ref, ob_ref):
        @pl.when(pl.program_id(2) == 0)
        def _():
            o_ref[...] = jnp.zeros_like(o_ref)

        a_blk = a_ref[...]
        if square_a:
            a_blk = _bf(jnp.square(a_blk.astype(F32)))
        o_ref[...] += _mm_tn(a_blk, b_ref[...])

        @pl.when(pl.program_id(2) == last)
        def _():
            ob_ref[...] = _bf(o_ref[...])

    if chunk_major:
        shape = (ndim // tn, kdim, tn)
        out_spec = pl.BlockSpec((None, tk, tn), lambda k, n, l: (n, k, 0))
    else:
        shape = (kdim, ndim)
        out_spec = pl.BlockSpec((tk, tn), lambda k, n, l: (k, n))
    return _fused_call(
        name, body, (kdim // tk, ndim // tn, seq // tl),
        [pl.BlockSpec((tl, tk), lambda k, n, l: (l, k)), pl.BlockSpec((tl, tn), lambda k, n, l: (l, n))],
        [out_spec, out_spec], [SDS(shape, F32), SDS(shape, BF16)], [], [a, b], exchange, _params(3, VMEM_BIG))


def _ew_call(name, fn, ins, n_out, after=None):
    rows, cols = ins[0].shape
    tr = rows
    while tr * cols * 4 > min(1 << 20, (9 << 20) // (len(ins) + n_out)) and tr % 16 == 0:
        tr //= 2
    spec = pl.BlockSpec((tr, cols), lambda i: (i, 0))
    extra = [] if after is None else [after]

    def body(*refs):
        outs = fn(*[r[...] for r in refs[:len(ins)]])
        for r, o in zip(refs[len(ins) + len(extra):], outs):
            r[...] = o

    return pl.pallas_call(
        body, grid=(rows // tr,), in_specs=[spec] * len(ins) + [ANY] * len(extra), out_specs=[spec] * n_out,
        out_shape=[SDS((rows, cols), F32)] * n_out, name=name, compiler_params=_params(1))(*ins, *extra)


def _adam_math(w, g, m, v):
    m2 = ADAM_B1 * m + (1.0 - ADAM_B1) * g
    v2 = ADAM_B2 * v + (1.0 - ADAM_B2) * (g * g)
    m_hat = m2 / (1.0 - ADAM_B1 ** ADAM_STEP)
    v_hat = v2 / (1.0 - ADAM_B2 ** ADAM_STEP)
    delta = -ADAM_LR * (m_hat / (jnp.sqrt(v_hat) + ADAM_EPS) + ADAM_WD * w)
    return delta, m2, v2


def _sum4(name, own, recv, idx):
    _, rows, cols = own.shape
    tr = rows
    while tr * cols * 4 > (1 << 20) and tr % 16 == 0:
        tr //= 2

    def body(idx_ref, o_ref, r0_ref, r1_ref, r2_ref, out_ref):
        out_ref[...] = ((o_ref[...] + r0_ref[...].astype(F32)) + r1_ref[...].astype(F32)) + r2_ref[...].astype(F32)

    blk = (None, tr, cols)
    grid_spec = pltpu.PrefetchScalarGridSpec(
        num_scalar_prefetch=1, grid=(rows // tr,),
        in_specs=[pl.BlockSpec(blk, lambda i, s: (s[0], i, 0)), pl.BlockSpec(blk, lambda i, s: (0, i, 0)),
                  pl.BlockSpec(blk, lambda i, s: (1, i, 0)), pl.BlockSpec(blk, lambda i, s: (2, i, 0))],
        out_specs=pl.BlockSpec((tr, cols), lambda i, s: (i, 0)))
    return pl.pallas_call(body, grid_spec=grid_spec, out_shape=SDS((rows, cols), F32), name=name,
                          compiler_params=_params(1))(jnp.reshape(idx, (1,)).astype(jnp.int32), own, recv, recv, recv)


def _adam_pair(name, item, after=None):
    def fn(w_, a, b, m_, v_):
        g = a + b
        return (g,) + _adam_math(w_, g, m_, v_)

    return _ew_call(name, fn, list(item), 4, after)


def _place():
    return lax.axis_index("x"), lax.axis_index("y"), lax.axis_index("c")


def _other_chips(x, y):
    return [(1 - x, y), (x, 1 - y), (1 - x, 1 - y)]


HBM = pl.BlockSpec(memory_space=pltpu.HBM)
SEM = pl.BlockSpec(memory_space=pltpu.SEMAPHORE)
DATAFLOW = pltpu.SideEffectType.DATAFLOW_SIDE_EFFECTING


class _Flight:
    def __init__(self, copies, n_copies, send, recv, srcs, lands, token):
        self.copies, self.n, self.send, self.recv = copies, n_copies, send, recv
        self.srcs, self.lands, self.token = list(srcs), list(lands), token


def _take_off(name, srcs, lands, copies, n_copies, after):
    n_s, n_l = len(srcs), len(lands)

    def body(*refs):
        src, land = refs[:n_s], refs[n_s:n_s + n_l]
        send, recv = refs[n_s + n_l + 1:n_s + n_l + 3]
        for cp in copies(src, land, send, recv):
            cp.start()
        refs[-1][...] = jnp.zeros_like(refs[-1])

    mem = lambda t: pltpu.HBM(t.shape, t.dtype)
    sems = pltpu.SemaphoreType.DMA((n_copies,))
    outs = pl.pallas_call(
        body, name=name,
        out_shape=(sems, sems, *map(mem, srcs), *map(mem, lands), SDS((SUBLANES, LANES), F32)),
        in_specs=[HBM] * (n_s + n_l) + [ANY],
        out_specs=(SEM, SEM, *[HBM] * (n_s + n_l), pl.BlockSpec(memory_space=pltpu.VMEM)),
        input_output_aliases={i: 2 + i for i in range(n_s + n_l)},
        compiler_params=pltpu.CompilerParams(has_side_effects=DATAFLOW),
    )(*[pltpu.with_memory_space_constraint(t, pltpu.HBM) for t in (*srcs, *lands)], after)
    return _Flight(copies, n_copies, outs[0], outs[1], outs[2:2 + n_s], outs[2 + n_s:2 + n_s + n_l], outs[-1])


def _land(name, flight, after):
    n_s, n_l = len(flight.srcs), len(flight.lands)

    def body(*refs):
        src, land = refs[:n_s], refs[n_s:n_s + n_l]
        send, recv = refs[n_s + n_l:n_s + n_l + 2]
        for cp in flight.copies(src, land, send, recv):
            cp.wait_send()
            cp.wait_recv()

    mem = lambda t: pltpu.HBM(t.shape, t.dtype)
    outs = pl.pallas_call(
        body, name=name, out_shape=(*map(mem, flight.srcs), *map(mem, flight.lands)),
        in_specs=[HBM] * (n_s + n_l) + [SEM, SEM, ANY], out_specs=tuple([HBM] * (n_s + n_l)),
        input_output_aliases={i: i for i in range(n_s + n_l)},
        compiler_params=pltpu.CompilerParams(has_side_effects=DATAFLOW),
    )(*flight.srcs, *flight.lands, flight.send, flight.recv, after)
    return list(outs[:n_s]), list(outs[n_s:])


def _empty_like(shapes_from, lead):
    return [lax.empty((lead,) + t.shape[1:], t.dtype) for t in shapes_from]


def _scatter_off(name, chunks, after):
    def copies(src, land, send, recv):
        x, y, c = _place()
        return [pltpu.make_async_remote_copy(
            src_ref=src[a].at[2 * px + py], dst_ref=land[a].at[k], send_sem=send.at[3 * a + k],
            recv_sem=recv.at[3 * a + k], device_id=(px, py, c), device_id_type=MESH_ID)
            for a in range(len(chunks)) for k, (px, py) in enumerate(_other_chips(x, y))]

    return _take_off(name, chunks, _empty_like(chunks, 3), copies, 3 * len(chunks), after)


def _swap_off(name, arrs, after):
    def copies(src, land, send, recv):
        x, y, c = _place()
        return [pltpu.make_async_remote_copy(
            src_ref=src[a], dst_ref=land[a], send_sem=send.at[a], recv_sem=recv.at[a],
            device_id=(x, y, 1 - c), device_id_type=MESH_ID) for a in range(len(arrs))]

    return _take_off(name, arrs, [lax.empty(t.shape, t.dtype) for t in arrs], copies, len(arrs), after)


def _devices_off(name, block, after):
    me = 4 * lax.axis_index("x") + 2 * lax.axis_index("y") + lax.axis_index("c")
    land = lax.dynamic_update_index_in_dim(lax.empty((N_DEV,) + block.shape, block.dtype), block, me, 0)

    def copies(src, land, send, recv):
        x, y, c = _place()
        mine = 4 * x + 2 * y + c
        return [pltpu.make_async_remote_copy(
            src_ref=src[0], dst_ref=land[0].at[mine], send_sem=send.at[k - 1], recv_sem=recv.at[k - 1],
            device_id=(x ^ (k >> 2), y ^ ((k >> 1) & 1), c ^ (k & 1)), device_id_type=MESH_ID)
            for k in range(1, N_DEV)]

    return _take_off(name, [block], [land], copies, N_DEV - 1, after)


def _half_rows(shape, c, other=False):
    half = shape[0] // 2
    return pl.ds(((1 - c) if other else c) * half, half)


def _gather_start(name, shards, lands, after):
    n = len(shards)

    def body(*refs):
        src, land, (send, recv) = refs[:n], refs[n:2 * n], refs[2 * n + 1:2 * n + 3]
        x, y, c = _place()
        me = 2 * x + y
        for a in range(n):
            mine = _half_rows(shards[a].shape, c)
            for j, (px, py) in enumerate(_other_chips(x, y)):
                pltpu.make_async_remote_copy(
                    src_ref=src[a].at[mine], dst_ref=land[a].at[me, mine], send_sem=send.at[3 * a + j],
                    recv_sem=recv.at[3 * a + j], device_id=(px, py, c), device_id_type=MESH_ID).start()
        token = refs[-1]
        token[...] = jnp.zeros_like(token)

    mem = lambda t: pltpu.HBM(t.shape, t.dtype)
    pair = pltpu.SemaphoreType.DMA((3 * n,))
    outs = pl.pallas_call(
        body, name=name,
        out_shape=(pair, pair, *map(mem, shards), *map(mem, lands), SDS((SUBLANES, LANES), F32)),
        in_specs=[HBM] * (2 * n) + [ANY],
        out_specs=(SEM, SEM, *[HBM] * (2 * n), pl.BlockSpec(memory_space=pltpu.VMEM)),
        input_output_aliases={i: 2 + i for i in range(2 * n)},
        compiler_params=pltpu.CompilerParams(has_side_effects=DATAFLOW),
    )(*[pltpu.with_memory_space_constraint(t, pltpu.HBM) for t in (*shards, *lands)], after)
    return outs[0], outs[1], list(outs[2:2 + n]), list(outs[2 + n:2 + 2 * n]), outs[-1]


def _gather_pass(name, send, recv, shards, lands, after):
    n = len(shards)

    def body(*refs):
        src, land, (send, recv, _) = refs[:n], refs[n:2 * n], refs[2 * n:2 * n + 3]
        fsend, frecv = refs[2 * n + 3], refs[2 * n + 4]
        x, y, c = _place()
        me = 2 * x + y
        for a in range(n):
            mine = _half_rows(shards[a].shape, c)
            for j, (px, py) in enumerate(_other_chips(x, y)):
                far = 2 * px + py
                ici = pltpu.make_async_remote_copy(
                    src_ref=src[a].at[mine], dst_ref=land[a].at[far, mine], send_sem=send.at[3 * a + j],
                    recv_sem=recv.at[3 * a + j], device_id=(px, py, c), device_id_type=MESH_ID)
                ici.wait_recv()
                ici.wait_send()
                pltpu.make_async_remote_copy(
                    src_ref=land[a].at[far, mine], dst_ref=land[a].at[far, mine], send_sem=fsend.at[3 * a + j],
                    recv_sem=frecv.at[3 * a + j], device_id=(x, y, 1 - c), device_id_type=MESH_ID).start()
        token = refs[-1]
        token[...] = jnp.zeros_like(token)

    mem = lambda t: pltpu.HBM(t.shape, t.dtype)
    pair = pltpu.SemaphoreType.DMA((3 * n,))
    outs = pl.pallas_call(
        body, name=name,
        out_shape=(pair, pair, *map(mem, lands), SDS((SUBLANES, LANES), F32)),
        in_specs=[HBM] * (2 * n) + [SEM, SEM, ANY],
        out_specs=(SEM, SEM, *[HBM] * n, pl.BlockSpec(memory_space=pltpu.VMEM)),
        input_output_aliases={n + i: 2 + i for i in range(n)},
        compiler_params=pltpu.CompilerParams(has_side_effects=DATAFLOW),
    )(*shards, *lands, send, recv, after)
    return outs[0], outs[1], list(outs[2:2 + n]), outs[-1]


def _gather_wait(name, fsend, frecv, lands, after):
    n = len(lands)

    def body(*refs):
        land, (fsend, frecv, _) = refs[:n], refs[n:n + 3]
        x, y, c = _place()
        for a in range(n):
            for j, (px, py) in enumerate(_other_chips(x, y)):
                far = 2 * px + py
                mine = _half_rows(lands[a].shape[1:], c)
                theirs = _half_rows(lands[a].shape[1:], c, other=True)
                pltpu.make_async_remote_copy(
                    src_ref=land[a].at[far, mine], dst_ref=land[a].at[far, mine], send_sem=fsend.at[3 * a + j],
                    recv_sem=frecv.at[3 * a + j], device_id=(x, y, 1 - c), device_id_type=MESH_ID).wait_send()
                pltpu.make_async_remote_copy(
                    src_ref=land[a].at[far, theirs], dst_ref=land[a].at[far, theirs], send_sem=fsend.at[3 * a + j],
                    recv_sem=frecv.at[3 * a + j], device_id=(x, y, 1 - c), device_id_type=MESH_ID).wait_recv()

    mem = lambda t: pltpu.HBM(t.shape, t.dtype)
    return list(pl.pallas_call(
        body, name=name, out_shape=tuple(map(mem, lands)), in_specs=[HBM] * n + [SEM, SEM, ANY],
        out_specs=tuple([HBM] * n), input_output_aliases={i: i for i in range(n)},
        compiler_params=pltpu.CompilerParams(has_side_effects=DATAFLOW),
    )(*lands, fsend, frecv, after))


def _after(token):
    return _Exchange([token], [], [], lambda *_: None, lambda *_: None)


def _swap_sibling(arrs):
    n = len(arrs)

    def copies(ins, outs, sems):
        send, recv = sems
        x, y, c = _place()
        return [pltpu.make_async_remote_copy(
            src_ref=ins[a], dst_ref=outs[a], send_sem=send.at[a], recv_sem=recv.at[a],
            device_id=(x, y, 1 - c), device_id_type=MESH_ID) for a in range(n)]

    def start(ins, outs, sems):
        for cp in copies(ins, outs, sems):
            cp.start()

    def wait(ins, outs, sems):
        cps = copies(ins, outs, sems)
        for cp in cps:
            cp.wait_recv()
        for cp in cps:
            cp.wait_send()

    return _Exchange(arrs, [SDS(s.shape, s.dtype) for s in arrs],
                     [pltpu.SemaphoreType.DMA((n,)), pltpu.SemaphoreType.DMA((n,))], start, wait)


def _sum_devices(slots):
    def body(s_ref, o_ref):
        acc = s_ref[0]
        for d in range(1, N_DEV):
            acc = acc + s_ref[d]
        o_ref[...] = acc

    return pl.pallas_call(
        body, in_specs=[pl.BlockSpec(memory_space=pltpu.VMEM)], out_specs=pl.BlockSpec(memory_space=pltpu.VMEM),
        out_shape=SDS(slots.shape[1:], F32), name="sum_small",
        compiler_params=pltpu.CompilerParams(vmem_limit_bytes=32 * 1024 * 1024))(slots)


def _adam_small(ws, gs, ms, vs):
    n = len(ws)

    def body(*refs):
        for i in range(n):
            w_ref, g_ref, m_ref, v_ref = (refs[k * n + i] for k in range(4))
            outs = _adam_math(w_ref[...], g_ref[...], m_ref[...], v_ref[...])
            for k in range(3):
                refs[(4 + k) * n + i][...] = outs[k]

    vmem = pl.BlockSpec(memory_space=pltpu.VMEM)
    return pl.pallas_call(
        body, in_specs=[vmem] * (4 * n), out_specs=[vmem] * (3 * n),
        out_shape=[SDS(w.shape, F32) for w in ws] * 3, name="adam_small",
        compiler_params=pltpu.CompilerParams(vmem_limit_bytes=32 * 1024 * 1024))(*ws, *gs, *ms, *vs)


def _local_step(x, target, small, big, tb, distributed):
    dist = distributed
    me = (2 * lax.axis_index("x") + lax.axis_index("y")) if dist else 0
    tb_ssm = min(tb, 256)
    bucket = jnp.asarray(_bucket_table())
    place_own = lambda t: lax.dynamic_update_index_in_dim(lax.empty((N_CHIPS,) + t.shape, t.dtype), t, me, 0)
    if dist:
        in_legs = _gather_start("gather_in_start", [big["w_in"]], [place_own(big["w_in"])], small["d_skip"])
        names = sorted(small)
        in_token, values = lax.optimization_barrier((in_legs[4], [small[n] for n in names]))
        small = dict(zip(names, values))
    g1, g2, g3, g4 = small["norm_mix_pre"], small["norm_mix_post"], small["norm_mlp_pre"], small["norm_mlp_post"]

    keys_first = lambda t: jnp.swapaxes(t, -1, -2)
    bias = _bias_table(small["rel_bias"], bucket)
    sink_rows = keys_first(_pair_layout(jnp.broadcast_to(small["sinks"].reshape(N_HEADS, 1, 1), (N_HEADS, BLOCK, 1))))
    disc_args = (small["lam_re"], small["lam_im"], small["log_dt"], small["b_re"], small["b_im"])
    (ab_re, ab_im, bb_re, bb_im), disc_vjp = jax.vjp(_ssm_discretize, *disc_args)
    tab_f, tab_b = _scan_tables(ab_re, ab_im)
    bmat = _bf(_b_matrix(bb_re, bb_im))
    cmat = _bf(_c_matrix(small["c_re"], small["c_im"]))
    d_skip = small["d_skip"]

    mix = ("w_glu", "w_attn_branch", "w_ssm_branch", "w_out")
    rest = [big[n] for n in mix + ("w_ff_in", "w_ff_out")]
    if dist:
        send, recv, src, lands, _ = in_legs
        rest_lands = [place_own(t) for t in rest]
        corner = lambda t: t.reshape(-1, t.shape[-1])[:1, :LANES].astype(F32)
        prepared = sum(map(corner, [tab_b, bias, sink_rows, bmat, cmat] + rest_lands), in_token[:1])
        send, recv, lands, in_passed = _gather_pass("gather_in_pass", send, recv, src, lands, prepared)
        (g_in,) = _gather_wait("gather_in_wait", send, recv, lands, in_passed)
        w_in = g_in.reshape(IN_W, D_MODEL)
    else:
        w_in = big["w_in"]
    token = None
    if dist:
        send, recv, rest, lands, token = _gather_start("gather_rest_start", rest, rest_lands, in_passed)
    h1, q, k, v, u, ga, gs = _inproj_fwd(x, g1, w_in, tb, _after(token) if dist else None)
    s, h = _ssm_fwd(u, bmat, cmat, tab_f, d_skip, tb)
    if dist:
        send, recv, lands, token = _gather_pass("gather_rest_pass", send, recv, rest, lands, s)
    att = _attn_fwd(q, k, v, bias, sink_rows, _after(token) if dist else None)[0]
    if dist:
        rest = _gather_wait("gather_rest_wait", send, recv, lands, att)
    w_glu, w_ab, w_sb, w_out, w_ffi, w_ffo = rest
    w_glu = w_glu.reshape(SSM_W, SSM_W)
    w_out = w_out.reshape(D_MODEL, D_MODEL)
    w_ffi = [w_ffi]
    x2 = _merge_fwd(x, s, att, ga, gs, g2, w_glu, w_ab, w_sb, w_out, tb)
    dy, df, h3, ra, loss_acc, dg4 = _mlp_fwd_loss(x2, target, g3, g4, w_ffi, w_ffo, tb)

    dx2, da, dg3 = _mlp_bwd(x2, dy, df, ra, g3, w_ffi, w_ffo, tb)
    tl = min(2048, x.shape[0])
    chunked = (N_CHIPS, D_FF // N_CHIPS, D_MODEL)
    d_ffi, b_ffi = _matmul_tn("grad_w_ff_in", h3, da, D_MODEL, D_FF // FF_CHUNKS, tl, True)
    d_ffo, b_ffo = _matmul_tn("grad_w_ff_out", ra, df, D_FF // FF_CHUNKS, D_MODEL, tl, False, square_a=True)
    d_ffo, b_ffo = d_ffo.reshape(chunked), b_ffo.reshape(chunked)
    behind = lambda flight: _after(flight.token) if dist else None
    ff_fl = _scatter_off("scatter_ff_off", [b_ffi, b_ffo], d_ffo) if dist else None
    outs = _merge_bwd(dx2, s, att, ga, gs, g2, w_glu, w_ab, w_sb, w_out, tb_ssm, behind(ff_fl))
    ds, datt, dga, dgs, dg2, d_glu, d_ab, d_sb, d_out, b_glu, b_ab, b_sb, b_out = outs
    glu4, out4 = (N_CHIPS, SSM_W // N_CHIPS, SSM_W), (N_CHIPS, D_MODEL // N_CHIPS, D_MODEL)
    d_mix = [d_glu.reshape(glu4), d_ab, d_sb, d_out.reshape(out4)]
    b_mix = [b_glu.reshape(glu4), b_ab, b_sb, b_out.reshape(out4)]
    mix_fl = _scatter_off("scatter_mix_off", b_mix, d_mix[-1]) if dist else None
    du, d_bmat, d_cmat, da_acc, dd_skip = _ssm_bwd(
        ds, u, h, bmat.transpose(0, 2, 1), cmat.transpose(0, 2, 1), tab_b, d_skip, tb, behind(mix_fl))
    dq, dk, dv, dbias, dsink_rows = _attn_bwd(q, k, v, datt, bias, sink_rows)
    swap_fl = None
    if dist:
        r_ffi, r_ffo = _land("scatter_ff_land", ff_fl, dq)[1]
        p_ffi = _sum4("sum_w_ff_in", d_ffi, r_ffi, me)
        p_ffo = _sum4("sum_w_ff_out", d_ffo, r_ffo, me)
        swap_fl = _swap_off("swap_ff_off", [p_ffi, p_ffo], r_ffo)
    dx, dpj, dg1 = _inproj_bwd(x, dx2, dq, dk, dv, du, dga, dgs, g1, w_in, tb, behind(swap_fl))

    dab_re, dab_im = _state_unlayout(jnp.sum(da_acc, axis=0))
    dbb_re, dbb_im = _b_matrix_grad(d_bmat)
    d_lam_re, d_lam_im, d_log_dt, d_b_re, d_b_im = disc_vjp((dab_re, dab_im, dbb_re, dbb_im))
    d_c_re, d_c_im = _c_matrix_grad(d_cmat)
    d_rel = _bias_grad(dbias, bucket)
    d_sinks = jnp.sum(_pair_unlayout(keys_first(dsink_rows)), axis=(1, 2))
    small_grads = dict(
        norm_mix_pre=dg1, norm_mix_post=dg2, norm_mlp_pre=dg3, norm_mlp_post=dg4, rel_bias=d_rel, sinks=d_sinks,
        lam_re=d_lam_re, lam_im=d_lam_im, log_dt=d_log_dt, b_re=d_b_re, b_im=d_b_im, c_re=d_c_re, c_im=d_c_im,
        d_skip=dd_skip)
    small_fl = _devices_off("small_off", _pack(small_grads, loss_acc), swap_fl.token) if dist else None
    outs = _matmul_tn("grad_w_in", dpj, h1, IN_W // 2, D_MODEL, tl, False, behind(small_fl))
    in4 = (N_CHIPS, IN_W // N_CHIPS, D_MODEL)
    d_in, b_in = outs[0].reshape(in4), outs[1].reshape(in4)
    if not dist:
        return loss_acc, dx, small_grads, dict(zip(BIG, [d_in] + d_mix + [d_ffi, d_ffo]))
    (p_ffi, p_ffo), (s_ffi, s_ffo) = _land("swap_ff_land", swap_fl, b_in)
    r_mix = _land("scatter_mix_land", mix_fl, b_in)[1]
    p_mix = [_sum4("sum_" + n, d, r, me) for n, d, r in zip(mix, d_mix, r_mix)]
    pending = dict(d_in=d_in, b_in=b_in, p_mix=p_mix, w_ff_in=(p_ffi, s_ffi), w_ff_out=(p_ffo, s_ffo), me=me)
    return loss_acc, dx, small_fl, pending


SMALL = ['norm_mix_pre', 'norm_mix_post', 'norm_mlp_pre', 'norm_mlp_post', 'rel_bias', 'sinks', 'lam_re', 'lam_im',
         'log_dt', 'b_re', 'b_im', 'c_re', 'c_im', 'd_skip']
BIG = ['w_in', 'w_glu', 'w_attn_branch', 'w_ssm_branch', 'w_out', 'w_ff_in', 'w_ff_out']
WEIGHTS = ['norm_mix_pre', 'norm_mix_post', 'norm_mlp_pre', 'norm_mlp_post', 'w_in', 'rel_bias', 'sinks', 'lam_re',
           'lam_im', 'log_dt', 'b_re', 'b_im', 'c_re', 'c_im', 'd_skip', 'w_glu', 'w_attn_branch', 'w_ssm_branch',
           'w_out', 'w_ff_in', 'w_ff_out']
PACK_COLS = 1024
PACK_ORDER = ['b_re', 'b_im', 'c_re', 'c_im', 'lam_re', 'lam_im', 'norm_mix_pre', 'norm_mix_post', 'norm_mlp_pre',
              'norm_mlp_post', 'rel_bias', 'sinks', 'log_dt', 'd_skip']


STATE_MINOR = ('b_re', 'b_im')
PACK_ROWS = 144
LOSS_ROW = 140


def _pack(named, loss_acc):
    parts = []
    for n in PACK_ORDER:
        a = jnp.swapaxes(named[n], -1, -2) if n in STATE_MINOR else named[n]
        flat = a.reshape(-1)
        rows = -(-flat.shape[0] // PACK_COLS)
        parts.append(jnp.pad(flat, (0, rows * PACK_COLS - flat.shape[0])).reshape(rows, PACK_COLS))
    assert sum(p.shape[0] for p in parts) == LOSS_ROW
    parts.append(jnp.pad(loss_acc[0:1], ((0, PACK_ROWS - LOSS_ROW - 1), (0, PACK_COLS - loss_acc.shape[1]))))
    return jnp.concatenate(parts, axis=0)


def _unpack(packed, shapes):
    out, at = {}, 0
    for n in PACK_ORDER:
        shape = shapes[n][:-2] + (shapes[n][-1], shapes[n][-2]) if n in STATE_MINOR else shapes[n]
        size = int(np.prod(shape))
        rows = -(-size // PACK_COLS)
        blk = packed[at:at + rows]
        out[n] = (blk.reshape(-1)[:size] if size % PACK_COLS else blk).reshape(shape)
        at += rows
    return out


def kernel(x, norm_mix_pre, norm_mix_post, norm_mlp_pre, norm_mlp_post, w_in, rel_bias, sinks, lam_re, lam_im, log_dt, b_re, b_im, c_re, c_im, d_skip, w_glu, w_attn_branch, w_ssm_branch, w_out, w_ff_in, w_ff_out, loss_target, m_norm_mix_pre, m_norm_mix_post, m_norm_mlp_pre, m_norm_mlp_post, m_w_in, m_rel_bias, m_sinks, m_lam_re, m_lam_im, m_log_dt, m_b_re, m_b_im, m_c_re, m_c_im, m_d_skip, m_w_glu, m_w_attn_branch, m_w_ssm_branch, m_w_out, m_w_ff_in, m_w_ff_out, v_norm_mix_pre, v_norm_mix_post, v_norm_mlp_pre, v_norm_mlp_post, v_w_in, v_rel_bias, v_sinks, v_lam_re, v_lam_im, v_log_dt, v_b_re, v_b_im, v_c_re, v_c_im, v_d_skip, v_w_glu, v_w_attn_branch, v_w_ssm_branch, v_w_out, v_w_ff_in, v_w_ff_out):
    env = dict(locals())
    w = {n: env[n] for n in WEIGHTS}
    m = {n: env["m_" + n] for n in WEIGHTS}
    v = {n: env["v_" + n] for n in WEIGHTS}
    seq = x.shape[1]
    tb = min(512, seq)

    small = {n: w[n] for n in ('norm_mix_pre', 'norm_mix_post', 'norm_mlp_pre', 'norm_mlp_post', 'rel_bias')}
    small.update({n: w[n][0] for n in ('sinks', 'lam_re', 'lam_im', 'log_dt', 'b_re', 'b_im', 'c_re', 'c_im')})
    small['d_skip'] = w['d_skip']
    shard = lambda t, n: t[n][0].T if n == 'w_in' else t[n][0]
    unshard = lambda a, n: (a.T if n == 'w_in' else a)[None]
    _, dx, small_fl, pending = _local_step(
        x[0], loss_target[0], small, {n: _bf(shard(w, n)) for n in BIG}, tb, True)

    grads, deltas, new_m, new_v = {}, {}, {}, {}

    def adam(n, partials, after=None):
        outs = _adam_pair("adam_" + n, (shard(w, n), *partials, shard(m, n), shard(v, n)), after)
        grads[n], deltas[n], new_m[n], new_v[n] = [unshard(a, n) for a in outs]
        return outs[3]

    mix = ("w_glu", "w_attn_branch", "w_ssm_branch", "w_out")
    in_fl = _scatter_off("scatter_w_in_off", [pending["b_in"]], pending["d_in"])
    sib_mix = _exchange_alone("swap_mix", _swap_sibling(pending["p_mix"]))
    last = None
    for n, partials in [(n, pending[n]) for n in ("w_ff_in", "w_ff_out")] + list(zip(mix, zip(pending["p_mix"], sib_mix))):
        last = adam(n, partials, in_fl.token)

    small_g = _sum_devices(_land("small_land", small_fl, last)[1][0])
    loss = small_g[LOSS_ROW, 0]
    minor = lambda t, n: jnp.swapaxes(t, -1, -2) if n in STATE_MINOR else t
    g_small = _unpack(small_g, {n: w[n].shape for n in SMALL})
    outs = _adam_small([minor(w[n], n) for n in SMALL], [g_small[n] for n in SMALL],
                       [minor(m[n], n) for n in SMALL], [minor(v[n], n) for n in SMALL])
    grads.update({n: minor(g_small[n], n) for n in SMALL})
    for k, dst in enumerate((deltas, new_m, new_v)):
        dst.update({n: minor(a, n) for n, a in zip(SMALL, outs[k * len(SMALL):(k + 1) * len(SMALL)])})

    (r_in,) = _land("scatter_w_in_land", in_fl, outs[0])[1]
    p_in = _sum4("sum_w_in", pending["d_in"], r_in, pending["me"])
    (s_in,) = _exchange_alone("swap_w_in", _swap_sibling([p_in]))
    adam("w_in", (p_in, s_in))

    return (loss, dx[None], *[grads[n] for n in WEIGHTS], *[deltas[n] for n in WEIGHTS],
            *[new_m[n] for n in WEIGHTS], *[new_v[n] for n in WEIGHTS])
```

```python
import functools
import math

import numpy as np
import jax
import jax.numpy as jnp
from jax import lax
from jax.experimental import pallas as pl
from jax.experimental.pallas import tpu as pltpu

F32 = jnp.float32
BF16 = jnp.bfloat16

D_MODEL = 1024
N_HEADS = 8
N_KV = 2
Q_GROUP = 4
HEAD_DIM = 64
ATTN_W = 512
KV_W = 128
BLOCK = 128
N_BUCKETS = 32
MAX_DISTANCE = 128
NEG_INF = -1e30
SSM_W = 512
SSM_GROUP = 16
SSM_GROUPS = 32
SSM_STATE = 64
N_SUPER = 4
GROUPS_PER_SUPER = SSM_GROUPS // N_SUPER
SUPER_IN = GROUPS_PER_SUPER * SSM_GROUP
SUPER_HALF = GROUPS_PER_SUPER * SSM_STATE
SUPER_W = 2 * SUPER_HALF
STATE_COLS = N_SUPER * SUPER_W
D_FF = 4096
FF_CHUNKS = 4
IN_W = 3328
SPLITS = (0, 512, 640, 768, 1280, 2304, 3328)
RMS_EPS = 1e-6
N_CHIPS = 4
N_DEV = 8
SUBLANES = 8
LANES = 128
STATE_TILES = STATE_COLS // LANES
SUPER_TILES = SUPER_W // LANES

ADAM_LR = 0.001
ADAM_B1 = 0.9
ADAM_B2 = 0.999
ADAM_EPS = 1e-08
ADAM_WD = 0.01
ADAM_STEP = 10

VMEM_BIG = 56 * 1024 * 1024
SDS = jax.ShapeDtypeStruct
MESH_ID = pl.DeviceIdType.MESH
ANY = pl.BlockSpec(memory_space=pl.ANY)


def _bf(x):
    return x.astype(BF16)


def _mm(a, b):
    return jnp.dot(a, b, preferred_element_type=F32)


def _mm_nt(a, b):
    return lax.dot_general(a, b, (((1,), (1,)), ((), ())), preferred_element_type=F32)


def _mm_tn(a, b):
    return lax.dot_general(a, b, (((0,), (0,)), ((), ())), preferred_element_type=F32)


def _sig(x):
    return 1.0 / (1.0 + jnp.exp(-x))


def _rms(x, g):
    r = lax.rsqrt(jnp.mean(x * x, axis=-1, keepdims=True) + RMS_EPS)
    xh = x * r
    return xh * g, xh, r


def _rms_bwd(dout, xh, r, g):
    dg = jnp.sum(dout * xh, axis=0, keepdims=True)
    dxh = dout * g
    dx = r * (dxh - xh * jnp.mean(dxh * xh, axis=-1, keepdims=True))
    return dx, dg


_GELU_C = math.sqrt(2.0 / math.pi)


def _gelu_and_grad(x):
    x2 = x * x
    inner = _GELU_C * (x + 0.044715 * (x2 * x))
    t = jnp.tanh(inner)
    y = 0.5 * x * (1.0 + t)
    dy = 0.5 * (1.0 + t) + 0.5 * x * (1.0 - t * t) * (_GELU_C * (1.0 + 3.0 * 0.044715 * x2))
    return y, dy


def _zero_map(nd, *_):
    return (0,) * nd


def _params(n_axes, vmem=None):
    return pltpu.CompilerParams(dimension_semantics=("arbitrary",) * n_axes, vmem_limit_bytes=vmem)


class _Exchange:
    def __init__(self, ins, outs, sems, start, wait):
        self.ins, self.outs, self.sems, self.start, self.wait = list(ins), list(outs), list(sems), start, wait


def _fused_call(name, body, grid, in_specs, out_specs, out_shape, scratch, args, exchange, params):
    n_in, n_out, n_scr = len(in_specs), len(out_specs), len(scratch)
    if exchange is None:
        fn = body
    else:
        ex = exchange
        n_xi, n_xo = len(ex.ins), len(ex.outs)

        def fn(*refs):
            at = 0
            parts = []
            for n in (n_in, n_xi, n_out, n_xo, n_scr, len(ex.sems)):
                parts.append(refs[at:at + n])
                at += n
            ins, x_in, outs, x_out, scr, x_sem = parts
            ids = [pl.program_id(a) for a in range(len(grid))]
            first = functools.reduce(jnp.logical_and, [i == 0 for i in ids])
            last = functools.reduce(jnp.logical_and, [i == g - 1 for i, g in zip(ids, grid)])

            @pl.when(first)
            def _():
                ex.start(x_in, x_out, x_sem)

            body(*ins, *outs, *scr)

            @pl.when(last)
            def _():
                ex.wait(x_in, x_out, x_sem)

        in_specs = list(in_specs) + [ANY] * n_xi
        out_specs = list(out_specs) + [ANY] * n_xo
        out_shape = list(out_shape) + ex.outs
        scratch = list(scratch) + ex.sems
        args = list(args) + ex.ins
    return pl.pallas_call(fn, grid=grid, in_specs=in_specs, out_specs=out_specs, out_shape=out_shape,
                          scratch_shapes=list(scratch), name=name, compiler_params=params)(*args)


def _exchange_alone(name, ex):
    def body(*refs):
        n_xi, n_xo = len(ex.ins), len(ex.outs)
        x_in, x_out, x_sem = refs[:n_xi], refs[n_xi:n_xi + n_xo], refs[n_xi + n_xo:]
        ex.start(x_in, x_out, x_sem)
        ex.wait(x_in, x_out, x_sem)

    return pl.pallas_call(body, in_specs=[ANY] * len(ex.ins), out_specs=[ANY] * len(ex.outs), out_shape=ex.outs,
                          scratch_shapes=ex.sems, name=name)(*ex.ins)


def _rowcall(name, body, seq, tb, rows, consts, row_outs, acc_outs, scratch=(), reverse=False, vmem=None,
             exchange=None):
    nb = seq // tb
    rmap = (lambda i: (nb - 1 - i, 0)) if reverse else (lambda i: (i, 0))
    tmap = lambda i: (0,) + rmap(i)

    def row_spec(width):
        if isinstance(width, tuple):
            return pl.BlockSpec((width[0], tb, width[1]), tmap)
        return pl.BlockSpec((tb, width), rmap)

    def row_shape(width):
        return (width[0], seq, width[1]) if isinstance(width, tuple) else (seq, width)

    in_specs = [row_spec(a.shape[1] if a.ndim == 2 else (a.shape[0], a.shape[2])) for a in rows]
    in_specs += [pl.BlockSpec(a.shape, functools.partial(_zero_map, a.ndim), pipeline_mode=pl.Buffered(1))
                 for a in consts]
    out_specs = [row_spec(c) for c, _ in row_outs] + [ANY] * len(acc_outs)
    out_shape = [SDS(row_shape(c), dt) for c, dt in row_outs] + [SDS(s, dt) for s, dt in acc_outs]
    n_main = len(rows) + len(consts) + len(row_outs)
    n_acc = len(acc_outs)

    def fn(*refs):
        main, acc_hbm, rest = refs[:n_main], refs[n_main:n_main + n_acc], refs[n_main + n_acc:]
        acc_vmem, own = rest[:n_acc], rest[n_acc:]
        body(*main, *acc_vmem, *own)

        @pl.when(pl.program_id(0) == nb - 1)
        def _():
            for src, dst in zip(acc_vmem, acc_hbm):
                pltpu.sync_copy(src, dst)

    buffers = [pltpu.VMEM(s, dt) for s, dt in acc_outs] + list(scratch)
    return _fused_call(name, fn if acc_outs else body, (nb,), in_specs, out_specs, out_shape, buffers,
                       [*rows, *consts], exchange, _params(1, vmem))


def _inproj_fwd(x, g1, w_in, tb, exchange=None):
    seq = x.shape[0]

    def body(x_ref, g_ref, w_ref, h_ref, q_ref, k_ref, v_ref, u_ref, ga_ref, gs_ref):
        h, _, _ = _rms(x_ref[...], g_ref[...])
        hb = _bf(h)
        h_ref[...] = hb
        pj = _mm_nt(hb, w_ref[...])
        q_ref[...] = _bf(pj[:, SPLITS[0]:SPLITS[1]])
        k_ref[...] = _bf(pj[:, SPLITS[1]:SPLITS[2]])
        v_ref[...] = _bf(pj[:, SPLITS[2]:SPLITS[3]])
        u_ref[...] = pj[:, SPLITS[3]:SPLITS[4]]
        ga_ref[...] = pj[:, SPLITS[4]:SPLITS[5]]
        gs_ref[...] = pj[:, SPLITS[5]:SPLITS[6]]

    return _rowcall("inproj_fwd", body, seq, tb, [x], [g1, w_in],
                    [(D_MODEL, BF16), (ATTN_W, BF16), (KV_W, BF16), (KV_W, BF16), (SSM_W, F32),
                     (D_MODEL, F32), (D_MODEL, F32)], [], vmem=VMEM_BIG, exchange=exchange)


def _inproj_bwd(x, dx2, dq, dk, dv, du, dga, dgs, g1, w_in, tb, exchange=None):
    seq = x.shape[0]

    def body(x_ref, dx2_ref, dq_ref, dk_ref, dv_ref, du_ref, dga_ref, dgs_ref, g_ref, w_ref,
             dx_ref, dpj_ref, dg_ref):
        @pl.when(pl.program_id(0) == 0)
        def _():
            dg_ref[...] = jnp.zeros_like(dg_ref)

        dpj = jnp.concatenate([dq_ref[...], dk_ref[...], dv_ref[...], _bf(du_ref[...]),
                               dga_ref[...], dgs_ref[...]], axis=1)
        dpj_ref[...] = dpj
        dh = _mm(dpj, w_ref[...])
        g = g_ref[...]
        _, xh, r = _rms(x_ref[...], g)
        dxn, dg = _rms_bwd(dh, xh, r, g)
        dx_ref[...] = dx2_ref[...] + dxn
        dg_ref[...] += dg

    return _rowcall("inproj_bwd", body, seq, tb, [x, dx2, dq, dk, dv, du, dga, dgs], [g1, w_in],
                    [(D_MODEL, F32), (IN_W, BF16)], [((1, D_MODEL), F32)], vmem=VMEM_BIG, exchange=exchange)


def _bucket_table():
    qi = np.arange(BLOCK)[:, None]
    kj = np.arange(2 * BLOCK)[None, :]
    dist = qi + BLOCK - kj
    max_exact = N_BUCKETS // 2
    d = np.maximum(dist, 0)
    df = np.maximum(d, 1).astype(np.float32)
    large = max_exact + (np.log(df / np.float32(max_exact)) / np.float32(math.log(MAX_DISTANCE / max_exact))
                         * np.float32(N_BUCKETS - max_exact)).astype(np.int32)
    large = np.minimum(large, N_BUCKETS - 1)
    bucket = np.where(d < max_exact, d, large)
    valid = (dist >= 0) & (dist < BLOCK)
    return np.where(valid, bucket, -1).astype(np.int32)


def _bias_table(rel_bias, bucket):
    def body(rb_ref, bk_ref, o_ref):
        bk = bk_ref[...]
        has_prev = lax.broadcasted_iota(jnp.int32, bk.shape, 1) >= BLOCK
        for h in range(N_HEADS):
            kh, j, par = h // Q_GROUP, (h // 2) % 2, h % 2
            acc = jnp.full((BLOCK, 2 * BLOCK), NEG_INF, F32)
            for b in range(N_BUCKETS):
                acc = jnp.where(bk == b, rb_ref[b, h], acc)
            o_ref[0, kh, par, :, j * BLOCK:(j + 1) * BLOCK] = jnp.where(has_prev, acc, NEG_INF).T
            o_ref[1, kh, par, :, j * BLOCK:(j + 1) * BLOCK] = acc.T

    return pl.pallas_call(
        body, out_shape=SDS((2, N_KV, 2, 2 * BLOCK, 2 * BLOCK), F32),
        in_specs=[pl.BlockSpec(memory_space=pltpu.SMEM), pl.BlockSpec(memory_space=pltpu.VMEM)],
        out_specs=pl.BlockSpec(memory_space=pltpu.VMEM), name="bias_table",
    )(rel_bias, bucket)


def _bias_grad(dbias, bucket):
    def body(db_ref, bk_ref, o_ref):
        bk = bk_ref[...]
        for h in range(N_HEADS):
            kh, j, par = h // Q_GROUP, (h // 2) % 2, h % 2
            db = db_ref[kh, par, :, j * BLOCK:(j + 1) * BLOCK].T
            for b in range(N_BUCKETS):
                o_ref[b, h] = jnp.sum(jnp.where(bk == b, db, 0.0))

    return pl.pallas_call(
        body, out_shape=SDS((N_BUCKETS, N_HEADS), F32),
        in_specs=[pl.BlockSpec(memory_space=pltpu.VMEM), pl.BlockSpec(memory_space=pltpu.VMEM)],
        out_specs=pl.BlockSpec(memory_space=pltpu.SMEM), name="bias_grad",
    )(dbias, bucket)


TILE = 2 * HEAD_DIM


def _pair_layout(t):
    lead = t.shape[:-3]
    t = t.reshape(lead + (N_KV, 2, 2) + t.shape[-2:])
    nl = len(lead)
    t = jnp.transpose(t, tuple(range(nl)) + (nl, nl + 2, nl + 1, nl + 3, nl + 4))
    return t.reshape(lead + (N_KV, 2, 2 * BLOCK, t.shape[-1]))


def _pair_unlayout(t):
    t = t.reshape(N_KV, 2, 2, BLOCK, t.shape[-1]).transpose(0, 2, 1, 3, 4)
    return t.reshape(N_HEADS, BLOCK, t.shape[-1])


def _halves(t):
    tf = t.astype(F32)
    low = lax.broadcasted_iota(jnp.int32, tf.shape, 1) < HEAD_DIM
    swapped = pltpu.roll(tf, HEAD_DIM, 1)
    zero = jnp.zeros_like(tf)
    return ((_bf(jnp.where(low, tf, zero)), _bf(jnp.where(low, zero, swapped))),
            (_bf(jnp.where(low, swapped, zero)), _bf(jnp.where(low, zero, tf))))


def _fold_halves(even, odd):
    low = lax.broadcasted_iota(jnp.int32, even.shape, 1) < HEAD_DIM
    comb = jnp.where(low, even, odd)
    return comb + pltpu.roll(comb, HEAD_DIM, 1)


def _tile_rows(ref, kh):
    return jnp.concatenate([ref[:, (2 * kh) * TILE:(2 * kh + 1) * TILE],
                            ref[:, (2 * kh + 1) * TILE:(2 * kh + 2) * TILE]], axis=0)


def _halves_t(t):
    tt = t.astype(F32).T
    top = lax.broadcasted_iota(jnp.int32, tt.shape, 0) < HEAD_DIM
    swapped = jnp.concatenate([tt[HEAD_DIM:], tt[:HEAD_DIM]], axis=0)
    zero = jnp.zeros_like(tt)
    return ((_bf(jnp.where(top, tt, zero)), _bf(jnp.where(top, zero, swapped))),
            (_bf(jnp.where(top, swapped, zero)), _bf(jnp.where(top, zero, tt))))


def _attn_probs(km, qk, bias, sink):
    lg = _mm_nt(km, qk) * (HEAD_DIM ** -0.5) + bias
    m = jnp.maximum(jnp.max(lg, axis=0, keepdims=True), sink)
    p = jnp.exp(lg - m)
    es = jnp.exp(sink - m)
    inv = 1.0 / (jnp.sum(p, axis=0, keepdims=True) + es)
    return p * inv, es * inv


def _attn_fwd(q, k, v, bias, sink_rows, exchange=None):
    seq = q.shape[0]
    nblk = seq // BLOCK

    def body(q_ref, kp_ref, kc_ref, vp_ref, vc_ref, b_ref, s_ref, o_ref):
        which = jnp.minimum(pl.program_id(0), 1)
        kms = _halves(jnp.concatenate([kp_ref[...], kc_ref[...]], axis=0))
        vts = _halves_t(jnp.concatenate([vp_ref[...], vc_ref[...]], axis=0))
        for kh in range(N_KV):
            qk = _tile_rows(q_ref, kh)
            acc = jnp.zeros((TILE, 2 * BLOCK), F32)
            for par in range(2):
                pr, _ = _attn_probs(kms[kh][par], qk, b_ref[which, kh, par], s_ref[kh, par])
                acc = acc + _mm(vts[kh][par], _bf(pr))
            acc = acc.T
            o_ref[:, (2 * kh) * TILE:(2 * kh + 1) * TILE] = _bf(acc[:BLOCK])
            o_ref[:, (2 * kh + 1) * TILE:(2 * kh + 2) * TILE] = _bf(acc[BLOCK:])

    cur = lambda n: (n, 0)
    prev = lambda n: (jnp.maximum(n - 1, 0), 0)
    return _fused_call(
        "attn_fwd", body, (nblk,),
        [pl.BlockSpec((BLOCK, ATTN_W), cur),
         pl.BlockSpec((BLOCK, KV_W), prev), pl.BlockSpec((BLOCK, KV_W), cur),
         pl.BlockSpec((BLOCK, KV_W), prev), pl.BlockSpec((BLOCK, KV_W), cur),
         pl.BlockSpec(bias.shape, functools.partial(_zero_map, bias.ndim)),
         pl.BlockSpec(sink_rows.shape, functools.partial(_zero_map, sink_rows.ndim))],
        [pl.BlockSpec((BLOCK, ATTN_W), cur)], [SDS((seq, ATTN_W), BF16)], [],
        [q, k, k, v, v, bias, sink_rows], exchange, _params(1))


def _attn_bwd(q, k, v, d_out, bias, sink_rows, exchange=None):
    seq = q.shape[0]
    nblk = seq // BLOCK

    def body(q_ref, kp_ref, kc_ref, vp_ref, vc_ref, do_ref, b_ref, s_ref,
             dq_ref, dk_ref, dv_ref, db_ref, ds_ref, ck_ref, cv_ref):
        n = pl.program_id(0)

        @pl.when(n == 0)
        def _():
            db_ref[...] = jnp.zeros_like(db_ref)
            ds_ref[...] = jnp.zeros_like(ds_ref)
            ck_ref[...] = jnp.zeros_like(ck_ref)
            cv_ref[...] = jnp.zeros_like(cv_ref)

        @pl.when(n < nblk)
        def _():
            which = jnp.minimum(n, 1)
            scale = HEAD_DIM ** -0.5
            kcat = jnp.concatenate([kp_ref[...], kc_ref[...]], axis=0)
            kms = _halves(kcat)
            kts = _halves_t(kcat)
            vms = _halves(jnp.concatenate([vp_ref[...], vc_ref[...]], axis=0))
            dks, dvs = [], []
            for kh in range(N_KV):
                qk = _tile_rows(q_ref, kh)
                dok = _tile_rows(do_ref, kh)
                dq = jnp.zeros((TILE, 2 * BLOCK), F32)
                dkp, dvp = [], []
                for par in range(2):
                    pr, ps = _attn_probs(kms[kh][par], qk, b_ref[which, kh, par], s_ref[kh, par])
                    dp = _mm_nt(vms[kh][par], dok)
                    rs = jnp.sum(pr * dp, axis=0, keepdims=True)
                    dlg = pr * (dp - rs)
                    ds_ref[kh, par] += -ps * rs
                    db_ref[kh, par] += dlg
                    dlb = _bf(dlg)
                    dq = dq + _mm(kts[kh][par], dlb)
                    dkp.append(_mm(dlb, qk))
                    dvp.append(_mm(_bf(pr), dok))
                dq = _bf((dq * scale).T)
                dq_ref[:, (2 * kh) * TILE:(2 * kh + 1) * TILE] = dq[:BLOCK]
                dq_ref[:, (2 * kh + 1) * TILE:(2 * kh + 2) * TILE] = dq[BLOCK:]
                dks.append(_fold_halves(*dkp))
                dvs.append(_fold_halves(*dvp))
            low = lax.broadcasted_iota(jnp.int32, (2 * BLOCK, TILE), 1) < HEAD_DIM
            dkk = jnp.where(low, dks[0], dks[1]) * scale
            dvv = jnp.where(low, dvs[0], dvs[1])
            dk_ref[...] = _bf(ck_ref[...] + dkk[:BLOCK])
            ck_ref[...] = dkk[BLOCK:]
            dv_ref[...] = _bf(cv_ref[...] + dvv[:BLOCK])
            cv_ref[...] = dvv[BLOCK:]

        @pl.when(n == nblk)
        def _():
            dk_ref[...] = _bf(ck_ref[...])
            dv_ref[...] = _bf(cv_ref[...])

    cur = lambda n: (jnp.minimum(n, nblk - 1), 0)
    prev = lambda n: (jnp.maximum(jnp.minimum(n, nblk - 1) - 1, 0), 0)
    late = lambda n: (jnp.maximum(n - 1, 0), 0)
    kv_spec = lambda m: pl.BlockSpec((BLOCK, KV_W), m)
    acc_b = pl.BlockSpec(bias.shape[1:], functools.partial(_zero_map, bias.ndim - 1))
    acc_s = pl.BlockSpec(sink_rows.shape, functools.partial(_zero_map, sink_rows.ndim))
    return _fused_call(
        "attn_bwd", body, (nblk + 1,),
        [pl.BlockSpec((BLOCK, ATTN_W), cur), kv_spec(prev), kv_spec(cur), kv_spec(prev), kv_spec(cur),
         pl.BlockSpec((BLOCK, ATTN_W), cur),
         pl.BlockSpec(bias.shape, functools.partial(_zero_map, bias.ndim)), acc_s],
        [pl.BlockSpec((BLOCK, ATTN_W), cur), kv_spec(late), kv_spec(late), acc_b, acc_s],
        [SDS((seq, ATTN_W), BF16), SDS((seq, KV_W), BF16), SDS((seq, KV_W), BF16),
         SDS(bias.shape[1:], F32), SDS(sink_rows.shape, F32)],
        [pltpu.VMEM((BLOCK, KV_W), F32), pltpu.VMEM((BLOCK, KV_W), F32)],
        [q, k, k, v, v, d_out, bias, sink_rows], exchange, _params(1))


def _ssm_discretize(lam_re, lam_im, log_dt, b_re, b_im):
    dt = jnp.exp(log_dt)[:, None]
    mag = jnp.exp(lam_re * dt)
    ab_re = mag * jnp.cos(lam_im * dt)
    ab_im = mag * jnp.sin(lam_im * dt)
    nr = ab_re - 1.0
    den = lam_re * lam_re + lam_im * lam_im
    f_re = (nr * lam_re + ab_im * lam_im) / den
    f_im = (ab_im * lam_re - nr * lam_im) / den
    bb_re = f_re[..., None] * b_re - f_im[..., None] * b_im
    bb_im = f_re[..., None] * b_im + f_im[..., None] * b_re
    return ab_re, ab_im, bb_re, bb_im


def _state_layout(re, im):
    z = jnp.stack([re, im]).reshape(2, N_SUPER, GROUPS_PER_SUPER, SSM_STATE)
    return z.transpose(1, 0, 2, 3).reshape(STATE_COLS)


def _state_unlayout(vec):
    z = vec.reshape(N_SUPER, 2, GROUPS_PER_SUPER, SSM_STATE).transpose(1, 0, 2, 3)
    z = z.reshape(2, SSM_GROUPS, SSM_STATE)
    return z[0], z[1]


SEG = 4
WINDOW = SEG * SUBLANES


def _scan_tables(ab_re, ab_im):
    pw = [None, (ab_re, ab_im)]
    for _ in range(2, WINDOW + 1):
        pr, pi_ = pw[-1]
        pw.append((pr * ab_re - pi_ * ab_im, pr * ab_im + pi_ * ab_re))
    rows = np.arange(SUBLANES)[:, None]
    ones = np.ones((SUBLANES, 1), np.float32)
    conj = lambda p: (p[0], -p[1])
    fwd, bwd = [], []
    for shift in (1, 2, 4):
        fwd.append(_state_layout(*pw[SEG * shift])[None, :] * (rows >= shift).astype(np.float32))
        bwd.append(_state_layout(*conj(pw[SEG * shift]))[None, :] * (rows < SUBLANES - shift).astype(np.float32))
    fwd.append(jnp.stack([_state_layout(*pw[SEG * (r + 1)]) for r in range(SUBLANES)]))
    bwd.append(jnp.stack([_state_layout(*conj(pw[SEG * (SUBLANES - r)])) for r in range(SUBLANES)]))
    for k in range(1, SEG):
        fwd.append(_state_layout(*pw[k])[None, :] * ones)
        bwd.append(_state_layout(*conj(pw[k]))[None, :] * ones)
    return jnp.stack(fwd), jnp.stack(bwd)


_EYE = np.eye(GROUPS_PER_SUPER, dtype=np.float32)


def _b_matrix(bb_re, bb_im):
    bb = jnp.stack([bb_re, bb_im]).reshape(2, N_SUPER, GROUPS_PER_SUPER, SSM_STATE, SSM_GROUP)
    m = jnp.einsum('rsgpc,gh->sgcrhp', bb, _EYE)
    return m.reshape(N_SUPER, SUPER_IN, SUPER_W)


def _b_matrix_grad(dm):
    d = dm.reshape(N_SUPER, GROUPS_PER_SUPER, SSM_GROUP, 2, GROUPS_PER_SUPER, SSM_STATE)
    d = jnp.sum(d * _EYE[None, :, None, None, :, None], axis=4)
    d = d.transpose(3, 0, 1, 4, 2).reshape(2, SSM_GROUPS, SSM_STATE, SSM_GROUP)
    return d[0], d[1]


def _c_matrix(c_re, c_im):
    cc = jnp.stack([c_re, -c_im]).reshape(2, N_SUPER, GROUPS_PER_SUPER, SSM_GROUP, SSM_STATE)
    m = jnp.einsum('rsgcp,gh->srgphc', cc, _EYE)
    return m.reshape(N_SUPER, SUPER_W, SUPER_IN)


def _c_matrix_grad(dm):
    d = dm.reshape(N_SUPER, 2, GROUPS_PER_SUPER, SSM_STATE, GROUPS_PER_SUPER, SSM_GROUP)
    d = jnp.sum(d * _EYE[None, None, :, None, :, None], axis=4)
    d = d.transpose(1, 0, 2, 4, 3).reshape(2, SSM_GROUPS, SSM_GROUP, SSM_STATE)
    return d[0], -d[1]


def _cmul_add(xr, xi, ar, ai, sr, si):
    return xr + ar * sr - ai * si, xi + ar * si + ai * sr


def _scan_rows(buf_ref, tab_ref, carry_ref, n_windows, reverse, h_ref=None, da_ref=None):
    order = list(range(SEG - 1, -1, -1)) if reverse else list(range(SEG))
    near = SUBLANES - 1 if reverse else 0
    far = 0 if reverse else SUBLANES - 1
    s_in = SUBLANES - 1 if reverse else 1
    lanes = lambda tile: pl.ds(tile * LANES, LANES)

    def window(w0, tile_re, tile_im, c_re, c_im, acc):
        rows = lambda t: pl.ds(w0 + t, SUBLANES, stride=SEG)
        get = lambda ref, t: (ref.at[tile_re][rows(t), :], ref.at[tile_im][rows(t), :])
        tab = lambda k: (tab_ref[k, :, lanes(tile_re)], tab_ref[k, :, lanes(tile_im)])

        def put(t, xr, xi):
            buf_ref.at[tile_re][rows(t), :] = xr
            buf_ref.at[tile_im][rows(t), :] = xi

        a1 = tab(4)
        er, ei = get(buf_ref, order[0])
        for t in order[1:]:
            er, ei = _cmul_add(*get(buf_ref, t), *a1, er, ei)
            if t != order[-1]:
                put(t, er, ei)
        for k, shift in enumerate((1, 2, 4)):
            s = (SUBLANES - shift) if reverse else shift
            er, ei = _cmul_add(er, ei, *tab(k), pltpu.roll(er, s, 0), pltpu.roll(ei, s, 0))
        er, ei = _cmul_add(er, ei, *tab(3), c_re, c_im)
        put(order[-1], er, ei)
        sub = lax.broadcasted_iota(jnp.int32, er.shape, 0)
        in_re = jnp.where(sub == near, c_re, pltpu.roll(er, s_in, 0))
        in_im = jnp.where(sub == near, c_im, pltpu.roll(ei, s_in, 0))
        true = {order[-1]: (er, ei)}
        for idx, t in enumerate(order[:-1]):
            true[t] = _cmul_add(*get(buf_ref, t), *tab(4 + idx), in_re, in_im)
            put(t, *true[t])
        carry = (jnp.broadcast_to(er[far:far + 1], er.shape), jnp.broadcast_to(ei[far:far + 1], ei.shape))
        if acc is None:
            return carry, None
        acc_re, acc_im = acc
        for t in range(SEG):
            if t + 1 < SEG:
                gr, gim = true[t + 1]
            else:
                gr = jnp.where(sub == SUBLANES - 1, c_re, pltpu.roll(true[0][0], SUBLANES - 1, 0))
                gim = jnp.where(sub == SUBLANES - 1, c_im, pltpu.roll(true[0][1], SUBLANES - 1, 0))
            hr, hi = get(h_ref, t)
            acc_re = acc_re + gr * hr + gim * hi
            acc_im = acc_im + gim * hr - gr * hi
        return carry, (acc_re, acc_im)

    half = SUPER_HALF // LANES
    per = 2 if h_ref is None else 4
    for sb in range(N_SUPER):
        pairs = [(2 * half * sb + j, 2 * half * sb + half + j) for j in range(half)]

        def step(wi, state, pairs=pairs):
            w = (n_windows - 1 - wi) if reverse else wi
            w0 = pl.multiple_of(w * WINDOW, WINDOW)
            out = []
            for j, (tile_re, tile_im) in enumerate(pairs):
                mine = state[per * j:per * (j + 1)]
                carry, acc = window(w0, tile_re, tile_im, mine[0], mine[1], mine[2:] or None)
                out += list(carry) + list(acc or ())
            return tuple(out)

        init = []
        for tile_re, tile_im in pairs:
            init += [carry_ref[:, lanes(tile_re)], carry_ref[:, lanes(tile_im)]]
            if h_ref is not None:
                init += [da_ref[:, lanes(tile_re)], da_ref[:, lanes(tile_im)]]
        fin = lax.fori_loop(0, n_windows, step, tuple(init))
        for j, (tile_re, tile_im) in enumerate(pairs):
            carry_ref[:, lanes(tile_re)] = fin[per * j]
            carry_ref[:, lanes(tile_im)] = fin[per * j + 1]
            if h_ref is not None:
                da_ref[:, lanes(tile_re)] = fin[per * j + 2]
                da_ref[:, lanes(tile_im)] = fin[per * j + 3]


def _put_tiles(ref, sb, block):
    for j in range(SUPER_TILES):
        ref[sb * SUPER_TILES + j] = block[:, j * LANES:(j + 1) * LANES]


def _get_tiles(ref, sb):
    return jnp.concatenate([ref[sb * SUPER_TILES + j] for j in range(SUPER_TILES)], axis=1)


def _ssm_fwd(u, bmat, cmat, tab, d_skip, tb, exchange=None):
    seq = u.shape[0]

    def body(u_ref, b_ref, c_ref, t_ref, d_ref, s_ref, h_ref, carry_ref):
        @pl.when(pl.program_id(0) == 0)
        def _():
            carry_ref[...] = jnp.zeros_like(carry_ref)

        u_blk = u_ref[...]
        ub = _bf(u_blk)
        for sb in range(N_SUPER):
            _put_tiles(h_ref, sb, _mm(ub[:, sb * SUPER_IN:(sb + 1) * SUPER_IN], b_ref[sb]))
        _scan_rows(h_ref, t_ref, carry_ref, tb // WINDOW, False)
        ys = [_mm(_bf(_get_tiles(h_ref, sb)), c_ref[sb]) for sb in range(N_SUPER)]
        s_ref[...] = jnp.concatenate(ys, axis=1) + d_ref[...] * u_blk

    return _rowcall("ssm_fwd", body, seq, tb, [u], [bmat, cmat, tab, d_skip],
                    [(SSM_W, F32), ((STATE_TILES, LANES), F32)], [],
                    scratch=[pltpu.VMEM((SUBLANES, STATE_COLS), F32)], vmem=VMEM_BIG, exchange=exchange)


def _ssm_bwd(ds, u, h, bmat_t, cmat_t, tab, d_skip, tb, exchange=None):
    seq = u.shape[0]

    def body(ds_ref, u_ref, h_ref, bt_ref, ct_ref, t_ref, d_ref,
             du_ref, db_ref, dc_ref, da_ref, dd_ref, g_ref, carry_ref):
        @pl.when(pl.program_id(0) == 0)
        def _():
            carry_ref[...] = jnp.zeros_like(carry_ref)
            db_ref[...] = jnp.zeros_like(db_ref)
            dc_ref[...] = jnp.zeros_like(dc_ref)
            da_ref[...] = jnp.zeros_like(da_ref)
            dd_ref[...] = jnp.zeros_like(dd_ref)

        ds_blk = ds_ref[...]
        dsb = _bf(ds_blk)
        u_blk = u_ref[...]
        ub = _bf(u_blk)
        for sb in range(N_SUPER):
            _put_tiles(g_ref, sb, _mm(dsb[:, sb * SUPER_IN:(sb + 1) * SUPER_IN], ct_ref[sb]))
        _scan_rows(g_ref, t_ref, carry_ref, tb // WINDOW, True, h_ref=h_ref, da_ref=da_ref)
        dus = []
        for sb in range(N_SUPER):
            gb = _bf(_get_tiles(g_ref, sb))
            dus.append(_mm(gb, bt_ref[sb]))
            db_ref[sb] += _mm_tn(ub[:, sb * SUPER_IN:(sb + 1) * SUPER_IN], gb)
            dc_ref[sb] += _mm_tn(_bf(_get_tiles(h_ref, sb)), dsb[:, sb * SUPER_IN:(sb + 1) * SUPER_IN])
        du_ref[...] = jnp.concatenate(dus, axis=1) + d_ref[...] * ds_blk
        dd_ref[...] += jnp.sum(ds_blk * u_blk, axis=0, keepdims=True)

    return _rowcall("ssm_bwd", body, seq, tb, [ds, u, h], [bmat_t, cmat_t, tab, d_skip],
                    [(SSM_W, F32)],
                    [((N_SUPER, SUPER_IN, SUPER_W), F32), ((N_SUPER, SUPER_W, SUPER_IN), F32),
                     ((SUBLANES, STATE_COLS), F32), ((1, SSM_W), F32)],
                    scratch=[pltpu.VMEM((STATE_TILES, tb, LANES), F32), pltpu.VMEM((SUBLANES, STATE_COLS), F32)],
                    reverse=True, vmem=VMEM_BIG, exchange=exchange)


def _merge_core(s, attb, ga, gs, wg_ref, wab_ref, wsb_ref, wout_ref):
    zg, dgelu = _gelu_and_grad(s)
    zgb = _bf(zg)
    sg = _sig(_mm(zgb, wg_ref[...]))
    z = zg * sg
    zb = _bf(z)
    ys = jnp.concatenate([_mm(zb, wsb_ref[j]) for j in range(N_CHIPS)], axis=1)
    ya = jnp.concatenate([_mm(attb, wab_ref[j]) for j in range(N_CHIPS)], axis=1)
    sa = _sig(ga)
    ss = _sig(gs)
    mgb = _bf(sa * ya + ss * ys)
    o = _mm(mgb, wout_ref[...])
    return dict(zg=zg, dgelu=dgelu, zgb=zgb, sg=sg, zb=zb, ys=ys, ya=ya, sa=sa, ss=ss, mgb=mgb, o=o)


def _merge_fwd(x, s, att, ga, gs, g2, w_glu, w_ab, w_sb, w_out, tb):
    seq = x.shape[0]

    def body(x_ref, s_ref, att_ref, ga_ref, gs_ref, g_ref, wg_ref, wab_ref, wsb_ref, wout_ref, x2_ref):
        f = _merge_core(s_ref[...], att_ref[...], ga_ref[...], gs_ref[...], wg_ref, wab_ref, wsb_ref, wout_ref)
        n, _, _ = _rms(f["o"], g_ref[...])
        x2_ref[...] = x_ref[...] + n

    return _rowcall("merge_fwd", body, seq, tb, [x, s, att, ga, gs], [g2, w_glu, w_ab, w_sb, w_out],
                    [(D_MODEL, F32)], [], vmem=VMEM_BIG)[0]


def _merge_bwd(dx2, s, att, ga, gs, g2, w_glu, w_ab, w_sb, w_out, tb, exchange=None):
    seq = s.shape[0]
    cw = D_MODEL // N_CHIPS
    last = seq // tb - 1

    def body(dx2_ref, s_ref, att_ref, ga_ref, gs_ref, g_ref, wg_ref, wab_ref, wsb_ref, wout_ref,
             ds_ref, datt_ref, dga_ref, dgs_ref, dg_ref, dwg_ref, dwab_ref, dwsb_ref, dwout_ref,
             bwg_ref, bwab_ref, bwsb_ref, bwout_ref):
        @pl.when(pl.program_id(0) == 0)
        def _():
            for r in (dg_ref, dwg_ref, dwab_ref, dwsb_ref, dwout_ref):
                r[...] = jnp.zeros_like(r)

        attb = att_ref[...]
        f = _merge_core(s_ref[...], attb, ga_ref[...], gs_ref[...], wg_ref, wab_ref, wsb_ref, wout_ref)
        g = g_ref[...]
        _, oh, r2 = _rms(f["o"], g)
        do, dg = _rms_bwd(dx2_ref[...], oh, r2, g)
        dg_ref[...] += dg
        dob = _bf(do)
        dwout_ref[...] += _mm_tn(f["mgb"], dob)
        dmg = _mm_nt(dob, wout_ref[...])
        sa, ss = f["sa"], f["ss"]
        dyab = _bf(dmg * sa)
        dysb = _bf(dmg * ss)
        dga_ref[...] = _bf(dmg * f["ya"] * sa * (1.0 - sa))
        dgs_ref[...] = _bf(dmg * f["ys"] * ss * (1.0 - ss))
        dwab = _mm_tn(attb, dyab)
        dwsb = _mm_tn(f["zb"], dysb)
        datt = jnp.zeros((tb, ATTN_W), F32)
        dz = jnp.zeros((tb, SSM_W), F32)
        for j in range(N_CHIPS):
            dwab_ref[j] += dwab[:, j * cw:(j + 1) * cw]
            dwsb_ref[j] += dwsb[:, j * cw:(j + 1) * cw]
            datt = datt + _mm_nt(dyab[:, j * cw:(j + 1) * cw], wab_ref[j])
            dz = dz + _mm_nt(dysb[:, j * cw:(j + 1) * cw], wsb_ref[j])
        datt_ref[...] = _bf(datt)
        sg, zg = f["sg"], f["zg"]
        dglb = _bf(dz * zg * sg * (1.0 - sg))
        dwg_ref[...] += _mm_tn(f["zgb"], dglb)
        dzg = dz * sg + _mm_nt(dglb, wg_ref[...])
        ds_ref[...] = dzg * f["dgelu"]

        @pl.when(pl.program_id(0) == last)
        def _():
            for dst, src in ((bwg_ref, dwg_ref), (bwab_ref, dwab_ref), (bwsb_ref, dwsb_ref), (bwout_ref, dwout_ref)):
                dst[...] = _bf(src[...])

    shapes = [w_glu.shape, w_ab.shape, w_sb.shape, w_out.shape]
    return _rowcall("merge_bwd", body, seq, tb, [dx2, s, att, ga, gs], [g2, w_glu, w_ab, w_sb, w_out],
                    [(SSM_W, F32), (ATTN_W, BF16), (D_MODEL, BF16), (D_MODEL, BF16)],
                    [((1, D_MODEL), F32)] + [(sh, F32) for sh in shapes] + [(sh, BF16) for sh in shapes],
                    vmem=VMEM_BIG, exchange=exchange)


def _mlp_fwd_loss(x2, target, g3, g4, w_ffi, w_ffo, tb):
    seq = x2.shape[0]
    n_slab = len(w_ffi)
    sw = D_FF // FF_CHUNKS // n_slab

    def body(x2_ref, t_ref, g3_ref, g4_ref, *rest):
        wi_refs, (wo_ref, dy_ref, df_ref, h_ref, ra_ref, loss_ref, dg_ref) = rest[:n_slab], rest[n_slab:]

        @pl.when(pl.program_id(0) == 0)
        def _():
            loss_ref[...] = jnp.zeros_like(loss_ref)
            dg_ref[...] = jnp.zeros_like(dg_ref)

        x2_blk = x2_ref[...]
        h3, _, _ = _rms(x2_blk, g3_ref[...])
        hb = _bf(h3)
        h_ref[...] = hb
        f = jnp.zeros((tb, D_MODEL), F32)
        for j in range(FF_CHUNKS):
            for k in range(n_slab):
                ra = jnp.maximum(_mm(hb, wi_refs[k][j]), 0.0)
                ra_ref[:, pl.ds((j * n_slab + k) * sw, sw)] = _bf(ra)
                f = f + _mm(_bf(ra * ra), wo_ref[j, pl.ds(k * sw, sw), :])
        g4 = g4_ref[...]
        n4, fh, r4 = _rms(f, g4)
        e = (x2_blk + n4) - t_ref[...]
        loss_ref[...] += 0.5 * jnp.sum(jnp.mean(e * e, axis=-1, keepdims=True))
        dy = e * (1.0 / D_MODEL)
        dy_ref[...] = dy
        df, dg = _rms_bwd(dy, fh, r4, g4)
        df_ref[...] = _bf(df)
        dg_ref[...] += dg

    return _rowcall("mlp_fwd_loss", body, seq, tb, [x2, target], [g3, g4, *w_ffi, w_ffo],
                    [(D_MODEL, F32), (D_MODEL, BF16), (D_MODEL, BF16), (D_FF, BF16)],
                    [((SUBLANES, 128), F32), ((1, D_MODEL), F32)], vmem=VMEM_BIG)


def _mlp_bwd(x2, dy, df, ra, g3, w_ffi, w_ffo, tb):
    seq = x2.shape[0]
    n_slab = len(w_ffi)
    sw = D_FF // FF_CHUNKS // n_slab

    def body(x2_ref, dy_ref, df_ref, ra_ref, g3_ref, *rest):
        wi_refs, (wo_ref, dx_ref, da_ref, dg_ref) = rest[:n_slab], rest[n_slab:]

        @pl.when(pl.program_id(0) == 0)
        def _():
            dg_ref[...] = jnp.zeros_like(dg_ref)

        dfb = df_ref[...]
        dh = jnp.zeros((tb, D_MODEL), F32)
        for j in range(FF_CHUNKS):
            for k in range(n_slab):
                cols = pl.ds((j * n_slab + k) * sw, sw)
                ra = ra_ref[:, cols].astype(F32)
                dab = _bf(_mm_nt(dfb, wo_ref[j, pl.ds(k * sw, sw), :]) * (2.0 * ra))
                da_ref[:, cols] = dab
                dh = dh + _mm_nt(dab, wi_refs[k][j])
        g3 = g3_ref[...]
        _, xh, r3 = _rms(x2_ref[...], g3)
        dxn, dg = _rms_bwd(dh, xh, r3, g3)
        dx_ref[...] = dy_ref[...] + dxn
        dg_ref[...] += dg

    return _rowcall("mlp_bwd", body, seq, tb, [x2, dy, df, ra], [g3, *w_ffi, w_ffo],
                    [(D_MODEL, F32), (D_FF, BF16)], [((1, D_MODEL), F32)], vmem=VMEM_BIG)


def _matmul_tn(name, a, b, tk, tn, tl, chunk_major, exchange=None, square_a=False):
    seq, kdim = a.shape
    ndim = b.shape[1]
    last = seq // tl - 1

    def body(a_ref, b_ref, o_ref, ob_ref):
        @pl.when(pl.program_id(2) == 0)
        def _():
            o_ref[...] = jnp.zeros_like(o_ref)

        a_blk = a_ref[...]
        if square_a:
            a_blk = _bf(jnp.square(a_blk.astype(F32)))
        o_ref[...] += _mm_tn(a_blk, b_ref[...])

        @pl.when(pl.program_id(2) == last)
        def _():
            ob_ref[...] = _bf(o_ref[...])

    if chunk_major:
        shape = (ndim // tn, kdim, tn)
        out_spec = pl.BlockSpec((None, tk, tn), lambda k, n, l: (n, k, 0))
    else:
        shape = (kdim, ndim)
        out_spec = pl.BlockSpec((tk, tn), lambda k, n, l: (k, n))
    return _fused_call(
        name, body, (kdim // tk, ndim // tn, seq // tl),
        [pl.BlockSpec((tl, tk), lambda k, n, l: (l, k)), pl.BlockSpec((tl, tn), lambda k, n, l: (l, n))],
        [out_spec, out_spec], [SDS(shape, F32), SDS(shape, BF16)], [], [a, b], exchange, _params(3, VMEM_BIG))


def _ew_call(name, fn, ins, n_out, after=None):
    rows, cols = ins[0].shape
    tr = rows
    while tr * cols * 4 > min(1 << 20, (9 << 20) // (len(ins) + n_out)) and tr % 16 == 0:
        tr //= 2
    spec = pl.BlockSpec((tr, cols), lambda i: (i, 0))
    extra = [] if after is None else [after]

    def body(*refs):
        outs = fn(*[r[...] for r in refs[:len(ins)]])
        for r, o in zip(refs[len(ins) + len(extra):], outs):
            r[...] = o

    return pl.pallas_call(
        body, grid=(rows // tr,), in_specs=[spec] * len(ins) + [ANY] * len(extra), out_specs=[spec] * n_out,
        out_shape=[SDS((rows, cols), F32)] * n_out, name=name, compiler_params=_params(1))(*ins, *extra)


def _adam_math(w, g, m, v):
    m2 = ADAM_B1 * m + (1.0 - ADAM_B1) * g
    v2 = ADAM_B2 * v + (1.0 - ADAM_B2) * (g * g)
    m_hat = m2 / (1.0 - ADAM_B1 ** ADAM_STEP)
    v_hat = v2 / (1.0 - ADAM_B2 ** ADAM_STEP)
    delta = -ADAM_LR * (m_hat / (jnp.sqrt(v_hat) + ADAM_EPS) + ADAM_WD * w)
    return delta, m2, v2


def _sum4(name, own, recv, idx):
    _, rows, cols = own.shape
    tr = rows
    while tr * cols * 4 > (1 << 20) and tr % 16 == 0:
        tr //= 2

    def body(idx_ref, o_ref, r0_ref, r1_ref, r2_ref, out_ref):
        out_ref[...] = ((o_ref[...] + r0_ref[...].astype(F32)) + r1_ref[...].astype(F32)) + r2_ref[...].astype(F32)

    blk = (None, tr, cols)
    grid_spec = pltpu.PrefetchScalarGridSpec(
        num_scalar_prefetch=1, grid=(rows // tr,),
        in_specs=[pl.BlockSpec(blk, lambda i, s: (s[0], i, 0)), pl.BlockSpec(blk, lambda i, s: (0, i, 0)),
                  pl.BlockSpec(blk, lambda i, s: (1, i, 0)), pl.BlockSpec(blk, lambda i, s: (2, i, 0))],
        out_specs=pl.BlockSpec((tr, cols), lambda i, s: (i, 0)))
    return pl.pallas_call(body, grid_spec=grid_spec, out_shape=SDS((rows, cols), F32), name=name,
                          compiler_params=_params(1))(jnp.reshape(idx, (1,)).astype(jnp.int32), own, recv, recv, recv)


def _adam_pair(name, item, after=None):
    def fn(w_, a, b, m_, v_):
        g = a + b
        return (g,) + _adam_math(w_, g, m_, v_)

    return _ew_call(name, fn, list(item), 4, after)


def _place():
    return lax.axis_index("x"), lax.axis_index("y"), lax.axis_index("c")


def _other_chips(x, y):
    return [(1 - x, y), (x, 1 - y), (1 - x, 1 - y)]


HBM = pl.BlockSpec(memory_space=pltpu.HBM)
SEM = pl.BlockSpec(memory_space=pltpu.SEMAPHORE)
DATAFLOW = pltpu.SideEffectType.DATAFLOW_SIDE_EFFECTING


class _Flight:
    def __init__(self, copies, n_copies, send, recv, srcs, lands, token):
        self.copies, self.n, self.send, self.recv = copies, n_copies, send, recv
        self.srcs, self.lands, self.token = list(srcs), list(lands), token


def _take_off(name, srcs, lands, copies, n_copies, after):
    n_s, n_l = len(srcs), len(lands)

    def body(*refs):
        src, land = refs[:n_s], refs[n_s:n_s + n_l]
        send, recv = refs[n_s + n_l + 1:n_s + n_l + 3]
        for cp in copies(src, land, send, recv):
            cp.start()
        refs[-1][...] = jnp.zeros_like(refs[-1])

    mem = lambda t: pltpu.HBM(t.shape, t.dtype)
    sems = pltpu.SemaphoreType.DMA((n_copies,))
    outs = pl.pallas_call(
        body, name=name,
        out_shape=(sems, sems, *map(mem, srcs), *map(mem, lands), SDS((SUBLANES, LANES), F32)),
        in_specs=[HBM] * (n_s + n_l) + [ANY],
        out_specs=(SEM, SEM, *[HBM] * (n_s + n_l), pl.BlockSpec(memory_space=pltpu.VMEM)),
        input_output_aliases={i: 2 + i for i in range(n_s + n_l)},
        compiler_params=pltpu.CompilerParams(has_side_effects=DATAFLOW),
    )(*[pltpu.with_memory_space_constraint(t, pltpu.HBM) for t in (*srcs, *lands)], after)
    return _Flight(copies, n_copies, outs[0], outs[1], outs[2:2 + n_s], outs[2 + n_s:2 + n_s + n_l], outs[-1])


def _land(name, flight, after):
    n_s, n_l = len(flight.srcs), len(flight.lands)

    def body(*refs):
        src, land = refs[:n_s], refs[n_s:n_s + n_l]
        send, recv = refs[n_s + n_l:n_s + n_l + 2]
        for cp in flight.copies(src, land, send, recv):
            cp.wait_send()
            cp.wait_recv()

    mem = lambda t: pltpu.HBM(t.shape, t.dtype)
    outs = pl.pallas_call(
        body, name=name, out_shape=(*map(mem, flight.srcs), *map(mem, flight.lands)),
        in_specs=[HBM] * (n_s + n_l) + [SEM, SEM, ANY], out_specs=tuple([HBM] * (n_s + n_l)),
        input_output_aliases={i: i for i in range(n_s + n_l)},
        compiler_params=pltpu.CompilerParams(has_side_effects=DATAFLOW),
    )(*flight.srcs, *flight.lands, flight.send, flight.recv, after)
    return list(outs[:n_s]), list(outs[n_s:])


def _empty_like(shapes_from, lead):
    return [lax.empty((lead,) + t.shape[1:], t.dtype) for t in shapes_from]


def _scatter_off(name, chunks, after):
    def copies(src, land, send, recv):
        x, y, c = _place()
        return [pltpu.make_async_remote_copy(
            src_ref=src[a].at[2 * px + py], dst_ref=land[a].at[k], send_sem=send.at[3 * a + k],
            recv_sem=recv.at[3 * a + k], device_id=(px, py, c), device_id_type=MESH_ID)
            for a in range(len(chunks)) for k, (px, py) in enumerate(_other_chips(x, y))]

    return _take_off(name, chunks, _empty_like(chunks, 3), copies, 3 * len(chunks), after)


def _swap_off(name, arrs, after):
    def copies(src, land, send, recv):
        x, y, c = _place()
        return [pltpu.make_async_remote_copy(
            src_ref=src[a], dst_ref=land[a], send_sem=send.at[a], recv_sem=recv.at[a],
            device_id=(x, y, 1 - c), device_id_type=MESH_ID) for a in range(len(arrs))]

    return _take_off(name, arrs, [lax.empty(t.shape, t.dtype) for t in arrs], copies, len(arrs), after)


def _devices_off(name, block, after):
    me = 4 * lax.axis_index("x") + 2 * lax.axis_index("y") + lax.axis_index("c")
    land = lax.dynamic_update_index_in_dim(lax.empty((N_DEV,) + block.shape, block.dtype), block, me, 0)

    def copies(src, land, send, recv):
        x, y, c = _place()
        mine = 4 * x + 2 * y + c
        return [pltpu.make_async_remote_copy(
            src_ref=src[0], dst_ref=land[0].at[mine], send_sem=send.at[k - 1], recv_sem=recv.at[k - 1],
            device_id=(x ^ (k >> 2), y ^ ((k >> 1) & 1), c ^ (k & 1)), device_id_type=MESH_ID)
            for k in range(1, N_DEV)]

    return _take_off(name, [block], [land], copies, N_DEV - 1, after)


def _half_rows(shape, c, other=False):
    half = shape[0] // 2
    return pl.ds(((1 - c) if other else c) * half, half)


def _gather_start(name, shards, lands, after):
    n = len(shards)

    def body(*refs):
        src, land, (send, recv) = refs[:n], refs[n:2 * n], refs[2 * n + 1:2 * n + 3]
        x, y, c = _place()
        me = 2 * x + y
        for a in range(n):
            mine = _half_rows(shards[a].shape, c)
            for j, (px, py) in enumerate(_other_chips(x, y)):
                pltpu.make_async_remote_copy(
                    src_ref=src[a].at[mine], dst_ref=land[a].at[me, mine], send_sem=send.at[3 * a + j],
                    recv_sem=recv.at[3 * a + j], device_id=(px, py, c), device_id_type=MESH_ID).start()
        token = refs[-1]
        token[...] = jnp.zeros_like(token)

    mem = lambda t: pltpu.HBM(t.shape, t.dtype)
    pair = pltpu.SemaphoreType.DMA((3 * n,))
    outs = pl.pallas_call(
        body, name=name,
        out_shape=(pair, pair, *map(mem, shards), *map(mem, lands), SDS((SUBLANES, LANES), F32)),
        in_specs=[HBM] * (2 * n) + [ANY],
        out_specs=(SEM, SEM, *[HBM] * (2 * n), pl.BlockSpec(memory_space=pltpu.VMEM)),
        input_output_aliases={i: 2 + i for i in range(2 * n)},
        compiler_params=pltpu.CompilerParams(has_side_effects=DATAFLOW),
    )(*[pltpu.with_memory_space_constraint(t, pltpu.HBM) for t in (*shards, *lands)], after)
    return outs[0], outs[1], list(outs[2:2 + n]), list(outs[2 + n:2 + 2 * n]), outs[-1]


def _gather_pass(name, send, recv, shards, lands, after):
    n = len(shards)

    def body(*refs):
        src, land, (send, recv, _) = refs[:n], refs[n:2 * n], refs[2 * n:2 * n + 3]
        fsend, frecv = refs[2 * n + 3], refs[2 * n + 4]
        x, y, c = _place()
        me = 2 * x + y
        for a in range(n):
            mine = _half_rows(shards[a].shape, c)
            for j, (px, py) in enumerate(_other_chips(x, y)):
                far = 2 * px + py
                ici = pltpu.make_async_remote_copy(
                    src_ref=src[a].at[mine], dst_ref=land[a].at[far, mine], send_sem=send.at[3 * a + j],
                    recv_sem=recv.at[3 * a + j], device_id=(px, py, c), device_id_type=MESH_ID)
                ici.wait_recv()
                ici.wait_send()
                pltpu.make_async_remote_copy(
                    src_ref=land[a].at[far, mine], dst_ref=land[a].at[far, mine], send_sem=fsend.at[3 * a + j],
                    recv_sem=frecv.at[3 * a + j], device_id=(x, y, 1 - c), device_id_type=MESH_ID).start()
        token = refs[-1]
        token[...] = jnp.zeros_like(token)

    mem = lambda t: pltpu.HBM(t.shape, t.dtype)
    pair = pltpu.SemaphoreType.DMA((3 * n,))
    outs = pl.pallas_call(
        body, name=name,
        out_shape=(pair, pair, *map(mem, lands), SDS((SUBLANES, LANES), F32)),
        in_specs=[HBM] * (2 * n) + [SEM, SEM, ANY],
        out_specs=(SEM, SEM, *[HBM] * n, pl.BlockSpec(memory_space=pltpu.VMEM)),
        input_output_aliases={n + i: 2 + i for i in range(n)},
        compiler_params=pltpu.CompilerParams(has_side_effects=DATAFLOW),
    )(*shards, *lands, send, recv, after)
    return outs[0], outs[1], list(outs[2:2 + n]), outs[-1]


def _gather_wait(name, fsend, frecv, lands, after):
    n = len(lands)

    def body(*refs):
        land, (fsend, frecv, _) = refs[:n], refs[n:n + 3]
        x, y, c = _place()
        for a in range(n):
            for j, (px, py) in enumerate(_other_chips(x, y)):
                far = 2 * px + py
                mine = _half_rows(lands[a].shape[1:], c)
                theirs = _half_rows(lands[a].shape[1:], c, other=True)
                pltpu.make_async_remote_copy(
                    src_ref=land[a].at[far, mine], dst_ref=land[a].at[far, mine], send_sem=fsend.at[3 * a + j],
                    recv_sem=frecv.at[3 * a + j], device_id=(x, y, 1 - c), device_id_type=MESH_ID).wait_send()
                pltpu.make_async_remote_copy(
                    src_ref=land[a].at[far, theirs], dst_ref=land[a].at[far, theirs], send_sem=fsend.at[3 * a + j],
                    recv_sem=frecv.at[3 * a + j], device_id=(x, y, 1 - c), device_id_type=MESH_ID).wait_recv()

    mem = lambda t: pltpu.HBM(t.shape, t.dtype)
    return list(pl.pallas_call(
        body, name=name, out_shape=tuple(map(mem, lands)), in_specs=[HBM] * n + [SEM, SEM, ANY],
        out_specs=tuple([HBM] * n), input_output_aliases={i: i for i in range(n)},
        compiler_params=pltpu.CompilerParams(has_side_effects=DATAFLOW),
    )(*lands, fsend, frecv, after))


def _after(token):
    return _Exchange([token], [], [], lambda *_: None, lambda *_: None)


def _swap_sibling(arrs):
    n = len(arrs)

    def copies(ins, outs, sems):
        send, recv = sems
        x, y, c = _place()
        return [pltpu.make_async_remote_copy(
            src_ref=ins[a], dst_ref=outs[a], send_sem=send.at[a], recv_sem=recv.at[a],
            device_id=(x, y, 1 - c), device_id_type=MESH_ID) for a in range(n)]

    def start(ins, outs, sems):
        for cp in copies(ins, outs, sems):
            cp.start()

    def wait(ins, outs, sems):
        cps = copies(ins, outs, sems)
        for cp in cps:
            cp.wait_recv()
        for cp in cps:
            cp.wait_send()

    return _Exchange(arrs, [SDS(s.shape, s.dtype) for s in arrs],
                     [pltpu.SemaphoreType.DMA((n,)), pltpu.SemaphoreType.DMA((n,))], start, wait)


def _sum_devices(slots):
    def body(s_ref, o_ref):
        acc = s_ref[0]
        for d in range(1, N_DEV):
            acc = acc + s_ref[d]
        o_ref[...] = acc

    return pl.pallas_call(
        body, in_specs=[pl.BlockSpec(memory_space=pltpu.VMEM)], out_specs=pl.BlockSpec(memory_space=pltpu.VMEM),
        out_shape=SDS(slots.shape[1:], F32), name="sum_small",
        compiler_params=pltpu.CompilerParams(vmem_limit_bytes=32 * 1024 * 1024))(slots)


def _adam_small(ws, gs, ms, vs):
    n = len(ws)

    def body(*refs):
        for i in range(n):
            w_ref, g_ref, m_ref, v_ref = (refs[k * n + i] for k in range(4))
            outs = _adam_math(w_ref[...], g_ref[...], m_ref[...], v_ref[...])
            for k in range(3):
                refs[(4 + k) * n + i][...] = outs[k]

    vmem = pl.BlockSpec(memory_space=pltpu.VMEM)
    return pl.pallas_call(
        body, in_specs=[vmem] * (4 * n), out_specs=[vmem] * (3 * n),
        out_shape=[SDS(w.shape, F32) for w in ws] * 3, name="adam_small",
        compiler_params=pltpu.CompilerParams(vmem_limit_bytes=32 * 1024 * 1024))(*ws, *gs, *ms, *vs)


def _local_step(x, target, small, big, tb, distributed):
    dist = distributed
    me = (2 * lax.axis_index("x") + lax.axis_index("y")) if dist else 0
    tb_ssm = min(tb, 256)
    bucket = jnp.asarray(_bucket_table())
    place_own = lambda t: lax.dynamic_update_index_in_dim(lax.empty((N_CHIPS,) + t.shape, t.dtype), t, me, 0)
    if dist:
        in_legs = _gather_start("gather_in_start", [big["w_in"]], [place_own(big["w_in"])], small["d_skip"])
        names = sorted(small)
        in_token, values = lax.optimization_barrier((in_legs[4], [small[n] for n in names]))
        small = dict(zip(names, values))
    g1, g2, g3, g4 = small["norm_mix_pre"], small["norm_mix_post"], small["norm_mlp_pre"], small["norm_mlp_post"]

    keys_first = lambda t: jnp.swapaxes(t, -1, -2)
    bias = _bias_table(small["rel_bias"], bucket)
    sink_rows = keys_first(_pair_layout(jnp.broadcast_to(small["sinks"].reshape(N_HEADS, 1, 1), (N_HEADS, BLOCK, 1))))
    disc_args = (small["lam_re"], small["lam_im"], small["log_dt"], small["b_re"], small["b_im"])
    (ab_re, ab_im, bb_re, bb_im), disc_vjp = jax.vjp(_ssm_discretize, *disc_args)
    tab_f, tab_b = _scan_tables(ab_re, ab_im)
    bmat = _bf(_b_matrix(bb_re, bb_im))
    cmat = _bf(_c_matrix(small["c_re"], small["c_im"]))
    d_skip = small["d_skip"]

    mix = ("w_glu", "w_attn_branch", "w_ssm_branch", "w_out")
    rest = [big[n] for n in mix + ("w_ff_in", "w_ff_out")]
    if dist:
        send, recv, src, lands, _ = in_legs
        rest_lands = [place_own(t) for t in rest]
        corner = lambda t: t.reshape(-1, t.shape[-1])[:1, :LANES].astype(F32)
        prepared = sum(map(corner, [tab_b, bias, sink_rows, bmat, cmat] + rest_lands), in_token[:1])
        send, recv, lands, in_passed = _gather_pass("gather_in_pass", send, recv, src, lands, prepared)
        (g_in,) = _gather_wait("gather_in_wait", send, recv, lands, in_passed)
        w_in = g_in.reshape(IN_W, D_MODEL)
    else:
        w_in = big["w_in"]
    token = None
    if dist:
        send, recv, rest, lands, token = _gather_start("gather_rest_start", rest, rest_lands, in_passed)
    h1, q, k, v, u, ga, gs = _inproj_fwd(x, g1, w_in, tb, _after(token) if dist else None)
    s, h = _ssm_fwd(u, bmat, cmat, tab_f, d_skip, tb)
    if dist:
        send, recv, lands, token = _gather_pass("gather_rest_pass", send, recv, rest, lands, s)
    att = _attn_fwd(q, k, v, bias, sink_rows, _after(token) if dist else None)[0]
    if dist:
        rest = _gather_wait("gather_rest_wait", send, recv, lands, att)
    w_glu, w_ab, w_sb, w_out, w_ffi, w_ffo = rest
    w_glu = w_glu.reshape(SSM_W, SSM_W)
    w_out = w_out.reshape(D_MODEL, D_MODEL)
    w_ffi = [w_ffi]
    x2 = _merge_fwd(x, s, att, ga, gs, g2, w_glu, w_ab, w_sb, w_out, tb)
    dy, df, h3, ra, loss_acc, dg4 = _mlp_fwd_loss(x2, target, g3, g4, w_ffi, w_ffo, tb)

    dx2, da, dg3 = _mlp_bwd(x2, dy, df, ra, g3, w_ffi, w_ffo, tb)
    tl = min(2048, x.shape[0])
    chunked = (N_CHIPS, D_FF // N_CHIPS, D_MODEL)
    d_ffi, b_ffi = _matmul_tn("grad_w_ff_in", h3, da, D_MODEL, D_FF // FF_CHUNKS, tl, True)
    d_ffo, b_ffo = _matmul_tn("grad_w_ff_out", ra, df, D_FF // FF_CHUNKS, D_MODEL, tl, False, square_a=True)
    d_ffo, b_ffo = d_ffo.reshape(chunked), b_ffo.reshape(chunked)
    behind = lambda flight: _after(flight.token) if dist else None
    ff_fl = _scatter_off("scatter_ff_off", [b_ffi, b_ffo], d_ffo) if dist else None
    outs = _merge_bwd(dx2, s, att, ga, gs, g2, w_glu, w_ab, w_sb, w_out, tb_ssm, behind(ff_fl))
    ds, datt, dga, dgs, dg2, d_glu, d_ab, d_sb, d_out, b_glu, b_ab, b_sb, b_out = outs
    glu4, out4 = (N_CHIPS, SSM_W // N_CHIPS, SSM_W), (N_CHIPS, D_MODEL // N_CHIPS, D_MODEL)
    d_mix = [d_glu.reshape(glu4), d_ab, d_sb, d_out.reshape(out4)]
    b_mix = [b_glu.reshape(glu4), b_ab, b_sb, b_out.reshape(out4)]
    mix_fl = _scatter_off("scatter_mix_off", b_mix, d_mix[-1]) if dist else None
    du, d_bmat, d_cmat, da_acc, dd_skip = _ssm_bwd(
        ds, u, h, bmat.transpose(0, 2, 1), cmat.transpose(0, 2, 1), tab_b, d_skip, tb, behind(mix_fl))
    dq, dk, dv, dbias, dsink_rows = _attn_bwd(q, k, v, datt, bias, sink_rows)
    swap_fl = None
    if dist:
        r_ffi, r_ffo = _land("scatter_ff_land", ff_fl, dq)[1]
        p_ffi = _sum4("sum_w_ff_in", d_ffi, r_ffi, me)
        p_ffo = _sum4("sum_w_ff_out", d_ffo, r_ffo, me)
        swap_fl = _swap_off("swap_ff_off", [p_ffi, p_ffo], r_ffo)
    dx, dpj, dg1 = _inproj_bwd(x, dx2, dq, dk, dv, du, dga, dgs, g1, w_in, tb, behind(swap_fl))

    dab_re, dab_im = _state_unlayout(jnp.sum(da_acc, axis=0))
    dbb_re, dbb_im = _b_matrix_grad(d_bmat)
    d_lam_re, d_lam_im, d_log_dt, d_b_re, d_b_im = disc_vjp((dab_re, dab_im, dbb_re, dbb_im))
    d_c_re, d_c_im = _c_matrix_grad(d_cmat)
    d_rel = _bias_grad(dbias, bucket)
    d_sinks = jnp.sum(_pair_unlayout(keys_first(dsink_rows)), axis=(1, 2))
    small_grads = dict(
        norm_mix_pre=dg1, norm_mix_post=dg2, norm_mlp_pre=dg3, norm_mlp_post=dg4, rel_bias=d_rel, sinks=d_sinks,
        lam_re=d_lam_re, lam_im=d_lam_im, log_dt=d_log_dt, b_re=d_b_re, b_im=d_b_im, c_re=d_c_re, c_im=d_c_im,
        d_skip=dd_skip)
    small_fl = _devices_off("small_off", _pack(small_grads, loss_acc), swap_fl.token) if dist else None
    outs = _matmul_tn("grad_w_in", dpj, h1, IN_W // 2, D_MODEL, tl, False, behind(small_fl))
    in4 = (N_CHIPS, IN_W // N_CHIPS, D_MODEL)
    d_in, b_in = outs[0].reshape(in4), outs[1].reshape(in4)
    if not dist:
        return loss_acc, dx, small_grads, dict(zip(BIG, [d_in] + d_mix + [d_ffi, d_ffo]))
    (p_ffi, p_ffo), (s_ffi, s_ffo) = _land("swap_ff_land", swap_fl, b_in)
    r_mix = _land("scatter_mix_land", mix_fl, b_in)[1]
    p_mix = [_sum4("sum_" + n, d, r, me) for n, d, r in zip(mix, d_mix, r_mix)]
    pending = dict(d_in=d_in, b_in=b_in, p_mix=p_mix, w_ff_in=(p_ffi, s_ffi), w_ff_out=(p_ffo, s_ffo), me=me)
    return loss_acc, dx, small_fl, pending


SMALL = ['norm_mix_pre', 'norm_mix_post', 'norm_mlp_pre', 'norm_mlp_post', 'rel_bias', 'sinks', 'lam_re', 'lam_im',
         'log_dt', 'b_re', 'b_im', 'c_re', 'c_im', 'd_skip']
BIG = ['w_in', 'w_glu', 'w_attn_branch', 'w_ssm_branch', 'w_out', 'w_ff_in', 'w_ff_out']
WEIGHTS = ['norm_mix_pre', 'norm_mix_post', 'norm_mlp_pre', 'norm_mlp_post', 'w_in', 'rel_bias', 'sinks', 'lam_re',
           'lam_im', 'log_dt', 'b_re', 'b_im', 'c_re', 'c_im', 'd_skip', 'w_glu', 'w_attn_branch', 'w_ssm_branch',
           'w_out', 'w_ff_in', 'w_ff_out']
PACK_COLS = 1024
PACK_ORDER = ['b_re', 'b_im', 'c_re', 'c_im', 'lam_re', 'lam_im', 'norm_mix_pre', 'norm_mix_post', 'norm_mlp_pre',
              'norm_mlp_post', 'rel_bias', 'sinks', 'log_dt', 'd_skip']


STATE_MINOR = ('b_re', 'b_im')
PACK_ROWS = 144
LOSS_ROW = 140


def _pack(named, loss_acc):
    parts = []
    for n in PACK_ORDER:
        a = jnp.swapaxes(named[n], -1, -2) if n in STATE_MINOR else named[n]
        flat = a.reshape(-1)
        rows = -(-flat.shape[0] // PACK_COLS)
        parts.append(jnp.pad(flat, (0, rows * PACK_COLS - flat.shape[0])).reshape(rows, PACK_COLS))
    assert sum(p.shape[0] for p in parts) == LOSS_ROW
    parts.append(jnp.pad(loss_acc[0:1], ((0, PACK_ROWS - LOSS_ROW - 1), (0, PACK_COLS - loss_acc.shape[1]))))
    return jnp.concatenate(parts, axis=0)


def _unpack(packed, shapes):
    out, at = {}, 0
    for n in PACK_ORDER:
        shape = shapes[n][:-2] + (shapes[n][-1], shapes[n][-2]) if n in STATE_MINOR else shapes[n]
        size = int(np.prod(shape))
        rows = -(-size // PACK_COLS)
        blk = packed[at:at + rows]
        out[n] = (blk.reshape(-1)[:size] if size % PACK_COLS else blk).reshape(shape)
        at += rows
    return out


def kernel(x, norm_mix_pre, norm_mix_post, norm_mlp_pre, norm_mlp_post, w_in, rel_bias, sinks, lam_re, lam_im, log_dt, b_re, b_im, c_re, c_im, d_skip, w_glu, w_attn_branch, w_ssm_branch, w_out, w_ff_in, w_ff_out, loss_target, m_norm_mix_pre, m_norm_mix_post, m_norm_mlp_pre, m_norm_mlp_post, m_w_in, m_rel_bias, m_sinks, m_lam_re, m_lam_im, m_log_dt, m_b_re, m_b_im, m_c_re, m_c_im, m_d_skip, m_w_glu, m_w_attn_branch, m_w_ssm_branch, m_w_out, m_w_ff_in, m_w_ff_out, v_norm_mix_pre, v_norm_mix_post, v_norm_mlp_pre, v_norm_mlp_post, v_w_in, v_rel_bias, v_sinks, v_lam_re, v_lam_im, v_log_dt, v_b_re, v_b_im, v_c_re, v_c_im, v_d_skip, v_w_glu, v_w_attn_branch, v_w_ssm_branch, v_w_out, v_w_ff_in, v_w_ff_out):
    env = dict(locals())
    w = {n: env[n] for n in WEIGHTS}
    m = {n: env["m_" + n] for n in WEIGHTS}
    v = {n: env["v_" + n] for n in WEIGHTS}
    seq = x.shape[1]
    tb = min(512, seq)

    small = {n: w[n] for n in ('norm_mix_pre', 'norm_mix_post', 'norm_mlp_pre', 'norm_mlp_post', 'rel_bias')}
    small.update({n: w[n][0] for n in ('sinks', 'lam_re', 'lam_im', 'log_dt', 'b_re', 'b_im', 'c_re', 'c_im')})
    small['d_skip'] = w['d_skip']
    shard = lambda t, n: t[n][0].T if n == 'w_in' else t[n][0]
    unshard = lambda a, n: (a.T if n == 'w_in' else a)[None]
    _, dx, small_fl, pending = _local_step(
        x[0], loss_target[0], small, {n: _bf(shard(w, n)) for n in BIG}, tb, True)

    grads, deltas, new_m, new_v = {}, {}, {}, {}

    def adam(n, partials, after=None):
        outs = _adam_pair("adam_" + n, (shard(w, n), *partials, shard(m, n), shard(v, n)), after)
        grads[n], deltas[n], new_m[n], new_v[n] = [unshard(a, n) for a in outs]
        return outs[3]

    mix = ("w_glu", "w_attn_branch", "w_ssm_branch", "w_out")
    in_fl = _scatter_off("scatter_w_in_off", [pending["b_in"]], pending["d_in"])
    sib_mix = _exchange_alone("swap_mix", _swap_sibling(pending["p_mix"]))
    last = in_fl.token
    for n, partials in [(n, pending[n]) for n in ("w_ff_in", "w_ff_out")] + list(zip(mix, zip(pending["p_mix"], sib_mix))):
        last = adam(n, partials, last)

    small_g = _sum_devices(_land("small_land", small_fl, last)[1][0])
    loss = small_g[LOSS_ROW, 0]
    minor = lambda t, n: jnp.swapaxes(t, -1, -2) if n in STATE_MINOR else t
    g_small = _unpack(small_g, {n: w[n].shape for n in SMALL})
    outs = _adam_small([minor(w[n], n) for n in SMALL], [g_small[n] for n in SMALL],
                       [minor(m[n], n) for n in SMALL], [minor(v[n], n) for n in SMALL])
    grads.update({n: minor(g_small[n], n) for n in SMALL})
    for k, dst in enumerate((deltas, new_m, new_v)):
        dst.update({n: minor(a, n) for n, a in zip(SMALL, outs[k * len(SMALL):(k + 1) * len(SMALL)])})

    (r_in,) = _land("scatter_w_in_land", in_fl, outs[0])[1]
    p_in = _sum4("sum_w_in", pending["d_in"], r_in, pending["me"])
    (s_in,) = _exchange_alone("swap_w_in", _swap_sibling([p_in]))
    adam("w_in", (p_in, s_in))

    return (loss, dx[None], *[grads[n] for n in WEIGHTS], *[deltas[n] for n in WEIGHTS],
            *[new_m[n] for n in WEIGHTS], *[new_v[n] for n in WEIGHTS])
```

```python
import functools
import math

import numpy as np
import jax
import jax.numpy as jnp
from jax import lax
from jax.experimental import pallas as pl
from jax.experimental.pallas import tpu as pltpu

F32 = jnp.float32
BF16 = jnp.bfloat16

D_MODEL = 1024
N_HEADS = 8
N_KV = 2
Q_GROUP = 4
HEAD_DIM = 64
ATTN_W = 512
KV_W = 128
BLOCK = 128
N_BUCKETS = 32
MAX_DISTANCE = 128
NEG_INF = -1e30
SSM_W = 512
SSM_GROUP = 16
SSM_GROUPS = 32
SSM_STATE = 64
N_SUPER = 4
GROUPS_PER_SUPER = SSM_GROUPS // N_SUPER
SUPER_IN = GROUPS_PER_SUPER * SSM_GROUP
SUPER_HALF = GROUPS_PER_SUPER * SSM_STATE
SUPER_W = 2 * SUPER_HALF
STATE_COLS = N_SUPER * SUPER_W
D_FF = 4096
FF_CHUNKS = 4
IN_W = 3328
SPLITS = (0, 512, 640, 768, 1280, 2304, 3328)
RMS_EPS = 1e-6
N_CHIPS = 4
N_DEV = 8
SUBLANES = 8
LANES = 128
STATE_TILES = STATE_COLS // LANES
SUPER_TILES = SUPER_W // LANES

ADAM_LR = 0.001
ADAM_B1 = 0.9
ADAM_B2 = 0.999
ADAM_EPS = 1e-08
ADAM_WD = 0.01
ADAM_STEP = 10

VMEM_BIG = 56 * 1024 * 1024
SDS = jax.ShapeDtypeStruct
MESH_ID = pl.DeviceIdType.MESH
ANY = pl.BlockSpec(memory_space=pl.ANY)


def _bf(x):
    return x.astype(BF16)


def _mm(a, b):
    return jnp.dot(a, b, preferred_element_type=F32)


def _mm_nt(a, b):
    return lax.dot_general(a, b, (((1,), (1,)), ((), ())), preferred_element_type=F32)


def _mm_tn(a, b):
    return lax.dot_general(a, b, (((0,), (0,)), ((), ())), preferred_element_type=F32)


def _sig(x):
    return 1.0 / (1.0 + jnp.exp(-x))


def _rms(x, g):
    r = lax.rsqrt(jnp.mean(x * x, axis=-1, keepdims=True) + RMS_EPS)
    xh = x * r
    return xh * g, xh, r


def _rms_bwd(dout, xh, r, g):
    dg = jnp.sum(dout * xh, axis=0, keepdims=True)
    dxh = dout * g
    dx = r * (dxh - xh * jnp.mean(dxh * xh, axis=-1, keepdims=True))
    return dx, dg


_GELU_C = math.sqrt(2.0 / math.pi)


def _gelu_and_grad(x):
    x2 = x * x
    inner = _GELU_C * (x + 0.044715 * (x2 * x))
    t = jnp.tanh(inner)
    y = 0.5 * x * (1.0 + t)
    dy = 0.5 * (1.0 + t) + 0.5 * x * (1.0 - t * t) * (_GELU_C * (1.0 + 3.0 * 0.044715 * x2))
    return y, dy


def _zero_map(nd, *_):
    return (0,) * nd


def _params(n_axes, vmem=None):
    return pltpu.CompilerParams(dimension_semantics=("arbitrary",) * n_axes, vmem_limit_bytes=vmem)


class _Exchange:
    def __init__(self, ins, outs, sems, start, wait):
        self.ins, self.outs, self.sems, self.start, self.wait = list(ins), list(outs), list(sems), start, wait


def _fused_call(name, body, grid, in_specs, out_specs, out_shape, scratch, args, exchange, params):
    n_in, n_out, n_scr = len(in_specs), len(out_specs), len(scratch)
    if exchange is None:
        fn = body
    else:
        ex = exchange
        n_xi, n_xo = len(ex.ins), len(ex.outs)

        def fn(*refs):
            at = 0
            parts = []
            for n in (n_in, n_xi, n_out, n_xo, n_scr, len(ex.sems)):
                parts.append(refs[at:at + n])
                at += n
            ins, x_in, outs, x_out, scr, x_sem = parts
            ids = [pl.program_id(a) for a in range(len(grid))]
            first = functools.reduce(jnp.logical_and, [i == 0 for i in ids])
            last = functools.reduce(jnp.logical_and, [i == g - 1 for i, g in zip(ids, grid)])

            @pl.when(first)
            def _():
                ex.start(x_in, x_out, x_sem)

            body(*ins, *outs, *scr)

            @pl.when(last)
            def _():
                ex.wait(x_in, x_out, x_sem)

        in_specs = list(in_specs) + [ANY] * n_xi
        out_specs = list(out_specs) + [ANY] * n_xo
        out_shape = list(out_shape) + ex.outs
        scratch = list(scratch) + ex.sems
        args = list(args) + ex.ins
    return pl.pallas_call(fn, grid=grid, in_specs=in_specs, out_specs=out_specs, out_shape=out_shape,
                          scratch_shapes=list(scratch), name=name, compiler_params=params)(*args)


def _exchange_alone(name, ex):
    def body(*refs):
        n_xi, n_xo = len(ex.ins), len(ex.outs)
        x_in, x_out, x_sem = refs[:n_xi], refs[n_xi:n_xi + n_xo], refs[n_xi + n_xo:]
        ex.start(x_in, x_out, x_sem)
        ex.wait(x_in, x_out, x_sem)

    return pl.pallas_call(body, in_specs=[ANY] * len(ex.ins), out_specs=[ANY] * len(ex.outs), out_shape=ex.outs,
                          scratch_shapes=ex.sems, name=name)(*ex.ins)


def _rowcall(name, body, seq, tb, rows, consts, row_outs, acc_outs, scratch=(), reverse=False, vmem=None,
             exchange=None):
    nb = seq // tb
    rmap = (lambda i: (nb - 1 - i, 0)) if reverse else (lambda i: (i, 0))
    tmap = lambda i: (0,) + rmap(i)

    def row_spec(width):
        if isinstance(width, tuple):
            return pl.BlockSpec((width[0], tb, width[1]), tmap)
        return pl.BlockSpec((tb, width), rmap)

    def row_shape(width):
        return (width[0], seq, width[1]) if isinstance(width, tuple) else (seq, width)

    in_specs = [row_spec(a.shape[1] if a.ndim == 2 else (a.shape[0], a.shape[2])) for a in rows]
    in_specs += [pl.BlockSpec(a.shape, functools.partial(_zero_map, a.ndim), pipeline_mode=pl.Buffered(1))
                 for a in consts]
    out_specs = [row_spec(c) for c, _ in row_outs] + [ANY] * len(acc_outs)
    out_shape = [SDS(row_shape(c), dt) for c, dt in row_outs] + [SDS(s, dt) for s, dt in acc_outs]
    n_main = len(rows) + len(consts) + len(row_outs)
    n_acc = len(acc_outs)

    def fn(*refs):
        main, acc_hbm, rest = refs[:n_main], refs[n_main:n_main + n_acc], refs[n_main + n_acc:]
        acc_vmem, own = rest[:n_acc], rest[n_acc:]
        body(*main, *acc_vmem, *own)

        @pl.when(pl.program_id(0) == nb - 1)
        def _():
            for src, dst in zip(acc_vmem, acc_hbm):
                pltpu.sync_copy(src, dst)

    buffers = [pltpu.VMEM(s, dt) for s, dt in acc_outs] + list(scratch)
    return _fused_call(name, fn if acc_outs else body, (nb,), in_specs, out_specs, out_shape, buffers,
                       [*rows, *consts], exchange, _params(1, vmem))


def _inproj_fwd(x, g1, w_in, tb, exchange=None):
    seq = x.shape[0]

    def body(x_ref, g_ref, w_ref, h_ref, q_ref, k_ref, v_ref, u_ref, ga_ref, gs_ref):
        h, _, _ = _rms(x_ref[...], g_ref[...])
        hb = _bf(h)
        h_ref[...] = hb
        pj = _mm_nt(hb, w_ref[...])
        q_ref[...] = _bf(pj[:, SPLITS[0]:SPLITS[1]])
        k_ref[...] = _bf(pj[:, SPLITS[1]:SPLITS[2]])
        v_ref[...] = _bf(pj[:, SPLITS[2]:SPLITS[3]])
        u_ref[...] = pj[:, SPLITS[3]:SPLITS[4]]
        ga_ref[...] = pj[:, SPLITS[4]:SPLITS[5]]
        gs_ref[...] = pj[:, SPLITS[5]:SPLITS[6]]

    return _rowcall("inproj_fwd", body, seq, tb, [x], [g1, w_in],
                    [(D_MODEL, BF16), (ATTN_W, BF16), (KV_W, BF16), (KV_W, BF16), (SSM_W, F32),
                     (D_MODEL, F32), (D_MODEL, F32)], [], vmem=VMEM_BIG, exchange=exchange)


def _inproj_bwd(x, dx2, dq, dk, dv, du, dga, dgs, g1, w_in, tb, exchange=None):
    seq = x.shape[0]

    def body(x_ref, dx2_ref, dq_ref, dk_ref, dv_ref, du_ref, dga_ref, dgs_ref, g_ref, w_ref,
             dx_ref, dpj_ref, dg_ref):
        @pl.when(pl.program_id(0) == 0)
        def _():
            dg_ref[...] = jnp.zeros_like(dg_ref)

        dpj = jnp.concatenate([dq_ref[...], dk_ref[...], dv_ref[...], _bf(du_ref[...]),
                               dga_ref[...], dgs_ref[...]], axis=1)
        dpj_ref[...] = dpj
        dh = _mm(dpj, w_ref[...])
        g = g_ref[...]
        _, xh, r = _rms(x_ref[...], g)
        dxn, dg = _rms_bwd(dh, xh, r, g)
        dx_ref[...] = dx2_ref[...] + dxn
        dg_ref[...] += dg

    return _rowcall("inproj_bwd", body, seq, tb, [x, dx2, dq, dk, dv, du, dga, dgs], [g1, w_in],
                    [(D_MODEL, F32), (IN_W, BF16)], [((1, D_MODEL), F32)], vmem=VMEM_BIG, exchange=exchange)


def _bucket_table():
    qi = np.arange(BLOCK)[:, None]
    kj = np.arange(2 * BLOCK)[None, :]
    dist = qi + BLOCK - kj
    max_exact = N_BUCKETS // 2
    d = np.maximum(dist, 0)
    df = np.maximum(d, 1).astype(np.float32)
    large = max_exact + (np.log(df / np.float32(max_exact)) / np.float32(math.log(MAX_DISTANCE / max_exact))
                         * np.float32(N_BUCKETS - max_exact)).astype(np.int32)
    large = np.minimum(large, N_BUCKETS - 1)
    bucket = np.where(d < max_exact, d, large)
    valid = (dist >= 0) & (dist < BLOCK)
    return np.where(valid, bucket, -1).astype(np.int32)


def _bias_table(rel_bias, bucket):
    def body(rb_ref, bk_ref, o_ref):
        bk = bk_ref[...]
        has_prev = lax.broadcasted_iota(jnp.int32, bk.shape, 1) >= BLOCK
        for h in range(N_HEADS):
            kh, j, par = h // Q_GROUP, (h // 2) % 2, h % 2
            acc = jnp.full((BLOCK, 2 * BLOCK), NEG_INF, F32)
            for b in range(N_BUCKETS):
                acc = jnp.where(bk == b, rb_ref[b, h], acc)
            o_ref[0, kh, par, :, j * BLOCK:(j + 1) * BLOCK] = jnp.where(has_prev, acc, NEG_INF).T
            o_ref[1, kh, par, :, j * BLOCK:(j + 1) * BLOCK] = acc.T

    return pl.pallas_call(
        body, out_shape=SDS((2, N_KV, 2, 2 * BLOCK, 2 * BLOCK), F32),
        in_specs=[pl.BlockSpec(memory_space=pltpu.SMEM), pl.BlockSpec(memory_space=pltpu.VMEM)],
        out_specs=pl.BlockSpec(memory_space=pltpu.VMEM), name="bias_table",
    )(rel_bias, bucket)


def _bias_grad(dbias, bucket):
    def body(db_ref, bk_ref, o_ref):
        bk = bk_ref[...]
        for h in range(N_HEADS):
            kh, j, par = h // Q_GROUP, (h // 2) % 2, h % 2
            db = db_ref[kh, par, :, j * BLOCK:(j + 1) * BLOCK].T
            for b in range(N_BUCKETS):
                o_ref[b, h] = jnp.sum(jnp.where(bk == b, db, 0.0))

    return pl.pallas_call(
        body, out_shape=SDS((N_BUCKETS, N_HEADS), F32),
        in_specs=[pl.BlockSpec(memory_space=pltpu.VMEM), pl.BlockSpec(memory_space=pltpu.VMEM)],
        out_specs=pl.BlockSpec(memory_space=pltpu.SMEM), name="bias_grad",
    )(dbias, bucket)


TILE = 2 * HEAD_DIM


def _pair_layout(t):
    lead = t.shape[:-3]
    t = t.reshape(lead + (N_KV, 2, 2) + t.shape[-2:])
    nl = len(lead)
    t = jnp.transpose(t, tuple(range(nl)) + (nl, nl + 2, nl + 1, nl + 3, nl + 4))
    return t.reshape(lead + (N_KV, 2, 2 * BLOCK, t.shape[-1]))


def _pair_unlayout(t):
    t = t.reshape(N_KV, 2, 2, BLOCK, t.shape[-1]).transpose(0, 2, 1, 3, 4)
    return t.reshape(N_HEADS, BLOCK, t.shape[-1])


def _halves(t):
    tf = t.astype(F32)
    low = lax.broadcasted_iota(jnp.int32, tf.shape, 1) < HEAD_DIM
    swapped = pltpu.roll(tf, HEAD_DIM, 1)
    zero = jnp.zeros_like(tf)
    return ((_bf(jnp.where(low, tf, zero)), _bf(jnp.where(low, zero, swapped))),
            (_bf(jnp.where(low, swapped, zero)), _bf(jnp.where(low, zero, tf))))


def _fold_halves(even, odd):
    low = lax.broadcasted_iota(jnp.int32, even.shape, 1) < HEAD_DIM
    comb = jnp.where(low, even, odd)
    return comb + pltpu.roll(comb, HEAD_DIM, 1)


def _tile_rows(ref, kh):
    return jnp.concatenate([ref[:, (2 * kh) * TILE:(2 * kh + 1) * TILE],
                            ref[:, (2 * kh + 1) * TILE:(2 * kh + 2) * TILE]], axis=0)


def _halves_t(t):
    tt = t.astype(F32).T
    top = lax.broadcasted_iota(jnp.int32, tt.shape, 0) < HEAD_DIM
    swapped = jnp.concatenate([tt[HEAD_DIM:], tt[:HEAD_DIM]], axis=0)
    zero = jnp.zeros_like(tt)
    return ((_bf(jnp.where(top, tt, zero)), _bf(jnp.where(top, zero, swapped))),
            (_bf(jnp.where(top, swapped, zero)), _bf(jnp.where(top, zero, tt))))


def _attn_probs(km, qk, bias, sink):
    lg = _mm_nt(km, qk) * (HEAD_DIM ** -0.5) + bias
    m = jnp.maximum(jnp.max(lg, axis=0, keepdims=True), sink)
    p = jnp.exp(lg - m)
    es = jnp.exp(sink - m)
    inv = 1.0 / (jnp.sum(p, axis=0, keepdims=True) + es)
    return p * inv, es * inv


def _attn_fwd(q, k, v, bias, sink_rows, exchange=None):
    seq = q.shape[0]
    nblk = seq // BLOCK

    def body(q_ref, kp_ref, kc_ref, vp_ref, vc_ref, b_ref, s_ref, o_ref):
        which = jnp.minimum(pl.program_id(0), 1)
        kms = _halves(jnp.concatenate([kp_ref[...], kc_ref[...]], axis=0))
        vts = _halves_t(jnp.concatenate([vp_ref[...], vc_ref[...]], axis=0))
        for kh in range(N_KV):
            qk = _tile_rows(q_ref, kh)
            acc = jnp.zeros((TILE, 2 * BLOCK), F32)
            for par in range(2):
                pr, _ = _attn_probs(kms[kh][par], qk, b_ref[which, kh, par], s_ref[kh, par])
                acc = acc + _mm(vts[kh][par], _bf(pr))
            acc = acc.T
            o_ref[:, (2 * kh) * TILE:(2 * kh + 1) * TILE] = _bf(acc[:BLOCK])
            o_ref[:, (2 * kh + 1) * TILE:(2 * kh + 2) * TILE] = _bf(acc[BLOCK:])

    cur = lambda n: (n, 0)
    prev = lambda n: (jnp.maximum(n - 1, 0), 0)
    return _fused_call(
        "attn_fwd", body, (nblk,),
        [pl.BlockSpec((BLOCK, ATTN_W), cur),
         pl.BlockSpec((BLOCK, KV_W), prev), pl.BlockSpec((BLOCK, KV_W), cur),
         pl.BlockSpec((BLOCK, KV_W), prev), pl.BlockSpec((BLOCK, KV_W), cur),
         pl.BlockSpec(bias.shape, functools.partial(_zero_map, bias.ndim)),
         pl.BlockSpec(sink_rows.shape, functools.partial(_zero_map, sink_rows.ndim))],
        [pl.BlockSpec((BLOCK, ATTN_W), cur)], [SDS((seq, ATTN_W), BF16)], [],
        [q, k, k, v, v, bias, sink_rows], exchange, _params(1))


def _attn_bwd(q, k, v, d_out, bias, sink_rows, exchange=None):
    seq = q.shape[0]
    nblk = seq // BLOCK

    def body(q_ref, kp_ref, kc_ref, vp_ref, vc_ref, do_ref, b_ref, s_ref,
             dq_ref, dk_ref, dv_ref, db_ref, ds_ref, ck_ref, cv_ref):
        n = pl.program_id(0)

        @pl.when(n == 0)
        def _():
            db_ref[...] = jnp.zeros_like(db_ref)
            ds_ref[...] = jnp.zeros_like(ds_ref)
            ck_ref[...] = jnp.zeros_like(ck_ref)
            cv_ref[...] = jnp.zeros_like(cv_ref)

        @pl.when(n < nblk)
        def _():
            which = jnp.minimum(n, 1)
            scale = HEAD_DIM ** -0.5
            kcat = jnp.concatenate([kp_ref[...], kc_ref[...]], axis=0)
            kms = _halves(kcat)
            kts = _halves_t(kcat)
            vms = _halves(jnp.concatenate([vp_ref[...], vc_ref[...]], axis=0))
            dks, dvs = [], []
            for kh in range(N_KV):
                qk = _tile_rows(q_ref, kh)
                dok = _tile_rows(do_ref, kh)
                dq = jnp.zeros((TILE, 2 * BLOCK), F32)
                dkp, dvp = [], []
                for par in range(2):
                    pr, ps = _attn_probs(kms[kh][par], qk, b_ref[which, kh, par], s_ref[kh, par])
                    dp = _mm_nt(vms[kh][par], dok)
                    rs = jnp.sum(pr * dp, axis=0, keepdims=True)
                    dlg = pr * (dp - rs)
                    ds_ref[kh, par] += -ps * rs
                    db_ref[kh, par] += dlg
                    dlb = _bf(dlg)
                    dq = dq + _mm(kts[kh][par], dlb)
                    dkp.append(_mm(dlb, qk))
                    dvp.append(_mm(_bf(pr), dok))
                dq = _bf((dq * scale).T)
                dq_ref[:, (2 * kh) * TILE:(2 * kh + 1) * TILE] = dq[:BLOCK]
                dq_ref[:, (2 * kh + 1) * TILE:(2 * kh + 2) * TILE] = dq[BLOCK:]
                dks.append(_fold_halves(*dkp))
                dvs.append(_fold_halves(*dvp))
            low = lax.broadcasted_iota(jnp.int32, (2 * BLOCK, TILE), 1) < HEAD_DIM
            dkk = jnp.where(low, dks[0], dks[1]) * scale
            dvv = jnp.where(low, dvs[0], dvs[1])
            dk_ref[...] = _bf(ck_ref[...] + dkk[:BLOCK])
            ck_ref[...] = dkk[BLOCK:]
            dv_ref[...] = _bf(cv_ref[...] + dvv[:BLOCK])
            cv_ref[...] = dvv[BLOCK:]

        @pl.when(n == nblk)
        def _():
            dk_ref[...] = _bf(ck_ref[...])
            dv_ref[...] = _bf(cv_ref[...])

    cur = lambda n: (jnp.minimum(n, nblk - 1), 0)
    prev = lambda n: (jnp.maximum(jnp.minimum(n, nblk - 1) - 1, 0), 0)
    late = lambda n: (jnp.maximum(n - 1, 0), 0)
    kv_spec = lambda m: pl.BlockSpec((BLOCK, KV_W), m)
    acc_b = pl.BlockSpec(bias.shape[1:], functools.partial(_zero_map, bias.ndim - 1))
    acc_s = pl.BlockSpec(sink_rows.shape, functools.partial(_zero_map, sink_rows.ndim))
    return _fused_call(
        "attn_bwd", body, (nblk + 1,),
        [pl.BlockSpec((BLOCK, ATTN_W), cur), kv_spec(prev), kv_spec(cur), kv_spec(prev), kv_spec(cur),
         pl.BlockSpec((BLOCK, ATTN_W), cur),
         pl.BlockSpec(bias.shape, functools.partial(_zero_map, bias.ndim)), acc_s],
        [pl.BlockSpec((BLOCK, ATTN_W), cur), kv_spec(late), kv_spec(late), acc_b, acc_s],
        [SDS((seq, ATTN_W), BF16), SDS((seq, KV_W), BF16), SDS((seq, KV_W), BF16),
         SDS(bias.shape[1:], F32), SDS(sink_rows.shape, F32)],
        [pltpu.VMEM((BLOCK, KV_W), F32), pltpu.VMEM((BLOCK, KV_W), F32)],
        [q, k, k, v, v, d_out, bias, sink_rows], exchange, _params(1))


def _ssm_discretize(lam_re, lam_im, log_dt, b_re, b_im):
    dt = jnp.exp(log_dt)[:, None]
    mag = jnp.exp(lam_re * dt)
    ab_re = mag * jnp.cos(lam_im * dt)
    ab_im = mag * jnp.sin(lam_im * dt)
    nr = ab_re - 1.0
    den = lam_re * lam_re + lam_im * lam_im
    f_re = (nr * lam_re + ab_im * lam_im) / den
    f_im = (ab_im * lam_re - nr * lam_im) / den
    bb_re = f_re[..., None] * b_re - f_im[..., None] * b_im
    bb_im = f_re[..., None] * b_im + f_im[..., None] * b_re
    return ab_re, ab_im, bb_re, bb_im


def _state_layout(re, im):
    lead = re.shape[:-2]
    z = jnp.stack([re, im], axis=-3).reshape(lead + (2, N_SUPER, GROUPS_PER_SUPER, SSM_STATE))
    return jnp.moveaxis(z, -4, -3).reshape(lead + (STATE_COLS,))


def _state_unlayout(vec):
    z = vec.reshape(N_SUPER, 2, GROUPS_PER_SUPER, SSM_STATE).transpose(1, 0, 2, 3)
    z = z.reshape(2, SSM_GROUPS, SSM_STATE)
    return z[0], z[1]


SEG = 4
WINDOW = SEG * SUBLANES


def _scan_tables(ab_re, ab_im):
    pw = [None, (ab_re, ab_im)]
    for _ in range(2, WINDOW + 1):
        pr, pi_ = pw[-1]
        pw.append((pr * ab_re - pi_ * ab_im, pr * ab_im + pi_ * ab_re))
    fwd = np.zeros((7, SUBLANES), np.int64)
    bwd = np.zeros((7, SUBLANES), np.int64)
    for k, shift in enumerate((1, 2, 4)):
        fwd[k] = [SEG * shift if r >= shift else 0 for r in range(SUBLANES)]
        bwd[k] = [SEG * shift if r < SUBLANES - shift else 0 for r in range(SUBLANES)]
    fwd[3] = [SEG * (r + 1) for r in range(SUBLANES)]
    bwd[3] = [SEG * (SUBLANES - r) for r in range(SUBLANES)]
    for k in range(1, SEG):
        fwd[3 + k] = bwd[3 + k] = k
    used = sorted((set(fwd.ravel()) | set(bwd.ravel())) - {0})
    select = lambda which: np.stack([(which == p) for p in used], axis=-1).astype(np.float32)
    stacked = _state_layout(jnp.stack([pw[p][0] for p in used]), jnp.stack([pw[p][1] for p in used]))
    conj_sign = np.where((np.arange(STATE_COLS) // SUPER_HALF) % 2 == 1, -1.0, 1.0).astype(np.float32)
    pick = functools.partial(jnp.einsum, 'krp,pc->krc', precision=lax.Precision.HIGHEST)
    return pick(select(fwd), stacked), pick(select(bwd), stacked) * conj_sign


_EYE = np.eye(GROUPS_PER_SUPER, dtype=np.float32)


def _b_matrix(bb_re, bb_im):
    bb = jnp.stack([bb_re, bb_im]).reshape(2, N_SUPER, GROUPS_PER_SUPER, SSM_STATE, SSM_GROUP)
    m = jnp.einsum('rsgpc,gh->sgcrhp', bb, _EYE)
    return m.reshape(N_SUPER, SUPER_IN, SUPER_W)


def _b_matrix_grad(dm):
    d = dm.reshape(N_SUPER, GROUPS_PER_SUPER, SSM_GROUP, 2, GROUPS_PER_SUPER, SSM_STATE)
    d = jnp.sum(d * _EYE[None, :, None, None, :, None], axis=4)
    d = d.transpose(3, 0, 1, 4, 2).reshape(2, SSM_GROUPS, SSM_STATE, SSM_GROUP)
    return d[0], d[1]


def _c_matrix(c_re, c_im):
    cc = jnp.stack([c_re, -c_im]).reshape(2, N_SUPER, GROUPS_PER_SUPER, SSM_GROUP, SSM_STATE)
    m = jnp.einsum('rsgcp,gh->srgphc', cc, _EYE)
    return m.reshape(N_SUPER, SUPER_W, SUPER_IN)


def _c_matrix_grad(dm):
    d = dm.reshape(N_SUPER, 2, GROUPS_PER_SUPER, SSM_STATE, GROUPS_PER_SUPER, SSM_GROUP)
    d = jnp.sum(d * _EYE[None, None, :, None, :, None], axis=4)
    d = d.transpose(1, 0, 2, 4, 3).reshape(2, SSM_GROUPS, SSM_GROUP, SSM_STATE)
    return d[0], -d[1]


def _cmul_add(xr, xi, ar, ai, sr, si):
    return xr + ar * sr - ai * si, xi + ar * si + ai * sr


def _scan_rows(buf_ref, tab_ref, carry_ref, n_windows, reverse, h_ref=None, da_ref=None):
    order = list(range(SEG - 1, -1, -1)) if reverse else list(range(SEG))
    near = SUBLANES - 1 if reverse else 0
    far = 0 if reverse else SUBLANES - 1
    s_in = SUBLANES - 1 if reverse else 1
    lanes = lambda tile: pl.ds(tile * LANES, LANES)

    def window(w0, tile_re, tile_im, c_re, c_im, acc):
        rows = lambda t: pl.ds(w0 + t, SUBLANES, stride=SEG)
        get = lambda ref, t: (ref.at[tile_re][rows(t), :], ref.at[tile_im][rows(t), :])
        tab = lambda k: (tab_ref[k, :, lanes(tile_re)], tab_ref[k, :, lanes(tile_im)])

        def put(t, xr, xi):
            buf_ref.at[tile_re][rows(t), :] = xr
            buf_ref.at[tile_im][rows(t), :] = xi

        a1 = tab(4)
        er, ei = get(buf_ref, order[0])
        for t in order[1:]:
            er, ei = _cmul_add(*get(buf_ref, t), *a1, er, ei)
            if t != order[-1]:
                put(t, er, ei)
        for k, shift in enumerate((1, 2, 4)):
            s = (SUBLANES - shift) if reverse else shift
            er, ei = _cmul_add(er, ei, *tab(k), pltpu.roll(er, s, 0), pltpu.roll(ei, s, 0))
        er, ei = _cmul_add(er, ei, *tab(3), c_re, c_im)
        put(order[-1], er, ei)
        sub = lax.broadcasted_iota(jnp.int32, er.shape, 0)
        in_re = jnp.where(sub == near, c_re, pltpu.roll(er, s_in, 0))
        in_im = jnp.where(sub == near, c_im, pltpu.roll(ei, s_in, 0))
        true = {order[-1]: (er, ei)}
        for idx, t in enumerate(order[:-1]):
            true[t] = _cmul_add(*get(buf_ref, t), *tab(4 + idx), in_re, in_im)
            put(t, *true[t])
        carry = (jnp.broadcast_to(er[far:far + 1], er.shape), jnp.broadcast_to(ei[far:far + 1], ei.shape))
        if acc is None:
            return carry, None
        acc_re, acc_im = acc
        for t in range(SEG):
            if t + 1 < SEG:
                gr, gim = true[t + 1]
            else:
                gr = jnp.where(sub == SUBLANES - 1, c_re, pltpu.roll(true[0][0], SUBLANES - 1, 0))
                gim = jnp.where(sub == SUBLANES - 1, c_im, pltpu.roll(true[0][1], SUBLANES - 1, 0))
            hr, hi = get(h_ref, t)
            acc_re = acc_re + gr * hr + gim * hi
            acc_im = acc_im + gim * hr - gr * hi
        return carry, (acc_re, acc_im)

    half = SUPER_HALF // LANES
    per = 2 if h_ref is None else 4
    for sb in range(N_SUPER):
        pairs = [(2 * half * sb + j, 2 * half * sb + half + j) for j in range(half)]

        def step(wi, state, pairs=pairs):
            w = (n_windows - 1 - wi) if reverse else wi
            w0 = pl.multiple_of(w * WINDOW, WINDOW)
            out = []
            for j, (tile_re, tile_im) in enumerate(pairs):
                mine = state[per * j:per * (j + 1)]
                carry, acc = window(w0, tile_re, tile_im, mine[0], mine[1], mine[2:] or None)
                out += list(carry) + list(acc or ())
            return tuple(out)

        init = []
        for tile_re, tile_im in pairs:
            init += [carry_ref[:, lanes(tile_re)], carry_ref[:, lanes(tile_im)]]
            if h_ref is not None:
                init += [da_ref[:, lanes(tile_re)], da_ref[:, lanes(tile_im)]]
        fin = lax.fori_loop(0, n_windows, step, tuple(init))
        for j, (tile_re, tile_im) in enumerate(pairs):
            carry_ref[:, lanes(tile_re)] = fin[per * j]
            carry_ref[:, lanes(tile_im)] = fin[per * j + 1]
            if h_ref is not None:
                da_ref[:, lanes(tile_re)] = fin[per * j + 2]
                da_ref[:, lanes(tile_im)] = fin[per * j + 3]


def _put_tiles(ref, sb, block):
    for j in range(SUPER_TILES):
        ref[sb * SUPER_TILES + j] = block[:, j * LANES:(j + 1) * LANES]


def _get_tiles(ref, sb):
    return jnp.concatenate([ref[sb * SUPER_TILES + j] for j in range(SUPER_TILES)], axis=1)


def _ssm_fwd(u, bmat, cmat, tab, d_skip, tb, exchange=None):
    seq = u.shape[0]

    def body(u_ref, b_ref, c_ref, t_ref, d_ref, s_ref, h_ref, carry_ref):
        @pl.when(pl.program_id(0) == 0)
        def _():
            carry_ref[...] = jnp.zeros_like(carry_ref)

        u_blk = u_ref[...]
        ub = _bf(u_blk)
        for sb in range(N_SUPER):
            _put_tiles(h_ref, sb, _mm(ub[:, sb * SUPER_IN:(sb + 1) * SUPER_IN], b_ref[sb]))
        _scan_rows(h_ref, t_ref, carry_ref, tb // WINDOW, False)
        ys = [_mm(_bf(_get_tiles(h_ref, sb)), c_ref[sb]) for sb in range(N_SUPER)]
        s_ref[...] = jnp.concatenate(ys, axis=1) + d_ref[...] * u_blk

    return _rowcall("ssm_fwd", body, seq, tb, [u], [bmat, cmat, tab, d_skip],
                    [(SSM_W, F32), ((STATE_TILES, LANES), F32)], [],
                    scratch=[pltpu.VMEM((SUBLANES, STATE_COLS), F32)], vmem=VMEM_BIG, exchange=exchange)


def _ssm_bwd(ds, u, h, bmat_t, cmat_t, tab, d_skip, tb, exchange=None):
    seq = u.shape[0]

    def body(ds_ref, u_ref, h_ref, bt_ref, ct_ref, t_ref, d_ref,
             du_ref, db_ref, dc_ref, da_ref, dd_ref, g_ref, carry_ref):
        @pl.when(pl.program_id(0) == 0)
        def _():
            carry_ref[...] = jnp.zeros_like(carry_ref)
            db_ref[...] = jnp.zeros_like(db_ref)
            dc_ref[...] = jnp.zeros_like(dc_ref)
            da_ref[...] = jnp.zeros_like(da_ref)
            dd_ref[...] = jnp.zeros_like(dd_ref)

        ds_blk = ds_ref[...]
        dsb = _bf(ds_blk)
        u_blk = u_ref[...]
        ub = _bf(u_blk)
        for sb in range(N_SUPER):
            _put_tiles(g_ref, sb, _mm(dsb[:, sb * SUPER_IN:(sb + 1) * SUPER_IN], ct_ref[sb]))
        _scan_rows(g_ref, t_ref, carry_ref, tb // WINDOW, True, h_ref=h_ref, da_ref=da_ref)
        dus = []
        for sb in range(N_SUPER):
            gb = _bf(_get_tiles(g_ref, sb))
            dus.append(_mm(gb, bt_ref[sb]))
            db_ref[sb] += _mm_tn(ub[:, sb * SUPER_IN:(sb + 1) * SUPER_IN], gb)
            dc_ref[sb] += _mm_tn(_bf(_get_tiles(h_ref, sb)), dsb[:, sb * SUPER_IN:(sb + 1) * SUPER_IN])
        du_ref[...] = jnp.concatenate(dus, axis=1) + d_ref[...] * ds_blk
        dd_ref[...] += jnp.sum(ds_blk * u_blk, axis=0, keepdims=True)

    return _rowcall("ssm_bwd", body, seq, tb, [ds, u, h], [bmat_t, cmat_t, tab, d_skip],
                    [(SSM_W, F32)],
                    [((N_SUPER, SUPER_IN, SUPER_W), F32), ((N_SUPER, SUPER_W, SUPER_IN), F32),
                     ((SUBLANES, STATE_COLS), F32), ((1, SSM_W), F32)],
                    scratch=[pltpu.VMEM((STATE_TILES, tb, LANES), F32), pltpu.VMEM((SUBLANES, STATE_COLS), F32)],
                    reverse=True, vmem=VMEM_BIG, exchange=exchange)


def _merge_core(s, attb, ga, gs, wg_ref, wab_ref, wsb_ref, wout_ref):
    zg, dgelu = _gelu_and_grad(s)
    zgb = _bf(zg)
    sg = _sig(_mm(zgb, wg_ref[...]))
    z = zg * sg
    zb = _bf(z)
    ys = jnp.concatenate([_mm(zb, wsb_ref[j]) for j in range(N_CHIPS)], axis=1)
    ya = jnp.concatenate([_mm(attb, wab_ref[j]) for j in range(N_CHIPS)], axis=1)
    sa = _sig(ga)
    ss = _sig(gs)
    mgb = _bf(sa * ya + ss * ys)
    o = _mm(mgb, wout_ref[...])
    return dict(zg=zg, dgelu=dgelu, zgb=zgb, sg=sg, zb=zb, ys=ys, ya=ya, sa=sa, ss=ss, mgb=mgb, o=o)


def _merge_fwd(x, s, att, ga, gs, g2, w_glu, w_ab, w_sb, w_out, tb):
    seq = x.shape[0]

    def body(x_ref, s_ref, att_ref, ga_ref, gs_ref, g_ref, wg_ref, wab_ref, wsb_ref, wout_ref, x2_ref):
        f = _merge_core(s_ref[...], att_ref[...], ga_ref[...], gs_ref[...], wg_ref, wab_ref, wsb_ref, wout_ref)
        n, _, _ = _rms(f["o"], g_ref[...])
        x2_ref[...] = x_ref[...] + n

    return _rowcall("merge_fwd", body, seq, tb, [x, s, att, ga, gs], [g2, w_glu, w_ab, w_sb, w_out],
                    [(D_MODEL, F32)], [], vmem=VMEM_BIG)[0]


def _merge_bwd(dx2, s, att, ga, gs, g2, w_glu, w_ab, w_sb, w_out, tb, exchange=None):
    seq = s.shape[0]
    cw = D_MODEL // N_CHIPS
    last = seq // tb - 1

    def body(dx2_ref, s_ref, att_ref, ga_ref, gs_ref, g_ref, wg_ref, wab_ref, wsb_ref, wout_ref,
             ds_ref, datt_ref, dga_ref, dgs_ref, dg_ref, dwg_ref, dwab_ref, dwsb_ref, dwout_ref,
             bwg_ref, bwab_ref, bwsb_ref, bwout_ref):
        @pl.when(pl.program_id(0) == 0)
        def _():
            for r in (dg_ref, dwg_ref, dwab_ref, dwsb_ref, dwout_ref):
                r[...] = jnp.zeros_like(r)

        attb = att_ref[...]
        f = _merge_core(s_ref[...], attb, ga_ref[...], gs_ref[...], wg_ref, wab_ref, wsb_ref, wout_ref)
        g = g_ref[...]
        _, oh, r2 = _rms(f["o"], g)
        do, dg = _rms_bwd(dx2_ref[...], oh, r2, g)
        dg_ref[...] += dg
        dob = _bf(do)
        dwout_ref[...] += _mm_tn(f["mgb"], dob)
        dmg = _mm_nt(dob, wout_ref[...])
        sa, ss = f["sa"], f["ss"]
        dyab = _bf(dmg * sa)
        dysb = _bf(dmg * ss)
        dga_ref[...] = _bf(dmg * f["ya"] * sa * (1.0 - sa))
        dgs_ref[...] = _bf(dmg * f["ys"] * ss * (1.0 - ss))
        dwab = _mm_tn(attb, dyab)
        dwsb = _mm_tn(f["zb"], dysb)
        datt = jnp.zeros((tb, ATTN_W), F32)
        dz = jnp.zeros((tb, SSM_W), F32)
        for j in range(N_CHIPS):
            dwab_ref[j] += dwab[:, j * cw:(j + 1) * cw]
            dwsb_ref[j] += dwsb[:, j * cw:(j + 1) * cw]
            datt = datt + _mm_nt(dyab[:, j * cw:(j + 1) * cw], wab_ref[j])
            dz = dz + _mm_nt(dysb[:, j * cw:(j + 1) * cw], wsb_ref[j])
        datt_ref[...] = _bf(datt)
        sg, zg = f["sg"], f["zg"]
        dglb = _bf(dz * zg * sg * (1.0 - sg))
        dwg_ref[...] += _mm_tn(f["zgb"], dglb)
        dzg = dz * sg + _mm_nt(dglb, wg_ref[...])
        ds_ref[...] = dzg * f["dgelu"]

        @pl.when(pl.program_id(0) == last)
        def _():
            for dst, src in ((bwg_ref, dwg_ref), (bwab_ref, dwab_ref), (bwsb_ref, dwsb_ref), (bwout_ref, dwout_ref)):
                dst[...] = _bf(src[...])

    shapes = [w_glu.shape, w_ab.shape, w_sb.shape, w_out.shape]
    return _rowcall("merge_bwd", body, seq, tb, [dx2, s, att, ga, gs], [g2, w_glu, w_ab, w_sb, w_out],
                    [(SSM_W, F32), (ATTN_W, BF16), (D_MODEL, BF16), (D_MODEL, BF16)],
                    [((1, D_MODEL), F32)] + [(sh, F32) for sh in shapes] + [(sh, BF16) for sh in shapes],
                    vmem=VMEM_BIG, exchange=exchange)


def _mlp_fwd_loss(x2, target, g3, g4, w_ffi, w_ffo, tb):
    seq = x2.shape[0]
    n_slab = len(w_ffi)
    sw = D_FF // FF_CHUNKS // n_slab

    def body(x2_ref, t_ref, g3_ref, g4_ref, *rest):
        wi_refs, (wo_ref, dy_ref, df_ref, h_ref, ra_ref, loss_ref, dg_ref) = rest[:n_slab], rest[n_slab:]

        @pl.when(pl.program_id(0) == 0)
        def _():
            loss_ref[...] = jnp.zeros_like(loss_ref)
            dg_ref[...] = jnp.zeros_like(dg_ref)

        x2_blk = x2_ref[...]
        h3, _, _ = _rms(x2_blk, g3_ref[...])
        hb = _bf(h3)
        h_ref[...] = hb
        f = jnp.zeros((tb, D_MODEL), F32)
        for j in range(FF_CHUNKS):
            for k in range(n_slab):
                ra = jnp.maximum(_mm(hb, wi_refs[k][j]), 0.0)
                ra_ref[:, pl.ds((j * n_slab + k) * sw, sw)] = _bf(ra)
                f = f + _mm(_bf(ra * ra), wo_ref[j, pl.ds(k * sw, sw), :])
        g4 = g4_ref[...]
        n4, fh, r4 = _rms(f, g4)
        e = (x2_blk + n4) - t_ref[...]
        loss_ref[...] += 0.5 * jnp.sum(jnp.mean(e * e, axis=-1, keepdims=True))
        dy = e * (1.0 / D_MODEL)
        dy_ref[...] = dy
        df, dg = _rms_bwd(dy, fh, r4, g4)
        df_ref[...] = _bf(df)
        dg_ref[...] += dg

    return _rowcall("mlp_fwd_loss", body, seq, tb, [x2, target], [g3, g4, *w_ffi, w_ffo],
                    [(D_MODEL, F32), (D_MODEL, BF16), (D_MODEL, BF16), (D_FF, BF16)],
                    [((SUBLANES, 128), F32), ((1, D_MODEL), F32)], vmem=VMEM_BIG)


def _mlp_bwd(x2, dy, df, ra, g3, w_ffi, w_ffo, tb):
    seq = x2.shape[0]
    n_slab = len(w_ffi)
    sw = D_FF // FF_CHUNKS // n_slab

    def body(x2_ref, dy_ref, df_ref, ra_ref, g3_ref, *rest):
        wi_refs, (wo_ref, dx_ref, da_ref, dg_ref) = rest[:n_slab], rest[n_slab:]

        @pl.when(pl.program_id(0) == 0)
        def _():
            dg_ref[...] = jnp.zeros_like(dg_ref)

        dfb = df_ref[...]
        dh = jnp.zeros((tb, D_MODEL), F32)
        for j in range(FF_CHUNKS):
            for k in range(n_slab):
                cols = pl.ds((j * n_slab + k) * sw, sw)
                ra = ra_ref[:, cols].astype(F32)
                dab = _bf(_mm_nt(dfb, wo_ref[j, pl.ds(k * sw, sw), :]) * (2.0 * ra))
                da_ref[:, cols] = dab
                dh = dh + _mm_nt(dab, wi_refs[k][j])
        g3 = g3_ref[...]
        _, xh, r3 = _rms(x2_ref[...], g3)
        dxn, dg = _rms_bwd(dh, xh, r3, g3)
        dx_ref[...] = dy_ref[...] + dxn
        dg_ref[...] += dg

    return _rowcall("mlp_bwd", body, seq, tb, [x2, dy, df, ra], [g3, *w_ffi, w_ffo],
                    [(D_MODEL, F32), (D_FF, BF16)], [((1, D_MODEL), F32)], vmem=VMEM_BIG)


def _matmul_tn(name, a, b, tk, tn, tl, chunk_major, exchange=None, square_a=False):
    seq, kdim = a.shape
    ndim = b.shape[1]
    last = seq // tl - 1

    def body(a_ref, b_ref, o_ref, ob_ref):
        @pl.when(pl.program_id(2) == 0)
        def _():
            o_ref[...] = jnp.zeros_like(o_ref)

        a_blk = a_ref[...]
        if square_a:
            a_blk = _bf(jnp.square(a_blk.astype(F32)))
        o_ref[...] += _mm_tn(a_blk, b_ref[...])

        @pl.when(pl.program_id(2) == last)
        def _():
            ob_ref[...] = _bf(o_ref[...])

    if chunk_major:
        shape = (ndim // tn, kdim, tn)
        out_spec = pl.BlockSpec((None, tk, tn), lambda k, n, l: (n, k, 0))
    else:
        shape = (kdim, ndim)
        out_spec = pl.BlockSpec((tk, tn), lambda k, n, l: (k, n))
    return _fused_call(
        name, body, (kdim // tk, ndim // tn, seq // tl),
        [pl.BlockSpec((tl, tk), lambda k, n, l: (l, k)), pl.BlockSpec((tl, tn), lambda k, n, l: (l, n))],
        [out_spec, out_spec], [SDS(shape, F32), SDS(shape, BF16)], [], [a, b], exchange, _params(3, VMEM_BIG))


def _ew_call(name, fn, ins, n_out, after=None):
    rows, cols = ins[0].shape
    tr = rows
    while tr * cols * 4 > min(1 << 20, (9 << 20) // (len(ins) + n_out)) and tr % 16 == 0:
        tr //= 2
    spec = pl.BlockSpec((tr, cols), lambda i: (i, 0))
    extra = [] if after is None else [after]

    def body(*refs):
        outs = fn(*[r[...] for r in refs[:len(ins)]])
        for r, o in zip(refs[len(ins) + len(extra):], outs):
            r[...] = o

    return pl.pallas_call(
        body, grid=(rows // tr,), in_specs=[spec] * len(ins) + [ANY] * len(extra), out_specs=[spec] * n_out,
        out_shape=[SDS((rows, cols), F32)] * n_out, name=name, compiler_params=_params(1))(*ins, *extra)


def _adam_math(w, g, m, v):
    m2 = ADAM_B1 * m + (1.0 - ADAM_B1) * g
    v2 = ADAM_B2 * v + (1.0 - ADAM_B2) * (g * g)
    m_hat = m2 / (1.0 - ADAM_B1 ** ADAM_STEP)
    v_hat = v2 / (1.0 - ADAM_B2 ** ADAM_STEP)
    delta = -ADAM_LR * (m_hat / (jnp.sqrt(v_hat) + ADAM_EPS) + ADAM_WD * w)
    return delta, m2, v2


def _sum4(name, own, recv, idx):
    _, rows, cols = own.shape
    tr = rows
    while tr * cols * 4 > (1 << 20) and tr % 16 == 0:
        tr //= 2

    def body(idx_ref, o_ref, r0_ref, r1_ref, r2_ref, out_ref):
        out_ref[...] = ((o_ref[...] + r0_ref[...].astype(F32)) + r1_ref[...].astype(F32)) + r2_ref[...].astype(F32)

    blk = (None, tr, cols)
    grid_spec = pltpu.PrefetchScalarGridSpec(
        num_scalar_prefetch=1, grid=(rows // tr,),
        in_specs=[pl.BlockSpec(blk, lambda i, s: (s[0], i, 0)), pl.BlockSpec(blk, lambda i, s: (0, i, 0)),
                  pl.BlockSpec(blk, lambda i, s: (1, i, 0)), pl.BlockSpec(blk, lambda i, s: (2, i, 0))],
        out_specs=pl.BlockSpec((tr, cols), lambda i, s: (i, 0)))
    return pl.pallas_call(body, grid_spec=grid_spec, out_shape=SDS((rows, cols), F32), name=name,
                          compiler_params=_params(1))(jnp.reshape(idx, (1,)).astype(jnp.int32), own, recv, recv, recv)


def _adam_pair(name, item, after=None):
    def fn(w_, a, b, m_, v_):
        g = a + b
        return (g,) + _adam_math(w_, g, m_, v_)

    return _ew_call(name, fn, list(item), 4, after)


def _place():
    return lax.axis_index("x"), lax.axis_index("y"), lax.axis_index("c")


def _other_chips(x, y):
    return [(1 - x, y), (x, 1 - y), (1 - x, 1 - y)]


HBM = pl.BlockSpec(memory_space=pltpu.HBM)
SEM = pl.BlockSpec(memory_space=pltpu.SEMAPHORE)
DATAFLOW = pltpu.SideEffectType.DATAFLOW_SIDE_EFFECTING


class _Flight:
    def __init__(self, copies, n_copies, send, recv, srcs, lands, token):
        self.copies, self.n, self.send, self.recv = copies, n_copies, send, recv
        self.srcs, self.lands, self.token = list(srcs), list(lands), token


def _take_off(name, srcs, lands, copies, n_copies, after):
    n_s, n_l = len(srcs), len(lands)

    def body(*refs):
        src, land = refs[:n_s], refs[n_s:n_s + n_l]
        send, recv = refs[n_s + n_l + 1:n_s + n_l + 3]
        for cp in copies(src, land, send, recv):
            cp.start()
        refs[-1][...] = jnp.zeros_like(refs[-1])

    mem = lambda t: pltpu.HBM(t.shape, t.dtype)
    sems = pltpu.SemaphoreType.DMA((n_copies,))
    outs = pl.pallas_call(
        body, name=name,
        out_shape=(sems, sems, *map(mem, srcs), *map(mem, lands), SDS((SUBLANES, LANES), F32)),
        in_specs=[HBM] * (n_s + n_l) + [ANY],
        out_specs=(SEM, SEM, *[HBM] * (n_s + n_l), pl.BlockSpec(memory_space=pltpu.VMEM)),
        input_output_aliases={i: 2 + i for i in range(n_s + n_l)},
        compiler_params=pltpu.CompilerParams(has_side_effects=DATAFLOW),
    )(*[pltpu.with_memory_space_constraint(t, pltpu.HBM) for t in (*srcs, *lands)], after)
    return _Flight(copies, n_copies, outs[0], outs[1], outs[2:2 + n_s], outs[2 + n_s:2 + n_s + n_l], outs[-1])


def _land(name, flight, after):
    n_s, n_l = len(flight.srcs), len(flight.lands)

    def body(*refs):
        src, land = refs[:n_s], refs[n_s:n_s + n_l]
        send, recv = refs[n_s + n_l:n_s + n_l + 2]
        for cp in flight.copies(src, land, send, recv):
            cp.wait_send()
            cp.wait_recv()

    mem = lambda t: pltpu.HBM(t.shape, t.dtype)
    outs = pl.pallas_call(
        body, name=name, out_shape=(*map(mem, flight.srcs), *map(mem, flight.lands)),
        in_specs=[HBM] * (n_s + n_l) + [SEM, SEM, ANY], out_specs=tuple([HBM] * (n_s + n_l)),
        input_output_aliases={i: i for i in range(n_s + n_l)},
        compiler_params=pltpu.CompilerParams(has_side_effects=DATAFLOW),
    )(*flight.srcs, *flight.lands, flight.send, flight.recv, after)
    return list(outs[:n_s]), list(outs[n_s:])


def _empty_like(shapes_from, lead):
    return [lax.empty((lead,) + t.shape[1:], t.dtype) for t in shapes_from]


def _scatter_off(name, chunks, after):
    def copies(src, land, send, recv):
        x, y, c = _place()
        return [pltpu.make_async_remote_copy(
            src_ref=src[a].at[2 * px + py], dst_ref=land[a].at[k], send_sem=send.at[3 * a + k],
            recv_sem=recv.at[3 * a + k], device_id=(px, py, c), device_id_type=MESH_ID)
            for a in range(len(chunks)) for k, (px, py) in enumerate(_other_chips(x, y))]

    return _take_off(name, chunks, _empty_like(chunks, 3), copies, 3 * len(chunks), after)


def _swap_off(name, arrs, after):
    def copies(src, land, send, recv):
        x, y, c = _place()
        return [pltpu.make_async_remote_copy(
            src_ref=src[a], dst_ref=land[a], send_sem=send.at[a], recv_sem=recv.at[a],
            device_id=(x, y, 1 - c), device_id_type=MESH_ID) for a in range(len(arrs))]

    return _take_off(name, arrs, [lax.empty(t.shape, t.dtype) for t in arrs], copies, len(arrs), after)


def _devices_off(name, block, after):
    me = 4 * lax.axis_index("x") + 2 * lax.axis_index("y") + lax.axis_index("c")
    land = lax.dynamic_update_index_in_dim(lax.empty((N_DEV,) + block.shape, block.dtype), block, me, 0)

    def copies(src, land, send, recv):
        x, y, c = _place()
        mine = 4 * x + 2 * y + c
        return [pltpu.make_async_remote_copy(
            src_ref=src[0], dst_ref=land[0].at[mine], send_sem=send.at[k - 1], recv_sem=recv.at[k - 1],
            device_id=(x ^ (k >> 2), y ^ ((k >> 1) & 1), c ^ (k & 1)), device_id_type=MESH_ID)
            for k in range(1, N_DEV)]

    return _take_off(name, [block], [land], copies, N_DEV - 1, after)


def _half_rows(shape, c, other=False):
    half = shape[0] // 2
    return pl.ds(((1 - c) if other else c) * half, half)


def _gather_start(name, shards, lands, after):
    n = len(shards)

    def body(*refs):
        src, land, (send, recv) = refs[:n], refs[n:2 * n], refs[2 * n + 1:2 * n + 3]
        x, y, c = _place()
        me = 2 * x + y
        for a in range(n):
            mine = _half_rows(shards[a].shape, c)
            for j, (px, py) in enumerate(_other_chips(x, y)):
                pltpu.make_async_remote_copy(
                    src_ref=src[a].at[mine], dst_ref=land[a].at[me, mine], send_sem=send.at[3 * a + j],
                    recv_sem=recv.at[3 * a + j], device_id=(px, py, c), device_id_type=MESH_ID).start()
        token = refs[-1]
        token[...] = jnp.zeros_like(token)

    mem = lambda t: pltpu.HBM(t.shape, t.dtype)
    pair = pltpu.SemaphoreType.DMA((3 * n,))
    outs = pl.pallas_call(
        body, name=name,
        out_shape=(pair, pair, *map(mem, shards), *map(mem, lands), SDS((SUBLANES, LANES), F32)),
        in_specs=[HBM] * (2 * n) + [ANY],
        out_specs=(SEM, SEM, *[HBM] * (2 * n), pl.BlockSpec(memory_space=pltpu.VMEM)),
        input_output_aliases={i: 2 + i for i in range(2 * n)},
        compiler_params=pltpu.CompilerParams(has_side_effects=DATAFLOW),
    )(*[pltpu.with_memory_space_constraint(t, pltpu.HBM) for t in (*shards, *lands)], after)
    return outs[0], outs[1], list(outs[2:2 + n]), list(outs[2 + n:2 + 2 * n]), outs[-1]


def _gather_pass(name, send, recv, shards, lands, after):
    n = len(shards)

    def body(*refs):
        src, land, (send, recv, _) = refs[:n], refs[n:2 * n], refs[2 * n:2 * n + 3]
        fsend, frecv = refs[2 * n + 3], refs[2 * n + 4]
        x, y, c = _place()
        me = 2 * x + y
        for a in range(n):
            mine = _half_rows(shards[a].shape, c)
            for j, (px, py) in enumerate(_other_chips(x, y)):
                far = 2 * px + py
                ici = pltpu.make_async_remote_copy(
                    src_ref=src[a].at[mine], dst_ref=land[a].at[far, mine], send_sem=send.at[3 * a + j],
                    recv_sem=recv.at[3 * a + j], device_id=(px, py, c), device_id_type=MESH_ID)
                ici.wait_recv()
                ici.wait_send()
                pltpu.make_async_remote_copy(
                    src_ref=land[a].at[far, mine], dst_ref=land[a].at[far, mine], send_sem=fsend.at[3 * a + j],
                    recv_sem=frecv.at[3 * a + j], device_id=(x, y, 1 - c), device_id_type=MESH_ID).start()
        token = refs[-1]
        token[...] = jnp.zeros_like(token)

    mem = lambda t: pltpu.HBM(t.shape, t.dtype)
    pair = pltpu.SemaphoreType.DMA((3 * n,))
    outs = pl.pallas_call(
        body, name=name,
        out_shape=(pair, pair, *map(mem, lands), SDS((SUBLANES, LANES), F32)),
        in_specs=[HBM] * (2 * n) + [SEM, SEM, ANY],
        out_specs=(SEM, SEM, *[HBM] * n, pl.BlockSpec(memory_space=pltpu.VMEM)),
        input_output_aliases={n + i: 2 + i for i in range(n)},
        compiler_params=pltpu.CompilerParams(has_side_effects=DATAFLOW),
    )(*shards, *lands, send, recv, after)
    return outs[0], outs[1], list(outs[2:2 + n]), outs[-1]


def _gather_wait(name, fsend, frecv, lands, after):
    n = len(lands)

    def body(*refs):
        land, (fsend, frecv, _) = refs[:n], refs[n:n + 3]
        x, y, c = _place()
        for a in range(n):
            for j, (px, py) in enumerate(_other_chips(x, y)):
                far = 2 * px + py
                mine = _half_rows(lands[a].shape[1:], c)
                theirs = _half_rows(lands[a].shape[1:], c, other=True)
                pltpu.make_async_remote_copy(
                    src_ref=land[a].at[far, mine], dst_ref=land[a].at[far, mine], send_sem=fsend.at[3 * a + j],
                    recv_sem=frecv.at[3 * a + j], device_id=(x, y, 1 - c), device_id_type=MESH_ID).wait_send()
                pltpu.make_async_remote_copy(
                    src_ref=land[a].at[far, theirs], dst_ref=land[a].at[far, theirs], send_sem=fsend.at[3 * a + j],
                    recv_sem=frecv.at[3 * a + j], device_id=(x, y, 1 - c), device_id_type=MESH_ID).wait_recv()

    mem = lambda t: pltpu.HBM(t.shape, t.dtype)
    return list(pl.pallas_call(
        body, name=name, out_shape=tuple(map(mem, lands)), in_specs=[HBM] * n + [SEM, SEM, ANY],
        out_specs=tuple([HBM] * n), input_output_aliases={i: i for i in range(n)},
        compiler_params=pltpu.CompilerParams(has_side_effects=DATAFLOW),
    )(*lands, fsend, frecv, after))


def _after(token):
    return _Exchange([token], [], [], lambda *_: None, lambda *_: None)


def _swap_sibling(arrs):
    n = len(arrs)

    def copies(ins, outs, sems):
        send, recv = sems
        x, y, c = _place()
        return [pltpu.make_async_remote_copy(
            src_ref=ins[a], dst_ref=outs[a], send_sem=send.at[a], recv_sem=recv.at[a],
            device_id=(x, y, 1 - c), device_id_type=MESH_ID) for a in range(n)]

    def start(ins, outs, sems):
        for cp in copies(ins, outs, sems):
            cp.start()

    def wait(ins, outs, sems):
        cps = copies(ins, outs, sems)
        for cp in cps:
            cp.wait_recv()
        for cp in cps:
            cp.wait_send()

    return _Exchange(arrs, [SDS(s.shape, s.dtype) for s in arrs],
                     [pltpu.SemaphoreType.DMA((n,)), pltpu.SemaphoreType.DMA((n,))], start, wait)


def _sum_devices(slots):
    def body(s_ref, o_ref):
        acc = s_ref[0]
        for d in range(1, N_DEV):
            acc = acc + s_ref[d]
        o_ref[...] = acc

    return pl.pallas_call(
        body, in_specs=[pl.BlockSpec(memory_space=pltpu.VMEM)], out_specs=pl.BlockSpec(memory_space=pltpu.VMEM),
        out_shape=SDS(slots.shape[1:], F32), name="sum_small",
        compiler_params=pltpu.CompilerParams(vmem_limit_bytes=32 * 1024 * 1024))(slots)


def _adam_small(ws, gs, ms, vs):
    n = len(ws)

    def body(*refs):
        for i in range(n):
            w_ref, g_ref, m_ref, v_ref = (refs[k * n + i] for k in range(4))
            outs = _adam_math(w_ref[...], g_ref[...], m_ref[...], v_ref[...])
            for k in range(3):
                refs[(4 + k) * n + i][...] = outs[k]

    vmem = pl.BlockSpec(memory_space=pltpu.VMEM)
    return pl.pallas_call(
        body, in_specs=[vmem] * (4 * n), out_specs=[vmem] * (3 * n),
        out_shape=[SDS(w.shape, F32) for w in ws] * 3, name="adam_small",
        compiler_params=pltpu.CompilerParams(vmem_limit_bytes=32 * 1024 * 1024))(*ws, *gs, *ms, *vs)


def _local_step(x, target, small, big, tb, distributed):
    dist = distributed
    me = (2 * lax.axis_index("x") + lax.axis_index("y")) if dist else 0
    tb_ssm = min(tb, 256)
    bucket = jnp.asarray(_bucket_table())
    place_own = lambda t: lax.dynamic_update_index_in_dim(lax.empty((N_CHIPS,) + t.shape, t.dtype), t, me, 0)
    if dist:
        in_legs = _gather_start("gather_in_start", [big["w_in"]], [place_own(big["w_in"])], small["d_skip"])
        names = sorted(small)
        in_token, values = lax.optimization_barrier((in_legs[4], [small[n] for n in names]))
        small = dict(zip(names, values))
    g1, g2, g3, g4 = small["norm_mix_pre"], small["norm_mix_post"], small["norm_mlp_pre"], small["norm_mlp_post"]

    keys_first = lambda t: jnp.swapaxes(t, -1, -2)
    bias = _bias_table(small["rel_bias"], bucket)
    sink_rows = keys_first(_pair_layout(jnp.broadcast_to(small["sinks"].reshape(N_HEADS, 1, 1), (N_HEADS, BLOCK, 1))))
    disc_args = (small["lam_re"], small["lam_im"], small["log_dt"], small["b_re"], small["b_im"])
    (ab_re, ab_im, bb_re, bb_im), disc_vjp = jax.vjp(_ssm_discretize, *disc_args)
    tab_f, tab_b = _scan_tables(ab_re, ab_im)
    bmat = _bf(_b_matrix(bb_re, bb_im))
    cmat = _bf(_c_matrix(small["c_re"], small["c_im"]))
    d_skip = small["d_skip"]

    mix = ("w_glu", "w_attn_branch", "w_ssm_branch", "w_out")
    rest = [big[n] for n in mix + ("w_ff_in", "w_ff_out")]
    if dist:
        send, recv, src, lands, _ = in_legs
        rest_lands = [place_own(t) for t in rest]
        corner = lambda t: t.reshape(-1, t.shape[-1])[:1, :LANES].astype(F32)
        prepared = sum(map(corner, [tab_b, bias, sink_rows, bmat, cmat] + rest_lands), in_token[:1])
        send, recv, lands, in_passed = _gather_pass("gather_in_pass", send, recv, src, lands, prepared)
        (g_in,) = _gather_wait("gather_in_wait", send, recv, lands, in_passed)
        w_in = g_in.reshape(IN_W, D_MODEL)
    else:
        w_in = big["w_in"]
    token = None
    if dist:
        send, recv, rest, lands, token = _gather_start("gather_rest_start", rest, rest_lands, in_passed)
    h1, q, k, v, u, ga, gs = _inproj_fwd(x, g1, w_in, tb, _after(token) if dist else None)
    s, h = _ssm_fwd(u, bmat, cmat, tab_f, d_skip, tb)
    if dist:
        send, recv, lands, token = _gather_pass("gather_rest_pass", send, recv, rest, lands, s)
    att = _attn_fwd(q, k, v, bias, sink_rows, _after(token) if dist else None)[0]
    if dist:
        rest = _gather_wait("gather_rest_wait", send, recv, lands, att)
    w_glu, w_ab, w_sb, w_out, w_ffi, w_ffo = rest
    w_glu = w_glu.reshape(SSM_W, SSM_W)
    w_out = w_out.reshape(D_MODEL, D_MODEL)
    w_ffi = [w_ffi]
    x2 = _merge_fwd(x, s, att, ga, gs, g2, w_glu, w_ab, w_sb, w_out, tb)
    dy, df, h3, ra, loss_acc, dg4 = _mlp_fwd_loss(x2, target, g3, g4, w_ffi, w_ffo, tb)

    dx2, da, dg3 = _mlp_bwd(x2, dy, df, ra, g3, w_ffi, w_ffo, tb)
    tl = min(2048, x.shape[0])
    chunked = (N_CHIPS, D_FF // N_CHIPS, D_MODEL)
    d_ffi, b_ffi = _matmul_tn("grad_w_ff_in", h3, da, D_MODEL, D_FF // FF_CHUNKS, tl, True)
    d_ffo, b_ffo = _matmul_tn("grad_w_ff_out", ra, df, D_FF // FF_CHUNKS, D_MODEL, tl, False, square_a=True)
    d_ffo, b_ffo = d_ffo.reshape(chunked), b_ffo.reshape(chunked)
    behind = lambda flight: _after(flight.token) if dist else None
    ff_fl = _scatter_off("scatter_ff_off", [b_ffi, b_ffo], d_ffo) if dist else None
    outs = _merge_bwd(dx2, s, att, ga, gs, g2, w_glu, w_ab, w_sb, w_out, tb_ssm, behind(ff_fl))
    ds, datt, dga, dgs, dg2, d_glu, d_ab, d_sb, d_out, b_glu, b_ab, b_sb, b_out = outs
    glu4, out4 = (N_CHIPS, SSM_W // N_CHIPS, SSM_W), (N_CHIPS, D_MODEL // N_CHIPS, D_MODEL)
    d_mix = [d_glu.reshape(glu4), d_ab, d_sb, d_out.reshape(out4)]
    b_mix = [b_glu.reshape(glu4), b_ab, b_sb, b_out.reshape(out4)]
    mix_fl = _scatter_off("scatter_mix_off", b_mix, d_mix[-1]) if dist else None
    du, d_bmat, d_cmat, da_acc, dd_skip = _ssm_bwd(
        ds, u, h, bmat.transpose(0, 2, 1), cmat.transpose(0, 2, 1), tab_b, d_skip, tb, behind(mix_fl))
    dq, dk, dv, dbias, dsink_rows = _attn_bwd(q, k, v, datt, bias, sink_rows)
    swap_fl = None
    if dist:
        r_ffi, r_ffo = _land("scatter_ff_land", ff_fl, dq)[1]
        p_ffi = _sum4("sum_w_ff_in", d_ffi, r_ffi, me)
        p_ffo = _sum4("sum_w_ff_out", d_ffo, r_ffo, me)
        swap_fl = _swap_off("swap_ff_off", [p_ffi, p_ffo], r_ffo)
    dx, dpj, dg1 = _inproj_bwd(x, dx2, dq, dk, dv, du, dga, dgs, g1, w_in, tb, behind(swap_fl))

    dab_re, dab_im = _state_unlayout(jnp.sum(da_acc, axis=0))
    dbb_re, dbb_im = _b_matrix_grad(d_bmat)
    d_lam_re, d_lam_im, d_log_dt, d_b_re, d_b_im = disc_vjp((dab_re, dab_im, dbb_re, dbb_im))
    d_c_re, d_c_im = _c_matrix_grad(d_cmat)
    d_rel = _bias_grad(dbias, bucket)
    d_sinks = jnp.sum(_pair_unlayout(keys_first(dsink_rows)), axis=(1, 2))
    small_grads = dict(
        norm_mix_pre=dg1, norm_mix_post=dg2, norm_mlp_pre=dg3, norm_mlp_post=dg4, rel_bias=d_rel, sinks=d_sinks,
        lam_re=d_lam_re, lam_im=d_lam_im, log_dt=d_log_dt, b_re=d_b_re, b_im=d_b_im, c_re=d_c_re, c_im=d_c_im,
        d_skip=dd_skip)
    small_fl = _devices_off("small_off", _pack(small_grads, loss_acc), swap_fl.token) if dist else None
    outs = _matmul_tn("grad_w_in", dpj, h1, IN_W // 2, D_MODEL, tl, False, behind(small_fl))
    in4 = (N_CHIPS, IN_W // N_CHIPS, D_MODEL)
    d_in, b_in = outs[0].reshape(in4), outs[1].reshape(in4)
    if not dist:
        return loss_acc, dx, small_grads, dict(zip(BIG, [d_in] + d_mix + [d_ffi, d_ffo]))
    (p_ffi, p_ffo), (s_ffi, s_ffo) = _land("swap_ff_land", swap_fl, b_in)
    r_mix = _land("scatter_mix_land", mix_fl, b_in)[1]
    p_mix = [_sum4("sum_" + n, d, r, me) for n, d, r in zip(mix, d_mix, r_mix)]
    pending = dict(d_in=d_in, b_in=b_in, p_mix=p_mix, w_ff_in=(p_ffi, s_ffi), w_ff_out=(p_ffo, s_ffo), me=me)
    return loss_acc, dx, small_fl, pending


SMALL = ['norm_mix_pre', 'norm_mix_post', 'norm_mlp_pre', 'norm_mlp_post', 'rel_bias', 'sinks', 'lam_re', 'lam_im',
         'log_dt', 'b_re', 'b_im', 'c_re', 'c_im', 'd_skip']
BIG = ['w_in', 'w_glu', 'w_attn_branch', 'w_ssm_branch', 'w_out', 'w_ff_in', 'w_ff_out']
WEIGHTS = ['norm_mix_pre', 'norm_mix_post', 'norm_mlp_pre', 'norm_mlp_post', 'w_in', 'rel_bias', 'sinks', 'lam_re',
           'lam_im', 'log_dt', 'b_re', 'b_im', 'c_re', 'c_im', 'd_skip', 'w_glu', 'w_attn_branch', 'w_ssm_branch',
           'w_out', 'w_ff_in', 'w_ff_out']
PACK_COLS = 1024
PACK_ORDER = ['b_re', 'b_im', 'c_re', 'c_im', 'lam_re', 'lam_im', 'norm_mix_pre', 'norm_mix_post', 'norm_mlp_pre',
              'norm_mlp_post', 'rel_bias', 'sinks', 'log_dt', 'd_skip']


STATE_MINOR = ('b_re', 'b_im')
PACK_ROWS = 144
LOSS_ROW = 140


def _pack(named, loss_acc):
    parts = []
    for n in PACK_ORDER:
        a = jnp.swapaxes(named[n], -1, -2) if n in STATE_MINOR else named[n]
        flat = a.reshape(-1)
        rows = -(-flat.shape[0] // PACK_COLS)
        parts.append(jnp.pad(flat, (0, rows * PACK_COLS - flat.shape[0])).reshape(rows, PACK_COLS))
    assert sum(p.shape[0] for p in parts) == LOSS_ROW
    parts.append(jnp.pad(loss_acc[0:1], ((0, PACK_ROWS - LOSS_ROW - 1), (0, PACK_COLS - loss_acc.shape[1]))))
    return jnp.concatenate(parts, axis=0)


def _unpack(packed, shapes):
    out, at = {}, 0
    for n in PACK_ORDER:
        shape = shapes[n][:-2] + (shapes[n][-1], shapes[n][-2]) if n in STATE_MINOR else shapes[n]
        size = int(np.prod(shape))
        rows = -(-size // PACK_COLS)
        blk = packed[at:at + rows]
        out[n] = (blk.reshape(-1)[:size] if size % PACK_COLS else blk).reshape(shape)
        at += rows
    return out


def kernel(x, norm_mix_pre, norm_mix_post, norm_mlp_pre, norm_mlp_post, w_in, rel_bias, sinks, lam_re, lam_im, log_dt, b_re, b_im, c_re, c_im, d_skip, w_glu, w_attn_branch, w_ssm_branch, w_out, w_ff_in, w_ff_out, loss_target, m_norm_mix_pre, m_norm_mix_post, m_norm_mlp_pre, m_norm_mlp_post, m_w_in, m_rel_bias, m_sinks, m_lam_re, m_lam_im, m_log_dt, m_b_re, m_b_im, m_c_re, m_c_im, m_d_skip, m_w_glu, m_w_attn_branch, m_w_ssm_branch, m_w_out, m_w_ff_in, m_w_ff_out, v_norm_mix_pre, v_norm_mix_post, v_norm_mlp_pre, v_norm_mlp_post, v_w_in, v_rel_bias, v_sinks, v_lam_re, v_lam_im, v_log_dt, v_b_re, v_b_im, v_c_re, v_c_im, v_d_skip, v_w_glu, v_w_attn_branch, v_w_ssm_branch, v_w_out, v_w_ff_in, v_w_ff_out):
    env = dict(locals())
    w = {n: env[n] for n in WEIGHTS}
    m = {n: env["m_" + n] for n in WEIGHTS}
    v = {n: env["v_" + n] for n in WEIGHTS}
    seq = x.shape[1]
    tb = min(512, seq)

    small = {n: w[n] for n in ('norm_mix_pre', 'norm_mix_post', 'norm_mlp_pre', 'norm_mlp_post', 'rel_bias')}
    small.update({n: w[n][0] for n in ('sinks', 'lam_re', 'lam_im', 'log_dt', 'b_re', 'b_im', 'c_re', 'c_im')})
    small['d_skip'] = w['d_skip']
    shard = lambda t, n: t[n][0].T if n == 'w_in' else t[n][0]
    unshard = lambda a, n: (a.T if n == 'w_in' else a)[None]
    _, dx, small_fl, pending = _local_step(
        x[0], loss_target[0], small, {n: _bf(shard(w, n)) for n in BIG}, tb, True)

    grads, deltas, new_m, new_v = {}, {}, {}, {}

    def adam(n, partials, after=None):
        outs = _adam_pair("adam_" + n, (shard(w, n), *partials, shard(m, n), shard(v, n)), after)
        grads[n], deltas[n], new_m[n], new_v[n] = [unshard(a, n) for a in outs]
        return outs[3]

    mix = ("w_glu", "w_attn_branch", "w_ssm_branch", "w_out")
    in_fl = _scatter_off("scatter_w_in_off", [pending["b_in"]], pending["d_in"])
    sib_mix = _exchange_alone("swap_mix", _swap_sibling(pending["p_mix"]))
    last = in_fl.token
    for n, partials in [(n, pending[n]) for n in ("w_ff_in", "w_ff_out")] + list(zip(mix, zip(pending["p_mix"], sib_mix))):
        last = adam(n, partials, last)

    small_g = _sum_devices(_land("small_land", small_fl, last)[1][0])
    loss = small_g[LOSS_ROW, 0]
    minor = lambda t, n: jnp.swapaxes(t, -1, -2) if n in STATE_MINOR else t
    g_small = _unpack(small_g, {n: w[n].shape for n in SMALL})
    outs = _adam_small([minor(w[n], n) for n in SMALL], [g_small[n] for n in SMALL],
                       [minor(m[n], n) for n in SMALL], [minor(v[n], n) for n in SMALL])
    grads.update({n: minor(g_small[n], n) for n in SMALL})
    for k, dst in enumerate((deltas, new_m, new_v)):
        dst.update({n: minor(a, n) for n, a in zip(SMALL, outs[k * len(SMALL):(k + 1) * len(SMALL)])})

    (r_in,) = _land("scatter_w_in_land", in_fl, outs[0])[1]
    p_in = _sum4("sum_w_in", pending["d_in"], r_in, pending["me"])
    (s_in,) = _exchange_alone("swap_w_in", _swap_sibling([p_in]))
    adam("w_in", (p_in, s_in))

    return (loss, dx[None], *[grads[n] for n in WEIGHTS], *[deltas[n] for n in WEIGHTS],
            *[new_m[n] for n in WEIGHTS], *[new_v[n] for n in WEIGHTS])
```

```python
import functools
import math

import numpy as np
import jax
import jax.numpy as jnp
from jax import lax
from jax.experimental import pallas as pl
from jax.experimental.pallas import tpu as pltpu

F32 = jnp.float32
BF16 = jnp.bfloat16

D_MODEL = 1024
N_HEADS = 8
N_KV = 2
Q_GROUP = 4
HEAD_DIM = 64
ATTN_W = 512
KV_W = 128
BLOCK = 128
N_BUCKETS = 32
MAX_DISTANCE = 128
NEG_INF = -1e30
SSM_W = 512
SSM_GROUP = 16
SSM_GROUPS = 32
SSM_STATE = 64
N_SUPER = 4
GROUPS_PER_SUPER = SSM_GROUPS // N_SUPER
SUPER_IN = GROUPS_PER_SUPER * SSM_GROUP
SUPER_HALF = GROUPS_PER_SUPER * SSM_STATE
SUPER_W = 2 * SUPER_HALF
STATE_COLS = N_SUPER * SUPER_W
D_FF = 4096
FF_CHUNKS = 4
IN_W = 3328
SPLITS = (0, 512, 640, 768, 1280, 2304, 3328)
RMS_EPS = 1e-6
N_CHIPS = 4
N_DEV = 8
SUBLANES = 8
LANES = 128
STATE_TILES = STATE_COLS // LANES
SUPER_TILES = SUPER_W // LANES

ADAM_LR = 0.001
ADAM_B1 = 0.9
ADAM_B2 = 0.999
ADAM_EPS = 1e-08
ADAM_WD = 0.01
ADAM_STEP = 10

VMEM_BIG = 56 * 1024 * 1024
SDS = jax.ShapeDtypeStruct
MESH_ID = pl.DeviceIdType.MESH
ANY = pl.BlockSpec(memory_space=pl.ANY)


def _bf(x):
    return x.astype(BF16)


def _mm(a, b):
    return jnp.dot(a, b, preferred_element_type=F32)


def _mm_nt(a, b):
    return lax.dot_general(a, b, (((1,), (1,)), ((), ())), preferred_element_type=F32)


def _mm_tn(a, b):
    return lax.dot_general(a, b, (((0,), (0,)), ((), ())), preferred_element_type=F32)


def _sig(x):
    return 1.0 / (1.0 + jnp.exp(-x))


def _rms(x, g):
    r = lax.rsqrt(jnp.mean(x * x, axis=-1, keepdims=True) + RMS_EPS)
    xh = x * r
    return xh * g, xh, r


def _rms_bwd(dout, xh, r, g):
    dg = jnp.sum(dout * xh, axis=0, keepdims=True)
    dxh = dout * g
    dx = r * (dxh - xh * jnp.mean(dxh * xh, axis=-1, keepdims=True))
    return dx, dg


_GELU_C = math.sqrt(2.0 / math.pi)


def _gelu_and_grad(x):
    x2 = x * x
    inner = _GELU_C * (x + 0.044715 * (x2 * x))
    t = jnp.tanh(inner)
    y = 0.5 * x * (1.0 + t)
    dy = 0.5 * (1.0 + t) + 0.5 * x * (1.0 - t * t) * (_GELU_C * (1.0 + 3.0 * 0.044715 * x2))
    return y, dy


def _zero_map(nd, *_):
    return (0,) * nd


def _params(n_axes, vmem=None):
    return pltpu.CompilerParams(dimension_semantics=("arbitrary",) * n_axes, vmem_limit_bytes=vmem)


class _Exchange:
    def __init__(self, ins, outs, sems, start, wait):
        self.ins, self.outs, self.sems, self.start, self.wait = list(ins), list(outs), list(sems), start, wait


def _fused_call(name, body, grid, in_specs, out_specs, out_shape, scratch, args, exchange, params):
    n_in, n_out, n_scr = len(in_specs), len(out_specs), len(scratch)
    if exchange is None:
        fn = body
    else:
        ex = exchange
        n_xi, n_xo = len(ex.ins), len(ex.outs)

        def fn(*refs):
            at = 0
            parts = []
            for n in (n_in, n_xi, n_out, n_xo, n_scr, len(ex.sems)):
                parts.append(refs[at:at + n])
                at += n
            ins, x_in, outs, x_out, scr, x_sem = parts
            ids = [pl.program_id(a) for a in range(len(grid))]
            first = functools.reduce(jnp.logical_and, [i == 0 for i in ids])
            last = functools.reduce(jnp.logical_and, [i == g - 1 for i, g in zip(ids, grid)])

            @pl.when(first)
            def _():
                ex.start(x_in, x_out, x_sem)

            body(*ins, *outs, *scr)

            @pl.when(last)
            def _():
                ex.wait(x_in, x_out, x_sem)

        in_specs = list(in_specs) + [ANY] * n_xi
        out_specs = list(out_specs) + [ANY] * n_xo
        out_shape = list(out_shape) + ex.outs
        scratch = list(scratch) + ex.sems
        args = list(args) + ex.ins
    return pl.pallas_call(fn, grid=grid, in_specs=in_specs, out_specs=out_specs, out_shape=out_shape,
                          scratch_shapes=list(scratch), name=name, compiler_params=params)(*args)


def _exchange_alone(name, ex):
    def body(*refs):
        n_xi, n_xo = len(ex.ins), len(ex.outs)
        x_in, x_out, x_sem = refs[:n_xi], refs[n_xi:n_xi + n_xo], refs[n_xi + n_xo:]
        ex.start(x_in, x_out, x_sem)
        ex.wait(x_in, x_out, x_sem)

    return pl.pallas_call(body, in_specs=[ANY] * len(ex.ins), out_specs=[ANY] * len(ex.outs), out_shape=ex.outs,
                          scratch_shapes=ex.sems, name=name)(*ex.ins)


def _rowcall(name, body, seq, tb, rows, consts, row_outs, acc_outs, scratch=(), reverse=False, vmem=None,
             exchange=None):
    nb = seq // tb
    rmap = (lambda i: (nb - 1 - i, 0)) if reverse else (lambda i: (i, 0))
    tmap = lambda i: (0,) + rmap(i)

    def row_spec(width):
        if isinstance(width, tuple):
            return pl.BlockSpec((width[0], tb, width[1]), tmap)
        return pl.BlockSpec((tb, width), rmap)

    def row_shape(width):
        return (width[0], seq, width[1]) if isinstance(width, tuple) else (seq, width)

    in_specs = [row_spec(a.shape[1] if a.ndim == 2 else (a.shape[0], a.shape[2])) for a in rows]
    in_specs += [pl.BlockSpec(a.shape, functools.partial(_zero_map, a.ndim), pipeline_mode=pl.Buffered(1))
                 for a in consts]
    out_specs = [row_spec(c) for c, _ in row_outs] + [ANY] * len(acc_outs)
    out_shape = [SDS(row_shape(c), dt) for c, dt in row_outs] + [SDS(s, dt) for s, dt in acc_outs]
    n_main = len(rows) + len(consts) + len(row_outs)
    n_acc = len(acc_outs)

    def fn(*refs):
        main, acc_hbm, rest = refs[:n_main], refs[n_main:n_main + n_acc], refs[n_main + n_acc:]
        acc_vmem, own = rest[:n_acc], rest[n_acc:]
        body(*main, *acc_vmem, *own)

        @pl.when(pl.program_id(0) == nb - 1)
        def _():
            for src, dst in zip(acc_vmem, acc_hbm):
                pltpu.sync_copy(src, dst)

    buffers = [pltpu.VMEM(s, dt) for s, dt in acc_outs] + list(scratch)
    return _fused_call(name, fn if acc_outs else body, (nb,), in_specs, out_specs, out_shape, buffers,
                       [*rows, *consts], exchange, _params(1, vmem))


def _inproj_fwd(x, g1, w_in, tb, exchange=None):
    seq = x.shape[0]

    def body(x_ref, g_ref, w_ref, h_ref, q_ref, k_ref, v_ref, u_ref, ga_ref, gs_ref):
        h, _, _ = _rms(x_ref[...], g_ref[...])
        hb = _bf(h)
        h_ref[...] = hb
        pj = _mm_nt(hb, w_ref[...])
        q_ref[...] = _bf(pj[:, SPLITS[0]:SPLITS[1]])
        k_ref[...] = _bf(pj[:, SPLITS[1]:SPLITS[2]])
        v_ref[...] = _bf(pj[:, SPLITS[2]:SPLITS[3]])
        u_ref[...] = pj[:, SPLITS[3]:SPLITS[4]]
        ga_ref[...] = pj[:, SPLITS[4]:SPLITS[5]]
        gs_ref[...] = pj[:, SPLITS[5]:SPLITS[6]]

    return _rowcall("inproj_fwd", body, seq, tb, [x], [g1, w_in],
                    [(D_MODEL, BF16), (ATTN_W, BF16), (KV_W, BF16), (KV_W, BF16), (SSM_W, F32),
                     (D_MODEL, F32), (D_MODEL, F32)], [], vmem=VMEM_BIG, exchange=exchange)


def _inproj_bwd(x, dx2, dq, dk, dv, du, dga, dgs, g1, w_in, tb, exchange=None):
    seq = x.shape[0]

    def body(x_ref, dx2_ref, dq_ref, dk_ref, dv_ref, du_ref, dga_ref, dgs_ref, g_ref, w_ref,
             dx_ref, dpj_ref, dg_ref):
        @pl.when(pl.program_id(0) == 0)
        def _():
            dg_ref[...] = jnp.zeros_like(dg_ref)

        dpj = jnp.concatenate([dq_ref[...], dk_ref[...], dv_ref[...], _bf(du_ref[...]),
                               dga_ref[...], dgs_ref[...]], axis=1)
        dpj_ref[...] = dpj
        dh = _mm(dpj, w_ref[...])
        g = g_ref[...]
        _, xh, r = _rms(x_ref[...], g)
        dxn, dg = _rms_bwd(dh, xh, r, g)
        dx_ref[...] = dx2_ref[...] + dxn
        dg_ref[...] += dg

    return _rowcall("inproj_bwd", body, seq, tb, [x, dx2, dq, dk, dv, du, dga, dgs], [g1, w_in],
                    [(D_MODEL, F32), (IN_W, BF16)], [((1, D_MODEL), F32)], vmem=VMEM_BIG, exchange=exchange)


def _bucket_table():
    qi = np.arange(BLOCK)[:, None]
    kj = np.arange(2 * BLOCK)[None, :]
    dist = qi + BLOCK - kj
    max_exact = N_BUCKETS // 2
    d = np.maximum(dist, 0)
    df = np.maximum(d, 1).astype(np.float32)
    large = max_exact + (np.log(df / np.float32(max_exact)) / np.float32(math.log(MAX_DISTANCE / max_exact))
                         * np.float32(N_BUCKETS - max_exact)).astype(np.int32)
    large = np.minimum(large, N_BUCKETS - 1)
    bucket = np.where(d < max_exact, d, large)
    valid = (dist >= 0) & (dist < BLOCK)
    return np.where(valid, bucket, -1).astype(np.int32)


def _bias_table(rel_bias, bucket):
    def body(rb_ref, bk_ref, o_ref):
        bk = bk_ref[...]
        has_prev = lax.broadcasted_iota(jnp.int32, bk.shape, 1) >= BLOCK
        for h in range(N_HEADS):
            kh, j, par = h // Q_GROUP, (h // 2) % 2, h % 2
            acc = jnp.full((BLOCK, 2 * BLOCK), NEG_INF, F32)
            for b in range(N_BUCKETS):
                acc = jnp.where(bk == b, rb_ref[b, h], acc)
            o_ref[0, kh, par, :, j * BLOCK:(j + 1) * BLOCK] = jnp.where(has_prev, acc, NEG_INF).T
            o_ref[1, kh, par, :, j * BLOCK:(j + 1) * BLOCK] = acc.T

    return pl.pallas_call(
        body, out_shape=SDS((2, N_KV, 2, 2 * BLOCK, 2 * BLOCK), F32),
        in_specs=[pl.BlockSpec(memory_space=pltpu.SMEM), pl.BlockSpec(memory_space=pltpu.VMEM)],
        out_specs=pl.BlockSpec(memory_space=pltpu.VMEM), name="bias_table",
    )(rel_bias, bucket)


def _bias_grad(dbias, bucket):
    def body(db_ref, bk_ref, o_ref):
        bk = bk_ref[...]
        for h in range(N_HEADS):
            kh, j, par = h // Q_GROUP, (h // 2) % 2, h % 2
            db = db_ref[kh, par, :, j * BLOCK:(j + 1) * BLOCK].T
            for b in range(N_BUCKETS):
                o_ref[b, h] = jnp.sum(jnp.where(bk == b, db, 0.0))

    return pl.pallas_call(
        body, out_shape=SDS((N_BUCKETS, N_HEADS), F32),
        in_specs=[pl.BlockSpec(memory_space=pltpu.VMEM), pl.BlockSpec(memory_space=pltpu.VMEM)],
        out_specs=pl.BlockSpec(memory_space=pltpu.SMEM), name="bias_grad",
    )(dbias, bucket)


TILE = 2 * HEAD_DIM


def _pair_layout(t):
    lead = t.shape[:-3]
    t = t.reshape(lead + (N_KV, 2, 2) + t.shape[-2:])
    nl = len(lead)
    t = jnp.transpose(t, tuple(range(nl)) + (nl, nl + 2, nl + 1, nl + 3, nl + 4))
    return t.reshape(lead + (N_KV, 2, 2 * BLOCK, t.shape[-1]))


def _pair_unlayout(t):
    t = t.reshape(N_KV, 2, 2, BLOCK, t.shape[-1]).transpose(0, 2, 1, 3, 4)
    return t.reshape(N_HEADS, BLOCK, t.shape[-1])


def _halves(t):
    tf = t.astype(F32)
    low = lax.broadcasted_iota(jnp.int32, tf.shape, 1) < HEAD_DIM
    swapped = pltpu.roll(tf, HEAD_DIM, 1)
    zero = jnp.zeros_like(tf)
    return ((_bf(jnp.where(low, tf, zero)), _bf(jnp.where(low, zero, swapped))),
            (_bf(jnp.where(low, swapped, zero)), _bf(jnp.where(low, zero, tf))))


def _fold_halves(even, odd):
    low = lax.broadcasted_iota(jnp.int32, even.shape, 1) < HEAD_DIM
    comb = jnp.where(low, even, odd)
    return comb + pltpu.roll(comb, HEAD_DIM, 1)


def _tile_rows(ref, kh):
    return jnp.concatenate([ref[:, (2 * kh) * TILE:(2 * kh + 1) * TILE],
                            ref[:, (2 * kh + 1) * TILE:(2 * kh + 2) * TILE]], axis=0)


def _halves_t(t):
    tt = t.astype(F32).T
    top = lax.broadcasted_iota(jnp.int32, tt.shape, 0) < HEAD_DIM
    swapped = jnp.concatenate([tt[HEAD_DIM:], tt[:HEAD_DIM]], axis=0)
    zero = jnp.zeros_like(tt)
    return ((_bf(jnp.where(top, tt, zero)), _bf(jnp.where(top, zero, swapped))),
            (_bf(jnp.where(top, swapped, zero)), _bf(jnp.where(top, zero, tt))))


def _attn_probs(km, qk, bias, sink):
    lg = _mm_nt(km, qk) * (HEAD_DIM ** -0.5) + bias
    m = jnp.maximum(jnp.max(lg, axis=0, keepdims=True), sink)
    p = jnp.exp(lg - m)
    es = jnp.exp(sink - m)
    inv = 1.0 / (jnp.sum(p, axis=0, keepdims=True) + es)
    return p * inv, es * inv


def _attn_fwd(q, k, v, bias, sink_rows, exchange=None):
    seq = q.shape[0]
    nblk = seq // BLOCK

    def body(q_ref, kp_ref, kc_ref, vp_ref, vc_ref, b_ref, s_ref, o_ref):
        which = jnp.minimum(pl.program_id(0), 1)
        kms = _halves(jnp.concatenate([kp_ref[...], kc_ref[...]], axis=0))
        vts = _halves_t(jnp.concatenate([vp_ref[...], vc_ref[...]], axis=0))
        for kh in range(N_KV):
            qk = _tile_rows(q_ref, kh)
            acc = jnp.zeros((TILE, 2 * BLOCK), F32)
            for par in range(2):
                pr, _ = _attn_probs(kms[kh][par], qk, b_ref[which, kh, par], s_ref[kh, par])
                acc = acc + _mm(vts[kh][par], _bf(pr))
            acc = acc.T
            o_ref[:, (2 * kh) * TILE:(2 * kh + 1) * TILE] = _bf(acc[:BLOCK])
            o_ref[:, (2 * kh + 1) * TILE:(2 * kh + 2) * TILE] = _bf(acc[BLOCK:])

    cur = lambda n: (n, 0)
    prev = lambda n: (jnp.maximum(n - 1, 0), 0)
    return _fused_call(
        "attn_fwd", body, (nblk,),
        [pl.BlockSpec((BLOCK, ATTN_W), cur),
         pl.BlockSpec((BLOCK, KV_W), prev), pl.BlockSpec((BLOCK, KV_W), cur),
         pl.BlockSpec((BLOCK, KV_W), prev), pl.BlockSpec((BLOCK, KV_W), cur),
         pl.BlockSpec(bias.shape, functools.partial(_zero_map, bias.ndim)),
         pl.BlockSpec(sink_rows.shape, functools.partial(_zero_map, sink_rows.ndim))],
        [pl.BlockSpec((BLOCK, ATTN_W), cur)], [SDS((seq, ATTN_W), BF16)], [],
        [q, k, k, v, v, bias, sink_rows], exchange, _params(1))


def _attn_bwd(q, k, v, d_out, bias, sink_rows, exchange=None):
    seq = q.shape[0]
    nblk = seq // BLOCK

    def body(q_ref, kp_ref, kc_ref, vp_ref, vc_ref, do_ref, b_ref, s_ref,
             dq_ref, dk_ref, dv_ref, db_ref, ds_ref, ck_ref, cv_ref):
        n = pl.program_id(0)

        @pl.when(n == 0)
        def _():
            db_ref[...] = jnp.zeros_like(db_ref)
            ds_ref[...] = jnp.zeros_like(ds_ref)
            ck_ref[...] = jnp.zeros_like(ck_ref)
            cv_ref[...] = jnp.zeros_like(cv_ref)

        @pl.when(n < nblk)
        def _():
            which = jnp.minimum(n, 1)
            scale = HEAD_DIM ** -0.5
            kcat = jnp.concatenate([kp_ref[...], kc_ref[...]], axis=0)
            kms = _halves(kcat)
            kts = _halves_t(kcat)
            vms = _halves(jnp.concatenate([vp_ref[...], vc_ref[...]], axis=0))
            dks, dvs = [], []
            for kh in range(N_KV):
                qk = _tile_rows(q_ref, kh)
                dok = _tile_rows(do_ref, kh)
                dq = jnp.zeros((TILE, 2 * BLOCK), F32)
                dkp, dvp = [], []
                for par in range(2):
                    pr, ps = _attn_probs(kms[kh][par], qk, b_ref[which, kh, par], s_ref[kh, par])
                    dp = _mm_nt(vms[kh][par], dok)
                    rs = jnp.sum(pr * dp, axis=0, keepdims=True)
                    dlg = pr * (dp - rs)
                    ds_ref[kh, par] += -ps * rs
                    db_ref[kh, par] += dlg
                    dlb = _bf(dlg)
                    dq = dq + _mm(kts[kh][par], dlb)
                    dkp.append(_mm(dlb, qk))
                    dvp.append(_mm(_bf(pr), dok))
                dq = _bf((dq * scale).T)
                dq_ref[:, (2 * kh) * TILE:(2 * kh + 1) * TILE] = dq[:BLOCK]
                dq_ref[:, (2 * kh + 1) * TILE:(2 * kh + 2) * TILE] = dq[BLOCK:]
                dks.append(_fold_halves(*dkp))
                dvs.append(_fold_halves(*dvp))
            low = lax.broadcasted_iota(jnp.int32, (2 * BLOCK, TILE), 1) < HEAD_DIM
            dkk = jnp.where(low, dks[0], dks[1]) * scale
            dvv = jnp.where(low, dvs[0], dvs[1])
            dk_ref[...] = _bf(ck_ref[...] + dkk[:BLOCK])
            ck_ref[...] = dkk[BLOCK:]
            dv_ref[...] = _bf(cv_ref[...] + dvv[:BLOCK])
            cv_ref[...] = dvv[BLOCK:]

        @pl.when(n == nblk)
        def _():
            dk_ref[...] = _bf(ck_ref[...])
            dv_ref[...] = _bf(cv_ref[...])

    cur = lambda n: (jnp.minimum(n, nblk - 1), 0)
    prev = lambda n: (jnp.maximum(jnp.minimum(n, nblk - 1) - 1, 0), 0)
    late = lambda n: (jnp.maximum(n - 1, 0), 0)
    kv_spec = lambda m: pl.BlockSpec((BLOCK, KV_W), m)
    acc_b = pl.BlockSpec(bias.shape[1:], functools.partial(_zero_map, bias.ndim - 1))
    acc_s = pl.BlockSpec(sink_rows.shape, functools.partial(_zero_map, sink_rows.ndim))
    return _fused_call(
        "attn_bwd", body, (nblk + 1,),
        [pl.BlockSpec((BLOCK, ATTN_W), cur), kv_spec(prev), kv_spec(cur), kv_spec(prev), kv_spec(cur),
         pl.BlockSpec((BLOCK, ATTN_W), cur),
         pl.BlockSpec(bias.shape, functools.partial(_zero_map, bias.ndim)), acc_s],
        [pl.BlockSpec((BLOCK, ATTN_W), cur), kv_spec(late), kv_spec(late), acc_b, acc_s],
        [SDS((seq, ATTN_W), BF16), SDS((seq, KV_W), BF16), SDS((seq, KV_W), BF16),
         SDS(bias.shape[1:], F32), SDS(sink_rows.shape, F32)],
        [pltpu.VMEM((BLOCK, KV_W), F32), pltpu.VMEM((BLOCK, KV_W), F32)],
        [q, k, k, v, v, d_out, bias, sink_rows], exchange, _params(1))


def _ssm_discretize(lam_re, lam_im, log_dt, b_re, b_im):
    dt = jnp.exp(log_dt)[:, None]
    mag = jnp.exp(lam_re * dt)
    ab_re = mag * jnp.cos(lam_im * dt)
    ab_im = mag * jnp.sin(lam_im * dt)
    nr = ab_re - 1.0
    den = lam_re * lam_re + lam_im * lam_im
    f_re = (nr * lam_re + ab_im * lam_im) / den
    f_im = (ab_im * lam_re - nr * lam_im) / den
    bb_re = f_re[..., None] * b_re - f_im[..., None] * b_im
    bb_im = f_re[..., None] * b_im + f_im[..., None] * b_re
    return ab_re, ab_im, bb_re, bb_im


def _state_layout(re, im):
    lead = re.shape[:-2]
    z = jnp.stack([re, im], axis=-3).reshape(lead + (2, N_SUPER, GROUPS_PER_SUPER, SSM_STATE))
    return jnp.moveaxis(z, -4, -3).reshape(lead + (STATE_COLS,))


def _state_unlayout(vec):
    z = vec.reshape(N_SUPER, 2, GROUPS_PER_SUPER, SSM_STATE).transpose(1, 0, 2, 3)
    z = z.reshape(2, SSM_GROUPS, SSM_STATE)
    return z[0], z[1]


SEG = 4
WINDOW = SEG * SUBLANES


def _scan_tables(ab_re, ab_im):
    pw = [None, (ab_re, ab_im)]
    for _ in range(2, WINDOW + 1):
        pr, pi_ = pw[-1]
        pw.append((pr * ab_re - pi_ * ab_im, pr * ab_im + pi_ * ab_re))
    fwd = np.zeros((7, SUBLANES), np.int64)
    bwd = np.zeros((7, SUBLANES), np.int64)
    for k, shift in enumerate((1, 2, 4)):
        fwd[k] = [SEG * shift if r >= shift else 0 for r in range(SUBLANES)]
        bwd[k] = [SEG * shift if r < SUBLANES - shift else 0 for r in range(SUBLANES)]
    fwd[3] = [SEG * (r + 1) for r in range(SUBLANES)]
    bwd[3] = [SEG * (SUBLANES - r) for r in range(SUBLANES)]
    for k in range(1, SEG):
        fwd[3 + k] = bwd[3 + k] = k
    used = sorted((set(fwd.ravel()) | set(bwd.ravel())) - {0})
    select = lambda which: np.stack([(which == p) for p in used], axis=-1).astype(np.float32)
    stacked = _state_layout(jnp.stack([pw[p][0] for p in used]), jnp.stack([pw[p][1] for p in used]))
    conj_sign = np.where((np.arange(STATE_COLS) // SUPER_HALF) % 2 == 1, -1.0, 1.0).astype(np.float32)
    pick = functools.partial(jnp.einsum, 'krp,pc->krc', precision=lax.Precision.HIGHEST)
    return pick(select(fwd), stacked), pick(select(bwd), stacked) * conj_sign


_EYE = np.eye(GROUPS_PER_SUPER, dtype=np.float32)


def _b_matrix(bb_re, bb_im):
    bb = jnp.stack([bb_re, bb_im]).reshape(2, N_SUPER, GROUPS_PER_SUPER, SSM_STATE, SSM_GROUP)
    m = jnp.einsum('rsgpc,gh->sgcrhp', bb, _EYE)
    return m.reshape(N_SUPER, SUPER_IN, SUPER_W)


def _b_matrix_grad(dm):
    d = dm.reshape(N_SUPER, GROUPS_PER_SUPER, SSM_GROUP, 2, GROUPS_PER_SUPER, SSM_STATE)
    d = jnp.sum(d * _EYE[None, :, None, None, :, None], axis=4)
    d = d.transpose(3, 0, 1, 4, 2).reshape(2, SSM_GROUPS, SSM_STATE, SSM_GROUP)
    return d[0], d[1]


def _c_matrix(c_re, c_im):
    cc = jnp.stack([c_re, -c_im]).reshape(2, N_SUPER, GROUPS_PER_SUPER, SSM_GROUP, SSM_STATE)
    m = jnp.einsum('rsgcp,gh->srgphc', cc, _EYE)
    return m.reshape(N_SUPER, SUPER_W, SUPER_IN)


def _c_matrix_grad(dm):
    d = dm.reshape(N_SUPER, 2, GROUPS_PER_SUPER, SSM_STATE, GROUPS_PER_SUPER, SSM_GROUP)
    d = jnp.sum(d * _EYE[None, None, :, None, :, None], axis=4)
    d = d.transpose(1, 0, 2, 4, 3).reshape(2, SSM_GROUPS, SSM_GROUP, SSM_STATE)
    return d[0], -d[1]


def _cmul_add(xr, xi, ar, ai, sr, si):
    return xr + ar * sr - ai * si, xi + ar * si + ai * sr


def _scan_rows(buf_ref, tab_ref, carry_ref, n_windows, reverse, h_ref=None, da_ref=None):
    order = list(range(SEG - 1, -1, -1)) if reverse else list(range(SEG))
    near = SUBLANES - 1 if reverse else 0
    far = 0 if reverse else SUBLANES - 1
    s_in = SUBLANES - 1 if reverse else 1
    lanes = lambda tile: pl.ds(tile * LANES, LANES)

    def window(w0, tile_re, tile_im, c_re, c_im, acc):
        rows = lambda t: pl.ds(w0 + t, SUBLANES, stride=SEG)
        get = lambda ref, t: (ref.at[tile_re][rows(t), :], ref.at[tile_im][rows(t), :])
        tab = lambda k: (tab_ref[k, :, lanes(tile_re)], tab_ref[k, :, lanes(tile_im)])

        def put(t, xr, xi):
            buf_ref.at[tile_re][rows(t), :] = xr
            buf_ref.at[tile_im][rows(t), :] = xi

        a1 = tab(4)
        er, ei = get(buf_ref, order[0])
        for t in order[1:]:
            er, ei = _cmul_add(*get(buf_ref, t), *a1, er, ei)
            if t != order[-1]:
                put(t, er, ei)
        for k, shift in enumerate((1, 2, 4)):
            s = (SUBLANES - shift) if reverse else shift
            er, ei = _cmul_add(er, ei, *tab(k), pltpu.roll(er, s, 0), pltpu.roll(ei, s, 0))
        er, ei = _cmul_add(er, ei, *tab(3), c_re, c_im)
        put(order[-1], er, ei)
        sub = lax.broadcasted_iota(jnp.int32, er.shape, 0)
        in_re = jnp.where(sub == near, c_re, pltpu.roll(er, s_in, 0))
        in_im = jnp.where(sub == near, c_im, pltpu.roll(ei, s_in, 0))
        true = {order[-1]: (er, ei)}
        for idx, t in enumerate(order[:-1]):
            true[t] = _cmul_add(*get(buf_ref, t), *tab(4 + idx), in_re, in_im)
            put(t, *true[t])
        carry = (jnp.broadcast_to(er[far:far + 1], er.shape), jnp.broadcast_to(ei[far:far + 1], ei.shape))
        if acc is None:
            return carry, None
        acc_re, acc_im = acc
        for t in range(SEG):
            if t + 1 < SEG:
                gr, gim = true[t + 1]
            else:
                gr = jnp.where(sub == SUBLANES - 1, c_re, pltpu.roll(true[0][0], SUBLANES - 1, 0))
                gim = jnp.where(sub == SUBLANES - 1, c_im, pltpu.roll(true[0][1], SUBLANES - 1, 0))
            hr, hi = get(h_ref, t)
            acc_re = acc_re + gr * hr + gim * hi
            acc_im = acc_im + gim * hr - gr * hi
        return carry, (acc_re, acc_im)

    half = SUPER_HALF // LANES
    per = 2 if h_ref is None else 4
    for sb in range(N_SUPER):
        pairs = [(2 * half * sb + j, 2 * half * sb + half + j) for j in range(half)]

        def step(wi, state, pairs=pairs):
            w = (n_windows - 1 - wi) if reverse else wi
            w0 = pl.multiple_of(w * WINDOW, WINDOW)
            out = []
            for j, (tile_re, tile_im) in enumerate(pairs):
                mine = state[per * j:per * (j + 1)]
                carry, acc = window(w0, tile_re, tile_im, mine[0], mine[1], mine[2:] or None)
                out += list(carry) + list(acc or ())
            return tuple(out)

        init = []
        for tile_re, tile_im in pairs:
            init += [carry_ref[:, lanes(tile_re)], carry_ref[:, lanes(tile_im)]]
            if h_ref is not None:
                init += [da_ref[:, lanes(tile_re)], da_ref[:, lanes(tile_im)]]
        fin = lax.fori_loop(0, n_windows, step, tuple(init))
        for j, (tile_re, tile_im) in enumerate(pairs):
            carry_ref[:, lanes(tile_re)] = fin[per * j]
            carry_ref[:, lanes(tile_im)] = fin[per * j + 1]
            if h_ref is not None:
                da_ref[:, lanes(tile_re)] = fin[per * j + 2]
                da_ref[:, lanes(tile_im)] = fin[per * j + 3]


def _put_tiles(ref, sb, block):
    for j in range(SUPER_TILES):
        ref[sb * SUPER_TILES + j] = block[:, j * LANES:(j + 1) * LANES]


def _get_tiles(ref, sb):
    return jnp.concatenate([ref[sb * SUPER_TILES + j] for j in range(SUPER_TILES)], axis=1)


def _ssm_fwd(u, bmat, cmat, tab, d_skip, tb, exchange=None):
    seq = u.shape[0]

    def body(u_ref, b_ref, c_ref, t_ref, d_ref, s_ref, h_ref, carry_ref):
        @pl.when(pl.program_id(0) == 0)
        def _():
            carry_ref[...] = jnp.zeros_like(carry_ref)

        u_blk = u_ref[...]
        ub = _bf(u_blk)
        for sb in range(N_SUPER):
            _put_tiles(h_ref, sb, _mm(ub[:, sb * SUPER_IN:(sb + 1) * SUPER_IN], b_ref[sb]))
        _scan_rows(h_ref, t_ref, carry_ref, tb // WINDOW, False)
        ys = [_mm(_bf(_get_tiles(h_ref, sb)), c_ref[sb]) for sb in range(N_SUPER)]
        s_ref[...] = jnp.concatenate(ys, axis=1) + d_ref[...] * u_blk

    return _rowcall("ssm_fwd", body, seq, tb, [u], [bmat, cmat, tab, d_skip],
                    [(SSM_W, F32), ((STATE_TILES, LANES), F32)], [],
                    scratch=[pltpu.VMEM((SUBLANES, STATE_COLS), F32)], vmem=VMEM_BIG, exchange=exchange)


def _ssm_bwd(ds, u, h, bmat_t, cmat_t, tab, d_skip, tb, exchange=None):
    seq = u.shape[0]

    def body(ds_ref, u_ref, h_ref, bt_ref, ct_ref, t_ref, d_ref,
             du_ref, db_ref, dc_ref, da_ref, dd_ref, g_ref, carry_ref):
        @pl.when(pl.program_id(0) == 0)
        def _():
            carry_ref[...] = jnp.zeros_like(carry_ref)
            db_ref[...] = jnp.zeros_like(db_ref)
            dc_ref[...] = jnp.zeros_like(dc_ref)
            da_ref[...] = jnp.zeros_like(da_ref)
            dd_ref[...] = jnp.zeros_like(dd_ref)

        ds_blk = ds_ref[...]
        dsb = _bf(ds_blk)
        u_blk = u_ref[...]
        ub = _bf(u_blk)
        for sb in range(N_SUPER):
            _put_tiles(g_ref, sb, _mm(dsb[:, sb * SUPER_IN:(sb + 1) * SUPER_IN], ct_ref[sb]))
        _scan_rows(g_ref, t_ref, carry_ref, tb // WINDOW, True, h_ref=h_ref, da_ref=da_ref)
        dus = []
        for sb in range(N_SUPER):
            gb = _bf(_get_tiles(g_ref, sb))
            dus.append(_mm(gb, bt_ref[sb]))
            db_ref[sb] += _mm_tn(ub[:, sb * SUPER_IN:(sb + 1) * SUPER_IN], gb)
            dc_ref[sb] += _mm_tn(_bf(_get_tiles(h_ref, sb)), dsb[:, sb * SUPER_IN:(sb + 1) * SUPER_IN])
        du_ref[...] = jnp.concatenate(dus, axis=1) + d_ref[...] * ds_blk
        dd_ref[...] += jnp.sum(ds_blk * u_blk, axis=0, keepdims=True)

    return _rowcall("ssm_bwd", body, seq, tb, [ds, u, h], [bmat_t, cmat_t, tab, d_skip],
                    [(SSM_W, F32)],
                    [((N_SUPER, SUPER_IN, SUPER_W), F32), ((N_SUPER, SUPER_W, SUPER_IN), F32),
                     ((SUBLANES, STATE_COLS), F32), ((1, SSM_W), F32)],
                    scratch=[pltpu.VMEM((STATE_TILES, tb, LANES), F32), pltpu.VMEM((SUBLANES, STATE_COLS), F32)],
                    reverse=True, vmem=VMEM_BIG, exchange=exchange)


def _merge_core(s, attb, ga, gs, wg_ref, wab_ref, wsb_ref, wout_ref):
    zg, dgelu = _gelu_and_grad(s)
    zgb = _bf(zg)
    sg = _sig(_mm(zgb, wg_ref[...]))
    z = zg * sg
    zb = _bf(z)
    ys = jnp.concatenate([_mm(zb, wsb_ref[j]) for j in range(N_CHIPS)], axis=1)
    ya = jnp.concatenate([_mm(attb, wab_ref[j]) for j in range(N_CHIPS)], axis=1)
    sa = _sig(ga)
    ss = _sig(gs)
    mgb = _bf(sa * ya + ss * ys)
    o = _mm(mgb, wout_ref[...])
    return dict(zg=zg, dgelu=dgelu, zgb=zgb, sg=sg, zb=zb, ys=ys, ya=ya, sa=sa, ss=ss, mgb=mgb, o=o)


def _merge_fwd(x, s, att, ga, gs, g2, w_glu, w_ab, w_sb, w_out, tb):
    seq = x.shape[0]

    def body(x_ref, s_ref, att_ref, ga_ref, gs_ref, g_ref, wg_ref, wab_ref, wsb_ref, wout_ref, x2_ref):
        f = _merge_core(s_ref[...], att_ref[...], ga_ref[...], gs_ref[...], wg_ref, wab_ref, wsb_ref, wout_ref)
        n, _, _ = _rms(f["o"], g_ref[...])
        x2_ref[...] = x_ref[...] + n

    return _rowcall("merge_fwd", body, seq, tb, [x, s, att, ga, gs], [g2, w_glu, w_ab, w_sb, w_out],
                    [(D_MODEL, F32)], [], vmem=VMEM_BIG)[0]


def _merge_bwd(dx2, s, att, ga, gs, g2, w_glu, w_ab, w_sb, w_out, tb, exchange=None):
    seq = s.shape[0]
    cw = D_MODEL // N_CHIPS
    last = seq // tb - 1

    def body(dx2_ref, s_ref, att_ref, ga_ref, gs_ref, g_ref, wg_ref, wab_ref, wsb_ref, wout_ref,
             ds_ref, datt_ref, dga_ref, dgs_ref, dg_ref, dwg_ref, dwab_ref, dwsb_ref, dwout_ref,
             bwg_ref, bwab_ref, bwsb_ref, bwout_ref):
        @pl.when(pl.program_id(0) == 0)
        def _():
            for r in (dg_ref, dwg_ref, dwab_ref, dwsb_ref, dwout_ref):
                r[...] = jnp.zeros_like(r)

        attb = att_ref[...]
        f = _merge_core(s_ref[...], attb, ga_ref[...], gs_ref[...], wg_ref, wab_ref, wsb_ref, wout_ref)
        g = g_ref[...]
        _, oh, r2 = _rms(f["o"], g)
        do, dg = _rms_bwd(dx2_ref[...], oh, r2, g)
        dg_ref[...] += dg
        dob = _bf(do)
        dwout_ref[...] += _mm_tn(f["mgb"], dob)
        dmg = _mm_nt(dob, wout_ref[...])
        sa, ss = f["sa"], f["ss"]
        dyab = _bf(dmg * sa)
        dysb = _bf(dmg * ss)
        dga_ref[...] = _bf(dmg * f["ya"] * sa * (1.0 - sa))
        dgs_ref[...] = _bf(dmg * f["ys"] * ss * (1.0 - ss))
        dwab = _mm_tn(attb, dyab)
        dwsb = _mm_tn(f["zb"], dysb)
        datt = jnp.zeros((tb, ATTN_W), F32)
        dz = jnp.zeros((tb, SSM_W), F32)
        for j in range(N_CHIPS):
            dwab_ref[j] += dwab[:, j * cw:(j + 1) * cw]
            dwsb_ref[j] += dwsb[:, j * cw:(j + 1) * cw]
            datt = datt + _mm_nt(dyab[:, j * cw:(j + 1) * cw], wab_ref[j])
            dz = dz + _mm_nt(dysb[:, j * cw:(j + 1) * cw], wsb_ref[j])
        datt_ref[...] = _bf(datt)
        sg, zg = f["sg"], f["zg"]
        dglb = _bf(dz * zg * sg * (1.0 - sg))
        dwg_ref[...] += _mm_tn(f["zgb"], dglb)
        dzg = dz * sg + _mm_nt(dglb, wg_ref[...])
        ds_ref[...] = dzg * f["dgelu"]

        @pl.when(pl.program_id(0) == last)
        def _():
            for dst, src in ((bwg_ref, dwg_ref), (bwab_ref, dwab_ref), (bwsb_ref, dwsb_ref), (bwout_ref, dwout_ref)):
                dst[...] = _bf(src[...])

    shapes = [w_glu.shape, w_ab.shape, w_sb.shape, w_out.shape]
    return _rowcall("merge_bwd", body, seq, tb, [dx2, s, att, ga, gs], [g2, w_glu, w_ab, w_sb, w_out],
                    [(SSM_W, F32), (ATTN_W, BF16), (D_MODEL, BF16), (D_MODEL, BF16)],
                    [((1, D_MODEL), F32)] + [(sh, F32) for sh in shapes] + [(sh, BF16) for sh in shapes],
                    vmem=VMEM_BIG, exchange=exchange)


def _mlp_fwd_loss(x2, target, g3, g4, w_ffi, w_ffo, tb):
    seq = x2.shape[0]
    n_slab = len(w_ffi)
    sw = D_FF // FF_CHUNKS // n_slab

    def body(x2_ref, t_ref, g3_ref, g4_ref, *rest):
        wi_refs, (wo_ref, dy_ref, df_ref, h_ref, ra_ref, loss_ref, dg_ref) = rest[:n_slab], rest[n_slab:]

        @pl.when(pl.program_id(0) == 0)
        def _():
            loss_ref[...] = jnp.zeros_like(loss_ref)
            dg_ref[...] = jnp.zeros_like(dg_ref)

        x2_blk = x2_ref[...]
        h3, _, _ = _rms(x2_blk, g3_ref[...])
        hb = _bf(h3)
        h_ref[...] = hb
        f = jnp.zeros((tb, D_MODEL), F32)
        for j in range(FF_CHUNKS):
            for k in range(n_slab):
                ra = jnp.maximum(_mm(hb, wi_refs[k][j]), 0.0)
                ra_ref[:, pl.ds((j * n_slab + k) * sw, sw)] = _bf(ra)
                f = f + _mm(_bf(ra * ra), wo_ref[j, pl.ds(k * sw, sw), :])
        g4 = g4_ref[...]
        n4, fh, r4 = _rms(f, g4)
        e = (x2_blk + n4) - t_ref[...]
        loss_ref[...] += 0.5 * jnp.sum(jnp.mean(e * e, axis=-1, keepdims=True))
        dy = e * (1.0 / D_MODEL)
        dy_ref[...] = dy
        df, dg = _rms_bwd(dy, fh, r4, g4)
        df_ref[...] = _bf(df)
        dg_ref[...] += dg

    return _rowcall("mlp_fwd_loss", body, seq, tb, [x2, target], [g3, g4, *w_ffi, w_ffo],
                    [(D_MODEL, F32), (D_MODEL, BF16), (D_MODEL, BF16), (D_FF, BF16)],
                    [((SUBLANES, 128), F32), ((1, D_MODEL), F32)], vmem=VMEM_BIG)


def _mlp_bwd(x2, dy, df, ra, g3, w_ffi, w_ffo, tb):
    seq = x2.shape[0]
    n_slab = len(w_ffi)
    sw = D_FF // FF_CHUNKS // n_slab

    def body(x2_ref, dy_ref, df_ref, ra_ref, g3_ref, *rest):
        wi_refs, (wo_ref, dx_ref, da_ref, dg_ref) = rest[:n_slab], rest[n_slab:]

        @pl.when(pl.program_id(0) == 0)
        def _():
            dg_ref[...] = jnp.zeros_like(dg_ref)

        dfb = df_ref[...]
        dh = jnp.zeros((tb, D_MODEL), F32)
        for j in range(FF_CHUNKS):
            for k in range(n_slab):
                cols = pl.ds((j * n_slab + k) * sw, sw)
                ra = ra_ref[:, cols].astype(F32)
                dab = _bf(_mm_nt(dfb, wo_ref[j, pl.ds(k * sw, sw), :]) * (2.0 * ra))
                da_ref[:, cols] = dab
                dh = dh + _mm_nt(dab, wi_refs[k][j])
        g3 = g3_ref[...]
        _, xh, r3 = _rms(x2_ref[...], g3)
        dxn, dg = _rms_bwd(dh, xh, r3, g3)
        dx_ref[...] = dy_ref[...] + dxn
        dg_ref[...] += dg

    return _rowcall("mlp_bwd", body, seq, tb, [x2, dy, df, ra], [g3, *w_ffi, w_ffo],
                    [(D_MODEL, F32), (D_FF, BF16)], [((1, D_MODEL), F32)], vmem=VMEM_BIG)


def _matmul_tn(name, a, b, tk, tn, tl, chunk_major, exchange=None, square_a=False):
    seq, kdim = a.shape
    ndim = b.shape[1]
    last = seq // tl - 1

    def body(a_ref, b_ref, o_ref, ob_ref):
        @pl.when(pl.program_id(2) == 0)
        def _():
            o_ref[...] = jnp.zeros_like(o_ref)

        a_blk = a_ref[...]
        if square_a:
            a_blk = _bf(jnp.square(a_blk.astype(F32)))
        o_ref[...] += _mm_tn(a_blk, b_ref[...])

        @pl.when(pl.program_id(2) == last)
        def _():
            ob_ref[...] = _bf(o_ref[...])

    if chunk_major:
        shape = (ndim // tn, kdim, tn)
        out_spec = pl.BlockSpec((None, tk, tn), lambda k, n, l: (n, k, 0))
    else:
        shape = (kdim, ndim)
        out_spec = pl.BlockSpec((tk, tn), lambda k, n, l: (k, n))
    return _fused_call(
        name, body, (kdim // tk, ndim // tn, seq // tl),
        [pl.BlockSpec((tl, tk), lambda k, n, l: (l, k)), pl.BlockSpec((tl, tn), lambda k, n, l: (l, n))],
        [out_spec, out_spec], [SDS(shape, F32), SDS(shape, BF16)], [], [a, b], exchange, _params(3, VMEM_BIG))


def _ew_call(name, fn, ins, n_out, after=None):
    rows, cols = ins[0].shape
    tr = rows
    while tr * cols * 4 > min(1 << 20, (9 << 20) // (len(ins) + n_out)) and tr % 16 == 0:
        tr //= 2
    spec = pl.BlockSpec((tr, cols), lambda i: (i, 0))
    extra = [] if after is None else [after]

    def body(*refs):
        outs = fn(*[r[...] for r in refs[:len(ins)]])
        for r, o in zip(refs[len(ins) + len(extra):], outs):
            r[...] = o

    return pl.pallas_call(
        body, grid=(rows // tr,), in_specs=[spec] * len(ins) + [ANY] * len(extra), out_specs=[spec] * n_out,
        out_shape=[SDS((rows, cols), F32)] * n_out, name=name, compiler_params=_params(1))(*ins, *extra)


def _adam_math(w, g, m, v):
    m2 = ADAM_B1 * m + (1.0 - ADAM_B1) * g
    v2 = ADAM_B2 * v + (1.0 - ADAM_B2) * (g * g)
    m_hat = m2 / (1.0 - ADAM_B1 ** ADAM_STEP)
    v_hat = v2 / (1.0 - ADAM_B2 ** ADAM_STEP)
    delta = -ADAM_LR * (m_hat / (jnp.sqrt(v_hat) + ADAM_EPS) + ADAM_WD * w)
    return delta, m2, v2


def _sum4(name, own, recv, idx):
    _, rows, cols = own.shape
    tr = rows
    while tr * cols * 4 > (1 << 20) and tr % 16 == 0:
        tr //= 2

    def body(idx_ref, o_ref, r0_ref, r1_ref, r2_ref, out_ref):
        out_ref[...] = ((o_ref[...] + r0_ref[...].astype(F32)) + r1_ref[...].astype(F32)) + r2_ref[...].astype(F32)

    blk = (None, tr, cols)
    grid_spec = pltpu.PrefetchScalarGridSpec(
        num_scalar_prefetch=1, grid=(rows // tr,),
        in_specs=[pl.BlockSpec(blk, lambda i, s: (s[0], i, 0)), pl.BlockSpec(blk, lambda i, s: (0, i, 0)),
                  pl.BlockSpec(blk, lambda i, s: (1, i, 0)), pl.BlockSpec(blk, lambda i, s: (2, i, 0))],
        out_specs=pl.BlockSpec((tr, cols), lambda i, s: (i, 0)))
    return pl.pallas_call(body, grid_spec=grid_spec, out_shape=SDS((rows, cols), F32), name=name,
                          compiler_params=_params(1))(jnp.reshape(idx, (1,)).astype(jnp.int32), own, recv, recv, recv)


def _adam_pair(name, item, after=None):
    def fn(w_, a, b, m_, v_):
        g = a + b
        return (g,) + _adam_math(w_, g, m_, v_)

    return _ew_call(name, fn, list(item), 4, after)


def _place():
    return lax.axis_index("x"), lax.axis_index("y"), lax.axis_index("c")


def _other_chips(x, y):
    return [(1 - x, y), (x, 1 - y), (1 - x, 1 - y)]


HBM = pl.BlockSpec(memory_space=pltpu.HBM)
SEM = pl.BlockSpec(memory_space=pltpu.SEMAPHORE)
DATAFLOW = pltpu.SideEffectType.DATAFLOW_SIDE_EFFECTING


class _Flight:
    def __init__(self, copies, n_copies, send, recv, srcs, lands, token):
        self.copies, self.n, self.send, self.recv = copies, n_copies, send, recv
        self.srcs, self.lands, self.token = list(srcs), list(lands), token


def _take_off(name, srcs, lands, copies, n_copies, after):
    n_s, n_l = len(srcs), len(lands)

    def body(*refs):
        src, land = refs[:n_s], refs[n_s:n_s + n_l]
        send, recv = refs[n_s + n_l + 1:n_s + n_l + 3]
        for cp in copies(src, land, send, recv):
            cp.start()
        refs[-1][...] = jnp.zeros_like(refs[-1])

    mem = lambda t: pltpu.HBM(t.shape, t.dtype)
    sems = pltpu.SemaphoreType.DMA((n_copies,))
    outs = pl.pallas_call(
        body, name=name,
        out_shape=(sems, sems, *map(mem, srcs), *map(mem, lands), SDS((SUBLANES, LANES), F32)),
        in_specs=[HBM] * (n_s + n_l) + [ANY],
        out_specs=(SEM, SEM, *[HBM] * (n_s + n_l), pl.BlockSpec(memory_space=pltpu.VMEM)),
        input_output_aliases={i: 2 + i for i in range(n_s + n_l)},
        compiler_params=pltpu.CompilerParams(has_side_effects=DATAFLOW),
    )(*[pltpu.with_memory_space_constraint(t, pltpu.HBM) for t in (*srcs, *lands)], after)
    return _Flight(copies, n_copies, outs[0], outs[1], outs[2:2 + n_s], outs[2 + n_s:2 + n_s + n_l], outs[-1])


def _land(name, flight, after):
    n_s, n_l = len(flight.srcs), len(flight.lands)

    def body(*refs):
        src, land = refs[:n_s], refs[n_s:n_s + n_l]
        send, recv = refs[n_s + n_l:n_s + n_l + 2]
        for cp in flight.copies(src, land, send, recv):
            cp.wait_send()
            cp.wait_recv()

    mem = lambda t: pltpu.HBM(t.shape, t.dtype)
    outs = pl.pallas_call(
        body, name=name, out_shape=(*map(mem, flight.srcs), *map(mem, flight.lands)),
        in_specs=[HBM] * (n_s + n_l) + [SEM, SEM, ANY], out_specs=tuple([HBM] * (n_s + n_l)),
        input_output_aliases={i: i for i in range(n_s + n_l)},
        compiler_params=pltpu.CompilerParams(has_side_effects=DATAFLOW),
    )(*flight.srcs, *flight.lands, flight.send, flight.recv, after)
    return list(outs[:n_s]), list(outs[n_s:])


def _empty_like(shapes_from, lead):
    return [lax.empty((lead,) + t.shape[1:], t.dtype) for t in shapes_from]


def _scatter_off(name, chunks, after):
    def copies(src, land, send, recv):
        x, y, c = _place()
        return [pltpu.make_async_remote_copy(
            src_ref=src[a].at[2 * px + py], dst_ref=land[a].at[k], send_sem=send.at[3 * a + k],
            recv_sem=recv.at[3 * a + k], device_id=(px, py, c), device_id_type=MESH_ID)
            for a in range(len(chunks)) for k, (px, py) in enumerate(_other_chips(x, y))]

    return _take_off(name, chunks, _empty_like(chunks, 3), copies, 3 * len(chunks), after)


def _swap_off(name, arrs, after):
    def copies(src, land, send, recv):
        x, y, c = _place()
        return [pltpu.make_async_remote_copy(
            src_ref=src[a], dst_ref=land[a], send_sem=send.at[a], recv_sem=recv.at[a],
            device_id=(x, y, 1 - c), device_id_type=MESH_ID) for a in range(len(arrs))]

    return _take_off(name, arrs, [lax.empty(t.shape, t.dtype) for t in arrs], copies, len(arrs), after)


def _devices_off(name, block, after):
    me = 4 * lax.axis_index("x") + 2 * lax.axis_index("y") + lax.axis_index("c")
    land = lax.dynamic_update_index_in_dim(lax.empty((N_DEV,) + block.shape, block.dtype), block, me, 0)

    def copies(src, land, send, recv):
        x, y, c = _place()
        mine = 4 * x + 2 * y + c
        return [pltpu.make_async_remote_copy(
            src_ref=src[0], dst_ref=land[0].at[mine], send_sem=send.at[k - 1], recv_sem=recv.at[k - 1],
            device_id=(x ^ (k >> 2), y ^ ((k >> 1) & 1), c ^ (k & 1)), device_id_type=MESH_ID)
            for k in range(1, N_DEV)]

    return _take_off(name, [block], [land], copies, N_DEV - 1, after)


def _half_rows(shape, c, other=False):
    half = shape[0] // 2
    return pl.ds(((1 - c) if other else c) * half, half)


def _gather_start(name, shards, lands, after):
    n = len(shards)

    def body(*refs):
        src, land, (send, recv) = refs[:n], refs[n:2 * n], refs[2 * n + 1:2 * n + 3]
        x, y, c = _place()
        me = 2 * x + y
        for a in range(n):
            mine = _half_rows(shards[a].shape, c)
            for j, (px, py) in enumerate(_other_chips(x, y)):
                pltpu.make_async_remote_copy(
                    src_ref=src[a].at[mine], dst_ref=land[a].at[me, mine], send_sem=send.at[3 * a + j],
                    recv_sem=recv.at[3 * a + j], device_id=(px, py, c), device_id_type=MESH_ID).start()
        token = refs[-1]
        token[...] = jnp.zeros_like(token)

    mem = lambda t: pltpu.HBM(t.shape, t.dtype)
    pair = pltpu.SemaphoreType.DMA((3 * n,))
    outs = pl.pallas_call(
        body, name=name,
        out_shape=(pair, pair, *map(mem, shards), *map(mem, lands), SDS((SUBLANES, LANES), F32)),
        in_specs=[HBM] * (2 * n) + [ANY],
        out_specs=(SEM, SEM, *[HBM] * (2 * n), pl.BlockSpec(memory_space=pltpu.VMEM)),
        input_output_aliases={i: 2 + i for i in range(2 * n)},
        compiler_params=pltpu.CompilerParams(has_side_effects=DATAFLOW),
    )(*[pltpu.with_memory_space_constraint(t, pltpu.HBM) for t in (*shards, *lands)], after)
    return outs[0], outs[1], list(outs[2:2 + n]), list(outs[2 + n:2 + 2 * n]), outs[-1]


def _gather_pass(name, send, recv, shards, lands, after):
    n = len(shards)

    def body(*refs):
        src, land, (send, recv, _) = refs[:n], refs[n:2 * n], refs[2 * n:2 * n + 3]
        fsend, frecv = refs[2 * n + 3], refs[2 * n + 4]
        x, y, c = _place()
        me = 2 * x + y
        for a in range(n):
            mine = _half_rows(shards[a].shape, c)
            for j, (px, py) in enumerate(_other_chips(x, y)):
                far = 2 * px + py
                ici = pltpu.make_async_remote_copy(
                    src_ref=src[a].at[mine], dst_ref=land[a].at[far, mine], send_sem=send.at[3 * a + j],
                    recv_sem=recv.at[3 * a + j], device_id=(px, py, c), device_id_type=MESH_ID)
                ici.wait_recv()
                ici.wait_send()
                pltpu.make_async_remote_copy(
                    src_ref=land[a].at[far, mine], dst_ref=land[a].at[far, mine], send_sem=fsend.at[3 * a + j],
                    recv_sem=frecv.at[3 * a + j], device_id=(x, y, 1 - c), device_id_type=MESH_ID).start()
        token = refs[-1]
        token[...] = jnp.zeros_like(token)

    mem = lambda t: pltpu.HBM(t.shape, t.dtype)
    pair = pltpu.SemaphoreType.DMA((3 * n,))
    outs = pl.pallas_call(
        body, name=name,
        out_shape=(pair, pair, *map(mem, lands), SDS((SUBLANES, LANES), F32)),
        in_specs=[HBM] * (2 * n) + [SEM, SEM, ANY],
        out_specs=(SEM, SEM, *[HBM] * n, pl.BlockSpec(memory_space=pltpu.VMEM)),
        input_output_aliases={n + i: 2 + i for i in range(n)},
        compiler_params=pltpu.CompilerParams(has_side_effects=DATAFLOW),
    )(*shards, *lands, send, recv, after)
    return outs[0], outs[1], list(outs[2:2 + n]), outs[-1]


def _gather_wait(name, fsend, frecv, lands, after):
    n = len(lands)

    def body(*refs):
        land, (fsend, frecv, _) = refs[:n], refs[n:n + 3]
        x, y, c = _place()
        for a in range(n):
            for j, (px, py) in enumerate(_other_chips(x, y)):
                far = 2 * px + py
                mine = _half_rows(lands[a].shape[1:], c)
                theirs = _half_rows(lands[a].shape[1:], c, other=True)
                pltpu.make_async_remote_copy(
                    src_ref=land[a].at[far, mine], dst_ref=land[a].at[far, mine], send_sem=fsend.at[3 * a + j],
                    recv_sem=frecv.at[3 * a + j], device_id=(x, y, 1 - c), device_id_type=MESH_ID).wait_send()
                pltpu.make_async_remote_copy(
                    src_ref=land[a].at[far, theirs], dst_ref=land[a].at[far, theirs], send_sem=fsend.at[3 * a + j],
                    recv_sem=frecv.at[3 * a + j], device_id=(x, y, 1 - c), device_id_type=MESH_ID).wait_recv()

    mem = lambda t: pltpu.HBM(t.shape, t.dtype)
    return list(pl.pallas_call(
        body, name=name, out_shape=tuple(map(mem, lands)), in_specs=[HBM] * n + [SEM, SEM, ANY],
        out_specs=tuple([HBM] * n), input_output_aliases={i: i for i in range(n)},
        compiler_params=pltpu.CompilerParams(has_side_effects=DATAFLOW),
    )(*lands, fsend, frecv, after))


def _after(token):
    return _Exchange([token], [], [], lambda *_: None, lambda *_: None)


def _swap_sibling(arrs):
    n = len(arrs)

    def copies(ins, outs, sems):
        send, recv = sems
        x, y, c = _place()
        return [pltpu.make_async_remote_copy(
            src_ref=ins[a], dst_ref=outs[a], send_sem=send.at[a], recv_sem=recv.at[a],
            device_id=(x, y, 1 - c), device_id_type=MESH_ID) for a in range(n)]

    def start(ins, outs, sems):
        for cp in copies(ins, outs, sems):
            cp.start()

    def wait(ins, outs, sems):
        cps = copies(ins, outs, sems)
        for cp in cps:
            cp.wait_recv()
        for cp in cps:
            cp.wait_send()

    return _Exchange(arrs, [SDS(s.shape, s.dtype) for s in arrs],
                     [pltpu.SemaphoreType.DMA((n,)), pltpu.SemaphoreType.DMA((n,))], start, wait)


def _sum_devices(slots):
    def body(s_ref, o_ref):
        acc = s_ref[0]
        for d in range(1, N_DEV):
            acc = acc + s_ref[d]
        o_ref[...] = acc

    return pl.pallas_call(
        body, in_specs=[pl.BlockSpec(memory_space=pltpu.VMEM)], out_specs=pl.BlockSpec(memory_space=pltpu.VMEM),
        out_shape=SDS(slots.shape[1:], F32), name="sum_small",
        compiler_params=pltpu.CompilerParams(vmem_limit_bytes=32 * 1024 * 1024))(slots)


def _adam_small(ws, gs, ms, vs):
    n = len(ws)

    def body(*refs):
        for i in range(n):
            w_ref, g_ref, m_ref, v_ref = (refs[k * n + i] for k in range(4))
            outs = _adam_math(w_ref[...], g_ref[...], m_ref[...], v_ref[...])
            for k in range(3):
                refs[(4 + k) * n + i][...] = outs[k]

    vmem = pl.BlockSpec(memory_space=pltpu.VMEM)
    return pl.pallas_call(
        body, in_specs=[vmem] * (4 * n), out_specs=[vmem] * (3 * n),
        out_shape=[SDS(w.shape, F32) for w in ws] * 3, name="adam_small",
        compiler_params=pltpu.CompilerParams(vmem_limit_bytes=32 * 1024 * 1024))(*ws, *gs, *ms, *vs)


def _local_step(x, target, small, big, tb, distributed):
    dist = distributed
    me = (2 * lax.axis_index("x") + lax.axis_index("y")) if dist else 0
    tb_ssm = min(tb, 256)
    bucket = jnp.asarray(_bucket_table())
    place_own = lambda t: lax.dynamic_update_index_in_dim(lax.empty((N_CHIPS,) + t.shape, t.dtype), t, me, 0)
    if dist:
        in_legs = _gather_start("gather_in_start", [big["w_in"]], [place_own(big["w_in"])], small["d_skip"])
        names = sorted(small)
        in_token, values = lax.optimization_barrier((in_legs[4], [small[n] for n in names]))
        small = dict(zip(names, values))
    g1, g2, g3, g4 = small["norm_mix_pre"], small["norm_mix_post"], small["norm_mlp_pre"], small["norm_mlp_post"]

    keys_first = lambda t: jnp.swapaxes(t, -1, -2)
    bias = _bias_table(small["rel_bias"], bucket)
    sink_rows = keys_first(_pair_layout(jnp.broadcast_to(small["sinks"].reshape(N_HEADS, 1, 1), (N_HEADS, BLOCK, 1))))
    disc_args = (small["lam_re"], small["lam_im"], small["log_dt"], small["b_re"], small["b_im"])
    (ab_re, ab_im, bb_re, bb_im), disc_vjp = jax.vjp(_ssm_discretize, *disc_args)
    tab_f, tab_b = _scan_tables(ab_re, ab_im)
    bmat = _bf(_b_matrix(bb_re, bb_im))
    cmat = _bf(_c_matrix(small["c_re"], small["c_im"]))
    d_skip = small["d_skip"]

    mix = ("w_glu", "w_attn_branch", "w_ssm_branch", "w_out")
    rest = [big[n] for n in mix + ("w_ff_in", "w_ff_out")]
    if dist:
        send, recv, src, lands, _ = in_legs
        rest_lands = [place_own(t) for t in rest]
        corner = lambda t: t.reshape(-1, t.shape[-1])[:1, :LANES].astype(F32)
        prepared = sum(map(corner, [tab_b, bias, sink_rows, bmat, cmat] + rest_lands), in_token[:1])
        send, recv, lands, in_passed = _gather_pass("gather_in_pass", send, recv, src, lands, prepared)
        (g_in,) = _gather_wait("gather_in_wait", send, recv, lands, in_passed)
        w_in = g_in.reshape(IN_W, D_MODEL)
    else:
        w_in = big["w_in"]
    token = None
    if dist:
        send, recv, rest, lands, token = _gather_start("gather_rest_start", rest, rest_lands, in_passed)
    h1, q, k, v, u, ga, gs = _inproj_fwd(x, g1, w_in, tb, _after(token) if dist else None)
    s, h = _ssm_fwd(u, bmat, cmat, tab_f, d_skip, tb)
    if dist:
        send, recv, lands, token = _gather_pass("gather_rest_pass", send, recv, rest, lands, s)
    att = _attn_fwd(q, k, v, bias, sink_rows, _after(token) if dist else None)[0]
    if dist:
        rest = _gather_wait("gather_rest_wait", send, recv, lands, att)
    w_glu, w_ab, w_sb, w_out, w_ffi, w_ffo = rest
    w_glu = w_glu.reshape(SSM_W, SSM_W)
    w_out = w_out.reshape(D_MODEL, D_MODEL)
    w_ffi = [w_ffi]
    x2 = _merge_fwd(x, s, att, ga, gs, g2, w_glu, w_ab, w_sb, w_out, tb)
    dy, df, h3, ra, loss_acc, dg4 = _mlp_fwd_loss(x2, target, g3, g4, w_ffi, w_ffo, tb)

    dx2, da, dg3 = _mlp_bwd(x2, dy, df, ra, g3, w_ffi, w_ffo, tb)
    tl = min(2048, x.shape[0])
    chunked = (N_CHIPS, D_FF // N_CHIPS, D_MODEL)
    d_ffi, b_ffi = _matmul_tn("grad_w_ff_in", h3, da, D_MODEL, D_FF // FF_CHUNKS, tl, True)
    d_ffo, b_ffo = _matmul_tn("grad_w_ff_out", ra, df, D_FF // FF_CHUNKS, D_MODEL, tl, False, square_a=True)
    d_ffo, b_ffo = d_ffo.reshape(chunked), b_ffo.reshape(chunked)
    behind = lambda flight: _after(flight.token) if dist else None
    ff_fl = _scatter_off("scatter_ff_off", [b_ffi, b_ffo], d_ffo) if dist else None
    outs = _merge_bwd(dx2, s, att, ga, gs, g2, w_glu, w_ab, w_sb, w_out, tb_ssm, behind(ff_fl))
    ds, datt, dga, dgs, dg2, d_glu, d_ab, d_sb, d_out, b_glu, b_ab, b_sb, b_out = outs
    glu4, out4 = (N_CHIPS, SSM_W // N_CHIPS, SSM_W), (N_CHIPS, D_MODEL // N_CHIPS, D_MODEL)
    d_mix = [d_glu.reshape(glu4), d_ab, d_sb, d_out.reshape(out4)]
    b_mix = [b_glu.reshape(glu4), b_ab, b_sb, b_out.reshape(out4)]
    mix_fl = _scatter_off("scatter_mix_off", b_mix, d_mix[-1]) if dist else None
    du, d_bmat, d_cmat, da_acc, dd_skip = _ssm_bwd(
        ds, u, h, bmat.transpose(0, 2, 1), cmat.transpose(0, 2, 1), tab_b, d_skip, tb, behind(mix_fl))
    dq, dk, dv, dbias, dsink_rows = _attn_bwd(q, k, v, datt, bias, sink_rows)
    swap_fl = None
    if dist:
        r_ffi, r_ffo = _land("scatter_ff_land", ff_fl, dq)[1]
        p_ffi = _sum4("sum_w_ff_in", d_ffi, r_ffi, me)
        p_ffo = _sum4("sum_w_ff_out", d_ffo, r_ffo, me)
        swap_fl = _swap_off("swap_ff_off", [p_ffi, p_ffo], r_ffo)
    dx, dpj, dg1 = _inproj_bwd(x, dx2, dq, dk, dv, du, dga, dgs, g1, w_in, tb, behind(swap_fl))

    dab_re, dab_im = _state_unlayout(jnp.sum(da_acc, axis=0))
    dbb_re, dbb_im = _b_matrix_grad(d_bmat)
    d_lam_re, d_lam_im, d_log_dt, d_b_re, d_b_im = disc_vjp((dab_re, dab_im, dbb_re, dbb_im))
    d_c_re, d_c_im = _c_matrix_grad(d_cmat)
    d_rel = _bias_grad(dbias, bucket)
    d_sinks = jnp.sum(_pair_unlayout(keys_first(dsink_rows)), axis=(1, 2))
    small_grads = dict(
        norm_mix_pre=dg1, norm_mix_post=dg2, norm_mlp_pre=dg3, norm_mlp_post=dg4, rel_bias=d_rel, sinks=d_sinks,
        lam_re=d_lam_re, lam_im=d_lam_im, log_dt=d_log_dt, b_re=d_b_re, b_im=d_b_im, c_re=d_c_re, c_im=d_c_im,
        d_skip=dd_skip)
    small_fl = _devices_off("small_off", _pack(small_grads, loss_acc), swap_fl.token) if dist else None
    outs = _matmul_tn("grad_w_in", dpj, h1, IN_W // 2, D_MODEL, tl, False, behind(small_fl))
    in4 = (N_CHIPS, IN_W // N_CHIPS, D_MODEL)
    d_in, b_in = outs[0].reshape(in4), outs[1].reshape(in4)
    if not dist:
        return loss_acc, dx, small_grads, dict(zip(BIG, [d_in] + d_mix + [d_ffi, d_ffo]))
    in_fl = _scatter_off("scatter_w_in_off", [b_in], d_in)
    (p_ffi, p_ffo), (s_ffi, s_ffo) = _land("swap_ff_land", swap_fl, in_fl.token)
    r_mix = _land("scatter_mix_land", mix_fl, in_fl.token)[1]
    p_mix = [_sum4("sum_" + n, d, r, me) for n, d, r in zip(mix, d_mix, r_mix)]
    pending = dict(d_in=d_in, in_fl=in_fl, p_mix=p_mix, w_ff_in=(p_ffi, s_ffi), w_ff_out=(p_ffo, s_ffo), me=me)
    return loss_acc, dx, small_fl, pending


SMALL = ['norm_mix_pre', 'norm_mix_post', 'norm_mlp_pre', 'norm_mlp_post', 'rel_bias', 'sinks', 'lam_re', 'lam_im',
         'log_dt', 'b_re', 'b_im', 'c_re', 'c_im', 'd_skip']
BIG = ['w_in', 'w_glu', 'w_attn_branch', 'w_ssm_branch', 'w_out', 'w_ff_in', 'w_ff_out']
WEIGHTS = ['norm_mix_pre', 'norm_mix_post', 'norm_mlp_pre', 'norm_mlp_post', 'w_in', 'rel_bias', 'sinks', 'lam_re',
           'lam_im', 'log_dt', 'b_re', 'b_im', 'c_re', 'c_im', 'd_skip', 'w_glu', 'w_attn_branch', 'w_ssm_branch',
           'w_out', 'w_ff_in', 'w_ff_out']
PACK_COLS = 1024
PACK_ORDER = ['b_re', 'b_im', 'c_re', 'c_im', 'lam_re', 'lam_im', 'norm_mix_pre', 'norm_mix_post', 'norm_mlp_pre',
              'norm_mlp_post', 'rel_bias', 'sinks', 'log_dt', 'd_skip']


STATE_MINOR = ('b_re', 'b_im')
PACK_ROWS = 144
LOSS_ROW = 140


def _pack(named, loss_acc):
    parts = []
    for n in PACK_ORDER:
        a = jnp.swapaxes(named[n], -1, -2) if n in STATE_MINOR else named[n]
        flat = a.reshape(-1)
        rows = -(-flat.shape[0] // PACK_COLS)
        parts.append(jnp.pad(flat, (0, rows * PACK_COLS - flat.shape[0])).reshape(rows, PACK_COLS))
    assert sum(p.shape[0] for p in parts) == LOSS_ROW
    parts.append(jnp.pad(loss_acc[0:1], ((0, PACK_ROWS - LOSS_ROW - 1), (0, PACK_COLS - loss_acc.shape[1]))))
    return jnp.concatenate(parts, axis=0)


def _unpack(packed, shapes):
    out, at = {}, 0
    for n in PACK_ORDER:
        shape = shapes[n][:-2] + (shapes[n][-1], shapes[n][-2]) if n in STATE_MINOR else shapes[n]
        size = int(np.prod(shape))
        rows = -(-size // PACK_COLS)
        blk = packed[at:at + rows]
        out[n] = (blk.reshape(-1)[:size] if size % PACK_COLS else blk).reshape(shape)
        at += rows
    return out


def kernel(x, norm_mix_pre, norm_mix_post, norm_mlp_pre, norm_mlp_post, w_in, rel_bias, sinks, lam_re, lam_im, log_dt, b_re, b_im, c_re, c_im, d_skip, w_glu, w_attn_branch, w_ssm_branch, w_out, w_ff_in, w_ff_out, loss_target, m_norm_mix_pre, m_norm_mix_post, m_norm_mlp_pre, m_norm_mlp_post, m_w_in, m_rel_bias, m_sinks, m_lam_re, m_lam_im, m_log_dt, m_b_re, m_b_im, m_c_re, m_c_im, m_d_skip, m_w_glu, m_w_attn_branch, m_w_ssm_branch, m_w_out, m_w_ff_in, m_w_ff_out, v_norm_mix_pre, v_norm_mix_post, v_norm_mlp_pre, v_norm_mlp_post, v_w_in, v_rel_bias, v_sinks, v_lam_re, v_lam_im, v_log_dt, v_b_re, v_b_im, v_c_re, v_c_im, v_d_skip, v_w_glu, v_w_attn_branch, v_w_ssm_branch, v_w_out, v_w_ff_in, v_w_ff_out):
    env = dict(locals())
    w = {n: env[n] for n in WEIGHTS}
    m = {n: env["m_" + n] for n in WEIGHTS}
    v = {n: env["v_" + n] for n in WEIGHTS}
    seq = x.shape[1]
    tb = min(512, seq)

    small = {n: w[n] for n in ('norm_mix_pre', 'norm_mix_post', 'norm_mlp_pre', 'norm_mlp_post', 'rel_bias')}
    small.update({n: w[n][0] for n in ('sinks', 'lam_re', 'lam_im', 'log_dt', 'b_re', 'b_im', 'c_re', 'c_im')})
    small['d_skip'] = w['d_skip']
    shard = lambda t, n: t[n][0].T if n == 'w_in' else t[n][0]
    unshard = lambda a, n: (a.T if n == 'w_in' else a)[None]
    _, dx, small_fl, pending = _local_step(
        x[0], loss_target[0], small, {n: _bf(shard(w, n)) for n in BIG}, tb, True)

    grads, deltas, new_m, new_v = {}, {}, {}, {}

    def adam(n, partials, after=None):
        outs = _adam_pair("adam_" + n, (shard(w, n), *partials, shard(m, n), shard(v, n)), after)
        grads[n], deltas[n], new_m[n], new_v[n] = [unshard(a, n) for a in outs]
        return outs[3]

    mix = ("w_glu", "w_attn_branch", "w_ssm_branch", "w_out")
    in_fl = pending["in_fl"]
    sib_mix = _exchange_alone("swap_mix", _swap_sibling(pending["p_mix"]))
    last = in_fl.token
    for n, partials in [(n, pending[n]) for n in ("w_ff_in", "w_ff_out")] + list(zip(mix, zip(pending["p_mix"], sib_mix))):
        last = adam(n, partials, last)

    small_g = _sum_devices(_land("small_land", small_fl, last)[1][0])
    loss = small_g[LOSS_ROW, 0]
    minor = lambda t, n: jnp.swapaxes(t, -1, -2) if n in STATE_MINOR else t
    g_small = _unpack(small_g, {n: w[n].shape for n in SMALL})
    outs = _adam_small([minor(w[n], n) for n in SMALL], [g_small[n] for n in SMALL],
                       [minor(m[n], n) for n in SMALL], [minor(v[n], n) for n in SMALL])
    grads.update({n: minor(g_small[n], n) for n in SMALL})
    for k, dst in enumerate((deltas, new_m, new_v)):
        dst.update({n: minor(a, n) for n, a in zip(SMALL, outs[k * len(SMALL):(k + 1) * len(SMALL)])})

    (r_in,) = _land("scatter_w_in_land", in_fl, outs[0])[1]
    p_in = _sum4("sum_w_in", pending["d_in"], r_in, pending["me"])
    (s_in,) = _exchange_alone("swap_w_in", _swap_sibling([p_in]))
    adam("w_in", (p_in, s_in))

    return (loss, dx[None], *[grads[n] for n in WEIGHTS], *[deltas[n] for n in WEIGHTS],
            *[new_m[n] for n in WEIGHTS], *[new_v[n] for n in WEIGHTS])
```

```python
import functools
import math

import numpy as np
import jax
import jax.numpy as jnp
from jax import lax
from jax.experimental import pallas as pl
from jax.experimental.pallas import tpu as pltpu

F32 = jnp.float32
BF16 = jnp.bfloat16

D_MODEL = 1024
N_HEADS = 8
N_KV = 2
Q_GROUP = 4
HEAD_DIM = 64
ATTN_W = 512
KV_W = 128
BLOCK = 128
N_BUCKETS = 32
MAX_DISTANCE = 128
NEG_INF = -1e30
SSM_W = 512
SSM_GROUP = 16
SSM_GROUPS = 32
SSM_STATE = 64
N_SUPER = 4
GROUPS_PER_SUPER = SSM_GROUPS // N_SUPER
SUPER_IN = GROUPS_PER_SUPER * SSM_GROUP
SUPER_HALF = GROUPS_PER_SUPER * SSM_STATE
SUPER_W = 2 * SUPER_HALF
STATE_COLS = N_SUPER * SUPER_W
D_FF = 4096
FF_CHUNKS = 4
IN_W = 3328
SPLITS = (0, 512, 640, 768, 1280, 2304, 3328)
RMS_EPS = 1e-6
N_CHIPS = 4
N_DEV = 8
SUBLANES = 8
LANES = 128
STATE_TILES = STATE_COLS // LANES
SUPER_TILES = SUPER_W // LANES

ADAM_LR = 0.001
ADAM_B1 = 0.9
ADAM_B2 = 0.999
ADAM_EPS = 1e-08
ADAM_WD = 0.01
ADAM_STEP = 10

VMEM_BIG = 56 * 1024 * 1024
SDS = jax.ShapeDtypeStruct
MESH_ID = pl.DeviceIdType.MESH
ANY = pl.BlockSpec(memory_space=pl.ANY)


def _bf(x):
    return x.astype(BF16)


def _mm(a, b):
    return jnp.dot(a, b, preferred_element_type=F32)


def _mm_nt(a, b):
    return lax.dot_general(a, b, (((1,), (1,)), ((), ())), preferred_element_type=F32)


def _mm_tn(a, b):
    return lax.dot_general(a, b, (((0,), (0,)), ((), ())), preferred_element_type=F32)


def _sig(x):
    return 1.0 / (1.0 + jnp.exp(-x))


def _rms(x, g):
    r = lax.rsqrt(jnp.mean(x * x, axis=-1, keepdims=True) + RMS_EPS)
    xh = x * r
    return xh * g, xh, r


def _rms_bwd(dout, xh, r, g):
    dg = jnp.sum(dout * xh, axis=0, keepdims=True)
    dxh = dout * g
    dx = r * (dxh - xh * jnp.mean(dxh * xh, axis=-1, keepdims=True))
    return dx, dg


_GELU_C = math.sqrt(2.0 / math.pi)


def _gelu_and_grad(x):
    x2 = x * x
    inner = _GELU_C * (x + 0.044715 * (x2 * x))
    t = jnp.tanh(inner)
    y = 0.5 * x * (1.0 + t)
    dy = 0.5 * (1.0 + t) + 0.5 * x * (1.0 - t * t) * (_GELU_C * (1.0 + 3.0 * 0.044715 * x2))
    return y, dy


def _zero_map(nd, *_):
    return (0,) * nd


def _params(n_axes, vmem=None):
    return pltpu.CompilerParams(dimension_semantics=("arbitrary",) * n_axes, vmem_limit_bytes=vmem)


class _Exchange:
    def __init__(self, ins, outs, sems, start, wait):
        self.ins, self.outs, self.sems, self.start, self.wait = list(ins), list(outs), list(sems), start, wait


def _fused_call(name, body, grid, in_specs, out_specs, out_shape, scratch, args, exchange, params):
    n_in, n_out, n_scr = len(in_specs), len(out_specs), len(scratch)
    if exchange is None:
        fn = body
    else:
        ex = exchange
        n_xi, n_xo = len(ex.ins), len(ex.outs)

        def fn(*refs):
            at = 0
            parts = []
            for n in (n_in, n_xi, n_out, n_xo, n_scr, len(ex.sems)):
                parts.append(refs[at:at + n])
                at += n
            ins, x_in, outs, x_out, scr, x_sem = parts
            ids = [pl.program_id(a) for a in range(len(grid))]
            first = functools.reduce(jnp.logical_and, [i == 0 for i in ids])
            last = functools.reduce(jnp.logical_and, [i == g - 1 for i, g in zip(ids, grid)])

            @pl.when(first)
            def _():
                ex.start(x_in, x_out, x_sem)

            body(*ins, *outs, *scr)

            @pl.when(last)
            def _():
                ex.wait(x_in, x_out, x_sem)

        in_specs = list(in_specs) + [ANY] * n_xi
        out_specs = list(out_specs) + [ANY] * n_xo
        out_shape = list(out_shape) + ex.outs
        scratch = list(scratch) + ex.sems
        args = list(args) + ex.ins
    return pl.pallas_call(fn, grid=grid, in_specs=in_specs, out_specs=out_specs, out_shape=out_shape,
                          scratch_shapes=list(scratch), name=name, compiler_params=params)(*args)


def _exchange_alone(name, ex):
    def body(*refs):
        n_xi, n_xo = len(ex.ins), len(ex.outs)
        x_in, x_out, x_sem = refs[:n_xi], refs[n_xi:n_xi + n_xo], refs[n_xi + n_xo:]
        ex.start(x_in, x_out, x_sem)
        ex.wait(x_in, x_out, x_sem)

    return pl.pallas_call(body, in_specs=[ANY] * len(ex.ins), out_specs=[ANY] * len(ex.outs), out_shape=ex.outs,
                          scratch_shapes=ex.sems, name=name)(*ex.ins)


def _rowcall(name, body, seq, tb, rows, consts, row_outs, acc_outs, scratch=(), reverse=False, vmem=None,
             exchange=None):
    nb = seq // tb
    rmap = (lambda i: (nb - 1 - i, 0)) if reverse else (lambda i: (i, 0))
    tmap = lambda i: (0,) + rmap(i)

    def row_spec(width):
        if isinstance(width, tuple):
            return pl.BlockSpec((width[0], tb, width[1]), tmap)
        return pl.BlockSpec((tb, width), rmap)

    def row_shape(width):
        return (width[0], seq, width[1]) if isinstance(width, tuple) else (seq, width)

    in_specs = [row_spec(a.shape[1] if a.ndim == 2 else (a.shape[0], a.shape[2])) for a in rows]
    in_specs += [pl.BlockSpec(a.shape, functools.partial(_zero_map, a.ndim), pipeline_mode=pl.Buffered(1))
                 for a in consts]
    out_specs = [row_spec(c) for c, _ in row_outs] + [ANY] * len(acc_outs)
    out_shape = [SDS(row_shape(c), dt) for c, dt in row_outs] + [SDS(s, dt) for s, dt in acc_outs]
    n_main = len(rows) + len(consts) + len(row_outs)
    n_acc = len(acc_outs)

    def fn(*refs):
        main, acc_hbm, rest = refs[:n_main], refs[n_main:n_main + n_acc], refs[n_main + n_acc:]
        acc_vmem, own = rest[:n_acc], rest[n_acc:]
        body(*main, *acc_vmem, *own)

        @pl.when(pl.program_id(0) == nb - 1)
        def _():
            for src, dst in zip(acc_vmem, acc_hbm):
                pltpu.sync_copy(src, dst)

    buffers = [pltpu.VMEM(s, dt) for s, dt in acc_outs] + list(scratch)
    return _fused_call(name, fn if acc_outs else body, (nb,), in_specs, out_specs, out_shape, buffers,
                       [*rows, *consts], exchange, _params(1, vmem))


def _inproj_fwd(x, g1, w_in, tb, exchange=None):
    seq = x.shape[0]

    def body(x_ref, g_ref, w_ref, h_ref, q_ref, k_ref, v_ref, u_ref, ga_ref, gs_ref):
        h, _, _ = _rms(x_ref[...], g_ref[...])
        hb = _bf(h)
        h_ref[...] = hb
        pj = _mm_nt(hb, w_ref[...])
        q_ref[...] = _bf(pj[:, SPLITS[0]:SPLITS[1]])
        k_ref[...] = _bf(pj[:, SPLITS[1]:SPLITS[2]])
        v_ref[...] = _bf(pj[:, SPLITS[2]:SPLITS[3]])
        u_ref[...] = pj[:, SPLITS[3]:SPLITS[4]]
        ga_ref[...] = pj[:, SPLITS[4]:SPLITS[5]]
        gs_ref[...] = pj[:, SPLITS[5]:SPLITS[6]]

    return _rowcall("inproj_fwd", body, seq, tb, [x], [g1, w_in],
                    [(D_MODEL, BF16), (ATTN_W, BF16), (KV_W, BF16), (KV_W, BF16), (SSM_W, F32),
                     (D_MODEL, F32), (D_MODEL, F32)], [], vmem=VMEM_BIG, exchange=exchange)


def _inproj_bwd(x, dx2, dq, dk, dv, du, dga, dgs, g1, w_in, tb, exchange=None):
    seq = x.shape[0]

    def body(x_ref, dx2_ref, dq_ref, dk_ref, dv_ref, du_ref, dga_ref, dgs_ref, g_ref, w_ref,
             dx_ref, dpj_ref, dg_ref):
        @pl.when(pl.program_id(0) == 0)
        def _():
            dg_ref[...] = jnp.zeros_like(dg_ref)

        dpj = jnp.concatenate([dq_ref[...], dk_ref[...], dv_ref[...], _bf(du_ref[...]),
                               dga_ref[...], dgs_ref[...]], axis=1)
        dpj_ref[...] = dpj
        dh = _mm(dpj, w_ref[...])
        g = g_ref[...]
        _, xh, r = _rms(x_ref[...], g)
        dxn, dg = _rms_bwd(dh, xh, r, g)
        dx_ref[...] = dx2_ref[...] + dxn
        dg_ref[...] += dg

    return _rowcall("inproj_bwd", body, seq, tb, [x, dx2, dq, dk, dv, du, dga, dgs], [g1, w_in],
                    [(D_MODEL, F32), (IN_W, BF16)], [((1, D_MODEL), F32)], vmem=VMEM_BIG, exchange=exchange)


def _bucket_table():
    qi = np.arange(BLOCK)[:, None]
    kj = np.arange(2 * BLOCK)[None, :]
    dist = qi + BLOCK - kj
    max_exact = N_BUCKETS // 2
    d = np.maximum(dist, 0)
    df = np.maximum(d, 1).astype(np.float32)
    large = max_exact + (np.log(df / np.float32(max_exact)) / np.float32(math.log(MAX_DISTANCE / max_exact))
                         * np.float32(N_BUCKETS - max_exact)).astype(np.int32)
    large = np.minimum(large, N_BUCKETS - 1)
    bucket = np.where(d < max_exact, d, large)
    valid = (dist >= 0) & (dist < BLOCK)
    return np.where(valid, bucket, -1).astype(np.int32)


def _bias_table(rel_bias, bucket):
    def body(rb_ref, bk_ref, o_ref):
        bk = bk_ref[...]
        has_prev = lax.broadcasted_iota(jnp.int32, bk.shape, 1) >= BLOCK
        for h in range(N_HEADS):
            kh, j, par = h // Q_GROUP, (h // 2) % 2, h % 2
            acc = jnp.full((BLOCK, 2 * BLOCK), NEG_INF, F32)
            for b in range(N_BUCKETS):
                acc = jnp.where(bk == b, rb_ref[b, h], acc)
            o_ref[0, kh, par, :, j * BLOCK:(j + 1) * BLOCK] = jnp.where(has_prev, acc, NEG_INF).T
            o_ref[1, kh, par, :, j * BLOCK:(j + 1) * BLOCK] = acc.T

    return pl.pallas_call(
        body, out_shape=SDS((2, N_KV, 2, 2 * BLOCK, 2 * BLOCK), F32),
        in_specs=[pl.BlockSpec(memory_space=pltpu.SMEM), pl.BlockSpec(memory_space=pltpu.VMEM)],
        out_specs=pl.BlockSpec(memory_space=pltpu.VMEM), name="bias_table",
    )(rel_bias, bucket)


def _bias_grad(dbias, bucket):
    def body(db_ref, bk_ref, o_ref):
        bk = bk_ref[...]
        for h in range(N_HEADS):
            kh, j, par = h // Q_GROUP, (h // 2) % 2, h % 2
            db = db_ref[kh, par, :, j * BLOCK:(j + 1) * BLOCK].T
            for b in range(N_BUCKETS):
                o_ref[b, h] = jnp.sum(jnp.where(bk == b, db, 0.0))

    return pl.pallas_call(
        body, out_shape=SDS((N_BUCKETS, N_HEADS), F32),
        in_specs=[pl.BlockSpec(memory_space=pltpu.VMEM), pl.BlockSpec(memory_space=pltpu.VMEM)],
        out_specs=pl.BlockSpec(memory_space=pltpu.SMEM), name="bias_grad",
    )(dbias, bucket)


TILE = 2 * HEAD_DIM


def _pair_layout(t):
    lead = t.shape[:-3]
    t = t.reshape(lead + (N_KV, 2, 2) + t.shape[-2:])
    nl = len(lead)
    t = jnp.transpose(t, tuple(range(nl)) + (nl, nl + 2, nl + 1, nl + 3, nl + 4))
    return t.reshape(lead + (N_KV, 2, 2 * BLOCK, t.shape[-1]))


def _pair_unlayout(t):
    t = t.reshape(N_KV, 2, 2, BLOCK, t.shape[-1]).transpose(0, 2, 1, 3, 4)
    return t.reshape(N_HEADS, BLOCK, t.shape[-1])


def _halves(t):
    tf = t.astype(F32)
    low = lax.broadcasted_iota(jnp.int32, tf.shape, 1) < HEAD_DIM
    swapped = pltpu.roll(tf, HEAD_DIM, 1)
    zero = jnp.zeros_like(tf)
    return ((_bf(jnp.where(low, tf, zero)), _bf(jnp.where(low, zero, swapped))),
            (_bf(jnp.where(low, swapped, zero)), _bf(jnp.where(low, zero, tf))))


def _fold_halves(even, odd):
    low = lax.broadcasted_iota(jnp.int32, even.shape, 1) < HEAD_DIM
    comb = jnp.where(low, even, odd)
    return comb + pltpu.roll(comb, HEAD_DIM, 1)


def _tile_rows(ref, kh):
    return jnp.concatenate([ref[:, (2 * kh) * TILE:(2 * kh + 1) * TILE],
                            ref[:, (2 * kh + 1) * TILE:(2 * kh + 2) * TILE]], axis=0)


def _halves_t(t):
    tt = t.astype(F32).T
    top = lax.broadcasted_iota(jnp.int32, tt.shape, 0) < HEAD_DIM
    swapped = jnp.concatenate([tt[HEAD_DIM:], tt[:HEAD_DIM]], axis=0)
    zero = jnp.zeros_like(tt)
    return ((_bf(jnp.where(top, tt, zero)), _bf(jnp.where(top, zero, swapped))),
            (_bf(jnp.where(top, swapped, zero)), _bf(jnp.where(top, zero, tt))))


def _attn_probs(km, qk, bias, sink):
    lg = _mm_nt(km, qk) * (HEAD_DIM ** -0.5) + bias
    m = jnp.maximum(jnp.max(lg, axis=0, keepdims=True), sink)
    p = jnp.exp(lg - m)
    es = jnp.exp(sink - m)
    inv = 1.0 / (jnp.sum(p, axis=0, keepdims=True) + es)
    return p * inv, es * inv


def _attn_fwd(q, k, v, bias, sink_rows, exchange=None):
    seq = q.shape[0]
    nblk = seq // BLOCK

    def body(q_ref, kp_ref, kc_ref, vp_ref, vc_ref, b_ref, s_ref, o_ref):
        which = jnp.minimum(pl.program_id(0), 1)
        kms = _halves(jnp.concatenate([kp_ref[...], kc_ref[...]], axis=0))
        vts = _halves_t(jnp.concatenate([vp_ref[...], vc_ref[...]], axis=0))
        for kh in range(N_KV):
            qk = _tile_rows(q_ref, kh)
            acc = jnp.zeros((TILE, 2 * BLOCK), F32)
            for par in range(2):
                pr, _ = _attn_probs(kms[kh][par], qk, b_ref[which, kh, par], s_ref[kh, par])
                acc = acc + _mm(vts[kh][par], _bf(pr))
            acc = acc.T
            o_ref[:, (2 * kh) * TILE:(2 * kh + 1) * TILE] = _bf(acc[:BLOCK])
            o_ref[:, (2 * kh + 1) * TILE:(2 * kh + 2) * TILE] = _bf(acc[BLOCK:])

    cur = lambda n: (n, 0)
    prev = lambda n: (jnp.maximum(n - 1, 0), 0)
    return _fused_call(
        "attn_fwd", body, (nblk,),
        [pl.BlockSpec((BLOCK, ATTN_W), cur),
         pl.BlockSpec((BLOCK, KV_W), prev), pl.BlockSpec((BLOCK, KV_W), cur),
         pl.BlockSpec((BLOCK, KV_W), prev), pl.BlockSpec((BLOCK, KV_W), cur),
         pl.BlockSpec(bias.shape, functools.partial(_zero_map, bias.ndim)),
         pl.BlockSpec(sink_rows.shape, functools.partial(_zero_map, sink_rows.ndim))],
        [pl.BlockSpec((BLOCK, ATTN_W), cur)], [SDS((seq, ATTN_W), BF16)], [],
        [q, k, k, v, v, bias, sink_rows], exchange, _params(1))


def _attn_bwd(q, k, v, d_out, bias, sink_rows, exchange=None):
    seq = q.shape[0]
    nblk = seq // BLOCK

    def body(q_ref, kp_ref, kc_ref, vp_ref, vc_ref, do_ref, b_ref, s_ref,
             dq_ref, dk_ref, dv_ref, db_ref, ds_ref, ck_ref, cv_ref):
        n = pl.program_id(0)

        @pl.when(n == 0)
        def _():
            db_ref[...] = jnp.zeros_like(db_ref)
            ds_ref[...] = jnp.zeros_like(ds_ref)
            ck_ref[...] = jnp.zeros_like(ck_ref)
            cv_ref[...] = jnp.zeros_like(cv_ref)

        @pl.when(n < nblk)
        def _():
            which = jnp.minimum(n, 1)
            scale = HEAD_DIM ** -0.5
            kcat = jnp.concatenate([kp_ref[...], kc_ref[...]], axis=0)
            kms = _halves(kcat)
            kts = _halves_t(kcat)
            vms = _halves(jnp.concatenate([vp_ref[...], vc_ref[...]], axis=0))
            dks, dvs = [], []
            for kh in range(N_KV):
                qk = _tile_rows(q_ref, kh)
                dok = _tile_rows(do_ref, kh)
                dq = jnp.zeros((TILE, 2 * BLOCK), F32)
                dkp, dvp = [], []
                for par in range(2):
                    pr, ps = _attn_probs(kms[kh][par], qk, b_ref[which, kh, par], s_ref[kh, par])
                    dp = _mm_nt(vms[kh][par], dok)
                    rs = jnp.sum(pr * dp, axis=0, keepdims=True)
                    dlg = pr * (dp - rs)
                    ds_ref[kh, par] += -ps * rs
                    db_ref[kh, par] += dlg
                    dlb = _bf(dlg)
                    dq = dq + _mm(kts[kh][par], dlb)
                    dkp.append(_mm(dlb, qk))
                    dvp.append(_mm(_bf(pr), dok))
                dq = _bf((dq * scale).T)
                dq_ref[:, (2 * kh) * TILE:(2 * kh + 1) * TILE] = dq[:BLOCK]
                dq_ref[:, (2 * kh + 1) * TILE:(2 * kh + 2) * TILE] = dq[BLOCK:]
                dks.append(_fold_halves(*dkp))
                dvs.append(_fold_halves(*dvp))
            low = lax.broadcasted_iota(jnp.int32, (2 * BLOCK, TILE), 1) < HEAD_DIM
            dkk = jnp.where(low, dks[0], dks[1]) * scale
            dvv = jnp.where(low, dvs[0], dvs[1])
            dk_ref[...] = _bf(ck_ref[...] + dkk[:BLOCK])
            ck_ref[...] = dkk[BLOCK:]
            dv_ref[...] = _bf(cv_ref[...] + dvv[:BLOCK])
            cv_ref[...] = dvv[BLOCK:]

        @pl.when(n == nblk)
        def _():
            dk_ref[...] = _bf(ck_ref[...])
            dv_ref[...] = _bf(cv_ref[...])

    cur = lambda n: (jnp.minimum(n, nblk - 1), 0)
    prev = lambda n: (jnp.maximum(jnp.minimum(n, nblk - 1) - 1, 0), 0)
    late = lambda n: (jnp.maximum(n - 1, 0), 0)
    kv_spec = lambda m: pl.BlockSpec((BLOCK, KV_W), m)
    acc_b = pl.BlockSpec(bias.shape[1:], functools.partial(_zero_map, bias.ndim - 1))
    acc_s = pl.BlockSpec(sink_rows.shape, functools.partial(_zero_map, sink_rows.ndim))
    return _fused_call(
        "attn_bwd", body, (nblk + 1,),
        [pl.BlockSpec((BLOCK, ATTN_W), cur), kv_spec(prev), kv_spec(cur), kv_spec(prev), kv_spec(cur),
         pl.BlockSpec((BLOCK, ATTN_W), cur),
         pl.BlockSpec(bias.shape, functools.partial(_zero_map, bias.ndim)), acc_s],
        [pl.BlockSpec((BLOCK, ATTN_W), cur), kv_spec(late), kv_spec(late), acc_b, acc_s],
        [SDS((seq, ATTN_W), BF16), SDS((seq, KV_W), BF16), SDS((seq, KV_W), BF16),
         SDS(bias.shape[1:], F32), SDS(sink_rows.shape, F32)],
        [pltpu.VMEM((BLOCK, KV_W), F32), pltpu.VMEM((BLOCK, KV_W), F32)],
        [q, k, k, v, v, d_out, bias, sink_rows], exchange, _params(1))


def _ssm_discretize(lam_re, lam_im, log_dt, b_re, b_im):
    dt = jnp.exp(log_dt)[:, None]
    mag = jnp.exp(lam_re * dt)
    ab_re = mag * jnp.cos(lam_im * dt)
    ab_im = mag * jnp.sin(lam_im * dt)
    nr = ab_re - 1.0
    den = lam_re * lam_re + lam_im * lam_im
    f_re = (nr * lam_re + ab_im * lam_im) / den
    f_im = (ab_im * lam_re - nr * lam_im) / den
    bb_re = f_re[..., None] * b_re - f_im[..., None] * b_im
    bb_im = f_re[..., None] * b_im + f_im[..., None] * b_re
    return ab_re, ab_im, bb_re, bb_im


def _state_layout(re, im):
    lead = re.shape[:-2]
    z = jnp.stack([re, im], axis=-3).reshape(lead + (2, N_SUPER, GROUPS_PER_SUPER, SSM_STATE))
    return jnp.moveaxis(z, -4, -3).reshape(lead + (STATE_COLS,))


def _state_unlayout(vec):
    z = vec.reshape(N_SUPER, 2, GROUPS_PER_SUPER, SSM_STATE).transpose(1, 0, 2, 3)
    z = z.reshape(2, SSM_GROUPS, SSM_STATE)
    return z[0], z[1]


SEG = 4
WINDOW = SEG * SUBLANES


def _scan_tables(ab_re, ab_im):
    pw = [None, (ab_re, ab_im)]
    for _ in range(2, WINDOW + 1):
        pr, pi_ = pw[-1]
        pw.append((pr * ab_re - pi_ * ab_im, pr * ab_im + pi_ * ab_re))
    fwd = np.zeros((7, SUBLANES), np.int64)
    bwd = np.zeros((7, SUBLANES), np.int64)
    for k, shift in enumerate((1, 2, 4)):
        fwd[k] = [SEG * shift if r >= shift else 0 for r in range(SUBLANES)]
        bwd[k] = [SEG * shift if r < SUBLANES - shift else 0 for r in range(SUBLANES)]
    fwd[3] = [SEG * (r + 1) for r in range(SUBLANES)]
    bwd[3] = [SEG * (SUBLANES - r) for r in range(SUBLANES)]
    for k in range(1, SEG):
        fwd[3 + k] = bwd[3 + k] = k
    used = sorted((set(fwd.ravel()) | set(bwd.ravel())) - {0})
    select = lambda which: np.stack([(which == p) for p in used], axis=-1).astype(np.float32)
    stacked = _state_layout(jnp.stack([pw[p][0] for p in used]), jnp.stack([pw[p][1] for p in used]))
    conj_sign = np.where((np.arange(STATE_COLS) // SUPER_HALF) % 2 == 1, -1.0, 1.0).astype(np.float32)
    pick = functools.partial(jnp.einsum, 'krp,pc->krc', precision=lax.Precision.HIGHEST)
    return pick(select(fwd), stacked), pick(select(bwd), stacked) * conj_sign


_EYE = np.eye(GROUPS_PER_SUPER, dtype=np.float32)


def _b_matrix(bb_re, bb_im):
    bb = jnp.stack([bb_re, bb_im]).reshape(2, N_SUPER, GROUPS_PER_SUPER, SSM_STATE, SSM_GROUP)
    m = jnp.einsum('rsgpc,gh->sgcrhp', bb, _EYE)
    return m.reshape(N_SUPER, SUPER_IN, SUPER_W)


def _b_matrix_grad(dm):
    d = dm.reshape(N_SUPER, GROUPS_PER_SUPER, SSM_GROUP, 2, GROUPS_PER_SUPER, SSM_STATE)
    d = jnp.sum(d * _EYE[None, :, None, None, :, None], axis=4)
    d = d.transpose(3, 0, 1, 4, 2).reshape(2, SSM_GROUPS, SSM_STATE, SSM_GROUP)
    return d[0], d[1]


def _c_matrix(c_re, c_im):
    cc = jnp.stack([c_re, -c_im]).reshape(2, N_SUPER, GROUPS_PER_SUPER, SSM_GROUP, SSM_STATE)
    m = jnp.einsum('rsgcp,gh->srgphc', cc, _EYE)
    return m.reshape(N_SUPER, SUPER_W, SUPER_IN)


def _c_matrix_grad(dm):
    d = dm.reshape(N_SUPER, 2, GROUPS_PER_SUPER, SSM_STATE, GROUPS_PER_SUPER, SSM_GROUP)
    d = jnp.sum(d * _EYE[None, None, :, None, :, None], axis=4)
    d = d.transpose(1, 0, 2, 4, 3).reshape(2, SSM_GROUPS, SSM_GROUP, SSM_STATE)
    return d[0], -d[1]


def _cmul_add(xr, xi, ar, ai, sr, si):
    return xr + ar * sr - ai * si, xi + ar * si + ai * sr


def _scan_rows(buf_ref, tab_ref, carry_ref, n_windows, reverse, h_ref=None, da_ref=None):
    order = list(range(SEG - 1, -1, -1)) if reverse else list(range(SEG))
    near = SUBLANES - 1 if reverse else 0
    far = 0 if reverse else SUBLANES - 1
    s_in = SUBLANES - 1 if reverse else 1
    lanes = lambda tile: pl.ds(tile * LANES, LANES)

    def window(w0, tile_re, tile_im, c_re, c_im, acc):
        rows = lambda t: pl.ds(w0 + t, SUBLANES, stride=SEG)
        get = lambda ref, t: (ref.at[tile_re][rows(t), :], ref.at[tile_im][rows(t), :])
        tab = lambda k: (tab_ref[k, :, lanes(tile_re)], tab_ref[k, :, lanes(tile_im)])

        def put(t, xr, xi):
            buf_ref.at[tile_re][rows(t), :] = xr
            buf_ref.at[tile_im][rows(t), :] = xi

        a1 = tab(4)
        er, ei = get(buf_ref, order[0])
        for t in order[1:]:
            er, ei = _cmul_add(*get(buf_ref, t), *a1, er, ei)
            if t != order[-1]:
                put(t, er, ei)
        for k, shift in enumerate((1, 2, 4)):
            s = (SUBLANES - shift) if reverse else shift
            er, ei = _cmul_add(er, ei, *tab(k), pltpu.roll(er, s, 0), pltpu.roll(ei, s, 0))
        er, ei = _cmul_add(er, ei, *tab(3), c_re, c_im)
        put(order[-1], er, ei)
        sub = lax.broadcasted_iota(jnp.int32, er.shape, 0)
        in_re = jnp.where(sub == near, c_re, pltpu.roll(er, s_in, 0))
        in_im = jnp.where(sub == near, c_im, pltpu.roll(ei, s_in, 0))
        true = {order[-1]: (er, ei)}
        for idx, t in enumerate(order[:-1]):
            true[t] = _cmul_add(*get(buf_ref, t), *tab(4 + idx), in_re, in_im)
            put(t, *true[t])
        carry = (jnp.broadcast_to(er[far:far + 1], er.shape), jnp.broadcast_to(ei[far:far + 1], ei.shape))
        if acc is None:
            return carry, None
        acc_re, acc_im = acc
        for t in range(SEG):
            if t + 1 < SEG:
                gr, gim = true[t + 1]
            else:
                gr = jnp.where(sub == SUBLANES - 1, c_re, pltpu.roll(true[0][0], SUBLANES - 1, 0))
                gim = jnp.where(sub == SUBLANES - 1, c_im, pltpu.roll(true[0][1], SUBLANES - 1, 0))
            hr, hi = get(h_ref, t)
            acc_re = acc_re + gr * hr + gim * hi
            acc_im = acc_im + gim * hr - gr * hi
        return carry, (acc_re, acc_im)

    half = SUPER_HALF // LANES
    per = 2 if h_ref is None else 4
    for sb in range(N_SUPER):
        pairs = [(2 * half * sb + j, 2 * half * sb + half + j) for j in range(half)]

        def step(wi, state, pairs=pairs):
            w = (n_windows - 1 - wi) if reverse else wi
            w0 = pl.multiple_of(w * WINDOW, WINDOW)
            out = []
            for j, (tile_re, tile_im) in enumerate(pairs):
                mine = state[per * j:per * (j + 1)]
                carry, acc = window(w0, tile_re, tile_im, mine[0], mine[1], mine[2:] or None)
                out += list(carry) + list(acc or ())
            return tuple(out)

        init = []
        for tile_re, tile_im in pairs:
            init += [carry_ref[:, lanes(tile_re)], carry_ref[:, lanes(tile_im)]]
            if h_ref is not None:
                init += [da_ref[:, lanes(tile_re)], da_ref[:, lanes(tile_im)]]
        fin = lax.fori_loop(0, n_windows, step, tuple(init))
        for j, (tile_re, tile_im) in enumerate(pairs):
            carry_ref[:, lanes(tile_re)] = fin[per * j]
            carry_ref[:, lanes(tile_im)] = fin[per * j + 1]
            if h_ref is not None:
                da_ref[:, lanes(tile_re)] = fin[per * j + 2]
                da_ref[:, lanes(tile_im)] = fin[per * j + 3]


def _put_tiles(ref, sb, block):
    for j in range(SUPER_TILES):
        ref[sb * SUPER_TILES + j] = block[:, j * LANES:(j + 1) * LANES]


def _get_tiles(ref, sb):
    return jnp.concatenate([ref[sb * SUPER_TILES + j] for j in range(SUPER_TILES)], axis=1)


def _ssm_fwd(u, bmat, cmat, tab, d_skip, tb, exchange=None):
    seq = u.shape[0]

    def body(u_ref, b_ref, c_ref, t_ref, d_ref, s_ref, h_ref, carry_ref):
        @pl.when(pl.program_id(0) == 0)
        def _():
            carry_ref[...] = jnp.zeros_like(carry_ref)

        u_blk = u_ref[...]
        ub = _bf(u_blk)
        for sb in range(N_SUPER):
            _put_tiles(h_ref, sb, _mm(ub[:, sb * SUPER_IN:(sb + 1) * SUPER_IN], b_ref[sb]))
        _scan_rows(h_ref, t_ref, carry_ref, tb // WINDOW, False)
        ys = [_mm(_bf(_get_tiles(h_ref, sb)), c_ref[sb]) for sb in range(N_SUPER)]
        s_ref[...] = jnp.concatenate(ys, axis=1) + d_ref[...] * u_blk

    return _rowcall("ssm_fwd", body, seq, tb, [u], [bmat, cmat, tab, d_skip],
                    [(SSM_W, F32), ((STATE_TILES, LANES), F32)], [],
                    scratch=[pltpu.VMEM((SUBLANES, STATE_COLS), F32)], vmem=VMEM_BIG, exchange=exchange)


def _ssm_bwd(ds, u, h, bmat_t, cmat_t, tab, d_skip, tb, exchange=None):
    seq = u.shape[0]

    def body(ds_ref, u_ref, h_ref, bt_ref, ct_ref, t_ref, d_ref,
             du_ref, db_ref, dc_ref, da_ref, dd_ref, g_ref, carry_ref):
        @pl.when(pl.program_id(0) == 0)
        def _():
            carry_ref[...] = jnp.zeros_like(carry_ref)
            db_ref[...] = jnp.zeros_like(db_ref)
            dc_ref[...] = jnp.zeros_like(dc_ref)
            da_ref[...] = jnp.zeros_like(da_ref)
            dd_ref[...] = jnp.zeros_like(dd_ref)

        ds_blk = ds_ref[...]
        dsb = _bf(ds_blk)
        u_blk = u_ref[...]
        ub = _bf(u_blk)
        for sb in range(N_SUPER):
            _put_tiles(g_ref, sb, _mm(dsb[:, sb * SUPER_IN:(sb + 1) * SUPER_IN], ct_ref[sb]))
        _scan_rows(g_ref, t_ref, carry_ref, tb // WINDOW, True, h_ref=h_ref, da_ref=da_ref)
        dus = []
        for sb in range(N_SUPER):
            gb = _bf(_get_tiles(g_ref, sb))
            dus.append(_mm(gb, bt_ref[sb]))
            db_ref[sb] += _mm_tn(ub[:, sb * SUPER_IN:(sb + 1) * SUPER_IN], gb)
            dc_ref[sb] += _mm_tn(_bf(_get_tiles(h_ref, sb)), dsb[:, sb * SUPER_IN:(sb + 1) * SUPER_IN])
        du_ref[...] = jnp.concatenate(dus, axis=1) + d_ref[...] * ds_blk
        dd_ref[...] += jnp.sum(ds_blk * u_blk, axis=0, keepdims=True)

    return _rowcall("ssm_bwd", body, seq, tb, [ds, u, h], [bmat_t, cmat_t, tab, d_skip],
                    [(SSM_W, F32)],
                    [((N_SUPER, SUPER_IN, SUPER_W), F32), ((N_SUPER, SUPER_W, SUPER_IN), F32),
                     ((SUBLANES, STATE_COLS), F32), ((1, SSM_W), F32)],
                    scratch=[pltpu.VMEM((STATE_TILES, tb, LANES), F32), pltpu.VMEM((SUBLANES, STATE_COLS), F32)],
                    reverse=True, vmem=VMEM_BIG, exchange=exchange)


def _merge_core(s, attb, ga, gs, wg_ref, wab_ref, wsb_ref, wout_ref):
    zg, dgelu = _gelu_and_grad(s)
    zgb = _bf(zg)
    sg = _sig(_mm(zgb, wg_ref[...]))
    z = zg * sg
    zb = _bf(z)
    ys = jnp.concatenate([_mm(zb, wsb_ref[j]) for j in range(N_CHIPS)], axis=1)
    ya = jnp.concatenate([_mm(attb, wab_ref[j]) for j in range(N_CHIPS)], axis=1)
    sa = _sig(ga)
    ss = _sig(gs)
    mgb = _bf(sa * ya + ss * ys)
    o = _mm(mgb, wout_ref[...])
    return dict(zg=zg, dgelu=dgelu, zgb=zgb, sg=sg, zb=zb, ys=ys, ya=ya, sa=sa, ss=ss, mgb=mgb, o=o)


def _merge_fwd(x, s, att, ga, gs, g2, w_glu, w_ab, w_sb, w_out, tb, exchange=None):
    seq = x.shape[0]

    def body(x_ref, s_ref, att_ref, ga_ref, gs_ref, g_ref, wg_ref, wab_ref, wsb_ref, wout_ref, x2_ref):
        f = _merge_core(s_ref[...], att_ref[...], ga_ref[...], gs_ref[...], wg_ref, wab_ref, wsb_ref, wout_ref)
        n, _, _ = _rms(f["o"], g_ref[...])
        x2_ref[...] = x_ref[...] + n

    return _rowcall("merge_fwd", body, seq, tb, [x, s, att, ga, gs], [g2, w_glu, w_ab, w_sb, w_out],
                    [(D_MODEL, F32)], [], vmem=VMEM_BIG, exchange=exchange)[0]


def _merge_bwd(dx2, s, att, ga, gs, g2, w_glu, w_ab, w_sb, w_out, tb, exchange=None):
    seq = s.shape[0]
    cw = D_MODEL // N_CHIPS
    last = seq // tb - 1

    def body(dx2_ref, s_ref, att_ref, ga_ref, gs_ref, g_ref, wg_ref, wab_ref, wsb_ref, wout_ref,
             ds_ref, datt_ref, dga_ref, dgs_ref, dg_ref, dwg_ref, dwab_ref, dwsb_ref, dwout_ref,
             bwg_ref, bwab_ref, bwsb_ref, bwout_ref):
        @pl.when(pl.program_id(0) == 0)
        def _():
            for r in (dg_ref, dwg_ref, dwab_ref, dwsb_ref, dwout_ref):
                r[...] = jnp.zeros_like(r)

        attb = att_ref[...]
        f = _merge_core(s_ref[...], attb, ga_ref[...], gs_ref[...], wg_ref, wab_ref, wsb_ref, wout_ref)
        g = g_ref[...]
        _, oh, r2 = _rms(f["o"], g)
        do, dg = _rms_bwd(dx2_ref[...], oh, r2, g)
        dg_ref[...] += dg
        dob = _bf(do)
        dwout_ref[...] += _mm_tn(f["mgb"], dob)
        dmg = _mm_nt(dob, wout_ref[...])
        sa, ss = f["sa"], f["ss"]
        dyab = _bf(dmg * sa)
        dysb = _bf(dmg * ss)
        dga_ref[...] = _bf(dmg * f["ya"] * sa * (1.0 - sa))
        dgs_ref[...] = _bf(dmg * f["ys"] * ss * (1.0 - ss))
        dwab = _mm_tn(attb, dyab)
        dwsb = _mm_tn(f["zb"], dysb)
        datt = jnp.zeros((tb, ATTN_W), F32)
        dz = jnp.zeros((tb, SSM_W), F32)
        for j in range(N_CHIPS):
            dwab_ref[j] += dwab[:, j * cw:(j + 1) * cw]
            dwsb_ref[j] += dwsb[:, j * cw:(j + 1) * cw]
            datt = datt + _mm_nt(dyab[:, j * cw:(j + 1) * cw], wab_ref[j])
            dz = dz + _mm_nt(dysb[:, j * cw:(j + 1) * cw], wsb_ref[j])
        datt_ref[...] = _bf(datt)
        sg, zg = f["sg"], f["zg"]
        dglb = _bf(dz * zg * sg * (1.0 - sg))
        dwg_ref[...] += _mm_tn(f["zgb"], dglb)
        dzg = dz * sg + _mm_nt(dglb, wg_ref[...])
        ds_ref[...] = dzg * f["dgelu"]

        @pl.when(pl.program_id(0) == last)
        def _():
            for dst, src in ((bwg_ref, dwg_ref), (bwab_ref, dwab_ref), (bwsb_ref, dwsb_ref), (bwout_ref, dwout_ref)):
                dst[...] = _bf(src[...])

    shapes = [w_glu.shape, w_ab.shape, w_sb.shape, w_out.shape]
    return _rowcall("merge_bwd", body, seq, tb, [dx2, s, att, ga, gs], [g2, w_glu, w_ab, w_sb, w_out],
                    [(SSM_W, F32), (ATTN_W, BF16), (D_MODEL, BF16), (D_MODEL, BF16)],
                    [((1, D_MODEL), F32)] + [(sh, F32) for sh in shapes] + [(sh, BF16) for sh in shapes],
                    vmem=VMEM_BIG, exchange=exchange)


def _mlp_fwd_loss(x2, target, g3, g4, w_ffi, w_ffo, tb):
    seq = x2.shape[0]
    n_slab = len(w_ffi)
    sw = D_FF // FF_CHUNKS // n_slab

    def body(x2_ref, t_ref, g3_ref, g4_ref, *rest):
        wi_refs, (wo_ref, dy_ref, df_ref, h_ref, ra_ref, loss_ref, dg_ref) = rest[:n_slab], rest[n_slab:]

        @pl.when(pl.program_id(0) == 0)
        def _():
            loss_ref[...] = jnp.zeros_like(loss_ref)
            dg_ref[...] = jnp.zeros_like(dg_ref)

        x2_blk = x2_ref[...]
        h3, _, _ = _rms(x2_blk, g3_ref[...])
        hb = _bf(h3)
        h_ref[...] = hb
        f = jnp.zeros((tb, D_MODEL), F32)
        for j in range(FF_CHUNKS):
            for k in range(n_slab):
                ra = jnp.maximum(_mm(hb, wi_refs[k][j]), 0.0)
                ra_ref[:, pl.ds((j * n_slab + k) * sw, sw)] = _bf(ra)
                f = f + _mm(_bf(ra * ra), wo_ref[j, pl.ds(k * sw, sw), :])
        g4 = g4_ref[...]
        n4, fh, r4 = _rms(f, g4)
        e = (x2_blk + n4) - t_ref[...]
        loss_ref[...] += 0.5 * jnp.sum(jnp.mean(e * e, axis=-1, keepdims=True))
        dy = e * (1.0 / D_MODEL)
        dy_ref[...] = dy
        df, dg = _rms_bwd(dy, fh, r4, g4)
        df_ref[...] = _bf(df)
        dg_ref[...] += dg

    return _rowcall("mlp_fwd_loss", body, seq, tb, [x2, target], [g3, g4, *w_ffi, w_ffo],
                    [(D_MODEL, F32), (D_MODEL, BF16), (D_MODEL, BF16), (D_FF, BF16)],
                    [((SUBLANES, 128), F32), ((1, D_MODEL), F32)], vmem=VMEM_BIG)


def _mlp_bwd(x2, dy, df, ra, g3, w_ffi, w_ffo, tb):
    seq = x2.shape[0]
    n_slab = len(w_ffi)
    sw = D_FF // FF_CHUNKS // n_slab

    def body(x2_ref, dy_ref, df_ref, ra_ref, g3_ref, *rest):
        wi_refs, (wo_ref, dx_ref, da_ref, dg_ref) = rest[:n_slab], rest[n_slab:]

        @pl.when(pl.program_id(0) == 0)
        def _():
            dg_ref[...] = jnp.zeros_like(dg_ref)

        dfb = df_ref[...]
        dh = jnp.zeros((tb, D_MODEL), F32)
        for j in range(FF_CHUNKS):
            for k in range(n_slab):
                cols = pl.ds((j * n_slab + k) * sw, sw)
                ra = ra_ref[:, cols].astype(F32)
                dab = _bf(_mm_nt(dfb, wo_ref[j, pl.ds(k * sw, sw), :]) * (2.0 * ra))
                da_ref[:, cols] = dab
                dh = dh + _mm_nt(dab, wi_refs[k][j])
        g3 = g3_ref[...]
        _, xh, r3 = _rms(x2_ref[...], g3)
        dxn, dg = _rms_bwd(dh, xh, r3, g3)
        dx_ref[...] = dy_ref[...] + dxn
        dg_ref[...] += dg

    return _rowcall("mlp_bwd", body, seq, tb, [x2, dy, df, ra], [g3, *w_ffi, w_ffo],
                    [(D_MODEL, F32), (D_FF, BF16)], [((1, D_MODEL), F32)], vmem=VMEM_BIG)


def _matmul_tn(name, a, b, tk, tn, tl, chunk_major, exchange=None, square_a=False):
    seq, kdim = a.shape
    ndim = b.shape[1]
    last = seq // tl - 1

    def body(a_ref, b_ref, o_ref, ob_ref):
        @pl.when(pl.program_id(2) == 0)
        def _():
            o_ref[...] = jnp.zeros_like(o_ref)

        a_blk = a_ref[...]
        if square_a:
            a_blk = _bf(jnp.square(a_blk.astype(F32)))
        o_ref[...] += _mm_tn(a_blk, b_ref[...])

        @pl.when(pl.program_id(2) == last)
        def _():
            ob_ref[...] = _bf(o_ref[...])

    if chunk_major:
        shape = (ndim // tn, kdim, tn)
        out_spec = pl.BlockSpec((None, tk, tn), lambda k, n, l: (n, k, 0))
    else:
        shape = (kdim, ndim)
        out_spec = pl.BlockSpec((tk, tn), lambda k, n, l: (k, n))
    return _fused_call(
        name, body, (kdim // tk, ndim // tn, seq // tl),
        [pl.BlockSpec((tl, tk), lambda k, n, l: (l, k)), pl.BlockSpec((tl, tn), lambda k, n, l: (l, n))],
        [out_spec, out_spec], [SDS(shape, F32), SDS(shape, BF16)], [], [a, b], exchange, _params(3, VMEM_BIG))


def _ew_call(name, fn, ins, n_out, after=None):
    rows, cols = ins[0].shape
    tr = rows
    while tr * cols * 4 > min(1 << 20, (9 << 20) // (len(ins) + n_out)) and tr % 16 == 0:
        tr //= 2
    spec = pl.BlockSpec((tr, cols), lambda i: (i, 0))
    extra = [] if after is None else [after]

    def body(*refs):
        outs = fn(*[r[...] for r in refs[:len(ins)]])
        for r, o in zip(refs[len(ins) + len(extra):], outs):
            r[...] = o

    return pl.pallas_call(
        body, grid=(rows // tr,), in_specs=[spec] * len(ins) + [ANY] * len(extra), out_specs=[spec] * n_out,
        out_shape=[SDS((rows, cols), F32)] * n_out, name=name, compiler_params=_params(1))(*ins, *extra)


def _adam_math(w, g, m, v):
    m2 = ADAM_B1 * m + (1.0 - ADAM_B1) * g
    v2 = ADAM_B2 * v + (1.0 - ADAM_B2) * (g * g)
    m_hat = m2 / (1.0 - ADAM_B1 ** ADAM_STEP)
    v_hat = v2 / (1.0 - ADAM_B2 ** ADAM_STEP)
    delta = -ADAM_LR * (m_hat / (jnp.sqrt(v_hat) + ADAM_EPS) + ADAM_WD * w)
    return delta, m2, v2


def _sum4(name, own, recv, idx):
    _, rows, cols = own.shape
    tr = rows
    while tr * cols * 4 > (1 << 20) and tr % 16 == 0:
        tr //= 2

    def body(idx_ref, o_ref, r0_ref, r1_ref, r2_ref, out_ref):
        out_ref[...] = ((o_ref[...] + r0_ref[...].astype(F32)) + r1_ref[...].astype(F32)) + r2_ref[...].astype(F32)

    blk = (None, tr, cols)
    grid_spec = pltpu.PrefetchScalarGridSpec(
        num_scalar_prefetch=1, grid=(rows // tr,),
        in_specs=[pl.BlockSpec(blk, lambda i, s: (s[0], i, 0)), pl.BlockSpec(blk, lambda i, s: (0, i, 0)),
                  pl.BlockSpec(blk, lambda i, s: (1, i, 0)), pl.BlockSpec(blk, lambda i, s: (2, i, 0))],
        out_specs=pl.BlockSpec((tr, cols), lambda i, s: (i, 0)))
    return pl.pallas_call(body, grid_spec=grid_spec, out_shape=SDS((rows, cols), F32), name=name,
                          compiler_params=_params(1))(jnp.reshape(idx, (1,)).astype(jnp.int32), own, recv, recv, recv)


def _adam_pair(name, item, after=None):
    def fn(w_, a, b, m_, v_):
        g = a + b
        return (g,) + _adam_math(w_, g, m_, v_)

    return _ew_call(name, fn, list(item), 4, after)


def _place():
    return lax.axis_index("x"), lax.axis_index("y"), lax.axis_index("c")


def _other_chips(x, y):
    return [(1 - x, y), (x, 1 - y), (1 - x, 1 - y)]


HBM = pl.BlockSpec(memory_space=pltpu.HBM)
SEM = pl.BlockSpec(memory_space=pltpu.SEMAPHORE)
DATAFLOW = pltpu.SideEffectType.DATAFLOW_SIDE_EFFECTING


class _Flight:
    def __init__(self, copies, n_copies, send, recv, srcs, lands, token):
        self.copies, self.n, self.send, self.recv = copies, n_copies, send, recv
        self.srcs, self.lands, self.token = list(srcs), list(lands), token


def _take_off(name, srcs, lands, copies, n_copies, after):
    n_s, n_l = len(srcs), len(lands)

    def body(*refs):
        src, land = refs[:n_s], refs[n_s:n_s + n_l]
        send, recv = refs[n_s + n_l + 1:n_s + n_l + 3]
        for cp in copies(src, land, send, recv):
            cp.start()
        refs[-1][...] = jnp.zeros_like(refs[-1])

    mem = lambda t: pltpu.HBM(t.shape, t.dtype)
    sems = pltpu.SemaphoreType.DMA((n_copies,))
    outs = pl.pallas_call(
        body, name=name,
        out_shape=(sems, sems, *map(mem, srcs), *map(mem, lands), SDS((SUBLANES, LANES), F32)),
        in_specs=[HBM] * (n_s + n_l) + [ANY],
        out_specs=(SEM, SEM, *[HBM] * (n_s + n_l), pl.BlockSpec(memory_space=pltpu.VMEM)),
        input_output_aliases={i: 2 + i for i in range(n_s + n_l)},
        compiler_params=pltpu.CompilerParams(has_side_effects=DATAFLOW),
    )(*[pltpu.with_memory_space_constraint(t, pltpu.HBM) for t in (*srcs, *lands)], after)
    return _Flight(copies, n_copies, outs[0], outs[1], outs[2:2 + n_s], outs[2 + n_s:2 + n_s + n_l], outs[-1])


def _land(name, flight, after):
    n_s, n_l = len(flight.srcs), len(flight.lands)

    def body(*refs):
        src, land = refs[:n_s], refs[n_s:n_s + n_l]
        send, recv = refs[n_s + n_l:n_s + n_l + 2]
        for cp in flight.copies(src, land, send, recv):
            cp.wait_send()
            cp.wait_recv()

    mem = lambda t: pltpu.HBM(t.shape, t.dtype)
    outs = pl.pallas_call(
        body, name=name, out_shape=(*map(mem, flight.srcs), *map(mem, flight.lands)),
        in_specs=[HBM] * (n_s + n_l) + [SEM, SEM, ANY], out_specs=tuple([HBM] * (n_s + n_l)),
        input_output_aliases={i: i for i in range(n_s + n_l)},
        compiler_params=pltpu.CompilerParams(has_side_effects=DATAFLOW),
    )(*flight.srcs, *flight.lands, flight.send, flight.recv, after)
    return list(outs[:n_s]), list(outs[n_s:])


def _empty_like(shapes_from, lead):
    return [lax.empty((lead,) + t.shape[1:], t.dtype) for t in shapes_from]


def _scatter_off(name, chunks, after):
    def copies(src, land, send, recv):
        x, y, c = _place()
        return [pltpu.make_async_remote_copy(
            src_ref=src[a].at[2 * px + py], dst_ref=land[a].at[k], send_sem=send.at[3 * a + k],
            recv_sem=recv.at[3 * a + k], device_id=(px, py, c), device_id_type=MESH_ID)
            for a in range(len(chunks)) for k, (px, py) in enumerate(_other_chips(x, y))]

    return _take_off(name, chunks, _empty_like(chunks, 3), copies, 3 * len(chunks), after)


def _swap_off(name, arrs, after):
    def copies(src, land, send, recv):
        x, y, c = _place()
        return [pltpu.make_async_remote_copy(
            src_ref=src[a], dst_ref=land[a], send_sem=send.at[a], recv_sem=recv.at[a],
            device_id=(x, y, 1 - c), device_id_type=MESH_ID) for a in range(len(arrs))]

    return _take_off(name, arrs, [lax.empty(t.shape, t.dtype) for t in arrs], copies, len(arrs), after)


def _devices_off(name, block, after):
    me = 4 * lax.axis_index("x") + 2 * lax.axis_index("y") + lax.axis_index("c")
    land = lax.dynamic_update_index_in_dim(lax.empty((N_DEV,) + block.shape, block.dtype), block, me, 0)

    def copies(src, land, send, recv):
        x, y, c = _place()
        mine = 4 * x + 2 * y + c
        return [pltpu.make_async_remote_copy(
            src_ref=src[0], dst_ref=land[0].at[mine], send_sem=send.at[k - 1], recv_sem=recv.at[k - 1],
            device_id=(x ^ (k >> 2), y ^ ((k >> 1) & 1), c ^ (k & 1)), device_id_type=MESH_ID)
            for k in range(1, N_DEV)]

    return _take_off(name, [block], [land], copies, N_DEV - 1, after)


def _half_rows(shape, c, other=False):
    half = shape[0] // 2
    return pl.ds(((1 - c) if other else c) * half, half)


def _gather_start(name, shards, lands, after):
    n = len(shards)

    def body(*refs):
        src, land, (send, recv) = refs[:n], refs[n:2 * n], refs[2 * n + 1:2 * n + 3]
        x, y, c = _place()
        me = 2 * x + y
        for a in range(n):
            mine = _half_rows(shards[a].shape, c)
            for j, (px, py) in enumerate(_other_chips(x, y)):
                pltpu.make_async_remote_copy(
                    src_ref=src[a].at[mine], dst_ref=land[a].at[me, mine], send_sem=send.at[3 * a + j],
                    recv_sem=recv.at[3 * a + j], device_id=(px, py, c), device_id_type=MESH_ID).start()
        token = refs[-1]
        token[...] = jnp.zeros_like(token)

    mem = lambda t: pltpu.HBM(t.shape, t.dtype)
    pair = pltpu.SemaphoreType.DMA((3 * n,))
    outs = pl.pallas_call(
        body, name=name,
        out_shape=(pair, pair, *map(mem, shards), *map(mem, lands), SDS((SUBLANES, LANES), F32)),
        in_specs=[HBM] * (2 * n) + [ANY],
        out_specs=(SEM, SEM, *[HBM] * (2 * n), pl.BlockSpec(memory_space=pltpu.VMEM)),
        input_output_aliases={i: 2 + i for i in range(2 * n)},
        compiler_params=pltpu.CompilerParams(has_side_effects=DATAFLOW),
    )(*[pltpu.with_memory_space_constraint(t, pltpu.HBM) for t in (*shards, *lands)], after)
    return outs[0], outs[1], list(outs[2:2 + n]), list(outs[2 + n:2 + 2 * n]), outs[-1]


def _gather_pass(name, send, recv, shards, lands, after, first=0):
    n = len(shards)

    def body(*refs):
        src, land, (send, recv, _) = refs[:n], refs[n:2 * n], refs[2 * n:2 * n + 3]
        fsend, frecv = refs[2 * n + 3], refs[2 * n + 4]
        x, y, c = _place()
        me = 2 * x + y
        for a in range(n):
            mine = _half_rows(shards[a].shape, c)
            for j, (px, py) in enumerate(_other_chips(x, y)):
                far = 2 * px + py
                ici = pltpu.make_async_remote_copy(
                    src_ref=src[a].at[mine], dst_ref=land[a].at[far, mine], send_sem=send.at[3 * (first + a) + j],
                    recv_sem=recv.at[3 * (first + a) + j], device_id=(px, py, c), device_id_type=MESH_ID)
                ici.wait_recv()
                ici.wait_send()
                pltpu.make_async_remote_copy(
                    src_ref=land[a].at[far, mine], dst_ref=land[a].at[far, mine], send_sem=fsend.at[3 * a + j],
                    recv_sem=frecv.at[3 * a + j], device_id=(x, y, 1 - c), device_id_type=MESH_ID).start()
        token = refs[-1]
        token[...] = jnp.zeros_like(token)

    mem = lambda t: pltpu.HBM(t.shape, t.dtype)
    pair = pltpu.SemaphoreType.DMA((3 * n,))
    outs = pl.pallas_call(
        body, name=name,
        out_shape=(pair, pair, *map(mem, lands), SDS((SUBLANES, LANES), F32)),
        in_specs=[HBM] * (2 * n) + [SEM, SEM, ANY],
        out_specs=(SEM, SEM, *[HBM] * n, pl.BlockSpec(memory_space=pltpu.VMEM)),
        input_output_aliases={n + i: 2 + i for i in range(n)},
        compiler_params=pltpu.CompilerParams(has_side_effects=DATAFLOW),
    )(*shards, *lands, send, recv, after)
    return outs[0], outs[1], list(outs[2:2 + n]), outs[-1]


def _gather_wait(name, fsend, frecv, lands, after):
    n = len(lands)

    def body(*refs):
        land, (fsend, frecv, _) = refs[:n], refs[n:n + 3]
        x, y, c = _place()
        for a in range(n):
            for j, (px, py) in enumerate(_other_chips(x, y)):
                far = 2 * px + py
                mine = _half_rows(lands[a].shape[1:], c)
                theirs = _half_rows(lands[a].shape[1:], c, other=True)
                pltpu.make_async_remote_copy(
                    src_ref=land[a].at[far, mine], dst_ref=land[a].at[far, mine], send_sem=fsend.at[3 * a + j],
                    recv_sem=frecv.at[3 * a + j], device_id=(x, y, 1 - c), device_id_type=MESH_ID).wait_send()
                pltpu.make_async_remote_copy(
                    src_ref=land[a].at[far, theirs], dst_ref=land[a].at[far, theirs], send_sem=fsend.at[3 * a + j],
                    recv_sem=frecv.at[3 * a + j], device_id=(x, y, 1 - c), device_id_type=MESH_ID).wait_recv()

    mem = lambda t: pltpu.HBM(t.shape, t.dtype)
    return list(pl.pallas_call(
        body, name=name, out_shape=tuple(map(mem, lands)), in_specs=[HBM] * n + [SEM, SEM, ANY],
        out_specs=tuple([HBM] * n), input_output_aliases={i: i for i in range(n)},
        compiler_params=pltpu.CompilerParams(has_side_effects=DATAFLOW),
    )(*lands, fsend, frecv, after))


def _after(token):
    return _Exchange([token], [], [], lambda *_: None, lambda *_: None)


def _swap_sibling(arrs):
    n = len(arrs)

    def copies(ins, outs, sems):
        send, recv = sems
        x, y, c = _place()
        return [pltpu.make_async_remote_copy(
            src_ref=ins[a], dst_ref=outs[a], send_sem=send.at[a], recv_sem=recv.at[a],
            device_id=(x, y, 1 - c), device_id_type=MESH_ID) for a in range(n)]

    def start(ins, outs, sems):
        for cp in copies(ins, outs, sems):
            cp.start()

    def wait(ins, outs, sems):
        cps = copies(ins, outs, sems)
        for cp in cps:
            cp.wait_recv()
        for cp in cps:
            cp.wait_send()

    return _Exchange(arrs, [SDS(s.shape, s.dtype) for s in arrs],
                     [pltpu.SemaphoreType.DMA((n,)), pltpu.SemaphoreType.DMA((n,))], start, wait)


def _sum_devices(slots):
    def body(s_ref, o_ref):
        acc = s_ref[0]
        for d in range(1, N_DEV):
            acc = acc + s_ref[d]
        o_ref[...] = acc

    return pl.pallas_call(
        body, in_specs=[pl.BlockSpec(memory_space=pltpu.VMEM)], out_specs=pl.BlockSpec(memory_space=pltpu.VMEM),
        out_shape=SDS(slots.shape[1:], F32), name="sum_small",
        compiler_params=pltpu.CompilerParams(vmem_limit_bytes=32 * 1024 * 1024))(slots)


def _adam_small(ws, gs, ms, vs):
    n = len(ws)

    def body(*refs):
        for i in range(n):
            w_ref, g_ref, m_ref, v_ref = (refs[k * n + i] for k in range(4))
            outs = _adam_math(w_ref[...], g_ref[...], m_ref[...], v_ref[...])
            for k in range(3):
                refs[(4 + k) * n + i][...] = outs[k]

    vmem = pl.BlockSpec(memory_space=pltpu.VMEM)
    return pl.pallas_call(
        body, in_specs=[vmem] * (4 * n), out_specs=[vmem] * (3 * n),
        out_shape=[SDS(w.shape, F32) for w in ws] * 3, name="adam_small",
        compiler_params=pltpu.CompilerParams(vmem_limit_bytes=32 * 1024 * 1024))(*ws, *gs, *ms, *vs)


def _local_step(x, target, small, big, tb, distributed):
    dist = distributed
    me = (2 * lax.axis_index("x") + lax.axis_index("y")) if dist else 0
    tb_ssm = min(tb, 256)
    bucket = jnp.asarray(_bucket_table())
    place_own = lambda t: lax.dynamic_update_index_in_dim(lax.empty((N_CHIPS,) + t.shape, t.dtype), t, me, 0)
    if dist:
        in_legs = _gather_start("gather_in_start", [big["w_in"]], [place_own(big["w_in"])], small["d_skip"])
        names = sorted(small)
        in_token, values = lax.optimization_barrier((in_legs[4], [small[n] for n in names]))
        small = dict(zip(names, values))
    g1, g2, g3, g4 = small["norm_mix_pre"], small["norm_mix_post"], small["norm_mlp_pre"], small["norm_mlp_post"]

    keys_first = lambda t: jnp.swapaxes(t, -1, -2)
    bias = _bias_table(small["rel_bias"], bucket)
    sink_rows = keys_first(_pair_layout(jnp.broadcast_to(small["sinks"].reshape(N_HEADS, 1, 1), (N_HEADS, BLOCK, 1))))
    disc_args = (small["lam_re"], small["lam_im"], small["log_dt"], small["b_re"], small["b_im"])
    (ab_re, ab_im, bb_re, bb_im), disc_vjp = jax.vjp(_ssm_discretize, *disc_args)
    tab_f, tab_b = _scan_tables(ab_re, ab_im)
    bmat = _bf(_b_matrix(bb_re, bb_im))
    cmat = _bf(_c_matrix(small["c_re"], small["c_im"]))
    d_skip = small["d_skip"]

    mix = ("w_glu", "w_attn_branch", "w_ssm_branch", "w_out")
    rest = [big[n] for n in mix + ("w_ff_in", "w_ff_out")]
    if dist:
        send, recv, src, lands, _ = in_legs
        rest_lands = [place_own(t) for t in rest]
        corner = lambda t: t.reshape(-1, t.shape[-1])[:1, :LANES].astype(F32)
        prepared = sum(map(corner, [tab_b, bias, sink_rows, bmat, cmat] + rest_lands), in_token[:1])
        send, recv, lands, in_passed = _gather_pass("gather_in_pass", send, recv, src, lands, prepared)
        (g_in,) = _gather_wait("gather_in_wait", send, recv, lands, in_passed)
        w_in = g_in.reshape(IN_W, D_MODEL)
    else:
        w_in = big["w_in"]
    token = None
    n_mix = len(mix)
    if dist:
        send, recv, rest, lands, token = _gather_start("gather_rest_start", rest, rest_lands, in_passed)
    h1, q, k, v, u, ga, gs = _inproj_fwd(x, g1, w_in, tb, _after(token) if dist else None)
    s, h = _ssm_fwd(u, bmat, cmat, tab_f, d_skip, tb)
    if dist:
        fsend, frecv, mix_lands, token = _gather_pass("gather_mix_pass", send, recv, rest[:n_mix], lands[:n_mix], s)
    att = _attn_fwd(q, k, v, bias, sink_rows, _after(token) if dist else None)[0]
    if dist:
        w_mix = _gather_wait("gather_mix_wait", fsend, frecv, mix_lands, att)
        fsend, frecv, ff_lands, token = _gather_pass(
            "gather_ff_pass", send, recv, rest[n_mix:], lands[n_mix:], w_mix[0], n_mix)
        rest = w_mix + ff_lands
    w_glu, w_ab, w_sb, w_out = rest[:n_mix]
    w_glu = w_glu.reshape(SSM_W, SSM_W)
    w_out = w_out.reshape(D_MODEL, D_MODEL)
    x2 = _merge_fwd(x, s, att, ga, gs, g2, w_glu, w_ab, w_sb, w_out, tb, _after(token) if dist else None)
    if dist:
        rest[n_mix:] = _gather_wait("gather_ff_wait", fsend, frecv, ff_lands, x2)
    w_ffi, w_ffo = [rest[n_mix]], rest[n_mix + 1]
    dy, df, h3, ra, loss_acc, dg4 = _mlp_fwd_loss(x2, target, g3, g4, w_ffi, w_ffo, tb)

    dx2, da, dg3 = _mlp_bwd(x2, dy, df, ra, g3, w_ffi, w_ffo, tb)
    tl = min(2048, x.shape[0])
    chunked = (N_CHIPS, D_FF // N_CHIPS, D_MODEL)
    d_ffi, b_ffi = _matmul_tn("grad_w_ff_in", h3, da, D_MODEL, D_FF // FF_CHUNKS, tl, True)
    d_ffo, b_ffo = _matmul_tn("grad_w_ff_out", ra, df, D_FF // FF_CHUNKS, D_MODEL, tl, False, square_a=True)
    d_ffo, b_ffo = d_ffo.reshape(chunked), b_ffo.reshape(chunked)
    behind = lambda flight: _after(flight.token) if dist else None
    ff_fl = _scatter_off("scatter_ff_off", [b_ffi, b_ffo], d_ffo) if dist else None
    outs = _merge_bwd(dx2, s, att, ga, gs, g2, w_glu, w_ab, w_sb, w_out, tb_ssm, behind(ff_fl))
    ds, datt, dga, dgs, dg2, d_glu, d_ab, d_sb, d_out, b_glu, b_ab, b_sb, b_out = outs
    glu4, out4 = (N_CHIPS, SSM_W // N_CHIPS, SSM_W), (N_CHIPS, D_MODEL // N_CHIPS, D_MODEL)
    d_mix = [d_glu.reshape(glu4), d_ab, d_sb, d_out.reshape(out4)]
    b_mix = [b_glu.reshape(glu4), b_ab, b_sb, b_out.reshape(out4)]
    mix_fl = _scatter_off("scatter_mix_off", b_mix, d_mix[-1]) if dist else None
    du, d_bmat, d_cmat, da_acc, dd_skip = _ssm_bwd(
        ds, u, h, bmat.transpose(0, 2, 1), cmat.transpose(0, 2, 1), tab_b, d_skip, tb, behind(mix_fl))
    dq, dk, dv, dbias, dsink_rows = _attn_bwd(q, k, v, datt, bias, sink_rows)
    swap_fl = None
    if dist:
        r_ffi, r_ffo = _land("scatter_ff_land", ff_fl, dq)[1]
        p_ffi = _sum4("sum_w_ff_in", d_ffi, r_ffi, me)
        p_ffo = _sum4("sum_w_ff_out", d_ffo, r_ffo, me)
        swap_fl = _swap_off("swap_ff_off", [p_ffi, p_ffo], r_ffo)
    dx, dpj, dg1 = _inproj_bwd(x, dx2, dq, dk, dv, du, dga, dgs, g1, w_in, tb, behind(swap_fl))

    dab_re, dab_im = _state_unlayout(jnp.sum(da_acc, axis=0))
    dbb_re, dbb_im = _b_matrix_grad(d_bmat)
    d_lam_re, d_lam_im, d_log_dt, d_b_re, d_b_im = disc_vjp((dab_re, dab_im, dbb_re, dbb_im))
    d_c_re, d_c_im = _c_matrix_grad(d_cmat)
    d_rel = _bias_grad(dbias, bucket)
    d_sinks = jnp.sum(_pair_unlayout(keys_first(dsink_rows)), axis=(1, 2))
    small_grads = dict(
        norm_mix_pre=dg1, norm_mix_post=dg2, norm_mlp_pre=dg3, norm_mlp_post=dg4, rel_bias=d_rel, sinks=d_sinks,
        lam_re=d_lam_re, lam_im=d_lam_im, log_dt=d_log_dt, b_re=d_b_re, b_im=d_b_im, c_re=d_c_re, c_im=d_c_im,
        d_skip=dd_skip)
    small_fl = _devices_off("small_off", _pack(small_grads, loss_acc), swap_fl.token) if dist else None
    outs = _matmul_tn("grad_w_in", dpj, h1, IN_W // 2, D_MODEL, tl, False, behind(small_fl))
    in4 = (N_CHIPS, IN_W // N_CHIPS, D_MODEL)
    d_in, b_in = outs[0].reshape(in4), outs[1].reshape(in4)
    if not dist:
        return loss_acc, dx, small_grads, dict(zip(BIG, [d_in] + d_mix + [d_ffi, d_ffo]))
    in_fl = _scatter_off("scatter_w_in_off", [b_in], d_in)
    (p_ffi, p_ffo), (s_ffi, s_ffo) = _land("swap_ff_land", swap_fl, in_fl.token)
    r_mix = _land("scatter_mix_land", mix_fl, in_fl.token)[1]
    p_mix = [_sum4("sum_" + n, d, r, me) for n, d, r in zip(mix, d_mix, r_mix)]
    mix_swap = _swap_off("swap_mix_off", p_mix, in_fl.token)
    pending = dict(d_in=d_in, in_fl=in_fl, mix_swap=mix_swap, w_ff_in=(p_ffi, s_ffi), w_ff_out=(p_ffo, s_ffo), me=me)
    return loss_acc, dx, small_fl, pending


SMALL = ['norm_mix_pre', 'norm_mix_post', 'norm_mlp_pre', 'norm_mlp_post', 'rel_bias', 'sinks', 'lam_re', 'lam_im',
         'log_dt', 'b_re', 'b_im', 'c_re', 'c_im', 'd_skip']
BIG = ['w_in', 'w_glu', 'w_attn_branch', 'w_ssm_branch', 'w_out', 'w_ff_in', 'w_ff_out']
WEIGHTS = ['norm_mix_pre', 'norm_mix_post', 'norm_mlp_pre', 'norm_mlp_post', 'w_in', 'rel_bias', 'sinks', 'lam_re',
           'lam_im', 'log_dt', 'b_re', 'b_im', 'c_re', 'c_im', 'd_skip', 'w_glu', 'w_attn_branch', 'w_ssm_branch',
           'w_out', 'w_ff_in', 'w_ff_out']
PACK_COLS = 1024
PACK_ORDER = ['b_re', 'b_im', 'c_re', 'c_im', 'lam_re', 'lam_im', 'norm_mix_pre', 'norm_mix_post', 'norm_mlp_pre',
              'norm_mlp_post', 'rel_bias', 'sinks', 'log_dt', 'd_skip']


STATE_MINOR = ('b_re', 'b_im')
PACK_ROWS = 144
LOSS_ROW = 140


def _pack(named, loss_acc):
    parts = []
    for n in PACK_ORDER:
        a = jnp.swapaxes(named[n], -1, -2) if n in STATE_MINOR else named[n]
        flat = a.reshape(-1)
        rows = -(-flat.shape[0] // PACK_COLS)
        parts.append(jnp.pad(flat, (0, rows * PACK_COLS - flat.shape[0])).reshape(rows, PACK_COLS))
    assert sum(p.shape[0] for p in parts) == LOSS_ROW
    parts.append(jnp.pad(loss_acc[0:1], ((0, PACK_ROWS - LOSS_ROW - 1), (0, PACK_COLS - loss_acc.shape[1]))))
    return jnp.concatenate(parts, axis=0)


def _unpack(packed, shapes):
    out, at = {}, 0
    for n in PACK_ORDER:
        shape = shapes[n][:-2] + (shapes[n][-1], shapes[n][-2]) if n in STATE_MINOR else shapes[n]
        size = int(np.prod(shape))
        rows = -(-size // PACK_COLS)
        blk = packed[at:at + rows]
        out[n] = (blk.reshape(-1)[:size] if size % PACK_COLS else blk).reshape(shape)
        at += rows
    return out


def kernel(x, norm_mix_pre, norm_mix_post, norm_mlp_pre, norm_mlp_post, w_in, rel_bias, sinks, lam_re, lam_im, log_dt, b_re, b_im, c_re, c_im, d_skip, w_glu, w_attn_branch, w_ssm_branch, w_out, w_ff_in, w_ff_out, loss_target, m_norm_mix_pre, m_norm_mix_post, m_norm_mlp_pre, m_norm_mlp_post, m_w_in, m_rel_bias, m_sinks, m_lam_re, m_lam_im, m_log_dt, m_b_re, m_b_im, m_c_re, m_c_im, m_d_skip, m_w_glu, m_w_attn_branch, m_w_ssm_branch, m_w_out, m_w_ff_in, m_w_ff_out, v_norm_mix_pre, v_norm_mix_post, v_norm_mlp_pre, v_norm_mlp_post, v_w_in, v_rel_bias, v_sinks, v_lam_re, v_lam_im, v_log_dt, v_b_re, v_b_im, v_c_re, v_c_im, v_d_skip, v_w_glu, v_w_attn_branch, v_w_ssm_branch, v_w_out, v_w_ff_in, v_w_ff_out):
    env = dict(locals())
    w = {n: env[n] for n in WEIGHTS}
    m = {n: env["m_" + n] for n in WEIGHTS}
    v = {n: env["v_" + n] for n in WEIGHTS}
    seq = x.shape[1]
    tb = min(512, seq)

    small = {n: w[n] for n in ('norm_mix_pre', 'norm_mix_post', 'norm_mlp_pre', 'norm_mlp_post', 'rel_bias')}
    small.update({n: w[n][0] for n in ('sinks', 'lam_re', 'lam_im', 'log_dt', 'b_re', 'b_im', 'c_re', 'c_im')})
    small['d_skip'] = w['d_skip']
    shard = lambda t, n: t[n][0].T if n == 'w_in' else t[n][0]
    unshard = lambda a, n: (a.T if n == 'w_in' else a)[None]
    _, dx, small_fl, pending = _local_step(
        x[0], loss_target[0], small, {n: _bf(shard(w, n)) for n in BIG}, tb, True)

    grads, deltas, new_m, new_v = {}, {}, {}, {}

    def adam(n, partials, after=None):
        outs = _adam_pair("adam_" + n, (shard(w, n), *partials, shard(m, n), shard(v, n)), after)
        grads[n], deltas[n], new_m[n], new_v[n] = [unshard(a, n) for a in outs]
        return outs[3]

    mix = ("w_glu", "w_attn_branch", "w_ssm_branch", "w_out")
    in_fl = pending["in_fl"]
    last = pending["mix_swap"].token
    for n in ("w_ff_in", "w_ff_out"):
        last = adam(n, pending[n], last)
    for n, partials in zip(mix, zip(*_land("swap_mix_land", pending["mix_swap"], last))):
        last = adam(n, partials, last)

    small_g = _sum_devices(_land("small_land", small_fl, last)[1][0])
    loss = small_g[LOSS_ROW, 0]
    minor = lambda t, n: jnp.swapaxes(t, -1, -2) if n in STATE_MINOR else t
    g_small = _unpack(small_g, {n: w[n].shape for n in SMALL})
    outs = _adam_small([minor(w[n], n) for n in SMALL], [g_small[n] for n in SMALL],
                       [minor(m[n], n) for n in SMALL], [minor(v[n], n) for n in SMALL])
    grads.update({n: minor(g_small[n], n) for n in SMALL})
    for k, dst in enumerate((deltas, new_m, new_v)):
        dst.update({n: minor(a, n) for n, a in zip(SMALL, outs[k * len(SMALL):(k + 1) * len(SMALL)])})

    (r_in,) = _land("scatter_w_in_land", in_fl, outs[0])[1]
    p_in = _sum4("sum_w_in", pending["d_in"], r_in, pending["me"])
    (s_in,) = _exchange_alone("swap_w_in", _swap_sibling([p_in]))
    adam("w_in", (p_in, s_in))

    return (loss, dx[None], *[grads[n] for n in WEIGHTS], *[deltas[n] for n in WEIGHTS],
            *[new_m[n] for n in WEIGHTS], *[new_v[n] for n in WEIGHTS])
```

```python
import functools
import math

import numpy as np
import jax
import jax.numpy as jnp
from jax import lax
from jax.experimental import pallas as pl
from jax.experimental.pallas import tpu as pltpu

F32 = jnp.float32
BF16 = jnp.bfloat16

D_MODEL = 1024
N_HEADS = 8
N_KV = 2
Q_GROUP = 4
HEAD_DIM = 64
ATTN_W = 512
KV_W = 128
BLOCK = 128
N_BUCKETS = 32
MAX_DISTANCE = 128
NEG_INF = -1e30
SSM_W = 512
SSM_GROUP = 16
SSM_GROUPS = 32
SSM_STATE = 64
N_SUPER = 4
GROUPS_PER_SUPER = SSM_GROUPS // N_SUPER
SUPER_IN = GROUPS_PER_SUPER * SSM_GROUP
SUPER_HALF = GROUPS_PER_SUPER * SSM_STATE
SUPER_W = 2 * SUPER_HALF
STATE_COLS = N_SUPER * SUPER_W
D_FF = 4096
FF_CHUNKS = 4
IN_W = 3328
SPLITS = (0, 512, 640, 768, 1280, 2304, 3328)
RMS_EPS = 1e-6
N_CHIPS = 4
N_DEV = 8
SUBLANES = 8
LANES = 128
STATE_TILES = STATE_COLS // LANES
SUPER_TILES = SUPER_W // LANES

ADAM_LR = 0.001
ADAM_B1 = 0.9
ADAM_B2 = 0.999
ADAM_EPS = 1e-08
ADAM_WD = 0.01
ADAM_STEP = 10

VMEM_BIG = 56 * 1024 * 1024
SDS = jax.ShapeDtypeStruct
MESH_ID = pl.DeviceIdType.MESH
ANY = pl.BlockSpec(memory_space=pl.ANY)


def _bf(x):
    return x.astype(BF16)


def _mm(a, b):
    return jnp.dot(a, b, preferred_element_type=F32)


def _mm_nt(a, b):
    return lax.dot_general(a, b, (((1,), (1,)), ((), ())), preferred_element_type=F32)


def _mm_tn(a, b):
    return lax.dot_general(a, b, (((0,), (0,)), ((), ())), preferred_element_type=F32)


def _sig(x):
    return 1.0 / (1.0 + jnp.exp(-x))


def _rms(x, g):
    r = lax.rsqrt(jnp.mean(x * x, axis=-1, keepdims=True) + RMS_EPS)
    xh = x * r
    return xh * g, xh, r


def _rms_bwd(dout, xh, r, g):
    dg = jnp.sum(dout * xh, axis=0, keepdims=True)
    dxh = dout * g
    dx = r * (dxh - xh * jnp.mean(dxh * xh, axis=-1, keepdims=True))
    return dx, dg


_GELU_C = math.sqrt(2.0 / math.pi)


def _gelu_and_grad(x):
    x2 = x * x
    inner = _GELU_C * (x + 0.044715 * (x2 * x))
    t = jnp.tanh(inner)
    y = 0.5 * x * (1.0 + t)
    dy = 0.5 * (1.0 + t) + 0.5 * x * (1.0 - t * t) * (_GELU_C * (1.0 + 3.0 * 0.044715 * x2))
    return y, dy


def _zero_map(nd, *_):
    return (0,) * nd


def _params(n_axes, vmem=None):
    return pltpu.CompilerParams(dimension_semantics=("arbitrary",) * n_axes, vmem_limit_bytes=vmem)


class _Exchange:
    def __init__(self, ins, outs, sems, start, wait):
        self.ins, self.outs, self.sems, self.start, self.wait = list(ins), list(outs), list(sems), start, wait


def _fused_call(name, body, grid, in_specs, out_specs, out_shape, scratch, args, exchange, params):
    n_in, n_out, n_scr = len(in_specs), len(out_specs), len(scratch)
    if exchange is None:
        fn = body
    else:
        ex = exchange
        n_xi, n_xo = len(ex.ins), len(ex.outs)

        def fn(*refs):
            at = 0
            parts = []
            for n in (n_in, n_xi, n_out, n_xo, n_scr, len(ex.sems)):
                parts.append(refs[at:at + n])
                at += n
            ins, x_in, outs, x_out, scr, x_sem = parts
            ids = [pl.program_id(a) for a in range(len(grid))]
            first = functools.reduce(jnp.logical_and, [i == 0 for i in ids])
            last = functools.reduce(jnp.logical_and, [i == g - 1 for i, g in zip(ids, grid)])

            @pl.when(first)
            def _():
                ex.start(x_in, x_out, x_sem)

            body(*ins, *outs, *scr)

            @pl.when(last)
            def _():
                ex.wait(x_in, x_out, x_sem)

        in_specs = list(in_specs) + [ANY] * n_xi
        out_specs = list(out_specs) + [ANY] * n_xo
        out_shape = list(out_shape) + ex.outs
        scratch = list(scratch) + ex.sems
        args = list(args) + ex.ins
    return pl.pallas_call(fn, grid=grid, in_specs=in_specs, out_specs=out_specs, out_shape=out_shape,
                          scratch_shapes=list(scratch), name=name, compiler_params=params)(*args)


def _exchange_alone(name, ex):
    def body(*refs):
        n_xi, n_xo = len(ex.ins), len(ex.outs)
        x_in, x_out, x_sem = refs[:n_xi], refs[n_xi:n_xi + n_xo], refs[n_xi + n_xo:]
        ex.start(x_in, x_out, x_sem)
        ex.wait(x_in, x_out, x_sem)

    return pl.pallas_call(body, in_specs=[ANY] * len(ex.ins), out_specs=[ANY] * len(ex.outs), out_shape=ex.outs,
                          scratch_shapes=ex.sems, name=name)(*ex.ins)


def _rowcall(name, body, seq, tb, rows, consts, row_outs, acc_outs, scratch=(), reverse=False, vmem=None,
             exchange=None):
    nb = seq // tb
    rmap = (lambda i: (nb - 1 - i, 0)) if reverse else (lambda i: (i, 0))
    tmap = lambda i: (0,) + rmap(i)

    def row_spec(width):
        if isinstance(width, tuple):
            return pl.BlockSpec((width[0], tb, width[1]), tmap)
        return pl.BlockSpec((tb, width), rmap)

    def row_shape(width):
        return (width[0], seq, width[1]) if isinstance(width, tuple) else (seq, width)

    in_specs = [row_spec(a.shape[1] if a.ndim == 2 else (a.shape[0], a.shape[2])) for a in rows]
    in_specs += [pl.BlockSpec(a.shape, functools.partial(_zero_map, a.ndim), pipeline_mode=pl.Buffered(1))
                 for a in consts]
    out_specs = [row_spec(c) for c, _ in row_outs] + [ANY] * len(acc_outs)
    out_shape = [SDS(row_shape(c), dt) for c, dt in row_outs] + [SDS(s, dt) for s, dt in acc_outs]
    n_main = len(rows) + len(consts) + len(row_outs)
    n_acc = len(acc_outs)

    def fn(*refs):
        main, acc_hbm, rest = refs[:n_main], refs[n_main:n_main + n_acc], refs[n_main + n_acc:]
        acc_vmem, own = rest[:n_acc], rest[n_acc:]
        body(*main, *acc_vmem, *own)

        @pl.when(pl.program_id(0) == nb - 1)
        def _():
            for src, dst in zip(acc_vmem, acc_hbm):
                pltpu.sync_copy(src, dst)

    buffers = [pltpu.VMEM(s, dt) for s, dt in acc_outs] + list(scratch)
    return _fused_call(name, fn if acc_outs else body, (nb,), in_specs, out_specs, out_shape, buffers,
                       [*rows, *consts], exchange, _params(1, vmem))


def _inproj_fwd(x, g1, w_in, tb, exchange=None):
    seq = x.shape[0]

    def body(x_ref, g_ref, w_ref, h_ref, q_ref, k_ref, v_ref, u_ref, ga_ref, gs_ref):
        h, _, _ = _rms(x_ref[...], g_ref[...])
        hb = _bf(h)
        h_ref[...] = hb
        pj = _mm_nt(hb, w_ref[...])
        q_ref[...] = _bf(pj[:, SPLITS[0]:SPLITS[1]])
        k_ref[...] = _bf(pj[:, SPLITS[1]:SPLITS[2]])
        v_ref[...] = _bf(pj[:, SPLITS[2]:SPLITS[3]])
        u_ref[...] = pj[:, SPLITS[3]:SPLITS[4]]
        ga_ref[...] = pj[:, SPLITS[4]:SPLITS[5]]
        gs_ref[...] = pj[:, SPLITS[5]:SPLITS[6]]

    return _rowcall("inproj_fwd", body, seq, tb, [x], [g1, w_in],
                    [(D_MODEL, BF16), (ATTN_W, BF16), (KV_W, BF16), (KV_W, BF16), (SSM_W, F32),
                     (D_MODEL, F32), (D_MODEL, F32)], [], vmem=VMEM_BIG, exchange=exchange)


def _inproj_bwd(x, dx2, dq, dk, dv, du, dga, dgs, g1, w_in, tb, exchange=None):
    seq = x.shape[0]

    def body(x_ref, dx2_ref, dq_ref, dk_ref, dv_ref, du_ref, dga_ref, dgs_ref, g_ref, w_ref,
             dx_ref, dpj_ref, dg_ref):
        @pl.when(pl.program_id(0) == 0)
        def _():
            dg_ref[...] = jnp.zeros_like(dg_ref)

        dpj = jnp.concatenate([dq_ref[...], dk_ref[...], dv_ref[...], _bf(du_ref[...]),
                               dga_ref[...], dgs_ref[...]], axis=1)
        dpj_ref[...] = dpj
        dh = _mm(dpj, w_ref[...])
        g = g_ref[...]
        _, xh, r = _rms(x_ref[...], g)
        dxn, dg = _rms_bwd(dh, xh, r, g)
        dx_ref[...] = dx2_ref[...] + dxn
        dg_ref[...] += dg

    return _rowcall("inproj_bwd", body, seq, tb, [x, dx2, dq, dk, dv, du, dga, dgs], [g1, w_in],
                    [(D_MODEL, F32), (IN_W, BF16)], [((1, D_MODEL), F32)], vmem=VMEM_BIG, exchange=exchange)


def _bucket_table():
    qi = np.arange(BLOCK)[:, None]
    kj = np.arange(2 * BLOCK)[None, :]
    dist = qi + BLOCK - kj
    max_exact = N_BUCKETS // 2
    d = np.maximum(dist, 0)
    df = np.maximum(d, 1).astype(np.float32)
    large = max_exact + (np.log(df / np.float32(max_exact)) / np.float32(math.log(MAX_DISTANCE / max_exact))
                         * np.float32(N_BUCKETS - max_exact)).astype(np.int32)
    large = np.minimum(large, N_BUCKETS - 1)
    bucket = np.where(d < max_exact, d, large)
    valid = (dist >= 0) & (dist < BLOCK)
    return np.where(valid, bucket, -1).astype(np.int32)


def _bias_table(rel_bias, bucket):
    def body(rb_ref, bk_ref, o_ref):
        bk = bk_ref[...]
        has_prev = lax.broadcasted_iota(jnp.int32, bk.shape, 1) >= BLOCK
        for h in range(N_HEADS):
            kh, j, par = h // Q_GROUP, (h // 2) % 2, h % 2
            acc = jnp.full((BLOCK, 2 * BLOCK), NEG_INF, F32)
            for b in range(N_BUCKETS):
                acc = jnp.where(bk == b, rb_ref[b, h], acc)
            o_ref[0, kh, par, :, j * BLOCK:(j + 1) * BLOCK] = jnp.where(has_prev, acc, NEG_INF).T
            o_ref[1, kh, par, :, j * BLOCK:(j + 1) * BLOCK] = acc.T

    return pl.pallas_call(
        body, out_shape=SDS((2, N_KV, 2, 2 * BLOCK, 2 * BLOCK), F32),
        in_specs=[pl.BlockSpec(memory_space=pltpu.SMEM), pl.BlockSpec(memory_space=pltpu.VMEM)],
        out_specs=pl.BlockSpec(memory_space=pltpu.VMEM), name="bias_table",
    )(rel_bias, bucket)


def _bias_grad(dbias, bucket):
    def body(db_ref, bk_ref, o_ref):
        bk = bk_ref[...]
        for h in range(N_HEADS):
            kh, j, par = h // Q_GROUP, (h // 2) % 2, h % 2
            db = db_ref[kh, par, :, j * BLOCK:(j + 1) * BLOCK].T
            for b in range(N_BUCKETS):
                o_ref[b, h] = jnp.sum(jnp.where(bk == b, db, 0.0))

    return pl.pallas_call(
        body, out_shape=SDS((N_BUCKETS, N_HEADS), F32),
        in_specs=[pl.BlockSpec(memory_space=pltpu.VMEM), pl.BlockSpec(memory_space=pltpu.VMEM)],
        out_specs=pl.BlockSpec(memory_space=pltpu.SMEM), name="bias_grad",
    )(dbias, bucket)


TILE = 2 * HEAD_DIM


def _pair_layout(t):
    lead = t.shape[:-3]
    t = t.reshape(lead + (N_KV, 2, 2) + t.shape[-2:])
    nl = len(lead)
    t = jnp.transpose(t, tuple(range(nl)) + (nl, nl + 2, nl + 1, nl + 3, nl + 4))
    return t.reshape(lead + (N_KV, 2, 2 * BLOCK, t.shape[-1]))


def _pair_unlayout(t):
    t = t.reshape(N_KV, 2, 2, BLOCK, t.shape[-1]).transpose(0, 2, 1, 3, 4)
    return t.reshape(N_HEADS, BLOCK, t.shape[-1])


def _halves(t):
    tf = t.astype(F32)
    low = lax.broadcasted_iota(jnp.int32, tf.shape, 1) < HEAD_DIM
    swapped = pltpu.roll(tf, HEAD_DIM, 1)
    zero = jnp.zeros_like(tf)
    return ((_bf(jnp.where(low, tf, zero)), _bf(jnp.where(low, zero, swapped))),
            (_bf(jnp.where(low, swapped, zero)), _bf(jnp.where(low, zero, tf))))


def _fold_halves(even, odd):
    low = lax.broadcasted_iota(jnp.int32, even.shape, 1) < HEAD_DIM
    comb = jnp.where(low, even, odd)
    return comb + pltpu.roll(comb, HEAD_DIM, 1)


def _tile_rows(ref, kh):
    return jnp.concatenate([ref[:, (2 * kh) * TILE:(2 * kh + 1) * TILE],
                            ref[:, (2 * kh + 1) * TILE:(2 * kh + 2) * TILE]], axis=0)


def _halves_t(t):
    tt = t.astype(F32).T
    top = lax.broadcasted_iota(jnp.int32, tt.shape, 0) < HEAD_DIM
    swapped = jnp.concatenate([tt[HEAD_DIM:], tt[:HEAD_DIM]], axis=0)
    zero = jnp.zeros_like(tt)
    return ((_bf(jnp.where(top, tt, zero)), _bf(jnp.where(top, zero, swapped))),
            (_bf(jnp.where(top, swapped, zero)), _bf(jnp.where(top, zero, tt))))


def _attn_probs(km, qk, bias, sink):
    lg = _mm_nt(km, qk) * (HEAD_DIM ** -0.5) + bias
    m = jnp.maximum(jnp.max(lg, axis=0, keepdims=True), sink)
    p = jnp.exp(lg - m)
    es = jnp.exp(sink - m)
    inv = 1.0 / (jnp.sum(p, axis=0, keepdims=True) + es)
    return p * inv, es * inv


def _attn_fwd(q, k, v, bias, sink_rows, exchange=None):
    seq = q.shape[0]
    nblk = seq // BLOCK

    def body(q_ref, kp_ref, kc_ref, vp_ref, vc_ref, b_ref, s_ref, o_ref):
        which = jnp.minimum(pl.program_id(0), 1)
        kms = _halves(jnp.concatenate([kp_ref[...], kc_ref[...]], axis=0))
        vts = _halves_t(jnp.concatenate([vp_ref[...], vc_ref[...]], axis=0))
        for kh in range(N_KV):
            qk = _tile_rows(q_ref, kh)
            acc = jnp.zeros((TILE, 2 * BLOCK), F32)
            for par in range(2):
                pr, _ = _attn_probs(kms[kh][par], qk, b_ref[which, kh, par], s_ref[kh, par])
                acc = acc + _mm(vts[kh][par], _bf(pr))
            acc = acc.T
            o_ref[:, (2 * kh) * TILE:(2 * kh + 1) * TILE] = _bf(acc[:BLOCK])
            o_ref[:, (2 * kh + 1) * TILE:(2 * kh + 2) * TILE] = _bf(acc[BLOCK:])

    cur = lambda n: (n, 0)
    prev = lambda n: (jnp.maximum(n - 1, 0), 0)
    return _fused_call(
        "attn_fwd", body, (nblk,),
        [pl.BlockSpec((BLOCK, ATTN_W), cur),
         pl.BlockSpec((BLOCK, KV_W), prev), pl.BlockSpec((BLOCK, KV_W), cur),
         pl.BlockSpec((BLOCK, KV_W), prev), pl.BlockSpec((BLOCK, KV_W), cur),
         pl.BlockSpec(bias.shape, functools.partial(_zero_map, bias.ndim)),
         pl.BlockSpec(sink_rows.shape, functools.partial(_zero_map, sink_rows.ndim))],
        [pl.BlockSpec((BLOCK, ATTN_W), cur)], [SDS((seq, ATTN_W), BF16)], [],
        [q, k, k, v, v, bias, sink_rows], exchange, _params(1))


def _attn_bwd(q, k, v, d_out, bias, sink_rows, exchange=None):
    seq = q.shape[0]
    nblk = seq // BLOCK

    def body(q_ref, kp_ref, kc_ref, vp_ref, vc_ref, do_ref, b_ref, s_ref,
             dq_ref, dk_ref, dv_ref, db_ref, ds_ref, ck_ref, cv_ref):
        n = pl.program_id(0)

        @pl.when(n == 0)
        def _():
            db_ref[...] = jnp.zeros_like(db_ref)
            ds_ref[...] = jnp.zeros_like(ds_ref)
            ck_ref[...] = jnp.zeros_like(ck_ref)
            cv_ref[...] = jnp.zeros_like(cv_ref)

        @pl.when(n < nblk)
        def _():
            which = jnp.minimum(n, 1)
            scale = HEAD_DIM ** -0.5
            kcat = jnp.concatenate([kp_ref[...], kc_ref[...]], axis=0)
            kms = _halves(kcat)
            kts = _halves_t(kcat)
            vms = _halves(jnp.concatenate([vp_ref[...], vc_ref[...]], axis=0))
            dks, dvs = [], []
            for kh in range(N_KV):
                qk = _tile_rows(q_ref, kh)
                dok = _tile_rows(do_ref, kh)
                dq = jnp.zeros((TILE, 2 * BLOCK), F32)
                dkp, dvp = [], []
                for par in range(2):
                    pr, ps = _attn_probs(kms[kh][par], qk, b_ref[which, kh, par], s_ref[kh, par])
                    dp = _mm_nt(vms[kh][par], dok)
                    rs = jnp.sum(pr * dp, axis=0, keepdims=True)
                    dlg = pr * (dp - rs)
                    ds_ref[kh, par] += -ps * rs
                    db_ref[kh, par] += dlg
                    dlb = _bf(dlg)
                    dq = dq + _mm(kts[kh][par], dlb)
                    dkp.append(_mm(dlb, qk))
                    dvp.append(_mm(_bf(pr), dok))
                dq = _bf((dq * scale).T)
                dq_ref[:, (2 * kh) * TILE:(2 * kh + 1) * TILE] = dq[:BLOCK]
                dq_ref[:, (2 * kh + 1) * TILE:(2 * kh + 2) * TILE] = dq[BLOCK:]
                dks.append(_fold_halves(*dkp))
                dvs.append(_fold_halves(*dvp))
            low = lax.broadcasted_iota(jnp.int32, (2 * BLOCK, TILE), 1) < HEAD_DIM
            dkk = jnp.where(low, dks[0], dks[1]) * scale
            dvv = jnp.where(low, dvs[0], dvs[1])
            dk_ref[...] = _bf(ck_ref[...] + dkk[:BLOCK])
            ck_ref[...] = dkk[BLOCK:]
            dv_ref[...] = _bf(cv_ref[...] + dvv[:BLOCK])
            cv_ref[...] = dvv[BLOCK:]

        @pl.when(n == nblk)
        def _():
            dk_ref[...] = _bf(ck_ref[...])
            dv_ref[...] = _bf(cv_ref[...])

    cur = lambda n: (jnp.minimum(n, nblk - 1), 0)
    prev = lambda n: (jnp.maximum(jnp.minimum(n, nblk - 1) - 1, 0), 0)
    late = lambda n: (jnp.maximum(n - 1, 0), 0)
    kv_spec = lambda m: pl.BlockSpec((BLOCK, KV_W), m)
    acc_b = pl.BlockSpec(bias.shape[1:], functools.partial(_zero_map, bias.ndim - 1))
    acc_s = pl.BlockSpec(sink_rows.shape, functools.partial(_zero_map, sink_rows.ndim))
    return _fused_call(
        "attn_bwd", body, (nblk + 1,),
        [pl.BlockSpec((BLOCK, ATTN_W), cur), kv_spec(prev), kv_spec(cur), kv_spec(prev), kv_spec(cur),
         pl.BlockSpec((BLOCK, ATTN_W), cur),
         pl.BlockSpec(bias.shape, functools.partial(_zero_map, bias.ndim)), acc_s],
        [pl.BlockSpec((BLOCK, ATTN_W), cur), kv_spec(late), kv_spec(late), acc_b, acc_s],
        [SDS((seq, ATTN_W), BF16), SDS((seq, KV_W), BF16), SDS((seq, KV_W), BF16),
         SDS(bias.shape[1:], F32), SDS(sink_rows.shape, F32)],
        [pltpu.VMEM((BLOCK, KV_W), F32), pltpu.VMEM((BLOCK, KV_W), F32)],
        [q, k, k, v, v, d_out, bias, sink_rows], exchange, _params(1))


def _ssm_discretize(lam_re, lam_im, log_dt, b_re, b_im):
    dt = jnp.exp(log_dt)[:, None]
    mag = jnp.exp(lam_re * dt)
    ab_re = mag * jnp.cos(lam_im * dt)
    ab_im = mag * jnp.sin(lam_im * dt)
    nr = ab_re - 1.0
    den = lam_re * lam_re + lam_im * lam_im
    f_re = (nr * lam_re + ab_im * lam_im) / den
    f_im = (ab_im * lam_re - nr * lam_im) / den
    bb_re = f_re[..., None] * b_re - f_im[..., None] * b_im
    bb_im = f_re[..., None] * b_im + f_im[..., None] * b_re
    return ab_re, ab_im, bb_re, bb_im


def _state_layout(re, im):
    lead = re.shape[:-2]
    z = jnp.stack([re, im], axis=-3).reshape(lead + (2, N_SUPER, GROUPS_PER_SUPER, SSM_STATE))
    return jnp.moveaxis(z, -4, -3).reshape(lead + (STATE_COLS,))


def _state_unlayout(vec):
    z = vec.reshape(N_SUPER, 2, GROUPS_PER_SUPER, SSM_STATE).transpose(1, 0, 2, 3)
    z = z.reshape(2, SSM_GROUPS, SSM_STATE)
    return z[0], z[1]


SEG = 4
WINDOW = SEG * SUBLANES


def _scan_tables(ab_re, ab_im):
    pw = [None, (ab_re, ab_im)]
    for _ in range(2, WINDOW + 1):
        pr, pi_ = pw[-1]
        pw.append((pr * ab_re - pi_ * ab_im, pr * ab_im + pi_ * ab_re))
    fwd = np.zeros((7, SUBLANES), np.int64)
    bwd = np.zeros((7, SUBLANES), np.int64)
    for k, shift in enumerate((1, 2, 4)):
        fwd[k] = [SEG * shift if r >= shift else 0 for r in range(SUBLANES)]
        bwd[k] = [SEG * shift if r < SUBLANES - shift else 0 for r in range(SUBLANES)]
    fwd[3] = [SEG * (r + 1) for r in range(SUBLANES)]
    bwd[3] = [SEG * (SUBLANES - r) for r in range(SUBLANES)]
    for k in range(1, SEG):
        fwd[3 + k] = bwd[3 + k] = k
    used = sorted((set(fwd.ravel()) | set(bwd.ravel())) - {0})
    select = lambda which: np.stack([(which == p) for p in used], axis=-1).astype(np.float32)
    stacked = _state_layout(jnp.stack([pw[p][0] for p in used]), jnp.stack([pw[p][1] for p in used]))
    conj_sign = np.where((np.arange(STATE_COLS) // SUPER_HALF) % 2 == 1, -1.0, 1.0).astype(np.float32)
    pick = functools.partial(jnp.einsum, 'krp,pc->krc', precision=lax.Precision.HIGHEST)
    return pick(select(fwd), stacked), pick(select(bwd), stacked) * conj_sign


_EYE = np.eye(GROUPS_PER_SUPER, dtype=np.float32)


def _b_matrix(bb_re, bb_im):
    bb = jnp.stack([bb_re, bb_im]).reshape(2, N_SUPER, GROUPS_PER_SUPER, SSM_STATE, SSM_GROUP)
    m = jnp.einsum('rsgpc,gh->sgcrhp', bb, _EYE)
    return m.reshape(N_SUPER, SUPER_IN, SUPER_W)


def _b_matrix_grad(dm):
    d = dm.reshape(N_SUPER, GROUPS_PER_SUPER, SSM_GROUP, 2, GROUPS_PER_SUPER, SSM_STATE)
    d = jnp.sum(d * _EYE[None, :, None, None, :, None], axis=4)
    d = d.transpose(3, 0, 1, 4, 2).reshape(2, SSM_GROUPS, SSM_STATE, SSM_GROUP)
    return d[0], d[1]


def _c_matrix(c_re, c_im):
    cc = jnp.stack([c_re, -c_im]).reshape(2, N_SUPER, GROUPS_PER_SUPER, SSM_GROUP, SSM_STATE)
    m = jnp.einsum('rsgcp,gh->srgphc', cc, _EYE)
    return m.reshape(N_SUPER, SUPER_W, SUPER_IN)


def _c_matrix_grad(dm):
    d = dm.reshape(N_SUPER, 2, GROUPS_PER_SUPER, SSM_STATE, GROUPS_PER_SUPER, SSM_GROUP)
    d = jnp.sum(d * _EYE[None, None, :, None, :, None], axis=4)
    d = d.transpose(1, 0, 2, 4, 3).reshape(2, SSM_GROUPS, SSM_GROUP, SSM_STATE)
    return d[0], -d[1]


def _cmul_add(xr, xi, ar, ai, sr, si):
    return xr + ar * sr - ai * si, xi + ar * si + ai * sr


def _scan_rows(buf_ref, tab_ref, carry_ref, n_windows, reverse, h_ref=None, da_ref=None):
    order = list(range(SEG - 1, -1, -1)) if reverse else list(range(SEG))
    near = SUBLANES - 1 if reverse else 0
    far = 0 if reverse else SUBLANES - 1
    s_in = SUBLANES - 1 if reverse else 1
    lanes = lambda tile: pl.ds(tile * LANES, LANES)

    def window(w0, tile_re, tile_im, c_re, c_im, acc):
        rows = lambda t: pl.ds(w0 + t, SUBLANES, stride=SEG)
        get = lambda ref, t: (ref.at[tile_re][rows(t), :], ref.at[tile_im][rows(t), :])
        tab = lambda k: (tab_ref[k, :, lanes(tile_re)], tab_ref[k, :, lanes(tile_im)])

        def put(t, xr, xi):
            buf_ref.at[tile_re][rows(t), :] = xr
            buf_ref.at[tile_im][rows(t), :] = xi

        a1 = tab(4)
        er, ei = get(buf_ref, order[0])
        for t in order[1:]:
            er, ei = _cmul_add(*get(buf_ref, t), *a1, er, ei)
            if t != order[-1]:
                put(t, er, ei)
        for k, shift in enumerate((1, 2, 4)):
            s = (SUBLANES - shift) if reverse else shift
            er, ei = _cmul_add(er, ei, *tab(k), pltpu.roll(er, s, 0), pltpu.roll(ei, s, 0))
        er, ei = _cmul_add(er, ei, *tab(3), c_re, c_im)
        put(order[-1], er, ei)
        sub = lax.broadcasted_iota(jnp.int32, er.shape, 0)
        in_re = jnp.where(sub == near, c_re, pltpu.roll(er, s_in, 0))
        in_im = jnp.where(sub == near, c_im, pltpu.roll(ei, s_in, 0))
        true = {order[-1]: (er, ei)}
        for idx, t in enumerate(order[:-1]):
            true[t] = _cmul_add(*get(buf_ref, t), *tab(4 + idx), in_re, in_im)
            put(t, *true[t])
        carry = (jnp.broadcast_to(er[far:far + 1], er.shape), jnp.broadcast_to(ei[far:far + 1], ei.shape))
        if acc is None:
            return carry, None
        acc_re, acc_im = acc
        for t in range(SEG):
            if t + 1 < SEG:
                gr, gim = true[t + 1]
            else:
                gr = jnp.where(sub == SUBLANES - 1, c_re, pltpu.roll(true[0][0], SUBLANES - 1, 0))
                gim = jnp.where(sub == SUBLANES - 1, c_im, pltpu.roll(true[0][1], SUBLANES - 1, 0))
            hr, hi = get(h_ref, t)
            acc_re = acc_re + gr * hr + gim * hi
            acc_im = acc_im + gim * hr - gr * hi
        return carry, (acc_re, acc_im)

    half = SUPER_HALF // LANES
    per = 2 if h_ref is None else 4
    for sb in range(N_SUPER):
        pairs = [(2 * half * sb + j, 2 * half * sb + half + j) for j in range(half)]

        def step(wi, state, pairs=pairs):
            w = (n_windows - 1 - wi) if reverse else wi
            w0 = pl.multiple_of(w * WINDOW, WINDOW)
            out = []
            for j, (tile_re, tile_im) in enumerate(pairs):
                mine = state[per * j:per * (j + 1)]
                carry, acc = window(w0, tile_re, tile_im, mine[0], mine[1], mine[2:] or None)
                out += list(carry) + list(acc or ())
            return tuple(out)

        init = []
        for tile_re, tile_im in pairs:
            init += [carry_ref[:, lanes(tile_re)], carry_ref[:, lanes(tile_im)]]
            if h_ref is not None:
                init += [da_ref[:, lanes(tile_re)], da_ref[:, lanes(tile_im)]]
        fin = lax.fori_loop(0, n_windows, step, tuple(init))
        for j, (tile_re, tile_im) in enumerate(pairs):
            carry_ref[:, lanes(tile_re)] = fin[per * j]
            carry_ref[:, lanes(tile_im)] = fin[per * j + 1]
            if h_ref is not None:
                da_ref[:, lanes(tile_re)] = fin[per * j + 2]
                da_ref[:, lanes(tile_im)] = fin[per * j + 3]


def _put_tiles(ref, sb, block):
    for j in range(SUPER_TILES):
        ref[sb * SUPER_TILES + j] = block[:, j * LANES:(j + 1) * LANES]


def _get_tiles(ref, sb):
    return jnp.concatenate([ref[sb * SUPER_TILES + j] for j in range(SUPER_TILES)], axis=1)


def _ssm_fwd(u, bmat, cmat, tab, d_skip, tb, exchange=None):
    seq = u.shape[0]

    def body(u_ref, b_ref, c_ref, t_ref, d_ref, s_ref, h_ref, carry_ref):
        @pl.when(pl.program_id(0) == 0)
        def _():
            carry_ref[...] = jnp.zeros_like(carry_ref)

        u_blk = u_ref[...]
        ub = _bf(u_blk)
        for sb in range(N_SUPER):
            _put_tiles(h_ref, sb, _mm(ub[:, sb * SUPER_IN:(sb + 1) * SUPER_IN], b_ref[sb]))
        _scan_rows(h_ref, t_ref, carry_ref, tb // WINDOW, False)
        ys = [_mm(_bf(_get_tiles(h_ref, sb)), c_ref[sb]) for sb in range(N_SUPER)]
        s_ref[...] = jnp.concatenate(ys, axis=1) + d_ref[...] * u_blk

    return _rowcall("ssm_fwd", body, seq, tb, [u], [bmat, cmat, tab, d_skip],
                    [(SSM_W, F32), ((STATE_TILES, LANES), F32)], [],
                    scratch=[pltpu.VMEM((SUBLANES, STATE_COLS), F32)], vmem=VMEM_BIG, exchange=exchange)


def _ssm_bwd(ds, u, h, bmat_t, cmat_t, tab, d_skip, tb, exchange=None):
    seq = u.shape[0]

    def body(ds_ref, u_ref, h_ref, bt_ref, ct_ref, t_ref, d_ref,
             du_ref, db_ref, dc_ref, da_ref, dd_ref, g_ref, carry_ref):
        @pl.when(pl.program_id(0) == 0)
        def _():
            carry_ref[...] = jnp.zeros_like(carry_ref)
            db_ref[...] = jnp.zeros_like(db_ref)
            dc_ref[...] = jnp.zeros_like(dc_ref)
            da_ref[...] = jnp.zeros_like(da_ref)
            dd_ref[...] = jnp.zeros_like(dd_ref)

        ds_blk = ds_ref[...]
        dsb = _bf(ds_blk)
        u_blk = u_ref[...]
        ub = _bf(u_blk)
        for sb in range(N_SUPER):
            _put_tiles(g_ref, sb, _mm(dsb[:, sb * SUPER_IN:(sb + 1) * SUPER_IN], ct_ref[sb]))
        _scan_rows(g_ref, t_ref, carry_ref, tb // WINDOW, True, h_ref=h_ref, da_ref=da_ref)
        dus = []
        for sb in range(N_SUPER):
            gb = _bf(_get_tiles(g_ref, sb))
            dus.append(_mm(gb, bt_ref[sb]))
            db_ref[sb] += _mm_tn(ub[:, sb * SUPER_IN:(sb + 1) * SUPER_IN], gb)
            dc_ref[sb] += _mm_tn(_bf(_get_tiles(h_ref, sb)), dsb[:, sb * SUPER_IN:(sb + 1) * SUPER_IN])
        du_ref[...] = jnp.concatenate(dus, axis=1) + d_ref[...] * ds_blk
        dd_ref[...] += jnp.sum(ds_blk * u_blk, axis=0, keepdims=True)

    return _rowcall("ssm_bwd", body, seq, tb, [ds, u, h], [bmat_t, cmat_t, tab, d_skip],
                    [(SSM_W, F32)],
                    [((N_SUPER, SUPER_IN, SUPER_W), F32), ((N_SUPER, SUPER_W, SUPER_IN), F32),
                     ((SUBLANES, STATE_COLS), F32), ((1, SSM_W), F32)],
                    scratch=[pltpu.VMEM((STATE_TILES, tb, LANES), F32), pltpu.VMEM((SUBLANES, STATE_COLS), F32)],
                    reverse=True, vmem=VMEM_BIG, exchange=exchange)


def _merge_core(s, attb, ga, gs, wg_ref, wab_ref, wsb_ref, wout_ref):
    zg, dgelu = _gelu_and_grad(s)
    zgb = _bf(zg)
    sg = _sig(_mm(zgb, wg_ref[...]))
    z = zg * sg
    zb = _bf(z)
    ys = jnp.concatenate([_mm(zb, wsb_ref[j]) for j in range(N_CHIPS)], axis=1)
    ya = jnp.concatenate([_mm(attb, wab_ref[j]) for j in range(N_CHIPS)], axis=1)
    sa = _sig(ga)
    ss = _sig(gs)
    mgb = _bf(sa * ya + ss * ys)
    o = _mm(mgb, wout_ref[...])
    return dict(zg=zg, dgelu=dgelu, zgb=zgb, sg=sg, zb=zb, ys=ys, ya=ya, sa=sa, ss=ss, mgb=mgb, o=o)


def _merge_fwd(x, s, att, ga, gs, g2, w_glu, w_ab, w_sb, w_out, tb, exchange=None):
    seq = x.shape[0]

    def body(x_ref, s_ref, att_ref, ga_ref, gs_ref, g_ref, wg_ref, wab_ref, wsb_ref, wout_ref, x2_ref):
        f = _merge_core(s_ref[...], att_ref[...], ga_ref[...], gs_ref[...], wg_ref, wab_ref, wsb_ref, wout_ref)
        n, _, _ = _rms(f["o"], g_ref[...])
        x2_ref[...] = x_ref[...] + n

    return _rowcall("merge_fwd", body, seq, tb, [x, s, att, ga, gs], [g2, w_glu, w_ab, w_sb, w_out],
                    [(D_MODEL, F32)], [], vmem=VMEM_BIG, exchange=exchange)[0]


def _merge_bwd(dx2, s, att, ga, gs, g2, w_glu, w_ab, w_sb, w_out, tb, exchange=None):
    seq = s.shape[0]
    cw = D_MODEL // N_CHIPS
    last = seq // tb - 1

    def body(dx2_ref, s_ref, att_ref, ga_ref, gs_ref, g_ref, wg_ref, wab_ref, wsb_ref, wout_ref,
             ds_ref, datt_ref, dga_ref, dgs_ref, dg_ref, dwg_ref, dwab_ref, dwsb_ref, dwout_ref,
             bwg_ref, bwab_ref, bwsb_ref, bwout_ref):
        @pl.when(pl.program_id(0) == 0)
        def _():
            for r in (dg_ref, dwg_ref, dwab_ref, dwsb_ref, dwout_ref):
                r[...] = jnp.zeros_like(r)

        attb = att_ref[...]
        f = _merge_core(s_ref[...], attb, ga_ref[...], gs_ref[...], wg_ref, wab_ref, wsb_ref, wout_ref)
        g = g_ref[...]
        _, oh, r2 = _rms(f["o"], g)
        do, dg = _rms_bwd(dx2_ref[...], oh, r2, g)
        dg_ref[...] += dg
        dob = _bf(do)
        dwout_ref[...] += _mm_tn(f["mgb"], dob)
        dmg = _mm_nt(dob, wout_ref[...])
        sa, ss = f["sa"], f["ss"]
        dyab = _bf(dmg * sa)
        dysb = _bf(dmg * ss)
        dga_ref[...] = _bf(dmg * f["ya"] * sa * (1.0 - sa))
        dgs_ref[...] = _bf(dmg * f["ys"] * ss * (1.0 - ss))
        dwab = _mm_tn(attb, dyab)
        dwsb = _mm_tn(f["zb"], dysb)
        datt = jnp.zeros((tb, ATTN_W), F32)
        dz = jnp.zeros((tb, SSM_W), F32)
        for j in range(N_CHIPS):
            dwab_ref[j] += dwab[:, j * cw:(j + 1) * cw]
            dwsb_ref[j] += dwsb[:, j * cw:(j + 1) * cw]
            datt = datt + _mm_nt(dyab[:, j * cw:(j + 1) * cw], wab_ref[j])
            dz = dz + _mm_nt(dysb[:, j * cw:(j + 1) * cw], wsb_ref[j])
        datt_ref[...] = _bf(datt)
        sg, zg = f["sg"], f["zg"]
        dglb = _bf(dz * zg * sg * (1.0 - sg))
        dwg_ref[...] += _mm_tn(f["zgb"], dglb)
        dzg = dz * sg + _mm_nt(dglb, wg_ref[...])
        ds_ref[...] = dzg * f["dgelu"]

        @pl.when(pl.program_id(0) == last)
        def _():
            for dst, src in ((bwg_ref, dwg_ref), (bwab_ref, dwab_ref), (bwsb_ref, dwsb_ref), (bwout_ref, dwout_ref)):
                dst[...] = _bf(src[...])

    shapes = [w_glu.shape, w_ab.shape, w_sb.shape, w_out.shape]
    return _rowcall("merge_bwd", body, seq, tb, [dx2, s, att, ga, gs], [g2, w_glu, w_ab, w_sb, w_out],
                    [(SSM_W, F32), (ATTN_W, BF16), (D_MODEL, BF16), (D_MODEL, BF16)],
                    [((1, D_MODEL), F32)] + [(sh, F32) for sh in shapes] + [(sh, BF16) for sh in shapes],
                    vmem=VMEM_BIG, exchange=exchange)


def _mlp_fwd_loss(x2, target, g3, g4, w_ffi, w_ffo, tb):
    seq = x2.shape[0]
    n_slab = len(w_ffi)
    sw = D_FF // FF_CHUNKS // n_slab

    def body(x2_ref, t_ref, g3_ref, g4_ref, *rest):
        wi_refs, (wo_ref, dy_ref, df_ref, h_ref, ra_ref, loss_ref, dg_ref) = rest[:n_slab], rest[n_slab:]

        @pl.when(pl.program_id(0) == 0)
        def _():
            loss_ref[...] = jnp.zeros_like(loss_ref)
            dg_ref[...] = jnp.zeros_like(dg_ref)

        x2_blk = x2_ref[...]
        h3, _, _ = _rms(x2_blk, g3_ref[...])
        hb = _bf(h3)
        h_ref[...] = hb
        f = jnp.zeros((tb, D_MODEL), F32)
        for j in range(FF_CHUNKS):
            for k in range(n_slab):
                ra = jnp.maximum(_mm(hb, wi_refs[k][j]), 0.0)
                ra_ref[:, pl.ds((j * n_slab + k) * sw, sw)] = _bf(ra)
                f = f + _mm(_bf(ra * ra), wo_ref[j, pl.ds(k * sw, sw), :])
        g4 = g4_ref[...]
        n4, fh, r4 = _rms(f, g4)
        e = (x2_blk + n4) - t_ref[...]
        loss_ref[...] += 0.5 * jnp.sum(jnp.mean(e * e, axis=-1, keepdims=True))
        dy = e * (1.0 / D_MODEL)
        dy_ref[...] = dy
        df, dg = _rms_bwd(dy, fh, r4, g4)
        df_ref[...] = _bf(df)
        dg_ref[...] += dg

    return _rowcall("mlp_fwd_loss", body, seq, tb, [x2, target], [g3, g4, *w_ffi, w_ffo],
                    [(D_MODEL, F32), (D_MODEL, BF16), (D_MODEL, BF16), (D_FF, BF16)],
                    [((SUBLANES, 128), F32), ((1, D_MODEL), F32)], vmem=VMEM_BIG)


def _mlp_bwd(x2, dy, df, ra, g3, w_ffi, w_ffo, tb):
    seq = x2.shape[0]
    n_slab = len(w_ffi)
    sw = D_FF // FF_CHUNKS // n_slab

    def body(x2_ref, dy_ref, df_ref, ra_ref, g3_ref, *rest):
        wi_refs, (wo_ref, dx_ref, da_ref, dg_ref) = rest[:n_slab], rest[n_slab:]

        @pl.when(pl.program_id(0) == 0)
        def _():
            dg_ref[...] = jnp.zeros_like(dg_ref)

        dfb = df_ref[...]
        dh = jnp.zeros((tb, D_MODEL), F32)
        for j in range(FF_CHUNKS):
            for k in range(n_slab):
                cols = pl.ds((j * n_slab + k) * sw, sw)
                ra = ra_ref[:, cols].astype(F32)
                dab = _bf(_mm_nt(dfb, wo_ref[j, pl.ds(k * sw, sw), :]) * (2.0 * ra))
                da_ref[:, cols] = dab
                dh = dh + _mm_nt(dab, wi_refs[k][j])
        g3 = g3_ref[...]
        _, xh, r3 = _rms(x2_ref[...], g3)
        dxn, dg = _rms_bwd(dh, xh, r3, g3)
        dx_ref[...] = dy_ref[...] + dxn
        dg_ref[...] += dg

    return _rowcall("mlp_bwd", body, seq, tb, [x2, dy, df, ra], [g3, *w_ffi, w_ffo],
                    [(D_MODEL, F32), (D_FF, BF16)], [((1, D_MODEL), F32)], vmem=VMEM_BIG)


def _matmul_tn(name, a, b, tk, tn, tl, chunk_major, exchange=None, square_a=False):
    seq, kdim = a.shape
    ndim = b.shape[1]
    last = seq // tl - 1

    def body(a_ref, b_ref, o_ref, ob_ref):
        @pl.when(pl.program_id(2) == 0)
        def _():
            o_ref[...] = jnp.zeros_like(o_ref)

        a_blk = a_ref[...]
        if square_a:
            a_blk = _bf(jnp.square(a_blk.astype(F32)))
        o_ref[...] += _mm_tn(a_blk, b_ref[...])

        @pl.when(pl.program_id(2) == last)
        def _():
            ob_ref[...] = _bf(o_ref[...])

    if chunk_major:
        shape = (ndim // tn, kdim, tn)
        out_spec = pl.BlockSpec((None, tk, tn), lambda k, n, l: (n, k, 0))
    else:
        shape = (kdim, ndim)
        out_spec = pl.BlockSpec((tk, tn), lambda k, n, l: (k, n))
    return _fused_call(
        name, body, (kdim // tk, ndim // tn, seq // tl),
        [pl.BlockSpec((tl, tk), lambda k, n, l: (l, k)), pl.BlockSpec((tl, tn), lambda k, n, l: (l, n))],
        [out_spec, out_spec], [SDS(shape, F32), SDS(shape, BF16)], [], [a, b], exchange, _params(3, VMEM_BIG))


def _ew_call(name, fn, ins, n_out, after=None):
    rows, cols = ins[0].shape
    tr = rows
    while tr * cols * 4 > min(1 << 20, (9 << 20) // (len(ins) + n_out)) and tr % 16 == 0:
        tr //= 2
    spec = pl.BlockSpec((tr, cols), lambda i: (i, 0))
    extra = [] if after is None else [after]

    def body(*refs):
        outs = fn(*[r[...] for r in refs[:len(ins)]])
        for r, o in zip(refs[len(ins) + len(extra):], outs):
            r[...] = o

    return pl.pallas_call(
        body, grid=(rows // tr,), in_specs=[spec] * len(ins) + [ANY] * len(extra), out_specs=[spec] * n_out,
        out_shape=[SDS((rows, cols), F32)] * n_out, name=name, compiler_params=_params(1))(*ins, *extra)


def _adam_math(w, g, m, v):
    m2 = ADAM_B1 * m + (1.0 - ADAM_B1) * g
    v2 = ADAM_B2 * v + (1.0 - ADAM_B2) * (g * g)
    m_hat = m2 / (1.0 - ADAM_B1 ** ADAM_STEP)
    v_hat = v2 / (1.0 - ADAM_B2 ** ADAM_STEP)
    delta = -ADAM_LR * (m_hat / (jnp.sqrt(v_hat) + ADAM_EPS) + ADAM_WD * w)
    return delta, m2, v2


def _sum4(name, own, recv, idx):
    _, rows, cols = own.shape
    tr = rows
    while tr * cols * 4 > (1 << 20) and tr % 16 == 0:
        tr //= 2

    def body(idx_ref, o_ref, r0_ref, r1_ref, r2_ref, out_ref):
        out_ref[...] = ((o_ref[...] + r0_ref[...].astype(F32)) + r1_ref[...].astype(F32)) + r2_ref[...].astype(F32)

    blk = (None, tr, cols)
    grid_spec = pltpu.PrefetchScalarGridSpec(
        num_scalar_prefetch=1, grid=(rows // tr,),
        in_specs=[pl.BlockSpec(blk, lambda i, s: (s[0], i, 0)), pl.BlockSpec(blk, lambda i, s: (0, i, 0)),
                  pl.BlockSpec(blk, lambda i, s: (1, i, 0)), pl.BlockSpec(blk, lambda i, s: (2, i, 0))],
        out_specs=pl.BlockSpec((tr, cols), lambda i, s: (i, 0)))
    return pl.pallas_call(body, grid_spec=grid_spec, out_shape=SDS((rows, cols), F32), name=name,
                          compiler_params=_params(1))(jnp.reshape(idx, (1,)).astype(jnp.int32), own, recv, recv, recv)


def _adam_pair(name, item, after=None):
    def fn(w_, a, b, m_, v_):
        g = a + b
        return (g,) + _adam_math(w_, g, m_, v_)

    return _ew_call(name, fn, list(item), 4, after)


def _place():
    return lax.axis_index("x"), lax.axis_index("y"), lax.axis_index("c")


def _other_chips(x, y):
    return [(1 - x, y), (x, 1 - y), (1 - x, 1 - y)]


HBM = pl.BlockSpec(memory_space=pltpu.HBM)
SEM = pl.BlockSpec(memory_space=pltpu.SEMAPHORE)
DATAFLOW = pltpu.SideEffectType.DATAFLOW_SIDE_EFFECTING


class _Flight:
    def __init__(self, copies, n_copies, send, recv, srcs, lands, token):
        self.copies, self.n, self.send, self.recv = copies, n_copies, send, recv
        self.srcs, self.lands, self.token = list(srcs), list(lands), token


def _take_off(name, srcs, lands, copies, n_copies, after):
    n_s, n_l = len(srcs), len(lands)

    def body(*refs):
        src, land = refs[:n_s], refs[n_s:n_s + n_l]
        send, recv = refs[n_s + n_l + 1:n_s + n_l + 3]
        for cp in copies(src, land, send, recv):
            cp.start()
        refs[-1][...] = jnp.zeros_like(refs[-1])

    mem = lambda t: pltpu.HBM(t.shape, t.dtype)
    sems = pltpu.SemaphoreType.DMA((n_copies,))
    outs = pl.pallas_call(
        body, name=name,
        out_shape=(sems, sems, *map(mem, srcs), *map(mem, lands), SDS((SUBLANES, LANES), F32)),
        in_specs=[HBM] * (n_s + n_l) + [ANY],
        out_specs=(SEM, SEM, *[HBM] * (n_s + n_l), pl.BlockSpec(memory_space=pltpu.VMEM)),
        input_output_aliases={i: 2 + i for i in range(n_s + n_l)},
        compiler_params=pltpu.CompilerParams(has_side_effects=DATAFLOW),
    )(*[pltpu.with_memory_space_constraint(t, pltpu.HBM) for t in (*srcs, *lands)], after)
    return _Flight(copies, n_copies, outs[0], outs[1], outs[2:2 + n_s], outs[2 + n_s:2 + n_s + n_l], outs[-1])


def _land(name, flight, after):
    n_s, n_l = len(flight.srcs), len(flight.lands)

    def body(*refs):
        src, land = refs[:n_s], refs[n_s:n_s + n_l]
        send, recv = refs[n_s + n_l:n_s + n_l + 2]
        for cp in flight.copies(src, land, send, recv):
            cp.wait_send()
            cp.wait_recv()

    mem = lambda t: pltpu.HBM(t.shape, t.dtype)
    outs = pl.pallas_call(
        body, name=name, out_shape=(*map(mem, flight.srcs), *map(mem, flight.lands)),
        in_specs=[HBM] * (n_s + n_l) + [SEM, SEM, ANY], out_specs=tuple([HBM] * (n_s + n_l)),
        input_output_aliases={i: i for i in range(n_s + n_l)},
        compiler_params=pltpu.CompilerParams(has_side_effects=DATAFLOW),
    )(*flight.srcs, *flight.lands, flight.send, flight.recv, after)
    return list(outs[:n_s]), list(outs[n_s:])


def _empty_like(shapes_from, lead):
    return [lax.empty((lead,) + t.shape[1:], t.dtype) for t in shapes_from]


def _scatter_off(name, chunks, after):
    def copies(src, land, send, recv):
        x, y, c = _place()
        return [pltpu.make_async_remote_copy(
            src_ref=src[a].at[2 * px + py], dst_ref=land[a].at[k], send_sem=send.at[3 * a + k],
            recv_sem=recv.at[3 * a + k], device_id=(px, py, c), device_id_type=MESH_ID)
            for a in range(len(chunks)) for k, (px, py) in enumerate(_other_chips(x, y))]

    return _take_off(name, chunks, _empty_like(chunks, 3), copies, 3 * len(chunks), after)


def _swap_off(name, arrs, after):
    def copies(src, land, send, recv):
        x, y, c = _place()
        return [pltpu.make_async_remote_copy(
            src_ref=src[a], dst_ref=land[a], send_sem=send.at[a], recv_sem=recv.at[a],
            device_id=(x, y, 1 - c), device_id_type=MESH_ID) for a in range(len(arrs))]

    return _take_off(name, arrs, [lax.empty(t.shape, t.dtype) for t in arrs], copies, len(arrs), after)


def _devices_off(name, block, after):
    me = 4 * lax.axis_index("x") + 2 * lax.axis_index("y") + lax.axis_index("c")
    land = lax.dynamic_update_index_in_dim(lax.empty((N_DEV,) + block.shape, block.dtype), block, me, 0)

    def copies(src, land, send, recv):
        x, y, c = _place()
        mine = 4 * x + 2 * y + c
        return [pltpu.make_async_remote_copy(
            src_ref=src[0], dst_ref=land[0].at[mine], send_sem=send.at[k - 1], recv_sem=recv.at[k - 1],
            device_id=(x ^ (k >> 2), y ^ ((k >> 1) & 1), c ^ (k & 1)), device_id_type=MESH_ID)
            for k in range(1, N_DEV)]

    return _take_off(name, [block], [land], copies, N_DEV - 1, after)


def _half_rows(shape, c, other=False):
    half = shape[0] // 2
    return pl.ds(((1 - c) if other else c) * half, half)


def _gather_start(name, shards, lands, after):
    n = len(shards)

    def body(*refs):
        src, land, (send, recv) = refs[:n], refs[n:2 * n], refs[2 * n + 1:2 * n + 3]
        x, y, c = _place()
        me = 2 * x + y
        for a in range(n):
            mine = _half_rows(shards[a].shape, c)
            for j, (px, py) in enumerate(_other_chips(x, y)):
                pltpu.make_async_remote_copy(
                    src_ref=src[a].at[mine], dst_ref=land[a].at[me, mine], send_sem=send.at[3 * a + j],
                    recv_sem=recv.at[3 * a + j], device_id=(px, py, c), device_id_type=MESH_ID).start()
        token = refs[-1]
        token[...] = jnp.zeros_like(token)

    mem = lambda t: pltpu.HBM(t.shape, t.dtype)
    pair = pltpu.SemaphoreType.DMA((3 * n,))
    outs = pl.pallas_call(
        body, name=name,
        out_shape=(pair, pair, *map(mem, shards), *map(mem, lands), SDS((SUBLANES, LANES), F32)),
        in_specs=[HBM] * (2 * n) + [ANY],
        out_specs=(SEM, SEM, *[HBM] * (2 * n), pl.BlockSpec(memory_space=pltpu.VMEM)),
        input_output_aliases={i: 2 + i for i in range(2 * n)},
        compiler_params=pltpu.CompilerParams(has_side_effects=DATAFLOW),
    )(*[pltpu.with_memory_space_constraint(t, pltpu.HBM) for t in (*shards, *lands)], after)
    return outs[0], outs[1], list(outs[2:2 + n]), list(outs[2 + n:2 + 2 * n]), outs[-1]


def _gather_pass(name, send, recv, shards, lands, after, first=0):
    n = len(shards)

    def body(*refs):
        src, land, (send, recv, _) = refs[:n], refs[n:2 * n], refs[2 * n:2 * n + 3]
        fsend, frecv = refs[2 * n + 3], refs[2 * n + 4]
        x, y, c = _place()
        me = 2 * x + y
        for a in range(n):
            mine = _half_rows(shards[a].shape, c)
            for j, (px, py) in enumerate(_other_chips(x, y)):
                far = 2 * px + py
                ici = pltpu.make_async_remote_copy(
                    src_ref=src[a].at[mine], dst_ref=land[a].at[far, mine], send_sem=send.at[3 * (first + a) + j],
                    recv_sem=recv.at[3 * (first + a) + j], device_id=(px, py, c), device_id_type=MESH_ID)
                ici.wait_recv()
                ici.wait_send()
                pltpu.make_async_remote_copy(
                    src_ref=land[a].at[far, mine], dst_ref=land[a].at[far, mine], send_sem=fsend.at[3 * a + j],
                    recv_sem=frecv.at[3 * a + j], device_id=(x, y, 1 - c), device_id_type=MESH_ID).start()
        token = refs[-1]
        token[...] = jnp.zeros_like(token)

    mem = lambda t: pltpu.HBM(t.shape, t.dtype)
    pair = pltpu.SemaphoreType.DMA((3 * n,))
    outs = pl.pallas_call(
        body, name=name,
        out_shape=(pair, pair, *map(mem, lands), SDS((SUBLANES, LANES), F32)),
        in_specs=[HBM] * (2 * n) + [SEM, SEM, ANY],
        out_specs=(SEM, SEM, *[HBM] * n, pl.BlockSpec(memory_space=pltpu.VMEM)),
        input_output_aliases={n + i: 2 + i for i in range(n)},
        compiler_params=pltpu.CompilerParams(has_side_effects=DATAFLOW),
    )(*shards, *lands, send, recv, after)
    return outs[0], outs[1], list(outs[2:2 + n]), outs[-1]


def _gather_wait(name, fsend, frecv, lands, after):
    n = len(lands)

    def body(*refs):
        land, (fsend, frecv, _) = refs[:n], refs[n:n + 3]
        x, y, c = _place()
        for a in range(n):
            for j, (px, py) in enumerate(_other_chips(x, y)):
                far = 2 * px + py
                mine = _half_rows(lands[a].shape[1:], c)
                theirs = _half_rows(lands[a].shape[1:], c, other=True)
                pltpu.make_async_remote_copy(
                    src_ref=land[a].at[far, mine], dst_ref=land[a].at[far, mine], send_sem=fsend.at[3 * a + j],
                    recv_sem=frecv.at[3 * a + j], device_id=(x, y, 1 - c), device_id_type=MESH_ID).wait_send()
                pltpu.make_async_remote_copy(
                    src_ref=land[a].at[far, theirs], dst_ref=land[a].at[far, theirs], send_sem=fsend.at[3 * a + j],
                    recv_sem=frecv.at[3 * a + j], device_id=(x, y, 1 - c), device_id_type=MESH_ID).wait_recv()

    mem = lambda t: pltpu.HBM(t.shape, t.dtype)
    return list(pl.pallas_call(
        body, name=name, out_shape=tuple(map(mem, lands)), in_specs=[HBM] * n + [SEM, SEM, ANY],
        out_specs=tuple([HBM] * n), input_output_aliases={i: i for i in range(n)},
        compiler_params=pltpu.CompilerParams(has_side_effects=DATAFLOW),
    )(*lands, fsend, frecv, after))


def _after(token):
    return _Exchange([token], [], [], lambda *_: None, lambda *_: None)


def _swap_sibling(arrs):
    n = len(arrs)

    def copies(ins, outs, sems):
        send, recv = sems
        x, y, c = _place()
        return [pltpu.make_async_remote_copy(
            src_ref=ins[a], dst_ref=outs[a], send_sem=send.at[a], recv_sem=recv.at[a],
            device_id=(x, y, 1 - c), device_id_type=MESH_ID) for a in range(n)]

    def start(ins, outs, sems):
        for cp in copies(ins, outs, sems):
            cp.start()

    def wait(ins, outs, sems):
        cps = copies(ins, outs, sems)
        for cp in cps:
            cp.wait_recv()
        for cp in cps:
            cp.wait_send()

    return _Exchange(arrs, [SDS(s.shape, s.dtype) for s in arrs],
                     [pltpu.SemaphoreType.DMA((n,)), pltpu.SemaphoreType.DMA((n,))], start, wait)


def _sum_devices(slots):
    def body(s_ref, o_ref):
        acc = s_ref[0]
        for d in range(1, N_DEV):
            acc = acc + s_ref[d]
        o_ref[...] = acc

    return pl.pallas_call(
        body, in_specs=[pl.BlockSpec(memory_space=pltpu.VMEM)], out_specs=pl.BlockSpec(memory_space=pltpu.VMEM),
        out_shape=SDS(slots.shape[1:], F32), name="sum_small",
        compiler_params=pltpu.CompilerParams(vmem_limit_bytes=32 * 1024 * 1024))(slots)


def _adam_small(ws, gs, ms, vs):
    n = len(ws)

    def body(*refs):
        for i in range(n):
            w_ref, g_ref, m_ref, v_ref = (refs[k * n + i] for k in range(4))
            outs = _adam_math(w_ref[...], g_ref[...], m_ref[...], v_ref[...])
            for k in range(3):
                refs[(4 + k) * n + i][...] = outs[k]

    vmem = pl.BlockSpec(memory_space=pltpu.VMEM)
    return pl.pallas_call(
        body, in_specs=[vmem] * (4 * n), out_specs=[vmem] * (3 * n),
        out_shape=[SDS(w.shape, F32) for w in ws] * 3, name="adam_small",
        compiler_params=pltpu.CompilerParams(vmem_limit_bytes=32 * 1024 * 1024))(*ws, *gs, *ms, *vs)


def _local_step(x, target, small, big, tb, distributed):
    dist = distributed
    me = (2 * lax.axis_index("x") + lax.axis_index("y")) if dist else 0
    tb_ssm = min(tb, 256)
    bucket = jnp.asarray(_bucket_table())
    place_own = lambda t: lax.dynamic_update_index_in_dim(lax.empty((N_CHIPS,) + t.shape, t.dtype), t, me, 0)
    if dist:
        in_legs = _gather_start("gather_in_start", [big["w_in"]], [place_own(big["w_in"])], small["d_skip"])
        names = sorted(small)
        in_token, values = lax.optimization_barrier((in_legs[4], [small[n] for n in names]))
        small = dict(zip(names, values))
    g1, g2, g3, g4 = small["norm_mix_pre"], small["norm_mix_post"], small["norm_mlp_pre"], small["norm_mlp_post"]

    keys_first = lambda t: jnp.swapaxes(t, -1, -2)
    bias = _bias_table(small["rel_bias"], bucket)
    sink_rows = keys_first(_pair_layout(jnp.broadcast_to(small["sinks"].reshape(N_HEADS, 1, 1), (N_HEADS, BLOCK, 1))))
    disc_args = (small["lam_re"], small["lam_im"], small["log_dt"], small["b_re"], small["b_im"])
    (ab_re, ab_im, bb_re, bb_im), disc_vjp = jax.vjp(_ssm_discretize, *disc_args)
    tab_f, tab_b = _scan_tables(ab_re, ab_im)
    bmat = _bf(_b_matrix(bb_re, bb_im))
    cmat = _bf(_c_matrix(small["c_re"], small["c_im"]))
    bmat_t, cmat_t = bmat.transpose(0, 2, 1), cmat.transpose(0, 2, 1)
    d_skip = small["d_skip"]

    mix = ("w_glu", "w_attn_branch", "w_ssm_branch", "w_out")
    rest = [big[n] for n in mix + ("w_ff_in", "w_ff_out")]
    if dist:
        send, recv, src, lands, _ = in_legs
        rest_lands = [place_own(t) for t in rest]
        tab_f, tab_b, bias, sink_rows, bmat, cmat, bmat_t, cmat_t = lax.optimization_barrier(
            (tab_f, tab_b, bias, sink_rows, bmat, cmat, bmat_t, cmat_t))
        corner = lambda t: t.reshape(-1, t.shape[-1])[:1, :LANES].astype(F32)
        prepared = sum(map(corner, [tab_b, bias, sink_rows, bmat, cmat] + rest_lands), in_token[:1])
        send, recv, lands, in_passed = _gather_pass("gather_in_pass", send, recv, src, lands, prepared)
        (g_in,) = _gather_wait("gather_in_wait", send, recv, lands, in_passed)
        w_in = g_in.reshape(IN_W, D_MODEL)
    else:
        w_in = big["w_in"]
    token = None
    n_mix = len(mix)
    if dist:
        send, recv, rest, lands, token = _gather_start("gather_rest_start", rest, rest_lands, in_passed)
    h1, q, k, v, u, ga, gs = _inproj_fwd(x, g1, w_in, tb, _after(token) if dist else None)
    s, h = _ssm_fwd(u, bmat, cmat, tab_f, d_skip, tb)
    if dist:
        fsend, frecv, mix_lands, token = _gather_pass("gather_mix_pass", send, recv, rest[:n_mix], lands[:n_mix], s)
    att = _attn_fwd(q, k, v, bias, sink_rows, _after(token) if dist else None)[0]
    if dist:
        w_mix = _gather_wait("gather_mix_wait", fsend, frecv, mix_lands, att)
        fsend, frecv, ff_lands, token = _gather_pass(
            "gather_ff_pass", send, recv, rest[n_mix:], lands[n_mix:], w_mix[0], n_mix)
        rest = w_mix + ff_lands
    w_glu, w_ab, w_sb, w_out = rest[:n_mix]
    w_glu = w_glu.reshape(SSM_W, SSM_W)
    w_out = w_out.reshape(D_MODEL, D_MODEL)
    x2 = _merge_fwd(x, s, att, ga, gs, g2, w_glu, w_ab, w_sb, w_out, tb, _after(token) if dist else None)
    if dist:
        rest[n_mix:] = _gather_wait("gather_ff_wait", fsend, frecv, ff_lands, x2)
    w_ffi, w_ffo = [rest[n_mix]], rest[n_mix + 1]
    dy, df, h3, ra, loss_acc, dg4 = _mlp_fwd_loss(x2, target, g3, g4, w_ffi, w_ffo, tb)

    dx2, da, dg3 = _mlp_bwd(x2, dy, df, ra, g3, w_ffi, w_ffo, tb)
    tl = min(2048, x.shape[0])
    chunked = (N_CHIPS, D_FF // N_CHIPS, D_MODEL)
    d_ffi, b_ffi = _matmul_tn("grad_w_ff_in", h3, da, D_MODEL, D_FF // FF_CHUNKS, tl, True)
    d_ffo, b_ffo = _matmul_tn("grad_w_ff_out", ra, df, D_FF // FF_CHUNKS, D_MODEL, tl, False, square_a=True)
    d_ffo, b_ffo = d_ffo.reshape(chunked), b_ffo.reshape(chunked)
    behind = lambda flight: _after(flight.token) if dist else None
    ff_fl = _scatter_off("scatter_ff_off", [b_ffi, b_ffo], d_ffo) if dist else None
    outs = _merge_bwd(dx2, s, att, ga, gs, g2, w_glu, w_ab, w_sb, w_out, tb_ssm, behind(ff_fl))
    ds, datt, dga, dgs, dg2, d_glu, d_ab, d_sb, d_out, b_glu, b_ab, b_sb, b_out = outs
    glu4, out4 = (N_CHIPS, SSM_W // N_CHIPS, SSM_W), (N_CHIPS, D_MODEL // N_CHIPS, D_MODEL)
    d_mix = [d_glu.reshape(glu4), d_ab, d_sb, d_out.reshape(out4)]
    b_mix = [b_glu.reshape(glu4), b_ab, b_sb, b_out.reshape(out4)]
    mix_fl = _scatter_off("scatter_mix_off", b_mix, d_mix[-1]) if dist else None
    du, d_bmat, d_cmat, da_acc, dd_skip = _ssm_bwd(
        ds, u, h, bmat_t, cmat_t, tab_b, d_skip, tb, behind(mix_fl))
    dq, dk, dv, dbias, dsink_rows = _attn_bwd(q, k, v, datt, bias, sink_rows)
    swap_fl = None
    if dist:
        r_ffi, r_ffo = _land("scatter_ff_land", ff_fl, dq)[1]
        p_ffi = _sum4("sum_w_ff_in", d_ffi, r_ffi, me)
        p_ffo = _sum4("sum_w_ff_out", d_ffo, r_ffo, me)
        swap_fl = _swap_off("swap_ff_off", [p_ffi, p_ffo], r_ffo)
    dx, dpj, dg1 = _inproj_bwd(x, dx2, dq, dk, dv, du, dga, dgs, g1, w_in, tb, behind(swap_fl))

    dab_re, dab_im = _state_unlayout(jnp.sum(da_acc, axis=0))
    dbb_re, dbb_im = _b_matrix_grad(d_bmat)
    d_lam_re, d_lam_im, d_log_dt, d_b_re, d_b_im = disc_vjp((dab_re, dab_im, dbb_re, dbb_im))
    d_c_re, d_c_im = _c_matrix_grad(d_cmat)
    d_rel = _bias_grad(dbias, bucket)
    d_sinks = jnp.sum(_pair_unlayout(keys_first(dsink_rows)), axis=(1, 2))
    small_grads = dict(
        norm_mix_pre=dg1, norm_mix_post=dg2, norm_mlp_pre=dg3, norm_mlp_post=dg4, rel_bias=d_rel, sinks=d_sinks,
        lam_re=d_lam_re, lam_im=d_lam_im, log_dt=d_log_dt, b_re=d_b_re, b_im=d_b_im, c_re=d_c_re, c_im=d_c_im,
        d_skip=dd_skip)
    small_fl = _devices_off("small_off", _pack(small_grads, loss_acc), swap_fl.token) if dist else None
    outs = _matmul_tn("grad_w_in", dpj, h1, IN_W // 2, D_MODEL, tl, False, behind(small_fl))
    in4 = (N_CHIPS, IN_W // N_CHIPS, D_MODEL)
    d_in, b_in = outs[0].reshape(in4), outs[1].reshape(in4)
    if not dist:
        return loss_acc, dx, small_grads, dict(zip(BIG, [d_in] + d_mix + [d_ffi, d_ffo]))
    in_fl = _scatter_off("scatter_w_in_off", [b_in], d_in)
    (p_ffi, p_ffo), (s_ffi, s_ffo) = _land("swap_ff_land", swap_fl, in_fl.token)
    r_mix = _land("scatter_mix_land", mix_fl, in_fl.token)[1]
    p_mix = [_sum4("sum_" + n, d, r, me) for n, d, r in zip(mix, d_mix, r_mix)]
    mix_swap = _swap_off("swap_mix_off", p_mix, in_fl.token)
    pending = dict(d_in=d_in, in_fl=in_fl, mix_swap=mix_swap, w_ff_in=(p_ffi, s_ffi), w_ff_out=(p_ffo, s_ffo), me=me)
    return loss_acc, dx, small_fl, pending


SMALL = ['norm_mix_pre', 'norm_mix_post', 'norm_mlp_pre', 'norm_mlp_post', 'rel_bias', 'sinks', 'lam_re', 'lam_im',
         'log_dt', 'b_re', 'b_im', 'c_re', 'c_im', 'd_skip']
BIG = ['w_in', 'w_glu', 'w_attn_branch', 'w_ssm_branch', 'w_out', 'w_ff_in', 'w_ff_out']
WEIGHTS = ['norm_mix_pre', 'norm_mix_post', 'norm_mlp_pre', 'norm_mlp_post', 'w_in', 'rel_bias', 'sinks', 'lam_re',
           'lam_im', 'log_dt', 'b_re', 'b_im', 'c_re', 'c_im', 'd_skip', 'w_glu', 'w_attn_branch', 'w_ssm_branch',
           'w_out', 'w_ff_in', 'w_ff_out']
PACK_COLS = 1024
PACK_ORDER = ['b_re', 'b_im', 'c_re', 'c_im', 'lam_re', 'lam_im', 'norm_mix_pre', 'norm_mix_post', 'norm_mlp_pre',
              'norm_mlp_post', 'rel_bias', 'sinks', 'log_dt', 'd_skip']


STATE_MINOR = ('b_re', 'b_im')
PACK_ROWS = 144
LOSS_ROW = 140


def _pack(named, loss_acc):
    parts = []
    for n in PACK_ORDER:
        a = jnp.swapaxes(named[n], -1, -2) if n in STATE_MINOR else named[n]
        flat = a.reshape(-1)
        rows = -(-flat.shape[0] // PACK_COLS)
        parts.append(jnp.pad(flat, (0, rows * PACK_COLS - flat.shape[0])).reshape(rows, PACK_COLS))
    assert sum(p.shape[0] for p in parts) == LOSS_ROW
    parts.append(jnp.pad(loss_acc[0:1], ((0, PACK_ROWS - LOSS_ROW - 1), (0, PACK_COLS - loss_acc.shape[1]))))
    return jnp.concatenate(parts, axis=0)


def _unpack(packed, shapes):
    out, at = {}, 0
    for n in PACK_ORDER:
        shape = shapes[n][:-2] + (shapes[n][-1], shapes[n][-2]) if n in STATE_MINOR else shapes[n]
        size = int(np.prod(shape))
        rows = -(-size // PACK_COLS)
        blk = packed[at:at + rows]
        out[n] = (blk.reshape(-1)[:size] if size % PACK_COLS else blk).reshape(shape)
        at += rows
    return out


def kernel(x, norm_mix_pre, norm_mix_post, norm_mlp_pre, norm_mlp_post, w_in, rel_bias, sinks, lam_re, lam_im, log_dt, b_re, b_im, c_re, c_im, d_skip, w_glu, w_attn_branch, w_ssm_branch, w_out, w_ff_in, w_ff_out, loss_target, m_norm_mix_pre, m_norm_mix_post, m_norm_mlp_pre, m_norm_mlp_post, m_w_in, m_rel_bias, m_sinks, m_lam_re, m_lam_im, m_log_dt, m_b_re, m_b_im, m_c_re, m_c_im, m_d_skip, m_w_glu, m_w_attn_branch, m_w_ssm_branch, m_w_out, m_w_ff_in, m_w_ff_out, v_norm_mix_pre, v_norm_mix_post, v_norm_mlp_pre, v_norm_mlp_post, v_w_in, v_rel_bias, v_sinks, v_lam_re, v_lam_im, v_log_dt, v_b_re, v_b_im, v_c_re, v_c_im, v_d_skip, v_w_glu, v_w_attn_branch, v_w_ssm_branch, v_w_out, v_w_ff_in, v_w_ff_out):
    env = dict(locals())
    w = {n: env[n] for n in WEIGHTS}
    m = {n: env["m_" + n] for n in WEIGHTS}
    v = {n: env["v_" + n] for n in WEIGHTS}
    seq = x.shape[1]
    tb = min(512, seq)

    small = {n: w[n] for n in ('norm_mix_pre', 'norm_mix_post', 'norm_mlp_pre', 'norm_mlp_post', 'rel_bias')}
    small.update({n: w[n][0] for n in ('sinks', 'lam_re', 'lam_im', 'log_dt', 'b_re', 'b_im', 'c_re', 'c_im')})
    small['d_skip'] = w['d_skip']
    shard = lambda t, n: t[n][0].T if n == 'w_in' else t[n][0]
    unshard = lambda a, n: (a.T if n == 'w_in' else a)[None]
    _, dx, small_fl, pending = _local_step(
        x[0], loss_target[0], small, {n: _bf(shard(w, n)) for n in BIG}, tb, True)

    grads, deltas, new_m, new_v = {}, {}, {}, {}

    def adam(n, partials, after=None):
        outs = _adam_pair("adam_" + n, (shard(w, n), *partials, shard(m, n), shard(v, n)), after)
        grads[n], deltas[n], new_m[n], new_v[n] = [unshard(a, n) for a in outs]
        return outs[3]

    mix = ("w_glu", "w_attn_branch", "w_ssm_branch", "w_out")
    in_fl = pending["in_fl"]
    last = pending["mix_swap"].token
    for n in ("w_ff_in", "w_ff_out"):
        last = adam(n, pending[n], last)
    for n, partials in zip(mix, zip(*_land("swap_mix_land", pending["mix_swap"], last))):
        last = adam(n, partials, last)

    small_g = _sum_devices(_land("small_land", small_fl, last)[1][0])
    loss = small_g[LOSS_ROW, 0]
    minor = lambda t, n: jnp.swapaxes(t, -1, -2) if n in STATE_MINOR else t
    g_small = _unpack(small_g, {n: w[n].shape for n in SMALL})
    outs = _adam_small([minor(w[n], n) for n in SMALL], [g_small[n] for n in SMALL],
                       [minor(m[n], n) for n in SMALL], [minor(v[n], n) for n in SMALL])
    grads.update({n: minor(g_small[n], n) for n in SMALL})
    for k, dst in enumerate((deltas, new_m, new_v)):
        dst.update({n: minor(a, n) for n, a in zip(SMALL, outs[k * len(SMALL):(k + 1) * len(SMALL)])})

    (r_in,) = _land("scatter_w_in_land", in_fl, outs[0])[1]
    p_in = _sum4("sum_w_in", pending["d_in"], r_in, pending["me"])
    (s_in,) = _exchange_alone("swap_w_in", _swap_sibling([p_in]))
    adam("w_in", (p_in, s_in))

    return (loss, dx[None], *[grads[n] for n in WEIGHTS], *[deltas[n] for n in WEIGHTS],
            *[new_m[n] for n in WEIGHTS], *[new_v[n] for n in WEIGHTS])
```

```python
import functools
import math

import numpy as np
import jax
import jax.numpy as jnp
from jax import lax
from jax.experimental import pallas as pl
from jax.experimental.pallas import tpu as pltpu

F32 = jnp.float32
BF16 = jnp.bfloat16

D_MODEL = 1024
N_HEADS = 8
N_KV = 2
Q_GROUP = 4
HEAD_DIM = 64
ATTN_W = 512
KV_W = 128
BLOCK = 128
N_BUCKETS = 32
MAX_DISTANCE = 128
NEG_INF = -1e30
SSM_W = 512
SSM_GROUP = 16
SSM_GROUPS = 32
SSM_STATE = 64
N_SUPER = 4
GROUPS_PER_SUPER = SSM_GROUPS // N_SUPER
SUPER_IN = GROUPS_PER_SUPER * SSM_GROUP
SUPER_HALF = GROUPS_PER_SUPER * SSM_STATE
SUPER_W = 2 * SUPER_HALF
STATE_COLS = N_SUPER * SUPER_W
D_FF = 4096
FF_CHUNKS = 4
IN_W = 3328
SPLITS = (0, 512, 640, 768, 1280, 2304, 3328)
RMS_EPS = 1e-6
N_CHIPS = 4
N_DEV = 8
SUBLANES = 8
LANES = 128
STATE_TILES = STATE_COLS // LANES
SUPER_TILES = SUPER_W // LANES

ADAM_LR = 0.001
ADAM_B1 = 0.9
ADAM_B2 = 0.999
ADAM_EPS = 1e-08
ADAM_WD = 0.01
ADAM_STEP = 10

VMEM_BIG = 56 * 1024 * 1024
SDS = jax.ShapeDtypeStruct
MESH_ID = pl.DeviceIdType.MESH
ANY = pl.BlockSpec(memory_space=pl.ANY)


def _bf(x):
    return x.astype(BF16)


def _mm(a, b):
    return jnp.dot(a, b, preferred_element_type=F32)


def _mm_nt(a, b):
    return lax.dot_general(a, b, (((1,), (1,)), ((), ())), preferred_element_type=F32)


def _mm_tn(a, b):
    return lax.dot_general(a, b, (((0,), (0,)), ((), ())), preferred_element_type=F32)


def _sig(x):
    return 1.0 / (1.0 + jnp.exp(-x))


def _rms(x, g):
    r = lax.rsqrt(jnp.mean(x * x, axis=-1, keepdims=True) + RMS_EPS)
    xh = x * r
    return xh * g, xh, r


def _rms_bwd(dout, xh, r, g):
    dg = jnp.sum(dout * xh, axis=0, keepdims=True)
    dxh = dout * g
    dx = r * (dxh - xh * jnp.mean(dxh * xh, axis=-1, keepdims=True))
    return dx, dg


_GELU_C = math.sqrt(2.0 / math.pi)


def _gelu_and_grad(x):
    x2 = x * x
    inner = _GELU_C * (x + 0.044715 * (x2 * x))
    t = jnp.tanh(inner)
    y = 0.5 * x * (1.0 + t)
    dy = 0.5 * (1.0 + t) + 0.5 * x * (1.0 - t * t) * (_GELU_C * (1.0 + 3.0 * 0.044715 * x2))
    return y, dy


def _zero_map(nd, *_):
    return (0,) * nd


def _params(n_axes, vmem=None):
    return pltpu.CompilerParams(dimension_semantics=("arbitrary",) * n_axes, vmem_limit_bytes=vmem)


class _Exchange:
    def __init__(self, ins, outs, sems, start, wait):
        self.ins, self.outs, self.sems, self.start, self.wait = list(ins), list(outs), list(sems), start, wait


def _fused_call(name, body, grid, in_specs, out_specs, out_shape, scratch, args, exchange, params):
    n_in, n_out, n_scr = len(in_specs), len(out_specs), len(scratch)
    if exchange is None:
        fn = body
    else:
        ex = exchange
        n_xi, n_xo = len(ex.ins), len(ex.outs)

        def fn(*refs):
            at = 0
            parts = []
            for n in (n_in, n_xi, n_out, n_xo, n_scr, len(ex.sems)):
                parts.append(refs[at:at + n])
                at += n
            ins, x_in, outs, x_out, scr, x_sem = parts
            ids = [pl.program_id(a) for a in range(len(grid))]
            first = functools.reduce(jnp.logical_and, [i == 0 for i in ids])
            last = functools.reduce(jnp.logical_and, [i == g - 1 for i, g in zip(ids, grid)])

            @pl.when(first)
            def _():
                ex.start(x_in, x_out, x_sem)

            body(*ins, *outs, *scr)

            @pl.when(last)
            def _():
                ex.wait(x_in, x_out, x_sem)

        in_specs = list(in_specs) + [ANY] * n_xi
        out_specs = list(out_specs) + [ANY] * n_xo
        out_shape = list(out_shape) + ex.outs
        scratch = list(scratch) + ex.sems
        args = list(args) + ex.ins
    return pl.pallas_call(fn, grid=grid, in_specs=in_specs, out_specs=out_specs, out_shape=out_shape,
                          scratch_shapes=list(scratch), name=name, compiler_params=params)(*args)


def _exchange_alone(name, ex):
    def body(*refs):
        n_xi, n_xo = len(ex.ins), len(ex.outs)
        x_in, x_out, x_sem = refs[:n_xi], refs[n_xi:n_xi + n_xo], refs[n_xi + n_xo:]
        ex.start(x_in, x_out, x_sem)
        ex.wait(x_in, x_out, x_sem)

    return pl.pallas_call(body, in_specs=[ANY] * len(ex.ins), out_specs=[ANY] * len(ex.outs), out_shape=ex.outs,
                          scratch_shapes=ex.sems, name=name)(*ex.ins)


def _rowcall(name, body, seq, tb, rows, consts, row_outs, acc_outs, scratch=(), reverse=False, vmem=None,
             exchange=None):
    nb = seq // tb
    rmap = (lambda i: (nb - 1 - i, 0)) if reverse else (lambda i: (i, 0))
    tmap = lambda i: (0,) + rmap(i)

    def row_spec(width):
        if isinstance(width, tuple):
            return pl.BlockSpec((width[0], tb, width[1]), tmap)
        return pl.BlockSpec((tb, width), rmap)

    def row_shape(width):
        return (width[0], seq, width[1]) if isinstance(width, tuple) else (seq, width)

    in_specs = [row_spec(a.shape[1] if a.ndim == 2 else (a.shape[0], a.shape[2])) for a in rows]
    in_specs += [pl.BlockSpec(a.shape, functools.partial(_zero_map, a.ndim), pipeline_mode=pl.Buffered(1))
                 for a in consts]
    out_specs = [row_spec(c) for c, _ in row_outs] + [ANY] * len(acc_outs)
    out_shape = [SDS(row_shape(c), dt) for c, dt in row_outs] + [SDS(s, dt) for s, dt in acc_outs]
    n_main = len(rows) + len(consts) + len(row_outs)
    n_acc = len(acc_outs)

    def fn(*refs):
        main, acc_hbm, rest = refs[:n_main], refs[n_main:n_main + n_acc], refs[n_main + n_acc:]
        acc_vmem, own = rest[:n_acc], rest[n_acc:]
        body(*main, *acc_vmem, *own)

        @pl.when(pl.program_id(0) == nb - 1)
        def _():
            for src, dst in zip(acc_vmem, acc_hbm):
                pltpu.sync_copy(src, dst)

    buffers = [pltpu.VMEM(s, dt) for s, dt in acc_outs] + list(scratch)
    return _fused_call(name, fn if acc_outs else body, (nb,), in_specs, out_specs, out_shape, buffers,
                       [*rows, *consts], exchange, _params(1, vmem))


def _inproj_fwd(x, g1, w_in, tb, exchange=None):
    seq = x.shape[0]

    def body(x_ref, g_ref, w_ref, h_ref, q_ref, k_ref, v_ref, u_ref, ga_ref, gs_ref):
        h, _, _ = _rms(x_ref[...], g_ref[...])
        hb = _bf(h)
        h_ref[...] = hb
        pj = _mm_nt(hb, w_ref[...])
        q_ref[...] = _bf(pj[:, SPLITS[0]:SPLITS[1]])
        k_ref[...] = _bf(pj[:, SPLITS[1]:SPLITS[2]])
        v_ref[...] = _bf(pj[:, SPLITS[2]:SPLITS[3]])
        u_ref[...] = pj[:, SPLITS[3]:SPLITS[4]]
        ga_ref[...] = pj[:, SPLITS[4]:SPLITS[5]]
        gs_ref[...] = pj[:, SPLITS[5]:SPLITS[6]]

    return _rowcall("inproj_fwd", body, seq, tb, [x], [g1, w_in],
                    [(D_MODEL, BF16), (ATTN_W, BF16), (KV_W, BF16), (KV_W, BF16), (SSM_W, F32),
                     (D_MODEL, F32), (D_MODEL, F32)], [], vmem=VMEM_BIG, exchange=exchange)


def _inproj_bwd(x, dx2, dq, dk, dv, du, dga, dgs, g1, w_in, tb, exchange=None):
    seq = x.shape[0]

    def body(x_ref, dx2_ref, dq_ref, dk_ref, dv_ref, du_ref, dga_ref, dgs_ref, g_ref, w_ref,
             dx_ref, dpj_ref, dg_ref):
        @pl.when(pl.program_id(0) == 0)
        def _():
            dg_ref[...] = jnp.zeros_like(dg_ref)

        dpj = jnp.concatenate([dq_ref[...], dk_ref[...], dv_ref[...], _bf(du_ref[...]),
                               dga_ref[...], dgs_ref[...]], axis=1)
        dpj_ref[...] = dpj
        dh = _mm(dpj, w_ref[...])
        g = g_ref[...]
        _, xh, r = _rms(x_ref[...], g)
        dxn, dg = _rms_bwd(dh, xh, r, g)
        dx_ref[...] = dx2_ref[...] + dxn
        dg_ref[...] += dg

    return _rowcall("inproj_bwd", body, seq, tb, [x, dx2, dq, dk, dv, du, dga, dgs], [g1, w_in],
                    [(D_MODEL, F32), (IN_W, BF16)], [((1, D_MODEL), F32)], vmem=VMEM_BIG, exchange=exchange)


def _bucket_table():
    qi = np.arange(BLOCK)[:, None]
    kj = np.arange(2 * BLOCK)[None, :]
    dist = qi + BLOCK - kj
    max_exact = N_BUCKETS // 2
    d = np.maximum(dist, 0)
    df = np.maximum(d, 1).astype(np.float32)
    large = max_exact + (np.log(df / np.float32(max_exact)) / np.float32(math.log(MAX_DISTANCE / max_exact))
                         * np.float32(N_BUCKETS - max_exact)).astype(np.int32)
    large = np.minimum(large, N_BUCKETS - 1)
    bucket = np.where(d < max_exact, d, large)
    valid = (dist >= 0) & (dist < BLOCK)
    return np.where(valid, bucket, -1).astype(np.int32)


def _bias_table(rel_bias, bucket):
    def body(rb_ref, bk_ref, o_ref):
        bk = bk_ref[...]
        has_prev = lax.broadcasted_iota(jnp.int32, bk.shape, 1) >= BLOCK
        for h in range(N_HEADS):
            kh, j, par = h // Q_GROUP, (h // 2) % 2, h % 2
            acc = jnp.full((BLOCK, 2 * BLOCK), NEG_INF, F32)
            for b in range(N_BUCKETS):
                acc = jnp.where(bk == b, rb_ref[b, h], acc)
            o_ref[0, kh, par, :, j * BLOCK:(j + 1) * BLOCK] = jnp.where(has_prev, acc, NEG_INF).T
            o_ref[1, kh, par, :, j * BLOCK:(j + 1) * BLOCK] = acc.T

    return pl.pallas_call(
        body, out_shape=SDS((2, N_KV, 2, 2 * BLOCK, 2 * BLOCK), F32),
        in_specs=[pl.BlockSpec(memory_space=pltpu.SMEM), pl.BlockSpec(memory_space=pltpu.VMEM)],
        out_specs=pl.BlockSpec(memory_space=pltpu.VMEM), name="bias_table",
    )(rel_bias, bucket)


def _bias_grad(dbias, bucket):
    def body(db_ref, bk_ref, o_ref):
        bk = bk_ref[...]
        for h in range(N_HEADS):
            kh, j, par = h // Q_GROUP, (h // 2) % 2, h % 2
            db = db_ref[kh, par, :, j * BLOCK:(j + 1) * BLOCK].T
            for b in range(N_BUCKETS):
                o_ref[b, h] = jnp.sum(jnp.where(bk == b, db, 0.0))

    return pl.pallas_call(
        body, out_shape=SDS((N_BUCKETS, N_HEADS), F32),
        in_specs=[pl.BlockSpec(memory_space=pltpu.VMEM), pl.BlockSpec(memory_space=pltpu.VMEM)],
        out_specs=pl.BlockSpec(memory_space=pltpu.SMEM), name="bias_grad",
    )(dbias, bucket)


TILE = 2 * HEAD_DIM


def _pair_layout(t):
    lead = t.shape[:-3]
    t = t.reshape(lead + (N_KV, 2, 2) + t.shape[-2:])
    nl = len(lead)
    t = jnp.transpose(t, tuple(range(nl)) + (nl, nl + 2, nl + 1, nl + 3, nl + 4))
    return t.reshape(lead + (N_KV, 2, 2 * BLOCK, t.shape[-1]))


def _pair_unlayout(t):
    t = t.reshape(N_KV, 2, 2, BLOCK, t.shape[-1]).transpose(0, 2, 1, 3, 4)
    return t.reshape(N_HEADS, BLOCK, t.shape[-1])


def _halves(t):
    tf = t.astype(F32)
    low = lax.broadcasted_iota(jnp.int32, tf.shape, 1) < HEAD_DIM
    swapped = pltpu.roll(tf, HEAD_DIM, 1)
    zero = jnp.zeros_like(tf)
    return ((_bf(jnp.where(low, tf, zero)), _bf(jnp.where(low, zero, swapped))),
            (_bf(jnp.where(low, swapped, zero)), _bf(jnp.where(low, zero, tf))))


def _fold_halves(even, odd):
    low = lax.broadcasted_iota(jnp.int32, even.shape, 1) < HEAD_DIM
    comb = jnp.where(low, even, odd)
    return comb + pltpu.roll(comb, HEAD_DIM, 1)


def _tile_rows(ref, kh):
    return jnp.concatenate([ref[:, (2 * kh) * TILE:(2 * kh + 1) * TILE],
                            ref[:, (2 * kh + 1) * TILE:(2 * kh + 2) * TILE]], axis=0)


def _halves_t(t):
    tt = t.astype(F32).T
    top = lax.broadcasted_iota(jnp.int32, tt.shape, 0) < HEAD_DIM
    swapped = jnp.concatenate([tt[HEAD_DIM:], tt[:HEAD_DIM]], axis=0)
    zero = jnp.zeros_like(tt)
    return ((_bf(jnp.where(top, tt, zero)), _bf(jnp.where(top, zero, swapped))),
            (_bf(jnp.where(top, swapped, zero)), _bf(jnp.where(top, zero, tt))))


def _attn_probs(km, qk, bias, sink):
    lg = _mm_nt(km, qk) * (HEAD_DIM ** -0.5) + bias
    m = jnp.maximum(jnp.max(lg, axis=0, keepdims=True), sink)
    p = jnp.exp(lg - m)
    es = jnp.exp(sink - m)
    inv = 1.0 / (jnp.sum(p, axis=0, keepdims=True) + es)
    return p * inv, es * inv


def _attn_fwd(q, k, v, bias, sink_rows, exchange=None):
    seq = q.shape[0]
    nblk = seq // BLOCK

    def body(q_ref, kp_ref, kc_ref, vp_ref, vc_ref, b_ref, s_ref, o_ref):
        which = jnp.minimum(pl.program_id(0), 1)
        kms = _halves(jnp.concatenate([kp_ref[...], kc_ref[...]], axis=0))
        vts = _halves_t(jnp.concatenate([vp_ref[...], vc_ref[...]], axis=0))
        for kh in range(N_KV):
            qk = _tile_rows(q_ref, kh)
            acc = jnp.zeros((TILE, 2 * BLOCK), F32)
            for par in range(2):
                pr, _ = _attn_probs(kms[kh][par], qk, b_ref[which, kh, par], s_ref[kh, par])
                acc = acc + _mm(vts[kh][par], _bf(pr))
            acc = acc.T
            o_ref[:, (2 * kh) * TILE:(2 * kh + 1) * TILE] = _bf(acc[:BLOCK])
            o_ref[:, (2 * kh + 1) * TILE:(2 * kh + 2) * TILE] = _bf(acc[BLOCK:])

    cur = lambda n: (n, 0)
    prev = lambda n: (jnp.maximum(n - 1, 0), 0)
    return _fused_call(
        "attn_fwd", body, (nblk,),
        [pl.BlockSpec((BLOCK, ATTN_W), cur),
         pl.BlockSpec((BLOCK, KV_W), prev), pl.BlockSpec((BLOCK, KV_W), cur),
         pl.BlockSpec((BLOCK, KV_W), prev), pl.BlockSpec((BLOCK, KV_W), cur),
         pl.BlockSpec(bias.shape, functools.partial(_zero_map, bias.ndim)),
         pl.BlockSpec(sink_rows.shape, functools.partial(_zero_map, sink_rows.ndim))],
        [pl.BlockSpec((BLOCK, ATTN_W), cur)], [SDS((seq, ATTN_W), BF16)], [],
        [q, k, k, v, v, bias, sink_rows], exchange, _params(1))


def _attn_bwd(q, k, v, d_out, bias, sink_rows, exchange=None):
    seq = q.shape[0]
    nblk = seq // BLOCK

    def body(q_ref, kp_ref, kc_ref, vp_ref, vc_ref, do_ref, b_ref, s_ref,
             dq_ref, dk_ref, dv_ref, db_ref, ds_ref, ck_ref, cv_ref):
        n = pl.program_id(0)

        @pl.when(n == 0)
        def _():
            db_ref[...] = jnp.zeros_like(db_ref)
            ds_ref[...] = jnp.zeros_like(ds_ref)
            ck_ref[...] = jnp.zeros_like(ck_ref)
            cv_ref[...] = jnp.zeros_like(cv_ref)

        @pl.when(n < nblk)
        def _():
            which = jnp.minimum(n, 1)
            scale = HEAD_DIM ** -0.5
            kcat = jnp.concatenate([kp_ref[...], kc_ref[...]], axis=0)
            kms = _halves(kcat)
            kts = _halves_t(kcat)
            vms = _halves(jnp.concatenate([vp_ref[...], vc_ref[...]], axis=0))
            dks, dvs = [], []
            for kh in range(N_KV):
                qk = _tile_rows(q_ref, kh)
                dok = _tile_rows(do_ref, kh)
                dq = jnp.zeros((TILE, 2 * BLOCK), F32)
                dkp, dvp = [], []
                for par in range(2):
                    pr, ps = _attn_probs(kms[kh][par], qk, b_ref[which, kh, par], s_ref[kh, par])
                    dp = _mm_nt(vms[kh][par], dok)
                    rs = jnp.sum(pr * dp, axis=0, keepdims=True)
                    dlg = pr * (dp - rs)
                    ds_ref[kh, par] += -ps * rs
                    db_ref[kh, par] += dlg
                    dlb = _bf(dlg)
                    dq = dq + _mm(kts[kh][par], dlb)
                    dkp.append(_mm(dlb, qk))
                    dvp.append(_mm(_bf(pr), dok))
                dq = _bf((dq * scale).T)
                dq_ref[:, (2 * kh) * TILE:(2 * kh + 1) * TILE] = dq[:BLOCK]
                dq_ref[:, (2 * kh + 1) * TILE:(2 * kh + 2) * TILE] = dq[BLOCK:]
                dks.append(_fold_halves(*dkp))
                dvs.append(_fold_halves(*dvp))
            low = lax.broadcasted_iota(jnp.int32, (2 * BLOCK, TILE), 1) < HEAD_DIM
            dkk = jnp.where(low, dks[0], dks[1]) * scale
            dvv = jnp.where(low, dvs[0], dvs[1])
            dk_ref[...] = _bf(ck_ref[...] + dkk[:BLOCK])
            ck_ref[...] = dkk[BLOCK:]
            dv_ref[...] = _bf(cv_ref[...] + dvv[:BLOCK])
            cv_ref[...] = dvv[BLOCK:]

        @pl.when(n == nblk)
        def _():
            dk_ref[...] = _bf(ck_ref[...])
            dv_ref[...] = _bf(cv_ref[...])

    cur = lambda n: (jnp.minimum(n, nblk - 1), 0)
    prev = lambda n: (jnp.maximum(jnp.minimum(n, nblk - 1) - 1, 0), 0)
    late = lambda n: (jnp.maximum(n - 1, 0), 0)
    kv_spec = lambda m: pl.BlockSpec((BLOCK, KV_W), m)
    acc_b = pl.BlockSpec(bias.shape[1:], functools.partial(_zero_map, bias.ndim - 1))
    acc_s = pl.BlockSpec(sink_rows.shape, functools.partial(_zero_map, sink_rows.ndim))
    return _fused_call(
        "attn_bwd", body, (nblk + 1,),
        [pl.BlockSpec((BLOCK, ATTN_W), cur), kv_spec(prev), kv_spec(cur), kv_spec(prev), kv_spec(cur),
         pl.BlockSpec((BLOCK, ATTN_W), cur),
         pl.BlockSpec(bias.shape, functools.partial(_zero_map, bias.ndim)), acc_s],
        [pl.BlockSpec((BLOCK, ATTN_W), cur), kv_spec(late), kv_spec(late), acc_b, acc_s],
        [SDS((seq, ATTN_W), BF16), SDS((seq, KV_W), BF16), SDS((seq, KV_W), BF16),
         SDS(bias.shape[1:], F32), SDS(sink_rows.shape, F32)],
        [pltpu.VMEM((BLOCK, KV_W), F32), pltpu.VMEM((BLOCK, KV_W), F32)],
        [q, k, k, v, v, d_out, bias, sink_rows], exchange, _params(1))


def _ssm_discretize(lam_re, lam_im, log_dt, b_re, b_im):
    dt = jnp.exp(log_dt)[:, None]
    mag = jnp.exp(lam_re * dt)
    ab_re = mag * jnp.cos(lam_im * dt)
    ab_im = mag * jnp.sin(lam_im * dt)
    nr = ab_re - 1.0
    den = lam_re * lam_re + lam_im * lam_im
    f_re = (nr * lam_re + ab_im * lam_im) / den
    f_im = (ab_im * lam_re - nr * lam_im) / den
    bb_re = f_re[..., None] * b_re - f_im[..., None] * b_im
    bb_im = f_re[..., None] * b_im + f_im[..., None] * b_re
    return ab_re, ab_im, bb_re, bb_im


def _state_layout(re, im):
    lead = re.shape[:-2]
    z = jnp.stack([re, im], axis=-3).reshape(lead + (2, N_SUPER, GROUPS_PER_SUPER, SSM_STATE))
    return jnp.moveaxis(z, -4, -3).reshape(lead + (STATE_COLS,))


def _state_unlayout(vec):
    z = vec.reshape(N_SUPER, 2, GROUPS_PER_SUPER, SSM_STATE).transpose(1, 0, 2, 3)
    z = z.reshape(2, SSM_GROUPS, SSM_STATE)
    return z[0], z[1]


SEG = 4
WINDOW = SEG * SUBLANES


def _scan_tables(ab_re, ab_im):
    pw = [None, (ab_re, ab_im)]
    for _ in range(2, WINDOW + 1):
        pr, pi_ = pw[-1]
        pw.append((pr * ab_re - pi_ * ab_im, pr * ab_im + pi_ * ab_re))
    fwd = np.zeros((7, SUBLANES), np.int64)
    bwd = np.zeros((7, SUBLANES), np.int64)
    for k, shift in enumerate((1, 2, 4)):
        fwd[k] = [SEG * shift if r >= shift else 0 for r in range(SUBLANES)]
        bwd[k] = [SEG * shift if r < SUBLANES - shift else 0 for r in range(SUBLANES)]
    fwd[3] = [SEG * (r + 1) for r in range(SUBLANES)]
    bwd[3] = [SEG * (SUBLANES - r) for r in range(SUBLANES)]
    for k in range(1, SEG):
        fwd[3 + k] = bwd[3 + k] = k
    used = sorted((set(fwd.ravel()) | set(bwd.ravel())) - {0})
    select = lambda which: np.stack([(which == p) for p in used], axis=-1).astype(np.float32)
    stacked = _state_layout(jnp.stack([pw[p][0] for p in used]), jnp.stack([pw[p][1] for p in used]))
    conj_sign = np.where((np.arange(STATE_COLS) // SUPER_HALF) % 2 == 1, -1.0, 1.0).astype(np.float32)
    pick = functools.partial(jnp.einsum, 'krp,pc->krc', precision=lax.Precision.HIGHEST)
    return pick(select(fwd), stacked), pick(select(bwd), stacked) * conj_sign


_EYE = np.eye(GROUPS_PER_SUPER, dtype=np.float32)


def _b_matrix(bb_re, bb_im):
    bb = jnp.stack([bb_re, bb_im]).reshape(2, N_SUPER, GROUPS_PER_SUPER, SSM_STATE, SSM_GROUP)
    m = jnp.einsum('rsgpc,gh->sgcrhp', bb, _EYE)
    return m.reshape(N_SUPER, SUPER_IN, SUPER_W)


def _b_matrix_grad(dm):
    d = dm.reshape(N_SUPER, GROUPS_PER_SUPER, SSM_GROUP, 2, GROUPS_PER_SUPER, SSM_STATE)
    d = jnp.sum(d * _EYE[None, :, None, None, :, None], axis=4)
    d = d.transpose(3, 0, 1, 4, 2).reshape(2, SSM_GROUPS, SSM_STATE, SSM_GROUP)
    return d[0], d[1]


def _c_matrix(c_re, c_im):
    cc = jnp.stack([c_re, -c_im]).reshape(2, N_SUPER, GROUPS_PER_SUPER, SSM_GROUP, SSM_STATE)
    m = jnp.einsum('rsgcp,gh->srgphc', cc, _EYE)
    return m.reshape(N_SUPER, SUPER_W, SUPER_IN)


def _c_matrix_grad(dm):
    d = dm.reshape(N_SUPER, 2, GROUPS_PER_SUPER, SSM_STATE, GROUPS_PER_SUPER, SSM_GROUP)
    d = jnp.sum(d * _EYE[None, None, :, None, :, None], axis=4)
    d = d.transpose(1, 0, 2, 4, 3).reshape(2, SSM_GROUPS, SSM_GROUP, SSM_STATE)
    return d[0], -d[1]


def _cmul_add(xr, xi, ar, ai, sr, si):
    return xr + ar * sr - ai * si, xi + ar * si + ai * sr


def _scan_rows(buf_ref, tab_ref, carry_ref, n_windows, reverse, h_ref=None, da_ref=None):
    order = list(range(SEG - 1, -1, -1)) if reverse else list(range(SEG))
    near = SUBLANES - 1 if reverse else 0
    far = 0 if reverse else SUBLANES - 1
    s_in = SUBLANES - 1 if reverse else 1
    lanes = lambda tile: pl.ds(tile * LANES, LANES)

    def window(w0, tile_re, tile_im, c_re, c_im, acc):
        rows = lambda t: pl.ds(w0 + t, SUBLANES, stride=SEG)
        get = lambda ref, t: (ref.at[tile_re][rows(t), :], ref.at[tile_im][rows(t), :])
        tab = lambda k: (tab_ref[k, :, lanes(tile_re)], tab_ref[k, :, lanes(tile_im)])

        def put(t, xr, xi):
            buf_ref.at[tile_re][rows(t), :] = xr
            buf_ref.at[tile_im][rows(t), :] = xi

        a1 = tab(4)
        er, ei = get(buf_ref, order[0])
        for t in order[1:]:
            er, ei = _cmul_add(*get(buf_ref, t), *a1, er, ei)
            if t != order[-1]:
                put(t, er, ei)
        for k, shift in enumerate((1, 2, 4)):
            s = (SUBLANES - shift) if reverse else shift
            er, ei = _cmul_add(er, ei, *tab(k), pltpu.roll(er, s, 0), pltpu.roll(ei, s, 0))
        er, ei = _cmul_add(er, ei, *tab(3), c_re, c_im)
        put(order[-1], er, ei)
        sub = lax.broadcasted_iota(jnp.int32, er.shape, 0)
        in_re = jnp.where(sub == near, c_re, pltpu.roll(er, s_in, 0))
        in_im = jnp.where(sub == near, c_im, pltpu.roll(ei, s_in, 0))
        true = {order[-1]: (er, ei)}
        for idx, t in enumerate(order[:-1]):
            true[t] = _cmul_add(*get(buf_ref, t), *tab(4 + idx), in_re, in_im)
            put(t, *true[t])
        carry = (jnp.broadcast_to(er[far:far + 1], er.shape), jnp.broadcast_to(ei[far:far + 1], ei.shape))
        if acc is None:
            return carry, None
        acc_re, acc_im = acc
        for t in range(SEG):
            if t + 1 < SEG:
                gr, gim = true[t + 1]
            else:
                gr = jnp.where(sub == SUBLANES - 1, c_re, pltpu.roll(true[0][0], SUBLANES - 1, 0))
                gim = jnp.where(sub == SUBLANES - 1, c_im, pltpu.roll(true[0][1], SUBLANES - 1, 0))
            hr, hi = get(h_ref, t)
            acc_re = acc_re + gr * hr + gim * hi
            acc_im = acc_im + gim * hr - gr * hi
        return carry, (acc_re, acc_im)

    half = SUPER_HALF // LANES
    per = 2 if h_ref is None else 4
    for sb in range(N_SUPER):
        pairs = [(2 * half * sb + j, 2 * half * sb + half + j) for j in range(half)]

        def step(wi, state, pairs=pairs):
            w = (n_windows - 1 - wi) if reverse else wi
            w0 = pl.multiple_of(w * WINDOW, WINDOW)
            out = []
            for j, (tile_re, tile_im) in enumerate(pairs):
                mine = state[per * j:per * (j + 1)]
                carry, acc = window(w0, tile_re, tile_im, mine[0], mine[1], mine[2:] or None)
                out += list(carry) + list(acc or ())
            return tuple(out)

        init = []
        for tile_re, tile_im in pairs:
            init += [carry_ref[:, lanes(tile_re)], carry_ref[:, lanes(tile_im)]]
            if h_ref is not None:
                init += [da_ref[:, lanes(tile_re)], da_ref[:, lanes(tile_im)]]
        fin = lax.fori_loop(0, n_windows, step, tuple(init))
        for j, (tile_re, tile_im) in enumerate(pairs):
            carry_ref[:, lanes(tile_re)] = fin[per * j]
            carry_ref[:, lanes(tile_im)] = fin[per * j + 1]
            if h_ref is not None:
                da_ref[:, lanes(tile_re)] = fin[per * j + 2]
                da_ref[:, lanes(tile_im)] = fin[per * j + 3]


def _put_tiles(ref, sb, block):
    for j in range(SUPER_TILES):
        ref[sb * SUPER_TILES + j] = block[:, j * LANES:(j + 1) * LANES]


def _get_tiles(ref, sb):
    return jnp.concatenate([ref[sb * SUPER_TILES + j] for j in range(SUPER_TILES)], axis=1)


def _ssm_fwd(u, bmat, cmat, tab, d_skip, tb, exchange=None):
    seq = u.shape[0]

    def body(u_ref, b_ref, c_ref, t_ref, d_ref, s_ref, h_ref, carry_ref):
        @pl.when(pl.program_id(0) == 0)
        def _():
            carry_ref[...] = jnp.zeros_like(carry_ref)

        u_blk = u_ref[...]
        ub = _bf(u_blk)
        for sb in range(N_SUPER):
            _put_tiles(h_ref, sb, _mm(ub[:, sb * SUPER_IN:(sb + 1) * SUPER_IN], b_ref[sb]))
        _scan_rows(h_ref, t_ref, carry_ref, tb // WINDOW, False)
        ys = [_mm(_bf(_get_tiles(h_ref, sb)), c_ref[sb]) for sb in range(N_SUPER)]
        s_ref[...] = jnp.concatenate(ys, axis=1) + d_ref[...] * u_blk

    return _rowcall("ssm_fwd", body, seq, tb, [u], [bmat, cmat, tab, d_skip],
                    [(SSM_W, F32), ((STATE_TILES, LANES), F32)], [],
                    scratch=[pltpu.VMEM((SUBLANES, STATE_COLS), F32)], vmem=VMEM_BIG, exchange=exchange)


def _ssm_bwd(ds, u, h, bmat_t, cmat_t, tab, d_skip, tb, exchange=None):
    seq = u.shape[0]

    def body(ds_ref, u_ref, h_ref, bt_ref, ct_ref, t_ref, d_ref,
             du_ref, db_ref, dc_ref, da_ref, dd_ref, g_ref, carry_ref):
        @pl.when(pl.program_id(0) == 0)
        def _():
            carry_ref[...] = jnp.zeros_like(carry_ref)
            db_ref[...] = jnp.zeros_like(db_ref)
            dc_ref[...] = jnp.zeros_like(dc_ref)
            da_ref[...] = jnp.zeros_like(da_ref)
            dd_ref[...] = jnp.zeros_like(dd_ref)

        ds_blk = ds_ref[...]
        dsb = _bf(ds_blk)
        u_blk = u_ref[...]
        ub = _bf(u_blk)
        for sb in range(N_SUPER):
            _put_tiles(g_ref, sb, _mm(dsb[:, sb * SUPER_IN:(sb + 1) * SUPER_IN], ct_ref[sb]))
        _scan_rows(g_ref, t_ref, carry_ref, tb // WINDOW, True, h_ref=h_ref, da_ref=da_ref)
        dus = []
        for sb in range(N_SUPER):
            gb = _bf(_get_tiles(g_ref, sb))
            dus.append(_mm(gb, bt_ref[sb]))
            db_ref[sb] += _mm_tn(ub[:, sb * SUPER_IN:(sb + 1) * SUPER_IN], gb)
            dc_ref[sb] += _mm_tn(_bf(_get_tiles(h_ref, sb)), dsb[:, sb * SUPER_IN:(sb + 1) * SUPER_IN])
        du_ref[...] = jnp.concatenate(dus, axis=1) + d_ref[...] * ds_blk
        dd_ref[...] += jnp.sum(ds_blk * u_blk, axis=0, keepdims=True)

    return _rowcall("ssm_bwd", body, seq, tb, [ds, u, h], [bmat_t, cmat_t, tab, d_skip],
                    [(SSM_W, F32)],
                    [((N_SUPER, SUPER_IN, SUPER_W), F32), ((N_SUPER, SUPER_W, SUPER_IN), F32),
                     ((SUBLANES, STATE_COLS), F32), ((1, SSM_W), F32)],
                    scratch=[pltpu.VMEM((STATE_TILES, tb, LANES), F32), pltpu.VMEM((SUBLANES, STATE_COLS), F32)],
                    reverse=True, vmem=VMEM_BIG, exchange=exchange)


def _merge_core(s, attb, ga, gs, wg_ref, wab_ref, wsb_ref, wout_ref):
    zg, dgelu = _gelu_and_grad(s)
    zgb = _bf(zg)
    sg = _sig(_mm(zgb, wg_ref[...]))
    z = zg * sg
    zb = _bf(z)
    ys = jnp.concatenate([_mm(zb, wsb_ref[j]) for j in range(N_CHIPS)], axis=1)
    ya = jnp.concatenate([_mm(attb, wab_ref[j]) for j in range(N_CHIPS)], axis=1)
    sa = _sig(ga)
    ss = _sig(gs)
    mgb = _bf(sa * ya + ss * ys)
    o = _mm(mgb, wout_ref[...])
    return dict(zg=zg, dgelu=dgelu, zgb=zgb, sg=sg, zb=zb, ys=ys, ya=ya, sa=sa, ss=ss, mgb=mgb, o=o)


def _merge_fwd(x, s, att, ga, gs, g2, w_glu, w_ab, w_sb, w_out, tb, exchange=None):
    seq = x.shape[0]

    def body(x_ref, s_ref, att_ref, ga_ref, gs_ref, g_ref, wg_ref, wab_ref, wsb_ref, wout_ref, x2_ref):
        f = _merge_core(s_ref[...], att_ref[...], ga_ref[...], gs_ref[...], wg_ref, wab_ref, wsb_ref, wout_ref)
        n, _, _ = _rms(f["o"], g_ref[...])
        x2_ref[...] = x_ref[...] + n

    return _rowcall("merge_fwd", body, seq, tb, [x, s, att, ga, gs], [g2, w_glu, w_ab, w_sb, w_out],
                    [(D_MODEL, F32)], [], vmem=VMEM_BIG, exchange=exchange)[0]


def _merge_bwd(dx2, s, att, ga, gs, g2, w_glu, w_ab, w_sb, w_out, tb, exchange=None):
    seq = s.shape[0]
    cw = D_MODEL // N_CHIPS
    last = seq // tb - 1

    def body(dx2_ref, s_ref, att_ref, ga_ref, gs_ref, g_ref, wg_ref, wab_ref, wsb_ref, wout_ref,
             ds_ref, datt_ref, dga_ref, dgs_ref, dg_ref, dwg_ref, dwab_ref, dwsb_ref, dwout_ref,
             bwg_ref, bwab_ref, bwsb_ref, bwout_ref):
        @pl.when(pl.program_id(0) == 0)
        def _():
            for r in (dg_ref, dwg_ref, dwab_ref, dwsb_ref, dwout_ref):
                r[...] = jnp.zeros_like(r)

        attb = att_ref[...]
        f = _merge_core(s_ref[...], attb, ga_ref[...], gs_ref[...], wg_ref, wab_ref, wsb_ref, wout_ref)
        g = g_ref[...]
        _, oh, r2 = _rms(f["o"], g)
        do, dg = _rms_bwd(dx2_ref[...], oh, r2, g)
        dg_ref[...] += dg
        dob = _bf(do)
        dwout_ref[...] += _mm_tn(f["mgb"], dob)
        dmg = _mm_nt(dob, wout_ref[...])
        sa, ss = f["sa"], f["ss"]
        dyab = _bf(dmg * sa)
        dysb = _bf(dmg * ss)
        dga_ref[...] = _bf(dmg * f["ya"] * sa * (1.0 - sa))
        dgs_ref[...] = _bf(dmg * f["ys"] * ss * (1.0 - ss))
        dwab = _mm_tn(attb, dyab)
        dwsb = _mm_tn(f["zb"], dysb)
        datt = jnp.zeros((tb, ATTN_W), F32)
        dz = jnp.zeros((tb, SSM_W), F32)
        for j in range(N_CHIPS):
            dwab_ref[j] += dwab[:, j * cw:(j + 1) * cw]
            dwsb_ref[j] += dwsb[:, j * cw:(j + 1) * cw]
            datt = datt + _mm_nt(dyab[:, j * cw:(j + 1) * cw], wab_ref[j])
            dz = dz + _mm_nt(dysb[:, j * cw:(j + 1) * cw], wsb_ref[j])
        datt_ref[...] = _bf(datt)
        sg, zg = f["sg"], f["zg"]
        dglb = _bf(dz * zg * sg * (1.0 - sg))
        dwg_ref[...] += _mm_tn(f["zgb"], dglb)
        dzg = dz * sg + _mm_nt(dglb, wg_ref[...])
        ds_ref[...] = dzg * f["dgelu"]

        @pl.when(pl.program_id(0) == last)
        def _():
            for dst, src in ((bwg_ref, dwg_ref), (bwab_ref, dwab_ref), (bwsb_ref, dwsb_ref), (bwout_ref, dwout_ref)):
                dst[...] = _bf(src[...])

    shapes = [w_glu.shape, w_ab.shape, w_sb.shape, w_out.shape]
    return _rowcall("merge_bwd", body, seq, tb, [dx2, s, att, ga, gs], [g2, w_glu, w_ab, w_sb, w_out],
                    [(SSM_W, F32), (ATTN_W, BF16), (D_MODEL, BF16), (D_MODEL, BF16)],
                    [((1, D_MODEL), F32)] + [(sh, F32) for sh in shapes] + [(sh, BF16) for sh in shapes],
                    vmem=VMEM_BIG, exchange=exchange)


def _mlp_fwd_loss(x2, target, g3, g4, w_ffi, w_ffo, tb):
    seq = x2.shape[0]
    n_slab = len(w_ffi)
    sw = D_FF // FF_CHUNKS // n_slab

    def body(x2_ref, t_ref, g3_ref, g4_ref, *rest):
        wi_refs, (wo_ref, dy_ref, df_ref, h_ref, ra_ref, loss_ref, dg_ref) = rest[:n_slab], rest[n_slab:]

        @pl.when(pl.program_id(0) == 0)
        def _():
            loss_ref[...] = jnp.zeros_like(loss_ref)
            dg_ref[...] = jnp.zeros_like(dg_ref)

        x2_blk = x2_ref[...]
        h3, _, _ = _rms(x2_blk, g3_ref[...])
        hb = _bf(h3)
        h_ref[...] = hb
        f = jnp.zeros((tb, D_MODEL), F32)
        for j in range(FF_CHUNKS):
            for k in range(n_slab):
                ra = jnp.maximum(_mm(hb, wi_refs[k][j]), 0.0)
                ra_ref[:, pl.ds((j * n_slab + k) * sw, sw)] = _bf(ra)
                f = f + _mm(_bf(ra * ra), wo_ref[j, pl.ds(k * sw, sw), :])
        g4 = g4_ref[...]
        n4, fh, r4 = _rms(f, g4)
        e = (x2_blk + n4) - t_ref[...]
        loss_ref[...] += 0.5 * jnp.sum(jnp.mean(e * e, axis=-1, keepdims=True))
        dy = e * (1.0 / D_MODEL)
        dy_ref[...] = dy
        df, dg = _rms_bwd(dy, fh, r4, g4)
        df_ref[...] = _bf(df)
        dg_ref[...] += dg

    return _rowcall("mlp_fwd_loss", body, seq, tb, [x2, target], [g3, g4, *w_ffi, w_ffo],
                    [(D_MODEL, F32), (D_MODEL, BF16), (D_MODEL, BF16), (D_FF, BF16)],
                    [((SUBLANES, 128), F32), ((1, D_MODEL), F32)], vmem=VMEM_BIG)


def _mlp_bwd(x2, dy, df, ra, g3, w_ffi, w_ffo, tb):
    seq = x2.shape[0]
    n_slab = len(w_ffi)
    sw = D_FF // FF_CHUNKS // n_slab

    def body(x2_ref, dy_ref, df_ref, ra_ref, g3_ref, *rest):
        wi_refs, (wo_ref, dx_ref, da_ref, dg_ref) = rest[:n_slab], rest[n_slab:]

        @pl.when(pl.program_id(0) == 0)
        def _():
            dg_ref[...] = jnp.zeros_like(dg_ref)

        dfb = df_ref[...]
        dh = jnp.zeros((tb, D_MODEL), F32)
        for j in range(FF_CHUNKS):
            for k in range(n_slab):
                cols = pl.ds((j * n_slab + k) * sw, sw)
                ra = ra_ref[:, cols].astype(F32)
                dab = _bf(_mm_nt(dfb, wo_ref[j, pl.ds(k * sw, sw), :]) * (2.0 * ra))
                da_ref[:, cols] = dab
                dh = dh + _mm_nt(dab, wi_refs[k][j])
        g3 = g3_ref[...]
        _, xh, r3 = _rms(x2_ref[...], g3)
        dxn, dg = _rms_bwd(dh, xh, r3, g3)
        dx_ref[...] = dy_ref[...] + dxn
        dg_ref[...] += dg

    return _rowcall("mlp_bwd", body, seq, tb, [x2, dy, df, ra], [g3, *w_ffi, w_ffo],
                    [(D_MODEL, F32), (D_FF, BF16)], [((1, D_MODEL), F32)], vmem=VMEM_BIG)


def _matmul_tn(name, a, b, tk, tn, tl, chunk_major, exchange=None, square_a=False):
    seq, kdim = a.shape
    ndim = b.shape[1]
    last = seq // tl - 1

    def body(a_ref, b_ref, o_ref, ob_ref):
        @pl.when(pl.program_id(2) == 0)
        def _():
            o_ref[...] = jnp.zeros_like(o_ref)

        a_blk = a_ref[...]
        if square_a:
            a_blk = _bf(jnp.square(a_blk.astype(F32)))
        o_ref[...] += _mm_tn(a_blk, b_ref[...])

        @pl.when(pl.program_id(2) == last)
        def _():
            ob_ref[...] = _bf(o_ref[...])

    if chunk_major:
        shape = (ndim // tn, kdim, tn)
        out_spec = pl.BlockSpec((None, tk, tn), lambda k, n, l: (n, k, 0))
    else:
        shape = (kdim, ndim)
        out_spec = pl.BlockSpec((tk, tn), lambda k, n, l: (k, n))
    return _fused_call(
        name, body, (kdim // tk, ndim // tn, seq // tl),
        [pl.BlockSpec((tl, tk), lambda k, n, l: (l, k)), pl.BlockSpec((tl, tn), lambda k, n, l: (l, n))],
        [out_spec, out_spec], [SDS(shape, F32), SDS(shape, BF16)], [], [a, b], exchange, _params(3, VMEM_BIG))


def _ew_call(name, fn, ins, n_out, after=None):
    rows, cols = ins[0].shape
    tr = rows
    while tr * cols * 4 > min(1 << 20, (9 << 20) // (len(ins) + n_out)) and tr % 16 == 0:
        tr //= 2
    spec = pl.BlockSpec((tr, cols), lambda i: (i, 0))
    extra = [] if after is None else [after]

    def body(*refs):
        outs = fn(*[r[...] for r in refs[:len(ins)]])
        for r, o in zip(refs[len(ins) + len(extra):], outs):
            r[...] = o

    return pl.pallas_call(
        body, grid=(rows // tr,), in_specs=[spec] * len(ins) + [ANY] * len(extra), out_specs=[spec] * n_out,
        out_shape=[SDS((rows, cols), F32)] * n_out, name=name, compiler_params=_params(1))(*ins, *extra)


def _adam_math(w, g, m, v):
    m2 = ADAM_B1 * m + (1.0 - ADAM_B1) * g
    v2 = ADAM_B2 * v + (1.0 - ADAM_B2) * (g * g)
    m_hat = m2 / (1.0 - ADAM_B1 ** ADAM_STEP)
    v_hat = v2 / (1.0 - ADAM_B2 ** ADAM_STEP)
    delta = -ADAM_LR * (m_hat / (jnp.sqrt(v_hat) + ADAM_EPS) + ADAM_WD * w)
    return delta, m2, v2


def _sum4(name, own, recv, idx):
    _, rows, cols = own.shape
    tr = rows
    while tr * cols * 4 > (1 << 20) and tr % 16 == 0:
        tr //= 2

    def body(idx_ref, o_ref, r0_ref, r1_ref, r2_ref, out_ref):
        out_ref[...] = ((o_ref[...] + r0_ref[...].astype(F32)) + r1_ref[...].astype(F32)) + r2_ref[...].astype(F32)

    blk = (None, tr, cols)
    grid_spec = pltpu.PrefetchScalarGridSpec(
        num_scalar_prefetch=1, grid=(rows // tr,),
        in_specs=[pl.BlockSpec(blk, lambda i, s: (s[0], i, 0)), pl.BlockSpec(blk, lambda i, s: (0, i, 0)),
                  pl.BlockSpec(blk, lambda i, s: (1, i, 0)), pl.BlockSpec(blk, lambda i, s: (2, i, 0))],
        out_specs=pl.BlockSpec((tr, cols), lambda i, s: (i, 0)))
    return pl.pallas_call(body, grid_spec=grid_spec, out_shape=SDS((rows, cols), F32), name=name,
                          compiler_params=_params(1))(jnp.reshape(idx, (1,)).astype(jnp.int32), own, recv, recv, recv)


def _adam_pair(name, item, after=None):
    def fn(w_, a, b, m_, v_):
        g = a + b
        return (g,) + _adam_math(w_, g, m_, v_)

    return _ew_call(name, fn, list(item), 4, after)


def _place():
    return lax.axis_index("x"), lax.axis_index("y"), lax.axis_index("c")


def _other_chips(x, y):
    return [(1 - x, y), (x, 1 - y), (1 - x, 1 - y)]


HBM = pl.BlockSpec(memory_space=pltpu.HBM)
SEM = pl.BlockSpec(memory_space=pltpu.SEMAPHORE)
DATAFLOW = pltpu.SideEffectType.DATAFLOW_SIDE_EFFECTING


class _Flight:
    def __init__(self, copies, n_copies, send, recv, srcs, lands, token):
        self.copies, self.n, self.send, self.recv = copies, n_copies, send, recv
        self.srcs, self.lands, self.token = list(srcs), list(lands), token


def _take_off(name, srcs, lands, copies, n_copies, after):
    n_s, n_l = len(srcs), len(lands)

    def body(*refs):
        src, land = refs[:n_s], refs[n_s:n_s + n_l]
        send, recv = refs[n_s + n_l + 1:n_s + n_l + 3]
        for cp in copies(src, land, send, recv):
            cp.start()
        refs[-1][...] = jnp.zeros_like(refs[-1])

    mem = lambda t: pltpu.HBM(t.shape, t.dtype)
    sems = pltpu.SemaphoreType.DMA((n_copies,))
    outs = pl.pallas_call(
        body, name=name,
        out_shape=(sems, sems, *map(mem, srcs), *map(mem, lands), SDS((SUBLANES, LANES), F32)),
        in_specs=[HBM] * (n_s + n_l) + [ANY],
        out_specs=(SEM, SEM, *[HBM] * (n_s + n_l), pl.BlockSpec(memory_space=pltpu.VMEM)),
        input_output_aliases={i: 2 + i for i in range(n_s + n_l)},
        compiler_params=pltpu.CompilerParams(has_side_effects=DATAFLOW),
    )(*[pltpu.with_memory_space_constraint(t, pltpu.HBM) for t in (*srcs, *lands)], after)
    return _Flight(copies, n_copies, outs[0], outs[1], outs[2:2 + n_s], outs[2 + n_s:2 + n_s + n_l], outs[-1])


def _land(name, flight, after):
    n_s, n_l = len(flight.srcs), len(flight.lands)

    def body(*refs):
        src, land = refs[:n_s], refs[n_s:n_s + n_l]
        send, recv = refs[n_s + n_l:n_s + n_l + 2]
        for cp in flight.copies(src, land, send, recv):
            cp.wait_send()
            cp.wait_recv()

    mem = lambda t: pltpu.HBM(t.shape, t.dtype)
    outs = pl.pallas_call(
        body, name=name, out_shape=(*map(mem, flight.srcs), *map(mem, flight.lands)),
        in_specs=[HBM] * (n_s + n_l) + [SEM, SEM, ANY], out_specs=tuple([HBM] * (n_s + n_l)),
        input_output_aliases={i: i for i in range(n_s + n_l)},
        compiler_params=pltpu.CompilerParams(has_side_effects=DATAFLOW),
    )(*flight.srcs, *flight.lands, flight.send, flight.recv, after)
    return list(outs[:n_s]), list(outs[n_s:])


def _empty_like(shapes_from, lead):
    return [lax.empty((lead,) + t.shape[1:], t.dtype) for t in shapes_from]


def _scatter_off(name, chunks, after):
    def copies(src, land, send, recv):
        x, y, c = _place()
        return [pltpu.make_async_remote_copy(
            src_ref=src[a].at[2 * px + py], dst_ref=land[a].at[k], send_sem=send.at[3 * a + k],
            recv_sem=recv.at[3 * a + k], device_id=(px, py, c), device_id_type=MESH_ID)
            for a in range(len(chunks)) for k, (px, py) in enumerate(_other_chips(x, y))]

    return _take_off(name, chunks, _empty_like(chunks, 3), copies, 3 * len(chunks), after)


def _swap_off(name, arrs, after):
    def copies(src, land, send, recv):
        x, y, c = _place()
        return [pltpu.make_async_remote_copy(
            src_ref=src[a], dst_ref=land[a], send_sem=send.at[a], recv_sem=recv.at[a],
            device_id=(x, y, 1 - c), device_id_type=MESH_ID) for a in range(len(arrs))]

    return _take_off(name, arrs, [lax.empty(t.shape, t.dtype) for t in arrs], copies, len(arrs), after)


def _devices_off(name, block, after):
    me = 4 * lax.axis_index("x") + 2 * lax.axis_index("y") + lax.axis_index("c")
    land = lax.dynamic_update_index_in_dim(lax.empty((N_DEV,) + block.shape, block.dtype), block, me, 0)

    def copies(src, land, send, recv):
        x, y, c = _place()
        mine = 4 * x + 2 * y + c
        return [pltpu.make_async_remote_copy(
            src_ref=src[0], dst_ref=land[0].at[mine], send_sem=send.at[k - 1], recv_sem=recv.at[k - 1],
            device_id=(x ^ (k >> 2), y ^ ((k >> 1) & 1), c ^ (k & 1)), device_id_type=MESH_ID)
            for k in range(1, N_DEV)]

    return _take_off(name, [block], [land], copies, N_DEV - 1, after)


def _half_rows(shape, c, other=False):
    half = shape[0] // 2
    return pl.ds(((1 - c) if other else c) * half, half)


def _gather_start(name, shards, lands, after):
    n = len(shards)

    def body(*refs):
        src, land, (send, recv) = refs[:n], refs[n:2 * n], refs[2 * n + 1:2 * n + 3]
        x, y, c = _place()
        me = 2 * x + y
        for a in range(n):
            mine = _half_rows(shards[a].shape, c)
            for j, (px, py) in enumerate(_other_chips(x, y)):
                pltpu.make_async_remote_copy(
                    src_ref=src[a].at[mine], dst_ref=land[a].at[me, mine], send_sem=send.at[3 * a + j],
                    recv_sem=recv.at[3 * a + j], device_id=(px, py, c), device_id_type=MESH_ID).start()
        token = refs[-1]
        token[...] = jnp.zeros_like(token)

    mem = lambda t: pltpu.HBM(t.shape, t.dtype)
    pair = pltpu.SemaphoreType.DMA((3 * n,))
    outs = pl.pallas_call(
        body, name=name,
        out_shape=(pair, pair, *map(mem, shards), *map(mem, lands), SDS((SUBLANES, LANES), F32)),
        in_specs=[HBM] * (2 * n) + [ANY],
        out_specs=(SEM, SEM, *[HBM] * (2 * n), pl.BlockSpec(memory_space=pltpu.VMEM)),
        input_output_aliases={i: 2 + i for i in range(2 * n)},
        compiler_params=pltpu.CompilerParams(has_side_effects=DATAFLOW),
    )(*[pltpu.with_memory_space_constraint(t, pltpu.HBM) for t in (*shards, *lands)], after)
    return outs[0], outs[1], list(outs[2:2 + n]), list(outs[2 + n:2 + 2 * n]), outs[-1]


def _gather_pass(name, send, recv, shards, lands, after, first=0):
    n = len(shards)

    def body(*refs):
        src, land, (send, recv, _) = refs[:n], refs[n:2 * n], refs[2 * n:2 * n + 3]
        fsend, frecv = refs[2 * n + 3], refs[2 * n + 4]
        x, y, c = _place()
        me = 2 * x + y
        for a in range(n):
            mine = _half_rows(shards[a].shape, c)
            for j, (px, py) in enumerate(_other_chips(x, y)):
                far = 2 * px + py
                ici = pltpu.make_async_remote_copy(
                    src_ref=src[a].at[mine], dst_ref=land[a].at[far, mine], send_sem=send.at[3 * (first + a) + j],
                    recv_sem=recv.at[3 * (first + a) + j], device_id=(px, py, c), device_id_type=MESH_ID)
                ici.wait_recv()
                ici.wait_send()
                pltpu.make_async_remote_copy(
                    src_ref=land[a].at[far, mine], dst_ref=land[a].at[far, mine], send_sem=fsend.at[3 * a + j],
                    recv_sem=frecv.at[3 * a + j], device_id=(x, y, 1 - c), device_id_type=MESH_ID).start()
        token = refs[-1]
        token[...] = jnp.zeros_like(token)

    mem = lambda t: pltpu.HBM(t.shape, t.dtype)
    pair = pltpu.SemaphoreType.DMA((3 * n,))
    outs = pl.pallas_call(
        body, name=name,
        out_shape=(pair, pair, *map(mem, lands), SDS((SUBLANES, LANES), F32)),
        in_specs=[HBM] * (2 * n) + [SEM, SEM, ANY],
        out_specs=(SEM, SEM, *[HBM] * n, pl.BlockSpec(memory_space=pltpu.VMEM)),
        input_output_aliases={n + i: 2 + i for i in range(n)},
        compiler_params=pltpu.CompilerParams(has_side_effects=DATAFLOW),
    )(*shards, *lands, send, recv, after)
    return outs[0], outs[1], list(outs[2:2 + n]), outs[-1]


def _gather_wait(name, fsend, frecv, lands, after):
    n = len(lands)

    def body(*refs):
        land, (fsend, frecv, _) = refs[:n], refs[n:n + 3]
        x, y, c = _place()
        for a in range(n):
            for j, (px, py) in enumerate(_other_chips(x, y)):
                far = 2 * px + py
                mine = _half_rows(lands[a].shape[1:], c)
                theirs = _half_rows(lands[a].shape[1:], c, other=True)
                pltpu.make_async_remote_copy(
                    src_ref=land[a].at[far, mine], dst_ref=land[a].at[far, mine], send_sem=fsend.at[3 * a + j],
                    recv_sem=frecv.at[3 * a + j], device_id=(x, y, 1 - c), device_id_type=MESH_ID).wait_send()
                pltpu.make_async_remote_copy(
                    src_ref=land[a].at[far, theirs], dst_ref=land[a].at[far, theirs], send_sem=fsend.at[3 * a + j],
                    recv_sem=frecv.at[3 * a + j], device_id=(x, y, 1 - c), device_id_type=MESH_ID).wait_recv()

    mem = lambda t: pltpu.HBM(t.shape, t.dtype)
    return list(pl.pallas_call(
        body, name=name, out_shape=tuple(map(mem, lands)), in_specs=[HBM] * n + [SEM, SEM, ANY],
        out_specs=tuple([HBM] * n), input_output_aliases={i: i for i in range(n)},
        compiler_params=pltpu.CompilerParams(has_side_effects=DATAFLOW),
    )(*lands, fsend, frecv, after))


def _after(token):
    return _Exchange([token], [], [], lambda *_: None, lambda *_: None)


def _swap_sibling(arrs):
    n = len(arrs)

    def copies(ins, outs, sems):
        send, recv = sems
        x, y, c = _place()
        return [pltpu.make_async_remote_copy(
            src_ref=ins[a], dst_ref=outs[a], send_sem=send.at[a], recv_sem=recv.at[a],
            device_id=(x, y, 1 - c), device_id_type=MESH_ID) for a in range(n)]

    def start(ins, outs, sems):
        for cp in copies(ins, outs, sems):
            cp.start()

    def wait(ins, outs, sems):
        cps = copies(ins, outs, sems)
        for cp in cps:
            cp.wait_recv()
        for cp in cps:
            cp.wait_send()

    return _Exchange(arrs, [SDS(s.shape, s.dtype) for s in arrs],
                     [pltpu.SemaphoreType.DMA((n,)), pltpu.SemaphoreType.DMA((n,))], start, wait)


def _sum_devices(slots):
    def body(s_ref, o_ref):
        acc = s_ref[0]
        for d in range(1, N_DEV):
            acc = acc + s_ref[d]
        o_ref[...] = acc

    return pl.pallas_call(
        body, in_specs=[pl.BlockSpec(memory_space=pltpu.VMEM)], out_specs=pl.BlockSpec(memory_space=pltpu.VMEM),
        out_shape=SDS(slots.shape[1:], F32), name="sum_small",
        compiler_params=pltpu.CompilerParams(vmem_limit_bytes=32 * 1024 * 1024))(slots)


def _adam_small(ws, gs, ms, vs):
    n = len(ws)

    def body(*refs):
        for i in range(n):
            w_ref, g_ref, m_ref, v_ref = (refs[k * n + i] for k in range(4))
            outs = _adam_math(w_ref[...], g_ref[...], m_ref[...], v_ref[...])
            for k in range(3):
                refs[(4 + k) * n + i][...] = outs[k]

    vmem = pl.BlockSpec(memory_space=pltpu.VMEM)
    return pl.pallas_call(
        body, in_specs=[vmem] * (4 * n), out_specs=[vmem] * (3 * n),
        out_shape=[SDS(w.shape, F32) for w in ws] * 3, name="adam_small",
        compiler_params=pltpu.CompilerParams(vmem_limit_bytes=32 * 1024 * 1024))(*ws, *gs, *ms, *vs)


def _local_step(x, target, small, big, tb, distributed):
    dist = distributed
    me = (2 * lax.axis_index("x") + lax.axis_index("y")) if dist else 0
    tb_ssm = min(tb, 256)
    bucket = jnp.asarray(_bucket_table())
    place_own = lambda t: lax.dynamic_update_index_in_dim(lax.empty((N_CHIPS,) + t.shape, t.dtype), t, me, 0)
    if dist:
        in_legs = _gather_start("gather_in_start", [big["w_in"]], [place_own(big["w_in"])], small["d_skip"])
        names = sorted(small)
        in_token, values = lax.optimization_barrier((in_legs[4], [small[n] for n in names]))
        small = dict(zip(names, values))
    g1, g2, g3, g4 = small["norm_mix_pre"], small["norm_mix_post"], small["norm_mlp_pre"], small["norm_mlp_post"]

    keys_first = lambda t: jnp.swapaxes(t, -1, -2)
    bias = _bias_table(small["rel_bias"], bucket)
    sink_rows = keys_first(_pair_layout(jnp.broadcast_to(small["sinks"].reshape(N_HEADS, 1, 1), (N_HEADS, BLOCK, 1))))
    disc_args = (small["lam_re"], small["lam_im"], small["log_dt"], small["b_re"], small["b_im"])
    (ab_re, ab_im, bb_re, bb_im), disc_vjp = jax.vjp(_ssm_discretize, *disc_args)
    tab_f, tab_b = _scan_tables(ab_re, ab_im)
    bmat = _bf(_b_matrix(bb_re, bb_im))
    cmat = _bf(_c_matrix(small["c_re"], small["c_im"]))
    bmat_t, cmat_t = bmat.transpose(0, 2, 1), cmat.transpose(0, 2, 1)
    d_skip = small["d_skip"]

    mix = ("w_glu", "w_attn_branch", "w_ssm_branch", "w_out")
    rest = [big[n] for n in mix + ("w_ff_in", "w_ff_out")]
    if dist:
        send, recv, src, lands, _ = in_legs
        tab_f, tab_b, bias, sink_rows, bmat, cmat, bmat_t, cmat_t = lax.optimization_barrier(
            (tab_f, tab_b, bias, sink_rows, bmat, cmat, bmat_t, cmat_t))
        corner = lambda t: t.reshape(-1, t.shape[-1])[:1, :LANES].astype(F32)
        prepared = sum(map(corner, [tab_b, bias, sink_rows, bmat, cmat]), in_token[:1])
        send, recv, lands, in_passed = _gather_pass("gather_in_pass", send, recv, src, lands, prepared)
        in_passed, rest = lax.optimization_barrier((in_passed, rest))
        in_passed, rest_lands = lax.optimization_barrier((in_passed, [place_own(t) for t in rest]))
        (g_in,) = _gather_wait("gather_in_wait", send, recv, lands, in_passed)
        w_in = g_in.reshape(IN_W, D_MODEL)
    else:
        w_in = big["w_in"]
    token = None
    n_mix = len(mix)
    if dist:
        send, recv, rest, lands, token = _gather_start("gather_rest_start", rest, rest_lands, in_passed)
    h1, q, k, v, u, ga, gs = _inproj_fwd(x, g1, w_in, tb, _after(token) if dist else None)
    s, h = _ssm_fwd(u, bmat, cmat, tab_f, d_skip, tb)
    if dist:
        fsend, frecv, mix_lands, token = _gather_pass("gather_mix_pass", send, recv, rest[:n_mix], lands[:n_mix], s)
    att = _attn_fwd(q, k, v, bias, sink_rows, _after(token) if dist else None)[0]
    if dist:
        w_mix = _gather_wait("gather_mix_wait", fsend, frecv, mix_lands, att)
        fsend, frecv, ff_lands, token = _gather_pass(
            "gather_ff_pass", send, recv, rest[n_mix:], lands[n_mix:], w_mix[0], n_mix)
        rest = w_mix + ff_lands
    w_glu, w_ab, w_sb, w_out = rest[:n_mix]
    w_glu = w_glu.reshape(SSM_W, SSM_W)
    w_out = w_out.reshape(D_MODEL, D_MODEL)
    x2 = _merge_fwd(x, s, att, ga, gs, g2, w_glu, w_ab, w_sb, w_out, tb, _after(token) if dist else None)
    if dist:
        rest[n_mix:] = _gather_wait("gather_ff_wait", fsend, frecv, ff_lands, x2)
    w_ffi, w_ffo = [rest[n_mix]], rest[n_mix + 1]
    dy, df, h3, ra, loss_acc, dg4 = _mlp_fwd_loss(x2, target, g3, g4, w_ffi, w_ffo, tb)

    dx2, da, dg3 = _mlp_bwd(x2, dy, df, ra, g3, w_ffi, w_ffo, tb)
    tl = min(2048, x.shape[0])
    chunked = (N_CHIPS, D_FF // N_CHIPS, D_MODEL)
    d_ffi, b_ffi = _matmul_tn("grad_w_ff_in", h3, da, D_MODEL, D_FF // FF_CHUNKS, tl, True)
    d_ffo, b_ffo = _matmul_tn("grad_w_ff_out", ra, df, D_FF // FF_CHUNKS, D_MODEL, tl, False, square_a=True)
    d_ffo, b_ffo = d_ffo.reshape(chunked), b_ffo.reshape(chunked)
    behind = lambda flight: _after(flight.token) if dist else None
    ff_fl = _scatter_off("scatter_ff_off", [b_ffi, b_ffo], d_ffo) if dist else None
    outs = _merge_bwd(dx2, s, att, ga, gs, g2, w_glu, w_ab, w_sb, w_out, tb_ssm, behind(ff_fl))
    ds, datt, dga, dgs, dg2, d_glu, d_ab, d_sb, d_out, b_glu, b_ab, b_sb, b_out = outs
    glu4, out4 = (N_CHIPS, SSM_W // N_CHIPS, SSM_W), (N_CHIPS, D_MODEL // N_CHIPS, D_MODEL)
    d_mix = [d_glu.reshape(glu4), d_ab, d_sb, d_out.reshape(out4)]
    b_mix = [b_glu.reshape(glu4), b_ab, b_sb, b_out.reshape(out4)]
    mix_fl = _scatter_off("scatter_mix_off", b_mix, d_mix[-1]) if dist else None
    du, d_bmat, d_cmat, da_acc, dd_skip = _ssm_bwd(
        ds, u, h, bmat_t, cmat_t, tab_b, d_skip, tb, behind(mix_fl))
    dq, dk, dv, dbias, dsink_rows = _attn_bwd(q, k, v, datt, bias, sink_rows)
    swap_fl = None
    if dist:
        r_ffi, r_ffo = _land("scatter_ff_land", ff_fl, dq)[1]
        p_ffi = _sum4("sum_w_ff_in", d_ffi, r_ffi, me)
        p_ffo = _sum4("sum_w_ff_out", d_ffo, r_ffo, me)
        swap_fl = _swap_off("swap_ff_off", [p_ffi, p_ffo], r_ffo)
    dx, dpj, dg1 = _inproj_bwd(x, dx2, dq, dk, dv, du, dga, dgs, g1, w_in, tb, behind(swap_fl))

    dab_re, dab_im = _state_unlayout(jnp.sum(da_acc, axis=0))
    dbb_re, dbb_im = _b_matrix_grad(d_bmat)
    d_lam_re, d_lam_im, d_log_dt, d_b_re, d_b_im = disc_vjp((dab_re, dab_im, dbb_re, dbb_im))
    d_c_re, d_c_im = _c_matrix_grad(d_cmat)
    d_rel = _bias_grad(dbias, bucket)
    d_sinks = jnp.sum(_pair_unlayout(keys_first(dsink_rows)), axis=(1, 2))
    small_grads = dict(
        norm_mix_pre=dg1, norm_mix_post=dg2, norm_mlp_pre=dg3, norm_mlp_post=dg4, rel_bias=d_rel, sinks=d_sinks,
        lam_re=d_lam_re, lam_im=d_lam_im, log_dt=d_log_dt, b_re=d_b_re, b_im=d_b_im, c_re=d_c_re, c_im=d_c_im,
        d_skip=dd_skip)
    small_fl = _devices_off("small_off", _pack(small_grads, loss_acc), swap_fl.token) if dist else None
    outs = _matmul_tn("grad_w_in", dpj, h1, IN_W // 2, D_MODEL, tl, False, behind(small_fl))
    in4 = (N_CHIPS, IN_W // N_CHIPS, D_MODEL)
    d_in, b_in = outs[0].reshape(in4), outs[1].reshape(in4)
    if not dist:
        return loss_acc, dx, small_grads, dict(zip(BIG, [d_in] + d_mix + [d_ffi, d_ffo]))
    in_fl = _scatter_off("scatter_w_in_off", [b_in], d_in)
    (p_ffi, p_ffo), (s_ffi, s_ffo) = _land("swap_ff_land", swap_fl, in_fl.token)
    r_mix = _land("scatter_mix_land", mix_fl, in_fl.token)[1]
    p_mix = [_sum4("sum_" + n, d, r, me) for n, d, r in zip(mix, d_mix, r_mix)]
    mix_swap = _swap_off("swap_mix_off", p_mix, in_fl.token)
    pending = dict(d_in=d_in, in_fl=in_fl, mix_swap=mix_swap, w_ff_in=(p_ffi, s_ffi), w_ff_out=(p_ffo, s_ffo), me=me)
    return loss_acc, dx, small_fl, pending


SMALL = ['norm_mix_pre', 'norm_mix_post', 'norm_mlp_pre', 'norm_mlp_post', 'rel_bias', 'sinks', 'lam_re', 'lam_im',
         'log_dt', 'b_re', 'b_im', 'c_re', 'c_im', 'd_skip']
BIG = ['w_in', 'w_glu', 'w_attn_branch', 'w_ssm_branch', 'w_out', 'w_ff_in', 'w_ff_out']
WEIGHTS = ['norm_mix_pre', 'norm_mix_post', 'norm_mlp_pre', 'norm_mlp_post', 'w_in', 'rel_bias', 'sinks', 'lam_re',
           'lam_im', 'log_dt', 'b_re', 'b_im', 'c_re', 'c_im', 'd_skip', 'w_glu', 'w_attn_branch', 'w_ssm_branch',
           'w_out', 'w_ff_in', 'w_ff_out']
PACK_COLS = 1024
PACK_ORDER = ['b_re', 'b_im', 'c_re', 'c_im', 'lam_re', 'lam_im', 'norm_mix_pre', 'norm_mix_post', 'norm_mlp_pre',
              'norm_mlp_post', 'rel_bias', 'sinks', 'log_dt', 'd_skip']


STATE_MINOR = ('b_re', 'b_im')
PACK_ROWS = 144
LOSS_ROW = 140


def _pack(named, loss_acc):
    parts = []
    for n in PACK_ORDER:
        a = jnp.swapaxes(named[n], -1, -2) if n in STATE_MINOR else named[n]
        flat = a.reshape(-1)
        rows = -(-flat.shape[0] // PACK_COLS)
        parts.append(jnp.pad(flat, (0, rows * PACK_COLS - flat.shape[0])).reshape(rows, PACK_COLS))
    assert sum(p.shape[0] for p in parts) == LOSS_ROW
    parts.append(jnp.pad(loss_acc[0:1], ((0, PACK_ROWS - LOSS_ROW - 1), (0, PACK_COLS - loss_acc.shape[1]))))
    return jnp.concatenate(parts, axis=0)


def _unpack(packed, shapes):
    out, at = {}, 0
    for n in PACK_ORDER:
        shape = shapes[n][:-2] + (shapes[n][-1], shapes[n][-2]) if n in STATE_MINOR else shapes[n]
        size = int(np.prod(shape))
        rows = -(-size // PACK_COLS)
        blk = packed[at:at + rows]
        out[n] = (blk.reshape(-1)[:size] if size % PACK_COLS else blk).reshape(shape)
        at += rows
    return out


def kernel(x, norm_mix_pre, norm_mix_post, norm_mlp_pre, norm_mlp_post, w_in, rel_bias, sinks, lam_re, lam_im, log_dt, b_re, b_im, c_re, c_im, d_skip, w_glu, w_attn_branch, w_ssm_branch, w_out, w_ff_in, w_ff_out, loss_target, m_norm_mix_pre, m_norm_mix_post, m_norm_mlp_pre, m_norm_mlp_post, m_w_in, m_rel_bias, m_sinks, m_lam_re, m_lam_im, m_log_dt, m_b_re, m_b_im, m_c_re, m_c_im, m_d_skip, m_w_glu, m_w_attn_branch, m_w_ssm_branch, m_w_out, m_w_ff_in, m_w_ff_out, v_norm_mix_pre, v_norm_mix_post, v_norm_mlp_pre, v_norm_mlp_post, v_w_in, v_rel_bias, v_sinks, v_lam_re, v_lam_im, v_log_dt, v_b_re, v_b_im, v_c_re, v_c_im, v_d_skip, v_w_glu, v_w_attn_branch, v_w_ssm_branch, v_w_out, v_w_ff_in, v_w_ff_out):
    env = dict(locals())
    w = {n: env[n] for n in WEIGHTS}
    m = {n: env["m_" + n] for n in WEIGHTS}
    v = {n: env["v_" + n] for n in WEIGHTS}
    seq = x.shape[1]
    tb = min(512, seq)

    small = {n: w[n] for n in ('norm_mix_pre', 'norm_mix_post', 'norm_mlp_pre', 'norm_mlp_post', 'rel_bias')}
    small.update({n: w[n][0] for n in ('sinks', 'lam_re', 'lam_im', 'log_dt', 'b_re', 'b_im', 'c_re', 'c_im')})
    small['d_skip'] = w['d_skip']
    shard = lambda t, n: t[n][0].T if n == 'w_in' else t[n][0]
    unshard = lambda a, n: (a.T if n == 'w_in' else a)[None]
    _, dx, small_fl, pending = _local_step(
        x[0], loss_target[0], small, {n: _bf(shard(w, n)) for n in BIG}, tb, True)

    grads, deltas, new_m, new_v = {}, {}, {}, {}

    def adam(n, partials, after=None):
        outs = _adam_pair("adam_" + n, (shard(w, n), *partials, shard(m, n), shard(v, n)), after)
        grads[n], deltas[n], new_m[n], new_v[n] = [unshard(a, n) for a in outs]
        return outs[3]

    mix = ("w_glu", "w_attn_branch", "w_ssm_branch", "w_out")
    in_fl = pending["in_fl"]
    last = pending["mix_swap"].token
    for n in ("w_ff_in", "w_ff_out"):
        last = adam(n, pending[n], last)
    for n, partials in zip(mix, zip(*_land("swap_mix_land", pending["mix_swap"], last))):
        last = adam(n, partials, last)

    small_g = _sum_devices(_land("small_land", small_fl, last)[1][0])
    loss = small_g[LOSS_ROW, 0]
    minor = lambda t, n: jnp.swapaxes(t, -1, -2) if n in STATE_MINOR else t
    g_small = _unpack(small_g, {n: w[n].shape for n in SMALL})
    outs = _adam_small([minor(w[n], n) for n in SMALL], [g_small[n] for n in SMALL],
                       [minor(m[n], n) for n in SMALL], [minor(v[n], n) for n in SMALL])
    grads.update({n: minor(g_small[n], n) for n in SMALL})
    for k, dst in enumerate((deltas, new_m, new_v)):
        dst.update({n: minor(a, n) for n, a in zip(SMALL, outs[k * len(SMALL):(k + 1) * len(SMALL)])})

    (r_in,) = _land("scatter_w_in_land", in_fl, outs[0])[1]
    p_in = _sum4("sum_w_in", pending["d_in"], r_in, pending["me"])
    (s_in,) = _exchange_alone("swap_w_in", _swap_sibling([p_in]))
    adam("w_in", (p_in, s_in))

    return (loss, dx[None], *[grads[n] for n in WEIGHTS], *[deltas[n] for n in WEIGHTS],
            *[new_m[n] for n in WEIGHTS], *[new_v[n] for n in WEIGHTS])
```

```python
import functools
import math

import numpy as np
import jax
import jax.numpy as jnp
from jax import lax
from jax.experimental import pallas as pl
from jax.experimental.pallas import tpu as pltpu

F32 = jnp.float32
BF16 = jnp.bfloat16

D_MODEL = 1024
N_HEADS = 8
N_KV = 2
Q_GROUP = 4
HEAD_DIM = 64
ATTN_W = 512
KV_W = 128
BLOCK = 128
N_BUCKETS = 32
MAX_DISTANCE = 128
NEG_INF = -1e30
SSM_W = 512
SSM_GROUP = 16
SSM_GROUPS = 32
SSM_STATE = 64
N_SUPER = 4
GROUPS_PER_SUPER = SSM_GROUPS // N_SUPER
SUPER_IN = GROUPS_PER_SUPER * SSM_GROUP
SUPER_HALF = GROUPS_PER_SUPER * SSM_STATE
SUPER_W = 2 * SUPER_HALF
STATE_COLS = N_SUPER * SUPER_W
D_FF = 4096
FF_CHUNKS = 4
IN_W = 3328
SPLITS = (0, 512, 640, 768, 1280, 2304, 3328)
RMS_EPS = 1e-6
N_CHIPS = 4
N_DEV = 8
SUBLANES = 8
LANES = 128
STATE_TILES = STATE_COLS // LANES
SUPER_TILES = SUPER_W // LANES

ADAM_LR = 0.001
ADAM_B1 = 0.9
ADAM_B2 = 0.999
ADAM_EPS = 1e-08
ADAM_WD = 0.01
ADAM_STEP = 10

VMEM_BIG = 56 * 1024 * 1024
SDS = jax.ShapeDtypeStruct
MESH_ID = pl.DeviceIdType.MESH
ANY = pl.BlockSpec(memory_space=pl.ANY)


def _bf(x):
    return x.astype(BF16)


def _mm(a, b):
    return jnp.dot(a, b, preferred_element_type=F32)


def _mm_nt(a, b):
    return lax.dot_general(a, b, (((1,), (1,)), ((), ())), preferred_element_type=F32)


def _mm_tn(a, b):
    return lax.dot_general(a, b, (((0,), (0,)), ((), ())), preferred_element_type=F32)


def _sig(x):
    return 1.0 / (1.0 + jnp.exp(-x))


def _rms(x, g):
    r = lax.rsqrt(jnp.mean(x * x, axis=-1, keepdims=True) + RMS_EPS)
    xh = x * r
    return xh * g, xh, r


def _rms_bwd(dout, xh, r, g):
    dg = jnp.sum(dout * xh, axis=0, keepdims=True)
    dxh = dout * g
    dx = r * (dxh - xh * jnp.mean(dxh * xh, axis=-1, keepdims=True))
    return dx, dg


_GELU_C = math.sqrt(2.0 / math.pi)


def _gelu_and_grad(x):
    x2 = x * x
    inner = _GELU_C * (x + 0.044715 * (x2 * x))
    t = jnp.tanh(inner)
    y = 0.5 * x * (1.0 + t)
    dy = 0.5 * (1.0 + t) + 0.5 * x * (1.0 - t * t) * (_GELU_C * (1.0 + 3.0 * 0.044715 * x2))
    return y, dy


def _zero_map(nd, *_):
    return (0,) * nd


def _params(n_axes, vmem=None):
    return pltpu.CompilerParams(dimension_semantics=("arbitrary",) * n_axes, vmem_limit_bytes=vmem)


class _Exchange:
    def __init__(self, ins, outs, sems, start, wait):
        self.ins, self.outs, self.sems, self.start, self.wait = list(ins), list(outs), list(sems), start, wait


def _fused_call(name, body, grid, in_specs, out_specs, out_shape, scratch, args, exchange, params):
    n_in, n_out, n_scr = len(in_specs), len(out_specs), len(scratch)
    if exchange is None:
        fn = body
    else:
        ex = exchange
        n_xi, n_xo = len(ex.ins), len(ex.outs)

        def fn(*refs):
            at = 0
            parts = []
            for n in (n_in, n_xi, n_out, n_xo, n_scr, len(ex.sems)):
                parts.append(refs[at:at + n])
                at += n
            ins, x_in, outs, x_out, scr, x_sem = parts
            ids = [pl.program_id(a) for a in range(len(grid))]
            first = functools.reduce(jnp.logical_and, [i == 0 for i in ids])
            last = functools.reduce(jnp.logical_and, [i == g - 1 for i, g in zip(ids, grid)])

            @pl.when(first)
            def _():
                ex.start(x_in, x_out, x_sem)

            body(*ins, *outs, *scr)

            @pl.when(last)
            def _():
                ex.wait(x_in, x_out, x_sem)

        in_specs = list(in_specs) + [ANY] * n_xi
        out_specs = list(out_specs) + [ANY] * n_xo
        out_shape = list(out_shape) + ex.outs
        scratch = list(scratch) + ex.sems
        args = list(args) + ex.ins
    return pl.pallas_call(fn, grid=grid, in_specs=in_specs, out_specs=out_specs, out_shape=out_shape,
                          scratch_shapes=list(scratch), name=name, compiler_params=params)(*args)


def _exchange_alone(name, ex):
    def body(*refs):
        n_xi, n_xo = len(ex.ins), len(ex.outs)
        x_in, x_out, x_sem = refs[:n_xi], refs[n_xi:n_xi + n_xo], refs[n_xi + n_xo:]
        ex.start(x_in, x_out, x_sem)
        ex.wait(x_in, x_out, x_sem)

    return pl.pallas_call(body, in_specs=[ANY] * len(ex.ins), out_specs=[ANY] * len(ex.outs), out_shape=ex.outs,
                          scratch_shapes=ex.sems, name=name)(*ex.ins)


def _rowcall(name, body, seq, tb, rows, consts, row_outs, acc_outs, scratch=(), reverse=False, vmem=None,
             exchange=None):
    nb = seq // tb
    rmap = (lambda i: (nb - 1 - i, 0)) if reverse else (lambda i: (i, 0))
    tmap = lambda i: (0,) + rmap(i)

    def row_spec(width):
        if isinstance(width, tuple):
            return pl.BlockSpec((width[0], tb, width[1]), tmap)
        return pl.BlockSpec((tb, width), rmap)

    def row_shape(width):
        return (width[0], seq, width[1]) if isinstance(width, tuple) else (seq, width)

    in_specs = [row_spec(a.shape[1] if a.ndim == 2 else (a.shape[0], a.shape[2])) for a in rows]
    in_specs += [pl.BlockSpec(a.shape, functools.partial(_zero_map, a.ndim), pipeline_mode=pl.Buffered(1))
                 for a in consts]
    out_specs = [row_spec(c) for c, _ in row_outs] + [ANY] * len(acc_outs)
    out_shape = [SDS(row_shape(c), dt) for c, dt in row_outs] + [SDS(s, dt) for s, dt in acc_outs]
    n_main = len(rows) + len(consts) + len(row_outs)
    n_acc = len(acc_outs)

    def fn(*refs):
        main, acc_hbm, rest = refs[:n_main], refs[n_main:n_main + n_acc], refs[n_main + n_acc:]
        acc_vmem, own = rest[:n_acc], rest[n_acc:]
        body(*main, *acc_vmem, *own)

        @pl.when(pl.program_id(0) == nb - 1)
        def _():
            for src, dst in zip(acc_vmem, acc_hbm):
                pltpu.sync_copy(src, dst)

    buffers = [pltpu.VMEM(s, dt) for s, dt in acc_outs] + list(scratch)
    return _fused_call(name, fn if acc_outs else body, (nb,), in_specs, out_specs, out_shape, buffers,
                       [*rows, *consts], exchange, _params(1, vmem))


def _inproj_fwd(x, g1, w_in, tb, exchange=None):
    seq = x.shape[0]

    def body(x_ref, g_ref, w_ref, h_ref, q_ref, k_ref, v_ref, u_ref, ga_ref, gs_ref):
        h, _, _ = _rms(x_ref[...], g_ref[...])
        hb = _bf(h)
        h_ref[...] = hb
        pj = _mm_nt(hb, w_ref[...])
        q_ref[...] = _bf(pj[:, SPLITS[0]:SPLITS[1]])
        k_ref[...] = _bf(pj[:, SPLITS[1]:SPLITS[2]])
        v_ref[...] = _bf(pj[:, SPLITS[2]:SPLITS[3]])
        u_ref[...] = pj[:, SPLITS[3]:SPLITS[4]]
        ga_ref[...] = pj[:, SPLITS[4]:SPLITS[5]]
        gs_ref[...] = pj[:, SPLITS[5]:SPLITS[6]]

    return _rowcall("inproj_fwd", body, seq, tb, [x], [g1, w_in],
                    [(D_MODEL, BF16), (ATTN_W, BF16), (KV_W, BF16), (KV_W, BF16), (SSM_W, F32),
                     (D_MODEL, F32), (D_MODEL, F32)], [], vmem=VMEM_BIG, exchange=exchange)


def _inproj_bwd(x, dx2, dq, dk, dv, du, dga, dgs, g1, w_in, tb, exchange=None):
    seq = x.shape[0]

    def body(x_ref, dx2_ref, dq_ref, dk_ref, dv_ref, du_ref, dga_ref, dgs_ref, g_ref, w_ref,
             dx_ref, dpj_ref, dg_ref):
        @pl.when(pl.program_id(0) == 0)
        def _():
            dg_ref[...] = jnp.zeros_like(dg_ref)

        dpj = jnp.concatenate([dq_ref[...], dk_ref[...], dv_ref[...], _bf(du_ref[...]),
                               dga_ref[...], dgs_ref[...]], axis=1)
        dpj_ref[...] = dpj
        dh = _mm(dpj, w_ref[...])
        g = g_ref[...]
        _, xh, r = _rms(x_ref[...], g)
        dxn, dg = _rms_bwd(dh, xh, r, g)
        dx_ref[...] = dx2_ref[...] + dxn
        dg_ref[...] += dg

    return _rowcall("inproj_bwd", body, seq, tb, [x, dx2, dq, dk, dv, du, dga, dgs], [g1, w_in],
                    [(D_MODEL, F32), (IN_W, BF16)], [((1, D_MODEL), F32)], vmem=VMEM_BIG, exchange=exchange)


def _bucket_table():
    qi = np.arange(BLOCK)[:, None]
    kj = np.arange(2 * BLOCK)[None, :]
    dist = qi + BLOCK - kj
    max_exact = N_BUCKETS // 2
    d = np.maximum(dist, 0)
    df = np.maximum(d, 1).astype(np.float32)
    large = max_exact + (np.log(df / np.float32(max_exact)) / np.float32(math.log(MAX_DISTANCE / max_exact))
                         * np.float32(N_BUCKETS - max_exact)).astype(np.int32)
    large = np.minimum(large, N_BUCKETS - 1)
    bucket = np.where(d < max_exact, d, large)
    valid = (dist >= 0) & (dist < BLOCK)
    return np.where(valid, bucket, -1).astype(np.int32)


def _bias_table(rel_bias, bucket):
    def body(rb_ref, bk_ref, o_ref):
        bk = bk_ref[...]
        has_prev = lax.broadcasted_iota(jnp.int32, bk.shape, 1) >= BLOCK
        for h in range(N_HEADS):
            kh, j, par = h // Q_GROUP, (h // 2) % 2, h % 2
            acc = jnp.full((BLOCK, 2 * BLOCK), NEG_INF, F32)
            for b in range(N_BUCKETS):
                acc = jnp.where(bk == b, rb_ref[b, h], acc)
            o_ref[0, kh, par, :, j * BLOCK:(j + 1) * BLOCK] = jnp.where(has_prev, acc, NEG_INF).T
            o_ref[1, kh, par, :, j * BLOCK:(j + 1) * BLOCK] = acc.T

    return pl.pallas_call(
        body, out_shape=SDS((2, N_KV, 2, 2 * BLOCK, 2 * BLOCK), F32),
        in_specs=[pl.BlockSpec(memory_space=pltpu.SMEM), pl.BlockSpec(memory_space=pltpu.VMEM)],
        out_specs=pl.BlockSpec(memory_space=pltpu.VMEM), name="bias_table",
    )(rel_bias, bucket)


def _bias_grad(dbias, bucket):
    def body(db_ref, bk_ref, o_ref):
        bk = bk_ref[...]
        for h in range(N_HEADS):
            kh, j, par = h // Q_GROUP, (h // 2) % 2, h % 2
            db = db_ref[kh, par, :, j * BLOCK:(j + 1) * BLOCK].T
            for b in range(N_BUCKETS):
                o_ref[b, h] = jnp.sum(jnp.where(bk == b, db, 0.0))

    return pl.pallas_call(
        body, out_shape=SDS((N_BUCKETS, N_HEADS), F32),
        in_specs=[pl.BlockSpec(memory_space=pltpu.VMEM), pl.BlockSpec(memory_space=pltpu.VMEM)],
        out_specs=pl.BlockSpec(memory_space=pltpu.SMEM), name="bias_grad",
    )(dbias, bucket)


TILE = 2 * HEAD_DIM


def _pair_layout(t):
    lead = t.shape[:-3]
    t = t.reshape(lead + (N_KV, 2, 2) + t.shape[-2:])
    nl = len(lead)
    t = jnp.transpose(t, tuple(range(nl)) + (nl, nl + 2, nl + 1, nl + 3, nl + 4))
    return t.reshape(lead + (N_KV, 2, 2 * BLOCK, t.shape[-1]))


def _pair_unlayout(t):
    t = t.reshape(N_KV, 2, 2, BLOCK, t.shape[-1]).transpose(0, 2, 1, 3, 4)
    return t.reshape(N_HEADS, BLOCK, t.shape[-1])


def _halves(t):
    tf = t.astype(F32)
    low = lax.broadcasted_iota(jnp.int32, tf.shape, 1) < HEAD_DIM
    swapped = pltpu.roll(tf, HEAD_DIM, 1)
    zero = jnp.zeros_like(tf)
    return ((_bf(jnp.where(low, tf, zero)), _bf(jnp.where(low, zero, swapped))),
            (_bf(jnp.where(low, swapped, zero)), _bf(jnp.where(low, zero, tf))))


def _fold_halves(even, odd):
    low = lax.broadcasted_iota(jnp.int32, even.shape, 1) < HEAD_DIM
    comb = jnp.where(low, even, odd)
    return comb + pltpu.roll(comb, HEAD_DIM, 1)


def _tile_rows(ref, kh):
    return jnp.concatenate([ref[:, (2 * kh) * TILE:(2 * kh + 1) * TILE],
                            ref[:, (2 * kh + 1) * TILE:(2 * kh + 2) * TILE]], axis=0)


def _halves_t(t):
    tt = t.astype(F32).T
    top = lax.broadcasted_iota(jnp.int32, tt.shape, 0) < HEAD_DIM
    swapped = jnp.concatenate([tt[HEAD_DIM:], tt[:HEAD_DIM]], axis=0)
    zero = jnp.zeros_like(tt)
    return ((_bf(jnp.where(top, tt, zero)), _bf(jnp.where(top, zero, swapped))),
            (_bf(jnp.where(top, swapped, zero)), _bf(jnp.where(top, zero, tt))))


def _attn_probs(km, qk, bias, sink):
    lg = _mm_nt(km, qk) * (HEAD_DIM ** -0.5) + bias
    m = jnp.maximum(jnp.max(lg, axis=0, keepdims=True), sink)
    p = jnp.exp(lg - m)
    es = jnp.exp(sink - m)
    inv = 1.0 / (jnp.sum(p, axis=0, keepdims=True) + es)
    return p * inv, es * inv


def _attn_fwd(q, k, v, bias, sink_rows, exchange=None):
    seq = q.shape[0]
    nblk = seq // BLOCK

    def body(q_ref, kp_ref, kc_ref, vp_ref, vc_ref, b_ref, s_ref, o_ref):
        which = jnp.minimum(pl.program_id(0), 1)
        kms = _halves(jnp.concatenate([kp_ref[...], kc_ref[...]], axis=0))
        vts = _halves_t(jnp.concatenate([vp_ref[...], vc_ref[...]], axis=0))
        for kh in range(N_KV):
            qk = _tile_rows(q_ref, kh)
            acc = jnp.zeros((TILE, 2 * BLOCK), F32)
            for par in range(2):
                pr, _ = _attn_probs(kms[kh][par], qk, b_ref[which, kh, par], s_ref[kh, par])
                acc = acc + _mm(vts[kh][par], _bf(pr))
            acc = acc.T
            o_ref[:, (2 * kh) * TILE:(2 * kh + 1) * TILE] = _bf(acc[:BLOCK])
            o_ref[:, (2 * kh + 1) * TILE:(2 * kh + 2) * TILE] = _bf(acc[BLOCK:])

    cur = lambda n: (n, 0)
    prev = lambda n: (jnp.maximum(n - 1, 0), 0)
    return _fused_call(
        "attn_fwd", body, (nblk,),
        [pl.BlockSpec((BLOCK, ATTN_W), cur),
         pl.BlockSpec((BLOCK, KV_W), prev), pl.BlockSpec((BLOCK, KV_W), cur),
         pl.BlockSpec((BLOCK, KV_W), prev), pl.BlockSpec((BLOCK, KV_W), cur),
         pl.BlockSpec(bias.shape, functools.partial(_zero_map, bias.ndim)),
         pl.BlockSpec(sink_rows.shape, functools.partial(_zero_map, sink_rows.ndim))],
        [pl.BlockSpec((BLOCK, ATTN_W), cur)], [SDS((seq, ATTN_W), BF16)], [],
        [q, k, k, v, v, bias, sink_rows], exchange, _params(1))


def _attn_bwd(q, k, v, d_out, bias, sink_rows, exchange=None):
    seq = q.shape[0]
    nblk = seq // BLOCK

    def body(q_ref, kp_ref, kc_ref, vp_ref, vc_ref, do_ref, b_ref, s_ref,
             dq_ref, dk_ref, dv_ref, db_ref, ds_ref, ck_ref, cv_ref):
        n = pl.program_id(0)

        @pl.when(n == 0)
        def _():
            db_ref[...] = jnp.zeros_like(db_ref)
            ds_ref[...] = jnp.zeros_like(ds_ref)
            ck_ref[...] = jnp.zeros_like(ck_ref)
            cv_ref[...] = jnp.zeros_like(cv_ref)

        @pl.when(n < nblk)
        def _():
            which = jnp.minimum(n, 1)
            scale = HEAD_DIM ** -0.5
            kcat = jnp.concatenate([kp_ref[...], kc_ref[...]], axis=0)
            kms = _halves(kcat)
            kts = _halves_t(kcat)
            vms = _halves(jnp.concatenate([vp_ref[...], vc_ref[...]], axis=0))
            dks, dvs = [], []
            for kh in range(N_KV):
                qk = _tile_rows(q_ref, kh)
                dok = _tile_rows(do_ref, kh)
                dq = jnp.zeros((TILE, 2 * BLOCK), F32)
                dkp, dvp = [], []
                for par in range(2):
                    pr, ps = _attn_probs(kms[kh][par], qk, b_ref[which, kh, par], s_ref[kh, par])
                    dp = _mm_nt(vms[kh][par], dok)
                    rs = jnp.sum(pr * dp, axis=0, keepdims=True)
                    dlg = pr * (dp - rs)
                    ds_ref[kh, par] += -ps * rs
                    db_ref[kh, par] += dlg
                    dlb = _bf(dlg)
                    dq = dq + _mm(kts[kh][par], dlb)
                    dkp.append(_mm(dlb, qk))
                    dvp.append(_mm(_bf(pr), dok))
                dq = _bf((dq * scale).T)
                dq_ref[:, (2 * kh) * TILE:(2 * kh + 1) * TILE] = dq[:BLOCK]
                dq_ref[:, (2 * kh + 1) * TILE:(2 * kh + 2) * TILE] = dq[BLOCK:]
                dks.append(_fold_halves(*dkp))
                dvs.append(_fold_halves(*dvp))
            low = lax.broadcasted_iota(jnp.int32, (2 * BLOCK, TILE), 1) < HEAD_DIM
            dkk = jnp.where(low, dks[0], dks[1]) * scale
            dvv = jnp.where(low, dvs[0], dvs[1])
            dk_ref[...] = _bf(ck_ref[...] + dkk[:BLOCK])
            ck_ref[...] = dkk[BLOCK:]
            dv_ref[...] = _bf(cv_ref[...] + dvv[:BLOCK])
            cv_ref[...] = dvv[BLOCK:]

        @pl.when(n == nblk)
        def _():
            dk_ref[...] = _bf(ck_ref[...])
            dv_ref[...] = _bf(cv_ref[...])

    cur = lambda n: (jnp.minimum(n, nblk - 1), 0)
    prev = lambda n: (jnp.maximum(jnp.minimum(n, nblk - 1) - 1, 0), 0)
    late = lambda n: (jnp.maximum(n - 1, 0), 0)
    kv_spec = lambda m: pl.BlockSpec((BLOCK, KV_W), m)
    acc_b = pl.BlockSpec(bias.shape[1:], functools.partial(_zero_map, bias.ndim - 1))
    acc_s = pl.BlockSpec(sink_rows.shape, functools.partial(_zero_map, sink_rows.ndim))
    return _fused_call(
        "attn_bwd", body, (nblk + 1,),
        [pl.BlockSpec((BLOCK, ATTN_W), cur), kv_spec(prev), kv_spec(cur), kv_spec(prev), kv_spec(cur),
         pl.BlockSpec((BLOCK, ATTN_W), cur),
         pl.BlockSpec(bias.shape, functools.partial(_zero_map, bias.ndim)), acc_s],
        [pl.BlockSpec((BLOCK, ATTN_W), cur), kv_spec(late), kv_spec(late), acc_b, acc_s],
        [SDS((seq, ATTN_W), BF16), SDS((seq, KV_W), BF16), SDS((seq, KV_W), BF16),
         SDS(bias.shape[1:], F32), SDS(sink_rows.shape, F32)],
        [pltpu.VMEM((BLOCK, KV_W), F32), pltpu.VMEM((BLOCK, KV_W), F32)],
        [q, k, k, v, v, d_out, bias, sink_rows], exchange, _params(1))


def _ssm_discretize(lam_re, lam_im, log_dt, b_re, b_im):
    dt = jnp.exp(log_dt)[:, None]
    mag = jnp.exp(lam_re * dt)
    ab_re = mag * jnp.cos(lam_im * dt)
    ab_im = mag * jnp.sin(lam_im * dt)
    nr = ab_re - 1.0
    den = lam_re * lam_re + lam_im * lam_im
    f_re = (nr * lam_re + ab_im * lam_im) / den
    f_im = (ab_im * lam_re - nr * lam_im) / den
    bb_re = f_re[..., None] * b_re - f_im[..., None] * b_im
    bb_im = f_re[..., None] * b_im + f_im[..., None] * b_re
    return ab_re, ab_im, bb_re, bb_im


def _state_layout(re, im):
    lead = re.shape[:-2]
    z = jnp.stack([re, im], axis=-3).reshape(lead + (2, N_SUPER, GROUPS_PER_SUPER, SSM_STATE))
    return jnp.moveaxis(z, -4, -3).reshape(lead + (STATE_COLS,))


def _state_unlayout(vec):
    z = vec.reshape(N_SUPER, 2, GROUPS_PER_SUPER, SSM_STATE).transpose(1, 0, 2, 3)
    z = z.reshape(2, SSM_GROUPS, SSM_STATE)
    return z[0], z[1]


SEG = 4
WINDOW = SEG * SUBLANES


def _scan_tables(ab_re, ab_im):
    pw = [None, (ab_re, ab_im)]
    for _ in range(2, WINDOW + 1):
        pr, pi_ = pw[-1]
        pw.append((pr * ab_re - pi_ * ab_im, pr * ab_im + pi_ * ab_re))
    fwd = np.zeros((7, SUBLANES), np.int64)
    bwd = np.zeros((7, SUBLANES), np.int64)
    for k, shift in enumerate((1, 2, 4)):
        fwd[k] = [SEG * shift if r >= shift else 0 for r in range(SUBLANES)]
        bwd[k] = [SEG * shift if r < SUBLANES - shift else 0 for r in range(SUBLANES)]
    fwd[3] = [SEG * (r + 1) for r in range(SUBLANES)]
    bwd[3] = [SEG * (SUBLANES - r) for r in range(SUBLANES)]
    for k in range(1, SEG):
        fwd[3 + k] = bwd[3 + k] = k
    used = sorted((set(fwd.ravel()) | set(bwd.ravel())) - {0})
    select = lambda which: np.stack([(which == p) for p in used], axis=-1).astype(np.float32)
    stacked = _state_layout(jnp.stack([pw[p][0] for p in used]), jnp.stack([pw[p][1] for p in used]))
    conj_sign = np.where((np.arange(STATE_COLS) // SUPER_HALF) % 2 == 1, -1.0, 1.0).astype(np.float32)
    pick = functools.partial(jnp.einsum, 'krp,pc->krc', precision=lax.Precision.HIGHEST)
    return pick(select(fwd), stacked), pick(select(bwd), stacked) * conj_sign


_EYE = np.eye(GROUPS_PER_SUPER, dtype=np.float32)


def _b_matrix(bb_re, bb_im):
    bb = jnp.stack([bb_re, bb_im]).reshape(2, N_SUPER, GROUPS_PER_SUPER, SSM_STATE, SSM_GROUP)
    m = jnp.einsum('rsgpc,gh->sgcrhp', bb, _EYE)
    return m.reshape(N_SUPER, SUPER_IN, SUPER_W)


def _b_matrix_grad(dm):
    d = dm.reshape(N_SUPER, GROUPS_PER_SUPER, SSM_GROUP, 2, GROUPS_PER_SUPER, SSM_STATE)
    d = jnp.sum(d * _EYE[None, :, None, None, :, None], axis=4)
    d = d.transpose(3, 0, 1, 4, 2).reshape(2, SSM_GROUPS, SSM_STATE, SSM_GROUP)
    return d[0], d[1]


def _c_matrix(c_re, c_im):
    cc = jnp.stack([c_re, -c_im]).reshape(2, N_SUPER, GROUPS_PER_SUPER, SSM_GROUP, SSM_STATE)
    m = jnp.einsum('rsgcp,gh->srgphc', cc, _EYE)
    return m.reshape(N_SUPER, SUPER_W, SUPER_IN)


def _c_matrix_grad(dm):
    d = dm.reshape(N_SUPER, 2, GROUPS_PER_SUPER, SSM_STATE, GROUPS_PER_SUPER, SSM_GROUP)
    d = jnp.sum(d * _EYE[None, None, :, None, :, None], axis=4)
    d = d.transpose(1, 0, 2, 4, 3).reshape(2, SSM_GROUPS, SSM_GROUP, SSM_STATE)
    return d[0], -d[1]


def _cmul_add(xr, xi, ar, ai, sr, si):
    return xr + ar * sr - ai * si, xi + ar * si + ai * sr


def _scan_rows(buf_ref, tab_ref, carry_ref, n_windows, reverse, h_ref=None, da_ref=None):
    order = list(range(SEG - 1, -1, -1)) if reverse else list(range(SEG))
    near = SUBLANES - 1 if reverse else 0
    far = 0 if reverse else SUBLANES - 1
    s_in = SUBLANES - 1 if reverse else 1
    lanes = lambda tile: pl.ds(tile * LANES, LANES)

    def window(w0, tile_re, tile_im, c_re, c_im, acc):
        rows = lambda t: pl.ds(w0 + t, SUBLANES, stride=SEG)
        get = lambda ref, t: (ref.at[tile_re][rows(t), :], ref.at[tile_im][rows(t), :])
        tab = lambda k: (tab_ref[k, :, lanes(tile_re)], tab_ref[k, :, lanes(tile_im)])

        def put(t, xr, xi):
            buf_ref.at[tile_re][rows(t), :] = xr
            buf_ref.at[tile_im][rows(t), :] = xi

        a1 = tab(4)
        er, ei = get(buf_ref, order[0])
        for t in order[1:]:
            er, ei = _cmul_add(*get(buf_ref, t), *a1, er, ei)
            if t != order[-1]:
                put(t, er, ei)
        for k, shift in enumerate((1, 2, 4)):
            s = (SUBLANES - shift) if reverse else shift
            er, ei = _cmul_add(er, ei, *tab(k), pltpu.roll(er, s, 0), pltpu.roll(ei, s, 0))
        er, ei = _cmul_add(er, ei, *tab(3), c_re, c_im)
        put(order[-1], er, ei)
        sub = lax.broadcasted_iota(jnp.int32, er.shape, 0)
        in_re = jnp.where(sub == near, c_re, pltpu.roll(er, s_in, 0))
        in_im = jnp.where(sub == near, c_im, pltpu.roll(ei, s_in, 0))
        true = {order[-1]: (er, ei)}
        for idx, t in enumerate(order[:-1]):
            true[t] = _cmul_add(*get(buf_ref, t), *tab(4 + idx), in_re, in_im)
            put(t, *true[t])
        carry = (jnp.broadcast_to(er[far:far + 1], er.shape), jnp.broadcast_to(ei[far:far + 1], ei.shape))
        if acc is None:
            return carry, None
        acc_re, acc_im = acc
        for t in range(SEG):
            if t + 1 < SEG:
                gr, gim = true[t + 1]
            else:
                gr = jnp.where(sub == SUBLANES - 1, c_re, pltpu.roll(true[0][0], SUBLANES - 1, 0))
                gim = jnp.where(sub == SUBLANES - 1, c_im, pltpu.roll(true[0][1], SUBLANES - 1, 0))
            hr, hi = get(h_ref, t)
            acc_re = acc_re + gr * hr + gim * hi
            acc_im = acc_im + gim * hr - gr * hi
        return carry, (acc_re, acc_im)

    half = SUPER_HALF // LANES
    per = 2 if h_ref is None else 4
    for sb in range(N_SUPER):
        pairs = [(2 * half * sb + j, 2 * half * sb + half + j) for j in range(half)]

        def step(wi, state, pairs=pairs):
            w = (n_windows - 1 - wi) if reverse else wi
            w0 = pl.multiple_of(w * WINDOW, WINDOW)
            out = []
            for j, (tile_re, tile_im) in enumerate(pairs):
                mine = state[per * j:per * (j + 1)]
                carry, acc = window(w0, tile_re, tile_im, mine[0], mine[1], mine[2:] or None)
                out += list(carry) + list(acc or ())
            return tuple(out)

        init = []
        for tile_re, tile_im in pairs:
            init += [carry_ref[:, lanes(tile_re)], carry_ref[:, lanes(tile_im)]]
            if h_ref is not None:
                init += [da_ref[:, lanes(tile_re)], da_ref[:, lanes(tile_im)]]
        fin = lax.fori_loop(0, n_windows, step, tuple(init))
        for j, (tile_re, tile_im) in enumerate(pairs):
            carry_ref[:, lanes(tile_re)] = fin[per * j]
            carry_ref[:, lanes(tile_im)] = fin[per * j + 1]
            if h_ref is not None:
                da_ref[:, lanes(tile_re)] = fin[per * j + 2]
                da_ref[:, lanes(tile_im)] = fin[per * j + 3]


def _put_tiles(ref, sb, block):
    for j in range(SUPER_TILES):
        ref[sb * SUPER_TILES + j] = block[:, j * LANES:(j + 1) * LANES]


def _get_tiles(ref, sb):
    return jnp.concatenate([ref[sb * SUPER_TILES + j] for j in range(SUPER_TILES)], axis=1)


def _ssm_fwd(u, bmat, cmat, tab, d_skip, tb, exchange=None):
    seq = u.shape[0]

    def body(u_ref, b_ref, c_ref, t_ref, d_ref, s_ref, h_ref, carry_ref):
        @pl.when(pl.program_id(0) == 0)
        def _():
            carry_ref[...] = jnp.zeros_like(carry_ref)

        u_blk = u_ref[...]
        ub = _bf(u_blk)
        for sb in range(N_SUPER):
            _put_tiles(h_ref, sb, _mm(ub[:, sb * SUPER_IN:(sb + 1) * SUPER_IN], b_ref[sb]))
        _scan_rows(h_ref, t_ref, carry_ref, tb // WINDOW, False)
        ys = [_mm(_bf(_get_tiles(h_ref, sb)), c_ref[sb]) for sb in range(N_SUPER)]
        s_ref[...] = jnp.concatenate(ys, axis=1) + d_ref[...] * u_blk

    return _rowcall("ssm_fwd", body, seq, tb, [u], [bmat, cmat, tab, d_skip],
                    [(SSM_W, F32), ((STATE_TILES, LANES), F32)], [],
                    scratch=[pltpu.VMEM((SUBLANES, STATE_COLS), F32)], vmem=VMEM_BIG, exchange=exchange)


def _ssm_bwd(ds, u, h, bmat_t, cmat_t, tab, d_skip, tb, exchange=None):
    seq = u.shape[0]

    def body(ds_ref, u_ref, h_ref, bt_ref, ct_ref, t_ref, d_ref,
             du_ref, db_ref, dc_ref, da_ref, dd_ref, g_ref, carry_ref):
        @pl.when(pl.program_id(0) == 0)
        def _():
            carry_ref[...] = jnp.zeros_like(carry_ref)
            db_ref[...] = jnp.zeros_like(db_ref)
            dc_ref[...] = jnp.zeros_like(dc_ref)
            da_ref[...] = jnp.zeros_like(da_ref)
            dd_ref[...] = jnp.zeros_like(dd_ref)

        ds_blk = ds_ref[...]
        dsb = _bf(ds_blk)
        u_blk = u_ref[...]
        ub = _bf(u_blk)
        for sb in range(N_SUPER):
            _put_tiles(g_ref, sb, _mm(dsb[:, sb * SUPER_IN:(sb + 1) * SUPER_IN], ct_ref[sb]))
        _scan_rows(g_ref, t_ref, carry_ref, tb // WINDOW, True, h_ref=h_ref, da_ref=da_ref)
        dus = []
        for sb in range(N_SUPER):
            gb = _bf(_get_tiles(g_ref, sb))
            dus.append(_mm(gb, bt_ref[sb]))
            db_ref[sb] += _mm_tn(ub[:, sb * SUPER_IN:(sb + 1) * SUPER_IN], gb)
            dc_ref[sb] += _mm_tn(_bf(_get_tiles(h_ref, sb)), dsb[:, sb * SUPER_IN:(sb + 1) * SUPER_IN])
        du_ref[...] = jnp.concatenate(dus, axis=1) + d_ref[...] * ds_blk
        dd_ref[...] += jnp.sum(ds_blk * u_blk, axis=0, keepdims=True)

    return _rowcall("ssm_bwd", body, seq, tb, [ds, u, h], [bmat_t, cmat_t, tab, d_skip],
                    [(SSM_W, F32)],
                    [((N_SUPER, SUPER_IN, SUPER_W), F32), ((N_SUPER, SUPER_W, SUPER_IN), F32),
                     ((SUBLANES, STATE_COLS), F32), ((1, SSM_W), F32)],
                    scratch=[pltpu.VMEM((STATE_TILES, tb, LANES), F32), pltpu.VMEM((SUBLANES, STATE_COLS), F32)],
                    reverse=True, vmem=VMEM_BIG, exchange=exchange)


def _merge_core(s, attb, ga, gs, wg_ref, wab_ref, wsb_ref, wout_ref):
    zg, dgelu = _gelu_and_grad(s)
    zgb = _bf(zg)
    sg = _sig(_mm(zgb, wg_ref[...]))
    z = zg * sg
    zb = _bf(z)
    ys = jnp.concatenate([_mm(zb, wsb_ref[j]) for j in range(N_CHIPS)], axis=1)
    ya = jnp.concatenate([_mm(attb, wab_ref[j]) for j in range(N_CHIPS)], axis=1)
    sa = _sig(ga)
    ss = _sig(gs)
    mgb = _bf(sa * ya + ss * ys)
    o = _mm(mgb, wout_ref[...])
    return dict(zg=zg, dgelu=dgelu, zgb=zgb, sg=sg, zb=zb, ys=ys, ya=ya, sa=sa, ss=ss, mgb=mgb, o=o)


def _merge_fwd(x, s, att, ga, gs, g2, w_glu, w_ab, w_sb, w_out, tb, exchange=None):
    seq = x.shape[0]

    def body(x_ref, s_ref, att_ref, ga_ref, gs_ref, g_ref, wg_ref, wab_ref, wsb_ref, wout_ref, x2_ref):
        f = _merge_core(s_ref[...], att_ref[...], ga_ref[...], gs_ref[...], wg_ref, wab_ref, wsb_ref, wout_ref)
        n, _, _ = _rms(f["o"], g_ref[...])
        x2_ref[...] = x_ref[...] + n

    return _rowcall("merge_fwd", body, seq, tb, [x, s, att, ga, gs], [g2, w_glu, w_ab, w_sb, w_out],
                    [(D_MODEL, F32)], [], vmem=VMEM_BIG, exchange=exchange)[0]


def _merge_bwd(dx2, s, att, ga, gs, g2, w_glu, w_ab, w_sb, w_out, tb, exchange=None):
    seq = s.shape[0]
    cw = D_MODEL // N_CHIPS
    last = seq // tb - 1

    def body(dx2_ref, s_ref, att_ref, ga_ref, gs_ref, g_ref, wg_ref, wab_ref, wsb_ref, wout_ref,
             ds_ref, datt_ref, dga_ref, dgs_ref, dg_ref, dwg_ref, dwab_ref, dwsb_ref, dwout_ref,
             bwg_ref, bwab_ref, bwsb_ref, bwout_ref):
        @pl.when(pl.program_id(0) == 0)
        def _():
            for r in (dg_ref, dwg_ref, dwab_ref, dwsb_ref, dwout_ref):
                r[...] = jnp.zeros_like(r)

        attb = att_ref[...]
        f = _merge_core(s_ref[...], attb, ga_ref[...], gs_ref[...], wg_ref, wab_ref, wsb_ref, wout_ref)
        g = g_ref[...]
        _, oh, r2 = _rms(f["o"], g)
        do, dg = _rms_bwd(dx2_ref[...], oh, r2, g)
        dg_ref[...] += dg
        dob = _bf(do)
        dwout_ref[...] += _mm_tn(f["mgb"], dob)
        dmg = _mm_nt(dob, wout_ref[...])
        sa, ss = f["sa"], f["ss"]
        dyab = _bf(dmg * sa)
        dysb = _bf(dmg * ss)
        dga_ref[...] = _bf(dmg * f["ya"] * sa * (1.0 - sa))
        dgs_ref[...] = _bf(dmg * f["ys"] * ss * (1.0 - ss))
        dwab = _mm_tn(attb, dyab)
        dwsb = _mm_tn(f["zb"], dysb)
        datt = jnp.zeros((tb, ATTN_W), F32)
        dz = jnp.zeros((tb, SSM_W), F32)
        for j in range(N_CHIPS):
            dwab_ref[j] += dwab[:, j * cw:(j + 1) * cw]
            dwsb_ref[j] += dwsb[:, j * cw:(j + 1) * cw]
            datt = datt + _mm_nt(dyab[:, j * cw:(j + 1) * cw], wab_ref[j])
            dz = dz + _mm_nt(dysb[:, j * cw:(j + 1) * cw], wsb_ref[j])
        datt_ref[...] = _bf(datt)
        sg, zg = f["sg"], f["zg"]
        dglb = _bf(dz * zg * sg * (1.0 - sg))
        dwg_ref[...] += _mm_tn(f["zgb"], dglb)
        dzg = dz * sg + _mm_nt(dglb, wg_ref[...])
        ds_ref[...] = dzg * f["dgelu"]

        @pl.when(pl.program_id(0) == last)
        def _():
            for dst, src in ((bwg_ref, dwg_ref), (bwab_ref, dwab_ref), (bwsb_ref, dwsb_ref), (bwout_ref, dwout_ref)):
                dst[...] = _bf(src[...])

    shapes = [w_glu.shape, w_ab.shape, w_sb.shape, w_out.shape]
    return _rowcall("merge_bwd", body, seq, tb, [dx2, s, att, ga, gs], [g2, w_glu, w_ab, w_sb, w_out],
                    [(SSM_W, F32), (ATTN_W, BF16), (D_MODEL, BF16), (D_MODEL, BF16)],
                    [((1, D_MODEL), F32)] + [(sh, F32) for sh in shapes] + [(sh, BF16) for sh in shapes],
                    vmem=VMEM_BIG, exchange=exchange)


def _mlp_fwd_loss(x2, target, g3, g4, w_ffi, w_ffo, tb):
    seq = x2.shape[0]
    n_slab = len(w_ffi)
    sw = D_FF // FF_CHUNKS // n_slab

    def body(x2_ref, t_ref, g3_ref, g4_ref, *rest):
        wi_refs, (wo_ref, dy_ref, df_ref, h_ref, ra_ref, loss_ref, dg_ref) = rest[:n_slab], rest[n_slab:]

        @pl.when(pl.program_id(0) == 0)
        def _():
            loss_ref[...] = jnp.zeros_like(loss_ref)
            dg_ref[...] = jnp.zeros_like(dg_ref)

        x2_blk = x2_ref[...]
        h3, _, _ = _rms(x2_blk, g3_ref[...])
        hb = _bf(h3)
        h_ref[...] = hb
        f = jnp.zeros((tb, D_MODEL), F32)
        for j in range(FF_CHUNKS):
            for k in range(n_slab):
                ra = jnp.maximum(_mm(hb, wi_refs[k][j]), 0.0)
                ra_ref[:, pl.ds((j * n_slab + k) * sw, sw)] = _bf(ra)
                f = f + _mm(_bf(ra * ra), wo_ref[j, pl.ds(k * sw, sw), :])
        g4 = g4_ref[...]
        n4, fh, r4 = _rms(f, g4)
        e = (x2_blk + n4) - t_ref[...]
        loss_ref[...] += 0.5 * jnp.sum(jnp.mean(e * e, axis=-1, keepdims=True))
        dy = e * (1.0 / D_MODEL)
        dy_ref[...] = dy
        df, dg = _rms_bwd(dy, fh, r4, g4)
        df_ref[...] = _bf(df)
        dg_ref[...] += dg

    return _rowcall("mlp_fwd_loss", body, seq, tb, [x2, target], [g3, g4, *w_ffi, w_ffo],
                    [(D_MODEL, F32), (D_MODEL, BF16), (D_MODEL, BF16), (D_FF, BF16)],
                    [((SUBLANES, 128), F32), ((1, D_MODEL), F32)], vmem=VMEM_BIG)


def _mlp_bwd(x2, dy, df, ra, g3, w_ffi, w_ffo, tb):
    seq = x2.shape[0]
    n_slab = len(w_ffi)
    sw = D_FF // FF_CHUNKS // n_slab

    def body(x2_ref, dy_ref, df_ref, ra_ref, g3_ref, *rest):
        wi_refs, (wo_ref, dx_ref, da_ref, dg_ref) = rest[:n_slab], rest[n_slab:]

        @pl.when(pl.program_id(0) == 0)
        def _():
            dg_ref[...] = jnp.zeros_like(dg_ref)

        dfb = df_ref[...]
        dh = jnp.zeros((tb, D_MODEL), F32)
        for j in range(FF_CHUNKS):
            for k in range(n_slab):
                cols = pl.ds((j * n_slab + k) * sw, sw)
                ra = ra_ref[:, cols].astype(F32)
                dab = _bf(_mm_nt(dfb, wo_ref[j, pl.ds(k * sw, sw), :]) * (2.0 * ra))
                da_ref[:, cols] = dab
                dh = dh + _mm_nt(dab, wi_refs[k][j])
        g3 = g3_ref[...]
        _, xh, r3 = _rms(x2_ref[...], g3)
        dxn, dg = _rms_bwd(dh, xh, r3, g3)
        dx_ref[...] = dy_ref[...] + dxn
        dg_ref[...] += dg

    return _rowcall("mlp_bwd", body, seq, tb, [x2, dy, df, ra], [g3, *w_ffi, w_ffo],
                    [(D_MODEL, F32), (D_FF, BF16)], [((1, D_MODEL), F32)], vmem=VMEM_BIG)


def _matmul_tn(name, a, b, tk, tn, tl, chunk_major, exchange=None, square_a=False):
    seq, kdim = a.shape
    ndim = b.shape[1]
    last = seq // tl - 1

    def body(a_ref, b_ref, o_ref, ob_ref):
        @pl.when(pl.program_id(2) == 0)
        def _():
            o_ref[...] = jnp.zeros_like(o_ref)

        a_blk = a_ref[...]
        if square_a:
            a_blk = _bf(jnp.square(a_blk.astype(F32)))
        o_ref[...] += _mm_tn(a_blk, b_ref[...])

        @pl.when(pl.program_id(2) == last)
        def _():
            ob_ref[...] = _bf(o_ref[...])

    if chunk_major:
        shape = (ndim // tn, kdim, tn)
        out_spec = pl.BlockSpec((None, tk, tn), lambda k, n, l: (n, k, 0))
    else:
        shape = (kdim, ndim)
        out_spec = pl.BlockSpec((tk, tn), lambda k, n, l: (k, n))
    return _fused_call(
        name, body, (kdim // tk, ndim // tn, seq // tl),
        [pl.BlockSpec((tl, tk), lambda k, n, l: (l, k)), pl.BlockSpec((tl, tn), lambda k, n, l: (l, n))],
        [out_spec, out_spec], [SDS(shape, F32), SDS(shape, BF16)], [], [a, b], exchange, _params(3, VMEM_BIG))


def _ew_call(name, fn, ins, n_out, after=None):
    rows, cols = ins[0].shape
    tr = rows
    while tr * cols * 4 > min(1 << 20, (9 << 20) // (len(ins) + n_out)) and tr % 16 == 0:
        tr //= 2
    spec = pl.BlockSpec((tr, cols), lambda i: (i, 0))
    extra = [] if after is None else [after]

    def body(*refs):
        outs = fn(*[r[...] for r in refs[:len(ins)]])
        for r, o in zip(refs[len(ins) + len(extra):], outs):
            r[...] = o

    return pl.pallas_call(
        body, grid=(rows // tr,), in_specs=[spec] * len(ins) + [ANY] * len(extra), out_specs=[spec] * n_out,
        out_shape=[SDS((rows, cols), F32)] * n_out, name=name, compiler_params=_params(1))(*ins, *extra)


def _adam_math(w, g, m, v):
    m2 = ADAM_B1 * m + (1.0 - ADAM_B1) * g
    v2 = ADAM_B2 * v + (1.0 - ADAM_B2) * (g * g)
    m_hat = m2 / (1.0 - ADAM_B1 ** ADAM_STEP)
    v_hat = v2 / (1.0 - ADAM_B2 ** ADAM_STEP)
    delta = -ADAM_LR * (m_hat / (jnp.sqrt(v_hat) + ADAM_EPS) + ADAM_WD * w)
    return delta, m2, v2


def _sum4(name, own, recv, idx):
    _, rows, cols = own.shape
    tr = rows
    while tr * cols * 4 > (1 << 20) and tr % 16 == 0:
        tr //= 2

    def body(idx_ref, o_ref, r0_ref, r1_ref, r2_ref, out_ref):
        out_ref[...] = ((o_ref[...] + r0_ref[...].astype(F32)) + r1_ref[...].astype(F32)) + r2_ref[...].astype(F32)

    blk = (None, tr, cols)
    grid_spec = pltpu.PrefetchScalarGridSpec(
        num_scalar_prefetch=1, grid=(rows // tr,),
        in_specs=[pl.BlockSpec(blk, lambda i, s: (s[0], i, 0)), pl.BlockSpec(blk, lambda i, s: (0, i, 0)),
                  pl.BlockSpec(blk, lambda i, s: (1, i, 0)), pl.BlockSpec(blk, lambda i, s: (2, i, 0))],
        out_specs=pl.BlockSpec((tr, cols), lambda i, s: (i, 0)))
    return pl.pallas_call(body, grid_spec=grid_spec, out_shape=SDS((rows, cols), F32), name=name,
                          compiler_params=_params(1))(jnp.reshape(idx, (1,)).astype(jnp.int32), own, recv, recv, recv)


def _adam_pair(name, item, after=None):
    def fn(w_, a, b, m_, v_):
        g = a + b
        return (g,) + _adam_math(w_, g, m_, v_)

    return _ew_call(name, fn, list(item), 4, after)


def _place():
    return lax.axis_index("x"), lax.axis_index("y"), lax.axis_index("c")


def _other_chips(x, y):
    return [(1 - x, y), (x, 1 - y), (1 - x, 1 - y)]


HBM = pl.BlockSpec(memory_space=pltpu.HBM)
SEM = pl.BlockSpec(memory_space=pltpu.SEMAPHORE)
DATAFLOW = pltpu.SideEffectType.DATAFLOW_SIDE_EFFECTING


class _Flight:
    def __init__(self, copies, n_copies, send, recv, srcs, lands, token):
        self.copies, self.n, self.send, self.recv = copies, n_copies, send, recv
        self.srcs, self.lands, self.token = list(srcs), list(lands), token


def _take_off(name, srcs, lands, copies, n_copies, after):
    n_s, n_l = len(srcs), len(lands)

    def body(*refs):
        src, land = refs[:n_s], refs[n_s:n_s + n_l]
        send, recv = refs[n_s + n_l + 1:n_s + n_l + 3]
        for cp in copies(src, land, send, recv):
            cp.start()
        refs[-1][...] = jnp.zeros_like(refs[-1])

    mem = lambda t: pltpu.HBM(t.shape, t.dtype)
    sems = pltpu.SemaphoreType.DMA((n_copies,))
    outs = pl.pallas_call(
        body, name=name,
        out_shape=(sems, sems, *map(mem, srcs), *map(mem, lands), SDS((SUBLANES, LANES), F32)),
        in_specs=[HBM] * (n_s + n_l) + [ANY],
        out_specs=(SEM, SEM, *[HBM] * (n_s + n_l), pl.BlockSpec(memory_space=pltpu.VMEM)),
        input_output_aliases={i: 2 + i for i in range(n_s + n_l)},
        compiler_params=pltpu.CompilerParams(has_side_effects=DATAFLOW),
    )(*[pltpu.with_memory_space_constraint(t, pltpu.HBM) for t in (*srcs, *lands)], after)
    return _Flight(copies, n_copies, outs[0], outs[1], outs[2:2 + n_s], outs[2 + n_s:2 + n_s + n_l], outs[-1])


def _land(name, flight, after):
    n_s, n_l = len(flight.srcs), len(flight.lands)

    def body(*refs):
        src, land = refs[:n_s], refs[n_s:n_s + n_l]
        send, recv = refs[n_s + n_l:n_s + n_l + 2]
        for cp in flight.copies(src, land, send, recv):
            cp.wait_send()
            cp.wait_recv()

    mem = lambda t: pltpu.HBM(t.shape, t.dtype)
    outs = pl.pallas_call(
        body, name=name, out_shape=(*map(mem, flight.srcs), *map(mem, flight.lands)),
        in_specs=[HBM] * (n_s + n_l) + [SEM, SEM, ANY], out_specs=tuple([HBM] * (n_s + n_l)),
        input_output_aliases={i: i for i in range(n_s + n_l)},
        compiler_params=pltpu.CompilerParams(has_side_effects=DATAFLOW),
    )(*flight.srcs, *flight.lands, flight.send, flight.recv, after)
    return list(outs[:n_s]), list(outs[n_s:])


def _empty_like(shapes_from, lead):
    return [lax.empty((lead,) + t.shape[1:], t.dtype) for t in shapes_from]


def _scatter_off(name, chunks, after):
    def copies(src, land, send, recv):
        x, y, c = _place()
        return [pltpu.make_async_remote_copy(
            src_ref=src[a].at[2 * px + py], dst_ref=land[a].at[k], send_sem=send.at[3 * a + k],
            recv_sem=recv.at[3 * a + k], device_id=(px, py, c), device_id_type=MESH_ID)
            for a in range(len(chunks)) for k, (px, py) in enumerate(_other_chips(x, y))]

    return _take_off(name, chunks, _empty_like(chunks, 3), copies, 3 * len(chunks), after)


def _swap_off(name, arrs, after):
    def copies(src, land, send, recv):
        x, y, c = _place()
        return [pltpu.make_async_remote_copy(
            src_ref=src[a], dst_ref=land[a], send_sem=send.at[a], recv_sem=recv.at[a],
            device_id=(x, y, 1 - c), device_id_type=MESH_ID) for a in range(len(arrs))]

    return _take_off(name, arrs, [lax.empty(t.shape, t.dtype) for t in arrs], copies, len(arrs), after)


def _devices_off(name, block, after):
    me = 4 * lax.axis_index("x") + 2 * lax.axis_index("y") + lax.axis_index("c")
    land = lax.dynamic_update_index_in_dim(lax.empty((N_DEV,) + block.shape, block.dtype), block, me, 0)

    def copies(src, land, send, recv):
        x, y, c = _place()
        mine = 4 * x + 2 * y + c
        return [pltpu.make_async_remote_copy(
            src_ref=src[0], dst_ref=land[0].at[mine], send_sem=send.at[k - 1], recv_sem=recv.at[k - 1],
            device_id=(x ^ (k >> 2), y ^ ((k >> 1) & 1), c ^ (k & 1)), device_id_type=MESH_ID)
            for k in range(1, N_DEV)]

    return _take_off(name, [block], [land], copies, N_DEV - 1, after)


def _half_rows(shape, c, other=False):
    half = shape[0] // 2
    return pl.ds(((1 - c) if other else c) * half, half)


def _gather_start(name, shards, lands, after):
    n = len(shards)

    def body(*refs):
        src, land, (send, recv) = refs[:n], refs[n:2 * n], refs[2 * n + 1:2 * n + 3]
        x, y, c = _place()
        me = 2 * x + y
        for a in range(n):
            mine = _half_rows(shards[a].shape, c)
            for j, (px, py) in enumerate(_other_chips(x, y)):
                pltpu.make_async_remote_copy(
                    src_ref=src[a].at[mine], dst_ref=land[a].at[me, mine], send_sem=send.at[3 * a + j],
                    recv_sem=recv.at[3 * a + j], device_id=(px, py, c), device_id_type=MESH_ID).start()
        token = refs[-1]
        token[...] = jnp.zeros_like(token)

    mem = lambda t: pltpu.HBM(t.shape, t.dtype)
    pair = pltpu.SemaphoreType.DMA((3 * n,))
    outs = pl.pallas_call(
        body, name=name,
        out_shape=(pair, pair, *map(mem, shards), *map(mem, lands), SDS((SUBLANES, LANES), F32)),
        in_specs=[HBM] * (2 * n) + [ANY],
        out_specs=(SEM, SEM, *[HBM] * (2 * n), pl.BlockSpec(memory_space=pltpu.VMEM)),
        input_output_aliases={i: 2 + i for i in range(2 * n)},
        compiler_params=pltpu.CompilerParams(has_side_effects=DATAFLOW),
    )(*[pltpu.with_memory_space_constraint(t, pltpu.HBM) for t in (*shards, *lands)], after)
    return outs[0], outs[1], list(outs[2:2 + n]), list(outs[2 + n:2 + 2 * n]), outs[-1]


def _gather_pass(name, send, recv, shards, lands, after, first=0):
    n = len(shards)

    def body(*refs):
        src, land, (send, recv, _) = refs[:n], refs[n:2 * n], refs[2 * n:2 * n + 3]
        fsend, frecv = refs[2 * n + 3], refs[2 * n + 4]
        x, y, c = _place()
        me = 2 * x + y
        for a in range(n):
            mine = _half_rows(shards[a].shape, c)
            for j, (px, py) in enumerate(_other_chips(x, y)):
                far = 2 * px + py
                ici = pltpu.make_async_remote_copy(
                    src_ref=src[a].at[mine], dst_ref=land[a].at[far, mine], send_sem=send.at[3 * (first + a) + j],
                    recv_sem=recv.at[3 * (first + a) + j], device_id=(px, py, c), device_id_type=MESH_ID)
                ici.wait_recv()
                ici.wait_send()
                pltpu.make_async_remote_copy(
                    src_ref=land[a].at[far, mine], dst_ref=land[a].at[far, mine], send_sem=fsend.at[3 * a + j],
                    recv_sem=frecv.at[3 * a + j], device_id=(x, y, 1 - c), device_id_type=MESH_ID).start()
        token = refs[-1]
        token[...] = jnp.zeros_like(token)

    mem = lambda t: pltpu.HBM(t.shape, t.dtype)
    pair = pltpu.SemaphoreType.DMA((3 * n,))
    outs = pl.pallas_call(
        body, name=name,
        out_shape=(pair, pair, *map(mem, lands), SDS((SUBLANES, LANES), F32)),
        in_specs=[HBM] * (2 * n) + [SEM, SEM, ANY],
        out_specs=(SEM, SEM, *[HBM] * n, pl.BlockSpec(memory_space=pltpu.VMEM)),
        input_output_aliases={n + i: 2 + i for i in range(n)},
        compiler_params=pltpu.CompilerParams(has_side_effects=DATAFLOW),
    )(*shards, *lands, send, recv, after)
    return outs[0], outs[1], list(outs[2:2 + n]), outs[-1]


def _gather_wait(name, fsend, frecv, lands, after):
    n = len(lands)

    def body(*refs):
        land, (fsend, frecv, _) = refs[:n], refs[n:n + 3]
        x, y, c = _place()
        for a in range(n):
            for j, (px, py) in enumerate(_other_chips(x, y)):
                far = 2 * px + py
                mine = _half_rows(lands[a].shape[1:], c)
                theirs = _half_rows(lands[a].shape[1:], c, other=True)
                pltpu.make_async_remote_copy(
                    src_ref=land[a].at[far, mine], dst_ref=land[a].at[far, mine], send_sem=fsend.at[3 * a + j],
                    recv_sem=frecv.at[3 * a + j], device_id=(x, y, 1 - c), device_id_type=MESH_ID).wait_send()
                pltpu.make_async_remote_copy(
                    src_ref=land[a].at[far, theirs], dst_ref=land[a].at[far, theirs], send_sem=fsend.at[3 * a + j],
                    recv_sem=frecv.at[3 * a + j], device_id=(x, y, 1 - c), device_id_type=MESH_ID).wait_recv()

    mem = lambda t: pltpu.HBM(t.shape, t.dtype)
    return list(pl.pallas_call(
        body, name=name, out_shape=tuple(map(mem, lands)), in_specs=[HBM] * n + [SEM, SEM, ANY],
        out_specs=tuple([HBM] * n), input_output_aliases={i: i for i in range(n)},
        compiler_params=pltpu.CompilerParams(has_side_effects=DATAFLOW),
    )(*lands, fsend, frecv, after))


def _after(token):
    return _Exchange([token], [], [], lambda *_: None, lambda *_: None)


def _swap_sibling(arrs):
    n = len(arrs)

    def copies(ins, outs, sems):
        send, recv = sems
        x, y, c = _place()
        return [pltpu.make_async_remote_copy(
            src_ref=ins[a], dst_ref=outs[a], send_sem=send.at[a], recv_sem=recv.at[a],
            device_id=(x, y, 1 - c), device_id_type=MESH_ID) for a in range(n)]

    def start(ins, outs, sems):
        for cp in copies(ins, outs, sems):
            cp.start()

    def wait(ins, outs, sems):
        cps = copies(ins, outs, sems)
        for cp in cps:
            cp.wait_recv()
        for cp in cps:
            cp.wait_send()

    return _Exchange(arrs, [SDS(s.shape, s.dtype) for s in arrs],
                     [pltpu.SemaphoreType.DMA((n,)), pltpu.SemaphoreType.DMA((n,))], start, wait)


def _sum_devices(slots):
    def body(s_ref, o_ref):
        acc = s_ref[0]
        for d in range(1, N_DEV):
            acc = acc + s_ref[d]
        o_ref[...] = acc

    return pl.pallas_call(
        body, in_specs=[pl.BlockSpec(memory_space=pltpu.VMEM)], out_specs=pl.BlockSpec(memory_space=pltpu.VMEM),
        out_shape=SDS(slots.shape[1:], F32), name="sum_small",
        compiler_params=pltpu.CompilerParams(vmem_limit_bytes=32 * 1024 * 1024))(slots)


def _adam_small(ws, gs, ms, vs):
    n = len(ws)

    def body(*refs):
        for i in range(n):
            w_ref, g_ref, m_ref, v_ref = (refs[k * n + i] for k in range(4))
            outs = _adam_math(w_ref[...], g_ref[...], m_ref[...], v_ref[...])
            for k in range(3):
                refs[(4 + k) * n + i][...] = outs[k]

    vmem = pl.BlockSpec(memory_space=pltpu.VMEM)
    return pl.pallas_call(
        body, in_specs=[vmem] * (4 * n), out_specs=[vmem] * (3 * n),
        out_shape=[SDS(w.shape, F32) for w in ws] * 3, name="adam_small",
        compiler_params=pltpu.CompilerParams(vmem_limit_bytes=32 * 1024 * 1024))(*ws, *gs, *ms, *vs)


def _local_step(x, target, small, big, tb, distributed):
    dist = distributed
    me = (2 * lax.axis_index("x") + lax.axis_index("y")) if dist else 0
    tb_ssm = min(tb, 256)
    bucket = jnp.asarray(_bucket_table())
    place_own = lambda t: lax.dynamic_update_index_in_dim(lax.empty((N_CHIPS,) + t.shape, t.dtype), t, me, 0)
    if dist:
        in_legs = _gather_start("gather_in_start", [big["w_in"]], [place_own(big["w_in"])], small["d_skip"])
        names = sorted(small)
        in_token, values = lax.optimization_barrier((in_legs[4], [small[n] for n in names]))
        small = dict(zip(names, values))
    g1, g2, g3, g4 = small["norm_mix_pre"], small["norm_mix_post"], small["norm_mlp_pre"], small["norm_mlp_post"]

    keys_first = lambda t: jnp.swapaxes(t, -1, -2)
    bias = _bias_table(small["rel_bias"], bucket)
    sink_rows = keys_first(_pair_layout(jnp.broadcast_to(small["sinks"].reshape(N_HEADS, 1, 1), (N_HEADS, BLOCK, 1))))
    disc_args = (small["lam_re"], small["lam_im"], small["log_dt"], small["b_re"], small["b_im"])
    (ab_re, ab_im, bb_re, bb_im), disc_vjp = jax.vjp(_ssm_discretize, *disc_args)
    tab_f, tab_b = _scan_tables(ab_re, ab_im)
    bmat = _bf(_b_matrix(bb_re, bb_im))
    cmat = _bf(_c_matrix(small["c_re"], small["c_im"]))
    bmat_t, cmat_t = bmat.transpose(0, 2, 1), cmat.transpose(0, 2, 1)
    d_skip = small["d_skip"]

    mix = ("w_glu", "w_attn_branch", "w_ssm_branch", "w_out")
    rest = [big[n] for n in mix + ("w_ff_in", "w_ff_out")]
    if dist:
        send, recv, src, lands, _ = in_legs
        tab_f, tab_b, bias, sink_rows, bmat, cmat, bmat_t, cmat_t, rest_lands = lax.optimization_barrier(
            (tab_f, tab_b, bias, sink_rows, bmat, cmat, bmat_t, cmat_t, [place_own(t) for t in rest]))
        corner = lambda t: t.reshape(-1, t.shape[-1])[:1, :LANES].astype(F32)
        prepared = sum(map(corner, [tab_b, bias, sink_rows, bmat, cmat] + rest_lands), in_token[:1])
        send, recv, lands, in_passed = _gather_pass("gather_in_pass", send, recv, src, lands, prepared)
        (g_in,) = _gather_wait("gather_in_wait", send, recv, lands, in_passed)
        w_in = g_in.reshape(IN_W, D_MODEL)
    else:
        w_in = big["w_in"]
    token = None
    n_mix = len(mix)
    if dist:
        send, recv, rest, lands, token = _gather_start("gather_rest_start", rest, rest_lands, in_passed)
    h1, q, k, v, u, ga, gs = _inproj_fwd(x, g1, w_in, tb, _after(token) if dist else None)
    s, h = _ssm_fwd(u, bmat, cmat, tab_f, d_skip, tb)
    if dist:
        fsend, frecv, mix_lands, token = _gather_pass("gather_mix_pass", send, recv, rest[:n_mix], lands[:n_mix], s)
    att = _attn_fwd(q, k, v, bias, sink_rows, _after(token) if dist else None)[0]
    if dist:
        w_mix = _gather_wait("gather_mix_wait", fsend, frecv, mix_lands, att)
        fsend, frecv, ff_lands, token = _gather_pass(
            "gather_ff_pass", send, recv, rest[n_mix:], lands[n_mix:], w_mix[0], n_mix)
        rest = w_mix + ff_lands
    w_glu, w_ab, w_sb, w_out = rest[:n_mix]
    w_glu = w_glu.reshape(SSM_W, SSM_W)
    w_out = w_out.reshape(D_MODEL, D_MODEL)
    x2 = _merge_fwd(x, s, att, ga, gs, g2, w_glu, w_ab, w_sb, w_out, tb, _after(token) if dist else None)
    if dist:
        rest[n_mix:] = _gather_wait("gather_ff_wait", fsend, frecv, ff_lands, x2)
    w_ffi, w_ffo = [rest[n_mix]], rest[n_mix + 1]
    dy, df, h3, ra, loss_acc, dg4 = _mlp_fwd_loss(x2, target, g3, g4, w_ffi, w_ffo, tb)

    dx2, da, dg3 = _mlp_bwd(x2, dy, df, ra, g3, w_ffi, w_ffo, tb)
    tl = min(2048, x.shape[0])
    chunked = (N_CHIPS, D_FF // N_CHIPS, D_MODEL)
    d_ffi, b_ffi = _matmul_tn("grad_w_ff_in", h3, da, D_MODEL, D_FF // FF_CHUNKS, tl, True)
    d_ffo, b_ffo = _matmul_tn("grad_w_ff_out", ra, df, D_FF // FF_CHUNKS, D_MODEL, tl, False, square_a=True)
    d_ffo, b_ffo = d_ffo.reshape(chunked), b_ffo.reshape(chunked)
    behind = lambda flight: _after(flight.token) if dist else None
    ff_fl = _scatter_off("scatter_ff_off", [b_ffi, b_ffo], d_ffo) if dist else None
    outs = _merge_bwd(dx2, s, att, ga, gs, g2, w_glu, w_ab, w_sb, w_out, tb_ssm, behind(ff_fl))
    ds, datt, dga, dgs, dg2, d_glu, d_ab, d_sb, d_out, b_glu, b_ab, b_sb, b_out = outs
    glu4, out4 = (N_CHIPS, SSM_W // N_CHIPS, SSM_W), (N_CHIPS, D_MODEL // N_CHIPS, D_MODEL)
    d_mix = [d_glu.reshape(glu4), d_ab, d_sb, d_out.reshape(out4)]
    b_mix = [b_glu.reshape(glu4), b_ab, b_sb, b_out.reshape(out4)]
    mix_fl = _scatter_off("scatter_mix_off", b_mix, d_mix[-1]) if dist else None
    du, d_bmat, d_cmat, da_acc, dd_skip = _ssm_bwd(
        ds, u, h, bmat_t, cmat_t, tab_b, d_skip, tb, behind(mix_fl))
    dq, dk, dv, dbias, dsink_rows = _attn_bwd(q, k, v, datt, bias, sink_rows)
    swap_fl = None
    if dist:
        r_ffi, r_ffo = _land("scatter_ff_land", ff_fl, dq)[1]
        p_ffi = _sum4("sum_w_ff_in", d_ffi, r_ffi, me)
        p_ffo = _sum4("sum_w_ff_out", d_ffo, r_ffo, me)
        swap_fl = _swap_off("swap_ff_off", [p_ffi, p_ffo], r_ffo)
    dx, dpj, dg1 = _inproj_bwd(x, dx2, dq, dk, dv, du, dga, dgs, g1, w_in, tb, behind(swap_fl))

    dab_re, dab_im = _state_unlayout(jnp.sum(da_acc, axis=0))
    dbb_re, dbb_im = _b_matrix_grad(d_bmat)
    d_lam_re, d_lam_im, d_log_dt, d_b_re, d_b_im = disc_vjp((dab_re, dab_im, dbb_re, dbb_im))
    d_c_re, d_c_im = _c_matrix_grad(d_cmat)
    d_rel = _bias_grad(dbias, bucket)
    d_sinks = jnp.sum(_pair_unlayout(keys_first(dsink_rows)), axis=(1, 2))
    small_grads = dict(
        norm_mix_pre=dg1, norm_mix_post=dg2, norm_mlp_pre=dg3, norm_mlp_post=dg4, rel_bias=d_rel, sinks=d_sinks,
        lam_re=d_lam_re, lam_im=d_lam_im, log_dt=d_log_dt, b_re=d_b_re, b_im=d_b_im, c_re=d_c_re, c_im=d_c_im,
        d_skip=dd_skip)
    small_fl = _devices_off("small_off", _pack(small_grads, loss_acc), swap_fl.token) if dist else None
    outs = _matmul_tn("grad_w_in", dpj, h1, IN_W // 2, D_MODEL, tl, False, behind(small_fl))
    in4 = (N_CHIPS, IN_W // N_CHIPS, D_MODEL)
    d_in, b_in = outs[0].reshape(in4), outs[1].reshape(in4)
    if not dist:
        return loss_acc, dx, small_grads, dict(zip(BIG, [d_in] + d_mix + [d_ffi, d_ffo]))
    in_fl = _scatter_off("scatter_w_in_off", [b_in], d_in)
    (p_ffi, p_ffo), (s_ffi, s_ffo) = _land("swap_ff_land", swap_fl, in_fl.token)
    r_mix = _land("scatter_mix_land", mix_fl, in_fl.token)[1]
    p_mix = [_sum4("sum_" + n, d, r, me) for n, d, r in zip(mix, d_mix, r_mix)]
    mix_swap = _swap_off("swap_mix_off", p_mix, in_fl.token)
    pending = dict(d_in=d_in, in_fl=in_fl, mix_swap=mix_swap, w_ff_in=(p_ffi, s_ffi), w_ff_out=(p_ffo, s_ffo), me=me)
    return loss_acc, dx, small_fl, pending


SMALL = ['norm_mix_pre', 'norm_mix_post', 'norm_mlp_pre', 'norm_mlp_post', 'rel_bias', 'sinks', 'lam_re', 'lam_im',
         'log_dt', 'b_re', 'b_im', 'c_re', 'c_im', 'd_skip']
BIG = ['w_in', 'w_glu', 'w_attn_branch', 'w_ssm_branch', 'w_out', 'w_ff_in', 'w_ff_out']
WEIGHTS = ['norm_mix_pre', 'norm_mix_post', 'norm_mlp_pre', 'norm_mlp_post', 'w_in', 'rel_bias', 'sinks', 'lam_re',
           'lam_im', 'log_dt', 'b_re', 'b_im', 'c_re', 'c_im', 'd_skip', 'w_glu', 'w_attn_branch', 'w_ssm_branch',
           'w_out', 'w_ff_in', 'w_ff_out']
PACK_COLS = 1024
PACK_ORDER = ['b_re', 'b_im', 'c_re', 'c_im', 'lam_re', 'lam_im', 'norm_mix_pre', 'norm_mix_post', 'norm_mlp_pre',
              'norm_mlp_post', 'rel_bias', 'sinks', 'log_dt', 'd_skip']


STATE_MINOR = ('b_re', 'b_im')
PACK_ROWS = 144
LOSS_ROW = 140


def _pack(named, loss_acc):
    parts = []
    for n in PACK_ORDER:
        a = jnp.swapaxes(named[n], -1, -2) if n in STATE_MINOR else named[n]
        flat = a.reshape(-1)
        rows = -(-flat.shape[0] // PACK_COLS)
        parts.append(jnp.pad(flat, (0, rows * PACK_COLS - flat.shape[0])).reshape(rows, PACK_COLS))
    assert sum(p.shape[0] for p in parts) == LOSS_ROW
    parts.append(jnp.pad(loss_acc[0:1], ((0, PACK_ROWS - LOSS_ROW - 1), (0, PACK_COLS - loss_acc.shape[1]))))
    return jnp.concatenate(parts, axis=0)


def _unpack(packed, shapes):
    out, at = {}, 0
    for n in PACK_ORDER:
        shape = shapes[n][:-2] + (shapes[n][-1], shapes[n][-2]) if n in STATE_MINOR else shapes[n]
        size = int(np.prod(shape))
        rows = -(-size // PACK_COLS)
        blk = packed[at:at + rows]
        out[n] = (blk.reshape(-1)[:size] if size % PACK_COLS else blk).reshape(shape)
        at += rows
    return out


def kernel(x, norm_mix_pre, norm_mix_post, norm_mlp_pre, norm_mlp_post, w_in, rel_bias, sinks, lam_re, lam_im, log_dt, b_re, b_im, c_re, c_im, d_skip, w_glu, w_attn_branch, w_ssm_branch, w_out, w_ff_in, w_ff_out, loss_target, m_norm_mix_pre, m_norm_mix_post, m_norm_mlp_pre, m_norm_mlp_post, m_w_in, m_rel_bias, m_sinks, m_lam_re, m_lam_im, m_log_dt, m_b_re, m_b_im, m_c_re, m_c_im, m_d_skip, m_w_glu, m_w_attn_branch, m_w_ssm_branch, m_w_out, m_w_ff_in, m_w_ff_out, v_norm_mix_pre, v_norm_mix_post, v_norm_mlp_pre, v_norm_mlp_post, v_w_in, v_rel_bias, v_sinks, v_lam_re, v_lam_im, v_log_dt, v_b_re, v_b_im, v_c_re, v_c_im, v_d_skip, v_w_glu, v_w_attn_branch, v_w_ssm_branch, v_w_out, v_w_ff_in, v_w_ff_out):
    env = dict(locals())
    w = {n: env[n] for n in WEIGHTS}
    m = {n: env["m_" + n] for n in WEIGHTS}
    v = {n: env["v_" + n] for n in WEIGHTS}
    seq = x.shape[1]
    tb = min(512, seq)

    small = {n: w[n] for n in ('norm_mix_pre', 'norm_mix_post', 'norm_mlp_pre', 'norm_mlp_post', 'rel_bias')}
    small.update({n: w[n][0] for n in ('sinks', 'lam_re', 'lam_im', 'log_dt', 'b_re', 'b_im', 'c_re', 'c_im')})
    small['d_skip'] = w['d_skip']
    shard = lambda t, n: t[n][0].T if n == 'w_in' else t[n][0]
    unshard = lambda a, n: (a.T if n == 'w_in' else a)[None]
    _, dx, small_fl, pending = _local_step(
        x[0], loss_target[0], small, {n: _bf(shard(w, n)) for n in BIG}, tb, True)

    grads, deltas, new_m, new_v = {}, {}, {}, {}

    def adam(n, partials, after=None):
        outs = _adam_pair("adam_" + n, (shard(w, n), *partials, shard(m, n), shard(v, n)), after)
        grads[n], deltas[n], new_m[n], new_v[n] = [unshard(a, n) for a in outs]
        return outs[3]

    mix = ("w_glu", "w_attn_branch", "w_ssm_branch", "w_out")
    in_fl = pending["in_fl"]
    last = pending["mix_swap"].token
    for n in ("w_ff_in", "w_ff_out"):
        last = adam(n, pending[n], last)
    for n, partials in zip(mix, zip(*_land("swap_mix_land", pending["mix_swap"], last))):
        last = adam(n, partials, last)

    small_g = _sum_devices(_land("small_land", small_fl, last)[1][0])
    loss = small_g[LOSS_ROW, 0]
    minor = lambda t, n: jnp.swapaxes(t, -1, -2) if n in STATE_MINOR else t
    g_small = _unpack(small_g, {n: w[n].shape for n in SMALL})
    outs = _adam_small([minor(w[n], n) for n in SMALL], [g_small[n] for n in SMALL],
                       [minor(m[n], n) for n in SMALL], [minor(v[n], n) for n in SMALL])
    grads.update({n: minor(g_small[n], n) for n in SMALL})
    for k, dst in enumerate((deltas, new_m, new_v)):
        dst.update({n: minor(a, n) for n, a in zip(SMALL, outs[k * len(SMALL):(k + 1) * len(SMALL)])})

    (r_in,) = _land("scatter_w_in_land", in_fl, outs[0])[1]
    p_in = _sum4("sum_w_in", pending["d_in"], r_in, pending["me"])
    (s_in,) = _exchange_alone("swap_w_in", _swap_sibling([p_in]))
    adam("w_in", (p_in, s_in))

    return (loss, dx[None], *[grads[n] for n in WEIGHTS], *[deltas[n] for n in WEIGHTS],
            *[new_m[n] for n in WEIGHTS], *[new_v[n] for n in WEIGHTS])
```

```python
import functools
import math

import numpy as np
import jax
import jax.numpy as jnp
from jax import lax
from jax.experimental import pallas as pl
from jax.experimental.pallas import tpu as pltpu

F32 = jnp.float32
BF16 = jnp.bfloat16

D_MODEL = 1024
N_HEADS = 8
N_KV = 2
Q_GROUP = 4
HEAD_DIM = 64
ATTN_W = 512
KV_W = 128
BLOCK = 128
N_BUCKETS = 32
MAX_DISTANCE = 128
NEG_INF = -1e30
SSM_W = 512
SSM_GROUP = 16
SSM_GROUPS = 32
SSM_STATE = 64
N_SUPER = 4
GROUPS_PER_SUPER = SSM_GROUPS // N_SUPER
SUPER_IN = GROUPS_PER_SUPER * SSM_GROUP
SUPER_HALF = GROUPS_PER_SUPER * SSM_STATE
SUPER_W = 2 * SUPER_HALF
STATE_COLS = N_SUPER * SUPER_W
D_FF = 4096
FF_CHUNKS = 4
IN_W = 3328
SPLITS = (0, 512, 640, 768, 1280, 2304, 3328)
RMS_EPS = 1e-6
N_CHIPS = 4
N_DEV = 8
SUBLANES = 8
LANES = 128
STATE_TILES = STATE_COLS // LANES
SUPER_TILES = SUPER_W // LANES

ADAM_LR = 0.001
ADAM_B1 = 0.9
ADAM_B2 = 0.999
ADAM_EPS = 1e-08
ADAM_WD = 0.01
ADAM_STEP = 10

VMEM_BIG = 56 * 1024 * 1024
SDS = jax.ShapeDtypeStruct
MESH_ID = pl.DeviceIdType.MESH
ANY = pl.BlockSpec(memory_space=pl.ANY)


def _bf(x):
    return x.astype(BF16)


def _mm(a, b):
    return jnp.dot(a, b, preferred_element_type=F32)


def _mm_nt(a, b):
    return lax.dot_general(a, b, (((1,), (1,)), ((), ())), preferred_element_type=F32)


def _mm_tn(a, b):
    return lax.dot_general(a, b, (((0,), (0,)), ((), ())), preferred_element_type=F32)


def _sig(x):
    return 1.0 / (1.0 + jnp.exp(-x))


def _rms(x, g):
    r = lax.rsqrt(jnp.mean(x * x, axis=-1, keepdims=True) + RMS_EPS)
    xh = x * r
    return xh * g, xh, r


def _rms_bwd(dout, xh, r, g):
    dg = jnp.sum(dout * xh, axis=0, keepdims=True)
    dxh = dout * g
    dx = r * (dxh - xh * jnp.mean(dxh * xh, axis=-1, keepdims=True))
    return dx, dg


_GELU_C = math.sqrt(2.0 / math.pi)


def _gelu_and_grad(x):
    x2 = x * x
    inner = _GELU_C * (x + 0.044715 * (x2 * x))
    t = jnp.tanh(inner)
    y = 0.5 * x * (1.0 + t)
    dy = 0.5 * (1.0 + t) + 0.5 * x * (1.0 - t * t) * (_GELU_C * (1.0 + 3.0 * 0.044715 * x2))
    return y, dy


def _zero_map(nd, *_):
    return (0,) * nd


def _params(n_axes, vmem=None):
    return pltpu.CompilerParams(dimension_semantics=("arbitrary",) * n_axes, vmem_limit_bytes=vmem)


class _Exchange:
    def __init__(self, ins, outs, sems, start, wait):
        self.ins, self.outs, self.sems, self.start, self.wait = list(ins), list(outs), list(sems), start, wait


def _fused_call(name, body, grid, in_specs, out_specs, out_shape, scratch, args, exchange, params):
    n_in, n_out, n_scr = len(in_specs), len(out_specs), len(scratch)
    if exchange is None:
        fn = body
    else:
        ex = exchange
        n_xi, n_xo = len(ex.ins), len(ex.outs)

        def fn(*refs):
            at = 0
            parts = []
            for n in (n_in, n_xi, n_out, n_xo, n_scr, len(ex.sems)):
                parts.append(refs[at:at + n])
                at += n
            ins, x_in, outs, x_out, scr, x_sem = parts
            ids = [pl.program_id(a) for a in range(len(grid))]
            first = functools.reduce(jnp.logical_and, [i == 0 for i in ids])
            last = functools.reduce(jnp.logical_and, [i == g - 1 for i, g in zip(ids, grid)])

            @pl.when(first)
            def _():
                ex.start(x_in, x_out, x_sem)

            body(*ins, *outs, *scr)

            @pl.when(last)
            def _():
                ex.wait(x_in, x_out, x_sem)

        in_specs = list(in_specs) + [ANY] * n_xi
        out_specs = list(out_specs) + [ANY] * n_xo
        out_shape = list(out_shape) + ex.outs
        scratch = list(scratch) + ex.sems
        args = list(args) + ex.ins
    return pl.pallas_call(fn, grid=grid, in_specs=in_specs, out_specs=out_specs, out_shape=out_shape,
                          scratch_shapes=list(scratch), name=name, compiler_params=params)(*args)


def _exchange_alone(name, ex):
    def body(*refs):
        n_xi, n_xo = len(ex.ins), len(ex.outs)
        x_in, x_out, x_sem = refs[:n_xi], refs[n_xi:n_xi + n_xo], refs[n_xi + n_xo:]
        ex.start(x_in, x_out, x_sem)
        ex.wait(x_in, x_out, x_sem)

    return pl.pallas_call(body, in_specs=[ANY] * len(ex.ins), out_specs=[ANY] * len(ex.outs), out_shape=ex.outs,
                          scratch_shapes=ex.sems, name=name)(*ex.ins)


def _rowcall(name, body, seq, tb, rows, consts, row_outs, acc_outs, scratch=(), reverse=False, vmem=None,
             exchange=None):
    nb = seq // tb
    rmap = (lambda i: (nb - 1 - i, 0)) if reverse else (lambda i: (i, 0))
    tmap = lambda i: (0,) + rmap(i)

    def row_spec(width):
        if isinstance(width, tuple):
            return pl.BlockSpec((width[0], tb, width[1]), tmap)
        return pl.BlockSpec((tb, width), rmap)

    def row_shape(width):
        return (width[0], seq, width[1]) if isinstance(width, tuple) else (seq, width)

    in_specs = [row_spec(a.shape[1] if a.ndim == 2 else (a.shape[0], a.shape[2])) for a in rows]
    in_specs += [pl.BlockSpec(a.shape, functools.partial(_zero_map, a.ndim), pipeline_mode=pl.Buffered(1))
                 for a in consts]
    out_specs = [row_spec(c) for c, _ in row_outs] + [ANY] * len(acc_outs)
    out_shape = [SDS(row_shape(c), dt) for c, dt in row_outs] + [SDS(s, dt) for s, dt in acc_outs]
    n_main = len(rows) + len(consts) + len(row_outs)
    n_acc = len(acc_outs)

    def fn(*refs):
        main, acc_hbm, rest = refs[:n_main], refs[n_main:n_main + n_acc], refs[n_main + n_acc:]
        acc_vmem, own = rest[:n_acc], rest[n_acc:]
        body(*main, *acc_vmem, *own)

        @pl.when(pl.program_id(0) == nb - 1)
        def _():
            for src, dst in zip(acc_vmem, acc_hbm):
                pltpu.sync_copy(src, dst)

    buffers = [pltpu.VMEM(s, dt) for s, dt in acc_outs] + list(scratch)
    return _fused_call(name, fn if acc_outs else body, (nb,), in_specs, out_specs, out_shape, buffers,
                       [*rows, *consts], exchange, _params(1, vmem))


def _inproj_fwd(x, g1, w_in, tb, exchange=None):
    seq = x.shape[0]

    def body(x_ref, g_ref, w_ref, h_ref, q_ref, k_ref, v_ref, u_ref, ga_ref, gs_ref):
        h, _, _ = _rms(x_ref[...], g_ref[...])
        hb = _bf(h)
        h_ref[...] = hb
        pj = _mm_nt(hb, w_ref[...])
        q_ref[...] = _bf(pj[:, SPLITS[0]:SPLITS[1]])
        k_ref[...] = _bf(pj[:, SPLITS[1]:SPLITS[2]])
        v_ref[...] = _bf(pj[:, SPLITS[2]:SPLITS[3]])
        u_ref[...] = pj[:, SPLITS[3]:SPLITS[4]]
        ga_ref[...] = pj[:, SPLITS[4]:SPLITS[5]]
        gs_ref[...] = pj[:, SPLITS[5]:SPLITS[6]]

    return _rowcall("inproj_fwd", body, seq, tb, [x], [g1, w_in],
                    [(D_MODEL, BF16), (ATTN_W, BF16), (KV_W, BF16), (KV_W, BF16), (SSM_W, F32),
                     (D_MODEL, F32), (D_MODEL, F32)], [], vmem=VMEM_BIG, exchange=exchange)


def _inproj_bwd(x, dx2, dq, dk, dv, du, dga, dgs, g1, w_in, tb, exchange=None):
    seq = x.shape[0]

    def body(x_ref, dx2_ref, dq_ref, dk_ref, dv_ref, du_ref, dga_ref, dgs_ref, g_ref, w_ref,
             dx_ref, dpj_ref, dg_ref):
        @pl.when(pl.program_id(0) == 0)
        def _():
            dg_ref[...] = jnp.zeros_like(dg_ref)

        dpj = jnp.concatenate([dq_ref[...], dk_ref[...], dv_ref[...], _bf(du_ref[...]),
                               dga_ref[...], dgs_ref[...]], axis=1)
        dpj_ref[...] = dpj
        dh = _mm(dpj, w_ref[...])
        g = g_ref[...]
        _, xh, r = _rms(x_ref[...], g)
        dxn, dg = _rms_bwd(dh, xh, r, g)
        dx_ref[...] = dx2_ref[...] + dxn
        dg_ref[...] += dg

    return _rowcall("inproj_bwd", body, seq, tb, [x, dx2, dq, dk, dv, du, dga, dgs], [g1, w_in],
                    [(D_MODEL, F32), (IN_W, BF16)], [((1, D_MODEL), F32)], vmem=VMEM_BIG, exchange=exchange)


def _bucket_table():
    qi = np.arange(BLOCK)[:, None]
    kj = np.arange(2 * BLOCK)[None, :]
    dist = qi + BLOCK - kj
    max_exact = N_BUCKETS // 2
    d = np.maximum(dist, 0)
    df = np.maximum(d, 1).astype(np.float32)
    large = max_exact + (np.log(df / np.float32(max_exact)) / np.float32(math.log(MAX_DISTANCE / max_exact))
                         * np.float32(N_BUCKETS - max_exact)).astype(np.int32)
    large = np.minimum(large, N_BUCKETS - 1)
    bucket = np.where(d < max_exact, d, large)
    valid = (dist >= 0) & (dist < BLOCK)
    return np.where(valid, bucket, -1).astype(np.int32)


def _bias_table(rel_bias, bucket):
    def body(rb_ref, bk_ref, o_ref):
        bk = bk_ref[...]
        has_prev = lax.broadcasted_iota(jnp.int32, bk.shape, 1) >= BLOCK
        for h in range(N_HEADS):
            kh, j, par = h // Q_GROUP, (h // 2) % 2, h % 2
            acc = jnp.full((BLOCK, 2 * BLOCK), NEG_INF, F32)
            for b in range(N_BUCKETS):
                acc = jnp.where(bk == b, rb_ref[b, h], acc)
            o_ref[0, kh, par, :, j * BLOCK:(j + 1) * BLOCK] = jnp.where(has_prev, acc, NEG_INF).T
            o_ref[1, kh, par, :, j * BLOCK:(j + 1) * BLOCK] = acc.T

    return pl.pallas_call(
        body, out_shape=SDS((2, N_KV, 2, 2 * BLOCK, 2 * BLOCK), F32),
        in_specs=[pl.BlockSpec(memory_space=pltpu.SMEM), pl.BlockSpec(memory_space=pltpu.VMEM)],
        out_specs=pl.BlockSpec(memory_space=pltpu.VMEM), name="bias_table",
    )(rel_bias, bucket)


def _bias_grad(dbias, bucket):
    def body(db_ref, bk_ref, o_ref):
        bk = bk_ref[...]
        for h in range(N_HEADS):
            kh, j, par = h // Q_GROUP, (h // 2) % 2, h % 2
            db = db_ref[kh, par, :, j * BLOCK:(j + 1) * BLOCK].T
            for b in range(N_BUCKETS):
                o_ref[b, h] = jnp.sum(jnp.where(bk == b, db, 0.0))

    return pl.pallas_call(
        body, out_shape=SDS((N_BUCKETS, N_HEADS), F32),
        in_specs=[pl.BlockSpec(memory_space=pltpu.VMEM), pl.BlockSpec(memory_space=pltpu.VMEM)],
        out_specs=pl.BlockSpec(memory_space=pltpu.SMEM), name="bias_grad",
    )(dbias, bucket)


TILE = 2 * HEAD_DIM


def _pair_layout(t):
    lead = t.shape[:-3]
    t = t.reshape(lead + (N_KV, 2, 2) + t.shape[-2:])
    nl = len(lead)
    t = jnp.transpose(t, tuple(range(nl)) + (nl, nl + 2, nl + 1, nl + 3, nl + 4))
    return t.reshape(lead + (N_KV, 2, 2 * BLOCK, t.shape[-1]))


def _pair_unlayout(t):
    t = t.reshape(N_KV, 2, 2, BLOCK, t.shape[-1]).transpose(0, 2, 1, 3, 4)
    return t.reshape(N_HEADS, BLOCK, t.shape[-1])


def _halves(t):
    tf = t.astype(F32)
    low = lax.broadcasted_iota(jnp.int32, tf.shape, 1) < HEAD_DIM
    swapped = pltpu.roll(tf, HEAD_DIM, 1)
    zero = jnp.zeros_like(tf)
    return ((_bf(jnp.where(low, tf, zero)), _bf(jnp.where(low, zero, swapped))),
            (_bf(jnp.where(low, swapped, zero)), _bf(jnp.where(low, zero, tf))))


def _fold_halves(even, odd):
    low = lax.broadcasted_iota(jnp.int32, even.shape, 1) < HEAD_DIM
    comb = jnp.where(low, even, odd)
    return comb + pltpu.roll(comb, HEAD_DIM, 1)


def _tile_rows(ref, kh):
    return jnp.concatenate([ref[:, (2 * kh) * TILE:(2 * kh + 1) * TILE],
                            ref[:, (2 * kh + 1) * TILE:(2 * kh + 2) * TILE]], axis=0)


def _halves_t(t):
    tt = t.astype(F32).T
    top = lax.broadcasted_iota(jnp.int32, tt.shape, 0) < HEAD_DIM
    swapped = jnp.concatenate([tt[HEAD_DIM:], tt[:HEAD_DIM]], axis=0)
    zero = jnp.zeros_like(tt)
    return ((_bf(jnp.where(top, tt, zero)), _bf(jnp.where(top, zero, swapped))),
            (_bf(jnp.where(top, swapped, zero)), _bf(jnp.where(top, zero, tt))))


def _attn_probs(km, qk, bias, sink):
    lg = _mm_nt(km, qk) * (HEAD_DIM ** -0.5) + bias
    m = jnp.maximum(jnp.max(lg, axis=0, keepdims=True), sink)
    p = jnp.exp(lg - m)
    es = jnp.exp(sink - m)
    inv = 1.0 / (jnp.sum(p, axis=0, keepdims=True) + es)
    return p * inv, es * inv


def _attn_fwd(q, k, v, bias, sink_rows, exchange=None):
    seq = q.shape[0]
    nblk = seq // BLOCK

    def body(q_ref, kp_ref, kc_ref, vp_ref, vc_ref, b_ref, s_ref, o_ref):
        which = jnp.minimum(pl.program_id(0), 1)
        kms = _halves(jnp.concatenate([kp_ref[...], kc_ref[...]], axis=0))
        vts = _halves_t(jnp.concatenate([vp_ref[...], vc_ref[...]], axis=0))
        for kh in range(N_KV):
            qk = _tile_rows(q_ref, kh)
            acc = jnp.zeros((TILE, 2 * BLOCK), F32)
            for par in range(2):
                pr, _ = _attn_probs(kms[kh][par], qk, b_ref[which, kh, par], s_ref[kh, par])
                acc = acc + _mm(vts[kh][par], _bf(pr))
            acc = acc.T
            o_ref[:, (2 * kh) * TILE:(2 * kh + 1) * TILE] = _bf(acc[:BLOCK])
            o_ref[:, (2 * kh + 1) * TILE:(2 * kh + 2) * TILE] = _bf(acc[BLOCK:])

    cur = lambda n: (n, 0)
    prev = lambda n: (jnp.maximum(n - 1, 0), 0)
    return _fused_call(
        "attn_fwd", body, (nblk,),
        [pl.BlockSpec((BLOCK, ATTN_W), cur),
         pl.BlockSpec((BLOCK, KV_W), prev), pl.BlockSpec((BLOCK, KV_W), cur),
         pl.BlockSpec((BLOCK, KV_W), prev), pl.BlockSpec((BLOCK, KV_W), cur),
         pl.BlockSpec(bias.shape, functools.partial(_zero_map, bias.ndim)),
         pl.BlockSpec(sink_rows.shape, functools.partial(_zero_map, sink_rows.ndim))],
        [pl.BlockSpec((BLOCK, ATTN_W), cur)], [SDS((seq, ATTN_W), BF16)], [],
        [q, k, k, v, v, bias, sink_rows], exchange, _params(1))


def _attn_bwd(q, k, v, d_out, bias, sink_rows, exchange=None):
    seq = q.shape[0]
    nblk = seq // BLOCK

    def body(q_ref, kp_ref, kc_ref, vp_ref, vc_ref, do_ref, b_ref, s_ref,
             dq_ref, dk_ref, dv_ref, db_ref, ds_ref, ck_ref, cv_ref):
        n = pl.program_id(0)

        @pl.when(n == 0)
        def _():
            db_ref[...] = jnp.zeros_like(db_ref)
            ds_ref[...] = jnp.zeros_like(ds_ref)
            ck_ref[...] = jnp.zeros_like(ck_ref)
            cv_ref[...] = jnp.zeros_like(cv_ref)

        @pl.when(n < nblk)
        def _():
            which = jnp.minimum(n, 1)
            scale = HEAD_DIM ** -0.5
            kcat = jnp.concatenate([kp_ref[...], kc_ref[...]], axis=0)
            kms = _halves(kcat)
            kts = _halves_t(kcat)
            vms = _halves(jnp.concatenate([vp_ref[...], vc_ref[...]], axis=0))
            dks, dvs = [], []
            for kh in range(N_KV):
                qk = _tile_rows(q_ref, kh)
                dok = _tile_rows(do_ref, kh)
                dq = jnp.zeros((TILE, 2 * BLOCK), F32)
                dkp, dvp = [], []
                for par in range(2):
                    pr, ps = _attn_probs(kms[kh][par], qk, b_ref[which, kh, par], s_ref[kh, par])
                    dp = _mm_nt(vms[kh][par], dok)
                    rs = jnp.sum(pr * dp, axis=0, keepdims=True)
                    dlg = pr * (dp - rs)
                    ds_ref[kh, par] += -ps * rs
                    db_ref[kh, par] += dlg
                    dlb = _bf(dlg)
                    dq = dq + _mm(kts[kh][par], dlb)
                    dkp.append(_mm(dlb, qk))
                    dvp.append(_mm(_bf(pr), dok))
                dq = _bf((dq * scale).T)
                dq_ref[:, (2 * kh) * TILE:(2 * kh + 1) * TILE] = dq[:BLOCK]
                dq_ref[:, (2 * kh + 1) * TILE:(2 * kh + 2) * TILE] = dq[BLOCK:]
                dks.append(_fold_halves(*dkp))
                dvs.append(_fold_halves(*dvp))
            low = lax.broadcasted_iota(jnp.int32, (2 * BLOCK, TILE), 1) < HEAD_DIM
            dkk = jnp.where(low, dks[0], dks[1]) * scale
            dvv = jnp.where(low, dvs[0], dvs[1])
            dk_ref[...] = _bf(ck_ref[...] + dkk[:BLOCK])
            ck_ref[...] = dkk[BLOCK:]
            dv_ref[...] = _bf(cv_ref[...] + dvv[:BLOCK])
            cv_ref[...] = dvv[BLOCK:]

        @pl.when(n == nblk)
        def _():
            dk_ref[...] = _bf(ck_ref[...])
            dv_ref[...] = _bf(cv_ref[...])

    cur = lambda n: (jnp.minimum(n, nblk - 1), 0)
    prev = lambda n: (jnp.maximum(jnp.minimum(n, nblk - 1) - 1, 0), 0)
    late = lambda n: (jnp.maximum(n - 1, 0), 0)
    kv_spec = lambda m: pl.BlockSpec((BLOCK, KV_W), m)
    acc_b = pl.BlockSpec(bias.shape[1:], functools.partial(_zero_map, bias.ndim - 1))
    acc_s = pl.BlockSpec(sink_rows.shape, functools.partial(_zero_map, sink_rows.ndim))
    return _fused_call(
        "attn_bwd", body, (nblk + 1,),
        [pl.BlockSpec((BLOCK, ATTN_W), cur), kv_spec(prev), kv_spec(cur), kv_spec(prev), kv_spec(cur),
         pl.BlockSpec((BLOCK, ATTN_W), cur),
         pl.BlockSpec(bias.shape, functools.partial(_zero_map, bias.ndim)), acc_s],
        [pl.BlockSpec((BLOCK, ATTN_W), cur), kv_spec(late), kv_spec(late), acc_b, acc_s],
        [SDS((seq, ATTN_W), BF16), SDS((seq, KV_W), BF16), SDS((seq, KV_W), BF16),
         SDS(bias.shape[1:], F32), SDS(sink_rows.shape, F32)],
        [pltpu.VMEM((BLOCK, KV_W), F32), pltpu.VMEM((BLOCK, KV_W), F32)],
        [q, k, k, v, v, d_out, bias, sink_rows], exchange, _params(1))


def _ssm_discretize(lam_re, lam_im, log_dt, b_re, b_im):
    dt = jnp.exp(log_dt)[:, None]
    mag = jnp.exp(lam_re * dt)
    ab_re = mag * jnp.cos(lam_im * dt)
    ab_im = mag * jnp.sin(lam_im * dt)
    nr = ab_re - 1.0
    den = lam_re * lam_re + lam_im * lam_im
    f_re = (nr * lam_re + ab_im * lam_im) / den
    f_im = (ab_im * lam_re - nr * lam_im) / den
    bb_re = f_re[..., None] * b_re - f_im[..., None] * b_im
    bb_im = f_re[..., None] * b_im + f_im[..., None] * b_re
    return ab_re, ab_im, bb_re, bb_im


def _state_layout(re, im):
    lead = re.shape[:-2]
    z = jnp.stack([re, im], axis=-3).reshape(lead + (2, N_SUPER, GROUPS_PER_SUPER, SSM_STATE))
    return jnp.moveaxis(z, -4, -3).reshape(lead + (STATE_COLS,))


def _state_unlayout(vec):
    z = vec.reshape(N_SUPER, 2, GROUPS_PER_SUPER, SSM_STATE).transpose(1, 0, 2, 3)
    z = z.reshape(2, SSM_GROUPS, SSM_STATE)
    return z[0], z[1]


SEG = 4
WINDOW = SEG * SUBLANES


def _scan_tables(ab_re, ab_im):
    pw = [None, (ab_re, ab_im)]
    for _ in range(2, WINDOW + 1):
        pr, pi_ = pw[-1]
        pw.append((pr * ab_re - pi_ * ab_im, pr * ab_im + pi_ * ab_re))
    fwd = np.zeros((7, SUBLANES), np.int64)
    bwd = np.zeros((7, SUBLANES), np.int64)
    for k, shift in enumerate((1, 2, 4)):
        fwd[k] = [SEG * shift if r >= shift else 0 for r in range(SUBLANES)]
        bwd[k] = [SEG * shift if r < SUBLANES - shift else 0 for r in range(SUBLANES)]
    fwd[3] = [SEG * (r + 1) for r in range(SUBLANES)]
    bwd[3] = [SEG * (SUBLANES - r) for r in range(SUBLANES)]
    for k in range(1, SEG):
        fwd[3 + k] = bwd[3 + k] = k
    used = sorted((set(fwd.ravel()) | set(bwd.ravel())) - {0})
    select = lambda which: np.stack([(which == p) for p in used], axis=-1).astype(np.float32)
    stacked = _state_layout(jnp.stack([pw[p][0] for p in used]), jnp.stack([pw[p][1] for p in used]))
    conj_sign = np.where((np.arange(STATE_COLS) // SUPER_HALF) % 2 == 1, -1.0, 1.0).astype(np.float32)
    pick = functools.partial(jnp.einsum, 'krp,pc->krc', precision=lax.Precision.HIGHEST)
    return pick(select(fwd), stacked), pick(select(bwd), stacked) * conj_sign


_EYE = np.eye(GROUPS_PER_SUPER, dtype=np.float32)


def _b_matrix(bb_re, bb_im):
    bb = jnp.stack([bb_re, bb_im]).reshape(2, N_SUPER, GROUPS_PER_SUPER, SSM_STATE, SSM_GROUP)
    m = jnp.einsum('rsgpc,gh->sgcrhp', bb, _EYE)
    return m.reshape(N_SUPER, SUPER_IN, SUPER_W)


def _b_matrix_grad(dm):
    d = dm.reshape(N_SUPER, GROUPS_PER_SUPER, SSM_GROUP, 2, GROUPS_PER_SUPER, SSM_STATE)
    d = jnp.sum(d * _EYE[None, :, None, None, :, None], axis=4)
    d = d.transpose(3, 0, 1, 4, 2).reshape(2, SSM_GROUPS, SSM_STATE, SSM_GROUP)
    return d[0], d[1]


def _c_matrix(c_re, c_im):
    cc = jnp.stack([c_re, -c_im]).reshape(2, N_SUPER, GROUPS_PER_SUPER, SSM_GROUP, SSM_STATE)
    m = jnp.einsum('rsgcp,gh->srgphc', cc, _EYE)
    return m.reshape(N_SUPER, SUPER_W, SUPER_IN)


def _c_matrix_grad(dm):
    d = dm.reshape(N_SUPER, 2, GROUPS_PER_SUPER, SSM_STATE, GROUPS_PER_SUPER, SSM_GROUP)
    d = jnp.sum(d * _EYE[None, None, :, None, :, None], axis=4)
    d = d.transpose(1, 0, 2, 4, 3).reshape(2, SSM_GROUPS, SSM_GROUP, SSM_STATE)
    return d[0], -d[1]


def _cmul_add(xr, xi, ar, ai, sr, si):
    return xr + ar * sr - ai * si, xi + ar * si + ai * sr


def _scan_rows(buf_ref, tab_ref, carry_ref, n_windows, reverse, h_ref=None, da_ref=None):
    order = list(range(SEG - 1, -1, -1)) if reverse else list(range(SEG))
    near = SUBLANES - 1 if reverse else 0
    far = 0 if reverse else SUBLANES - 1
    s_in = SUBLANES - 1 if reverse else 1
    lanes = lambda tile: pl.ds(tile * LANES, LANES)

    def window(w0, tile_re, tile_im, c_re, c_im, acc):
        rows = lambda t: pl.ds(w0 + t, SUBLANES, stride=SEG)
        get = lambda ref, t: (ref.at[tile_re][rows(t), :], ref.at[tile_im][rows(t), :])
        tab = lambda k: (tab_ref[k, :, lanes(tile_re)], tab_ref[k, :, lanes(tile_im)])

        def put(t, xr, xi):
            buf_ref.at[tile_re][rows(t), :] = xr
            buf_ref.at[tile_im][rows(t), :] = xi

        a1 = tab(4)
        er, ei = get(buf_ref, order[0])
        for t in order[1:]:
            er, ei = _cmul_add(*get(buf_ref, t), *a1, er, ei)
            if t != order[-1]:
                put(t, er, ei)
        for k, shift in enumerate((1, 2, 4)):
            s = (SUBLANES - shift) if reverse else shift
            er, ei = _cmul_add(er, ei, *tab(k), pltpu.roll(er, s, 0), pltpu.roll(ei, s, 0))
        er, ei = _cmul_add(er, ei, *tab(3), c_re, c_im)
        put(order[-1], er, ei)
        sub = lax.broadcasted_iota(jnp.int32, er.shape, 0)
        in_re = jnp.where(sub == near, c_re, pltpu.roll(er, s_in, 0))
        in_im = jnp.where(sub == near, c_im, pltpu.roll(ei, s_in, 0))
        true = {order[-1]: (er, ei)}
        for idx, t in enumerate(order[:-1]):
            true[t] = _cmul_add(*get(buf_ref, t), *tab(4 + idx), in_re, in_im)
            put(t, *true[t])
        carry = (jnp.broadcast_to(er[far:far + 1], er.shape), jnp.broadcast_to(ei[far:far + 1], ei.shape))
        if acc is None:
            return carry, None
        acc_re, acc_im = acc
        for t in range(SEG):
            if t + 1 < SEG:
                gr, gim = true[t + 1]
            else:
                gr = jnp.where(sub == SUBLANES - 1, c_re, pltpu.roll(true[0][0], SUBLANES - 1, 0))
                gim = jnp.where(sub == SUBLANES - 1, c_im, pltpu.roll(true[0][1], SUBLANES - 1, 0))
            hr, hi = get(h_ref, t)
            acc_re = acc_re + gr * hr + gim * hi
            acc_im = acc_im + gim * hr - gr * hi
        return carry, (acc_re, acc_im)

    half = SUPER_HALF // LANES
    per = 2 if h_ref is None else 4
    for sb in range(N_SUPER):
        pairs = [(2 * half * sb + j, 2 * half * sb + half + j) for j in range(half)]

        def step(wi, state, pairs=pairs):
            w = (n_windows - 1 - wi) if reverse else wi
            w0 = pl.multiple_of(w * WINDOW, WINDOW)
            out = []
            for j, (tile_re, tile_im) in enumerate(pairs):
                mine = state[per * j:per * (j + 1)]
                carry, acc = window(w0, tile_re, tile_im, mine[0], mine[1], mine[2:] or None)
                out += list(carry) + list(acc or ())
            return tuple(out)

        init = []
        for tile_re, tile_im in pairs:
            init += [carry_ref[:, lanes(tile_re)], carry_ref[:, lanes(tile_im)]]
            if h_ref is not None:
                init += [da_ref[:, lanes(tile_re)], da_ref[:, lanes(tile_im)]]
        fin = lax.fori_loop(0, n_windows, step, tuple(init))
        for j, (tile_re, tile_im) in enumerate(pairs):
            carry_ref[:, lanes(tile_re)] = fin[per * j]
            carry_ref[:, lanes(tile_im)] = fin[per * j + 1]
            if h_ref is not None:
                da_ref[:, lanes(tile_re)] = fin[per * j + 2]
                da_ref[:, lanes(tile_im)] = fin[per * j + 3]


def _put_tiles(ref, sb, block):
    for j in range(SUPER_TILES):
        ref[sb * SUPER_TILES + j] = block[:, j * LANES:(j + 1) * LANES]


def _get_tiles(ref, sb):
    return jnp.concatenate([ref[sb * SUPER_TILES + j] for j in range(SUPER_TILES)], axis=1)


def _ssm_fwd(u, bmat, cmat, tab, d_skip, tb, exchange=None):
    seq = u.shape[0]

    def body(u_ref, b_ref, c_ref, t_ref, d_ref, s_ref, h_ref, carry_ref):
        @pl.when(pl.program_id(0) == 0)
        def _():
            carry_ref[...] = jnp.zeros_like(carry_ref)

        u_blk = u_ref[...]
        ub = _bf(u_blk)
        for sb in range(N_SUPER):
            _put_tiles(h_ref, sb, _mm(ub[:, sb * SUPER_IN:(sb + 1) * SUPER_IN], b_ref[sb]))
        _scan_rows(h_ref, t_ref, carry_ref, tb // WINDOW, False)
        ys = [_mm(_bf(_get_tiles(h_ref, sb)), c_ref[sb]) for sb in range(N_SUPER)]
        s_ref[...] = jnp.concatenate(ys, axis=1) + d_ref[...] * u_blk

    return _rowcall("ssm_fwd", body, seq, tb, [u], [bmat, cmat, tab, d_skip],
                    [(SSM_W, F32), ((STATE_TILES, LANES), F32)], [],
                    scratch=[pltpu.VMEM((SUBLANES, STATE_COLS), F32)], vmem=VMEM_BIG, exchange=exchange)


def _ssm_bwd(ds, u, h, bmat_t, cmat_t, tab, d_skip, tb, exchange=None):
    seq = u.shape[0]

    def body(ds_ref, u_ref, h_ref, bt_ref, ct_ref, t_ref, d_ref,
             du_ref, db_ref, dc_ref, da_ref, dd_ref, g_ref, carry_ref):
        @pl.when(pl.program_id(0) == 0)
        def _():
            carry_ref[...] = jnp.zeros_like(carry_ref)
            db_ref[...] = jnp.zeros_like(db_ref)
            dc_ref[...] = jnp.zeros_like(dc_ref)
            da_ref[...] = jnp.zeros_like(da_ref)
            dd_ref[...] = jnp.zeros_like(dd_ref)

        ds_blk = ds_ref[...]
        dsb = _bf(ds_blk)
        u_blk = u_ref[...]
        ub = _bf(u_blk)
        for sb in range(N_SUPER):
            _put_tiles(g_ref, sb, _mm(dsb[:, sb * SUPER_IN:(sb + 1) * SUPER_IN], ct_ref[sb]))
        _scan_rows(g_ref, t_ref, carry_ref, tb // WINDOW, True, h_ref=h_ref, da_ref=da_ref)
        dus = []
        for sb in range(N_SUPER):
            gb = _bf(_get_tiles(g_ref, sb))
            dus.append(_mm(gb, bt_ref[sb]))
            db_ref[sb] += _mm_tn(ub[:, sb * SUPER_IN:(sb + 1) * SUPER_IN], gb)
            dc_ref[sb] += _mm_tn(_bf(_get_tiles(h_ref, sb)), dsb[:, sb * SUPER_IN:(sb + 1) * SUPER_IN])
        du_ref[...] = jnp.concatenate(dus, axis=1) + d_ref[...] * ds_blk
        dd_ref[...] += jnp.sum(ds_blk * u_blk, axis=0, keepdims=True)

    return _rowcall("ssm_bwd", body, seq, tb, [ds, u, h], [bmat_t, cmat_t, tab, d_skip],
                    [(SSM_W, F32)],
                    [((N_SUPER, SUPER_IN, SUPER_W), F32), ((N_SUPER, SUPER_W, SUPER_IN), F32),
                     ((SUBLANES, STATE_COLS), F32), ((1, SSM_W), F32)],
                    scratch=[pltpu.VMEM((STATE_TILES, tb, LANES), F32), pltpu.VMEM((SUBLANES, STATE_COLS), F32)],
                    reverse=True, vmem=VMEM_BIG, exchange=exchange)


def _merge_core(s, attb, ga, gs, wg_ref, wab_ref, wsb_ref, wout_ref):
    zg, dgelu = _gelu_and_grad(s)
    zgb = _bf(zg)
    sg = _sig(_mm(zgb, wg_ref[...]))
    z = zg * sg
    zb = _bf(z)
    ys = jnp.concatenate([_mm(zb, wsb_ref[j]) for j in range(N_CHIPS)], axis=1)
    ya = jnp.concatenate([_mm(attb, wab_ref[j]) for j in range(N_CHIPS)], axis=1)
    sa = _sig(ga)
    ss = _sig(gs)
    mgb = _bf(sa * ya + ss * ys)
    o = _mm(mgb, wout_ref[...])
    return dict(zg=zg, dgelu=dgelu, zgb=zgb, sg=sg, zb=zb, ys=ys, ya=ya, sa=sa, ss=ss, mgb=mgb, o=o)


def _merge_fwd(x, s, att, ga, gs, g2, w_glu, w_ab, w_sb, w_out, tb, exchange=None):
    seq = x.shape[0]

    def body(x_ref, s_ref, att_ref, ga_ref, gs_ref, g_ref, wg_ref, wab_ref, wsb_ref, wout_ref, x2_ref):
        f = _merge_core(s_ref[...], att_ref[...], ga_ref[...], gs_ref[...], wg_ref, wab_ref, wsb_ref, wout_ref)
        n, _, _ = _rms(f["o"], g_ref[...])
        x2_ref[...] = x_ref[...] + n

    return _rowcall("merge_fwd", body, seq, tb, [x, s, att, ga, gs], [g2, w_glu, w_ab, w_sb, w_out],
                    [(D_MODEL, F32)], [], vmem=VMEM_BIG, exchange=exchange)[0]


def _merge_bwd(dx2, s, att, ga, gs, g2, w_glu, w_ab, w_sb, w_out, tb, exchange=None):
    seq = s.shape[0]
    cw = D_MODEL // N_CHIPS
    last = seq // tb - 1

    def body(dx2_ref, s_ref, att_ref, ga_ref, gs_ref, g_ref, wg_ref, wab_ref, wsb_ref, wout_ref,
             ds_ref, datt_ref, dga_ref, dgs_ref, dg_ref, dwg_ref, dwab_ref, dwsb_ref, dwout_ref,
             bwg_ref, bwab_ref, bwsb_ref, bwout_ref):
        @pl.when(pl.program_id(0) == 0)
        def _():
            for r in (dg_ref, dwg_ref, dwab_ref, dwsb_ref, dwout_ref):
                r[...] = jnp.zeros_like(r)

        attb = att_ref[...]
        f = _merge_core(s_ref[...], attb, ga_ref[...], gs_ref[...], wg_ref, wab_ref, wsb_ref, wout_ref)
        g = g_ref[...]
        _, oh, r2 = _rms(f["o"], g)
        do, dg = _rms_bwd(dx2_ref[...], oh, r2, g)
        dg_ref[...] += dg
        dob = _bf(do)
        dwout_ref[...] += _mm_tn(f["mgb"], dob)
        dmg = _mm_nt(dob, wout_ref[...])
        sa, ss = f["sa"], f["ss"]
        dyab = _bf(dmg * sa)
        dysb = _bf(dmg * ss)
        dga_ref[...] = _bf(dmg * f["ya"] * sa * (1.0 - sa))
        dgs_ref[...] = _bf(dmg * f["ys"] * ss * (1.0 - ss))
        dwab = _mm_tn(attb, dyab)
        dwsb = _mm_tn(f["zb"], dysb)
        datt = jnp.zeros((tb, ATTN_W), F32)
        dz = jnp.zeros((tb, SSM_W), F32)
        for j in range(N_CHIPS):
            dwab_ref[j] += dwab[:, j * cw:(j + 1) * cw]
            dwsb_ref[j] += dwsb[:, j * cw:(j + 1) * cw]
            datt = datt + _mm_nt(dyab[:, j * cw:(j + 1) * cw], wab_ref[j])
            dz = dz + _mm_nt(dysb[:, j * cw:(j + 1) * cw], wsb_ref[j])
        datt_ref[...] = _bf(datt)
        sg, zg = f["sg"], f["zg"]
        dglb = _bf(dz * zg * sg * (1.0 - sg))
        dwg_ref[...] += _mm_tn(f["zgb"], dglb)
        dzg = dz * sg + _mm_nt(dglb, wg_ref[...])
        ds_ref[...] = dzg * f["dgelu"]

        @pl.when(pl.program_id(0) == last)
        def _():
            for dst, src in ((bwg_ref, dwg_ref), (bwab_ref, dwab_ref), (bwsb_ref, dwsb_ref), (bwout_ref, dwout_ref)):
                dst[...] = _bf(src[...])

    shapes = [w_glu.shape, w_ab.shape, w_sb.shape, w_out.shape]
    return _rowcall("merge_bwd", body, seq, tb, [dx2, s, att, ga, gs], [g2, w_glu, w_ab, w_sb, w_out],
                    [(SSM_W, F32), (ATTN_W, BF16), (D_MODEL, BF16), (D_MODEL, BF16)],
                    [((1, D_MODEL), F32)] + [(sh, F32) for sh in shapes] + [(sh, BF16) for sh in shapes],
                    vmem=VMEM_BIG, exchange=exchange)


def _mlp_fwd_loss(x2, target, g3, g4, w_ffi, w_ffo, tb):
    seq = x2.shape[0]
    n_slab = len(w_ffi)
    sw = D_FF // FF_CHUNKS // n_slab

    def body(x2_ref, t_ref, g3_ref, g4_ref, *rest):
        wi_refs, (wo_ref, dy_ref, df_ref, h_ref, ra_ref, loss_ref, dg_ref) = rest[:n_slab], rest[n_slab:]

        @pl.when(pl.program_id(0) == 0)
        def _():
            loss_ref[...] = jnp.zeros_like(loss_ref)
            dg_ref[...] = jnp.zeros_like(dg_ref)

        x2_blk = x2_ref[...]
        h3, _, _ = _rms(x2_blk, g3_ref[...])
        hb = _bf(h3)
        h_ref[...] = hb
        f = jnp.zeros((tb, D_MODEL), F32)
        for j in range(FF_CHUNKS):
            for k in range(n_slab):
                ra = jnp.maximum(_mm(hb, wi_refs[k][j]), 0.0)
                ra_ref[:, pl.ds((j * n_slab + k) * sw, sw)] = _bf(ra)
                f = f + _mm(_bf(ra * ra), wo_ref[j, pl.ds(k * sw, sw), :])
        g4 = g4_ref[...]
        n4, fh, r4 = _rms(f, g4)
        e = (x2_blk + n4) - t_ref[...]
        loss_ref[...] += 0.5 * jnp.sum(jnp.mean(e * e, axis=-1, keepdims=True))
        dy = e * (1.0 / D_MODEL)
        dy_ref[...] = dy
        df, dg = _rms_bwd(dy, fh, r4, g4)
        df_ref[...] = _bf(df)
        dg_ref[...] += dg

    return _rowcall("mlp_fwd_loss", body, seq, tb, [x2, target], [g3, g4, *w_ffi, w_ffo],
                    [(D_MODEL, F32), (D_MODEL, BF16), (D_MODEL, BF16), (D_FF, BF16)],
                    [((SUBLANES, 128), F32), ((1, D_MODEL), F32)], vmem=VMEM_BIG)


def _mlp_bwd(x2, dy, df, ra, g3, w_ffi, w_ffo, tb):
    seq = x2.shape[0]
    n_slab = len(w_ffi)
    sw = D_FF // FF_CHUNKS // n_slab

    def body(x2_ref, dy_ref, df_ref, ra_ref, g3_ref, *rest):
        wi_refs, (wo_ref, dx_ref, da_ref, dg_ref) = rest[:n_slab], rest[n_slab:]

        @pl.when(pl.program_id(0) == 0)
        def _():
            dg_ref[...] = jnp.zeros_like(dg_ref)

        dfb = df_ref[...]
        dh = jnp.zeros((tb, D_MODEL), F32)
        for j in range(FF_CHUNKS):
            for k in range(n_slab):
                cols = pl.ds((j * n_slab + k) * sw, sw)
                ra = ra_ref[:, cols].astype(F32)
                dab = _bf(_mm_nt(dfb, wo_ref[j, pl.ds(k * sw, sw), :]) * (2.0 * ra))
                da_ref[:, cols] = dab
                dh = dh + _mm_nt(dab, wi_refs[k][j])
        g3 = g3_ref[...]
        _, xh, r3 = _rms(x2_ref[...], g3)
        dxn, dg = _rms_bwd(dh, xh, r3, g3)
        dx_ref[...] = dy_ref[...] + dxn
        dg_ref[...] += dg

    return _rowcall("mlp_bwd", body, seq, tb, [x2, dy, df, ra], [g3, *w_ffi, w_ffo],
                    [(D_MODEL, F32), (D_FF, BF16)], [((1, D_MODEL), F32)], vmem=VMEM_BIG)


def _matmul_tn(name, a, b, tk, tn, tl, chunk_major, exchange=None, square_a=False):
    seq, kdim = a.shape
    ndim = b.shape[1]
    last = seq // tl - 1

    def body(a_ref, b_ref, o_ref, ob_ref):
        @pl.when(pl.program_id(2) == 0)
        def _():
            o_ref[...] = jnp.zeros_like(o_ref)

        a_blk = a_ref[...]
        if square_a:
            a_blk = _bf(jnp.square(a_blk.astype(F32)))
        o_ref[...] += _mm_tn(a_blk, b_ref[...])

        @pl.when(pl.program_id(2) == last)
        def _():
            ob_ref[...] = _bf(o_ref[...])

    if chunk_major:
        shape = (ndim // tn, kdim, tn)
        out_spec = pl.BlockSpec((None, tk, tn), lambda k, n, l: (n, k, 0))
    else:
        shape = (kdim, ndim)
        out_spec = pl.BlockSpec((tk, tn), lambda k, n, l: (k, n))
    return _fused_call(
        name, body, (kdim // tk, ndim // tn, seq // tl),
        [pl.BlockSpec((tl, tk), lambda k, n, l: (l, k)), pl.BlockSpec((tl, tn), lambda k, n, l: (l, n))],
        [out_spec, out_spec], [SDS(shape, F32), SDS(shape, BF16)], [], [a, b], exchange, _params(3, VMEM_BIG))


def _ew_call(name, fn, ins, n_out, after=None):
    rows, cols = ins[0].shape
    tr = rows
    while tr * cols * 4 > min(1 << 20, (9 << 20) // (len(ins) + n_out)) and tr % 16 == 0:
        tr //= 2
    spec = pl.BlockSpec((tr, cols), lambda i: (i, 0))
    extra = [] if after is None else [after]

    def body(*refs):
        outs = fn(*[r[...] for r in refs[:len(ins)]])
        for r, o in zip(refs[len(ins) + len(extra):], outs):
            r[...] = o

    return pl.pallas_call(
        body, grid=(rows // tr,), in_specs=[spec] * len(ins) + [ANY] * len(extra), out_specs=[spec] * n_out,
        out_shape=[SDS((rows, cols), F32)] * n_out, name=name, compiler_params=_params(1))(*ins, *extra)


def _adam_math(w, g, m, v):
    m2 = ADAM_B1 * m + (1.0 - ADAM_B1) * g
    v2 = ADAM_B2 * v + (1.0 - ADAM_B2) * (g * g)
    m_hat = m2 / (1.0 - ADAM_B1 ** ADAM_STEP)
    v_hat = v2 / (1.0 - ADAM_B2 ** ADAM_STEP)
    delta = -ADAM_LR * (m_hat / (jnp.sqrt(v_hat) + ADAM_EPS) + ADAM_WD * w)
    return delta, m2, v2


def _sum4(name, own, recv, idx):
    _, rows, cols = own.shape
    tr = rows
    while tr * cols * 4 > (1 << 20) and tr % 16 == 0:
        tr //= 2

    def body(idx_ref, o_ref, r0_ref, r1_ref, r2_ref, out_ref):
        out_ref[...] = ((o_ref[...] + r0_ref[...].astype(F32)) + r1_ref[...].astype(F32)) + r2_ref[...].astype(F32)

    blk = (None, tr, cols)
    grid_spec = pltpu.PrefetchScalarGridSpec(
        num_scalar_prefetch=1, grid=(rows // tr,),
        in_specs=[pl.BlockSpec(blk, lambda i, s: (s[0], i, 0)), pl.BlockSpec(blk, lambda i, s: (0, i, 0)),
                  pl.BlockSpec(blk, lambda i, s: (1, i, 0)), pl.BlockSpec(blk, lambda i, s: (2, i, 0))],
        out_specs=pl.BlockSpec((tr, cols), lambda i, s: (i, 0)))
    return pl.pallas_call(body, grid_spec=grid_spec, out_shape=SDS((rows, cols), F32), name=name,
                          compiler_params=_params(1))(jnp.reshape(idx, (1,)).astype(jnp.int32), own, recv, recv, recv)


def _adam_pair(name, item, after=None):
    def fn(w_, a, b, m_, v_):
        g = a + b
        return (g,) + _adam_math(w_, g, m_, v_)

    return _ew_call(name, fn, list(item), 4, after)


def _place():
    return lax.axis_index("x"), lax.axis_index("y"), lax.axis_index("c")


def _other_chips(x, y):
    return [(1 - x, y), (x, 1 - y), (1 - x, 1 - y)]


HBM = pl.BlockSpec(memory_space=pltpu.HBM)
SEM = pl.BlockSpec(memory_space=pltpu.SEMAPHORE)
DATAFLOW = pltpu.SideEffectType.DATAFLOW_SIDE_EFFECTING


class _Flight:
    def __init__(self, copies, n_copies, send, recv, srcs, lands, token):
        self.copies, self.n, self.send, self.recv = copies, n_copies, send, recv
        self.srcs, self.lands, self.token = list(srcs), list(lands), token


def _take_off(name, srcs, lands, copies, n_copies, after):
    n_s, n_l = len(srcs), len(lands)

    def body(*refs):
        src, land = refs[:n_s], refs[n_s:n_s + n_l]
        send, recv = refs[n_s + n_l + 1:n_s + n_l + 3]
        for cp in copies(src, land, send, recv):
            cp.start()
        refs[-1][...] = jnp.zeros_like(refs[-1])

    mem = lambda t: pltpu.HBM(t.shape, t.dtype)
    sems = pltpu.SemaphoreType.DMA((n_copies,))
    outs = pl.pallas_call(
        body, name=name,
        out_shape=(sems, sems, *map(mem, srcs), *map(mem, lands), SDS((SUBLANES, LANES), F32)),
        in_specs=[HBM] * (n_s + n_l) + [ANY],
        out_specs=(SEM, SEM, *[HBM] * (n_s + n_l), pl.BlockSpec(memory_space=pltpu.VMEM)),
        input_output_aliases={i: 2 + i for i in range(n_s + n_l)},
        compiler_params=pltpu.CompilerParams(has_side_effects=DATAFLOW),
    )(*[pltpu.with_memory_space_constraint(t, pltpu.HBM) for t in (*srcs, *lands)], after)
    return _Flight(copies, n_copies, outs[0], outs[1], outs[2:2 + n_s], outs[2 + n_s:2 + n_s + n_l], outs[-1])


def _land(name, flight, after):
    n_s, n_l = len(flight.srcs), len(flight.lands)

    def body(*refs):
        src, land = refs[:n_s], refs[n_s:n_s + n_l]
        send, recv = refs[n_s + n_l:n_s + n_l + 2]
        for cp in flight.copies(src, land, send, recv):
            cp.wait_send()
            cp.wait_recv()

    mem = lambda t: pltpu.HBM(t.shape, t.dtype)
    outs = pl.pallas_call(
        body, name=name, out_shape=(*map(mem, flight.srcs), *map(mem, flight.lands)),
        in_specs=[HBM] * (n_s + n_l) + [SEM, SEM, ANY], out_specs=tuple([HBM] * (n_s + n_l)),
        input_output_aliases={i: i for i in range(n_s + n_l)},
        compiler_params=pltpu.CompilerParams(has_side_effects=DATAFLOW),
    )(*flight.srcs, *flight.lands, flight.send, flight.recv, after)
    return list(outs[:n_s]), list(outs[n_s:])


def _empty_like(shapes_from, lead):
    return [lax.empty((lead,) + t.shape[1:], t.dtype) for t in shapes_from]


def _scatter_off(name, chunks, after):
    def copies(src, land, send, recv):
        x, y, c = _place()
        return [pltpu.make_async_remote_copy(
            src_ref=src[a].at[2 * px + py], dst_ref=land[a].at[k], send_sem=send.at[3 * a + k],
            recv_sem=recv.at[3 * a + k], device_id=(px, py, c), device_id_type=MESH_ID)
            for a in range(len(chunks)) for k, (px, py) in enumerate(_other_chips(x, y))]

    return _take_off(name, chunks, _empty_like(chunks, 3), copies, 3 * len(chunks), after)


def _swap_off(name, arrs, after):
    def copies(src, land, send, recv):
        x, y, c = _place()
        return [pltpu.make_async_remote_copy(
            src_ref=src[a], dst_ref=land[a], send_sem=send.at[a], recv_sem=recv.at[a],
            device_id=(x, y, 1 - c), device_id_type=MESH_ID) for a in range(len(arrs))]

    return _take_off(name, arrs, [lax.empty(t.shape, t.dtype) for t in arrs], copies, len(arrs), after)


def _devices_off(name, block, after):
    me = 4 * lax.axis_index("x") + 2 * lax.axis_index("y") + lax.axis_index("c")
    land = lax.dynamic_update_index_in_dim(lax.empty((N_DEV,) + block.shape, block.dtype), block, me, 0)

    def copies(src, land, send, recv):
        x, y, c = _place()
        mine = 4 * x + 2 * y + c
        return [pltpu.make_async_remote_copy(
            src_ref=src[0], dst_ref=land[0].at[mine], send_sem=send.at[k - 1], recv_sem=recv.at[k - 1],
            device_id=(x ^ (k >> 2), y ^ ((k >> 1) & 1), c ^ (k & 1)), device_id_type=MESH_ID)
            for k in range(1, N_DEV)]

    return _take_off(name, [block], [land], copies, N_DEV - 1, after)


def _half_rows(shape, c, other=False):
    half = shape[0] // 2
    return pl.ds(((1 - c) if other else c) * half, half)


def _gather_start(name, shards, lands, after):
    n = len(shards)

    def body(*refs):
        src, land, (send, recv) = refs[:n], refs[n:2 * n], refs[2 * n + 1:2 * n + 3]
        x, y, c = _place()
        me = 2 * x + y
        for a in range(n):
            mine = _half_rows(shards[a].shape, c)
            for j, (px, py) in enumerate(_other_chips(x, y)):
                pltpu.make_async_remote_copy(
                    src_ref=src[a].at[mine], dst_ref=land[a].at[me, mine], send_sem=send.at[3 * a + j],
                    recv_sem=recv.at[3 * a + j], device_id=(px, py, c), device_id_type=MESH_ID).start()
        token = refs[-1]
        token[...] = jnp.zeros_like(token)

    mem = lambda t: pltpu.HBM(t.shape, t.dtype)
    pair = pltpu.SemaphoreType.DMA((3 * n,))
    outs = pl.pallas_call(
        body, name=name,
        out_shape=(pair, pair, *map(mem, shards), *map(mem, lands), SDS((SUBLANES, LANES), F32)),
        in_specs=[HBM] * (2 * n) + [ANY],
        out_specs=(SEM, SEM, *[HBM] * (2 * n), pl.BlockSpec(memory_space=pltpu.VMEM)),
        input_output_aliases={i: 2 + i for i in range(2 * n)},
        compiler_params=pltpu.CompilerParams(has_side_effects=DATAFLOW),
    )(*[pltpu.with_memory_space_constraint(t, pltpu.HBM) for t in (*shards, *lands)], after)
    return outs[0], outs[1], list(outs[2:2 + n]), list(outs[2 + n:2 + 2 * n]), outs[-1]


def _gather_pass(name, send, recv, shards, lands, after, first=0):
    n = len(shards)

    def body(*refs):
        src, land, (send, recv, _) = refs[:n], refs[n:2 * n], refs[2 * n:2 * n + 3]
        fsend, frecv = refs[2 * n + 3], refs[2 * n + 4]
        x, y, c = _place()
        me = 2 * x + y
        for a in range(n):
            mine = _half_rows(shards[a].shape, c)
            for j, (px, py) in enumerate(_other_chips(x, y)):
                far = 2 * px + py
                ici = pltpu.make_async_remote_copy(
                    src_ref=src[a].at[mine], dst_ref=land[a].at[far, mine], send_sem=send.at[3 * (first + a) + j],
                    recv_sem=recv.at[3 * (first + a) + j], device_id=(px, py, c), device_id_type=MESH_ID)
                ici.wait_recv()
                ici.wait_send()
                pltpu.make_async_remote_copy(
                    src_ref=land[a].at[far, mine], dst_ref=land[a].at[far, mine], send_sem=fsend.at[3 * a + j],
                    recv_sem=frecv.at[3 * a + j], device_id=(x, y, 1 - c), device_id_type=MESH_ID).start()
        token = refs[-1]
        token[...] = jnp.zeros_like(token)

    mem = lambda t: pltpu.HBM(t.shape, t.dtype)
    pair = pltpu.SemaphoreType.DMA((3 * n,))
    outs = pl.pallas_call(
        body, name=name,
        out_shape=(pair, pair, *map(mem, lands), SDS((SUBLANES, LANES), F32)),
        in_specs=[HBM] * (2 * n) + [SEM, SEM, ANY],
        out_specs=(SEM, SEM, *[HBM] * n, pl.BlockSpec(memory_space=pltpu.VMEM)),
        input_output_aliases={n + i: 2 + i for i in range(n)},
        compiler_params=pltpu.CompilerParams(has_side_effects=DATAFLOW),
    )(*shards, *lands, send, recv, after)
    return outs[0], outs[1], list(outs[2:2 + n]), outs[-1]


def _gather_wait(name, fsend, frecv, lands, after):
    n = len(lands)

    def body(*refs):
        land, (fsend, frecv, _) = refs[:n], refs[n:n + 3]
        x, y, c = _place()
        for a in range(n):
            for j, (px, py) in enumerate(_other_chips(x, y)):
                far = 2 * px + py
                mine = _half_rows(lands[a].shape[1:], c)
                theirs = _half_rows(lands[a].shape[1:], c, other=True)
                pltpu.make_async_remote_copy(
                    src_ref=land[a].at[far, mine], dst_ref=land[a].at[far, mine], send_sem=fsend.at[3 * a + j],
                    recv_sem=frecv.at[3 * a + j], device_id=(x, y, 1 - c), device_id_type=MESH_ID).wait_send()
                pltpu.make_async_remote_copy(
                    src_ref=land[a].at[far, theirs], dst_ref=land[a].at[far, theirs], send_sem=fsend.at[3 * a + j],
                    recv_sem=frecv.at[3 * a + j], device_id=(x, y, 1 - c), device_id_type=MESH_ID).wait_recv()

    mem = lambda t: pltpu.HBM(t.shape, t.dtype)
    return list(pl.pallas_call(
        body, name=name, out_shape=tuple(map(mem, lands)), in_specs=[HBM] * n + [SEM, SEM, ANY],
        out_specs=tuple([HBM] * n), input_output_aliases={i: i for i in range(n)},
        compiler_params=pltpu.CompilerParams(has_side_effects=DATAFLOW),
    )(*lands, fsend, frecv, after))


def _after(token):
    return _Exchange([token], [], [], lambda *_: None, lambda *_: None)


def _swap_sibling(arrs):
    n = len(arrs)

    def copies(ins, outs, sems):
        send, recv = sems
        x, y, c = _place()
        return [pltpu.make_async_remote_copy(
            src_ref=ins[a], dst_ref=outs[a], send_sem=send.at[a], recv_sem=recv.at[a],
            device_id=(x, y, 1 - c), device_id_type=MESH_ID) for a in range(n)]

    def start(ins, outs, sems):
        for cp in copies(ins, outs, sems):
            cp.start()

    def wait(ins, outs, sems):
        cps = copies(ins, outs, sems)
        for cp in cps:
            cp.wait_recv()
        for cp in cps:
            cp.wait_send()

    return _Exchange(arrs, [SDS(s.shape, s.dtype) for s in arrs],
                     [pltpu.SemaphoreType.DMA((n,)), pltpu.SemaphoreType.DMA((n,))], start, wait)


def _sum_devices(slots):
    def body(s_ref, o_ref):
        acc = s_ref[0]
        for d in range(1, N_DEV):
            acc = acc + s_ref[d]
        o_ref[...] = acc

    return pl.pallas_call(
        body, in_specs=[pl.BlockSpec(memory_space=pltpu.VMEM)], out_specs=pl.BlockSpec(memory_space=pltpu.VMEM),
        out_shape=SDS(slots.shape[1:], F32), name="sum_small",
        compiler_params=pltpu.CompilerParams(vmem_limit_bytes=32 * 1024 * 1024))(slots)


def _adam_small(ws, gs, ms, vs):
    n = len(ws)

    def body(*refs):
        for i in range(n):
            w_ref, g_ref, m_ref, v_ref = (refs[k * n + i] for k in range(4))
            outs = _adam_math(w_ref[...], g_ref[...], m_ref[...], v_ref[...])
            for k in range(3):
                refs[(4 + k) * n + i][...] = outs[k]

    vmem = pl.BlockSpec(memory_space=pltpu.VMEM)
    return pl.pallas_call(
        body, in_specs=[vmem] * (4 * n), out_specs=[vmem] * (3 * n),
        out_shape=[SDS(w.shape, F32) for w in ws] * 3, name="adam_small",
        compiler_params=pltpu.CompilerParams(vmem_limit_bytes=32 * 1024 * 1024))(*ws, *gs, *ms, *vs)


def _local_step(x, target, small, big, tb, distributed):
    dist = distributed
    me = (2 * lax.axis_index("x") + lax.axis_index("y")) if dist else 0
    tb_ssm = min(tb, 256)
    bucket = jnp.asarray(_bucket_table())
    place_own = lambda t: lax.dynamic_update_index_in_dim(lax.empty((N_CHIPS,) + t.shape, t.dtype), t, me, 0)
    if dist:
        in_legs = _gather_start("gather_in_start", [big["w_in"]], [place_own(big["w_in"])], small["d_skip"])
        names = sorted(small)
        in_token, values = lax.optimization_barrier((in_legs[4], [small[n] for n in names]))
        small = dict(zip(names, values))
    g1, g2, g3, g4 = small["norm_mix_pre"], small["norm_mix_post"], small["norm_mlp_pre"], small["norm_mlp_post"]

    keys_first = lambda t: jnp.swapaxes(t, -1, -2)
    bias = _bias_table(small["rel_bias"], bucket)
    sink_rows = keys_first(_pair_layout(jnp.broadcast_to(small["sinks"].reshape(N_HEADS, 1, 1), (N_HEADS, BLOCK, 1))))
    disc_args = (small["lam_re"], small["lam_im"], small["log_dt"], small["b_re"], small["b_im"])
    (ab_re, ab_im, bb_re, bb_im), disc_vjp = jax.vjp(_ssm_discretize, *disc_args)
    tab_f, tab_b = _scan_tables(ab_re, ab_im)
    bmat = _bf(_b_matrix(bb_re, bb_im))
    cmat = _bf(_c_matrix(small["c_re"], small["c_im"]))
    bmat_t, cmat_t = bmat.transpose(0, 2, 1), cmat.transpose(0, 2, 1)
    d_skip = small["d_skip"]

    mix = ("w_glu", "w_attn_branch", "w_ssm_branch", "w_out")
    rest = [big[n] for n in mix + ("w_ff_in", "w_ff_out")]
    if dist:
        send, recv, src, lands, _ = in_legs
        tab_f, tab_b, bias, sink_rows, bmat, cmat, bmat_t, cmat_t, rest, rest_lands = lax.optimization_barrier(
            (tab_f, tab_b, bias, sink_rows, bmat, cmat, bmat_t, cmat_t, rest, [place_own(t) for t in rest]))
        corner = lambda t: t.reshape(-1, t.shape[-1])[:1, :LANES].astype(F32)
        prepared = sum(map(corner, [tab_b, bias, sink_rows, bmat, cmat] + rest_lands), in_token[:1])
        send, recv, lands, in_passed = _gather_pass("gather_in_pass", send, recv, src, lands, prepared)
        (g_in,) = _gather_wait("gather_in_wait", send, recv, lands, in_passed)
        w_in = g_in.reshape(IN_W, D_MODEL)
    else:
        w_in = big["w_in"]
    token = None
    n_mix = len(mix)
    if dist:
        send, recv, rest, lands, token = _gather_start("gather_rest_start", rest, rest_lands, in_passed)
    h1, q, k, v, u, ga, gs = _inproj_fwd(x, g1, w_in, tb, _after(token) if dist else None)
    s, h = _ssm_fwd(u, bmat, cmat, tab_f, d_skip, tb)
    if dist:
        fsend, frecv, mix_lands, token = _gather_pass("gather_mix_pass", send, recv, rest[:n_mix], lands[:n_mix], s)
    att = _attn_fwd(q, k, v, bias, sink_rows, _after(token) if dist else None)[0]
    if dist:
        w_mix = _gather_wait("gather_mix_wait", fsend, frecv, mix_lands, att)
        fsend, frecv, ff_lands, token = _gather_pass(
            "gather_ff_pass", send, recv, rest[n_mix:], lands[n_mix:], w_mix[0], n_mix)
        rest = w_mix + ff_lands
    w_glu, w_ab, w_sb, w_out = rest[:n_mix]
    w_glu = w_glu.reshape(SSM_W, SSM_W)
    w_out = w_out.reshape(D_MODEL, D_MODEL)
    x2 = _merge_fwd(x, s, att, ga, gs, g2, w_glu, w_ab, w_sb, w_out, tb, _after(token) if dist else None)
    if dist:
        rest[n_mix:] = _gather_wait("gather_ff_wait", fsend, frecv, ff_lands, x2)
    w_ffi, w_ffo = [rest[n_mix]], rest[n_mix + 1]
    dy, df, h3, ra, loss_acc, dg4 = _mlp_fwd_loss(x2, target, g3, g4, w_ffi, w_ffo, tb)

    dx2, da, dg3 = _mlp_bwd(x2, dy, df, ra, g3, w_ffi, w_ffo, tb)
    tl = min(2048, x.shape[0])
    chunked = (N_CHIPS, D_FF // N_CHIPS, D_MODEL)
    d_ffi, b_ffi = _matmul_tn("grad_w_ff_in", h3, da, D_MODEL, D_FF // FF_CHUNKS, tl, True)
    d_ffo, b_ffo = _matmul_tn("grad_w_ff_out", ra, df, D_FF // FF_CHUNKS, D_MODEL, tl, False, square_a=True)
    d_ffo, b_ffo = d_ffo.reshape(chunked), b_ffo.reshape(chunked)
    behind = lambda flight: _after(flight.token) if dist else None
    ff_fl = _scatter_off("scatter_ff_off", [b_ffi, b_ffo], d_ffo) if dist else None
    outs = _merge_bwd(dx2, s, att, ga, gs, g2, w_glu, w_ab, w_sb, w_out, tb_ssm, behind(ff_fl))
    ds, datt, dga, dgs, dg2, d_glu, d_ab, d_sb, d_out, b_glu, b_ab, b_sb, b_out = outs
    glu4, out4 = (N_CHIPS, SSM_W // N_CHIPS, SSM_W), (N_CHIPS, D_MODEL // N_CHIPS, D_MODEL)
    d_mix = [d_glu.reshape(glu4), d_ab, d_sb, d_out.reshape(out4)]
    b_mix = [b_glu.reshape(glu4), b_ab, b_sb, b_out.reshape(out4)]
    mix_fl = _scatter_off("scatter_mix_off", b_mix, d_mix[-1]) if dist else None
    du, d_bmat, d_cmat, da_acc, dd_skip = _ssm_bwd(
        ds, u, h, bmat_t, cmat_t, tab_b, d_skip, tb, behind(mix_fl))
    dq, dk, dv, dbias, dsink_rows = _attn_bwd(q, k, v, datt, bias, sink_rows)
    swap_fl = None
    if dist:
        r_ffi, r_ffo = _land("scatter_ff_land", ff_fl, dq)[1]
        p_ffi = _sum4("sum_w_ff_in", d_ffi, r_ffi, me)
        p_ffo = _sum4("sum_w_ff_out", d_ffo, r_ffo, me)
        swap_fl = _swap_off("swap_ff_off", [p_ffi, p_ffo], r_ffo)
    dx, dpj, dg1 = _inproj_bwd(x, dx2, dq, dk, dv, du, dga, dgs, g1, w_in, tb, behind(swap_fl))

    dab_re, dab_im = _state_unlayout(jnp.sum(da_acc, axis=0))
    dbb_re, dbb_im = _b_matrix_grad(d_bmat)
    d_lam_re, d_lam_im, d_log_dt, d_b_re, d_b_im = disc_vjp((dab_re, dab_im, dbb_re, dbb_im))
    d_c_re, d_c_im = _c_matrix_grad(d_cmat)
    d_rel = _bias_grad(dbias, bucket)
    d_sinks = jnp.sum(_pair_unlayout(keys_first(dsink_rows)), axis=(1, 2))
    small_grads = dict(
        norm_mix_pre=dg1, norm_mix_post=dg2, norm_mlp_pre=dg3, norm_mlp_post=dg4, rel_bias=d_rel, sinks=d_sinks,
        lam_re=d_lam_re, lam_im=d_lam_im, log_dt=d_log_dt, b_re=d_b_re, b_im=d_b_im, c_re=d_c_re, c_im=d_c_im,
        d_skip=dd_skip)
    small_fl = _devices_off("small_off", _pack(small_grads, loss_acc), swap_fl.token) if dist else None
    outs = _matmul_tn("grad_w_in", dpj, h1, IN_W // 2, D_MODEL, tl, False, behind(small_fl))
    in4 = (N_CHIPS, IN_W // N_CHIPS, D_MODEL)
    d_in, b_in = outs[0].reshape(in4), outs[1].reshape(in4)
    if not dist:
        return loss_acc, dx, small_grads, dict(zip(BIG, [d_in] + d_mix + [d_ffi, d_ffo]))
    in_fl = _scatter_off("scatter_w_in_off", [b_in], d_in)
    (p_ffi, p_ffo), (s_ffi, s_ffo) = _land("swap_ff_land", swap_fl, in_fl.token)
    r_mix = _land("scatter_mix_land", mix_fl, in_fl.token)[1]
    p_mix = [_sum4("sum_" + n, d, r, me) for n, d, r in zip(mix, d_mix, r_mix)]
    mix_swap = _swap_off("swap_mix_off", p_mix, in_fl.token)
    pending = dict(d_in=d_in, in_fl=in_fl, mix_swap=mix_swap, w_ff_in=(p_ffi, s_ffi), w_ff_out=(p_ffo, s_ffo), me=me)
    return loss_acc, dx, small_fl, pending


SMALL = ['norm_mix_pre', 'norm_mix_post', 'norm_mlp_pre', 'norm_mlp_post', 'rel_bias', 'sinks', 'lam_re', 'lam_im',
         'log_dt', 'b_re', 'b_im', 'c_re', 'c_im', 'd_skip']
BIG = ['w_in', 'w_glu', 'w_attn_branch', 'w_ssm_branch', 'w_out', 'w_ff_in', 'w_ff_out']
WEIGHTS = ['norm_mix_pre', 'norm_mix_post', 'norm_mlp_pre', 'norm_mlp_post', 'w_in', 'rel_bias', 'sinks', 'lam_re',
           'lam_im', 'log_dt', 'b_re', 'b_im', 'c_re', 'c_im', 'd_skip', 'w_glu', 'w_attn_branch', 'w_ssm_branch',
           'w_out', 'w_ff_in', 'w_ff_out']
PACK_COLS = 1024
PACK_ORDER = ['b_re', 'b_im', 'c_re', 'c_im', 'lam_re', 'lam_im', 'norm_mix_pre', 'norm_mix_post', 'norm_mlp_pre',
              'norm_mlp_post', 'rel_bias', 'sinks', 'log_dt', 'd_skip']


STATE_MINOR = ('b_re', 'b_im')
PACK_ROWS = 144
LOSS_ROW = 140


def _pack(named, loss_acc):
    parts = []
    for n in PACK_ORDER:
        a = jnp.swapaxes(named[n], -1, -2) if n in STATE_MINOR else named[n]
        flat = a.reshape(-1)
        rows = -(-flat.shape[0] // PACK_COLS)
        parts.append(jnp.pad(flat, (0, rows * PACK_COLS - flat.shape[0])).reshape(rows, PACK_COLS))
    assert sum(p.shape[0] for p in parts) == LOSS_ROW
    parts.append(jnp.pad(loss_acc[0:1], ((0, PACK_ROWS - LOSS_ROW - 1), (0, PACK_COLS - loss_acc.shape[1]))))
    return jnp.concatenate(parts, axis=0)


def _unpack(packed, shapes):
    out, at = {}, 0
    for n in PACK_ORDER:
        shape = shapes[n][:-2] + (shapes[n][-1], shapes[n][-2]) if n in STATE_MINOR else shapes[n]
        size = int(np.prod(shape))
        rows = -(-size // PACK_COLS)
        blk = packed[at:at + rows]
        out[n] = (blk.reshape(-1)[:size] if size % PACK_COLS else blk).reshape(shape)
        at += rows
    return out


def kernel(x, norm_mix_pre, norm_mix_post, norm_mlp_pre, norm_mlp_post, w_in, rel_bias, sinks, lam_re, lam_im, log_dt, b_re, b_im, c_re, c_im, d_skip, w_glu, w_attn_branch, w_ssm_branch, w_out, w_ff_in, w_ff_out, loss_target, m_norm_mix_pre, m_norm_mix_post, m_norm_mlp_pre, m_norm_mlp_post, m_w_in, m_rel_bias, m_sinks, m_lam_re, m_lam_im, m_log_dt, m_b_re, m_b_im, m_c_re, m_c_im, m_d_skip, m_w_glu, m_w_attn_branch, m_w_ssm_branch, m_w_out, m_w_ff_in, m_w_ff_out, v_norm_mix_pre, v_norm_mix_post, v_norm_mlp_pre, v_norm_mlp_post, v_w_in, v_rel_bias, v_sinks, v_lam_re, v_lam_im, v_log_dt, v_b_re, v_b_im, v_c_re, v_c_im, v_d_skip, v_w_glu, v_w_attn_branch, v_w_ssm_branch, v_w_out, v_w_ff_in, v_w_ff_out):
    env = dict(locals())
    w = {n: env[n] for n in WEIGHTS}
    m = {n: env["m_" + n] for n in WEIGHTS}
    v = {n: env["v_" + n] for n in WEIGHTS}
    seq = x.shape[1]
    tb = min(512, seq)

    small = {n: w[n] for n in ('norm_mix_pre', 'norm_mix_post', 'norm_mlp_pre', 'norm_mlp_post', 'rel_bias')}
    small.update({n: w[n][0] for n in ('sinks', 'lam_re', 'lam_im', 'log_dt', 'b_re', 'b_im', 'c_re', 'c_im')})
    small['d_skip'] = w['d_skip']
    shard = lambda t, n: t[n][0].T if n == 'w_in' else t[n][0]
    unshard = lambda a, n: (a.T if n == 'w_in' else a)[None]
    _, dx, small_fl, pending = _local_step(
        x[0], loss_target[0], small, {n: _bf(shard(w, n)) for n in BIG}, tb, True)

    grads, deltas, new_m, new_v = {}, {}, {}, {}

    def adam(n, partials, after=None):
        outs = _adam_pair("adam_" + n, (shard(w, n), *partials, shard(m, n), shard(v, n)), after)
        grads[n], deltas[n], new_m[n], new_v[n] = [unshard(a, n) for a in outs]
        return outs[3]

    mix = ("w_glu", "w_attn_branch", "w_ssm_branch", "w_out")
    in_fl = pending["in_fl"]
    last = pending["mix_swap"].token
    for n in ("w_ff_in", "w_ff_out"):
        last = adam(n, pending[n], last)
    for n, partials in zip(mix, zip(*_land("swap_mix_land", pending["mix_swap"], last))):
        last = adam(n, partials, last)

    small_g = _sum_devices(_land("small_land", small_fl, last)[1][0])
    loss = small_g[LOSS_ROW, 0]
    minor = lambda t, n: jnp.swapaxes(t, -1, -2) if n in STATE_MINOR else t
    g_small = _unpack(small_g, {n: w[n].shape for n in SMALL})
    outs = _adam_small([minor(w[n], n) for n in SMALL], [g_small[n] for n in SMALL],
                       [minor(m[n], n) for n in SMALL], [minor(v[n], n) for n in SMALL])
    grads.update({n: minor(g_small[n], n) for n in SMALL})
    for k, dst in enumerate((deltas, new_m, new_v)):
        dst.update({n: minor(a, n) for n, a in zip(SMALL, outs[k * len(SMALL):(k + 1) * len(SMALL)])})

    (r_in,) = _land("scatter_w_in_land", in_fl, outs[0])[1]
    p_in = _sum4("sum_w_in", pending["d_in"], r_in, pending["me"])
    (s_in,) = _exchange_alone("swap_w_in", _swap_sibling([p_in]))
    adam("w_in", (p_in, s_in))

    return (loss, dx[None], *[grads[n] for n in WEIGHTS], *[deltas[n] for n in WEIGHTS],
            *[new_m[n] for n in WEIGHTS], *[new_v[n] for n in WEIGHTS])
```

```python
import functools
import math

import numpy as np
import jax
import jax.numpy as jnp
from jax import lax
from jax.experimental import pallas as pl
from jax.experimental.pallas import tpu as pltpu

F32 = jnp.float32
BF16 = jnp.bfloat16

D_MODEL = 1024
N_HEADS = 8
N_KV = 2
Q_GROUP = 4
HEAD_DIM = 64
ATTN_W = 512
KV_W = 128
BLOCK = 128
N_BUCKETS = 32
MAX_DISTANCE = 128
NEG_INF = -1e30
SSM_W = 512
SSM_GROUP = 16
SSM_GROUPS = 32
SSM_STATE = 64
N_SUPER = 4
GROUPS_PER_SUPER = SSM_GROUPS // N_SUPER
SUPER_IN = GROUPS_PER_SUPER * SSM_GROUP
SUPER_HALF = GROUPS_PER_SUPER * SSM_STATE
SUPER_W = 2 * SUPER_HALF
STATE_COLS = N_SUPER * SUPER_W
D_FF = 4096
FF_CHUNKS = 4
IN_W = 3328
SPLITS = (0, 512, 640, 768, 1280, 2304, 3328)
RMS_EPS = 1e-6
N_CHIPS = 4
N_DEV = 8
SUBLANES = 8
LANES = 128
STATE_TILES = STATE_COLS // LANES
SUPER_TILES = SUPER_W // LANES

ADAM_LR = 0.001
ADAM_B1 = 0.9
ADAM_B2 = 0.999
ADAM_EPS = 1e-08
ADAM_WD = 0.01
ADAM_STEP = 10

VMEM_BIG = 56 * 1024 * 1024
SDS = jax.ShapeDtypeStruct
MESH_ID = pl.DeviceIdType.MESH
ANY = pl.BlockSpec(memory_space=pl.ANY)


def _bf(x):
    return x.astype(BF16)


def _mm(a, b):
    return jnp.dot(a, b, preferred_element_type=F32)


def _mm_nt(a, b):
    return lax.dot_general(a, b, (((1,), (1,)), ((), ())), preferred_element_type=F32)


def _mm_tn(a, b):
    return lax.dot_general(a, b, (((0,), (0,)), ((), ())), preferred_element_type=F32)


def _sig(x):
    return 1.0 / (1.0 + jnp.exp(-x))


def _rms(x, g):
    r = lax.rsqrt(jnp.mean(x * x, axis=-1, keepdims=True) + RMS_EPS)
    xh = x * r
    return xh * g, xh, r


def _rms_bwd(dout, xh, r, g):
    dg = jnp.sum(dout * xh, axis=0, keepdims=True)
    dxh = dout * g
    dx = r * (dxh - xh * jnp.mean(dxh * xh, axis=-1, keepdims=True))
    return dx, dg


_GELU_C = math.sqrt(2.0 / math.pi)


def _gelu_and_grad(x):
    x2 = x * x
    inner = _GELU_C * (x + 0.044715 * (x2 * x))
    t = jnp.tanh(inner)
    y = 0.5 * x * (1.0 + t)
    dy = 0.5 * (1.0 + t) + 0.5 * x * (1.0 - t * t) * (_GELU_C * (1.0 + 3.0 * 0.044715 * x2))
    return y, dy


def _zero_map(nd, *_):
    return (0,) * nd


def _params(n_axes, vmem=None):
    return pltpu.CompilerParams(dimension_semantics=("arbitrary",) * n_axes, vmem_limit_bytes=vmem)


class _Exchange:
    def __init__(self, ins, outs, sems, start, wait):
        self.ins, self.outs, self.sems, self.start, self.wait = list(ins), list(outs), list(sems), start, wait


def _fused_call(name, body, grid, in_specs, out_specs, out_shape, scratch, args, exchange, params):
    n_in, n_out, n_scr = len(in_specs), len(out_specs), len(scratch)
    if exchange is None:
        fn = body
    else:
        ex = exchange
        n_xi, n_xo = len(ex.ins), len(ex.outs)

        def fn(*refs):
            at = 0
            parts = []
            for n in (n_in, n_xi, n_out, n_xo, n_scr, len(ex.sems)):
                parts.append(refs[at:at + n])
                at += n
            ins, x_in, outs, x_out, scr, x_sem = parts
            ids = [pl.program_id(a) for a in range(len(grid))]
            first = functools.reduce(jnp.logical_and, [i == 0 for i in ids])
            last = functools.reduce(jnp.logical_and, [i == g - 1 for i, g in zip(ids, grid)])

            @pl.when(first)
            def _():
                ex.start(x_in, x_out, x_sem)

            body(*ins, *outs, *scr)

            @pl.when(last)
            def _():
                ex.wait(x_in, x_out, x_sem)

        in_specs = list(in_specs) + [ANY] * n_xi
        out_specs = list(out_specs) + [ANY] * n_xo
        out_shape = list(out_shape) + ex.outs
        scratch = list(scratch) + ex.sems
        args = list(args) + ex.ins
    return pl.pallas_call(fn, grid=grid, in_specs=in_specs, out_specs=out_specs, out_shape=out_shape,
                          scratch_shapes=list(scratch), name=name, compiler_params=params)(*args)


def _rowcall(name, body, seq, tb, rows, consts, row_outs, acc_outs, scratch=(), reverse=False, vmem=None,
             exchange=None):
    nb = seq // tb
    rmap = (lambda i: (nb - 1 - i, 0)) if reverse else (lambda i: (i, 0))
    tmap = lambda i: (0,) + rmap(i)

    def row_spec(width):
        if isinstance(width, tuple):
            return pl.BlockSpec((width[0], tb, width[1]), tmap)
        return pl.BlockSpec((tb, width), rmap)

    def row_shape(width):
        return (width[0], seq, width[1]) if isinstance(width, tuple) else (seq, width)

    in_specs = [row_spec(a.shape[1] if a.ndim == 2 else (a.shape[0], a.shape[2])) for a in rows]
    in_specs += [pl.BlockSpec(a.shape, functools.partial(_zero_map, a.ndim), pipeline_mode=pl.Buffered(1))
                 for a in consts]
    out_specs = [row_spec(c) for c, _ in row_outs] + [ANY] * len(acc_outs)
    out_shape = [SDS(row_shape(c), dt) for c, dt in row_outs] + [SDS(s, dt) for s, dt in acc_outs]
    n_main = len(rows) + len(consts) + len(row_outs)
    n_acc = len(acc_outs)

    def fn(*refs):
        main, acc_hbm, rest = refs[:n_main], refs[n_main:n_main + n_acc], refs[n_main + n_acc:]
        acc_vmem, own = rest[:n_acc], rest[n_acc:]
        body(*main, *acc_vmem, *own)

        @pl.when(pl.program_id(0) == nb - 1)
        def _():
            for src, dst in zip(acc_vmem, acc_hbm):
                pltpu.sync_copy(src, dst)

    buffers = [pltpu.VMEM(s, dt) for s, dt in acc_outs] + list(scratch)
    return _fused_call(name, fn if acc_outs else body, (nb,), in_specs, out_specs, out_shape, buffers,
                       [*rows, *consts], exchange, _params(1, vmem))


def _inproj_fwd(x, g1, w_in, tb, exchange=None):
    seq = x.shape[0]

    def body(x_ref, g_ref, w_ref, h_ref, q_ref, k_ref, v_ref, u_ref, ga_ref, gs_ref):
        h, _, _ = _rms(x_ref[...], g_ref[...])
        hb = _bf(h)
        h_ref[...] = hb
        pj = _mm_nt(hb, w_ref[...])
        q_ref[...] = _bf(pj[:, SPLITS[0]:SPLITS[1]])
        k_ref[...] = _bf(pj[:, SPLITS[1]:SPLITS[2]])
        v_ref[...] = _bf(pj[:, SPLITS[2]:SPLITS[3]])
        u_ref[...] = pj[:, SPLITS[3]:SPLITS[4]]
        ga_ref[...] = pj[:, SPLITS[4]:SPLITS[5]]
        gs_ref[...] = pj[:, SPLITS[5]:SPLITS[6]]

    return _rowcall("inproj_fwd", body, seq, tb, [x], [g1, w_in],
                    [(D_MODEL, BF16), (ATTN_W, BF16), (KV_W, BF16), (KV_W, BF16), (SSM_W, F32),
                     (D_MODEL, F32), (D_MODEL, F32)], [], vmem=VMEM_BIG, exchange=exchange)


def _inproj_bwd(x, dx2, dq, dk, dv, du, dga, dgs, g1, w_in, tb, exchange=None):
    seq = x.shape[0]

    def body(x_ref, dx2_ref, dq_ref, dk_ref, dv_ref, du_ref, dga_ref, dgs_ref, g_ref, w_ref,
             dx_ref, dpj_ref, dg_ref):
        @pl.when(pl.program_id(0) == 0)
        def _():
            dg_ref[...] = jnp.zeros_like(dg_ref)

        dpj = jnp.concatenate([dq_ref[...], dk_ref[...], dv_ref[...], _bf(du_ref[...]),
                               dga_ref[...], dgs_ref[...]], axis=1)
        dpj_ref[...] = dpj
        dh = _mm(dpj, w_ref[...])
        g = g_ref[...]
        _, xh, r = _rms(x_ref[...], g)
        dxn, dg = _rms_bwd(dh, xh, r, g)
        dx_ref[...] = dx2_ref[...] + dxn
        dg_ref[...] += dg

    return _rowcall("inproj_bwd", body, seq, tb, [x, dx2, dq, dk, dv, du, dga, dgs], [g1, w_in],
                    [(D_MODEL, F32), (IN_W, BF16)], [((1, D_MODEL), F32)], vmem=VMEM_BIG, exchange=exchange)


def _bucket_table():
    qi = np.arange(BLOCK)[:, None]
    kj = np.arange(2 * BLOCK)[None, :]
    dist = qi + BLOCK - kj
    max_exact = N_BUCKETS // 2
    d = np.maximum(dist, 0)
    df = np.maximum(d, 1).astype(np.float32)
    large = max_exact + (np.log(df / np.float32(max_exact)) / np.float32(math.log(MAX_DISTANCE / max_exact))
                         * np.float32(N_BUCKETS - max_exact)).astype(np.int32)
    large = np.minimum(large, N_BUCKETS - 1)
    bucket = np.where(d < max_exact, d, large)
    valid = (dist >= 0) & (dist < BLOCK)
    return np.where(valid, bucket, -1).astype(np.int32)


def _bias_table(rel_bias, bucket):
    def body(rb_ref, bk_ref, o_ref):
        bk = bk_ref[...]
        has_prev = lax.broadcasted_iota(jnp.int32, bk.shape, 1) >= BLOCK
        for h in range(N_HEADS):
            kh, j, par = h // Q_GROUP, (h // 2) % 2, h % 2
            acc = jnp.full((BLOCK, 2 * BLOCK), NEG_INF, F32)
            for b in range(N_BUCKETS):
                acc = jnp.where(bk == b, rb_ref[b, h], acc)
            o_ref[0, kh, par, :, j * BLOCK:(j + 1) * BLOCK] = jnp.where(has_prev, acc, NEG_INF).T
            o_ref[1, kh, par, :, j * BLOCK:(j + 1) * BLOCK] = acc.T

    return pl.pallas_call(
        body, out_shape=SDS((2, N_KV, 2, 2 * BLOCK, 2 * BLOCK), F32),
        in_specs=[pl.BlockSpec(memory_space=pltpu.SMEM), pl.BlockSpec(memory_space=pltpu.VMEM)],
        out_specs=pl.BlockSpec(memory_space=pltpu.VMEM), name="bias_table",
    )(rel_bias, bucket)


def _bias_grad(dbias, bucket):
    def body(db_ref, bk_ref, o_ref):
        bk = bk_ref[...]
        for h in range(N_HEADS):
            kh, j, par = h // Q_GROUP, (h // 2) % 2, h % 2
            db = db_ref[kh, par, :, j * BLOCK:(j + 1) * BLOCK].T
            for b in range(N_BUCKETS):
                o_ref[b, h] = jnp.sum(jnp.where(bk == b, db, 0.0))

    return pl.pallas_call(
        body, out_shape=SDS((N_BUCKETS, N_HEADS), F32),
        in_specs=[pl.BlockSpec(memory_space=pltpu.VMEM), pl.BlockSpec(memory_space=pltpu.VMEM)],
        out_specs=pl.BlockSpec(memory_space=pltpu.SMEM), name="bias_grad",
    )(dbias, bucket)


TILE = 2 * HEAD_DIM


def _pair_layout(t):
    lead = t.shape[:-3]
    t = t.reshape(lead + (N_KV, 2, 2) + t.shape[-2:])
    nl = len(lead)
    t = jnp.transpose(t, tuple(range(nl)) + (nl, nl + 2, nl + 1, nl + 3, nl + 4))
    return t.reshape(lead + (N_KV, 2, 2 * BLOCK, t.shape[-1]))


def _pair_unlayout(t):
    t = t.reshape(N_KV, 2, 2, BLOCK, t.shape[-1]).transpose(0, 2, 1, 3, 4)
    return t.reshape(N_HEADS, BLOCK, t.shape[-1])


def _halves(t):
    tf = t.astype(F32)
    low = lax.broadcasted_iota(jnp.int32, tf.shape, 1) < HEAD_DIM
    swapped = pltpu.roll(tf, HEAD_DIM, 1)
    zero = jnp.zeros_like(tf)
    return ((_bf(jnp.where(low, tf, zero)), _bf(jnp.where(low, zero, swapped))),
            (_bf(jnp.where(low, swapped, zero)), _bf(jnp.where(low, zero, tf))))


def _fold_halves(even, odd):
    low = lax.broadcasted_iota(jnp.int32, even.shape, 1) < HEAD_DIM
    comb = jnp.where(low, even, odd)
    return comb + pltpu.roll(comb, HEAD_DIM, 1)


def _tile_rows(ref, kh):
    return jnp.concatenate([ref[:, (2 * kh) * TILE:(2 * kh + 1) * TILE],
                            ref[:, (2 * kh + 1) * TILE:(2 * kh + 2) * TILE]], axis=0)


def _halves_t(t):
    tt = t.astype(F32).T
    top = lax.broadcasted_iota(jnp.int32, tt.shape, 0) < HEAD_DIM
    swapped = jnp.concatenate([tt[HEAD_DIM:], tt[:HEAD_DIM]], axis=0)
    zero = jnp.zeros_like(tt)
    return ((_bf(jnp.where(top, tt, zero)), _bf(jnp.where(top, zero, swapped))),
            (_bf(jnp.where(top, swapped, zero)), _bf(jnp.where(top, zero, tt))))


def _attn_probs(km, qk, bias, sink):
    lg = _mm_nt(km, qk) * (HEAD_DIM ** -0.5) + bias
    m = jnp.maximum(jnp.max(lg, axis=0, keepdims=True), sink)
    p = jnp.exp(lg - m)
    es = jnp.exp(sink - m)
    inv = 1.0 / (jnp.sum(p, axis=0, keepdims=True) + es)
    return p * inv, es * inv


def _attn_fwd(q, k, v, bias, sink_rows, exchange=None):
    seq = q.shape[0]
    nblk = seq // BLOCK

    def body(q_ref, kp_ref, kc_ref, vp_ref, vc_ref, b_ref, s_ref, o_ref):
        which = jnp.minimum(pl.program_id(0), 1)
        kms = _halves(jnp.concatenate([kp_ref[...], kc_ref[...]], axis=0))
        vts = _halves_t(jnp.concatenate([vp_ref[...], vc_ref[...]], axis=0))
        for kh in range(N_KV):
            qk = _tile_rows(q_ref, kh)
            acc = jnp.zeros((TILE, 2 * BLOCK), F32)
            for par in range(2):
                pr, _ = _attn_probs(kms[kh][par], qk, b_ref[which, kh, par], s_ref[kh, par])
                acc = acc + _mm(vts[kh][par], _bf(pr))
            acc = acc.T
            o_ref[:, (2 * kh) * TILE:(2 * kh + 1) * TILE] = _bf(acc[:BLOCK])
            o_ref[:, (2 * kh + 1) * TILE:(2 * kh + 2) * TILE] = _bf(acc[BLOCK:])

    cur = lambda n: (n, 0)
    prev = lambda n: (jnp.maximum(n - 1, 0), 0)
    return _fused_call(
        "attn_fwd", body, (nblk,),
        [pl.BlockSpec((BLOCK, ATTN_W), cur),
         pl.BlockSpec((BLOCK, KV_W), prev), pl.BlockSpec((BLOCK, KV_W), cur),
         pl.BlockSpec((BLOCK, KV_W), prev), pl.BlockSpec((BLOCK, KV_W), cur),
         pl.BlockSpec(bias.shape, functools.partial(_zero_map, bias.ndim)),
         pl.BlockSpec(sink_rows.shape, functools.partial(_zero_map, sink_rows.ndim))],
        [pl.BlockSpec((BLOCK, ATTN_W), cur)], [SDS((seq, ATTN_W), BF16)], [],
        [q, k, k, v, v, bias, sink_rows], exchange, _params(1))


def _attn_bwd(q, k, v, d_out, bias, sink_rows, exchange=None):
    seq = q.shape[0]
    nblk = seq // BLOCK

    def body(q_ref, kp_ref, kc_ref, vp_ref, vc_ref, do_ref, b_ref, s_ref,
             dq_ref, dk_ref, dv_ref, db_ref, ds_ref, ck_ref, cv_ref):
        n = pl.program_id(0)

        @pl.when(n == 0)
        def _():
            db_ref[...] = jnp.zeros_like(db_ref)
            ds_ref[...] = jnp.zeros_like(ds_ref)
            ck_ref[...] = jnp.zeros_like(ck_ref)
            cv_ref[...] = jnp.zeros_like(cv_ref)

        @pl.when(n < nblk)
        def _():
            which = jnp.minimum(n, 1)
            scale = HEAD_DIM ** -0.5
            kcat = jnp.concatenate([kp_ref[...], kc_ref[...]], axis=0)
            kms = _halves(kcat)
            kts = _halves_t(kcat)
            vms = _halves(jnp.concatenate([vp_ref[...], vc_ref[...]], axis=0))
            dks, dvs = [], []
            for kh in range(N_KV):
                qk = _tile_rows(q_ref, kh)
                dok = _tile_rows(do_ref, kh)
                dq = jnp.zeros((TILE, 2 * BLOCK), F32)
                dkp, dvp = [], []
                for par in range(2):
                    pr, ps = _attn_probs(kms[kh][par], qk, b_ref[which, kh, par], s_ref[kh, par])
                    dp = _mm_nt(vms[kh][par], dok)
                    rs = jnp.sum(pr * dp, axis=0, keepdims=True)
                    dlg = pr * (dp - rs)
                    ds_ref[kh, par] += -ps * rs
                    db_ref[kh, par] += dlg
                    dlb = _bf(dlg)
                    dq = dq + _mm(kts[kh][par], dlb)
                    dkp.append(_mm(dlb, qk))
                    dvp.append(_mm(_bf(pr), dok))
                dq = _bf((dq * scale).T)
                dq_ref[:, (2 * kh) * TILE:(2 * kh + 1) * TILE] = dq[:BLOCK]
                dq_ref[:, (2 * kh + 1) * TILE:(2 * kh + 2) * TILE] = dq[BLOCK:]
                dks.append(_fold_halves(*dkp))
                dvs.append(_fold_halves(*dvp))
            low = lax.broadcasted_iota(jnp.int32, (2 * BLOCK, TILE), 1) < HEAD_DIM
            dkk = jnp.where(low, dks[0], dks[1]) * scale
            dvv = jnp.where(low, dvs[0], dvs[1])
            dk_ref[...] = _bf(ck_ref[...] + dkk[:BLOCK])
            ck_ref[...] = dkk[BLOCK:]
            dv_ref[...] = _bf(cv_ref[...] + dvv[:BLOCK])
            cv_ref[...] = dvv[BLOCK:]

        @pl.when(n == nblk)
        def _():
            dk_ref[...] = _bf(ck_ref[...])
            dv_ref[...] = _bf(cv_ref[...])

    cur = lambda n: (jnp.minimum(n, nblk - 1), 0)
    prev = lambda n: (jnp.maximum(jnp.minimum(n, nblk - 1) - 1, 0), 0)
    late = lambda n: (jnp.maximum(n - 1, 0), 0)
    kv_spec = lambda m: pl.BlockSpec((BLOCK, KV_W), m)
    acc_b = pl.BlockSpec(bias.shape[1:], functools.partial(_zero_map, bias.ndim - 1))
    acc_s = pl.BlockSpec(sink_rows.shape, functools.partial(_zero_map, sink_rows.ndim))
    return _fused_call(
        "attn_bwd", body, (nblk + 1,),
        [pl.BlockSpec((BLOCK, ATTN_W), cur), kv_spec(prev), kv_spec(cur), kv_spec(prev), kv_spec(cur),
         pl.BlockSpec((BLOCK, ATTN_W), cur),
         pl.BlockSpec(bias.shape, functools.partial(_zero_map, bias.ndim)), acc_s],
        [pl.BlockSpec((BLOCK, ATTN_W), cur), kv_spec(late), kv_spec(late), acc_b, acc_s],
        [SDS((seq, ATTN_W), BF16), SDS((seq, KV_W), BF16), SDS((seq, KV_W), BF16),
         SDS(bias.shape[1:], F32), SDS(sink_rows.shape, F32)],
        [pltpu.VMEM((BLOCK, KV_W), F32), pltpu.VMEM((BLOCK, KV_W), F32)],
        [q, k, k, v, v, d_out, bias, sink_rows], exchange, _params(1))


def _ssm_discretize(lam_re, lam_im, log_dt, b_re, b_im):
    dt = jnp.exp(log_dt)[:, None]
    mag = jnp.exp(lam_re * dt)
    ab_re = mag * jnp.cos(lam_im * dt)
    ab_im = mag * jnp.sin(lam_im * dt)
    nr = ab_re - 1.0
    den = lam_re * lam_re + lam_im * lam_im
    f_re = (nr * lam_re + ab_im * lam_im) / den
    f_im = (ab_im * lam_re - nr * lam_im) / den
    bb_re = f_re[..., None] * b_re - f_im[..., None] * b_im
    bb_im = f_re[..., None] * b_im + f_im[..., None] * b_re
    return ab_re, ab_im, bb_re, bb_im


def _state_layout(re, im):
    lead = re.shape[:-2]
    z = jnp.stack([re, im], axis=-3).reshape(lead + (2, N_SUPER, GROUPS_PER_SUPER, SSM_STATE))
    return jnp.moveaxis(z, -4, -3).reshape(lead + (STATE_COLS,))


def _state_unlayout(vec):
    z = vec.reshape(N_SUPER, 2, GROUPS_PER_SUPER, SSM_STATE).transpose(1, 0, 2, 3)
    z = z.reshape(2, SSM_GROUPS, SSM_STATE)
    return z[0], z[1]


SEG = 4
WINDOW = SEG * SUBLANES


def _scan_tables(ab_re, ab_im):
    pw = [None, (ab_re, ab_im)]
    for _ in range(2, WINDOW + 1):
        pr, pi_ = pw[-1]
        pw.append((pr * ab_re - pi_ * ab_im, pr * ab_im + pi_ * ab_re))
    fwd = np.zeros((7, SUBLANES), np.int64)
    bwd = np.zeros((7, SUBLANES), np.int64)
    for k, shift in enumerate((1, 2, 4)):
        fwd[k] = [SEG * shift if r >= shift else 0 for r in range(SUBLANES)]
        bwd[k] = [SEG * shift if r < SUBLANES - shift else 0 for r in range(SUBLANES)]
    fwd[3] = [SEG * (r + 1) for r in range(SUBLANES)]
    bwd[3] = [SEG * (SUBLANES - r) for r in range(SUBLANES)]
    for k in range(1, SEG):
        fwd[3 + k] = bwd[3 + k] = k
    used = sorted((set(fwd.ravel()) | set(bwd.ravel())) - {0})
    select = lambda which: np.stack([(which == p) for p in used], axis=-1).astype(np.float32)
    stacked = _state_layout(jnp.stack([pw[p][0] for p in used]), jnp.stack([pw[p][1] for p in used]))
    conj_sign = np.where((np.arange(STATE_COLS) // SUPER_HALF) % 2 == 1, -1.0, 1.0).astype(np.float32)
    pick = functools.partial(jnp.einsum, 'krp,pc->krc', precision=lax.Precision.HIGHEST)
    return pick(select(fwd), stacked), pick(select(bwd), stacked) * conj_sign


_EYE = np.eye(GROUPS_PER_SUPER, dtype=np.float32)


def _b_matrix(bb_re, bb_im):
    bb = jnp.stack([bb_re, bb_im]).reshape(2, N_SUPER, GROUPS_PER_SUPER, SSM_STATE, SSM_GROUP)
    m = jnp.einsum('rsgpc,gh->sgcrhp', bb, _EYE)
    return m.reshape(N_SUPER, SUPER_IN, SUPER_W)


def _b_matrix_grad(dm):
    d = dm.reshape(N_SUPER, GROUPS_PER_SUPER, SSM_GROUP, 2, GROUPS_PER_SUPER, SSM_STATE)
    d = jnp.sum(d * _EYE[None, :, None, None, :, None], axis=4)
    d = d.transpose(3, 0, 1, 4, 2).reshape(2, SSM_GROUPS, SSM_STATE, SSM_GROUP)
    return d[0], d[1]


def _c_matrix(c_re, c_im):
    cc = jnp.stack([c_re, -c_im]).reshape(2, N_SUPER, GROUPS_PER_SUPER, SSM_GROUP, SSM_STATE)
    m = jnp.einsum('rsgcp,gh->srgphc', cc, _EYE)
    return m.reshape(N_SUPER, SUPER_W, SUPER_IN)


def _c_matrix_grad(dm):
    d = dm.reshape(N_SUPER, 2, GROUPS_PER_SUPER, SSM_STATE, GROUPS_PER_SUPER, SSM_GROUP)
    d = jnp.sum(d * _EYE[None, None, :, None, :, None], axis=4)
    d = d.transpose(1, 0, 2, 4, 3).reshape(2, SSM_GROUPS, SSM_GROUP, SSM_STATE)
    return d[0], -d[1]


def _cmul_add(xr, xi, ar, ai, sr, si):
    return xr + ar * sr - ai * si, xi + ar * si + ai * sr


def _scan_rows(buf_ref, tab_ref, carry_ref, n_windows, reverse, h_ref=None, da_ref=None):
    order = list(range(SEG - 1, -1, -1)) if reverse else list(range(SEG))
    near = SUBLANES - 1 if reverse else 0
    far = 0 if reverse else SUBLANES - 1
    s_in = SUBLANES - 1 if reverse else 1
    lanes = lambda tile: pl.ds(tile * LANES, LANES)

    def window(w0, tile_re, tile_im, c_re, c_im, acc):
        rows = lambda t: pl.ds(w0 + t, SUBLANES, stride=SEG)
        get = lambda ref, t: (ref.at[tile_re][rows(t), :], ref.at[tile_im][rows(t), :])
        tab = lambda k: (tab_ref[k, :, lanes(tile_re)], tab_ref[k, :, lanes(tile_im)])

        def put(t, xr, xi):
            buf_ref.at[tile_re][rows(t), :] = xr
            buf_ref.at[tile_im][rows(t), :] = xi

        a1 = tab(4)
        er, ei = get(buf_ref, order[0])
        for t in order[1:]:
            er, ei = _cmul_add(*get(buf_ref, t), *a1, er, ei)
            if t != order[-1]:
                put(t, er, ei)
        for k, shift in enumerate((1, 2, 4)):
            s = (SUBLANES - shift) if reverse else shift
            er, ei = _cmul_add(er, ei, *tab(k), pltpu.roll(er, s, 0), pltpu.roll(ei, s, 0))
        er, ei = _cmul_add(er, ei, *tab(3), c_re, c_im)
        put(order[-1], er, ei)
        sub = lax.broadcasted_iota(jnp.int32, er.shape, 0)
        in_re = jnp.where(sub == near, c_re, pltpu.roll(er, s_in, 0))
        in_im = jnp.where(sub == near, c_im, pltpu.roll(ei, s_in, 0))
        true = {order[-1]: (er, ei)}
        for idx, t in enumerate(order[:-1]):
            true[t] = _cmul_add(*get(buf_ref, t), *tab(4 + idx), in_re, in_im)
            put(t, *true[t])
        carry = (jnp.broadcast_to(er[far:far + 1], er.shape), jnp.broadcast_to(ei[far:far + 1], ei.shape))
        if acc is None:
            return carry, None
        acc_re, acc_im = acc
        for t in range(SEG):
            if t + 1 < SEG:
                gr, gim = true[t + 1]
            else:
                gr = jnp.where(sub == SUBLANES - 1, c_re, pltpu.roll(true[0][0], SUBLANES - 1, 0))
                gim = jnp.where(sub == SUBLANES - 1, c_im, pltpu.roll(true[0][1], SUBLANES - 1, 0))
            hr, hi = get(h_ref, t)
            acc_re = acc_re + gr * hr + gim * hi
            acc_im = acc_im + gim * hr - gr * hi
        return carry, (acc_re, acc_im)

    half = SUPER_HALF // LANES
    per = 2 if h_ref is None else 4
    for sb in range(N_SUPER):
        pairs = [(2 * half * sb + j, 2 * half * sb + half + j) for j in range(half)]

        def step(wi, state, pairs=pairs):
            w = (n_windows - 1 - wi) if reverse else wi
            w0 = pl.multiple_of(w * WINDOW, WINDOW)
            out = []
            for j, (tile_re, tile_im) in enumerate(pairs):
                mine = state[per * j:per * (j + 1)]
                carry, acc = window(w0, tile_re, tile_im, mine[0], mine[1], mine[2:] or None)
                out += list(carry) + list(acc or ())
            return tuple(out)

        init = []
        for tile_re, tile_im in pairs:
            init += [carry_ref[:, lanes(tile_re)], carry_ref[:, lanes(tile_im)]]
            if h_ref is not None:
                init += [da_ref[:, lanes(tile_re)], da_ref[:, lanes(tile_im)]]
        fin = lax.fori_loop(0, n_windows, step, tuple(init))
        for j, (tile_re, tile_im) in enumerate(pairs):
            carry_ref[:, lanes(tile_re)] = fin[per * j]
            carry_ref[:, lanes(tile_im)] = fin[per * j + 1]
            if h_ref is not None:
                da_ref[:, lanes(tile_re)] = fin[per * j + 2]
                da_ref[:, lanes(tile_im)] = fin[per * j + 3]


def _put_tiles(ref, sb, block):
    for j in range(SUPER_TILES):
        ref[sb * SUPER_TILES + j] = block[:, j * LANES:(j + 1) * LANES]


def _get_tiles(ref, sb):
    return jnp.concatenate([ref[sb * SUPER_TILES + j] for j in range(SUPER_TILES)], axis=1)


def _ssm_fwd(u, bmat, cmat, tab, d_skip, tb, exchange=None):
    seq = u.shape[0]

    def body(u_ref, b_ref, c_ref, t_ref, d_ref, s_ref, h_ref, carry_ref):
        @pl.when(pl.program_id(0) == 0)
        def _():
            carry_ref[...] = jnp.zeros_like(carry_ref)

        u_blk = u_ref[...]
        ub = _bf(u_blk)
        for sb in range(N_SUPER):
            _put_tiles(h_ref, sb, _mm(ub[:, sb * SUPER_IN:(sb + 1) * SUPER_IN], b_ref[sb]))
        _scan_rows(h_ref, t_ref, carry_ref, tb // WINDOW, False)
        ys = [_mm(_bf(_get_tiles(h_ref, sb)), c_ref[sb]) for sb in range(N_SUPER)]
        s_ref[...] = jnp.concatenate(ys, axis=1) + d_ref[...] * u_blk

    return _rowcall("ssm_fwd", body, seq, tb, [u], [bmat, cmat, tab, d_skip],
                    [(SSM_W, F32), ((STATE_TILES, LANES), F32)], [],
                    scratch=[pltpu.VMEM((SUBLANES, STATE_COLS), F32)], vmem=VMEM_BIG, exchange=exchange)


def _ssm_bwd(ds, u, h, bmat_t, cmat_t, tab, d_skip, tb, exchange=None):
    seq = u.shape[0]

    def body(ds_ref, u_ref, h_ref, bt_ref, ct_ref, t_ref, d_ref,
             du_ref, db_ref, dc_ref, da_ref, dd_ref, g_ref, carry_ref):
        @pl.when(pl.program_id(0) == 0)
        def _():
            carry_ref[...] = jnp.zeros_like(carry_ref)
            db_ref[...] = jnp.zeros_like(db_ref)
            dc_ref[...] = jnp.zeros_like(dc_ref)
            da_ref[...] = jnp.zeros_like(da_ref)
            dd_ref[...] = jnp.zeros_like(dd_ref)

        ds_blk = ds_ref[...]
        dsb = _bf(ds_blk)
        u_blk = u_ref[...]
        ub = _bf(u_blk)
        for sb in range(N_SUPER):
            _put_tiles(g_ref, sb, _mm(dsb[:, sb * SUPER_IN:(sb + 1) * SUPER_IN], ct_ref[sb]))
        _scan_rows(g_ref, t_ref, carry_ref, tb // WINDOW, True, h_ref=h_ref, da_ref=da_ref)
        dus = []
        for sb in range(N_SUPER):
            gb = _bf(_get_tiles(g_ref, sb))
            dus.append(_mm(gb, bt_ref[sb]))
            db_ref[sb] += _mm_tn(ub[:, sb * SUPER_IN:(sb + 1) * SUPER_IN], gb)
            dc_ref[sb] += _mm_tn(_bf(_get_tiles(h_ref, sb)), dsb[:, sb * SUPER_IN:(sb + 1) * SUPER_IN])
        du_ref[...] = jnp.concatenate(dus, axis=1) + d_ref[...] * ds_blk
        dd_ref[...] += jnp.sum(ds_blk * u_blk, axis=0, keepdims=True)

    return _rowcall("ssm_bwd", body, seq, tb, [ds, u, h], [bmat_t, cmat_t, tab, d_skip],
                    [(SSM_W, F32)],
                    [((N_SUPER, SUPER_IN, SUPER_W), F32), ((N_SUPER, SUPER_W, SUPER_IN), F32),
                     ((SUBLANES, STATE_COLS), F32), ((1, SSM_W), F32)],
                    scratch=[pltpu.VMEM((STATE_TILES, tb, LANES), F32), pltpu.VMEM((SUBLANES, STATE_COLS), F32)],
                    reverse=True, vmem=VMEM_BIG, exchange=exchange)


def _merge_core(s, attb, ga, gs, wg_ref, wab_ref, wsb_ref, wout_ref):
    zg, dgelu = _gelu_and_grad(s)
    zgb = _bf(zg)
    sg = _sig(_mm(zgb, wg_ref[...]))
    z = zg * sg
    zb = _bf(z)
    ys = jnp.concatenate([_mm(zb, wsb_ref[j]) for j in range(N_CHIPS)], axis=1)
    ya = jnp.concatenate([_mm(attb, wab_ref[j]) for j in range(N_CHIPS)], axis=1)
    sa = _sig(ga)
    ss = _sig(gs)
    mgb = _bf(sa * ya + ss * ys)
    o = _mm(mgb, wout_ref[...])
    return dict(zg=zg, dgelu=dgelu, zgb=zgb, sg=sg, zb=zb, ys=ys, ya=ya, sa=sa, ss=ss, mgb=mgb, o=o)


def _merge_fwd(x, s, att, ga, gs, g2, w_glu, w_ab, w_sb, w_out, tb, exchange=None):
    seq = x.shape[0]

    def body(x_ref, s_ref, att_ref, ga_ref, gs_ref, g_ref, wg_ref, wab_ref, wsb_ref, wout_ref, x2_ref):
        f = _merge_core(s_ref[...], att_ref[...], ga_ref[...], gs_ref[...], wg_ref, wab_ref, wsb_ref, wout_ref)
        n, _, _ = _rms(f["o"], g_ref[...])
        x2_ref[...] = x_ref[...] + n

    return _rowcall("merge_fwd", body, seq, tb, [x, s, att, ga, gs], [g2, w_glu, w_ab, w_sb, w_out],
                    [(D_MODEL, F32)], [], vmem=VMEM_BIG, exchange=exchange)[0]


def _merge_bwd(dx2, s, att, ga, gs, g2, w_glu, w_ab, w_sb, w_out, tb, exchange=None):
    seq = s.shape[0]
    cw = D_MODEL // N_CHIPS
    last = seq // tb - 1

    def body(dx2_ref, s_ref, att_ref, ga_ref, gs_ref, g_ref, wg_ref, wab_ref, wsb_ref, wout_ref,
             ds_ref, datt_ref, dga_ref, dgs_ref, dg_ref, dwg_ref, dwab_ref, dwsb_ref, dwout_ref,
             bwg_ref, bwab_ref, bwsb_ref, bwout_ref):
        @pl.when(pl.program_id(0) == 0)
        def _():
            for r in (dg_ref, dwg_ref, dwab_ref, dwsb_ref, dwout_ref):
                r[...] = jnp.zeros_like(r)

        attb = att_ref[...]
        f = _merge_core(s_ref[...], attb, ga_ref[...], gs_ref[...], wg_ref, wab_ref, wsb_ref, wout_ref)
        g = g_ref[...]
        _, oh, r2 = _rms(f["o"], g)
        do, dg = _rms_bwd(dx2_ref[...], oh, r2, g)
        dg_ref[...] += dg
        dob = _bf(do)
        dwout_ref[...] += _mm_tn(f["mgb"], dob)
        dmg = _mm_nt(dob, wout_ref[...])
        sa, ss = f["sa"], f["ss"]
        dyab = _bf(dmg * sa)
        dysb = _bf(dmg * ss)
        dga_ref[...] = _bf(dmg * f["ya"] * sa * (1.0 - sa))
        dgs_ref[...] = _bf(dmg * f["ys"] * ss * (1.0 - ss))
        dwab = _mm_tn(attb, dyab)
        dwsb = _mm_tn(f["zb"], dysb)
        datt = jnp.zeros((tb, ATTN_W), F32)
        dz = jnp.zeros((tb, SSM_W), F32)
        for j in range(N_CHIPS):
            dwab_ref[j] += dwab[:, j * cw:(j + 1) * cw]
            dwsb_ref[j] += dwsb[:, j * cw:(j + 1) * cw]
            datt = datt + _mm_nt(dyab[:, j * cw:(j + 1) * cw], wab_ref[j])
            dz = dz + _mm_nt(dysb[:, j * cw:(j + 1) * cw], wsb_ref[j])
        datt_ref[...] = _bf(datt)
        sg, zg = f["sg"], f["zg"]
        dglb = _bf(dz * zg * sg * (1.0 - sg))
        dwg_ref[...] += _mm_tn(f["zgb"], dglb)
        dzg = dz * sg + _mm_nt(dglb, wg_ref[...])
        ds_ref[...] = dzg * f["dgelu"]

        @pl.when(pl.program_id(0) == last)
        def _():
            for dst, src in ((bwg_ref, dwg_ref), (bwab_ref, dwab_ref), (bwsb_ref, dwsb_ref), (bwout_ref, dwout_ref)):
                dst[...] = _bf(src[...])

    shapes = [w_glu.shape, w_ab.shape, w_sb.shape, w_out.shape]
    return _rowcall("merge_bwd", body, seq, tb, [dx2, s, att, ga, gs], [g2, w_glu, w_ab, w_sb, w_out],
                    [(SSM_W, F32), (ATTN_W, BF16), (D_MODEL, BF16), (D_MODEL, BF16)],
                    [((1, D_MODEL), F32)] + [(sh, F32) for sh in shapes] + [(sh, BF16) for sh in shapes],
                    vmem=VMEM_BIG, exchange=exchange)


def _mlp_fwd_loss(x2, target, g3, g4, w_ffi, w_ffo, tb):
    seq = x2.shape[0]
    n_slab = len(w_ffi)
    sw = D_FF // FF_CHUNKS // n_slab

    def body(x2_ref, t_ref, g3_ref, g4_ref, *rest):
        wi_refs, (wo_ref, dy_ref, df_ref, h_ref, ra_ref, loss_ref, dg_ref) = rest[:n_slab], rest[n_slab:]

        @pl.when(pl.program_id(0) == 0)
        def _():
            loss_ref[...] = jnp.zeros_like(loss_ref)
            dg_ref[...] = jnp.zeros_like(dg_ref)

        x2_blk = x2_ref[...]
        h3, _, _ = _rms(x2_blk, g3_ref[...])
        hb = _bf(h3)
        h_ref[...] = hb
        f = jnp.zeros((tb, D_MODEL), F32)
        for j in range(FF_CHUNKS):
            for k in range(n_slab):
                ra = jnp.maximum(_mm(hb, wi_refs[k][j]), 0.0)
                ra_ref[:, pl.ds((j * n_slab + k) * sw, sw)] = _bf(ra)
                f = f + _mm(_bf(ra * ra), wo_ref[j, pl.ds(k * sw, sw), :])
        g4 = g4_ref[...]
        n4, fh, r4 = _rms(f, g4)
        e = (x2_blk + n4) - t_ref[...]
        loss_ref[...] += 0.5 * jnp.sum(jnp.mean(e * e, axis=-1, keepdims=True))
        dy = e * (1.0 / D_MODEL)
        dy_ref[...] = dy
        df, dg = _rms_bwd(dy, fh, r4, g4)
        df_ref[...] = _bf(df)
        dg_ref[...] += dg

    return _rowcall("mlp_fwd_loss", body, seq, tb, [x2, target], [g3, g4, *w_ffi, w_ffo],
                    [(D_MODEL, F32), (D_MODEL, BF16), (D_MODEL, BF16), (D_FF, BF16)],
                    [((SUBLANES, 128), F32), ((1, D_MODEL), F32)], vmem=VMEM_BIG)


def _mlp_bwd(x2, dy, df, ra, g3, w_ffi, w_ffo, tb):
    seq = x2.shape[0]
    n_slab = len(w_ffi)
    sw = D_FF // FF_CHUNKS // n_slab

    def body(x2_ref, dy_ref, df_ref, ra_ref, g3_ref, *rest):
        wi_refs, (wo_ref, dx_ref, da_ref, dg_ref) = rest[:n_slab], rest[n_slab:]

        @pl.when(pl.program_id(0) == 0)
        def _():
            dg_ref[...] = jnp.zeros_like(dg_ref)

        dfb = df_ref[...]
        dh = jnp.zeros((tb, D_MODEL), F32)
        for j in range(FF_CHUNKS):
            for k in range(n_slab):
                cols = pl.ds((j * n_slab + k) * sw, sw)
                ra = ra_ref[:, cols].astype(F32)
                dab = _bf(_mm_nt(dfb, wo_ref[j, pl.ds(k * sw, sw), :]) * (2.0 * ra))
                da_ref[:, cols] = dab
                dh = dh + _mm_nt(dab, wi_refs[k][j])
        g3 = g3_ref[...]
        _, xh, r3 = _rms(x2_ref[...], g3)
        dxn, dg = _rms_bwd(dh, xh, r3, g3)
        dx_ref[...] = dy_ref[...] + dxn
        dg_ref[...] += dg

    return _rowcall("mlp_bwd", body, seq, tb, [x2, dy, df, ra], [g3, *w_ffi, w_ffo],
                    [(D_MODEL, F32), (D_FF, BF16)], [((1, D_MODEL), F32)], vmem=VMEM_BIG)


def _matmul_tn(name, a, b, tk, tn, tl, chunk_major, exchange=None, square_a=False):
    seq, kdim = a.shape
    ndim = b.shape[1]
    last = seq // tl - 1

    def body(a_ref, b_ref, o_ref, ob_ref):
        @pl.when(pl.program_id(2) == 0)
        def _():
            o_ref[...] = jnp.zeros_like(o_ref)

        a_blk = a_ref[...]
        if square_a:
            a_blk = _bf(jnp.square(a_blk.astype(F32)))
        o_ref[...] += _mm_tn(a_blk, b_ref[...])

        @pl.when(pl.program_id(2) == last)
        def _():
            ob_ref[...] = _bf(o_ref[...])

    if chunk_major:
        shape = (ndim // tn, kdim, tn)
        out_spec = pl.BlockSpec((None, tk, tn), lambda k, n, l: (n, k, 0))
    else:
        shape = (kdim, ndim)
        out_spec = pl.BlockSpec((tk, tn), lambda k, n, l: (k, n))
    return _fused_call(
        name, body, (kdim // tk, ndim // tn, seq // tl),
        [pl.BlockSpec((tl, tk), lambda k, n, l: (l, k)), pl.BlockSpec((tl, tn), lambda k, n, l: (l, n))],
        [out_spec, out_spec], [SDS(shape, F32), SDS(shape, BF16)], [], [a, b], exchange, _params(3, VMEM_BIG))


def _ew_call(name, fn, ins, n_out, after=None):
    rows, cols = ins[0].shape
    tr = rows
    while tr * cols * 4 > min(1 << 20, (9 << 20) // (len(ins) + n_out)) and tr % 16 == 0:
        tr //= 2
    spec = pl.BlockSpec((tr, cols), lambda i: (i, 0))
    extra = [] if after is None else [after]

    def body(*refs):
        outs = fn(*[r[...] for r in refs[:len(ins)]])
        for r, o in zip(refs[len(ins) + len(extra):], outs):
            r[...] = o

    return pl.pallas_call(
        body, grid=(rows // tr,), in_specs=[spec] * len(ins) + [ANY] * len(extra), out_specs=[spec] * n_out,
        out_shape=[SDS((rows, cols), F32)] * n_out, name=name, compiler_params=_params(1))(*ins, *extra)


def _adam_math(w, g, m, v):
    m2 = ADAM_B1 * m + (1.0 - ADAM_B1) * g
    v2 = ADAM_B2 * v + (1.0 - ADAM_B2) * (g * g)
    m_hat = m2 / (1.0 - ADAM_B1 ** ADAM_STEP)
    v_hat = v2 / (1.0 - ADAM_B2 ** ADAM_STEP)
    delta = -ADAM_LR * (m_hat / (jnp.sqrt(v_hat) + ADAM_EPS) + ADAM_WD * w)
    return delta, m2, v2


def _sum4(name, own, recv, idx):
    _, rows, cols = own.shape
    tr = rows
    while tr * cols * 4 > (1 << 20) and tr % 16 == 0:
        tr //= 2

    def body(idx_ref, o_ref, r0_ref, r1_ref, r2_ref, out_ref):
        out_ref[...] = ((o_ref[...] + r0_ref[...].astype(F32)) + r1_ref[...].astype(F32)) + r2_ref[...].astype(F32)

    blk = (None, tr, cols)
    grid_spec = pltpu.PrefetchScalarGridSpec(
        num_scalar_prefetch=1, grid=(rows // tr,),
        in_specs=[pl.BlockSpec(blk, lambda i, s: (s[0], i, 0)), pl.BlockSpec(blk, lambda i, s: (0, i, 0)),
                  pl.BlockSpec(blk, lambda i, s: (1, i, 0)), pl.BlockSpec(blk, lambda i, s: (2, i, 0))],
        out_specs=pl.BlockSpec((tr, cols), lambda i, s: (i, 0)))
    return pl.pallas_call(body, grid_spec=grid_spec, out_shape=SDS((rows, cols), F32), name=name,
                          compiler_params=_params(1))(jnp.reshape(idx, (1,)).astype(jnp.int32), own, recv, recv, recv)


def _sum4_swap(name, own, recv, idx, n_blocks=4):
    _, rows, cols = own.shape
    tr = rows // n_blocks
    assert tr * n_blocks == rows and tr % SUBLANES == 0

    def body(idx_ref, o_ref, r0_ref, r1_ref, r2_ref, mine_ref, theirs_ref, buf, kept, sent, arrived):
        i = pl.program_id(0)
        slot = lax.rem(i, 2)
        x, y, c = _place()

        def copies(j, s):
            block = pl.ds(j * tr, tr)
            return (pltpu.make_async_copy(buf.at[s], mine_ref.at[block], kept.at[s]),
                    pltpu.make_async_remote_copy(
                        src_ref=buf.at[s], dst_ref=theirs_ref.at[block], send_sem=sent.at[s], recv_sem=arrived.at[j],
                        device_id=(x, y, 1 - c), device_id_type=MESH_ID))

        def finish(j, s):
            keep, send = copies(j, s)
            keep.wait()
            send.wait_send()
            send.wait_recv()

        @pl.when(i >= 2)
        def _():
            finish(i - 2, slot)

        buf[slot] = ((o_ref[...] + r0_ref[...].astype(F32)) + r1_ref[...].astype(F32)) + r2_ref[...].astype(F32)
        for cp in copies(i, slot):
            cp.start()

        @pl.when(i == n_blocks - 1)
        def _():
            if n_blocks > 1:
                finish(i - 1, 1 - slot)
            finish(i, slot)

    blk = (None, tr, cols)
    grid_spec = pltpu.PrefetchScalarGridSpec(
        num_scalar_prefetch=1, grid=(n_blocks,),
        in_specs=[pl.BlockSpec(blk, lambda i, s: (s[0], i, 0)), pl.BlockSpec(blk, lambda i, s: (0, i, 0)),
                  pl.BlockSpec(blk, lambda i, s: (1, i, 0)), pl.BlockSpec(blk, lambda i, s: (2, i, 0))],
        out_specs=[HBM, HBM],
        scratch_shapes=[pltpu.VMEM((2, tr, cols), F32), pltpu.SemaphoreType.DMA((2,)), pltpu.SemaphoreType.DMA((2,)),
                        pltpu.SemaphoreType.DMA((n_blocks,))])
    return pl.pallas_call(body, grid_spec=grid_spec, out_shape=[SDS((rows, cols), F32)] * 2, name=name,
                          compiler_params=_params(1))(jnp.reshape(idx, (1,)).astype(jnp.int32), own, recv, recv, recv)


def _adam_pair(name, item, after=None):
    def fn(w_, a, b, m_, v_):
        g = a + b
        return (g,) + _adam_math(w_, g, m_, v_)

    return _ew_call(name, fn, list(item), 4, after)


def _place():
    return lax.axis_index("x"), lax.axis_index("y"), lax.axis_index("c")


def _other_chips(x, y):
    return [(1 - x, y), (x, 1 - y), (1 - x, 1 - y)]


HBM = pl.BlockSpec(memory_space=pltpu.HBM)
SEM = pl.BlockSpec(memory_space=pltpu.SEMAPHORE)
DATAFLOW = pltpu.SideEffectType.DATAFLOW_SIDE_EFFECTING


class _Flight:
    def __init__(self, copies, n_copies, send, recv, srcs, lands, token):
        self.copies, self.n, self.send, self.recv = copies, n_copies, send, recv
        self.srcs, self.lands, self.token = list(srcs), list(lands), token


def _take_off(name, srcs, lands, copies, n_copies, after):
    n_s, n_l = len(srcs), len(lands)

    def body(*refs):
        src, land = refs[:n_s], refs[n_s:n_s + n_l]
        send, recv = refs[n_s + n_l + 1:n_s + n_l + 3]
        for cp in copies(src, land, send, recv):
            cp.start()
        refs[-1][...] = jnp.zeros_like(refs[-1])

    mem = lambda t: pltpu.HBM(t.shape, t.dtype)
    sems = pltpu.SemaphoreType.DMA((n_copies,))
    outs = pl.pallas_call(
        body, name=name,
        out_shape=(sems, sems, *map(mem, srcs), *map(mem, lands), SDS((SUBLANES, LANES), F32)),
        in_specs=[HBM] * (n_s + n_l) + [ANY],
        out_specs=(SEM, SEM, *[HBM] * (n_s + n_l), pl.BlockSpec(memory_space=pltpu.VMEM)),
        input_output_aliases={i: 2 + i for i in range(n_s + n_l)},
        compiler_params=pltpu.CompilerParams(has_side_effects=DATAFLOW),
    )(*[pltpu.with_memory_space_constraint(t, pltpu.HBM) for t in (*srcs, *lands)], after)
    return _Flight(copies, n_copies, outs[0], outs[1], outs[2:2 + n_s], outs[2 + n_s:2 + n_s + n_l], outs[-1])


def _land(name, flight, after):
    n_s, n_l = len(flight.srcs), len(flight.lands)

    def body(*refs):
        src, land = refs[:n_s], refs[n_s:n_s + n_l]
        send, recv = refs[n_s + n_l:n_s + n_l + 2]
        for cp in flight.copies(src, land, send, recv):
            cp.wait_send()
            cp.wait_recv()

    mem = lambda t: pltpu.HBM(t.shape, t.dtype)
    outs = pl.pallas_call(
        body, name=name, out_shape=(*map(mem, flight.srcs), *map(mem, flight.lands)),
        in_specs=[HBM] * (n_s + n_l) + [SEM, SEM, ANY], out_specs=tuple([HBM] * (n_s + n_l)),
        input_output_aliases={i: i for i in range(n_s + n_l)},
        compiler_params=pltpu.CompilerParams(has_side_effects=DATAFLOW),
    )(*flight.srcs, *flight.lands, flight.send, flight.recv, after)
    return list(outs[:n_s]), list(outs[n_s:])


def _empty_like(shapes_from, lead):
    return [lax.empty((lead,) + t.shape[1:], t.dtype) for t in shapes_from]


def _scatter_off(name, chunks, after):
    def copies(src, land, send, recv):
        x, y, c = _place()
        return [pltpu.make_async_remote_copy(
            src_ref=src[a].at[2 * px + py], dst_ref=land[a].at[k], send_sem=send.at[3 * a + k],
            recv_sem=recv.at[3 * a + k], device_id=(px, py, c), device_id_type=MESH_ID)
            for a in range(len(chunks)) for k, (px, py) in enumerate(_other_chips(x, y))]

    return _take_off(name, chunks, _empty_like(chunks, 3), copies, 3 * len(chunks), after)


def _swap_off(name, arrs, after):
    def copies(src, land, send, recv):
        x, y, c = _place()
        return [pltpu.make_async_remote_copy(
            src_ref=src[a], dst_ref=land[a], send_sem=send.at[a], recv_sem=recv.at[a],
            device_id=(x, y, 1 - c), device_id_type=MESH_ID) for a in range(len(arrs))]

    return _take_off(name, arrs, [lax.empty(t.shape, t.dtype) for t in arrs], copies, len(arrs), after)


def _devices_off(name, block, after):
    me = 4 * lax.axis_index("x") + 2 * lax.axis_index("y") + lax.axis_index("c")
    land = lax.dynamic_update_index_in_dim(lax.empty((N_DEV,) + block.shape, block.dtype), block, me, 0)

    def copies(src, land, send, recv):
        x, y, c = _place()
        mine = 4 * x + 2 * y + c
        return [pltpu.make_async_remote_copy(
            src_ref=src[0], dst_ref=land[0].at[mine], send_sem=send.at[k - 1], recv_sem=recv.at[k - 1],
            device_id=(x ^ (k >> 2), y ^ ((k >> 1) & 1), c ^ (k & 1)), device_id_type=MESH_ID)
            for k in range(1, N_DEV)]

    return _take_off(name, [block], [land], copies, N_DEV - 1, after)


def _half_rows(shape, c, other=False):
    half = shape[0] // 2
    return pl.ds(((1 - c) if other else c) * half, half)


def _gather_start(name, shards, lands, after):
    n = len(shards)

    def body(*refs):
        src, land, (send, recv) = refs[:n], refs[n:2 * n], refs[2 * n + 1:2 * n + 3]
        x, y, c = _place()
        me = 2 * x + y
        for a in range(n):
            mine = _half_rows(shards[a].shape, c)
            for j, (px, py) in enumerate(_other_chips(x, y)):
                pltpu.make_async_remote_copy(
                    src_ref=src[a].at[mine], dst_ref=land[a].at[me, mine], send_sem=send.at[3 * a + j],
                    recv_sem=recv.at[3 * a + j], device_id=(px, py, c), device_id_type=MESH_ID).start()
        token = refs[-1]
        token[...] = jnp.zeros_like(token)

    mem = lambda t: pltpu.HBM(t.shape, t.dtype)
    pair = pltpu.SemaphoreType.DMA((3 * n,))
    outs = pl.pallas_call(
        body, name=name,
        out_shape=(pair, pair, *map(mem, shards), *map(mem, lands), SDS((SUBLANES, LANES), F32)),
        in_specs=[HBM] * (2 * n) + [ANY],
        out_specs=(SEM, SEM, *[HBM] * (2 * n), pl.BlockSpec(memory_space=pltpu.VMEM)),
        input_output_aliases={i: 2 + i for i in range(2 * n)},
        compiler_params=pltpu.CompilerParams(has_side_effects=DATAFLOW),
    )(*[pltpu.with_memory_space_constraint(t, pltpu.HBM) for t in (*shards, *lands)], after)
    return outs[0], outs[1], list(outs[2:2 + n]), list(outs[2 + n:2 + 2 * n]), outs[-1]


def _gather_pass(name, send, recv, shards, lands, after, first=0):
    n = len(shards)

    def body(*refs):
        src, land, (send, recv, _) = refs[:n], refs[n:2 * n], refs[2 * n:2 * n + 3]
        fsend, frecv = refs[2 * n + 3], refs[2 * n + 4]
        x, y, c = _place()
        me = 2 * x + y
        for a in range(n):
            mine = _half_rows(shards[a].shape, c)
            for j, (px, py) in enumerate(_other_chips(x, y)):
                far = 2 * px + py
                ici = pltpu.make_async_remote_copy(
                    src_ref=src[a].at[mine], dst_ref=land[a].at[far, mine], send_sem=send.at[3 * (first + a) + j],
                    recv_sem=recv.at[3 * (first + a) + j], device_id=(px, py, c), device_id_type=MESH_ID)
                ici.wait_recv()
                ici.wait_send()
                pltpu.make_async_remote_copy(
                    src_ref=land[a].at[far, mine], dst_ref=land[a].at[far, mine], send_sem=fsend.at[3 * a + j],
                    recv_sem=frecv.at[3 * a + j], device_id=(x, y, 1 - c), device_id_type=MESH_ID).start()
        token = refs[-1]
        token[...] = jnp.zeros_like(token)

    mem = lambda t: pltpu.HBM(t.shape, t.dtype)
    pair = pltpu.SemaphoreType.DMA((3 * n,))
    outs = pl.pallas_call(
        body, name=name,
        out_shape=(pair, pair, *map(mem, lands), SDS((SUBLANES, LANES), F32)),
        in_specs=[HBM] * (2 * n) + [SEM, SEM, ANY],
        out_specs=(SEM, SEM, *[HBM] * n, pl.BlockSpec(memory_space=pltpu.VMEM)),
        input_output_aliases={n + i: 2 + i for i in range(n)},
        compiler_params=pltpu.CompilerParams(has_side_effects=DATAFLOW),
    )(*shards, *lands, send, recv, after)
    return outs[0], outs[1], list(outs[2:2 + n]), outs[-1]


def _gather_wait(name, fsend, frecv, lands, after):
    n = len(lands)

    def body(*refs):
        land, (fsend, frecv, _) = refs[:n], refs[n:n + 3]
        x, y, c = _place()
        for a in range(n):
            for j, (px, py) in enumerate(_other_chips(x, y)):
                far = 2 * px + py
                mine = _half_rows(lands[a].shape[1:], c)
                theirs = _half_rows(lands[a].shape[1:], c, other=True)
                pltpu.make_async_remote_copy(
                    src_ref=land[a].at[far, mine], dst_ref=land[a].at[far, mine], send_sem=fsend.at[3 * a + j],
                    recv_sem=frecv.at[3 * a + j], device_id=(x, y, 1 - c), device_id_type=MESH_ID).wait_send()
                pltpu.make_async_remote_copy(
                    src_ref=land[a].at[far, theirs], dst_ref=land[a].at[far, theirs], send_sem=fsend.at[3 * a + j],
                    recv_sem=frecv.at[3 * a + j], device_id=(x, y, 1 - c), device_id_type=MESH_ID).wait_recv()

    mem = lambda t: pltpu.HBM(t.shape, t.dtype)
    return list(pl.pallas_call(
        body, name=name, out_shape=tuple(map(mem, lands)), in_specs=[HBM] * n + [SEM, SEM, ANY],
        out_specs=tuple([HBM] * n), input_output_aliases={i: i for i in range(n)},
        compiler_params=pltpu.CompilerParams(has_side_effects=DATAFLOW),
    )(*lands, fsend, frecv, after))


def _after(token):
    return _Exchange([token], [], [], lambda *_: None, lambda *_: None)


def _sum_devices(slots):
    def body(s_ref, o_ref):
        acc = s_ref[0]
        for d in range(1, N_DEV):
            acc = acc + s_ref[d]
        o_ref[...] = acc

    return pl.pallas_call(
        body, in_specs=[pl.BlockSpec(memory_space=pltpu.VMEM)], out_specs=pl.BlockSpec(memory_space=pltpu.VMEM),
        out_shape=SDS(slots.shape[1:], F32), name="sum_small",
        compiler_params=pltpu.CompilerParams(vmem_limit_bytes=32 * 1024 * 1024))(slots)


def _adam_small(ws, gs, ms, vs):
    n = len(ws)

    def body(*refs):
        for i in range(n):
            w_ref, g_ref, m_ref, v_ref = (refs[k * n + i] for k in range(4))
            outs = _adam_math(w_ref[...], g_ref[...], m_ref[...], v_ref[...])
            for k in range(3):
                refs[(4 + k) * n + i][...] = outs[k]

    vmem = pl.BlockSpec(memory_space=pltpu.VMEM)
    return pl.pallas_call(
        body, in_specs=[vmem] * (4 * n), out_specs=[vmem] * (3 * n),
        out_shape=[SDS(w.shape, F32) for w in ws] * 3, name="adam_small",
        compiler_params=pltpu.CompilerParams(vmem_limit_bytes=32 * 1024 * 1024))(*ws, *gs, *ms, *vs)


def _local_step(x, target, small, big, tb, distributed):
    dist = distributed
    me = (2 * lax.axis_index("x") + lax.axis_index("y")) if dist else 0
    tb_ssm = min(tb, 256)
    bucket = jnp.asarray(_bucket_table())
    place_own = lambda t: lax.dynamic_update_index_in_dim(lax.empty((N_CHIPS,) + t.shape, t.dtype), t, me, 0)
    if dist:
        in_legs = _gather_start("gather_in_start", [big["w_in"]], [place_own(big["w_in"])], small["d_skip"])
        names = sorted(small)
        in_token, values = lax.optimization_barrier((in_legs[4], [small[n] for n in names]))
        small = dict(zip(names, values))
    g1, g2, g3, g4 = small["norm_mix_pre"], small["norm_mix_post"], small["norm_mlp_pre"], small["norm_mlp_post"]

    keys_first = lambda t: jnp.swapaxes(t, -1, -2)
    bias = _bias_table(small["rel_bias"], bucket)
    sink_rows = keys_first(_pair_layout(jnp.broadcast_to(small["sinks"].reshape(N_HEADS, 1, 1), (N_HEADS, BLOCK, 1))))
    disc_args = (small["lam_re"], small["lam_im"], small["log_dt"], small["b_re"], small["b_im"])
    (ab_re, ab_im, bb_re, bb_im), disc_vjp = jax.vjp(_ssm_discretize, *disc_args)
    tab_f, tab_b = _scan_tables(ab_re, ab_im)
    bmat = _bf(_b_matrix(bb_re, bb_im))
    cmat = _bf(_c_matrix(small["c_re"], small["c_im"]))
    bmat_t, cmat_t = bmat.transpose(0, 2, 1), cmat.transpose(0, 2, 1)
    d_skip = small["d_skip"]

    mix = ("w_glu", "w_attn_branch", "w_ssm_branch", "w_out")
    rest = [big[n] for n in mix + ("w_ff_in", "w_ff_out")]
    if dist:
        send, recv, src, lands, _ = in_legs
        tab_f, tab_b, bias, sink_rows, bmat, cmat, bmat_t, cmat_t, rest, rest_lands = lax.optimization_barrier(
            (tab_f, tab_b, bias, sink_rows, bmat, cmat, bmat_t, cmat_t, rest, [place_own(t) for t in rest]))
        corner = lambda t: t.reshape(-1, t.shape[-1])[:1, :LANES].astype(F32)
        prepared = sum(map(corner, [tab_b, bias, sink_rows, bmat, cmat] + rest_lands), in_token[:1])
        send, recv, lands, in_passed = _gather_pass("gather_in_pass", send, recv, src, lands, prepared)
        (g_in,) = _gather_wait("gather_in_wait", send, recv, lands, in_passed)
        w_in = g_in.reshape(IN_W, D_MODEL)
    else:
        w_in = big["w_in"]
    token = None
    n_mix = len(mix)
    if dist:
        send, recv, rest, lands, token = _gather_start("gather_rest_start", rest, rest_lands, in_passed)
    h1, q, k, v, u, ga, gs = _inproj_fwd(x, g1, w_in, tb, _after(token) if dist else None)
    s, h = _ssm_fwd(u, bmat, cmat, tab_f, d_skip, tb)
    if dist:
        fsend, frecv, mix_lands, token = _gather_pass("gather_mix_pass", send, recv, rest[:n_mix], lands[:n_mix], s)
    att = _attn_fwd(q, k, v, bias, sink_rows, _after(token) if dist else None)[0]
    if dist:
        w_mix = _gather_wait("gather_mix_wait", fsend, frecv, mix_lands, att)
        fsend, frecv, ff_lands, token = _gather_pass(
            "gather_ff_pass", send, recv, rest[n_mix:], lands[n_mix:], w_mix[0], n_mix)
        rest = w_mix + ff_lands
    w_glu, w_ab, w_sb, w_out = rest[:n_mix]
    w_glu = w_glu.reshape(SSM_W, SSM_W)
    w_out = w_out.reshape(D_MODEL, D_MODEL)
    x2 = _merge_fwd(x, s, att, ga, gs, g2, w_glu, w_ab, w_sb, w_out, tb, _after(token) if dist else None)
    if dist:
        rest[n_mix:] = _gather_wait("gather_ff_wait", fsend, frecv, ff_lands, x2)
    w_ffi, w_ffo = [rest[n_mix]], rest[n_mix + 1]
    dy, df, h3, ra, loss_acc, dg4 = _mlp_fwd_loss(x2, target, g3, g4, w_ffi, w_ffo, tb)

    dx2, da, dg3 = _mlp_bwd(x2, dy, df, ra, g3, w_ffi, w_ffo, tb)
    tl = min(2048, x.shape[0])
    chunked = (N_CHIPS, D_FF // N_CHIPS, D_MODEL)
    d_ffi, b_ffi = _matmul_tn("grad_w_ff_in", h3, da, D_MODEL, D_FF // FF_CHUNKS, tl, True)
    d_ffo, b_ffo = _matmul_tn("grad_w_ff_out", ra, df, D_FF // FF_CHUNKS, D_MODEL, tl, False, square_a=True)
    d_ffo, b_ffo = d_ffo.reshape(chunked), b_ffo.reshape(chunked)
    behind = lambda flight: _after(flight.token) if dist else None
    ff_fl = _scatter_off("scatter_ff_off", [b_ffi, b_ffo], d_ffo) if dist else None
    outs = _merge_bwd(dx2, s, att, ga, gs, g2, w_glu, w_ab, w_sb, w_out, tb_ssm, behind(ff_fl))
    ds, datt, dga, dgs, dg2, d_glu, d_ab, d_sb, d_out, b_glu, b_ab, b_sb, b_out = outs
    glu4, out4 = (N_CHIPS, SSM_W // N_CHIPS, SSM_W), (N_CHIPS, D_MODEL // N_CHIPS, D_MODEL)
    d_mix = [d_glu.reshape(glu4), d_ab, d_sb, d_out.reshape(out4)]
    b_mix = [b_glu.reshape(glu4), b_ab, b_sb, b_out.reshape(out4)]
    mix_fl = _scatter_off("scatter_mix_off", b_mix, d_mix[-1]) if dist else None
    du, d_bmat, d_cmat, da_acc, dd_skip = _ssm_bwd(
        ds, u, h, bmat_t, cmat_t, tab_b, d_skip, tb, behind(mix_fl))
    dq, dk, dv, dbias, dsink_rows = _attn_bwd(q, k, v, datt, bias, sink_rows)
    swap_fl = None
    if dist:
        r_ffi, r_ffo = _land("scatter_ff_land", ff_fl, dq)[1]
        p_ffi = _sum4("sum_w_ff_in", d_ffi, r_ffi, me)
        p_ffo = _sum4("sum_w_ff_out", d_ffo, r_ffo, me)
        swap_fl = _swap_off("swap_ff_off", [p_ffi, p_ffo], r_ffo)
    dx, dpj, dg1 = _inproj_bwd(x, dx2, dq, dk, dv, du, dga, dgs, g1, w_in, tb, behind(swap_fl))

    dab_re, dab_im = _state_unlayout(jnp.sum(da_acc, axis=0))
    dbb_re, dbb_im = _b_matrix_grad(d_bmat)
    d_lam_re, d_lam_im, d_log_dt, d_b_re, d_b_im = disc_vjp((dab_re, dab_im, dbb_re, dbb_im))
    d_c_re, d_c_im = _c_matrix_grad(d_cmat)
    d_rel = _bias_grad(dbias, bucket)
    d_sinks = jnp.sum(_pair_unlayout(keys_first(dsink_rows)), axis=(1, 2))
    small_grads = dict(
        norm_mix_pre=dg1, norm_mix_post=dg2, norm_mlp_pre=dg3, norm_mlp_post=dg4, rel_bias=d_rel, sinks=d_sinks,
        lam_re=d_lam_re, lam_im=d_lam_im, log_dt=d_log_dt, b_re=d_b_re, b_im=d_b_im, c_re=d_c_re, c_im=d_c_im,
        d_skip=dd_skip)
    small_fl = _devices_off("small_off", _pack(small_grads, loss_acc), swap_fl.token) if dist else None
    outs = _matmul_tn("grad_w_in", dpj, h1, IN_W // 2, D_MODEL, tl, False, behind(small_fl))
    in4 = (N_CHIPS, IN_W // N_CHIPS, D_MODEL)
    d_in, b_in = outs[0].reshape(in4), outs[1].reshape(in4)
    if not dist:
        return loss_acc, dx, small_grads, dict(zip(BIG, [d_in] + d_mix + [d_ffi, d_ffo]))
    in_fl = _scatter_off("scatter_w_in_off", [b_in], d_in)
    (p_ffi, p_ffo), (s_ffi, s_ffo) = _land("swap_ff_land", swap_fl, in_fl.token)
    r_mix = _land("scatter_mix_land", mix_fl, in_fl.token)[1]
    p_mix = [_sum4("sum_" + n, d, r, me) for n, d, r in zip(mix, d_mix, r_mix)]
    mix_swap = _swap_off("swap_mix_off", p_mix, in_fl.token)
    pending = dict(d_in=d_in, in_fl=in_fl, mix_swap=mix_swap, w_ff_in=(p_ffi, s_ffi), w_ff_out=(p_ffo, s_ffo), me=me)
    return loss_acc, dx, small_fl, pending


SMALL = ['norm_mix_pre', 'norm_mix_post', 'norm_mlp_pre', 'norm_mlp_post', 'rel_bias', 'sinks', 'lam_re', 'lam_im',
         'log_dt', 'b_re', 'b_im', 'c_re', 'c_im', 'd_skip']
BIG = ['w_in', 'w_glu', 'w_attn_branch', 'w_ssm_branch', 'w_out', 'w_ff_in', 'w_ff_out']
WEIGHTS = ['norm_mix_pre', 'norm_mix_post', 'norm_mlp_pre', 'norm_mlp_post', 'w_in', 'rel_bias', 'sinks', 'lam_re',
           'lam_im', 'log_dt', 'b_re', 'b_im', 'c_re', 'c_im', 'd_skip', 'w_glu', 'w_attn_branch', 'w_ssm_branch',
           'w_out', 'w_ff_in', 'w_ff_out']
PACK_COLS = 1024
PACK_ORDER = ['b_re', 'b_im', 'c_re', 'c_im', 'lam_re', 'lam_im', 'norm_mix_pre', 'norm_mix_post', 'norm_mlp_pre',
              'norm_mlp_post', 'rel_bias', 'sinks', 'log_dt', 'd_skip']


STATE_MINOR = ('b_re', 'b_im')
PACK_ROWS = 144
LOSS_ROW = 140


def _pack(named, loss_acc):
    parts = []
    for n in PACK_ORDER:
        a = jnp.swapaxes(named[n], -1, -2) if n in STATE_MINOR else named[n]
        flat = a.reshape(-1)
        rows = -(-flat.shape[0] // PACK_COLS)
        parts.append(jnp.pad(flat, (0, rows * PACK_COLS - flat.shape[0])).reshape(rows, PACK_COLS))
    assert sum(p.shape[0] for p in parts) == LOSS_ROW
    parts.append(jnp.pad(loss_acc[0:1], ((0, PACK_ROWS - LOSS_ROW - 1), (0, PACK_COLS - loss_acc.shape[1]))))
    return jnp.concatenate(parts, axis=0)


def _unpack(packed, shapes):
    out, at = {}, 0
    for n in PACK_ORDER:
        shape = shapes[n][:-2] + (shapes[n][-1], shapes[n][-2]) if n in STATE_MINOR else shapes[n]
        size = int(np.prod(shape))
        rows = -(-size // PACK_COLS)
        blk = packed[at:at + rows]
        out[n] = (blk.reshape(-1)[:size] if size % PACK_COLS else blk).reshape(shape)
        at += rows
    return out


def kernel(x, norm_mix_pre, norm_mix_post, norm_mlp_pre, norm_mlp_post, w_in, rel_bias, sinks, lam_re, lam_im, log_dt, b_re, b_im, c_re, c_im, d_skip, w_glu, w_attn_branch, w_ssm_branch, w_out, w_ff_in, w_ff_out, loss_target, m_norm_mix_pre, m_norm_mix_post, m_norm_mlp_pre, m_norm_mlp_post, m_w_in, m_rel_bias, m_sinks, m_lam_re, m_lam_im, m_log_dt, m_b_re, m_b_im, m_c_re, m_c_im, m_d_skip, m_w_glu, m_w_attn_branch, m_w_ssm_branch, m_w_out, m_w_ff_in, m_w_ff_out, v_norm_mix_pre, v_norm_mix_post, v_norm_mlp_pre, v_norm_mlp_post, v_w_in, v_rel_bias, v_sinks, v_lam_re, v_lam_im, v_log_dt, v_b_re, v_b_im, v_c_re, v_c_im, v_d_skip, v_w_glu, v_w_attn_branch, v_w_ssm_branch, v_w_out, v_w_ff_in, v_w_ff_out):
    env = dict(locals())
    w = {n: env[n] for n in WEIGHTS}
    m = {n: env["m_" + n] for n in WEIGHTS}
    v = {n: env["v_" + n] for n in WEIGHTS}
    seq = x.shape[1]
    tb = min(512, seq)

    small = {n: w[n] for n in ('norm_mix_pre', 'norm_mix_post', 'norm_mlp_pre', 'norm_mlp_post', 'rel_bias')}
    small.update({n: w[n][0] for n in ('sinks', 'lam_re', 'lam_im', 'log_dt', 'b_re', 'b_im', 'c_re', 'c_im')})
    small['d_skip'] = w['d_skip']
    shard = lambda t, n: t[n][0].T if n == 'w_in' else t[n][0]
    unshard = lambda a, n: (a.T if n == 'w_in' else a)[None]
    _, dx, small_fl, pending = _local_step(
        x[0], loss_target[0], small, {n: _bf(shard(w, n)) for n in BIG}, tb, True)

    grads, deltas, new_m, new_v = {}, {}, {}, {}

    def adam(n, partials, after=None):
        outs = _adam_pair("adam_" + n, (shard(w, n), *partials, shard(m, n), shard(v, n)), after)
        grads[n], deltas[n], new_m[n], new_v[n] = [unshard(a, n) for a in outs]
        return outs[3]

    mix = ("w_glu", "w_attn_branch", "w_ssm_branch", "w_out")
    in_fl = pending["in_fl"]
    last = pending["mix_swap"].token
    for n in ("w_ff_in", "w_ff_out"):
        last = adam(n, pending[n], last)
    for n, partials in zip(mix, zip(*_land("swap_mix_land", pending["mix_swap"], last))):
        last = adam(n, partials, last)

    small_g = _sum_devices(_land("small_land", small_fl, last)[1][0])
    loss = small_g[LOSS_ROW, 0]
    minor = lambda t, n: jnp.swapaxes(t, -1, -2) if n in STATE_MINOR else t
    g_small = _unpack(small_g, {n: w[n].shape for n in SMALL})
    outs = _adam_small([minor(w[n], n) for n in SMALL], [g_small[n] for n in SMALL],
                       [minor(m[n], n) for n in SMALL], [minor(v[n], n) for n in SMALL])
    grads.update({n: minor(g_small[n], n) for n in SMALL})
    for k, dst in enumerate((deltas, new_m, new_v)):
        dst.update({n: minor(a, n) for n, a in zip(SMALL, outs[k * len(SMALL):(k + 1) * len(SMALL)])})

    (r_in,) = _land("scatter_w_in_land", in_fl, outs[0])[1]
    adam("w_in", _sum4_swap("sum_swap_w_in", pending["d_in"], r_in, pending["me"]))

    return (loss, dx[None], *[grads[n] for n in WEIGHTS], *[deltas[n] for n in WEIGHTS],
            *[new_m[n] for n in WEIGHTS], *[new_v[n] for n in WEIGHTS])
```

```python
import functools
import math

import numpy as np
import jax
import jax.numpy as jnp
from jax import lax
from jax.experimental import pallas as pl
from jax.experimental.pallas import tpu as pltpu

F32 = jnp.float32
BF16 = jnp.bfloat16

D_MODEL = 1024
N_HEADS = 8
N_KV = 2
Q_GROUP = 4
HEAD_DIM = 64
ATTN_W = 512
KV_W = 128
BLOCK = 128
N_BUCKETS = 32
MAX_DISTANCE = 128
NEG_INF = -1e30
SSM_W = 512
SSM_GROUP = 16
SSM_GROUPS = 32
SSM_STATE = 64
N_SUPER = 4
GROUPS_PER_SUPER = SSM_GROUPS // N_SUPER
SUPER_IN = GROUPS_PER_SUPER * SSM_GROUP
SUPER_HALF = GROUPS_PER_SUPER * SSM_STATE
SUPER_W = 2 * SUPER_HALF
STATE_COLS = N_SUPER * SUPER_W
D_FF = 4096
FF_CHUNKS = 4
IN_W = 3328
SPLITS = (0, 512, 640, 768, 1280, 2304, 3328)
RMS_EPS = 1e-6
N_CHIPS = 4
N_DEV = 8
SUBLANES = 8
LANES = 128
STATE_TILES = STATE_COLS // LANES
SUPER_TILES = SUPER_W // LANES

ADAM_LR = 0.001
ADAM_B1 = 0.9
ADAM_B2 = 0.999
ADAM_EPS = 1e-08
ADAM_WD = 0.01
ADAM_STEP = 10

VMEM_BIG = 56 * 1024 * 1024
SDS = jax.ShapeDtypeStruct
MESH_ID = pl.DeviceIdType.MESH
ANY = pl.BlockSpec(memory_space=pl.ANY)


def _bf(x):
    return x.astype(BF16)


def _mm(a, b):
    return jnp.dot(a, b, preferred_element_type=F32)


def _mm_nt(a, b):
    return lax.dot_general(a, b, (((1,), (1,)), ((), ())), preferred_element_type=F32)


def _mm_tn(a, b):
    return lax.dot_general(a, b, (((0,), (0,)), ((), ())), preferred_element_type=F32)


def _sig(x):
    return 1.0 / (1.0 + jnp.exp(-x))


def _rms(x, g):
    r = lax.rsqrt(jnp.mean(x * x, axis=-1, keepdims=True) + RMS_EPS)
    xh = x * r
    return xh * g, xh, r


def _rms_bwd(dout, xh, r, g):
    dg = jnp.sum(dout * xh, axis=0, keepdims=True)
    dxh = dout * g
    dx = r * (dxh - xh * jnp.mean(dxh * xh, axis=-1, keepdims=True))
    return dx, dg


_GELU_C = math.sqrt(2.0 / math.pi)


def _gelu_and_grad(x):
    x2 = x * x
    inner = _GELU_C * (x + 0.044715 * (x2 * x))
    t = jnp.tanh(inner)
    y = 0.5 * x * (1.0 + t)
    dy = 0.5 * (1.0 + t) + 0.5 * x * (1.0 - t * t) * (_GELU_C * (1.0 + 3.0 * 0.044715 * x2))
    return y, dy


def _zero_map(nd, *_):
    return (0,) * nd


def _params(n_axes, vmem=None):
    return pltpu.CompilerParams(dimension_semantics=("arbitrary",) * n_axes, vmem_limit_bytes=vmem)


class _Exchange:
    def __init__(self, ins, outs, sems, start, wait):
        self.ins, self.outs, self.sems, self.start, self.wait = list(ins), list(outs), list(sems), start, wait


def _fused_call(name, body, grid, in_specs, out_specs, out_shape, scratch, args, exchange, params):
    n_in, n_out, n_scr = len(in_specs), len(out_specs), len(scratch)
    if exchange is None:
        fn = body
    else:
        ex = exchange
        n_xi, n_xo = len(ex.ins), len(ex.outs)

        def fn(*refs):
            at = 0
            parts = []
            for n in (n_in, n_xi, n_out, n_xo, n_scr, len(ex.sems)):
                parts.append(refs[at:at + n])
                at += n
            ins, x_in, outs, x_out, scr, x_sem = parts
            ids = [pl.program_id(a) for a in range(len(grid))]
            first = functools.reduce(jnp.logical_and, [i == 0 for i in ids])
            last = functools.reduce(jnp.logical_and, [i == g - 1 for i, g in zip(ids, grid)])

            @pl.when(first)
            def _():
                ex.start(x_in, x_out, x_sem)

            body(*ins, *outs, *scr)

            @pl.when(last)
            def _():
                ex.wait(x_in, x_out, x_sem)

        in_specs = list(in_specs) + [ANY] * n_xi
        out_specs = list(out_specs) + [ANY] * n_xo
        out_shape = list(out_shape) + ex.outs
        scratch = list(scratch) + ex.sems
        args = list(args) + ex.ins
    return pl.pallas_call(fn, grid=grid, in_specs=in_specs, out_specs=out_specs, out_shape=out_shape,
                          scratch_shapes=list(scratch), name=name, compiler_params=params)(*args)


def _rowcall(name, body, seq, tb, rows, consts, row_outs, acc_outs, scratch=(), reverse=False, vmem=None,
             exchange=None):
    nb = seq // tb
    rmap = (lambda i: (nb - 1 - i, 0)) if reverse else (lambda i: (i, 0))
    tmap = lambda i: (0,) + rmap(i)

    def row_spec(width):
        if isinstance(width, tuple):
            return pl.BlockSpec((width[0], tb, width[1]), tmap)
        return pl.BlockSpec((tb, width), rmap)

    def row_shape(width):
        return (width[0], seq, width[1]) if isinstance(width, tuple) else (seq, width)

    in_specs = [row_spec(a.shape[1] if a.ndim == 2 else (a.shape[0], a.shape[2])) for a in rows]
    in_specs += [pl.BlockSpec(a.shape, functools.partial(_zero_map, a.ndim), pipeline_mode=pl.Buffered(1))
                 for a in consts]
    out_specs = [row_spec(c) for c, _ in row_outs] + [ANY] * len(acc_outs)
    out_shape = [SDS(row_shape(c), dt) for c, dt in row_outs] + [SDS(s, dt) for s, dt in acc_outs]
    n_main = len(rows) + len(consts) + len(row_outs)
    n_acc = len(acc_outs)

    def fn(*refs):
        main, acc_hbm, rest = refs[:n_main], refs[n_main:n_main + n_acc], refs[n_main + n_acc:]
        acc_vmem, own = rest[:n_acc], rest[n_acc:]
        body(*main, *acc_vmem, *own)

        @pl.when(pl.program_id(0) == nb - 1)
        def _():
            for src, dst in zip(acc_vmem, acc_hbm):
                pltpu.sync_copy(src, dst)

    buffers = [pltpu.VMEM(s, dt) for s, dt in acc_outs] + list(scratch)
    return _fused_call(name, fn if acc_outs else body, (nb,), in_specs, out_specs, out_shape, buffers,
                       [*rows, *consts], exchange, _params(1, vmem))


def _inproj_fwd(x, g1, w_in, tb, exchange=None):
    seq = x.shape[0]

    def body(x_ref, g_ref, w_ref, h_ref, q_ref, k_ref, v_ref, u_ref, ga_ref, gs_ref):
        h, _, _ = _rms(x_ref[...], g_ref[...])
        hb = _bf(h)
        h_ref[...] = hb
        pj = _mm_nt(hb, w_ref[...])
        q_ref[...] = _bf(pj[:, SPLITS[0]:SPLITS[1]])
        k_ref[...] = _bf(pj[:, SPLITS[1]:SPLITS[2]])
        v_ref[...] = _bf(pj[:, SPLITS[2]:SPLITS[3]])
        u_ref[...] = pj[:, SPLITS[3]:SPLITS[4]]
        ga_ref[...] = pj[:, SPLITS[4]:SPLITS[5]]
        gs_ref[...] = pj[:, SPLITS[5]:SPLITS[6]]

    return _rowcall("inproj_fwd", body, seq, tb, [x], [g1, w_in],
                    [(D_MODEL, BF16), (ATTN_W, BF16), (KV_W, BF16), (KV_W, BF16), (SSM_W, F32),
                     (D_MODEL, F32), (D_MODEL, F32)], [], vmem=VMEM_BIG, exchange=exchange)


def _inproj_bwd(x, dx2, dq, dk, dv, du, dga, dgs, g1, w_in, tb, exchange=None):
    seq = x.shape[0]

    def body(x_ref, dx2_ref, dq_ref, dk_ref, dv_ref, du_ref, dga_ref, dgs_ref, g_ref, w_ref,
             dx_ref, dpj_ref, dg_ref):
        @pl.when(pl.program_id(0) == 0)
        def _():
            dg_ref[...] = jnp.zeros_like(dg_ref)

        dpj = jnp.concatenate([dq_ref[...], dk_ref[...], dv_ref[...], _bf(du_ref[...]),
                               dga_ref[...], dgs_ref[...]], axis=1)
        dpj_ref[...] = dpj
        dh = _mm(dpj, w_ref[...])
        g = g_ref[...]
        _, xh, r = _rms(x_ref[...], g)
        dxn, dg = _rms_bwd(dh, xh, r, g)
        dx_ref[...] = dx2_ref[...] + dxn
        dg_ref[...] += dg

    return _rowcall("inproj_bwd", body, seq, tb, [x, dx2, dq, dk, dv, du, dga, dgs], [g1, w_in],
                    [(D_MODEL, F32), (IN_W, BF16)], [((1, D_MODEL), F32)], vmem=VMEM_BIG, exchange=exchange)


def _bucket_table():
    qi = np.arange(BLOCK)[:, None]
    kj = np.arange(2 * BLOCK)[None, :]
    dist = qi + BLOCK - kj
    max_exact = N_BUCKETS // 2
    d = np.maximum(dist, 0)
    df = np.maximum(d, 1).astype(np.float32)
    large = max_exact + (np.log(df / np.float32(max_exact)) / np.float32(math.log(MAX_DISTANCE / max_exact))
                         * np.float32(N_BUCKETS - max_exact)).astype(np.int32)
    large = np.minimum(large, N_BUCKETS - 1)
    bucket = np.where(d < max_exact, d, large)
    valid = (dist >= 0) & (dist < BLOCK)
    return np.where(valid, bucket, -1).astype(np.int32)


def _bias_table(rel_bias, bucket):
    def body(rb_ref, bk_ref, o_ref):
        bk = bk_ref[...]
        has_prev = lax.broadcasted_iota(jnp.int32, bk.shape, 1) >= BLOCK
        for h in range(N_HEADS):
            kh, j, par = h // Q_GROUP, (h // 2) % 2, h % 2
            acc = jnp.full((BLOCK, 2 * BLOCK), NEG_INF, F32)
            for b in range(N_BUCKETS):
                acc = jnp.where(bk == b, rb_ref[b, h], acc)
            o_ref[0, kh, par, :, j * BLOCK:(j + 1) * BLOCK] = jnp.where(has_prev, acc, NEG_INF).T
            o_ref[1, kh, par, :, j * BLOCK:(j + 1) * BLOCK] = acc.T

    return pl.pallas_call(
        body, out_shape=SDS((2, N_KV, 2, 2 * BLOCK, 2 * BLOCK), F32),
        in_specs=[pl.BlockSpec(memory_space=pltpu.SMEM), pl.BlockSpec(memory_space=pltpu.VMEM)],
        out_specs=pl.BlockSpec(memory_space=pltpu.VMEM), name="bias_table",
    )(rel_bias, bucket)


def _bias_grad(dbias, bucket):
    def body(db_ref, bk_ref, o_ref):
        bk = bk_ref[...]
        for h in range(N_HEADS):
            kh, j, par = h // Q_GROUP, (h // 2) % 2, h % 2
            db = db_ref[kh, par, :, j * BLOCK:(j + 1) * BLOCK].T
            for b in range(N_BUCKETS):
                o_ref[b, h] = jnp.sum(jnp.where(bk == b, db, 0.0))

    return pl.pallas_call(
        body, out_shape=SDS((N_BUCKETS, N_HEADS), F32),
        in_specs=[pl.BlockSpec(memory_space=pltpu.VMEM), pl.BlockSpec(memory_space=pltpu.VMEM)],
        out_specs=pl.BlockSpec(memory_space=pltpu.SMEM), name="bias_grad",
    )(dbias, bucket)


TILE = 2 * HEAD_DIM


def _pair_layout(t):
    lead = t.shape[:-3]
    t = t.reshape(lead + (N_KV, 2, 2) + t.shape[-2:])
    nl = len(lead)
    t = jnp.transpose(t, tuple(range(nl)) + (nl, nl + 2, nl + 1, nl + 3, nl + 4))
    return t.reshape(lead + (N_KV, 2, 2 * BLOCK, t.shape[-1]))


def _pair_unlayout(t):
    t = t.reshape(N_KV, 2, 2, BLOCK, t.shape[-1]).transpose(0, 2, 1, 3, 4)
    return t.reshape(N_HEADS, BLOCK, t.shape[-1])


def _halves(t):
    tf = t.astype(F32)
    low = lax.broadcasted_iota(jnp.int32, tf.shape, 1) < HEAD_DIM
    swapped = pltpu.roll(tf, HEAD_DIM, 1)
    zero = jnp.zeros_like(tf)
    return ((_bf(jnp.where(low, tf, zero)), _bf(jnp.where(low, zero, swapped))),
            (_bf(jnp.where(low, swapped, zero)), _bf(jnp.where(low, zero, tf))))


def _fold_halves(even, odd):
    low = lax.broadcasted_iota(jnp.int32, even.shape, 1) < HEAD_DIM
    comb = jnp.where(low, even, odd)
    return comb + pltpu.roll(comb, HEAD_DIM, 1)


def _tile_rows(ref, kh):
    return jnp.concatenate([ref[:, (2 * kh) * TILE:(2 * kh + 1) * TILE],
                            ref[:, (2 * kh + 1) * TILE:(2 * kh + 2) * TILE]], axis=0)


def _halves_t(t):
    tt = t.astype(F32).T
    top = lax.broadcasted_iota(jnp.int32, tt.shape, 0) < HEAD_DIM
    swapped = jnp.concatenate([tt[HEAD_DIM:], tt[:HEAD_DIM]], axis=0)
    zero = jnp.zeros_like(tt)
    return ((_bf(jnp.where(top, tt, zero)), _bf(jnp.where(top, zero, swapped))),
            (_bf(jnp.where(top, swapped, zero)), _bf(jnp.where(top, zero, tt))))


def _attn_probs(km, qk, bias, sink):
    lg = _mm_nt(km, qk) * (HEAD_DIM ** -0.5) + bias
    m = jnp.maximum(jnp.max(lg, axis=0, keepdims=True), sink)
    p = jnp.exp(lg - m)
    es = jnp.exp(sink - m)
    inv = 1.0 / (jnp.sum(p, axis=0, keepdims=True) + es)
    return p * inv, es * inv


def _attn_fwd(q, k, v, bias, sink_rows, exchange=None):
    seq = q.shape[0]
    nblk = seq // BLOCK

    def body(q_ref, kp_ref, kc_ref, vp_ref, vc_ref, b_ref, s_ref, o_ref):
        which = jnp.minimum(pl.program_id(0), 1)
        kms = _halves(jnp.concatenate([kp_ref[...], kc_ref[...]], axis=0))
        vts = _halves_t(jnp.concatenate([vp_ref[...], vc_ref[...]], axis=0))
        for kh in range(N_KV):
            qk = _tile_rows(q_ref, kh)
            acc = jnp.zeros((TILE, 2 * BLOCK), F32)
            for par in range(2):
                pr, _ = _attn_probs(kms[kh][par], qk, b_ref[which, kh, par], s_ref[kh, par])
                acc = acc + _mm(vts[kh][par], _bf(pr))
            acc = acc.T
            o_ref[:, (2 * kh) * TILE:(2 * kh + 1) * TILE] = _bf(acc[:BLOCK])
            o_ref[:, (2 * kh + 1) * TILE:(2 * kh + 2) * TILE] = _bf(acc[BLOCK:])

    cur = lambda n: (n, 0)
    prev = lambda n: (jnp.maximum(n - 1, 0), 0)
    return _fused_call(
        "attn_fwd", body, (nblk,),
        [pl.BlockSpec((BLOCK, ATTN_W), cur),
         pl.BlockSpec((BLOCK, KV_W), prev), pl.BlockSpec((BLOCK, KV_W), cur),
         pl.BlockSpec((BLOCK, KV_W), prev), pl.BlockSpec((BLOCK, KV_W), cur),
         pl.BlockSpec(bias.shape, functools.partial(_zero_map, bias.ndim)),
         pl.BlockSpec(sink_rows.shape, functools.partial(_zero_map, sink_rows.ndim))],
        [pl.BlockSpec((BLOCK, ATTN_W), cur)], [SDS((seq, ATTN_W), BF16)], [],
        [q, k, k, v, v, bias, sink_rows], exchange, _params(1))


def _attn_bwd(q, k, v, d_out, bias, sink_rows, exchange=None):
    seq = q.shape[0]
    nblk = seq // BLOCK

    def body(q_ref, kp_ref, kc_ref, vp_ref, vc_ref, do_ref, b_ref, s_ref,
             dq_ref, dk_ref, dv_ref, db_ref, ds_ref, ck_ref, cv_ref):
        n = pl.program_id(0)

        @pl.when(n == 0)
        def _():
            db_ref[...] = jnp.zeros_like(db_ref)
            ds_ref[...] = jnp.zeros_like(ds_ref)
            ck_ref[...] = jnp.zeros_like(ck_ref)
            cv_ref[...] = jnp.zeros_like(cv_ref)

        @pl.when(n < nblk)
        def _():
            which = jnp.minimum(n, 1)
            scale = HEAD_DIM ** -0.5
            kcat = jnp.concatenate([kp_ref[...], kc_ref[...]], axis=0)
            kms = _halves(kcat)
            kts = _halves_t(kcat)
            vms = _halves(jnp.concatenate([vp_ref[...], vc_ref[...]], axis=0))
            dks, dvs = [], []
            for kh in range(N_KV):
                qk = _tile_rows(q_ref, kh)
                dok = _tile_rows(do_ref, kh)
                dq = jnp.zeros((TILE, 2 * BLOCK), F32)
                dkp, dvp = [], []
                for par in range(2):
                    pr, ps = _attn_probs(kms[kh][par], qk, b_ref[which, kh, par], s_ref[kh, par])
                    dp = _mm_nt(vms[kh][par], dok)
                    rs = jnp.sum(pr * dp, axis=0, keepdims=True)
                    dlg = pr * (dp - rs)
                    ds_ref[kh, par] += -ps * rs
                    db_ref[kh, par] += dlg
                    dlb = _bf(dlg)
                    dq = dq + _mm(kts[kh][par], dlb)
                    dkp.append(_mm(dlb, qk))
                    dvp.append(_mm(_bf(pr), dok))
                dq = _bf((dq * scale).T)
                dq_ref[:, (2 * kh) * TILE:(2 * kh + 1) * TILE] = dq[:BLOCK]
                dq_ref[:, (2 * kh + 1) * TILE:(2 * kh + 2) * TILE] = dq[BLOCK:]
                dks.append(_fold_halves(*dkp))
                dvs.append(_fold_halves(*dvp))
            low = lax.broadcasted_iota(jnp.int32, (2 * BLOCK, TILE), 1) < HEAD_DIM
            dkk = jnp.where(low, dks[0], dks[1]) * scale
            dvv = jnp.where(low, dvs[0], dvs[1])
            dk_ref[...] = _bf(ck_ref[...] + dkk[:BLOCK])
            ck_ref[...] = dkk[BLOCK:]
            dv_ref[...] = _bf(cv_ref[...] + dvv[:BLOCK])
            cv_ref[...] = dvv[BLOCK:]

        @pl.when(n == nblk)
        def _():
            dk_ref[...] = _bf(ck_ref[...])
            dv_ref[...] = _bf(cv_ref[...])

    cur = lambda n: (jnp.minimum(n, nblk - 1), 0)
    prev = lambda n: (jnp.maximum(jnp.minimum(n, nblk - 1) - 1, 0), 0)
    late = lambda n: (jnp.maximum(n - 1, 0), 0)
    kv_spec = lambda m: pl.BlockSpec((BLOCK, KV_W), m)
    acc_b = pl.BlockSpec(bias.shape[1:], functools.partial(_zero_map, bias.ndim - 1))
    acc_s = pl.BlockSpec(sink_rows.shape, functools.partial(_zero_map, sink_rows.ndim))
    return _fused_call(
        "attn_bwd", body, (nblk + 1,),
        [pl.BlockSpec((BLOCK, ATTN_W), cur), kv_spec(prev), kv_spec(cur), kv_spec(prev), kv_spec(cur),
         pl.BlockSpec((BLOCK, ATTN_W), cur),
         pl.BlockSpec(bias.shape, functools.partial(_zero_map, bias.ndim)), acc_s],
        [pl.BlockSpec((BLOCK, ATTN_W), cur), kv_spec(late), kv_spec(late), acc_b, acc_s],
        [SDS((seq, ATTN_W), BF16), SDS((seq, KV_W), BF16), SDS((seq, KV_W), BF16),
         SDS(bias.shape[1:], F32), SDS(sink_rows.shape, F32)],
        [pltpu.VMEM((BLOCK, KV_W), F32), pltpu.VMEM((BLOCK, KV_W), F32)],
        [q, k, k, v, v, d_out, bias, sink_rows], exchange, _params(1))


def _ssm_discretize(lam_re, lam_im, log_dt, b_re, b_im):
    dt = jnp.exp(log_dt)[:, None]
    mag = jnp.exp(lam_re * dt)
    ab_re = mag * jnp.cos(lam_im * dt)
    ab_im = mag * jnp.sin(lam_im * dt)
    nr = ab_re - 1.0
    den = lam_re * lam_re + lam_im * lam_im
    f_re = (nr * lam_re + ab_im * lam_im) / den
    f_im = (ab_im * lam_re - nr * lam_im) / den
    bb_re = f_re[..., None] * b_re - f_im[..., None] * b_im
    bb_im = f_re[..., None] * b_im + f_im[..., None] * b_re
    return ab_re, ab_im, bb_re, bb_im


def _state_layout(re, im):
    lead = re.shape[:-2]
    z = jnp.stack([re, im], axis=-3).reshape(lead + (2, N_SUPER, GROUPS_PER_SUPER, SSM_STATE))
    return jnp.moveaxis(z, -4, -3).reshape(lead + (STATE_COLS,))


def _state_unlayout(vec):
    z = vec.reshape(N_SUPER, 2, GROUPS_PER_SUPER, SSM_STATE).transpose(1, 0, 2, 3)
    z = z.reshape(2, SSM_GROUPS, SSM_STATE)
    return z[0], z[1]


SEG = 4
WINDOW = SEG * SUBLANES


def _scan_tables(ab_re, ab_im):
    pw = [None, (ab_re, ab_im)]
    for _ in range(2, WINDOW + 1):
        pr, pi_ = pw[-1]
        pw.append((pr * ab_re - pi_ * ab_im, pr * ab_im + pi_ * ab_re))
    fwd = np.zeros((7, SUBLANES), np.int64)
    bwd = np.zeros((7, SUBLANES), np.int64)
    for k, shift in enumerate((1, 2, 4)):
        fwd[k] = [SEG * shift if r >= shift else 0 for r in range(SUBLANES)]
        bwd[k] = [SEG * shift if r < SUBLANES - shift else 0 for r in range(SUBLANES)]
    fwd[3] = [SEG * (r + 1) for r in range(SUBLANES)]
    bwd[3] = [SEG * (SUBLANES - r) for r in range(SUBLANES)]
    for k in range(1, SEG):
        fwd[3 + k] = bwd[3 + k] = k
    used = sorted((set(fwd.ravel()) | set(bwd.ravel())) - {0})
    select = lambda which: np.stack([(which == p) for p in used], axis=-1).astype(np.float32)
    stacked = _state_layout(jnp.stack([pw[p][0] for p in used]), jnp.stack([pw[p][1] for p in used]))
    conj_sign = np.where((np.arange(STATE_COLS) // SUPER_HALF) % 2 == 1, -1.0, 1.0).astype(np.float32)
    pick = functools.partial(jnp.einsum, 'krp,pc->krc', precision=lax.Precision.HIGHEST)
    return pick(select(fwd), stacked), pick(select(bwd), stacked) * conj_sign


_EYE = np.eye(GROUPS_PER_SUPER, dtype=np.float32)


def _b_matrix(bb_re, bb_im):
    bb = jnp.stack([bb_re, bb_im]).reshape(2, N_SUPER, GROUPS_PER_SUPER, SSM_STATE, SSM_GROUP)
    m = jnp.einsum('rsgpc,gh->sgcrhp', bb, _EYE)
    return m.reshape(N_SUPER, SUPER_IN, SUPER_W)


def _b_matrix_grad(dm):
    d = dm.reshape(N_SUPER, GROUPS_PER_SUPER, SSM_GROUP, 2, GROUPS_PER_SUPER, SSM_STATE)
    d = jnp.sum(d * _EYE[None, :, None, None, :, None], axis=4)
    d = d.transpose(3, 0, 1, 4, 2).reshape(2, SSM_GROUPS, SSM_STATE, SSM_GROUP)
    return d[0], d[1]


def _c_matrix(c_re, c_im):
    cc = jnp.stack([c_re, -c_im]).reshape(2, N_SUPER, GROUPS_PER_SUPER, SSM_GROUP, SSM_STATE)
    m = jnp.einsum('rsgcp,gh->srgphc', cc, _EYE)
    return m.reshape(N_SUPER, SUPER_W, SUPER_IN)


def _c_matrix_grad(dm):
    d = dm.reshape(N_SUPER, 2, GROUPS_PER_SUPER, SSM_STATE, GROUPS_PER_SUPER, SSM_GROUP)
    d = jnp.sum(d * _EYE[None, None, :, None, :, None], axis=4)
    d = d.transpose(1, 0, 2, 4, 3).reshape(2, SSM_GROUPS, SSM_GROUP, SSM_STATE)
    return d[0], -d[1]


def _cmul_add(xr, xi, ar, ai, sr, si):
    return xr + ar * sr - ai * si, xi + ar * si + ai * sr


def _scan_rows(buf_ref, tab_ref, carry_ref, n_windows, reverse, h_ref=None, da_ref=None):
    order = list(range(SEG - 1, -1, -1)) if reverse else list(range(SEG))
    near = SUBLANES - 1 if reverse else 0
    far = 0 if reverse else SUBLANES - 1
    s_in = SUBLANES - 1 if reverse else 1
    lanes = lambda tile: pl.ds(tile * LANES, LANES)

    def window(w0, tile_re, tile_im, c_re, c_im, acc):
        rows = lambda t: pl.ds(w0 + t, SUBLANES, stride=SEG)
        get = lambda ref, t: (ref.at[tile_re][rows(t), :], ref.at[tile_im][rows(t), :])
        tab = lambda k: (tab_ref[k, :, lanes(tile_re)], tab_ref[k, :, lanes(tile_im)])

        def put(t, xr, xi):
            buf_ref.at[tile_re][rows(t), :] = xr
            buf_ref.at[tile_im][rows(t), :] = xi

        a1 = tab(4)
        er, ei = get(buf_ref, order[0])
        for t in order[1:]:
            er, ei = _cmul_add(*get(buf_ref, t), *a1, er, ei)
            if t != order[-1]:
                put(t, er, ei)
        for k, shift in enumerate((1, 2, 4)):
            s = (SUBLANES - shift) if reverse else shift
            er, ei = _cmul_add(er, ei, *tab(k), pltpu.roll(er, s, 0), pltpu.roll(ei, s, 0))
        er, ei = _cmul_add(er, ei, *tab(3), c_re, c_im)
        put(order[-1], er, ei)
        sub = lax.broadcasted_iota(jnp.int32, er.shape, 0)
        in_re = jnp.where(sub == near, c_re, pltpu.roll(er, s_in, 0))
        in_im = jnp.where(sub == near, c_im, pltpu.roll(ei, s_in, 0))
        true = {order[-1]: (er, ei)}
        for idx, t in enumerate(order[:-1]):
            true[t] = _cmul_add(*get(buf_ref, t), *tab(4 + idx), in_re, in_im)
            put(t, *true[t])
        carry = (jnp.broadcast_to(er[far:far + 1], er.shape), jnp.broadcast_to(ei[far:far + 1], ei.shape))
        if acc is None:
            return carry, None
        acc_re, acc_im = acc
        for t in range(SEG):
            if t + 1 < SEG:
                gr, gim = true[t + 1]
            else:
                gr = jnp.where(sub == SUBLANES - 1, c_re, pltpu.roll(true[0][0], SUBLANES - 1, 0))
                gim = jnp.where(sub == SUBLANES - 1, c_im, pltpu.roll(true[0][1], SUBLANES - 1, 0))
            hr, hi = get(h_ref, t)
            acc_re = acc_re + gr * hr + gim * hi
            acc_im = acc_im + gim * hr - gr * hi
        return carry, (acc_re, acc_im)

    half = SUPER_HALF // LANES
    per = 2 if h_ref is None else 4
    for sb in range(N_SUPER):
        pairs = [(2 * half * sb + j, 2 * half * sb + half + j) for j in range(half)]

        def step(wi, state, pairs=pairs):
            w = (n_windows - 1 - wi) if reverse else wi
            w0 = pl.multiple_of(w * WINDOW, WINDOW)
            out = []
            for j, (tile_re, tile_im) in enumerate(pairs):
                mine = state[per * j:per * (j + 1)]
                carry, acc = window(w0, tile_re, tile_im, mine[0], mine[1], mine[2:] or None)
                out += list(carry) + list(acc or ())
            return tuple(out)

        init = []
        for tile_re, tile_im in pairs:
            init += [carry_ref[:, lanes(tile_re)], carry_ref[:, lanes(tile_im)]]
            if h_ref is not None:
                init += [da_ref[:, lanes(tile_re)], da_ref[:, lanes(tile_im)]]
        fin = lax.fori_loop(0, n_windows, step, tuple(init))
        for j, (tile_re, tile_im) in enumerate(pairs):
            carry_ref[:, lanes(tile_re)] = fin[per * j]
            carry_ref[:, lanes(tile_im)] = fin[per * j + 1]
            if h_ref is not None:
                da_ref[:, lanes(tile_re)] = fin[per * j + 2]
                da_ref[:, lanes(tile_im)] = fin[per * j + 3]


def _put_tiles(ref, sb, block):
    for j in range(SUPER_TILES):
        ref[sb * SUPER_TILES + j] = block[:, j * LANES:(j + 1) * LANES]


def _get_tiles(ref, sb):
    return jnp.concatenate([ref[sb * SUPER_TILES + j] for j in range(SUPER_TILES)], axis=1)


def _ssm_fwd(u, bmat, cmat, tab, d_skip, tb, exchange=None):
    seq = u.shape[0]

    def body(u_ref, b_ref, c_ref, t_ref, d_ref, s_ref, h_ref, carry_ref):
        @pl.when(pl.program_id(0) == 0)
        def _():
            carry_ref[...] = jnp.zeros_like(carry_ref)

        u_blk = u_ref[...]
        ub = _bf(u_blk)
        for sb in range(N_SUPER):
            _put_tiles(h_ref, sb, _mm(ub[:, sb * SUPER_IN:(sb + 1) * SUPER_IN], b_ref[sb]))
        _scan_rows(h_ref, t_ref, carry_ref, tb // WINDOW, False)
        ys = [_mm(_bf(_get_tiles(h_ref, sb)), c_ref[sb]) for sb in range(N_SUPER)]
        s_ref[...] = jnp.concatenate(ys, axis=1) + d_ref[...] * u_blk

    return _rowcall("ssm_fwd", body, seq, tb, [u], [bmat, cmat, tab, d_skip],
                    [(SSM_W, F32), ((STATE_TILES, LANES), F32)], [],
                    scratch=[pltpu.VMEM((SUBLANES, STATE_COLS), F32)], vmem=VMEM_BIG, exchange=exchange)


def _ssm_bwd(ds, u, h, bmat_t, cmat_t, tab, d_skip, tb, exchange=None):
    seq = u.shape[0]

    def body(ds_ref, u_ref, h_ref, bt_ref, ct_ref, t_ref, d_ref,
             du_ref, db_ref, dc_ref, da_ref, dd_ref, g_ref, carry_ref):
        @pl.when(pl.program_id(0) == 0)
        def _():
            carry_ref[...] = jnp.zeros_like(carry_ref)
            db_ref[...] = jnp.zeros_like(db_ref)
            dc_ref[...] = jnp.zeros_like(dc_ref)
            da_ref[...] = jnp.zeros_like(da_ref)
            dd_ref[...] = jnp.zeros_like(dd_ref)

        ds_blk = ds_ref[...]
        dsb = _bf(ds_blk)
        u_blk = u_ref[...]
        ub = _bf(u_blk)
        for sb in range(N_SUPER):
            _put_tiles(g_ref, sb, _mm(dsb[:, sb * SUPER_IN:(sb + 1) * SUPER_IN], ct_ref[sb]))
        _scan_rows(g_ref, t_ref, carry_ref, tb // WINDOW, True, h_ref=h_ref, da_ref=da_ref)
        dus = []
        for sb in range(N_SUPER):
            gb = _bf(_get_tiles(g_ref, sb))
            dus.append(_mm(gb, bt_ref[sb]))
            db_ref[sb] += _mm_tn(ub[:, sb * SUPER_IN:(sb + 1) * SUPER_IN], gb)
            dc_ref[sb] += _mm_tn(_bf(_get_tiles(h_ref, sb)), dsb[:, sb * SUPER_IN:(sb + 1) * SUPER_IN])
        du_ref[...] = jnp.concatenate(dus, axis=1) + d_ref[...] * ds_blk
        dd_ref[...] += jnp.sum(ds_blk * u_blk, axis=0, keepdims=True)

    return _rowcall("ssm_bwd", body, seq, tb, [ds, u, h], [bmat_t, cmat_t, tab, d_skip],
                    [(SSM_W, F32)],
                    [((N_SUPER, SUPER_IN, SUPER_W), F32), ((N_SUPER, SUPER_W, SUPER_IN), F32),
                     ((SUBLANES, STATE_COLS), F32), ((1, SSM_W), F32)],
                    scratch=[pltpu.VMEM((STATE_TILES, tb, LANES), F32), pltpu.VMEM((SUBLANES, STATE_COLS), F32)],
                    reverse=True, vmem=VMEM_BIG, exchange=exchange)


def _merge_core(s, attb, ga, gs, wg_ref, wab_ref, wsb_ref, wout_ref):
    zg, dgelu = _gelu_and_grad(s)
    zgb = _bf(zg)
    sg = _sig(_mm(zgb, wg_ref[...]))
    z = zg * sg
    zb = _bf(z)
    ys = jnp.concatenate([_mm(zb, wsb_ref[j]) for j in range(N_CHIPS)], axis=1)
    ya = jnp.concatenate([_mm(attb, wab_ref[j]) for j in range(N_CHIPS)], axis=1)
    sa = _sig(ga)
    ss = _sig(gs)
    mgb = _bf(sa * ya + ss * ys)
    o = _mm(mgb, wout_ref[...])
    return dict(zg=zg, dgelu=dgelu, zgb=zgb, sg=sg, zb=zb, ys=ys, ya=ya, sa=sa, ss=ss, mgb=mgb, o=o)


def _merge_fwd(x, s, att, ga, gs, g2, w_glu, w_ab, w_sb, w_out, tb, exchange=None):
    seq = x.shape[0]

    def body(x_ref, s_ref, att_ref, ga_ref, gs_ref, g_ref, wg_ref, wab_ref, wsb_ref, wout_ref, x2_ref):
        f = _merge_core(s_ref[...], att_ref[...], ga_ref[...], gs_ref[...], wg_ref, wab_ref, wsb_ref, wout_ref)
        n, _, _ = _rms(f["o"], g_ref[...])
        x2_ref[...] = x_ref[...] + n

    return _rowcall("merge_fwd", body, seq, tb, [x, s, att, ga, gs], [g2, w_glu, w_ab, w_sb, w_out],
                    [(D_MODEL, F32)], [], vmem=VMEM_BIG, exchange=exchange)[0]


def _merge_bwd(dx2, s, att, ga, gs, g2, w_glu, w_ab, w_sb, w_out, tb, exchange=None):
    seq = s.shape[0]
    cw = D_MODEL // N_CHIPS
    last = seq // tb - 1

    def body(dx2_ref, s_ref, att_ref, ga_ref, gs_ref, g_ref, wg_ref, wab_ref, wsb_ref, wout_ref,
             ds_ref, datt_ref, dga_ref, dgs_ref, dg_ref, dwg_ref, dwab_ref, dwsb_ref, dwout_ref,
             bwg_ref, bwab_ref, bwsb_ref, bwout_ref):
        @pl.when(pl.program_id(0) == 0)
        def _():
            for r in (dg_ref, dwg_ref, dwab_ref, dwsb_ref, dwout_ref):
                r[...] = jnp.zeros_like(r)

        attb = att_ref[...]
        f = _merge_core(s_ref[...], attb, ga_ref[...], gs_ref[...], wg_ref, wab_ref, wsb_ref, wout_ref)
        g = g_ref[...]
        _, oh, r2 = _rms(f["o"], g)
        do, dg = _rms_bwd(dx2_ref[...], oh, r2, g)
        dg_ref[...] += dg
        dob = _bf(do)
        dwout_ref[...] += _mm_tn(f["mgb"], dob)
        dmg = _mm_nt(dob, wout_ref[...])
        sa, ss = f["sa"], f["ss"]
        dyab = _bf(dmg * sa)
        dysb = _bf(dmg * ss)
        dga_ref[...] = _bf(dmg * f["ya"] * sa * (1.0 - sa))
        dgs_ref[...] = _bf(dmg * f["ys"] * ss * (1.0 - ss))
        dwab = _mm_tn(attb, dyab)
        dwsb = _mm_tn(f["zb"], dysb)
        datt = jnp.zeros((tb, ATTN_W), F32)
        dz = jnp.zeros((tb, SSM_W), F32)
        for j in range(N_CHIPS):
            dwab_ref[j] += dwab[:, j * cw:(j + 1) * cw]
            dwsb_ref[j] += dwsb[:, j * cw:(j + 1) * cw]
            datt = datt + _mm_nt(dyab[:, j * cw:(j + 1) * cw], wab_ref[j])
            dz = dz + _mm_nt(dysb[:, j * cw:(j + 1) * cw], wsb_ref[j])
        datt_ref[...] = _bf(datt)
        sg, zg = f["sg"], f["zg"]
        dglb = _bf(dz * zg * sg * (1.0 - sg))
        dwg_ref[...] += _mm_tn(f["zgb"], dglb)
        dzg = dz * sg + _mm_nt(dglb, wg_ref[...])
        ds_ref[...] = dzg * f["dgelu"]

        @pl.when(pl.program_id(0) == last)
        def _():
            for dst, src in ((bwg_ref, dwg_ref), (bwab_ref, dwab_ref), (bwsb_ref, dwsb_ref), (bwout_ref, dwout_ref)):
                dst[...] = _bf(src[...])

    shapes = [w_glu.shape, w_ab.shape, w_sb.shape, w_out.shape]
    return _rowcall("merge_bwd", body, seq, tb, [dx2, s, att, ga, gs], [g2, w_glu, w_ab, w_sb, w_out],
                    [(SSM_W, F32), (ATTN_W, BF16), (D_MODEL, BF16), (D_MODEL, BF16)],
                    [((1, D_MODEL), F32)] + [(sh, F32) for sh in shapes] + [(sh, BF16) for sh in shapes],
                    vmem=VMEM_BIG, exchange=exchange)


def _mlp_fwd_loss(x2, target, g3, g4, w_ffi, w_ffo, tb):
    seq = x2.shape[0]
    n_slab = len(w_ffi)
    sw = D_FF // FF_CHUNKS // n_slab

    def body(x2_ref, t_ref, g3_ref, g4_ref, *rest):
        wi_refs, (wo_ref, dy_ref, df_ref, h_ref, ra_ref, loss_ref, dg_ref) = rest[:n_slab], rest[n_slab:]

        @pl.when(pl.program_id(0) == 0)
        def _():
            loss_ref[...] = jnp.zeros_like(loss_ref)
            dg_ref[...] = jnp.zeros_like(dg_ref)

        x2_blk = x2_ref[...]
        h3, _, _ = _rms(x2_blk, g3_ref[...])
        hb = _bf(h3)
        h_ref[...] = hb
        f = jnp.zeros((tb, D_MODEL), F32)
        for j in range(FF_CHUNKS):
            for k in range(n_slab):
                ra = jnp.maximum(_mm(hb, wi_refs[k][j]), 0.0)
                ra_ref[:, pl.ds((j * n_slab + k) * sw, sw)] = _bf(ra)
                f = f + _mm(_bf(ra * ra), wo_ref[j, pl.ds(k * sw, sw), :])
        g4 = g4_ref[...]
        n4, fh, r4 = _rms(f, g4)
        e = (x2_blk + n4) - t_ref[...]
        loss_ref[...] += 0.5 * jnp.sum(jnp.mean(e * e, axis=-1, keepdims=True))
        dy = e * (1.0 / D_MODEL)
        dy_ref[...] = dy
        df, dg = _rms_bwd(dy, fh, r4, g4)
        df_ref[...] = _bf(df)
        dg_ref[...] += dg

    return _rowcall("mlp_fwd_loss", body, seq, tb, [x2, target], [g3, g4, *w_ffi, w_ffo],
                    [(D_MODEL, F32), (D_MODEL, BF16), (D_MODEL, BF16), (D_FF, BF16)],
                    [((SUBLANES, 128), F32), ((1, D_MODEL), F32)], vmem=VMEM_BIG)


def _mlp_bwd(x2, dy, df, ra, g3, w_ffi, w_ffo, tb):
    seq = x2.shape[0]
    n_slab = len(w_ffi)
    sw = D_FF // FF_CHUNKS // n_slab

    def body(x2_ref, dy_ref, df_ref, ra_ref, g3_ref, *rest):
        wi_refs, (wo_ref, dx_ref, da_ref, dg_ref) = rest[:n_slab], rest[n_slab:]

        @pl.when(pl.program_id(0) == 0)
        def _():
            dg_ref[...] = jnp.zeros_like(dg_ref)

        dfb = df_ref[...]
        dh = jnp.zeros((tb, D_MODEL), F32)
        for j in range(FF_CHUNKS):
            for k in range(n_slab):
                cols = pl.ds((j * n_slab + k) * sw, sw)
                ra = ra_ref[:, cols].astype(F32)
                dab = _bf(_mm_nt(dfb, wo_ref[j, pl.ds(k * sw, sw), :]) * (2.0 * ra))
                da_ref[:, cols] = dab
                dh = dh + _mm_nt(dab, wi_refs[k][j])
        g3 = g3_ref[...]
        _, xh, r3 = _rms(x2_ref[...], g3)
        dxn, dg = _rms_bwd(dh, xh, r3, g3)
        dx_ref[...] = dy_ref[...] + dxn
        dg_ref[...] += dg

    return _rowcall("mlp_bwd", body, seq, tb, [x2, dy, df, ra], [g3, *w_ffi, w_ffo],
                    [(D_MODEL, F32), (D_FF, BF16)], [((1, D_MODEL), F32)], vmem=VMEM_BIG)


def _matmul_tn(name, a, b, tk, tn, tl, chunk_major, exchange=None, square_a=False):
    seq, kdim = a.shape
    ndim = b.shape[1]
    last = seq // tl - 1

    def body(a_ref, b_ref, o_ref, ob_ref):
        @pl.when(pl.program_id(2) == 0)
        def _():
            o_ref[...] = jnp.zeros_like(o_ref)

        a_blk = a_ref[...]
        if square_a:
            a_blk = _bf(jnp.square(a_blk.astype(F32)))
        o_ref[...] += _mm_tn(a_blk, b_ref[...])

        @pl.when(pl.program_id(2) == last)
        def _():
            ob_ref[...] = _bf(o_ref[...])

    if chunk_major:
        shape = (ndim // tn, kdim, tn)
        out_spec = pl.BlockSpec((None, tk, tn), lambda k, n, l: (n, k, 0))
    else:
        shape = (kdim, ndim)
        out_spec = pl.BlockSpec((tk, tn), lambda k, n, l: (k, n))
    return _fused_call(
        name, body, (kdim // tk, ndim // tn, seq // tl),
        [pl.BlockSpec((tl, tk), lambda k, n, l: (l, k)), pl.BlockSpec((tl, tn), lambda k, n, l: (l, n))],
        [out_spec, out_spec], [SDS(shape, F32), SDS(shape, BF16)], [], [a, b], exchange, _params(3, VMEM_BIG))


def _ew_call(name, fn, ins, n_out, after=None):
    rows, cols = ins[0].shape
    tr = rows
    while tr * cols * 4 > min(1 << 20, (9 << 20) // (len(ins) + n_out)) and tr % 16 == 0:
        tr //= 2
    spec = pl.BlockSpec((tr, cols), lambda i: (i, 0))
    extra = [] if after is None else [after]

    def body(*refs):
        outs = fn(*[r[...] for r in refs[:len(ins)]])
        for r, o in zip(refs[len(ins) + len(extra):], outs):
            r[...] = o

    return pl.pallas_call(
        body, grid=(rows // tr,), in_specs=[spec] * len(ins) + [ANY] * len(extra), out_specs=[spec] * n_out,
        out_shape=[SDS((rows, cols), F32)] * n_out, name=name, compiler_params=_params(1))(*ins, *extra)


def _adam_math(w, g, m, v):
    m2 = ADAM_B1 * m + (1.0 - ADAM_B1) * g
    v2 = ADAM_B2 * v + (1.0 - ADAM_B2) * (g * g)
    m_hat = m2 / (1.0 - ADAM_B1 ** ADAM_STEP)
    v_hat = v2 / (1.0 - ADAM_B2 ** ADAM_STEP)
    delta = -ADAM_LR * (m_hat / (jnp.sqrt(v_hat) + ADAM_EPS) + ADAM_WD * w)
    return delta, m2, v2


def _sum4(name, own, recv, idx):
    _, rows, cols = own.shape
    tr = rows
    while tr * cols * 4 > (1 << 20) and tr % 16 == 0:
        tr //= 2

    def body(idx_ref, o_ref, r0_ref, r1_ref, r2_ref, out_ref):
        out_ref[...] = ((o_ref[...] + r0_ref[...].astype(F32)) + r1_ref[...].astype(F32)) + r2_ref[...].astype(F32)

    blk = (None, tr, cols)
    grid_spec = pltpu.PrefetchScalarGridSpec(
        num_scalar_prefetch=1, grid=(rows // tr,),
        in_specs=[pl.BlockSpec(blk, lambda i, s: (s[0], i, 0)), pl.BlockSpec(blk, lambda i, s: (0, i, 0)),
                  pl.BlockSpec(blk, lambda i, s: (1, i, 0)), pl.BlockSpec(blk, lambda i, s: (2, i, 0))],
        out_specs=pl.BlockSpec((tr, cols), lambda i, s: (i, 0)))
    return pl.pallas_call(body, grid_spec=grid_spec, out_shape=SDS((rows, cols), F32), name=name,
                          compiler_params=_params(1))(jnp.reshape(idx, (1,)).astype(jnp.int32), own, recv, recv, recv)


def _sum4_swap(name, own, recv, idx, n_blocks=4):
    _, rows, cols = own.shape
    tr = rows // n_blocks
    assert tr * n_blocks == rows and tr % SUBLANES == 0

    def body(idx_ref, o_ref, r0_ref, r1_ref, r2_ref, mine_ref, theirs_ref, buf, kept, sent, arrived):
        i = pl.program_id(0)
        slot = lax.rem(i, 2)
        x, y, c = _place()

        def copies(j, s):
            block = pl.ds(j * tr, tr)
            return (pltpu.make_async_copy(buf.at[s], mine_ref.at[block], kept.at[s]),
                    pltpu.make_async_remote_copy(
                        src_ref=buf.at[s], dst_ref=theirs_ref.at[block], send_sem=sent.at[s], recv_sem=arrived.at[j],
                        device_id=(x, y, 1 - c), device_id_type=MESH_ID))

        def finish(j, s):
            keep, send = copies(j, s)
            keep.wait()
            send.wait_send()
            send.wait_recv()

        @pl.when(i >= 2)
        def _():
            finish(i - 2, slot)

        buf[slot] = ((o_ref[...] + r0_ref[...].astype(F32)) + r1_ref[...].astype(F32)) + r2_ref[...].astype(F32)
        for cp in copies(i, slot):
            cp.start()

        @pl.when(i == n_blocks - 1)
        def _():
            if n_blocks > 1:
                finish(i - 1, 1 - slot)
            finish(i, slot)

    blk = (None, tr, cols)
    grid_spec = pltpu.PrefetchScalarGridSpec(
        num_scalar_prefetch=1, grid=(n_blocks,),
        in_specs=[pl.BlockSpec(blk, lambda i, s: (s[0], i, 0)), pl.BlockSpec(blk, lambda i, s: (0, i, 0)),
                  pl.BlockSpec(blk, lambda i, s: (1, i, 0)), pl.BlockSpec(blk, lambda i, s: (2, i, 0))],
        out_specs=[HBM, HBM],
        scratch_shapes=[pltpu.VMEM((2, tr, cols), F32), pltpu.SemaphoreType.DMA((2,)), pltpu.SemaphoreType.DMA((2,)),
                        pltpu.SemaphoreType.DMA((n_blocks,))])
    return pl.pallas_call(body, grid_spec=grid_spec, out_shape=[SDS((rows, cols), F32)] * 2, name=name,
                          compiler_params=_params(1))(jnp.reshape(idx, (1,)).astype(jnp.int32), own, recv, recv, recv)


def _adam_pair(name, item, after=None):
    def fn(w_, a, b, m_, v_):
        g = a + b
        return (g,) + _adam_math(w_, g, m_, v_)

    return _ew_call(name, fn, list(item), 4, after)


def _place():
    return lax.axis_index("x"), lax.axis_index("y"), lax.axis_index("c")


def _other_chips(x, y):
    return [(1 - x, y), (x, 1 - y), (1 - x, 1 - y)]


HBM = pl.BlockSpec(memory_space=pltpu.HBM)
SEM = pl.BlockSpec(memory_space=pltpu.SEMAPHORE)
DATAFLOW = pltpu.SideEffectType.DATAFLOW_SIDE_EFFECTING


class _Flight:
    def __init__(self, copies, n_copies, send, recv, srcs, lands, token):
        self.copies, self.n, self.send, self.recv = copies, n_copies, send, recv
        self.srcs, self.lands, self.token = list(srcs), list(lands), token


def _take_off(name, srcs, lands, copies, n_copies, after):
    n_s, n_l = len(srcs), len(lands)

    def body(*refs):
        src, land = refs[:n_s], refs[n_s:n_s + n_l]
        send, recv = refs[n_s + n_l + 1:n_s + n_l + 3]
        for cp in copies(src, land, send, recv):
            cp.start()
        refs[-1][...] = jnp.zeros_like(refs[-1])

    mem = lambda t: pltpu.HBM(t.shape, t.dtype)
    sems = pltpu.SemaphoreType.DMA((n_copies,))
    outs = pl.pallas_call(
        body, name=name,
        out_shape=(sems, sems, *map(mem, srcs), *map(mem, lands), SDS((SUBLANES, LANES), F32)),
        in_specs=[HBM] * (n_s + n_l) + [ANY],
        out_specs=(SEM, SEM, *[HBM] * (n_s + n_l), pl.BlockSpec(memory_space=pltpu.VMEM)),
        input_output_aliases={i: 2 + i for i in range(n_s + n_l)},
        compiler_params=pltpu.CompilerParams(has_side_effects=DATAFLOW),
    )(*[pltpu.with_memory_space_constraint(t, pltpu.HBM) for t in (*srcs, *lands)], after)
    return _Flight(copies, n_copies, outs[0], outs[1], outs[2:2 + n_s], outs[2 + n_s:2 + n_s + n_l], outs[-1])


def _land(name, flight, after):
    n_s, n_l = len(flight.srcs), len(flight.lands)

    def body(*refs):
        src, land = refs[:n_s], refs[n_s:n_s + n_l]
        send, recv = refs[n_s + n_l:n_s + n_l + 2]
        for cp in flight.copies(src, land, send, recv):
            cp.wait_send()
            cp.wait_recv()

    mem = lambda t: pltpu.HBM(t.shape, t.dtype)
    outs = pl.pallas_call(
        body, name=name, out_shape=(*map(mem, flight.srcs), *map(mem, flight.lands)),
        in_specs=[HBM] * (n_s + n_l) + [SEM, SEM, ANY], out_specs=tuple([HBM] * (n_s + n_l)),
        input_output_aliases={i: i for i in range(n_s + n_l)},
        compiler_params=pltpu.CompilerParams(has_side_effects=DATAFLOW),
    )(*flight.srcs, *flight.lands, flight.send, flight.recv, after)
    return list(outs[:n_s]), list(outs[n_s:])


def _empty_like(shapes_from, lead):
    return [lax.empty((lead,) + t.shape[1:], t.dtype) for t in shapes_from]


def _scatter_off(name, chunks, after):
    def copies(src, land, send, recv):
        x, y, c = _place()
        return [pltpu.make_async_remote_copy(
            src_ref=src[a].at[2 * px + py], dst_ref=land[a].at[k], send_sem=send.at[3 * a + k],
            recv_sem=recv.at[3 * a + k], device_id=(px, py, c), device_id_type=MESH_ID)
            for a in range(len(chunks)) for k, (px, py) in enumerate(_other_chips(x, y))]

    return _take_off(name, chunks, _empty_like(chunks, 3), copies, 3 * len(chunks), after)


def _swap_off(name, arrs, after):
    def copies(src, land, send, recv):
        x, y, c = _place()
        return [pltpu.make_async_remote_copy(
            src_ref=src[a], dst_ref=land[a], send_sem=send.at[a], recv_sem=recv.at[a],
            device_id=(x, y, 1 - c), device_id_type=MESH_ID) for a in range(len(arrs))]

    return _take_off(name, arrs, [lax.empty(t.shape, t.dtype) for t in arrs], copies, len(arrs), after)


def _devices_off(name, block, after):
    me = 4 * lax.axis_index("x") + 2 * lax.axis_index("y") + lax.axis_index("c")
    land = lax.dynamic_update_index_in_dim(lax.empty((N_DEV,) + block.shape, block.dtype), block, me, 0)

    def copies(src, land, send, recv):
        x, y, c = _place()
        mine = 4 * x + 2 * y + c
        return [pltpu.make_async_remote_copy(
            src_ref=src[0], dst_ref=land[0].at[mine], send_sem=send.at[k - 1], recv_sem=recv.at[k - 1],
            device_id=(x ^ (k >> 2), y ^ ((k >> 1) & 1), c ^ (k & 1)), device_id_type=MESH_ID)
            for k in range(1, N_DEV)]

    return _take_off(name, [block], [land], copies, N_DEV - 1, after)


def _half_rows(shape, c, other=False):
    half = shape[0] // 2
    return pl.ds(((1 - c) if other else c) * half, half)


def _gather_start(name, shards, lands, after):
    n = len(shards)

    def body(*refs):
        src, land, (send, recv) = refs[:n], refs[n:2 * n], refs[2 * n + 1:2 * n + 3]
        x, y, c = _place()
        me = 2 * x + y
        for a in range(n):
            mine = _half_rows(shards[a].shape, c)
            for j, (px, py) in enumerate(_other_chips(x, y)):
                pltpu.make_async_remote_copy(
                    src_ref=src[a].at[mine], dst_ref=land[a].at[me, mine], send_sem=send.at[3 * a + j],
                    recv_sem=recv.at[3 * a + j], device_id=(px, py, c), device_id_type=MESH_ID).start()
        token = refs[-1]
        token[...] = jnp.zeros_like(token)

    mem = lambda t: pltpu.HBM(t.shape, t.dtype)
    pair = pltpu.SemaphoreType.DMA((3 * n,))
    outs = pl.pallas_call(
        body, name=name,
        out_shape=(pair, pair, *map(mem, shards), *map(mem, lands), SDS((SUBLANES, LANES), F32)),
        in_specs=[HBM] * (2 * n) + [ANY],
        out_specs=(SEM, SEM, *[HBM] * (2 * n), pl.BlockSpec(memory_space=pltpu.VMEM)),
        input_output_aliases={i: 2 + i for i in range(2 * n)},
        compiler_params=pltpu.CompilerParams(has_side_effects=DATAFLOW),
    )(*[pltpu.with_memory_space_constraint(t, pltpu.HBM) for t in (*shards, *lands)], after)
    return outs[0], outs[1], list(outs[2:2 + n]), list(outs[2 + n:2 + 2 * n]), outs[-1]


def _gather_pass(name, send, recv, shards, lands, after, first=0):
    n = len(shards)

    def body(*refs):
        src, land, (send, recv, _) = refs[:n], refs[n:2 * n], refs[2 * n:2 * n + 3]
        fsend, frecv = refs[2 * n + 3], refs[2 * n + 4]
        x, y, c = _place()
        me = 2 * x + y
        for a in range(n):
            mine = _half_rows(shards[a].shape, c)
            for j, (px, py) in enumerate(_other_chips(x, y)):
                far = 2 * px + py
                ici = pltpu.make_async_remote_copy(
                    src_ref=src[a].at[mine], dst_ref=land[a].at[far, mine], send_sem=send.at[3 * (first + a) + j],
                    recv_sem=recv.at[3 * (first + a) + j], device_id=(px, py, c), device_id_type=MESH_ID)
                ici.wait_recv()
                ici.wait_send()
                pltpu.make_async_remote_copy(
                    src_ref=land[a].at[far, mine], dst_ref=land[a].at[far, mine], send_sem=fsend.at[3 * a + j],
                    recv_sem=frecv.at[3 * a + j], device_id=(x, y, 1 - c), device_id_type=MESH_ID).start()
        token = refs[-1]
        token[...] = jnp.zeros_like(token)

    mem = lambda t: pltpu.HBM(t.shape, t.dtype)
    pair = pltpu.SemaphoreType.DMA((3 * n,))
    outs = pl.pallas_call(
        body, name=name,
        out_shape=(pair, pair, *map(mem, lands), SDS((SUBLANES, LANES), F32)),
        in_specs=[HBM] * (2 * n) + [SEM, SEM, ANY],
        out_specs=(SEM, SEM, *[HBM] * n, pl.BlockSpec(memory_space=pltpu.VMEM)),
        input_output_aliases={n + i: 2 + i for i in range(n)},
        compiler_params=pltpu.CompilerParams(has_side_effects=DATAFLOW),
    )(*shards, *lands, send, recv, after)
    return outs[0], outs[1], list(outs[2:2 + n]), outs[-1]


def _gather_wait(name, fsend, frecv, lands, after):
    n = len(lands)

    def body(*refs):
        land, (fsend, frecv, _) = refs[:n], refs[n:n + 3]
        x, y, c = _place()
        for a in range(n):
            for j, (px, py) in enumerate(_other_chips(x, y)):
                far = 2 * px + py
                mine = _half_rows(lands[a].shape[1:], c)
                theirs = _half_rows(lands[a].shape[1:], c, other=True)
                pltpu.make_async_remote_copy(
                    src_ref=land[a].at[far, mine], dst_ref=land[a].at[far, mine], send_sem=fsend.at[3 * a + j],
                    recv_sem=frecv.at[3 * a + j], device_id=(x, y, 1 - c), device_id_type=MESH_ID).wait_send()
                pltpu.make_async_remote_copy(
                    src_ref=land[a].at[far, theirs], dst_ref=land[a].at[far, theirs], send_sem=fsend.at[3 * a + j],
                    recv_sem=frecv.at[3 * a + j], device_id=(x, y, 1 - c), device_id_type=MESH_ID).wait_recv()

    mem = lambda t: pltpu.HBM(t.shape, t.dtype)
    return list(pl.pallas_call(
        body, name=name, out_shape=tuple(map(mem, lands)), in_specs=[HBM] * n + [SEM, SEM, ANY],
        out_specs=tuple([HBM] * n), input_output_aliases={i: i for i in range(n)},
        compiler_params=pltpu.CompilerParams(has_side_effects=DATAFLOW),
    )(*lands, fsend, frecv, after))


def _after(token):
    return _Exchange([token], [], [], lambda *_: None, lambda *_: None)


def _sum_devices(slots):
    def body(s_ref, o_ref):
        acc = s_ref[0]
        for d in range(1, N_DEV):
            acc = acc + s_ref[d]
        o_ref[...] = acc

    return pl.pallas_call(
        body, in_specs=[pl.BlockSpec(memory_space=pltpu.VMEM)], out_specs=pl.BlockSpec(memory_space=pltpu.VMEM),
        out_shape=SDS(slots.shape[1:], F32), name="sum_small",
        compiler_params=pltpu.CompilerParams(vmem_limit_bytes=32 * 1024 * 1024))(slots)


def _adam_small(ws, gs, ms, vs):
    n = len(ws)

    def body(*refs):
        for i in range(n):
            w_ref, g_ref, m_ref, v_ref = (refs[k * n + i] for k in range(4))
            outs = _adam_math(w_ref[...], g_ref[...], m_ref[...], v_ref[...])
            for k in range(3):
                refs[(4 + k) * n + i][...] = outs[k]

    vmem = pl.BlockSpec(memory_space=pltpu.VMEM)
    return pl.pallas_call(
        body, in_specs=[vmem] * (4 * n), out_specs=[vmem] * (3 * n),
        out_shape=[SDS(w.shape, F32) for w in ws] * 3, name="adam_small",
        compiler_params=pltpu.CompilerParams(vmem_limit_bytes=32 * 1024 * 1024))(*ws, *gs, *ms, *vs)


def _local_step(x, target, small, big, tb, distributed):
    dist = distributed
    me = (2 * lax.axis_index("x") + lax.axis_index("y")) if dist else 0
    tb_ssm = min(tb, 256)
    bucket = jnp.asarray(_bucket_table())
    place_own = lambda t: lax.dynamic_update_index_in_dim(lax.empty((N_CHIPS,) + t.shape, t.dtype), t, me, 0)
    if dist:
        in_legs = _gather_start("gather_in_start", [big["w_in"]], [place_own(big["w_in"])], small["d_skip"])
        names = sorted(small)
        in_token, values = lax.optimization_barrier((in_legs[4], [small[n] for n in names]))
        small = dict(zip(names, values))
    g1, g2, g3, g4 = small["norm_mix_pre"], small["norm_mix_post"], small["norm_mlp_pre"], small["norm_mlp_post"]

    keys_first = lambda t: jnp.swapaxes(t, -1, -2)
    bias = _bias_table(small["rel_bias"], bucket)
    sink_rows = keys_first(_pair_layout(jnp.broadcast_to(small["sinks"].reshape(N_HEADS, 1, 1), (N_HEADS, BLOCK, 1))))
    disc_args = (small["lam_re"], small["lam_im"], small["log_dt"], small["b_re"], small["b_im"])
    (ab_re, ab_im, bb_re, bb_im), disc_vjp = jax.vjp(_ssm_discretize, *disc_args)
    tab_f, tab_b = _scan_tables(ab_re, ab_im)
    bmat = _bf(_b_matrix(bb_re, bb_im))
    cmat = _bf(_c_matrix(small["c_re"], small["c_im"]))
    bmat_t, cmat_t = bmat.transpose(0, 2, 1), cmat.transpose(0, 2, 1)
    d_skip = small["d_skip"]

    mix = ("w_glu", "w_attn_branch", "w_ssm_branch", "w_out")
    rest = [big[n] for n in mix + ("w_ff_in", "w_ff_out")]
    if dist:
        send, recv, src, lands, _ = in_legs
        tab_f, tab_b, bias, sink_rows, bmat, cmat, bmat_t, cmat_t, rest, rest_lands = lax.optimization_barrier(
            (tab_f, tab_b, bias, sink_rows, bmat, cmat, bmat_t, cmat_t, rest, [place_own(t) for t in rest]))
        corner = lambda t: t.reshape(-1, t.shape[-1])[:1, :LANES].astype(F32)
        prepared = sum(map(corner, [tab_b, bias, sink_rows, bmat, cmat] + rest_lands), in_token[:1])
        send, recv, lands, in_passed = _gather_pass("gather_in_pass", send, recv, src, lands, prepared)
        (g_in,) = _gather_wait("gather_in_wait", send, recv, lands, in_passed)
        w_in = g_in.reshape(IN_W, D_MODEL)
    else:
        w_in = big["w_in"]
    token = None
    n_mix = len(mix)
    if dist:
        send, recv, rest, lands, token = _gather_start("gather_rest_start", rest, rest_lands, in_passed)
    h1, q, k, v, u, ga, gs = _inproj_fwd(x, g1, w_in, tb, _after(token) if dist else None)
    s, h = _ssm_fwd(u, bmat, cmat, tab_f, d_skip, tb)
    if dist:
        fsend, frecv, mix_lands, token = _gather_pass("gather_mix_pass", send, recv, rest[:n_mix], lands[:n_mix], s)
    att = _attn_fwd(q, k, v, bias, sink_rows, _after(token) if dist else None)[0]
    if dist:
        w_mix = _gather_wait("gather_mix_wait", fsend, frecv, mix_lands, att)
        fsend, frecv, ff_lands, token = _gather_pass(
            "gather_ff_pass", send, recv, rest[n_mix:], lands[n_mix:], w_mix[0], n_mix)
        rest = w_mix + ff_lands
    w_glu, w_ab, w_sb, w_out = rest[:n_mix]
    w_glu = w_glu.reshape(SSM_W, SSM_W)
    w_out = w_out.reshape(D_MODEL, D_MODEL)
    x2 = _merge_fwd(x, s, att, ga, gs, g2, w_glu, w_ab, w_sb, w_out, tb, _after(token) if dist else None)
    if dist:
        rest[n_mix:] = _gather_wait("gather_ff_wait", fsend, frecv, ff_lands, x2)
    w_ffi, w_ffo = [rest[n_mix]], rest[n_mix + 1]
    dy, df, h3, ra, loss_acc, dg4 = _mlp_fwd_loss(x2, target, g3, g4, w_ffi, w_ffo, tb)

    dx2, da, dg3 = _mlp_bwd(x2, dy, df, ra, g3, w_ffi, w_ffo, tb)
    tl = min(2048, x.shape[0])
    chunked = (N_CHIPS, D_FF // N_CHIPS, D_MODEL)
    d_ffi, b_ffi = _matmul_tn("grad_w_ff_in", h3, da, D_MODEL, D_FF // FF_CHUNKS, tl, True)
    d_ffo, b_ffo = _matmul_tn("grad_w_ff_out", ra, df, D_FF // FF_CHUNKS, D_MODEL, tl, False, square_a=True)
    d_ffo, b_ffo = d_ffo.reshape(chunked), b_ffo.reshape(chunked)
    behind = lambda flight: _after(flight.token) if dist else None
    ff_fl = _scatter_off("scatter_ff_off", [b_ffi, b_ffo], d_ffo) if dist else None
    outs = _merge_bwd(dx2, s, att, ga, gs, g2, w_glu, w_ab, w_sb, w_out, tb_ssm, behind(ff_fl))
    ds, datt, dga, dgs, dg2, d_glu, d_ab, d_sb, d_out, b_glu, b_ab, b_sb, b_out = outs
    glu4, out4 = (N_CHIPS, SSM_W // N_CHIPS, SSM_W), (N_CHIPS, D_MODEL // N_CHIPS, D_MODEL)
    d_mix = [d_glu.reshape(glu4), d_ab, d_sb, d_out.reshape(out4)]
    b_mix = [b_glu.reshape(glu4), b_ab, b_sb, b_out.reshape(out4)]
    mix_fl = _scatter_off("scatter_mix_off", b_mix, d_mix[-1]) if dist else None
    du, d_bmat, d_cmat, da_acc, dd_skip = _ssm_bwd(
        ds, u, h, bmat_t, cmat_t, tab_b, d_skip, tb, behind(mix_fl))
    dq, dk, dv, dbias, dsink_rows = _attn_bwd(q, k, v, datt, bias, sink_rows)
    dx, dpj, dg1 = _inproj_bwd(x, dx2, dq, dk, dv, du, dga, dgs, g1, w_in, tb)

    dab_re, dab_im = _state_unlayout(jnp.sum(da_acc, axis=0))
    dbb_re, dbb_im = _b_matrix_grad(d_bmat)
    d_lam_re, d_lam_im, d_log_dt, d_b_re, d_b_im = disc_vjp((dab_re, dab_im, dbb_re, dbb_im))
    d_c_re, d_c_im = _c_matrix_grad(d_cmat)
    d_rel = _bias_grad(dbias, bucket)
    d_sinks = jnp.sum(_pair_unlayout(keys_first(dsink_rows)), axis=(1, 2))
    small_grads = dict(
        norm_mix_pre=dg1, norm_mix_post=dg2, norm_mlp_pre=dg3, norm_mlp_post=dg4, rel_bias=d_rel, sinks=d_sinks,
        lam_re=d_lam_re, lam_im=d_lam_im, log_dt=d_log_dt, b_re=d_b_re, b_im=d_b_im, c_re=d_c_re, c_im=d_c_im,
        d_skip=dd_skip)
    small_fl = _devices_off("small_off", _pack(small_grads, loss_acc), dg1) if dist else None
    outs = _matmul_tn("grad_w_in", dpj, h1, IN_W // 2, D_MODEL, tl, False, behind(small_fl))
    in4 = (N_CHIPS, IN_W // N_CHIPS, D_MODEL)
    d_in, b_in = outs[0].reshape(in4), outs[1].reshape(in4)
    if not dist:
        return loss_acc, dx, small_grads, dict(zip(BIG, [d_in] + d_mix + [d_ffi, d_ffo]))
    in_fl = _scatter_off("scatter_w_in_off", [b_in], d_in)
    r_ffi, r_ffo = _land("scatter_ff_land", ff_fl, in_fl.token)[1]
    p_ffi = _sum4("sum_w_ff_in", d_ffi, r_ffi, me)
    p_ffo = _sum4("sum_w_ff_out", d_ffo, r_ffo, me)
    swap_fl = _swap_off("swap_ff_off", [p_ffi, p_ffo], r_ffo)
    r_mix = _land("scatter_mix_land", mix_fl, swap_fl.token)[1]
    p_mix = [_sum4("sum_" + n, d, r, me) for n, d, r in zip(mix, d_mix, r_mix)]
    mix_swap = _swap_off("swap_mix_off", p_mix, swap_fl.token)
    (p_ffi, p_ffo), (s_ffi, s_ffo) = _land("swap_ff_land", swap_fl, mix_swap.token)
    pending = dict(d_in=d_in, in_fl=in_fl, mix_swap=mix_swap, w_ff_in=(p_ffi, s_ffi), w_ff_out=(p_ffo, s_ffo), me=me)
    return loss_acc, dx, small_fl, pending


SMALL = ['norm_mix_pre', 'norm_mix_post', 'norm_mlp_pre', 'norm_mlp_post', 'rel_bias', 'sinks', 'lam_re', 'lam_im',
         'log_dt', 'b_re', 'b_im', 'c_re', 'c_im', 'd_skip']
BIG = ['w_in', 'w_glu', 'w_attn_branch', 'w_ssm_branch', 'w_out', 'w_ff_in', 'w_ff_out']
WEIGHTS = ['norm_mix_pre', 'norm_mix_post', 'norm_mlp_pre', 'norm_mlp_post', 'w_in', 'rel_bias', 'sinks', 'lam_re',
           'lam_im', 'log_dt', 'b_re', 'b_im', 'c_re', 'c_im', 'd_skip', 'w_glu', 'w_attn_branch', 'w_ssm_branch',
           'w_out', 'w_ff_in', 'w_ff_out']
PACK_COLS = 1024
PACK_ORDER = ['b_re', 'b_im', 'c_re', 'c_im', 'lam_re', 'lam_im', 'norm_mix_pre', 'norm_mix_post', 'norm_mlp_pre',
              'norm_mlp_post', 'rel_bias', 'sinks', 'log_dt', 'd_skip']


STATE_MINOR = ('b_re', 'b_im')
PACK_ROWS = 144
LOSS_ROW = 140


def _pack(named, loss_acc):
    parts = []
    for n in PACK_ORDER:
        a = jnp.swapaxes(named[n], -1, -2) if n in STATE_MINOR else named[n]
        flat = a.reshape(-1)
        rows = -(-flat.shape[0] // PACK_COLS)
        parts.append(jnp.pad(flat, (0, rows * PACK_COLS - flat.shape[0])).reshape(rows, PACK_COLS))
    assert sum(p.shape[0] for p in parts) == LOSS_ROW
    parts.append(jnp.pad(loss_acc[0:1], ((0, PACK_ROWS - LOSS_ROW - 1), (0, PACK_COLS - loss_acc.shape[1]))))
    return jnp.concatenate(parts, axis=0)


def _unpack(packed, shapes):
    out, at = {}, 0
    for n in PACK_ORDER:
        shape = shapes[n][:-2] + (shapes[n][-1], shapes[n][-2]) if n in STATE_MINOR else shapes[n]
        size = int(np.prod(shape))
        rows = -(-size // PACK_COLS)
        blk = packed[at:at + rows]
        out[n] = (blk.reshape(-1)[:size] if size % PACK_COLS else blk).reshape(shape)
        at += rows
    return out


def kernel(x, norm_mix_pre, norm_mix_post, norm_mlp_pre, norm_mlp_post, w_in, rel_bias, sinks, lam_re, lam_im, log_dt, b_re, b_im, c_re, c_im, d_skip, w_glu, w_attn_branch, w_ssm_branch, w_out, w_ff_in, w_ff_out, loss_target, m_norm_mix_pre, m_norm_mix_post, m_norm_mlp_pre, m_norm_mlp_post, m_w_in, m_rel_bias, m_sinks, m_lam_re, m_lam_im, m_log_dt, m_b_re, m_b_im, m_c_re, m_c_im, m_d_skip, m_w_glu, m_w_attn_branch, m_w_ssm_branch, m_w_out, m_w_ff_in, m_w_ff_out, v_norm_mix_pre, v_norm_mix_post, v_norm_mlp_pre, v_norm_mlp_post, v_w_in, v_rel_bias, v_sinks, v_lam_re, v_lam_im, v_log_dt, v_b_re, v_b_im, v_c_re, v_c_im, v_d_skip, v_w_glu, v_w_attn_branch, v_w_ssm_branch, v_w_out, v_w_ff_in, v_w_ff_out):
    env = dict(locals())
    w = {n: env[n] for n in WEIGHTS}
    m = {n: env["m_" + n] for n in WEIGHTS}
    v = {n: env["v_" + n] for n in WEIGHTS}
    seq = x.shape[1]
    tb = min(512, seq)

    small = {n: w[n] for n in ('norm_mix_pre', 'norm_mix_post', 'norm_mlp_pre', 'norm_mlp_post', 'rel_bias')}
    small.update({n: w[n][0] for n in ('sinks', 'lam_re', 'lam_im', 'log_dt', 'b_re', 'b_im', 'c_re', 'c_im')})
    small['d_skip'] = w['d_skip']
    shard = lambda t, n: t[n][0].T if n == 'w_in' else t[n][0]
    unshard = lambda a, n: (a.T if n == 'w_in' else a)[None]
    _, dx, small_fl, pending = _local_step(
        x[0], loss_target[0], small, {n: _bf(shard(w, n)) for n in BIG}, tb, True)

    grads, deltas, new_m, new_v = {}, {}, {}, {}

    def adam(n, partials, after=None):
        outs = _adam_pair("adam_" + n, (shard(w, n), *partials, shard(m, n), shard(v, n)), after)
        grads[n], deltas[n], new_m[n], new_v[n] = [unshard(a, n) for a in outs]
        return outs[3]

    mix = ("w_glu", "w_attn_branch", "w_ssm_branch", "w_out")
    in_fl = pending["in_fl"]
    last = pending["mix_swap"].token
    for n in ("w_ff_in", "w_ff_out"):
        last = adam(n, pending[n], last)
    for n, partials in zip(mix, zip(*_land("swap_mix_land", pending["mix_swap"], last))):
        last = adam(n, partials, last)

    small_g = _sum_devices(_land("small_land", small_fl, last)[1][0])
    loss = small_g[LOSS_ROW, 0]
    minor = lambda t, n: jnp.swapaxes(t, -1, -2) if n in STATE_MINOR else t
    g_small = _unpack(small_g, {n: w[n].shape for n in SMALL})
    outs = _adam_small([minor(w[n], n) for n in SMALL], [g_small[n] for n in SMALL],
                       [minor(m[n], n) for n in SMALL], [minor(v[n], n) for n in SMALL])
    grads.update({n: minor(g_small[n], n) for n in SMALL})
    for k, dst in enumerate((deltas, new_m, new_v)):
        dst.update({n: minor(a, n) for n, a in zip(SMALL, outs[k * len(SMALL):(k + 1) * len(SMALL)])})

    (r_in,) = _land("scatter_w_in_land", in_fl, outs[0])[1]
    adam("w_in", _sum4_swap("sum_swap_w_in", pending["d_in"], r_in, pending["me"]))

    return (loss, dx[None], *[grads[n] for n in WEIGHTS], *[deltas[n] for n in WEIGHTS],
            *[new_m[n] for n in WEIGHTS], *[new_v[n] for n in WEIGHTS])
```

```python
import functools
import math

import numpy as np
import jax
import jax.numpy as jnp
from jax import lax
from jax.experimental import pallas as pl
from jax.experimental.pallas import tpu as pltpu

F32 = jnp.float32
BF16 = jnp.bfloat16

D_MODEL = 1024
N_HEADS = 8
N_KV = 2
Q_GROUP = 4
HEAD_DIM = 64
ATTN_W = 512
KV_W = 128
BLOCK = 128
N_BUCKETS = 32
MAX_DISTANCE = 128
NEG_INF = -1e30
SSM_W = 512
SSM_GROUP = 16
SSM_GROUPS = 32
SSM_STATE = 64
N_SUPER = 4
GROUPS_PER_SUPER = SSM_GROUPS // N_SUPER
SUPER_IN = GROUPS_PER_SUPER * SSM_GROUP
SUPER_HALF = GROUPS_PER_SUPER * SSM_STATE
SUPER_W = 2 * SUPER_HALF
STATE_COLS = N_SUPER * SUPER_W
D_FF = 4096
FF_CHUNKS = 4
IN_W = 3328
SPLITS = (0, 512, 640, 768, 1280, 2304, 3328)
RMS_EPS = 1e-6
N_CHIPS = 4
N_DEV = 8
SUBLANES = 8
LANES = 128
STATE_TILES = STATE_COLS // LANES
SUPER_TILES = SUPER_W // LANES

ADAM_LR = 0.001
ADAM_B1 = 0.9
ADAM_B2 = 0.999
ADAM_EPS = 1e-08
ADAM_WD = 0.01
ADAM_STEP = 10

VMEM_BIG = 56 * 1024 * 1024
SDS = jax.ShapeDtypeStruct
MESH_ID = pl.DeviceIdType.MESH
ANY = pl.BlockSpec(memory_space=pl.ANY)


def _bf(x):
    return x.astype(BF16)


def _mm(a, b):
    return jnp.dot(a, b, preferred_element_type=F32)


def _mm_nt(a, b):
    return lax.dot_general(a, b, (((1,), (1,)), ((), ())), preferred_element_type=F32)


def _mm_tn(a, b):
    return lax.dot_general(a, b, (((0,), (0,)), ((), ())), preferred_element_type=F32)


def _sig(x):
    return 1.0 / (1.0 + jnp.exp(-x))


def _rms(x, g):
    r = lax.rsqrt(jnp.mean(x * x, axis=-1, keepdims=True) + RMS_EPS)
    xh = x * r
    return xh * g, xh, r


def _rms_bwd(dout, xh, r, g):
    dg = jnp.sum(dout * xh, axis=0, keepdims=True)
    dxh = dout * g
    dx = r * (dxh - xh * jnp.mean(dxh * xh, axis=-1, keepdims=True))
    return dx, dg


_GELU_C = math.sqrt(2.0 / math.pi)


def _gelu_and_grad(x):
    x2 = x * x
    inner = _GELU_C * (x + 0.044715 * (x2 * x))
    t = jnp.tanh(inner)
    y = 0.5 * x * (1.0 + t)
    dy = 0.5 * (1.0 + t) + 0.5 * x * (1.0 - t * t) * (_GELU_C * (1.0 + 3.0 * 0.044715 * x2))
    return y, dy


def _zero_map(nd, *_):
    return (0,) * nd


def _params(n_axes, vmem=None):
    return pltpu.CompilerParams(dimension_semantics=("arbitrary",) * n_axes, vmem_limit_bytes=vmem)


class _Exchange:
    def __init__(self, ins, outs, sems, start, wait):
        self.ins, self.outs, self.sems, self.start, self.wait = list(ins), list(outs), list(sems), start, wait


def _fused_call(name, body, grid, in_specs, out_specs, out_shape, scratch, args, exchange, params):
    n_in, n_out, n_scr = len(in_specs), len(out_specs), len(scratch)
    if exchange is None:
        fn = body
    else:
        ex = exchange
        n_xi, n_xo = len(ex.ins), len(ex.outs)

        def fn(*refs):
            at = 0
            parts = []
            for n in (n_in, n_xi, n_out, n_xo, n_scr, len(ex.sems)):
                parts.append(refs[at:at + n])
                at += n
            ins, x_in, outs, x_out, scr, x_sem = parts
            ids = [pl.program_id(a) for a in range(len(grid))]
            first = functools.reduce(jnp.logical_and, [i == 0 for i in ids])
            last = functools.reduce(jnp.logical_and, [i == g - 1 for i, g in zip(ids, grid)])

            @pl.when(first)
            def _():
                ex.start(x_in, x_out, x_sem)

            body(*ins, *outs, *scr)

            @pl.when(last)
            def _():
                ex.wait(x_in, x_out, x_sem)

        in_specs = list(in_specs) + [ANY] * n_xi
        out_specs = list(out_specs) + [ANY] * n_xo
        out_shape = list(out_shape) + ex.outs
        scratch = list(scratch) + ex.sems
        args = list(args) + ex.ins
    return pl.pallas_call(fn, grid=grid, in_specs=in_specs, out_specs=out_specs, out_shape=out_shape,
                          scratch_shapes=list(scratch), name=name, compiler_params=params)(*args)


def _rowcall(name, body, seq, tb, rows, consts, row_outs, acc_outs, scratch=(), reverse=False, vmem=None,
             exchange=None):
    nb = seq // tb
    rmap = (lambda i: (nb - 1 - i, 0)) if reverse else (lambda i: (i, 0))
    tmap = lambda i: (0,) + rmap(i)

    def row_spec(width):
        if isinstance(width, tuple):
            return pl.BlockSpec((width[0], tb, width[1]), tmap)
        return pl.BlockSpec((tb, width), rmap)

    def row_shape(width):
        return (width[0], seq, width[1]) if isinstance(width, tuple) else (seq, width)

    in_specs = [row_spec(a.shape[1] if a.ndim == 2 else (a.shape[0], a.shape[2])) for a in rows]
    in_specs += [pl.BlockSpec(a.shape, functools.partial(_zero_map, a.ndim), pipeline_mode=pl.Buffered(1))
                 for a in consts]
    out_specs = [row_spec(c) for c, _ in row_outs] + [ANY] * len(acc_outs)
    out_shape = [SDS(row_shape(c), dt) for c, dt in row_outs] + [SDS(s, dt) for s, dt in acc_outs]
    n_main = len(rows) + len(consts) + len(row_outs)
    n_acc = len(acc_outs)

    def fn(*refs):
        main, acc_hbm, rest = refs[:n_main], refs[n_main:n_main + n_acc], refs[n_main + n_acc:]
        acc_vmem, own = rest[:n_acc], rest[n_acc:]
        body(*main, *acc_vmem, *own)

        @pl.when(pl.program_id(0) == nb - 1)
        def _():
            for src, dst in zip(acc_vmem, acc_hbm):
                pltpu.sync_copy(src, dst)

    buffers = [pltpu.VMEM(s, dt) for s, dt in acc_outs] + list(scratch)
    return _fused_call(name, fn if acc_outs else body, (nb,), in_specs, out_specs, out_shape, buffers,
                       [*rows, *consts], exchange, _params(1, vmem))


def _inproj_fwd(x, g1, w_in, tb, exchange=None):
    seq = x.shape[0]

    def body(x_ref, g_ref, w_ref, h_ref, q_ref, k_ref, v_ref, u_ref, ga_ref, gs_ref):
        h, _, _ = _rms(x_ref[...], g_ref[...])
        hb = _bf(h)
        h_ref[...] = hb
        pj = _mm_nt(hb, w_ref[...])
        q_ref[...] = _bf(pj[:, SPLITS[0]:SPLITS[1]])
        k_ref[...] = _bf(pj[:, SPLITS[1]:SPLITS[2]])
        v_ref[...] = _bf(pj[:, SPLITS[2]:SPLITS[3]])
        u_ref[...] = pj[:, SPLITS[3]:SPLITS[4]]
        ga_ref[...] = pj[:, SPLITS[4]:SPLITS[5]]
        gs_ref[...] = pj[:, SPLITS[5]:SPLITS[6]]

    return _rowcall("inproj_fwd", body, seq, tb, [x], [g1, w_in],
                    [(D_MODEL, BF16), (ATTN_W, BF16), (KV_W, BF16), (KV_W, BF16), (SSM_W, F32),
                     (D_MODEL, F32), (D_MODEL, F32)], [], vmem=VMEM_BIG, exchange=exchange)


def _inproj_bwd(x, dx2, dq, dk, dv, du, dga, dgs, g1, w_in, tb, exchange=None):
    seq = x.shape[0]

    def body(x_ref, dx2_ref, dq_ref, dk_ref, dv_ref, du_ref, dga_ref, dgs_ref, g_ref, w_ref,
             dx_ref, dpj_ref, dg_ref):
        @pl.when(pl.program_id(0) == 0)
        def _():
            dg_ref[...] = jnp.zeros_like(dg_ref)

        dpj = jnp.concatenate([dq_ref[...], dk_ref[...], dv_ref[...], _bf(du_ref[...]),
                               dga_ref[...], dgs_ref[...]], axis=1)
        dpj_ref[...] = dpj
        dh = _mm(dpj, w_ref[...])
        g = g_ref[...]
        _, xh, r = _rms(x_ref[...], g)
        dxn, dg = _rms_bwd(dh, xh, r, g)
        dx_ref[...] = dx2_ref[...] + dxn
        dg_ref[...] += dg

    return _rowcall("inproj_bwd", body, seq, tb, [x, dx2, dq, dk, dv, du, dga, dgs], [g1, w_in],
                    [(D_MODEL, F32), (IN_W, BF16)], [((1, D_MODEL), F32)], vmem=VMEM_BIG, exchange=exchange)


def _bucket_table():
    qi = np.arange(BLOCK)[:, None]
    kj = np.arange(2 * BLOCK)[None, :]
    dist = qi + BLOCK - kj
    max_exact = N_BUCKETS // 2
    d = np.maximum(dist, 0)
    df = np.maximum(d, 1).astype(np.float32)
    large = max_exact + (np.log(df / np.float32(max_exact)) / np.float32(math.log(MAX_DISTANCE / max_exact))
                         * np.float32(N_BUCKETS - max_exact)).astype(np.int32)
    large = np.minimum(large, N_BUCKETS - 1)
    bucket = np.where(d < max_exact, d, large)
    valid = (dist >= 0) & (dist < BLOCK)
    return np.where(valid, bucket, -1).astype(np.int32)


def _bias_table(rel_bias, bucket):
    def body(rb_ref, bk_ref, o_ref):
        bk = bk_ref[...]
        has_prev = lax.broadcasted_iota(jnp.int32, bk.shape, 1) >= BLOCK
        for h in range(N_HEADS):
            kh, j, par = h // Q_GROUP, (h // 2) % 2, h % 2
            acc = jnp.full((BLOCK, 2 * BLOCK), NEG_INF, F32)
            for b in range(N_BUCKETS):
                acc = jnp.where(bk == b, rb_ref[b, h], acc)
            o_ref[0, kh, par, :, j * BLOCK:(j + 1) * BLOCK] = jnp.where(has_prev, acc, NEG_INF).T
            o_ref[1, kh, par, :, j * BLOCK:(j + 1) * BLOCK] = acc.T

    return pl.pallas_call(
        body, out_shape=SDS((2, N_KV, 2, 2 * BLOCK, 2 * BLOCK), F32),
        in_specs=[pl.BlockSpec(memory_space=pltpu.SMEM), pl.BlockSpec(memory_space=pltpu.VMEM)],
        out_specs=pl.BlockSpec(memory_space=pltpu.VMEM), name="bias_table",
    )(rel_bias, bucket)


def _bias_grad(dbias, bucket):
    def body(db_ref, bk_ref, o_ref):
        bk = bk_ref[...]
        for h in range(N_HEADS):
            kh, j, par = h // Q_GROUP, (h // 2) % 2, h % 2
            db = db_ref[kh, par, :, j * BLOCK:(j + 1) * BLOCK].T
            for b in range(N_BUCKETS):
                o_ref[b, h] = jnp.sum(jnp.where(bk == b, db, 0.0))

    return pl.pallas_call(
        body, out_shape=SDS((N_BUCKETS, N_HEADS), F32),
        in_specs=[pl.BlockSpec(memory_space=pltpu.VMEM), pl.BlockSpec(memory_space=pltpu.VMEM)],
        out_specs=pl.BlockSpec(memory_space=pltpu.SMEM), name="bias_grad",
    )(dbias, bucket)


TILE = 2 * HEAD_DIM


def _pair_layout(t):
    lead = t.shape[:-3]
    t = t.reshape(lead + (N_KV, 2, 2) + t.shape[-2:])
    nl = len(lead)
    t = jnp.transpose(t, tuple(range(nl)) + (nl, nl + 2, nl + 1, nl + 3, nl + 4))
    return t.reshape(lead + (N_KV, 2, 2 * BLOCK, t.shape[-1]))


def _pair_unlayout(t):
    t = t.reshape(N_KV, 2, 2, BLOCK, t.shape[-1]).transpose(0, 2, 1, 3, 4)
    return t.reshape(N_HEADS, BLOCK, t.shape[-1])


def _halves(t):
    tf = t.astype(F32)
    low = lax.broadcasted_iota(jnp.int32, tf.shape, 1) < HEAD_DIM
    swapped = pltpu.roll(tf, HEAD_DIM, 1)
    zero = jnp.zeros_like(tf)
    return ((_bf(jnp.where(low, tf, zero)), _bf(jnp.where(low, zero, swapped))),
            (_bf(jnp.where(low, swapped, zero)), _bf(jnp.where(low, zero, tf))))


def _fold_halves(even, odd):
    low = lax.broadcasted_iota(jnp.int32, even.shape, 1) < HEAD_DIM
    comb = jnp.where(low, even, odd)
    return comb + pltpu.roll(comb, HEAD_DIM, 1)


def _tile_rows(ref, kh):
    return jnp.concatenate([ref[:, (2 * kh) * TILE:(2 * kh + 1) * TILE],
                            ref[:, (2 * kh + 1) * TILE:(2 * kh + 2) * TILE]], axis=0)


def _halves_t(t):
    tt = t.astype(F32).T
    top = lax.broadcasted_iota(jnp.int32, tt.shape, 0) < HEAD_DIM
    swapped = jnp.concatenate([tt[HEAD_DIM:], tt[:HEAD_DIM]], axis=0)
    zero = jnp.zeros_like(tt)
    return ((_bf(jnp.where(top, tt, zero)), _bf(jnp.where(top, zero, swapped))),
            (_bf(jnp.where(top, swapped, zero)), _bf(jnp.where(top, zero, tt))))


def _attn_probs(km, qk, bias, sink):
    lg = _mm_nt(km, qk) * (HEAD_DIM ** -0.5) + bias
    m = jnp.maximum(jnp.max(lg, axis=0, keepdims=True), sink)
    p = jnp.exp(lg - m)
    es = jnp.exp(sink - m)
    inv = 1.0 / (jnp.sum(p, axis=0, keepdims=True) + es)
    return p * inv, es * inv


def _attn_fwd(q, k, v, bias, sink_rows, exchange=None):
    seq = q.shape[0]
    nblk = seq // BLOCK

    def body(q_ref, kp_ref, kc_ref, vp_ref, vc_ref, b_ref, s_ref, o_ref):
        which = jnp.minimum(pl.program_id(0), 1)
        kms = _halves(jnp.concatenate([kp_ref[...], kc_ref[...]], axis=0))
        vts = _halves_t(jnp.concatenate([vp_ref[...], vc_ref[...]], axis=0))
        for kh in range(N_KV):
            qk = _tile_rows(q_ref, kh)
            acc = jnp.zeros((TILE, 2 * BLOCK), F32)
            for par in range(2):
                pr, _ = _attn_probs(kms[kh][par], qk, b_ref[which, kh, par], s_ref[kh, par])
                acc = acc + _mm(vts[kh][par], _bf(pr))
            acc = acc.T
            o_ref[:, (2 * kh) * TILE:(2 * kh + 1) * TILE] = _bf(acc[:BLOCK])
            o_ref[:, (2 * kh + 1) * TILE:(2 * kh + 2) * TILE] = _bf(acc[BLOCK:])

    cur = lambda n: (n, 0)
    prev = lambda n: (jnp.maximum(n - 1, 0), 0)
    return _fused_call(
        "attn_fwd", body, (nblk,),
        [pl.BlockSpec((BLOCK, ATTN_W), cur),
         pl.BlockSpec((BLOCK, KV_W), prev), pl.BlockSpec((BLOCK, KV_W), cur),
         pl.BlockSpec((BLOCK, KV_W), prev), pl.BlockSpec((BLOCK, KV_W), cur),
         pl.BlockSpec(bias.shape, functools.partial(_zero_map, bias.ndim)),
         pl.BlockSpec(sink_rows.shape, functools.partial(_zero_map, sink_rows.ndim))],
        [pl.BlockSpec((BLOCK, ATTN_W), cur)], [SDS((seq, ATTN_W), BF16)], [],
        [q, k, k, v, v, bias, sink_rows], exchange, _params(1))


def _attn_bwd(q, k, v, d_out, bias, sink_rows, exchange=None):
    seq = q.shape[0]
    nblk = seq // BLOCK

    def body(q_ref, kp_ref, kc_ref, vp_ref, vc_ref, do_ref, b_ref, s_ref,
             dq_ref, dk_ref, dv_ref, db_ref, ds_ref, ck_ref, cv_ref):
        n = pl.program_id(0)

        @pl.when(n == 0)
        def _():
            db_ref[...] = jnp.zeros_like(db_ref)
            ds_ref[...] = jnp.zeros_like(ds_ref)
            ck_ref[...] = jnp.zeros_like(ck_ref)
            cv_ref[...] = jnp.zeros_like(cv_ref)

        @pl.when(n < nblk)
        def _():
            which = jnp.minimum(n, 1)
            scale = HEAD_DIM ** -0.5
            kcat = jnp.concatenate([kp_ref[...], kc_ref[...]], axis=0)
            kms = _halves(kcat)
            kts = _halves_t(kcat)
            vms = _halves(jnp.concatenate([vp_ref[...], vc_ref[...]], axis=0))
            dks, dvs = [], []
            for kh in range(N_KV):
                qk = _tile_rows(q_ref, kh)
                dok = _tile_rows(do_ref, kh)
                dq = jnp.zeros((TILE, 2 * BLOCK), F32)
                dkp, dvp = [], []
                for par in range(2):
                    pr, ps = _attn_probs(kms[kh][par], qk, b_ref[which, kh, par], s_ref[kh, par])
                    dp = _mm_nt(vms[kh][par], dok)
                    rs = jnp.sum(pr * dp, axis=0, keepdims=True)
                    dlg = pr * (dp - rs)
                    ds_ref[kh, par] += -ps * rs
                    db_ref[kh, par] += dlg
                    dlb = _bf(dlg)
                    dq = dq + _mm(kts[kh][par], dlb)
                    dkp.append(_mm(dlb, qk))
                    dvp.append(_mm(_bf(pr), dok))
                dq = _bf((dq * scale).T)
                dq_ref[:, (2 * kh) * TILE:(2 * kh + 1) * TILE] = dq[:BLOCK]
                dq_ref[:, (2 * kh + 1) * TILE:(2 * kh + 2) * TILE] = dq[BLOCK:]
                dks.append(_fold_halves(*dkp))
                dvs.append(_fold_halves(*dvp))
            low = lax.broadcasted_iota(jnp.int32, (2 * BLOCK, TILE), 1) < HEAD_DIM
            dkk = jnp.where(low, dks[0], dks[1]) * scale
            dvv = jnp.where(low, dvs[0], dvs[1])
            dk_ref[...] = _bf(ck_ref[...] + dkk[:BLOCK])
            ck_ref[...] = dkk[BLOCK:]
            dv_ref[...] = _bf(cv_ref[...] + dvv[:BLOCK])
            cv_ref[...] = dvv[BLOCK:]

        @pl.when(n == nblk)
        def _():
            dk_ref[...] = _bf(ck_ref[...])
            dv_ref[...] = _bf(cv_ref[...])

    cur = lambda n: (jnp.minimum(n, nblk - 1), 0)
    prev = lambda n: (jnp.maximum(jnp.minimum(n, nblk - 1) - 1, 0), 0)
    late = lambda n: (jnp.maximum(n - 1, 0), 0)
    kv_spec = lambda m: pl.BlockSpec((BLOCK, KV_W), m)
    acc_b = pl.BlockSpec(bias.shape[1:], functools.partial(_zero_map, bias.ndim - 1))
    acc_s = pl.BlockSpec(sink_rows.shape, functools.partial(_zero_map, sink_rows.ndim))
    return _fused_call(
        "attn_bwd", body, (nblk + 1,),
        [pl.BlockSpec((BLOCK, ATTN_W), cur), kv_spec(prev), kv_spec(cur), kv_spec(prev), kv_spec(cur),
         pl.BlockSpec((BLOCK, ATTN_W), cur),
         pl.BlockSpec(bias.shape, functools.partial(_zero_map, bias.ndim)), acc_s],
        [pl.BlockSpec((BLOCK, ATTN_W), cur), kv_spec(late), kv_spec(late), acc_b, acc_s],
        [SDS((seq, ATTN_W), BF16), SDS((seq, KV_W), BF16), SDS((seq, KV_W), BF16),
         SDS(bias.shape[1:], F32), SDS(sink_rows.shape, F32)],
        [pltpu.VMEM((BLOCK, KV_W), F32), pltpu.VMEM((BLOCK, KV_W), F32)],
        [q, k, k, v, v, d_out, bias, sink_rows], exchange, _params(1))


def _ssm_discretize(lam_re, lam_im, log_dt, b_re, b_im):
    dt = jnp.exp(log_dt)[:, None]
    mag = jnp.exp(lam_re * dt)
    ab_re = mag * jnp.cos(lam_im * dt)
    ab_im = mag * jnp.sin(lam_im * dt)
    nr = ab_re - 1.0
    den = lam_re * lam_re + lam_im * lam_im
    f_re = (nr * lam_re + ab_im * lam_im) / den
    f_im = (ab_im * lam_re - nr * lam_im) / den
    bb_re = f_re[..., None] * b_re - f_im[..., None] * b_im
    bb_im = f_re[..., None] * b_im + f_im[..., None] * b_re
    return ab_re, ab_im, bb_re, bb_im


def _state_layout(re, im):
    lead = re.shape[:-2]
    z = jnp.stack([re, im], axis=-3).reshape(lead + (2, N_SUPER, GROUPS_PER_SUPER, SSM_STATE))
    return jnp.moveaxis(z, -4, -3).reshape(lead + (STATE_COLS,))


def _state_unlayout(vec):
    z = vec.reshape(N_SUPER, 2, GROUPS_PER_SUPER, SSM_STATE).transpose(1, 0, 2, 3)
    z = z.reshape(2, SSM_GROUPS, SSM_STATE)
    return z[0], z[1]


SEG = 4
WINDOW = SEG * SUBLANES


def _scan_tables(ab_re, ab_im):
    pw = [None, (ab_re, ab_im)]
    for _ in range(2, WINDOW + 1):
        pr, pi_ = pw[-1]
        pw.append((pr * ab_re - pi_ * ab_im, pr * ab_im + pi_ * ab_re))
    fwd = np.zeros((7, SUBLANES), np.int64)
    bwd = np.zeros((7, SUBLANES), np.int64)
    for k, shift in enumerate((1, 2, 4)):
        fwd[k] = [SEG * shift if r >= shift else 0 for r in range(SUBLANES)]
        bwd[k] = [SEG * shift if r < SUBLANES - shift else 0 for r in range(SUBLANES)]
    fwd[3] = [SEG * (r + 1) for r in range(SUBLANES)]
    bwd[3] = [SEG * (SUBLANES - r) for r in range(SUBLANES)]
    for k in range(1, SEG):
        fwd[3 + k] = bwd[3 + k] = k
    used = sorted((set(fwd.ravel()) | set(bwd.ravel())) - {0})
    select = lambda which: np.stack([(which == p) for p in used], axis=-1).astype(np.float32)
    stacked = _state_layout(jnp.stack([pw[p][0] for p in used]), jnp.stack([pw[p][1] for p in used]))
    conj_sign = np.where((np.arange(STATE_COLS) // SUPER_HALF) % 2 == 1, -1.0, 1.0).astype(np.float32)
    pick = functools.partial(jnp.einsum, 'krp,pc->krc', precision=lax.Precision.HIGHEST)
    return pick(select(fwd), stacked), pick(select(bwd), stacked) * conj_sign


_EYE = np.eye(GROUPS_PER_SUPER, dtype=np.float32)


def _b_matrix(bb_re, bb_im):
    bb = jnp.stack([bb_re, bb_im]).reshape(2, N_SUPER, GROUPS_PER_SUPER, SSM_STATE, SSM_GROUP)
    m = jnp.einsum('rsgpc,gh->sgcrhp', bb, _EYE)
    return m.reshape(N_SUPER, SUPER_IN, SUPER_W)


def _b_matrix_grad(dm):
    d = dm.reshape(N_SUPER, GROUPS_PER_SUPER, SSM_GROUP, 2, GROUPS_PER_SUPER, SSM_STATE)
    d = jnp.sum(d * _EYE[None, :, None, None, :, None], axis=4)
    d = d.transpose(3, 0, 1, 4, 2).reshape(2, SSM_GROUPS, SSM_STATE, SSM_GROUP)
    return d[0], d[1]


def _c_matrix(c_re, c_im):
    cc = jnp.stack([c_re, -c_im]).reshape(2, N_SUPER, GROUPS_PER_SUPER, SSM_GROUP, SSM_STATE)
    m = jnp.einsum('rsgcp,gh->srgphc', cc, _EYE)
    return m.reshape(N_SUPER, SUPER_W, SUPER_IN)


def _c_matrix_grad(dm):
    d = dm.reshape(N_SUPER, 2, GROUPS_PER_SUPER, SSM_STATE, GROUPS_PER_SUPER, SSM_GROUP)
    d = jnp.sum(d * _EYE[None, None, :, None, :, None], axis=4)
    d = d.transpose(1, 0, 2, 4, 3).reshape(2, SSM_GROUPS, SSM_GROUP, SSM_STATE)
    return d[0], -d[1]


def _cmul_add(xr, xi, ar, ai, sr, si):
    return xr + ar * sr - ai * si, xi + ar * si + ai * sr


def _scan_rows(buf_ref, tab_ref, carry_ref, n_windows, reverse, h_ref=None, da_ref=None):
    order = list(range(SEG - 1, -1, -1)) if reverse else list(range(SEG))
    near = SUBLANES - 1 if reverse else 0
    far = 0 if reverse else SUBLANES - 1
    s_in = SUBLANES - 1 if reverse else 1
    lanes = lambda tile: pl.ds(tile * LANES, LANES)

    def window(w0, tile_re, tile_im, c_re, c_im, acc):
        rows = lambda t: pl.ds(w0 + t, SUBLANES, stride=SEG)
        get = lambda ref, t: (ref.at[tile_re][rows(t), :], ref.at[tile_im][rows(t), :])
        tab = lambda k: (tab_ref[k, :, lanes(tile_re)], tab_ref[k, :, lanes(tile_im)])

        def put(t, xr, xi):
            buf_ref.at[tile_re][rows(t), :] = xr
            buf_ref.at[tile_im][rows(t), :] = xi

        a1 = tab(4)
        er, ei = get(buf_ref, order[0])
        for t in order[1:]:
            er, ei = _cmul_add(*get(buf_ref, t), *a1, er, ei)
            if t != order[-1]:
                put(t, er, ei)
        for k, shift in enumerate((1, 2, 4)):
            s = (SUBLANES - shift) if reverse else shift
            er, ei = _cmul_add(er, ei, *tab(k), pltpu.roll(er, s, 0), pltpu.roll(ei, s, 0))
        er, ei = _cmul_add(er, ei, *tab(3), c_re, c_im)
        put(order[-1], er, ei)
        sub = lax.broadcasted_iota(jnp.int32, er.shape, 0)
        in_re = jnp.where(sub == near, c_re, pltpu.roll(er, s_in, 0))
        in_im = jnp.where(sub == near, c_im, pltpu.roll(ei, s_in, 0))
        true = {order[-1]: (er, ei)}
        for idx, t in enumerate(order[:-1]):
            true[t] = _cmul_add(*get(buf_ref, t), *tab(4 + idx), in_re, in_im)
            put(t, *true[t])
        carry = (jnp.broadcast_to(er[far:far + 1], er.shape), jnp.broadcast_to(ei[far:far + 1], ei.shape))
        if acc is None:
            return carry, None
        acc_re, acc_im = acc
        for t in range(SEG):
            if t + 1 < SEG:
                gr, gim = true[t + 1]
            else:
                gr = jnp.where(sub == SUBLANES - 1, c_re, pltpu.roll(true[0][0], SUBLANES - 1, 0))
                gim = jnp.where(sub == SUBLANES - 1, c_im, pltpu.roll(true[0][1], SUBLANES - 1, 0))
            hr, hi = get(h_ref, t)
            acc_re = acc_re + gr * hr + gim * hi
            acc_im = acc_im + gim * hr - gr * hi
        return carry, (acc_re, acc_im)

    half = SUPER_HALF // LANES
    per = 2 if h_ref is None else 4
    for sb in range(N_SUPER):
        pairs = [(2 * half * sb + j, 2 * half * sb + half + j) for j in range(half)]

        def step(wi, state, pairs=pairs):
            w = (n_windows - 1 - wi) if reverse else wi
            w0 = pl.multiple_of(w * WINDOW, WINDOW)
            out = []
            for j, (tile_re, tile_im) in enumerate(pairs):
                mine = state[per * j:per * (j + 1)]
                carry, acc = window(w0, tile_re, tile_im, mine[0], mine[1], mine[2:] or None)
                out += list(carry) + list(acc or ())
            return tuple(out)

        init = []
        for tile_re, tile_im in pairs:
            init += [carry_ref[:, lanes(tile_re)], carry_ref[:, lanes(tile_im)]]
            if h_ref is not None:
                init += [da_ref[:, lanes(tile_re)], da_ref[:, lanes(tile_im)]]
        fin = lax.fori_loop(0, n_windows, step, tuple(init))
        for j, (tile_re, tile_im) in enumerate(pairs):
            carry_ref[:, lanes(tile_re)] = fin[per * j]
            carry_ref[:, lanes(tile_im)] = fin[per * j + 1]
            if h_ref is not None:
                da_ref[:, lanes(tile_re)] = fin[per * j + 2]
                da_ref[:, lanes(tile_im)] = fin[per * j + 3]


def _put_tiles(ref, sb, block):
    for j in range(SUPER_TILES):
        ref[sb * SUPER_TILES + j] = block[:, j * LANES:(j + 1) * LANES]


def _get_tiles(ref, sb):
    return jnp.concatenate([ref[sb * SUPER_TILES + j] for j in range(SUPER_TILES)], axis=1)


def _ssm_fwd(u, bmat, cmat, tab, d_skip, tb, exchange=None):
    seq = u.shape[0]

    def body(u_ref, b_ref, c_ref, t_ref, d_ref, s_ref, h_ref, carry_ref):
        @pl.when(pl.program_id(0) == 0)
        def _():
            carry_ref[...] = jnp.zeros_like(carry_ref)

        u_blk = u_ref[...]
        ub = _bf(u_blk)
        for sb in range(N_SUPER):
            _put_tiles(h_ref, sb, _mm(ub[:, sb * SUPER_IN:(sb + 1) * SUPER_IN], b_ref[sb]))
        _scan_rows(h_ref, t_ref, carry_ref, tb // WINDOW, False)
        ys = [_mm(_bf(_get_tiles(h_ref, sb)), c_ref[sb]) for sb in range(N_SUPER)]
        s_ref[...] = jnp.concatenate(ys, axis=1) + d_ref[...] * u_blk

    return _rowcall("ssm_fwd", body, seq, tb, [u], [bmat, cmat, tab, d_skip],
                    [(SSM_W, F32), ((STATE_TILES, LANES), F32)], [],
                    scratch=[pltpu.VMEM((SUBLANES, STATE_COLS), F32)], vmem=VMEM_BIG, exchange=exchange)


def _ssm_bwd(ds, u, h, bmat_t, cmat_t, tab, d_skip, tb, exchange=None):
    seq = u.shape[0]

    def body(ds_ref, u_ref, h_ref, bt_ref, ct_ref, t_ref, d_ref,
             du_ref, db_ref, dc_ref, da_ref, dd_ref, g_ref, carry_ref):
        @pl.when(pl.program_id(0) == 0)
        def _():
            carry_ref[...] = jnp.zeros_like(carry_ref)
            db_ref[...] = jnp.zeros_like(db_ref)
            dc_ref[...] = jnp.zeros_like(dc_ref)
            da_ref[...] = jnp.zeros_like(da_ref)
            dd_ref[...] = jnp.zeros_like(dd_ref)

        ds_blk = ds_ref[...]
        dsb = _bf(ds_blk)
        u_blk = u_ref[...]
        ub = _bf(u_blk)
        for sb in range(N_SUPER):
            _put_tiles(g_ref, sb, _mm(dsb[:, sb * SUPER_IN:(sb + 1) * SUPER_IN], ct_ref[sb]))
        _scan_rows(g_ref, t_ref, carry_ref, tb // WINDOW, True, h_ref=h_ref, da_ref=da_ref)
        dus = []
        for sb in range(N_SUPER):
            gb = _bf(_get_tiles(g_ref, sb))
            dus.append(_mm(gb, bt_ref[sb]))
            db_ref[sb] += _mm_tn(ub[:, sb * SUPER_IN:(sb + 1) * SUPER_IN], gb)
            dc_ref[sb] += _mm_tn(_bf(_get_tiles(h_ref, sb)), dsb[:, sb * SUPER_IN:(sb + 1) * SUPER_IN])
        du_ref[...] = jnp.concatenate(dus, axis=1) + d_ref[...] * ds_blk
        dd_ref[...] += jnp.sum(ds_blk * u_blk, axis=0, keepdims=True)

    return _rowcall("ssm_bwd", body, seq, tb, [ds, u, h], [bmat_t, cmat_t, tab, d_skip],
                    [(SSM_W, F32)],
                    [((N_SUPER, SUPER_IN, SUPER_W), F32), ((N_SUPER, SUPER_W, SUPER_IN), F32),
                     ((SUBLANES, STATE_COLS), F32), ((1, SSM_W), F32)],
                    scratch=[pltpu.VMEM((STATE_TILES, tb, LANES), F32), pltpu.VMEM((SUBLANES, STATE_COLS), F32)],
                    reverse=True, vmem=VMEM_BIG, exchange=exchange)


def _merge_core(s, attb, ga, gs, wg_ref, wab_ref, wsb_ref, wout_ref):
    zg, dgelu = _gelu_and_grad(s)
    zgb = _bf(zg)
    sg = _sig(_mm(zgb, wg_ref[...]))
    z = zg * sg
    zb = _bf(z)
    ys = jnp.concatenate([_mm(zb, wsb_ref[j]) for j in range(N_CHIPS)], axis=1)
    ya = jnp.concatenate([_mm(attb, wab_ref[j]) for j in range(N_CHIPS)], axis=1)
    sa = _sig(ga)
    ss = _sig(gs)
    mgb = _bf(sa * ya + ss * ys)
    o = _mm(mgb, wout_ref[...])
    return dict(zg=zg, dgelu=dgelu, zgb=zgb, sg=sg, zb=zb, ys=ys, ya=ya, sa=sa, ss=ss, mgb=mgb, o=o)


def _merge_fwd(x, s, att, ga, gs, g2, w_glu, w_ab, w_sb, w_out, tb, exchange=None):
    seq = x.shape[0]

    def body(x_ref, s_ref, att_ref, ga_ref, gs_ref, g_ref, wg_ref, wab_ref, wsb_ref, wout_ref, x2_ref):
        f = _merge_core(s_ref[...], att_ref[...], ga_ref[...], gs_ref[...], wg_ref, wab_ref, wsb_ref, wout_ref)
        n, _, _ = _rms(f["o"], g_ref[...])
        x2_ref[...] = x_ref[...] + n

    return _rowcall("merge_fwd", body, seq, tb, [x, s, att, ga, gs], [g2, w_glu, w_ab, w_sb, w_out],
                    [(D_MODEL, F32)], [], vmem=VMEM_BIG, exchange=exchange)[0]


def _merge_bwd(dx2, s, att, ga, gs, g2, w_glu, w_ab, w_sb, w_out, tb, exchange=None):
    seq = s.shape[0]
    cw = D_MODEL // N_CHIPS
    last = seq // tb - 1

    def body(dx2_ref, s_ref, att_ref, ga_ref, gs_ref, g_ref, wg_ref, wab_ref, wsb_ref, wout_ref,
             ds_ref, datt_ref, dga_ref, dgs_ref, dg_ref, dwg_ref, dwab_ref, dwsb_ref, dwout_ref,
             bwg_ref, bwab_ref, bwsb_ref, bwout_ref):
        @pl.when(pl.program_id(0) == 0)
        def _():
            for r in (dg_ref, dwg_ref, dwab_ref, dwsb_ref, dwout_ref):
                r[...] = jnp.zeros_like(r)

        attb = att_ref[...]
        f = _merge_core(s_ref[...], attb, ga_ref[...], gs_ref[...], wg_ref, wab_ref, wsb_ref, wout_ref)
        g = g_ref[...]
        _, oh, r2 = _rms(f["o"], g)
        do, dg = _rms_bwd(dx2_ref[...], oh, r2, g)
        dg_ref[...] += dg
        dob = _bf(do)
        dwout_ref[...] += _mm_tn(f["mgb"], dob)
        dmg = _mm_nt(dob, wout_ref[...])
        sa, ss = f["sa"], f["ss"]
        dyab = _bf(dmg * sa)
        dysb = _bf(dmg * ss)
        dga_ref[...] = _bf(dmg * f["ya"] * sa * (1.0 - sa))
        dgs_ref[...] = _bf(dmg * f["ys"] * ss * (1.0 - ss))
        dwab = _mm_tn(attb, dyab)
        dwsb = _mm_tn(f["zb"], dysb)
        datt = jnp.zeros((tb, ATTN_W), F32)
        dz = jnp.zeros((tb, SSM_W), F32)
        for j in range(N_CHIPS):
            dwab_ref[j] += dwab[:, j * cw:(j + 1) * cw]
            dwsb_ref[j] += dwsb[:, j * cw:(j + 1) * cw]
            datt = datt + _mm_nt(dyab[:, j * cw:(j + 1) * cw], wab_ref[j])
            dz = dz + _mm_nt(dysb[:, j * cw:(j + 1) * cw], wsb_ref[j])
        datt_ref[...] = _bf(datt)
        sg, zg = f["sg"], f["zg"]
        dglb = _bf(dz * zg * sg * (1.0 - sg))
        dwg_ref[...] += _mm_tn(f["zgb"], dglb)
        dzg = dz * sg + _mm_nt(dglb, wg_ref[...])
        ds_ref[...] = dzg * f["dgelu"]

        @pl.when(pl.program_id(0) == last)
        def _():
            for dst, src in ((bwg_ref, dwg_ref), (bwab_ref, dwab_ref), (bwsb_ref, dwsb_ref), (bwout_ref, dwout_ref)):
                dst[...] = _bf(src[...])

    shapes = [w_glu.shape, w_ab.shape, w_sb.shape, w_out.shape]
    return _rowcall("merge_bwd", body, seq, tb, [dx2, s, att, ga, gs], [g2, w_glu, w_ab, w_sb, w_out],
                    [(SSM_W, F32), (ATTN_W, BF16), (D_MODEL, BF16), (D_MODEL, BF16)],
                    [((1, D_MODEL), F32)] + [(sh, F32) for sh in shapes] + [(sh, BF16) for sh in shapes],
                    vmem=VMEM_BIG, exchange=exchange)


def _mlp_fwd_loss(x2, target, g3, g4, w_ffi, w_ffo, tb):
    seq = x2.shape[0]
    n_slab = len(w_ffi)
    sw = D_FF // FF_CHUNKS // n_slab

    def body(x2_ref, t_ref, g3_ref, g4_ref, *rest):
        wi_refs, (wo_ref, dy_ref, df_ref, h_ref, ra_ref, loss_ref, dg_ref) = rest[:n_slab], rest[n_slab:]

        @pl.when(pl.program_id(0) == 0)
        def _():
            loss_ref[...] = jnp.zeros_like(loss_ref)
            dg_ref[...] = jnp.zeros_like(dg_ref)

        x2_blk = x2_ref[...]
        h3, _, _ = _rms(x2_blk, g3_ref[...])
        hb = _bf(h3)
        h_ref[...] = hb
        f = jnp.zeros((tb, D_MODEL), F32)
        for j in range(FF_CHUNKS):
            for k in range(n_slab):
                ra = jnp.maximum(_mm(hb, wi_refs[k][j]), 0.0)
                ra_ref[:, pl.ds((j * n_slab + k) * sw, sw)] = _bf(ra)
                f = f + _mm(_bf(ra * ra), wo_ref[j, pl.ds(k * sw, sw), :])
        g4 = g4_ref[...]
        n4, fh, r4 = _rms(f, g4)
        e = (x2_blk + n4) - t_ref[...]
        loss_ref[...] += 0.5 * jnp.sum(jnp.mean(e * e, axis=-1, keepdims=True))
        dy = e * (1.0 / D_MODEL)
        dy_ref[...] = dy
        df, dg = _rms_bwd(dy, fh, r4, g4)
        df_ref[...] = _bf(df)
        dg_ref[...] += dg

    return _rowcall("mlp_fwd_loss", body, seq, tb, [x2, target], [g3, g4, *w_ffi, w_ffo],
                    [(D_MODEL, F32), (D_MODEL, BF16), (D_MODEL, BF16), (D_FF, BF16)],
                    [((SUBLANES, 128), F32), ((1, D_MODEL), F32)], vmem=VMEM_BIG)


def _mlp_bwd(x2, dy, df, ra, g3, w_ffi, w_ffo, tb):
    seq = x2.shape[0]
    n_slab = len(w_ffi)
    sw = D_FF // FF_CHUNKS // n_slab

    def body(x2_ref, dy_ref, df_ref, ra_ref, g3_ref, *rest):
        wi_refs, (wo_ref, dx_ref, da_ref, dg_ref) = rest[:n_slab], rest[n_slab:]

        @pl.when(pl.program_id(0) == 0)
        def _():
            dg_ref[...] = jnp.zeros_like(dg_ref)

        dfb = df_ref[...]
        dh = jnp.zeros((tb, D_MODEL), F32)
        for j in range(FF_CHUNKS):
            for k in range(n_slab):
                cols = pl.ds((j * n_slab + k) * sw, sw)
                ra = ra_ref[:, cols].astype(F32)
                dab = _bf(_mm_nt(dfb, wo_ref[j, pl.ds(k * sw, sw), :]) * (2.0 * ra))
                da_ref[:, cols] = dab
                dh = dh + _mm_nt(dab, wi_refs[k][j])
        g3 = g3_ref[...]
        _, xh, r3 = _rms(x2_ref[...], g3)
        dxn, dg = _rms_bwd(dh, xh, r3, g3)
        dx_ref[...] = dy_ref[...] + dxn
        dg_ref[...] += dg

    return _rowcall("mlp_bwd", body, seq, tb, [x2, dy, df, ra], [g3, *w_ffi, w_ffo],
                    [(D_MODEL, F32), (D_FF, BF16)], [((1, D_MODEL), F32)], vmem=VMEM_BIG)


def _matmul_tn(name, a, b, tk, tn, tl, chunk_major, exchange=None, square_a=False):
    seq, kdim = a.shape
    ndim = b.shape[1]
    last = seq // tl - 1

    def body(a_ref, b_ref, o_ref, ob_ref):
        @pl.when(pl.program_id(2) == 0)
        def _():
            o_ref[...] = jnp.zeros_like(o_ref)

        a_blk = a_ref[...]
        if square_a:
            a_blk = _bf(jnp.square(a_blk.astype(F32)))
        o_ref[...] += _mm_tn(a_blk, b_ref[...])

        @pl.when(pl.program_id(2) == last)
        def _():
            ob_ref[...] = _bf(o_ref[...])

    if chunk_major:
        shape = (ndim // tn, kdim, tn)
        out_spec = pl.BlockSpec((None, tk, tn), lambda k, n, l: (n, k, 0))
    else:
        shape = (kdim, ndim)
        out_spec = pl.BlockSpec((tk, tn), lambda k, n, l: (k, n))
    return _fused_call(
        name, body, (kdim // tk, ndim // tn, seq // tl),
        [pl.BlockSpec((tl, tk), lambda k, n, l: (l, k)), pl.BlockSpec((tl, tn), lambda k, n, l: (l, n))],
        [out_spec, out_spec], [SDS(shape, F32), SDS(shape, BF16)], [], [a, b], exchange, _params(3, VMEM_BIG))


def _ew_call(name, fn, ins, n_out, after=None):
    rows, cols = ins[0].shape
    tr = rows
    while tr * cols * 4 > min(1 << 20, (9 << 20) // (len(ins) + n_out)) and tr % 16 == 0:
        tr //= 2
    spec = pl.BlockSpec((tr, cols), lambda i: (i, 0))
    extra = [] if after is None else [after]

    def body(*refs):
        outs = fn(*[r[...] for r in refs[:len(ins)]])
        for r, o in zip(refs[len(ins) + len(extra):], outs):
            r[...] = o

    return pl.pallas_call(
        body, grid=(rows // tr,), in_specs=[spec] * len(ins) + [ANY] * len(extra), out_specs=[spec] * n_out,
        out_shape=[SDS((rows, cols), F32)] * n_out, name=name, compiler_params=_params(1))(*ins, *extra)


def _adam_math(w, g, m, v):
    m2 = ADAM_B1 * m + (1.0 - ADAM_B1) * g
    v2 = ADAM_B2 * v + (1.0 - ADAM_B2) * (g * g)
    m_hat = m2 / (1.0 - ADAM_B1 ** ADAM_STEP)
    v_hat = v2 / (1.0 - ADAM_B2 ** ADAM_STEP)
    delta = -ADAM_LR * (m_hat / (jnp.sqrt(v_hat) + ADAM_EPS) + ADAM_WD * w)
    return delta, m2, v2


def _sum4(name, own, recv, idx):
    _, rows, cols = own.shape
    tr = rows
    while tr * cols * 4 > (1 << 20) and tr % 16 == 0:
        tr //= 2

    def body(idx_ref, o_ref, r0_ref, r1_ref, r2_ref, out_ref):
        out_ref[...] = ((o_ref[...] + r0_ref[...].astype(F32)) + r1_ref[...].astype(F32)) + r2_ref[...].astype(F32)

    blk = (None, tr, cols)
    grid_spec = pltpu.PrefetchScalarGridSpec(
        num_scalar_prefetch=1, grid=(rows // tr,),
        in_specs=[pl.BlockSpec(blk, lambda i, s: (s[0], i, 0)), pl.BlockSpec(blk, lambda i, s: (0, i, 0)),
                  pl.BlockSpec(blk, lambda i, s: (1, i, 0)), pl.BlockSpec(blk, lambda i, s: (2, i, 0))],
        out_specs=pl.BlockSpec((tr, cols), lambda i, s: (i, 0)))
    return pl.pallas_call(body, grid_spec=grid_spec, out_shape=SDS((rows, cols), F32), name=name,
                          compiler_params=_params(1))(jnp.reshape(idx, (1,)).astype(jnp.int32), own, recv, recv, recv)


def _sum4_group(name, owns, recvs, idx):
    n = len(owns)

    def body(idx_ref, *refs):
        for a in range(n):
            o_ref, r0_ref, r1_ref, r2_ref = refs[4 * a:4 * a + 4]
            refs[4 * n + a][...] = (((o_ref[...] + r0_ref[...].astype(F32)) + r1_ref[...].astype(F32))
                                    + r2_ref[...].astype(F32))

    in_specs, args = [], []
    for own, recv in zip(owns, recvs):
        blk = (None,) + own.shape[1:]
        in_specs += [pl.BlockSpec(blk, lambda i, s: (s[0], 0, 0))]
        in_specs += [pl.BlockSpec(blk, lambda i, s, k=k: (k, 0, 0)) for k in range(3)]
        args += [own, recv, recv, recv]
    grid_spec = pltpu.PrefetchScalarGridSpec(
        num_scalar_prefetch=1, grid=(1,), in_specs=in_specs,
        out_specs=[pl.BlockSpec(o.shape[1:], lambda i, s: (0, 0)) for o in owns])
    return pl.pallas_call(body, grid_spec=grid_spec, out_shape=[SDS(o.shape[1:], F32) for o in owns], name=name,
                          compiler_params=_params(1, VMEM_BIG))(jnp.reshape(idx, (1,)).astype(jnp.int32), *args)


def _adam_group(name, items, after):
    n = len(items)
    flat = [t for item in items for t in item]

    def body(*refs):
        outs = refs[5 * n + 1:]
        for a in range(n):
            w_, p, q, m_, v_ = [r[...] for r in refs[5 * a:5 * a + 5]]
            g = p + q
            for r, o in zip(outs[4 * a:4 * a + 4], (g,) + _adam_math(w_, g, m_, v_)):
                r[...] = o

    whole = lambda t: pl.BlockSpec(t.shape, lambda i: (0, 0))
    outs = pl.pallas_call(
        body, grid=(1,), in_specs=[whole(t) for t in flat] + [ANY],
        out_specs=[whole(item[0]) for item in items for _ in range(4)],
        out_shape=[SDS(item[0].shape, F32) for item in items for _ in range(4)], name=name,
        compiler_params=_params(1, VMEM_BIG))(*flat, after)
    return [outs[4 * a:4 * a + 4] for a in range(n)]


def _sum4_swap(name, own, recv, idx, n_blocks=4):
    _, rows, cols = own.shape
    tr = rows // n_blocks
    assert tr * n_blocks == rows and tr % SUBLANES == 0

    def body(idx_ref, o_ref, r0_ref, r1_ref, r2_ref, mine_ref, theirs_ref, buf, kept, sent, arrived):
        i = pl.program_id(0)
        slot = lax.rem(i, 2)
        x, y, c = _place()

        def copies(j, s):
            block = pl.ds(j * tr, tr)
            return (pltpu.make_async_copy(buf.at[s], mine_ref.at[block], kept.at[s]),
                    pltpu.make_async_remote_copy(
                        src_ref=buf.at[s], dst_ref=theirs_ref.at[block], send_sem=sent.at[s], recv_sem=arrived.at[j],
                        device_id=(x, y, 1 - c), device_id_type=MESH_ID))

        def finish(j, s):
            keep, send = copies(j, s)
            keep.wait()
            send.wait_send()
            send.wait_recv()

        @pl.when(i >= 2)
        def _():
            finish(i - 2, slot)

        buf[slot] = ((o_ref[...] + r0_ref[...].astype(F32)) + r1_ref[...].astype(F32)) + r2_ref[...].astype(F32)
        for cp in copies(i, slot):
            cp.start()

        @pl.when(i == n_blocks - 1)
        def _():
            if n_blocks > 1:
                finish(i - 1, 1 - slot)
            finish(i, slot)

    blk = (None, tr, cols)
    grid_spec = pltpu.PrefetchScalarGridSpec(
        num_scalar_prefetch=1, grid=(n_blocks,),
        in_specs=[pl.BlockSpec(blk, lambda i, s: (s[0], i, 0)), pl.BlockSpec(blk, lambda i, s: (0, i, 0)),
                  pl.BlockSpec(blk, lambda i, s: (1, i, 0)), pl.BlockSpec(blk, lambda i, s: (2, i, 0))],
        out_specs=[HBM, HBM],
        scratch_shapes=[pltpu.VMEM((2, tr, cols), F32), pltpu.SemaphoreType.DMA((2,)), pltpu.SemaphoreType.DMA((2,)),
                        pltpu.SemaphoreType.DMA((n_blocks,))])
    return pl.pallas_call(body, grid_spec=grid_spec, out_shape=[SDS((rows, cols), F32)] * 2, name=name,
                          compiler_params=_params(1))(jnp.reshape(idx, (1,)).astype(jnp.int32), own, recv, recv, recv)


def _adam_pair(name, item, after=None):
    def fn(w_, a, b, m_, v_):
        g = a + b
        return (g,) + _adam_math(w_, g, m_, v_)

    return _ew_call(name, fn, list(item), 4, after)


def _place():
    return lax.axis_index("x"), lax.axis_index("y"), lax.axis_index("c")


def _other_chips(x, y):
    return [(1 - x, y), (x, 1 - y), (1 - x, 1 - y)]


HBM = pl.BlockSpec(memory_space=pltpu.HBM)
SEM = pl.BlockSpec(memory_space=pltpu.SEMAPHORE)
DATAFLOW = pltpu.SideEffectType.DATAFLOW_SIDE_EFFECTING


class _Flight:
    def __init__(self, copies, n_copies, send, recv, srcs, lands, token):
        self.copies, self.n, self.send, self.recv = copies, n_copies, send, recv
        self.srcs, self.lands, self.token = list(srcs), list(lands), token


def _take_off(name, srcs, lands, copies, n_copies, after):
    n_s, n_l = len(srcs), len(lands)

    def body(*refs):
        src, land = refs[:n_s], refs[n_s:n_s + n_l]
        send, recv = refs[n_s + n_l + 1:n_s + n_l + 3]
        for cp in copies(src, land, send, recv):
            cp.start()
        refs[-1][...] = jnp.zeros_like(refs[-1])

    mem = lambda t: pltpu.HBM(t.shape, t.dtype)
    sems = pltpu.SemaphoreType.DMA((n_copies,))
    outs = pl.pallas_call(
        body, name=name,
        out_shape=(sems, sems, *map(mem, srcs), *map(mem, lands), SDS((SUBLANES, LANES), F32)),
        in_specs=[HBM] * (n_s + n_l) + [ANY],
        out_specs=(SEM, SEM, *[HBM] * (n_s + n_l), pl.BlockSpec(memory_space=pltpu.VMEM)),
        input_output_aliases={i: 2 + i for i in range(n_s + n_l)},
        compiler_params=pltpu.CompilerParams(has_side_effects=DATAFLOW),
    )(*[pltpu.with_memory_space_constraint(t, pltpu.HBM) for t in (*srcs, *lands)], after)
    return _Flight(copies, n_copies, outs[0], outs[1], outs[2:2 + n_s], outs[2 + n_s:2 + n_s + n_l], outs[-1])


def _land(name, flight, after):
    n_s, n_l = len(flight.srcs), len(flight.lands)

    def body(*refs):
        src, land = refs[:n_s], refs[n_s:n_s + n_l]
        send, recv = refs[n_s + n_l:n_s + n_l + 2]
        for cp in flight.copies(src, land, send, recv):
            cp.wait_send()
            cp.wait_recv()

    mem = lambda t: pltpu.HBM(t.shape, t.dtype)
    outs = pl.pallas_call(
        body, name=name, out_shape=(*map(mem, flight.srcs), *map(mem, flight.lands)),
        in_specs=[HBM] * (n_s + n_l) + [SEM, SEM, ANY], out_specs=tuple([HBM] * (n_s + n_l)),
        input_output_aliases={i: i for i in range(n_s + n_l)},
        compiler_params=pltpu.CompilerParams(has_side_effects=DATAFLOW),
    )(*flight.srcs, *flight.lands, flight.send, flight.recv, after)
    return list(outs[:n_s]), list(outs[n_s:])


def _empty_like(shapes_from, lead):
    return [lax.empty((lead,) + t.shape[1:], t.dtype) for t in shapes_from]


def _scatter_off(name, chunks, after):
    def copies(src, land, send, recv):
        x, y, c = _place()
        return [pltpu.make_async_remote_copy(
            src_ref=src[a].at[2 * px + py], dst_ref=land[a].at[k], send_sem=send.at[3 * a + k],
            recv_sem=recv.at[3 * a + k], device_id=(px, py, c), device_id_type=MESH_ID)
            for a in range(len(chunks)) for k, (px, py) in enumerate(_other_chips(x, y))]

    return _take_off(name, chunks, _empty_like(chunks, 3), copies, 3 * len(chunks), after)


def _swap_off(name, arrs, after):
    def copies(src, land, send, recv):
        x, y, c = _place()
        return [pltpu.make_async_remote_copy(
            src_ref=src[a], dst_ref=land[a], send_sem=send.at[a], recv_sem=recv.at[a],
            device_id=(x, y, 1 - c), device_id_type=MESH_ID) for a in range(len(arrs))]

    return _take_off(name, arrs, [lax.empty(t.shape, t.dtype) for t in arrs], copies, len(arrs), after)


def _devices_off(name, block, after):
    me = 4 * lax.axis_index("x") + 2 * lax.axis_index("y") + lax.axis_index("c")
    land = lax.dynamic_update_index_in_dim(lax.empty((N_DEV,) + block.shape, block.dtype), block, me, 0)

    def copies(src, land, send, recv):
        x, y, c = _place()
        mine = 4 * x + 2 * y + c
        return [pltpu.make_async_remote_copy(
            src_ref=src[0], dst_ref=land[0].at[mine], send_sem=send.at[k - 1], recv_sem=recv.at[k - 1],
            device_id=(x ^ (k >> 2), y ^ ((k >> 1) & 1), c ^ (k & 1)), device_id_type=MESH_ID)
            for k in range(1, N_DEV)]

    return _take_off(name, [block], [land], copies, N_DEV - 1, after)


def _half_rows(shape, c, other=False):
    half = shape[0] // 2
    return pl.ds(((1 - c) if other else c) * half, half)


def _gather_start(name, shards, lands, after):
    n = len(shards)

    def body(*refs):
        src, land, (send, recv) = refs[:n], refs[n:2 * n], refs[2 * n + 1:2 * n + 3]
        x, y, c = _place()
        me = 2 * x + y
        for a in range(n):
            mine = _half_rows(shards[a].shape, c)
            for j, (px, py) in enumerate(_other_chips(x, y)):
                pltpu.make_async_remote_copy(
                    src_ref=src[a].at[mine], dst_ref=land[a].at[me, mine], send_sem=send.at[3 * a + j],
                    recv_sem=recv.at[3 * a + j], device_id=(px, py, c), device_id_type=MESH_ID).start()
        token = refs[-1]
        token[...] = jnp.zeros_like(token)

    mem = lambda t: pltpu.HBM(t.shape, t.dtype)
    pair = pltpu.SemaphoreType.DMA((3 * n,))
    outs = pl.pallas_call(
        body, name=name,
        out_shape=(pair, pair, *map(mem, shards), *map(mem, lands), SDS((SUBLANES, LANES), F32)),
        in_specs=[HBM] * (2 * n) + [ANY],
        out_specs=(SEM, SEM, *[HBM] * (2 * n), pl.BlockSpec(memory_space=pltpu.VMEM)),
        input_output_aliases={i: 2 + i for i in range(2 * n)},
        compiler_params=pltpu.CompilerParams(has_side_effects=DATAFLOW),
    )(*[pltpu.with_memory_space_constraint(t, pltpu.HBM) for t in (*shards, *lands)], after)
    return outs[0], outs[1], list(outs[2:2 + n]), list(outs[2 + n:2 + 2 * n]), outs[-1]


def _gather_pass(name, send, recv, shards, lands, after, first=0):
    n = len(shards)

    def body(*refs):
        src, land, (send, recv, _) = refs[:n], refs[n:2 * n], refs[2 * n:2 * n + 3]
        fsend, frecv = refs[2 * n + 3], refs[2 * n + 4]
        x, y, c = _place()
        me = 2 * x + y
        for a in range(n):
            mine = _half_rows(shards[a].shape, c)
            for j, (px, py) in enumerate(_other_chips(x, y)):
                far = 2 * px + py
                ici = pltpu.make_async_remote_copy(
                    src_ref=src[a].at[mine], dst_ref=land[a].at[far, mine], send_sem=send.at[3 * (first + a) + j],
                    recv_sem=recv.at[3 * (first + a) + j], device_id=(px, py, c), device_id_type=MESH_ID)
                ici.wait_recv()
                ici.wait_send()
                pltpu.make_async_remote_copy(
                    src_ref=land[a].at[far, mine], dst_ref=land[a].at[far, mine], send_sem=fsend.at[3 * a + j],
                    recv_sem=frecv.at[3 * a + j], device_id=(x, y, 1 - c), device_id_type=MESH_ID).start()
        token = refs[-1]
        token[...] = jnp.zeros_like(token)

    mem = lambda t: pltpu.HBM(t.shape, t.dtype)
    pair = pltpu.SemaphoreType.DMA((3 * n,))
    outs = pl.pallas_call(
        body, name=name,
        out_shape=(pair, pair, *map(mem, lands), SDS((SUBLANES, LANES), F32)),
        in_specs=[HBM] * (2 * n) + [SEM, SEM, ANY],
        out_specs=(SEM, SEM, *[HBM] * n, pl.BlockSpec(memory_space=pltpu.VMEM)),
        input_output_aliases={n + i: 2 + i for i in range(n)},
        compiler_params=pltpu.CompilerParams(has_side_effects=DATAFLOW),
    )(*shards, *lands, send, recv, after)
    return outs[0], outs[1], list(outs[2:2 + n]), outs[-1]


def _gather_wait(name, fsend, frecv, lands, after):
    n = len(lands)

    def body(*refs):
        land, (fsend, frecv, _) = refs[:n], refs[n:n + 3]
        x, y, c = _place()
        for a in range(n):
            for j, (px, py) in enumerate(_other_chips(x, y)):
                far = 2 * px + py
                mine = _half_rows(lands[a].shape[1:], c)
                theirs = _half_rows(lands[a].shape[1:], c, other=True)
                pltpu.make_async_remote_copy(
                    src_ref=land[a].at[far, mine], dst_ref=land[a].at[far, mine], send_sem=fsend.at[3 * a + j],
                    recv_sem=frecv.at[3 * a + j], device_id=(x, y, 1 - c), device_id_type=MESH_ID).wait_send()
                pltpu.make_async_remote_copy(
                    src_ref=land[a].at[far, theirs], dst_ref=land[a].at[far, theirs], send_sem=fsend.at[3 * a + j],
                    recv_sem=frecv.at[3 * a + j], device_id=(x, y, 1 - c), device_id_type=MESH_ID).wait_recv()

    mem = lambda t: pltpu.HBM(t.shape, t.dtype)
    return list(pl.pallas_call(
        body, name=name, out_shape=tuple(map(mem, lands)), in_specs=[HBM] * n + [SEM, SEM, ANY],
        out_specs=tuple([HBM] * n), input_output_aliases={i: i for i in range(n)},
        compiler_params=pltpu.CompilerParams(has_side_effects=DATAFLOW),
    )(*lands, fsend, frecv, after))


def _after(token):
    return _Exchange([token], [], [], lambda *_: None, lambda *_: None)


def _sum_devices(slots):
    def body(s_ref, o_ref):
        acc = s_ref[0]
        for d in range(1, N_DEV):
            acc = acc + s_ref[d]
        o_ref[...] = acc

    return pl.pallas_call(
        body, in_specs=[pl.BlockSpec(memory_space=pltpu.VMEM)], out_specs=pl.BlockSpec(memory_space=pltpu.VMEM),
        out_shape=SDS(slots.shape[1:], F32), name="sum_small",
        compiler_params=pltpu.CompilerParams(vmem_limit_bytes=32 * 1024 * 1024))(slots)


def _adam_small(ws, gs, ms, vs):
    n = len(ws)

    def body(*refs):
        for i in range(n):
            w_ref, g_ref, m_ref, v_ref = (refs[k * n + i] for k in range(4))
            outs = _adam_math(w_ref[...], g_ref[...], m_ref[...], v_ref[...])
            for k in range(3):
                refs[(4 + k) * n + i][...] = outs[k]

    vmem = pl.BlockSpec(memory_space=pltpu.VMEM)
    return pl.pallas_call(
        body, in_specs=[vmem] * (4 * n), out_specs=[vmem] * (3 * n),
        out_shape=[SDS(w.shape, F32) for w in ws] * 3, name="adam_small",
        compiler_params=pltpu.CompilerParams(vmem_limit_bytes=32 * 1024 * 1024))(*ws, *gs, *ms, *vs)


def _local_step(x, target, small, big, tb, distributed):
    dist = distributed
    me = (2 * lax.axis_index("x") + lax.axis_index("y")) if dist else 0
    tb_ssm = min(tb, 256)
    bucket = jnp.asarray(_bucket_table())
    place_own = lambda t: lax.dynamic_update_index_in_dim(lax.empty((N_CHIPS,) + t.shape, t.dtype), t, me, 0)
    if dist:
        in_legs = _gather_start("gather_in_start", [big["w_in"]], [place_own(big["w_in"])], small["d_skip"])
        names = sorted(small)
        in_token, values = lax.optimization_barrier((in_legs[4], [small[n] for n in names]))
        small = dict(zip(names, values))
    g1, g2, g3, g4 = small["norm_mix_pre"], small["norm_mix_post"], small["norm_mlp_pre"], small["norm_mlp_post"]

    keys_first = lambda t: jnp.swapaxes(t, -1, -2)
    bias = _bias_table(small["rel_bias"], bucket)
    sink_rows = keys_first(_pair_layout(jnp.broadcast_to(small["sinks"].reshape(N_HEADS, 1, 1), (N_HEADS, BLOCK, 1))))
    disc_args = (small["lam_re"], small["lam_im"], small["log_dt"], small["b_re"], small["b_im"])
    (ab_re, ab_im, bb_re, bb_im), disc_vjp = jax.vjp(_ssm_discretize, *disc_args)
    tab_f, tab_b = _scan_tables(ab_re, ab_im)
    bmat = _bf(_b_matrix(bb_re, bb_im))
    cmat = _bf(_c_matrix(small["c_re"], small["c_im"]))
    bmat_t, cmat_t = bmat.transpose(0, 2, 1), cmat.transpose(0, 2, 1)
    d_skip = small["d_skip"]

    mix = ("w_glu", "w_attn_branch", "w_ssm_branch", "w_out")
    rest = [big[n] for n in mix + ("w_ff_in", "w_ff_out")]
    if dist:
        send, recv, src, lands, _ = in_legs
        tab_f, tab_b, bias, sink_rows, bmat, cmat, bmat_t, cmat_t, rest, rest_lands = lax.optimization_barrier(
            (tab_f, tab_b, bias, sink_rows, bmat, cmat, bmat_t, cmat_t, rest, [place_own(t) for t in rest]))
        corner = lambda t: t.reshape(-1, t.shape[-1])[:1, :LANES].astype(F32)
        prepared = sum(map(corner, [tab_b, bias, sink_rows, bmat, cmat] + rest_lands), in_token[:1])
        send, recv, lands, in_passed = _gather_pass("gather_in_pass", send, recv, src, lands, prepared)
        (g_in,) = _gather_wait("gather_in_wait", send, recv, lands, in_passed)
        w_in = g_in.reshape(IN_W, D_MODEL)
    else:
        w_in = big["w_in"]
    token = None
    n_mix = len(mix)
    if dist:
        send, recv, rest, lands, token = _gather_start("gather_rest_start", rest, rest_lands, in_passed)
    h1, q, k, v, u, ga, gs = _inproj_fwd(x, g1, w_in, tb, _after(token) if dist else None)
    s, h = _ssm_fwd(u, bmat, cmat, tab_f, d_skip, tb)
    if dist:
        fsend, frecv, mix_lands, token = _gather_pass("gather_mix_pass", send, recv, rest[:n_mix], lands[:n_mix], s)
    att = _attn_fwd(q, k, v, bias, sink_rows, _after(token) if dist else None)[0]
    if dist:
        w_mix = _gather_wait("gather_mix_wait", fsend, frecv, mix_lands, att)
        fsend, frecv, ff_lands, token = _gather_pass(
            "gather_ff_pass", send, recv, rest[n_mix:], lands[n_mix:], w_mix[0], n_mix)
        rest = w_mix + ff_lands
    w_glu, w_ab, w_sb, w_out = rest[:n_mix]
    w_glu = w_glu.reshape(SSM_W, SSM_W)
    w_out = w_out.reshape(D_MODEL, D_MODEL)
    x2 = _merge_fwd(x, s, att, ga, gs, g2, w_glu, w_ab, w_sb, w_out, tb, _after(token) if dist else None)
    if dist:
        rest[n_mix:] = _gather_wait("gather_ff_wait", fsend, frecv, ff_lands, x2)
    w_ffi, w_ffo = [rest[n_mix]], rest[n_mix + 1]
    dy, df, h3, ra, loss_acc, dg4 = _mlp_fwd_loss(x2, target, g3, g4, w_ffi, w_ffo, tb)

    dx2, da, dg3 = _mlp_bwd(x2, dy, df, ra, g3, w_ffi, w_ffo, tb)
    tl = min(2048, x.shape[0])
    chunked = (N_CHIPS, D_FF // N_CHIPS, D_MODEL)
    d_ffi, b_ffi = _matmul_tn("grad_w_ff_in", h3, da, D_MODEL, D_FF // FF_CHUNKS, tl, True)
    d_ffo, b_ffo = _matmul_tn("grad_w_ff_out", ra, df, D_FF // FF_CHUNKS, D_MODEL, tl, False, square_a=True)
    d_ffo, b_ffo = d_ffo.reshape(chunked), b_ffo.reshape(chunked)
    behind = lambda flight: _after(flight.token) if dist else None
    ff_fl = _scatter_off("scatter_ff_off", [b_ffi, b_ffo], d_ffo) if dist else None
    outs = _merge_bwd(dx2, s, att, ga, gs, g2, w_glu, w_ab, w_sb, w_out, tb_ssm, behind(ff_fl))
    ds, datt, dga, dgs, dg2, d_glu, d_ab, d_sb, d_out, b_glu, b_ab, b_sb, b_out = outs
    glu4, out4 = (N_CHIPS, SSM_W // N_CHIPS, SSM_W), (N_CHIPS, D_MODEL // N_CHIPS, D_MODEL)
    d_mix = [d_glu.reshape(glu4), d_ab, d_sb, d_out.reshape(out4)]
    b_mix = [b_glu.reshape(glu4), b_ab, b_sb, b_out.reshape(out4)]
    mix_fl = _scatter_off("scatter_mix_off", b_mix, d_mix[-1]) if dist else None
    du, d_bmat, d_cmat, da_acc, dd_skip = _ssm_bwd(
        ds, u, h, bmat_t, cmat_t, tab_b, d_skip, tb, behind(mix_fl))
    dq, dk, dv, dbias, dsink_rows = _attn_bwd(q, k, v, datt, bias, sink_rows)
    dx, dpj, dg1 = _inproj_bwd(x, dx2, dq, dk, dv, du, dga, dgs, g1, w_in, tb)

    dab_re, dab_im = _state_unlayout(jnp.sum(da_acc, axis=0))
    dbb_re, dbb_im = _b_matrix_grad(d_bmat)
    d_lam_re, d_lam_im, d_log_dt, d_b_re, d_b_im = disc_vjp((dab_re, dab_im, dbb_re, dbb_im))
    d_c_re, d_c_im = _c_matrix_grad(d_cmat)
    d_rel = _bias_grad(dbias, bucket)
    d_sinks = jnp.sum(_pair_unlayout(keys_first(dsink_rows)), axis=(1, 2))
    small_grads = dict(
        norm_mix_pre=dg1, norm_mix_post=dg2, norm_mlp_pre=dg3, norm_mlp_post=dg4, rel_bias=d_rel, sinks=d_sinks,
        lam_re=d_lam_re, lam_im=d_lam_im, log_dt=d_log_dt, b_re=d_b_re, b_im=d_b_im, c_re=d_c_re, c_im=d_c_im,
        d_skip=dd_skip)
    small_fl = _devices_off("small_off", _pack(small_grads, loss_acc), dg1) if dist else None
    outs = _matmul_tn("grad_w_in", dpj, h1, IN_W // 2, D_MODEL, tl, False, behind(small_fl))
    in4 = (N_CHIPS, IN_W // N_CHIPS, D_MODEL)
    d_in, b_in = outs[0].reshape(in4), outs[1].reshape(in4)
    if not dist:
        return loss_acc, dx, small_grads, dict(zip(BIG, [d_in] + d_mix + [d_ffi, d_ffo]))
    in_fl = _scatter_off("scatter_w_in_off", [b_in], d_in)
    r_ffi, r_ffo = _land("scatter_ff_land", ff_fl, in_fl.token)[1]
    p_ffi = _sum4("sum_w_ff_in", d_ffi, r_ffi, me)
    p_ffo = _sum4("sum_w_ff_out", d_ffo, r_ffo, me)
    swap_fl = _swap_off("swap_ff_off", [p_ffi, p_ffo], r_ffo)
    r_mix = _land("scatter_mix_land", mix_fl, swap_fl.token)[1]
    p_mix = _sum4_group("sum_mix", d_mix, r_mix, me)
    mix_swap = _swap_off("swap_mix_off", p_mix, swap_fl.token)
    (p_ffi, p_ffo), (s_ffi, s_ffo) = _land("swap_ff_land", swap_fl, mix_swap.token)
    pending = dict(d_in=d_in, in_fl=in_fl, mix_swap=mix_swap, w_ff_in=(p_ffi, s_ffi), w_ff_out=(p_ffo, s_ffo), me=me)
    return loss_acc, dx, small_fl, pending


SMALL = ['norm_mix_pre', 'norm_mix_post', 'norm_mlp_pre', 'norm_mlp_post', 'rel_bias', 'sinks', 'lam_re', 'lam_im',
         'log_dt', 'b_re', 'b_im', 'c_re', 'c_im', 'd_skip']
BIG = ['w_in', 'w_glu', 'w_attn_branch', 'w_ssm_branch', 'w_out', 'w_ff_in', 'w_ff_out']
WEIGHTS = ['norm_mix_pre', 'norm_mix_post', 'norm_mlp_pre', 'norm_mlp_post', 'w_in', 'rel_bias', 'sinks', 'lam_re',
           'lam_im', 'log_dt', 'b_re', 'b_im', 'c_re', 'c_im', 'd_skip', 'w_glu', 'w_attn_branch', 'w_ssm_branch',
           'w_out', 'w_ff_in', 'w_ff_out']
PACK_COLS = 1024
PACK_ORDER = ['b_re', 'b_im', 'c_re', 'c_im', 'lam_re', 'lam_im', 'norm_mix_pre', 'norm_mix_post', 'norm_mlp_pre',
              'norm_mlp_post', 'rel_bias', 'sinks', 'log_dt', 'd_skip']


STATE_MINOR = ('b_re', 'b_im')
PACK_ROWS = 144
LOSS_ROW = 140


def _pack(named, loss_acc):
    parts = []
    for n in PACK_ORDER:
        a = jnp.swapaxes(named[n], -1, -2) if n in STATE_MINOR else named[n]
        flat = a.reshape(-1)
        rows = -(-flat.shape[0] // PACK_COLS)
        parts.append(jnp.pad(flat, (0, rows * PACK_COLS - flat.shape[0])).reshape(rows, PACK_COLS))
    assert sum(p.shape[0] for p in parts) == LOSS_ROW
    parts.append(jnp.pad(loss_acc[0:1], ((0, PACK_ROWS - LOSS_ROW - 1), (0, PACK_COLS - loss_acc.shape[1]))))
    return jnp.concatenate(parts, axis=0)


def _unpack(packed, shapes):
    out, at = {}, 0
    for n in PACK_ORDER:
        shape = shapes[n][:-2] + (shapes[n][-1], shapes[n][-2]) if n in STATE_MINOR else shapes[n]
        size = int(np.prod(shape))
        rows = -(-size // PACK_COLS)
        blk = packed[at:at + rows]
        out[n] = (blk.reshape(-1)[:size] if size % PACK_COLS else blk).reshape(shape)
        at += rows
    return out


def kernel(x, norm_mix_pre, norm_mix_post, norm_mlp_pre, norm_mlp_post, w_in, rel_bias, sinks, lam_re, lam_im, log_dt, b_re, b_im, c_re, c_im, d_skip, w_glu, w_attn_branch, w_ssm_branch, w_out, w_ff_in, w_ff_out, loss_target, m_norm_mix_pre, m_norm_mix_post, m_norm_mlp_pre, m_norm_mlp_post, m_w_in, m_rel_bias, m_sinks, m_lam_re, m_lam_im, m_log_dt, m_b_re, m_b_im, m_c_re, m_c_im, m_d_skip, m_w_glu, m_w_attn_branch, m_w_ssm_branch, m_w_out, m_w_ff_in, m_w_ff_out, v_norm_mix_pre, v_norm_mix_post, v_norm_mlp_pre, v_norm_mlp_post, v_w_in, v_rel_bias, v_sinks, v_lam_re, v_lam_im, v_log_dt, v_b_re, v_b_im, v_c_re, v_c_im, v_d_skip, v_w_glu, v_w_attn_branch, v_w_ssm_branch, v_w_out, v_w_ff_in, v_w_ff_out):
    env = dict(locals())
    w = {n: env[n] for n in WEIGHTS}
    m = {n: env["m_" + n] for n in WEIGHTS}
    v = {n: env["v_" + n] for n in WEIGHTS}
    seq = x.shape[1]
    tb = min(512, seq)

    small = {n: w[n] for n in ('norm_mix_pre', 'norm_mix_post', 'norm_mlp_pre', 'norm_mlp_post', 'rel_bias')}
    small.update({n: w[n][0] for n in ('sinks', 'lam_re', 'lam_im', 'log_dt', 'b_re', 'b_im', 'c_re', 'c_im')})
    small['d_skip'] = w['d_skip']
    shard = lambda t, n: t[n][0].T if n == 'w_in' else t[n][0]
    unshard = lambda a, n: (a.T if n == 'w_in' else a)[None]
    _, dx, small_fl, pending = _local_step(
        x[0], loss_target[0], small, {n: _bf(shard(w, n)) for n in BIG}, tb, True)

    grads, deltas, new_m, new_v = {}, {}, {}, {}

    def adam(n, partials, after=None):
        outs = _adam_pair("adam_" + n, (shard(w, n), *partials, shard(m, n), shard(v, n)), after)
        grads[n], deltas[n], new_m[n], new_v[n] = [unshard(a, n) for a in outs]
        return outs[3]

    mix = ("w_glu", "w_attn_branch", "w_ssm_branch", "w_out")
    in_fl = pending["in_fl"]
    last = pending["mix_swap"].token
    for n in ("w_ff_in", "w_ff_out"):
        last = adam(n, pending[n], last)
    own_mix, sib_mix = _land("swap_mix_land", pending["mix_swap"], last)
    outs = _adam_group("adam_mix", [(shard(w, n), p, s, shard(m, n), shard(v, n))
                                    for n, p, s in zip(mix, own_mix, sib_mix)], last)
    for n, item in zip(mix, outs):
        grads[n], deltas[n], new_m[n], new_v[n] = [unshard(a, n) for a in item]
    last = outs[-1][3]

    small_g = _sum_devices(_land("small_land", small_fl, last)[1][0])
    loss = small_g[LOSS_ROW, 0]
    minor = lambda t, n: jnp.swapaxes(t, -1, -2) if n in STATE_MINOR else t
    g_small = _unpack(small_g, {n: w[n].shape for n in SMALL})
    outs = _adam_small([minor(w[n], n) for n in SMALL], [g_small[n] for n in SMALL],
                       [minor(m[n], n) for n in SMALL], [minor(v[n], n) for n in SMALL])
    grads.update({n: minor(g_small[n], n) for n in SMALL})
    for k, dst in enumerate((deltas, new_m, new_v)):
        dst.update({n: minor(a, n) for n, a in zip(SMALL, outs[k * len(SMALL):(k + 1) * len(SMALL)])})

    (r_in,) = _land("scatter_w_in_land", in_fl, outs[0])[1]
    adam("w_in", _sum4_swap("sum_swap_w_in", pending["d_in"], r_in, pending["me"]))

    return (loss, dx[None], *[grads[n] for n in WEIGHTS], *[deltas[n] for n in WEIGHTS],
            *[new_m[n] for n in WEIGHTS], *[new_v[n] for n in WEIGHTS])
```

```python
import functools
import math

import numpy as np
import jax
import jax.numpy as jnp
from jax import lax
from jax.experimental import pallas as pl
from jax.experimental.pallas import tpu as pltpu

F32 = jnp.float32
BF16 = jnp.bfloat16

D_MODEL = 1024
N_HEADS = 8
N_KV = 2
Q_GROUP = 4
HEAD_DIM = 64
ATTN_W = 512
KV_W = 128
BLOCK = 128
N_BUCKETS = 32
MAX_DISTANCE = 128
NEG_INF = -1e30
SSM_W = 512
SSM_GROUP = 16
SSM_GROUPS = 32
SSM_STATE = 64
N_SUPER = 4
GROUPS_PER_SUPER = SSM_GROUPS // N_SUPER
SUPER_IN = GROUPS_PER_SUPER * SSM_GROUP
SUPER_HALF = GROUPS_PER_SUPER * SSM_STATE
SUPER_W = 2 * SUPER_HALF
STATE_COLS = N_SUPER * SUPER_W
D_FF = 4096
FF_CHUNKS = 4
IN_W = 3328
SPLITS = (0, 512, 640, 768, 1280, 2304, 3328)
RMS_EPS = 1e-6
N_CHIPS = 4
N_DEV = 8
SUBLANES = 8
LANES = 128
STATE_TILES = STATE_COLS // LANES
SUPER_TILES = SUPER_W // LANES

ADAM_LR = 0.001
ADAM_B1 = 0.9
ADAM_B2 = 0.999
ADAM_EPS = 1e-08
ADAM_WD = 0.01
ADAM_STEP = 10

VMEM_BIG = 56 * 1024 * 1024
SDS = jax.ShapeDtypeStruct
MESH_ID = pl.DeviceIdType.MESH
ANY = pl.BlockSpec(memory_space=pl.ANY)


def _bf(x):
    return x.astype(BF16)


def _mm(a, b):
    return jnp.dot(a, b, preferred_element_type=F32)


def _mm_nt(a, b):
    return lax.dot_general(a, b, (((1,), (1,)), ((), ())), preferred_element_type=F32)


def _mm_tn(a, b):
    return lax.dot_general(a, b, (((0,), (0,)), ((), ())), preferred_element_type=F32)


def _sig(x):
    return 1.0 / (1.0 + jnp.exp(-x))


def _rms(x, g):
    r = lax.rsqrt(jnp.mean(x * x, axis=-1, keepdims=True) + RMS_EPS)
    xh = x * r
    return xh * g, xh, r


def _rms_bwd(dout, xh, r, g):
    dg = jnp.sum(dout * xh, axis=0, keepdims=True)
    dxh = dout * g
    dx = r * (dxh - xh * jnp.mean(dxh * xh, axis=-1, keepdims=True))
    return dx, dg


_GELU_C = math.sqrt(2.0 / math.pi)


def _gelu_and_grad(x):
    x2 = x * x
    inner = _GELU_C * (x + 0.044715 * (x2 * x))
    t = jnp.tanh(inner)
    y = 0.5 * x * (1.0 + t)
    dy = 0.5 * (1.0 + t) + 0.5 * x * (1.0 - t * t) * (_GELU_C * (1.0 + 3.0 * 0.044715 * x2))
    return y, dy


def _zero_map(nd, *_):
    return (0,) * nd


def _params(n_axes, vmem=None):
    return pltpu.CompilerParams(dimension_semantics=("arbitrary",) * n_axes, vmem_limit_bytes=vmem)


class _Exchange:
    def __init__(self, ins, outs, sems, start, wait):
        self.ins, self.outs, self.sems, self.start, self.wait = list(ins), list(outs), list(sems), start, wait


def _fused_call(name, body, grid, in_specs, out_specs, out_shape, scratch, args, exchange, params):
    n_in, n_out, n_scr = len(in_specs), len(out_specs), len(scratch)
    if exchange is None:
        fn = body
    else:
        ex = exchange
        n_xi, n_xo = len(ex.ins), len(ex.outs)

        def fn(*refs):
            at = 0
            parts = []
            for n in (n_in, n_xi, n_out, n_xo, n_scr, len(ex.sems)):
                parts.append(refs[at:at + n])
                at += n
            ins, x_in, outs, x_out, scr, x_sem = parts
            ids = [pl.program_id(a) for a in range(len(grid))]
            first = functools.reduce(jnp.logical_and, [i == 0 for i in ids])
            last = functools.reduce(jnp.logical_and, [i == g - 1 for i, g in zip(ids, grid)])

            @pl.when(first)
            def _():
                ex.start(x_in, x_out, x_sem)

            body(*ins, *outs, *scr)

            @pl.when(last)
            def _():
                ex.wait(x_in, x_out, x_sem)

        in_specs = list(in_specs) + [ANY] * n_xi
        out_specs = list(out_specs) + [ANY] * n_xo
        out_shape = list(out_shape) + ex.outs
        scratch = list(scratch) + ex.sems
        args = list(args) + ex.ins
    return pl.pallas_call(fn, grid=grid, in_specs=in_specs, out_specs=out_specs, out_shape=out_shape,
                          scratch_shapes=list(scratch), name=name, compiler_params=params)(*args)


def _rowcall(name, body, seq, tb, rows, consts, row_outs, acc_outs, scratch=(), reverse=False, vmem=None,
             exchange=None):
    nb = seq // tb
    rmap = (lambda i: (nb - 1 - i, 0)) if reverse else (lambda i: (i, 0))
    tmap = lambda i: (0,) + rmap(i)

    def row_spec(width):
        if isinstance(width, tuple):
            return pl.BlockSpec((width[0], tb, width[1]), tmap)
        return pl.BlockSpec((tb, width), rmap)

    def row_shape(width):
        return (width[0], seq, width[1]) if isinstance(width, tuple) else (seq, width)

    in_specs = [row_spec(a.shape[1] if a.ndim == 2 else (a.shape[0], a.shape[2])) for a in rows]
    in_specs += [pl.BlockSpec(a.shape, functools.partial(_zero_map, a.ndim), pipeline_mode=pl.Buffered(1))
                 for a in consts]
    out_specs = [row_spec(c) for c, _ in row_outs] + [ANY] * len(acc_outs)
    out_shape = [SDS(row_shape(c), dt) for c, dt in row_outs] + [SDS(s, dt) for s, dt in acc_outs]
    n_main = len(rows) + len(consts) + len(row_outs)
    n_acc = len(acc_outs)

    def fn(*refs):
        main, acc_hbm, rest = refs[:n_main], refs[n_main:n_main + n_acc], refs[n_main + n_acc:]
        acc_vmem, own = rest[:n_acc], rest[n_acc:]
        body(*main, *acc_vmem, *own)

        @pl.when(pl.program_id(0) == nb - 1)
        def _():
            for src, dst in zip(acc_vmem, acc_hbm):
                pltpu.sync_copy(src, dst)

    buffers = [pltpu.VMEM(s, dt) for s, dt in acc_outs] + list(scratch)
    return _fused_call(name, fn if acc_outs else body, (nb,), in_specs, out_specs, out_shape, buffers,
                       [*rows, *consts], exchange, _params(1, vmem))


def _inproj_fwd(x, g1, w_in, tb, exchange=None):
    seq = x.shape[0]

    def body(x_ref, g_ref, w_ref, h_ref, q_ref, k_ref, v_ref, u_ref, ga_ref, gs_ref):
        h, _, _ = _rms(x_ref[...], g_ref[...])
        hb = _bf(h)
        h_ref[...] = hb
        pj = _mm_nt(hb, w_ref[...])
        q_ref[...] = _bf(pj[:, SPLITS[0]:SPLITS[1]])
        k_ref[...] = _bf(pj[:, SPLITS[1]:SPLITS[2]])
        v_ref[...] = _bf(pj[:, SPLITS[2]:SPLITS[3]])
        u_ref[...] = pj[:, SPLITS[3]:SPLITS[4]]
        ga_ref[...] = pj[:, SPLITS[4]:SPLITS[5]]
        gs_ref[...] = pj[:, SPLITS[5]:SPLITS[6]]

    return _rowcall("inproj_fwd", body, seq, tb, [x], [g1, w_in],
                    [(D_MODEL, BF16), (ATTN_W, BF16), (KV_W, BF16), (KV_W, BF16), (SSM_W, F32),
                     (D_MODEL, F32), (D_MODEL, F32)], [], vmem=VMEM_BIG, exchange=exchange)


def _inproj_bwd(x, dx2, dq, dk, dv, du, dga, dgs, g1, w_in, tb, exchange=None):
    seq = x.shape[0]

    def body(x_ref, dx2_ref, dq_ref, dk_ref, dv_ref, du_ref, dga_ref, dgs_ref, g_ref, w_ref,
             dx_ref, dpj_ref, dg_ref):
        @pl.when(pl.program_id(0) == 0)
        def _():
            dg_ref[...] = jnp.zeros_like(dg_ref)

        dpj = jnp.concatenate([dq_ref[...], dk_ref[...], dv_ref[...], _bf(du_ref[...]),
                               dga_ref[...], dgs_ref[...]], axis=1)
        dpj_ref[...] = dpj
        dh = _mm(dpj, w_ref[...])
        g = g_ref[...]
        _, xh, r = _rms(x_ref[...], g)
        dxn, dg = _rms_bwd(dh, xh, r, g)
        dx_ref[...] = dx2_ref[...] + dxn
        dg_ref[...] += dg

    return _rowcall("inproj_bwd", body, seq, tb, [x, dx2, dq, dk, dv, du, dga, dgs], [g1, w_in],
                    [(D_MODEL, F32), (IN_W, BF16)], [((1, D_MODEL), F32)], vmem=VMEM_BIG, exchange=exchange)


def _bucket_table():
    qi = np.arange(BLOCK)[:, None]
    kj = np.arange(2 * BLOCK)[None, :]
    dist = qi + BLOCK - kj
    max_exact = N_BUCKETS // 2
    d = np.maximum(dist, 0)
    df = np.maximum(d, 1).astype(np.float32)
    large = max_exact + (np.log(df / np.float32(max_exact)) / np.float32(math.log(MAX_DISTANCE / max_exact))
                         * np.float32(N_BUCKETS - max_exact)).astype(np.int32)
    large = np.minimum(large, N_BUCKETS - 1)
    bucket = np.where(d < max_exact, d, large)
    valid = (dist >= 0) & (dist < BLOCK)
    return np.where(valid, bucket, -1).astype(np.int32)


def _bias_table(rel_bias, bucket):
    def body(rb_ref, bk_ref, o_ref):
        bk = bk_ref[...]
        has_prev = lax.broadcasted_iota(jnp.int32, bk.shape, 1) >= BLOCK
        for h in range(N_HEADS):
            kh, j, par = h // Q_GROUP, (h // 2) % 2, h % 2
            acc = jnp.full((BLOCK, 2 * BLOCK), NEG_INF, F32)
            for b in range(N_BUCKETS):
                acc = jnp.where(bk == b, rb_ref[b, h], acc)
            o_ref[0, kh, par, :, j * BLOCK:(j + 1) * BLOCK] = jnp.where(has_prev, acc, NEG_INF).T
            o_ref[1, kh, par, :, j * BLOCK:(j + 1) * BLOCK] = acc.T

    return pl.pallas_call(
        body, out_shape=SDS((2, N_KV, 2, 2 * BLOCK, 2 * BLOCK), F32),
        in_specs=[pl.BlockSpec(memory_space=pltpu.SMEM), pl.BlockSpec(memory_space=pltpu.VMEM)],
        out_specs=pl.BlockSpec(memory_space=pltpu.VMEM), name="bias_table",
    )(rel_bias, bucket)


def _bias_grad(dbias, bucket):
    def body(db_ref, bk_ref, o_ref):
        bk = bk_ref[...]
        for h in range(N_HEADS):
            kh, j, par = h // Q_GROUP, (h // 2) % 2, h % 2
            db = db_ref[kh, par, :, j * BLOCK:(j + 1) * BLOCK].T
            for b in range(N_BUCKETS):
                o_ref[b, h] = jnp.sum(jnp.where(bk == b, db, 0.0))

    return pl.pallas_call(
        body, out_shape=SDS((N_BUCKETS, N_HEADS), F32),
        in_specs=[pl.BlockSpec(memory_space=pltpu.VMEM), pl.BlockSpec(memory_space=pltpu.VMEM)],
        out_specs=pl.BlockSpec(memory_space=pltpu.SMEM), name="bias_grad",
    )(dbias, bucket)


TILE = 2 * HEAD_DIM


def _pair_layout(t):
    lead = t.shape[:-3]
    t = t.reshape(lead + (N_KV, 2, 2) + t.shape[-2:])
    nl = len(lead)
    t = jnp.transpose(t, tuple(range(nl)) + (nl, nl + 2, nl + 1, nl + 3, nl + 4))
    return t.reshape(lead + (N_KV, 2, 2 * BLOCK, t.shape[-1]))


def _pair_unlayout(t):
    t = t.reshape(N_KV, 2, 2, BLOCK, t.shape[-1]).transpose(0, 2, 1, 3, 4)
    return t.reshape(N_HEADS, BLOCK, t.shape[-1])


def _halves(t):
    tf = t.astype(F32)
    low = lax.broadcasted_iota(jnp.int32, tf.shape, 1) < HEAD_DIM
    swapped = pltpu.roll(tf, HEAD_DIM, 1)
    zero = jnp.zeros_like(tf)
    return ((_bf(jnp.where(low, tf, zero)), _bf(jnp.where(low, zero, swapped))),
            (_bf(jnp.where(low, swapped, zero)), _bf(jnp.where(low, zero, tf))))


def _fold_halves(even, odd):
    low = lax.broadcasted_iota(jnp.int32, even.shape, 1) < HEAD_DIM
    comb = jnp.where(low, even, odd)
    return comb + pltpu.roll(comb, HEAD_DIM, 1)


def _tile_rows(ref, kh):
    return jnp.concatenate([ref[:, (2 * kh) * TILE:(2 * kh + 1) * TILE],
                            ref[:, (2 * kh + 1) * TILE:(2 * kh + 2) * TILE]], axis=0)


def _halves_t(t):
    tt = t.astype(F32).T
    top = lax.broadcasted_iota(jnp.int32, tt.shape, 0) < HEAD_DIM
    swapped = jnp.concatenate([tt[HEAD_DIM:], tt[:HEAD_DIM]], axis=0)
    zero = jnp.zeros_like(tt)
    return ((_bf(jnp.where(top, tt, zero)), _bf(jnp.where(top, zero, swapped))),
            (_bf(jnp.where(top, swapped, zero)), _bf(jnp.where(top, zero, tt))))


def _attn_probs(km, qk, bias, sink):
    lg = _mm_nt(km, qk) * (HEAD_DIM ** -0.5) + bias
    m = jnp.maximum(jnp.max(lg, axis=0, keepdims=True), sink)
    p = jnp.exp(lg - m)
    es = jnp.exp(sink - m)
    inv = 1.0 / (jnp.sum(p, axis=0, keepdims=True) + es)
    return p * inv, es * inv


def _attn_fwd(q, k, v, bias, sink_rows, exchange=None):
    seq = q.shape[0]
    nblk = seq // BLOCK

    def body(q_ref, kp_ref, kc_ref, vp_ref, vc_ref, b_ref, s_ref, o_ref):
        which = jnp.minimum(pl.program_id(0), 1)
        kms = _halves(jnp.concatenate([kp_ref[...], kc_ref[...]], axis=0))
        vts = _halves_t(jnp.concatenate([vp_ref[...], vc_ref[...]], axis=0))
        for kh in range(N_KV):
            qk = _tile_rows(q_ref, kh)
            acc = jnp.zeros((TILE, 2 * BLOCK), F32)
            for par in range(2):
                pr, _ = _attn_probs(kms[kh][par], qk, b_ref[which, kh, par], s_ref[kh, par])
                acc = acc + _mm(vts[kh][par], _bf(pr))
            acc = acc.T
            o_ref[:, (2 * kh) * TILE:(2 * kh + 1) * TILE] = _bf(acc[:BLOCK])
            o_ref[:, (2 * kh + 1) * TILE:(2 * kh + 2) * TILE] = _bf(acc[BLOCK:])

    cur = lambda n: (n, 0)
    prev = lambda n: (jnp.maximum(n - 1, 0), 0)
    return _fused_call(
        "attn_fwd", body, (nblk,),
        [pl.BlockSpec((BLOCK, ATTN_W), cur),
         pl.BlockSpec((BLOCK, KV_W), prev), pl.BlockSpec((BLOCK, KV_W), cur),
         pl.BlockSpec((BLOCK, KV_W), prev), pl.BlockSpec((BLOCK, KV_W), cur),
         pl.BlockSpec(bias.shape, functools.partial(_zero_map, bias.ndim)),
         pl.BlockSpec(sink_rows.shape, functools.partial(_zero_map, sink_rows.ndim))],
        [pl.BlockSpec((BLOCK, ATTN_W), cur)], [SDS((seq, ATTN_W), BF16)], [],
        [q, k, k, v, v, bias, sink_rows], exchange, _params(1))


def _attn_bwd(q, k, v, d_out, bias, sink_rows, exchange=None):
    seq = q.shape[0]
    nblk = seq // BLOCK

    def body(q_ref, kp_ref, kc_ref, vp_ref, vc_ref, do_ref, b_ref, s_ref,
             dq_ref, dk_ref, dv_ref, db_ref, ds_ref, ck_ref, cv_ref):
        n = pl.program_id(0)

        @pl.when(n == 0)
        def _():
            db_ref[...] = jnp.zeros_like(db_ref)
            ds_ref[...] = jnp.zeros_like(ds_ref)
            ck_ref[...] = jnp.zeros_like(ck_ref)
            cv_ref[...] = jnp.zeros_like(cv_ref)

        @pl.when(n < nblk)
        def _():
            which = jnp.minimum(n, 1)
            scale = HEAD_DIM ** -0.5
            kcat = jnp.concatenate([kp_ref[...], kc_ref[...]], axis=0)
            kms = _halves(kcat)
            kts = _halves_t(kcat)
            vms = _halves(jnp.concatenate([vp_ref[...], vc_ref[...]], axis=0))
            dks, dvs = [], []
            for kh in range(N_KV):
                qk = _tile_rows(q_ref, kh)
                dok = _tile_rows(do_ref, kh)
                dq = jnp.zeros((TILE, 2 * BLOCK), F32)
                dkp, dvp = [], []
                for par in range(2):
                    pr, ps = _attn_probs(kms[kh][par], qk, b_ref[which, kh, par], s_ref[kh, par])
                    dp = _mm_nt(vms[kh][par], dok)
                    rs = jnp.sum(pr * dp, axis=0, keepdims=True)
                    dlg = pr * (dp - rs)
                    ds_ref[kh, par] += -ps * rs
                    db_ref[kh, par] += dlg
                    dlb = _bf(dlg)
                    dq = dq + _mm(kts[kh][par], dlb)
                    dkp.append(_mm(dlb, qk))
                    dvp.append(_mm(_bf(pr), dok))
                dq = _bf((dq * scale).T)
                dq_ref[:, (2 * kh) * TILE:(2 * kh + 1) * TILE] = dq[:BLOCK]
                dq_ref[:, (2 * kh + 1) * TILE:(2 * kh + 2) * TILE] = dq[BLOCK:]
                dks.append(_fold_halves(*dkp))
                dvs.append(_fold_halves(*dvp))
            low = lax.broadcasted_iota(jnp.int32, (2 * BLOCK, TILE), 1) < HEAD_DIM
            dkk = jnp.where(low, dks[0], dks[1]) * scale
            dvv = jnp.where(low, dvs[0], dvs[1])
            dk_ref[...] = _bf(ck_ref[...] + dkk[:BLOCK])
            ck_ref[...] = dkk[BLOCK:]
            dv_ref[...] = _bf(cv_ref[...] + dvv[:BLOCK])
            cv_ref[...] = dvv[BLOCK:]

        @pl.when(n == nblk)
        def _():
            dk_ref[...] = _bf(ck_ref[...])
            dv_ref[...] = _bf(cv_ref[...])

    cur = lambda n: (jnp.minimum(n, nblk - 1), 0)
    prev = lambda n: (jnp.maximum(jnp.minimum(n, nblk - 1) - 1, 0), 0)
    late = lambda n: (jnp.maximum(n - 1, 0), 0)
    kv_spec = lambda m: pl.BlockSpec((BLOCK, KV_W), m)
    acc_b = pl.BlockSpec(bias.shape[1:], functools.partial(_zero_map, bias.ndim - 1))
    acc_s = pl.BlockSpec(sink_rows.shape, functools.partial(_zero_map, sink_rows.ndim))
    return _fused_call(
        "attn_bwd", body, (nblk + 1,),
        [pl.BlockSpec((BLOCK, ATTN_W), cur), kv_spec(prev), kv_spec(cur), kv_spec(prev), kv_spec(cur),
         pl.BlockSpec((BLOCK, ATTN_W), cur),
         pl.BlockSpec(bias.shape, functools.partial(_zero_map, bias.ndim)), acc_s],
        [pl.BlockSpec((BLOCK, ATTN_W), cur), kv_spec(late), kv_spec(late), acc_b, acc_s],
        [SDS((seq, ATTN_W), BF16), SDS((seq, KV_W), BF16), SDS((seq, KV_W), BF16),
         SDS(bias.shape[1:], F32), SDS(sink_rows.shape, F32)],
        [pltpu.VMEM((BLOCK, KV_W), F32), pltpu.VMEM((BLOCK, KV_W), F32)],
        [q, k, k, v, v, d_out, bias, sink_rows], exchange, _params(1))


def _ssm_discretize(lam_re, lam_im, log_dt, b_re, b_im):
    dt = jnp.exp(log_dt)[:, None]
    mag = jnp.exp(lam_re * dt)
    ab_re = mag * jnp.cos(lam_im * dt)
    ab_im = mag * jnp.sin(lam_im * dt)
    nr = ab_re - 1.0
    den = lam_re * lam_re + lam_im * lam_im
    f_re = (nr * lam_re + ab_im * lam_im) / den
    f_im = (ab_im * lam_re - nr * lam_im) / den
    bb_re = f_re[..., None] * b_re - f_im[..., None] * b_im
    bb_im = f_re[..., None] * b_im + f_im[..., None] * b_re
    return ab_re, ab_im, bb_re, bb_im


def _state_layout(re, im):
    lead = re.shape[:-2]
    z = jnp.stack([re, im], axis=-3).reshape(lead + (2, N_SUPER, GROUPS_PER_SUPER, SSM_STATE))
    return jnp.moveaxis(z, -4, -3).reshape(lead + (STATE_COLS,))


def _state_unlayout(vec):
    z = vec.reshape(N_SUPER, 2, GROUPS_PER_SUPER, SSM_STATE).transpose(1, 0, 2, 3)
    z = z.reshape(2, SSM_GROUPS, SSM_STATE)
    return z[0], z[1]


SEG = 4
WINDOW = SEG * SUBLANES


def _scan_tables(ab_re, ab_im):
    pw = [None, (ab_re, ab_im)]
    for _ in range(2, WINDOW + 1):
        pr, pi_ = pw[-1]
        pw.append((pr * ab_re - pi_ * ab_im, pr * ab_im + pi_ * ab_re))
    fwd = np.zeros((7, SUBLANES), np.int64)
    bwd = np.zeros((7, SUBLANES), np.int64)
    for k, shift in enumerate((1, 2, 4)):
        fwd[k] = [SEG * shift if r >= shift else 0 for r in range(SUBLANES)]
        bwd[k] = [SEG * shift if r < SUBLANES - shift else 0 for r in range(SUBLANES)]
    fwd[3] = [SEG * (r + 1) for r in range(SUBLANES)]
    bwd[3] = [SEG * (SUBLANES - r) for r in range(SUBLANES)]
    for k in range(1, SEG):
        fwd[3 + k] = bwd[3 + k] = k
    used = sorted((set(fwd.ravel()) | set(bwd.ravel())) - {0})
    select = lambda which: np.stack([(which == p) for p in used], axis=-1).astype(np.float32)
    stacked = _state_layout(jnp.stack([pw[p][0] for p in used]), jnp.stack([pw[p][1] for p in used]))
    conj_sign = np.where((np.arange(STATE_COLS) // SUPER_HALF) % 2 == 1, -1.0, 1.0).astype(np.float32)
    pick = functools.partial(jnp.einsum, 'krp,pc->krc', precision=lax.Precision.HIGHEST)
    return pick(select(fwd), stacked), pick(select(bwd), stacked) * conj_sign


_EYE = np.eye(GROUPS_PER_SUPER, dtype=np.float32)


def _b_matrix(bb_re, bb_im):
    bb = jnp.stack([bb_re, bb_im]).reshape(2, N_SUPER, GROUPS_PER_SUPER, SSM_STATE, SSM_GROUP)
    m = jnp.einsum('rsgpc,gh->sgcrhp', bb, _EYE)
    return m.reshape(N_SUPER, SUPER_IN, SUPER_W)


def _b_matrix_grad(dm):
    d = dm.reshape(N_SUPER, GROUPS_PER_SUPER, SSM_GROUP, 2, GROUPS_PER_SUPER, SSM_STATE)
    d = jnp.sum(d * _EYE[None, :, None, None, :, None], axis=4)
    d = d.transpose(3, 0, 1, 4, 2).reshape(2, SSM_GROUPS, SSM_STATE, SSM_GROUP)
    return d[0], d[1]


def _c_matrix(c_re, c_im):
    cc = jnp.stack([c_re, -c_im]).reshape(2, N_SUPER, GROUPS_PER_SUPER, SSM_GROUP, SSM_STATE)
    m = jnp.einsum('rsgcp,gh->srgphc', cc, _EYE)
    return m.reshape(N_SUPER, SUPER_W, SUPER_IN)


def _c_matrix_grad(dm):
    d = dm.reshape(N_SUPER, 2, GROUPS_PER_SUPER, SSM_STATE, GROUPS_PER_SUPER, SSM_GROUP)
    d = jnp.sum(d * _EYE[None, None, :, None, :, None], axis=4)
    d = d.transpose(1, 0, 2, 4, 3).reshape(2, SSM_GROUPS, SSM_GROUP, SSM_STATE)
    return d[0], -d[1]


def _cmul_add(xr, xi, ar, ai, sr, si):
    return xr + ar * sr - ai * si, xi + ar * si + ai * sr


def _scan_rows(buf_ref, tab_ref, carry_ref, n_windows, reverse, h_ref=None, da_ref=None):
    order = list(range(SEG - 1, -1, -1)) if reverse else list(range(SEG))
    near = SUBLANES - 1 if reverse else 0
    far = 0 if reverse else SUBLANES - 1
    s_in = SUBLANES - 1 if reverse else 1
    lanes = lambda tile: pl.ds(tile * LANES, LANES)

    def window(w0, tile_re, tile_im, c_re, c_im, acc):
        rows = lambda t: pl.ds(w0 + t, SUBLANES, stride=SEG)
        get = lambda ref, t: (ref.at[tile_re][rows(t), :], ref.at[tile_im][rows(t), :])
        tab = lambda k: (tab_ref[k, :, lanes(tile_re)], tab_ref[k, :, lanes(tile_im)])

        def put(t, xr, xi):
            buf_ref.at[tile_re][rows(t), :] = xr
            buf_ref.at[tile_im][rows(t), :] = xi

        a1 = tab(4)
        er, ei = get(buf_ref, order[0])
        for t in order[1:]:
            er, ei = _cmul_add(*get(buf_ref, t), *a1, er, ei)
            if t != order[-1]:
                put(t, er, ei)
        for k, shift in enumerate((1, 2, 4)):
            s = (SUBLANES - shift) if reverse else shift
            er, ei = _cmul_add(er, ei, *tab(k), pltpu.roll(er, s, 0), pltpu.roll(ei, s, 0))
        er, ei = _cmul_add(er, ei, *tab(3), c_re, c_im)
        put(order[-1], er, ei)
        sub = lax.broadcasted_iota(jnp.int32, er.shape, 0)
        in_re = jnp.where(sub == near, c_re, pltpu.roll(er, s_in, 0))
        in_im = jnp.where(sub == near, c_im, pltpu.roll(ei, s_in, 0))
        true = {order[-1]: (er, ei)}
        for idx, t in enumerate(order[:-1]):
            true[t] = _cmul_add(*get(buf_ref, t), *tab(4 + idx), in_re, in_im)
            put(t, *true[t])
        carry = (jnp.broadcast_to(er[far:far + 1], er.shape), jnp.broadcast_to(ei[far:far + 1], ei.shape))
        if acc is None:
            return carry, None
        acc_re, acc_im = acc
        for t in range(SEG):
            if t + 1 < SEG:
                gr, gim = true[t + 1]
            else:
                gr = jnp.where(sub == SUBLANES - 1, c_re, pltpu.roll(true[0][0], SUBLANES - 1, 0))
                gim = jnp.where(sub == SUBLANES - 1, c_im, pltpu.roll(true[0][1], SUBLANES - 1, 0))
            hr, hi = get(h_ref, t)
            acc_re = acc_re + gr * hr + gim * hi
            acc_im = acc_im + gim * hr - gr * hi
        return carry, (acc_re, acc_im)

    half = SUPER_HALF // LANES
    per = 2 if h_ref is None else 4
    for sb in range(N_SUPER):
        pairs = [(2 * half * sb + j, 2 * half * sb + half + j) for j in range(half)]

        def step(wi, state, pairs=pairs):
            w = (n_windows - 1 - wi) if reverse else wi
            w0 = pl.multiple_of(w * WINDOW, WINDOW)
            out = []
            for j, (tile_re, tile_im) in enumerate(pairs):
                mine = state[per * j:per * (j + 1)]
                carry, acc = window(w0, tile_re, tile_im, mine[0], mine[1], mine[2:] or None)
                out += list(carry) + list(acc or ())
            return tuple(out)

        init = []
        for tile_re, tile_im in pairs:
            init += [carry_ref[:, lanes(tile_re)], carry_ref[:, lanes(tile_im)]]
            if h_ref is not None:
                init += [da_ref[:, lanes(tile_re)], da_ref[:, lanes(tile_im)]]
        fin = lax.fori_loop(0, n_windows, step, tuple(init))
        for j, (tile_re, tile_im) in enumerate(pairs):
            carry_ref[:, lanes(tile_re)] = fin[per * j]
            carry_ref[:, lanes(tile_im)] = fin[per * j + 1]
            if h_ref is not None:
                da_ref[:, lanes(tile_re)] = fin[per * j + 2]
                da_ref[:, lanes(tile_im)] = fin[per * j + 3]


def _put_tiles(ref, sb, block):
    for j in range(SUPER_TILES):
        ref[sb * SUPER_TILES + j] = block[:, j * LANES:(j + 1) * LANES]


def _get_tiles(ref, sb):
    return jnp.concatenate([ref[sb * SUPER_TILES + j] for j in range(SUPER_TILES)], axis=1)


def _ssm_fwd(u, bmat, cmat, tab, d_skip, tb, exchange=None):
    seq = u.shape[0]

    def body(u_ref, b_ref, c_ref, t_ref, d_ref, s_ref, h_ref, carry_ref):
        @pl.when(pl.program_id(0) == 0)
        def _():
            carry_ref[...] = jnp.zeros_like(carry_ref)

        u_blk = u_ref[...]
        ub = _bf(u_blk)
        for sb in range(N_SUPER):
            _put_tiles(h_ref, sb, _mm(ub[:, sb * SUPER_IN:(sb + 1) * SUPER_IN], b_ref[sb]))
        _scan_rows(h_ref, t_ref, carry_ref, tb // WINDOW, False)
        ys = [_mm(_bf(_get_tiles(h_ref, sb)), c_ref[sb]) for sb in range(N_SUPER)]
        s_ref[...] = jnp.concatenate(ys, axis=1) + d_ref[...] * u_blk

    return _rowcall("ssm_fwd", body, seq, tb, [u], [bmat, cmat, tab, d_skip],
                    [(SSM_W, F32), ((STATE_TILES, LANES), F32)], [],
                    scratch=[pltpu.VMEM((SUBLANES, STATE_COLS), F32)], vmem=VMEM_BIG, exchange=exchange)


def _ssm_bwd(ds, u, h, bmat_t, cmat_t, tab, d_skip, tb, exchange=None):
    seq = u.shape[0]

    def body(ds_ref, u_ref, h_ref, bt_ref, ct_ref, t_ref, d_ref,
             du_ref, db_ref, dc_ref, da_ref, dd_ref, g_ref, carry_ref):
        @pl.when(pl.program_id(0) == 0)
        def _():
            carry_ref[...] = jnp.zeros_like(carry_ref)
            db_ref[...] = jnp.zeros_like(db_ref)
            dc_ref[...] = jnp.zeros_like(dc_ref)
            da_ref[...] = jnp.zeros_like(da_ref)
            dd_ref[...] = jnp.zeros_like(dd_ref)

        ds_blk = ds_ref[...]
        dsb = _bf(ds_blk)
        u_blk = u_ref[...]
        ub = _bf(u_blk)
        for sb in range(N_SUPER):
            _put_tiles(g_ref, sb, _mm(dsb[:, sb * SUPER_IN:(sb + 1) * SUPER_IN], ct_ref[sb]))
        _scan_rows(g_ref, t_ref, carry_ref, tb // WINDOW, True, h_ref=h_ref, da_ref=da_ref)
        dus = []
        for sb in range(N_SUPER):
            gb = _bf(_get_tiles(g_ref, sb))
            dus.append(_mm(gb, bt_ref[sb]))
            db_ref[sb] += _mm_tn(ub[:, sb * SUPER_IN:(sb + 1) * SUPER_IN], gb)
            dc_ref[sb] += _mm_tn(_bf(_get_tiles(h_ref, sb)), dsb[:, sb * SUPER_IN:(sb + 1) * SUPER_IN])
        du_ref[...] = jnp.concatenate(dus, axis=1) + d_ref[...] * ds_blk
        dd_ref[...] += jnp.sum(ds_blk * u_blk, axis=0, keepdims=True)

    return _rowcall("ssm_bwd", body, seq, tb, [ds, u, h], [bmat_t, cmat_t, tab, d_skip],
                    [(SSM_W, F32)],
                    [((N_SUPER, SUPER_IN, SUPER_W), F32), ((N_SUPER, SUPER_W, SUPER_IN), F32),
                     ((SUBLANES, STATE_COLS), F32), ((1, SSM_W), F32)],
                    scratch=[pltpu.VMEM((STATE_TILES, tb, LANES), F32), pltpu.VMEM((SUBLANES, STATE_COLS), F32)],
                    reverse=True, vmem=VMEM_BIG, exchange=exchange)


def _merge_core(s, attb, ga, gs, wg_ref, wab_ref, wsb_ref, wout_ref):
    zg, dgelu = _gelu_and_grad(s)
    zgb = _bf(zg)
    sg = _sig(_mm(zgb, wg_ref[...]))
    z = zg * sg
    zb = _bf(z)
    ys = jnp.concatenate([_mm(zb, wsb_ref[j]) for j in range(N_CHIPS)], axis=1)
    ya = jnp.concatenate([_mm(attb, wab_ref[j]) for j in range(N_CHIPS)], axis=1)
    sa = _sig(ga)
    ss = _sig(gs)
    mgb = _bf(sa * ya + ss * ys)
    o = _mm(mgb, wout_ref[...])
    return dict(zg=zg, dgelu=dgelu, zgb=zgb, sg=sg, zb=zb, ys=ys, ya=ya, sa=sa, ss=ss, mgb=mgb, o=o)


def _merge_fwd(x, s, att, ga, gs, g2, w_glu, w_ab, w_sb, w_out, tb, exchange=None):
    seq = x.shape[0]

    def body(x_ref, s_ref, att_ref, ga_ref, gs_ref, g_ref, wg_ref, wab_ref, wsb_ref, wout_ref, x2_ref):
        f = _merge_core(s_ref[...], att_ref[...], ga_ref[...], gs_ref[...], wg_ref, wab_ref, wsb_ref, wout_ref)
        n, _, _ = _rms(f["o"], g_ref[...])
        x2_ref[...] = x_ref[...] + n

    return _rowcall("merge_fwd", body, seq, tb, [x, s, att, ga, gs], [g2, w_glu, w_ab, w_sb, w_out],
                    [(D_MODEL, F32)], [], vmem=VMEM_BIG, exchange=exchange)[0]


def _merge_bwd(dx2, s, att, ga, gs, g2, w_glu, w_ab, w_sb, w_out, tb, exchange=None):
    seq = s.shape[0]
    cw = D_MODEL // N_CHIPS
    last = seq // tb - 1

    def body(dx2_ref, s_ref, att_ref, ga_ref, gs_ref, g_ref, wg_ref, wab_ref, wsb_ref, wout_ref,
             ds_ref, datt_ref, dga_ref, dgs_ref, dg_ref, dwg_ref, dwab_ref, dwsb_ref, dwout_ref,
             bwg_ref, bwab_ref, bwsb_ref, bwout_ref):
        @pl.when(pl.program_id(0) == 0)
        def _():
            for r in (dg_ref, dwg_ref, dwab_ref, dwsb_ref, dwout_ref):
                r[...] = jnp.zeros_like(r)

        attb = att_ref[...]
        f = _merge_core(s_ref[...], attb, ga_ref[...], gs_ref[...], wg_ref, wab_ref, wsb_ref, wout_ref)
        g = g_ref[...]
        _, oh, r2 = _rms(f["o"], g)
        do, dg = _rms_bwd(dx2_ref[...], oh, r2, g)
        dg_ref[...] += dg
        dob = _bf(do)
        dwout_ref[...] += _mm_tn(f["mgb"], dob)
        dmg = _mm_nt(dob, wout_ref[...])
        sa, ss = f["sa"], f["ss"]
        dyab = _bf(dmg * sa)
        dysb = _bf(dmg * ss)
        dga_ref[...] = _bf(dmg * f["ya"] * sa * (1.0 - sa))
        dgs_ref[...] = _bf(dmg * f["ys"] * ss * (1.0 - ss))
        dwab = _mm_tn(attb, dyab)
        dwsb = _mm_tn(f["zb"], dysb)
        datt = jnp.zeros((tb, ATTN_W), F32)
        dz = jnp.zeros((tb, SSM_W), F32)
        for j in range(N_CHIPS):
            dwab_ref[j] += dwab[:, j * cw:(j + 1) * cw]
            dwsb_ref[j] += dwsb[:, j * cw:(j + 1) * cw]
            datt = datt + _mm_nt(dyab[:, j * cw:(j + 1) * cw], wab_ref[j])
            dz = dz + _mm_nt(dysb[:, j * cw:(j + 1) * cw], wsb_ref[j])
        datt_ref[...] = _bf(datt)
        sg, zg = f["sg"], f["zg"]
        dglb = _bf(dz * zg * sg * (1.0 - sg))
        dwg_ref[...] += _mm_tn(f["zgb"], dglb)
        dzg = dz * sg + _mm_nt(dglb, wg_ref[...])
        ds_ref[...] = dzg * f["dgelu"]

        @pl.when(pl.program_id(0) == last)
        def _():
            for dst, src in ((bwg_ref, dwg_ref), (bwab_ref, dwab_ref), (bwsb_ref, dwsb_ref), (bwout_ref, dwout_ref)):
                dst[...] = _bf(src[...])

    shapes = [w_glu.shape, w_ab.shape, w_sb.shape, w_out.shape]
    return _rowcall("merge_bwd", body, seq, tb, [dx2, s, att, ga, gs], [g2, w_glu, w_ab, w_sb, w_out],
                    [(SSM_W, F32), (ATTN_W, BF16), (D_MODEL, BF16), (D_MODEL, BF16)],
                    [((1, D_MODEL), F32)] + [(sh, F32) for sh in shapes] + [(sh, BF16) for sh in shapes],
                    vmem=VMEM_BIG, exchange=exchange)


def _mlp_fwd_loss(x2, target, g3, g4, w_ffi, w_ffo, tb):
    seq = x2.shape[0]
    n_slab = len(w_ffi)
    sw = D_FF // FF_CHUNKS // n_slab

    def body(x2_ref, t_ref, g3_ref, g4_ref, *rest):
        wi_refs, (wo_ref, dy_ref, df_ref, h_ref, ra_ref, loss_ref, dg_ref) = rest[:n_slab], rest[n_slab:]

        @pl.when(pl.program_id(0) == 0)
        def _():
            loss_ref[...] = jnp.zeros_like(loss_ref)
            dg_ref[...] = jnp.zeros_like(dg_ref)

        x2_blk = x2_ref[...]
        h3, _, _ = _rms(x2_blk, g3_ref[...])
        hb = _bf(h3)
        h_ref[...] = hb
        f = jnp.zeros((tb, D_MODEL), F32)
        for j in range(FF_CHUNKS):
            for k in range(n_slab):
                ra = jnp.maximum(_mm(hb, wi_refs[k][j]), 0.0)
                ra_ref[:, pl.ds((j * n_slab + k) * sw, sw)] = _bf(ra)
                f = f + _mm(_bf(ra * ra), wo_ref[j, pl.ds(k * sw, sw), :])
        g4 = g4_ref[...]
        n4, fh, r4 = _rms(f, g4)
        e = (x2_blk + n4) - t_ref[...]
        loss_ref[...] += 0.5 * jnp.sum(jnp.mean(e * e, axis=-1, keepdims=True))
        dy = e * (1.0 / D_MODEL)
        dy_ref[...] = dy
        df, dg = _rms_bwd(dy, fh, r4, g4)
        df_ref[...] = _bf(df)
        dg_ref[...] += dg

    return _rowcall("mlp_fwd_loss", body, seq, tb, [x2, target], [g3, g4, *w_ffi, w_ffo],
                    [(D_MODEL, F32), (D_MODEL, BF16), (D_MODEL, BF16), (D_FF, BF16)],
                    [((SUBLANES, 128), F32), ((1, D_MODEL), F32)], vmem=VMEM_BIG)


def _mlp_bwd(x2, dy, df, ra, g3, w_ffi, w_ffo, tb):
    seq = x2.shape[0]
    n_slab = len(w_ffi)
    sw = D_FF // FF_CHUNKS // n_slab

    def body(x2_ref, dy_ref, df_ref, ra_ref, g3_ref, *rest):
        wi_refs, (wo_ref, dx_ref, da_ref, dg_ref) = rest[:n_slab], rest[n_slab:]

        @pl.when(pl.program_id(0) == 0)
        def _():
            dg_ref[...] = jnp.zeros_like(dg_ref)

        dfb = df_ref[...]
        dh = jnp.zeros((tb, D_MODEL), F32)
        for j in range(FF_CHUNKS):
            for k in range(n_slab):
                cols = pl.ds((j * n_slab + k) * sw, sw)
                ra = ra_ref[:, cols].astype(F32)
                dab = _bf(_mm_nt(dfb, wo_ref[j, pl.ds(k * sw, sw), :]) * (2.0 * ra))
                da_ref[:, cols] = dab
                dh = dh + _mm_nt(dab, wi_refs[k][j])
        g3 = g3_ref[...]
        _, xh, r3 = _rms(x2_ref[...], g3)
        dxn, dg = _rms_bwd(dh, xh, r3, g3)
        dx_ref[...] = dy_ref[...] + dxn
        dg_ref[...] += dg

    return _rowcall("mlp_bwd", body, seq, tb, [x2, dy, df, ra], [g3, *w_ffi, w_ffo],
                    [(D_MODEL, F32), (D_FF, BF16)], [((1, D_MODEL), F32)], vmem=VMEM_BIG)


def _matmul_tn(name, a, b, tk, tn, tl, chunk_major, exchange=None, square_a=False):
    seq, kdim = a.shape
    ndim = b.shape[1]
    last = seq // tl - 1

    def body(a_ref, b_ref, o_ref, ob_ref):
        @pl.when(pl.program_id(2) == 0)
        def _():
            o_ref[...] = jnp.zeros_like(o_ref)

        a_blk = a_ref[...]
        if square_a:
            a_blk = _bf(jnp.square(a_blk.astype(F32)))
        o_ref[...] += _mm_tn(a_blk, b_ref[...])

        @pl.when(pl.program_id(2) == last)
        def _():
            ob_ref[...] = _bf(o_ref[...])

    if chunk_major:
        shape = (ndim // tn, kdim, tn)
        out_spec = pl.BlockSpec((None, tk, tn), lambda k, n, l: (n, k, 0))
    else:
        shape = (kdim, ndim)
        out_spec = pl.BlockSpec((tk, tn), lambda k, n, l: (k, n))
    return _fused_call(
        name, body, (kdim // tk, ndim // tn, seq // tl),
        [pl.BlockSpec((tl, tk), lambda k, n, l: (l, k)), pl.BlockSpec((tl, tn), lambda k, n, l: (l, n))],
        [out_spec, out_spec], [SDS(shape, F32), SDS(shape, BF16)], [], [a, b], exchange, _params(3, VMEM_BIG))


def _ew_call(name, fn, ins, n_out, after=None):
    rows, cols = ins[0].shape
    tr = rows
    while tr * cols * 4 > min(1 << 20, (9 << 20) // (len(ins) + n_out)) and tr % 16 == 0:
        tr //= 2
    spec = pl.BlockSpec((tr, cols), lambda i: (i, 0))
    extra = [] if after is None else [after]

    def body(*refs):
        outs = fn(*[r[...] for r in refs[:len(ins)]])
        for r, o in zip(refs[len(ins) + len(extra):], outs):
            r[...] = o

    return pl.pallas_call(
        body, grid=(rows // tr,), in_specs=[spec] * len(ins) + [ANY] * len(extra), out_specs=[spec] * n_out,
        out_shape=[SDS((rows, cols), F32)] * n_out, name=name, compiler_params=_params(1))(*ins, *extra)


def _adam_math(w, g, m, v):
    m2 = ADAM_B1 * m + (1.0 - ADAM_B1) * g
    v2 = ADAM_B2 * v + (1.0 - ADAM_B2) * (g * g)
    m_hat = m2 / (1.0 - ADAM_B1 ** ADAM_STEP)
    v_hat = v2 / (1.0 - ADAM_B2 ** ADAM_STEP)
    delta = -ADAM_LR * (m_hat / (jnp.sqrt(v_hat) + ADAM_EPS) + ADAM_WD * w)
    return delta, m2, v2


def _sum4(name, own, recv, idx):
    _, rows, cols = own.shape
    tr = rows
    while tr * cols * 4 > (1 << 20) and tr % 16 == 0:
        tr //= 2

    def body(idx_ref, o_ref, r0_ref, r1_ref, r2_ref, out_ref):
        out_ref[...] = ((o_ref[...] + r0_ref[...].astype(F32)) + r1_ref[...].astype(F32)) + r2_ref[...].astype(F32)

    blk = (None, tr, cols)
    grid_spec = pltpu.PrefetchScalarGridSpec(
        num_scalar_prefetch=1, grid=(rows // tr,),
        in_specs=[pl.BlockSpec(blk, lambda i, s: (s[0], i, 0)), pl.BlockSpec(blk, lambda i, s: (0, i, 0)),
                  pl.BlockSpec(blk, lambda i, s: (1, i, 0)), pl.BlockSpec(blk, lambda i, s: (2, i, 0))],
        out_specs=pl.BlockSpec((tr, cols), lambda i, s: (i, 0)))
    return pl.pallas_call(body, grid_spec=grid_spec, out_shape=SDS((rows, cols), F32), name=name,
                          compiler_params=_params(1))(jnp.reshape(idx, (1,)).astype(jnp.int32), own, recv, recv, recv)


def _sum4_group(name, owns, recvs, idx):
    n = len(owns)

    def body(idx_ref, *refs):
        for a in range(n):
            o_ref, r0_ref, r1_ref, r2_ref = refs[4 * a:4 * a + 4]
            refs[4 * n + a][...] = (((o_ref[...] + r0_ref[...].astype(F32)) + r1_ref[...].astype(F32))
                                    + r2_ref[...].astype(F32))

    in_specs, args = [], []
    for own, recv in zip(owns, recvs):
        blk = (None,) + own.shape[1:]
        in_specs += [pl.BlockSpec(blk, lambda i, s: (s[0], 0, 0))]
        in_specs += [pl.BlockSpec(blk, lambda i, s, k=k: (k, 0, 0)) for k in range(3)]
        args += [own, recv, recv, recv]
    grid_spec = pltpu.PrefetchScalarGridSpec(
        num_scalar_prefetch=1, grid=(1,), in_specs=in_specs,
        out_specs=[pl.BlockSpec(o.shape[1:], lambda i, s: (0, 0)) for o in owns])
    return pl.pallas_call(body, grid_spec=grid_spec, out_shape=[SDS(o.shape[1:], F32) for o in owns], name=name,
                          compiler_params=_params(1, VMEM_BIG))(jnp.reshape(idx, (1,)).astype(jnp.int32), *args)


def _adam_group(name, items, after):
    n = len(items)
    flat = [t for item in items for t in item]

    def body(*refs):
        outs = refs[5 * n + 1:]
        for a in range(n):
            w_, p, q, m_, v_ = [r[...] for r in refs[5 * a:5 * a + 5]]
            g = p + q
            for r, o in zip(outs[4 * a:4 * a + 4], (g,) + _adam_math(w_, g, m_, v_)):
                r[...] = o

    whole = lambda t: pl.BlockSpec(t.shape, lambda i: (0, 0))
    outs = pl.pallas_call(
        body, grid=(1,), in_specs=[whole(t) for t in flat] + [ANY],
        out_specs=[whole(item[0]) for item in items for _ in range(4)],
        out_shape=[SDS(item[0].shape, F32) for item in items for _ in range(4)], name=name,
        compiler_params=_params(1, VMEM_BIG))(*flat, after)
    return [outs[4 * a:4 * a + 4] for a in range(n)]


def _sum4_swap(name, own, recv, idx, n_blocks=4):
    _, rows, cols = own.shape
    tr = rows // n_blocks
    assert tr * n_blocks == rows and tr % SUBLANES == 0

    def body(idx_ref, o_ref, r0_ref, r1_ref, r2_ref, mine_ref, theirs_ref, buf, kept, sent, arrived):
        i = pl.program_id(0)
        slot = lax.rem(i, 2)
        x, y, c = _place()

        def copies(j, s):
            block = pl.ds(j * tr, tr)
            return (pltpu.make_async_copy(buf.at[s], mine_ref.at[block], kept.at[s]),
                    pltpu.make_async_remote_copy(
                        src_ref=buf.at[s], dst_ref=theirs_ref.at[block], send_sem=sent.at[s], recv_sem=arrived.at[j],
                        device_id=(x, y, 1 - c), device_id_type=MESH_ID))

        def finish(j, s):
            keep, send = copies(j, s)
            keep.wait()
            send.wait_send()
            send.wait_recv()

        @pl.when(i >= 2)
        def _():
            finish(i - 2, slot)

        buf[slot] = ((o_ref[...] + r0_ref[...].astype(F32)) + r1_ref[...].astype(F32)) + r2_ref[...].astype(F32)
        for cp in copies(i, slot):
            cp.start()

        @pl.when(i == n_blocks - 1)
        def _():
            if n_blocks > 1:
                finish(i - 1, 1 - slot)
            finish(i, slot)

    blk = (None, tr, cols)
    grid_spec = pltpu.PrefetchScalarGridSpec(
        num_scalar_prefetch=1, grid=(n_blocks,),
        in_specs=[pl.BlockSpec(blk, lambda i, s: (s[0], i, 0)), pl.BlockSpec(blk, lambda i, s: (0, i, 0)),
                  pl.BlockSpec(blk, lambda i, s: (1, i, 0)), pl.BlockSpec(blk, lambda i, s: (2, i, 0))],
        out_specs=[HBM, HBM],
        scratch_shapes=[pltpu.VMEM((2, tr, cols), F32), pltpu.SemaphoreType.DMA((2,)), pltpu.SemaphoreType.DMA((2,)),
                        pltpu.SemaphoreType.DMA((n_blocks,))])
    return pl.pallas_call(body, grid_spec=grid_spec, out_shape=[SDS((rows, cols), F32)] * 2, name=name,
                          compiler_params=_params(1))(jnp.reshape(idx, (1,)).astype(jnp.int32), own, recv, recv, recv)


def _adam_pair(name, item, after=None):
    def fn(w_, a, b, m_, v_):
        g = a + b
        return (g,) + _adam_math(w_, g, m_, v_)

    return _ew_call(name, fn, list(item), 4, after)


def _place():
    return lax.axis_index("x"), lax.axis_index("y"), lax.axis_index("c")


def _other_chips(x, y):
    return [(1 - x, y), (x, 1 - y), (1 - x, 1 - y)]


HBM = pl.BlockSpec(memory_space=pltpu.HBM)
SEM = pl.BlockSpec(memory_space=pltpu.SEMAPHORE)
DATAFLOW = pltpu.SideEffectType.DATAFLOW_SIDE_EFFECTING


class _Flight:
    def __init__(self, copies, n_copies, send, recv, srcs, lands, token):
        self.copies, self.n, self.send, self.recv = copies, n_copies, send, recv
        self.srcs, self.lands, self.token = list(srcs), list(lands), token


def _take_off(name, srcs, lands, copies, n_copies, after):
    n_s, n_l = len(srcs), len(lands)

    def body(*refs):
        src, land = refs[:n_s], refs[n_s:n_s + n_l]
        send, recv = refs[n_s + n_l + 1:n_s + n_l + 3]
        for cp in copies(src, land, send, recv):
            cp.start()
        refs[-1][...] = jnp.zeros_like(refs[-1])

    mem = lambda t: pltpu.HBM(t.shape, t.dtype)
    sems = pltpu.SemaphoreType.DMA((n_copies,))
    outs = pl.pallas_call(
        body, name=name,
        out_shape=(sems, sems, *map(mem, srcs), *map(mem, lands), SDS((SUBLANES, LANES), F32)),
        in_specs=[HBM] * (n_s + n_l) + [ANY],
        out_specs=(SEM, SEM, *[HBM] * (n_s + n_l), pl.BlockSpec(memory_space=pltpu.VMEM)),
        input_output_aliases={i: 2 + i for i in range(n_s + n_l)},
        compiler_params=pltpu.CompilerParams(has_side_effects=DATAFLOW),
    )(*[pltpu.with_memory_space_constraint(t, pltpu.HBM) for t in (*srcs, *lands)], after)
    return _Flight(copies, n_copies, outs[0], outs[1], outs[2:2 + n_s], outs[2 + n_s:2 + n_s + n_l], outs[-1])


def _land(name, flight, after):
    n_s, n_l = len(flight.srcs), len(flight.lands)

    def body(*refs):
        src, land = refs[:n_s], refs[n_s:n_s + n_l]
        send, recv = refs[n_s + n_l:n_s + n_l + 2]
        for cp in flight.copies(src, land, send, recv):
            cp.wait_send()
            cp.wait_recv()

    mem = lambda t: pltpu.HBM(t.shape, t.dtype)
    outs = pl.pallas_call(
        body, name=name, out_shape=(*map(mem, flight.srcs), *map(mem, flight.lands)),
        in_specs=[HBM] * (n_s + n_l) + [SEM, SEM, ANY], out_specs=tuple([HBM] * (n_s + n_l)),
        input_output_aliases={i: i for i in range(n_s + n_l)},
        compiler_params=pltpu.CompilerParams(has_side_effects=DATAFLOW),
    )(*flight.srcs, *flight.lands, flight.send, flight.recv, after)
    return list(outs[:n_s]), list(outs[n_s:])


def _empty_like(shapes_from, lead):
    return [lax.empty((lead,) + t.shape[1:], t.dtype) for t in shapes_from]


def _scatter_off(name, chunks, after):
    def copies(src, land, send, recv):
        x, y, c = _place()
        return [pltpu.make_async_remote_copy(
            src_ref=src[a].at[2 * px + py], dst_ref=land[a].at[k], send_sem=send.at[3 * a + k],
            recv_sem=recv.at[3 * a + k], device_id=(px, py, c), device_id_type=MESH_ID)
            for a in range(len(chunks)) for k, (px, py) in enumerate(_other_chips(x, y))]

    return _take_off(name, chunks, _empty_like(chunks, 3), copies, 3 * len(chunks), after)


def _swap_off(name, arrs, after):
    def copies(src, land, send, recv):
        x, y, c = _place()
        return [pltpu.make_async_remote_copy(
            src_ref=src[a], dst_ref=land[a], send_sem=send.at[a], recv_sem=recv.at[a],
            device_id=(x, y, 1 - c), device_id_type=MESH_ID) for a in range(len(arrs))]

    return _take_off(name, arrs, [lax.empty(t.shape, t.dtype) for t in arrs], copies, len(arrs), after)


def _devices_off(name, block, after):
    me = 4 * lax.axis_index("x") + 2 * lax.axis_index("y") + lax.axis_index("c")
    land = lax.dynamic_update_index_in_dim(lax.empty((N_DEV,) + block.shape, block.dtype), block, me, 0)

    def copies(src, land, send, recv):
        x, y, c = _place()
        mine = 4 * x + 2 * y + c
        return [pltpu.make_async_remote_copy(
            src_ref=src[0], dst_ref=land[0].at[mine], send_sem=send.at[k - 1], recv_sem=recv.at[k - 1],
            device_id=(x ^ (k >> 2), y ^ ((k >> 1) & 1), c ^ (k & 1)), device_id_type=MESH_ID)
            for k in range(1, N_DEV)]

    return _take_off(name, [block], [land], copies, N_DEV - 1, after)


def _half_rows(shape, c, other=False):
    half = shape[0] // 2
    return pl.ds(((1 - c) if other else c) * half, half)


def _gather_start(name, shards, lands, after):
    n = len(shards)

    def body(*refs):
        src, land, (send, recv) = refs[:n], refs[n:2 * n], refs[2 * n + 1:2 * n + 3]
        x, y, c = _place()
        me = 2 * x + y
        for a in range(n):
            mine = _half_rows(shards[a].shape, c)
            for j, (px, py) in enumerate(_other_chips(x, y)):
                pltpu.make_async_remote_copy(
                    src_ref=src[a].at[mine], dst_ref=land[a].at[me, mine], send_sem=send.at[3 * a + j],
                    recv_sem=recv.at[3 * a + j], device_id=(px, py, c), device_id_type=MESH_ID).start()
        token = refs[-1]
        token[...] = jnp.zeros_like(token)

    mem = lambda t: pltpu.HBM(t.shape, t.dtype)
    pair = pltpu.SemaphoreType.DMA((3 * n,))
    outs = pl.pallas_call(
        body, name=name,
        out_shape=(pair, pair, *map(mem, shards), *map(mem, lands), SDS((SUBLANES, LANES), F32)),
        in_specs=[HBM] * (2 * n) + [ANY],
        out_specs=(SEM, SEM, *[HBM] * (2 * n), pl.BlockSpec(memory_space=pltpu.VMEM)),
        input_output_aliases={i: 2 + i for i in range(2 * n)},
        compiler_params=pltpu.CompilerParams(has_side_effects=DATAFLOW),
    )(*[pltpu.with_memory_space_constraint(t, pltpu.HBM) for t in (*shards, *lands)], after)
    return outs[0], outs[1], list(outs[2:2 + n]), list(outs[2 + n:2 + 2 * n]), outs[-1]


def _gather_pass(name, send, recv, shards, lands, after, first=0):
    n = len(shards)

    def body(*refs):
        src, land, (send, recv, _) = refs[:n], refs[n:2 * n], refs[2 * n:2 * n + 3]
        fsend, frecv = refs[2 * n + 3], refs[2 * n + 4]
        x, y, c = _place()
        me = 2 * x + y
        for a in range(n):
            mine = _half_rows(shards[a].shape, c)
            for j, (px, py) in enumerate(_other_chips(x, y)):
                far = 2 * px + py
                ici = pltpu.make_async_remote_copy(
                    src_ref=src[a].at[mine], dst_ref=land[a].at[far, mine], send_sem=send.at[3 * (first + a) + j],
                    recv_sem=recv.at[3 * (first + a) + j], device_id=(px, py, c), device_id_type=MESH_ID)
                ici.wait_recv()
                ici.wait_send()
                pltpu.make_async_remote_copy(
                    src_ref=land[a].at[far, mine], dst_ref=land[a].at[far, mine], send_sem=fsend.at[3 * a + j],
                    recv_sem=frecv.at[3 * a + j], device_id=(x, y, 1 - c), device_id_type=MESH_ID).start()
        token = refs[-1]
        token[...] = jnp.zeros_like(token)

    mem = lambda t: pltpu.HBM(t.shape, t.dtype)
    pair = pltpu.SemaphoreType.DMA((3 * n,))
    outs = pl.pallas_call(
        body, name=name,
        out_shape=(pair, pair, *map(mem, lands), SDS((SUBLANES, LANES), F32)),
        in_specs=[HBM] * (2 * n) + [SEM, SEM, ANY],
        out_specs=(SEM, SEM, *[HBM] * n, pl.BlockSpec(memory_space=pltpu.VMEM)),
        input_output_aliases={n + i: 2 + i for i in range(n)},
        compiler_params=pltpu.CompilerParams(has_side_effects=DATAFLOW),
    )(*shards, *lands, send, recv, after)
    return outs[0], outs[1], list(outs[2:2 + n]), outs[-1]


def _gather_wait(name, fsend, frecv, lands, after):
    n = len(lands)

    def body(*refs):
        land, (fsend, frecv, _) = refs[:n], refs[n:n + 3]
        x, y, c = _place()
        for a in range(n):
            for j, (px, py) in enumerate(_other_chips(x, y)):
                far = 2 * px + py
                mine = _half_rows(lands[a].shape[1:], c)
                theirs = _half_rows(lands[a].shape[1:], c, other=True)
                pltpu.make_async_remote_copy(
                    src_ref=land[a].at[far, mine], dst_ref=land[a].at[far, mine], send_sem=fsend.at[3 * a + j],
                    recv_sem=frecv.at[3 * a + j], device_id=(x, y, 1 - c), device_id_type=MESH_ID).wait_send()
                pltpu.make_async_remote_copy(
                    src_ref=land[a].at[far, theirs], dst_ref=land[a].at[far, theirs], send_sem=fsend.at[3 * a + j],
                    recv_sem=frecv.at[3 * a + j], device_id=(x, y, 1 - c), device_id_type=MESH_ID).wait_recv()

    mem = lambda t: pltpu.HBM(t.shape, t.dtype)
    return list(pl.pallas_call(
        body, name=name, out_shape=tuple(map(mem, lands)), in_specs=[HBM] * n + [SEM, SEM, ANY],
        out_specs=tuple([HBM] * n), input_output_aliases={i: i for i in range(n)},
        compiler_params=pltpu.CompilerParams(has_side_effects=DATAFLOW),
    )(*lands, fsend, frecv, after))


def _after(token):
    return _Exchange([token], [], [], lambda *_: None, lambda *_: None)


def _sum_devices(slots):
    def body(s_ref, o_ref):
        acc = s_ref[0]
        for d in range(1, N_DEV):
            acc = acc + s_ref[d]
        o_ref[...] = acc

    return pl.pallas_call(
        body, in_specs=[pl.BlockSpec(memory_space=pltpu.VMEM)], out_specs=pl.BlockSpec(memory_space=pltpu.VMEM),
        out_shape=SDS(slots.shape[1:], F32), name="sum_small",
        compiler_params=pltpu.CompilerParams(vmem_limit_bytes=32 * 1024 * 1024))(slots)


def _adam_small(ws, gs, ms, vs):
    n = len(ws)

    def body(*refs):
        for i in range(n):
            w_ref, g_ref, m_ref, v_ref = (refs[k * n + i] for k in range(4))
            outs = _adam_math(w_ref[...], g_ref[...], m_ref[...], v_ref[...])
            for k in range(3):
                refs[(4 + k) * n + i][...] = outs[k]

    vmem = pl.BlockSpec(memory_space=pltpu.VMEM)
    return pl.pallas_call(
        body, in_specs=[vmem] * (4 * n), out_specs=[vmem] * (3 * n),
        out_shape=[SDS(w.shape, F32) for w in ws] * 3, name="adam_small",
        compiler_params=pltpu.CompilerParams(vmem_limit_bytes=32 * 1024 * 1024))(*ws, *gs, *ms, *vs)


def _local_step(x, target, small, big, tb, distributed):
    dist = distributed
    me = (2 * lax.axis_index("x") + lax.axis_index("y")) if dist else 0
    tb_ssm = min(tb, 256)
    bucket = jnp.asarray(_bucket_table())
    place_own = lambda t: lax.dynamic_update_index_in_dim(lax.empty((N_CHIPS,) + t.shape, t.dtype), t, me, 0)
    if dist:
        in_legs = _gather_start("gather_in_start", [big["w_in"]], [place_own(big["w_in"])], small["d_skip"])
        names = sorted(small)
        in_token, values = lax.optimization_barrier((in_legs[4], [small[n] for n in names]))
        small = dict(zip(names, values))
    g1, g2, g3, g4 = small["norm_mix_pre"], small["norm_mix_post"], small["norm_mlp_pre"], small["norm_mlp_post"]

    keys_first = lambda t: jnp.swapaxes(t, -1, -2)
    bias = _bias_table(small["rel_bias"], bucket)
    sink_rows = keys_first(_pair_layout(jnp.broadcast_to(small["sinks"].reshape(N_HEADS, 1, 1), (N_HEADS, BLOCK, 1))))
    disc_args = (small["lam_re"], small["lam_im"], small["log_dt"], small["b_re"], small["b_im"])
    (ab_re, ab_im, bb_re, bb_im), disc_vjp = jax.vjp(_ssm_discretize, *disc_args)
    tab_f, tab_b = _scan_tables(ab_re, ab_im)
    bmat = _bf(_b_matrix(bb_re, bb_im))
    cmat = _bf(_c_matrix(small["c_re"], small["c_im"]))
    bmat_t, cmat_t = bmat.transpose(0, 2, 1), cmat.transpose(0, 2, 1)
    d_skip = small["d_skip"]

    mix = ("w_glu", "w_attn_branch", "w_ssm_branch", "w_out")
    rest = [big[n] for n in mix + ("w_ff_in", "w_ff_out")]
    if dist:
        send, recv, src, lands, _ = in_legs
        tab_f, tab_b, bias, sink_rows, bmat, cmat, bmat_t, cmat_t, rest, rest_lands = lax.optimization_barrier(
            (tab_f, tab_b, bias, sink_rows, bmat, cmat, bmat_t, cmat_t, rest, [place_own(t) for t in rest]))
        corner = lambda t: t.reshape(-1, t.shape[-1])[:1, :LANES].astype(F32)
        prepared = sum(map(corner, [tab_b, bias, sink_rows, bmat, cmat] + rest_lands), in_token[:1])
        send, recv, lands, in_passed = _gather_pass("gather_in_pass", send, recv, src, lands, prepared)
        (g_in,) = _gather_wait("gather_in_wait", send, recv, lands, in_passed)
        w_in = g_in.reshape(IN_W, D_MODEL)
    else:
        w_in = big["w_in"]
    token = None
    n_mix = len(mix)
    if dist:
        send, recv, rest, lands, token = _gather_start("gather_rest_start", rest, rest_lands, in_passed)
    h1, q, k, v, u, ga, gs = _inproj_fwd(x, g1, w_in, tb, _after(token) if dist else None)
    s, h = _ssm_fwd(u, bmat, cmat, tab_f, d_skip, tb)
    if dist:
        fsend, frecv, mix_lands, token = _gather_pass("gather_mix_pass", send, recv, rest[:n_mix], lands[:n_mix], s)
    att = _attn_fwd(q, k, v, bias, sink_rows, _after(token) if dist else None)[0]
    if dist:
        w_mix = _gather_wait("gather_mix_wait", fsend, frecv, mix_lands, att)
        fsend, frecv, ff_lands, token = _gather_pass(
            "gather_ff_pass", send, recv, rest[n_mix:], lands[n_mix:], w_mix[0], n_mix)
        rest = w_mix + ff_lands
    w_glu, w_ab, w_sb, w_out = rest[:n_mix]
    w_glu = w_glu.reshape(SSM_W, SSM_W)
    w_out = w_out.reshape(D_MODEL, D_MODEL)
    x2 = _merge_fwd(x, s, att, ga, gs, g2, w_glu, w_ab, w_sb, w_out, tb, _after(token) if dist else None)
    if dist:
        rest[n_mix:] = _gather_wait("gather_ff_wait", fsend, frecv, ff_lands, x2)
    w_ffi, w_ffo = [rest[n_mix]], rest[n_mix + 1]
    dy, df, h3, ra, loss_acc, dg4 = _mlp_fwd_loss(x2, target, g3, g4, w_ffi, w_ffo, tb)

    dx2, da, dg3 = _mlp_bwd(x2, dy, df, ra, g3, w_ffi, w_ffo, tb)
    tl = min(2048, x.shape[0])
    chunked = (N_CHIPS, D_FF // N_CHIPS, D_MODEL)
    d_ffi, b_ffi = _matmul_tn("grad_w_ff_in", h3, da, D_MODEL, D_FF // FF_CHUNKS, tl, True)
    d_ffo, b_ffo = _matmul_tn("grad_w_ff_out", ra, df, D_FF // FF_CHUNKS, D_MODEL, tl, False, square_a=True)
    d_ffo, b_ffo = d_ffo.reshape(chunked), b_ffo.reshape(chunked)
    behind = lambda flight: _after(flight.token) if dist else None
    ff_fl = _scatter_off("scatter_ff_off", [b_ffi, b_ffo], d_ffo) if dist else None
    outs = _merge_bwd(dx2, s, att, ga, gs, g2, w_glu, w_ab, w_sb, w_out, tb_ssm, behind(ff_fl))
    ds, datt, dga, dgs, dg2, d_glu, d_ab, d_sb, d_out, b_glu, b_ab, b_sb, b_out = outs
    glu4, out4 = (N_CHIPS, SSM_W // N_CHIPS, SSM_W), (N_CHIPS, D_MODEL // N_CHIPS, D_MODEL)
    d_mix = [d_glu.reshape(glu4), d_ab, d_sb, d_out.reshape(out4)]
    b_mix = [b_glu.reshape(glu4), b_ab, b_sb, b_out.reshape(out4)]
    mix_fl = _scatter_off("scatter_mix_off", b_mix, d_mix[-1]) if dist else None
    du, d_bmat, d_cmat, da_acc, dd_skip = _ssm_bwd(
        ds, u, h, bmat_t, cmat_t, tab_b, d_skip, tb, behind(mix_fl))
    dq, dk, dv, dbias, dsink_rows = _attn_bwd(q, k, v, datt, bias, sink_rows)
    dx, dpj, dg1 = _inproj_bwd(x, dx2, dq, dk, dv, du, dga, dgs, g1, w_in, tb)

    dab_re, dab_im = _state_unlayout(jnp.sum(da_acc, axis=0))
    dbb_re, dbb_im = _b_matrix_grad(d_bmat)
    d_lam_re, d_lam_im, d_log_dt, d_b_re, d_b_im = disc_vjp((dab_re, dab_im, dbb_re, dbb_im))
    d_c_re, d_c_im = _c_matrix_grad(d_cmat)
    d_rel = _bias_grad(dbias, bucket)
    d_sinks = jnp.sum(_pair_unlayout(keys_first(dsink_rows)), axis=(1, 2))
    small_grads = dict(
        norm_mix_pre=dg1, norm_mix_post=dg2, norm_mlp_pre=dg3, norm_mlp_post=dg4, rel_bias=d_rel, sinks=d_sinks,
        lam_re=d_lam_re, lam_im=d_lam_im, log_dt=d_log_dt, b_re=d_b_re, b_im=d_b_im, c_re=d_c_re, c_im=d_c_im,
        d_skip=dd_skip)
    small_fl = _devices_off("small_off", _pack(small_grads, loss_acc), dg1) if dist else None
    outs = _matmul_tn("grad_w_in", dpj, h1, IN_W // 2, D_MODEL, tl, False, behind(small_fl))
    in4 = (N_CHIPS, IN_W // N_CHIPS, D_MODEL)
    d_in, b_in = outs[0].reshape(in4), outs[1].reshape(in4)
    if not dist:
        return loss_acc, dx, small_grads, dict(zip(BIG, [d_in] + d_mix + [d_ffi, d_ffo]))
    in_fl = _scatter_off("scatter_w_in_off", [b_in], d_in)
    r_ffi, r_ffo = _land("scatter_ff_land", ff_fl, in_fl.token)[1]
    p_ffi = _sum4("sum_w_ff_in", d_ffi, r_ffi, me)
    p_ffo = _sum4("sum_w_ff_out", d_ffo, r_ffo, me)
    swap_fl = _swap_off("swap_ff_off", [p_ffi, p_ffo], r_ffo)
    r_mix = _land("scatter_mix_land", mix_fl, swap_fl.token)[1]
    p_mix = _sum4_group("sum_mix", d_mix, r_mix, me)
    mix_swap = _swap_off("swap_mix_off", p_mix, swap_fl.token)
    pending = dict(d_in=d_in, in_fl=in_fl, mix_swap=mix_swap, ff_swap=swap_fl, me=me)
    return loss_acc, dx, small_fl, pending


SMALL = ['norm_mix_pre', 'norm_mix_post', 'norm_mlp_pre', 'norm_mlp_post', 'rel_bias', 'sinks', 'lam_re', 'lam_im',
         'log_dt', 'b_re', 'b_im', 'c_re', 'c_im', 'd_skip']
BIG = ['w_in', 'w_glu', 'w_attn_branch', 'w_ssm_branch', 'w_out', 'w_ff_in', 'w_ff_out']
WEIGHTS = ['norm_mix_pre', 'norm_mix_post', 'norm_mlp_pre', 'norm_mlp_post', 'w_in', 'rel_bias', 'sinks', 'lam_re',
           'lam_im', 'log_dt', 'b_re', 'b_im', 'c_re', 'c_im', 'd_skip', 'w_glu', 'w_attn_branch', 'w_ssm_branch',
           'w_out', 'w_ff_in', 'w_ff_out']
PACK_COLS = 1024
PACK_ORDER = ['b_re', 'b_im', 'c_re', 'c_im', 'lam_re', 'lam_im', 'norm_mix_pre', 'norm_mix_post', 'norm_mlp_pre',
              'norm_mlp_post', 'rel_bias', 'sinks', 'log_dt', 'd_skip']


STATE_MINOR = ('b_re', 'b_im')
PACK_ROWS = 144
LOSS_ROW = 140


def _pack(named, loss_acc):
    parts = []
    for n in PACK_ORDER:
        a = jnp.swapaxes(named[n], -1, -2) if n in STATE_MINOR else named[n]
        flat = a.reshape(-1)
        rows = -(-flat.shape[0] // PACK_COLS)
        parts.append(jnp.pad(flat, (0, rows * PACK_COLS - flat.shape[0])).reshape(rows, PACK_COLS))
    assert sum(p.shape[0] for p in parts) == LOSS_ROW
    parts.append(jnp.pad(loss_acc[0:1], ((0, PACK_ROWS - LOSS_ROW - 1), (0, PACK_COLS - loss_acc.shape[1]))))
    return jnp.concatenate(parts, axis=0)


def _unpack(packed, shapes):
    out, at = {}, 0
    for n in PACK_ORDER:
        shape = shapes[n][:-2] + (shapes[n][-1], shapes[n][-2]) if n in STATE_MINOR else shapes[n]
        size = int(np.prod(shape))
        rows = -(-size // PACK_COLS)
        blk = packed[at:at + rows]
        out[n] = (blk.reshape(-1)[:size] if size % PACK_COLS else blk).reshape(shape)
        at += rows
    return out


def kernel(x, norm_mix_pre, norm_mix_post, norm_mlp_pre, norm_mlp_post, w_in, rel_bias, sinks, lam_re, lam_im, log_dt, b_re, b_im, c_re, c_im, d_skip, w_glu, w_attn_branch, w_ssm_branch, w_out, w_ff_in, w_ff_out, loss_target, m_norm_mix_pre, m_norm_mix_post, m_norm_mlp_pre, m_norm_mlp_post, m_w_in, m_rel_bias, m_sinks, m_lam_re, m_lam_im, m_log_dt, m_b_re, m_b_im, m_c_re, m_c_im, m_d_skip, m_w_glu, m_w_attn_branch, m_w_ssm_branch, m_w_out, m_w_ff_in, m_w_ff_out, v_norm_mix_pre, v_norm_mix_post, v_norm_mlp_pre, v_norm_mlp_post, v_w_in, v_rel_bias, v_sinks, v_lam_re, v_lam_im, v_log_dt, v_b_re, v_b_im, v_c_re, v_c_im, v_d_skip, v_w_glu, v_w_attn_branch, v_w_ssm_branch, v_w_out, v_w_ff_in, v_w_ff_out):
    env = dict(locals())
    w = {n: env[n] for n in WEIGHTS}
    m = {n: env["m_" + n] for n in WEIGHTS}
    v = {n: env["v_" + n] for n in WEIGHTS}
    seq = x.shape[1]
    tb = min(512, seq)

    small = {n: w[n] for n in ('norm_mix_pre', 'norm_mix_post', 'norm_mlp_pre', 'norm_mlp_post', 'rel_bias')}
    small.update({n: w[n][0] for n in ('sinks', 'lam_re', 'lam_im', 'log_dt', 'b_re', 'b_im', 'c_re', 'c_im')})
    small['d_skip'] = w['d_skip']
    shard = lambda t, n: t[n][0].T if n == 'w_in' else t[n][0]
    unshard = lambda a, n: (a.T if n == 'w_in' else a)[None]
    _, dx, small_fl, pending = _local_step(
        x[0], loss_target[0], small, {n: _bf(shard(w, n)) for n in BIG}, tb, True)

    grads, deltas, new_m, new_v = {}, {}, {}, {}

    def adam(n, partials, after=None):
        outs = _adam_pair("adam_" + n, (shard(w, n), *partials, shard(m, n), shard(v, n)), after)
        grads[n], deltas[n], new_m[n], new_v[n] = [unshard(a, n) for a in outs]
        return outs[3]

    mix = ("w_glu", "w_attn_branch", "w_ssm_branch", "w_out")
    in_fl = pending["in_fl"]

    small_g = _sum_devices(_land("small_land", small_fl, pending["mix_swap"].token)[1][0])
    loss = small_g[LOSS_ROW, 0]
    minor = lambda t, n: jnp.swapaxes(t, -1, -2) if n in STATE_MINOR else t
    g_small = _unpack(small_g, {n: w[n].shape for n in SMALL})
    outs = _adam_small([minor(w[n], n) for n in SMALL], [g_small[n] for n in SMALL],
                       [minor(m[n], n) for n in SMALL], [minor(v[n], n) for n in SMALL])
    grads.update({n: minor(g_small[n], n) for n in SMALL})
    for k, dst in enumerate((deltas, new_m, new_v)):
        dst.update({n: minor(a, n) for n, a in zip(SMALL, outs[k * len(SMALL):(k + 1) * len(SMALL)])})

    last = outs[0]
    own_ff, sib_ff = _land("swap_ff_land", pending["ff_swap"], last)
    for n, partials in zip(("w_ff_in", "w_ff_out"), zip(own_ff, sib_ff)):
        last = adam(n, partials, last)
    own_mix, sib_mix = _land("swap_mix_land", pending["mix_swap"], last)
    outs = _adam_group("adam_mix", [(shard(w, n), p, s, shard(m, n), shard(v, n))
                                    for n, p, s in zip(mix, own_mix, sib_mix)], last)
    for n, item in zip(mix, outs):
        grads[n], deltas[n], new_m[n], new_v[n] = [unshard(a, n) for a in item]

    (r_in,) = _land("scatter_w_in_land", in_fl, outs[-1][3])[1]
    adam("w_in", _sum4_swap("sum_swap_w_in", pending["d_in"], r_in, pending["me"]))

    return (loss, dx[None], *[grads[n] for n in WEIGHTS], *[deltas[n] for n in WEIGHTS],
            *[new_m[n] for n in WEIGHTS], *[new_v[n] for n in WEIGHTS])
```

```python
import functools
import math

import numpy as np
import jax
import jax.numpy as jnp
from jax import lax
from jax.experimental import pallas as pl
from jax.experimental.pallas import tpu as pltpu

F32 = jnp.float32
BF16 = jnp.bfloat16

D_MODEL = 1024
N_HEADS = 8
N_KV = 2
Q_GROUP = 4
HEAD_DIM = 64
ATTN_W = 512
KV_W = 128
BLOCK = 128
N_BUCKETS = 32
MAX_DISTANCE = 128
NEG_INF = -1e30
SSM_W = 512
SSM_GROUP = 16
SSM_GROUPS = 32
SSM_STATE = 64
N_SUPER = 4
GROUPS_PER_SUPER = SSM_GROUPS // N_SUPER
SUPER_IN = GROUPS_PER_SUPER * SSM_GROUP
SUPER_HALF = GROUPS_PER_SUPER * SSM_STATE
SUPER_W = 2 * SUPER_HALF
STATE_COLS = N_SUPER * SUPER_W
D_FF = 4096
FF_CHUNKS = 4
IN_W = 3328
SPLITS = (0, 512, 640, 768, 1280, 2304, 3328)
RMS_EPS = 1e-6
N_CHIPS = 4
N_DEV = 8
SUBLANES = 8
LANES = 128
STATE_TILES = STATE_COLS // LANES
SUPER_TILES = SUPER_W // LANES

ADAM_LR = 0.001
ADAM_B1 = 0.9
ADAM_B2 = 0.999
ADAM_EPS = 1e-08
ADAM_WD = 0.01
ADAM_STEP = 10

VMEM_BIG = 56 * 1024 * 1024
SDS = jax.ShapeDtypeStruct
MESH_ID = pl.DeviceIdType.MESH
ANY = pl.BlockSpec(memory_space=pl.ANY)


def _bf(x):
    return x.astype(BF16)


def _mm(a, b):
    return jnp.dot(a, b, preferred_element_type=F32)


def _mm_nt(a, b):
    return lax.dot_general(a, b, (((1,), (1,)), ((), ())), preferred_element_type=F32)


def _mm_tn(a, b):
    return lax.dot_general(a, b, (((0,), (0,)), ((), ())), preferred_element_type=F32)


def _sig(x):
    return 1.0 / (1.0 + jnp.exp(-x))


def _rms(x, g):
    r = lax.rsqrt(jnp.mean(x * x, axis=-1, keepdims=True) + RMS_EPS)
    xh = x * r
    return xh * g, xh, r


def _rms_bwd(dout, xh, r, g):
    dg = jnp.sum(dout * xh, axis=0, keepdims=True)
    dxh = dout * g
    dx = r * (dxh - xh * jnp.mean(dxh * xh, axis=-1, keepdims=True))
    return dx, dg


_GELU_C = math.sqrt(2.0 / math.pi)


def _gelu_and_grad(x):
    x2 = x * x
    inner = _GELU_C * (x + 0.044715 * (x2 * x))
    t = jnp.tanh(inner)
    y = 0.5 * x * (1.0 + t)
    dy = 0.5 * (1.0 + t) + 0.5 * x * (1.0 - t * t) * (_GELU_C * (1.0 + 3.0 * 0.044715 * x2))
    return y, dy


def _zero_map(nd, *_):
    return (0,) * nd


def _params(n_axes, vmem=None):
    return pltpu.CompilerParams(dimension_semantics=("arbitrary",) * n_axes, vmem_limit_bytes=vmem)


class _Exchange:
    def __init__(self, ins, outs, sems, start, wait):
        self.ins, self.outs, self.sems, self.start, self.wait = list(ins), list(outs), list(sems), start, wait


def _fused_call(name, body, grid, in_specs, out_specs, out_shape, scratch, args, exchange, params):
    n_in, n_out, n_scr = len(in_specs), len(out_specs), len(scratch)
    if exchange is None:
        fn = body
    else:
        ex = exchange
        n_xi, n_xo = len(ex.ins), len(ex.outs)

        def fn(*refs):
            at = 0
            parts = []
            for n in (n_in, n_xi, n_out, n_xo, n_scr, len(ex.sems)):
                parts.append(refs[at:at + n])
                at += n
            ins, x_in, outs, x_out, scr, x_sem = parts
            ids = [pl.program_id(a) for a in range(len(grid))]
            first = functools.reduce(jnp.logical_and, [i == 0 for i in ids])
            last = functools.reduce(jnp.logical_and, [i == g - 1 for i, g in zip(ids, grid)])

            @pl.when(first)
            def _():
                ex.start(x_in, x_out, x_sem)

            body(*ins, *outs, *scr)

            @pl.when(last)
            def _():
                ex.wait(x_in, x_out, x_sem)

        in_specs = list(in_specs) + [ANY] * n_xi
        out_specs = list(out_specs) + [ANY] * n_xo
        out_shape = list(out_shape) + ex.outs
        scratch = list(scratch) + ex.sems
        args = list(args) + ex.ins
    return pl.pallas_call(fn, grid=grid, in_specs=in_specs, out_specs=out_specs, out_shape=out_shape,
                          scratch_shapes=list(scratch), name=name, compiler_params=params)(*args)


def _rowcall(name, body, seq, tb, rows, consts, row_outs, acc_outs, scratch=(), reverse=False, vmem=None,
             exchange=None):
    nb = seq // tb
    rmap = (lambda i: (nb - 1 - i, 0)) if reverse else (lambda i: (i, 0))
    tmap = lambda i: (0,) + rmap(i)

    def row_spec(width):
        if isinstance(width, tuple):
            return pl.BlockSpec((width[0], tb, width[1]), tmap)
        return pl.BlockSpec((tb, width), rmap)

    def row_shape(width):
        return (width[0], seq, width[1]) if isinstance(width, tuple) else (seq, width)

    in_specs = [row_spec(a.shape[1] if a.ndim == 2 else (a.shape[0], a.shape[2])) for a in rows]
    in_specs += [pl.BlockSpec(a.shape, functools.partial(_zero_map, a.ndim), pipeline_mode=pl.Buffered(1))
                 for a in consts]
    out_specs = [row_spec(c) for c, _ in row_outs] + [ANY] * len(acc_outs)
    out_shape = [SDS(row_shape(c), dt) for c, dt in row_outs] + [SDS(s, dt) for s, dt in acc_outs]
    n_main = len(rows) + len(consts) + len(row_outs)
    n_acc = len(acc_outs)

    def fn(*refs):
        main, acc_hbm, rest = refs[:n_main], refs[n_main:n_main + n_acc], refs[n_main + n_acc:]
        acc_vmem, own = rest[:n_acc], rest[n_acc:]
        body(*main, *acc_vmem, *own)

        @pl.when(pl.program_id(0) == nb - 1)
        def _():
            for src, dst in zip(acc_vmem, acc_hbm):
                pltpu.sync_copy(src, dst)

    buffers = [pltpu.VMEM(s, dt) for s, dt in acc_outs] + list(scratch)
    return _fused_call(name, fn if acc_outs else body, (nb,), in_specs, out_specs, out_shape, buffers,
                       [*rows, *consts], exchange, _params(1, vmem))


def _inproj_fwd(x, g1, w_in, tb, exchange=None):
    seq = x.shape[0]

    def body(x_ref, g_ref, w_ref, h_ref, q_ref, k_ref, v_ref, u_ref, ga_ref, gs_ref):
        h, _, _ = _rms(x_ref[...], g_ref[...])
        hb = _bf(h)
        h_ref[...] = hb
        pj = _mm_nt(hb, w_ref[...])
        q_ref[...] = _bf(pj[:, SPLITS[0]:SPLITS[1]])
        k_ref[...] = _bf(pj[:, SPLITS[1]:SPLITS[2]])
        v_ref[...] = _bf(pj[:, SPLITS[2]:SPLITS[3]])
        u_ref[...] = pj[:, SPLITS[3]:SPLITS[4]]
        ga_ref[...] = pj[:, SPLITS[4]:SPLITS[5]]
        gs_ref[...] = pj[:, SPLITS[5]:SPLITS[6]]

    return _rowcall("inproj_fwd", body, seq, tb, [x], [g1, w_in],
                    [(D_MODEL, BF16), (ATTN_W, BF16), (KV_W, BF16), (KV_W, BF16), (SSM_W, F32),
                     (D_MODEL, F32), (D_MODEL, F32)], [], vmem=VMEM_BIG, exchange=exchange)


def _inproj_bwd(x, dx2, dq, dk, dv, du, dga, dgs, g1, w_in, tb, exchange=None):
    seq = x.shape[0]

    def body(x_ref, dx2_ref, dq_ref, dk_ref, dv_ref, du_ref, dga_ref, dgs_ref, g_ref, w_ref,
             dx_ref, dpj_ref, dg_ref):
        @pl.when(pl.program_id(0) == 0)
        def _():
            dg_ref[...] = jnp.zeros_like(dg_ref)

        dpj = jnp.concatenate([dq_ref[...], dk_ref[...], dv_ref[...], _bf(du_ref[...]),
                               dga_ref[...], dgs_ref[...]], axis=1)
        dpj_ref[...] = dpj
        dh = _mm(dpj, w_ref[...])
        g = g_ref[...]
        _, xh, r = _rms(x_ref[...], g)
        dxn, dg = _rms_bwd(dh, xh, r, g)
        dx_ref[...] = dx2_ref[...] + dxn
        dg_ref[...] += dg

    return _rowcall("inproj_bwd", body, seq, tb, [x, dx2, dq, dk, dv, du, dga, dgs], [g1, w_in],
                    [(D_MODEL, F32), (IN_W, BF16)], [((1, D_MODEL), F32)], vmem=VMEM_BIG, exchange=exchange)


def _bucket_table():
    qi = np.arange(BLOCK)[:, None]
    kj = np.arange(2 * BLOCK)[None, :]
    dist = qi + BLOCK - kj
    max_exact = N_BUCKETS // 2
    d = np.maximum(dist, 0)
    df = np.maximum(d, 1).astype(np.float32)
    large = max_exact + (np.log(df / np.float32(max_exact)) / np.float32(math.log(MAX_DISTANCE / max_exact))
                         * np.float32(N_BUCKETS - max_exact)).astype(np.int32)
    large = np.minimum(large, N_BUCKETS - 1)
    bucket = np.where(d < max_exact, d, large)
    valid = (dist >= 0) & (dist < BLOCK)
    return np.where(valid, bucket, -1).astype(np.int32)


def _bias_table(rel_bias, bucket):
    def body(rb_ref, bk_ref, o_ref):
        bk = bk_ref[...]
        has_prev = lax.broadcasted_iota(jnp.int32, bk.shape, 1) >= BLOCK
        for h in range(N_HEADS):
            kh, j, par = h // Q_GROUP, (h // 2) % 2, h % 2
            acc = jnp.full((BLOCK, 2 * BLOCK), NEG_INF, F32)
            for b in range(N_BUCKETS):
                acc = jnp.where(bk == b, rb_ref[b, h], acc)
            o_ref[0, kh, par, :, j * BLOCK:(j + 1) * BLOCK] = jnp.where(has_prev, acc, NEG_INF).T
            o_ref[1, kh, par, :, j * BLOCK:(j + 1) * BLOCK] = acc.T

    return pl.pallas_call(
        body, out_shape=SDS((2, N_KV, 2, 2 * BLOCK, 2 * BLOCK), F32),
        in_specs=[pl.BlockSpec(memory_space=pltpu.SMEM), pl.BlockSpec(memory_space=pltpu.VMEM)],
        out_specs=pl.BlockSpec(memory_space=pltpu.VMEM), name="bias_table",
    )(rel_bias, bucket)


def _bias_grad(dbias, bucket):
    def body(db_ref, bk_ref, o_ref):
        bk = bk_ref[...]
        for h in range(N_HEADS):
            kh, j, par = h // Q_GROUP, (h // 2) % 2, h % 2
            db = db_ref[kh, par, :, j * BLOCK:(j + 1) * BLOCK].T
            for b in range(N_BUCKETS):
                o_ref[b, h] = jnp.sum(jnp.where(bk == b, db, 0.0))

    return pl.pallas_call(
        body, out_shape=SDS((N_BUCKETS, N_HEADS), F32),
        in_specs=[pl.BlockSpec(memory_space=pltpu.VMEM), pl.BlockSpec(memory_space=pltpu.VMEM)],
        out_specs=pl.BlockSpec(memory_space=pltpu.SMEM), name="bias_grad",
    )(dbias, bucket)


TILE = 2 * HEAD_DIM


def _pair_layout(t):
    lead = t.shape[:-3]
    t = t.reshape(lead + (N_KV, 2, 2) + t.shape[-2:])
    nl = len(lead)
    t = jnp.transpose(t, tuple(range(nl)) + (nl, nl + 2, nl + 1, nl + 3, nl + 4))
    return t.reshape(lead + (N_KV, 2, 2 * BLOCK, t.shape[-1]))


def _pair_unlayout(t):
    t = t.reshape(N_KV, 2, 2, BLOCK, t.shape[-1]).transpose(0, 2, 1, 3, 4)
    return t.reshape(N_HEADS, BLOCK, t.shape[-1])


def _halves(t):
    tf = t.astype(F32)
    low = lax.broadcasted_iota(jnp.int32, tf.shape, 1) < HEAD_DIM
    swapped = pltpu.roll(tf, HEAD_DIM, 1)
    zero = jnp.zeros_like(tf)
    return ((_bf(jnp.where(low, tf, zero)), _bf(jnp.where(low, zero, swapped))),
            (_bf(jnp.where(low, swapped, zero)), _bf(jnp.where(low, zero, tf))))


def _fold_halves(even, odd):
    low = lax.broadcasted_iota(jnp.int32, even.shape, 1) < HEAD_DIM
    comb = jnp.where(low, even, odd)
    return comb + pltpu.roll(comb, HEAD_DIM, 1)


def _tile_rows(ref, kh):
    return jnp.concatenate([ref[:, (2 * kh) * TILE:(2 * kh + 1) * TILE],
                            ref[:, (2 * kh + 1) * TILE:(2 * kh + 2) * TILE]], axis=0)


def _halves_t(t):
    tt = t.astype(F32).T
    top = lax.broadcasted_iota(jnp.int32, tt.shape, 0) < HEAD_DIM
    swapped = jnp.concatenate([tt[HEAD_DIM:], tt[:HEAD_DIM]], axis=0)
    zero = jnp.zeros_like(tt)
    return ((_bf(jnp.where(top, tt, zero)), _bf(jnp.where(top, zero, swapped))),
            (_bf(jnp.where(top, swapped, zero)), _bf(jnp.where(top, zero, tt))))


def _attn_probs(km, qk, bias, sink):
    lg = _mm_nt(km, qk) * (HEAD_DIM ** -0.5) + bias
    m = jnp.maximum(jnp.max(lg, axis=0, keepdims=True), sink)
    p = jnp.exp(lg - m)
    es = jnp.exp(sink - m)
    inv = 1.0 / (jnp.sum(p, axis=0, keepdims=True) + es)
    return p * inv, es * inv


def _attn_fwd(q, k, v, bias, sink_rows, exchange=None):
    seq = q.shape[0]
    nblk = seq // BLOCK

    def body(q_ref, kp_ref, kc_ref, vp_ref, vc_ref, b_ref, s_ref, o_ref):
        which = jnp.minimum(pl.program_id(0), 1)
        kms = _halves(jnp.concatenate([kp_ref[...], kc_ref[...]], axis=0))
        vts = _halves_t(jnp.concatenate([vp_ref[...], vc_ref[...]], axis=0))
        for kh in range(N_KV):
            qk = _tile_rows(q_ref, kh)
            acc = jnp.zeros((TILE, 2 * BLOCK), F32)
            for par in range(2):
                pr, _ = _attn_probs(kms[kh][par], qk, b_ref[which, kh, par], s_ref[kh, par])
                acc = acc + _mm(vts[kh][par], _bf(pr))
            acc = acc.T
            o_ref[:, (2 * kh) * TILE:(2 * kh + 1) * TILE] = _bf(acc[:BLOCK])
            o_ref[:, (2 * kh + 1) * TILE:(2 * kh + 2) * TILE] = _bf(acc[BLOCK:])

    cur = lambda n: (n, 0)
    prev = lambda n: (jnp.maximum(n - 1, 0), 0)
    return _fused_call(
        "attn_fwd", body, (nblk,),
        [pl.BlockSpec((BLOCK, ATTN_W), cur),
         pl.BlockSpec((BLOCK, KV_W), prev), pl.BlockSpec((BLOCK, KV_W), cur),
         pl.BlockSpec((BLOCK, KV_W), prev), pl.BlockSpec((BLOCK, KV_W), cur),
         pl.BlockSpec(bias.shape, functools.partial(_zero_map, bias.ndim)),
         pl.BlockSpec(sink_rows.shape, functools.partial(_zero_map, sink_rows.ndim))],
        [pl.BlockSpec((BLOCK, ATTN_W), cur)], [SDS((seq, ATTN_W), BF16)], [],
        [q, k, k, v, v, bias, sink_rows], exchange, _params(1))


def _attn_bwd(q, k, v, d_out, bias, sink_rows, exchange=None):
    seq = q.shape[0]
    nblk = seq // BLOCK

    def body(q_ref, kp_ref, kc_ref, vp_ref, vc_ref, do_ref, b_ref, s_ref,
             dq_ref, dk_ref, dv_ref, db_ref, ds_ref, ck_ref, cv_ref):
        n = pl.program_id(0)

        @pl.when(n == 0)
        def _():
            db_ref[...] = jnp.zeros_like(db_ref)
            ds_ref[...] = jnp.zeros_like(ds_ref)
            ck_ref[...] = jnp.zeros_like(ck_ref)
            cv_ref[...] = jnp.zeros_like(cv_ref)

        @pl.when(n < nblk)
        def _():
            which = jnp.minimum(n, 1)
            scale = HEAD_DIM ** -0.5
            kcat = jnp.concatenate([kp_ref[...], kc_ref[...]], axis=0)
            kms = _halves(kcat)
            kts = _halves_t(kcat)
            vms = _halves(jnp.concatenate([vp_ref[...], vc_ref[...]], axis=0))
            dks, dvs = [], []
            for kh in range(N_KV):
                qk = _tile_rows(q_ref, kh)
                dok = _tile_rows(do_ref, kh)
                dq = jnp.zeros((TILE, 2 * BLOCK), F32)
                dkp, dvp = [], []
                for par in range(2):
                    pr, ps = _attn_probs(kms[kh][par], qk, b_ref[which, kh, par], s_ref[kh, par])
                    dp = _mm_nt(vms[kh][par], dok)
                    rs = jnp.sum(pr * dp, axis=0, keepdims=True)
                    dlg = pr * (dp - rs)
                    ds_ref[kh, par] += -ps * rs
                    db_ref[kh, par] += dlg
                    dlb = _bf(dlg)
                    dq = dq + _mm(kts[kh][par], dlb)
                    dkp.append(_mm(dlb, qk))
                    dvp.append(_mm(_bf(pr), dok))
                dq = _bf((dq * scale).T)
                dq_ref[:, (2 * kh) * TILE:(2 * kh + 1) * TILE] = dq[:BLOCK]
                dq_ref[:, (2 * kh + 1) * TILE:(2 * kh + 2) * TILE] = dq[BLOCK:]
                dks.append(_fold_halves(*dkp))
                dvs.append(_fold_halves(*dvp))
            low = lax.broadcasted_iota(jnp.int32, (2 * BLOCK, TILE), 1) < HEAD_DIM
            dkk = jnp.where(low, dks[0], dks[1]) * scale
            dvv = jnp.where(low, dvs[0], dvs[1])
            dk_ref[...] = _bf(ck_ref[...] + dkk[:BLOCK])
            ck_ref[...] = dkk[BLOCK:]
            dv_ref[...] = _bf(cv_ref[...] + dvv[:BLOCK])
            cv_ref[...] = dvv[BLOCK:]

        @pl.when(n == nblk)
        def _():
            dk_ref[...] = _bf(ck_ref[...])
            dv_ref[...] = _bf(cv_ref[...])

    cur = lambda n: (jnp.minimum(n, nblk - 1), 0)
    prev = lambda n: (jnp.maximum(jnp.minimum(n, nblk - 1) - 1, 0), 0)
    late = lambda n: (jnp.maximum(n - 1, 0), 0)
    kv_spec = lambda m: pl.BlockSpec((BLOCK, KV_W), m)
    acc_b = pl.BlockSpec(bias.shape[1:], functools.partial(_zero_map, bias.ndim - 1))
    acc_s = pl.BlockSpec(sink_rows.shape, functools.partial(_zero_map, sink_rows.ndim))
    return _fused_call(
        "attn_bwd", body, (nblk + 1,),
        [pl.BlockSpec((BLOCK, ATTN_W), cur), kv_spec(prev), kv_spec(cur), kv_spec(prev), kv_spec(cur),
         pl.BlockSpec((BLOCK, ATTN_W), cur),
         pl.BlockSpec(bias.shape, functools.partial(_zero_map, bias.ndim)), acc_s],
        [pl.BlockSpec((BLOCK, ATTN_W), cur), kv_spec(late), kv_spec(late), acc_b, acc_s],
        [SDS((seq, ATTN_W), BF16), SDS((seq, KV_W), BF16), SDS((seq, KV_W), BF16),
         SDS(bias.shape[1:], F32), SDS(sink_rows.shape, F32)],
        [pltpu.VMEM((BLOCK, KV_W), F32), pltpu.VMEM((BLOCK, KV_W), F32)],
        [q, k, k, v, v, d_out, bias, sink_rows], exchange, _params(1))


def _ssm_discretize(lam_re, lam_im, log_dt, b_re, b_im):
    dt = jnp.exp(log_dt)[:, None]
    mag = jnp.exp(lam_re * dt)
    ab_re = mag * jnp.cos(lam_im * dt)
    ab_im = mag * jnp.sin(lam_im * dt)
    nr = ab_re - 1.0
    den = lam_re * lam_re + lam_im * lam_im
    f_re = (nr * lam_re + ab_im * lam_im) / den
    f_im = (ab_im * lam_re - nr * lam_im) / den
    bb_re = f_re[..., None] * b_re - f_im[..., None] * b_im
    bb_im = f_re[..., None] * b_im + f_im[..., None] * b_re
    return ab_re, ab_im, bb_re, bb_im


def _state_layout(re, im):
    lead = re.shape[:-2]
    z = jnp.stack([re, im], axis=-3).reshape(lead + (2, N_SUPER, GROUPS_PER_SUPER, SSM_STATE))
    return jnp.moveaxis(z, -4, -3).reshape(lead + (STATE_COLS,))


def _state_unlayout(vec):
    z = vec.reshape(N_SUPER, 2, GROUPS_PER_SUPER, SSM_STATE).transpose(1, 0, 2, 3)
    z = z.reshape(2, SSM_GROUPS, SSM_STATE)
    return z[0], z[1]


SEG = 4
WINDOW = SEG * SUBLANES


def _scan_tables(ab_re, ab_im):
    pw = [None, (ab_re, ab_im)]
    for _ in range(2, WINDOW + 1):
        pr, pi_ = pw[-1]
        pw.append((pr * ab_re - pi_ * ab_im, pr * ab_im + pi_ * ab_re))
    fwd = np.zeros((7, SUBLANES), np.int64)
    bwd = np.zeros((7, SUBLANES), np.int64)
    for k, shift in enumerate((1, 2, 4)):
        fwd[k] = [SEG * shift if r >= shift else 0 for r in range(SUBLANES)]
        bwd[k] = [SEG * shift if r < SUBLANES - shift else 0 for r in range(SUBLANES)]
    fwd[3] = [SEG * (r + 1) for r in range(SUBLANES)]
    bwd[3] = [SEG * (SUBLANES - r) for r in range(SUBLANES)]
    for k in range(1, SEG):
        fwd[3 + k] = bwd[3 + k] = k
    used = sorted((set(fwd.ravel()) | set(bwd.ravel())) - {0})
    select = lambda which: np.stack([(which == p) for p in used], axis=-1).astype(np.float32)
    stacked = _state_layout(jnp.stack([pw[p][0] for p in used]), jnp.stack([pw[p][1] for p in used]))
    conj_sign = np.where((np.arange(STATE_COLS) // SUPER_HALF) % 2 == 1, -1.0, 1.0).astype(np.float32)
    pick = functools.partial(jnp.einsum, 'krp,pc->krc', precision=lax.Precision.HIGHEST)
    return pick(select(fwd), stacked), pick(select(bwd), stacked) * conj_sign


_EYE = np.eye(GROUPS_PER_SUPER, dtype=np.float32)


def _b_matrix(bb_re, bb_im):
    bb = jnp.stack([bb_re, bb_im]).reshape(2, N_SUPER, GROUPS_PER_SUPER, SSM_STATE, SSM_GROUP)
    m = jnp.einsum('rsgpc,gh->sgcrhp', bb, _EYE)
    return m.reshape(N_SUPER, SUPER_IN, SUPER_W)


def _b_matrix_grad(dm):
    d = dm.reshape(N_SUPER, GROUPS_PER_SUPER, SSM_GROUP, 2, GROUPS_PER_SUPER, SSM_STATE)
    d = jnp.sum(d * _EYE[None, :, None, None, :, None], axis=4)
    d = d.transpose(3, 0, 1, 4, 2).reshape(2, SSM_GROUPS, SSM_STATE, SSM_GROUP)
    return d[0], d[1]


def _c_matrix(c_re, c_im):
    cc = jnp.stack([c_re, -c_im]).reshape(2, N_SUPER, GROUPS_PER_SUPER, SSM_GROUP, SSM_STATE)
    m = jnp.einsum('rsgcp,gh->srgphc', cc, _EYE)
    return m.reshape(N_SUPER, SUPER_W, SUPER_IN)


def _c_matrix_grad(dm):
    d = dm.reshape(N_SUPER, 2, GROUPS_PER_SUPER, SSM_STATE, GROUPS_PER_SUPER, SSM_GROUP)
    d = jnp.sum(d * _EYE[None, None, :, None, :, None], axis=4)
    d = d.transpose(1, 0, 2, 4, 3).reshape(2, SSM_GROUPS, SSM_GROUP, SSM_STATE)
    return d[0], -d[1]


def _cmul_add(xr, xi, ar, ai, sr, si):
    return xr + ar * sr - ai * si, xi + ar * si + ai * sr


def _scan_rows(buf_ref, tab_ref, carry_ref, n_windows, reverse, h_ref=None, da_ref=None):
    order = list(range(SEG - 1, -1, -1)) if reverse else list(range(SEG))
    near = SUBLANES - 1 if reverse else 0
    far = 0 if reverse else SUBLANES - 1
    s_in = SUBLANES - 1 if reverse else 1
    lanes = lambda tile: pl.ds(tile * LANES, LANES)

    def window(w0, tile_re, tile_im, c_re, c_im, acc):
        rows = lambda t: pl.ds(w0 + t, SUBLANES, stride=SEG)
        get = lambda ref, t: (ref.at[tile_re][rows(t), :], ref.at[tile_im][rows(t), :])
        tab = lambda k: (tab_ref[k, :, lanes(tile_re)], tab_ref[k, :, lanes(tile_im)])

        def put(t, xr, xi):
            buf_ref.at[tile_re][rows(t), :] = xr
            buf_ref.at[tile_im][rows(t), :] = xi

        a1 = tab(4)
        er, ei = get(buf_ref, order[0])
        for t in order[1:]:
            er, ei = _cmul_add(*get(buf_ref, t), *a1, er, ei)
            if t != order[-1]:
                put(t, er, ei)
        for k, shift in enumerate((1, 2, 4)):
            s = (SUBLANES - shift) if reverse else shift
            er, ei = _cmul_add(er, ei, *tab(k), pltpu.roll(er, s, 0), pltpu.roll(ei, s, 0))
        er, ei = _cmul_add(er, ei, *tab(3), c_re, c_im)
        put(order[-1], er, ei)
        sub = lax.broadcasted_iota(jnp.int32, er.shape, 0)
        in_re = jnp.where(sub == near, c_re, pltpu.roll(er, s_in, 0))
        in_im = jnp.where(sub == near, c_im, pltpu.roll(ei, s_in, 0))
        true = {order[-1]: (er, ei)}
        for idx, t in enumerate(order[:-1]):
            true[t] = _cmul_add(*get(buf_ref, t), *tab(4 + idx), in_re, in_im)
            put(t, *true[t])
        carry = (jnp.broadcast_to(er[far:far + 1], er.shape), jnp.broadcast_to(ei[far:far + 1], ei.shape))
        if acc is None:
            return carry, None
        acc_re, acc_im = acc
        for t in range(SEG):
            if t + 1 < SEG:
                gr, gim = true[t + 1]
            else:
                gr = jnp.where(sub == SUBLANES - 1, c_re, pltpu.roll(true[0][0], SUBLANES - 1, 0))
                gim = jnp.where(sub == SUBLANES - 1, c_im, pltpu.roll(true[0][1], SUBLANES - 1, 0))
            hr, hi = get(h_ref, t)
            acc_re = acc_re + gr * hr + gim * hi
            acc_im = acc_im + gim * hr - gr * hi
        return carry, (acc_re, acc_im)

    half = SUPER_HALF // LANES
    per = 2 if h_ref is None else 4
    for sb in range(N_SUPER):
        pairs = [(2 * half * sb + j, 2 * half * sb + half + j) for j in range(half)]

        def step(wi, state, pairs=pairs):
            w = (n_windows - 1 - wi) if reverse else wi
            w0 = pl.multiple_of(w * WINDOW, WINDOW)
            out = []
            for j, (tile_re, tile_im) in enumerate(pairs):
                mine = state[per * j:per * (j + 1)]
                carry, acc = window(w0, tile_re, tile_im, mine[0], mine[1], mine[2:] or None)
                out += list(carry) + list(acc or ())
            return tuple(out)

        init = []
        for tile_re, tile_im in pairs:
            init += [carry_ref[:, lanes(tile_re)], carry_ref[:, lanes(tile_im)]]
            if h_ref is not None:
                init += [da_ref[:, lanes(tile_re)], da_ref[:, lanes(tile_im)]]
        fin = lax.fori_loop(0, n_windows, step, tuple(init))
        for j, (tile_re, tile_im) in enumerate(pairs):
            carry_ref[:, lanes(tile_re)] = fin[per * j]
            carry_ref[:, lanes(tile_im)] = fin[per * j + 1]
            if h_ref is not None:
                da_ref[:, lanes(tile_re)] = fin[per * j + 2]
                da_ref[:, lanes(tile_im)] = fin[per * j + 3]


def _put_tiles(ref, sb, block):
    for j in range(SUPER_TILES):
        ref[sb * SUPER_TILES + j] = block[:, j * LANES:(j + 1) * LANES]


def _get_tiles(ref, sb):
    return jnp.concatenate([ref[sb * SUPER_TILES + j] for j in range(SUPER_TILES)], axis=1)


def _ssm_fwd(u, bmat, cmat, tab, d_skip, tb, exchange=None):
    seq = u.shape[0]

    def body(u_ref, b_ref, c_ref, t_ref, d_ref, s_ref, h_ref, carry_ref):
        @pl.when(pl.program_id(0) == 0)
        def _():
            carry_ref[...] = jnp.zeros_like(carry_ref)

        u_blk = u_ref[...]
        ub = _bf(u_blk)
        for sb in range(N_SUPER):
            _put_tiles(h_ref, sb, _mm(ub[:, sb * SUPER_IN:(sb + 1) * SUPER_IN], b_ref[sb]))
        _scan_rows(h_ref, t_ref, carry_ref, tb // WINDOW, False)
        ys = [_mm(_bf(_get_tiles(h_ref, sb)), c_ref[sb]) for sb in range(N_SUPER)]
        s_ref[...] = jnp.concatenate(ys, axis=1) + d_ref[...] * u_blk

    return _rowcall("ssm_fwd", body, seq, tb, [u], [bmat, cmat, tab, d_skip],
                    [(SSM_W, F32), ((STATE_TILES, LANES), F32)], [],
                    scratch=[pltpu.VMEM((SUBLANES, STATE_COLS), F32)], vmem=VMEM_BIG, exchange=exchange)


def _ssm_bwd(ds, u, h, bmat_t, cmat_t, tab, d_skip, tb, exchange=None):
    seq = u.shape[0]

    def body(ds_ref, u_ref, h_ref, bt_ref, ct_ref, t_ref, d_ref,
             du_ref, db_ref, dc_ref, da_ref, dd_ref, g_ref, carry_ref):
        @pl.when(pl.program_id(0) == 0)
        def _():
            carry_ref[...] = jnp.zeros_like(carry_ref)
            db_ref[...] = jnp.zeros_like(db_ref)
            dc_ref[...] = jnp.zeros_like(dc_ref)
            da_ref[...] = jnp.zeros_like(da_ref)
            dd_ref[...] = jnp.zeros_like(dd_ref)

        ds_blk = ds_ref[...]
        dsb = _bf(ds_blk)
        u_blk = u_ref[...]
        ub = _bf(u_blk)
        for sb in range(N_SUPER):
            _put_tiles(g_ref, sb, _mm(dsb[:, sb * SUPER_IN:(sb + 1) * SUPER_IN], ct_ref[sb]))
        _scan_rows(g_ref, t_ref, carry_ref, tb // WINDOW, True, h_ref=h_ref, da_ref=da_ref)
        dus = []
        for sb in range(N_SUPER):
            gb = _bf(_get_tiles(g_ref, sb))
            dus.append(_mm(gb, bt_ref[sb]))
            db_ref[sb] += _mm_tn(ub[:, sb * SUPER_IN:(sb + 1) * SUPER_IN], gb)
            dc_ref[sb] += _mm_tn(_bf(_get_tiles(h_ref, sb)), dsb[:, sb * SUPER_IN:(sb + 1) * SUPER_IN])
        du_ref[...] = jnp.concatenate(dus, axis=1) + d_ref[...] * ds_blk
        dd_ref[...] += jnp.sum(ds_blk * u_blk, axis=0, keepdims=True)

    return _rowcall("ssm_bwd", body, seq, tb, [ds, u, h], [bmat_t, cmat_t, tab, d_skip],
                    [(SSM_W, F32)],
                    [((N_SUPER, SUPER_IN, SUPER_W), F32), ((N_SUPER, SUPER_W, SUPER_IN), F32),
                     ((SUBLANES, STATE_COLS), F32), ((1, SSM_W), F32)],
                    scratch=[pltpu.VMEM((STATE_TILES, tb, LANES), F32), pltpu.VMEM((SUBLANES, STATE_COLS), F32)],
                    reverse=True, vmem=VMEM_BIG, exchange=exchange)


def _merge_core(s, attb, ga, gs, wg_ref, wab_ref, wsb_ref, wout_ref):
    zg, dgelu = _gelu_and_grad(s)
    zgb = _bf(zg)
    sg = _sig(_mm(zgb, wg_ref[...]))
    z = zg * sg
    zb = _bf(z)
    ys = jnp.concatenate([_mm(zb, wsb_ref[j]) for j in range(N_CHIPS)], axis=1)
    ya = jnp.concatenate([_mm(attb, wab_ref[j]) for j in range(N_CHIPS)], axis=1)
    sa = _sig(ga)
    ss = _sig(gs)
    mgb = _bf(sa * ya + ss * ys)
    o = _mm(mgb, wout_ref[...])
    return dict(zg=zg, dgelu=dgelu, zgb=zgb, sg=sg, zb=zb, ys=ys, ya=ya, sa=sa, ss=ss, mgb=mgb, o=o)


def _merge_fwd(x, s, att, ga, gs, g2, w_glu, w_ab, w_sb, w_out, tb, exchange=None):
    seq = x.shape[0]

    def body(x_ref, s_ref, att_ref, ga_ref, gs_ref, g_ref, wg_ref, wab_ref, wsb_ref, wout_ref, x2_ref):
        f = _merge_core(s_ref[...], att_ref[...], ga_ref[...], gs_ref[...], wg_ref, wab_ref, wsb_ref, wout_ref)
        n, _, _ = _rms(f["o"], g_ref[...])
        x2_ref[...] = x_ref[...] + n

    return _rowcall("merge_fwd", body, seq, tb, [x, s, att, ga, gs], [g2, w_glu, w_ab, w_sb, w_out],
                    [(D_MODEL, F32)], [], vmem=VMEM_BIG, exchange=exchange)[0]


def _merge_bwd(dx2, s, att, ga, gs, g2, w_glu, w_ab, w_sb, w_out, tb, exchange=None):
    seq = s.shape[0]
    cw = D_MODEL // N_CHIPS
    last = seq // tb - 1

    def body(dx2_ref, s_ref, att_ref, ga_ref, gs_ref, g_ref, wg_ref, wab_ref, wsb_ref, wout_ref,
             ds_ref, datt_ref, dga_ref, dgs_ref, dg_ref, dwg_ref, dwab_ref, dwsb_ref, dwout_ref,
             bwg_ref, bwab_ref, bwsb_ref, bwout_ref):
        @pl.when(pl.program_id(0) == 0)
        def _():
            for r in (dg_ref, dwg_ref, dwab_ref, dwsb_ref, dwout_ref):
                r[...] = jnp.zeros_like(r)

        attb = att_ref[...]
        f = _merge_core(s_ref[...], attb, ga_ref[...], gs_ref[...], wg_ref, wab_ref, wsb_ref, wout_ref)
        g = g_ref[...]
        _, oh, r2 = _rms(f["o"], g)
        do, dg = _rms_bwd(dx2_ref[...], oh, r2, g)
        dg_ref[...] += dg
        dob = _bf(do)
        dwout_ref[...] += _mm_tn(f["mgb"], dob)
        dmg = _mm_nt(dob, wout_ref[...])
        sa, ss = f["sa"], f["ss"]
        dyab = _bf(dmg * sa)
        dysb = _bf(dmg * ss)
        dga_ref[...] = _bf(dmg * f["ya"] * sa * (1.0 - sa))
        dgs_ref[...] = _bf(dmg * f["ys"] * ss * (1.0 - ss))
        dwab = _mm_tn(attb, dyab)
        dwsb = _mm_tn(f["zb"], dysb)
        datt = jnp.zeros((tb, ATTN_W), F32)
        dz = jnp.zeros((tb, SSM_W), F32)
        for j in range(N_CHIPS):
            dwab_ref[j] += dwab[:, j * cw:(j + 1) * cw]
            dwsb_ref[j] += dwsb[:, j * cw:(j + 1) * cw]
            datt = datt + _mm_nt(dyab[:, j * cw:(j + 1) * cw], wab_ref[j])
            dz = dz + _mm_nt(dysb[:, j * cw:(j + 1) * cw], wsb_ref[j])
        datt_ref[...] = _bf(datt)
        sg, zg = f["sg"], f["zg"]
        dglb = _bf(dz * zg * sg * (1.0 - sg))
        dwg_ref[...] += _mm_tn(f["zgb"], dglb)
        dzg = dz * sg + _mm_nt(dglb, wg_ref[...])
        ds_ref[...] = dzg * f["dgelu"]

        @pl.when(pl.program_id(0) == last)
        def _():
            for dst, src in ((bwg_ref, dwg_ref), (bwab_ref, dwab_ref), (bwsb_ref, dwsb_ref), (bwout_ref, dwout_ref)):
                dst[...] = _bf(src[...])

    shapes = [w_glu.shape, w_ab.shape, w_sb.shape, w_out.shape]
    return _rowcall("merge_bwd", body, seq, tb, [dx2, s, att, ga, gs], [g2, w_glu, w_ab, w_sb, w_out],
                    [(SSM_W, F32), (ATTN_W, BF16), (D_MODEL, BF16), (D_MODEL, BF16)],
                    [((1, D_MODEL), F32)] + [(sh, F32) for sh in shapes] + [(sh, BF16) for sh in shapes],
                    vmem=VMEM_BIG, exchange=exchange)


def _mlp_fwd_loss(x2, target, g3, g4, w_ffi, w_ffo, tb):
    seq = x2.shape[0]
    n_slab = len(w_ffi)
    sw = D_FF // FF_CHUNKS // n_slab

    def body(x2_ref, t_ref, g3_ref, g4_ref, *rest):
        wi_refs, (wo_ref, dy_ref, df_ref, h_ref, ra_ref, loss_ref, dg_ref) = rest[:n_slab], rest[n_slab:]

        @pl.when(pl.program_id(0) == 0)
        def _():
            loss_ref[...] = jnp.zeros_like(loss_ref)
            dg_ref[...] = jnp.zeros_like(dg_ref)

        x2_blk = x2_ref[...]
        h3, _, _ = _rms(x2_blk, g3_ref[...])
        hb = _bf(h3)
        h_ref[...] = hb
        f = jnp.zeros((tb, D_MODEL), F32)
        for j in range(FF_CHUNKS):
            for k in range(n_slab):
                ra = jnp.maximum(_mm(hb, wi_refs[k][j]), 0.0)
                ra_ref[:, pl.ds((j * n_slab + k) * sw, sw)] = _bf(ra)
                f = f + _mm(_bf(ra * ra), wo_ref[j, pl.ds(k * sw, sw), :])
        g4 = g4_ref[...]
        n4, fh, r4 = _rms(f, g4)
        e = (x2_blk + n4) - t_ref[...]
        loss_ref[...] += 0.5 * jnp.sum(jnp.mean(e * e, axis=-1, keepdims=True))
        dy = e * (1.0 / D_MODEL)
        dy_ref[...] = dy
        df, dg = _rms_bwd(dy, fh, r4, g4)
        df_ref[...] = _bf(df)
        dg_ref[...] += dg

    return _rowcall("mlp_fwd_loss", body, seq, tb, [x2, target], [g3, g4, *w_ffi, w_ffo],
                    [(D_MODEL, F32), (D_MODEL, BF16), (D_MODEL, BF16), (D_FF, BF16)],
                    [((SUBLANES, 128), F32), ((1, D_MODEL), F32)], vmem=VMEM_BIG)


def _mlp_bwd(x2, dy, df, ra, g3, w_ffi, w_ffo, tb):
    seq = x2.shape[0]
    n_slab = len(w_ffi)
    sw = D_FF // FF_CHUNKS // n_slab

    def body(x2_ref, dy_ref, df_ref, ra_ref, g3_ref, *rest):
        wi_refs, (wo_ref, dx_ref, da_ref, dg_ref) = rest[:n_slab], rest[n_slab:]

        @pl.when(pl.program_id(0) == 0)
        def _():
            dg_ref[...] = jnp.zeros_like(dg_ref)

        dfb = df_ref[...]
        dh = jnp.zeros((tb, D_MODEL), F32)
        for j in range(FF_CHUNKS):
            for k in range(n_slab):
                cols = pl.ds((j * n_slab + k) * sw, sw)
                ra = ra_ref[:, cols].astype(F32)
                dab = _bf(_mm_nt(dfb, wo_ref[j, pl.ds(k * sw, sw), :]) * (2.0 * ra))
                da_ref[:, cols] = dab
                dh = dh + _mm_nt(dab, wi_refs[k][j])
        g3 = g3_ref[...]
        _, xh, r3 = _rms(x2_ref[...], g3)
        dxn, dg = _rms_bwd(dh, xh, r3, g3)
        dx_ref[...] = dy_ref[...] + dxn
        dg_ref[...] += dg

    return _rowcall("mlp_bwd", body, seq, tb, [x2, dy, df, ra], [g3, *w_ffi, w_ffo],
                    [(D_MODEL, F32), (D_FF, BF16)], [((1, D_MODEL), F32)], vmem=VMEM_BIG)


def _matmul_tn(name, a, b, tk, tn, tl, chunk_major, exchange=None, square_a=False):
    seq, kdim = a.shape
    ndim = b.shape[1]
    last = seq // tl - 1

    def body(a_ref, b_ref, o_ref, ob_ref):
        @pl.when(pl.program_id(2) == 0)
        def _():
            o_ref[...] = jnp.zeros_like(o_ref)

        a_blk = a_ref[...]
        if square_a:
            a_blk = _bf(jnp.square(a_blk.astype(F32)))
        o_ref[...] += _mm_tn(a_blk, b_ref[...])

        @pl.when(pl.program_id(2) == last)
        def _():
            ob_ref[...] = _bf(o_ref[...])

    if chunk_major:
        shape = (ndim // tn, kdim, tn)
        out_spec = pl.BlockSpec((None, tk, tn), lambda k, n, l: (n, k, 0))
    else:
        shape = (kdim, ndim)
        out_spec = pl.BlockSpec((tk, tn), lambda k, n, l: (k, n))
    return _fused_call(
        name, body, (kdim // tk, ndim // tn, seq // tl),
        [pl.BlockSpec((tl, tk), lambda k, n, l: (l, k)), pl.BlockSpec((tl, tn), lambda k, n, l: (l, n))],
        [out_spec, out_spec], [SDS(shape, F32), SDS(shape, BF16)], [], [a, b], exchange, _params(3, VMEM_BIG))


def _ew_call(name, fn, ins, n_out, after=None):
    rows, cols = ins[0].shape
    tr = rows
    while tr * cols * 4 > min(1 << 20, (9 << 20) // (len(ins) + n_out)) and tr % 16 == 0:
        tr //= 2
    spec = pl.BlockSpec((tr, cols), lambda i: (i, 0))
    extra = [] if after is None else [after]

    def body(*refs):
        outs = fn(*[r[...] for r in refs[:len(ins)]])
        for r, o in zip(refs[len(ins) + len(extra):], outs):
            r[...] = o

    return pl.pallas_call(
        body, grid=(rows // tr,), in_specs=[spec] * len(ins) + [ANY] * len(extra), out_specs=[spec] * n_out,
        out_shape=[SDS((rows, cols), F32)] * n_out, name=name, compiler_params=_params(1))(*ins, *extra)


def _adam_math(w, g, m, v):
    m2 = ADAM_B1 * m + (1.0 - ADAM_B1) * g
    v2 = ADAM_B2 * v + (1.0 - ADAM_B2) * (g * g)
    m_hat = m2 / (1.0 - ADAM_B1 ** ADAM_STEP)
    v_hat = v2 / (1.0 - ADAM_B2 ** ADAM_STEP)
    delta = -ADAM_LR * (m_hat / (jnp.sqrt(v_hat) + ADAM_EPS) + ADAM_WD * w)
    return delta, m2, v2


def _sum4(name, own, recv, idx):
    _, rows, cols = own.shape
    tr = rows
    while tr * cols * 4 > (1 << 20) and tr % 16 == 0:
        tr //= 2

    def body(idx_ref, o_ref, r0_ref, r1_ref, r2_ref, out_ref):
        out_ref[...] = ((o_ref[...] + r0_ref[...].astype(F32)) + r1_ref[...].astype(F32)) + r2_ref[...].astype(F32)

    blk = (None, tr, cols)
    grid_spec = pltpu.PrefetchScalarGridSpec(
        num_scalar_prefetch=1, grid=(rows // tr,),
        in_specs=[pl.BlockSpec(blk, lambda i, s: (s[0], i, 0)), pl.BlockSpec(blk, lambda i, s: (0, i, 0)),
                  pl.BlockSpec(blk, lambda i, s: (1, i, 0)), pl.BlockSpec(blk, lambda i, s: (2, i, 0))],
        out_specs=pl.BlockSpec((tr, cols), lambda i, s: (i, 0)))
    return pl.pallas_call(body, grid_spec=grid_spec, out_shape=SDS((rows, cols), F32), name=name,
                          compiler_params=_params(1))(jnp.reshape(idx, (1,)).astype(jnp.int32), own, recv, recv, recv)


def _sum4_group(name, owns, recvs, idx):
    n = len(owns)

    def body(idx_ref, *refs):
        for a in range(n):
            o_ref, r0_ref, r1_ref, r2_ref = refs[4 * a:4 * a + 4]
            refs[4 * n + a][...] = (((o_ref[...] + r0_ref[...].astype(F32)) + r1_ref[...].astype(F32))
                                    + r2_ref[...].astype(F32))

    in_specs, args = [], []
    for own, recv in zip(owns, recvs):
        blk = (None,) + own.shape[1:]
        in_specs += [pl.BlockSpec(blk, lambda i, s: (s[0], 0, 0))]
        in_specs += [pl.BlockSpec(blk, lambda i, s, k=k: (k, 0, 0)) for k in range(3)]
        args += [own, recv, recv, recv]
    grid_spec = pltpu.PrefetchScalarGridSpec(
        num_scalar_prefetch=1, grid=(1,), in_specs=in_specs,
        out_specs=[pl.BlockSpec(o.shape[1:], lambda i, s: (0, 0)) for o in owns])
    return pl.pallas_call(body, grid_spec=grid_spec, out_shape=[SDS(o.shape[1:], F32) for o in owns], name=name,
                          compiler_params=_params(1, VMEM_BIG))(jnp.reshape(idx, (1,)).astype(jnp.int32), *args)


def _adam_group(name, items, after):
    n = len(items)
    flat = [t for item in items for t in item]

    def body(*refs):
        outs = refs[5 * n + 1:]
        for a in range(n):
            w_, p, q, m_, v_ = [r[...] for r in refs[5 * a:5 * a + 5]]
            g = p + q
            for r, o in zip(outs[4 * a:4 * a + 4], (g,) + _adam_math(w_, g, m_, v_)):
                r[...] = o

    whole = lambda t: pl.BlockSpec(t.shape, lambda i: (0, 0))
    outs = pl.pallas_call(
        body, grid=(1,), in_specs=[whole(t) for t in flat] + [ANY],
        out_specs=[whole(item[0]) for item in items for _ in range(4)],
        out_shape=[SDS(item[0].shape, F32) for item in items for _ in range(4)], name=name,
        compiler_params=_params(1, VMEM_BIG))(*flat, after)
    return [outs[4 * a:4 * a + 4] for a in range(n)]


def _sum4_swap(name, own, recv, idx, n_blocks=4):
    _, rows, cols = own.shape
    tr = rows // n_blocks
    assert tr * n_blocks == rows and tr % SUBLANES == 0

    def body(idx_ref, o_ref, r0_ref, r1_ref, r2_ref, mine_ref, theirs_ref, buf, kept, sent, arrived):
        i = pl.program_id(0)
        slot = lax.rem(i, 2)
        x, y, c = _place()

        def copies(j, s):
            block = pl.ds(j * tr, tr)
            return (pltpu.make_async_copy(buf.at[s], mine_ref.at[block], kept.at[s]),
                    pltpu.make_async_remote_copy(
                        src_ref=buf.at[s], dst_ref=theirs_ref.at[block], send_sem=sent.at[s], recv_sem=arrived.at[j],
                        device_id=(x, y, 1 - c), device_id_type=MESH_ID))

        def finish(j, s):
            keep, send = copies(j, s)
            keep.wait()
            send.wait_send()
            send.wait_recv()

        @pl.when(i >= 2)
        def _():
            finish(i - 2, slot)

        buf[slot] = ((o_ref[...] + r0_ref[...].astype(F32)) + r1_ref[...].astype(F32)) + r2_ref[...].astype(F32)
        for cp in copies(i, slot):
            cp.start()

        @pl.when(i == n_blocks - 1)
        def _():
            if n_blocks > 1:
                finish(i - 1, 1 - slot)
            finish(i, slot)

    blk = (None, tr, cols)
    grid_spec = pltpu.PrefetchScalarGridSpec(
        num_scalar_prefetch=1, grid=(n_blocks,),
        in_specs=[pl.BlockSpec(blk, lambda i, s: (s[0], i, 0)), pl.BlockSpec(blk, lambda i, s: (0, i, 0)),
                  pl.BlockSpec(blk, lambda i, s: (1, i, 0)), pl.BlockSpec(blk, lambda i, s: (2, i, 0))],
        out_specs=[HBM, HBM],
        scratch_shapes=[pltpu.VMEM((2, tr, cols), F32), pltpu.SemaphoreType.DMA((2,)), pltpu.SemaphoreType.DMA((2,)),
                        pltpu.SemaphoreType.DMA((n_blocks,))])
    return pl.pallas_call(body, grid_spec=grid_spec, out_shape=[SDS((rows, cols), F32)] * 2, name=name,
                          compiler_params=_params(1))(jnp.reshape(idx, (1,)).astype(jnp.int32), own, recv, recv, recv)


def _adam_pair(name, item, after=None):
    def fn(w_, a, b, m_, v_):
        g = a + b
        return (g,) + _adam_math(w_, g, m_, v_)

    return _ew_call(name, fn, list(item), 4, after)


def _place():
    return lax.axis_index("x"), lax.axis_index("y"), lax.axis_index("c")


def _other_chips(x, y):
    return [(1 - x, y), (x, 1 - y), (1 - x, 1 - y)]


HBM = pl.BlockSpec(memory_space=pltpu.HBM)
SEM = pl.BlockSpec(memory_space=pltpu.SEMAPHORE)
DATAFLOW = pltpu.SideEffectType.DATAFLOW_SIDE_EFFECTING


class _Flight:
    def __init__(self, copies, n_copies, send, recv, srcs, lands, token):
        self.copies, self.n, self.send, self.recv = copies, n_copies, send, recv
        self.srcs, self.lands, self.token = list(srcs), list(lands), token


def _take_off(name, srcs, lands, copies, n_copies, after):
    n_s, n_l = len(srcs), len(lands)

    def body(*refs):
        src, land = refs[:n_s], refs[n_s:n_s + n_l]
        send, recv = refs[n_s + n_l + 1:n_s + n_l + 3]
        for cp in copies(src, land, send, recv):
            cp.start()
        refs[-1][...] = jnp.zeros_like(refs[-1])

    mem = lambda t: pltpu.HBM(t.shape, t.dtype)
    sems = pltpu.SemaphoreType.DMA((n_copies,))
    outs = pl.pallas_call(
        body, name=name,
        out_shape=(sems, sems, *map(mem, srcs), *map(mem, lands), SDS((SUBLANES, LANES), F32)),
        in_specs=[HBM] * (n_s + n_l) + [ANY],
        out_specs=(SEM, SEM, *[HBM] * (n_s + n_l), pl.BlockSpec(memory_space=pltpu.VMEM)),
        input_output_aliases={i: 2 + i for i in range(n_s + n_l)},
        compiler_params=pltpu.CompilerParams(has_side_effects=DATAFLOW),
    )(*[pltpu.with_memory_space_constraint(t, pltpu.HBM) for t in (*srcs, *lands)], after)
    return _Flight(copies, n_copies, outs[0], outs[1], outs[2:2 + n_s], outs[2 + n_s:2 + n_s + n_l], outs[-1])


def _land(name, flight, after):
    n_s, n_l = len(flight.srcs), len(flight.lands)

    def body(*refs):
        src, land = refs[:n_s], refs[n_s:n_s + n_l]
        send, recv = refs[n_s + n_l:n_s + n_l + 2]
        for cp in flight.copies(src, land, send, recv):
            cp.wait_send()
            cp.wait_recv()

    mem = lambda t: pltpu.HBM(t.shape, t.dtype)
    outs = pl.pallas_call(
        body, name=name, out_shape=(*map(mem, flight.srcs), *map(mem, flight.lands)),
        in_specs=[HBM] * (n_s + n_l) + [SEM, SEM, ANY], out_specs=tuple([HBM] * (n_s + n_l)),
        input_output_aliases={i: i for i in range(n_s + n_l)},
        compiler_params=pltpu.CompilerParams(has_side_effects=DATAFLOW),
    )(*flight.srcs, *flight.lands, flight.send, flight.recv, after)
    return list(outs[:n_s]), list(outs[n_s:])


def _empty_like(shapes_from, lead):
    return [lax.empty((lead,) + t.shape[1:], t.dtype) for t in shapes_from]


def _scatter_off(name, chunks, after):
    def copies(src, land, send, recv):
        x, y, c = _place()
        return [pltpu.make_async_remote_copy(
            src_ref=src[a].at[2 * px + py], dst_ref=land[a].at[k], send_sem=send.at[3 * a + k],
            recv_sem=recv.at[3 * a + k], device_id=(px, py, c), device_id_type=MESH_ID)
            for a in range(len(chunks)) for k, (px, py) in enumerate(_other_chips(x, y))]

    return _take_off(name, chunks, _empty_like(chunks, 3), copies, 3 * len(chunks), after)


def _swap_off(name, arrs, after):
    def copies(src, land, send, recv):
        x, y, c = _place()
        return [pltpu.make_async_remote_copy(
            src_ref=src[a], dst_ref=land[a], send_sem=send.at[a], recv_sem=recv.at[a],
            device_id=(x, y, 1 - c), device_id_type=MESH_ID) for a in range(len(arrs))]

    return _take_off(name, arrs, [lax.empty(t.shape, t.dtype) for t in arrs], copies, len(arrs), after)


def _devices_off(name, block, after):
    me = 4 * lax.axis_index("x") + 2 * lax.axis_index("y") + lax.axis_index("c")
    land = lax.dynamic_update_index_in_dim(lax.empty((N_DEV,) + block.shape, block.dtype), block, me, 0)

    def copies(src, land, send, recv):
        x, y, c = _place()
        mine = 4 * x + 2 * y + c
        return [pltpu.make_async_remote_copy(
            src_ref=src[0], dst_ref=land[0].at[mine], send_sem=send.at[k - 1], recv_sem=recv.at[k - 1],
            device_id=(x ^ (k >> 2), y ^ ((k >> 1) & 1), c ^ (k & 1)), device_id_type=MESH_ID)
            for k in range(1, N_DEV)]

    return _take_off(name, [block], [land], copies, N_DEV - 1, after)


def _half_rows(shape, c, other=False):
    half = shape[0] // 2
    return pl.ds(((1 - c) if other else c) * half, half)


def _gather_start(name, shards, lands, after):
    n = len(shards)

    def body(*refs):
        src, land, (send, recv) = refs[:n], refs[n:2 * n], refs[2 * n + 1:2 * n + 3]
        x, y, c = _place()
        me = 2 * x + y
        for a in range(n):
            mine = _half_rows(shards[a].shape, c)
            for j, (px, py) in enumerate(_other_chips(x, y)):
                pltpu.make_async_remote_copy(
                    src_ref=src[a].at[mine], dst_ref=land[a].at[me, mine], send_sem=send.at[3 * a + j],
                    recv_sem=recv.at[3 * a + j], device_id=(px, py, c), device_id_type=MESH_ID).start()
        token = refs[-1]
        token[...] = jnp.zeros_like(token)

    mem = lambda t: pltpu.HBM(t.shape, t.dtype)
    pair = pltpu.SemaphoreType.DMA((3 * n,))
    outs = pl.pallas_call(
        body, name=name,
        out_shape=(pair, pair, *map(mem, shards), *map(mem, lands), SDS((SUBLANES, LANES), F32)),
        in_specs=[HBM] * (2 * n) + [ANY],
        out_specs=(SEM, SEM, *[HBM] * (2 * n), pl.BlockSpec(memory_space=pltpu.VMEM)),
        input_output_aliases={i: 2 + i for i in range(2 * n)},
        compiler_params=pltpu.CompilerParams(has_side_effects=DATAFLOW),
    )(*[pltpu.with_memory_space_constraint(t, pltpu.HBM) for t in (*shards, *lands)], after)
    return outs[0], outs[1], list(outs[2:2 + n]), list(outs[2 + n:2 + 2 * n]), outs[-1]


def _gather_pass(name, send, recv, shards, lands, after, first=0):
    n = len(shards)

    def body(*refs):
        src, land, (send, recv, _) = refs[:n], refs[n:2 * n], refs[2 * n:2 * n + 3]
        fsend, frecv = refs[2 * n + 3], refs[2 * n + 4]
        x, y, c = _place()
        me = 2 * x + y
        for a in range(n):
            mine = _half_rows(shards[a].shape, c)
            for j, (px, py) in enumerate(_other_chips(x, y)):
                far = 2 * px + py
                ici = pltpu.make_async_remote_copy(
                    src_ref=src[a].at[mine], dst_ref=land[a].at[far, mine], send_sem=send.at[3 * (first + a) + j],
                    recv_sem=recv.at[3 * (first + a) + j], device_id=(px, py, c), device_id_type=MESH_ID)
                ici.wait_recv()
                ici.wait_send()
                pltpu.make_async_remote_copy(
                    src_ref=land[a].at[far, mine], dst_ref=land[a].at[far, mine], send_sem=fsend.at[3 * a + j],
                    recv_sem=frecv.at[3 * a + j], device_id=(x, y, 1 - c), device_id_type=MESH_ID).start()
        token = refs[-1]
        token[...] = jnp.zeros_like(token)

    mem = lambda t: pltpu.HBM(t.shape, t.dtype)
    pair = pltpu.SemaphoreType.DMA((3 * n,))
    outs = pl.pallas_call(
        body, name=name,
        out_shape=(pair, pair, *map(mem, lands), SDS((SUBLANES, LANES), F32)),
        in_specs=[HBM] * (2 * n) + [SEM, SEM, ANY],
        out_specs=(SEM, SEM, *[HBM] * n, pl.BlockSpec(memory_space=pltpu.VMEM)),
        input_output_aliases={n + i: 2 + i for i in range(n)},
        compiler_params=pltpu.CompilerParams(has_side_effects=DATAFLOW),
    )(*shards, *lands, send, recv, after)
    return outs[0], outs[1], list(outs[2:2 + n]), outs[-1]


def _gather_wait(name, fsend, frecv, lands, after):
    n = len(lands)

    def body(*refs):
        land, (fsend, frecv, _) = refs[:n], refs[n:n + 3]
        x, y, c = _place()
        for a in range(n):
            for j, (px, py) in enumerate(_other_chips(x, y)):
                far = 2 * px + py
                mine = _half_rows(lands[a].shape[1:], c)
                theirs = _half_rows(lands[a].shape[1:], c, other=True)
                pltpu.make_async_remote_copy(
                    src_ref=land[a].at[far, mine], dst_ref=land[a].at[far, mine], send_sem=fsend.at[3 * a + j],
                    recv_sem=frecv.at[3 * a + j], device_id=(x, y, 1 - c), device_id_type=MESH_ID).wait_send()
                pltpu.make_async_remote_copy(
                    src_ref=land[a].at[far, theirs], dst_ref=land[a].at[far, theirs], send_sem=fsend.at[3 * a + j],
                    recv_sem=frecv.at[3 * a + j], device_id=(x, y, 1 - c), device_id_type=MESH_ID).wait_recv()

    mem = lambda t: pltpu.HBM(t.shape, t.dtype)
    return list(pl.pallas_call(
        body, name=name, out_shape=tuple(map(mem, lands)), in_specs=[HBM] * n + [SEM, SEM, ANY],
        out_specs=tuple([HBM] * n), input_output_aliases={i: i for i in range(n)},
        compiler_params=pltpu.CompilerParams(has_side_effects=DATAFLOW),
    )(*lands, fsend, frecv, after))


def _after(token):
    return _Exchange([token], [], [], lambda *_: None, lambda *_: None)


def _sum_devices(slots):
    def body(s_ref, o_ref):
        acc = s_ref[0]
        for d in range(1, N_DEV):
            acc = acc + s_ref[d]
        o_ref[...] = acc

    return pl.pallas_call(
        body, in_specs=[pl.BlockSpec(memory_space=pltpu.VMEM)], out_specs=pl.BlockSpec(memory_space=pltpu.VMEM),
        out_shape=SDS(slots.shape[1:], F32), name="sum_small",
        compiler_params=pltpu.CompilerParams(vmem_limit_bytes=32 * 1024 * 1024))(slots)


def _adam_small(ws, gs, ms, vs):
    n = len(ws)

    def body(*refs):
        for i in range(n):
            w_ref, g_ref, m_ref, v_ref = (refs[k * n + i] for k in range(4))
            outs = _adam_math(w_ref[...], g_ref[...], m_ref[...], v_ref[...])
            for k in range(3):
                refs[(4 + k) * n + i][...] = outs[k]

    vmem = pl.BlockSpec(memory_space=pltpu.VMEM)
    return pl.pallas_call(
        body, in_specs=[vmem] * (4 * n), out_specs=[vmem] * (3 * n),
        out_shape=[SDS(w.shape, F32) for w in ws] * 3, name="adam_small",
        compiler_params=pltpu.CompilerParams(vmem_limit_bytes=32 * 1024 * 1024))(*ws, *gs, *ms, *vs)


def _local_step(x, target, small, big, tb, distributed):
    dist = distributed
    me = (2 * lax.axis_index("x") + lax.axis_index("y")) if dist else 0
    tb_ssm = min(tb, 256)
    bucket = jnp.asarray(_bucket_table())
    place_own = lambda t: lax.dynamic_update_index_in_dim(lax.empty((N_CHIPS,) + t.shape, t.dtype), t, me, 0)
    if dist:
        in_legs = _gather_start("gather_in_start", [big["w_in"]], [place_own(big["w_in"])], small["d_skip"])
        names = sorted(small)
        in_token, values = lax.optimization_barrier((in_legs[4], [small[n] for n in names]))
        small = dict(zip(names, values))
    g1, g2, g3, g4 = small["norm_mix_pre"], small["norm_mix_post"], small["norm_mlp_pre"], small["norm_mlp_post"]

    keys_first = lambda t: jnp.swapaxes(t, -1, -2)
    bias = _bias_table(small["rel_bias"], bucket)
    sink_rows = keys_first(_pair_layout(jnp.broadcast_to(small["sinks"].reshape(N_HEADS, 1, 1), (N_HEADS, BLOCK, 1))))
    disc_args = (small["lam_re"], small["lam_im"], small["log_dt"], small["b_re"], small["b_im"])
    (ab_re, ab_im, bb_re, bb_im), disc_vjp = jax.vjp(_ssm_discretize, *disc_args)
    tab_f, tab_b = _scan_tables(ab_re, ab_im)
    bmat = _bf(_b_matrix(bb_re, bb_im))
    cmat = _bf(_c_matrix(small["c_re"], small["c_im"]))
    bmat_t, cmat_t = bmat.transpose(0, 2, 1), cmat.transpose(0, 2, 1)
    d_skip = small["d_skip"]

    mix = ("w_glu", "w_attn_branch", "w_ssm_branch", "w_out")
    rest = [big[n] for n in mix + ("w_ff_in", "w_ff_out")]
    if dist:
        send, recv, src, lands, _ = in_legs
        tab_f, tab_b, bias, sink_rows, bmat, cmat, bmat_t, cmat_t, rest, rest_lands = lax.optimization_barrier(
            (tab_f, tab_b, bias, sink_rows, bmat, cmat, bmat_t, cmat_t, rest, [place_own(t) for t in rest]))
        corner = lambda t: t.reshape(-1, t.shape[-1])[:1, :LANES].astype(F32)
        prepared = sum(map(corner, [tab_b, bias, sink_rows, bmat, cmat] + rest_lands), in_token[:1])
        in_send, in_recv, in_lands, in_passed = _gather_pass("gather_in_pass", send, recv, src, lands, prepared)
    token = None
    n_mix = len(mix)
    if dist:
        send, recv, rest, lands, token = _gather_start("gather_rest_start", rest, rest_lands, in_passed)
        (g_in,) = _gather_wait("gather_in_wait", in_send, in_recv, in_lands, token)
        w_in = g_in.reshape(IN_W, D_MODEL)
    else:
        w_in = big["w_in"]
    h1, q, k, v, u, ga, gs = _inproj_fwd(x, g1, w_in, tb)
    s, h = _ssm_fwd(u, bmat, cmat, tab_f, d_skip, tb)
    if dist:
        fsend, frecv, mix_lands, token = _gather_pass("gather_mix_pass", send, recv, rest[:n_mix], lands[:n_mix], s)
    att = _attn_fwd(q, k, v, bias, sink_rows, _after(token) if dist else None)[0]
    if dist:
        w_mix = _gather_wait("gather_mix_wait", fsend, frecv, mix_lands, att)
        fsend, frecv, ff_lands, token = _gather_pass(
            "gather_ff_pass", send, recv, rest[n_mix:], lands[n_mix:], w_mix[0], n_mix)
        rest = w_mix + ff_lands
    w_glu, w_ab, w_sb, w_out = rest[:n_mix]
    w_glu = w_glu.reshape(SSM_W, SSM_W)
    w_out = w_out.reshape(D_MODEL, D_MODEL)
    x2 = _merge_fwd(x, s, att, ga, gs, g2, w_glu, w_ab, w_sb, w_out, tb, _after(token) if dist else None)
    if dist:
        rest[n_mix:] = _gather_wait("gather_ff_wait", fsend, frecv, ff_lands, x2)
    w_ffi, w_ffo = [rest[n_mix]], rest[n_mix + 1]
    dy, df, h3, ra, loss_acc, dg4 = _mlp_fwd_loss(x2, target, g3, g4, w_ffi, w_ffo, tb)

    dx2, da, dg3 = _mlp_bwd(x2, dy, df, ra, g3, w_ffi, w_ffo, tb)
    tl = min(2048, x.shape[0])
    chunked = (N_CHIPS, D_FF // N_CHIPS, D_MODEL)
    d_ffi, b_ffi = _matmul_tn("grad_w_ff_in", h3, da, D_MODEL, D_FF // FF_CHUNKS, tl, True)
    d_ffo, b_ffo = _matmul_tn("grad_w_ff_out", ra, df, D_FF // FF_CHUNKS, D_MODEL, tl, False, square_a=True)
    d_ffo, b_ffo = d_ffo.reshape(chunked), b_ffo.reshape(chunked)
    behind = lambda flight: _after(flight.token) if dist else None
    ff_fl = _scatter_off("scatter_ff_off", [b_ffi, b_ffo], d_ffo) if dist else None
    outs = _merge_bwd(dx2, s, att, ga, gs, g2, w_glu, w_ab, w_sb, w_out, tb_ssm, behind(ff_fl))
    ds, datt, dga, dgs, dg2, d_glu, d_ab, d_sb, d_out, b_glu, b_ab, b_sb, b_out = outs
    glu4, out4 = (N_CHIPS, SSM_W // N_CHIPS, SSM_W), (N_CHIPS, D_MODEL // N_CHIPS, D_MODEL)
    d_mix = [d_glu.reshape(glu4), d_ab, d_sb, d_out.reshape(out4)]
    b_mix = [b_glu.reshape(glu4), b_ab, b_sb, b_out.reshape(out4)]
    mix_fl = _scatter_off("scatter_mix_off", b_mix, d_mix[-1]) if dist else None
    du, d_bmat, d_cmat, da_acc, dd_skip = _ssm_bwd(
        ds, u, h, bmat_t, cmat_t, tab_b, d_skip, tb, behind(mix_fl))
    dq, dk, dv, dbias, dsink_rows = _attn_bwd(q, k, v, datt, bias, sink_rows)
    dx, dpj, dg1 = _inproj_bwd(x, dx2, dq, dk, dv, du, dga, dgs, g1, w_in, tb)

    dab_re, dab_im = _state_unlayout(jnp.sum(da_acc, axis=0))
    dbb_re, dbb_im = _b_matrix_grad(d_bmat)
    d_lam_re, d_lam_im, d_log_dt, d_b_re, d_b_im = disc_vjp((dab_re, dab_im, dbb_re, dbb_im))
    d_c_re, d_c_im = _c_matrix_grad(d_cmat)
    d_rel = _bias_grad(dbias, bucket)
    d_sinks = jnp.sum(_pair_unlayout(keys_first(dsink_rows)), axis=(1, 2))
    small_grads = dict(
        norm_mix_pre=dg1, norm_mix_post=dg2, norm_mlp_pre=dg3, norm_mlp_post=dg4, rel_bias=d_rel, sinks=d_sinks,
        lam_re=d_lam_re, lam_im=d_lam_im, log_dt=d_log_dt, b_re=d_b_re, b_im=d_b_im, c_re=d_c_re, c_im=d_c_im,
        d_skip=dd_skip)
    small_fl = _devices_off("small_off", _pack(small_grads, loss_acc), dg1) if dist else None
    outs = _matmul_tn("grad_w_in", dpj, h1, IN_W // 2, D_MODEL, tl, False, behind(small_fl))
    in4 = (N_CHIPS, IN_W // N_CHIPS, D_MODEL)
    d_in, b_in = outs[0].reshape(in4), outs[1].reshape(in4)
    if not dist:
        return loss_acc, dx, small_grads, dict(zip(BIG, [d_in] + d_mix + [d_ffi, d_ffo]))
    in_fl = _scatter_off("scatter_w_in_off", [b_in], d_in)
    r_ffi, r_ffo = _land("scatter_ff_land", ff_fl, in_fl.token)[1]
    p_ffi = _sum4("sum_w_ff_in", d_ffi, r_ffi, me)
    p_ffo = _sum4("sum_w_ff_out", d_ffo, r_ffo, me)
    swap_fl = _swap_off("swap_ff_off", [p_ffi, p_ffo], r_ffo)
    r_mix = _land("scatter_mix_land", mix_fl, swap_fl.token)[1]
    p_mix = _sum4_group("sum_mix", d_mix, r_mix, me)
    mix_swap = _swap_off("swap_mix_off", p_mix, swap_fl.token)
    pending = dict(d_in=d_in, in_fl=in_fl, mix_swap=mix_swap, ff_swap=swap_fl, me=me)
    return loss_acc, dx, small_fl, pending


SMALL = ['norm_mix_pre', 'norm_mix_post', 'norm_mlp_pre', 'norm_mlp_post', 'rel_bias', 'sinks', 'lam_re', 'lam_im',
         'log_dt', 'b_re', 'b_im', 'c_re', 'c_im', 'd_skip']
BIG = ['w_in', 'w_glu', 'w_attn_branch', 'w_ssm_branch', 'w_out', 'w_ff_in', 'w_ff_out']
WEIGHTS = ['norm_mix_pre', 'norm_mix_post', 'norm_mlp_pre', 'norm_mlp_post', 'w_in', 'rel_bias', 'sinks', 'lam_re',
           'lam_im', 'log_dt', 'b_re', 'b_im', 'c_re', 'c_im', 'd_skip', 'w_glu', 'w_attn_branch', 'w_ssm_branch',
           'w_out', 'w_ff_in', 'w_ff_out']
PACK_COLS = 1024
PACK_ORDER = ['b_re', 'b_im', 'c_re', 'c_im', 'lam_re', 'lam_im', 'norm_mix_pre', 'norm_mix_post', 'norm_mlp_pre',
              'norm_mlp_post', 'rel_bias', 'sinks', 'log_dt', 'd_skip']


STATE_MINOR = ('b_re', 'b_im')
PACK_ROWS = 144
LOSS_ROW = 140


def _pack(named, loss_acc):
    parts = []
    for n in PACK_ORDER:
        a = jnp.swapaxes(named[n], -1, -2) if n in STATE_MINOR else named[n]
        flat = a.reshape(-1)
        rows = -(-flat.shape[0] // PACK_COLS)
        parts.append(jnp.pad(flat, (0, rows * PACK_COLS - flat.shape[0])).reshape(rows, PACK_COLS))
    assert sum(p.shape[0] for p in parts) == LOSS_ROW
    parts.append(jnp.pad(loss_acc[0:1], ((0, PACK_ROWS - LOSS_ROW - 1), (0, PACK_COLS - loss_acc.shape[1]))))
    return jnp.concatenate(parts, axis=0)


def _unpack(packed, shapes):
    out, at = {}, 0
    for n in PACK_ORDER:
        shape = shapes[n][:-2] + (shapes[n][-1], shapes[n][-2]) if n in STATE_MINOR else shapes[n]
        size = int(np.prod(shape))
        rows = -(-size // PACK_COLS)
        blk = packed[at:at + rows]
        out[n] = (blk.reshape(-1)[:size] if size % PACK_COLS else blk).reshape(shape)
        at += rows
    return out


def kernel(x, norm_mix_pre, norm_mix_post, norm_mlp_pre, norm_mlp_post, w_in, rel_bias, sinks, lam_re, lam_im, log_dt, b_re, b_im, c_re, c_im, d_skip, w_glu, w_attn_branch, w_ssm_branch, w_out, w_ff_in, w_ff_out, loss_target, m_norm_mix_pre, m_norm_mix_post, m_norm_mlp_pre, m_norm_mlp_post, m_w_in, m_rel_bias, m_sinks, m_lam_re, m_lam_im, m_log_dt, m_b_re, m_b_im, m_c_re, m_c_im, m_d_skip, m_w_glu, m_w_attn_branch, m_w_ssm_branch, m_w_out, m_w_ff_in, m_w_ff_out, v_norm_mix_pre, v_norm_mix_post, v_norm_mlp_pre, v_norm_mlp_post, v_w_in, v_rel_bias, v_sinks, v_lam_re, v_lam_im, v_log_dt, v_b_re, v_b_im, v_c_re, v_c_im, v_d_skip, v_w_glu, v_w_attn_branch, v_w_ssm_branch, v_w_out, v_w_ff_in, v_w_ff_out):
    env = dict(locals())
    w = {n: env[n] for n in WEIGHTS}
    m = {n: env["m_" + n] for n in WEIGHTS}
    v = {n: env["v_" + n] for n in WEIGHTS}
    seq = x.shape[1]
    tb = min(512, seq)

    small = {n: w[n] for n in ('norm_mix_pre', 'norm_mix_post', 'norm_mlp_pre', 'norm_mlp_post', 'rel_bias')}
    small.update({n: w[n][0] for n in ('sinks', 'lam_re', 'lam_im', 'log_dt', 'b_re', 'b_im', 'c_re', 'c_im')})
    small['d_skip'] = w['d_skip']
    shard = lambda t, n: t[n][0].T if n == 'w_in' else t[n][0]
    unshard = lambda a, n: (a.T if n == 'w_in' else a)[None]
    _, dx, small_fl, pending = _local_step(
        x[0], loss_target[0], small, {n: _bf(shard(w, n)) for n in BIG}, tb, True)

    grads, deltas, new_m, new_v = {}, {}, {}, {}

    def adam(n, partials, after=None):
        outs = _adam_pair("adam_" + n, (shard(w, n), *partials, shard(m, n), shard(v, n)), after)
        grads[n], deltas[n], new_m[n], new_v[n] = [unshard(a, n) for a in outs]
        return outs[3]

    mix = ("w_glu", "w_attn_branch", "w_ssm_branch", "w_out")
    in_fl = pending["in_fl"]

    small_g = _sum_devices(_land("small_land", small_fl, pending["mix_swap"].token)[1][0])
    loss = small_g[LOSS_ROW, 0]
    minor = lambda t, n: jnp.swapaxes(t, -1, -2) if n in STATE_MINOR else t
    g_small = _unpack(small_g, {n: w[n].shape for n in SMALL})
    outs = _adam_small([minor(w[n], n) for n in SMALL], [g_small[n] for n in SMALL],
                       [minor(m[n], n) for n in SMALL], [minor(v[n], n) for n in SMALL])
    grads.update({n: minor(g_small[n], n) for n in SMALL})
    for k, dst in enumerate((deltas, new_m, new_v)):
        dst.update({n: minor(a, n) for n, a in zip(SMALL, outs[k * len(SMALL):(k + 1) * len(SMALL)])})

    last = outs[0]
    own_ff, sib_ff = _land("swap_ff_land", pending["ff_swap"], last)
    for n, partials in zip(("w_ff_in", "w_ff_out"), zip(own_ff, sib_ff)):
        last = adam(n, partials, last)
    own_mix, sib_mix = _land("swap_mix_land", pending["mix_swap"], last)
    outs = _adam_group("adam_mix", [(shard(w, n), p, s, shard(m, n), shard(v, n))
                                    for n, p, s in zip(mix, own_mix, sib_mix)], last)
    for n, item in zip(mix, outs):
        grads[n], deltas[n], new_m[n], new_v[n] = [unshard(a, n) for a in item]

    (r_in,) = _land("scatter_w_in_land", in_fl, outs[-1][3])[1]
    adam("w_in", _sum4_swap("sum_swap_w_in", pending["d_in"], r_in, pending["me"]))

    return (loss, dx[None], *[grads[n] for n in WEIGHTS], *[deltas[n] for n in WEIGHTS],
            *[new_m[n] for n in WEIGHTS], *[new_v[n] for n in WEIGHTS])
```

```python
import functools
import math

import numpy as np
import jax
import jax.numpy as jnp
from jax import lax
from jax.experimental import pallas as pl
from jax.experimental.pallas import tpu as pltpu

F32 = jnp.float32
BF16 = jnp.bfloat16

D_MODEL = 1024
N_HEADS = 8
N_KV = 2
Q_GROUP = 4
HEAD_DIM = 64
ATTN_W = 512
KV_W = 128
BLOCK = 128
N_BUCKETS = 32
MAX_DISTANCE = 128
NEG_INF = -1e30
SSM_W = 512
SSM_GROUP = 16
SSM_GROUPS = 32
SSM_STATE = 64
N_SUPER = 4
GROUPS_PER_SUPER = SSM_GROUPS // N_SUPER
SUPER_IN = GROUPS_PER_SUPER * SSM_GROUP
SUPER_HALF = GROUPS_PER_SUPER * SSM_STATE
SUPER_W = 2 * SUPER_HALF
STATE_COLS = N_SUPER * SUPER_W
D_FF = 4096
FF_CHUNKS = 4
IN_W = 3328
SPLITS = (0, 512, 640, 768, 1280, 2304, 3328)
RMS_EPS = 1e-6
N_CHIPS = 4
N_DEV = 8
SUBLANES = 8
LANES = 128
STATE_TILES = STATE_COLS // LANES
SUPER_TILES = SUPER_W // LANES

ADAM_LR = 0.001
ADAM_B1 = 0.9
ADAM_B2 = 0.999
ADAM_EPS = 1e-08
ADAM_WD = 0.01
ADAM_STEP = 10

VMEM_BIG = 56 * 1024 * 1024
SDS = jax.ShapeDtypeStruct
MESH_ID = pl.DeviceIdType.MESH
ANY = pl.BlockSpec(memory_space=pl.ANY)


def _bf(x):
    return x.astype(BF16)


def _mm(a, b):
    return jnp.dot(a, b, preferred_element_type=F32)


def _mm_nt(a, b):
    return lax.dot_general(a, b, (((1,), (1,)), ((), ())), preferred_element_type=F32)


def _mm_tn(a, b):
    return lax.dot_general(a, b, (((0,), (0,)), ((), ())), preferred_element_type=F32)


def _sig(x):
    return 1.0 / (1.0 + jnp.exp(-x))


def _rms(x, g):
    r = lax.rsqrt(jnp.mean(x * x, axis=-1, keepdims=True) + RMS_EPS)
    xh = x * r
    return xh * g, xh, r


def _rms_bwd(dout, xh, r, g):
    dg = jnp.sum(dout * xh, axis=0, keepdims=True)
    dxh = dout * g
    dx = r * (dxh - xh * jnp.mean(dxh * xh, axis=-1, keepdims=True))
    return dx, dg


_GELU_C = math.sqrt(2.0 / math.pi)


def _gelu_and_grad(x):
    x2 = x * x
    inner = _GELU_C * (x + 0.044715 * (x2 * x))
    t = jnp.tanh(inner)
    y = 0.5 * x * (1.0 + t)
    dy = 0.5 * (1.0 + t) + 0.5 * x * (1.0 - t * t) * (_GELU_C * (1.0 + 3.0 * 0.044715 * x2))
    return y, dy


def _zero_map(nd, *_):
    return (0,) * nd


def _params(n_axes, vmem=None):
    return pltpu.CompilerParams(dimension_semantics=("arbitrary",) * n_axes, vmem_limit_bytes=vmem)


class _Exchange:
    def __init__(self, ins, outs, sems, start, wait):
        self.ins, self.outs, self.sems, self.start, self.wait = list(ins), list(outs), list(sems), start, wait


def _fused_call(name, body, grid, in_specs, out_specs, out_shape, scratch, args, exchange, params):
    n_in, n_out, n_scr = len(in_specs), len(out_specs), len(scratch)
    if exchange is None:
        fn = body
    else:
        ex = exchange
        n_xi, n_xo = len(ex.ins), len(ex.outs)

        def fn(*refs):
            at = 0
            parts = []
            for n in (n_in, n_xi, n_out, n_xo, n_scr, len(ex.sems)):
                parts.append(refs[at:at + n])
                at += n
            ins, x_in, outs, x_out, scr, x_sem = parts
            ids = [pl.program_id(a) for a in range(len(grid))]
            first = functools.reduce(jnp.logical_and, [i == 0 for i in ids])
            last = functools.reduce(jnp.logical_and, [i == g - 1 for i, g in zip(ids, grid)])

            @pl.when(first)
            def _():
                ex.start(x_in, x_out, x_sem)

            body(*ins, *outs, *scr)

            @pl.when(last)
            def _():
                ex.wait(x_in, x_out, x_sem)

        in_specs = list(in_specs) + [ANY] * n_xi
        out_specs = list(out_specs) + [ANY] * n_xo
        out_shape = list(out_shape) + ex.outs
        scratch = list(scratch) + ex.sems
        args = list(args) + ex.ins
    return pl.pallas_call(fn, grid=grid, in_specs=in_specs, out_specs=out_specs, out_shape=out_shape,
                          scratch_shapes=list(scratch), name=name, compiler_params=params)(*args)


def _rowcall(name, body, seq, tb, rows, consts, row_outs, acc_outs, scratch=(), reverse=False, vmem=None,
             exchange=None):
    nb = seq // tb
    rmap = (lambda i: (nb - 1 - i, 0)) if reverse else (lambda i: (i, 0))
    tmap = lambda i: (0,) + rmap(i)

    def row_spec(width):
        if isinstance(width, tuple):
            return pl.BlockSpec((width[0], tb, width[1]), tmap)
        return pl.BlockSpec((tb, width), rmap)

    def row_shape(width):
        return (width[0], seq, width[1]) if isinstance(width, tuple) else (seq, width)

    in_specs = [row_spec(a.shape[1] if a.ndim == 2 else (a.shape[0], a.shape[2])) for a in rows]
    in_specs += [pl.BlockSpec(a.shape, functools.partial(_zero_map, a.ndim), pipeline_mode=pl.Buffered(1))
                 for a in consts]
    out_specs = [row_spec(c) for c, _ in row_outs] + [ANY] * len(acc_outs)
    out_shape = [SDS(row_shape(c), dt) for c, dt in row_outs] + [SDS(s, dt) for s, dt in acc_outs]
    n_main = len(rows) + len(consts) + len(row_outs)
    n_acc = len(acc_outs)

    def fn(*refs):
        main, acc_hbm, rest = refs[:n_main], refs[n_main:n_main + n_acc], refs[n_main + n_acc:]
        acc_vmem, own = rest[:n_acc], rest[n_acc:]
        body(*main, *acc_vmem, *own)

        @pl.when(pl.program_id(0) == nb - 1)
        def _():
            for src, dst in zip(acc_vmem, acc_hbm):
                pltpu.sync_copy(src, dst)

    buffers = [pltpu.VMEM(s, dt) for s, dt in acc_outs] + list(scratch)
    return _fused_call(name, fn if acc_outs else body, (nb,), in_specs, out_specs, out_shape, buffers,
                       [*rows, *consts], exchange, _params(1, vmem))


def _inproj_fwd(x, g1, w_in, tb, exchange=None):
    seq = x.shape[0]

    def body(x_ref, g_ref, w_ref, h_ref, q_ref, k_ref, v_ref, u_ref, ga_ref, gs_ref):
        h, _, _ = _rms(x_ref[...], g_ref[...])
        hb = _bf(h)
        h_ref[...] = hb
        pj = _mm_nt(hb, w_ref[...])
        q_ref[...] = _bf(pj[:, SPLITS[0]:SPLITS[1]])
        k_ref[...] = _bf(pj[:, SPLITS[1]:SPLITS[2]])
        v_ref[...] = _bf(pj[:, SPLITS[2]:SPLITS[3]])
        u_ref[...] = pj[:, SPLITS[3]:SPLITS[4]]
        ga_ref[...] = pj[:, SPLITS[4]:SPLITS[5]]
        gs_ref[...] = pj[:, SPLITS[5]:SPLITS[6]]

    return _rowcall("inproj_fwd", body, seq, tb, [x], [g1, w_in],
                    [(D_MODEL, BF16), (ATTN_W, BF16), (KV_W, BF16), (KV_W, BF16), (SSM_W, F32),
                     (D_MODEL, F32), (D_MODEL, F32)], [], vmem=VMEM_BIG, exchange=exchange)


def _inproj_bwd(x, dx2, dq, dk, dv, du, dga, dgs, g1, w_in, tb, exchange=None):
    seq = x.shape[0]

    def body(x_ref, dx2_ref, dq_ref, dk_ref, dv_ref, du_ref, dga_ref, dgs_ref, g_ref, w_ref,
             dx_ref, dpj_ref, dg_ref):
        @pl.when(pl.program_id(0) == 0)
        def _():
            dg_ref[...] = jnp.zeros_like(dg_ref)

        dpj = jnp.concatenate([dq_ref[...], dk_ref[...], dv_ref[...], _bf(du_ref[...]),
                               dga_ref[...], dgs_ref[...]], axis=1)
        dpj_ref[...] = dpj
        dh = _mm(dpj, w_ref[...])
        g = g_ref[...]
        _, xh, r = _rms(x_ref[...], g)
        dxn, dg = _rms_bwd(dh, xh, r, g)
        dx_ref[...] = dx2_ref[...] + dxn
        dg_ref[...] += dg

    return _rowcall("inproj_bwd", body, seq, tb, [x, dx2, dq, dk, dv, du, dga, dgs], [g1, w_in],
                    [(D_MODEL, F32), (IN_W, BF16)], [((1, D_MODEL), F32)], vmem=VMEM_BIG, exchange=exchange)


def _bucket_table():
    qi = np.arange(BLOCK)[:, None]
    kj = np.arange(2 * BLOCK)[None, :]
    dist = qi + BLOCK - kj
    max_exact = N_BUCKETS // 2
    d = np.maximum(dist, 0)
    df = np.maximum(d, 1).astype(np.float32)
    large = max_exact + (np.log(df / np.float32(max_exact)) / np.float32(math.log(MAX_DISTANCE / max_exact))
                         * np.float32(N_BUCKETS - max_exact)).astype(np.int32)
    large = np.minimum(large, N_BUCKETS - 1)
    bucket = np.where(d < max_exact, d, large)
    valid = (dist >= 0) & (dist < BLOCK)
    return np.where(valid, bucket, -1).astype(np.int32)


def _bias_table(rel_bias, bucket):
    def body(rb_ref, bk_ref, o_ref):
        bk = bk_ref[...]
        has_prev = lax.broadcasted_iota(jnp.int32, bk.shape, 1) >= BLOCK
        for h in range(N_HEADS):
            kh, j, par = h // Q_GROUP, (h // 2) % 2, h % 2
            acc = jnp.full((BLOCK, 2 * BLOCK), NEG_INF, F32)
            for b in range(N_BUCKETS):
                acc = jnp.where(bk == b, rb_ref[b, h], acc)
            o_ref[0, kh, par, :, j * BLOCK:(j + 1) * BLOCK] = jnp.where(has_prev, acc, NEG_INF).T
            o_ref[1, kh, par, :, j * BLOCK:(j + 1) * BLOCK] = acc.T

    return pl.pallas_call(
        body, out_shape=SDS((2, N_KV, 2, 2 * BLOCK, 2 * BLOCK), F32),
        in_specs=[pl.BlockSpec(memory_space=pltpu.SMEM), pl.BlockSpec(memory_space=pltpu.VMEM)],
        out_specs=pl.BlockSpec(memory_space=pltpu.VMEM), name="bias_table",
    )(rel_bias, bucket)


def _bias_grad(dbias, bucket):
    def body(db_ref, bk_ref, o_ref):
        bk = bk_ref[...]
        for h in range(N_HEADS):
            kh, j, par = h // Q_GROUP, (h // 2) % 2, h % 2
            db = db_ref[kh, par, :, j * BLOCK:(j + 1) * BLOCK].T
            for b in range(N_BUCKETS):
                o_ref[b, h] = jnp.sum(jnp.where(bk == b, db, 0.0))

    return pl.pallas_call(
        body, out_shape=SDS((N_BUCKETS, N_HEADS), F32),
        in_specs=[pl.BlockSpec(memory_space=pltpu.VMEM), pl.BlockSpec(memory_space=pltpu.VMEM)],
        out_specs=pl.BlockSpec(memory_space=pltpu.SMEM), name="bias_grad",
    )(dbias, bucket)


TILE = 2 * HEAD_DIM


def _pair_layout(t):
    lead = t.shape[:-3]
    t = t.reshape(lead + (N_KV, 2, 2) + t.shape[-2:])
    nl = len(lead)
    t = jnp.transpose(t, tuple(range(nl)) + (nl, nl + 2, nl + 1, nl + 3, nl + 4))
    return t.reshape(lead + (N_KV, 2, 2 * BLOCK, t.shape[-1]))


def _pair_unlayout(t):
    t = t.reshape(N_KV, 2, 2, BLOCK, t.shape[-1]).transpose(0, 2, 1, 3, 4)
    return t.reshape(N_HEADS, BLOCK, t.shape[-1])


def _halves(t):
    tf = t.astype(F32)
    low = lax.broadcasted_iota(jnp.int32, tf.shape, 1) < HEAD_DIM
    swapped = pltpu.roll(tf, HEAD_DIM, 1)
    zero = jnp.zeros_like(tf)
    return ((_bf(jnp.where(low, tf, zero)), _bf(jnp.where(low, zero, swapped))),
            (_bf(jnp.where(low, swapped, zero)), _bf(jnp.where(low, zero, tf))))


def _fold_halves(even, odd):
    low = lax.broadcasted_iota(jnp.int32, even.shape, 1) < HEAD_DIM
    comb = jnp.where(low, even, odd)
    return comb + pltpu.roll(comb, HEAD_DIM, 1)


def _tile_rows(ref, kh):
    return jnp.concatenate([ref[:, (2 * kh) * TILE:(2 * kh + 1) * TILE],
                            ref[:, (2 * kh + 1) * TILE:(2 * kh + 2) * TILE]], axis=0)


def _halves_t(t):
    tt = t.astype(F32).T
    top = lax.broadcasted_iota(jnp.int32, tt.shape, 0) < HEAD_DIM
    swapped = jnp.concatenate([tt[HEAD_DIM:], tt[:HEAD_DIM]], axis=0)
    zero = jnp.zeros_like(tt)
    return ((_bf(jnp.where(top, tt, zero)), _bf(jnp.where(top, zero, swapped))),
            (_bf(jnp.where(top, swapped, zero)), _bf(jnp.where(top, zero, tt))))


def _attn_probs(km, qk, bias, sink):
    lg = _mm_nt(km, qk) * (HEAD_DIM ** -0.5) + bias
    m = jnp.maximum(jnp.max(lg, axis=0, keepdims=True), sink)
    p = jnp.exp(lg - m)
    es = jnp.exp(sink - m)
    inv = 1.0 / (jnp.sum(p, axis=0, keepdims=True) + es)
    return p * inv, es * inv


def _attn_fwd(q, k, v, bias, sink_rows, exchange=None):
    seq = q.shape[0]
    nblk = seq // BLOCK

    def body(q_ref, kp_ref, kc_ref, vp_ref, vc_ref, b_ref, s_ref, o_ref):
        which = jnp.minimum(pl.program_id(0), 1)
        kms = _halves(jnp.concatenate([kp_ref[...], kc_ref[...]], axis=0))
        vts = _halves_t(jnp.concatenate([vp_ref[...], vc_ref[...]], axis=0))
        for kh in range(N_KV):
            qk = _tile_rows(q_ref, kh)
            acc = jnp.zeros((TILE, 2 * BLOCK), F32)
            for par in range(2):
                pr, _ = _attn_probs(kms[kh][par], qk, b_ref[which, kh, par], s_ref[kh, par])
                acc = acc + _mm(vts[kh][par], _bf(pr))
            acc = acc.T
            o_ref[:, (2 * kh) * TILE:(2 * kh + 1) * TILE] = _bf(acc[:BLOCK])
            o_ref[:, (2 * kh + 1) * TILE:(2 * kh + 2) * TILE] = _bf(acc[BLOCK:])

    cur = lambda n: (n, 0)
    prev = lambda n: (jnp.maximum(n - 1, 0), 0)
    return _fused_call(
        "attn_fwd", body, (nblk,),
        [pl.BlockSpec((BLOCK, ATTN_W), cur),
         pl.BlockSpec((BLOCK, KV_W), prev), pl.BlockSpec((BLOCK, KV_W), cur),
         pl.BlockSpec((BLOCK, KV_W), prev), pl.BlockSpec((BLOCK, KV_W), cur),
         pl.BlockSpec(bias.shape, functools.partial(_zero_map, bias.ndim)),
         pl.BlockSpec(sink_rows.shape, functools.partial(_zero_map, sink_rows.ndim))],
        [pl.BlockSpec((BLOCK, ATTN_W), cur)], [SDS((seq, ATTN_W), BF16)], [],
        [q, k, k, v, v, bias, sink_rows], exchange, _params(1))


def _attn_bwd(q, k, v, d_out, bias, sink_rows, exchange=None):
    seq = q.shape[0]
    nblk = seq // BLOCK

    def body(q_ref, kp_ref, kc_ref, vp_ref, vc_ref, do_ref, b_ref, s_ref,
             dq_ref, dk_ref, dv_ref, db_ref, ds_ref, ck_ref, cv_ref):
        n = pl.program_id(0)

        @pl.when(n == 0)
        def _():
            db_ref[...] = jnp.zeros_like(db_ref)
            ds_ref[...] = jnp.zeros_like(ds_ref)
            ck_ref[...] = jnp.zeros_like(ck_ref)
            cv_ref[...] = jnp.zeros_like(cv_ref)

        @pl.when(n < nblk)
        def _():
            which = jnp.minimum(n, 1)
            scale = HEAD_DIM ** -0.5
            kcat = jnp.concatenate([kp_ref[...], kc_ref[...]], axis=0)
            kms = _halves(kcat)
            kts = _halves_t(kcat)
            vms = _halves(jnp.concatenate([vp_ref[...], vc_ref[...]], axis=0))
            dks, dvs = [], []
            for kh in range(N_KV):
                qk = _tile_rows(q_ref, kh)
                dok = _tile_rows(do_ref, kh)
                dq = jnp.zeros((TILE, 2 * BLOCK), F32)
                dkp, dvp = [], []
                for par in range(2):
                    pr, ps = _attn_probs(kms[kh][par], qk, b_ref[which, kh, par], s_ref[kh, par])
                    dp = _mm_nt(vms[kh][par], dok)
                    rs = jnp.sum(pr * dp, axis=0, keepdims=True)
                    dlg = pr * (dp - rs)
                    ds_ref[kh, par] += -ps * rs
                    db_ref[kh, par] += dlg
                    dlb = _bf(dlg)
                    dq = dq + _mm(kts[kh][par], dlb)
                    dkp.append(_mm(dlb, qk))
                    dvp.append(_mm(_bf(pr), dok))
                dq = _bf((dq * scale).T)
                dq_ref[:, (2 * kh) * TILE:(2 * kh + 1) * TILE] = dq[:BLOCK]
                dq_ref[:, (2 * kh + 1) * TILE:(2 * kh + 2) * TILE] = dq[BLOCK:]
                dks.append(_fold_halves(*dkp))
                dvs.append(_fold_halves(*dvp))
            low = lax.broadcasted_iota(jnp.int32, (2 * BLOCK, TILE), 1) < HEAD_DIM
            dkk = jnp.where(low, dks[0], dks[1]) * scale
            dvv = jnp.where(low, dvs[0], dvs[1])
            dk_ref[...] = _bf(ck_ref[...] + dkk[:BLOCK])
            ck_ref[...] = dkk[BLOCK:]
            dv_ref[...] = _bf(cv_ref[...] + dvv[:BLOCK])
            cv_ref[...] = dvv[BLOCK:]

        @pl.when(n == nblk)
        def _():
            dk_ref[...] = _bf(ck_ref[...])
            dv_ref[...] = _bf(cv_ref[...])

    cur = lambda n: (jnp.minimum(n, nblk - 1), 0)
    prev = lambda n: (jnp.maximum(jnp.minimum(n, nblk - 1) - 1, 0), 0)
    late = lambda n: (jnp.maximum(n - 1, 0), 0)
    kv_spec = lambda m: pl.BlockSpec((BLOCK, KV_W), m)
    acc_b = pl.BlockSpec(bias.shape[1:], functools.partial(_zero_map, bias.ndim - 1))
    acc_s = pl.BlockSpec(sink_rows.shape, functools.partial(_zero_map, sink_rows.ndim))
    return _fused_call(
        "attn_bwd", body, (nblk + 1,),
        [pl.BlockSpec((BLOCK, ATTN_W), cur), kv_spec(prev), kv_spec(cur), kv_spec(prev), kv_spec(cur),
         pl.BlockSpec((BLOCK, ATTN_W), cur),
         pl.BlockSpec(bias.shape, functools.partial(_zero_map, bias.ndim)), acc_s],
        [pl.BlockSpec((BLOCK, ATTN_W), cur), kv_spec(late), kv_spec(late), acc_b, acc_s],
        [SDS((seq, ATTN_W), BF16), SDS((seq, KV_W), BF16), SDS((seq, KV_W), BF16),
         SDS(bias.shape[1:], F32), SDS(sink_rows.shape, F32)],
        [pltpu.VMEM((BLOCK, KV_W), F32), pltpu.VMEM((BLOCK, KV_W), F32)],
        [q, k, k, v, v, d_out, bias, sink_rows], exchange, _params(1))


def _ssm_discretize(lam_re, lam_im, log_dt, b_re, b_im):
    dt = jnp.exp(log_dt)[:, None]
    mag = jnp.exp(lam_re * dt)
    ab_re = mag * jnp.cos(lam_im * dt)
    ab_im = mag * jnp.sin(lam_im * dt)
    nr = ab_re - 1.0
    den = lam_re * lam_re + lam_im * lam_im
    f_re = (nr * lam_re + ab_im * lam_im) / den
    f_im = (ab_im * lam_re - nr * lam_im) / den
    bb_re = f_re[..., None] * b_re - f_im[..., None] * b_im
    bb_im = f_re[..., None] * b_im + f_im[..., None] * b_re
    return ab_re, ab_im, bb_re, bb_im


def _state_layout(re, im):
    lead = re.shape[:-2]
    z = jnp.stack([re, im], axis=-3).reshape(lead + (2, N_SUPER, GROUPS_PER_SUPER, SSM_STATE))
    return jnp.moveaxis(z, -4, -3).reshape(lead + (STATE_COLS,))


def _state_unlayout(vec):
    z = vec.reshape(N_SUPER, 2, GROUPS_PER_SUPER, SSM_STATE).transpose(1, 0, 2, 3)
    z = z.reshape(2, SSM_GROUPS, SSM_STATE)
    return z[0], z[1]


SEG = 4
WINDOW = SEG * SUBLANES


def _scan_tables(ab_re, ab_im):
    pw = [None, (ab_re, ab_im)]
    for _ in range(2, WINDOW + 1):
        pr, pi_ = pw[-1]
        pw.append((pr * ab_re - pi_ * ab_im, pr * ab_im + pi_ * ab_re))
    fwd = np.zeros((7, SUBLANES), np.int64)
    bwd = np.zeros((7, SUBLANES), np.int64)
    for k, shift in enumerate((1, 2, 4)):
        fwd[k] = [SEG * shift if r >= shift else 0 for r in range(SUBLANES)]
        bwd[k] = [SEG * shift if r < SUBLANES - shift else 0 for r in range(SUBLANES)]
    fwd[3] = [SEG * (r + 1) for r in range(SUBLANES)]
    bwd[3] = [SEG * (SUBLANES - r) for r in range(SUBLANES)]
    for k in range(1, SEG):
        fwd[3 + k] = bwd[3 + k] = k
    used = sorted((set(fwd.ravel()) | set(bwd.ravel())) - {0})
    select = lambda which: np.stack([(which == p) for p in used], axis=-1).astype(np.float32)
    stacked = _state_layout(jnp.stack([pw[p][0] for p in used]), jnp.stack([pw[p][1] for p in used]))
    conj_sign = np.where((np.arange(STATE_COLS) // SUPER_HALF) % 2 == 1, -1.0, 1.0).astype(np.float32)
    pick = functools.partial(jnp.einsum, 'krp,pc->krc', precision=lax.Precision.HIGHEST)
    return pick(select(fwd), stacked), pick(select(bwd), stacked) * conj_sign


_EYE = np.eye(GROUPS_PER_SUPER, dtype=np.float32)


def _b_matrix(bb_re, bb_im):
    bb = jnp.stack([bb_re, bb_im]).reshape(2, N_SUPER, GROUPS_PER_SUPER, SSM_STATE, SSM_GROUP)
    m = jnp.einsum('rsgpc,gh->sgcrhp', bb, _EYE)
    return m.reshape(N_SUPER, SUPER_IN, SUPER_W)


def _b_matrix_grad(dm):
    d = dm.reshape(N_SUPER, GROUPS_PER_SUPER, SSM_GROUP, 2, GROUPS_PER_SUPER, SSM_STATE)
    d = jnp.sum(d * _EYE[None, :, None, None, :, None], axis=4)
    d = d.transpose(3, 0, 1, 4, 2).reshape(2, SSM_GROUPS, SSM_STATE, SSM_GROUP)
    return d[0], d[1]


def _c_matrix(c_re, c_im):
    cc = jnp.stack([c_re, -c_im]).reshape(2, N_SUPER, GROUPS_PER_SUPER, SSM_GROUP, SSM_STATE)
    m = jnp.einsum('rsgcp,gh->srgphc', cc, _EYE)
    return m.reshape(N_SUPER, SUPER_W, SUPER_IN)


def _c_matrix_grad(dm):
    d = dm.reshape(N_SUPER, 2, GROUPS_PER_SUPER, SSM_STATE, GROUPS_PER_SUPER, SSM_GROUP)
    d = jnp.sum(d * _EYE[None, None, :, None, :, None], axis=4)
    d = d.transpose(1, 0, 2, 4, 3).reshape(2, SSM_GROUPS, SSM_GROUP, SSM_STATE)
    return d[0], -d[1]


def _cmul_add(xr, xi, ar, ai, sr, si):
    return xr + ar * sr - ai * si, xi + ar * si + ai * sr


def _scan_rows(buf_ref, tab_ref, carry_ref, n_windows, reverse, h_ref=None, da_ref=None):
    order = list(range(SEG - 1, -1, -1)) if reverse else list(range(SEG))
    near = SUBLANES - 1 if reverse else 0
    far = 0 if reverse else SUBLANES - 1
    s_in = SUBLANES - 1 if reverse else 1
    lanes = lambda tile: pl.ds(tile * LANES, LANES)

    def window(w0, tile_re, tile_im, c_re, c_im, acc):
        rows = lambda t: pl.ds(w0 + t, SUBLANES, stride=SEG)
        get = lambda ref, t: (ref.at[tile_re][rows(t), :], ref.at[tile_im][rows(t), :])
        tab = lambda k: (tab_ref[k, :, lanes(tile_re)], tab_ref[k, :, lanes(tile_im)])

        def put(t, xr, xi):
            buf_ref.at[tile_re][rows(t), :] = xr
            buf_ref.at[tile_im][rows(t), :] = xi

        a1 = tab(4)
        er, ei = get(buf_ref, order[0])
        for t in order[1:]:
            er, ei = _cmul_add(*get(buf_ref, t), *a1, er, ei)
            if t != order[-1]:
                put(t, er, ei)
        for k, shift in enumerate((1, 2, 4)):
            s = (SUBLANES - shift) if reverse else shift
            er, ei = _cmul_add(er, ei, *tab(k), pltpu.roll(er, s, 0), pltpu.roll(ei, s, 0))
        er, ei = _cmul_add(er, ei, *tab(3), c_re, c_im)
        put(order[-1], er, ei)
        sub = lax.broadcasted_iota(jnp.int32, er.shape, 0)
        in_re = jnp.where(sub == near, c_re, pltpu.roll(er, s_in, 0))
        in_im = jnp.where(sub == near, c_im, pltpu.roll(ei, s_in, 0))
        true = {order[-1]: (er, ei)}
        for idx, t in enumerate(order[:-1]):
            true[t] = _cmul_add(*get(buf_ref, t), *tab(4 + idx), in_re, in_im)
            put(t, *true[t])
        carry = (jnp.broadcast_to(er[far:far + 1], er.shape), jnp.broadcast_to(ei[far:far + 1], ei.shape))
        if acc is None:
            return carry, None
        acc_re, acc_im = acc
        for t in range(SEG):
            if t + 1 < SEG:
                gr, gim = true[t + 1]
            else:
                gr = jnp.where(sub == SUBLANES - 1, c_re, pltpu.roll(true[0][0], SUBLANES - 1, 0))
                gim = jnp.where(sub == SUBLANES - 1, c_im, pltpu.roll(true[0][1], SUBLANES - 1, 0))
            hr, hi = get(h_ref, t)
            acc_re = acc_re + gr * hr + gim * hi
            acc_im = acc_im + gim * hr - gr * hi
        return carry, (acc_re, acc_im)

    half = SUPER_HALF // LANES
    per = 2 if h_ref is None else 4
    for sb in range(N_SUPER):
        pairs = [(2 * half * sb + j, 2 * half * sb + half + j) for j in range(half)]

        def step(wi, state, pairs=pairs):
            w = (n_windows - 1 - wi) if reverse else wi
            w0 = pl.multiple_of(w * WINDOW, WINDOW)
            out = []
            for j, (tile_re, tile_im) in enumerate(pairs):
                mine = state[per * j:per * (j + 1)]
                carry, acc = window(w0, tile_re, tile_im, mine[0], mine[1], mine[2:] or None)
                out += list(carry) + list(acc or ())
            return tuple(out)

        init = []
        for tile_re, tile_im in pairs:
            init += [carry_ref[:, lanes(tile_re)], carry_ref[:, lanes(tile_im)]]
            if h_ref is not None:
                init += [da_ref[:, lanes(tile_re)], da_ref[:, lanes(tile_im)]]
        fin = lax.fori_loop(0, n_windows, step, tuple(init))
        for j, (tile_re, tile_im) in enumerate(pairs):
            carry_ref[:, lanes(tile_re)] = fin[per * j]
            carry_ref[:, lanes(tile_im)] = fin[per * j + 1]
            if h_ref is not None:
                da_ref[:, lanes(tile_re)] = fin[per * j + 2]
                da_ref[:, lanes(tile_im)] = fin[per * j + 3]


def _put_tiles(ref, sb, block):
    for j in range(SUPER_TILES):
        ref[sb * SUPER_TILES + j] = block[:, j * LANES:(j + 1) * LANES]


def _get_tiles(ref, sb):
    return jnp.concatenate([ref[sb * SUPER_TILES + j] for j in range(SUPER_TILES)], axis=1)


def _ssm_fwd(u, bmat, cmat, tab, d_skip, tb, exchange=None):
    seq = u.shape[0]

    def body(u_ref, b_ref, c_ref, t_ref, d_ref, s_ref, h_ref, carry_ref):
        @pl.when(pl.program_id(0) == 0)
        def _():
            carry_ref[...] = jnp.zeros_like(carry_ref)

        u_blk = u_ref[...]
        ub = _bf(u_blk)
        for sb in range(N_SUPER):
            _put_tiles(h_ref, sb, _mm(ub[:, sb * SUPER_IN:(sb + 1) * SUPER_IN], b_ref[sb]))
        _scan_rows(h_ref, t_ref, carry_ref, tb // WINDOW, False)
        ys = [_mm(_bf(_get_tiles(h_ref, sb)), c_ref[sb]) for sb in range(N_SUPER)]
        s_ref[...] = jnp.concatenate(ys, axis=1) + d_ref[...] * u_blk

    return _rowcall("ssm_fwd", body, seq, tb, [u], [bmat, cmat, tab, d_skip],
                    [(SSM_W, F32), ((STATE_TILES, LANES), F32)], [],
                    scratch=[pltpu.VMEM((SUBLANES, STATE_COLS), F32)], vmem=VMEM_BIG, exchange=exchange)


def _ssm_bwd(ds, u, h, bmat_t, cmat_t, tab, d_skip, tb, exchange=None):
    seq = u.shape[0]

    def body(ds_ref, u_ref, h_ref, bt_ref, ct_ref, t_ref, d_ref,
             du_ref, db_ref, dc_ref, da_ref, dd_ref, g_ref, carry_ref):
        @pl.when(pl.program_id(0) == 0)
        def _():
            carry_ref[...] = jnp.zeros_like(carry_ref)
            db_ref[...] = jnp.zeros_like(db_ref)
            dc_ref[...] = jnp.zeros_like(dc_ref)
            da_ref[...] = jnp.zeros_like(da_ref)
            dd_ref[...] = jnp.zeros_like(dd_ref)

        ds_blk = ds_ref[...]
        dsb = _bf(ds_blk)
        u_blk = u_ref[...]
        ub = _bf(u_blk)
        for sb in range(N_SUPER):
            _put_tiles(g_ref, sb, _mm(dsb[:, sb * SUPER_IN:(sb + 1) * SUPER_IN], ct_ref[sb]))
        _scan_rows(g_ref, t_ref, carry_ref, tb // WINDOW, True, h_ref=h_ref, da_ref=da_ref)
        dus = []
        for sb in range(N_SUPER):
            gb = _bf(_get_tiles(g_ref, sb))
            dus.append(_mm(gb, bt_ref[sb]))
            db_ref[sb] += _mm_tn(ub[:, sb * SUPER_IN:(sb + 1) * SUPER_IN], gb)
            dc_ref[sb] += _mm_tn(_bf(_get_tiles(h_ref, sb)), dsb[:, sb * SUPER_IN:(sb + 1) * SUPER_IN])
        du_ref[...] = jnp.concatenate(dus, axis=1) + d_ref[...] * ds_blk
        dd_ref[...] += jnp.sum(ds_blk * u_blk, axis=0, keepdims=True)

    return _rowcall("ssm_bwd", body, seq, tb, [ds, u, h], [bmat_t, cmat_t, tab, d_skip],
                    [(SSM_W, F32)],
                    [((N_SUPER, SUPER_IN, SUPER_W), F32), ((N_SUPER, SUPER_W, SUPER_IN), F32),
                     ((SUBLANES, STATE_COLS), F32), ((1, SSM_W), F32)],
                    scratch=[pltpu.VMEM((STATE_TILES, tb, LANES), F32), pltpu.VMEM((SUBLANES, STATE_COLS), F32)],
                    reverse=True, vmem=VMEM_BIG, exchange=exchange)


def _merge_core(s, attb, ga, gs, wg_ref, wab_ref, wsb_ref, wout_ref):
    zg, dgelu = _gelu_and_grad(s)
    zgb = _bf(zg)
    sg = _sig(_mm(zgb, wg_ref[...]))
    z = zg * sg
    zb = _bf(z)
    ys = jnp.concatenate([_mm(zb, wsb_ref[j]) for j in range(N_CHIPS)], axis=1)
    ya = jnp.concatenate([_mm(attb, wab_ref[j]) for j in range(N_CHIPS)], axis=1)
    sa = _sig(ga)
    ss = _sig(gs)
    mgb = _bf(sa * ya + ss * ys)
    o = _mm(mgb, wout_ref[...])
    return dict(zg=zg, dgelu=dgelu, zgb=zgb, sg=sg, zb=zb, ys=ys, ya=ya, sa=sa, ss=ss, mgb=mgb, o=o)


def _merge_fwd(x, s, att, ga, gs, g2, w_glu, w_ab, w_sb, w_out, tb, exchange=None):
    seq = x.shape[0]

    def body(x_ref, s_ref, att_ref, ga_ref, gs_ref, g_ref, wg_ref, wab_ref, wsb_ref, wout_ref, x2_ref):
        f = _merge_core(s_ref[...], att_ref[...], ga_ref[...], gs_ref[...], wg_ref, wab_ref, wsb_ref, wout_ref)
        n, _, _ = _rms(f["o"], g_ref[...])
        x2_ref[...] = x_ref[...] + n

    return _rowcall("merge_fwd", body, seq, tb, [x, s, att, ga, gs], [g2, w_glu, w_ab, w_sb, w_out],
                    [(D_MODEL, F32)], [], vmem=VMEM_BIG, exchange=exchange)[0]


def _merge_bwd(dx2, s, att, ga, gs, g2, w_glu, w_ab, w_sb, w_out, tb, exchange=None):
    seq = s.shape[0]
    cw = D_MODEL // N_CHIPS
    last = seq // tb - 1

    def body(dx2_ref, s_ref, att_ref, ga_ref, gs_ref, g_ref, wg_ref, wab_ref, wsb_ref, wout_ref,
             ds_ref, datt_ref, dga_ref, dgs_ref, dg_ref, dwg_ref, dwab_ref, dwsb_ref, dwout_ref,
             bwg_ref, bwab_ref, bwsb_ref, bwout_ref):
        @pl.when(pl.program_id(0) == 0)
        def _():
            for r in (dg_ref, dwg_ref, dwab_ref, dwsb_ref, dwout_ref):
                r[...] = jnp.zeros_like(r)

        attb = att_ref[...]
        f = _merge_core(s_ref[...], attb, ga_ref[...], gs_ref[...], wg_ref, wab_ref, wsb_ref, wout_ref)
        g = g_ref[...]
        _, oh, r2 = _rms(f["o"], g)
        do, dg = _rms_bwd(dx2_ref[...], oh, r2, g)
        dg_ref[...] += dg
        dob = _bf(do)
        dwout_ref[...] += _mm_tn(f["mgb"], dob)
        dmg = _mm_nt(dob, wout_ref[...])
        sa, ss = f["sa"], f["ss"]
        dyab = _bf(dmg * sa)
        dysb = _bf(dmg * ss)
        dga_ref[...] = _bf(dmg * f["ya"] * sa * (1.0 - sa))
        dgs_ref[...] = _bf(dmg * f["ys"] * ss * (1.0 - ss))
        dwab = _mm_tn(attb, dyab)
        dwsb = _mm_tn(f["zb"], dysb)
        datt = jnp.zeros((tb, ATTN_W), F32)
        dz = jnp.zeros((tb, SSM_W), F32)
        for j in range(N_CHIPS):
            dwab_ref[j] += dwab[:, j * cw:(j + 1) * cw]
            dwsb_ref[j] += dwsb[:, j * cw:(j + 1) * cw]
            datt = datt + _mm_nt(dyab[:, j * cw:(j + 1) * cw], wab_ref[j])
            dz = dz + _mm_nt(dysb[:, j * cw:(j + 1) * cw], wsb_ref[j])
        datt_ref[...] = _bf(datt)
        sg, zg = f["sg"], f["zg"]
        dglb = _bf(dz * zg * sg * (1.0 - sg))
        dwg_ref[...] += _mm_tn(f["zgb"], dglb)
        dzg = dz * sg + _mm_nt(dglb, wg_ref[...])
        ds_ref[...] = dzg * f["dgelu"]

        @pl.when(pl.program_id(0) == last)
        def _():
            for dst, src in ((bwg_ref, dwg_ref), (bwab_ref, dwab_ref), (bwsb_ref, dwsb_ref), (bwout_ref, dwout_ref)):
                dst[...] = _bf(src[...])

    shapes = [w_glu.shape, w_ab.shape, w_sb.shape, w_out.shape]
    return _rowcall("merge_bwd", body, seq, tb, [dx2, s, att, ga, gs], [g2, w_glu, w_ab, w_sb, w_out],
                    [(SSM_W, F32), (ATTN_W, BF16), (D_MODEL, BF16), (D_MODEL, BF16)],
                    [((1, D_MODEL), F32)] + [(sh, F32) for sh in shapes] + [(sh, BF16) for sh in shapes],
                    vmem=VMEM_BIG, exchange=exchange)


def _mlp_fwd_loss(x2, target, g3, g4, w_ffi, w_ffo, tb):
    seq = x2.shape[0]
    n_slab = len(w_ffi)
    sw = D_FF // FF_CHUNKS // n_slab

    def body(x2_ref, t_ref, g3_ref, g4_ref, *rest):
        wi_refs, (wo_ref, dy_ref, df_ref, h_ref, ra_ref, loss_ref, dg_ref) = rest[:n_slab], rest[n_slab:]

        @pl.when(pl.program_id(0) == 0)
        def _():
            loss_ref[...] = jnp.zeros_like(loss_ref)
            dg_ref[...] = jnp.zeros_like(dg_ref)

        x2_blk = x2_ref[...]
        h3, _, _ = _rms(x2_blk, g3_ref[...])
        hb = _bf(h3)
        h_ref[...] = hb
        f = jnp.zeros((tb, D_MODEL), F32)
        for j in range(FF_CHUNKS):
            for k in range(n_slab):
                ra = jnp.maximum(_mm(hb, wi_refs[k][j]), 0.0)
                ra_ref[:, pl.ds((j * n_slab + k) * sw, sw)] = _bf(ra)
                f = f + _mm(_bf(ra * ra), wo_ref[j, pl.ds(k * sw, sw), :])
        g4 = g4_ref[...]
        n4, fh, r4 = _rms(f, g4)
        e = (x2_blk + n4) - t_ref[...]
        loss_ref[...] += 0.5 * jnp.sum(jnp.mean(e * e, axis=-1, keepdims=True))
        dy = e * (1.0 / D_MODEL)
        dy_ref[...] = dy
        df, dg = _rms_bwd(dy, fh, r4, g4)
        df_ref[...] = _bf(df)
        dg_ref[...] += dg

    return _rowcall("mlp_fwd_loss", body, seq, tb, [x2, target], [g3, g4, *w_ffi, w_ffo],
                    [(D_MODEL, F32), (D_MODEL, BF16), (D_MODEL, BF16), (D_FF, BF16)],
                    [((SUBLANES, 128), F32), ((1, D_MODEL), F32)], vmem=VMEM_BIG)


def _mlp_bwd(x2, dy, df, ra, g3, w_ffi, w_ffo, tb):
    seq = x2.shape[0]
    n_slab = len(w_ffi)
    sw = D_FF // FF_CHUNKS // n_slab

    def body(x2_ref, dy_ref, df_ref, ra_ref, g3_ref, *rest):
        wi_refs, (wo_ref, dx_ref, da_ref, dg_ref) = rest[:n_slab], rest[n_slab:]

        @pl.when(pl.program_id(0) == 0)
        def _():
            dg_ref[...] = jnp.zeros_like(dg_ref)

        dfb = df_ref[...]
        dh = jnp.zeros((tb, D_MODEL), F32)
        for j in range(FF_CHUNKS):
            for k in range(n_slab):
                cols = pl.ds((j * n_slab + k) * sw, sw)
                ra = ra_ref[:, cols].astype(F32)
                dab = _bf(_mm_nt(dfb, wo_ref[j, pl.ds(k * sw, sw), :]) * (2.0 * ra))
                da_ref[:, cols] = dab
                dh = dh + _mm_nt(dab, wi_refs[k][j])
        g3 = g3_ref[...]
        _, xh, r3 = _rms(x2_ref[...], g3)
        dxn, dg = _rms_bwd(dh, xh, r3, g3)
        dx_ref[...] = dy_ref[...] + dxn
        dg_ref[...] += dg

    return _rowcall("mlp_bwd", body, seq, tb, [x2, dy, df, ra], [g3, *w_ffi, w_ffo],
                    [(D_MODEL, F32), (D_FF, BF16)], [((1, D_MODEL), F32)], vmem=VMEM_BIG)


def _matmul_tn(name, a, b, tk, tn, tl, chunk_major, exchange=None, square_a=False):
    seq, kdim = a.shape
    ndim = b.shape[1]
    last = seq // tl - 1

    def body(a_ref, b_ref, o_ref, ob_ref):
        @pl.when(pl.program_id(2) == 0)
        def _():
            o_ref[...] = jnp.zeros_like(o_ref)

        a_blk = a_ref[...]
        if square_a:
            a_blk = _bf(jnp.square(a_blk.astype(F32)))
        o_ref[...] += _mm_tn(a_blk, b_ref[...])

        @pl.when(pl.program_id(2) == last)
        def _():
            ob_ref[...] = _bf(o_ref[...])

    if chunk_major:
        shape = (ndim // tn, kdim, tn)
        out_spec = pl.BlockSpec((None, tk, tn), lambda k, n, l: (n, k, 0))
    else:
        shape = (kdim, ndim)
        out_spec = pl.BlockSpec((tk, tn), lambda k, n, l: (k, n))
    return _fused_call(
        name, body, (kdim // tk, ndim // tn, seq // tl),
        [pl.BlockSpec((tl, tk), lambda k, n, l: (l, k)), pl.BlockSpec((tl, tn), lambda k, n, l: (l, n))],
        [out_spec, out_spec], [SDS(shape, F32), SDS(shape, BF16)], [], [a, b], exchange, _params(3, VMEM_BIG))


def _ew_call(name, fn, ins, n_out, after=None):
    rows, cols = ins[0].shape
    tr = rows
    while tr * cols * 4 > min(1 << 20, (9 << 20) // (len(ins) + n_out)) and tr % 16 == 0:
        tr //= 2
    spec = pl.BlockSpec((tr, cols), lambda i: (i, 0))
    extra = [] if after is None else [after]

    def body(*refs):
        outs = fn(*[r[...] for r in refs[:len(ins)]])
        for r, o in zip(refs[len(ins) + len(extra):], outs):
            r[...] = o

    return pl.pallas_call(
        body, grid=(rows // tr,), in_specs=[spec] * len(ins) + [ANY] * len(extra), out_specs=[spec] * n_out,
        out_shape=[SDS((rows, cols), F32)] * n_out, name=name, compiler_params=_params(1))(*ins, *extra)


def _adam_math(w, g, m, v):
    m2 = ADAM_B1 * m + (1.0 - ADAM_B1) * g
    v2 = ADAM_B2 * v + (1.0 - ADAM_B2) * (g * g)
    m_hat = m2 / (1.0 - ADAM_B1 ** ADAM_STEP)
    v_hat = v2 / (1.0 - ADAM_B2 ** ADAM_STEP)
    delta = -ADAM_LR * (m_hat / (jnp.sqrt(v_hat) + ADAM_EPS) + ADAM_WD * w)
    return delta, m2, v2


def _sum4(name, own, recv, idx):
    _, rows, cols = own.shape
    tr = rows
    while tr * cols * 4 > (1 << 20) and tr % 16 == 0:
        tr //= 2

    def body(idx_ref, o_ref, r0_ref, r1_ref, r2_ref, out_ref):
        out_ref[...] = ((o_ref[...] + r0_ref[...].astype(F32)) + r1_ref[...].astype(F32)) + r2_ref[...].astype(F32)

    blk = (None, tr, cols)
    grid_spec = pltpu.PrefetchScalarGridSpec(
        num_scalar_prefetch=1, grid=(rows // tr,),
        in_specs=[pl.BlockSpec(blk, lambda i, s: (s[0], i, 0)), pl.BlockSpec(blk, lambda i, s: (0, i, 0)),
                  pl.BlockSpec(blk, lambda i, s: (1, i, 0)), pl.BlockSpec(blk, lambda i, s: (2, i, 0))],
        out_specs=pl.BlockSpec((tr, cols), lambda i, s: (i, 0)))
    return pl.pallas_call(body, grid_spec=grid_spec, out_shape=SDS((rows, cols), F32), name=name,
                          compiler_params=_params(1))(jnp.reshape(idx, (1,)).astype(jnp.int32), own, recv, recv, recv)


def _sum4_group(name, owns, recvs, idx):
    n = len(owns)

    def body(idx_ref, *refs):
        for a in range(n):
            o_ref, r0_ref, r1_ref, r2_ref = refs[4 * a:4 * a + 4]
            refs[4 * n + a][...] = (((o_ref[...] + r0_ref[...].astype(F32)) + r1_ref[...].astype(F32))
                                    + r2_ref[...].astype(F32))

    in_specs, args = [], []
    for own, recv in zip(owns, recvs):
        blk = (None,) + own.shape[1:]
        in_specs += [pl.BlockSpec(blk, lambda i, s: (s[0], 0, 0))]
        in_specs += [pl.BlockSpec(blk, lambda i, s, k=k: (k, 0, 0)) for k in range(3)]
        args += [own, recv, recv, recv]
    grid_spec = pltpu.PrefetchScalarGridSpec(
        num_scalar_prefetch=1, grid=(1,), in_specs=in_specs,
        out_specs=[pl.BlockSpec(o.shape[1:], lambda i, s: (0, 0)) for o in owns])
    return pl.pallas_call(body, grid_spec=grid_spec, out_shape=[SDS(o.shape[1:], F32) for o in owns], name=name,
                          compiler_params=_params(1, VMEM_BIG))(jnp.reshape(idx, (1,)).astype(jnp.int32), *args)


def _adam_group(name, items, after):
    n = len(items)
    flat = [t for item in items for t in item]

    def body(*refs):
        outs = refs[5 * n + 1:]
        for a in range(n):
            w_, p, q, m_, v_ = [r[...] for r in refs[5 * a:5 * a + 5]]
            g = p + q
            for r, o in zip(outs[4 * a:4 * a + 4], (g,) + _adam_math(w_, g, m_, v_)):
                r[...] = o

    whole = lambda t: pl.BlockSpec(t.shape, lambda i: (0, 0))
    outs = pl.pallas_call(
        body, grid=(1,), in_specs=[whole(t) for t in flat] + [ANY],
        out_specs=[whole(item[0]) for item in items for _ in range(4)],
        out_shape=[SDS(item[0].shape, F32) for item in items for _ in range(4)], name=name,
        compiler_params=_params(1, VMEM_BIG))(*flat, after)
    return [outs[4 * a:4 * a + 4] for a in range(n)]


def _sum4_swap(name, own, recv, idx, n_blocks=4):
    _, rows, cols = own.shape
    tr = rows // n_blocks
    assert tr * n_blocks == rows and tr % SUBLANES == 0

    def body(idx_ref, o_ref, r0_ref, r1_ref, r2_ref, mine_ref, theirs_ref, buf, kept, sent, arrived):
        i = pl.program_id(0)
        slot = lax.rem(i, 2)
        x, y, c = _place()

        def copies(j, s):
            block = pl.ds(j * tr, tr)
            return (pltpu.make_async_copy(buf.at[s], mine_ref.at[block], kept.at[s]),
                    pltpu.make_async_remote_copy(
                        src_ref=buf.at[s], dst_ref=theirs_ref.at[block], send_sem=sent.at[s], recv_sem=arrived.at[j],
                        device_id=(x, y, 1 - c), device_id_type=MESH_ID))

        def finish(j, s):
            keep, send = copies(j, s)
            keep.wait()
            send.wait_send()
            send.wait_recv()

        @pl.when(i >= 2)
        def _():
            finish(i - 2, slot)

        buf[slot] = ((o_ref[...] + r0_ref[...].astype(F32)) + r1_ref[...].astype(F32)) + r2_ref[...].astype(F32)
        for cp in copies(i, slot):
            cp.start()

        @pl.when(i == n_blocks - 1)
        def _():
            if n_blocks > 1:
                finish(i - 1, 1 - slot)
            finish(i, slot)

    blk = (None, tr, cols)
    grid_spec = pltpu.PrefetchScalarGridSpec(
        num_scalar_prefetch=1, grid=(n_blocks,),
        in_specs=[pl.BlockSpec(blk, lambda i, s: (s[0], i, 0)), pl.BlockSpec(blk, lambda i, s: (0, i, 0)),
                  pl.BlockSpec(blk, lambda i, s: (1, i, 0)), pl.BlockSpec(blk, lambda i, s: (2, i, 0))],
        out_specs=[HBM, HBM],
        scratch_shapes=[pltpu.VMEM((2, tr, cols), F32), pltpu.SemaphoreType.DMA((2,)), pltpu.SemaphoreType.DMA((2,)),
                        pltpu.SemaphoreType.DMA((n_blocks,))])
    return pl.pallas_call(body, grid_spec=grid_spec, out_shape=[SDS((rows, cols), F32)] * 2, name=name,
                          compiler_params=_params(1))(jnp.reshape(idx, (1,)).astype(jnp.int32), own, recv, recv, recv)


def _sum4_swap_adam(name, own, recv, idx, w, m, v, n_blocks=4):
    _, rows, cols = own.shape
    tr = rows // n_blocks
    assert tr * n_blocks == rows and tr % SUBLANES == 0

    def body(idx_ref, o_ref, r0_ref, r1_ref, r2_ref, w_ref, m_ref, v_ref, g_out, d_out, m_out, v_out,
             mine, theirs, sent, arrived):
        j = pl.program_id(0)
        x, y, c = _place()

        def send(k):
            s = lax.rem(k, 2)
            return pltpu.make_async_remote_copy(
                src_ref=mine.at[s], dst_ref=theirs.at[s], send_sem=sent.at[s], recv_sem=arrived.at[k],
                device_id=(x, y, 1 - c), device_id_type=MESH_ID)

        @pl.when(j >= 1)
        def _():
            send(j - 1).wait_recv()
            s = lax.rem(j - 1, 2)
            g = mine[s] + theirs[s]
            g_out[...] = g
            d_out[...], m_out[...], v_out[...] = _adam_math(w_ref[...], g, m_ref[...], v_ref[...])

        @pl.when(j >= 2)
        def _():
            send(j - 2).wait_send()

        @pl.when(j < n_blocks)
        def _():
            mine[lax.rem(j, 2)] = (((o_ref[...] + r0_ref[...].astype(F32)) + r1_ref[...].astype(F32))
                                   + r2_ref[...].astype(F32))
            send(j).start()

        @pl.when(j == n_blocks)
        def _():
            send(j - 1).wait_send()

    blk = (None, tr, cols)
    ahead = lambda i: jnp.minimum(i, n_blocks - 1)
    behind = lambda i: jnp.maximum(i - 1, 0)
    late = pl.BlockSpec((tr, cols), lambda i, s: (behind(i), 0))
    grid_spec = pltpu.PrefetchScalarGridSpec(
        num_scalar_prefetch=1, grid=(n_blocks + 1,),
        in_specs=[pl.BlockSpec(blk, lambda i, s: (s[0], ahead(i), 0))]
        + [pl.BlockSpec(blk, lambda i, s, k=k: (k, ahead(i), 0)) for k in range(3)] + [late] * 3,
        out_specs=[late] * 4,
        scratch_shapes=[pltpu.VMEM((2, tr, cols), F32), pltpu.VMEM((2, tr, cols), F32),
                        pltpu.SemaphoreType.DMA((2,)), pltpu.SemaphoreType.DMA((n_blocks,))])
    return pl.pallas_call(
        body, grid_spec=grid_spec, out_shape=[SDS((rows, cols), F32)] * 4, name=name,
        compiler_params=_params(1))(jnp.reshape(idx, (1,)).astype(jnp.int32), own, recv, recv, recv, w, m, v)


def _adam_pair(name, item, after=None):
    def fn(w_, a, b, m_, v_):
        g = a + b
        return (g,) + _adam_math(w_, g, m_, v_)

    return _ew_call(name, fn, list(item), 4, after)


def _place():
    return lax.axis_index("x"), lax.axis_index("y"), lax.axis_index("c")


def _other_chips(x, y):
    return [(1 - x, y), (x, 1 - y), (1 - x, 1 - y)]


HBM = pl.BlockSpec(memory_space=pltpu.HBM)
SEM = pl.BlockSpec(memory_space=pltpu.SEMAPHORE)
DATAFLOW = pltpu.SideEffectType.DATAFLOW_SIDE_EFFECTING


class _Flight:
    def __init__(self, copies, n_copies, send, recv, srcs, lands, token):
        self.copies, self.n, self.send, self.recv = copies, n_copies, send, recv
        self.srcs, self.lands, self.token = list(srcs), list(lands), token


def _take_off(name, srcs, lands, copies, n_copies, after):
    n_s, n_l = len(srcs), len(lands)

    def body(*refs):
        src, land = refs[:n_s], refs[n_s:n_s + n_l]
        send, recv = refs[n_s + n_l + 1:n_s + n_l + 3]
        for cp in copies(src, land, send, recv):
            cp.start()
        refs[-1][...] = jnp.zeros_like(refs[-1])

    mem = lambda t: pltpu.HBM(t.shape, t.dtype)
    sems = pltpu.SemaphoreType.DMA((n_copies,))
    outs = pl.pallas_call(
        body, name=name,
        out_shape=(sems, sems, *map(mem, srcs), *map(mem, lands), SDS((SUBLANES, LANES), F32)),
        in_specs=[HBM] * (n_s + n_l) + [ANY],
        out_specs=(SEM, SEM, *[HBM] * (n_s + n_l), pl.BlockSpec(memory_space=pltpu.VMEM)),
        input_output_aliases={i: 2 + i for i in range(n_s + n_l)},
        compiler_params=pltpu.CompilerParams(has_side_effects=DATAFLOW),
    )(*[pltpu.with_memory_space_constraint(t, pltpu.HBM) for t in (*srcs, *lands)], after)
    return _Flight(copies, n_copies, outs[0], outs[1], outs[2:2 + n_s], outs[2 + n_s:2 + n_s + n_l], outs[-1])


def _land(name, flight, after):
    n_s, n_l = len(flight.srcs), len(flight.lands)

    def body(*refs):
        src, land = refs[:n_s], refs[n_s:n_s + n_l]
        send, recv = refs[n_s + n_l:n_s + n_l + 2]
        for cp in flight.copies(src, land, send, recv):
            cp.wait_send()
            cp.wait_recv()

    mem = lambda t: pltpu.HBM(t.shape, t.dtype)
    outs = pl.pallas_call(
        body, name=name, out_shape=(*map(mem, flight.srcs), *map(mem, flight.lands)),
        in_specs=[HBM] * (n_s + n_l) + [SEM, SEM, ANY], out_specs=tuple([HBM] * (n_s + n_l)),
        input_output_aliases={i: i for i in range(n_s + n_l)},
        compiler_params=pltpu.CompilerParams(has_side_effects=DATAFLOW),
    )(*flight.srcs, *flight.lands, flight.send, flight.recv, after)
    return list(outs[:n_s]), list(outs[n_s:])


def _empty_like(shapes_from, lead):
    return [lax.empty((lead,) + t.shape[1:], t.dtype) for t in shapes_from]


def _scatter_off(name, chunks, after):
    def copies(src, land, send, recv):
        x, y, c = _place()
        return [pltpu.make_async_remote_copy(
            src_ref=src[a].at[2 * px + py], dst_ref=land[a].at[k], send_sem=send.at[3 * a + k],
            recv_sem=recv.at[3 * a + k], device_id=(px, py, c), device_id_type=MESH_ID)
            for a in range(len(chunks)) for k, (px, py) in enumerate(_other_chips(x, y))]

    return _take_off(name, chunks, _empty_like(chunks, 3), copies, 3 * len(chunks), after)


def _swap_off(name, arrs, after):
    def copies(src, land, send, recv):
        x, y, c = _place()
        return [pltpu.make_async_remote_copy(
            src_ref=src[a], dst_ref=land[a], send_sem=send.at[a], recv_sem=recv.at[a],
            device_id=(x, y, 1 - c), device_id_type=MESH_ID) for a in range(len(arrs))]

    return _take_off(name, arrs, [lax.empty(t.shape, t.dtype) for t in arrs], copies, len(arrs), after)


def _devices_off(name, block, after):
    me = 4 * lax.axis_index("x") + 2 * lax.axis_index("y") + lax.axis_index("c")
    land = lax.dynamic_update_index_in_dim(lax.empty((N_DEV,) + block.shape, block.dtype), block, me, 0)

    def copies(src, land, send, recv):
        x, y, c = _place()
        mine = 4 * x + 2 * y + c
        return [pltpu.make_async_remote_copy(
            src_ref=src[0], dst_ref=land[0].at[mine], send_sem=send.at[k - 1], recv_sem=recv.at[k - 1],
            device_id=(x ^ (k >> 2), y ^ ((k >> 1) & 1), c ^ (k & 1)), device_id_type=MESH_ID)
            for k in range(1, N_DEV)]

    return _take_off(name, [block], [land], copies, N_DEV - 1, after)


def _half_rows(shape, c, other=False):
    half = shape[0] // 2
    return pl.ds(((1 - c) if other else c) * half, half)


def _gather_start(name, shards, lands, after):
    n = len(shards)

    def body(*refs):
        src, land, (send, recv) = refs[:n], refs[n:2 * n], refs[2 * n + 1:2 * n + 3]
        x, y, c = _place()
        me = 2 * x + y
        for a in range(n):
            mine = _half_rows(shards[a].shape, c)
            for j, (px, py) in enumerate(_other_chips(x, y)):
                pltpu.make_async_remote_copy(
                    src_ref=src[a].at[mine], dst_ref=land[a].at[me, mine], send_sem=send.at[3 * a + j],
                    recv_sem=recv.at[3 * a + j], device_id=(px, py, c), device_id_type=MESH_ID).start()
        token = refs[-1]
        token[...] = jnp.zeros_like(token)

    mem = lambda t: pltpu.HBM(t.shape, t.dtype)
    pair = pltpu.SemaphoreType.DMA((3 * n,))
    outs = pl.pallas_call(
        body, name=name,
        out_shape=(pair, pair, *map(mem, shards), *map(mem, lands), SDS((SUBLANES, LANES), F32)),
        in_specs=[HBM] * (2 * n) + [ANY],
        out_specs=(SEM, SEM, *[HBM] * (2 * n), pl.BlockSpec(memory_space=pltpu.VMEM)),
        input_output_aliases={i: 2 + i for i in range(2 * n)},
        compiler_params=pltpu.CompilerParams(has_side_effects=DATAFLOW),
    )(*[pltpu.with_memory_space_constraint(t, pltpu.HBM) for t in (*shards, *lands)], after)
    return outs[0], outs[1], list(outs[2:2 + n]), list(outs[2 + n:2 + 2 * n]), outs[-1]


def _gather_pass(name, send, recv, shards, lands, after, first=0):
    n = len(shards)

    def body(*refs):
        src, land, (send, recv, _) = refs[:n], refs[n:2 * n], refs[2 * n:2 * n + 3]
        fsend, frecv = refs[2 * n + 3], refs[2 * n + 4]
        x, y, c = _place()
        me = 2 * x + y
        for a in range(n):
            mine = _half_rows(shards[a].shape, c)
            for j, (px, py) in enumerate(_other_chips(x, y)):
                far = 2 * px + py
                ici = pltpu.make_async_remote_copy(
                    src_ref=src[a].at[mine], dst_ref=land[a].at[far, mine], send_sem=send.at[3 * (first + a) + j],
                    recv_sem=recv.at[3 * (first + a) + j], device_id=(px, py, c), device_id_type=MESH_ID)
                ici.wait_recv()
                ici.wait_send()
                pltpu.make_async_remote_copy(
                    src_ref=land[a].at[far, mine], dst_ref=land[a].at[far, mine], send_sem=fsend.at[3 * a + j],
                    recv_sem=frecv.at[3 * a + j], device_id=(x, y, 1 - c), device_id_type=MESH_ID).start()
        token = refs[-1]
        token[...] = jnp.zeros_like(token)

    mem = lambda t: pltpu.HBM(t.shape, t.dtype)
    pair = pltpu.SemaphoreType.DMA((3 * n,))
    outs = pl.pallas_call(
        body, name=name,
        out_shape=(pair, pair, *map(mem, lands), SDS((SUBLANES, LANES), F32)),
        in_specs=[HBM] * (2 * n) + [SEM, SEM, ANY],
        out_specs=(SEM, SEM, *[HBM] * n, pl.BlockSpec(memory_space=pltpu.VMEM)),
        input_output_aliases={n + i: 2 + i for i in range(n)},
        compiler_params=pltpu.CompilerParams(has_side_effects=DATAFLOW),
    )(*shards, *lands, send, recv, after)
    return outs[0], outs[1], list(outs[2:2 + n]), outs[-1]


def _gather_wait(name, fsend, frecv, lands, after):
    n = len(lands)

    def body(*refs):
        land, (fsend, frecv, _) = refs[:n], refs[n:n + 3]
        x, y, c = _place()
        for a in range(n):
            for j, (px, py) in enumerate(_other_chips(x, y)):
                far = 2 * px + py
                mine = _half_rows(lands[a].shape[1:], c)
                theirs = _half_rows(lands[a].shape[1:], c, other=True)
                pltpu.make_async_remote_copy(
                    src_ref=land[a].at[far, mine], dst_ref=land[a].at[far, mine], send_sem=fsend.at[3 * a + j],
                    recv_sem=frecv.at[3 * a + j], device_id=(x, y, 1 - c), device_id_type=MESH_ID).wait_send()
                pltpu.make_async_remote_copy(
                    src_ref=land[a].at[far, theirs], dst_ref=land[a].at[far, theirs], send_sem=fsend.at[3 * a + j],
                    recv_sem=frecv.at[3 * a + j], device_id=(x, y, 1 - c), device_id_type=MESH_ID).wait_recv()

    mem = lambda t: pltpu.HBM(t.shape, t.dtype)
    return list(pl.pallas_call(
        body, name=name, out_shape=tuple(map(mem, lands)), in_specs=[HBM] * n + [SEM, SEM, ANY],
        out_specs=tuple([HBM] * n), input_output_aliases={i: i for i in range(n)},
        compiler_params=pltpu.CompilerParams(has_side_effects=DATAFLOW),
    )(*lands, fsend, frecv, after))


def _after(token):
    return _Exchange([token], [], [], lambda *_: None, lambda *_: None)


def _sum_devices(slots):
    def body(s_ref, o_ref):
        acc = s_ref[0]
        for d in range(1, N_DEV):
            acc = acc + s_ref[d]
        o_ref[...] = acc

    return pl.pallas_call(
        body, in_specs=[pl.BlockSpec(memory_space=pltpu.VMEM)], out_specs=pl.BlockSpec(memory_space=pltpu.VMEM),
        out_shape=SDS(slots.shape[1:], F32), name="sum_small",
        compiler_params=pltpu.CompilerParams(vmem_limit_bytes=32 * 1024 * 1024))(slots)


def _adam_small(ws, gs, ms, vs):
    n = len(ws)

    def body(*refs):
        for i in range(n):
            w_ref, g_ref, m_ref, v_ref = (refs[k * n + i] for k in range(4))
            outs = _adam_math(w_ref[...], g_ref[...], m_ref[...], v_ref[...])
            for k in range(3):
                refs[(4 + k) * n + i][...] = outs[k]

    vmem = pl.BlockSpec(memory_space=pltpu.VMEM)
    return pl.pallas_call(
        body, in_specs=[vmem] * (4 * n), out_specs=[vmem] * (3 * n),
        out_shape=[SDS(w.shape, F32) for w in ws] * 3, name="adam_small",
        compiler_params=pltpu.CompilerParams(vmem_limit_bytes=32 * 1024 * 1024))(*ws, *gs, *ms, *vs)


def _local_step(x, target, small, big, tb, distributed):
    dist = distributed
    me = (2 * lax.axis_index("x") + lax.axis_index("y")) if dist else 0
    tb_ssm = min(tb, 256)
    bucket = jnp.asarray(_bucket_table())
    place_own = lambda t: lax.dynamic_update_index_in_dim(lax.empty((N_CHIPS,) + t.shape, t.dtype), t, me, 0)
    if dist:
        in_legs = _gather_start("gather_in_start", [big["w_in"]], [place_own(big["w_in"])], small["d_skip"])
        names = sorted(small)
        in_token, values = lax.optimization_barrier((in_legs[4], [small[n] for n in names]))
        small = dict(zip(names, values))
    g1, g2, g3, g4 = small["norm_mix_pre"], small["norm_mix_post"], small["norm_mlp_pre"], small["norm_mlp_post"]

    keys_first = lambda t: jnp.swapaxes(t, -1, -2)
    bias = _bias_table(small["rel_bias"], bucket)
    sink_rows = keys_first(_pair_layout(jnp.broadcast_to(small["sinks"].reshape(N_HEADS, 1, 1), (N_HEADS, BLOCK, 1))))
    disc_args = (small["lam_re"], small["lam_im"], small["log_dt"], small["b_re"], small["b_im"])
    (ab_re, ab_im, bb_re, bb_im), disc_vjp = jax.vjp(_ssm_discretize, *disc_args)
    tab_f, tab_b = _scan_tables(ab_re, ab_im)
    bmat = _bf(_b_matrix(bb_re, bb_im))
    cmat = _bf(_c_matrix(small["c_re"], small["c_im"]))
    bmat_t, cmat_t = bmat.transpose(0, 2, 1), cmat.transpose(0, 2, 1)
    d_skip = small["d_skip"]

    mix = ("w_glu", "w_attn_branch", "w_ssm_branch", "w_out")
    rest = [big[n] for n in mix + ("w_ff_in", "w_ff_out")]
    if dist:
        send, recv, src, lands, _ = in_legs
        tab_f, tab_b, bias, sink_rows, bmat, cmat, bmat_t, cmat_t, rest, rest_lands = lax.optimization_barrier(
            (tab_f, tab_b, bias, sink_rows, bmat, cmat, bmat_t, cmat_t, rest, [place_own(t) for t in rest]))
        corner = lambda t: t.reshape(-1, t.shape[-1])[:1, :LANES].astype(F32)
        prepared = sum(map(corner, [tab_b, bias, sink_rows, bmat, cmat] + rest_lands), in_token[:1])
        in_send, in_recv, in_lands, in_passed = _gather_pass("gather_in_pass", send, recv, src, lands, prepared)
    token = None
    n_mix = len(mix)
    if dist:
        send, recv, rest, lands, token = _gather_start("gather_rest_start", rest, rest_lands, in_passed)
        (g_in,) = _gather_wait("gather_in_wait", in_send, in_recv, in_lands, token)
        w_in = g_in.reshape(IN_W, D_MODEL)
    else:
        w_in = big["w_in"]
    h1, q, k, v, u, ga, gs = _inproj_fwd(x, g1, w_in, tb)
    s, h = _ssm_fwd(u, bmat, cmat, tab_f, d_skip, tb)
    if dist:
        fsend, frecv, mix_lands, token = _gather_pass("gather_mix_pass", send, recv, rest[:n_mix], lands[:n_mix], s)
    att = _attn_fwd(q, k, v, bias, sink_rows, _after(token) if dist else None)[0]
    if dist:
        w_mix = _gather_wait("gather_mix_wait", fsend, frecv, mix_lands, att)
        fsend, frecv, ff_lands, token = _gather_pass(
            "gather_ff_pass", send, recv, rest[n_mix:], lands[n_mix:], w_mix[0], n_mix)
        rest = w_mix + ff_lands
    w_glu, w_ab, w_sb, w_out = rest[:n_mix]
    w_glu = w_glu.reshape(SSM_W, SSM_W)
    w_out = w_out.reshape(D_MODEL, D_MODEL)
    x2 = _merge_fwd(x, s, att, ga, gs, g2, w_glu, w_ab, w_sb, w_out, tb, _after(token) if dist else None)
    if dist:
        rest[n_mix:] = _gather_wait("gather_ff_wait", fsend, frecv, ff_lands, x2)
    w_ffi, w_ffo = [rest[n_mix]], rest[n_mix + 1]
    dy, df, h3, ra, loss_acc, dg4 = _mlp_fwd_loss(x2, target, g3, g4, w_ffi, w_ffo, tb)

    dx2, da, dg3 = _mlp_bwd(x2, dy, df, ra, g3, w_ffi, w_ffo, tb)
    tl = min(2048, x.shape[0])
    chunked = (N_CHIPS, D_FF // N_CHIPS, D_MODEL)
    d_ffi, b_ffi = _matmul_tn("grad_w_ff_in", h3, da, D_MODEL, D_FF // FF_CHUNKS, tl, True)
    d_ffo, b_ffo = _matmul_tn("grad_w_ff_out", ra, df, D_FF // FF_CHUNKS, D_MODEL, tl, False, square_a=True)
    d_ffo, b_ffo = d_ffo.reshape(chunked), b_ffo.reshape(chunked)
    behind = lambda flight: _after(flight.token) if dist else None
    ff_fl = _scatter_off("scatter_ff_off", [b_ffi, b_ffo], d_ffo) if dist else None
    outs = _merge_bwd(dx2, s, att, ga, gs, g2, w_glu, w_ab, w_sb, w_out, tb_ssm, behind(ff_fl))
    ds, datt, dga, dgs, dg2, d_glu, d_ab, d_sb, d_out, b_glu, b_ab, b_sb, b_out = outs
    glu4, out4 = (N_CHIPS, SSM_W // N_CHIPS, SSM_W), (N_CHIPS, D_MODEL // N_CHIPS, D_MODEL)
    d_mix = [d_glu.reshape(glu4), d_ab, d_sb, d_out.reshape(out4)]
    b_mix = [b_glu.reshape(glu4), b_ab, b_sb, b_out.reshape(out4)]
    mix_fl = _scatter_off("scatter_mix_off", b_mix, d_mix[-1]) if dist else None
    du, d_bmat, d_cmat, da_acc, dd_skip = _ssm_bwd(
        ds, u, h, bmat_t, cmat_t, tab_b, d_skip, tb, behind(mix_fl))
    dq, dk, dv, dbias, dsink_rows = _attn_bwd(q, k, v, datt, bias, sink_rows)
    dx, dpj, dg1 = _inproj_bwd(x, dx2, dq, dk, dv, du, dga, dgs, g1, w_in, tb)

    dab_re, dab_im = _state_unlayout(jnp.sum(da_acc, axis=0))
    dbb_re, dbb_im = _b_matrix_grad(d_bmat)
    d_lam_re, d_lam_im, d_log_dt, d_b_re, d_b_im = disc_vjp((dab_re, dab_im, dbb_re, dbb_im))
    d_c_re, d_c_im = _c_matrix_grad(d_cmat)
    d_rel = _bias_grad(dbias, bucket)
    d_sinks = jnp.sum(_pair_unlayout(keys_first(dsink_rows)), axis=(1, 2))
    small_grads = dict(
        norm_mix_pre=dg1, norm_mix_post=dg2, norm_mlp_pre=dg3, norm_mlp_post=dg4, rel_bias=d_rel, sinks=d_sinks,
        lam_re=d_lam_re, lam_im=d_lam_im, log_dt=d_log_dt, b_re=d_b_re, b_im=d_b_im, c_re=d_c_re, c_im=d_c_im,
        d_skip=dd_skip)
    small_fl = _devices_off("small_off", _pack(small_grads, loss_acc), dg1) if dist else None
    outs = _matmul_tn("grad_w_in", dpj, h1, IN_W // 2, D_MODEL, tl, False, behind(small_fl))
    in4 = (N_CHIPS, IN_W // N_CHIPS, D_MODEL)
    d_in, b_in = outs[0].reshape(in4), outs[1].reshape(in4)
    if not dist:
        return loss_acc, dx, small_grads, dict(zip(BIG, [d_in] + d_mix + [d_ffi, d_ffo]))
    in_fl = _scatter_off("scatter_w_in_off", [b_in], d_in)
    r_ffi, r_ffo = _land("scatter_ff_land", ff_fl, in_fl.token)[1]
    p_ffi = _sum4("sum_w_ff_in", d_ffi, r_ffi, me)
    p_ffo = _sum4("sum_w_ff_out", d_ffo, r_ffo, me)
    swap_fl = _swap_off("swap_ff_off", [p_ffi, p_ffo], r_ffo)
    r_mix = _land("scatter_mix_land", mix_fl, swap_fl.token)[1]
    p_mix = _sum4_group("sum_mix", d_mix, r_mix, me)
    mix_swap = _swap_off("swap_mix_off", p_mix, swap_fl.token)
    pending = dict(d_in=d_in, in_fl=in_fl, mix_swap=mix_swap, ff_swap=swap_fl, me=me)
    return loss_acc, dx, small_fl, pending


SMALL = ['norm_mix_pre', 'norm_mix_post', 'norm_mlp_pre', 'norm_mlp_post', 'rel_bias', 'sinks', 'lam_re', 'lam_im',
         'log_dt', 'b_re', 'b_im', 'c_re', 'c_im', 'd_skip']
BIG = ['w_in', 'w_glu', 'w_attn_branch', 'w_ssm_branch', 'w_out', 'w_ff_in', 'w_ff_out']
WEIGHTS = ['norm_mix_pre', 'norm_mix_post', 'norm_mlp_pre', 'norm_mlp_post', 'w_in', 'rel_bias', 'sinks', 'lam_re',
           'lam_im', 'log_dt', 'b_re', 'b_im', 'c_re', 'c_im', 'd_skip', 'w_glu', 'w_attn_branch', 'w_ssm_branch',
           'w_out', 'w_ff_in', 'w_ff_out']
PACK_COLS = 1024
PACK_ORDER = ['b_re', 'b_im', 'c_re', 'c_im', 'lam_re', 'lam_im', 'norm_mix_pre', 'norm_mix_post', 'norm_mlp_pre',
              'norm_mlp_post', 'rel_bias', 'sinks', 'log_dt', 'd_skip']


STATE_MINOR = ('b_re', 'b_im')
PACK_ROWS = 144
LOSS_ROW = 140


def _pack(named, loss_acc):
    parts = []
    for n in PACK_ORDER:
        a = jnp.swapaxes(named[n], -1, -2) if n in STATE_MINOR else named[n]
        flat = a.reshape(-1)
        rows = -(-flat.shape[0] // PACK_COLS)
        parts.append(jnp.pad(flat, (0, rows * PACK_COLS - flat.shape[0])).reshape(rows, PACK_COLS))
    assert sum(p.shape[0] for p in parts) == LOSS_ROW
    parts.append(jnp.pad(loss_acc[0:1], ((0, PACK_ROWS - LOSS_ROW - 1), (0, PACK_COLS - loss_acc.shape[1]))))
    return jnp.concatenate(parts, axis=0)


def _unpack(packed, shapes):
    out, at = {}, 0
    for n in PACK_ORDER:
        shape = shapes[n][:-2] + (shapes[n][-1], shapes[n][-2]) if n in STATE_MINOR else shapes[n]
        size = int(np.prod(shape))
        rows = -(-size // PACK_COLS)
        blk = packed[at:at + rows]
        out[n] = (blk.reshape(-1)[:size] if size % PACK_COLS else blk).reshape(shape)
        at += rows
    return out


def kernel(x, norm_mix_pre, norm_mix_post, norm_mlp_pre, norm_mlp_post, w_in, rel_bias, sinks, lam_re, lam_im, log_dt, b_re, b_im, c_re, c_im, d_skip, w_glu, w_attn_branch, w_ssm_branch, w_out, w_ff_in, w_ff_out, loss_target, m_norm_mix_pre, m_norm_mix_post, m_norm_mlp_pre, m_norm_mlp_post, m_w_in, m_rel_bias, m_sinks, m_lam_re, m_lam_im, m_log_dt, m_b_re, m_b_im, m_c_re, m_c_im, m_d_skip, m_w_glu, m_w_attn_branch, m_w_ssm_branch, m_w_out, m_w_ff_in, m_w_ff_out, v_norm_mix_pre, v_norm_mix_post, v_norm_mlp_pre, v_norm_mlp_post, v_w_in, v_rel_bias, v_sinks, v_lam_re, v_lam_im, v_log_dt, v_b_re, v_b_im, v_c_re, v_c_im, v_d_skip, v_w_glu, v_w_attn_branch, v_w_ssm_branch, v_w_out, v_w_ff_in, v_w_ff_out):
    env = dict(locals())
    w = {n: env[n] for n in WEIGHTS}
    m = {n: env["m_" + n] for n in WEIGHTS}
    v = {n: env["v_" + n] for n in WEIGHTS}
    seq = x.shape[1]
    tb = min(512, seq)

    small = {n: w[n] for n in ('norm_mix_pre', 'norm_mix_post', 'norm_mlp_pre', 'norm_mlp_post', 'rel_bias')}
    small.update({n: w[n][0] for n in ('sinks', 'lam_re', 'lam_im', 'log_dt', 'b_re', 'b_im', 'c_re', 'c_im')})
    small['d_skip'] = w['d_skip']
    shard = lambda t, n: t[n][0].T if n == 'w_in' else t[n][0]
    unshard = lambda a, n: (a.T if n == 'w_in' else a)[None]
    _, dx, small_fl, pending = _local_step(
        x[0], loss_target[0], small, {n: _bf(shard(w, n)) for n in BIG}, tb, True)

    grads, deltas, new_m, new_v = {}, {}, {}, {}

    def adam(n, partials, after=None):
        outs = _adam_pair("adam_" + n, (shard(w, n), *partials, shard(m, n), shard(v, n)), after)
        grads[n], deltas[n], new_m[n], new_v[n] = [unshard(a, n) for a in outs]
        return outs[3]

    mix = ("w_glu", "w_attn_branch", "w_ssm_branch", "w_out")
    in_fl = pending["in_fl"]

    small_g = _sum_devices(_land("small_land", small_fl, pending["mix_swap"].token)[1][0])
    loss = small_g[LOSS_ROW, 0]
    minor = lambda t, n: jnp.swapaxes(t, -1, -2) if n in STATE_MINOR else t
    g_small = _unpack(small_g, {n: w[n].shape for n in SMALL})
    outs = _adam_small([minor(w[n], n) for n in SMALL], [g_small[n] for n in SMALL],
                       [minor(m[n], n) for n in SMALL], [minor(v[n], n) for n in SMALL])
    grads.update({n: minor(g_small[n], n) for n in SMALL})
    for k, dst in enumerate((deltas, new_m, new_v)):
        dst.update({n: minor(a, n) for n, a in zip(SMALL, outs[k * len(SMALL):(k + 1) * len(SMALL)])})

    last = outs[0]
    own_ff, sib_ff = _land("swap_ff_land", pending["ff_swap"], last)
    for n, partials in zip(("w_ff_in", "w_ff_out"), zip(own_ff, sib_ff)):
        last = adam(n, partials, last)
    own_mix, sib_mix = _land("swap_mix_land", pending["mix_swap"], last)
    outs = _adam_group("adam_mix", [(shard(w, n), p, s, shard(m, n), shard(v, n))
                                    for n, p, s in zip(mix, own_mix, sib_mix)], last)
    for n, item in zip(mix, outs):
        grads[n], deltas[n], new_m[n], new_v[n] = [unshard(a, n) for a in item]

    (r_in,) = _land("scatter_w_in_land", in_fl, outs[-1][3])[1]
    outs = _sum4_swap_adam("sum_swap_adam_w_in", pending["d_in"], r_in, pending["me"],
                           shard(w, "w_in"), shard(m, "w_in"), shard(v, "w_in"))
    grads["w_in"], deltas["w_in"], new_m["w_in"], new_v["w_in"] = [unshard(a, "w_in") for a in outs]

    return (loss, dx[None], *[grads[n] for n in WEIGHTS], *[deltas[n] for n in WEIGHTS],
            *[new_m[n] for n in WEIGHTS], *[new_v[n] for n in WEIGHTS])
```
